```python
import jax, jax.numpy as jnp
from jax import lax
import numpy as np

D_MODEL = 1024
BATCH = 8
SEQ = 4096
DEPTH = 1

D_MIX = D_MODEL
D_ATT = D_MIX // 2
D_POOL = D_MIX - D_ATT
N_ATT_HEADS = 8
HEAD_DIM = D_ATT // N_ATT_HEADS
POOL_WINDOWS = (2, 4, 8, 16)
N_POOL_GROUPS = len(POOL_WINDOWS)
POOL_GROUP_DIM = D_POOL // N_POOL_GROUPS
Q_BLOCK = 128
LN_EPS = 1e-5
FORGET_BIAS_INIT = 3.0
DEEPNORM_ALPHA = (2.0 * DEPTH) ** 0.25
DEEPNORM_BETA = (8.0 * DEPTH) ** -0.25
D_IN = 3 * D_ATT + N_ATT_HEADS + D_POOL + D_ATT + D_POOL
SPLIT_POINTS = (D_ATT, 2 * D_ATT, 3 * D_ATT, 3 * D_ATT + N_ATT_HEADS,
                3 * D_ATT + N_ATT_HEADS + D_POOL, 3 * D_ATT + N_ATT_HEADS + D_POOL + D_ATT)

kernel_name = "hymba_fox_poolformer_deepnorm_adaln"


def _layer_norm(h, g, b):
    h32 = h.astype(jnp.float32)
    mu = jnp.mean(h32, axis=-1, keepdims=True)
    var = jnp.mean(jnp.square(h32 - mu), axis=-1, keepdims=True)
    y = (h32 - mu) * lax.rsqrt(var + LN_EPS)
    return (y * g.astype(jnp.float32) + b.astype(jnp.float32)).astype(h.dtype)


def _forgetting_attention(q, k, v, f_logit):
    B, S, _ = q.shape
    q = q.reshape(B, S, N_ATT_HEADS, HEAD_DIM)
    k = k.reshape(B, S, N_ATT_HEADS, HEAD_DIM)
    v = v.reshape(B, S, N_ATT_HEADS, HEAD_DIM)
    log_f = jax.nn.log_sigmoid(f_logit.astype(jnp.float32))
    cum = jnp.cumsum(log_f, axis=1)
    nb = S // Q_BLOCK
    q_blocks = q.reshape(B, nb, Q_BLOCK, N_ATT_HEADS, HEAD_DIM).transpose(1, 0, 2, 3, 4)
    cum_blocks = cum.reshape(B, nb, Q_BLOCK, N_ATT_HEADS).transpose(1, 0, 2, 3)
    cum_k = cum.transpose(0, 2, 1)[:, :, None, :]
    k_pos = jnp.arange(S)
    scale = HEAD_DIM ** -0.5

    def one_block(args):
        q_blk, cum_q, idx = args
        s = jnp.einsum('bqhd,bkhd->bhqk', q_blk, k).astype(jnp.float32) * scale
        s = s + cum_q.transpose(0, 2, 1)[..., None] - cum_k
        q_pos = idx * Q_BLOCK + jnp.arange(Q_BLOCK)
        causal = k_pos[None, :] <= q_pos[:, None]
        s = jnp.where(causal[None, None], s, -jnp.inf)
        p = jax.nn.softmax(s, axis=-1).astype(v.dtype)
        return jnp.einsum('bhqk,bkhd->bqhd', p, v)

    out = lax.map(one_block, (q_blocks, cum_blocks, jnp.arange(nb)))
    return out.transpose(1, 0, 2, 3, 4).reshape(B, S, D_ATT)


def _multiscale_pool(p, w_pool_mix, b_pool_mix, pool_scale):
    B, S, _ = p.shape
    p32 = p.astype(jnp.float32)
    cs = jnp.cumsum(p32, axis=1)
    count = jnp.arange(1, S + 1, dtype=jnp.float32)
    outs = []
    for g, w in enumerate(POOL_WINDOWS):
        sl = slice(g * POOL_GROUP_DIM, (g + 1) * POOL_GROUP_DIM)
        cs_g = cs[..., sl]
        lagged = jnp.pad(cs_g, ((0, 0), (w, 0), (0, 0)))[:, :S]
        mean = (cs_g - lagged) / jnp.minimum(count, float(w))[None, :, None]
        outs.append(mean - p32[..., sl])
    pooled = jnp.stack(outs, axis=2).astype(p.dtype)
    mixed = jnp.einsum('bsgc,gce->bsge', pooled, w_pool_mix) + b_pool_mix
    return mixed.reshape(B, S, D_POOL) * pool_scale


def _hybrid_layer(x, c, w_ada, b_ada, w_in, b_in, w_pool_mix, b_pool_mix, pool_scale,
                  w_out, b_out, ln_g, ln_b):
    ada = jnp.einsum('bd,de->be', jax.nn.silu(c), w_ada) + b_ada
    shift, scale, gate = jnp.split(ada, 3, axis=-1)
    u = x * (1 + scale[:, None, :]) + shift[:, None, :]
    proj = jnp.einsum('bsd,de->bse', u, w_in) + b_in
    q, k, v, f_logit, p, g_att, g_pool = jnp.split(proj, SPLIT_POINTS, axis=-1)
    att = _forgetting_attention(q, k, v, f_logit)
    pool = _multiscale_pool(p, w_pool_mix, b_pool_mix, pool_scale)
    y = jnp.concatenate([att * jax.nn.silu(g_att), pool * jax.nn.silu(g_pool)], axis=-1)
    y = jnp.einsum('bse,ed->bsd', y, w_out) + b_out
    h = DEEPNORM_ALPHA * x + gate[:, None, :] * y
    return _layer_norm(h, ln_g, ln_b)


def _fwd_setup_inputs(seed: int = 0) -> dict:
    key = jax.random.key(seed)
    ks = jax.random.split(key, 20)
    D = D_MODEL
    s_d = D ** -0.5
    x = jax.random.normal(ks[0], (BATCH, SEQ, D), jnp.float32)
    c = jax.random.normal(ks[1], (BATCH, D), jnp.float32)
    w_ada = jax.random.normal(ks[2], (DEPTH, D, 3 * D), jnp.float32) * s_d
    b_ada = 0.01 * jax.random.normal(ks[3], (DEPTH, 3 * D), jnp.float32)
    w_q = jax.random.normal(ks[4], (DEPTH, D, D_ATT), jnp.float32) * s_d
    w_k = jax.random.normal(ks[5], (DEPTH, D, D_ATT), jnp.float32) * s_d
    w_v = jax.random.normal(ks[6], (DEPTH, D, D_ATT), jnp.float32) * (s_d * DEEPNORM_BETA)
    w_f = jax.random.normal(ks[7], (DEPTH, D, N_ATT_HEADS), jnp.float32) * (0.1 * s_d)
    w_p = jax.random.normal(ks[8], (DEPTH, D, D_POOL), jnp.float32) * s_d
    w_g = jax.random.normal(ks[9], (DEPTH, D, D_ATT + D_POOL), jnp.float32) * s_d
    w_in = jnp.concatenate([w_q, w_k, w_v, w_f, w_p, w_g], axis=-1)
    b_in = 0.01 * jax.random.normal(ks[10], (DEPTH, D_IN), jnp.float32)
    b_in = b_in.at[:, 3 * D_ATT:3 * D_ATT + N_ATT_HEADS].add(FORGET_BIAS_INIT)
    w_pool_mix = jax.random.normal(ks[11], (DEPTH, N_POOL_GROUPS, POOL_GROUP_DIM, POOL_GROUP_DIM),
                                   jnp.float32) * POOL_GROUP_DIM ** -0.5
    b_pool_mix = 0.01 * jax.random.normal(ks[12], (DEPTH, N_POOL_GROUPS, POOL_GROUP_DIM), jnp.float32)
    pool_scale = 1.0 + 0.02 * jax.random.normal(ks[13], (DEPTH, D_POOL), jnp.float32)
    w_out = jax.random.normal(ks[14], (DEPTH, D_MIX, D), jnp.float32) * (D_MIX ** -0.5 * DEEPNORM_BETA)
    b_out = 0.01 * jax.random.normal(ks[15], (DEPTH, D), jnp.float32)
    ln_g = 1.0 + 0.02 * jax.random.normal(ks[16], (DEPTH, D), jnp.float32)
    ln_b = 0.01 * jax.random.normal(ks[17], (DEPTH, D), jnp.float32)
    return {"x": x, "c": c, "w_ada": w_ada, "b_ada": b_ada, "w_in": w_in, "b_in": b_in,
            "w_pool_mix": w_pool_mix, "b_pool_mix": b_pool_mix, "pool_scale": pool_scale,
            "w_out": w_out, "b_out": b_out, "ln_g": ln_g, "ln_b": ln_b}


def _fwd_reference(x, c, w_ada, b_ada, w_in, b_in, w_pool_mix, b_pool_mix, pool_scale,
              w_out, b_out, ln_g, ln_b):
    for layer in range(DEPTH):
        x = _hybrid_layer(x, c, w_ada[layer], b_ada[layer], w_in[layer], b_in[layer],
                          w_pool_mix[layer], b_pool_mix[layer], pool_scale[layer],
                          w_out[layer], b_out[layer], ln_g[layer], ln_b[layer])
    return x


import jax as _jax
import jax.numpy as _jnp

TWIN_FORMAT = 'train_step'
FWD_PARAMS = ['x', 'c', 'w_ada', 'b_ada', 'w_in', 'b_in', 'w_pool_mix', 'b_pool_mix', 'pool_scale', 'w_out', 'b_out', 'ln_g', 'ln_b']
TWIN_WEIGHTS = ['w_ada', 'b_ada', 'w_in', 'b_in', 'w_pool_mix', 'b_pool_mix', 'pool_scale', 'w_out', 'b_out', 'ln_g', 'ln_b']
TWIN_DIFF_INPUT = 'x'
TWIN_INPUTS = ['x', 'c', 'w_ada', 'b_ada', 'w_in', 'b_in', 'w_pool_mix', 'b_pool_mix', 'pool_scale', 'w_out', 'b_out', 'ln_g', 'ln_b', 'loss_target', 'm_w_ada', 'm_b_ada', 'm_w_in', 'm_b_in', 'm_w_pool_mix', 'm_b_pool_mix', 'm_pool_scale', 'm_w_out', 'm_b_out', 'm_ln_g', 'm_ln_b', 'v_w_ada', 'v_b_ada', 'v_w_in', 'v_b_in', 'v_w_pool_mix', 'v_b_pool_mix', 'v_pool_scale', 'v_w_out', 'v_b_out', 'v_ln_g', 'v_ln_b']
TWIN_OUTPUTS = ['loss', 'grad_x', 'grad_w_ada', 'grad_b_ada', 'grad_w_in', 'grad_b_in', 'grad_w_pool_mix', 'grad_b_pool_mix', 'grad_pool_scale', 'grad_w_out', 'grad_b_out', 'grad_ln_g', 'grad_ln_b', 'delta_w_ada', 'delta_b_ada', 'delta_w_in', 'delta_b_in', 'delta_w_pool_mix', 'delta_b_pool_mix', 'delta_pool_scale', 'delta_w_out', 'delta_b_out', 'delta_ln_g', 'delta_ln_b', 'new_m_w_ada', 'new_m_b_ada', 'new_m_w_in', 'new_m_b_in', 'new_m_w_pool_mix', 'new_m_b_pool_mix', 'new_m_pool_scale', 'new_m_w_out', 'new_m_b_out', 'new_m_ln_g', 'new_m_ln_b', 'new_v_w_ada', 'new_v_b_ada', 'new_v_w_in', 'new_v_b_in', 'new_v_w_pool_mix', 'new_v_b_pool_mix', 'new_v_pool_scale', 'new_v_w_out', 'new_v_b_out', 'new_v_ln_g', 'new_v_ln_b']
TWIN_LEAF_KINDS = {'loss': 'loss', 'grad_x': 'grad_x', 'grad_w_ada': 'grad_w', 'grad_b_ada': 'grad_w', 'grad_w_in': 'grad_w', 'grad_b_in': 'grad_w', 'grad_w_pool_mix': 'grad_w', 'grad_b_pool_mix': 'grad_w', 'grad_pool_scale': 'grad_w', 'grad_w_out': 'grad_w', 'grad_b_out': 'grad_w', 'grad_ln_g': 'grad_w', 'grad_ln_b': 'grad_w', 'delta_w_ada': 'delta_w', 'delta_b_ada': 'delta_w', 'delta_w_in': 'delta_w', 'delta_b_in': 'delta_w', 'delta_w_pool_mix': 'delta_w', 'delta_b_pool_mix': 'delta_w', 'delta_pool_scale': 'delta_w', 'delta_w_out': 'delta_w', 'delta_b_out': 'delta_w', 'delta_ln_g': 'delta_w', 'delta_ln_b': 'delta_w', 'new_m_w_ada': 'new_m', 'new_m_b_ada': 'new_m', 'new_m_w_in': 'new_m', 'new_m_b_in': 'new_m', 'new_m_w_pool_mix': 'new_m', 'new_m_b_pool_mix': 'new_m', 'new_m_pool_scale': 'new_m', 'new_m_w_out': 'new_m', 'new_m_b_out': 'new_m', 'new_m_ln_g': 'new_m', 'new_m_ln_b': 'new_m', 'new_v_w_ada': 'new_v', 'new_v_b_ada': 'new_v', 'new_v_w_in': 'new_v', 'new_v_b_in': 'new_v', 'new_v_w_pool_mix': 'new_v', 'new_v_b_pool_mix': 'new_v', 'new_v_pool_scale': 'new_v', 'new_v_w_out': 'new_v', 'new_v_b_out': 'new_v', 'new_v_ln_g': 'new_v', 'new_v_ln_b': 'new_v'}


def _forward(args):
    return _fwd_reference(*[args[k] for k in FWD_PARAMS])


def _output_shape():
    def fwd():
        inp = _fwd_setup_inputs(0)
        return _fwd_reference(*[inp[k] for k in FWD_PARAMS])
    out = _jax.eval_shape(fwd)
    return out.shape, out.dtype

N_MICROBATCH = 1
ADAM_LR = 0.001
ADAM_B1 = 0.9
ADAM_B2 = 0.999
ADAM_EPS = 1e-08
ADAM_WD = 0.01
ADAM_STEP = 10
PER_EXAMPLE_BATCH_AXIS = {'x': 0, 'c': 0, 'loss_target': 0}
SHARED_INPUTS = []
_WEIGHT_DTYPES = {'w_ada': _jnp.float32, 'b_ada': _jnp.float32, 'w_in': _jnp.float32, 'b_in': _jnp.float32, 'w_pool_mix': _jnp.float32, 'b_pool_mix': _jnp.float32, 'pool_scale': _jnp.float32, 'w_out': _jnp.float32, 'b_out': _jnp.float32, 'ln_g': _jnp.float32, 'ln_b': _jnp.float32}
MOMENT_SCALE = {'w_ada': 2.752780e-02, 'b_ada': 4.544988e-02, 'w_in': 3.114356e-02, 'b_in': 2.447524e-02, 'w_pool_mix': 4.488489e-02, 'b_pool_mix': 4.471733e-02, 'pool_scale': 4.687131e-02, 'w_out': 5.719853e-02, 'b_out': 9.699027e-02, 'ln_g': 3.202949e+01, 'ln_b': 4.746680e-01}


def _to_microbatches(a, axis):
    t = _jnp.moveaxis(a, axis, 0)
    t = t.reshape((N_MICROBATCH, t.shape[0] // N_MICROBATCH) + t.shape[1:])
    return _jnp.moveaxis(t, 1, axis + 1)


def setup_inputs(seed: int = 0) -> dict:
    inp = _fwd_setup_inputs(seed)
    key = _jax.random.fold_in(_jax.random.key(seed), 7919)
    shape, _ = _output_shape()
    out = dict(inp)
    out["loss_target"] = _jax.random.normal(_jax.random.fold_in(key, 0), shape, _jnp.float32)
    for i, name in enumerate(TWIN_WEIGHTS):
        w = inp[name].astype(_jnp.float32)
        if MOMENT_SCALE is None:
            s = _jnp.sqrt(_jnp.mean(_jnp.square(w)) + 1e-30)
        else:
            s = MOMENT_SCALE[name]
        km, kv = _jax.random.split(_jax.random.fold_in(key, i + 1))
        out[name] = w
        out["m_" + name] = s * _jax.random.normal(km, w.shape, _jnp.float32)
        out["v_" + name] = (s * s) * _jax.random.uniform(kv, w.shape, _jnp.float32, 0.5, 1.5)
    if N_MICROBATCH > 1:
        for name, axis in PER_EXAMPLE_BATCH_AXIS.items():
            out[name] = _to_microbatches(out[name], axis)
    return {'x': out['x'], 'c': out['c'], 'w_ada': out['w_ada'], 'b_ada': out['b_ada'], 'w_in': out['w_in'], 'b_in': out['b_in'], 'w_pool_mix': out['w_pool_mix'], 'b_pool_mix': out['b_pool_mix'], 'pool_scale': out['pool_scale'], 'w_out': out['w_out'], 'b_out': out['b_out'], 'ln_g': out['ln_g'], 'ln_b': out['ln_b'], 'loss_target': out['loss_target'], 'm_w_ada': out['m_w_ada'], 'm_b_ada': out['m_b_ada'], 'm_w_in': out['m_w_in'], 'm_b_in': out['m_b_in'], 'm_w_pool_mix': out['m_w_pool_mix'], 'm_b_pool_mix': out['m_b_pool_mix'], 'm_pool_scale': out['m_pool_scale'], 'm_w_out': out['m_w_out'], 'm_b_out': out['m_b_out'], 'm_ln_g': out['m_ln_g'], 'm_ln_b': out['m_ln_b'], 'v_w_ada': out['v_w_ada'], 'v_b_ada': out['v_b_ada'], 'v_w_in': out['v_w_in'], 'v_b_in': out['v_b_in'], 'v_w_pool_mix': out['v_w_pool_mix'], 'v_b_pool_mix': out['v_b_pool_mix'], 'v_pool_scale': out['v_pool_scale'], 'v_w_out': out['v_w_out'], 'v_b_out': out['v_b_out'], 'v_ln_g': out['v_ln_g'], 'v_ln_b': out['v_ln_b']}


def _loss(weights, diff, rest, loss_target):
    with _jax.named_scope("forward"):
        args = {**rest, TWIN_DIFF_INPUT: diff, **{k: w.astype(_WEIGHT_DTYPES[k]) for k, w in weights.items()}}
        y = _forward(args)
    with _jax.named_scope("loss_head"):
        err = _jnp.square(y.astype(_jnp.float32) - loss_target)
        return 0.5 * _jnp.sum(_jnp.mean(err, axis=-1)) if err.ndim else 0.5 * err


def _adamw(w, g, m, v):
    m = ADAM_B1 * m + (1.0 - ADAM_B1) * g
    v = ADAM_B2 * v + (1.0 - ADAM_B2) * _jnp.square(g)
    m_hat = m / (1.0 - ADAM_B1 ** ADAM_STEP)
    v_hat = v / (1.0 - ADAM_B2 ** ADAM_STEP)
    delta = -ADAM_LR * (m_hat / (_jnp.sqrt(v_hat) + ADAM_EPS) + ADAM_WD * w)
    return delta, m, v


def reference(x, c, w_ada, b_ada, w_in, b_in, w_pool_mix, b_pool_mix, pool_scale, w_out, b_out, ln_g, ln_b, loss_target, m_w_ada, m_b_ada, m_w_in, m_b_in, m_w_pool_mix, m_b_pool_mix, m_pool_scale, m_w_out, m_b_out, m_ln_g, m_ln_b, v_w_ada, v_b_ada, v_w_in, v_b_in, v_w_pool_mix, v_b_pool_mix, v_pool_scale, v_w_out, v_b_out, v_ln_g, v_ln_b):
    given = dict(x=x, c=c, w_ada=w_ada, b_ada=b_ada, w_in=w_in, b_in=b_in, w_pool_mix=w_pool_mix, b_pool_mix=b_pool_mix, pool_scale=pool_scale, w_out=w_out, b_out=b_out, ln_g=ln_g, ln_b=ln_b, loss_target=loss_target, m_w_ada=m_w_ada, m_b_ada=m_b_ada, m_w_in=m_w_in, m_b_in=m_b_in, m_w_pool_mix=m_w_pool_mix, m_b_pool_mix=m_b_pool_mix, m_pool_scale=m_pool_scale, m_w_out=m_w_out, m_b_out=m_b_out, m_ln_g=m_ln_g, m_ln_b=m_ln_b, v_w_ada=v_w_ada, v_b_ada=v_b_ada, v_w_in=v_w_in, v_b_in=v_b_in, v_w_pool_mix=v_w_pool_mix, v_b_pool_mix=v_b_pool_mix, v_pool_scale=v_pool_scale, v_w_out=v_w_out, v_b_out=v_b_out, v_ln_g=v_ln_g, v_ln_b=v_ln_b)
    weights = {n: given[n] for n in TWIN_WEIGHTS}
    shared = {n: given[n] for n in SHARED_INPUTS}
    per_example = {n: given[n] for n in ['x', 'c']}
    grad_fn = _jax.value_and_grad(_loss, argnums=(0, 1))

    def one_microbatch(ex, loss_target):
        ex = dict(ex)
        diff = ex.pop(TWIN_DIFF_INPUT)
        return grad_fn(weights, diff, {**shared, **ex}, loss_target)

    if N_MICROBATCH == 1:
        loss, (grad_w, grad_x) = one_microbatch(per_example, given["loss_target"])
    else:
        def body(carry, xs):
            loss_sum, grad_sum = carry
            l_k, (gw_k, gx_k) = one_microbatch(xs[0], xs[1])
            with _jax.named_scope("update"):
                return (loss_sum + l_k, _jax.tree.map(_jnp.add, grad_sum, gw_k)), gx_k

        init = (_jnp.zeros((), _jnp.float32), _jax.tree.map(_jnp.zeros_like, weights))
        (loss, grad_w), grad_x = _jax.lax.scan(body, init, (per_example, given["loss_target"]))
    with _jax.named_scope("update"):
        delta_w, new_m, new_v = {}, {}, {}
        for n in TWIN_WEIGHTS:
            delta_w[n], new_m[n], new_v[n] = _adamw(weights[n], grad_w[n], given["m_" + n], given["v_" + n])
    return (loss, grad_x, *[grad_w[n] for n in TWIN_WEIGHTS], *[delta_w[n] for n in TWIN_WEIGHTS],
            *[new_m[n] for n in TWIN_WEIGHTS], *[new_v[n] for n in TWIN_WEIGHTS])
```

```python
import functools

import numpy as np
import jax
import jax.numpy as jnp
from jax import lax
from jax.experimental import pallas as pl
from jax.experimental.pallas import tpu as pltpu

F32 = jnp.float32
BF16 = jnp.bfloat16
MESH = pl.DeviceIdType.MESH

D = 1024
D_ATT = 512
D_POOL = 512
N_HEADS = 8
HEAD_DIM = 64
N_PAIR = N_HEADS // 2
POOL_WINDOWS = (2, 4, 8, 16)
POOL_GROUP = 128
POOL_HALO = 16
LN_EPS = 1e-5
ALPHA = 2.0 ** 0.25
D_IN = 3 * D_ATT + N_HEADS + D_POOL + D_ATT + D_POOL
N_CHIPS = 4
SHARD_IN = D_IN // N_CHIPS
SHARD_ADA = 3 * D // N_CHIPS
SHARD_OUT = D // N_CHIPS

O_QKV, O_F, O_P, O_G, D_PAD = 0, 1536, 1664, 2176, 3200
Q_SCALE = HEAD_DIM ** -0.5

ADAM_LR, ADAM_B1, ADAM_B2, ADAM_EPS, ADAM_WD, ADAM_STEP = 0.001, 0.9, 0.999, 1e-08, 0.01, 10

NEG = -1e30

VMEM_LIMIT = 56 * 1024 * 1024

TM_PROJ = 512
T_ATT = 512
TM_MID = 256
TM_TAIL = 512
TM_GW = 512
TM_DU = 512

REL7 = [(0, 0, 1), (0, 1, 0), (0, 1, 1), (1, 0, 0), (1, 0, 1), (1, 1, 0), (1, 1, 1)]
REL3 = [(0, 1), (1, 0), (1, 1)]

SMALL_SEGS = {}
_row = 0
for _name, _n in (("b_in", 3200), ("w_pool_mix", 65536), ("b_pool_mix", 512), ("pool_scale", 512),
                  ("b_out", 1024), ("ln_g", 1024), ("ln_b", 1024)):
    SMALL_SEGS[_name] = (_row, _n // 128)
    _row += _n // 128
SMALL_REDUCED_ROWS = 576
SMALL_SEGS["b_ada"] = (SMALL_REDUCED_ROWS, 24)
SMALL_ROWS = 600


def _params(**kw):
    return pltpu.CompilerParams(vmem_limit_bytes=VMEM_LIMIT, **kw)


def _flip(v, d):
    return v if d == 0 else 1 - v


def _dot(a, b):
    return jnp.dot(a, b, preferred_element_type=F32)


def _dot_nt(a, b):
    return lax.dot_general(a, b, (((1,), (1,)), ((), ())), preferred_element_type=F32)


def _dot_tn(a, b):
    return lax.dot_general(a, b, (((0,), (0,)), ((), ())), preferred_element_type=F32)


def _sigmoid(v):
    return 1.0 / (1.0 + jnp.exp(-v))


def _colsum(v):
    return jnp.sum(v, axis=0, keepdims=True)


def _ada_exchange(c, w_ada, b_ada4):
    def body(c_ref, w_ref, b_ref, call_ref, ada_ref, cslab, sbuf, rbuf, cs_sem, cr_sem, as_sem, ar_sem):
        x, y, cc = lax.axis_index("x"), lax.axis_index("y"), lax.axis_index("c")
        me = 4 * x + 2 * y + cc
        chip = 2 * x + y
        cslab[...] = jnp.broadcast_to(c_ref[...], (8, D))
        call_ref[me] = cslab[...]
        gathers = []
        for k, (dx, dy, dc) in enumerate(REL7):
            cp = pltpu.make_async_remote_copy(
                src_ref=cslab, dst_ref=call_ref.at[me], send_sem=cs_sem.at[k], recv_sem=cr_sem.at[k],
                device_id=(_flip(x, dx), _flip(y, dy), _flip(cc, dc)), device_id_type=MESH)
            cp.start()
            gathers.append(cp)
        for cp in gathers:
            cp.wait()
        rows = lax.broadcasted_iota(jnp.int32, (8, 1), 0)
        mat = jnp.zeros((8, D), F32)
        for r in range(8):
            mat = jnp.where(rows == r, call_ref[r], mat)
        act = (mat * _sigmoid(mat)).astype(BF16)
        part = _dot(act, w_ref[...].astype(BF16))
        sends = []
        for k, (dx, dy) in enumerate(REL3):
            px, py = _flip(x, dx), _flip(y, dy)
            r = 4 * px + 2 * py + cc
            piece = _colsum(jnp.where(rows == r, part, 0.0))
            sbuf[k] = jnp.broadcast_to(piece, (8, SHARD_ADA))
            cp = pltpu.make_async_remote_copy(
                src_ref=sbuf.at[k], dst_ref=rbuf.at[k], send_sem=as_sem.at[k], recv_sem=ar_sem.at[k],
                device_id=(px, py, cc), device_id_type=MESH)
            cp.start()
            sends.append(cp)
        own = _colsum(jnp.where(rows == me, part, 0.0))
        ada_ref[chip] = jnp.broadcast_to(own, (8, SHARD_ADA)) + b_ref[chip]
        for k, (dx, dy) in enumerate(REL3):
            sends[k].wait()
            a = 2 * _flip(x, dx) + _flip(y, dy)
            ada_ref[a] = rbuf[k] + b_ref[a]

    vm = pl.BlockSpec(memory_space=pltpu.VMEM)
    return pl.pallas_call(
        body, name="ada_exchange",
        out_shape=(jax.ShapeDtypeStruct((8, 8, D), F32), jax.ShapeDtypeStruct((4, 8, SHARD_ADA), F32)),
        in_specs=[vm, vm, vm], out_specs=(vm, vm),
        scratch_shapes=[pltpu.VMEM((8, D), F32), pltpu.VMEM((3, 8, SHARD_ADA), F32),
                        pltpu.VMEM((3, 8, SHARD_ADA), F32),
                        pltpu.SemaphoreType.DMA((7,)), pltpu.SemaphoreType.DMA((7,)),
                        pltpu.SemaphoreType.DMA((3,)), pltpu.SemaphoreType.DMA((3,))],
        compiler_params=_params(),
    )(c, w_ada, b_ada4)


def _gather_weights(w_in_sh, w_out_sh):
    def body(win_ref, wout_ref, win_all, wout_all, own_sem, s_sem, r_sem, fs_sem, fr_sem):
        x, y, cc = lax.axis_index("x"), lax.axis_index("y"), lax.axis_index("c")
        chip = 2 * x + y
        sib = (x, y, 1 - cc)
        srcs = (win_ref, wout_ref)
        dsts = (win_all, wout_all)
        halves = (D // 2, SHARD_OUT // 2)

        def rows(t, which):
            h = halves[t]
            return pl.ds(pl.multiple_of(which * h, h), h)

        own = [pltpu.make_async_copy(srcs[t], dsts[t].at[chip], own_sem.at[t]) for t in range(2)]
        for cp in own:
            cp.start()
        first = []
        for k, (dx, dy) in enumerate(REL3):
            peer = (_flip(x, dx), _flip(y, dy), cc)
            for t in range(2):
                cp = pltpu.make_async_remote_copy(
                    src_ref=srcs[t].at[rows(t, cc), :], dst_ref=dsts[t].at[chip, rows(t, cc), :],
                    send_sem=s_sem.at[2 * k + t], recv_sem=r_sem.at[2 * k + t],
                    device_id=peer, device_id_type=MESH)
                cp.start()
                first.append(cp)
        passed = []
        for k, (dx, dy) in enumerate(REL3):
            a = 2 * _flip(x, dx) + _flip(y, dy)
            for t in range(2):
                landed = dsts[t].at[a, rows(t, cc), :]
                pltpu.make_async_remote_copy(
                    src_ref=landed, dst_ref=landed, send_sem=s_sem.at[2 * k + t], recv_sem=r_sem.at[2 * k + t],
                    device_id=sib, device_id_type=MESH).wait_recv()
                cp = pltpu.make_async_remote_copy(
                    src_ref=landed, dst_ref=landed, send_sem=fs_sem.at[2 * k + t], recv_sem=fr_sem.at[2 * k + t],
                    device_id=sib, device_id_type=MESH)
                cp.start()
                passed.append(cp)
        for k, (dx, dy) in enumerate(REL3):
            a = 2 * _flip(x, dx) + _flip(y, dy)
            for t in range(2):
                other = dsts[t].at[a, rows(t, 1 - cc), :]
                pltpu.make_async_remote_copy(
                    src_ref=other, dst_ref=other, send_sem=fs_sem.at[2 * k + t], recv_sem=fr_sem.at[2 * k + t],
                    device_id=sib, device_id_type=MESH).wait_recv()
        for cp in first + passed:
            cp.wait_send()
        for cp in own:
            cp.wait()

    vm = pl.BlockSpec(memory_space=pltpu.VMEM)
    return pl.pallas_call(
        body, name="gather_weights",
        out_shape=(jax.ShapeDtypeStruct((N_CHIPS, D, SHARD_IN), BF16),
                   jax.ShapeDtypeStruct((N_CHIPS, SHARD_OUT, D), BF16)),
        in_specs=[vm, vm], out_specs=(vm, vm),
        scratch_shapes=[pltpu.SemaphoreType.DMA((2,)), pltpu.SemaphoreType.DMA((6,)),
                        pltpu.SemaphoreType.DMA((6,)), pltpu.SemaphoreType.DMA((6,)),
                        pltpu.SemaphoreType.DMA((6,))],
        compiler_params=_params(),
    )(w_in_sh, w_out_sh)


def _reduce_scatter(g4, name):
    _, R, C = g4.shape
    RH = R // 2

    def body(g_ref, out_ref, sib_buf, ici_buf, sem1, sem2s, sem2r, sem3):
        x, y, cc = lax.axis_index("x"), lax.axis_index("y"), lax.axis_index("c")
        chip = 2 * x + y
        sib = (x, y, 1 - cc)
        mine = pl.ds(pl.multiple_of(cc * RH, RH), RH)
        theirs = pl.ds(pl.multiple_of((1 - cc) * RH, RH), RH)
        cp1 = pltpu.make_async_remote_copy(
            src_ref=g_ref.at[:, theirs, :], dst_ref=sib_buf, send_sem=sem1.at[0], recv_sem=sem1.at[1],
            device_id=sib, device_id_type=MESH)
        cp1.start()
        cp1.wait()
        for a in range(N_CHIPS):
            sib_buf[a] = g_ref[a, mine, :] + sib_buf[a]
        sends = []
        for k, (dx, dy) in enumerate(REL3):
            px, py = _flip(x, dx), _flip(y, dy)
            cp = pltpu.make_async_remote_copy(
                src_ref=sib_buf.at[2 * px + py], dst_ref=ici_buf.at[chip],
                send_sem=sem2s.at[k], recv_sem=sem2r.at[k], device_id=(px, py, cc), device_id_type=MESH)
            cp.start()
            sends.append(cp)
        ici_buf[chip] = sib_buf[chip]
        for cp in sends:
            cp.wait()
        out_ref[mine, :] = (ici_buf[0] + ici_buf[1]) + (ici_buf[2] + ici_buf[3])
        cp3 = pltpu.make_async_remote_copy(
            src_ref=out_ref.at[mine, :], dst_ref=out_ref.at[mine, :], send_sem=sem3.at[0], recv_sem=sem3.at[1],
            device_id=sib, device_id_type=MESH)
        cp3.start()
        cp3.wait()

    vm = pl.BlockSpec(memory_space=pltpu.VMEM)
    return pl.pallas_call(
        body, name=name,
        out_shape=jax.ShapeDtypeStruct((R, C), F32),
        in_specs=[vm], out_specs=vm,
        scratch_shapes=[pltpu.VMEM((N_CHIPS, RH, C), F32), pltpu.VMEM((N_CHIPS, RH, C), F32),
                        pltpu.SemaphoreType.DMA((2,)), pltpu.SemaphoreType.DMA((3,)),
                        pltpu.SemaphoreType.DMA((3,)), pltpu.SemaphoreType.DMA((2,))],
        compiler_params=_params(),
    )(g4)


def _all_reduce_small(g, dada):
    R = g.shape[0]
    RH = R // 2
    W = dada.shape[1]

    def body(g_ref, d_ref, out_ref, dall_ref, dslab, sib_buf, ici_buf, ds_sem, dr_sem, sem1, sem2s, sem2r, sem3):
        x, y, cc = lax.axis_index("x"), lax.axis_index("y"), lax.axis_index("c")
        me = 4 * x + 2 * y + cc
        chip = 2 * x + y
        sib = (x, y, 1 - cc)
        dslab[...] = jnp.broadcast_to(d_ref[...], (8, W))
        dall_ref[me] = dslab[...]
        gathers = []
        for k, (dx, dy, dc) in enumerate(REL7):
            cp = pltpu.make_async_remote_copy(
                src_ref=dslab, dst_ref=dall_ref.at[me], send_sem=ds_sem.at[k], recv_sem=dr_sem.at[k],
                device_id=(_flip(x, dx), _flip(y, dy), _flip(cc, dc)), device_id_type=MESH)
            cp.start()
            gathers.append(cp)
        mine = pl.ds(pl.multiple_of(cc * RH, 8), RH)
        theirs = pl.ds(pl.multiple_of((1 - cc) * RH, 8), RH)
        cp1 = pltpu.make_async_remote_copy(
            src_ref=g_ref.at[theirs, :], dst_ref=sib_buf, send_sem=sem1.at[0], recv_sem=sem1.at[1],
            device_id=sib, device_id_type=MESH)
        cp1.start()
        cp1.wait()
        sib_buf[...] = g_ref[mine, :] + sib_buf[...]
        sends = []
        for k, (dx, dy) in enumerate(REL3):
            px, py = _flip(x, dx), _flip(y, dy)
            cp = pltpu.make_async_remote_copy(
                src_ref=sib_buf, dst_ref=ici_buf.at[chip],
                send_sem=sem2s.at[k], recv_sem=sem2r.at[k], device_id=(px, py, cc), device_id_type=MESH)
            cp.start()
            sends.append(cp)
        ici_buf[chip] = sib_buf[...]
        for cp in sends:
            cp.wait()
        out_ref[mine, :] = (ici_buf[0] + ici_buf[1]) + (ici_buf[2] + ici_buf[3])
        cp3 = pltpu.make_async_remote_copy(
            src_ref=out_ref.at[mine, :], dst_ref=out_ref.at[mine, :], send_sem=sem3.at[0], recv_sem=sem3.at[1],
            device_id=sib, device_id_type=MESH)
        cp3.start()
        cp3.wait()
        for cp in gathers:
            cp.wait()

    vm = pl.BlockSpec(memory_space=pltpu.VMEM)
    return pl.pallas_call(
        body, name="all_reduce_small",
        out_shape=(jax.ShapeDtypeStruct((R, 128), F32), jax.ShapeDtypeStruct((8, 8, W), F32)),
        in_specs=[vm, vm], out_specs=(vm, vm),
        scratch_shapes=[pltpu.VMEM((8, W), F32), pltpu.VMEM((RH, 128), F32), pltpu.VMEM((N_CHIPS, RH, 128), F32),
                        pltpu.SemaphoreType.DMA((7,)), pltpu.SemaphoreType.DMA((7,)),
                        pltpu.SemaphoreType.DMA((2,)), pltpu.SemaphoreType.DMA((3,)),
                        pltpu.SemaphoreType.DMA((3,)), pltpu.SemaphoreType.DMA((2,))],
        compiler_params=_params(),
    )(g, dada)


def _in_proj(x, shift, scale, w_pad, b_pad):
    S = x.shape[0]
    tm = min(TM_PROJ, S)

    def body(x_ref, sh_ref, sc_ref, w_ref, b_ref, u_ref, qkv_ref, f_ref, p_ref, g_ref):
        u = (x_ref[...] * (1.0 + sc_ref[...]) + sh_ref[...]).astype(BF16)
        u_ref[...] = u
        qkv_ref[...] = (_dot(u, w_ref[:, O_QKV:O_F]) + b_ref[:, O_QKV:O_F]).astype(BF16)
        f_ref[...] = _dot(u, w_ref[:, O_F:O_P]) + b_ref[:, O_F:O_P]
        p_ref[...] = _dot(u, w_ref[:, O_P:O_G]) + b_ref[:, O_P:O_G]
        g_ref[...] = _dot(u, w_ref[:, O_G:D_PAD]) + b_ref[:, O_G:D_PAD]

    row = lambda w: pl.BlockSpec((tm, w), lambda i: (i, 0))
    full = lambda a: pl.BlockSpec(a.shape, lambda i: (0, 0))
    return pl.pallas_call(
        body, name="in_proj", grid=(S // tm,),
        out_shape=(jax.ShapeDtypeStruct((S, D), BF16), jax.ShapeDtypeStruct((S, 3 * D_ATT), BF16),
                   jax.ShapeDtypeStruct((S, 128), F32), jax.ShapeDtypeStruct((S, D_POOL), F32),
                   jax.ShapeDtypeStruct((S, D), F32)),
        in_specs=[row(D), full(shift), full(scale), full(w_pad), full(b_pad)],
        out_specs=(row(D), row(3 * D_ATT), row(128), row(D_POOL), row(D)),
        compiler_params=_params(dimension_semantics=("arbitrary",)),
    )(x, shift, scale, w_pad, b_pad)


def _forget_cumsum(f):
    S = f.shape[0]
    tm = min(512, S)

    def body(f_ref, out_ref, carry):
        @pl.when(pl.program_id(0) == 0)
        def _():
            carry[...] = jnp.zeros_like(carry)
        v = f_ref[...]
        logf = jnp.minimum(v, 0.0) - jnp.log(1.0 + jnp.exp(-jnp.abs(v)))
        r = lax.broadcasted_iota(jnp.int32, (tm, tm), 0)
        c = lax.broadcasted_iota(jnp.int32, (tm, tm), 1)
        tri = (r >= c).astype(F32)
        cum = jnp.dot(tri, logf, preferred_element_type=F32, precision=lax.Precision.HIGHEST) + carry[...]
        out_ref[...] = cum
        carry[...] = cum[tm - 8:tm, :][7:8, :]

    return pl.pallas_call(
        body, name="forget_cumsum", grid=(S // tm,),
        out_shape=jax.ShapeDtypeStruct((S, 128), F32),
        in_specs=[pl.BlockSpec((tm, 128), lambda i: (i, 0))],
        out_specs=pl.BlockSpec((tm, 128), lambda i: (i, 0)),
        scratch_shapes=[pltpu.VMEM((1, 128), F32)],
        compiler_params=_params(dimension_semantics=("arbitrary",)),
    )(f)


def _pair_select(is_a, va, vb):
    return jnp.where(is_a, va, vb)


def _attention_fwd(qkv, f_pairs, f_rows):
    S = qkv.shape[0]
    T = min(T_ATT, S)
    n_t = S // T

    def body(q_ref, k_ref, v_ref, fq_ref, fk_ref, o_ref, lse_ref, m_sc, l_sc, acc_sc):
        i = pl.program_id(1)
        is_a = lax.broadcasted_iota(jnp.int32, (1, 128), 1) < HEAD_DIM
        q = q_ref[...]
        fq = fq_ref[...]
        fq_heads = (fq[:, 0:1], fq[:, HEAD_DIM:HEAD_DIM + 1])
        m_sc[...] = jnp.full((T, 128), NEG, F32)
        l_sc[...] = jnp.zeros((T, 128), F32)
        acc_sc[...] = jnp.zeros((T, 128), F32)
        zero = jnp.zeros((), BF16)

        def step(j, masked):
            rows = pl.ds(pl.multiple_of(j * T, T), T)
            k = k_ref[rows, :]
            v = v_ref[rows, :]
            fk = fk_ref[j]
            m_old = m_sc[...]
            l_old = l_sc[...]
            new_m, new_l, alphas, pvs = [], [], [], []
            for h in range(2):
                sel = is_a if h == 0 else jnp.logical_not(is_a)
                kh = jnp.where(sel, k, zero)
                vh = jnp.where(sel, v, zero)
                s = _dot_nt(q, kh) + (fq_heads[h] - fk[h:h + 1, :])
                if masked:
                    rr = lax.broadcasted_iota(jnp.int32, (T, T), 0)
                    cc = lax.broadcasted_iota(jnp.int32, (T, T), 1)
                    s = jnp.where(cc <= rr, s, NEG)
                m_prev = m_old[:, h * HEAD_DIM:h * HEAD_DIM + 1]
                l_prev = l_old[:, h * HEAD_DIM:h * HEAD_DIM + 1]
                m_new = jnp.maximum(m_prev, jnp.max(s, axis=1, keepdims=True))
                alpha = jnp.exp(m_prev - m_new)
                p = jnp.exp(s - m_new)
                new_l.append(alpha * l_prev + jnp.sum(p, axis=1, keepdims=True))
                new_m.append(m_new)
                alphas.append(alpha)
                pvs.append(_dot(p.astype(BF16), vh))
            acc_sc[...] = acc_sc[...] * _pair_select(is_a, alphas[0], alphas[1]) + (pvs[0] + pvs[1])
            m_sc[...] = _pair_select(is_a, new_m[0], new_m[1])
            l_sc[...] = _pair_select(is_a, new_l[0], new_l[1])

        def off_diagonal(j, carry):
            step(j, False)
            return carry

        lax.fori_loop(0, i, off_diagonal, 0)
        step(i, True)
        l = l_sc[...]
        o_ref[...] = acc_sc[...] / l
        lse_ref[...] = m_sc[...] + jnp.log(l)

    return pl.pallas_call(
        body, name="attention_fwd", grid=(N_PAIR, n_t),
        out_shape=(jax.ShapeDtypeStruct((S, D_ATT), F32), jax.ShapeDtypeStruct((N_PAIR, S, 128), F32)),
        in_specs=[pl.BlockSpec((T, 128), lambda hp, i: (i, hp)),
                  pl.BlockSpec((S, 128), lambda hp, i: (0, N_PAIR + hp)),
                  pl.BlockSpec((S, 128), lambda hp, i: (0, 2 * N_PAIR + hp)),
                  pl.BlockSpec((None, T, 128), lambda hp, i: (hp, i, 0)),
                  pl.BlockSpec((None, n_t, 2, T), lambda hp, i: (hp, 0, 0, 0))],
        out_specs=(pl.BlockSpec((T, 128), lambda hp, i: (i, hp)),
                   pl.BlockSpec((None, T, 128), lambda hp, i: (hp, i, 0))),
        scratch_shapes=[pltpu.VMEM((T, 128), F32), pltpu.VMEM((T, 128), F32), pltpu.VMEM((T, 128), F32)],
        compiler_params=_params(dimension_semantics=("arbitrary", "arbitrary")),
    )(qkv, qkv, qkv, f_pairs, f_rows)


def _attention_bwd(qkv, datt, att, lse, f_pairs):
    S = qkv.shape[0]
    T = min(T_ATT, S)
    n_t = S // T

    def body(q_ref, do_ref, o_ref, lse_ref, fq_ref, k_ref, v_ref, fk_ref,
             dq_ref, dk_ref, dv_ref, cs_ref, dfk_ref, dfq_ref, stat_sc, dqt_sc):
        j = pl.program_id(1)
        is_a = lax.broadcasted_iota(jnp.int32, (1, 128), 1) < HEAD_DIM
        zero = jnp.zeros((), BF16)

        @pl.when(j == 0)
        def _():
            dqt_sc[...] = jnp.zeros_like(dqt_sc)
            cs_ref[...] = jnp.zeros_like(cs_ref)
            dfq_ref[...] = jnp.zeros_like(dfq_ref)

            def prep(i, carry):
                rows = pl.ds(pl.multiple_of(i * T, T), T)
                prod = o_ref[rows, :] * do_ref[rows, :].astype(F32)
                d_a = jnp.sum(jnp.where(is_a, prod, 0.0), axis=1, keepdims=True)
                d_b = jnp.sum(jnp.where(is_a, 0.0, prod), axis=1, keepdims=True)
                delta_t = _pair_select(is_a, d_a, d_b).T
                bias_t = (fq_ref[rows, :] - lse_ref[rows, :]).T
                stat_sc[i, 0:1, :] = bias_t[0:1, :]
                stat_sc[i, 1:2, :] = bias_t[HEAD_DIM:HEAD_DIM + 1, :]
                stat_sc[i, 2:3, :] = delta_t[0:1, :]
                stat_sc[i, 3:4, :] = delta_t[HEAD_DIM:HEAD_DIM + 1, :]
                return carry

            lax.fori_loop(0, n_t, prep, 0)

        k = k_ref[...]
        v = v_ref[...]
        fk = fk_ref[...]
        kt = k.astype(F32).T
        sub_a = lax.broadcasted_iota(jnp.int32, (128, 1), 0) < HEAD_DIM
        heads = []
        for h in range(2):
            sel = is_a if h == 0 else jnp.logical_not(is_a)
            sel_t = sub_a if h == 0 else jnp.logical_not(sub_a)
            heads.append((sel, jnp.where(sel, k, zero), jnp.where(sel, v, zero),
                          jnp.where(sel_t, kt, 0.0).astype(BF16), fk[:, h * HEAD_DIM:h * HEAD_DIM + 1]))

        def step(i, masked, acc):
            dk_acc, dv_acc, dfa, dfb = acc
            rows = pl.ds(pl.multiple_of(i * T, T), T)
            q = q_ref[rows, :]
            do = do_ref[rows, :]
            stat = stat_sc[i]
            dqt = dqt_sc[i]
            dfs = [dfa, dfb]
            for h in range(2):
                sel, kh, vh, kth, fkh = heads[h]
                arg = _dot_nt(kh, q) + (stat[h:h + 1, :] - fkh)
                if masked:
                    rr = lax.broadcasted_iota(jnp.int32, (T, T), 0)
                    cc = lax.broadcasted_iota(jnp.int32, (T, T), 1)
                    arg = jnp.where(rr <= cc, arg, NEG)
                p_t = jnp.exp(arg)
                dp_t = _dot_nt(vh, do)
                ds_t = p_t * (dp_t - stat[2 + h:3 + h, :])
                ds_bf = ds_t.astype(BF16)
                dv_acc = dv_acc + _dot(p_t.astype(BF16), jnp.where(sel, do, zero))
                dk_acc = dk_acc + _dot(ds_bf, jnp.where(sel, q, zero))
                dqt = dqt + _dot(kth, ds_bf)
                dfs[h] = dfs[h] + jnp.sum(ds_t, axis=1, keepdims=True)
                dfq_ref[i, h:h + 1, :] += jnp.sum(ds_t, axis=0, keepdims=True)
            dqt_sc[i] = dqt
            return dk_acc, dv_acc, dfs[0], dfs[1]

        acc0 = (jnp.zeros((T, 128), F32), jnp.zeros((T, 128), F32), jnp.zeros((T, 1), F32), jnp.zeros((T, 1), F32))
        acc1 = step(j, True, acc0)
        dk_acc, dv_acc, dfa, dfb = lax.fori_loop(j + 1, n_t, lambda i, a: step(i, False, a), acc1)
        dk_ref[...] = dk_acc.astype(BF16)
        dv_ref[...] = dv_acc.astype(BF16)
        dfk_ref[...] = -_pair_select(is_a, dfa, dfb)
        cs_ref[:, 128:256] = cs_ref[:, 128:256] + _colsum(dk_acc)
        cs_ref[:, 256:384] = cs_ref[:, 256:384] + _colsum(dv_acc)

        @pl.when(j == n_t - 1)
        def _():
            def finish(i, tot):
                dq = dqt_sc[i].T
                dq_ref[pl.ds(pl.multiple_of(i * T, T), T), :] = dq.astype(BF16)
                return tot + _colsum(dq)

            cs_ref[:, 0:128] = lax.fori_loop(0, n_t, finish, jnp.zeros((1, 128), F32))

    pair_rows = lambda hp, j: (hp, 0, 0)
    return pl.pallas_call(
        body, name="attention_bwd", grid=(N_PAIR, n_t),
        out_shape=(jax.ShapeDtypeStruct((S, D_ATT), BF16), jax.ShapeDtypeStruct((S, D_ATT), BF16),
                   jax.ShapeDtypeStruct((S, D_ATT), BF16), jax.ShapeDtypeStruct((N_PAIR, 1, 384), F32),
                   jax.ShapeDtypeStruct((N_PAIR, S, 128), F32),
                   jax.ShapeDtypeStruct((N_PAIR, n_t, 8, T), F32)),
        in_specs=[pl.BlockSpec((S, 128), lambda hp, j: (0, hp)),
                  pl.BlockSpec((S, 128), lambda hp, j: (0, hp)),
                  pl.BlockSpec((S, 128), lambda hp, j: (0, hp)),
                  pl.BlockSpec((None, S, 128), pair_rows),
                  pl.BlockSpec((None, S, 128), pair_rows),
                  pl.BlockSpec((T, 128), lambda hp, j: (j, N_PAIR + hp)),
                  pl.BlockSpec((T, 128), lambda hp, j: (j, 2 * N_PAIR + hp)),
                  pl.BlockSpec((None, T, 128), lambda hp, j: (hp, j, 0))],
        out_specs=(pl.BlockSpec((S, 128), lambda hp, j: (0, hp)),
                   pl.BlockSpec((T, 128), lambda hp, j: (j, hp)),
                   pl.BlockSpec((T, 128), lambda hp, j: (j, hp)),
                   pl.BlockSpec((None, 1, 384), pair_rows),
                   pl.BlockSpec((None, T, 128), lambda hp, j: (hp, j, 0)),
                   pl.BlockSpec((None, n_t, 8, T), lambda hp, j: (hp, 0, 0, 0))),
        scratch_shapes=[pltpu.VMEM((n_t, 8, T), F32), pltpu.VMEM((n_t, 128, T), F32)],
        compiler_params=_params(dimension_semantics=("arbitrary", "arbitrary")),
    )(qkv, datt, att, lse, f_pairs, qkv, qkv, f_pairs)


def _window_counts(first_row, n_rows, window):
    t = lax.broadcasted_iota(jnp.int32, (n_rows, 1), 0) + first_row
    return jnp.minimum((t + 1).astype(F32), float(window))


def _middle(x, tgt, att, g, p, gate, w_mix, b_mix, pool_scale, w_out, b_out, ln_g, ln_b):
    S = x.shape[0]
    tm = min(TM_MID, S)
    halo_blocks = tm // POOL_HALO

    def body(x_ref, t_ref, att_ref, g_ref, p_ref, ph_ref, gate_ref, wm_ref, bm_ref, ps_ref, wo_ref, bo_ref,
             lg_ref, lb_ref,
             dh_ref, datt_ref, dg_ref, dpl_ref, gwo_ref, gwm_ref, vec_ref, loss_ref):
        i = pl.program_id(0)

        @pl.when(i == 0)
        def _():
            gwo_ref[...] = jnp.zeros_like(gwo_ref)
            gwm_ref[...] = jnp.zeros_like(gwm_ref)
            vec_ref[...] = jnp.zeros_like(vec_ref)
            loss_ref[...] = jnp.zeros_like(loss_ref)

        pc = p_ref[...]
        halo = jnp.where(i > 0, ph_ref[...], 0.0)
        pe = jnp.concatenate([halo, pc], axis=0)
        pooled_parts = []
        for gi, w in enumerate(POOL_WINDOWS):
            cur = pe[:, gi * POOL_GROUP:(gi + 1) * POOL_GROUP]
            span = 1
            while span < w:
                cur = cur + pltpu.roll(cur, span, 0)
                span *= 2
            wsum = cur[POOL_HALO:, :]
            mean = wsum / _window_counts(i * tm, tm, w)
            pooled_parts.append(mean - pc[:, gi * POOL_GROUP:(gi + 1) * POOL_GROUP])
        pooled_bf =[v.astype(BF16) for v in pooled_parts]
        mixed = jnp.concatenate([_dot(pooled_bf[gi], wm_ref[gi]) for gi in range(4)], axis=1) + bm_ref[...]
        ps = ps_ref[...]
        pool_out = mixed * ps
        gv = g_ref[...]
        sig = _sigmoid(gv)
        silu = gv * sig
        att = att_ref[...]
        y = jnp.concatenate([att * silu[:, :D_ATT], pool_out * silu[:, D_ATT:]], axis=1)
        y_bf = y.astype(BF16)
        wo = wo_ref[...]
        yo = _dot(y_bf, wo) + bo_ref[...]
        gate = gate_ref[...]
        h = ALPHA * x_ref[...] + gate * yo
        mu = jnp.mean(h, axis=1, keepdims=True)
        hc = h - mu
        var = jnp.mean(hc * hc, axis=1, keepdims=True)
        rstd = lax.rsqrt(var + LN_EPS)
        yhat = hc * rstd
        lg = lg_ref[...]
        out = yhat * lg + lb_ref[...]
        err = out - t_ref[...]
        loss_ref[...] += 0.5 * jnp.sum(jnp.mean(err * err, axis=1, keepdims=True), axis=0, keepdims=True)

        dout = err * (1.0 / D)
        g_ln_b = _colsum(dout)
        g_ln_g = _colsum(dout * yhat)
        dyh = dout * lg
        dh = rstd * (dyh - jnp.mean(dyh, axis=1, keepdims=True)
                     - yhat * jnp.mean(dyh * yhat, axis=1, keepdims=True))
        dh_ref[...] = dh
        d_gate = _colsum(dh * yo)
        dyo = gate * dh
        g_b_out = _colsum(dyo)
        dyo_bf = dyo.astype(BF16)
        gwo_ref[...] += _dot_tn(y_bf, dyo_bf)
        dy = _dot_nt(dyo_bf, wo)
        dsilu = sig * (1.0 + gv * (1.0 - sig))
        dy_a = dy[:, :D_ATT]
        dy_p = dy[:, D_ATT:]
        datt_ref[...] = (dy_a * silu[:, :D_ATT]).astype(BF16)
        dpo = dy_p * silu[:, D_ATT:]
        dg = jnp.concatenate([dy_a * att * dsilu[:, :D_ATT], dy_p * pool_out * dsilu[:, D_ATT:]], axis=1)
        dg_ref[...] = dg.astype(BF16)
        g_dg = _colsum(dg)
        g_ps = _colsum(dpo * mixed)
        dmixed = dpo * ps
        g_bm = _colsum(dmixed)
        dmixed_bf = dmixed.astype(BF16)
        dpl = []
        for gi in range(4):
            dm = dmixed_bf[:, gi * POOL_GROUP:(gi + 1) * POOL_GROUP]
            gwm_ref[gi] += _dot_tn(pooled_bf[gi], dm)
            dpl.append(_dot_nt(dm, wm_ref[gi]))
        dpl_ref[...] = jnp.concatenate(dpl, axis=1)
        vec_ref[0:1, :] += g_ln_g
        vec_ref[1:2, :] += g_ln_b
        vec_ref[2:3, :] += d_gate
        vec_ref[3:4, :] += g_b_out
        vec_ref[4:5, :] += g_dg
        vec_ref[5:6, 0:D_POOL] += g_ps
        vec_ref[6:7, 0:D_POOL] += g_bm

    row = lambda w: pl.BlockSpec((tm, w), lambda i: (i, 0))
    full2 = lambda a: pl.BlockSpec(a.shape, lambda i: (0, 0))
    full3 = lambda a: pl.BlockSpec(a.shape, lambda i: (0, 0, 0))
    return pl.pallas_call(
        body, name="middle", grid=(S // tm,),
        out_shape=(jax.ShapeDtypeStruct((S, D), F32),
                   jax.ShapeDtypeStruct((S, D_ATT), BF16),
                   jax.ShapeDtypeStruct((S, D), BF16),
                   jax.ShapeDtypeStruct((S, D_POOL), F32),
                   jax.ShapeDtypeStruct((D, D), F32),
                   jax.ShapeDtypeStruct((4, POOL_GROUP, POOL_GROUP), F32),
                   jax.ShapeDtypeStruct((8, D), F32),
                   jax.ShapeDtypeStruct((1, 1), F32)),
        in_specs=[row(D), row(D), row(D_ATT), row(D), row(D_POOL),
                  pl.BlockSpec((POOL_HALO, D_POOL), lambda i: (jnp.maximum(i * halo_blocks - 1, 0), 0)),
                  full2(gate), full3(w_mix), full2(b_mix), full2(pool_scale), full2(w_out), full2(b_out),
                  full2(ln_g), full2(ln_b)],
        out_specs=(row(D), row(D_ATT), row(D), row(D_POOL),
                   pl.BlockSpec((D, D), lambda i: (0, 0)),
                   pl.BlockSpec((4, POOL_GROUP, POOL_GROUP), lambda i: (0, 0, 0)),
                   pl.BlockSpec((8, D), lambda i: (0, 0)),
                   pl.BlockSpec((1, 1), lambda i: (0, 0))),
        compiler_params=_params(dimension_semantics=("arbitrary",)),
    )(x, tgt, att, g, p, p, gate, w_mix, b_mix, pool_scale, w_out, b_out, ln_g, ln_b)


def _tail(dpl, dfk8, f):
    S = dpl.shape[0]
    tm = min(TM_TAIL, S)
    n_t = S // tm
    halo_blocks = tm // POOL_HALO
    last_halo = S // POOL_HALO - 1

    def body(d_ref, dn_ref, dfk_ref, f_ref, dp_ref, df_ref, cs_ref, carry):
        s = pl.program_id(0)
        i = n_t - 1 - s

        @pl.when(s == 0)
        def _():
            carry[...] = jnp.zeros_like(carry)
            cs_ref[...] = jnp.zeros_like(cs_ref)

        dc = d_ref[...]
        nxt = jnp.where(s > 0, dn_ref[...], 0.0)
        de = jnp.concatenate([dc, nxt], axis=0)
        n_e = tm + POOL_HALO
        parts = []
        for gi, w in enumerate(POOL_WINDOWS):
            cur = de[:, gi * POOL_GROUP:(gi + 1) * POOL_GROUP] / _window_counts(i * tm, n_e, w)
            span = 1
            while span < w:
                cur = cur + pltpu.roll(cur, n_e - span, 0)
                span *= 2
            parts.append(cur[:tm, :] - dc[:, gi * POOL_GROUP:(gi + 1) * POOL_GROUP])
        dp = jnp.concatenate(parts, axis=1)
        dp_ref[...] = dp.astype(BF16)
        cs_ref[0:1, :] += _colsum(dp)

        r = lax.broadcasted_iota(jnp.int32, (tm, tm), 0)
        c = lax.broadcasted_iota(jnp.int32, (tm, tm), 1)
        tri = (r <= c).astype(F32)
        dlogf = jnp.dot(tri, dfk_ref[...], preferred_element_type=F32, precision=lax.Precision.HIGHEST) + carry[...]
        carry[...] = dlogf[0:1, :]
        df = dlogf * _sigmoid(-f_ref[...])
        df_ref[...] = df.astype(BF16)
        cs_ref[1:2, 0:128] += _colsum(df)

    rev = lambda w: pl.BlockSpec((tm, w), lambda s: (n_t - 1 - s, 0))
    return pl.pallas_call(
        body, name="tail", grid=(n_t,),
        out_shape=(jax.ShapeDtypeStruct((S, D_POOL), BF16), jax.ShapeDtypeStruct((S, 128), BF16),
                   jax.ShapeDtypeStruct((8, D_POOL), F32)),
        in_specs=[rev(D_POOL),
                  pl.BlockSpec((POOL_HALO, D_POOL),
                               lambda s: (jnp.minimum((n_t - s) * halo_blocks, last_halo), 0)),
                  rev(128), rev(128)],
        out_specs=(rev(D_POOL), rev(128), pl.BlockSpec((8, D_POOL), lambda s: (0, 0))),
        scratch_shapes=[pltpu.VMEM((1, 128), F32)],
        compiler_params=_params(dimension_semantics=("arbitrary",)),
    )(dpl, dpl, dfk8, f)


PIECES = ((O_QKV, D_ATT), (O_QKV + D_ATT, D_ATT), (O_QKV + 2 * D_ATT, D_ATT), (O_F, 128), (O_P, D_POOL), (O_G, D))


def _grad_w_in(u, pieces):
    S = u.shape[0]
    tm = min(TM_GW, S)
    n_t = S // tm

    def body(u_ref, *rest):
        piece_refs, out_ref, acc, sem = rest[:6], rest[6], rest[7], rest[8]
        i = pl.program_id(0)

        @pl.when(i == 0)
        def _():
            acc[...] = jnp.zeros_like(acc)

        u_t = u_ref[...]
        for (off, w), ref in zip(PIECES, piece_refs):
            acc[:, off:off + w] += _dot_tn(u_t, ref[...])

        @pl.when(i == n_t - 1)
        def _():
            cp = pltpu.make_async_copy(acc, out_ref, sem)
            cp.start()
            cp.wait()

    return pl.pallas_call(
        body, name="grad_w_in", grid=(n_t,),
        out_shape=jax.ShapeDtypeStruct((D, D_PAD), F32),
        in_specs=[pl.BlockSpec((tm, D), lambda i: (i, 0))]
        + [pl.BlockSpec((tm, w), lambda i: (i, 0)) for _, w in PIECES],
        out_specs=pl.BlockSpec(memory_space=pl.ANY),
        scratch_shapes=[pltpu.VMEM((D, D_PAD), F32), pltpu.SemaphoreType.DMA],
        compiler_params=_params(dimension_semantics=("arbitrary",)),
    )(u, *pieces)


def _grad_x(pieces, w_pad, dh, x, scale):
    S = x.shape[0]
    tm = min(TM_DU, S)

    def body(*refs):
        piece_refs = refs[:6]
        w_ref, dh_ref, x_ref, sc_ref, gx_ref, vec_ref = refs[6:]

        @pl.when(pl.program_id(0) == 0)
        def _():
            vec_ref[...] = jnp.zeros_like(vec_ref)

        du = jnp.zeros((tm, D), F32)
        for (off, w), ref in zip(PIECES, piece_refs):
            du = du + _dot_nt(ref[...], w_ref[:, off:off + w])
        xv = x_ref[...]
        gx_ref[...] = ALPHA * dh_ref[...] + du * (1.0 + sc_ref[...])
        vec_ref[0:1, :] += _colsum(du)
        vec_ref[1:2, :] += _colsum(du * xv)

    row = lambda w: pl.BlockSpec((tm, w), lambda i: (i, 0))
    return pl.pallas_call(
        body, name="grad_x", grid=(S // tm,),
        out_shape=(jax.ShapeDtypeStruct((S, D), F32), jax.ShapeDtypeStruct((8, D), F32)),
        in_specs=[row(w) for _, w in PIECES]
        + [pl.BlockSpec(w_pad.shape, lambda i: (0, 0)), row(D), row(D), pl.BlockSpec((1, D), lambda i: (0, 0))],
        out_specs=(row(D), pl.BlockSpec((8, D), lambda i: (0, 0))),
        compiler_params=_params(dimension_semantics=("arbitrary",)),
    )(*pieces, w_pad, dh, x, scale)


def _grad_ada(c_all, dada_all, dada_cols):
    def body(c_ref, dall_ref, dcol_ref, gw_ref, gb_ref):
        rows = lax.broadcasted_iota(jnp.int32, (8, 1), 0)
        cm = jnp.zeros((8, D), F32)
        dm = jnp.zeros((8, 3 * D), F32)
        for r in range(8):
            cm = jnp.where(rows == r, c_ref[r], cm)
            dm = jnp.where(rows == r, dall_ref[r], dm)
        act = cm * _sigmoid(cm)
        pad = jnp.zeros((8, D), F32)
        lhs = jnp.concatenate([act, pad], axis=0).astype(BF16)
        rhs = jnp.concatenate([dcol_ref[...], jnp.zeros((8, SHARD_ADA), F32)], axis=0).astype(BF16)
        gw_ref[...] = _dot_tn(lhs, rhs)
        gb_ref[...] = _colsum(dm)

    vm = pl.BlockSpec(memory_space=pltpu.VMEM)
    return pl.pallas_call(
        body, name="grad_ada",
        out_shape=(jax.ShapeDtypeStruct((D, SHARD_ADA), F32), jax.ShapeDtypeStruct((1, 3 * D), F32)),
        in_specs=[vm, vm, vm], out_specs=(vm, vm),
        compiler_params=_params(),
    )(c_all, dada_all, dada_cols)


def _adamw_math(w, g, m, v):
    m = ADAM_B1 * m + (1.0 - ADAM_B1) * g
    v = ADAM_B2 * v + (1.0 - ADAM_B2) * (g * g)
    m_hat = m / (1.0 - ADAM_B1 ** ADAM_STEP)
    v_hat = v / (1.0 - ADAM_B2 ** ADAM_STEP)
    delta = -ADAM_LR * (m_hat / (jnp.sqrt(v_hat) + ADAM_EPS) + ADAM_WD * w)
    return delta, m, v


def _adamw(groups, n_steps):
    n = len(groups)

    def body(*refs):
        ins, outs = refs[:4 * n], refs[4 * n:]
        for t in range(n):
            w, g, m, v = (r[...] for r in ins[4 * t:4 * t + 4])
            d, m2, v2 = _adamw_math(w, g, m, v)
            outs[3 * t][...] = d
            outs[3 * t + 1][...] = m2
            outs[3 * t + 2][...] = v2

    in_specs, out_specs, out_shape, args = [], [], [], []
    for (w, g, m, v) in groups:
        r, c = w.shape
        spec = pl.BlockSpec((r // n_steps, c), lambda i: (i, 0))
        in_specs += [spec] * 4
        out_specs += [spec] * 3
        out_shape += [jax.ShapeDtypeStruct((r, c), F32)] * 3
        args += [w, g, m, v]
    return pl.pallas_call(
        body, name="adamw_%d" % n, grid=(n_steps,),
        out_shape=tuple(out_shape), in_specs=in_specs, out_specs=tuple(out_specs),
        compiler_params=_params(dimension_semantics=("arbitrary",)),
    )(*args)


def _pack_small(parts):
    rows = []
    used = 0
    for name, (first, n_rows) in SMALL_SEGS.items():
        if first > used:
            rows.append(jnp.zeros((first - used, 128), F32))
        flat = parts[name].reshape(-1)
        flat = jnp.pad(flat, (0, n_rows * 128 - flat.shape[0]))
        rows.append(flat.reshape(n_rows, 128))
        used = first + n_rows
    rows.append(jnp.zeros((SMALL_ROWS - used, 128), F32))
    return jnp.concatenate(rows, axis=0)


def _unpack_small(buf, name, shape):
    first, n_rows = SMALL_SEGS[name]
    n = int(np.prod(shape))
    return buf[first:first + n_rows].reshape(-1)[:n].reshape(shape)


def _pad_in(v):
    r = v.shape[0]
    z = jnp.zeros((r, O_P - O_F - N_HEADS), v.dtype)
    return jnp.concatenate([v[:, :3 * D_ATT + N_HEADS], z, v[:, 3 * D_ATT + N_HEADS:]], axis=1)


def _unpad_in(v):
    return jnp.concatenate([v[:, :O_F + N_HEADS], v[:, O_P:]], axis=1)


def kernel(x, c, w_ada, b_ada, w_in, b_in, w_pool_mix, b_pool_mix, pool_scale, w_out, b_out, ln_g, ln_b, loss_target, m_w_ada, m_b_ada, m_w_in, m_b_in, m_w_pool_mix, m_b_pool_mix, m_pool_scale, m_w_out, m_b_out, m_ln_g, m_ln_b, v_w_ada, v_b_ada, v_w_in, v_b_in, v_w_pool_mix, v_b_pool_mix, v_pool_scale, v_w_out, v_b_out, v_ln_g, v_ln_b):
    S = x.shape[1]
    T = min(T_ATT, S)
    n_t = S // T
    chip = 2 * lax.axis_index("x") + lax.axis_index("y")
    x2 = x[0]
    tgt = loss_target[0]
    q_scale = jnp.concatenate([jnp.full((1, D_ATT), Q_SCALE, F32), jnp.ones((1, D_PAD - D_ATT), F32)], axis=1)

    c_all, ada4 = _ada_exchange(c, w_ada[0], b_ada.reshape(4, 1, SHARD_ADA))
    ada = ada4[:, 0, :].reshape(1, 3 * D)
    shift, scale, gate = ada[:, :D], ada[:, D:2 * D], ada[:, 2 * D:]
    w_in_all, w_out_all = _gather_weights(w_in[0].astype(BF16), w_out[0].astype(BF16))
    w_in_full = jnp.transpose(w_in_all, (1, 0, 2)).reshape(D, D_IN)
    w_pad = _pad_in(w_in_full) * q_scale.astype(BF16)
    b_pad = _pad_in(b_in) * q_scale
    w_out_full = w_out_all.reshape(D, D)
    w_mix_bf = w_pool_mix[0].astype(BF16)

    u, qkv, f, p, g = _in_proj(x2, shift, scale, w_pad, b_pad)
    big_f = _forget_cumsum(f)
    f8 = big_f[:, :N_HEADS]
    f_pairs = jnp.repeat(jnp.transpose(f8.reshape(S, N_PAIR, 2), (1, 0, 2)), HEAD_DIM, axis=2)
    f_rows = jnp.transpose(f8.reshape(n_t, T, N_PAIR, 2), (2, 0, 3, 1))
    att, lse = _attention_fwd(qkv, f_pairs, f_rows)

    dh, datt, dg, dpl, gw_out, gw_mix, vec, loss_part = _middle(
        x2, tgt, att, g, p, gate, w_mix_bf, b_pool_mix.reshape(1, D_POOL), pool_scale, w_out_full, b_out, ln_g, ln_b)
    dq, dk, dv, cs_att, dfk, dfq = _attention_bwd(qkv, datt, att, lse, f_pairs)
    dfk8 = jnp.transpose(dfk[:, :, ::HEAD_DIM], (1, 0, 2)).reshape(S, N_HEADS)
    dfk8 = dfk8 + jnp.transpose(dfq[:, :, 0:2, :], (1, 3, 0, 2)).reshape(S, N_HEADS)
    dfk8 = jnp.pad(dfk8, ((0, 0), (0, 128 - N_HEADS)))
    dp, df, cs_tail = _tail(dpl, dfk8, f)
    pieces = (dq, dk, dv, df, dp, dg)
    gw_pad = _grad_w_in(u, pieces)
    grad_x, vec_x = _grad_x(pieces, w_pad, dh, x2, scale)

    gw_in_part = _unpad_in(gw_pad * q_scale)
    cs_qkv = jnp.transpose(cs_att.reshape(N_PAIR, 3, 128), (1, 0, 2)).reshape(1, 3 * D_ATT)
    gb_pad = jnp.concatenate([cs_qkv, cs_tail[1:2, 0:128], cs_tail[0:1, :], vec[4:5, :]], axis=1) * q_scale
    dada = jnp.concatenate([vec_x[0:1, :], vec_x[1:2, :], vec[2:3, :]], axis=1)
    small = _pack_small({
        "b_in": gb_pad, "w_pool_mix": gw_mix, "b_pool_mix": vec[6:7, :D_POOL], "pool_scale": vec[5:6, :D_POOL],
        "b_out": vec[3:4, :], "ln_g": vec[0:1, :], "ln_b": vec[1:2, :], "b_ada": jnp.zeros((1, 3 * D), F32)})

    g_w_in = _reduce_scatter(jnp.transpose(gw_in_part.reshape(D, N_CHIPS, SHARD_IN), (1, 0, 2)), "reduce_w_in")
    g_w_out = _reduce_scatter(gw_out.reshape(N_CHIPS, SHARD_OUT, D), "reduce_w_out")
    small_sum, dada_all = _all_reduce_small(small[:SMALL_REDUCED_ROWS], dada)
    dada_cols = lax.dynamic_slice(dada_all[:, 0, :], (0, chip * SHARD_ADA), (8, SHARD_ADA))
    g_w_ada, g_b_ada = _grad_ada(c_all, dada_all, dada_cols)
    loss = lax.psum(loss_part[0, 0], ("x", "y", "c"))

    grads_small = jnp.concatenate(
        [small_sum, jnp.pad(g_b_ada.reshape(24, 128), ((0, SMALL_ROWS - SMALL_REDUCED_ROWS - 24), (0, 0)))], axis=0)
    small_w = {"b_in": _pad_in(b_in), "w_pool_mix": w_pool_mix, "b_pool_mix": b_pool_mix, "pool_scale": pool_scale,
               "b_out": b_out, "ln_g": ln_g, "ln_b": ln_b, "b_ada": b_ada}
    small_m = {"b_in": _pad_in(m_b_in), "w_pool_mix": m_w_pool_mix, "b_pool_mix": m_b_pool_mix,
               "pool_scale": m_pool_scale, "b_out": m_b_out, "ln_g": m_ln_g, "ln_b": m_ln_b, "b_ada": m_b_ada}
    small_v = {"b_in": _pad_in(v_b_in), "w_pool_mix": v_w_pool_mix, "b_pool_mix": v_b_pool_mix,
               "pool_scale": v_pool_scale, "b_out": v_b_out, "ln_g": v_ln_g, "ln_b": v_ln_b, "b_ada": v_b_ada}
    big = _adamw([(w_ada[0], g_w_ada, m_w_ada[0], v_w_ada[0]),
                  (w_in[0], g_w_in, m_w_in[0], v_w_in[0]),
                  (w_out[0], g_w_out, m_w_out[0], v_w_out[0])], 8)
    sm = _adamw([(_pack_small(small_w), grads_small, _pack_small(small_m), _pack_small(small_v))], 1)

    names = ["w_ada", "b_ada", "w_in", "b_in", "w_pool_mix", "b_pool_mix", "pool_scale", "w_out", "b_out",
             "ln_g", "ln_b"]
    shapes = {"b_ada": (1, 3 * D), "b_in": (1, D_PAD), "w_pool_mix": (1, 4, POOL_GROUP, POOL_GROUP),
              "b_pool_mix": (1, 4, POOL_GROUP), "pool_scale": (1, D_POOL), "b_out": (1, D), "ln_g": (1, D),
              "ln_b": (1, D)}
    big_idx = {"w_ada": 0, "w_in": 1, "w_out": 2}

    def leaf(kind, name):
        if name in big_idx:
            if kind == 0:
                return (g_w_ada, g_w_in, g_w_out)[big_idx[name]][None]
            return big[3 * big_idx[name] + kind - 1][None]
        buf = grads_small if kind == 0 else sm[kind - 1]
        val = _unpack_small(buf, name, shapes[name])
        if name == "b_in":
            val = _unpad_in(val)
        return val

    outs = [loss, grad_x[None]]
    for kind in range(4):
        outs += [leaf(kind, n) for n in names]
    return tuple(outs)
```

```python
import functools

import numpy as np
import jax
import jax.numpy as jnp
from jax import lax
from jax.experimental import pallas as pl
from jax.experimental.pallas import tpu as pltpu

F32 = jnp.float32
BF16 = jnp.bfloat16
MESH = pl.DeviceIdType.MESH

D = 1024
D_ATT = 512
D_POOL = 512
N_HEADS = 8
HEAD_DIM = 64
N_PAIR = N_HEADS // 2
POOL_WINDOWS = (2, 4, 8, 16)
POOL_GROUP = 128
POOL_HALO = 16
LN_EPS = 1e-5
ALPHA = 2.0 ** 0.25
D_IN = 3 * D_ATT + N_HEADS + D_POOL + D_ATT + D_POOL
N_CHIPS = 4
SHARD_IN = D_IN // N_CHIPS
SHARD_ADA = 3 * D // N_CHIPS
SHARD_OUT = D // N_CHIPS

O_QKV, O_F, O_P, O_G, D_PAD = 0, 1536, 1664, 2176, 3200
Q_SCALE = HEAD_DIM ** -0.5

ADAM_LR, ADAM_B1, ADAM_B2, ADAM_EPS, ADAM_WD, ADAM_STEP = 0.001, 0.9, 0.999, 1e-08, 0.01, 10

NEG = -1e30

VMEM_LIMIT = 56 * 1024 * 1024

TM_PROJ = 512
T_ATT = 512
ATT_CHUNK = 32
TM_MID = 256
TM_TAIL = 512
TM_GW = 512
TM_DU = 512

REL7 = [(0, 0, 1), (0, 1, 0), (0, 1, 1), (1, 0, 0), (1, 0, 1), (1, 1, 0), (1, 1, 1)]
REL3 = [(0, 1), (1, 0), (1, 1)]

SMALL_SEGS = {}
_row = 0
for _name, _n in (("b_in", 3200), ("w_pool_mix", 65536), ("b_pool_mix", 512), ("pool_scale", 512),
                  ("b_out", 1024), ("ln_g", 1024), ("ln_b", 1024)):
    _rows = -(-_n // 1024) * 8
    SMALL_SEGS[_name] = (_row, _rows)
    _row += _rows
SMALL_REDUCED_ROWS = -(-_row // 16) * 16
SMALL_SEGS["b_ada"] = (SMALL_REDUCED_ROWS, 24)
SMALL_ROWS = SMALL_REDUCED_ROWS + 24


def _params(**kw):
    return pltpu.CompilerParams(vmem_limit_bytes=VMEM_LIMIT, **kw)


def _flip(v, d):
    return v if d == 0 else 1 - v


def _dot(a, b):
    return jnp.dot(a, b, preferred_element_type=F32)


def _dot_nt(a, b):
    return lax.dot_general(a, b, (((1,), (1,)), ((), ())), preferred_element_type=F32)


def _dot_tn(a, b):
    return lax.dot_general(a, b, (((0,), (0,)), ((), ())), preferred_element_type=F32)


def _sigmoid(v):
    return 1.0 / (1.0 + jnp.exp(-v))


def _colsum(v):
    return jnp.sum(v, axis=0, keepdims=True)


def _ada_exchange(c, w_ada, b_ada4):
    def body(c_ref, w_ref, b_ref, call_ref, ada_ref, cslab, sbuf, rbuf, cs_sem, cr_sem, as_sem, ar_sem):
        x, y, cc = lax.axis_index("x"), lax.axis_index("y"), lax.axis_index("c")
        me = 4 * x + 2 * y + cc
        chip = 2 * x + y
        cslab[...] = jnp.broadcast_to(c_ref[...], (8, D))
        call_ref[me] = cslab[...]
        gathers = []
        for k, (dx, dy, dc) in enumerate(REL7):
            cp = pltpu.make_async_remote_copy(
                src_ref=cslab, dst_ref=call_ref.at[me], send_sem=cs_sem.at[k], recv_sem=cr_sem.at[k],
                device_id=(_flip(x, dx), _flip(y, dy), _flip(cc, dc)), device_id_type=MESH)
            cp.start()
            gathers.append(cp)
        for cp in gathers:
            cp.wait()
        rows = lax.broadcasted_iota(jnp.int32, (8, 1), 0)
        mat = jnp.zeros((8, D), F32)
        for r in range(8):
            mat = jnp.where(rows == r, call_ref[r], mat)
        act = (mat * _sigmoid(mat)).astype(BF16)
        part = _dot(act, w_ref[...].astype(BF16))
        sends = []
        for k, (dx, dy) in enumerate(REL3):
            px, py = _flip(x, dx), _flip(y, dy)
            r = 4 * px + 2 * py + cc
            piece = _colsum(jnp.where(rows == r, part, 0.0))
            sbuf[k] = jnp.broadcast_to(piece, (8, SHARD_ADA))
            cp = pltpu.make_async_remote_copy(
                src_ref=sbuf.at[k], dst_ref=rbuf.at[k], send_sem=as_sem.at[k], recv_sem=ar_sem.at[k],
                device_id=(px, py, cc), device_id_type=MESH)
            cp.start()
            sends.append(cp)
        own = _colsum(jnp.where(rows == me, part, 0.0))
        ada_ref[chip] = jnp.broadcast_to(own, (8, SHARD_ADA)) + b_ref[chip]
        for k, (dx, dy) in enumerate(REL3):
            sends[k].wait()
            a = 2 * _flip(x, dx) + _flip(y, dy)
            ada_ref[a] = rbuf[k] + b_ref[a]

    vm = pl.BlockSpec(memory_space=pltpu.VMEM)
    return pl.pallas_call(
        body, name="ada_exchange",
        out_shape=(jax.ShapeDtypeStruct((8, 8, D), F32), jax.ShapeDtypeStruct((4, 8, SHARD_ADA), F32)),
        in_specs=[vm, vm, vm], out_specs=(vm, vm),
        scratch_shapes=[pltpu.VMEM((8, D), F32), pltpu.VMEM((3, 8, SHARD_ADA), F32),
                        pltpu.VMEM((3, 8, SHARD_ADA), F32),
                        pltpu.SemaphoreType.DMA((7,)), pltpu.SemaphoreType.DMA((7,)),
                        pltpu.SemaphoreType.DMA((3,)), pltpu.SemaphoreType.DMA((3,))],
        compiler_params=_params(),
    )(c, w_ada, b_ada4)


def _gather_weights(w_in_sh, w_out_sh):
    def body(win_ref, wout_ref, win_all, wout_all, own_sem, s_sem, r_sem, fs_sem, fr_sem):
        x, y, cc = lax.axis_index("x"), lax.axis_index("y"), lax.axis_index("c")
        chip = 2 * x + y
        sib = (x, y, 1 - cc)
        srcs = (win_ref, wout_ref)
        dsts = (win_all, wout_all)
        halves = (D // 2, SHARD_OUT // 2)

        def rows(t, which):
            h = halves[t]
            return pl.ds(pl.multiple_of(which * h, h), h)

        own = [pltpu.make_async_copy(srcs[t], dsts[t].at[chip], own_sem.at[t]) for t in range(2)]
        for cp in own:
            cp.start()
        first = []
        for k, (dx, dy) in enumerate(REL3):
            peer = (_flip(x, dx), _flip(y, dy), cc)
            for t in range(2):
                cp = pltpu.make_async_remote_copy(
                    src_ref=srcs[t].at[rows(t, cc), :], dst_ref=dsts[t].at[chip, rows(t, cc), :],
                    send_sem=s_sem.at[2 * k + t], recv_sem=r_sem.at[2 * k + t],
                    device_id=peer, device_id_type=MESH)
                cp.start()
                first.append(cp)
        passed = []
        for k, (dx, dy) in enumerate(REL3):
            a = 2 * _flip(x, dx) + _flip(y, dy)
            for t in range(2):
                landed = dsts[t].at[a, rows(t, cc), :]
                pltpu.make_async_remote_copy(
                    src_ref=landed, dst_ref=landed, send_sem=s_sem.at[2 * k + t], recv_sem=r_sem.at[2 * k + t],
                    device_id=sib, device_id_type=MESH).wait_recv()
                cp = pltpu.make_async_remote_copy(
                    src_ref=landed, dst_ref=landed, send_sem=fs_sem.at[2 * k + t], recv_sem=fr_sem.at[2 * k + t],
                    device_id=sib, device_id_type=MESH)
                cp.start()
                passed.append(cp)
        for k, (dx, dy) in enumerate(REL3):
            a = 2 * _flip(x, dx) + _flip(y, dy)
            for t in range(2):
                other = dsts[t].at[a, rows(t, 1 - cc), :]
                pltpu.make_async_remote_copy(
                    src_ref=other, dst_ref=other, send_sem=fs_sem.at[2 * k + t], recv_sem=fr_sem.at[2 * k + t],
                    device_id=sib, device_id_type=MESH).wait_recv()
        for cp in first + passed:
            cp.wait_send()
        for cp in own:
            cp.wait()

    vm = pl.BlockSpec(memory_space=pltpu.VMEM)
    return pl.pallas_call(
        body, name="gather_weights",
        out_shape=(jax.ShapeDtypeStruct((N_CHIPS, D, SHARD_IN), BF16),
                   jax.ShapeDtypeStruct((N_CHIPS, SHARD_OUT, D), BF16)),
        in_specs=[vm, vm], out_specs=(vm, vm),
        scratch_shapes=[pltpu.SemaphoreType.DMA((2,)), pltpu.SemaphoreType.DMA((6,)),
                        pltpu.SemaphoreType.DMA((6,)), pltpu.SemaphoreType.DMA((6,)),
                        pltpu.SemaphoreType.DMA((6,))],
        compiler_params=_params(),
    )(w_in_sh, w_out_sh)


def _reduce_scatter(g4, scale4, name):
    _, R, C = g4.shape
    RH = R // 2

    def body(g_ref, sc_ref, out_ref, sib_buf, send_buf, ici_buf, sem1, sem2s, sem2r, sem3):
        x, y, cc = lax.axis_index("x"), lax.axis_index("y"), lax.axis_index("c")
        chip = 2 * x + y
        sib = (x, y, 1 - cc)
        mine = pl.ds(pl.multiple_of(cc * RH, RH), RH)
        theirs = pl.ds(pl.multiple_of((1 - cc) * RH, RH), RH)
        cp1 = pltpu.make_async_remote_copy(
            src_ref=g_ref.at[:, theirs, :], dst_ref=sib_buf, send_sem=sem1.at[0], recv_sem=sem1.at[1],
            device_id=sib, device_id_type=MESH)
        cp1.start()
        cp1.wait()
        for a in range(N_CHIPS):
            both = g_ref[a, mine, :] + sib_buf[a]
            sib_buf[a] = both
            send_buf[a] = both.astype(BF16)
        sends = []
        for k, (dx, dy) in enumerate(REL3):
            px, py = _flip(x, dx), _flip(y, dy)
            cp = pltpu.make_async_remote_copy(
                src_ref=send_buf.at[2 * px + py], dst_ref=ici_buf.at[chip],
                send_sem=sem2s.at[k], recv_sem=sem2r.at[k], device_id=(px, py, cc), device_id_type=MESH)
            cp.start()
            sends.append(cp)
        ici_buf[chip] = send_buf[chip]
        for cp in sends:
            cp.wait()
        own = sib_buf[chip]
        parts = [jnp.where(chip == a, own, ici_buf[a].astype(F32)) for a in range(N_CHIPS)]
        out_ref[mine, :] = ((parts[0] + parts[1]) + (parts[2] + parts[3])) * sc_ref[chip]
        cp3 = pltpu.make_async_remote_copy(
            src_ref=out_ref.at[mine, :], dst_ref=out_ref.at[mine, :], send_sem=sem3.at[0], recv_sem=sem3.at[1],
            device_id=sib, device_id_type=MESH)
        cp3.start()
        cp3.wait()

    vm = pl.BlockSpec(memory_space=pltpu.VMEM)
    return pl.pallas_call(
        body, name=name,
        out_shape=jax.ShapeDtypeStruct((R, C), F32),
        in_specs=[vm, vm], out_specs=vm,
        scratch_shapes=[pltpu.VMEM((N_CHIPS, RH, C), F32), pltpu.VMEM((N_CHIPS, RH, C), BF16),
                        pltpu.VMEM((N_CHIPS, RH, C), BF16),
                        pltpu.SemaphoreType.DMA((2,)), pltpu.SemaphoreType.DMA((3,)),
                        pltpu.SemaphoreType.DMA((3,)), pltpu.SemaphoreType.DMA((2,))],
        compiler_params=_params(),
    )(g4, scale4)


def _all_reduce_small(g, dada):
    R = g.shape[0]
    RH = R // 2
    W = dada.shape[1]

    def body(g_ref, d_ref, out_ref, dall_ref, dslab, sib_buf, ici_buf, ds_sem, dr_sem, sem1, sem2s, sem2r, sem3):
        x, y, cc = lax.axis_index("x"), lax.axis_index("y"), lax.axis_index("c")
        me = 4 * x + 2 * y + cc
        chip = 2 * x + y
        sib = (x, y, 1 - cc)
        dslab[...] = jnp.broadcast_to(d_ref[...], (8, W))
        dall_ref[me] = dslab[...]
        gathers = []
        for k, (dx, dy, dc) in enumerate(REL7):
            cp = pltpu.make_async_remote_copy(
                src_ref=dslab, dst_ref=dall_ref.at[me], send_sem=ds_sem.at[k], recv_sem=dr_sem.at[k],
                device_id=(_flip(x, dx), _flip(y, dy), _flip(cc, dc)), device_id_type=MESH)
            cp.start()
            gathers.append(cp)
        mine = pl.ds(pl.multiple_of(cc * RH, 8), RH)
        theirs = pl.ds(pl.multiple_of((1 - cc) * RH, 8), RH)
        cp1 = pltpu.make_async_remote_copy(
            src_ref=g_ref.at[theirs, :], dst_ref=sib_buf, send_sem=sem1.at[0], recv_sem=sem1.at[1],
            device_id=sib, device_id_type=MESH)
        cp1.start()
        cp1.wait()
        sib_buf[...] = g_ref[mine, :] + sib_buf[...]
        sends = []
        for k, (dx, dy) in enumerate(REL3):
            px, py = _flip(x, dx), _flip(y, dy)
            cp = pltpu.make_async_remote_copy(
                src_ref=sib_buf, dst_ref=ici_buf.at[chip],
                send_sem=sem2s.at[k], recv_sem=sem2r.at[k], device_id=(px, py, cc), device_id_type=MESH)
            cp.start()
            sends.append(cp)
        ici_buf[chip] = sib_buf[...]
        for cp in sends:
            cp.wait()
        out_ref[mine, :] = (ici_buf[0] + ici_buf[1]) + (ici_buf[2] + ici_buf[3])
        cp3 = pltpu.make_async_remote_copy(
            src_ref=out_ref.at[mine, :], dst_ref=out_ref.at[mine, :], send_sem=sem3.at[0], recv_sem=sem3.at[1],
            device_id=sib, device_id_type=MESH)
        cp3.start()
        cp3.wait()
        for cp in gathers:
            cp.wait()

    vm = pl.BlockSpec(memory_space=pltpu.VMEM)
    return pl.pallas_call(
        body, name="all_reduce_small",
        out_shape=(jax.ShapeDtypeStruct((R, 128), F32), jax.ShapeDtypeStruct((8, 8, W), F32)),
        in_specs=[vm, vm], out_specs=(vm, vm),
        scratch_shapes=[pltpu.VMEM((8, W), F32), pltpu.VMEM((RH, 128), F32), pltpu.VMEM((N_CHIPS, RH, 128), F32),
                        pltpu.SemaphoreType.DMA((7,)), pltpu.SemaphoreType.DMA((7,)),
                        pltpu.SemaphoreType.DMA((2,)), pltpu.SemaphoreType.DMA((3,)),
                        pltpu.SemaphoreType.DMA((3,)), pltpu.SemaphoreType.DMA((2,))],
        compiler_params=_params(),
    )(g, dada)


def _in_proj(x, shift, scale, w_pad, b_pad):
    S = x.shape[0]
    tm = min(TM_PROJ, S)

    def body(x_ref, sh_ref, sc_ref, w_ref, b_ref, u_ref, qkv_ref, f_ref, p_ref, g_ref):
        u = (x_ref[...] * (1.0 + sc_ref[...]) + sh_ref[...]).astype(BF16)
        u_ref[...] = u
        qkv_ref[...] = (_dot(u, w_ref[:, O_QKV:O_F]) + b_ref[:, O_QKV:O_F]).astype(BF16)
        f_ref[...] = _dot(u, w_ref[:, O_F:O_P]) + b_ref[:, O_F:O_P]
        p_ref[...] = _dot(u, w_ref[:, O_P:O_G]) + b_ref[:, O_P:O_G]
        g_ref[...] = _dot(u, w_ref[:, O_G:D_PAD]) + b_ref[:, O_G:D_PAD]

    row = lambda w: pl.BlockSpec((tm, w), lambda i: (i, 0))
    full = lambda a: pl.BlockSpec(a.shape, lambda i: (0, 0))
    return pl.pallas_call(
        body, name="in_proj", grid=(S // tm,),
        out_shape=(jax.ShapeDtypeStruct((S, D), BF16), jax.ShapeDtypeStruct((S, 3 * D_ATT), BF16),
                   jax.ShapeDtypeStruct((S, 128), F32), jax.ShapeDtypeStruct((S, D_POOL), F32),
                   jax.ShapeDtypeStruct((S, D), F32)),
        in_specs=[row(D), full(shift), full(scale), full(w_pad), full(b_pad)],
        out_specs=(row(D), row(3 * D_ATT), row(128), row(D_POOL), row(D)),
        compiler_params=_params(dimension_semantics=("arbitrary",)),
    )(x, shift, scale, w_pad, b_pad)


def _forget_cumsum(f):
    S = f.shape[0]
    tm = min(T_ATT, S)

    def body(f_ref, out_ref, out_t_ref, carry):
        @pl.when(pl.program_id(0) == 0)
        def _():
            carry[...] = jnp.zeros_like(carry)
        v = f_ref[...]
        logf = jnp.minimum(v, 0.0) - jnp.log(1.0 + jnp.exp(-jnp.abs(v)))
        r = lax.broadcasted_iota(jnp.int32, (tm, tm), 0)
        c = lax.broadcasted_iota(jnp.int32, (tm, tm), 1)
        tri = (r >= c).astype(F32)
        cum = jnp.dot(tri, logf, preferred_element_type=F32, precision=lax.Precision.HIGHEST) + carry[...]
        out_ref[...] = cum
        out_t_ref[...] = cum.T
        carry[...] = cum[tm - 8:tm, :][7:8, :]

    return pl.pallas_call(
        body, name="forget_cumsum", grid=(S // tm,),
        out_shape=(jax.ShapeDtypeStruct((S, 128), F32), jax.ShapeDtypeStruct((S // tm, 128, tm), F32)),
        in_specs=[pl.BlockSpec((tm, 128), lambda i: (i, 0))],
        out_specs=(pl.BlockSpec((tm, 128), lambda i: (i, 0)), pl.BlockSpec((None, 128, tm), lambda i: (i, 0, 0))),
        scratch_shapes=[pltpu.VMEM((1, 128), F32)],
        compiler_params=_params(dimension_semantics=("arbitrary",)),
    )(f)


def _pair_select(is_a, va, vb):
    return jnp.where(is_a, va, vb)


def _attention_fwd(qkv, f_t):
    S = qkv.shape[0]
    T = min(T_ATT, S)
    n_t = S // T

    def body(q_ref, k_ref, v_ref, fk_ref, o_ref, lse_ref, m_sc, l_sc, acc_sc):
        hp = pl.program_id(0)
        i = pl.program_id(1)
        is_a = lax.broadcasted_iota(jnp.int32, (1, 128), 1) < HEAD_DIM
        head_row = lax.broadcasted_iota(jnp.int32, (8, 1), 0)
        q = q_ref[...]
        m_sc[...] = jnp.full((T, 128), NEG, F32)
        l_sc[...] = jnp.zeros((T, 128), F32)
        acc_sc[...] = jnp.zeros((T, 128), F32)
        zero = jnp.zeros((), BF16)

        def step(j, masked):
            rows = pl.ds(pl.multiple_of(j * T, T), T)
            k = k_ref[rows, :]
            v = v_ref[rows, :]
            fk = fk_ref[j]
            m_old = m_sc[...]
            l_old = l_sc[...]
            new_m, new_l, alphas, pvs = [], [], [], []
            for h in range(2):
                sel = is_a if h == 0 else jnp.logical_not(is_a)
                kh = jnp.where(sel, k, zero)
                vh = jnp.where(sel, v, zero)
                fkh = _colsum(jnp.where(head_row == 2 * hp + h, fk, 0.0))
                s = _dot_nt(q, kh) - fkh
                if masked:
                    rr = lax.broadcasted_iota(jnp.int32, (T, T), 0)
                    cc = lax.broadcasted_iota(jnp.int32, (T, T), 1)
                    s = jnp.where(cc <= rr, s, NEG)
                m_prev = m_old[:, h * HEAD_DIM:h * HEAD_DIM + 1]
                m_new = jnp.maximum(m_prev, jnp.max(s, axis=1, keepdims=True))
                alpha = jnp.exp(m_prev - m_new)
                p = jnp.exp(s - m_new)
                new_l.append(alpha * l_old[:, h * HEAD_DIM:h * HEAD_DIM + 1] + jnp.sum(p, axis=1, keepdims=True))
                new_m.append(m_new)
                alphas.append(alpha)
                pvs.append(_dot(p.astype(BF16), vh))
            acc_sc[...] = acc_sc[...] * _pair_select(is_a, alphas[0], alphas[1]) + (pvs[0] + pvs[1])
            m_sc[...] = _pair_select(is_a, new_m[0], new_m[1])
            l_sc[...] = _pair_select(is_a, new_l[0], new_l[1])

        def off_diagonal(j, carry):
            step(j, False)
            return carry

        lax.fori_loop(0, i, off_diagonal, 0)
        step(i, True)
        l = l_sc[...]
        o_ref[...] = acc_sc[...] / l
        lse_ref[...] = m_sc[...] + jnp.log(l)

    return pl.pallas_call(
        body, name="attention_fwd", grid=(N_PAIR, n_t),
        out_shape=(jax.ShapeDtypeStruct((S, D_ATT), F32), jax.ShapeDtypeStruct((N_PAIR, S, 128), F32)),
        in_specs=[pl.BlockSpec((T, 128), lambda hp, i: (i, hp)),
                  pl.BlockSpec((S, 128), lambda hp, i: (0, N_PAIR + hp)),
                  pl.BlockSpec((S, 128), lambda hp, i: (0, 2 * N_PAIR + hp)),
                  pl.BlockSpec((n_t, 8, T), lambda hp, i: (0, 0, 0))],
        out_specs=(pl.BlockSpec((T, 128), lambda hp, i: (i, hp)),
                   pl.BlockSpec((None, T, 128), lambda hp, i: (hp, i, 0))),
        scratch_shapes=[pltpu.VMEM((T, 128), F32), pltpu.VMEM((T, 128), F32), pltpu.VMEM((T, 128), F32)],
        compiler_params=_params(dimension_semantics=("arbitrary", "arbitrary")),
    )(qkv, qkv, qkv, f_t)


def _attention_bwd(qkv, datt, att, lse, big_f):
    S = qkv.shape[0]
    T = min(T_ATT, S)
    n_t = S // T

    rc = min(ATT_CHUNK, T)

    def body(q_ref, do_ref, o_ref, lse_ref, k_ref, v_ref, fk_ref,
             dq_ref, dk_ref, dv_ref, cs_ref, dfk_ref, dfq_ref, stat_sc, dqt_sc):
        hp = pl.program_id(0)
        j = pl.program_id(1)
        lane = lax.broadcasted_iota(jnp.int32, (1, 128), 1)
        is_a = lane < HEAD_DIM
        zero = jnp.zeros((), BF16)

        @pl.when(j == 0)
        def _():
            dqt_sc[...] = jnp.zeros_like(dqt_sc)
            cs_ref[...] = jnp.zeros_like(cs_ref)
            dfq_ref[...] = jnp.zeros_like(dfq_ref)

            def prep(i, carry):
                rows = pl.ds(pl.multiple_of(i * T, T), T)
                prod = o_ref[rows, :] * do_ref[rows, :].astype(F32)
                d_a = jnp.sum(jnp.where(is_a, prod, 0.0), axis=1, keepdims=True)
                d_b = jnp.sum(jnp.where(is_a, 0.0, prod), axis=1, keepdims=True)
                delta_t = _pair_select(is_a, d_a, d_b).T
                bias_t = (-lse_ref[rows, :]).T
                stat_sc[i, 0:1, :] = bias_t[0:1, :]
                stat_sc[i, 1:2, :] = bias_t[HEAD_DIM:HEAD_DIM + 1, :]
                stat_sc[i, 2:3, :] = delta_t[0:1, :]
                stat_sc[i, 3:4, :] = delta_t[HEAD_DIM:HEAD_DIM + 1, :]
                return carry

            lax.fori_loop(0, n_t, prep, 0)

        k = k_ref[...]
        v = v_ref[...]
        fk = fk_ref[...]
        kt = k.astype(F32).T
        sub_a = lax.broadcasted_iota(jnp.int32, (128, 1), 0) < HEAD_DIM
        heads = []
        for h in range(2):
            sel = is_a if h == 0 else jnp.logical_not(is_a)
            sel_t = sub_a if h == 0 else jnp.logical_not(sub_a)
            fkh = jnp.sum(jnp.where(lane == 2 * hp + h, fk, 0.0), axis=1, keepdims=True)
            heads.append((sel, jnp.where(sel, k, zero), jnp.where(sel, v, zero),
                          jnp.where(sel_t, kt, 0.0).astype(BF16), fkh))

        def step(i, masked, acc):
            dk_acc, dv_acc, dfa, dfb = acc
            rows = pl.ds(pl.multiple_of(i * T, T), T)
            q = q_ref[rows, :]
            do = do_ref[rows, :]
            stat = stat_sc[i]
            dqt = dqt_sc[i]
            dfs = [dfa, dfb]
            for h in range(2):
                sel, kh, vh, kth, fkh = heads[h]
                s_full = _dot_nt(kh, q)
                dp_full = _dot_nt(vh, do)
                bias = stat[h:h + 1, :]
                delta = stat[2 + h:3 + h, :]
                p_c, ds_c, dfk_c = [], [], []
                dfq8 = jnp.zeros((8, T), F32)
                for r in range(T // rc):
                    sl = slice(r * rc, (r + 1) * rc)
                    arg = s_full[sl, :] + (bias - fkh[sl, :])
                    if masked:
                        rr = lax.broadcasted_iota(jnp.int32, (rc, T), 0) + r * rc
                        cc = lax.broadcasted_iota(jnp.int32, (rc, T), 1)
                        arg = jnp.where(rr <= cc, arg, NEG)
                    p_t = jnp.exp(arg)
                    ds_t = p_t * (dp_full[sl, :] - delta)
                    p_c.append(p_t.astype(BF16))
                    ds_c.append(ds_t.astype(BF16))
                    dfk_c.append(jnp.sum(ds_t, axis=1, keepdims=True))
                    for g in range(rc // 8):
                        dfq8 = dfq8 + ds_t[8 * g:8 * g + 8, :]
                ds_bf = jnp.concatenate(ds_c, axis=0)
                dv_acc = dv_acc + _dot(jnp.concatenate(p_c, axis=0), jnp.where(sel, do, zero))
                dk_acc = dk_acc + _dot(ds_bf, jnp.where(sel, q, zero))
                dqt = dqt + _dot(kth, ds_bf)
                dfs[h] = dfs[h] + jnp.concatenate(dfk_c, axis=0)
                dfq_ref[i, h:h + 1, :] += _colsum(dfq8)
            dqt_sc[i] = dqt
            return dk_acc, dv_acc, dfs[0], dfs[1]

        acc0 = (jnp.zeros((T, 128), F32), jnp.zeros((T, 128), F32), jnp.zeros((T, 1), F32), jnp.zeros((T, 1), F32))
        acc1 = step(j, True, acc0)
        dk_acc, dv_acc, dfa, dfb = lax.fori_loop(j + 1, n_t, lambda i, a: step(i, False, a), acc1)
        dk_ref[...] = dk_acc.astype(BF16)
        dv_ref[...] = dv_acc.astype(BF16)
        dfk_ref[...] = -jnp.where(lane == 0, dfa, jnp.where(lane == 1, dfb, 0.0))
        cs_ref[:, 128:256] = cs_ref[:, 128:256] + _colsum(dk_acc)
        cs_ref[:, 256:384] = cs_ref[:, 256:384] + _colsum(dv_acc)

        @pl.when(j == n_t - 1)
        def _():
            def finish(i, tot):
                dq = dqt_sc[i].T
                dq_ref[pl.ds(pl.multiple_of(i * T, T), T), :] = dq.astype(BF16)
                return tot + _colsum(dq)

            cs_ref[:, 0:128] = lax.fori_loop(0, n_t, finish, jnp.zeros((1, 128), F32))

    pair_rows = lambda hp, j: (hp, 0, 0)
    return pl.pallas_call(
        body, name="attention_bwd", grid=(N_PAIR, n_t),
        out_shape=(jax.ShapeDtypeStruct((S, D_ATT), BF16), jax.ShapeDtypeStruct((S, D_ATT), BF16),
                   jax.ShapeDtypeStruct((S, D_ATT), BF16), jax.ShapeDtypeStruct((N_PAIR, 1, 384), F32),
                   jax.ShapeDtypeStruct((N_PAIR, S, 128), F32),
                   jax.ShapeDtypeStruct((N_PAIR, n_t, 8, T), F32)),
        in_specs=[pl.BlockSpec((S, 128), lambda hp, j: (0, hp)),
                  pl.BlockSpec((S, 128), lambda hp, j: (0, hp)),
                  pl.BlockSpec((S, 128), lambda hp, j: (0, hp)),
                  pl.BlockSpec((None, S, 128), pair_rows),
                  pl.BlockSpec((T, 128), lambda hp, j: (j, N_PAIR + hp)),
                  pl.BlockSpec((T, 128), lambda hp, j: (j, 2 * N_PAIR + hp)),
                  pl.BlockSpec((T, 128), lambda hp, j: (j, 0))],
        out_specs=(pl.BlockSpec((S, 128), lambda hp, j: (0, hp)),
                   pl.BlockSpec((T, 128), lambda hp, j: (j, hp)),
                   pl.BlockSpec((T, 128), lambda hp, j: (j, hp)),
                   pl.BlockSpec((None, 1, 384), pair_rows),
                   pl.BlockSpec((None, T, 128), lambda hp, j: (hp, j, 0)),
                   pl.BlockSpec((None, n_t, 8, T), lambda hp, j: (hp, 0, 0, 0))),
        scratch_shapes=[pltpu.VMEM((n_t, 8, T), F32), pltpu.VMEM((n_t, 128, T), F32)],
        compiler_params=_params(dimension_semantics=("arbitrary", "arbitrary")),
    )(qkv, datt, att, lse, qkv, qkv, big_f)


def _window_counts(first_row, n_rows, window):
    t = lax.broadcasted_iota(jnp.int32, (n_rows, 1), 0) + first_row
    return jnp.minimum((t + 1).astype(F32), float(window))


def _middle(x, tgt, att, g, p, gate, w_mix, b_mix, pool_scale, w_out, b_out, ln_g, ln_b):
    S = x.shape[0]
    tm = min(TM_MID, S)
    halo_blocks = tm // POOL_HALO

    def body(x_ref, t_ref, att_ref, g_ref, p_ref, ph_ref, gate_ref, wm_ref, bm_ref, ps_ref, wo_ref, bo_ref,
             lg_ref, lb_ref,
             dh_ref, datt_ref, dg_ref, dpl_ref, gwo_ref, gwm_ref, vec_ref, loss_ref):
        i = pl.program_id(0)

        @pl.when(i == 0)
        def _():
            gwo_ref[...] = jnp.zeros_like(gwo_ref)
            gwm_ref[...] = jnp.zeros_like(gwm_ref)
            vec_ref[...] = jnp.zeros_like(vec_ref)
            loss_ref[...] = jnp.zeros_like(loss_ref)

        pc = p_ref[...]
        halo = jnp.where(i > 0, ph_ref[...], 0.0)
        pe = jnp.concatenate([halo, pc], axis=0)
        pooled_parts = []
        for gi, w in enumerate(POOL_WINDOWS):
            cur = pe[:, gi * POOL_GROUP:(gi + 1) * POOL_GROUP]
            span = 1
            while span < w:
                cur = cur + pltpu.roll(cur, span, 0)
                span *= 2
            wsum = cur[POOL_HALO:, :]
            mean = wsum / _window_counts(i * tm, tm, w)
            pooled_parts.append(mean - pc[:, gi * POOL_GROUP:(gi + 1) * POOL_GROUP])
        pooled_bf =[v.astype(BF16) for v in pooled_parts]
        mixed = jnp.concatenate([_dot(pooled_bf[gi], wm_ref[gi]) for gi in range(4)], axis=1) + bm_ref[...]
        ps = ps_ref[...]
        pool_out = mixed * ps
        gv = g_ref[...]
        sig = _sigmoid(gv)
        silu = gv * sig
        att = att_ref[...]
        y = jnp.concatenate([att * silu[:, :D_ATT], pool_out * silu[:, D_ATT:]], axis=1)
        y_bf = y.astype(BF16)
        wo = wo_ref[...]
        yo = _dot(y_bf, wo) + bo_ref[...]
        gate = gate_ref[...]
        h = ALPHA * x_ref[...] + gate * yo
        mu = jnp.mean(h, axis=1, keepdims=True)
        hc = h - mu
        var = jnp.mean(hc * hc, axis=1, keepdims=True)
        rstd = lax.rsqrt(var + LN_EPS)
        yhat = hc * rstd
        lg = lg_ref[...]
        out = yhat * lg + lb_ref[...]
        err = out - t_ref[...]
        loss_ref[...] += 0.5 * jnp.sum(jnp.mean(err * err, axis=1, keepdims=True), axis=0, keepdims=True)

        dout = err * (1.0 / D)
        g_ln_b = _colsum(dout)
        g_ln_g = _colsum(dout * yhat)
        dyh = dout * lg
        dh = rstd * (dyh - jnp.mean(dyh, axis=1, keepdims=True)
                     - yhat * jnp.mean(dyh * yhat, axis=1, keepdims=True))
        dh_ref[...] = dh
        d_gate = _colsum(dh * yo)
        dyo = gate * dh
        g_b_out = _colsum(dyo)
        dyo_bf = dyo.astype(BF16)
        gwo_ref[...] += _dot_tn(y_bf, dyo_bf)
        dy = _dot_nt(dyo_bf, wo)
        dsilu = sig * (1.0 + gv * (1.0 - sig))
        dy_a = dy[:, :D_ATT]
        dy_p = dy[:, D_ATT:]
        datt_ref[...] = (dy_a * silu[:, :D_ATT]).astype(BF16)
        dpo = dy_p * silu[:, D_ATT:]
        dg = jnp.concatenate([dy_a * att * dsilu[:, :D_ATT], dy_p * pool_out * dsilu[:, D_ATT:]], axis=1)
        dg_ref[...] = dg.astype(BF16)
        g_dg = _colsum(dg)
        g_ps = _colsum(dpo * mixed)
        dmixed = dpo * ps
        g_bm = _colsum(dmixed)
        dmixed_bf = dmixed.astype(BF16)
        dpl = []
        for gi in range(4):
            dm = dmixed_bf[:, gi * POOL_GROUP:(gi + 1) * POOL_GROUP]
            gwm_ref[gi] += _dot_tn(pooled_bf[gi], dm)
            dpl.append(_dot_nt(dm, wm_ref[gi]))
        dpl_ref[...] = jnp.concatenate(dpl, axis=1)
        vec_ref[0:1, :] += g_ln_g
        vec_ref[1:2, :] += g_ln_b
        vec_ref[2:3, :] += d_gate
        vec_ref[3:4, :] += g_b_out
        vec_ref[4:5, :] += g_dg
        vec_ref[5:6, 0:D_POOL] += g_ps
        vec_ref[6:7, 0:D_POOL] += g_bm

    row = lambda w: pl.BlockSpec((tm, w), lambda i: (i, 0))
    full2 = lambda a: pl.BlockSpec(a.shape, lambda i: (0, 0))
    full3 = lambda a: pl.BlockSpec(a.shape, lambda i: (0, 0, 0))
    return pl.pallas_call(
        body, name="middle", grid=(S // tm,),
        out_shape=(jax.ShapeDtypeStruct((S, D), F32),
                   jax.ShapeDtypeStruct((S, D_ATT), BF16),
                   jax.ShapeDtypeStruct((S, D), BF16),
                   jax.ShapeDtypeStruct((S, D_POOL), F32),
                   jax.ShapeDtypeStruct((D, D), F32),
                   jax.ShapeDtypeStruct((4, POOL_GROUP, POOL_GROUP), F32),
                   jax.ShapeDtypeStruct((8, D), F32),
                   jax.ShapeDtypeStruct((1, 1), F32)),
        in_specs=[row(D), row(D), row(D_ATT), row(D), row(D_POOL),
                  pl.BlockSpec((POOL_HALO, D_POOL), lambda i: (jnp.maximum(i * halo_blocks - 1, 0), 0)),
                  full2(gate), full3(w_mix), full2(b_mix), full2(pool_scale), full2(w_out), full2(b_out),
                  full2(ln_g), full2(ln_b)],
        out_specs=(row(D), row(D_ATT), row(D), row(D_POOL),
                   pl.BlockSpec((D, D), lambda i: (0, 0)),
                   pl.BlockSpec((4, POOL_GROUP, POOL_GROUP), lambda i: (0, 0, 0)),
                   pl.BlockSpec((8, D), lambda i: (0, 0)),
                   pl.BlockSpec((1, 1), lambda i: (0, 0))),
        compiler_params=_params(dimension_semantics=("arbitrary",)),
    )(x, tgt, att, g, p, p, gate, w_mix, b_mix, pool_scale, w_out, b_out, ln_g, ln_b)


def _tail(dpl, dfk8, f):
    S = dpl.shape[0]
    tm = min(TM_TAIL, S)
    n_t = S // tm
    halo_blocks = tm // POOL_HALO
    last_halo = S // POOL_HALO - 1

    def body(d_ref, dn_ref, dfk_ref, f_ref, dp_ref, df_ref, cs_ref, carry):
        s = pl.program_id(0)
        i = n_t - 1 - s

        @pl.when(s == 0)
        def _():
            carry[...] = jnp.zeros_like(carry)
            cs_ref[...] = jnp.zeros_like(cs_ref)

        dc = d_ref[...]
        nxt = jnp.where(s > 0, dn_ref[...], 0.0)
        de = jnp.concatenate([dc, nxt], axis=0)
        n_e = tm + POOL_HALO
        parts = []
        for gi, w in enumerate(POOL_WINDOWS):
            cur = de[:, gi * POOL_GROUP:(gi + 1) * POOL_GROUP] / _window_counts(i * tm, n_e, w)
            span = 1
            while span < w:
                cur = cur + pltpu.roll(cur, n_e - span, 0)
                span *= 2
            parts.append(cur[:tm, :] - dc[:, gi * POOL_GROUP:(gi + 1) * POOL_GROUP])
        dp = jnp.concatenate(parts, axis=1)
        dp_ref[...] = dp.astype(BF16)
        cs_ref[0:1, :] += _colsum(dp)

        r = lax.broadcasted_iota(jnp.int32, (tm, tm), 0)
        c = lax.broadcasted_iota(jnp.int32, (tm, tm), 1)
        tri = (r <= c).astype(F32)
        dlogf = jnp.dot(tri, dfk_ref[...], preferred_element_type=F32, precision=lax.Precision.HIGHEST) + carry[...]
        carry[...] = dlogf[0:1, :]
        df = dlogf * _sigmoid(-f_ref[...])
        df_ref[...] = df.astype(BF16)
        cs_ref[1:2, 0:128] += _colsum(df)

    rev = lambda w: pl.BlockSpec((tm, w), lambda s: (n_t - 1 - s, 0))
    return pl.pallas_call(
        body, name="tail", grid=(n_t,),
        out_shape=(jax.ShapeDtypeStruct((S, D_POOL), BF16), jax.ShapeDtypeStruct((S, 128), BF16),
                   jax.ShapeDtypeStruct((8, D_POOL), F32)),
        in_specs=[rev(D_POOL),
                  pl.BlockSpec((POOL_HALO, D_POOL),
                               lambda s: (jnp.minimum((n_t - s) * halo_blocks, last_halo), 0)),
                  rev(128), rev(128)],
        out_specs=(rev(D_POOL), rev(128), pl.BlockSpec((8, D_POOL), lambda s: (0, 0))),
        scratch_shapes=[pltpu.VMEM((1, 128), F32)],
        compiler_params=_params(dimension_semantics=("arbitrary",)),
    )(dpl, dpl, dfk8, f)


PIECES = ((O_QKV, D_ATT), (O_QKV + D_ATT, D_ATT), (O_QKV + 2 * D_ATT, D_ATT), (O_F, 128), (O_P, D_POOL), (O_G, D))


def _grad_w_in(u, pieces):
    S = u.shape[0]
    tm = min(TM_GW, S)
    n_t = S // tm

    def body(u_ref, *rest):
        piece_refs, out_ref, acc, sem = rest[:6], rest[6], rest[7], rest[8]
        i = pl.program_id(0)

        @pl.when(i == 0)
        def _():
            acc[...] = jnp.zeros_like(acc)

        u_t = u_ref[...]
        for (off, w), ref in zip(PIECES, piece_refs):
            acc[:, off:off + w] += _dot_tn(u_t, ref[...])

        @pl.when(i == n_t - 1)
        def _():
            cp = pltpu.make_async_copy(acc, out_ref, sem)
            cp.start()
            cp.wait()

    return pl.pallas_call(
        body, name="grad_w_in", grid=(n_t,),
        out_shape=jax.ShapeDtypeStruct((D, D_PAD), F32),
        in_specs=[pl.BlockSpec((tm, D), lambda i: (i, 0))]
        + [pl.BlockSpec((tm, w), lambda i: (i, 0)) for _, w in PIECES],
        out_specs=pl.BlockSpec(memory_space=pl.ANY),
        scratch_shapes=[pltpu.VMEM((D, D_PAD), F32), pltpu.SemaphoreType.DMA],
        compiler_params=_params(dimension_semantics=("arbitrary",)),
    )(u, *pieces)


def _grad_x(pieces, w_pad, dh, x, scale):
    S = x.shape[0]
    tm = min(TM_DU, S)

    def body(*refs):
        piece_refs = refs[:6]
        w_ref, dh_ref, x_ref, sc_ref, gx_ref, vec_ref = refs[6:]

        @pl.when(pl.program_id(0) == 0)
        def _():
            vec_ref[...] = jnp.zeros_like(vec_ref)

        du = jnp.zeros((tm, D), F32)
        for (off, w), ref in zip(PIECES, piece_refs):
            du = du + _dot_nt(ref[...], w_ref[:, off:off + w])
        xv = x_ref[...]
        gx_ref[...] = ALPHA * dh_ref[...] + du * (1.0 + sc_ref[...])
        vec_ref[0:1, :] += _colsum(du)
        vec_ref[1:2, :] += _colsum(du * xv)

    row = lambda w: pl.BlockSpec((tm, w), lambda i: (i, 0))
    return pl.pallas_call(
        body, name="grad_x", grid=(S // tm,),
        out_shape=(jax.ShapeDtypeStruct((S, D), F32), jax.ShapeDtypeStruct((8, D), F32)),
        in_specs=[row(w) for _, w in PIECES]
        + [pl.BlockSpec(w_pad.shape, lambda i: (0, 0)), row(D), row(D), pl.BlockSpec((1, D), lambda i: (0, 0))],
        out_specs=(row(D), pl.BlockSpec((8, D), lambda i: (0, 0))),
        compiler_params=_params(dimension_semantics=("arbitrary",)),
    )(*pieces, w_pad, dh, x, scale)


def _grad_ada(c_all, dada_all, dada_cols):
    def body(c_ref, dall_ref, dcol_ref, gw_ref, gb_ref):
        rows = lax.broadcasted_iota(jnp.int32, (8, 1), 0)
        cm = jnp.zeros((8, D), F32)
        dm = jnp.zeros((8, 3 * D), F32)
        for r in range(8):
            cm = jnp.where(rows == r, c_ref[r], cm)
            dm = jnp.where(rows == r, dall_ref[r], dm)
        act = cm * _sigmoid(cm)
        pad = jnp.zeros((8, D), F32)
        lhs = jnp.concatenate([act, pad], axis=0).astype(BF16)
        rhs = jnp.concatenate([dcol_ref[...], jnp.zeros((8, SHARD_ADA), F32)], axis=0).astype(BF16)
        gw_ref[...] = _dot_tn(lhs, rhs)
        gb_ref[...] = _colsum(dm)

    vm = pl.BlockSpec(memory_space=pltpu.VMEM)
    return pl.pallas_call(
        body, name="grad_ada",
        out_shape=(jax.ShapeDtypeStruct((D, SHARD_ADA), F32), jax.ShapeDtypeStruct((1, 3 * D), F32)),
        in_specs=[vm, vm, vm], out_specs=(vm, vm),
        compiler_params=_params(),
    )(c_all, dada_all, dada_cols)


def _adamw_math(w, g, m, v):
    m = ADAM_B1 * m + (1.0 - ADAM_B1) * g
    v = ADAM_B2 * v + (1.0 - ADAM_B2) * (g * g)
    m_hat = m / (1.0 - ADAM_B1 ** ADAM_STEP)
    v_hat = v / (1.0 - ADAM_B2 ** ADAM_STEP)
    delta = -ADAM_LR * (m_hat / (jnp.sqrt(v_hat) + ADAM_EPS) + ADAM_WD * w)
    return delta, m, v


def _adamw(groups, n_steps):
    n = len(groups)

    def body(*refs):
        ins, outs = refs[:4 * n], refs[4 * n:]
        for t in range(n):
            w, g, m, v = (r[...] for r in ins[4 * t:4 * t + 4])
            d, m2, v2 = _adamw_math(w, g, m, v)
            outs[3 * t][...] = d
            outs[3 * t + 1][...] = m2
            outs[3 * t + 2][...] = v2

    in_specs, out_specs, out_shape, args = [], [], [], []
    for (w, g, m, v) in groups:
        r, c = w.shape
        spec = pl.BlockSpec((r // n_steps, c), lambda i: (i, 0))
        in_specs += [spec] * 4
        out_specs += [spec] * 3
        out_shape += [jax.ShapeDtypeStruct((r, c), F32)] * 3
        args += [w, g, m, v]
    return pl.pallas_call(
        body, name="adamw_%d" % n, grid=(n_steps,),
        out_shape=tuple(out_shape), in_specs=in_specs, out_specs=tuple(out_specs),
        compiler_params=_params(dimension_semantics=("arbitrary",)),
    )(*args)


def _pack_small(parts):
    rows = []
    used = 0
    for name, (first, n_rows) in SMALL_SEGS.items():
        if first > used:
            rows.append(jnp.zeros((first - used, 128), F32))
        flat = parts[name].reshape(-1)
        flat = jnp.pad(flat, (0, n_rows * 128 - flat.shape[0]))
        rows.append(flat.reshape(n_rows, 128))
        used = first + n_rows
    rows.append(jnp.zeros((SMALL_ROWS - used, 128), F32))
    return jnp.concatenate(rows, axis=0)


def _unpack_small(buf, name, shape):
    first, n_rows = SMALL_SEGS[name]
    n = int(np.prod(shape))
    return buf[first:first + n_rows].reshape(-1)[:n].reshape(shape)


def _pad_in(v):
    r = v.shape[0]
    z = jnp.zeros((r, O_P - O_F - N_HEADS), v.dtype)
    return jnp.concatenate([v[:, :3 * D_ATT + N_HEADS], z, v[:, 3 * D_ATT + N_HEADS:]], axis=1)


def _unpad_in(v):
    return jnp.concatenate([v[:, :O_F + N_HEADS], v[:, O_P:]], axis=1)


def _shards_in(v):
    gap = O_P - (O_F + N_HEADS)
    parts = []
    for a in range(N_CHIPS):
        lo, hi = a * SHARD_IN, (a + 1) * SHARD_IN
        cut = O_F + N_HEADS
        if hi <= cut:
            parts.append(v[:, lo:hi])
        elif lo >= cut:
            parts.append(v[:, lo + gap:hi + gap])
        else:
            parts.append(jnp.concatenate([v[:, lo:cut], v[:, cut + gap:hi + gap]], axis=1))
    return jnp.stack(parts, axis=0)


def kernel(x, c, w_ada, b_ada, w_in, b_in, w_pool_mix, b_pool_mix, pool_scale, w_out, b_out, ln_g, ln_b, loss_target, m_w_ada, m_b_ada, m_w_in, m_b_in, m_w_pool_mix, m_b_pool_mix, m_pool_scale, m_w_out, m_b_out, m_ln_g, m_ln_b, v_w_ada, v_b_ada, v_w_in, v_b_in, v_w_pool_mix, v_b_pool_mix, v_pool_scale, v_w_out, v_b_out, v_ln_g, v_ln_b):
    S = x.shape[1]
    T = min(T_ATT, S)
    n_t = S // T
    chip = 2 * lax.axis_index("x") + lax.axis_index("y")
    x2 = x[0]
    tgt = loss_target[0]
    q_scale = jnp.concatenate([jnp.full((1, D_ATT), Q_SCALE, F32), jnp.ones((1, D_PAD - D_ATT), F32)], axis=1)

    c_all, ada4 = _ada_exchange(c, w_ada[0], b_ada.reshape(4, 1, SHARD_ADA))
    ada = ada4[:, 0, :].reshape(1, 3 * D)
    shift, scale, gate = ada[:, :D], ada[:, D:2 * D], ada[:, 2 * D:]
    w_in_all, w_out_all = _gather_weights(w_in[0].astype(BF16), w_out[0].astype(BF16))
    w_in_full = jnp.transpose(w_in_all, (1, 0, 2)).reshape(D, D_IN)
    w_pad = _pad_in(w_in_full) * q_scale.astype(BF16)
    b_pad = _pad_in(b_in) * q_scale
    w_out_full = w_out_all.reshape(D, D)
    w_mix_bf = w_pool_mix[0].astype(BF16)

    u, qkv, f, p, g = _in_proj(x2, shift, scale, w_pad, b_pad)
    big_f, f_t = _forget_cumsum(f)
    att, lse = _attention_fwd(qkv, f_t)

    dh, datt, dg, dpl, gw_out, gw_mix, vec, loss_part = _middle(
        x2, tgt, att, g, p, gate, w_mix_bf, b_pool_mix.reshape(1, D_POOL), pool_scale, w_out_full, b_out, ln_g, ln_b)
    dq, dk, dv, cs_att, dfk, dfq = _attention_bwd(qkv, datt, att, lse, big_f)
    dfk8 = jnp.transpose(dfk[:, :, 0:2], (1, 0, 2)).reshape(S, N_HEADS)
    dfk8 = dfk8 + jnp.transpose(dfq[:, :, 0:2, :], (1, 3, 0, 2)).reshape(S, N_HEADS)
    dfk8 = jnp.pad(dfk8, ((0, 0), (0, 128 - N_HEADS)))
    dp, df, cs_tail = _tail(dpl, dfk8, f)
    pieces = (dq, dk, dv, df, dp, dg)
    gw_pad = _grad_w_in(u, pieces)
    grad_x, vec_x = _grad_x(pieces, w_pad, dh, x2, scale)

    cs_qkv = jnp.transpose(cs_att.reshape(N_PAIR, 3, 128), (1, 0, 2)).reshape(1, 3 * D_ATT)
    gb_pad = jnp.concatenate([cs_qkv, cs_tail[1:2, 0:128], cs_tail[0:1, :], vec[4:5, :]], axis=1) * q_scale
    dada = jnp.concatenate([vec_x[0:1, :], vec_x[1:2, :], vec[2:3, :]], axis=1)
    small = _pack_small({
        "b_in": gb_pad, "w_pool_mix": gw_mix, "b_pool_mix": vec[6:7, :D_POOL], "pool_scale": vec[5:6, :D_POOL],
        "b_out": vec[3:4, :], "ln_g": vec[0:1, :], "ln_b": vec[1:2, :], "b_ada": jnp.zeros((1, 3 * D), F32)})

    g_w_in = _reduce_scatter(_shards_in(gw_pad), _shards_in(q_scale), "reduce_w_in")
    g_w_out = _reduce_scatter(gw_out.reshape(N_CHIPS, SHARD_OUT, D), jnp.ones((N_CHIPS, 1, D), F32), "reduce_w_out")
    small_sum, dada_all = _all_reduce_small(small[:SMALL_REDUCED_ROWS], dada)
    dada_cols = lax.dynamic_slice(dada_all[:, 0, :], (0, chip * SHARD_ADA), (8, SHARD_ADA))
    g_w_ada, g_b_ada = _grad_ada(c_all, dada_all, dada_cols)
    loss = lax.psum(loss_part[0, 0], ("x", "y", "c"))

    grads_small = jnp.concatenate([small_sum, g_b_ada.reshape(24, 128)], axis=0)
    small_w = {"b_in": _pad_in(b_in), "w_pool_mix": w_pool_mix, "b_pool_mix": b_pool_mix, "pool_scale": pool_scale,
               "b_out": b_out, "ln_g": ln_g, "ln_b": ln_b, "b_ada": b_ada}
    small_m = {"b_in": _pad_in(m_b_in), "w_pool_mix": m_w_pool_mix, "b_pool_mix": m_b_pool_mix,
               "pool_scale": m_pool_scale, "b_out": m_b_out, "ln_g": m_ln_g, "ln_b": m_ln_b, "b_ada": m_b_ada}
    small_v = {"b_in": _pad_in(v_b_in), "w_pool_mix": v_w_pool_mix, "b_pool_mix": v_b_pool_mix,
               "pool_scale": v_pool_scale, "b_out": v_b_out, "ln_g": v_ln_g, "ln_b": v_ln_b, "b_ada": v_b_ada}
    big = _adamw([(w_ada[0], g_w_ada, m_w_ada[0], v_w_ada[0]),
                  (w_in[0], g_w_in, m_w_in[0], v_w_in[0]),
                  (w_out[0], g_w_out, m_w_out[0], v_w_out[0])], 8)
    sm = _adamw([(_pack_small(small_w), grads_small, _pack_small(small_m), _pack_small(small_v))], 1)

    names = ["w_ada", "b_ada", "w_in", "b_in", "w_pool_mix", "b_pool_mix", "pool_scale", "w_out", "b_out",
             "ln_g", "ln_b"]
    shapes = {"b_ada": (1, 3 * D), "b_in": (1, D_PAD), "w_pool_mix": (1, 4, POOL_GROUP, POOL_GROUP),
              "b_pool_mix": (1, 4, POOL_GROUP), "pool_scale": (1, D_POOL), "b_out": (1, D), "ln_g": (1, D),
              "ln_b": (1, D)}
    big_idx = {"w_ada": 0, "w_in": 1, "w_out": 2}

    def leaf(kind, name):
        if name in big_idx:
            if kind == 0:
                return (g_w_ada, g_w_in, g_w_out)[big_idx[name]][None]
            return big[3 * big_idx[name] + kind - 1][None]
        buf = grads_small if kind == 0 else sm[kind - 1]
        val = _unpack_small(buf, name, shapes[name])
        if name == "b_in":
            val = _unpad_in(val)
        return val

    outs = [loss, grad_x[None]]
    for kind in range(4):
        outs += [leaf(kind, n) for n in names]
    return tuple(outs)
```

```python
import functools

import numpy as np
import jax
import jax.numpy as jnp
from jax import lax
from jax.experimental import pallas as pl
from jax.experimental.pallas import tpu as pltpu

F32 = jnp.float32
BF16 = jnp.bfloat16
MESH = pl.DeviceIdType.MESH

D = 1024
D_ATT = 512
D_POOL = 512
N_HEADS = 8
HEAD_DIM = 64
N_PAIR = N_HEADS // 2
POOL_WINDOWS = (2, 4, 8, 16)
POOL_GROUP = 128
POOL_HALO = 16
LN_EPS = 1e-5
ALPHA = 2.0 ** 0.25
D_IN = 3 * D_ATT + N_HEADS + D_POOL + D_ATT + D_POOL
N_CHIPS = 4
SHARD_IN = D_IN // N_CHIPS
SHARD_ADA = 3 * D // N_CHIPS
SHARD_OUT = D // N_CHIPS

O_QKV, O_F, O_P, O_G, D_PAD = 0, 1536, 1664, 2176, 3200
Q_SCALE = HEAD_DIM ** -0.5

ADAM_LR, ADAM_B1, ADAM_B2, ADAM_EPS, ADAM_WD, ADAM_STEP = 0.001, 0.9, 0.999, 1e-08, 0.01, 10

NEG = -1e30

VMEM_LIMIT = 56 * 1024 * 1024

TM_PROJ = 512
T_ATT = 512
ATT_CHUNK = 32
TM_MID = 256
TM_TAIL = 512
TM_GW = 512
TM_DU = 512

REL7 = [(0, 0, 1), (0, 1, 0), (0, 1, 1), (1, 0, 0), (1, 0, 1), (1, 1, 0), (1, 1, 1)]
REL3 = [(0, 1), (1, 0), (1, 1)]

SMALL_SEGS = {}
_row = 0
for _name, _n in (("b_in", 3200), ("w_pool_mix", 65536), ("b_pool_mix", 512), ("pool_scale", 512),
                  ("b_out", 1024), ("ln_g", 1024), ("ln_b", 1024)):
    _rows = -(-_n // 1024) * 8
    SMALL_SEGS[_name] = (_row, _rows)
    _row += _rows
SMALL_REDUCED_ROWS = -(-_row // 16) * 16
SMALL_SEGS["b_ada"] = (SMALL_REDUCED_ROWS, 24)
SMALL_ROWS = SMALL_REDUCED_ROWS + 24


def _params(**kw):
    return pltpu.CompilerParams(vmem_limit_bytes=VMEM_LIMIT, **kw)


def _flip(v, d):
    return v if d == 0 else 1 - v


def _dot(a, b):
    return jnp.dot(a, b, preferred_element_type=F32)


def _dot_nt(a, b):
    return lax.dot_general(a, b, (((1,), (1,)), ((), ())), preferred_element_type=F32)


def _dot_tn(a, b):
    return lax.dot_general(a, b, (((0,), (0,)), ((), ())), preferred_element_type=F32)


def _sigmoid(v):
    return 1.0 / (1.0 + jnp.exp(-v))


def _colsum(v):
    return jnp.sum(v, axis=0, keepdims=True)


def _ada_exchange(c, w_ada, b_ada4):
    def body(c_ref, w_ref, b_ref, call_ref, ada_ref, cslab, sbuf, rbuf, cs_sem, cr_sem, as_sem, ar_sem):
        x, y, cc = lax.axis_index("x"), lax.axis_index("y"), lax.axis_index("c")
        me = 4 * x + 2 * y + cc
        chip = 2 * x + y
        cslab[...] = jnp.broadcast_to(c_ref[...], (8, D))
        call_ref[me] = cslab[...]
        gathers = []
        for k, (dx, dy, dc) in enumerate(REL7):
            cp = pltpu.make_async_remote_copy(
                src_ref=cslab, dst_ref=call_ref.at[me], send_sem=cs_sem.at[k], recv_sem=cr_sem.at[k],
                device_id=(_flip(x, dx), _flip(y, dy), _flip(cc, dc)), device_id_type=MESH)
            cp.start()
            gathers.append(cp)
        for cp in gathers:
            cp.wait()
        rows = lax.broadcasted_iota(jnp.int32, (8, 1), 0)
        mat = jnp.zeros((8, D), F32)
        for r in range(8):
            mat = jnp.where(rows == r, call_ref[r], mat)
        act = (mat * _sigmoid(mat)).astype(BF16)
        part = _dot(act, w_ref[...].astype(BF16))
        sends = []
        for k, (dx, dy) in enumerate(REL3):
            px, py = _flip(x, dx), _flip(y, dy)
            r = 4 * px + 2 * py + cc
            piece = _colsum(jnp.where(rows == r, part, 0.0))
            sbuf[k] = jnp.broadcast_to(piece, (8, SHARD_ADA))
            cp = pltpu.make_async_remote_copy(
                src_ref=sbuf.at[k], dst_ref=rbuf.at[k], send_sem=as_sem.at[k], recv_sem=ar_sem.at[k],
                device_id=(px, py, cc), device_id_type=MESH)
            cp.start()
            sends.append(cp)
        own = _colsum(jnp.where(rows == me, part, 0.0))
        ada_ref[chip] = jnp.broadcast_to(own, (8, SHARD_ADA)) + b_ref[chip]
        for k, (dx, dy) in enumerate(REL3):
            sends[k].wait()
            a = 2 * _flip(x, dx) + _flip(y, dy)
            ada_ref[a] = rbuf[k] + b_ref[a]

    vm = pl.BlockSpec(memory_space=pltpu.VMEM)
    return pl.pallas_call(
        body, name="ada_exchange",
        out_shape=(jax.ShapeDtypeStruct((8, 8, D), F32), jax.ShapeDtypeStruct((4, 8, SHARD_ADA), F32)),
        in_specs=[vm, vm, vm], out_specs=(vm, vm),
        scratch_shapes=[pltpu.VMEM((8, D), F32), pltpu.VMEM((3, 8, SHARD_ADA), F32),
                        pltpu.VMEM((3, 8, SHARD_ADA), F32),
                        pltpu.SemaphoreType.DMA((7,)), pltpu.SemaphoreType.DMA((7,)),
                        pltpu.SemaphoreType.DMA((3,)), pltpu.SemaphoreType.DMA((3,))],
        compiler_params=_params(),
    )(c, w_ada, b_ada4)


def _gather_weights(w_in_sh, w_out_sh):
    def body(win_ref, wout_ref, win_all, wout_all, own_sem, s_sem, r_sem, fs_sem, fr_sem):
        x, y, cc = lax.axis_index("x"), lax.axis_index("y"), lax.axis_index("c")
        chip = 2 * x + y
        sib = (x, y, 1 - cc)
        srcs = (win_ref, wout_ref)
        dsts = (win_all, wout_all)
        halves = (D // 2, SHARD_OUT // 2)

        def rows(t, which):
            h = halves[t]
            return pl.ds(pl.multiple_of(which * h, h), h)

        own = [pltpu.make_async_copy(srcs[t], dsts[t].at[chip], own_sem.at[t]) for t in range(2)]
        for cp in own:
            cp.start()
        first = []
        for k, (dx, dy) in enumerate(REL3):
            peer = (_flip(x, dx), _flip(y, dy), cc)
            for t in range(2):
                cp = pltpu.make_async_remote_copy(
                    src_ref=srcs[t].at[rows(t, cc), :], dst_ref=dsts[t].at[chip, rows(t, cc), :],
                    send_sem=s_sem.at[2 * k + t], recv_sem=r_sem.at[2 * k + t],
                    device_id=peer, device_id_type=MESH)
                cp.start()
                first.append(cp)
        passed = []
        for k, (dx, dy) in enumerate(REL3):
            a = 2 * _flip(x, dx) + _flip(y, dy)
            for t in range(2):
                landed = dsts[t].at[a, rows(t, cc), :]
                pltpu.make_async_remote_copy(
                    src_ref=landed, dst_ref=landed, send_sem=s_sem.at[2 * k + t], recv_sem=r_sem.at[2 * k + t],
                    device_id=sib, device_id_type=MESH).wait_recv()
                cp = pltpu.make_async_remote_copy(
                    src_ref=landed, dst_ref=landed, send_sem=fs_sem.at[2 * k + t], recv_sem=fr_sem.at[2 * k + t],
                    device_id=sib, device_id_type=MESH)
                cp.start()
                passed.append(cp)
        for k, (dx, dy) in enumerate(REL3):
            a = 2 * _flip(x, dx) + _flip(y, dy)
            for t in range(2):
                other = dsts[t].at[a, rows(t, 1 - cc), :]
                pltpu.make_async_remote_copy(
                    src_ref=other, dst_ref=other, send_sem=fs_sem.at[2 * k + t], recv_sem=fr_sem.at[2 * k + t],
                    device_id=sib, device_id_type=MESH).wait_recv()
        for cp in first + passed:
            cp.wait_send()
        for cp in own:
            cp.wait()

    vm = pl.BlockSpec(memory_space=pltpu.VMEM)
    return pl.pallas_call(
        body, name="gather_weights",
        out_shape=(jax.ShapeDtypeStruct((N_CHIPS, D, SHARD_IN), BF16),
                   jax.ShapeDtypeStruct((N_CHIPS, SHARD_OUT, D), BF16)),
        in_specs=[vm, vm], out_specs=(vm, vm),
        scratch_shapes=[pltpu.SemaphoreType.DMA((2,)), pltpu.SemaphoreType.DMA((6,)),
                        pltpu.SemaphoreType.DMA((6,)), pltpu.SemaphoreType.DMA((6,)),
                        pltpu.SemaphoreType.DMA((6,))],
        compiler_params=_params(),
    )(w_in_sh, w_out_sh)


def _reduce_scatter(g4, scale4, name):
    _, R, C = g4.shape
    RH = R // 2

    def body(g_ref, sc_ref, out_ref, sib_buf, send_buf, ici_buf, sem1, sem2s, sem2r, sem3):
        x, y, cc = lax.axis_index("x"), lax.axis_index("y"), lax.axis_index("c")
        chip = 2 * x + y
        sib = (x, y, 1 - cc)
        mine = pl.ds(pl.multiple_of(cc * RH, RH), RH)
        theirs = pl.ds(pl.multiple_of((1 - cc) * RH, RH), RH)
        cp1 = pltpu.make_async_remote_copy(
            src_ref=g_ref.at[:, theirs, :], dst_ref=sib_buf, send_sem=sem1.at[0], recv_sem=sem1.at[1],
            device_id=sib, device_id_type=MESH)
        cp1.start()
        cp1.wait()
        for a in range(N_CHIPS):
            both = g_ref[a, mine, :] + sib_buf[a]
            sib_buf[a] = both
            send_buf[a] = both.astype(BF16)
        sends = []
        for k, (dx, dy) in enumerate(REL3):
            px, py = _flip(x, dx), _flip(y, dy)
            cp = pltpu.make_async_remote_copy(
                src_ref=send_buf.at[2 * px + py], dst_ref=ici_buf.at[chip],
                send_sem=sem2s.at[k], recv_sem=sem2r.at[k], device_id=(px, py, cc), device_id_type=MESH)
            cp.start()
            sends.append(cp)
        ici_buf[chip] = send_buf[chip]
        for cp in sends:
            cp.wait()
        own = sib_buf[chip]
        parts = [jnp.where(chip == a, own, ici_buf[a].astype(F32)) for a in range(N_CHIPS)]
        out_ref[mine, :] = ((parts[0] + parts[1]) + (parts[2] + parts[3])) * sc_ref[chip]
        cp3 = pltpu.make_async_remote_copy(
            src_ref=out_ref.at[mine, :], dst_ref=out_ref.at[mine, :], send_sem=sem3.at[0], recv_sem=sem3.at[1],
            device_id=sib, device_id_type=MESH)
        cp3.start()
        cp3.wait()

    vm = pl.BlockSpec(memory_space=pltpu.VMEM)
    return pl.pallas_call(
        body, name=name,
        out_shape=jax.ShapeDtypeStruct((R, C), F32),
        in_specs=[vm, vm], out_specs=vm,
        scratch_shapes=[pltpu.VMEM((N_CHIPS, RH, C), F32), pltpu.VMEM((N_CHIPS, RH, C), BF16),
                        pltpu.VMEM((N_CHIPS, RH, C), BF16),
                        pltpu.SemaphoreType.DMA((2,)), pltpu.SemaphoreType.DMA((3,)),
                        pltpu.SemaphoreType.DMA((3,)), pltpu.SemaphoreType.DMA((2,))],
        compiler_params=_params(),
    )(g4, scale4)


def _all_reduce_small(g, dada):
    R = g.shape[0]
    RH = R // 2
    W = dada.shape[1]

    def body(g_ref, d_ref, out_ref, dall_ref, dslab, sib_buf, ici_buf, ds_sem, dr_sem, sem1, sem2s, sem2r, sem3):
        x, y, cc = lax.axis_index("x"), lax.axis_index("y"), lax.axis_index("c")
        me = 4 * x + 2 * y + cc
        chip = 2 * x + y
        sib = (x, y, 1 - cc)
        dslab[...] = jnp.broadcast_to(d_ref[...], (8, W))
        dall_ref[me] = dslab[...]
        gathers = []
        for k, (dx, dy, dc) in enumerate(REL7):
            cp = pltpu.make_async_remote_copy(
                src_ref=dslab, dst_ref=dall_ref.at[me], send_sem=ds_sem.at[k], recv_sem=dr_sem.at[k],
                device_id=(_flip(x, dx), _flip(y, dy), _flip(cc, dc)), device_id_type=MESH)
            cp.start()
            gathers.append(cp)
        mine = pl.ds(pl.multiple_of(cc * RH, 8), RH)
        theirs = pl.ds(pl.multiple_of((1 - cc) * RH, 8), RH)
        cp1 = pltpu.make_async_remote_copy(
            src_ref=g_ref.at[theirs, :], dst_ref=sib_buf, send_sem=sem1.at[0], recv_sem=sem1.at[1],
            device_id=sib, device_id_type=MESH)
        cp1.start()
        cp1.wait()
        sib_buf[...] = g_ref[mine, :] + sib_buf[...]
        sends = []
        for k, (dx, dy) in enumerate(REL3):
            px, py = _flip(x, dx), _flip(y, dy)
            cp = pltpu.make_async_remote_copy(
                src_ref=sib_buf, dst_ref=ici_buf.at[chip],
                send_sem=sem2s.at[k], recv_sem=sem2r.at[k], device_id=(px, py, cc), device_id_type=MESH)
            cp.start()
            sends.append(cp)
        ici_buf[chip] = sib_buf[...]
        for cp in sends:
            cp.wait()
        out_ref[mine, :] = (ici_buf[0] + ici_buf[1]) + (ici_buf[2] + ici_buf[3])
        cp3 = pltpu.make_async_remote_copy(
            src_ref=out_ref.at[mine, :], dst_ref=out_ref.at[mine, :], send_sem=sem3.at[0], recv_sem=sem3.at[1],
            device_id=sib, device_id_type=MESH)
        cp3.start()
        cp3.wait()
        for cp in gathers:
            cp.wait()

    vm = pl.BlockSpec(memory_space=pltpu.VMEM)
    return pl.pallas_call(
        body, name="all_reduce_small",
        out_shape=(jax.ShapeDtypeStruct((R, 128), F32), jax.ShapeDtypeStruct((8, 8, W), F32)),
        in_specs=[vm, vm], out_specs=(vm, vm),
        scratch_shapes=[pltpu.VMEM((8, W), F32), pltpu.VMEM((RH, 128), F32), pltpu.VMEM((N_CHIPS, RH, 128), F32),
                        pltpu.SemaphoreType.DMA((7,)), pltpu.SemaphoreType.DMA((7,)),
                        pltpu.SemaphoreType.DMA((2,)), pltpu.SemaphoreType.DMA((3,)),
                        pltpu.SemaphoreType.DMA((3,)), pltpu.SemaphoreType.DMA((2,))],
        compiler_params=_params(),
    )(g, dada)


def _in_proj(x, shift, scale, w_pad, b_pad):
    S = x.shape[0]
    tm = min(TM_PROJ, S)

    def body(x_ref, sh_ref, sc_ref, w_ref, b_ref, u_ref, qkv_ref, f_ref, p_ref, g_ref):
        u = (x_ref[...] * (1.0 + sc_ref[...]) + sh_ref[...]).astype(BF16)
        u_ref[...] = u
        qkv_ref[...] = (_dot(u, w_ref[:, O_QKV:O_F]) + b_ref[:, O_QKV:O_F]).astype(BF16)
        f_ref[...] = _dot(u, w_ref[:, O_F:O_P]) + b_ref[:, O_F:O_P]
        p_ref[...] = _dot(u, w_ref[:, O_P:O_G]) + b_ref[:, O_P:O_G]
        g_ref[...] = _dot(u, w_ref[:, O_G:D_PAD]) + b_ref[:, O_G:D_PAD]

    row = lambda w: pl.BlockSpec((tm, w), lambda i: (i, 0))
    full = lambda a: pl.BlockSpec(a.shape, lambda i: (0, 0))
    return pl.pallas_call(
        body, name="in_proj", grid=(S // tm,),
        out_shape=(jax.ShapeDtypeStruct((S, D), BF16), jax.ShapeDtypeStruct((S, 3 * D_ATT), BF16),
                   jax.ShapeDtypeStruct((S, 128), F32), jax.ShapeDtypeStruct((S, D_POOL), F32),
                   jax.ShapeDtypeStruct((S, D), F32)),
        in_specs=[row(D), full(shift), full(scale), full(w_pad), full(b_pad)],
        out_specs=(row(D), row(3 * D_ATT), row(128), row(D_POOL), row(D)),
        compiler_params=_params(dimension_semantics=("arbitrary",)),
    )(x, shift, scale, w_pad, b_pad)


def _forget_cumsum(f):
    S = f.shape[0]
    tm = min(T_ATT, S)

    def body(f_ref, out_ref, carry):
        @pl.when(pl.program_id(0) == 0)
        def _():
            carry[...] = jnp.zeros_like(carry)
        v = f_ref[...]
        logf = jnp.minimum(v, 0.0) - jnp.log(1.0 + jnp.exp(-jnp.abs(v)))
        r = lax.broadcasted_iota(jnp.int32, (tm, tm), 0)
        c = lax.broadcasted_iota(jnp.int32, (tm, tm), 1)
        tri = (r >= c).astype(F32)
        cum = jnp.dot(tri, logf, preferred_element_type=F32, precision=lax.Precision.HIGHEST) + carry[...]
        out_ref[...] = cum
        carry[...] = cum[tm - 8:tm, :][7:8, :]

    return pl.pallas_call(
        body, name="forget_cumsum", grid=(S // tm,),
        out_shape=jax.ShapeDtypeStruct((S, 128), F32),
        in_specs=[pl.BlockSpec((tm, 128), lambda i: (i, 0))],
        out_specs=pl.BlockSpec((tm, 128), lambda i: (i, 0)),
        scratch_shapes=[pltpu.VMEM((1, 128), F32)],
        compiler_params=_params(dimension_semantics=("arbitrary",)),
    )(f)


def _pair_select(is_a, va, vb):
    return jnp.where(is_a, va, vb)


def _split3(v):
    hi = v.astype(BF16)
    rest = v - hi.astype(F32)
    mid = rest.astype(BF16)
    lo = (rest - mid.astype(F32)).astype(BF16)
    return hi, mid, lo


def _attention_fwd(qkv, big_f):
    S = qkv.shape[0]
    T = min(T_ATT, S)
    n_t = S // T

    def body(q_ref, k_ref, v_ref, f_ref, o_ref, lse_ref, kaug_sc, vt_sc, m_sc, l_sc, acc_sc):
        hp = pl.program_id(0)
        i = pl.program_id(1)
        lane = lax.broadcasted_iota(jnp.int32, (1, 128), 1)
        sub = lax.broadcasted_iota(jnp.int32, (128, 1), 0)
        head_sel = (lane < HEAD_DIM, lane >= HEAD_DIM)
        head_sel_t = (sub < HEAD_DIM, sub >= HEAD_DIM)
        spare = (HEAD_DIM, 0)
        zero = jnp.zeros((), BF16)

        @pl.when(i == 0)
        def _():
            def prep(jt, carry):
                rows = pl.ds(pl.multiple_of(jt * T, T), T)
                k = k_ref[rows, :]
                ft = f_ref[rows, :]
                vt = v_ref[rows, :].astype(F32).T
                for h in range(2):
                    fh = jnp.sum(jnp.where(lane == 2 * hp + h, ft, 0.0), axis=1, keepdims=True)
                    hi, mid, lo = _split3(-fh)
                    b = spare[h]
                    bias = jnp.where(lane == b, hi, jnp.where(lane == b + 1, mid, jnp.where(lane == b + 2, lo, zero)))
                    kaug_sc[h, rows, :] = jnp.where(head_sel[h], k, bias)
                    vt_sc[h, jt] = jnp.where(head_sel_t[h], vt, 0.0).astype(BF16)
                return carry

            lax.fori_loop(0, n_t, prep, 0)

        q = q_ref[...]
        q_heads = []
        for h in range(2):
            ones = jnp.where((lane >= spare[h]) & (lane < spare[h] + 3), jnp.ones((), BF16), zero)
            q_heads.append(jnp.where(head_sel[h], q, ones))
        m_sc[...] = jnp.full((8, T), NEG, F32)
        l_sc[...] = jnp.zeros((8, T), F32)
        acc_sc[...] = jnp.zeros((128, T), F32)

        def step(j, masked):
            rows = pl.ds(pl.multiple_of(j * T, T), T)
            alphas, pvs = [], []
            for h in range(2):
                s_t = _dot_nt(kaug_sc[h, rows, :], q_heads[h])
                if masked:
                    rr = lax.broadcasted_iota(jnp.int32, (T, T), 0)
                    cc = lax.broadcasted_iota(jnp.int32, (T, T), 1)
                    s_t = jnp.where(rr <= cc, s_t, NEG)
                m_prev = m_sc[h:h + 1, :]
                m_new = jnp.maximum(m_prev, jnp.max(s_t, axis=0, keepdims=True))
                alpha = jnp.exp(m_prev - m_new)
                p_t = jnp.exp(s_t - m_new)
                l_sc[h:h + 1, :] = alpha * l_sc[h:h + 1, :] + jnp.sum(p_t, axis=0, keepdims=True)
                m_sc[h:h + 1, :] = m_new
                alphas.append(alpha)
                pvs.append(_dot(vt_sc[h, j], p_t.astype(BF16)))
            acc_sc[...] = acc_sc[...] * jnp.where(head_sel_t[0], alphas[0], alphas[1]) + (pvs[0] + pvs[1])

        def off_diagonal(j, carry):
            step(j, False)
            return carry

        lax.fori_loop(0, i, off_diagonal, 0)
        step(i, True)
        l = l_sc[...]
        o_ref[...] = (acc_sc[...] / jnp.where(head_sel_t[0], l[0:1, :], l[1:2, :])).T
        is_head = lax.broadcasted_iota(jnp.int32, (8, 1), 0) < 2
        lse_ref[...] = jnp.where(is_head, m_sc[...] + jnp.log(jnp.where(is_head, l, 1.0)), 0.0)

    return pl.pallas_call(
        body, name="attention_fwd", grid=(N_PAIR, n_t),
        out_shape=(jax.ShapeDtypeStruct((S, D_ATT), F32), jax.ShapeDtypeStruct((N_PAIR, n_t, 8, T), F32)),
        in_specs=[pl.BlockSpec((T, 128), lambda hp, i: (i, hp)),
                  pl.BlockSpec((S, 128), lambda hp, i: (0, N_PAIR + hp)),
                  pl.BlockSpec((S, 128), lambda hp, i: (0, 2 * N_PAIR + hp)),
                  pl.BlockSpec((S, 128), lambda hp, i: (0, 0))],
        out_specs=(pl.BlockSpec((T, 128), lambda hp, i: (i, hp)),
                   pl.BlockSpec((None, None, 8, T), lambda hp, i: (hp, i, 0, 0))),
        scratch_shapes=[pltpu.VMEM((2, S, 128), BF16), pltpu.VMEM((2, n_t, 128, T), BF16),
                        pltpu.VMEM((8, T), F32), pltpu.VMEM((8, T), F32), pltpu.VMEM((128, T), F32)],
        compiler_params=_params(dimension_semantics=("arbitrary", "arbitrary")),
    )(qkv, qkv, qkv, big_f)


def _attention_bwd(qkv, datt, att, lse, big_f):
    S = qkv.shape[0]
    T = min(T_ATT, S)
    n_t = S // T

    rc = min(ATT_CHUNK, T)

    def body(q_ref, do_ref, o_ref, lse_ref, k_ref, v_ref, fk_ref,
             dq_ref, dk_ref, dv_ref, cs_ref, dfk_ref, dfq_ref, stat_sc, dqt_sc):
        hp = pl.program_id(0)
        j = pl.program_id(1)
        lane = lax.broadcasted_iota(jnp.int32, (1, 128), 1)
        is_a = lane < HEAD_DIM
        zero = jnp.zeros((), BF16)

        @pl.when(j == 0)
        def _():
            dqt_sc[...] = jnp.zeros_like(dqt_sc)
            cs_ref[...] = jnp.zeros_like(cs_ref)
            dfq_ref[...] = jnp.zeros_like(dfq_ref)

            def prep(i, carry):
                rows = pl.ds(pl.multiple_of(i * T, T), T)
                prod = o_ref[rows, :] * do_ref[rows, :].astype(F32)
                d_a = jnp.sum(jnp.where(is_a, prod, 0.0), axis=1, keepdims=True)
                d_b = jnp.sum(jnp.where(is_a, 0.0, prod), axis=1, keepdims=True)
                delta_t = _pair_select(is_a, d_a, d_b).T
                stat_sc[i, 0:2, :] = -lse_ref[i, 0:2, :]
                stat_sc[i, 2:3, :] = delta_t[0:1, :]
                stat_sc[i, 3:4, :] = delta_t[HEAD_DIM:HEAD_DIM + 1, :]
                return carry

            lax.fori_loop(0, n_t, prep, 0)

        k = k_ref[...]
        v = v_ref[...]
        fk = fk_ref[...]
        kt = k.astype(F32).T
        sub_a = lax.broadcasted_iota(jnp.int32, (128, 1), 0) < HEAD_DIM
        heads = []
        for h in range(2):
            sel = is_a if h == 0 else jnp.logical_not(is_a)
            sel_t = sub_a if h == 0 else jnp.logical_not(sub_a)
            fkh = jnp.sum(jnp.where(lane == 2 * hp + h, fk, 0.0), axis=1, keepdims=True)
            heads.append((sel, jnp.where(sel, k, zero), jnp.where(sel, v, zero),
                          jnp.where(sel_t, kt, 0.0).astype(BF16), fkh))

        def step(i, masked, acc):
            dk_acc, dv_acc, dfa, dfb = acc
            rows = pl.ds(pl.multiple_of(i * T, T), T)
            q = q_ref[rows, :]
            do = do_ref[rows, :]
            stat = stat_sc[i]
            dqt = dqt_sc[i]
            dfs = [dfa, dfb]
            for h in range(2):
                sel, kh, vh, kth, fkh = heads[h]
                s_full = _dot_nt(kh, q)
                dp_full = _dot_nt(vh, do)
                bias = stat[h:h + 1, :]
                delta = stat[2 + h:3 + h, :]
                p_c, ds_c, dfk_c = [], [], []
                dfq8 = jnp.zeros((8, T), F32)
                for r in range(T // rc):
                    sl = slice(r * rc, (r + 1) * rc)
                    arg = s_full[sl, :] + (bias - fkh[sl, :])
                    if masked:
                        rr = lax.broadcasted_iota(jnp.int32, (rc, T), 0) + r * rc
                        cc = lax.broadcasted_iota(jnp.int32, (rc, T), 1)
                        arg = jnp.where(rr <= cc, arg, NEG)
                    p_t = jnp.exp(arg)
                    ds_t = p_t * (dp_full[sl, :] - delta)
                    p_c.append(p_t.astype(BF16))
                    ds_c.append(ds_t.astype(BF16))
                    dfk_c.append(jnp.sum(ds_t, axis=1, keepdims=True))
                    for g in range(rc // 8):
                        dfq8 = dfq8 + ds_t[8 * g:8 * g + 8, :]
                ds_bf = jnp.concatenate(ds_c, axis=0)
                dv_acc = dv_acc + _dot(jnp.concatenate(p_c, axis=0), jnp.where(sel, do, zero))
                dk_acc = dk_acc + _dot(ds_bf, jnp.where(sel, q, zero))
                dqt = dqt + _dot(kth, ds_bf)
                dfs[h] = dfs[h] + jnp.concatenate(dfk_c, axis=0)
                dfq_ref[i, h:h + 1, :] += _colsum(dfq8)
            dqt_sc[i] = dqt
            return dk_acc, dv_acc, dfs[0], dfs[1]

        acc0 = (jnp.zeros((T, 128), F32), jnp.zeros((T, 128), F32), jnp.zeros((T, 1), F32), jnp.zeros((T, 1), F32))
        acc1 = step(j, True, acc0)
        dk_acc, dv_acc, dfa, dfb = lax.fori_loop(j + 1, n_t, lambda i, a: step(i, False, a), acc1)
        dk_ref[...] = dk_acc.astype(BF16)
        dv_ref[...] = dv_acc.astype(BF16)
        dfk_ref[...] = -jnp.where(lane == 0, dfa, jnp.where(lane == 1, dfb, 0.0))
        cs_ref[:, 128:256] = cs_ref[:, 128:256] + _colsum(dk_acc)
        cs_ref[:, 256:384] = cs_ref[:, 256:384] + _colsum(dv_acc)

        @pl.when(j == n_t - 1)
        def _():
            def finish(i, tot):
                dq = dqt_sc[i].T
                dq_ref[pl.ds(pl.multiple_of(i * T, T), T), :] = dq.astype(BF16)
                return tot + _colsum(dq)

            cs_ref[:, 0:128] = lax.fori_loop(0, n_t, finish, jnp.zeros((1, 128), F32))

    pair_rows = lambda hp, j: (hp, 0, 0)
    return pl.pallas_call(
        body, name="attention_bwd", grid=(N_PAIR, n_t),
        out_shape=(jax.ShapeDtypeStruct((S, D_ATT), BF16), jax.ShapeDtypeStruct((S, D_ATT), BF16),
                   jax.ShapeDtypeStruct((S, D_ATT), BF16), jax.ShapeDtypeStruct((N_PAIR, 1, 384), F32),
                   jax.ShapeDtypeStruct((N_PAIR, S, 128), F32),
                   jax.ShapeDtypeStruct((N_PAIR, n_t, 8, T), F32)),
        in_specs=[pl.BlockSpec((S, 128), lambda hp, j: (0, hp)),
                  pl.BlockSpec((S, 128), lambda hp, j: (0, hp)),
                  pl.BlockSpec((S, 128), lambda hp, j: (0, hp)),
                  pl.BlockSpec((None, n_t, 8, T), lambda hp, j: (hp, 0, 0, 0)),
                  pl.BlockSpec((T, 128), lambda hp, j: (j, N_PAIR + hp)),
                  pl.BlockSpec((T, 128), lambda hp, j: (j, 2 * N_PAIR + hp)),
                  pl.BlockSpec((T, 128), lambda hp, j: (j, 0))],
        out_specs=(pl.BlockSpec((S, 128), lambda hp, j: (0, hp)),
                   pl.BlockSpec((T, 128), lambda hp, j: (j, hp)),
                   pl.BlockSpec((T, 128), lambda hp, j: (j, hp)),
                   pl.BlockSpec((None, 1, 384), pair_rows),
                   pl.BlockSpec((None, T, 128), lambda hp, j: (hp, j, 0)),
                   pl.BlockSpec((None, n_t, 8, T), lambda hp, j: (hp, 0, 0, 0))),
        scratch_shapes=[pltpu.VMEM((n_t, 8, T), F32), pltpu.VMEM((n_t, 128, T), F32)],
        compiler_params=_params(dimension_semantics=("arbitrary", "arbitrary")),
    )(qkv, datt, att, lse, qkv, qkv, big_f)


def _window_counts(first_row, n_rows, window):
    t = lax.broadcasted_iota(jnp.int32, (n_rows, 1), 0) + first_row
    return jnp.minimum((t + 1).astype(F32), float(window))


def _middle(x, tgt, att, g, p, gate, w_mix, b_mix, pool_scale, w_out, b_out, ln_g, ln_b):
    S = x.shape[0]
    tm = min(TM_MID, S)
    halo_blocks = tm // POOL_HALO

    def body(x_ref, t_ref, att_ref, g_ref, p_ref, ph_ref, gate_ref, wm_ref, bm_ref, ps_ref, wo_ref, bo_ref,
             lg_ref, lb_ref,
             dh_ref, datt_ref, dg_ref, dpl_ref, gwo_ref, gwm_ref, vec_ref, loss_ref):
        i = pl.program_id(0)

        @pl.when(i == 0)
        def _():
            gwo_ref[...] = jnp.zeros_like(gwo_ref)
            gwm_ref[...] = jnp.zeros_like(gwm_ref)
            vec_ref[...] = jnp.zeros_like(vec_ref)
            loss_ref[...] = jnp.zeros_like(loss_ref)

        pc = p_ref[...]
        halo = jnp.where(i > 0, ph_ref[...], 0.0)
        pe = jnp.concatenate([halo, pc], axis=0)
        pooled_parts = []
        for gi, w in enumerate(POOL_WINDOWS):
            cur = pe[:, gi * POOL_GROUP:(gi + 1) * POOL_GROUP]
            span = 1
            while span < w:
                cur = cur + pltpu.roll(cur, span, 0)
                span *= 2
            wsum = cur[POOL_HALO:, :]
            mean = wsum / _window_counts(i * tm, tm, w)
            pooled_parts.append(mean - pc[:, gi * POOL_GROUP:(gi + 1) * POOL_GROUP])
        pooled_bf =[v.astype(BF16) for v in pooled_parts]
        mixed = jnp.concatenate([_dot(pooled_bf[gi], wm_ref[gi]) for gi in range(4)], axis=1) + bm_ref[...]
        ps = ps_ref[...]
        pool_out = mixed * ps
        gv = g_ref[...]
        sig = _sigmoid(gv)
        silu = gv * sig
        att = att_ref[...]
        y = jnp.concatenate([att * silu[:, :D_ATT], pool_out * silu[:, D_ATT:]], axis=1)
        y_bf = y.astype(BF16)
        wo = wo_ref[...]
        yo = _dot(y_bf, wo) + bo_ref[...]
        gate = gate_ref[...]
        h = ALPHA * x_ref[...] + gate * yo
        mu = jnp.mean(h, axis=1, keepdims=True)
        hc = h - mu
        var = jnp.mean(hc * hc, axis=1, keepdims=True)
        rstd = lax.rsqrt(var + LN_EPS)
        yhat = hc * rstd
        lg = lg_ref[...]
        out = yhat * lg + lb_ref[...]
        err = out - t_ref[...]
        loss_ref[...] += 0.5 * jnp.sum(jnp.mean(err * err, axis=1, keepdims=True), axis=0, keepdims=True)

        dout = err * (1.0 / D)
        g_ln_b = _colsum(dout)
        g_ln_g = _colsum(dout * yhat)
        dyh = dout * lg
        dh = rstd * (dyh - jnp.mean(dyh, axis=1, keepdims=True)
                     - yhat * jnp.mean(dyh * yhat, axis=1, keepdims=True))
        dh_ref[...] = dh
        d_gate = _colsum(dh * yo)
        dyo = gate * dh
        g_b_out = _colsum(dyo)
        dyo_bf = dyo.astype(BF16)
        gwo_ref[...] += _dot_tn(y_bf, dyo_bf)
        dy = _dot_nt(dyo_bf, wo)
        dsilu = sig * (1.0 + gv * (1.0 - sig))
        dy_a = dy[:, :D_ATT]
        dy_p = dy[:, D_ATT:]
        datt_ref[...] = (dy_a * silu[:, :D_ATT]).astype(BF16)
        dpo = dy_p * silu[:, D_ATT:]
        dg = jnp.concatenate([dy_a * att * dsilu[:, :D_ATT], dy_p * pool_out * dsilu[:, D_ATT:]], axis=1)
        dg_ref[...] = dg.astype(BF16)
        g_dg = _colsum(dg)
        g_ps = _colsum(dpo * mixed)
        dmixed = dpo * ps
        g_bm = _colsum(dmixed)
        dmixed_bf = dmixed.astype(BF16)
        dpl = []
        for gi in range(4):
            dm = dmixed_bf[:, gi * POOL_GROUP:(gi + 1) * POOL_GROUP]
            gwm_ref[gi] += _dot_tn(pooled_bf[gi], dm)
            dpl.append(_dot_nt(dm, wm_ref[gi]))
        dpl_ref[...] = jnp.concatenate(dpl, axis=1)
        vec_ref[0:1, :] += g_ln_g
        vec_ref[1:2, :] += g_ln_b
        vec_ref[2:3, :] += d_gate
        vec_ref[3:4, :] += g_b_out
        vec_ref[4:5, :] += g_dg
        vec_ref[5:6, 0:D_POOL] += g_ps
        vec_ref[6:7, 0:D_POOL] += g_bm

    row = lambda w: pl.BlockSpec((tm, w), lambda i: (i, 0))
    full2 = lambda a: pl.BlockSpec(a.shape, lambda i: (0, 0))
    full3 = lambda a: pl.BlockSpec(a.shape, lambda i: (0, 0, 0))
    return pl.pallas_call(
        body, name="middle", grid=(S // tm,),
        out_shape=(jax.ShapeDtypeStruct((S, D), F32),
                   jax.ShapeDtypeStruct((S, D_ATT), BF16),
                   jax.ShapeDtypeStruct((S, D), BF16),
                   jax.ShapeDtypeStruct((S, D_POOL), F32),
                   jax.ShapeDtypeStruct((D, D), F32),
                   jax.ShapeDtypeStruct((4, POOL_GROUP, POOL_GROUP), F32),
                   jax.ShapeDtypeStruct((8, D), F32),
                   jax.ShapeDtypeStruct((1, 1), F32)),
        in_specs=[row(D), row(D), row(D_ATT), row(D), row(D_POOL),
                  pl.BlockSpec((POOL_HALO, D_POOL), lambda i: (jnp.maximum(i * halo_blocks - 1, 0), 0)),
                  full2(gate), full3(w_mix), full2(b_mix), full2(pool_scale), full2(w_out), full2(b_out),
                  full2(ln_g), full2(ln_b)],
        out_specs=(row(D), row(D_ATT), row(D), row(D_POOL),
                   pl.BlockSpec((D, D), lambda i: (0, 0)),
                   pl.BlockSpec((4, POOL_GROUP, POOL_GROUP), lambda i: (0, 0, 0)),
                   pl.BlockSpec((8, D), lambda i: (0, 0)),
                   pl.BlockSpec((1, 1), lambda i: (0, 0))),
        compiler_params=_params(dimension_semantics=("arbitrary",)),
    )(x, tgt, att, g, p, p, gate, w_mix, b_mix, pool_scale, w_out, b_out, ln_g, ln_b)


def _tail(dpl, dfk8, f):
    S = dpl.shape[0]
    tm = min(TM_TAIL, S)
    n_t = S // tm
    halo_blocks = tm // POOL_HALO
    last_halo = S // POOL_HALO - 1

    def body(d_ref, dn_ref, dfk_ref, f_ref, dp_ref, df_ref, cs_ref, carry):
        s = pl.program_id(0)
        i = n_t - 1 - s

        @pl.when(s == 0)
        def _():
            carry[...] = jnp.zeros_like(carry)
            cs_ref[...] = jnp.zeros_like(cs_ref)

        dc = d_ref[...]
        nxt = jnp.where(s > 0, dn_ref[...], 0.0)
        de = jnp.concatenate([dc, nxt], axis=0)
        n_e = tm + POOL_HALO
        parts = []
        for gi, w in enumerate(POOL_WINDOWS):
            cur = de[:, gi * POOL_GROUP:(gi + 1) * POOL_GROUP] / _window_counts(i * tm, n_e, w)
            span = 1
            while span < w:
                cur = cur + pltpu.roll(cur, n_e - span, 0)
                span *= 2
            parts.append(cur[:tm, :] - dc[:, gi * POOL_GROUP:(gi + 1) * POOL_GROUP])
        dp = jnp.concatenate(parts, axis=1)
        dp_ref[...] = dp.astype(BF16)
        cs_ref[0:1, :] += _colsum(dp)

        r = lax.broadcasted_iota(jnp.int32, (tm, tm), 0)
        c = lax.broadcasted_iota(jnp.int32, (tm, tm), 1)
        tri = (r <= c).astype(F32)
        dlogf = jnp.dot(tri, dfk_ref[...], preferred_element_type=F32, precision=lax.Precision.HIGHEST) + carry[...]
        carry[...] = dlogf[0:1, :]
        df = dlogf * _sigmoid(-f_ref[...])
        df_ref[...] = df.astype(BF16)
        cs_ref[1:2, 0:128] += _colsum(df)

    rev = lambda w: pl.BlockSpec((tm, w), lambda s: (n_t - 1 - s, 0))
    return pl.pallas_call(
        body, name="tail", grid=(n_t,),
        out_shape=(jax.ShapeDtypeStruct((S, D_POOL), BF16), jax.ShapeDtypeStruct((S, 128), BF16),
                   jax.ShapeDtypeStruct((8, D_POOL), F32)),
        in_specs=[rev(D_POOL),
                  pl.BlockSpec((POOL_HALO, D_POOL),
                               lambda s: (jnp.minimum((n_t - s) * halo_blocks, last_halo), 0)),
                  rev(128), rev(128)],
        out_specs=(rev(D_POOL), rev(128), pl.BlockSpec((8, D_POOL), lambda s: (0, 0))),
        scratch_shapes=[pltpu.VMEM((1, 128), F32)],
        compiler_params=_params(dimension_semantics=("arbitrary",)),
    )(dpl, dpl, dfk8, f)


PIECES = ((O_QKV, D_ATT), (O_QKV + D_ATT, D_ATT), (O_QKV + 2 * D_ATT, D_ATT), (O_F, 128), (O_P, D_POOL), (O_G, D))


def _grad_w_in(u, pieces):
    S = u.shape[0]
    tm = min(TM_GW, S)
    n_t = S // tm

    def body(u_ref, *rest):
        piece_refs, out_ref, acc, sem = rest[:6], rest[6], rest[7], rest[8]
        i = pl.program_id(0)

        @pl.when(i == 0)
        def _():
            acc[...] = jnp.zeros_like(acc)

        u_t = u_ref[...]
        for (off, w), ref in zip(PIECES, piece_refs):
            acc[:, off:off + w] += _dot_tn(u_t, ref[...])

        @pl.when(i == n_t - 1)
        def _():
            cp = pltpu.make_async_copy(acc, out_ref, sem)
            cp.start()
            cp.wait()

    return pl.pallas_call(
        body, name="grad_w_in", grid=(n_t,),
        out_shape=jax.ShapeDtypeStruct((D, D_PAD), F32),
        in_specs=[pl.BlockSpec((tm, D), lambda i: (i, 0))]
        + [pl.BlockSpec((tm, w), lambda i: (i, 0)) for _, w in PIECES],
        out_specs=pl.BlockSpec(memory_space=pl.ANY),
        scratch_shapes=[pltpu.VMEM((D, D_PAD), F32), pltpu.SemaphoreType.DMA],
        compiler_params=_params(dimension_semantics=("arbitrary",)),
    )(u, *pieces)


def _grad_x(pieces, w_pad, dh, x, scale):
    S = x.shape[0]
    tm = min(TM_DU, S)

    def body(*refs):
        piece_refs = refs[:6]
        w_ref, dh_ref, x_ref, sc_ref, gx_ref, vec_ref = refs[6:]

        @pl.when(pl.program_id(0) == 0)
        def _():
            vec_ref[...] = jnp.zeros_like(vec_ref)

        du = jnp.zeros((tm, D), F32)
        for (off, w), ref in zip(PIECES, piece_refs):
            du = du + _dot_nt(ref[...], w_ref[:, off:off + w])
        xv = x_ref[...]
        gx_ref[...] = ALPHA * dh_ref[...] + du * (1.0 + sc_ref[...])
        vec_ref[0:1, :] += _colsum(du)
        vec_ref[1:2, :] += _colsum(du * xv)

    row = lambda w: pl.BlockSpec((tm, w), lambda i: (i, 0))
    return pl.pallas_call(
        body, name="grad_x", grid=(S // tm,),
        out_shape=(jax.ShapeDtypeStruct((S, D), F32), jax.ShapeDtypeStruct((8, D), F32)),
        in_specs=[row(w) for _, w in PIECES]
        + [pl.BlockSpec(w_pad.shape, lambda i: (0, 0)), row(D), row(D), pl.BlockSpec((1, D), lambda i: (0, 0))],
        out_specs=(row(D), pl.BlockSpec((8, D), lambda i: (0, 0))),
        compiler_params=_params(dimension_semantics=("arbitrary",)),
    )(*pieces, w_pad, dh, x, scale)


def _grad_ada(c_all, dada_all, dada_cols):
    def body(c_ref, dall_ref, dcol_ref, gw_ref, gb_ref):
        rows = lax.broadcasted_iota(jnp.int32, (8, 1), 0)
        cm = jnp.zeros((8, D), F32)
        dm = jnp.zeros((8, 3 * D), F32)
        for r in range(8):
            cm = jnp.where(rows == r, c_ref[r], cm)
            dm = jnp.where(rows == r, dall_ref[r], dm)
        act = cm * _sigmoid(cm)
        pad = jnp.zeros((8, D), F32)
        lhs = jnp.concatenate([act, pad], axis=0).astype(BF16)
        rhs = jnp.concatenate([dcol_ref[...], jnp.zeros((8, SHARD_ADA), F32)], axis=0).astype(BF16)
        gw_ref[...] = _dot_tn(lhs, rhs)
        gb_ref[...] = _colsum(dm)

    vm = pl.BlockSpec(memory_space=pltpu.VMEM)
    return pl.pallas_call(
        body, name="grad_ada",
        out_shape=(jax.ShapeDtypeStruct((D, SHARD_ADA), F32), jax.ShapeDtypeStruct((1, 3 * D), F32)),
        in_specs=[vm, vm, vm], out_specs=(vm, vm),
        compiler_params=_params(),
    )(c_all, dada_all, dada_cols)


def _adamw_math(w, g, m, v):
    m = ADAM_B1 * m + (1.0 - ADAM_B1) * g
    v = ADAM_B2 * v + (1.0 - ADAM_B2) * (g * g)
    m_hat = m / (1.0 - ADAM_B1 ** ADAM_STEP)
    v_hat = v / (1.0 - ADAM_B2 ** ADAM_STEP)
    delta = -ADAM_LR * (m_hat / (jnp.sqrt(v_hat) + ADAM_EPS) + ADAM_WD * w)
    return delta, m, v


def _adamw(groups, n_steps):
    n = len(groups)

    def body(*refs):
        ins, outs = refs[:4 * n], refs[4 * n:]
        for t in range(n):
            w, g, m, v = (r[...] for r in ins[4 * t:4 * t + 4])
            d, m2, v2 = _adamw_math(w, g, m, v)
            outs[3 * t][...] = d
            outs[3 * t + 1][...] = m2
            outs[3 * t + 2][...] = v2

    in_specs, out_specs, out_shape, args = [], [], [], []
    for (w, g, m, v) in groups:
        r, c = w.shape
        spec = pl.BlockSpec((r // n_steps, c), lambda i: (i, 0))
        in_specs += [spec] * 4
        out_specs += [spec] * 3
        out_shape += [jax.ShapeDtypeStruct((r, c), F32)] * 3
        args += [w, g, m, v]
    return pl.pallas_call(
        body, name="adamw_%d" % n, grid=(n_steps,),
        out_shape=tuple(out_shape), in_specs=in_specs, out_specs=tuple(out_specs),
        compiler_params=_params(dimension_semantics=("arbitrary",)),
    )(*args)


def _pack_small(parts):
    rows = []
    used = 0
    for name, (first, n_rows) in SMALL_SEGS.items():
        if first > used:
            rows.append(jnp.zeros((first - used, 128), F32))
        flat = parts[name].reshape(-1)
        flat = jnp.pad(flat, (0, n_rows * 128 - flat.shape[0]))
        rows.append(flat.reshape(n_rows, 128))
        used = first + n_rows
    rows.append(jnp.zeros((SMALL_ROWS - used, 128), F32))
    return jnp.concatenate(rows, axis=0)


def _unpack_small(buf, name, shape):
    first, n_rows = SMALL_SEGS[name]
    n = int(np.prod(shape))
    return buf[first:first + n_rows].reshape(-1)[:n].reshape(shape)


def _pad_in(v):
    r = v.shape[0]
    z = jnp.zeros((r, O_P - O_F - N_HEADS), v.dtype)
    return jnp.concatenate([v[:, :3 * D_ATT + N_HEADS], z, v[:, 3 * D_ATT + N_HEADS:]], axis=1)


def _unpad_in(v):
    return jnp.concatenate([v[:, :O_F + N_HEADS], v[:, O_P:]], axis=1)


def _shards_in(v):
    gap = O_P - (O_F + N_HEADS)
    parts = []
    for a in range(N_CHIPS):
        lo, hi = a * SHARD_IN, (a + 1) * SHARD_IN
        cut = O_F + N_HEADS
        if hi <= cut:
            parts.append(v[:, lo:hi])
        elif lo >= cut:
            parts.append(v[:, lo + gap:hi + gap])
        else:
            parts.append(jnp.concatenate([v[:, lo:cut], v[:, cut + gap:hi + gap]], axis=1))
    return jnp.stack(parts, axis=0)


def kernel(x, c, w_ada, b_ada, w_in, b_in, w_pool_mix, b_pool_mix, pool_scale, w_out, b_out, ln_g, ln_b, loss_target, m_w_ada, m_b_ada, m_w_in, m_b_in, m_w_pool_mix, m_b_pool_mix, m_pool_scale, m_w_out, m_b_out, m_ln_g, m_ln_b, v_w_ada, v_b_ada, v_w_in, v_b_in, v_w_pool_mix, v_b_pool_mix, v_pool_scale, v_w_out, v_b_out, v_ln_g, v_ln_b):
    S = x.shape[1]
    T = min(T_ATT, S)
    n_t = S // T
    chip = 2 * lax.axis_index("x") + lax.axis_index("y")
    x2 = x[0]
    tgt = loss_target[0]
    q_scale = jnp.concatenate([jnp.full((1, D_ATT), Q_SCALE, F32), jnp.ones((1, D_PAD - D_ATT), F32)], axis=1)

    c_all, ada4 = _ada_exchange(c, w_ada[0], b_ada.reshape(4, 1, SHARD_ADA))
    ada = ada4[:, 0, :].reshape(1, 3 * D)
    shift, scale, gate = ada[:, :D], ada[:, D:2 * D], ada[:, 2 * D:]
    w_in_all, w_out_all = _gather_weights(w_in[0].astype(BF16), w_out[0].astype(BF16))
    w_in_full = jnp.transpose(w_in_all, (1, 0, 2)).reshape(D, D_IN)
    w_pad = _pad_in(w_in_full) * q_scale.astype(BF16)
    b_pad = _pad_in(b_in) * q_scale
    w_out_full = w_out_all.reshape(D, D)
    w_mix_bf = w_pool_mix[0].astype(BF16)

    u, qkv, f, p, g = _in_proj(x2, shift, scale, w_pad, b_pad)
    big_f = _forget_cumsum(f)
    att, lse = _attention_fwd(qkv, big_f)

    dh, datt, dg, dpl, gw_out, gw_mix, vec, loss_part = _middle(
        x2, tgt, att, g, p, gate, w_mix_bf, b_pool_mix.reshape(1, D_POOL), pool_scale, w_out_full, b_out, ln_g, ln_b)
    dq, dk, dv, cs_att, dfk, dfq = _attention_bwd(qkv, datt, att, lse, big_f)
    dfk8 = jnp.transpose(dfk[:, :, 0:2], (1, 0, 2)).reshape(S, N_HEADS)
    dfk8 = dfk8 + jnp.transpose(dfq[:, :, 0:2, :], (1, 3, 0, 2)).reshape(S, N_HEADS)
    dfk8 = jnp.pad(dfk8, ((0, 0), (0, 128 - N_HEADS)))
    dp, df, cs_tail = _tail(dpl, dfk8, f)
    pieces = (dq, dk, dv, df, dp, dg)
    gw_pad = _grad_w_in(u, pieces)
    grad_x, vec_x = _grad_x(pieces, w_pad, dh, x2, scale)

    cs_qkv = jnp.transpose(cs_att.reshape(N_PAIR, 3, 128), (1, 0, 2)).reshape(1, 3 * D_ATT)
    gb_pad = jnp.concatenate([cs_qkv, cs_tail[1:2, 0:128], cs_tail[0:1, :], vec[4:5, :]], axis=1) * q_scale
    dada = jnp.concatenate([vec_x[0:1, :], vec_x[1:2, :], vec[2:3, :]], axis=1)
    small = _pack_small({
        "b_in": gb_pad, "w_pool_mix": gw_mix, "b_pool_mix": vec[6:7, :D_POOL], "pool_scale": vec[5:6, :D_POOL],
        "b_out": vec[3:4, :], "ln_g": vec[0:1, :], "ln_b": vec[1:2, :], "b_ada": jnp.zeros((1, 3 * D), F32)})

    g_w_in = _reduce_scatter(_shards_in(gw_pad), _shards_in(q_scale), "reduce_w_in")
    g_w_out = _reduce_scatter(gw_out.reshape(N_CHIPS, SHARD_OUT, D), jnp.ones((N_CHIPS, 1, D), F32), "reduce_w_out")
    small_sum, dada_all = _all_reduce_small(small[:SMALL_REDUCED_ROWS], dada)
    dada_cols = lax.dynamic_slice(dada_all[:, 0, :], (0, chip * SHARD_ADA), (8, SHARD_ADA))
    g_w_ada, g_b_ada = _grad_ada(c_all, dada_all, dada_cols)
    loss = lax.psum(loss_part[0, 0], ("x", "y", "c"))

    grads_small = jnp.concatenate([small_sum, g_b_ada.reshape(24, 128)], axis=0)
    small_w = {"b_in": _pad_in(b_in), "w_pool_mix": w_pool_mix, "b_pool_mix": b_pool_mix, "pool_scale": pool_scale,
               "b_out": b_out, "ln_g": ln_g, "ln_b": ln_b, "b_ada": b_ada}
    small_m = {"b_in": _pad_in(m_b_in), "w_pool_mix": m_w_pool_mix, "b_pool_mix": m_b_pool_mix,
               "pool_scale": m_pool_scale, "b_out": m_b_out, "ln_g": m_ln_g, "ln_b": m_ln_b, "b_ada": m_b_ada}
    small_v = {"b_in": _pad_in(v_b_in), "w_pool_mix": v_w_pool_mix, "b_pool_mix": v_b_pool_mix,
               "pool_scale": v_pool_scale, "b_out": v_b_out, "ln_g": v_ln_g, "ln_b": v_ln_b, "b_ada": v_b_ada}
    big = _adamw([(w_ada[0], g_w_ada, m_w_ada[0], v_w_ada[0]),
                  (w_in[0], g_w_in, m_w_in[0], v_w_in[0]),
                  (w_out[0], g_w_out, m_w_out[0], v_w_out[0])], 8)
    sm = _adamw([(_pack_small(small_w), grads_small, _pack_small(small_m), _pack_small(small_v))], 1)

    names = ["w_ada", "b_ada", "w_in", "b_in", "w_pool_mix", "b_pool_mix", "pool_scale", "w_out", "b_out",
             "ln_g", "ln_b"]
    shapes = {"b_ada": (1, 3 * D), "b_in": (1, D_PAD), "w_pool_mix": (1, 4, POOL_GROUP, POOL_GROUP),
              "b_pool_mix": (1, 4, POOL_GROUP), "pool_scale": (1, D_POOL), "b_out": (1, D), "ln_g": (1, D),
              "ln_b": (1, D)}
    big_idx = {"w_ada": 0, "w_in": 1, "w_out": 2}

    def leaf(kind, name):
        if name in big_idx:
            if kind == 0:
                return (g_w_ada, g_w_in, g_w_out)[big_idx[name]][None]
            return big[3 * big_idx[name] + kind - 1][None]
        buf = grads_small if kind == 0 else sm[kind - 1]
        val = _unpack_small(buf, name, shapes[name])
        if name == "b_in":
            val = _unpad_in(val)
        return val

    outs = [loss, grad_x[None]]
    for kind in range(4):
        outs += [leaf(kind, n) for n in names]
    return tuple(outs)
```

```python
import functools

import numpy as np
import jax
import jax.numpy as jnp
from jax import lax
from jax.experimental import pallas as pl
from jax.experimental.pallas import tpu as pltpu

F32 = jnp.float32
BF16 = jnp.bfloat16
MESH = pl.DeviceIdType.MESH

D = 1024
D_ATT = 512
D_POOL = 512
N_HEADS = 8
HEAD_DIM = 64
N_PAIR = N_HEADS // 2
POOL_WINDOWS = (2, 4, 8, 16)
POOL_GROUP = 128
POOL_HALO = 16
LN_EPS = 1e-5
ALPHA = 2.0 ** 0.25
D_IN = 3 * D_ATT + N_HEADS + D_POOL + D_ATT + D_POOL
N_CHIPS = 4
SHARD_IN = D_IN // N_CHIPS
SHARD_ADA = 3 * D // N_CHIPS
SHARD_OUT = D // N_CHIPS

O_QKV, O_F, O_P, O_G, D_PAD = 0, 1536, 1664, 2176, 3200
Q_SCALE = HEAD_DIM ** -0.5

ADAM_LR, ADAM_B1, ADAM_B2, ADAM_EPS, ADAM_WD, ADAM_STEP = 0.001, 0.9, 0.999, 1e-08, 0.01, 10

NEG = -1e30

VMEM_LIMIT = 56 * 1024 * 1024

TM_PROJ = 512
T_ATT = 512
ATT_CHUNK = 32
TM_MID = 256
TM_TAIL = 512
TM_GW = 512
TM_DU = 512

REL7 = [(0, 0, 1), (0, 1, 0), (0, 1, 1), (1, 0, 0), (1, 0, 1), (1, 1, 0), (1, 1, 1)]
REL3 = [(0, 1), (1, 0), (1, 1)]

SMALL_SEGS = {}
_row = 0
for _name, _n in (("b_in", 3200), ("w_pool_mix", 65536), ("b_pool_mix", 512), ("pool_scale", 512),
                  ("b_out", 1024), ("ln_g", 1024), ("ln_b", 1024)):
    _rows = -(-_n // 1024) * 8
    SMALL_SEGS[_name] = (_row, _rows)
    _row += _rows
SMALL_REDUCED_ROWS = -(-_row // 16) * 16
SMALL_SEGS["b_ada"] = (SMALL_REDUCED_ROWS, 24)
SMALL_ROWS = SMALL_REDUCED_ROWS + 24


def _params(**kw):
    return pltpu.CompilerParams(vmem_limit_bytes=VMEM_LIMIT, **kw)


def _flip(v, d):
    return v if d == 0 else 1 - v


def _dot(a, b):
    return jnp.dot(a, b, preferred_element_type=F32)


def _dot_nt(a, b):
    return lax.dot_general(a, b, (((1,), (1,)), ((), ())), preferred_element_type=F32)


def _dot_tn(a, b):
    return lax.dot_general(a, b, (((0,), (0,)), ((), ())), preferred_element_type=F32)


def _sigmoid(v):
    return 1.0 / (1.0 + jnp.exp(-v))


def _colsum(v):
    return jnp.sum(v, axis=0, keepdims=True)


def _ada_exchange(c, w_ada, b_ada4):
    def body(c_ref, w_ref, b_ref, call_ref, ada_ref, cslab, sbuf, rbuf, cs_sem, cr_sem, as_sem, ar_sem):
        x, y, cc = lax.axis_index("x"), lax.axis_index("y"), lax.axis_index("c")
        me = 4 * x + 2 * y + cc
        chip = 2 * x + y
        cslab[...] = jnp.broadcast_to(c_ref[...], (8, D))
        call_ref[me] = cslab[...]
        gathers = []
        for k, (dx, dy, dc) in enumerate(REL7):
            cp = pltpu.make_async_remote_copy(
                src_ref=cslab, dst_ref=call_ref.at[me], send_sem=cs_sem.at[k], recv_sem=cr_sem.at[k],
                device_id=(_flip(x, dx), _flip(y, dy), _flip(cc, dc)), device_id_type=MESH)
            cp.start()
            gathers.append(cp)
        for cp in gathers:
            cp.wait()
        rows = lax.broadcasted_iota(jnp.int32, (8, 1), 0)
        mat = jnp.zeros((8, D), F32)
        for r in range(8):
            mat = jnp.where(rows == r, call_ref[r], mat)
        act = (mat * _sigmoid(mat)).astype(BF16)
        part = _dot(act, w_ref[...].astype(BF16))
        sends = []
        for k, (dx, dy) in enumerate(REL3):
            px, py = _flip(x, dx), _flip(y, dy)
            r = 4 * px + 2 * py + cc
            piece = _colsum(jnp.where(rows == r, part, 0.0))
            sbuf[k] = jnp.broadcast_to(piece, (8, SHARD_ADA))
            cp = pltpu.make_async_remote_copy(
                src_ref=sbuf.at[k], dst_ref=rbuf.at[k], send_sem=as_sem.at[k], recv_sem=ar_sem.at[k],
                device_id=(px, py, cc), device_id_type=MESH)
            cp.start()
            sends.append(cp)
        own = _colsum(jnp.where(rows == me, part, 0.0))
        ada_ref[chip] = jnp.broadcast_to(own, (8, SHARD_ADA)) + b_ref[chip]
        for k, (dx, dy) in enumerate(REL3):
            sends[k].wait()
            a = 2 * _flip(x, dx) + _flip(y, dy)
            ada_ref[a] = rbuf[k] + b_ref[a]

    vm = pl.BlockSpec(memory_space=pltpu.VMEM)
    return pl.pallas_call(
        body, name="ada_exchange",
        out_shape=(jax.ShapeDtypeStruct((8, 8, D), F32), jax.ShapeDtypeStruct((4, 8, SHARD_ADA), F32)),
        in_specs=[vm, vm, vm], out_specs=(vm, vm),
        scratch_shapes=[pltpu.VMEM((8, D), F32), pltpu.VMEM((3, 8, SHARD_ADA), F32),
                        pltpu.VMEM((3, 8, SHARD_ADA), F32),
                        pltpu.SemaphoreType.DMA((7,)), pltpu.SemaphoreType.DMA((7,)),
                        pltpu.SemaphoreType.DMA((3,)), pltpu.SemaphoreType.DMA((3,))],
        compiler_params=_params(),
    )(c, w_ada, b_ada4)


def _gather_weights(w_in_sh, w_out_sh):
    def body(win_ref, wout_ref, win_all, wout_all, own_sem, s_sem, r_sem, fs_sem, fr_sem):
        x, y, cc = lax.axis_index("x"), lax.axis_index("y"), lax.axis_index("c")
        chip = 2 * x + y
        sib = (x, y, 1 - cc)
        srcs = (win_ref, wout_ref)
        dsts = (win_all, wout_all)
        halves = (D // 2, SHARD_OUT // 2)

        def rows(t, which):
            h = halves[t]
            return pl.ds(pl.multiple_of(which * h, h), h)

        own = [pltpu.make_async_copy(srcs[t], dsts[t].at[chip], own_sem.at[t]) for t in range(2)]
        for cp in own:
            cp.start()
        first = []
        for k, (dx, dy) in enumerate(REL3):
            peer = (_flip(x, dx), _flip(y, dy), cc)
            for t in range(2):
                cp = pltpu.make_async_remote_copy(
                    src_ref=srcs[t].at[rows(t, cc), :], dst_ref=dsts[t].at[chip, rows(t, cc), :],
                    send_sem=s_sem.at[2 * k + t], recv_sem=r_sem.at[2 * k + t],
                    device_id=peer, device_id_type=MESH)
                cp.start()
                first.append(cp)
        passed = []
        for k, (dx, dy) in enumerate(REL3):
            a = 2 * _flip(x, dx) + _flip(y, dy)
            for t in range(2):
                landed = dsts[t].at[a, rows(t, cc), :]
                pltpu.make_async_remote_copy(
                    src_ref=landed, dst_ref=landed, send_sem=s_sem.at[2 * k + t], recv_sem=r_sem.at[2 * k + t],
                    device_id=sib, device_id_type=MESH).wait_recv()
                cp = pltpu.make_async_remote_copy(
                    src_ref=landed, dst_ref=landed, send_sem=fs_sem.at[2 * k + t], recv_sem=fr_sem.at[2 * k + t],
                    device_id=sib, device_id_type=MESH)
                cp.start()
                passed.append(cp)
        for k, (dx, dy) in enumerate(REL3):
            a = 2 * _flip(x, dx) + _flip(y, dy)
            for t in range(2):
                other = dsts[t].at[a, rows(t, 1 - cc), :]
                pltpu.make_async_remote_copy(
                    src_ref=other, dst_ref=other, send_sem=fs_sem.at[2 * k + t], recv_sem=fr_sem.at[2 * k + t],
                    device_id=sib, device_id_type=MESH).wait_recv()
        for cp in first + passed:
            cp.wait_send()
        for cp in own:
            cp.wait()

    vm = pl.BlockSpec(memory_space=pltpu.VMEM)
    return pl.pallas_call(
        body, name="gather_weights",
        out_shape=(jax.ShapeDtypeStruct((N_CHIPS, D, SHARD_IN), BF16),
                   jax.ShapeDtypeStruct((N_CHIPS, SHARD_OUT, D), BF16)),
        in_specs=[vm, vm], out_specs=(vm, vm),
        scratch_shapes=[pltpu.SemaphoreType.DMA((2,)), pltpu.SemaphoreType.DMA((6,)),
                        pltpu.SemaphoreType.DMA((6,)), pltpu.SemaphoreType.DMA((6,)),
                        pltpu.SemaphoreType.DMA((6,))],
        compiler_params=_params(),
    )(w_in_sh, w_out_sh)


def _reduce_scatter(g4, scale4, name):
    _, R, C = g4.shape
    RH = R // 2

    def body(g_ref, sc_ref, out_ref, sib_buf, send_buf, ici_buf, sem1, sem2s, sem2r, sem3):
        x, y, cc = lax.axis_index("x"), lax.axis_index("y"), lax.axis_index("c")
        chip = 2 * x + y
        sib = (x, y, 1 - cc)
        mine = pl.ds(pl.multiple_of(cc * RH, RH), RH)
        theirs = pl.ds(pl.multiple_of((1 - cc) * RH, RH), RH)
        cp1 = pltpu.make_async_remote_copy(
            src_ref=g_ref.at[:, theirs, :], dst_ref=sib_buf, send_sem=sem1.at[0], recv_sem=sem1.at[1],
            device_id=sib, device_id_type=MESH)
        cp1.start()
        cp1.wait()
        for a in range(N_CHIPS):
            both = g_ref[a, mine, :] + sib_buf[a]
            sib_buf[a] = both
            send_buf[a] = both.astype(BF16)
        sends = []
        for k, (dx, dy) in enumerate(REL3):
            px, py = _flip(x, dx), _flip(y, dy)
            cp = pltpu.make_async_remote_copy(
                src_ref=send_buf.at[2 * px + py], dst_ref=ici_buf.at[chip],
                send_sem=sem2s.at[k], recv_sem=sem2r.at[k], device_id=(px, py, cc), device_id_type=MESH)
            cp.start()
            sends.append(cp)
        ici_buf[chip] = send_buf[chip]
        for cp in sends:
            cp.wait()
        own = sib_buf[chip]
        parts = [jnp.where(chip == a, own, ici_buf[a].astype(F32)) for a in range(N_CHIPS)]
        out_ref[mine, :] = ((parts[0] + parts[1]) + (parts[2] + parts[3])) * sc_ref[chip]
        cp3 = pltpu.make_async_remote_copy(
            src_ref=out_ref.at[mine, :], dst_ref=out_ref.at[mine, :], send_sem=sem3.at[0], recv_sem=sem3.at[1],
            device_id=sib, device_id_type=MESH)
        cp3.start()
        cp3.wait()

    vm = pl.BlockSpec(memory_space=pltpu.VMEM)
    return pl.pallas_call(
        body, name=name,
        out_shape=jax.ShapeDtypeStruct((R, C), F32),
        in_specs=[vm, vm], out_specs=vm,
        scratch_shapes=[pltpu.VMEM((N_CHIPS, RH, C), F32), pltpu.VMEM((N_CHIPS, RH, C), BF16),
                        pltpu.VMEM((N_CHIPS, RH, C), BF16),
                        pltpu.SemaphoreType.DMA((2,)), pltpu.SemaphoreType.DMA((3,)),
                        pltpu.SemaphoreType.DMA((3,)), pltpu.SemaphoreType.DMA((2,))],
        compiler_params=_params(),
    )(g4, scale4)


def _all_reduce_small(g, dada):
    R = g.shape[0]
    RH = R // 2
    W = dada.shape[1]

    def body(g_ref, d_ref, out_ref, dall_ref, dslab, sib_buf, ici_buf, ds_sem, dr_sem, sem1, sem2s, sem2r, sem3):
        x, y, cc = lax.axis_index("x"), lax.axis_index("y"), lax.axis_index("c")
        me = 4 * x + 2 * y + cc
        chip = 2 * x + y
        sib = (x, y, 1 - cc)
        dslab[...] = jnp.broadcast_to(d_ref[...], (8, W))
        dall_ref[me] = dslab[...]
        gathers = []
        for k, (dx, dy, dc) in enumerate(REL7):
            cp = pltpu.make_async_remote_copy(
                src_ref=dslab, dst_ref=dall_ref.at[me], send_sem=ds_sem.at[k], recv_sem=dr_sem.at[k],
                device_id=(_flip(x, dx), _flip(y, dy), _flip(cc, dc)), device_id_type=MESH)
            cp.start()
            gathers.append(cp)
        mine = pl.ds(pl.multiple_of(cc * RH, 8), RH)
        theirs = pl.ds(pl.multiple_of((1 - cc) * RH, 8), RH)
        cp1 = pltpu.make_async_remote_copy(
            src_ref=g_ref.at[theirs, :], dst_ref=sib_buf, send_sem=sem1.at[0], recv_sem=sem1.at[1],
            device_id=sib, device_id_type=MESH)
        cp1.start()
        cp1.wait()
        sib_buf[...] = g_ref[mine, :] + sib_buf[...]
        sends = []
        for k, (dx, dy) in enumerate(REL3):
            px, py = _flip(x, dx), _flip(y, dy)
            cp = pltpu.make_async_remote_copy(
                src_ref=sib_buf, dst_ref=ici_buf.at[chip],
                send_sem=sem2s.at[k], recv_sem=sem2r.at[k], device_id=(px, py, cc), device_id_type=MESH)
            cp.start()
            sends.append(cp)
        ici_buf[chip] = sib_buf[...]
        for cp in sends:
            cp.wait()
        out_ref[mine, :] = (ici_buf[0] + ici_buf[1]) + (ici_buf[2] + ici_buf[3])
        cp3 = pltpu.make_async_remote_copy(
            src_ref=out_ref.at[mine, :], dst_ref=out_ref.at[mine, :], send_sem=sem3.at[0], recv_sem=sem3.at[1],
            device_id=sib, device_id_type=MESH)
        cp3.start()
        cp3.wait()
        for cp in gathers:
            cp.wait()

    vm = pl.BlockSpec(memory_space=pltpu.VMEM)
    return pl.pallas_call(
        body, name="all_reduce_small",
        out_shape=(jax.ShapeDtypeStruct((R, 128), F32), jax.ShapeDtypeStruct((8, 8, W), F32)),
        in_specs=[vm, vm], out_specs=(vm, vm),
        scratch_shapes=[pltpu.VMEM((8, W), F32), pltpu.VMEM((RH, 128), F32), pltpu.VMEM((N_CHIPS, RH, 128), F32),
                        pltpu.SemaphoreType.DMA((7,)), pltpu.SemaphoreType.DMA((7,)),
                        pltpu.SemaphoreType.DMA((2,)), pltpu.SemaphoreType.DMA((3,)),
                        pltpu.SemaphoreType.DMA((3,)), pltpu.SemaphoreType.DMA((2,))],
        compiler_params=_params(),
    )(g, dada)


def _in_proj(x, shift, scale, w_pad, b_pad):
    S = x.shape[0]
    tm = min(TM_PROJ, S)

    def body(x_ref, sh_ref, sc_ref, w_ref, b_ref, u_ref, qkv_ref, f_ref, p_ref, g_ref):
        u = (x_ref[...] * (1.0 + sc_ref[...]) + sh_ref[...]).astype(BF16)
        u_ref[...] = u
        qkv_ref[...] = (_dot(u, w_ref[:, O_QKV:O_F]) + b_ref[:, O_QKV:O_F]).astype(BF16)
        f_ref[...] = _dot(u, w_ref[:, O_F:O_P]) + b_ref[:, O_F:O_P]
        p_ref[...] = _dot(u, w_ref[:, O_P:O_G]) + b_ref[:, O_P:O_G]
        g_ref[...] = _dot(u, w_ref[:, O_G:D_PAD]) + b_ref[:, O_G:D_PAD]

    row = lambda w: pl.BlockSpec((tm, w), lambda i: (i, 0))
    full = lambda a: pl.BlockSpec(a.shape, lambda i: (0, 0))
    return pl.pallas_call(
        body, name="in_proj", grid=(S // tm,),
        out_shape=(jax.ShapeDtypeStruct((S, D), BF16), jax.ShapeDtypeStruct((S, 3 * D_ATT), BF16),
                   jax.ShapeDtypeStruct((S, 128), F32), jax.ShapeDtypeStruct((S, D_POOL), F32),
                   jax.ShapeDtypeStruct((S, D), F32)),
        in_specs=[row(D), full(shift), full(scale), full(w_pad), full(b_pad)],
        out_specs=(row(D), row(3 * D_ATT), row(128), row(D_POOL), row(D)),
        compiler_params=_params(dimension_semantics=("arbitrary",)),
    )(x, shift, scale, w_pad, b_pad)


def _forget_cumsum(f):
    S = f.shape[0]
    tm = min(T_ATT, S)

    def body(f_ref, out_ref, carry):
        @pl.when(pl.program_id(0) == 0)
        def _():
            carry[...] = jnp.zeros_like(carry)
        v = f_ref[...]
        logf = jnp.minimum(v, 0.0) - jnp.log(1.0 + jnp.exp(-jnp.abs(v)))
        r = lax.broadcasted_iota(jnp.int32, (tm, tm), 0)
        c = lax.broadcasted_iota(jnp.int32, (tm, tm), 1)
        tri = (r >= c).astype(F32)
        cum = jnp.dot(tri, logf, preferred_element_type=F32, precision=lax.Precision.HIGHEST) + carry[...]
        out_ref[...] = cum
        carry[...] = cum[tm - 8:tm, :][7:8, :]

    return pl.pallas_call(
        body, name="forget_cumsum", grid=(S // tm,),
        out_shape=jax.ShapeDtypeStruct((S, 128), F32),
        in_specs=[pl.BlockSpec((tm, 128), lambda i: (i, 0))],
        out_specs=pl.BlockSpec((tm, 128), lambda i: (i, 0)),
        scratch_shapes=[pltpu.VMEM((1, 128), F32)],
        compiler_params=_params(dimension_semantics=("arbitrary",)),
    )(f)


def _pair_select(is_a, va, vb):
    return jnp.where(is_a, va, vb)


def _split3(v):
    hi = v.astype(BF16)
    rest = v - hi.astype(F32)
    mid = rest.astype(BF16)
    lo = (rest - mid.astype(F32)).astype(BF16)
    return hi, mid, lo


def _attention_fwd(qkv, big_f):
    S = qkv.shape[0]
    T = min(T_ATT, S)
    n_t = S // T

    def body(q_ref, k_ref, v_ref, f_ref, o_ref, lse_ref, kaug_sc, vt_sc, m_sc, l_sc, acc_sc):
        hp = pl.program_id(0)
        i = pl.program_id(1)
        lane = lax.broadcasted_iota(jnp.int32, (1, 128), 1)
        sub = lax.broadcasted_iota(jnp.int32, (128, 1), 0)
        head_sel = (lane < HEAD_DIM, lane >= HEAD_DIM)
        head_sel_t = (sub < HEAD_DIM, sub >= HEAD_DIM)
        spare = (HEAD_DIM, 0)
        zero = jnp.zeros((), BF16)

        @pl.when(i == 0)
        def _():
            def prep(jt, carry):
                rows = pl.ds(pl.multiple_of(jt * T, T), T)
                k = k_ref[rows, :]
                ft = f_ref[rows, :]
                vt = v_ref[rows, :].astype(F32).T
                for h in range(2):
                    fh = jnp.sum(jnp.where(lane == 2 * hp + h, ft, 0.0), axis=1, keepdims=True)
                    hi, mid, lo = _split3(-fh)
                    b = spare[h]
                    bias = jnp.where(lane == b, hi, jnp.where(lane == b + 1, mid, jnp.where(lane == b + 2, lo, zero)))
                    kaug_sc[h, rows, :] = jnp.where(head_sel[h], k, bias)
                    vt_sc[h, jt] = jnp.where(head_sel_t[h], vt, 0.0).astype(BF16)
                return carry

            lax.fori_loop(0, n_t, prep, 0)

        q = q_ref[...]
        q_heads = []
        for h in range(2):
            ones = jnp.where((lane >= spare[h]) & (lane < spare[h] + 3), jnp.ones((), BF16), zero)
            q_heads.append(jnp.where(head_sel[h], q, ones))
        m_sc[...] = jnp.full((8, T), NEG, F32)
        l_sc[...] = jnp.zeros((8, T), F32)
        acc_sc[...] = jnp.zeros((128, T), F32)

        def step(j, masked):
            rows = pl.ds(pl.multiple_of(j * T, T), T)
            alphas, pvs = [], []
            for h in range(2):
                s_t = _dot_nt(kaug_sc[h, rows, :], q_heads[h])
                if masked:
                    rr = lax.broadcasted_iota(jnp.int32, (T, T), 0)
                    cc = lax.broadcasted_iota(jnp.int32, (T, T), 1)
                    s_t = jnp.where(rr <= cc, s_t, NEG)
                m_prev = m_sc[h:h + 1, :]
                m_new = jnp.maximum(m_prev, jnp.max(s_t, axis=0, keepdims=True))
                alpha = jnp.exp(m_prev - m_new)
                p_t = jnp.exp(s_t - m_new)
                l_sc[h:h + 1, :] = alpha * l_sc[h:h + 1, :] + jnp.sum(p_t, axis=0, keepdims=True)
                m_sc[h:h + 1, :] = m_new
                alphas.append(alpha)
                pvs.append(_dot(vt_sc[h, j], p_t.astype(BF16)))
            acc_sc[...] = acc_sc[...] * jnp.where(head_sel_t[0], alphas[0], alphas[1]) + (pvs[0] + pvs[1])

        def off_diagonal(j, carry):
            step(j, False)
            return carry

        lax.fori_loop(0, i, off_diagonal, 0)
        step(i, True)
        l = l_sc[...]
        o_ref[...] = (acc_sc[...] / jnp.where(head_sel_t[0], l[0:1, :], l[1:2, :])).T
        is_head = lax.broadcasted_iota(jnp.int32, (8, 1), 0) < 2
        lse_ref[...] = jnp.where(is_head, m_sc[...] + jnp.log(jnp.where(is_head, l, 1.0)), 0.0)

    return pl.pallas_call(
        body, name="attention_fwd", grid=(N_PAIR, n_t),
        out_shape=(jax.ShapeDtypeStruct((S, D_ATT), F32), jax.ShapeDtypeStruct((N_PAIR, n_t, 8, T), F32)),
        in_specs=[pl.BlockSpec((T, 128), lambda hp, i: (i, hp)),
                  pl.BlockSpec((S, 128), lambda hp, i: (0, N_PAIR + hp)),
                  pl.BlockSpec((S, 128), lambda hp, i: (0, 2 * N_PAIR + hp)),
                  pl.BlockSpec((S, 128), lambda hp, i: (0, 0))],
        out_specs=(pl.BlockSpec((T, 128), lambda hp, i: (i, hp)),
                   pl.BlockSpec((None, None, 8, T), lambda hp, i: (hp, i, 0, 0))),
        scratch_shapes=[pltpu.VMEM((2, S, 128), BF16), pltpu.VMEM((2, n_t, 128, T), BF16),
                        pltpu.VMEM((8, T), F32), pltpu.VMEM((8, T), F32), pltpu.VMEM((128, T), F32)],
        compiler_params=_params(dimension_semantics=("arbitrary", "arbitrary")),
    )(qkv, qkv, qkv, big_f)


def _attention_bwd(qkv, datt, att, lse, big_f):
    S = qkv.shape[0]
    T = min(T_ATT, S)
    n_t = S // T

    def body(q_ref, do_ref, o_ref, lse_ref, k_ref, v_ref, fk_ref,
             dq_ref, dk_ref, dv_ref, cs_ref, dfk_ref, dfq_ref, stat_sc, dqt_sc, qaug_sc, qt_sc, dot_sc):
        hp = pl.program_id(0)
        j = pl.program_id(1)
        lane = lax.broadcasted_iota(jnp.int32, (1, 128), 1)
        sub = lax.broadcasted_iota(jnp.int32, (128, 1), 0)
        head_sel = (lane < HEAD_DIM, lane >= HEAD_DIM)
        head_sel_t = (sub < HEAD_DIM, sub >= HEAD_DIM)
        spare = (HEAD_DIM, 0)
        zero = jnp.zeros((), BF16)
        one = jnp.ones((), BF16)

        def bias_lanes(first, pieces):
            hi, mid, lo = pieces
            return lambda rest: jnp.where(lane == first, hi, jnp.where(lane == first + 1, mid,
                                                                        jnp.where(lane == first + 2, lo, rest)))

        @pl.when(j == 0)
        def _():
            dqt_sc[...] = jnp.zeros_like(dqt_sc)
            cs_ref[...] = jnp.zeros_like(cs_ref)
            dfq_ref[...] = jnp.zeros_like(dfq_ref)

            def prep(i, carry):
                rows = pl.ds(pl.multiple_of(i * T, T), T)
                q = q_ref[rows, :]
                do = do_ref[rows, :]
                prod = o_ref[rows, :] * do.astype(F32)
                d_a = jnp.sum(jnp.where(head_sel[0], prod, 0.0), axis=1, keepdims=True)
                d_b = jnp.sum(jnp.where(head_sel[0], 0.0, prod), axis=1, keepdims=True)
                delta_t = jnp.where(head_sel[0], d_a, d_b).T
                stat_sc[i, 0:1, :] = delta_t[0:1, :]
                stat_sc[i, 1:2, :] = delta_t[HEAD_DIM:HEAD_DIM + 1, :]
                lse = lse_ref[i]
                lse_cols = jnp.where(head_sel_t[0], lse[0:1, :], lse[1:2, :]).T
                q_t = q.astype(F32).T
                do_t = do.astype(F32).T
                for h in range(2):
                    neg_lse = -lse_cols[:, h * HEAD_DIM:h * HEAD_DIM + 1]
                    ones = jnp.where((lane >= spare[h]) & (lane < spare[h] + 3), one, zero)
                    qaug_sc[h, rows, :] = jnp.where(head_sel[h], q, bias_lanes(spare[h] + 3, _split3(neg_lse))(ones))
                    qt_sc[h, i] = jnp.where(head_sel_t[h], q_t, 0.0).astype(BF16)
                    dot_sc[h, i] = jnp.where(head_sel_t[h], do_t, 0.0).astype(BF16)
                return carry

            lax.fori_loop(0, n_t, prep, 0)

        k = k_ref[...]
        v = v_ref[...]
        fk = fk_ref[...]
        kt = k.astype(F32).T
        heads = []
        for h in range(2):
            fkh = jnp.sum(jnp.where(lane == 2 * hp + h, fk, 0.0), axis=1, keepdims=True)
            ones = jnp.where((lane >= spare[h] + 3) & (lane < spare[h] + 6), one, zero)
            kaug = jnp.where(head_sel[h], k, bias_lanes(spare[h], _split3(-fkh))(ones))
            heads.append((kaug, jnp.where(head_sel[h], v, zero), jnp.where(head_sel_t[h], kt, 0.0).astype(BF16)))

        def step(i, masked, acc):
            dkt_acc, dvt_acc, dfa, dfb = acc
            rows = pl.ds(pl.multiple_of(i * T, T), T)
            do = do_ref[rows, :]
            stat = stat_sc[i]
            dqt = dqt_sc[i]
            dfs = [dfa, dfb]
            for h in range(2):
                kaug, vh, kth = heads[h]
                arg = _dot_nt(kaug, qaug_sc[h, rows, :])
                if masked:
                    rr = lax.broadcasted_iota(jnp.int32, (T, T), 0)
                    cc = lax.broadcasted_iota(jnp.int32, (T, T), 1)
                    arg = jnp.where(rr <= cc, arg, NEG)
                p_t = jnp.exp(arg)
                ds_t = p_t * (_dot_nt(vh, do) - stat[h:h + 1, :])
                ds_bf = ds_t.astype(BF16)
                dvt_acc = dvt_acc + _dot_nt(dot_sc[h, i], p_t.astype(BF16))
                dkt_acc = dkt_acc + _dot_nt(qt_sc[h, i], ds_bf)
                dqt = dqt + _dot(kth, ds_bf)
                dfs[h] = dfs[h] + jnp.sum(ds_t, axis=1, keepdims=True)
                dfq_ref[i, h:h + 1, :] += _colsum(ds_t)
            dqt_sc[i] = dqt
            return dkt_acc, dvt_acc, dfs[0], dfs[1]

        acc0 = (jnp.zeros((128, T), F32), jnp.zeros((128, T), F32), jnp.zeros((T, 1), F32), jnp.zeros((T, 1), F32))
        acc1 = step(j, True, acc0)
        dkt_acc, dvt_acc, dfa, dfb = lax.fori_loop(j + 1, n_t, lambda i, a: step(i, False, a), acc1)
        dk_acc = dkt_acc.T
        dv_acc = dvt_acc.T
        dk_ref[...] = dk_acc.astype(BF16)
        dv_ref[...] = dv_acc.astype(BF16)
        dfk_ref[...] = -jnp.where(lane == 0, dfa, jnp.where(lane == 1, dfb, 0.0))
        cs_ref[:, 128:256] = cs_ref[:, 128:256] + _colsum(dk_acc)
        cs_ref[:, 256:384] = cs_ref[:, 256:384] + _colsum(dv_acc)

        @pl.when(j == n_t - 1)
        def _():
            def finish(i, tot):
                dq = dqt_sc[i].T
                dq_ref[pl.ds(pl.multiple_of(i * T, T), T), :] = dq.astype(BF16)
                return tot + _colsum(dq)

            cs_ref[:, 0:128] = lax.fori_loop(0, n_t, finish, jnp.zeros((1, 128), F32))

    pair_rows = lambda hp, j: (hp, 0, 0)
    return pl.pallas_call(
        body, name="attention_bwd", grid=(N_PAIR, n_t),
        out_shape=(jax.ShapeDtypeStruct((S, D_ATT), BF16), jax.ShapeDtypeStruct((S, D_ATT), BF16),
                   jax.ShapeDtypeStruct((S, D_ATT), BF16), jax.ShapeDtypeStruct((N_PAIR, 1, 384), F32),
                   jax.ShapeDtypeStruct((N_PAIR, S, 128), F32),
                   jax.ShapeDtypeStruct((N_PAIR, n_t, 8, T), F32)),
        in_specs=[pl.BlockSpec((S, 128), lambda hp, j: (0, hp)),
                  pl.BlockSpec((S, 128), lambda hp, j: (0, hp)),
                  pl.BlockSpec((S, 128), lambda hp, j: (0, hp)),
                  pl.BlockSpec((None, n_t, 8, T), lambda hp, j: (hp, 0, 0, 0)),
                  pl.BlockSpec((T, 128), lambda hp, j: (j, N_PAIR + hp)),
                  pl.BlockSpec((T, 128), lambda hp, j: (j, 2 * N_PAIR + hp)),
                  pl.BlockSpec((T, 128), lambda hp, j: (j, 0))],
        out_specs=(pl.BlockSpec((S, 128), lambda hp, j: (0, hp)),
                   pl.BlockSpec((T, 128), lambda hp, j: (j, hp)),
                   pl.BlockSpec((T, 128), lambda hp, j: (j, hp)),
                   pl.BlockSpec((None, 1, 384), pair_rows),
                   pl.BlockSpec((None, T, 128), lambda hp, j: (hp, j, 0)),
                   pl.BlockSpec((None, n_t, 8, T), lambda hp, j: (hp, 0, 0, 0))),
        scratch_shapes=[pltpu.VMEM((n_t, 8, T), F32), pltpu.VMEM((n_t, 128, T), F32),
                        pltpu.VMEM((2, S, 128), BF16), pltpu.VMEM((2, n_t, 128, T), BF16),
                        pltpu.VMEM((2, n_t, 128, T), BF16)],
        compiler_params=_params(dimension_semantics=("arbitrary", "arbitrary")),
    )(qkv, datt, att, lse, qkv, qkv, big_f)


def _window_counts(first_row, n_rows, window):
    t = lax.broadcasted_iota(jnp.int32, (n_rows, 1), 0) + first_row
    return jnp.minimum((t + 1).astype(F32), float(window))


def _middle(x, tgt, att, g, p, gate, w_mix, b_mix, pool_scale, w_out, b_out, ln_g, ln_b):
    S = x.shape[0]
    tm = min(TM_MID, S)
    halo_blocks = tm // POOL_HALO

    def body(x_ref, t_ref, att_ref, g_ref, p_ref, ph_ref, gate_ref, wm_ref, bm_ref, ps_ref, wo_ref, bo_ref,
             lg_ref, lb_ref,
             dh_ref, datt_ref, dg_ref, dpl_ref, gwo_ref, gwm_ref, vec_ref, loss_ref):
        i = pl.program_id(0)

        @pl.when(i == 0)
        def _():
            gwo_ref[...] = jnp.zeros_like(gwo_ref)
            gwm_ref[...] = jnp.zeros_like(gwm_ref)
            vec_ref[...] = jnp.zeros_like(vec_ref)
            loss_ref[...] = jnp.zeros_like(loss_ref)

        pc = p_ref[...]
        halo = jnp.where(i > 0, ph_ref[...], 0.0)
        pe = jnp.concatenate([halo, pc], axis=0)
        pooled_parts = []
        for gi, w in enumerate(POOL_WINDOWS):
            cur = pe[:, gi * POOL_GROUP:(gi + 1) * POOL_GROUP]
            span = 1
            while span < w:
                cur = cur + pltpu.roll(cur, span, 0)
                span *= 2
            wsum = cur[POOL_HALO:, :]
            mean = wsum / _window_counts(i * tm, tm, w)
            pooled_parts.append(mean - pc[:, gi * POOL_GROUP:(gi + 1) * POOL_GROUP])
        pooled_bf =[v.astype(BF16) for v in pooled_parts]
        mixed = jnp.concatenate([_dot(pooled_bf[gi], wm_ref[gi]) for gi in range(4)], axis=1) + bm_ref[...]
        ps = ps_ref[...]
        pool_out = mixed * ps
        gv = g_ref[...]
        sig = _sigmoid(gv)
        silu = gv * sig
        att = att_ref[...]
        y = jnp.concatenate([att * silu[:, :D_ATT], pool_out * silu[:, D_ATT:]], axis=1)
        y_bf = y.astype(BF16)
        wo = wo_ref[...]
        yo = _dot(y_bf, wo) + bo_ref[...]
        gate = gate_ref[...]
        h = ALPHA * x_ref[...] + gate * yo
        mu = jnp.mean(h, axis=1, keepdims=True)
        hc = h - mu
        var = jnp.mean(hc * hc, axis=1, keepdims=True)
        rstd = lax.rsqrt(var + LN_EPS)
        yhat = hc * rstd
        lg = lg_ref[...]
        out = yhat * lg + lb_ref[...]
        err = out - t_ref[...]
        loss_ref[...] += 0.5 * jnp.sum(jnp.mean(err * err, axis=1, keepdims=True), axis=0, keepdims=True)

        dout = err * (1.0 / D)
        g_ln_b = _colsum(dout)
        g_ln_g = _colsum(dout * yhat)
        dyh = dout * lg
        dh = rstd * (dyh - jnp.mean(dyh, axis=1, keepdims=True)
                     - yhat * jnp.mean(dyh * yhat, axis=1, keepdims=True))
        dh_ref[...] = dh
        d_gate = _colsum(dh * yo)
        dyo = gate * dh
        g_b_out = _colsum(dyo)
        dyo_bf = dyo.astype(BF16)
        gwo_ref[...] += _dot_tn(y_bf, dyo_bf)
        dy = _dot_nt(dyo_bf, wo)
        dsilu = sig * (1.0 + gv * (1.0 - sig))
        dy_a = dy[:, :D_ATT]
        dy_p = dy[:, D_ATT:]
        datt_ref[...] = (dy_a * silu[:, :D_ATT]).astype(BF16)
        dpo = dy_p * silu[:, D_ATT:]
        dg = jnp.concatenate([dy_a * att * dsilu[:, :D_ATT], dy_p * pool_out * dsilu[:, D_ATT:]], axis=1)
        dg_ref[...] = dg.astype(BF16)
        g_dg = _colsum(dg)
        g_ps = _colsum(dpo * mixed)
        dmixed = dpo * ps
        g_bm = _colsum(dmixed)
        dmixed_bf = dmixed.astype(BF16)
        dpl = []
        for gi in range(4):
            dm = dmixed_bf[:, gi * POOL_GROUP:(gi + 1) * POOL_GROUP]
            gwm_ref[gi] += _dot_tn(pooled_bf[gi], dm)
            dpl.append(_dot_nt(dm, wm_ref[gi]))
        dpl_ref[...] = jnp.concatenate(dpl, axis=1)
        vec_ref[0:1, :] += g_ln_g
        vec_ref[1:2, :] += g_ln_b
        vec_ref[2:3, :] += d_gate
        vec_ref[3:4, :] += g_b_out
        vec_ref[4:5, :] += g_dg
        vec_ref[5:6, 0:D_POOL] += g_ps
        vec_ref[6:7, 0:D_POOL] += g_bm

    row = lambda w: pl.BlockSpec((tm, w), lambda i: (i, 0))
    full2 = lambda a: pl.BlockSpec(a.shape, lambda i: (0, 0))
    full3 = lambda a: pl.BlockSpec(a.shape, lambda i: (0, 0, 0))
    return pl.pallas_call(
        body, name="middle", grid=(S // tm,),
        out_shape=(jax.ShapeDtypeStruct((S, D), F32),
                   jax.ShapeDtypeStruct((S, D_ATT), BF16),
                   jax.ShapeDtypeStruct((S, D), BF16),
                   jax.ShapeDtypeStruct((S, D_POOL), F32),
                   jax.ShapeDtypeStruct((D, D), F32),
                   jax.ShapeDtypeStruct((4, POOL_GROUP, POOL_GROUP), F32),
                   jax.ShapeDtypeStruct((8, D), F32),
                   jax.ShapeDtypeStruct((1, 1), F32)),
        in_specs=[row(D), row(D), row(D_ATT), row(D), row(D_POOL),
                  pl.BlockSpec((POOL_HALO, D_POOL), lambda i: (jnp.maximum(i * halo_blocks - 1, 0), 0)),
                  full2(gate), full3(w_mix), full2(b_mix), full2(pool_scale), full2(w_out), full2(b_out),
                  full2(ln_g), full2(ln_b)],
        out_specs=(row(D), row(D_ATT), row(D), row(D_POOL),
                   pl.BlockSpec((D, D), lambda i: (0, 0)),
                   pl.BlockSpec((4, POOL_GROUP, POOL_GROUP), lambda i: (0, 0, 0)),
                   pl.BlockSpec((8, D), lambda i: (0, 0)),
                   pl.BlockSpec((1, 1), lambda i: (0, 0))),
        compiler_params=_params(dimension_semantics=("arbitrary",)),
    )(x, tgt, att, g, p, p, gate, w_mix, b_mix, pool_scale, w_out, b_out, ln_g, ln_b)


def _tail(dpl, dfk8, f):
    S = dpl.shape[0]
    tm = min(TM_TAIL, S)
    n_t = S // tm
    halo_blocks = tm // POOL_HALO
    last_halo = S // POOL_HALO - 1

    def body(d_ref, dn_ref, dfk_ref, f_ref, dp_ref, df_ref, cs_ref, carry):
        s = pl.program_id(0)
        i = n_t - 1 - s

        @pl.when(s == 0)
        def _():
            carry[...] = jnp.zeros_like(carry)
            cs_ref[...] = jnp.zeros_like(cs_ref)

        dc = d_ref[...]
        nxt = jnp.where(s > 0, dn_ref[...], 0.0)
        de = jnp.concatenate([dc, nxt], axis=0)
        n_e = tm + POOL_HALO
        parts = []
        for gi, w in enumerate(POOL_WINDOWS):
            cur = de[:, gi * POOL_GROUP:(gi + 1) * POOL_GROUP] / _window_counts(i * tm, n_e, w)
            span = 1
            while span < w:
                cur = cur + pltpu.roll(cur, n_e - span, 0)
                span *= 2
            parts.append(cur[:tm, :] - dc[:, gi * POOL_GROUP:(gi + 1) * POOL_GROUP])
        dp = jnp.concatenate(parts, axis=1)
        dp_ref[...] = dp.astype(BF16)
        cs_ref[0:1, :] += _colsum(dp)

        r = lax.broadcasted_iota(jnp.int32, (tm, tm), 0)
        c = lax.broadcasted_iota(jnp.int32, (tm, tm), 1)
        tri = (r <= c).astype(F32)
        dlogf = jnp.dot(tri, dfk_ref[...], preferred_element_type=F32, precision=lax.Precision.HIGHEST) + carry[...]
        carry[...] = dlogf[0:1, :]
        df = dlogf * _sigmoid(-f_ref[...])
        df_ref[...] = df.astype(BF16)
        cs_ref[1:2, 0:128] += _colsum(df)

    rev = lambda w: pl.BlockSpec((tm, w), lambda s: (n_t - 1 - s, 0))
    return pl.pallas_call(
        body, name="tail", grid=(n_t,),
        out_shape=(jax.ShapeDtypeStruct((S, D_POOL), BF16), jax.ShapeDtypeStruct((S, 128), BF16),
                   jax.ShapeDtypeStruct((8, D_POOL), F32)),
        in_specs=[rev(D_POOL),
                  pl.BlockSpec((POOL_HALO, D_POOL),
                               lambda s: (jnp.minimum((n_t - s) * halo_blocks, last_halo), 0)),
                  rev(128), rev(128)],
        out_specs=(rev(D_POOL), rev(128), pl.BlockSpec((8, D_POOL), lambda s: (0, 0))),
        scratch_shapes=[pltpu.VMEM((1, 128), F32)],
        compiler_params=_params(dimension_semantics=("arbitrary",)),
    )(dpl, dpl, dfk8, f)


PIECES = ((O_QKV, D_ATT), (O_QKV + D_ATT, D_ATT), (O_QKV + 2 * D_ATT, D_ATT), (O_F, 128), (O_P, D_POOL), (O_G, D))


def _grad_w_in(u, pieces):
    S = u.shape[0]
    tm = min(TM_GW, S)
    n_t = S // tm

    def body(u_ref, *rest):
        piece_refs, out_ref, acc, sem = rest[:6], rest[6], rest[7], rest[8]
        i = pl.program_id(0)

        @pl.when(i == 0)
        def _():
            acc[...] = jnp.zeros_like(acc)

        u_t = u_ref[...]
        for (off, w), ref in zip(PIECES, piece_refs):
            acc[:, off:off + w] += _dot_tn(u_t, ref[...])

        @pl.when(i == n_t - 1)
        def _():
            cp = pltpu.make_async_copy(acc, out_ref, sem)
            cp.start()
            cp.wait()

    return pl.pallas_call(
        body, name="grad_w_in", grid=(n_t,),
        out_shape=jax.ShapeDtypeStruct((D, D_PAD), F32),
        in_specs=[pl.BlockSpec((tm, D), lambda i: (i, 0))]
        + [pl.BlockSpec((tm, w), lambda i: (i, 0)) for _, w in PIECES],
        out_specs=pl.BlockSpec(memory_space=pl.ANY),
        scratch_shapes=[pltpu.VMEM((D, D_PAD), F32), pltpu.SemaphoreType.DMA],
        compiler_params=_params(dimension_semantics=("arbitrary",)),
    )(u, *pieces)


def _grad_x(pieces, w_pad, dh, x, scale):
    S = x.shape[0]
    tm = min(TM_DU, S)

    def body(*refs):
        piece_refs = refs[:6]
        w_ref, dh_ref, x_ref, sc_ref, gx_ref, vec_ref = refs[6:]

        @pl.when(pl.program_id(0) == 0)
        def _():
            vec_ref[...] = jnp.zeros_like(vec_ref)

        du = jnp.zeros((tm, D), F32)
        for (off, w), ref in zip(PIECES, piece_refs):
            du = du + _dot_nt(ref[...], w_ref[:, off:off + w])
        xv = x_ref[...]
        gx_ref[...] = ALPHA * dh_ref[...] + du * (1.0 + sc_ref[...])
        vec_ref[0:1, :] += _colsum(du)
        vec_ref[1:2, :] += _colsum(du * xv)

    row = lambda w: pl.BlockSpec((tm, w), lambda i: (i, 0))
    return pl.pallas_call(
        body, name="grad_x", grid=(S // tm,),
        out_shape=(jax.ShapeDtypeStruct((S, D), F32), jax.ShapeDtypeStruct((8, D), F32)),
        in_specs=[row(w) for _, w in PIECES]
        + [pl.BlockSpec(w_pad.shape, lambda i: (0, 0)), row(D), row(D), pl.BlockSpec((1, D), lambda i: (0, 0))],
        out_specs=(row(D), pl.BlockSpec((8, D), lambda i: (0, 0))),
        compiler_params=_params(dimension_semantics=("arbitrary",)),
    )(*pieces, w_pad, dh, x, scale)


def _grad_ada(c_all, dada_all, dada_cols):
    def body(c_ref, dall_ref, dcol_ref, gw_ref, gb_ref):
        rows = lax.broadcasted_iota(jnp.int32, (8, 1), 0)
        cm = jnp.zeros((8, D), F32)
        dm = jnp.zeros((8, 3 * D), F32)
        for r in range(8):
            cm = jnp.where(rows == r, c_ref[r], cm)
            dm = jnp.where(rows == r, dall_ref[r], dm)
        act = cm * _sigmoid(cm)
        pad = jnp.zeros((8, D), F32)
        lhs = jnp.concatenate([act, pad], axis=0).astype(BF16)
        rhs = jnp.concatenate([dcol_ref[...], jnp.zeros((8, SHARD_ADA), F32)], axis=0).astype(BF16)
        gw_ref[...] = _dot_tn(lhs, rhs)
        gb_ref[...] = _colsum(dm)

    vm = pl.BlockSpec(memory_space=pltpu.VMEM)
    return pl.pallas_call(
        body, name="grad_ada",
        out_shape=(jax.ShapeDtypeStruct((D, SHARD_ADA), F32), jax.ShapeDtypeStruct((1, 3 * D), F32)),
        in_specs=[vm, vm, vm], out_specs=(vm, vm),
        compiler_params=_params(),
    )(c_all, dada_all, dada_cols)


def _adamw_math(w, g, m, v):
    m = ADAM_B1 * m + (1.0 - ADAM_B1) * g
    v = ADAM_B2 * v + (1.0 - ADAM_B2) * (g * g)
    m_hat = m / (1.0 - ADAM_B1 ** ADAM_STEP)
    v_hat = v / (1.0 - ADAM_B2 ** ADAM_STEP)
    delta = -ADAM_LR * (m_hat / (jnp.sqrt(v_hat) + ADAM_EPS) + ADAM_WD * w)
    return delta, m, v


def _adamw(groups, n_steps):
    n = len(groups)

    def body(*refs):
        ins, outs = refs[:4 * n], refs[4 * n:]
        for t in range(n):
            w, g, m, v = (r[...] for r in ins[4 * t:4 * t + 4])
            d, m2, v2 = _adamw_math(w, g, m, v)
            outs[3 * t][...] = d
            outs[3 * t + 1][...] = m2
            outs[3 * t + 2][...] = v2

    in_specs, out_specs, out_shape, args = [], [], [], []
    for (w, g, m, v) in groups:
        r, c = w.shape
        spec = pl.BlockSpec((r // n_steps, c), lambda i: (i, 0))
        in_specs += [spec] * 4
        out_specs += [spec] * 3
        out_shape += [jax.ShapeDtypeStruct((r, c), F32)] * 3
        args += [w, g, m, v]
    return pl.pallas_call(
        body, name="adamw_%d" % n, grid=(n_steps,),
        out_shape=tuple(out_shape), in_specs=in_specs, out_specs=tuple(out_specs),
        compiler_params=_params(dimension_semantics=("arbitrary",)),
    )(*args)


def _pack_small(parts):
    rows = []
    used = 0
    for name, (first, n_rows) in SMALL_SEGS.items():
        if first > used:
            rows.append(jnp.zeros((first - used, 128), F32))
        flat = parts[name].reshape(-1)
        flat = jnp.pad(flat, (0, n_rows * 128 - flat.shape[0]))
        rows.append(flat.reshape(n_rows, 128))
        used = first + n_rows
    rows.append(jnp.zeros((SMALL_ROWS - used, 128), F32))
    return jnp.concatenate(rows, axis=0)


def _unpack_small(buf, name, shape):
    first, n_rows = SMALL_SEGS[name]
    n = int(np.prod(shape))
    return buf[first:first + n_rows].reshape(-1)[:n].reshape(shape)


def _pad_in(v):
    r = v.shape[0]
    z = jnp.zeros((r, O_P - O_F - N_HEADS), v.dtype)
    return jnp.concatenate([v[:, :3 * D_ATT + N_HEADS], z, v[:, 3 * D_ATT + N_HEADS:]], axis=1)


def _unpad_in(v):
    return jnp.concatenate([v[:, :O_F + N_HEADS], v[:, O_P:]], axis=1)


def _shards_in(v):
    gap = O_P - (O_F + N_HEADS)
    parts = []
    for a in range(N_CHIPS):
        lo, hi = a * SHARD_IN, (a + 1) * SHARD_IN
        cut = O_F + N_HEADS
        if hi <= cut:
            parts.append(v[:, lo:hi])
        elif lo >= cut:
            parts.append(v[:, lo + gap:hi + gap])
        else:
            parts.append(jnp.concatenate([v[:, lo:cut], v[:, cut + gap:hi + gap]], axis=1))
    return jnp.stack(parts, axis=0)


def kernel(x, c, w_ada, b_ada, w_in, b_in, w_pool_mix, b_pool_mix, pool_scale, w_out, b_out, ln_g, ln_b, loss_target, m_w_ada, m_b_ada, m_w_in, m_b_in, m_w_pool_mix, m_b_pool_mix, m_pool_scale, m_w_out, m_b_out, m_ln_g, m_ln_b, v_w_ada, v_b_ada, v_w_in, v_b_in, v_w_pool_mix, v_b_pool_mix, v_pool_scale, v_w_out, v_b_out, v_ln_g, v_ln_b):
    S = x.shape[1]
    T = min(T_ATT, S)
    n_t = S // T
    chip = 2 * lax.axis_index("x") + lax.axis_index("y")
    x2 = x[0]
    tgt = loss_target[0]
    q_scale = jnp.concatenate([jnp.full((1, D_ATT), Q_SCALE, F32), jnp.ones((1, D_PAD - D_ATT), F32)], axis=1)

    c_all, ada4 = _ada_exchange(c, w_ada[0], b_ada.reshape(4, 1, SHARD_ADA))
    ada = ada4[:, 0, :].reshape(1, 3 * D)
    shift, scale, gate = ada[:, :D], ada[:, D:2 * D], ada[:, 2 * D:]
    w_in_all, w_out_all = _gather_weights(w_in[0].astype(BF16), w_out[0].astype(BF16))
    w_in_full = jnp.transpose(w_in_all, (1, 0, 2)).reshape(D, D_IN)
    w_pad = _pad_in(w_in_full) * q_scale.astype(BF16)
    b_pad = _pad_in(b_in) * q_scale
    w_out_full = w_out_all.reshape(D, D)
    w_mix_bf = w_pool_mix[0].astype(BF16)

    u, qkv, f, p, g = _in_proj(x2, shift, scale, w_pad, b_pad)
    big_f = _forget_cumsum(f)
    att, lse = _attention_fwd(qkv, big_f)

    dh, datt, dg, dpl, gw_out, gw_mix, vec, loss_part = _middle(
        x2, tgt, att, g, p, gate, w_mix_bf, b_pool_mix.reshape(1, D_POOL), pool_scale, w_out_full, b_out, ln_g, ln_b)
    dq, dk, dv, cs_att, dfk, dfq = _attention_bwd(qkv, datt, att, lse, big_f)
    dfk8 = jnp.transpose(dfk[:, :, 0:2], (1, 0, 2)).reshape(S, N_HEADS)
    dfk8 = dfk8 + jnp.transpose(dfq[:, :, 0:2, :], (1, 3, 0, 2)).reshape(S, N_HEADS)
    dfk8 = jnp.pad(dfk8, ((0, 0), (0, 128 - N_HEADS)))
    dp, df, cs_tail = _tail(dpl, dfk8, f)
    pieces = (dq, dk, dv, df, dp, dg)
    gw_pad = _grad_w_in(u, pieces)
    grad_x, vec_x = _grad_x(pieces, w_pad, dh, x2, scale)

    cs_qkv = jnp.transpose(cs_att.reshape(N_PAIR, 3, 128), (1, 0, 2)).reshape(1, 3 * D_ATT)
    gb_pad = jnp.concatenate([cs_qkv, cs_tail[1:2, 0:128], cs_tail[0:1, :], vec[4:5, :]], axis=1) * q_scale
    dada = jnp.concatenate([vec_x[0:1, :], vec_x[1:2, :], vec[2:3, :]], axis=1)
    small = _pack_small({
        "b_in": gb_pad, "w_pool_mix": gw_mix, "b_pool_mix": vec[6:7, :D_POOL], "pool_scale": vec[5:6, :D_POOL],
        "b_out": vec[3:4, :], "ln_g": vec[0:1, :], "ln_b": vec[1:2, :], "b_ada": jnp.zeros((1, 3 * D), F32)})

    g_w_in = _reduce_scatter(_shards_in(gw_pad), _shards_in(q_scale), "reduce_w_in")
    g_w_out = _reduce_scatter(gw_out.reshape(N_CHIPS, SHARD_OUT, D), jnp.ones((N_CHIPS, 1, D), F32), "reduce_w_out")
    small_sum, dada_all = _all_reduce_small(small[:SMALL_REDUCED_ROWS], dada)
    dada_cols = lax.dynamic_slice(dada_all[:, 0, :], (0, chip * SHARD_ADA), (8, SHARD_ADA))
    g_w_ada, g_b_ada = _grad_ada(c_all, dada_all, dada_cols)
    loss = lax.psum(loss_part[0, 0], ("x", "y", "c"))

    grads_small = jnp.concatenate([small_sum, g_b_ada.reshape(24, 128)], axis=0)
    small_w = {"b_in": _pad_in(b_in), "w_pool_mix": w_pool_mix, "b_pool_mix": b_pool_mix, "pool_scale": pool_scale,
               "b_out": b_out, "ln_g": ln_g, "ln_b": ln_b, "b_ada": b_ada}
    small_m = {"b_in": _pad_in(m_b_in), "w_pool_mix": m_w_pool_mix, "b_pool_mix": m_b_pool_mix,
               "pool_scale": m_pool_scale, "b_out": m_b_out, "ln_g": m_ln_g, "ln_b": m_ln_b, "b_ada": m_b_ada}
    small_v = {"b_in": _pad_in(v_b_in), "w_pool_mix": v_w_pool_mix, "b_pool_mix": v_b_pool_mix,
               "pool_scale": v_pool_scale, "b_out": v_b_out, "ln_g": v_ln_g, "ln_b": v_ln_b, "b_ada": v_b_ada}
    big = _adamw([(w_ada[0], g_w_ada, m_w_ada[0], v_w_ada[0]),
                  (w_in[0], g_w_in, m_w_in[0], v_w_in[0]),
                  (w_out[0], g_w_out, m_w_out[0], v_w_out[0])], 8)
    sm = _adamw([(_pack_small(small_w), grads_small, _pack_small(small_m), _pack_small(small_v))], 1)

    names = ["w_ada", "b_ada", "w_in", "b_in", "w_pool_mix", "b_pool_mix", "pool_scale", "w_out", "b_out",
             "ln_g", "ln_b"]
    shapes = {"b_ada": (1, 3 * D), "b_in": (1, D_PAD), "w_pool_mix": (1, 4, POOL_GROUP, POOL_GROUP),
              "b_pool_mix": (1, 4, POOL_GROUP), "pool_scale": (1, D_POOL), "b_out": (1, D), "ln_g": (1, D),
              "ln_b": (1, D)}
    big_idx = {"w_ada": 0, "w_in": 1, "w_out": 2}

    def leaf(kind, name):
        if name in big_idx:
            if kind == 0:
                return (g_w_ada, g_w_in, g_w_out)[big_idx[name]][None]
            return big[3 * big_idx[name] + kind - 1][None]
        buf = grads_small if kind == 0 else sm[kind - 1]
        val = _unpack_small(buf, name, shapes[name])
        if name == "b_in":
            val = _unpad_in(val)
        return val

    outs = [loss, grad_x[None]]
    for kind in range(4):
        outs += [leaf(kind, n) for n in names]
    return tuple(outs)
```

```python
import functools

import numpy as np
import jax
import jax.numpy as jnp
from jax import lax
from jax.experimental import pallas as pl
from jax.experimental.pallas import tpu as pltpu

F32 = jnp.float32
BF16 = jnp.bfloat16
MESH = pl.DeviceIdType.MESH

D = 1024
D_ATT = 512
D_POOL = 512
N_HEADS = 8
HEAD_DIM = 64
N_PAIR = N_HEADS // 2
POOL_WINDOWS = (2, 4, 8, 16)
POOL_GROUP = 128
POOL_HALO = 16
LN_EPS = 1e-5
ALPHA = 2.0 ** 0.25
D_IN = 3 * D_ATT + N_HEADS + D_POOL + D_ATT + D_POOL
N_CHIPS = 4
SHARD_IN = D_IN // N_CHIPS
SHARD_ADA = 3 * D // N_CHIPS
SHARD_OUT = D // N_CHIPS

O_QKV, O_F, O_P, O_G, D_PAD = 0, 1536, 1664, 2176, 3200
Q_SCALE = HEAD_DIM ** -0.5

ADAM_LR, ADAM_B1, ADAM_B2, ADAM_EPS, ADAM_WD, ADAM_STEP = 0.001, 0.9, 0.999, 1e-08, 0.01, 10

NEG = -1e30

VMEM_LIMIT = 56 * 1024 * 1024

TM_PROJ = 512
T_ATT = 512
ATT_CHUNK = 32
TM_MID = 256
TM_TAIL = 512
TM_GW = 512
TM_DU = 512

REL7 = [(0, 0, 1), (0, 1, 0), (0, 1, 1), (1, 0, 0), (1, 0, 1), (1, 1, 0), (1, 1, 1)]
REL3 = [(0, 1), (1, 0), (1, 1)]

SMALL_SEGS = {}
_row = 0
for _name, _n in (("b_in", 3200), ("w_pool_mix", 65536), ("b_pool_mix", 512), ("pool_scale", 512),
                  ("b_out", 1024), ("ln_g", 1024), ("ln_b", 1024)):
    _rows = -(-_n // 1024) * 8
    SMALL_SEGS[_name] = (_row, _rows)
    _row += _rows
SMALL_REDUCED_ROWS = -(-_row // 16) * 16
SMALL_SEGS["b_ada"] = (SMALL_REDUCED_ROWS, 24)
SMALL_ROWS = SMALL_REDUCED_ROWS + 24


def _params(**kw):
    return pltpu.CompilerParams(vmem_limit_bytes=VMEM_LIMIT, **kw)


def _flip(v, d):
    return v if d == 0 else 1 - v


def _dot(a, b):
    return jnp.dot(a, b, preferred_element_type=F32)


def _dot_nt(a, b):
    return lax.dot_general(a, b, (((1,), (1,)), ((), ())), preferred_element_type=F32)


def _dot_tn(a, b):
    return lax.dot_general(a, b, (((0,), (0,)), ((), ())), preferred_element_type=F32)


def _sigmoid(v):
    return 1.0 / (1.0 + jnp.exp(-v))


def _colsum(v):
    return jnp.sum(v, axis=0, keepdims=True)


def _ada_exchange(c, w_ada, b_ada4):
    def body(c_ref, w_ref, b_ref, call_ref, ada_ref, cslab, sbuf, rbuf, cs_sem, cr_sem, as_sem, ar_sem):
        x, y, cc = lax.axis_index("x"), lax.axis_index("y"), lax.axis_index("c")
        me = 4 * x + 2 * y + cc
        chip = 2 * x + y
        cslab[...] = jnp.broadcast_to(c_ref[...], (8, D))
        call_ref[me] = cslab[...]
        gathers = []
        for k, (dx, dy, dc) in enumerate(REL7):
            cp = pltpu.make_async_remote_copy(
                src_ref=cslab, dst_ref=call_ref.at[me], send_sem=cs_sem.at[k], recv_sem=cr_sem.at[k],
                device_id=(_flip(x, dx), _flip(y, dy), _flip(cc, dc)), device_id_type=MESH)
            cp.start()
            gathers.append(cp)
        for cp in gathers:
            cp.wait()
        rows = lax.broadcasted_iota(jnp.int32, (8, 1), 0)
        mat = jnp.zeros((8, D), F32)
        for r in range(8):
            mat = jnp.where(rows == r, call_ref[r], mat)
        act = (mat * _sigmoid(mat)).astype(BF16)
        part = _dot(act, w_ref[...].astype(BF16))
        sends = []
        for k, (dx, dy) in enumerate(REL3):
            px, py = _flip(x, dx), _flip(y, dy)
            r = 4 * px + 2 * py + cc
            piece = _colsum(jnp.where(rows == r, part, 0.0))
            sbuf[k] = jnp.broadcast_to(piece, (8, SHARD_ADA))
            cp = pltpu.make_async_remote_copy(
                src_ref=sbuf.at[k], dst_ref=rbuf.at[k], send_sem=as_sem.at[k], recv_sem=ar_sem.at[k],
                device_id=(px, py, cc), device_id_type=MESH)
            cp.start()
            sends.append(cp)
        own = _colsum(jnp.where(rows == me, part, 0.0))
        ada_ref[chip] = jnp.broadcast_to(own, (8, SHARD_ADA)) + b_ref[chip]
        for k, (dx, dy) in enumerate(REL3):
            sends[k].wait()
            a = 2 * _flip(x, dx) + _flip(y, dy)
            ada_ref[a] = rbuf[k] + b_ref[a]

    vm = pl.BlockSpec(memory_space=pltpu.VMEM)
    return pl.pallas_call(
        body, name="ada_exchange",
        out_shape=(jax.ShapeDtypeStruct((8, 8, D), F32), jax.ShapeDtypeStruct((4, 8, SHARD_ADA), F32)),
        in_specs=[vm, vm, vm], out_specs=(vm, vm),
        scratch_shapes=[pltpu.VMEM((8, D), F32), pltpu.VMEM((3, 8, SHARD_ADA), F32),
                        pltpu.VMEM((3, 8, SHARD_ADA), F32),
                        pltpu.SemaphoreType.DMA((7,)), pltpu.SemaphoreType.DMA((7,)),
                        pltpu.SemaphoreType.DMA((3,)), pltpu.SemaphoreType.DMA((3,))],
        compiler_params=_params(),
    )(c, w_ada, b_ada4)


def _gather_weights(w_in_sh, w_out_sh):
    def body(win_ref, wout_ref, win_all, wout_all, own_sem, s_sem, r_sem, fs_sem, fr_sem):
        x, y, cc = lax.axis_index("x"), lax.axis_index("y"), lax.axis_index("c")
        chip = 2 * x + y
        sib = (x, y, 1 - cc)
        srcs = (win_ref, wout_ref)
        dsts = (win_all, wout_all)
        halves = (D // 2, SHARD_OUT // 2)

        def rows(t, which):
            h = halves[t]
            return pl.ds(pl.multiple_of(which * h, h), h)

        own = [pltpu.make_async_copy(srcs[t], dsts[t].at[chip], own_sem.at[t]) for t in range(2)]
        for cp in own:
            cp.start()
        first = []
        for k, (dx, dy) in enumerate(REL3):
            peer = (_flip(x, dx), _flip(y, dy), cc)
            for t in range(2):
                cp = pltpu.make_async_remote_copy(
                    src_ref=srcs[t].at[rows(t, cc), :], dst_ref=dsts[t].at[chip, rows(t, cc), :],
                    send_sem=s_sem.at[2 * k + t], recv_sem=r_sem.at[2 * k + t],
                    device_id=peer, device_id_type=MESH)
                cp.start()
                first.append(cp)
        passed = []
        for k, (dx, dy) in enumerate(REL3):
            a = 2 * _flip(x, dx) + _flip(y, dy)
            for t in range(2):
                landed = dsts[t].at[a, rows(t, cc), :]
                pltpu.make_async_remote_copy(
                    src_ref=landed, dst_ref=landed, send_sem=s_sem.at[2 * k + t], recv_sem=r_sem.at[2 * k + t],
                    device_id=sib, device_id_type=MESH).wait_recv()
                cp = pltpu.make_async_remote_copy(
                    src_ref=landed, dst_ref=landed, send_sem=fs_sem.at[2 * k + t], recv_sem=fr_sem.at[2 * k + t],
                    device_id=sib, device_id_type=MESH)
                cp.start()
                passed.append(cp)
        for k, (dx, dy) in enumerate(REL3):
            a = 2 * _flip(x, dx) + _flip(y, dy)
            for t in range(2):
                other = dsts[t].at[a, rows(t, 1 - cc), :]
                pltpu.make_async_remote_copy(
                    src_ref=other, dst_ref=other, send_sem=fs_sem.at[2 * k + t], recv_sem=fr_sem.at[2 * k + t],
                    device_id=sib, device_id_type=MESH).wait_recv()
        for cp in first + passed:
            cp.wait_send()
        for cp in own:
            cp.wait()

    vm = pl.BlockSpec(memory_space=pltpu.VMEM)
    return pl.pallas_call(
        body, name="gather_weights",
        out_shape=(jax.ShapeDtypeStruct((N_CHIPS, D, SHARD_IN), BF16),
                   jax.ShapeDtypeStruct((N_CHIPS, SHARD_OUT, D), BF16)),
        in_specs=[vm, vm], out_specs=(vm, vm),
        scratch_shapes=[pltpu.SemaphoreType.DMA((2,)), pltpu.SemaphoreType.DMA((6,)),
                        pltpu.SemaphoreType.DMA((6,)), pltpu.SemaphoreType.DMA((6,)),
                        pltpu.SemaphoreType.DMA((6,))],
        compiler_params=_params(),
    )(w_in_sh, w_out_sh)


def _reduce_scatter(g4, scale4, name):
    _, R, C = g4.shape
    RH = R // 2

    def body(g_ref, sc_ref, out_ref, sib_buf, send_buf, ici_buf, sem1, sem2s, sem2r, sem3):
        x, y, cc = lax.axis_index("x"), lax.axis_index("y"), lax.axis_index("c")
        chip = 2 * x + y
        sib = (x, y, 1 - cc)
        mine = pl.ds(pl.multiple_of(cc * RH, RH), RH)
        theirs = pl.ds(pl.multiple_of((1 - cc) * RH, RH), RH)
        cp1 = pltpu.make_async_remote_copy(
            src_ref=g_ref.at[:, theirs, :], dst_ref=sib_buf, send_sem=sem1.at[0], recv_sem=sem1.at[1],
            device_id=sib, device_id_type=MESH)
        cp1.start()
        cp1.wait()
        for a in range(N_CHIPS):
            both = g_ref[a, mine, :] + sib_buf[a]
            sib_buf[a] = both
            send_buf[a] = both.astype(BF16)
        sends = []
        for k, (dx, dy) in enumerate(REL3):
            px, py = _flip(x, dx), _flip(y, dy)
            cp = pltpu.make_async_remote_copy(
                src_ref=send_buf.at[2 * px + py], dst_ref=ici_buf.at[chip],
                send_sem=sem2s.at[k], recv_sem=sem2r.at[k], device_id=(px, py, cc), device_id_type=MESH)
            cp.start()
            sends.append(cp)
        ici_buf[chip] = send_buf[chip]
        for cp in sends:
            cp.wait()
        own = sib_buf[chip]
        parts = [jnp.where(chip == a, own, ici_buf[a].astype(F32)) for a in range(N_CHIPS)]
        out_ref[mine, :] = ((parts[0] + parts[1]) + (parts[2] + parts[3])) * sc_ref[chip]
        cp3 = pltpu.make_async_remote_copy(
            src_ref=out_ref.at[mine, :], dst_ref=out_ref.at[mine, :], send_sem=sem3.at[0], recv_sem=sem3.at[1],
            device_id=sib, device_id_type=MESH)
        cp3.start()
        cp3.wait()

    vm = pl.BlockSpec(memory_space=pltpu.VMEM)
    return pl.pallas_call(
        body, name=name,
        out_shape=jax.ShapeDtypeStruct((R, C), F32),
        in_specs=[vm, vm], out_specs=vm,
        scratch_shapes=[pltpu.VMEM((N_CHIPS, RH, C), F32), pltpu.VMEM((N_CHIPS, RH, C), BF16),
                        pltpu.VMEM((N_CHIPS, RH, C), BF16),
                        pltpu.SemaphoreType.DMA((2,)), pltpu.SemaphoreType.DMA((3,)),
                        pltpu.SemaphoreType.DMA((3,)), pltpu.SemaphoreType.DMA((2,))],
        compiler_params=_params(),
    )(g4, scale4)


def _all_reduce_small(g, dada):
    R = g.shape[0]
    RH = R // 2
    W = dada.shape[1]

    def body(g_ref, d_ref, out_ref, dall_ref, dslab, sib_buf, ici_buf, ds_sem, dr_sem, sem1, sem2s, sem2r, sem3):
        x, y, cc = lax.axis_index("x"), lax.axis_index("y"), lax.axis_index("c")
        me = 4 * x + 2 * y + cc
        chip = 2 * x + y
        sib = (x, y, 1 - cc)
        dslab[...] = jnp.broadcast_to(d_ref[...], (8, W))
        dall_ref[me] = dslab[...]
        gathers = []
        for k, (dx, dy, dc) in enumerate(REL7):
            cp = pltpu.make_async_remote_copy(
                src_ref=dslab, dst_ref=dall_ref.at[me], send_sem=ds_sem.at[k], recv_sem=dr_sem.at[k],
                device_id=(_flip(x, dx), _flip(y, dy), _flip(cc, dc)), device_id_type=MESH)
            cp.start()
            gathers.append(cp)
        mine = pl.ds(pl.multiple_of(cc * RH, 8), RH)
        theirs = pl.ds(pl.multiple_of((1 - cc) * RH, 8), RH)
        cp1 = pltpu.make_async_remote_copy(
            src_ref=g_ref.at[theirs, :], dst_ref=sib_buf, send_sem=sem1.at[0], recv_sem=sem1.at[1],
            device_id=sib, device_id_type=MESH)
        cp1.start()
        cp1.wait()
        sib_buf[...] = g_ref[mine, :] + sib_buf[...]
        sends = []
        for k, (dx, dy) in enumerate(REL3):
            px, py = _flip(x, dx), _flip(y, dy)
            cp = pltpu.make_async_remote_copy(
                src_ref=sib_buf, dst_ref=ici_buf.at[chip],
                send_sem=sem2s.at[k], recv_sem=sem2r.at[k], device_id=(px, py, cc), device_id_type=MESH)
            cp.start()
            sends.append(cp)
        ici_buf[chip] = sib_buf[...]
        for cp in sends:
            cp.wait()
        out_ref[mine, :] = (ici_buf[0] + ici_buf[1]) + (ici_buf[2] + ici_buf[3])
        cp3 = pltpu.make_async_remote_copy(
            src_ref=out_ref.at[mine, :], dst_ref=out_ref.at[mine, :], send_sem=sem3.at[0], recv_sem=sem3.at[1],
            device_id=sib, device_id_type=MESH)
        cp3.start()
        cp3.wait()
        for cp in gathers:
            cp.wait()

    vm = pl.BlockSpec(memory_space=pltpu.VMEM)
    return pl.pallas_call(
        body, name="all_reduce_small",
        out_shape=(jax.ShapeDtypeStruct((R, 128), F32), jax.ShapeDtypeStruct((8, 8, W), F32)),
        in_specs=[vm, vm], out_specs=(vm, vm),
        scratch_shapes=[pltpu.VMEM((8, W), F32), pltpu.VMEM((RH, 128), F32), pltpu.VMEM((N_CHIPS, RH, 128), F32),
                        pltpu.SemaphoreType.DMA((7,)), pltpu.SemaphoreType.DMA((7,)),
                        pltpu.SemaphoreType.DMA((2,)), pltpu.SemaphoreType.DMA((3,)),
                        pltpu.SemaphoreType.DMA((3,)), pltpu.SemaphoreType.DMA((2,))],
        compiler_params=_params(),
    )(g, dada)


def _in_proj(x, shift, scale, w_pad, b_pad):
    S = x.shape[0]
    tm = min(TM_PROJ, S)

    def body(x_ref, sh_ref, sc_ref, w_ref, b_ref, u_ref, qkv_ref, f_ref, p_ref, g_ref):
        u = (x_ref[...] * (1.0 + sc_ref[...]) + sh_ref[...]).astype(BF16)
        u_ref[...] = u
        qkv_ref[...] = (_dot(u, w_ref[:, O_QKV:O_F]) + b_ref[:, O_QKV:O_F]).astype(BF16)
        f_ref[...] = _dot(u, w_ref[:, O_F:O_P]) + b_ref[:, O_F:O_P]
        p_ref[...] = _dot(u, w_ref[:, O_P:O_G]) + b_ref[:, O_P:O_G]
        g_ref[...] = _dot(u, w_ref[:, O_G:D_PAD]) + b_ref[:, O_G:D_PAD]

    row = lambda w: pl.BlockSpec((tm, w), lambda i: (i, 0))
    full = lambda a: pl.BlockSpec(a.shape, lambda i: (0, 0))
    return pl.pallas_call(
        body, name="in_proj", grid=(S // tm,),
        out_shape=(jax.ShapeDtypeStruct((S, D), BF16), jax.ShapeDtypeStruct((S, 3 * D_ATT), BF16),
                   jax.ShapeDtypeStruct((S, 128), F32), jax.ShapeDtypeStruct((S, D_POOL), F32),
                   jax.ShapeDtypeStruct((S, D), F32)),
        in_specs=[row(D), full(shift), full(scale), full(w_pad), full(b_pad)],
        out_specs=(row(D), row(3 * D_ATT), row(128), row(D_POOL), row(D)),
        compiler_params=_params(dimension_semantics=("arbitrary",)),
    )(x, shift, scale, w_pad, b_pad)


def _forget_cumsum(f):
    S = f.shape[0]
    tm = min(T_ATT, S)

    def body(f_ref, out_ref, carry):
        @pl.when(pl.program_id(0) == 0)
        def _():
            carry[...] = jnp.zeros_like(carry)
        v = f_ref[...]
        logf = jnp.minimum(v, 0.0) - jnp.log(1.0 + jnp.exp(-jnp.abs(v)))
        r = lax.broadcasted_iota(jnp.int32, (tm, tm), 0)
        c = lax.broadcasted_iota(jnp.int32, (tm, tm), 1)
        tri = (r >= c).astype(F32)
        cum = jnp.dot(tri, logf, preferred_element_type=F32, precision=lax.Precision.HIGHEST) + carry[...]
        out_ref[...] = cum
        carry[...] = cum[tm - 8:tm, :][7:8, :]

    return pl.pallas_call(
        body, name="forget_cumsum", grid=(S // tm,),
        out_shape=jax.ShapeDtypeStruct((S, 128), F32),
        in_specs=[pl.BlockSpec((tm, 128), lambda i: (i, 0))],
        out_specs=pl.BlockSpec((tm, 128), lambda i: (i, 0)),
        scratch_shapes=[pltpu.VMEM((1, 128), F32)],
        compiler_params=_params(dimension_semantics=("arbitrary",)),
    )(f)


def _pair_select(is_a, va, vb):
    return jnp.where(is_a, va, vb)


def _split3(v):
    hi = v.astype(BF16)
    rest = v - hi.astype(F32)
    mid = rest.astype(BF16)
    lo = (rest - mid.astype(F32)).astype(BF16)
    return hi, mid, lo


def _attention_fwd(qkv, big_f):
    S = qkv.shape[0]
    T = min(T_ATT, S)
    n_t = S // T

    def body(q_ref, k_ref, v_ref, f_ref, o_ref, lse_ref, kaug_sc, vt_sc, m_sc, l_sc, acc_sc):
        hp = pl.program_id(0)
        i = pl.program_id(1)
        lane = lax.broadcasted_iota(jnp.int32, (1, 128), 1)
        sub = lax.broadcasted_iota(jnp.int32, (128, 1), 0)
        head_sel = (lane < HEAD_DIM, lane >= HEAD_DIM)
        head_sel_t = (sub < HEAD_DIM, sub >= HEAD_DIM)
        spare = (HEAD_DIM, 0)
        zero = jnp.zeros((), BF16)

        @pl.when(i == 0)
        def _():
            def prep(jt, carry):
                rows = pl.ds(pl.multiple_of(jt * T, T), T)
                k = k_ref[rows, :]
                ft = f_ref[rows, :]
                vt = v_ref[rows, :].astype(F32).T
                for h in range(2):
                    fh = jnp.sum(jnp.where(lane == 2 * hp + h, ft, 0.0), axis=1, keepdims=True)
                    hi, mid, lo = _split3(-fh)
                    b = spare[h]
                    bias = jnp.where(lane == b, hi, jnp.where(lane == b + 1, mid, jnp.where(lane == b + 2, lo, zero)))
                    kaug_sc[h, rows, :] = jnp.where(head_sel[h], k, bias)
                    vt_sc[h, jt] = jnp.where(head_sel_t[h], vt, 0.0).astype(BF16)
                return carry

            lax.fori_loop(0, n_t, prep, 0)

        q = q_ref[...]
        q_heads = []
        for h in range(2):
            ones = jnp.where((lane >= spare[h]) & (lane < spare[h] + 3), jnp.ones((), BF16), zero)
            q_heads.append(jnp.where(head_sel[h], q, ones))
        m_sc[...] = jnp.full((8, T), NEG, F32)
        l_sc[...] = jnp.zeros((8, T), F32)
        acc_sc[...] = jnp.zeros((128, T), F32)

        def step(j, masked):
            rows = pl.ds(pl.multiple_of(j * T, T), T)
            alphas, pvs = [], []
            for h in range(2):
                s_t = _dot_nt(kaug_sc[h, rows, :], q_heads[h])
                if masked:
                    rr = lax.broadcasted_iota(jnp.int32, (T, T), 0)
                    cc = lax.broadcasted_iota(jnp.int32, (T, T), 1)
                    s_t = jnp.where(rr <= cc, s_t, NEG)
                m_prev = m_sc[h:h + 1, :]
                m_new = jnp.maximum(m_prev, jnp.max(s_t, axis=0, keepdims=True))
                alpha = jnp.exp(m_prev - m_new)
                p_t = jnp.exp(s_t - m_new)
                l_sc[h:h + 1, :] = alpha * l_sc[h:h + 1, :] + jnp.sum(p_t, axis=0, keepdims=True)
                m_sc[h:h + 1, :] = m_new
                alphas.append(alpha)
                pvs.append(_dot(vt_sc[h, j], p_t.astype(BF16)))
            acc_sc[...] = acc_sc[...] * jnp.where(head_sel_t[0], alphas[0], alphas[1]) + (pvs[0] + pvs[1])

        def off_diagonal(j, carry):
            step(j, False)
            return carry

        lax.fori_loop(0, i, off_diagonal, 0)
        step(i, True)
        l = l_sc[...]
        o_ref[...] = (acc_sc[...] / jnp.where(head_sel_t[0], l[0:1, :], l[1:2, :])).T
        is_head = lax.broadcasted_iota(jnp.int32, (8, 1), 0) < 2
        lse_ref[...] = jnp.where(is_head, m_sc[...] + jnp.log(jnp.where(is_head, l, 1.0)), 0.0)

    return pl.pallas_call(
        body, name="attention_fwd", grid=(N_PAIR, n_t),
        out_shape=(jax.ShapeDtypeStruct((S, D_ATT), F32), jax.ShapeDtypeStruct((N_PAIR, n_t, 8, T), F32)),
        in_specs=[pl.BlockSpec((T, 128), lambda hp, i: (i, hp)),
                  pl.BlockSpec((S, 128), lambda hp, i: (0, N_PAIR + hp)),
                  pl.BlockSpec((S, 128), lambda hp, i: (0, 2 * N_PAIR + hp)),
                  pl.BlockSpec((S, 128), lambda hp, i: (0, 0))],
        out_specs=(pl.BlockSpec((T, 128), lambda hp, i: (i, hp)),
                   pl.BlockSpec((None, None, 8, T), lambda hp, i: (hp, i, 0, 0))),
        scratch_shapes=[pltpu.VMEM((2, S, 128), BF16), pltpu.VMEM((2, n_t, 128, T), BF16),
                        pltpu.VMEM((8, T), F32), pltpu.VMEM((8, T), F32), pltpu.VMEM((128, T), F32)],
        compiler_params=_params(dimension_semantics=("arbitrary", "arbitrary")),
    )(qkv, qkv, qkv, big_f)


def _attention_bwd(qkv, datt, att, lse, big_f):
    S = qkv.shape[0]
    T = min(T_ATT, S)
    n_t = S // T

    def body(q_ref, do_ref, o_ref, lse_ref, k_ref, v_ref, fk_ref,
             dq_ref, dk_ref, dv_ref, cs_ref, dfk_ref, dfq_ref, stat_sc, dqt_sc, qaug_sc):
        hp = pl.program_id(0)
        j = pl.program_id(1)
        lane = lax.broadcasted_iota(jnp.int32, (1, 128), 1)
        sub = lax.broadcasted_iota(jnp.int32, (128, 1), 0)
        head_sel = (lane < HEAD_DIM, lane >= HEAD_DIM)
        head_sel_t = (sub < HEAD_DIM, sub >= HEAD_DIM)
        spare = (HEAD_DIM, 0)
        zero = jnp.zeros((), BF16)
        one = jnp.ones((), BF16)

        def bias_lanes(first, pieces):
            hi, mid, lo = pieces
            return lambda rest: jnp.where(lane == first, hi, jnp.where(lane == first + 1, mid,
                                                                        jnp.where(lane == first + 2, lo, rest)))

        @pl.when(j == 0)
        def _():
            dqt_sc[...] = jnp.zeros_like(dqt_sc)
            cs_ref[...] = jnp.zeros_like(cs_ref)
            dfq_ref[...] = jnp.zeros_like(dfq_ref)

            def prep(i, carry):
                rows = pl.ds(pl.multiple_of(i * T, T), T)
                q = q_ref[rows, :]
                do = do_ref[rows, :]
                prod = o_ref[rows, :] * do.astype(F32)
                d_a = jnp.sum(jnp.where(head_sel[0], prod, 0.0), axis=1, keepdims=True)
                d_b = jnp.sum(jnp.where(head_sel[0], 0.0, prod), axis=1, keepdims=True)
                delta_t = jnp.where(head_sel[0], d_a, d_b).T
                stat_sc[i, 0:1, :] = delta_t[0:1, :]
                stat_sc[i, 1:2, :] = delta_t[HEAD_DIM:HEAD_DIM + 1, :]
                lse = lse_ref[i]
                lse_cols = jnp.where(head_sel_t[0], lse[0:1, :], lse[1:2, :]).T
                for h in range(2):
                    neg_lse = -lse_cols[:, h * HEAD_DIM:h * HEAD_DIM + 1]
                    ones = jnp.where((lane >= spare[h]) & (lane < spare[h] + 3), one, zero)
                    qaug_sc[h, rows, :] = jnp.where(head_sel[h], q, bias_lanes(spare[h] + 3, _split3(neg_lse))(ones))
                return carry

            lax.fori_loop(0, n_t, prep, 0)

        k = k_ref[...]
        v = v_ref[...]
        fk = fk_ref[...]
        kt = k.astype(F32).T
        heads = []
        for h in range(2):
            fkh = jnp.sum(jnp.where(lane == 2 * hp + h, fk, 0.0), axis=1, keepdims=True)
            ones = jnp.where((lane >= spare[h] + 3) & (lane < spare[h] + 6), one, zero)
            kaug = jnp.where(head_sel[h], k, bias_lanes(spare[h], _split3(-fkh))(ones))
            heads.append((kaug, jnp.where(head_sel[h], v, zero), jnp.where(head_sel_t[h], kt, 0.0).astype(BF16)))

        def step(i, masked, acc):
            dk_acc, dv_acc, dfa, dfb = acc
            rows = pl.ds(pl.multiple_of(i * T, T), T)
            do = do_ref[rows, :]
            stat = stat_sc[i]
            dqt = dqt_sc[i]
            dfs = [dfa, dfb]
            for h in range(2):
                kaug, vh, kth = heads[h]
                arg = _dot_nt(kaug, qaug_sc[h, rows, :])
                if masked:
                    rr = lax.broadcasted_iota(jnp.int32, (T, T), 0)
                    cc = lax.broadcasted_iota(jnp.int32, (T, T), 1)
                    arg = jnp.where(rr <= cc, arg, NEG)
                p_t = jnp.exp(arg)
                ds_t = p_t * (_dot_nt(vh, do) - stat[h:h + 1, :])
                ds_bf = ds_t.astype(BF16)
                dv_acc = dv_acc + _dot(p_t.astype(BF16), jnp.where(head_sel[h], do, zero))
                dk_acc = dk_acc + _dot(ds_bf, jnp.where(head_sel[h], q_ref[rows, :], zero))
                dqt = dqt + _dot(kth, ds_bf)
                dfs[h] = dfs[h] + jnp.sum(ds_t, axis=1, keepdims=True)
                dfq_ref[i, h:h + 1, :] += _colsum(ds_t)
            dqt_sc[i] = dqt
            return dk_acc, dv_acc, dfs[0], dfs[1]

        acc0 = (jnp.zeros((T, 128), F32), jnp.zeros((T, 128), F32), jnp.zeros((T, 1), F32), jnp.zeros((T, 1), F32))
        acc1 = step(j, True, acc0)
        dk_acc, dv_acc, dfa, dfb = lax.fori_loop(j + 1, n_t, lambda i, a: step(i, False, a), acc1)
        dk_ref[...] = dk_acc.astype(BF16)
        dv_ref[...] = dv_acc.astype(BF16)
        dfk_ref[...] = -jnp.where(lane == 0, dfa, jnp.where(lane == 1, dfb, 0.0))
        cs_ref[:, 128:256] = cs_ref[:, 128:256] + _colsum(dk_acc)
        cs_ref[:, 256:384] = cs_ref[:, 256:384] + _colsum(dv_acc)

        @pl.when(j == n_t - 1)
        def _():
            def finish(i, tot):
                dq = dqt_sc[i].T
                dq_ref[pl.ds(pl.multiple_of(i * T, T), T), :] = dq.astype(BF16)
                return tot + _colsum(dq)

            cs_ref[:, 0:128] = lax.fori_loop(0, n_t, finish, jnp.zeros((1, 128), F32))

    pair_rows = lambda hp, j: (hp, 0, 0)
    return pl.pallas_call(
        body, name="attention_bwd", grid=(N_PAIR, n_t),
        out_shape=(jax.ShapeDtypeStruct((S, D_ATT), BF16), jax.ShapeDtypeStruct((S, D_ATT), BF16),
                   jax.ShapeDtypeStruct((S, D_ATT), BF16), jax.ShapeDtypeStruct((N_PAIR, 1, 384), F32),
                   jax.ShapeDtypeStruct((N_PAIR, S, 128), F32),
                   jax.ShapeDtypeStruct((N_PAIR, n_t, 8, T), F32)),
        in_specs=[pl.BlockSpec((S, 128), lambda hp, j: (0, hp)),
                  pl.BlockSpec((S, 128), lambda hp, j: (0, hp)),
                  pl.BlockSpec((S, 128), lambda hp, j: (0, hp)),
                  pl.BlockSpec((None, n_t, 8, T), lambda hp, j: (hp, 0, 0, 0)),
                  pl.BlockSpec((T, 128), lambda hp, j: (j, N_PAIR + hp)),
                  pl.BlockSpec((T, 128), lambda hp, j: (j, 2 * N_PAIR + hp)),
                  pl.BlockSpec((T, 128), lambda hp, j: (j, 0))],
        out_specs=(pl.BlockSpec((S, 128), lambda hp, j: (0, hp)),
                   pl.BlockSpec((T, 128), lambda hp, j: (j, hp)),
                   pl.BlockSpec((T, 128), lambda hp, j: (j, hp)),
                   pl.BlockSpec((None, 1, 384), pair_rows),
                   pl.BlockSpec((None, T, 128), lambda hp, j: (hp, j, 0)),
                   pl.BlockSpec((None, n_t, 8, T), lambda hp, j: (hp, 0, 0, 0))),
        scratch_shapes=[pltpu.VMEM((n_t, 8, T), F32), pltpu.VMEM((n_t, 128, T), F32),
                        pltpu.VMEM((2, S, 128), BF16)],
        compiler_params=_params(dimension_semantics=("arbitrary", "arbitrary")),
    )(qkv, datt, att, lse, qkv, qkv, big_f)


def _window_counts(first_row, n_rows, window):
    t = lax.broadcasted_iota(jnp.int32, (n_rows, 1), 0) + first_row
    return jnp.minimum((t + 1).astype(F32), float(window))


def _middle(x, tgt, att, g, p, gate, w_mix, b_mix, pool_scale, w_out, b_out, ln_g, ln_b):
    S = x.shape[0]
    tm = min(TM_MID, S)
    halo_blocks = tm // POOL_HALO

    def body(x_ref, t_ref, att_ref, g_ref, p_ref, ph_ref, gate_ref, wm_ref, bm_ref, ps_ref, wo_ref, bo_ref,
             lg_ref, lb_ref,
             dh_ref, datt_ref, dg_ref, dpl_ref, gwo_ref, gwm_ref, vec_ref, loss_ref):
        i = pl.program_id(0)

        @pl.when(i == 0)
        def _():
            gwo_ref[...] = jnp.zeros_like(gwo_ref)
            gwm_ref[...] = jnp.zeros_like(gwm_ref)
            vec_ref[...] = jnp.zeros_like(vec_ref)
            loss_ref[...] = jnp.zeros_like(loss_ref)

        pc = p_ref[...]
        halo = jnp.where(i > 0, ph_ref[...], 0.0)
        pe = jnp.concatenate([halo, pc], axis=0)
        pooled_parts = []
        for gi, w in enumerate(POOL_WINDOWS):
            cur = pe[:, gi * POOL_GROUP:(gi + 1) * POOL_GROUP]
            span = 1
            while span < w:
                cur = cur + pltpu.roll(cur, span, 0)
                span *= 2
            wsum = cur[POOL_HALO:, :]
            mean = wsum / _window_counts(i * tm, tm, w)
            pooled_parts.append(mean - pc[:, gi * POOL_GROUP:(gi + 1) * POOL_GROUP])
        pooled_bf =[v.astype(BF16) for v in pooled_parts]
        mixed = jnp.concatenate([_dot(pooled_bf[gi], wm_ref[gi]) for gi in range(4)], axis=1) + bm_ref[...]
        ps = ps_ref[...]
        pool_out = mixed * ps
        gv = g_ref[...]
        sig = _sigmoid(gv)
        silu = gv * sig
        att = att_ref[...]
        y = jnp.concatenate([att * silu[:, :D_ATT], pool_out * silu[:, D_ATT:]], axis=1)
        y_bf = y.astype(BF16)
        wo = wo_ref[...]
        yo = _dot(y_bf, wo) + bo_ref[...]
        gate = gate_ref[...]
        h = ALPHA * x_ref[...] + gate * yo
        mu = jnp.mean(h, axis=1, keepdims=True)
        hc = h - mu
        var = jnp.mean(hc * hc, axis=1, keepdims=True)
        rstd = lax.rsqrt(var + LN_EPS)
        yhat = hc * rstd
        lg = lg_ref[...]
        out = yhat * lg + lb_ref[...]
        err = out - t_ref[...]
        loss_ref[...] += 0.5 * jnp.sum(jnp.mean(err * err, axis=1, keepdims=True), axis=0, keepdims=True)

        dout = err * (1.0 / D)
        g_ln_b = _colsum(dout)
        g_ln_g = _colsum(dout * yhat)
        dyh = dout * lg
        dh = rstd * (dyh - jnp.mean(dyh, axis=1, keepdims=True)
                     - yhat * jnp.mean(dyh * yhat, axis=1, keepdims=True))
        dh_ref[...] = dh
        d_gate = _colsum(dh * yo)
        dyo = gate * dh
        g_b_out = _colsum(dyo)
        dyo_bf = dyo.astype(BF16)
        gwo_ref[...] += _dot_tn(y_bf, dyo_bf)
        dy = _dot_nt(dyo_bf, wo)
        dsilu = sig * (1.0 + gv * (1.0 - sig))
        dy_a = dy[:, :D_ATT]
        dy_p = dy[:, D_ATT:]
        datt_ref[...] = (dy_a * silu[:, :D_ATT]).astype(BF16)
        dpo = dy_p * silu[:, D_ATT:]
        dg = jnp.concatenate([dy_a * att * dsilu[:, :D_ATT], dy_p * pool_out * dsilu[:, D_ATT:]], axis=1)
        dg_ref[...] = dg.astype(BF16)
        g_dg = _colsum(dg)
        g_ps = _colsum(dpo * mixed)
        dmixed = dpo * ps
        g_bm = _colsum(dmixed)
        dmixed_bf = dmixed.astype(BF16)
        dpl = []
        for gi in range(4):
            dm = dmixed_bf[:, gi * POOL_GROUP:(gi + 1) * POOL_GROUP]
            gwm_ref[gi] += _dot_tn(pooled_bf[gi], dm)
            dpl.append(_dot_nt(dm, wm_ref[gi]))
        dpl_ref[...] = jnp.concatenate(dpl, axis=1)
        vec_ref[0:1, :] += g_ln_g
        vec_ref[1:2, :] += g_ln_b
        vec_ref[2:3, :] += d_gate
        vec_ref[3:4, :] += g_b_out
        vec_ref[4:5, :] += g_dg
        vec_ref[5:6, 0:D_POOL] += g_ps
        vec_ref[6:7, 0:D_POOL] += g_bm

    row = lambda w: pl.BlockSpec((tm, w), lambda i: (i, 0))
    full2 = lambda a: pl.BlockSpec(a.shape, lambda i: (0, 0))
    full3 = lambda a: pl.BlockSpec(a.shape, lambda i: (0, 0, 0))
    return pl.pallas_call(
        body, name="middle", grid=(S // tm,),
        out_shape=(jax.ShapeDtypeStruct((S, D), F32),
                   jax.ShapeDtypeStruct((S, D_ATT), BF16),
                   jax.ShapeDtypeStruct((S, D), BF16),
                   jax.ShapeDtypeStruct((S, D_POOL), F32),
                   jax.ShapeDtypeStruct((D, D), F32),
                   jax.ShapeDtypeStruct((4, POOL_GROUP, POOL_GROUP), F32),
                   jax.ShapeDtypeStruct((8, D), F32),
                   jax.ShapeDtypeStruct((1, 1), F32)),
        in_specs=[row(D), row(D), row(D_ATT), row(D), row(D_POOL),
                  pl.BlockSpec((POOL_HALO, D_POOL), lambda i: (jnp.maximum(i * halo_blocks - 1, 0), 0)),
                  full2(gate), full3(w_mix), full2(b_mix), full2(pool_scale), full2(w_out), full2(b_out),
                  full2(ln_g), full2(ln_b)],
        out_specs=(row(D), row(D_ATT), row(D), row(D_POOL),
                   pl.BlockSpec((D, D), lambda i: (0, 0)),
                   pl.BlockSpec((4, POOL_GROUP, POOL_GROUP), lambda i: (0, 0, 0)),
                   pl.BlockSpec((8, D), lambda i: (0, 0)),
                   pl.BlockSpec((1, 1), lambda i: (0, 0))),
        compiler_params=_params(dimension_semantics=("arbitrary",)),
    )(x, tgt, att, g, p, p, gate, w_mix, b_mix, pool_scale, w_out, b_out, ln_g, ln_b)


def _tail(dpl, dfk8, f):
    S = dpl.shape[0]
    tm = min(TM_TAIL, S)
    n_t = S // tm
    halo_blocks = tm // POOL_HALO
    last_halo = S // POOL_HALO - 1

    def body(d_ref, dn_ref, dfk_ref, f_ref, dp_ref, df_ref, cs_ref, carry):
        s = pl.program_id(0)
        i = n_t - 1 - s

        @pl.when(s == 0)
        def _():
            carry[...] = jnp.zeros_like(carry)
            cs_ref[...] = jnp.zeros_like(cs_ref)

        dc = d_ref[...]
        nxt = jnp.where(s > 0, dn_ref[...], 0.0)
        de = jnp.concatenate([dc, nxt], axis=0)
        n_e = tm + POOL_HALO
        parts = []
        for gi, w in enumerate(POOL_WINDOWS):
            cur = de[:, gi * POOL_GROUP:(gi + 1) * POOL_GROUP] / _window_counts(i * tm, n_e, w)
            span = 1
            while span < w:
                cur = cur + pltpu.roll(cur, n_e - span, 0)
                span *= 2
            parts.append(cur[:tm, :] - dc[:, gi * POOL_GROUP:(gi + 1) * POOL_GROUP])
        dp = jnp.concatenate(parts, axis=1)
        dp_ref[...] = dp.astype(BF16)
        cs_ref[0:1, :] += _colsum(dp)

        r = lax.broadcasted_iota(jnp.int32, (tm, tm), 0)
        c = lax.broadcasted_iota(jnp.int32, (tm, tm), 1)
        tri = (r <= c).astype(F32)
        dlogf = jnp.dot(tri, dfk_ref[...], preferred_element_type=F32, precision=lax.Precision.HIGHEST) + carry[...]
        carry[...] = dlogf[0:1, :]
        df = dlogf * _sigmoid(-f_ref[...])
        df_ref[...] = df.astype(BF16)
        cs_ref[1:2, 0:128] += _colsum(df)

    rev = lambda w: pl.BlockSpec((tm, w), lambda s: (n_t - 1 - s, 0))
    return pl.pallas_call(
        body, name="tail", grid=(n_t,),
        out_shape=(jax.ShapeDtypeStruct((S, D_POOL), BF16), jax.ShapeDtypeStruct((S, 128), BF16),
                   jax.ShapeDtypeStruct((8, D_POOL), F32)),
        in_specs=[rev(D_POOL),
                  pl.BlockSpec((POOL_HALO, D_POOL),
                               lambda s: (jnp.minimum((n_t - s) * halo_blocks, last_halo), 0)),
                  rev(128), rev(128)],
        out_specs=(rev(D_POOL), rev(128), pl.BlockSpec((8, D_POOL), lambda s: (0, 0))),
        scratch_shapes=[pltpu.VMEM((1, 128), F32)],
        compiler_params=_params(dimension_semantics=("arbitrary",)),
    )(dpl, dpl, dfk8, f)


PIECES = ((O_QKV, D_ATT), (O_QKV + D_ATT, D_ATT), (O_QKV + 2 * D_ATT, D_ATT), (O_F, 128), (O_P, D_POOL), (O_G, D))


def _grad_w_in(u, pieces):
    S = u.shape[0]
    tm = min(TM_GW, S)
    n_t = S // tm

    def body(u_ref, *rest):
        piece_refs, out_ref, acc, sem = rest[:6], rest[6], rest[7], rest[8]
        i = pl.program_id(0)

        @pl.when(i == 0)
        def _():
            acc[...] = jnp.zeros_like(acc)

        u_t = u_ref[...]
        for (off, w), ref in zip(PIECES, piece_refs):
            acc[:, off:off + w] += _dot_tn(u_t, ref[...])

        @pl.when(i == n_t - 1)
        def _():
            cp = pltpu.make_async_copy(acc, out_ref, sem)
            cp.start()
            cp.wait()

    return pl.pallas_call(
        body, name="grad_w_in", grid=(n_t,),
        out_shape=jax.ShapeDtypeStruct((D, D_PAD), F32),
        in_specs=[pl.BlockSpec((tm, D), lambda i: (i, 0))]
        + [pl.BlockSpec((tm, w), lambda i: (i, 0)) for _, w in PIECES],
        out_specs=pl.BlockSpec(memory_space=pl.ANY),
        scratch_shapes=[pltpu.VMEM((D, D_PAD), F32), pltpu.SemaphoreType.DMA],
        compiler_params=_params(dimension_semantics=("arbitrary",)),
    )(u, *pieces)


def _grad_x(pieces, w_pad, dh, x, scale):
    S = x.shape[0]
    tm = min(TM_DU, S)

    def body(*refs):
        piece_refs = refs[:6]
        w_ref, dh_ref, x_ref, sc_ref, gx_ref, vec_ref = refs[6:]

        @pl.when(pl.program_id(0) == 0)
        def _():
            vec_ref[...] = jnp.zeros_like(vec_ref)

        du = jnp.zeros((tm, D), F32)
        for (off, w), ref in zip(PIECES, piece_refs):
            du = du + _dot_nt(ref[...], w_ref[:, off:off + w])
        xv = x_ref[...]
        gx_ref[...] = ALPHA * dh_ref[...] + du * (1.0 + sc_ref[...])
        vec_ref[0:1, :] += _colsum(du)
        vec_ref[1:2, :] += _colsum(du * xv)

    row = lambda w: pl.BlockSpec((tm, w), lambda i: (i, 0))
    return pl.pallas_call(
        body, name="grad_x", grid=(S // tm,),
        out_shape=(jax.ShapeDtypeStruct((S, D), F32), jax.ShapeDtypeStruct((8, D), F32)),
        in_specs=[row(w) for _, w in PIECES]
        + [pl.BlockSpec(w_pad.shape, lambda i: (0, 0)), row(D), row(D), pl.BlockSpec((1, D), lambda i: (0, 0))],
        out_specs=(row(D), pl.BlockSpec((8, D), lambda i: (0, 0))),
        compiler_params=_params(dimension_semantics=("arbitrary",)),
    )(*pieces, w_pad, dh, x, scale)


def _grad_ada(c_all, dada_all, dada_cols):
    def body(c_ref, dall_ref, dcol_ref, gw_ref, gb_ref):
        rows = lax.broadcasted_iota(jnp.int32, (8, 1), 0)
        cm = jnp.zeros((8, D), F32)
        dm = jnp.zeros((8, 3 * D), F32)
        for r in range(8):
            cm = jnp.where(rows == r, c_ref[r], cm)
            dm = jnp.where(rows == r, dall_ref[r], dm)
        act = cm * _sigmoid(cm)
        pad = jnp.zeros((8, D), F32)
        lhs = jnp.concatenate([act, pad], axis=0).astype(BF16)
        rhs = jnp.concatenate([dcol_ref[...], jnp.zeros((8, SHARD_ADA), F32)], axis=0).astype(BF16)
        gw_ref[...] = _dot_tn(lhs, rhs)
        gb_ref[...] = _colsum(dm)

    vm = pl.BlockSpec(memory_space=pltpu.VMEM)
    return pl.pallas_call(
        body, name="grad_ada",
        out_shape=(jax.ShapeDtypeStruct((D, SHARD_ADA), F32), jax.ShapeDtypeStruct((1, 3 * D), F32)),
        in_specs=[vm, vm, vm], out_specs=(vm, vm),
        compiler_params=_params(),
    )(c_all, dada_all, dada_cols)


def _adamw_math(w, g, m, v):
    m = ADAM_B1 * m + (1.0 - ADAM_B1) * g
    v = ADAM_B2 * v + (1.0 - ADAM_B2) * (g * g)
    m_hat = m / (1.0 - ADAM_B1 ** ADAM_STEP)
    v_hat = v / (1.0 - ADAM_B2 ** ADAM_STEP)
    delta = -ADAM_LR * (m_hat / (jnp.sqrt(v_hat) + ADAM_EPS) + ADAM_WD * w)
    return delta, m, v


def _adamw(groups, n_steps):
    n = len(groups)

    def body(*refs):
        ins, outs = refs[:4 * n], refs[4 * n:]
        for t in range(n):
            w, g, m, v = (r[...] for r in ins[4 * t:4 * t + 4])
            d, m2, v2 = _adamw_math(w, g, m, v)
            outs[3 * t][...] = d
            outs[3 * t + 1][...] = m2
            outs[3 * t + 2][...] = v2

    in_specs, out_specs, out_shape, args = [], [], [], []
    for (w, g, m, v) in groups:
        r, c = w.shape
        spec = pl.BlockSpec((r // n_steps, c), lambda i: (i, 0))
        in_specs += [spec] * 4
        out_specs += [spec] * 3
        out_shape += [jax.ShapeDtypeStruct((r, c), F32)] * 3
        args += [w, g, m, v]
    return pl.pallas_call(
        body, name="adamw_%d" % n, grid=(n_steps,),
        out_shape=tuple(out_shape), in_specs=in_specs, out_specs=tuple(out_specs),
        compiler_params=_params(dimension_semantics=("arbitrary",)),
    )(*args)


def _pack_small(parts):
    rows = []
    used = 0
    for name, (first, n_rows) in SMALL_SEGS.items():
        if first > used:
            rows.append(jnp.zeros((first - used, 128), F32))
        flat = parts[name].reshape(-1)
        flat = jnp.pad(flat, (0, n_rows * 128 - flat.shape[0]))
        rows.append(flat.reshape(n_rows, 128))
        used = first + n_rows
    rows.append(jnp.zeros((SMALL_ROWS - used, 128), F32))
    return jnp.concatenate(rows, axis=0)


def _unpack_small(buf, name, shape):
    first, n_rows = SMALL_SEGS[name]
    n = int(np.prod(shape))
    return buf[first:first + n_rows].reshape(-1)[:n].reshape(shape)


def _pad_in(v):
    r = v.shape[0]
    z = jnp.zeros((r, O_P - O_F - N_HEADS), v.dtype)
    return jnp.concatenate([v[:, :3 * D_ATT + N_HEADS], z, v[:, 3 * D_ATT + N_HEADS:]], axis=1)


def _unpad_in(v):
    return jnp.concatenate([v[:, :O_F + N_HEADS], v[:, O_P:]], axis=1)


def _shards_in(v):
    gap = O_P - (O_F + N_HEADS)
    parts = []
    for a in range(N_CHIPS):
        lo, hi = a * SHARD_IN, (a + 1) * SHARD_IN
        cut = O_F + N_HEADS
        if hi <= cut:
            parts.append(v[:, lo:hi])
        elif lo >= cut:
            parts.append(v[:, lo + gap:hi + gap])
        else:
            parts.append(jnp.concatenate([v[:, lo:cut], v[:, cut + gap:hi + gap]], axis=1))
    return jnp.stack(parts, axis=0)


def kernel(x, c, w_ada, b_ada, w_in, b_in, w_pool_mix, b_pool_mix, pool_scale, w_out, b_out, ln_g, ln_b, loss_target, m_w_ada, m_b_ada, m_w_in, m_b_in, m_w_pool_mix, m_b_pool_mix, m_pool_scale, m_w_out, m_b_out, m_ln_g, m_ln_b, v_w_ada, v_b_ada, v_w_in, v_b_in, v_w_pool_mix, v_b_pool_mix, v_pool_scale, v_w_out, v_b_out, v_ln_g, v_ln_b):
    S = x.shape[1]
    T = min(T_ATT, S)
    n_t = S // T
    chip = 2 * lax.axis_index("x") + lax.axis_index("y")
    x2 = x[0]
    tgt = loss_target[0]
    q_scale = jnp.concatenate([jnp.full((1, D_ATT), Q_SCALE, F32), jnp.ones((1, D_PAD - D_ATT), F32)], axis=1)

    c_all, ada4 = _ada_exchange(c, w_ada[0], b_ada.reshape(4, 1, SHARD_ADA))
    ada = ada4[:, 0, :].reshape(1, 3 * D)
    shift, scale, gate = ada[:, :D], ada[:, D:2 * D], ada[:, 2 * D:]
    w_in_all, w_out_all = _gather_weights(w_in[0].astype(BF16), w_out[0].astype(BF16))
    w_in_full = jnp.transpose(w_in_all, (1, 0, 2)).reshape(D, D_IN)
    w_pad = _pad_in(w_in_full) * q_scale.astype(BF16)
    b_pad = _pad_in(b_in) * q_scale
    w_out_full = w_out_all.reshape(D, D)
    w_mix_bf = w_pool_mix[0].astype(BF16)

    u, qkv, f, p, g = _in_proj(x2, shift, scale, w_pad, b_pad)
    big_f = _forget_cumsum(f)
    att, lse = _attention_fwd(qkv, big_f)

    dh, datt, dg, dpl, gw_out, gw_mix, vec, loss_part = _middle(
        x2, tgt, att, g, p, gate, w_mix_bf, b_pool_mix.reshape(1, D_POOL), pool_scale, w_out_full, b_out, ln_g, ln_b)
    dq, dk, dv, cs_att, dfk, dfq = _attention_bwd(qkv, datt, att, lse, big_f)
    dfk8 = jnp.transpose(dfk[:, :, 0:2], (1, 0, 2)).reshape(S, N_HEADS)
    dfk8 = dfk8 + jnp.transpose(dfq[:, :, 0:2, :], (1, 3, 0, 2)).reshape(S, N_HEADS)
    dfk8 = jnp.pad(dfk8, ((0, 0), (0, 128 - N_HEADS)))
    dp, df, cs_tail = _tail(dpl, dfk8, f)
    pieces = (dq, dk, dv, df, dp, dg)
    gw_pad = _grad_w_in(u, pieces)
    grad_x, vec_x = _grad_x(pieces, w_pad, dh, x2, scale)

    cs_qkv = jnp.transpose(cs_att.reshape(N_PAIR, 3, 128), (1, 0, 2)).reshape(1, 3 * D_ATT)
    gb_pad = jnp.concatenate([cs_qkv, cs_tail[1:2, 0:128], cs_tail[0:1, :], vec[4:5, :]], axis=1) * q_scale
    dada = jnp.concatenate([vec_x[0:1, :], vec_x[1:2, :], vec[2:3, :]], axis=1)
    small = _pack_small({
        "b_in": gb_pad, "w_pool_mix": gw_mix, "b_pool_mix": vec[6:7, :D_POOL], "pool_scale": vec[5:6, :D_POOL],
        "b_out": vec[3:4, :], "ln_g": vec[0:1, :], "ln_b": vec[1:2, :], "b_ada": jnp.zeros((1, 3 * D), F32)})

    g_w_in = _reduce_scatter(_shards_in(gw_pad), _shards_in(q_scale), "reduce_w_in")
    g_w_out = _reduce_scatter(gw_out.reshape(N_CHIPS, SHARD_OUT, D), jnp.ones((N_CHIPS, 1, D), F32), "reduce_w_out")
    small_sum, dada_all = _all_reduce_small(small[:SMALL_REDUCED_ROWS], dada)
    dada_cols = lax.dynamic_slice(dada_all[:, 0, :], (0, chip * SHARD_ADA), (8, SHARD_ADA))
    g_w_ada, g_b_ada = _grad_ada(c_all, dada_all, dada_cols)
    loss = lax.psum(loss_part[0, 0], ("x", "y", "c"))

    grads_small = jnp.concatenate([small_sum, g_b_ada.reshape(24, 128)], axis=0)
    small_w = {"b_in": _pad_in(b_in), "w_pool_mix": w_pool_mix, "b_pool_mix": b_pool_mix, "pool_scale": pool_scale,
               "b_out": b_out, "ln_g": ln_g, "ln_b": ln_b, "b_ada": b_ada}
    small_m = {"b_in": _pad_in(m_b_in), "w_pool_mix": m_w_pool_mix, "b_pool_mix": m_b_pool_mix,
               "pool_scale": m_pool_scale, "b_out": m_b_out, "ln_g": m_ln_g, "ln_b": m_ln_b, "b_ada": m_b_ada}
    small_v = {"b_in": _pad_in(v_b_in), "w_pool_mix": v_w_pool_mix, "b_pool_mix": v_b_pool_mix,
               "pool_scale": v_pool_scale, "b_out": v_b_out, "ln_g": v_ln_g, "ln_b": v_ln_b, "b_ada": v_b_ada}
    big = _adamw([(w_ada[0], g_w_ada, m_w_ada[0], v_w_ada[0]),
                  (w_in[0], g_w_in, m_w_in[0], v_w_in[0]),
                  (w_out[0], g_w_out, m_w_out[0], v_w_out[0])], 8)
    sm = _adamw([(_pack_small(small_w), grads_small, _pack_small(small_m), _pack_small(small_v))], 1)

    names = ["w_ada", "b_ada", "w_in", "b_in", "w_pool_mix", "b_pool_mix", "pool_scale", "w_out", "b_out",
             "ln_g", "ln_b"]
    shapes = {"b_ada": (1, 3 * D), "b_in": (1, D_PAD), "w_pool_mix": (1, 4, POOL_GROUP, POOL_GROUP),
              "b_pool_mix": (1, 4, POOL_GROUP), "pool_scale": (1, D_POOL), "b_out": (1, D), "ln_g": (1, D),
              "ln_b": (1, D)}
    big_idx = {"w_ada": 0, "w_in": 1, "w_out": 2}

    def leaf(kind, name):
        if name in big_idx:
            if kind == 0:
                return (g_w_ada, g_w_in, g_w_out)[big_idx[name]][None]
            return big[3 * big_idx[name] + kind - 1][None]
        buf = grads_small if kind == 0 else sm[kind - 1]
        val = _unpack_small(buf, name, shapes[name])
        if name == "b_in":
            val = _unpad_in(val)
        return val

    outs = [loss, grad_x[None]]
    for kind in range(4):
        outs += [leaf(kind, n) for n in names]
    return tuple(outs)
```

```python
import functools

import numpy as np
import jax
import jax.numpy as jnp
from jax import lax
from jax.experimental import pallas as pl
from jax.experimental.pallas import tpu as pltpu

F32 = jnp.float32
BF16 = jnp.bfloat16
MESH = pl.DeviceIdType.MESH

D = 1024
D_ATT = 512
D_POOL = 512
N_HEADS = 8
HEAD_DIM = 64
N_PAIR = N_HEADS // 2
POOL_WINDOWS = (2, 4, 8, 16)
POOL_GROUP = 128
POOL_HALO = 16
LN_EPS = 1e-5
ALPHA = 2.0 ** 0.25
D_IN = 3 * D_ATT + N_HEADS + D_POOL + D_ATT + D_POOL
N_CHIPS = 4
SHARD_IN = D_IN // N_CHIPS
SHARD_ADA = 3 * D // N_CHIPS
SHARD_OUT = D // N_CHIPS

O_QKV, O_F, O_P, O_G, D_PAD = 0, 1536, 1664, 2176, 3200
Q_SCALE = HEAD_DIM ** -0.5

ADAM_LR, ADAM_B1, ADAM_B2, ADAM_EPS, ADAM_WD, ADAM_STEP = 0.001, 0.9, 0.999, 1e-08, 0.01, 10

NEG = -1e30

VMEM_LIMIT = 56 * 1024 * 1024

TM_PROJ = 512
T_ATT = 512
ATT_CHUNK = 32
TM_MID = 256
TM_TAIL = 512
TM_GW = 512
TM_DU = 512

REL7 = [(0, 0, 1), (0, 1, 0), (0, 1, 1), (1, 0, 0), (1, 0, 1), (1, 1, 0), (1, 1, 1)]
REL3 = [(0, 1), (1, 0), (1, 1)]

SMALL_SEGS = {}
_row = 0
for _name, _n in (("b_in", 3200), ("w_pool_mix", 65536), ("b_pool_mix", 512), ("pool_scale", 512),
                  ("b_out", 1024), ("ln_g", 1024), ("ln_b", 1024)):
    _rows = -(-_n // 1024) * 8
    SMALL_SEGS[_name] = (_row, _rows)
    _row += _rows
SMALL_REDUCED_ROWS = -(-_row // 16) * 16
SMALL_SEGS["b_ada"] = (SMALL_REDUCED_ROWS, 24)
SMALL_ROWS = SMALL_REDUCED_ROWS + 24


def _params(**kw):
    return pltpu.CompilerParams(vmem_limit_bytes=VMEM_LIMIT, **kw)


def _flip(v, d):
    return v if d == 0 else 1 - v


def _dot(a, b):
    return jnp.dot(a, b, preferred_element_type=F32)


def _dot_nt(a, b):
    return lax.dot_general(a, b, (((1,), (1,)), ((), ())), preferred_element_type=F32)


def _dot_tn(a, b):
    return lax.dot_general(a, b, (((0,), (0,)), ((), ())), preferred_element_type=F32)


def _sigmoid(v):
    return 1.0 / (1.0 + jnp.exp(-v))


def _colsum(v):
    return jnp.sum(v, axis=0, keepdims=True)


def _gather_and_ada(c, w_ada, b_ada4, w_in_sh, w_out_sh):
    def body(c_ref, w_ref, b_ref, win_ref, wout_ref, call_ref, ada_ref, win_all, wout_all,
             cslab, sbuf, rbuf, cs_sem, cr_sem, as_sem, ar_sem, own_sem, s_sem, r_sem, fs_sem, fr_sem):
        x, y, cc = lax.axis_index("x"), lax.axis_index("y"), lax.axis_index("c")
        me = 4 * x + 2 * y + cc
        chip = 2 * x + y
        sib = (x, y, 1 - cc)
        srcs = (win_ref, wout_ref)
        dsts = (win_all, wout_all)
        halves = (D // 2, SHARD_OUT // 2)

        def rows(t, which):
            h = halves[t]
            return pl.ds(pl.multiple_of(which * h, h), h)

        own = [pltpu.make_async_copy(srcs[t], dsts[t].at[chip], own_sem.at[t]) for t in range(2)]
        for cp in own:
            cp.start()
        first = []
        for k, (dx, dy) in enumerate(REL3):
            peer = (_flip(x, dx), _flip(y, dy), cc)
            for t in range(2):
                cp = pltpu.make_async_remote_copy(
                    src_ref=srcs[t].at[rows(t, cc), :], dst_ref=dsts[t].at[chip, rows(t, cc), :],
                    send_sem=s_sem.at[2 * k + t], recv_sem=r_sem.at[2 * k + t],
                    device_id=peer, device_id_type=MESH)
                cp.start()
                first.append(cp)

        cslab[...] = jnp.broadcast_to(c_ref[...], (8, D))
        call_ref[me] = cslab[...]
        gathers = []
        for k, (dx, dy, dc) in enumerate(REL7):
            cp = pltpu.make_async_remote_copy(
                src_ref=cslab, dst_ref=call_ref.at[me], send_sem=cs_sem.at[k], recv_sem=cr_sem.at[k],
                device_id=(_flip(x, dx), _flip(y, dy), _flip(cc, dc)), device_id_type=MESH)
            cp.start()
            gathers.append(cp)
        for cp in gathers:
            cp.wait()
        slab_row = lax.broadcasted_iota(jnp.int32, (8, 1), 0)
        mat = jnp.zeros((8, D), F32)
        for r in range(8):
            mat = jnp.where(slab_row == r, call_ref[r], mat)
        act = (mat * _sigmoid(mat)).astype(BF16)
        part = _dot(act, w_ref[...].astype(BF16))
        sends = []
        for k, (dx, dy) in enumerate(REL3):
            px, py = _flip(x, dx), _flip(y, dy)
            r = 4 * px + 2 * py + cc
            piece = _colsum(jnp.where(slab_row == r, part, 0.0))
            sbuf[k] = jnp.broadcast_to(piece, (8, SHARD_ADA))
            cp = pltpu.make_async_remote_copy(
                src_ref=sbuf.at[k], dst_ref=rbuf.at[k], send_sem=as_sem.at[k], recv_sem=ar_sem.at[k],
                device_id=(px, py, cc), device_id_type=MESH)
            cp.start()
            sends.append(cp)
        own_piece = _colsum(jnp.where(slab_row == me, part, 0.0))
        ada_ref[chip] = jnp.broadcast_to(own_piece, (8, SHARD_ADA)) + b_ref[chip]
        for k, (dx, dy) in enumerate(REL3):
            sends[k].wait()
            a = 2 * _flip(x, dx) + _flip(y, dy)
            ada_ref[a] = rbuf[k] + b_ref[a]

        passed = []
        for k, (dx, dy) in enumerate(REL3):
            a = 2 * _flip(x, dx) + _flip(y, dy)
            for t in range(2):
                landed = dsts[t].at[a, rows(t, cc), :]
                pltpu.make_async_remote_copy(
                    src_ref=landed, dst_ref=landed, send_sem=s_sem.at[2 * k + t], recv_sem=r_sem.at[2 * k + t],
                    device_id=sib, device_id_type=MESH).wait_recv()
                cp = pltpu.make_async_remote_copy(
                    src_ref=landed, dst_ref=landed, send_sem=fs_sem.at[2 * k + t], recv_sem=fr_sem.at[2 * k + t],
                    device_id=sib, device_id_type=MESH)
                cp.start()
                passed.append(cp)
        for k, (dx, dy) in enumerate(REL3):
            a = 2 * _flip(x, dx) + _flip(y, dy)
            for t in range(2):
                other = dsts[t].at[a, rows(t, 1 - cc), :]
                pltpu.make_async_remote_copy(
                    src_ref=other, dst_ref=other, send_sem=fs_sem.at[2 * k + t], recv_sem=fr_sem.at[2 * k + t],
                    device_id=sib, device_id_type=MESH).wait_recv()
        for cp in first + passed:
            cp.wait_send()
        for cp in own:
            cp.wait()

    vm = pl.BlockSpec(memory_space=pltpu.VMEM)
    return pl.pallas_call(
        body, name="gather_and_ada",
        out_shape=(jax.ShapeDtypeStruct((8, 8, D), F32), jax.ShapeDtypeStruct((4, 8, SHARD_ADA), F32),
                   jax.ShapeDtypeStruct((N_CHIPS, D, SHARD_IN), BF16),
                   jax.ShapeDtypeStruct((N_CHIPS, SHARD_OUT, D), BF16)),
        in_specs=[vm] * 5, out_specs=(vm,) * 4,
        scratch_shapes=[pltpu.VMEM((8, D), F32), pltpu.VMEM((3, 8, SHARD_ADA), F32),
                        pltpu.VMEM((3, 8, SHARD_ADA), F32),
                        pltpu.SemaphoreType.DMA((7,)), pltpu.SemaphoreType.DMA((7,)),
                        pltpu.SemaphoreType.DMA((3,)), pltpu.SemaphoreType.DMA((3,)),
                        pltpu.SemaphoreType.DMA((2,)), pltpu.SemaphoreType.DMA((6,)),
                        pltpu.SemaphoreType.DMA((6,)), pltpu.SemaphoreType.DMA((6,)),
                        pltpu.SemaphoreType.DMA((6,))],
        compiler_params=_params(),
    )(c, w_ada, b_ada4, w_in_sh, w_out_sh)


def _scatter_stages(pos, g_ref, sc_ref, out_ref, sib_buf, send_buf, ici_buf, sem1, sem2s, sem2r, sem3):
    x, y, cc, chip, sib = pos
    RH = g_ref.shape[1] // 2
    mine = pl.ds(pl.multiple_of(cc * RH, RH), RH)
    theirs = pl.ds(pl.multiple_of((1 - cc) * RH, RH), RH)
    cp1 = pltpu.make_async_remote_copy(
        src_ref=g_ref.at[:, theirs, :], dst_ref=sib_buf, send_sem=sem1.at[0], recv_sem=sem1.at[1],
        device_id=sib, device_id_type=MESH)
    sends = []
    for k, (dx, dy) in enumerate(REL3):
        px, py = _flip(x, dx), _flip(y, dy)
        sends.append(pltpu.make_async_remote_copy(
            src_ref=send_buf.at[2 * px + py], dst_ref=ici_buf.at[chip],
            send_sem=sem2s.at[k], recv_sem=sem2r.at[k], device_id=(px, py, cc), device_id_type=MESH))
    cp3 = pltpu.make_async_remote_copy(
        src_ref=out_ref.at[mine, :], dst_ref=out_ref.at[mine, :], send_sem=sem3.at[0], recv_sem=sem3.at[1],
        device_id=sib, device_id_type=MESH)

    def finish1():
        cp1.wait()
        for a in range(N_CHIPS):
            both = g_ref[a, mine, :] + sib_buf[a]
            sib_buf[a] = both
            send_buf[a] = both.astype(BF16)

    def start2():
        for cp in sends:
            cp.start()
        ici_buf[chip] = send_buf[chip]

    def finish2():
        for cp in sends:
            cp.wait()
        own = sib_buf[chip]
        parts = [jnp.where(chip == a, own, ici_buf[a].astype(F32)) for a in range(N_CHIPS)]
        out_ref[mine, :] = ((parts[0] + parts[1]) + (parts[2] + parts[3])) * sc_ref[chip]

    return [(cp1.start, finish1), (start2, finish2), (cp3.start, cp3.wait)]


def _all_reduce_stages(pos, g_ref, out_ref, sib_buf, ici_buf, sem1, sem2s, sem2r, sem3):
    x, y, cc, chip, sib = pos
    RH = g_ref.shape[0] // 2
    mine = pl.ds(pl.multiple_of(cc * RH, 8), RH)
    theirs = pl.ds(pl.multiple_of((1 - cc) * RH, 8), RH)
    cp1 = pltpu.make_async_remote_copy(
        src_ref=g_ref.at[theirs, :], dst_ref=sib_buf, send_sem=sem1.at[0], recv_sem=sem1.at[1],
        device_id=sib, device_id_type=MESH)
    sends = []
    for k, (dx, dy) in enumerate(REL3):
        px, py = _flip(x, dx), _flip(y, dy)
        sends.append(pltpu.make_async_remote_copy(
            src_ref=sib_buf, dst_ref=ici_buf.at[chip],
            send_sem=sem2s.at[k], recv_sem=sem2r.at[k], device_id=(px, py, cc), device_id_type=MESH))
    cp3 = pltpu.make_async_remote_copy(
        src_ref=out_ref.at[mine, :], dst_ref=out_ref.at[mine, :], send_sem=sem3.at[0], recv_sem=sem3.at[1],
        device_id=sib, device_id_type=MESH)

    def finish1():
        cp1.wait()
        sib_buf[...] = g_ref[mine, :] + sib_buf[...]

    def start2():
        for cp in sends:
            cp.start()
        ici_buf[chip] = sib_buf[...]

    def finish2():
        for cp in sends:
            cp.wait()
        out_ref[mine, :] = (ici_buf[0] + ici_buf[1]) + (ici_buf[2] + ici_buf[3])

    return [(cp1.start, finish1), (start2, finish2), (cp3.start, cp3.wait)]


def _reduce_all(g4_in, sc_in, g4_out, sc_out, small, dada):
    shapes = [g4_in.shape[1:], g4_out.shape[1:]]
    R = small.shape[0]
    W = dada.shape[1]
    n_sem = 4

    def body(gin_ref, scin_ref, gout_ref, scout_ref, sm_ref, d_ref, oin_ref, oout_ref, osm_ref, dall_ref, *scratch):
        x, y, cc = lax.axis_index("x"), lax.axis_index("y"), lax.axis_index("c")
        me = 4 * x + 2 * y + cc
        pos = (x, y, cc, 2 * x + y, (x, y, 1 - cc))
        dslab, ds_sem, dr_sem = scratch[0:3]
        rest = scratch[3:]
        in_bufs, rest = rest[0:3 + n_sem], rest[3 + n_sem:]
        out_bufs, rest = rest[0:3 + n_sem], rest[3 + n_sem:]
        sm_bufs = rest
        dslab[...] = jnp.broadcast_to(d_ref[...], (8, W))
        dall_ref[me] = dslab[...]
        gathers = []
        for k, (dx, dy, dc) in enumerate(REL7):
            cp = pltpu.make_async_remote_copy(
                src_ref=dslab, dst_ref=dall_ref.at[me], send_sem=ds_sem.at[k], recv_sem=dr_sem.at[k],
                device_id=(_flip(x, dx), _flip(y, dy), _flip(cc, dc)), device_id_type=MESH)
            cp.start()
            gathers.append(cp)
        plans = [_scatter_stages(pos, gin_ref, scin_ref, oin_ref, *in_bufs),
                 _scatter_stages(pos, gout_ref, scout_ref, oout_ref, *out_bufs),
                 _all_reduce_stages(pos, sm_ref, osm_ref, *sm_bufs)]
        for stage in range(3):
            for plan in plans:
                plan[stage][0]()
            for plan in plans:
                plan[stage][1]()
        for cp in gathers:
            cp.wait()

    def sems():
        return [pltpu.SemaphoreType.DMA((2,)), pltpu.SemaphoreType.DMA((3,)),
                pltpu.SemaphoreType.DMA((3,)), pltpu.SemaphoreType.DMA((2,))]

    scratch = [pltpu.VMEM((8, W), F32), pltpu.SemaphoreType.DMA((7,)), pltpu.SemaphoreType.DMA((7,))]
    for r, c in shapes:
        scratch += [pltpu.VMEM((N_CHIPS, r // 2, c), F32), pltpu.VMEM((N_CHIPS, r // 2, c), BF16),
                    pltpu.VMEM((N_CHIPS, r // 2, c), BF16)] + sems()
    scratch += [pltpu.VMEM((R // 2, 128), F32), pltpu.VMEM((N_CHIPS, R // 2, 128), F32)] + sems()
    vm = pl.BlockSpec(memory_space=pltpu.VMEM)
    return pl.pallas_call(
        body, name="reduce_all",
        out_shape=(jax.ShapeDtypeStruct(g4_in.shape[1:], F32), jax.ShapeDtypeStruct(g4_out.shape[1:], F32),
                   jax.ShapeDtypeStruct((R, 128), F32), jax.ShapeDtypeStruct((8, 8, W), F32)),
        in_specs=[vm] * 6, out_specs=(vm,) * 4,
        scratch_shapes=scratch,
        compiler_params=_params(),
    )(g4_in, sc_in, g4_out, sc_out, small, dada)


def _in_proj(x, shift, scale, w_pad, b_pad):
    S = x.shape[0]
    tm = min(TM_PROJ, S)

    def body(x_ref, sh_ref, sc_ref, w_ref, b_ref, u_ref, qkv_ref, f_ref, p_ref, g_ref):
        u = (x_ref[...] * (1.0 + sc_ref[...]) + sh_ref[...]).astype(BF16)
        u_ref[...] = u
        qkv_ref[...] = (_dot(u, w_ref[:, O_QKV:O_F]) + b_ref[:, O_QKV:O_F]).astype(BF16)
        f_ref[...] = _dot(u, w_ref[:, O_F:O_P]) + b_ref[:, O_F:O_P]
        p_ref[...] = _dot(u, w_ref[:, O_P:O_G]) + b_ref[:, O_P:O_G]
        g_ref[...] = _dot(u, w_ref[:, O_G:D_PAD]) + b_ref[:, O_G:D_PAD]

    row = lambda w: pl.BlockSpec((tm, w), lambda i: (i, 0))
    full = lambda a: pl.BlockSpec(a.shape, lambda i: (0, 0))
    return pl.pallas_call(
        body, name="in_proj", grid=(S // tm,),
        out_shape=(jax.ShapeDtypeStruct((S, D), BF16), jax.ShapeDtypeStruct((S, 3 * D_ATT), BF16),
                   jax.ShapeDtypeStruct((S, 128), F32), jax.ShapeDtypeStruct((S, D_POOL), F32),
                   jax.ShapeDtypeStruct((S, D), F32)),
        in_specs=[row(D), full(shift), full(scale), full(w_pad), full(b_pad)],
        out_specs=(row(D), row(3 * D_ATT), row(128), row(D_POOL), row(D)),
        compiler_params=_params(dimension_semantics=("arbitrary",)),
    )(x, shift, scale, w_pad, b_pad)


def _forget_cumsum(f):
    S = f.shape[0]
    tm = min(T_ATT, S)

    def body(f_ref, out_ref, carry):
        @pl.when(pl.program_id(0) == 0)
        def _():
            carry[...] = jnp.zeros_like(carry)
        v = f_ref[...]
        logf = jnp.minimum(v, 0.0) - jnp.log(1.0 + jnp.exp(-jnp.abs(v)))
        r = lax.broadcasted_iota(jnp.int32, (tm, tm), 0)
        c = lax.broadcasted_iota(jnp.int32, (tm, tm), 1)
        tri = (r >= c).astype(F32)
        cum = jnp.dot(tri, logf, preferred_element_type=F32, precision=lax.Precision.HIGHEST) + carry[...]
        out_ref[...] = cum
        carry[...] = cum[tm - 8:tm, :][7:8, :]

    return pl.pallas_call(
        body, name="forget_cumsum", grid=(S // tm,),
        out_shape=jax.ShapeDtypeStruct((S, 128), F32),
        in_specs=[pl.BlockSpec((tm, 128), lambda i: (i, 0))],
        out_specs=pl.BlockSpec((tm, 128), lambda i: (i, 0)),
        scratch_shapes=[pltpu.VMEM((1, 128), F32)],
        compiler_params=_params(dimension_semantics=("arbitrary",)),
    )(f)


def _pair_select(is_a, va, vb):
    return jnp.where(is_a, va, vb)


def _split3(v):
    hi = v.astype(BF16)
    rest = v - hi.astype(F32)
    mid = rest.astype(BF16)
    lo = (rest - mid.astype(F32)).astype(BF16)
    return hi, mid, lo


def _attention_fwd(qkv, big_f):
    S = qkv.shape[0]
    T = min(T_ATT, S)
    n_t = S // T

    def body(q_ref, k_ref, v_ref, f_ref, o_ref, lse_ref, kaug_sc, vt_sc, m_sc, l_sc, acc_sc):
        hp = pl.program_id(0)
        i = pl.program_id(1)
        lane = lax.broadcasted_iota(jnp.int32, (1, 128), 1)
        sub = lax.broadcasted_iota(jnp.int32, (128, 1), 0)
        head_sel = (lane < HEAD_DIM, lane >= HEAD_DIM)
        head_sel_t = (sub < HEAD_DIM, sub >= HEAD_DIM)
        spare = (HEAD_DIM, 0)
        zero = jnp.zeros((), BF16)

        @pl.when(i == 0)
        def _():
            def prep(jt, carry):
                rows = pl.ds(pl.multiple_of(jt * T, T), T)
                k = k_ref[rows, :]
                ft = f_ref[rows, :]
                vt = v_ref[rows, :].astype(F32).T
                for h in range(2):
                    fh = jnp.sum(jnp.where(lane == 2 * hp + h, ft, 0.0), axis=1, keepdims=True)
                    hi, mid, lo = _split3(-fh)
                    b = spare[h]
                    bias = jnp.where(lane == b, hi, jnp.where(lane == b + 1, mid, jnp.where(lane == b + 2, lo, zero)))
                    kaug_sc[h, rows, :] = jnp.where(head_sel[h], k, bias)
                    vt_sc[h, jt] = jnp.where(head_sel_t[h], vt, 0.0).astype(BF16)
                return carry

            lax.fori_loop(0, n_t, prep, 0)

        q = q_ref[...]
        q_heads = []
        for h in range(2):
            ones = jnp.where((lane >= spare[h]) & (lane < spare[h] + 3), jnp.ones((), BF16), zero)
            q_heads.append(jnp.where(head_sel[h], q, ones))
        m_sc[...] = jnp.full((8, T), NEG, F32)
        l_sc[...] = jnp.zeros((8, T), F32)
        acc_sc[...] = jnp.zeros((128, T), F32)

        def step(j, masked):
            rows = pl.ds(pl.multiple_of(j * T, T), T)
            alphas, pvs = [], []
            for h in range(2):
                s_t = _dot_nt(kaug_sc[h, rows, :], q_heads[h])
                if masked:
                    rr = lax.broadcasted_iota(jnp.int32, (T, T), 0)
                    cc = lax.broadcasted_iota(jnp.int32, (T, T), 1)
                    s_t = jnp.where(rr <= cc, s_t, NEG)
                m_prev = m_sc[h:h + 1, :]
                m_new = jnp.maximum(m_prev, jnp.max(s_t, axis=0, keepdims=True))
                alpha = jnp.exp(m_prev - m_new)
                p_t = jnp.exp(s_t - m_new)
                l_sc[h:h + 1, :] = alpha * l_sc[h:h + 1, :] + jnp.sum(p_t, axis=0, keepdims=True)
                m_sc[h:h + 1, :] = m_new
                alphas.append(alpha)
                pvs.append(_dot(vt_sc[h, j], p_t.astype(BF16)))
            acc_sc[...] = acc_sc[...] * jnp.where(head_sel_t[0], alphas[0], alphas[1]) + (pvs[0] + pvs[1])

        def off_diagonal(j, carry):
            step(j, False)
            return carry

        lax.fori_loop(0, i, off_diagonal, 0)
        step(i, True)
        l = l_sc[...]
        o_ref[...] = (acc_sc[...] / jnp.where(head_sel_t[0], l[0:1, :], l[1:2, :])).T
        is_head = lax.broadcasted_iota(jnp.int32, (8, 1), 0) < 2
        lse_ref[...] = jnp.where(is_head, m_sc[...] + jnp.log(jnp.where(is_head, l, 1.0)), 0.0)

    return pl.pallas_call(
        body, name="attention_fwd", grid=(N_PAIR, n_t),
        out_shape=(jax.ShapeDtypeStruct((S, D_ATT), F32), jax.ShapeDtypeStruct((N_PAIR, n_t, 8, T), F32)),
        in_specs=[pl.BlockSpec((T, 128), lambda hp, i: (i, hp)),
                  pl.BlockSpec((S, 128), lambda hp, i: (0, N_PAIR + hp)),
                  pl.BlockSpec((S, 128), lambda hp, i: (0, 2 * N_PAIR + hp)),
                  pl.BlockSpec((S, 128), lambda hp, i: (0, 0))],
        out_specs=(pl.BlockSpec((T, 128), lambda hp, i: (i, hp)),
                   pl.BlockSpec((None, None, 8, T), lambda hp, i: (hp, i, 0, 0))),
        scratch_shapes=[pltpu.VMEM((2, S, 128), BF16), pltpu.VMEM((2, n_t, 128, T), BF16),
                        pltpu.VMEM((8, T), F32), pltpu.VMEM((8, T), F32), pltpu.VMEM((128, T), F32)],
        compiler_params=_params(dimension_semantics=("arbitrary", "arbitrary")),
    )(qkv, qkv, qkv, big_f)


def _attention_bwd(qkv, datt, att, lse, big_f):
    S = qkv.shape[0]
    T = min(T_ATT, S)
    n_t = S // T

    def body(q_ref, do_ref, o_ref, lse_ref, k_ref, v_ref, fk_ref,
             dq_ref, dk_ref, dv_ref, cs_ref, dfk_ref, dfq_ref, stat_sc, dqt_sc, qaug_sc):
        hp = pl.program_id(0)
        j = pl.program_id(1)
        lane = lax.broadcasted_iota(jnp.int32, (1, 128), 1)
        sub = lax.broadcasted_iota(jnp.int32, (128, 1), 0)
        head_sel = (lane < HEAD_DIM, lane >= HEAD_DIM)
        head_sel_t = (sub < HEAD_DIM, sub >= HEAD_DIM)
        spare = (HEAD_DIM, 0)
        zero = jnp.zeros((), BF16)
        one = jnp.ones((), BF16)

        def bias_lanes(first, pieces):
            hi, mid, lo = pieces
            return lambda rest: jnp.where(lane == first, hi, jnp.where(lane == first + 1, mid,
                                                                        jnp.where(lane == first + 2, lo, rest)))

        @pl.when(j == 0)
        def _():
            dqt_sc[...] = jnp.zeros_like(dqt_sc)
            cs_ref[...] = jnp.zeros_like(cs_ref)
            dfq_ref[...] = jnp.zeros_like(dfq_ref)

            def prep(i, carry):
                rows = pl.ds(pl.multiple_of(i * T, T), T)
                q = q_ref[rows, :]
                do = do_ref[rows, :]
                prod = o_ref[rows, :] * do.astype(F32)
                d_a = jnp.sum(jnp.where(head_sel[0], prod, 0.0), axis=1, keepdims=True)
                d_b = jnp.sum(jnp.where(head_sel[0], 0.0, prod), axis=1, keepdims=True)
                delta_t = jnp.where(head_sel[0], d_a, d_b).T
                stat_sc[i, 0:1, :] = delta_t[0:1, :]
                stat_sc[i, 1:2, :] = delta_t[HEAD_DIM:HEAD_DIM + 1, :]
                lse = lse_ref[i]
                lse_cols = jnp.where(head_sel_t[0], lse[0:1, :], lse[1:2, :]).T
                for h in range(2):
                    neg_lse = -lse_cols[:, h * HEAD_DIM:h * HEAD_DIM + 1]
                    ones = jnp.where((lane >= spare[h]) & (lane < spare[h] + 3), one, zero)
                    qaug_sc[h, rows, :] = jnp.where(head_sel[h], q, bias_lanes(spare[h] + 3, _split3(neg_lse))(ones))
                return carry

            lax.fori_loop(0, n_t, prep, 0)

        k = k_ref[...]
        v = v_ref[...]
        fk = fk_ref[...]
        kt = k.astype(F32).T
        heads = []
        for h in range(2):
            fkh = jnp.sum(jnp.where(lane == 2 * hp + h, fk, 0.0), axis=1, keepdims=True)
            ones = jnp.where((lane >= spare[h] + 3) & (lane < spare[h] + 6), one, zero)
            kaug = jnp.where(head_sel[h], k, bias_lanes(spare[h], _split3(-fkh))(ones))
            heads.append((kaug, jnp.where(head_sel[h], v, zero), jnp.where(head_sel_t[h], kt, 0.0).astype(BF16)))

        def step(i, masked, acc):
            dk_acc, dv_acc, dfa, dfb = acc
            rows = pl.ds(pl.multiple_of(i * T, T), T)
            do = do_ref[rows, :]
            stat = stat_sc[i]
            dqt = dqt_sc[i]
            dfs = [dfa, dfb]
            for h in range(2):
                kaug, vh, kth = heads[h]
                arg = _dot_nt(kaug, qaug_sc[h, rows, :])
                if masked:
                    rr = lax.broadcasted_iota(jnp.int32, (T, T), 0)
                    cc = lax.broadcasted_iota(jnp.int32, (T, T), 1)
                    arg = jnp.where(rr <= cc, arg, NEG)
                p_t = jnp.exp(arg)
                ds_t = p_t * (_dot_nt(vh, do) - stat[h:h + 1, :])
                ds_bf = ds_t.astype(BF16)
                dv_acc = dv_acc + _dot(p_t.astype(BF16), jnp.where(head_sel[h], do, zero))
                dk_acc = dk_acc + _dot(ds_bf, jnp.where(head_sel[h], q_ref[rows, :], zero))
                dqt = dqt + _dot(kth, ds_bf)
                dfs[h] = dfs[h] + jnp.sum(ds_t, axis=1, keepdims=True)
                dfq_ref[i, h:h + 1, :] += _colsum(ds_t)
            dqt_sc[i] = dqt
            return dk_acc, dv_acc, dfs[0], dfs[1]

        acc0 = (jnp.zeros((T, 128), F32), jnp.zeros((T, 128), F32), jnp.zeros((T, 1), F32), jnp.zeros((T, 1), F32))
        acc1 = step(j, True, acc0)
        dk_acc, dv_acc, dfa, dfb = lax.fori_loop(j + 1, n_t, lambda i, a: step(i, False, a), acc1)
        dk_ref[...] = dk_acc.astype(BF16)
        dv_ref[...] = dv_acc.astype(BF16)
        dfk_ref[...] = -jnp.where(lane == 0, dfa, jnp.where(lane == 1, dfb, 0.0))
        cs_ref[:, 128:256] = cs_ref[:, 128:256] + _colsum(dk_acc)
        cs_ref[:, 256:384] = cs_ref[:, 256:384] + _colsum(dv_acc)

        @pl.when(j == n_t - 1)
        def _():
            def finish(i, tot):
                dq = dqt_sc[i].T
                dq_ref[pl.ds(pl.multiple_of(i * T, T), T), :] = dq.astype(BF16)
                return tot + _colsum(dq)

            cs_ref[:, 0:128] = lax.fori_loop(0, n_t, finish, jnp.zeros((1, 128), F32))

    pair_rows = lambda hp, j: (hp, 0, 0)
    return pl.pallas_call(
        body, name="attention_bwd", grid=(N_PAIR, n_t),
        out_shape=(jax.ShapeDtypeStruct((S, D_ATT), BF16), jax.ShapeDtypeStruct((S, D_ATT), BF16),
                   jax.ShapeDtypeStruct((S, D_ATT), BF16), jax.ShapeDtypeStruct((N_PAIR, 1, 384), F32),
                   jax.ShapeDtypeStruct((N_PAIR, S, 128), F32),
                   jax.ShapeDtypeStruct((N_PAIR, n_t, 8, T), F32)),
        in_specs=[pl.BlockSpec((S, 128), lambda hp, j: (0, hp)),
                  pl.BlockSpec((S, 128), lambda hp, j: (0, hp)),
                  pl.BlockSpec((S, 128), lambda hp, j: (0, hp)),
                  pl.BlockSpec((None, n_t, 8, T), lambda hp, j: (hp, 0, 0, 0)),
                  pl.BlockSpec((T, 128), lambda hp, j: (j, N_PAIR + hp)),
                  pl.BlockSpec((T, 128), lambda hp, j: (j, 2 * N_PAIR + hp)),
                  pl.BlockSpec((T, 128), lambda hp, j: (j, 0))],
        out_specs=(pl.BlockSpec((S, 128), lambda hp, j: (0, hp)),
                   pl.BlockSpec((T, 128), lambda hp, j: (j, hp)),
                   pl.BlockSpec((T, 128), lambda hp, j: (j, hp)),
                   pl.BlockSpec((None, 1, 384), pair_rows),
                   pl.BlockSpec((None, T, 128), lambda hp, j: (hp, j, 0)),
                   pl.BlockSpec((None, n_t, 8, T), lambda hp, j: (hp, 0, 0, 0))),
        scratch_shapes=[pltpu.VMEM((n_t, 8, T), F32), pltpu.VMEM((n_t, 128, T), F32),
                        pltpu.VMEM((2, S, 128), BF16)],
        compiler_params=_params(dimension_semantics=("arbitrary", "arbitrary")),
    )(qkv, datt, att, lse, qkv, qkv, big_f)


def _window_counts(first_row, n_rows, window):
    t = lax.broadcasted_iota(jnp.int32, (n_rows, 1), 0) + first_row
    return jnp.minimum((t + 1).astype(F32), float(window))


def _middle(x, tgt, att, g, p, gate, w_mix, b_mix, pool_scale, w_out, b_out, ln_g, ln_b):
    S = x.shape[0]
    tm = min(TM_MID, S)
    halo_blocks = tm // POOL_HALO

    def body(x_ref, t_ref, att_ref, g_ref, p_ref, ph_ref, gate_ref, wm_ref, bm_ref, ps_ref, wo_ref, bo_ref,
             lg_ref, lb_ref,
             dh_ref, datt_ref, dg_ref, dpl_ref, gwo_ref, gwm_ref, vec_ref, loss_ref):
        i = pl.program_id(0)

        @pl.when(i == 0)
        def _():
            gwo_ref[...] = jnp.zeros_like(gwo_ref)
            gwm_ref[...] = jnp.zeros_like(gwm_ref)
            vec_ref[...] = jnp.zeros_like(vec_ref)
            loss_ref[...] = jnp.zeros_like(loss_ref)

        pc = p_ref[...]
        halo = jnp.where(i > 0, ph_ref[...], 0.0)
        pe = jnp.concatenate([halo, pc], axis=0)
        pooled_parts = []
        for gi, w in enumerate(POOL_WINDOWS):
            cur = pe[:, gi * POOL_GROUP:(gi + 1) * POOL_GROUP]
            span = 1
            while span < w:
                cur = cur + pltpu.roll(cur, span, 0)
                span *= 2
            wsum = cur[POOL_HALO:, :]
            mean = wsum / _window_counts(i * tm, tm, w)
            pooled_parts.append(mean - pc[:, gi * POOL_GROUP:(gi + 1) * POOL_GROUP])
        pooled_bf =[v.astype(BF16) for v in pooled_parts]
        mixed = jnp.concatenate([_dot(pooled_bf[gi], wm_ref[gi]) for gi in range(4)], axis=1) + bm_ref[...]
        ps = ps_ref[...]
        pool_out = mixed * ps
        gv = g_ref[...]
        sig = _sigmoid(gv)
        silu = gv * sig
        att = att_ref[...]
        y = jnp.concatenate([att * silu[:, :D_ATT], pool_out * silu[:, D_ATT:]], axis=1)
        y_bf = y.astype(BF16)
        wo = wo_ref[...]
        yo = _dot(y_bf, wo) + bo_ref[...]
        gate = gate_ref[...]
        h = ALPHA * x_ref[...] + gate * yo
        mu = jnp.mean(h, axis=1, keepdims=True)
        hc = h - mu
        var = jnp.mean(hc * hc, axis=1, keepdims=True)
        rstd = lax.rsqrt(var + LN_EPS)
        yhat = hc * rstd
        lg = lg_ref[...]
        out = yhat * lg + lb_ref[...]
        err = out - t_ref[...]
        loss_ref[...] += 0.5 * jnp.sum(jnp.mean(err * err, axis=1, keepdims=True), axis=0, keepdims=True)

        dout = err * (1.0 / D)
        g_ln_b = _colsum(dout)
        g_ln_g = _colsum(dout * yhat)
        dyh = dout * lg
        dh = rstd * (dyh - jnp.mean(dyh, axis=1, keepdims=True)
                     - yhat * jnp.mean(dyh * yhat, axis=1, keepdims=True))
        dh_ref[...] = dh
        d_gate = _colsum(dh * yo)
        dyo = gate * dh
        g_b_out = _colsum(dyo)
        dyo_bf = dyo.astype(BF16)
        gwo_ref[...] += _dot_tn(y_bf, dyo_bf)
        dy = _dot_nt(dyo_bf, wo)
        dsilu = sig * (1.0 + gv * (1.0 - sig))
        dy_a = dy[:, :D_ATT]
        dy_p = dy[:, D_ATT:]
        datt_ref[...] = (dy_a * silu[:, :D_ATT]).astype(BF16)
        dpo = dy_p * silu[:, D_ATT:]
        dg = jnp.concatenate([dy_a * att * dsilu[:, :D_ATT], dy_p * pool_out * dsilu[:, D_ATT:]], axis=1)
        dg_ref[...] = dg.astype(BF16)
        g_dg = _colsum(dg)
        g_ps = _colsum(dpo * mixed)
        dmixed = dpo * ps
        g_bm = _colsum(dmixed)
        dmixed_bf = dmixed.astype(BF16)
        dpl = []
        for gi in range(4):
            dm = dmixed_bf[:, gi * POOL_GROUP:(gi + 1) * POOL_GROUP]
            gwm_ref[gi] += _dot_tn(pooled_bf[gi], dm)
            dpl.append(_dot_nt(dm, wm_ref[gi]))
        dpl_ref[...] = jnp.concatenate(dpl, axis=1)
        vec_ref[0:1, :] += g_ln_g
        vec_ref[1:2, :] += g_ln_b
        vec_ref[2:3, :] += d_gate
        vec_ref[3:4, :] += g_b_out
        vec_ref[4:5, :] += g_dg
        vec_ref[5:6, 0:D_POOL] += g_ps
        vec_ref[6:7, 0:D_POOL] += g_bm

    row = lambda w: pl.BlockSpec((tm, w), lambda i: (i, 0))
    full2 = lambda a: pl.BlockSpec(a.shape, lambda i: (0, 0))
    full3 = lambda a: pl.BlockSpec(a.shape, lambda i: (0, 0, 0))
    return pl.pallas_call(
        body, name="middle", grid=(S // tm,),
        out_shape=(jax.ShapeDtypeStruct((S, D), F32),
                   jax.ShapeDtypeStruct((S, D_ATT), BF16),
                   jax.ShapeDtypeStruct((S, D), BF16),
                   jax.ShapeDtypeStruct((S, D_POOL), F32),
                   jax.ShapeDtypeStruct((D, D), F32),
                   jax.ShapeDtypeStruct((4, POOL_GROUP, POOL_GROUP), F32),
                   jax.ShapeDtypeStruct((8, D), F32),
                   jax.ShapeDtypeStruct((1, 1), F32)),
        in_specs=[row(D), row(D), row(D_ATT), row(D), row(D_POOL),
                  pl.BlockSpec((POOL_HALO, D_POOL), lambda i: (jnp.maximum(i * halo_blocks - 1, 0), 0)),
                  full2(gate), full3(w_mix), full2(b_mix), full2(pool_scale), full2(w_out), full2(b_out),
                  full2(ln_g), full2(ln_b)],
        out_specs=(row(D), row(D_ATT), row(D), row(D_POOL),
                   pl.BlockSpec((D, D), lambda i: (0, 0)),
                   pl.BlockSpec((4, POOL_GROUP, POOL_GROUP), lambda i: (0, 0, 0)),
                   pl.BlockSpec((8, D), lambda i: (0, 0)),
                   pl.BlockSpec((1, 1), lambda i: (0, 0))),
        compiler_params=_params(dimension_semantics=("arbitrary",)),
    )(x, tgt, att, g, p, p, gate, w_mix, b_mix, pool_scale, w_out, b_out, ln_g, ln_b)


def _tail(dpl, dfk8, f):
    S = dpl.shape[0]
    tm = min(TM_TAIL, S)
    n_t = S // tm
    halo_blocks = tm // POOL_HALO
    last_halo = S // POOL_HALO - 1

    def body(d_ref, dn_ref, dfk_ref, f_ref, dp_ref, df_ref, cs_ref, carry):
        s = pl.program_id(0)
        i = n_t - 1 - s

        @pl.when(s == 0)
        def _():
            carry[...] = jnp.zeros_like(carry)
            cs_ref[...] = jnp.zeros_like(cs_ref)

        dc = d_ref[...]
        nxt = jnp.where(s > 0, dn_ref[...], 0.0)
        de = jnp.concatenate([dc, nxt], axis=0)
        n_e = tm + POOL_HALO
        parts = []
        for gi, w in enumerate(POOL_WINDOWS):
            cur = de[:, gi * POOL_GROUP:(gi + 1) * POOL_GROUP] / _window_counts(i * tm, n_e, w)
            span = 1
            while span < w:
                cur = cur + pltpu.roll(cur, n_e - span, 0)
                span *= 2
            parts.append(cur[:tm, :] - dc[:, gi * POOL_GROUP:(gi + 1) * POOL_GROUP])
        dp = jnp.concatenate(parts, axis=1)
        dp_ref[...] = dp.astype(BF16)
        cs_ref[0:1, :] += _colsum(dp)

        r = lax.broadcasted_iota(jnp.int32, (tm, tm), 0)
        c = lax.broadcasted_iota(jnp.int32, (tm, tm), 1)
        tri = (r <= c).astype(F32)
        dlogf = jnp.dot(tri, dfk_ref[...], preferred_element_type=F32, precision=lax.Precision.HIGHEST) + carry[...]
        carry[...] = dlogf[0:1, :]
        df = dlogf * _sigmoid(-f_ref[...])
        df_ref[...] = df.astype(BF16)
        cs_ref[1:2, 0:128] += _colsum(df)

    rev = lambda w: pl.BlockSpec((tm, w), lambda s: (n_t - 1 - s, 0))
    return pl.pallas_call(
        body, name="tail", grid=(n_t,),
        out_shape=(jax.ShapeDtypeStruct((S, D_POOL), BF16), jax.ShapeDtypeStruct((S, 128), BF16),
                   jax.ShapeDtypeStruct((8, D_POOL), F32)),
        in_specs=[rev(D_POOL),
                  pl.BlockSpec((POOL_HALO, D_POOL),
                               lambda s: (jnp.minimum((n_t - s) * halo_blocks, last_halo), 0)),
                  rev(128), rev(128)],
        out_specs=(rev(D_POOL), rev(128), pl.BlockSpec((8, D_POOL), lambda s: (0, 0))),
        scratch_shapes=[pltpu.VMEM((1, 128), F32)],
        compiler_params=_params(dimension_semantics=("arbitrary",)),
    )(dpl, dpl, dfk8, f)


PIECES = ((O_QKV, D_ATT), (O_QKV + D_ATT, D_ATT), (O_QKV + 2 * D_ATT, D_ATT), (O_F, 128), (O_P, D_POOL), (O_G, D))


def _grad_w_in(u, pieces):
    S = u.shape[0]
    tm = min(TM_GW, S)
    n_t = S // tm

    def body(u_ref, *rest):
        piece_refs, out_ref, acc, sem = rest[:6], rest[6], rest[7], rest[8]
        i = pl.program_id(0)

        @pl.when(i == 0)
        def _():
            acc[...] = jnp.zeros_like(acc)

        u_t = u_ref[...]
        for (off, w), ref in zip(PIECES, piece_refs):
            acc[:, off:off + w] += _dot_tn(u_t, ref[...])

        @pl.when(i == n_t - 1)
        def _():
            cp = pltpu.make_async_copy(acc, out_ref, sem)
            cp.start()
            cp.wait()

    return pl.pallas_call(
        body, name="grad_w_in", grid=(n_t,),
        out_shape=jax.ShapeDtypeStruct((D, D_PAD), F32),
        in_specs=[pl.BlockSpec((tm, D), lambda i: (i, 0))]
        + [pl.BlockSpec((tm, w), lambda i: (i, 0)) for _, w in PIECES],
        out_specs=pl.BlockSpec(memory_space=pl.ANY),
        scratch_shapes=[pltpu.VMEM((D, D_PAD), F32), pltpu.SemaphoreType.DMA],
        compiler_params=_params(dimension_semantics=("arbitrary",)),
    )(u, *pieces)


def _grad_x(pieces, w_pad, dh, x, scale):
    S = x.shape[0]
    tm = min(TM_DU, S)

    def body(*refs):
        piece_refs = refs[:6]
        w_ref, dh_ref, x_ref, sc_ref, gx_ref, vec_ref = refs[6:]

        @pl.when(pl.program_id(0) == 0)
        def _():
            vec_ref[...] = jnp.zeros_like(vec_ref)

        du = jnp.zeros((tm, D), F32)
        for (off, w), ref in zip(PIECES, piece_refs):
            du = du + _dot_nt(ref[...], w_ref[:, off:off + w])
        xv = x_ref[...]
        gx_ref[...] = ALPHA * dh_ref[...] + du * (1.0 + sc_ref[...])
        vec_ref[0:1, :] += _colsum(du)
        vec_ref[1:2, :] += _colsum(du * xv)

    row = lambda w: pl.BlockSpec((tm, w), lambda i: (i, 0))
    return pl.pallas_call(
        body, name="grad_x", grid=(S // tm,),
        out_shape=(jax.ShapeDtypeStruct((S, D), F32), jax.ShapeDtypeStruct((8, D), F32)),
        in_specs=[row(w) for _, w in PIECES]
        + [pl.BlockSpec(w_pad.shape, lambda i: (0, 0)), row(D), row(D), pl.BlockSpec((1, D), lambda i: (0, 0))],
        out_specs=(row(D), pl.BlockSpec((8, D), lambda i: (0, 0))),
        compiler_params=_params(dimension_semantics=("arbitrary",)),
    )(*pieces, w_pad, dh, x, scale)


def _grad_ada(c_all, dada_all, dada_cols):
    def body(c_ref, dall_ref, dcol_ref, gw_ref, gb_ref):
        rows = lax.broadcasted_iota(jnp.int32, (8, 1), 0)
        cm = jnp.zeros((8, D), F32)
        dm = jnp.zeros((8, 3 * D), F32)
        for r in range(8):
            cm = jnp.where(rows == r, c_ref[r], cm)
            dm = jnp.where(rows == r, dall_ref[r], dm)
        act = cm * _sigmoid(cm)
        pad = jnp.zeros((8, D), F32)
        lhs = jnp.concatenate([act, pad], axis=0).astype(BF16)
        rhs = jnp.concatenate([dcol_ref[...], jnp.zeros((8, SHARD_ADA), F32)], axis=0).astype(BF16)
        gw_ref[...] = _dot_tn(lhs, rhs)
        gb_ref[...] = _colsum(dm)

    vm = pl.BlockSpec(memory_space=pltpu.VMEM)
    return pl.pallas_call(
        body, name="grad_ada",
        out_shape=(jax.ShapeDtypeStruct((D, SHARD_ADA), F32), jax.ShapeDtypeStruct((1, 3 * D), F32)),
        in_specs=[vm, vm, vm], out_specs=(vm, vm),
        compiler_params=_params(),
    )(c_all, dada_all, dada_cols)


def _adamw_math(w, g, m, v):
    m = ADAM_B1 * m + (1.0 - ADAM_B1) * g
    v = ADAM_B2 * v + (1.0 - ADAM_B2) * (g * g)
    m_hat = m / (1.0 - ADAM_B1 ** ADAM_STEP)
    v_hat = v / (1.0 - ADAM_B2 ** ADAM_STEP)
    delta = -ADAM_LR * (m_hat / (jnp.sqrt(v_hat) + ADAM_EPS) + ADAM_WD * w)
    return delta, m, v


def _adamw(groups, n_steps):
    n = len(groups)

    def body(*refs):
        ins, outs = refs[:4 * n], refs[4 * n:]
        for t in range(n):
            w, g, m, v = (r[...] for r in ins[4 * t:4 * t + 4])
            d, m2, v2 = _adamw_math(w, g, m, v)
            outs[3 * t][...] = d
            outs[3 * t + 1][...] = m2
            outs[3 * t + 2][...] = v2

    in_specs, out_specs, out_shape, args = [], [], [], []
    for (w, g, m, v) in groups:
        r, c = w.shape
        spec = pl.BlockSpec((r // n_steps, c), lambda i: (i, 0))
        in_specs += [spec] * 4
        out_specs += [spec] * 3
        out_shape += [jax.ShapeDtypeStruct((r, c), F32)] * 3
        args += [w, g, m, v]
    return pl.pallas_call(
        body, name="adamw_%d" % n, grid=(n_steps,),
        out_shape=tuple(out_shape), in_specs=in_specs, out_specs=tuple(out_specs),
        compiler_params=_params(dimension_semantics=("arbitrary",)),
    )(*args)


def _pack_small(parts):
    rows = []
    used = 0
    for name, (first, n_rows) in SMALL_SEGS.items():
        if first > used:
            rows.append(jnp.zeros((first - used, 128), F32))
        flat = parts[name].reshape(-1)
        flat = jnp.pad(flat, (0, n_rows * 128 - flat.shape[0]))
        rows.append(flat.reshape(n_rows, 128))
        used = first + n_rows
    rows.append(jnp.zeros((SMALL_ROWS - used, 128), F32))
    return jnp.concatenate(rows, axis=0)


def _unpack_small(buf, name, shape):
    first, n_rows = SMALL_SEGS[name]
    n = int(np.prod(shape))
    return buf[first:first + n_rows].reshape(-1)[:n].reshape(shape)


def _pad_in(v):
    r = v.shape[0]
    z = jnp.zeros((r, O_P - O_F - N_HEADS), v.dtype)
    return jnp.concatenate([v[:, :3 * D_ATT + N_HEADS], z, v[:, 3 * D_ATT + N_HEADS:]], axis=1)


def _unpad_in(v):
    return jnp.concatenate([v[:, :O_F + N_HEADS], v[:, O_P:]], axis=1)


def _shards_in(v):
    gap = O_P - (O_F + N_HEADS)
    parts = []
    for a in range(N_CHIPS):
        lo, hi = a * SHARD_IN, (a + 1) * SHARD_IN
        cut = O_F + N_HEADS
        if hi <= cut:
            parts.append(v[:, lo:hi])
        elif lo >= cut:
            parts.append(v[:, lo + gap:hi + gap])
        else:
            parts.append(jnp.concatenate([v[:, lo:cut], v[:, cut + gap:hi + gap]], axis=1))
    return jnp.stack(parts, axis=0)


def kernel(x, c, w_ada, b_ada, w_in, b_in, w_pool_mix, b_pool_mix, pool_scale, w_out, b_out, ln_g, ln_b, loss_target, m_w_ada, m_b_ada, m_w_in, m_b_in, m_w_pool_mix, m_b_pool_mix, m_pool_scale, m_w_out, m_b_out, m_ln_g, m_ln_b, v_w_ada, v_b_ada, v_w_in, v_b_in, v_w_pool_mix, v_b_pool_mix, v_pool_scale, v_w_out, v_b_out, v_ln_g, v_ln_b):
    S = x.shape[1]
    T = min(T_ATT, S)
    n_t = S // T
    chip = 2 * lax.axis_index("x") + lax.axis_index("y")
    x2 = x[0]
    tgt = loss_target[0]
    q_scale = jnp.concatenate([jnp.full((1, D_ATT), Q_SCALE, F32), jnp.ones((1, D_PAD - D_ATT), F32)], axis=1)

    c_all, ada4, w_in_all, w_out_all = _gather_and_ada(
        c, w_ada[0], b_ada.reshape(4, 1, SHARD_ADA), w_in[0].astype(BF16), w_out[0].astype(BF16))
    ada = ada4[:, 0, :].reshape(1, 3 * D)
    shift, scale, gate = ada[:, :D], ada[:, D:2 * D], ada[:, 2 * D:]
    w_in_full = jnp.transpose(w_in_all, (1, 0, 2)).reshape(D, D_IN)
    w_pad = _pad_in(w_in_full) * q_scale.astype(BF16)
    b_pad = _pad_in(b_in) * q_scale
    w_out_full = w_out_all.reshape(D, D)
    w_mix_bf = w_pool_mix[0].astype(BF16)

    u, qkv, f, p, g = _in_proj(x2, shift, scale, w_pad, b_pad)
    big_f = _forget_cumsum(f)
    att, lse = _attention_fwd(qkv, big_f)

    dh, datt, dg, dpl, gw_out, gw_mix, vec, loss_part = _middle(
        x2, tgt, att, g, p, gate, w_mix_bf, b_pool_mix.reshape(1, D_POOL), pool_scale, w_out_full, b_out, ln_g, ln_b)
    dq, dk, dv, cs_att, dfk, dfq = _attention_bwd(qkv, datt, att, lse, big_f)
    dfk8 = jnp.transpose(dfk[:, :, 0:2], (1, 0, 2)).reshape(S, N_HEADS)
    dfk8 = dfk8 + jnp.transpose(dfq[:, :, 0:2, :], (1, 3, 0, 2)).reshape(S, N_HEADS)
    dfk8 = jnp.pad(dfk8, ((0, 0), (0, 128 - N_HEADS)))
    dp, df, cs_tail = _tail(dpl, dfk8, f)
    pieces = (dq, dk, dv, df, dp, dg)
    gw_pad = _grad_w_in(u, pieces)
    grad_x, vec_x = _grad_x(pieces, w_pad, dh, x2, scale)

    cs_qkv = jnp.transpose(cs_att.reshape(N_PAIR, 3, 128), (1, 0, 2)).reshape(1, 3 * D_ATT)
    gb_pad = jnp.concatenate([cs_qkv, cs_tail[1:2, 0:128], cs_tail[0:1, :], vec[4:5, :]], axis=1) * q_scale
    dada = jnp.concatenate([vec_x[0:1, :], vec_x[1:2, :], vec[2:3, :]], axis=1)
    small = _pack_small({
        "b_in": gb_pad, "w_pool_mix": gw_mix, "b_pool_mix": vec[6:7, :D_POOL], "pool_scale": vec[5:6, :D_POOL],
        "b_out": vec[3:4, :], "ln_g": vec[0:1, :], "ln_b": vec[1:2, :], "b_ada": jnp.zeros((1, 3 * D), F32)})

    g_w_in, g_w_out, small_sum, dada_all = _reduce_all(
        _shards_in(gw_pad), _shards_in(q_scale), gw_out.reshape(N_CHIPS, SHARD_OUT, D),
        jnp.ones((N_CHIPS, 1, D), F32), small[:SMALL_REDUCED_ROWS], dada)
    dada_cols = lax.dynamic_slice(dada_all[:, 0, :], (0, chip * SHARD_ADA), (8, SHARD_ADA))
    g_w_ada, g_b_ada = _grad_ada(c_all, dada_all, dada_cols)
    loss = lax.psum(loss_part[0, 0], ("x", "y", "c"))

    grads_small = jnp.concatenate([small_sum, g_b_ada.reshape(24, 128)], axis=0)
    small_w = {"b_in": _pad_in(b_in), "w_pool_mix": w_pool_mix, "b_pool_mix": b_pool_mix, "pool_scale": pool_scale,
               "b_out": b_out, "ln_g": ln_g, "ln_b": ln_b, "b_ada": b_ada}
    small_m = {"b_in": _pad_in(m_b_in), "w_pool_mix": m_w_pool_mix, "b_pool_mix": m_b_pool_mix,
               "pool_scale": m_pool_scale, "b_out": m_b_out, "ln_g": m_ln_g, "ln_b": m_ln_b, "b_ada": m_b_ada}
    small_v = {"b_in": _pad_in(v_b_in), "w_pool_mix": v_w_pool_mix, "b_pool_mix": v_b_pool_mix,
               "pool_scale": v_pool_scale, "b_out": v_b_out, "ln_g": v_ln_g, "ln_b": v_ln_b, "b_ada": v_b_ada}
    big = _adamw([(w_ada[0], g_w_ada, m_w_ada[0], v_w_ada[0]),
                  (w_in[0], g_w_in, m_w_in[0], v_w_in[0]),
                  (w_out[0], g_w_out, m_w_out[0], v_w_out[0])], 8)
    sm = _adamw([(_pack_small(small_w), grads_small, _pack_small(small_m), _pack_small(small_v))], 1)

    names = ["w_ada", "b_ada", "w_in", "b_in", "w_pool_mix", "b_pool_mix", "pool_scale", "w_out", "b_out",
             "ln_g", "ln_b"]
    shapes = {"b_ada": (1, 3 * D), "b_in": (1, D_PAD), "w_pool_mix": (1, 4, POOL_GROUP, POOL_GROUP),
              "b_pool_mix": (1, 4, POOL_GROUP), "pool_scale": (1, D_POOL), "b_out": (1, D), "ln_g": (1, D),
              "ln_b": (1, D)}
    big_idx = {"w_ada": 0, "w_in": 1, "w_out": 2}

    def leaf(kind, name):
        if name in big_idx:
            if kind == 0:
                return (g_w_ada, g_w_in, g_w_out)[big_idx[name]][None]
            return big[3 * big_idx[name] + kind - 1][None]
        buf = grads_small if kind == 0 else sm[kind - 1]
        val = _unpack_small(buf, name, shapes[name])
        if name == "b_in":
            val = _unpad_in(val)
        return val

    outs = [loss, grad_x[None]]
    for kind in range(4):
        outs += [leaf(kind, n) for n in names]
    return tuple(outs)
```

```python
import functools

import numpy as np
import jax
import jax.numpy as jnp
from jax import lax
from jax.experimental import pallas as pl
from jax.experimental.pallas import tpu as pltpu

F32 = jnp.float32
BF16 = jnp.bfloat16
MESH = pl.DeviceIdType.MESH

D = 1024
D_ATT = 512
D_POOL = 512
N_HEADS = 8
HEAD_DIM = 64
N_PAIR = N_HEADS // 2
POOL_WINDOWS = (2, 4, 8, 16)
POOL_GROUP = 128
POOL_HALO = 16
LN_EPS = 1e-5
ALPHA = 2.0 ** 0.25
D_IN = 3 * D_ATT + N_HEADS + D_POOL + D_ATT + D_POOL
N_CHIPS = 4
SHARD_IN = D_IN // N_CHIPS
SHARD_ADA = 3 * D // N_CHIPS
SHARD_OUT = D // N_CHIPS

O_QKV, O_F, O_P, O_G, D_PAD = 0, 1536, 1664, 2176, 3200
Q_SCALE = HEAD_DIM ** -0.5

ADAM_LR, ADAM_B1, ADAM_B2, ADAM_EPS, ADAM_WD, ADAM_STEP = 0.001, 0.9, 0.999, 1e-08, 0.01, 10

NEG = -1e30

VMEM_LIMIT = 56 * 1024 * 1024

TM_PROJ = 512
T_ATT = 512
ATT_CHUNK = 32
TM_MID = 256
TM_TAIL = 512
TM_GW = 512
TM_DU = 512

REL7 = [(0, 0, 1), (0, 1, 0), (0, 1, 1), (1, 0, 0), (1, 0, 1), (1, 1, 0), (1, 1, 1)]
REL3 = [(0, 1), (1, 0), (1, 1)]

SMALL_SEGS = {}
_row = 0
for _name, _n in (("b_in", 3200), ("w_pool_mix", 65536), ("b_pool_mix", 512), ("pool_scale", 512),
                  ("b_out", 1024), ("ln_g", 1024), ("ln_b", 1024)):
    _rows = -(-_n // 1024) * 8
    SMALL_SEGS[_name] = (_row, _rows)
    _row += _rows
SMALL_REDUCED_ROWS = -(-_row // 16) * 16
SMALL_SEGS["b_ada"] = (SMALL_REDUCED_ROWS, 24)
SMALL_ROWS = SMALL_REDUCED_ROWS + 24


def _params(**kw):
    return pltpu.CompilerParams(vmem_limit_bytes=VMEM_LIMIT, **kw)


def _flip(v, d):
    return v if d == 0 else 1 - v


def _dot(a, b):
    return jnp.dot(a, b, preferred_element_type=F32)


def _dot_nt(a, b):
    return lax.dot_general(a, b, (((1,), (1,)), ((), ())), preferred_element_type=F32)


def _dot_tn(a, b):
    return lax.dot_general(a, b, (((0,), (0,)), ((), ())), preferred_element_type=F32)


def _sigmoid(v):
    return 1.0 / (1.0 + jnp.exp(-v))


def _colsum(v):
    return jnp.sum(v, axis=0, keepdims=True)


def _gather_and_ada(c, w_ada, b_ada4, w_in_sh, w_out_sh):
    def body(c_ref, w_ref, b_ref, win_ref, wout_ref, call_ref, ada_ref, win_all, wout_all,
             cslab, sbuf, rbuf, cs_sem, cr_sem, as_sem, ar_sem, own_sem, s_sem, r_sem, fs_sem, fr_sem):
        x, y, cc = lax.axis_index("x"), lax.axis_index("y"), lax.axis_index("c")
        me = 4 * x + 2 * y + cc
        chip = 2 * x + y
        sib = (x, y, 1 - cc)
        srcs = (win_ref, wout_ref)
        dsts = (win_all, wout_all)
        halves = (D // 2, SHARD_OUT // 2)

        def rows(t, which):
            h = halves[t]
            return pl.ds(pl.multiple_of(which * h, h), h)

        own = [pltpu.make_async_copy(srcs[t], dsts[t].at[chip], own_sem.at[t]) for t in range(2)]
        for cp in own:
            cp.start()
        first = []
        for k, (dx, dy) in enumerate(REL3):
            peer = (_flip(x, dx), _flip(y, dy), cc)
            for t in range(2):
                cp = pltpu.make_async_remote_copy(
                    src_ref=srcs[t].at[rows(t, cc), :], dst_ref=dsts[t].at[chip, rows(t, cc), :],
                    send_sem=s_sem.at[2 * k + t], recv_sem=r_sem.at[2 * k + t],
                    device_id=peer, device_id_type=MESH)
                cp.start()
                first.append(cp)

        cslab[...] = jnp.broadcast_to(c_ref[...], (8, D))
        call_ref[me] = cslab[...]
        gathers = []
        for k, (dx, dy, dc) in enumerate(REL7):
            cp = pltpu.make_async_remote_copy(
                src_ref=cslab, dst_ref=call_ref.at[me], send_sem=cs_sem.at[k], recv_sem=cr_sem.at[k],
                device_id=(_flip(x, dx), _flip(y, dy), _flip(cc, dc)), device_id_type=MESH)
            cp.start()
            gathers.append(cp)
        for cp in gathers:
            cp.wait()
        slab_row = lax.broadcasted_iota(jnp.int32, (8, 1), 0)
        mat = jnp.zeros((8, D), F32)
        for r in range(8):
            mat = jnp.where(slab_row == r, call_ref[r], mat)
        act = (mat * _sigmoid(mat)).astype(BF16)
        part = _dot(act, w_ref[...].astype(BF16))
        sends = []
        for k, (dx, dy) in enumerate(REL3):
            px, py = _flip(x, dx), _flip(y, dy)
            r = 4 * px + 2 * py + cc
            piece = _colsum(jnp.where(slab_row == r, part, 0.0))
            sbuf[k] = jnp.broadcast_to(piece, (8, SHARD_ADA))
            cp = pltpu.make_async_remote_copy(
                src_ref=sbuf.at[k], dst_ref=rbuf.at[k], send_sem=as_sem.at[k], recv_sem=ar_sem.at[k],
                device_id=(px, py, cc), device_id_type=MESH)
            cp.start()
            sends.append(cp)
        own_piece = _colsum(jnp.where(slab_row == me, part, 0.0))
        ada_ref[chip] = jnp.broadcast_to(own_piece, (8, SHARD_ADA)) + b_ref[chip]
        for k, (dx, dy) in enumerate(REL3):
            sends[k].wait()
            a = 2 * _flip(x, dx) + _flip(y, dy)
            ada_ref[a] = rbuf[k] + b_ref[a]

        passed = []
        for k, (dx, dy) in enumerate(REL3):
            a = 2 * _flip(x, dx) + _flip(y, dy)
            for t in range(2):
                landed = dsts[t].at[a, rows(t, cc), :]
                pltpu.make_async_remote_copy(
                    src_ref=landed, dst_ref=landed, send_sem=s_sem.at[2 * k + t], recv_sem=r_sem.at[2 * k + t],
                    device_id=sib, device_id_type=MESH).wait_recv()
                cp = pltpu.make_async_remote_copy(
                    src_ref=landed, dst_ref=landed, send_sem=fs_sem.at[2 * k + t], recv_sem=fr_sem.at[2 * k + t],
                    device_id=sib, device_id_type=MESH)
                cp.start()
                passed.append(cp)
        for k, (dx, dy) in enumerate(REL3):
            a = 2 * _flip(x, dx) + _flip(y, dy)
            for t in range(2):
                other = dsts[t].at[a, rows(t, 1 - cc), :]
                pltpu.make_async_remote_copy(
                    src_ref=other, dst_ref=other, send_sem=fs_sem.at[2 * k + t], recv_sem=fr_sem.at[2 * k + t],
                    device_id=sib, device_id_type=MESH).wait_recv()
        for cp in first + passed:
            cp.wait_send()
        for cp in own:
            cp.wait()

    vm = pl.BlockSpec(memory_space=pltpu.VMEM)
    return pl.pallas_call(
        body, name="gather_and_ada",
        out_shape=(jax.ShapeDtypeStruct((8, 8, D), F32), jax.ShapeDtypeStruct((4, 8, SHARD_ADA), F32),
                   jax.ShapeDtypeStruct((N_CHIPS, D, SHARD_IN), BF16),
                   jax.ShapeDtypeStruct((N_CHIPS, SHARD_OUT, D), BF16)),
        in_specs=[vm] * 5, out_specs=(vm,) * 4,
        scratch_shapes=[pltpu.VMEM((8, D), F32), pltpu.VMEM((3, 8, SHARD_ADA), F32),
                        pltpu.VMEM((3, 8, SHARD_ADA), F32),
                        pltpu.SemaphoreType.DMA((7,)), pltpu.SemaphoreType.DMA((7,)),
                        pltpu.SemaphoreType.DMA((3,)), pltpu.SemaphoreType.DMA((3,)),
                        pltpu.SemaphoreType.DMA((2,)), pltpu.SemaphoreType.DMA((6,)),
                        pltpu.SemaphoreType.DMA((6,)), pltpu.SemaphoreType.DMA((6,)),
                        pltpu.SemaphoreType.DMA((6,))],
        compiler_params=_params(),
    )(c, w_ada, b_ada4, w_in_sh, w_out_sh)


def _scatter_stages(pos, g_ref, sc_ref, out_ref, sib_buf, send_buf, ici_buf, sem1, sem2s, sem2r, sem3):
    x, y, cc, chip, sib = pos
    RH = g_ref.shape[1] // 2
    mine = pl.ds(pl.multiple_of(cc * RH, RH), RH)
    theirs = pl.ds(pl.multiple_of((1 - cc) * RH, RH), RH)
    cp1 = pltpu.make_async_remote_copy(
        src_ref=g_ref.at[:, theirs, :], dst_ref=sib_buf, send_sem=sem1.at[0], recv_sem=sem1.at[1],
        device_id=sib, device_id_type=MESH)
    sends = []
    for k, (dx, dy) in enumerate(REL3):
        px, py = _flip(x, dx), _flip(y, dy)
        sends.append(pltpu.make_async_remote_copy(
            src_ref=send_buf.at[2 * px + py], dst_ref=ici_buf.at[chip],
            send_sem=sem2s.at[k], recv_sem=sem2r.at[k], device_id=(px, py, cc), device_id_type=MESH))
    cp3 = pltpu.make_async_remote_copy(
        src_ref=out_ref.at[mine, :], dst_ref=out_ref.at[mine, :], send_sem=sem3.at[0], recv_sem=sem3.at[1],
        device_id=sib, device_id_type=MESH)

    def finish1():
        cp1.wait()
        for a in range(N_CHIPS):
            both = g_ref[a, mine, :] + sib_buf[a]
            sib_buf[a] = both
            send_buf[a] = both.astype(BF16)

    def start2():
        for cp in sends:
            cp.start()
        ici_buf[chip] = send_buf[chip]

    def finish2():
        for cp in sends:
            cp.wait()
        own = sib_buf[chip]
        parts = [jnp.where(chip == a, own, ici_buf[a].astype(F32)) for a in range(N_CHIPS)]
        out_ref[mine, :] = ((parts[0] + parts[1]) + (parts[2] + parts[3])) * sc_ref[chip]

    return [(cp1.start, finish1), (start2, finish2), (cp3.start, cp3.wait)]


def _all_reduce_stages(pos, g_ref, out_ref, sib_buf, ici_buf, sem1, sem2s, sem2r, sem3):
    x, y, cc, chip, sib = pos
    RH = g_ref.shape[0] // 2
    mine = pl.ds(pl.multiple_of(cc * RH, 8), RH)
    theirs = pl.ds(pl.multiple_of((1 - cc) * RH, 8), RH)
    cp1 = pltpu.make_async_remote_copy(
        src_ref=g_ref.at[theirs, :], dst_ref=sib_buf, send_sem=sem1.at[0], recv_sem=sem1.at[1],
        device_id=sib, device_id_type=MESH)
    sends = []
    for k, (dx, dy) in enumerate(REL3):
        px, py = _flip(x, dx), _flip(y, dy)
        sends.append(pltpu.make_async_remote_copy(
            src_ref=sib_buf, dst_ref=ici_buf.at[chip],
            send_sem=sem2s.at[k], recv_sem=sem2r.at[k], device_id=(px, py, cc), device_id_type=MESH))
    cp3 = pltpu.make_async_remote_copy(
        src_ref=out_ref.at[mine, :], dst_ref=out_ref.at[mine, :], send_sem=sem3.at[0], recv_sem=sem3.at[1],
        device_id=sib, device_id_type=MESH)

    def finish1():
        cp1.wait()
        sib_buf[...] = g_ref[mine, :] + sib_buf[...]

    def start2():
        for cp in sends:
            cp.start()
        ici_buf[chip] = sib_buf[...]

    def finish2():
        for cp in sends:
            cp.wait()
        out_ref[mine, :] = (ici_buf[0] + ici_buf[1]) + (ici_buf[2] + ici_buf[3])

    return [(cp1.start, finish1), (start2, finish2), (cp3.start, cp3.wait)]


def _reduce_all(g4_in, sc_in, g4_out, sc_out, small, dada):
    shapes = [g4_in.shape[1:], g4_out.shape[1:]]
    R = small.shape[0]
    W = dada.shape[1]
    n_sem = 4

    def body(gin_ref, scin_ref, gout_ref, scout_ref, sm_ref, d_ref, oin_ref, oout_ref, osm_ref, dall_ref, *scratch):
        x, y, cc = lax.axis_index("x"), lax.axis_index("y"), lax.axis_index("c")
        me = 4 * x + 2 * y + cc
        pos = (x, y, cc, 2 * x + y, (x, y, 1 - cc))
        dslab, ds_sem, dr_sem = scratch[0:3]
        rest = scratch[3:]
        in_bufs, rest = rest[0:3 + n_sem], rest[3 + n_sem:]
        out_bufs, rest = rest[0:3 + n_sem], rest[3 + n_sem:]
        sm_bufs = rest
        dslab[...] = jnp.broadcast_to(d_ref[...], (8, W))
        dall_ref[me] = dslab[...]
        gathers = []
        for k, (dx, dy, dc) in enumerate(REL7):
            cp = pltpu.make_async_remote_copy(
                src_ref=dslab, dst_ref=dall_ref.at[me], send_sem=ds_sem.at[k], recv_sem=dr_sem.at[k],
                device_id=(_flip(x, dx), _flip(y, dy), _flip(cc, dc)), device_id_type=MESH)
            cp.start()
            gathers.append(cp)
        plans = [_scatter_stages(pos, gin_ref, scin_ref, oin_ref, *in_bufs),
                 _scatter_stages(pos, gout_ref, scout_ref, oout_ref, *out_bufs),
                 _all_reduce_stages(pos, sm_ref, osm_ref, *sm_bufs)]
        for stage in range(3):
            for plan in plans:
                plan[stage][0]()
            for plan in plans:
                plan[stage][1]()
        for cp in gathers:
            cp.wait()

    def sems():
        return [pltpu.SemaphoreType.DMA((2,)), pltpu.SemaphoreType.DMA((3,)),
                pltpu.SemaphoreType.DMA((3,)), pltpu.SemaphoreType.DMA((2,))]

    scratch = [pltpu.VMEM((8, W), F32), pltpu.SemaphoreType.DMA((7,)), pltpu.SemaphoreType.DMA((7,))]
    for r, c in shapes:
        scratch += [pltpu.VMEM((N_CHIPS, r // 2, c), F32), pltpu.VMEM((N_CHIPS, r // 2, c), BF16),
                    pltpu.VMEM((N_CHIPS, r // 2, c), BF16)] + sems()
    scratch += [pltpu.VMEM((R // 2, 128), F32), pltpu.VMEM((N_CHIPS, R // 2, 128), F32)] + sems()
    vm = pl.BlockSpec(memory_space=pltpu.VMEM)
    return pl.pallas_call(
        body, name="reduce_all",
        out_shape=(jax.ShapeDtypeStruct(g4_in.shape[1:], F32), jax.ShapeDtypeStruct(g4_out.shape[1:], F32),
                   jax.ShapeDtypeStruct((R, 128), F32), jax.ShapeDtypeStruct((8, 8, W), F32)),
        in_specs=[vm] * 6, out_specs=(vm,) * 4,
        scratch_shapes=scratch,
        compiler_params=_params(),
    )(g4_in, sc_in, g4_out, sc_out, small, dada)


def _in_proj(x, shift, scale, w_pad, b_pad):
    S = x.shape[0]
    tm = min(TM_PROJ, S)

    def body(x_ref, sh_ref, sc_ref, w_ref, b_ref, u_ref, qkv_ref, f_ref, p_ref, g_ref):
        u = (x_ref[...] * (1.0 + sc_ref[...]) + sh_ref[...]).astype(BF16)
        u_ref[...] = u
        qkv_ref[...] = (_dot(u, w_ref[:, O_QKV:O_F]) + b_ref[:, O_QKV:O_F]).astype(BF16)
        f_ref[...] = _dot(u, w_ref[:, O_F:O_P]) + b_ref[:, O_F:O_P]
        p_ref[...] = _dot(u, w_ref[:, O_P:O_G]) + b_ref[:, O_P:O_G]
        g_ref[...] = _dot(u, w_ref[:, O_G:D_PAD]) + b_ref[:, O_G:D_PAD]

    row = lambda w: pl.BlockSpec((tm, w), lambda i: (i, 0))
    full = lambda a: pl.BlockSpec(a.shape, lambda i: (0, 0))
    return pl.pallas_call(
        body, name="in_proj", grid=(S // tm,),
        out_shape=(jax.ShapeDtypeStruct((S, D), BF16), jax.ShapeDtypeStruct((S, 3 * D_ATT), BF16),
                   jax.ShapeDtypeStruct((S, 128), F32), jax.ShapeDtypeStruct((S, D_POOL), F32),
                   jax.ShapeDtypeStruct((S, D), F32)),
        in_specs=[row(D), full(shift), full(scale), full(w_pad), full(b_pad)],
        out_specs=(row(D), row(3 * D_ATT), row(128), row(D_POOL), row(D)),
        compiler_params=_params(dimension_semantics=("arbitrary",)),
    )(x, shift, scale, w_pad, b_pad)


def _forget_cumsum(f):
    S = f.shape[0]
    tm = min(T_ATT, S)

    def body(f_ref, out_ref, carry):
        @pl.when(pl.program_id(0) == 0)
        def _():
            carry[...] = jnp.zeros_like(carry)
        v = f_ref[...]
        logf = jnp.minimum(v, 0.0) - jnp.log(1.0 + jnp.exp(-jnp.abs(v)))
        r = lax.broadcasted_iota(jnp.int32, (tm, tm), 0)
        c = lax.broadcasted_iota(jnp.int32, (tm, tm), 1)
        tri = (r >= c).astype(F32)
        cum = jnp.dot(tri, logf, preferred_element_type=F32, precision=lax.Precision.HIGHEST) + carry[...]
        out_ref[...] = cum
        carry[...] = cum[tm - 8:tm, :][7:8, :]

    return pl.pallas_call(
        body, name="forget_cumsum", grid=(S // tm,),
        out_shape=jax.ShapeDtypeStruct((S, 128), F32),
        in_specs=[pl.BlockSpec((tm, 128), lambda i: (i, 0))],
        out_specs=pl.BlockSpec((tm, 128), lambda i: (i, 0)),
        scratch_shapes=[pltpu.VMEM((1, 128), F32)],
        compiler_params=_params(dimension_semantics=("arbitrary",)),
    )(f)


def _pair_select(is_a, va, vb):
    return jnp.where(is_a, va, vb)


def _split3(v):
    hi = v.astype(BF16)
    rest = v - hi.astype(F32)
    mid = rest.astype(BF16)
    lo = (rest - mid.astype(F32)).astype(BF16)
    return hi, mid, lo


def _attention_fwd(qkv, big_f):
    S = qkv.shape[0]
    T = min(T_ATT, S)
    n_t = S // T

    def body(q_ref, k_ref, v_ref, f_ref, o_ref, lse_ref, kaug_sc, vt_sc, m_sc, l_sc, acc_sc):
        hp = pl.program_id(0)
        i = pl.program_id(1)
        lane = lax.broadcasted_iota(jnp.int32, (1, 128), 1)
        sub = lax.broadcasted_iota(jnp.int32, (128, 1), 0)
        head_sel = (lane < HEAD_DIM, lane >= HEAD_DIM)
        head_sel_t = (sub < HEAD_DIM, sub >= HEAD_DIM)
        spare = (HEAD_DIM, 0)
        zero = jnp.zeros((), BF16)

        @pl.when(i == 0)
        def _():
            def prep(jt, carry):
                rows = pl.ds(pl.multiple_of(jt * T, T), T)
                k = k_ref[rows, :]
                ft = f_ref[rows, :]
                vt = v_ref[rows, :].astype(F32).T
                for h in range(2):
                    fh = jnp.sum(jnp.where(lane == 2 * hp + h, ft, 0.0), axis=1, keepdims=True)
                    hi, mid, lo = _split3(-fh)
                    b = spare[h]
                    bias = jnp.where(lane == b, hi, jnp.where(lane == b + 1, mid, jnp.where(lane == b + 2, lo, zero)))
                    kaug_sc[h, rows, :] = jnp.where(head_sel[h], k, bias)
                    vt_sc[h, jt] = jnp.where(head_sel_t[h], vt, 0.0).astype(BF16)
                return carry

            lax.fori_loop(0, n_t, prep, 0)

        q = q_ref[...]
        q_heads = []
        for h in range(2):
            ones = jnp.where((lane >= spare[h]) & (lane < spare[h] + 3), jnp.ones((), BF16), zero)
            q_heads.append(jnp.where(head_sel[h], q, ones))
        m_sc[...] = jnp.full((8, T), NEG, F32)
        l_sc[...] = jnp.zeros((8, T), F32)
        acc_sc[...] = jnp.zeros((128, T), F32)

        def step(j, masked):
            rows = pl.ds(pl.multiple_of(j * T, T), T)
            alphas, pvs = [], []
            for h in range(2):
                s_t = _dot_nt(kaug_sc[h, rows, :], q_heads[h])
                if masked:
                    rr = lax.broadcasted_iota(jnp.int32, (T, T), 0)
                    cc = lax.broadcasted_iota(jnp.int32, (T, T), 1)
                    s_t = jnp.where(rr <= cc, s_t, NEG)
                m_prev = m_sc[h:h + 1, :]
                m_new = jnp.maximum(m_prev, jnp.max(s_t, axis=0, keepdims=True))
                alpha = jnp.exp(m_prev - m_new)
                p_t = jnp.exp(s_t - m_new)
                l_sc[h:h + 1, :] = alpha * l_sc[h:h + 1, :] + jnp.sum(p_t, axis=0, keepdims=True)
                m_sc[h:h + 1, :] = m_new
                alphas.append(alpha)
                pvs.append(_dot(vt_sc[h, j], p_t.astype(BF16)))
            acc_sc[...] = acc_sc[...] * jnp.where(head_sel_t[0], alphas[0], alphas[1]) + (pvs[0] + pvs[1])

        def off_diagonal(j, carry):
            step(j, False)
            return carry

        lax.fori_loop(0, i, off_diagonal, 0)
        step(i, True)
        l = l_sc[...]
        o_ref[...] = (acc_sc[...] / jnp.where(head_sel_t[0], l[0:1, :], l[1:2, :])).T
        is_head = lax.broadcasted_iota(jnp.int32, (8, 1), 0) < 2
        lse_ref[...] = jnp.where(is_head, m_sc[...] + jnp.log(jnp.where(is_head, l, 1.0)), 0.0)

    return pl.pallas_call(
        body, name="attention_fwd", grid=(N_PAIR, n_t),
        out_shape=(jax.ShapeDtypeStruct((S, D_ATT), F32), jax.ShapeDtypeStruct((N_PAIR, n_t, 8, T), F32)),
        in_specs=[pl.BlockSpec((T, 128), lambda hp, i: (i, hp)),
                  pl.BlockSpec((S, 128), lambda hp, i: (0, N_PAIR + hp)),
                  pl.BlockSpec((S, 128), lambda hp, i: (0, 2 * N_PAIR + hp)),
                  pl.BlockSpec((S, 128), lambda hp, i: (0, 0))],
        out_specs=(pl.BlockSpec((T, 128), lambda hp, i: (i, hp)),
                   pl.BlockSpec((None, None, 8, T), lambda hp, i: (hp, i, 0, 0))),
        scratch_shapes=[pltpu.VMEM((2, S, 128), BF16), pltpu.VMEM((2, n_t, 128, T), BF16),
                        pltpu.VMEM((8, T), F32), pltpu.VMEM((8, T), F32), pltpu.VMEM((128, T), F32)],
        compiler_params=_params(dimension_semantics=("arbitrary", "arbitrary")),
    )(qkv, qkv, qkv, big_f)


def _attention_bwd(qkv, datt, att, lse, big_f):
    S = qkv.shape[0]
    T = min(T_ATT, S)
    n_t = S // T

    def body(q_ref, do_ref, o_ref, lse_ref, k_ref, v_ref, fk_ref,
             dq_ref, dk_ref, dv_ref, cs_ref, dfk_ref, dfq_ref, stat_sc, dqt_sc, qaug_sc):
        hp = pl.program_id(0)
        j = pl.program_id(1)
        lane = lax.broadcasted_iota(jnp.int32, (1, 128), 1)
        sub = lax.broadcasted_iota(jnp.int32, (128, 1), 0)
        head_sel = (lane < HEAD_DIM, lane >= HEAD_DIM)
        head_sel_t = (sub < HEAD_DIM, sub >= HEAD_DIM)
        spare = (HEAD_DIM, 0)
        zero = jnp.zeros((), BF16)
        one = jnp.ones((), BF16)

        def bias_lanes(first, pieces):
            hi, mid, lo = pieces
            return lambda rest: jnp.where(lane == first, hi, jnp.where(lane == first + 1, mid,
                                                                        jnp.where(lane == first + 2, lo, rest)))

        @pl.when(j == 0)
        def _():
            dqt_sc[...] = jnp.zeros_like(dqt_sc)
            cs_ref[...] = jnp.zeros_like(cs_ref)
            dfq_ref[...] = jnp.zeros_like(dfq_ref)

            def prep(i, carry):
                rows = pl.ds(pl.multiple_of(i * T, T), T)
                q = q_ref[rows, :]
                do = do_ref[rows, :]
                prod = o_ref[rows, :] * do.astype(F32)
                d_a = jnp.sum(jnp.where(head_sel[0], prod, 0.0), axis=1, keepdims=True)
                d_b = jnp.sum(jnp.where(head_sel[0], 0.0, prod), axis=1, keepdims=True)
                delta_t = jnp.where(head_sel[0], d_a, d_b).T
                stat_sc[i, 0:1, :] = delta_t[0:1, :]
                stat_sc[i, 1:2, :] = delta_t[HEAD_DIM:HEAD_DIM + 1, :]
                lse = lse_ref[i]
                lse_cols = jnp.where(head_sel_t[0], lse[0:1, :], lse[1:2, :]).T
                for h in range(2):
                    neg_lse = -lse_cols[:, h * HEAD_DIM:h * HEAD_DIM + 1]
                    ones = jnp.where((lane >= spare[h]) & (lane < spare[h] + 3), one, zero)
                    qaug_sc[h, rows, :] = jnp.where(head_sel[h], q, bias_lanes(spare[h] + 3, _split3(neg_lse))(ones))
                return carry

            lax.fori_loop(0, n_t, prep, 0)

        k = k_ref[...]
        v = v_ref[...]
        fk = fk_ref[...]
        kt = k.astype(F32).T
        heads = []
        for h in range(2):
            fkh = jnp.sum(jnp.where(lane == 2 * hp + h, fk, 0.0), axis=1, keepdims=True)
            ones = jnp.where((lane >= spare[h] + 3) & (lane < spare[h] + 6), one, zero)
            kaug = jnp.where(head_sel[h], k, bias_lanes(spare[h], _split3(-fkh))(ones))
            heads.append((kaug, jnp.where(head_sel[h], v, zero), jnp.where(head_sel_t[h], kt, 0.0).astype(BF16)))

        def step(i, masked, acc):
            dk_acc, dv_acc, dfa, dfb = acc
            rows = pl.ds(pl.multiple_of(i * T, T), T)
            do = do_ref[rows, :]
            stat = stat_sc[i]
            dqt = dqt_sc[i]
            dfs = [dfa, dfb]
            for h in range(2):
                kaug, vh, kth = heads[h]
                arg = _dot_nt(kaug, qaug_sc[h, rows, :])
                if masked:
                    rr = lax.broadcasted_iota(jnp.int32, (T, T), 0)
                    cc = lax.broadcasted_iota(jnp.int32, (T, T), 1)
                    arg = jnp.where(rr <= cc, arg, NEG)
                p_t = jnp.exp(arg)
                ds_t = p_t * (_dot_nt(vh, do) - stat[h:h + 1, :])
                ds_bf = ds_t.astype(BF16)
                dv_acc = dv_acc + _dot(p_t.astype(BF16), jnp.where(head_sel[h], do, zero))
                dk_acc = dk_acc + _dot(ds_bf, jnp.where(head_sel[h], q_ref[rows, :], zero))
                dqt = dqt + _dot(kth, ds_bf)
                dfs[h] = dfs[h] + jnp.sum(ds_t, axis=1, keepdims=True)
                dfq_ref[i, h:h + 1, :] += _colsum(ds_t)
            dqt_sc[i] = dqt
            return dk_acc, dv_acc, dfs[0], dfs[1]

        acc0 = (jnp.zeros((T, 128), F32), jnp.zeros((T, 128), F32), jnp.zeros((T, 1), F32), jnp.zeros((T, 1), F32))
        acc1 = step(j, True, acc0)
        dk_acc, dv_acc, dfa, dfb = lax.fori_loop(j + 1, n_t, lambda i, a: step(i, False, a), acc1)
        dk_ref[...] = dk_acc.astype(BF16)
        dv_ref[...] = dv_acc.astype(BF16)
        dfk_ref[...] = -jnp.where(lane == 0, dfa, jnp.where(lane == 1, dfb, 0.0))
        cs_ref[:, 128:256] = cs_ref[:, 128:256] + _colsum(dk_acc)
        cs_ref[:, 256:384] = cs_ref[:, 256:384] + _colsum(dv_acc)

        @pl.when(j == n_t - 1)
        def _():
            def finish(i, tot):
                dq = dqt_sc[i].T
                dq_ref[pl.ds(pl.multiple_of(i * T, T), T), :] = dq.astype(BF16)
                return tot + _colsum(dq)

            cs_ref[:, 0:128] = lax.fori_loop(0, n_t, finish, jnp.zeros((1, 128), F32))

    pair_rows = lambda hp, j: (hp, 0, 0)
    return pl.pallas_call(
        body, name="attention_bwd", grid=(N_PAIR, n_t),
        out_shape=(jax.ShapeDtypeStruct((S, D_ATT), BF16), jax.ShapeDtypeStruct((S, D_ATT), BF16),
                   jax.ShapeDtypeStruct((S, D_ATT), BF16), jax.ShapeDtypeStruct((N_PAIR, 1, 384), F32),
                   jax.ShapeDtypeStruct((N_PAIR, S, 128), F32),
                   jax.ShapeDtypeStruct((N_PAIR, n_t, 8, T), F32)),
        in_specs=[pl.BlockSpec((S, 128), lambda hp, j: (0, hp)),
                  pl.BlockSpec((S, 128), lambda hp, j: (0, hp)),
                  pl.BlockSpec((S, 128), lambda hp, j: (0, hp)),
                  pl.BlockSpec((None, n_t, 8, T), lambda hp, j: (hp, 0, 0, 0)),
                  pl.BlockSpec((T, 128), lambda hp, j: (j, N_PAIR + hp)),
                  pl.BlockSpec((T, 128), lambda hp, j: (j, 2 * N_PAIR + hp)),
                  pl.BlockSpec((T, 128), lambda hp, j: (j, 0))],
        out_specs=(pl.BlockSpec((S, 128), lambda hp, j: (0, hp)),
                   pl.BlockSpec((T, 128), lambda hp, j: (j, hp)),
                   pl.BlockSpec((T, 128), lambda hp, j: (j, hp)),
                   pl.BlockSpec((None, 1, 384), pair_rows),
                   pl.BlockSpec((None, T, 128), lambda hp, j: (hp, j, 0)),
                   pl.BlockSpec((None, n_t, 8, T), lambda hp, j: (hp, 0, 0, 0))),
        scratch_shapes=[pltpu.VMEM((n_t, 8, T), F32), pltpu.VMEM((n_t, 128, T), F32),
                        pltpu.VMEM((2, S, 128), BF16)],
        compiler_params=_params(dimension_semantics=("arbitrary", "arbitrary")),
    )(qkv, datt, att, lse, qkv, qkv, big_f)


def _window_counts(first_row, n_rows, window):
    t = lax.broadcasted_iota(jnp.int32, (n_rows, 1), 0) + first_row
    return jnp.minimum((t + 1).astype(F32), float(window))


def _middle(x, tgt, att, g, p, gate, w_mix, b_mix, pool_scale, w_out, b_out, ln_g, ln_b):
    S = x.shape[0]
    tm = min(TM_MID, S)
    halo_blocks = tm // POOL_HALO

    def body(x_ref, t_ref, att_ref, g_ref, p_ref, ph_ref, gate_ref, wm_ref, bm_ref, ps_ref, wo_ref, bo_ref,
             lg_ref, lb_ref,
             dh_ref, datt_ref, dg_ref, dpl_ref, gwo_ref, gwm_ref, vec_ref, loss_ref):
        i = pl.program_id(0)

        @pl.when(i == 0)
        def _():
            gwo_ref[...] = jnp.zeros_like(gwo_ref)
            gwm_ref[...] = jnp.zeros_like(gwm_ref)
            vec_ref[...] = jnp.zeros_like(vec_ref)
            loss_ref[...] = jnp.zeros_like(loss_ref)

        pc = p_ref[...]
        halo = jnp.where(i > 0, ph_ref[...], 0.0)
        pe = jnp.concatenate([halo, pc], axis=0)
        pooled_parts = []
        for gi, w in enumerate(POOL_WINDOWS):
            cur = pe[:, gi * POOL_GROUP:(gi + 1) * POOL_GROUP]
            span = 1
            while span < w:
                cur = cur + pltpu.roll(cur, span, 0)
                span *= 2
            wsum = cur[POOL_HALO:, :]
            mean = wsum / _window_counts(i * tm, tm, w)
            pooled_parts.append(mean - pc[:, gi * POOL_GROUP:(gi + 1) * POOL_GROUP])
        pooled_bf =[v.astype(BF16) for v in pooled_parts]
        mixed = jnp.concatenate([_dot(pooled_bf[gi], wm_ref[gi]) for gi in range(4)], axis=1) + bm_ref[...]
        ps = ps_ref[...]
        pool_out = mixed * ps
        gv = g_ref[...]
        sig = _sigmoid(gv)
        silu = gv * sig
        att = att_ref[...]
        y = jnp.concatenate([att * silu[:, :D_ATT], pool_out * silu[:, D_ATT:]], axis=1)
        y_bf = y.astype(BF16)
        wo = wo_ref[...]
        yo = _dot(y_bf, wo) + bo_ref[...]
        gate = gate_ref[...]
        h = ALPHA * x_ref[...] + gate * yo
        mu = jnp.mean(h, axis=1, keepdims=True)
        hc = h - mu
        var = jnp.mean(hc * hc, axis=1, keepdims=True)
        rstd = lax.rsqrt(var + LN_EPS)
        yhat = hc * rstd
        lg = lg_ref[...]
        out = yhat * lg + lb_ref[...]
        err = out - t_ref[...]
        loss_ref[...] += 0.5 * jnp.sum(jnp.mean(err * err, axis=1, keepdims=True), axis=0, keepdims=True)

        dout = err * (1.0 / D)
        g_ln_b = _colsum(dout)
        g_ln_g = _colsum(dout * yhat)
        dyh = dout * lg
        dh = rstd * (dyh - jnp.mean(dyh, axis=1, keepdims=True)
                     - yhat * jnp.mean(dyh * yhat, axis=1, keepdims=True))
        dh_ref[...] = dh
        d_gate = _colsum(dh * yo)
        dyo = gate * dh
        g_b_out = _colsum(dyo)
        dyo_bf = dyo.astype(BF16)
        gwo_ref[...] += _dot_tn(y_bf, dyo_bf)
        dy = _dot_nt(dyo_bf, wo)
        dsilu = sig * (1.0 + gv * (1.0 - sig))
        dy_a = dy[:, :D_ATT]
        dy_p = dy[:, D_ATT:]
        datt_ref[...] = (dy_a * silu[:, :D_ATT]).astype(BF16)
        dpo = dy_p * silu[:, D_ATT:]
        dg = jnp.concatenate([dy_a * att * dsilu[:, :D_ATT], dy_p * pool_out * dsilu[:, D_ATT:]], axis=1)
        dg_ref[...] = dg.astype(BF16)
        g_dg = _colsum(dg)
        g_ps = _colsum(dpo * mixed)
        dmixed = dpo * ps
        g_bm = _colsum(dmixed)
        dmixed_bf = dmixed.astype(BF16)
        dpl = []
        for gi in range(4):
            dm = dmixed_bf[:, gi * POOL_GROUP:(gi + 1) * POOL_GROUP]
            gwm_ref[gi] += _dot_tn(pooled_bf[gi], dm)
            dpl.append(_dot_nt(dm, wm_ref[gi]))
        dpl_ref[...] = jnp.concatenate(dpl, axis=1)
        vec_ref[0:1, :] += g_ln_g
        vec_ref[1:2, :] += g_ln_b
        vec_ref[2:3, :] += d_gate
        vec_ref[3:4, :] += g_b_out
        vec_ref[4:5, :] += g_dg
        vec_ref[5:6, 0:D_POOL] += g_ps
        vec_ref[6:7, 0:D_POOL] += g_bm

    row = lambda w: pl.BlockSpec((tm, w), lambda i: (i, 0))
    full2 = lambda a: pl.BlockSpec(a.shape, lambda i: (0, 0))
    full3 = lambda a: pl.BlockSpec(a.shape, lambda i: (0, 0, 0))
    return pl.pallas_call(
        body, name="middle", grid=(S // tm,),
        out_shape=(jax.ShapeDtypeStruct((S, D), F32),
                   jax.ShapeDtypeStruct((S, D_ATT), BF16),
                   jax.ShapeDtypeStruct((S, D), BF16),
                   jax.ShapeDtypeStruct((S, D_POOL), F32),
                   jax.ShapeDtypeStruct((D, D), F32),
                   jax.ShapeDtypeStruct((4, POOL_GROUP, POOL_GROUP), F32),
                   jax.ShapeDtypeStruct((8, D), F32),
                   jax.ShapeDtypeStruct((1, 1), F32)),
        in_specs=[row(D), row(D), row(D_ATT), row(D), row(D_POOL),
                  pl.BlockSpec((POOL_HALO, D_POOL), lambda i: (jnp.maximum(i * halo_blocks - 1, 0), 0)),
                  full2(gate), full3(w_mix), full2(b_mix), full2(pool_scale), full2(w_out), full2(b_out),
                  full2(ln_g), full2(ln_b)],
        out_specs=(row(D), row(D_ATT), row(D), row(D_POOL),
                   pl.BlockSpec((D, D), lambda i: (0, 0)),
                   pl.BlockSpec((4, POOL_GROUP, POOL_GROUP), lambda i: (0, 0, 0)),
                   pl.BlockSpec((8, D), lambda i: (0, 0)),
                   pl.BlockSpec((1, 1), lambda i: (0, 0))),
        compiler_params=_params(dimension_semantics=("arbitrary",)),
    )(x, tgt, att, g, p, p, gate, w_mix, b_mix, pool_scale, w_out, b_out, ln_g, ln_b)


def _tail(dpl, dfk8, f):
    S = dpl.shape[0]
    tm = min(TM_TAIL, S)
    n_t = S // tm
    halo_blocks = tm // POOL_HALO
    last_halo = S // POOL_HALO - 1

    def body(d_ref, dn_ref, dfk_ref, f_ref, dp_ref, df_ref, cs_ref, carry):
        s = pl.program_id(0)
        i = n_t - 1 - s

        @pl.when(s == 0)
        def _():
            carry[...] = jnp.zeros_like(carry)
            cs_ref[...] = jnp.zeros_like(cs_ref)

        dc = d_ref[...]
        nxt = jnp.where(s > 0, dn_ref[...], 0.0)
        de = jnp.concatenate([dc, nxt], axis=0)
        n_e = tm + POOL_HALO
        parts = []
        for gi, w in enumerate(POOL_WINDOWS):
            cur = de[:, gi * POOL_GROUP:(gi + 1) * POOL_GROUP] / _window_counts(i * tm, n_e, w)
            span = 1
            while span < w:
                cur = cur + pltpu.roll(cur, n_e - span, 0)
                span *= 2
            parts.append(cur[:tm, :] - dc[:, gi * POOL_GROUP:(gi + 1) * POOL_GROUP])
        dp = jnp.concatenate(parts, axis=1)
        dp_ref[...] = dp.astype(BF16)
        cs_ref[0:1, :] += _colsum(dp)

        r = lax.broadcasted_iota(jnp.int32, (tm, tm), 0)
        c = lax.broadcasted_iota(jnp.int32, (tm, tm), 1)
        tri = (r <= c).astype(F32)
        dlogf = jnp.dot(tri, dfk_ref[...], preferred_element_type=F32, precision=lax.Precision.HIGHEST) + carry[...]
        carry[...] = dlogf[0:1, :]
        df = dlogf * _sigmoid(-f_ref[...])
        df_ref[...] = df.astype(BF16)
        cs_ref[1:2, 0:128] += _colsum(df)

    rev = lambda w: pl.BlockSpec((tm, w), lambda s: (n_t - 1 - s, 0))
    return pl.pallas_call(
        body, name="tail", grid=(n_t,),
        out_shape=(jax.ShapeDtypeStruct((S, D_POOL), BF16), jax.ShapeDtypeStruct((S, 128), BF16),
                   jax.ShapeDtypeStruct((8, D_POOL), F32)),
        in_specs=[rev(D_POOL),
                  pl.BlockSpec((POOL_HALO, D_POOL),
                               lambda s: (jnp.minimum((n_t - s) * halo_blocks, last_halo), 0)),
                  rev(128), rev(128)],
        out_specs=(rev(D_POOL), rev(128), pl.BlockSpec((8, D_POOL), lambda s: (0, 0))),
        scratch_shapes=[pltpu.VMEM((1, 128), F32)],
        compiler_params=_params(dimension_semantics=("arbitrary",)),
    )(dpl, dpl, dfk8, f)


PIECES = ((O_QKV, D_ATT), (O_QKV + D_ATT, D_ATT), (O_QKV + 2 * D_ATT, D_ATT), (O_F, 128), (O_P, D_POOL), (O_G, D))


def _grad_w_in(u, pieces):
    S = u.shape[0]
    tm = min(TM_GW, S)
    n_t = S // tm

    def body(u_ref, *rest):
        piece_refs, out_ref, acc, sem = rest[:6], rest[6], rest[7], rest[8]
        i = pl.program_id(0)

        @pl.when(i == 0)
        def _():
            acc[...] = jnp.zeros_like(acc)

        u_t = u_ref[...]
        for (off, w), ref in zip(PIECES, piece_refs):
            acc[:, off:off + w] += _dot_tn(u_t, ref[...])

        @pl.when(i == n_t - 1)
        def _():
            cp = pltpu.make_async_copy(acc, out_ref, sem)
            cp.start()
            cp.wait()

    return pl.pallas_call(
        body, name="grad_w_in", grid=(n_t,),
        out_shape=jax.ShapeDtypeStruct((D, D_PAD), F32),
        in_specs=[pl.BlockSpec((tm, D), lambda i: (i, 0))]
        + [pl.BlockSpec((tm, w), lambda i: (i, 0)) for _, w in PIECES],
        out_specs=pl.BlockSpec(memory_space=pl.ANY),
        scratch_shapes=[pltpu.VMEM((D, D_PAD), F32), pltpu.SemaphoreType.DMA],
        compiler_params=_params(dimension_semantics=("arbitrary",)),
    )(u, *pieces)


def _grad_x(pieces, w_pad, dh, x, scale):
    S = x.shape[0]
    tm = min(TM_DU, S)

    def body(*refs):
        piece_refs = refs[:6]
        w_ref, dh_ref, x_ref, sc_ref, gx_ref, vec_ref = refs[6:]

        @pl.when(pl.program_id(0) == 0)
        def _():
            vec_ref[...] = jnp.zeros_like(vec_ref)

        du = jnp.zeros((tm, D), F32)
        for (off, w), ref in zip(PIECES, piece_refs):
            du = du + _dot_nt(ref[...], w_ref[:, off:off + w])
        xv = x_ref[...]
        gx_ref[...] = ALPHA * dh_ref[...] + du * (1.0 + sc_ref[...])
        vec_ref[0:1, :] += _colsum(du)
        vec_ref[1:2, :] += _colsum(du * xv)

    row = lambda w: pl.BlockSpec((tm, w), lambda i: (i, 0))
    return pl.pallas_call(
        body, name="grad_x", grid=(S // tm,),
        out_shape=(jax.ShapeDtypeStruct((S, D), F32), jax.ShapeDtypeStruct((8, D), F32)),
        in_specs=[row(w) for _, w in PIECES]
        + [pl.BlockSpec(w_pad.shape, lambda i: (0, 0)), row(D), row(D), pl.BlockSpec((1, D), lambda i: (0, 0))],
        out_specs=(row(D), pl.BlockSpec((8, D), lambda i: (0, 0))),
        compiler_params=_params(dimension_semantics=("arbitrary",)),
    )(*pieces, w_pad, dh, x, scale)


def _grad_ada(c_all, dada_all, dada_cols):
    def body(c_ref, dall_ref, dcol_ref, gw_ref, gb_ref):
        rows = lax.broadcasted_iota(jnp.int32, (8, 1), 0)
        cm = jnp.zeros((8, D), F32)
        dm = jnp.zeros((8, 3 * D), F32)
        for r in range(8):
            cm = jnp.where(rows == r, c_ref[r], cm)
            dm = jnp.where(rows == r, dall_ref[r], dm)
        act = cm * _sigmoid(cm)
        pad = jnp.zeros((8, D), F32)
        lhs = jnp.concatenate([act, pad], axis=0).astype(BF16)
        rhs = jnp.concatenate([dcol_ref[...], jnp.zeros((8, SHARD_ADA), F32)], axis=0).astype(BF16)
        gw_ref[...] = _dot_tn(lhs, rhs)
        gb_ref[...] = _colsum(dm)

    vm = pl.BlockSpec(memory_space=pltpu.VMEM)
    return pl.pallas_call(
        body, name="grad_ada",
        out_shape=(jax.ShapeDtypeStruct((D, SHARD_ADA), F32), jax.ShapeDtypeStruct((1, 3 * D), F32)),
        in_specs=[vm, vm, vm], out_specs=(vm, vm),
        compiler_params=_params(),
    )(c_all, dada_all, dada_cols)


def _adamw_math(w, g, m, v):
    m = ADAM_B1 * m + (1.0 - ADAM_B1) * g
    v = ADAM_B2 * v + (1.0 - ADAM_B2) * (g * g)
    m_hat = m / (1.0 - ADAM_B1 ** ADAM_STEP)
    v_hat = v / (1.0 - ADAM_B2 ** ADAM_STEP)
    delta = -ADAM_LR * (m_hat / (jnp.sqrt(v_hat) + ADAM_EPS) + ADAM_WD * w)
    return delta, m, v


def _adamw(groups, n_steps):
    n = len(groups)

    def body(*refs):
        ins, outs = refs[:4 * n], refs[4 * n:]
        for t in range(n):
            w, g, m, v = (r[...] for r in ins[4 * t:4 * t + 4])
            d, m2, v2 = _adamw_math(w, g, m, v)
            outs[3 * t][...] = d
            outs[3 * t + 1][...] = m2
            outs[3 * t + 2][...] = v2

    in_specs, out_specs, out_shape, args = [], [], [], []
    for (w, g, m, v) in groups:
        rest = w.shape[1:]
        spec = pl.BlockSpec((w.shape[0] // n_steps,) + rest, lambda i, nd=len(rest): (i,) + (0,) * nd)
        in_specs += [spec] * 4
        out_specs += [spec] * 3
        out_shape += [jax.ShapeDtypeStruct(w.shape, F32)] * 3
        args += [w, g, m, v]
    return pl.pallas_call(
        body, name="adamw_%d_%d" % (n, n_steps), grid=(n_steps,),
        out_shape=tuple(out_shape), in_specs=in_specs, out_specs=tuple(out_specs),
        compiler_params=_params(dimension_semantics=("arbitrary",)),
    )(*args)


def _pack_small(parts):
    rows = []
    used = 0
    for name, (first, n_rows) in SMALL_SEGS.items():
        if first > used:
            rows.append(jnp.zeros((first - used, 128), F32))
        flat = parts[name].reshape(-1)
        flat = jnp.pad(flat, (0, n_rows * 128 - flat.shape[0]))
        rows.append(flat.reshape(n_rows, 128))
        used = first + n_rows
    rows.append(jnp.zeros((SMALL_ROWS - used, 128), F32))
    return jnp.concatenate(rows, axis=0)


def _unpack_small(buf, name, shape):
    first, n_rows = SMALL_SEGS[name]
    n = int(np.prod(shape))
    return buf[first:first + n_rows].reshape(-1)[:n].reshape(shape)


def _pad_in(v):
    r = v.shape[0]
    z = jnp.zeros((r, O_P - O_F - N_HEADS), v.dtype)
    return jnp.concatenate([v[:, :3 * D_ATT + N_HEADS], z, v[:, 3 * D_ATT + N_HEADS:]], axis=1)


def _unpad_in(v):
    return jnp.concatenate([v[:, :O_F + N_HEADS], v[:, O_P:]], axis=1)


def _shards_in(v):
    gap = O_P - (O_F + N_HEADS)
    parts = []
    for a in range(N_CHIPS):
        lo, hi = a * SHARD_IN, (a + 1) * SHARD_IN
        cut = O_F + N_HEADS
        if hi <= cut:
            parts.append(v[:, lo:hi])
        elif lo >= cut:
            parts.append(v[:, lo + gap:hi + gap])
        else:
            parts.append(jnp.concatenate([v[:, lo:cut], v[:, cut + gap:hi + gap]], axis=1))
    return jnp.stack(parts, axis=0)


def kernel(x, c, w_ada, b_ada, w_in, b_in, w_pool_mix, b_pool_mix, pool_scale, w_out, b_out, ln_g, ln_b, loss_target, m_w_ada, m_b_ada, m_w_in, m_b_in, m_w_pool_mix, m_b_pool_mix, m_pool_scale, m_w_out, m_b_out, m_ln_g, m_ln_b, v_w_ada, v_b_ada, v_w_in, v_b_in, v_w_pool_mix, v_b_pool_mix, v_pool_scale, v_w_out, v_b_out, v_ln_g, v_ln_b):
    S = x.shape[1]
    T = min(T_ATT, S)
    n_t = S // T
    chip = 2 * lax.axis_index("x") + lax.axis_index("y")
    x2 = x[0]
    tgt = loss_target[0]
    q_scale = jnp.concatenate([jnp.full((1, D_ATT), Q_SCALE, F32), jnp.ones((1, D_PAD - D_ATT), F32)], axis=1)

    c_all, ada4, w_in_all, w_out_all = _gather_and_ada(
        c, w_ada[0], b_ada.reshape(4, 1, SHARD_ADA), w_in[0].astype(BF16), w_out[0].astype(BF16))
    ada = ada4[:, 0, :].reshape(1, 3 * D)
    shift, scale, gate = ada[:, :D], ada[:, D:2 * D], ada[:, 2 * D:]
    w_in_full = jnp.transpose(w_in_all, (1, 0, 2)).reshape(D, D_IN)
    w_pad = _pad_in(w_in_full) * q_scale.astype(BF16)
    b_pad = _pad_in(b_in) * q_scale
    w_out_full = w_out_all.reshape(D, D)
    w_mix_bf = w_pool_mix[0].astype(BF16)

    u, qkv, f, p, g = _in_proj(x2, shift, scale, w_pad, b_pad)
    big_f = _forget_cumsum(f)
    att, lse = _attention_fwd(qkv, big_f)

    dh, datt, dg, dpl, gw_out, gw_mix, vec, loss_part = _middle(
        x2, tgt, att, g, p, gate, w_mix_bf, b_pool_mix.reshape(1, D_POOL), pool_scale, w_out_full, b_out, ln_g, ln_b)
    dq, dk, dv, cs_att, dfk, dfq = _attention_bwd(qkv, datt, att, lse, big_f)
    dfk8 = jnp.transpose(dfk[:, :, 0:2], (1, 0, 2)).reshape(S, N_HEADS)
    dfk8 = dfk8 + jnp.transpose(dfq[:, :, 0:2, :], (1, 3, 0, 2)).reshape(S, N_HEADS)
    dfk8 = jnp.pad(dfk8, ((0, 0), (0, 128 - N_HEADS)))
    dp, df, cs_tail = _tail(dpl, dfk8, f)
    pieces = (dq, dk, dv, df, dp, dg)
    gw_pad = _grad_w_in(u, pieces)
    grad_x, vec_x = _grad_x(pieces, w_pad, dh, x2, scale)

    cs_qkv = jnp.transpose(cs_att.reshape(N_PAIR, 3, 128), (1, 0, 2)).reshape(1, 3 * D_ATT)
    gb_pad = jnp.concatenate([cs_qkv, cs_tail[1:2, 0:128], cs_tail[0:1, :], vec[4:5, :]], axis=1) * q_scale
    dada = jnp.concatenate([vec_x[0:1, :], vec_x[1:2, :], vec[2:3, :]], axis=1)
    small = _pack_small({
        "b_in": gb_pad, "w_pool_mix": gw_mix, "b_pool_mix": vec[6:7, :D_POOL], "pool_scale": vec[5:6, :D_POOL],
        "b_out": vec[3:4, :], "ln_g": vec[0:1, :], "ln_b": vec[1:2, :], "b_ada": jnp.zeros((1, 3 * D), F32)})

    g_w_in, g_w_out, small_sum, dada_all = _reduce_all(
        _shards_in(gw_pad), _shards_in(q_scale), gw_out.reshape(N_CHIPS, SHARD_OUT, D),
        jnp.ones((N_CHIPS, 1, D), F32), small[:SMALL_REDUCED_ROWS], dada)
    dada_cols = lax.dynamic_slice(dada_all[:, 0, :], (0, chip * SHARD_ADA), (8, SHARD_ADA))
    g_w_ada, g_b_ada = _grad_ada(c_all, dada_all, dada_cols)
    loss = lax.psum(loss_part[0, 0], ("x", "y", "c"))

    grads_small = jnp.concatenate([small_sum, g_b_ada.reshape(24, 128)], axis=0)
    small_w = {"b_in": _pad_in(b_in), "w_pool_mix": w_pool_mix, "b_pool_mix": b_pool_mix, "pool_scale": pool_scale,
               "b_out": b_out, "ln_g": ln_g, "ln_b": ln_b, "b_ada": b_ada}
    small_m = {"b_in": _pad_in(m_b_in), "w_pool_mix": m_w_pool_mix, "b_pool_mix": m_b_pool_mix,
               "pool_scale": m_pool_scale, "b_out": m_b_out, "ln_g": m_ln_g, "ln_b": m_ln_b, "b_ada": m_b_ada}
    small_v = {"b_in": _pad_in(v_b_in), "w_pool_mix": v_w_pool_mix, "b_pool_mix": v_b_pool_mix,
               "pool_scale": v_pool_scale, "b_out": v_b_out, "ln_g": v_ln_g, "ln_b": v_ln_b, "b_ada": v_b_ada}
    big = _adamw([(w_ada[0], g_w_ada, m_w_ada[0], v_w_ada[0]),
                  (w_out[0], g_w_out, m_w_out[0], v_w_out[0])], 8)
    to_cols = lambda a: jnp.transpose(a, (2, 0, 1))
    from_cols = lambda a: jnp.transpose(a, (1, 2, 0))
    g_w_in_cols = to_cols(g_w_in[None])
    big_in = _adamw([(to_cols(w_in), g_w_in_cols, to_cols(m_w_in), to_cols(v_w_in))], 14)
    sm = _adamw([(_pack_small(small_w), grads_small, _pack_small(small_m), _pack_small(small_v))], 1)

    names = ["w_ada", "b_ada", "w_in", "b_in", "w_pool_mix", "b_pool_mix", "pool_scale", "w_out", "b_out",
             "ln_g", "ln_b"]
    shapes = {"b_ada": (1, 3 * D), "b_in": (1, D_PAD), "w_pool_mix": (1, 4, POOL_GROUP, POOL_GROUP),
              "b_pool_mix": (1, 4, POOL_GROUP), "pool_scale": (1, D_POOL), "b_out": (1, D), "ln_g": (1, D),
              "ln_b": (1, D)}
    big_idx = {"w_ada": 0, "w_out": 1}

    def leaf(kind, name):
        if name == "w_in":
            return from_cols(g_w_in_cols if kind == 0 else big_in[kind - 1])
        if name in big_idx:
            if kind == 0:
                return (g_w_ada, g_w_out)[big_idx[name]][None]
            return big[3 * big_idx[name] + kind - 1][None]
        buf = grads_small if kind == 0 else sm[kind - 1]
        val = _unpack_small(buf, name, shapes[name])
        if name == "b_in":
            val = _unpad_in(val)
        return val

    outs = [loss, grad_x[None]]
    for kind in range(4):
        outs += [leaf(kind, n) for n in names]
    return tuple(outs)
```

```python
import functools

import numpy as np
import jax
import jax.numpy as jnp
from jax import lax
from jax.experimental import pallas as pl
from jax.experimental.pallas import tpu as pltpu

F32 = jnp.float32
BF16 = jnp.bfloat16
MESH = pl.DeviceIdType.MESH

D = 1024
D_ATT = 512
D_POOL = 512
N_HEADS = 8
HEAD_DIM = 64
N_PAIR = N_HEADS // 2
POOL_WINDOWS = (2, 4, 8, 16)
POOL_GROUP = 128
POOL_HALO = 16
LN_EPS = 1e-5
ALPHA = 2.0 ** 0.25
D_IN = 3 * D_ATT + N_HEADS + D_POOL + D_ATT + D_POOL
N_CHIPS = 4
SHARD_IN = D_IN // N_CHIPS
SHARD_ADA = 3 * D // N_CHIPS
SHARD_OUT = D // N_CHIPS

O_QKV, O_F, O_P, O_G, D_PAD = 0, 1536, 1664, 2176, 3200
Q_SCALE = HEAD_DIM ** -0.5

ADAM_LR, ADAM_B1, ADAM_B2, ADAM_EPS, ADAM_WD, ADAM_STEP = 0.001, 0.9, 0.999, 1e-08, 0.01, 10

NEG = -1e30

VMEM_LIMIT = 56 * 1024 * 1024

TM_PROJ = 512
T_ATT = 512
ATT_CHUNK = 32
TM_MID = 256
TM_TAIL = 512
TM_GW = 512
TM_DU = 512

REL7 = [(0, 0, 1), (0, 1, 0), (0, 1, 1), (1, 0, 0), (1, 0, 1), (1, 1, 0), (1, 1, 1)]
REL3 = [(0, 1), (1, 0), (1, 1)]

SMALL_SEGS = {}
_row = 0
for _name, _n in (("b_in", 3200), ("w_pool_mix", 65536), ("b_pool_mix", 512), ("pool_scale", 512),
                  ("b_out", 1024), ("ln_g", 1024), ("ln_b", 1024), ("loss", 1)):
    _rows = -(-_n // 1024) * 8
    SMALL_SEGS[_name] = (_row, _rows)
    _row += _rows
SMALL_REDUCED_ROWS = -(-_row // 16) * 16
SMALL_SEGS["b_ada"] = (SMALL_REDUCED_ROWS, 24)
SMALL_ROWS = SMALL_REDUCED_ROWS + 24


def _params(**kw):
    return pltpu.CompilerParams(vmem_limit_bytes=VMEM_LIMIT, **kw)


def _flip(v, d):
    return v if d == 0 else 1 - v


def _dot(a, b):
    return jnp.dot(a, b, preferred_element_type=F32)


def _dot_nt(a, b):
    return lax.dot_general(a, b, (((1,), (1,)), ((), ())), preferred_element_type=F32)


def _dot_tn(a, b):
    return lax.dot_general(a, b, (((0,), (0,)), ((), ())), preferred_element_type=F32)


def _sigmoid(v):
    return 1.0 / (1.0 + jnp.exp(-v))


def _colsum(v):
    return jnp.sum(v, axis=0, keepdims=True)


def _gather_and_ada(c, w_ada, b_ada4, w_in_sh, w_out_sh):
    def body(c_ref, w_ref, b_ref, win_ref, wout_ref, call_ref, ada_ref, win_all, wout_all,
             cslab, sbuf, rbuf, cs_sem, cr_sem, as_sem, ar_sem, own_sem, s_sem, r_sem, fs_sem, fr_sem):
        x, y, cc = lax.axis_index("x"), lax.axis_index("y"), lax.axis_index("c")
        me = 4 * x + 2 * y + cc
        chip = 2 * x + y
        sib = (x, y, 1 - cc)
        srcs = (win_ref, wout_ref)
        dsts = (win_all, wout_all)

        def half(t, which):
            if t == 0:
                return (slice(None), pl.ds(pl.multiple_of(which * (D // 2), D // 2), D // 2))
            return (pl.ds(pl.multiple_of(which * (SHARD_OUT // 2), SHARD_OUT // 2), SHARD_OUT // 2), slice(None))

        own = [pltpu.make_async_copy(srcs[t], dsts[t].at[chip], own_sem.at[t]) for t in range(2)]
        for cp in own:
            cp.start()
        first = []
        for k, (dx, dy) in enumerate(REL3):
            peer = (_flip(x, dx), _flip(y, dy), cc)
            for t in range(2):
                cp = pltpu.make_async_remote_copy(
                    src_ref=srcs[t].at[half(t, cc)], dst_ref=dsts[t].at[(chip,) + half(t, cc)],
                    send_sem=s_sem.at[2 * k + t], recv_sem=r_sem.at[2 * k + t],
                    device_id=peer, device_id_type=MESH)
                cp.start()
                first.append(cp)

        cslab[...] = jnp.broadcast_to(c_ref[...], (8, D))
        call_ref[me] = cslab[...]
        gathers = []
        for k, (dx, dy, dc) in enumerate(REL7):
            cp = pltpu.make_async_remote_copy(
                src_ref=cslab, dst_ref=call_ref.at[me], send_sem=cs_sem.at[k], recv_sem=cr_sem.at[k],
                device_id=(_flip(x, dx), _flip(y, dy), _flip(cc, dc)), device_id_type=MESH)
            cp.start()
            gathers.append(cp)
        for cp in gathers:
            cp.wait()
        slab_row = lax.broadcasted_iota(jnp.int32, (8, 1), 0)
        mat = jnp.zeros((8, D), F32)
        for r in range(8):
            mat = jnp.where(slab_row == r, call_ref[r], mat)
        act = (mat * _sigmoid(mat)).astype(BF16)
        part = _dot(act, w_ref[...].astype(BF16))
        sends = []
        for k, (dx, dy) in enumerate(REL3):
            px, py = _flip(x, dx), _flip(y, dy)
            r = 4 * px + 2 * py + cc
            piece = _colsum(jnp.where(slab_row == r, part, 0.0))
            sbuf[k] = jnp.broadcast_to(piece, (8, SHARD_ADA))
            cp = pltpu.make_async_remote_copy(
                src_ref=sbuf.at[k], dst_ref=rbuf.at[k], send_sem=as_sem.at[k], recv_sem=ar_sem.at[k],
                device_id=(px, py, cc), device_id_type=MESH)
            cp.start()
            sends.append(cp)
        own_piece = _colsum(jnp.where(slab_row == me, part, 0.0))
        ada_ref[chip] = jnp.broadcast_to(own_piece, (8, SHARD_ADA)) + b_ref[chip]
        for k, (dx, dy) in enumerate(REL3):
            sends[k].wait()
            a = 2 * _flip(x, dx) + _flip(y, dy)
            ada_ref[a] = rbuf[k] + b_ref[a]

        passed = []
        for k, (dx, dy) in enumerate(REL3):
            a = 2 * _flip(x, dx) + _flip(y, dy)
            for t in range(2):
                landed = dsts[t].at[(a,) + half(t, cc)]
                pltpu.make_async_remote_copy(
                    src_ref=landed, dst_ref=landed, send_sem=s_sem.at[2 * k + t], recv_sem=r_sem.at[2 * k + t],
                    device_id=sib, device_id_type=MESH).wait_recv()
                cp = pltpu.make_async_remote_copy(
                    src_ref=landed, dst_ref=landed, send_sem=fs_sem.at[2 * k + t], recv_sem=fr_sem.at[2 * k + t],
                    device_id=sib, device_id_type=MESH)
                cp.start()
                passed.append(cp)
        for k, (dx, dy) in enumerate(REL3):
            a = 2 * _flip(x, dx) + _flip(y, dy)
            for t in range(2):
                other = dsts[t].at[(a,) + half(t, 1 - cc)]
                pltpu.make_async_remote_copy(
                    src_ref=other, dst_ref=other, send_sem=fs_sem.at[2 * k + t], recv_sem=fr_sem.at[2 * k + t],
                    device_id=sib, device_id_type=MESH).wait_recv()
        for cp in first + passed:
            cp.wait_send()
        for cp in own:
            cp.wait()

    vm = pl.BlockSpec(memory_space=pltpu.VMEM)
    return pl.pallas_call(
        body, name="gather_and_ada",
        out_shape=(jax.ShapeDtypeStruct((8, 8, D), F32), jax.ShapeDtypeStruct((4, 8, SHARD_ADA), F32),
                   jax.ShapeDtypeStruct((N_CHIPS, SHARD_IN, D), BF16),
                   jax.ShapeDtypeStruct((N_CHIPS, SHARD_OUT, D), BF16)),
        in_specs=[vm] * 5, out_specs=(vm,) * 4,
        scratch_shapes=[pltpu.VMEM((8, D), F32), pltpu.VMEM((3, 8, SHARD_ADA), F32),
                        pltpu.VMEM((3, 8, SHARD_ADA), F32),
                        pltpu.SemaphoreType.DMA((7,)), pltpu.SemaphoreType.DMA((7,)),
                        pltpu.SemaphoreType.DMA((3,)), pltpu.SemaphoreType.DMA((3,)),
                        pltpu.SemaphoreType.DMA((2,)), pltpu.SemaphoreType.DMA((6,)),
                        pltpu.SemaphoreType.DMA((6,)), pltpu.SemaphoreType.DMA((6,)),
                        pltpu.SemaphoreType.DMA((6,))],
        compiler_params=_params(),
    )(c, w_ada, b_ada4, w_in_sh, w_out_sh)


def _scatter_stages(pos, g_ref, sc_ref, out_ref, sib_buf, send_buf, ici_buf, sem1, sem2s, sem2r, sem3):
    x, y, cc, chip, sib = pos
    RH = g_ref.shape[1] // 2
    mine = pl.ds(pl.multiple_of(cc * RH, RH), RH)
    theirs = pl.ds(pl.multiple_of((1 - cc) * RH, RH), RH)
    cp1 = pltpu.make_async_remote_copy(
        src_ref=g_ref.at[:, theirs, :], dst_ref=sib_buf, send_sem=sem1.at[0], recv_sem=sem1.at[1],
        device_id=sib, device_id_type=MESH)
    sends = []
    for k, (dx, dy) in enumerate(REL3):
        px, py = _flip(x, dx), _flip(y, dy)
        sends.append(pltpu.make_async_remote_copy(
            src_ref=send_buf.at[2 * px + py], dst_ref=ici_buf.at[chip],
            send_sem=sem2s.at[k], recv_sem=sem2r.at[k], device_id=(px, py, cc), device_id_type=MESH))
    cp3 = pltpu.make_async_remote_copy(
        src_ref=out_ref.at[mine, :], dst_ref=out_ref.at[mine, :], send_sem=sem3.at[0], recv_sem=sem3.at[1],
        device_id=sib, device_id_type=MESH)

    def finish1():
        cp1.wait()
        for a in range(N_CHIPS):
            both = g_ref[a, mine, :] + sib_buf[a]
            sib_buf[a] = both
            send_buf[a] = both.astype(BF16)

    def start2():
        for cp in sends:
            cp.start()
        ici_buf[chip] = send_buf[chip]

    def finish2():
        for cp in sends:
            cp.wait()
        own = sib_buf[chip]
        parts = [jnp.where(chip == a, own, ici_buf[a].astype(F32)) for a in range(N_CHIPS)]
        out_ref[mine, :] = ((parts[0] + parts[1]) + (parts[2] + parts[3])) * sc_ref[chip]

    return [(cp1.start, finish1), (start2, finish2), (cp3.start, cp3.wait)]


def _all_reduce_stages(pos, g_ref, out_ref, sib_buf, ici_buf, sem1, sem2s, sem2r, sem3):
    x, y, cc, chip, sib = pos
    RH = g_ref.shape[0] // 2
    mine = pl.ds(pl.multiple_of(cc * RH, 8), RH)
    theirs = pl.ds(pl.multiple_of((1 - cc) * RH, 8), RH)
    cp1 = pltpu.make_async_remote_copy(
        src_ref=g_ref.at[theirs, :], dst_ref=sib_buf, send_sem=sem1.at[0], recv_sem=sem1.at[1],
        device_id=sib, device_id_type=MESH)
    sends = []
    for k, (dx, dy) in enumerate(REL3):
        px, py = _flip(x, dx), _flip(y, dy)
        sends.append(pltpu.make_async_remote_copy(
            src_ref=sib_buf, dst_ref=ici_buf.at[chip],
            send_sem=sem2s.at[k], recv_sem=sem2r.at[k], device_id=(px, py, cc), device_id_type=MESH))
    cp3 = pltpu.make_async_remote_copy(
        src_ref=out_ref.at[mine, :], dst_ref=out_ref.at[mine, :], send_sem=sem3.at[0], recv_sem=sem3.at[1],
        device_id=sib, device_id_type=MESH)

    def finish1():
        cp1.wait()
        sib_buf[...] = g_ref[mine, :] + sib_buf[...]

    def start2():
        for cp in sends:
            cp.start()
        ici_buf[chip] = sib_buf[...]

    def finish2():
        for cp in sends:
            cp.wait()
        out_ref[mine, :] = (ici_buf[0] + ici_buf[1]) + (ici_buf[2] + ici_buf[3])

    return [(cp1.start, finish1), (start2, finish2), (cp3.start, cp3.wait)]


def _reduce_all(g4_in, sc_in, g4_out, sc_out, small, dada):
    shapes = [g4_in.shape[1:], g4_out.shape[1:]]
    R = small.shape[0]
    W = dada.shape[1]
    n_sem = 4

    def body(gin_ref, scin_ref, gout_ref, scout_ref, sm_ref, d_ref, oin_ref, oout_ref, osm_ref, dall_ref, *scratch):
        x, y, cc = lax.axis_index("x"), lax.axis_index("y"), lax.axis_index("c")
        me = 4 * x + 2 * y + cc
        pos = (x, y, cc, 2 * x + y, (x, y, 1 - cc))
        dslab, ds_sem, dr_sem = scratch[0:3]
        rest = scratch[3:]
        in_bufs, rest = rest[0:3 + n_sem], rest[3 + n_sem:]
        out_bufs, rest = rest[0:3 + n_sem], rest[3 + n_sem:]
        sm_bufs = rest
        dslab[...] = jnp.broadcast_to(d_ref[...], (8, W))
        dall_ref[me] = dslab[...]
        gathers = []
        for k, (dx, dy, dc) in enumerate(REL7):
            cp = pltpu.make_async_remote_copy(
                src_ref=dslab, dst_ref=dall_ref.at[me], send_sem=ds_sem.at[k], recv_sem=dr_sem.at[k],
                device_id=(_flip(x, dx), _flip(y, dy), _flip(cc, dc)), device_id_type=MESH)
            cp.start()
            gathers.append(cp)
        plans = [_scatter_stages(pos, gin_ref, scin_ref, oin_ref, *in_bufs),
                 _scatter_stages(pos, gout_ref, scout_ref, oout_ref, *out_bufs),
                 _all_reduce_stages(pos, sm_ref, osm_ref, *sm_bufs)]
        for stage in range(3):
            for plan in plans:
                plan[stage][0]()
            for plan in plans:
                plan[stage][1]()
        for cp in gathers:
            cp.wait()

    def sems():
        return [pltpu.SemaphoreType.DMA((2,)), pltpu.SemaphoreType.DMA((3,)),
                pltpu.SemaphoreType.DMA((3,)), pltpu.SemaphoreType.DMA((2,))]

    scratch = [pltpu.VMEM((8, W), F32), pltpu.SemaphoreType.DMA((7,)), pltpu.SemaphoreType.DMA((7,))]
    for r, c in shapes:
        scratch += [pltpu.VMEM((N_CHIPS, r // 2, c), F32), pltpu.VMEM((N_CHIPS, r // 2, c), BF16),
                    pltpu.VMEM((N_CHIPS, r // 2, c), BF16)] + sems()
    scratch += [pltpu.VMEM((R // 2, 128), F32), pltpu.VMEM((N_CHIPS, R // 2, 128), F32)] + sems()
    vm = pl.BlockSpec(memory_space=pltpu.VMEM)
    return pl.pallas_call(
        body, name="reduce_all",
        out_shape=(jax.ShapeDtypeStruct(g4_in.shape[1:], F32), jax.ShapeDtypeStruct(g4_out.shape[1:], F32),
                   jax.ShapeDtypeStruct((R, 128), F32), jax.ShapeDtypeStruct((8, 8, W), F32)),
        in_specs=[vm] * 6, out_specs=(vm,) * 4,
        scratch_shapes=scratch,
        compiler_params=_params(),
    )(g4_in, sc_in, g4_out, sc_out, small, dada)


def _in_proj(x, shift, scale, wt_pad, b_pad):
    S = x.shape[0]
    tm = min(TM_PROJ, S)

    def body(x_ref, sh_ref, sc_ref, w_ref, b_ref, u_ref, qkv_ref, f_ref, p_ref, g_ref):
        u = (x_ref[...] * (1.0 + sc_ref[...]) + sh_ref[...]).astype(BF16)
        u_ref[...] = u
        qkv_ref[...] = (_dot_nt(u, w_ref[O_QKV:O_F, :]) + b_ref[:, O_QKV:O_F]).astype(BF16)
        f_ref[...] = _dot_nt(u, w_ref[O_F:O_P, :]) + b_ref[:, O_F:O_P]
        p_ref[...] = _dot_nt(u, w_ref[O_P:O_G, :]) + b_ref[:, O_P:O_G]
        g_ref[...] = _dot_nt(u, w_ref[O_G:D_PAD, :]) + b_ref[:, O_G:D_PAD]

    row = lambda w: pl.BlockSpec((tm, w), lambda i: (i, 0))
    full = lambda a: pl.BlockSpec(a.shape, lambda i: (0, 0))
    return pl.pallas_call(
        body, name="in_proj", grid=(S // tm,),
        out_shape=(jax.ShapeDtypeStruct((S, D), BF16), jax.ShapeDtypeStruct((S, 3 * D_ATT), BF16),
                   jax.ShapeDtypeStruct((S, 128), F32), jax.ShapeDtypeStruct((S, D_POOL), F32),
                   jax.ShapeDtypeStruct((S, D), F32)),
        in_specs=[row(D), full(shift), full(scale), full(wt_pad), full(b_pad)],
        out_specs=(row(D), row(3 * D_ATT), row(128), row(D_POOL), row(D)),
        compiler_params=_params(dimension_semantics=("arbitrary",)),
    )(x, shift, scale, wt_pad, b_pad)


def _forget_cumsum(f):
    S = f.shape[0]
    tm = min(T_ATT, S)

    def body(f_ref, out_ref, carry):
        @pl.when(pl.program_id(0) == 0)
        def _():
            carry[...] = jnp.zeros_like(carry)
        v = f_ref[...]
        logf = jnp.minimum(v, 0.0) - jnp.log(1.0 + jnp.exp(-jnp.abs(v)))
        r = lax.broadcasted_iota(jnp.int32, (tm, tm), 0)
        c = lax.broadcasted_iota(jnp.int32, (tm, tm), 1)
        tri = (r >= c).astype(F32)
        cum = jnp.dot(tri, logf, preferred_element_type=F32, precision=lax.Precision.HIGHEST) + carry[...]
        out_ref[...] = cum
        carry[...] = cum[tm - 8:tm, :][7:8, :]

    return pl.pallas_call(
        body, name="forget_cumsum", grid=(S // tm,),
        out_shape=jax.ShapeDtypeStruct((S, 128), F32),
        in_specs=[pl.BlockSpec((tm, 128), lambda i: (i, 0))],
        out_specs=pl.BlockSpec((tm, 128), lambda i: (i, 0)),
        scratch_shapes=[pltpu.VMEM((1, 128), F32)],
        compiler_params=_params(dimension_semantics=("arbitrary",)),
    )(f)


def _pair_select(is_a, va, vb):
    return jnp.where(is_a, va, vb)


def _split3(v):
    hi = v.astype(BF16)
    rest = v - hi.astype(F32)
    mid = rest.astype(BF16)
    lo = (rest - mid.astype(F32)).astype(BF16)
    return hi, mid, lo


def _attention_fwd(qkv, big_f):
    S = qkv.shape[0]
    T = min(T_ATT, S)
    n_t = S // T

    def body(q_ref, k_ref, v_ref, f_ref, o_ref, lse_ref, kaug_sc, vt_sc, m_sc, l_sc, acc_sc):
        hp = pl.program_id(0)
        i = pl.program_id(1)
        lane = lax.broadcasted_iota(jnp.int32, (1, 128), 1)
        sub = lax.broadcasted_iota(jnp.int32, (128, 1), 0)
        head_sel = (lane < HEAD_DIM, lane >= HEAD_DIM)
        head_sel_t = (sub < HEAD_DIM, sub >= HEAD_DIM)
        spare = (HEAD_DIM, 0)
        zero = jnp.zeros((), BF16)

        @pl.when(i == 0)
        def _():
            def prep(jt, carry):
                rows = pl.ds(pl.multiple_of(jt * T, T), T)
                k = k_ref[rows, :]
                ft = f_ref[rows, :]
                vt = v_ref[rows, :].astype(F32).T
                for h in range(2):
                    fh = jnp.sum(jnp.where(lane == 2 * hp + h, ft, 0.0), axis=1, keepdims=True)
                    hi, mid, lo = _split3(-fh)
                    b = spare[h]
                    bias = jnp.where(lane == b, hi, jnp.where(lane == b + 1, mid, jnp.where(lane == b + 2, lo, zero)))
                    kaug_sc[h, rows, :] = jnp.where(head_sel[h], k, bias)
                    vt_sc[h, jt] = jnp.where(head_sel_t[h], vt, 0.0).astype(BF16)
                return carry

            lax.fori_loop(0, n_t, prep, 0)

        q = q_ref[...]
        q_heads = []
        for h in range(2):
            ones = jnp.where((lane >= spare[h]) & (lane < spare[h] + 3), jnp.ones((), BF16), zero)
            q_heads.append(jnp.where(head_sel[h], q, ones))
        m_sc[...] = jnp.full((8, T), NEG, F32)
        l_sc[...] = jnp.zeros((8, T), F32)
        acc_sc[...] = jnp.zeros((128, T), F32)

        def step(j, masked):
            rows = pl.ds(pl.multiple_of(j * T, T), T)
            alphas, pvs = [], []
            for h in range(2):
                s_t = _dot_nt(kaug_sc[h, rows, :], q_heads[h])
                if masked:
                    rr = lax.broadcasted_iota(jnp.int32, (T, T), 0)
                    cc = lax.broadcasted_iota(jnp.int32, (T, T), 1)
                    s_t = jnp.where(rr <= cc, s_t, NEG)
                m_prev = m_sc[h:h + 1, :]
                m_new = jnp.maximum(m_prev, jnp.max(s_t, axis=0, keepdims=True))
                alpha = jnp.exp(m_prev - m_new)
                p_t = jnp.exp(s_t - m_new)
                l_sc[h:h + 1, :] = alpha * l_sc[h:h + 1, :] + jnp.sum(p_t, axis=0, keepdims=True)
                m_sc[h:h + 1, :] = m_new
                alphas.append(alpha)
                pvs.append(_dot(vt_sc[h, j], p_t.astype(BF16)))
            acc_sc[...] = acc_sc[...] * jnp.where(head_sel_t[0], alphas[0], alphas[1]) + (pvs[0] + pvs[1])

        def off_diagonal(j, carry):
            step(j, False)
            return carry

        lax.fori_loop(0, i, off_diagonal, 0)
        step(i, True)
        l = l_sc[...]
        o_ref[...] = (acc_sc[...] / jnp.where(head_sel_t[0], l[0:1, :], l[1:2, :])).T
        is_head = lax.broadcasted_iota(jnp.int32, (8, 1), 0) < 2
        lse_ref[...] = jnp.where(is_head, m_sc[...] + jnp.log(jnp.where(is_head, l, 1.0)), 0.0)

    return pl.pallas_call(
        body, name="attention_fwd", grid=(N_PAIR, n_t),
        out_shape=(jax.ShapeDtypeStruct((S, D_ATT), F32), jax.ShapeDtypeStruct((N_PAIR, n_t, 8, T), F32)),
        in_specs=[pl.BlockSpec((T, 128), lambda hp, i: (i, hp)),
                  pl.BlockSpec((S, 128), lambda hp, i: (0, N_PAIR + hp)),
                  pl.BlockSpec((S, 128), lambda hp, i: (0, 2 * N_PAIR + hp)),
                  pl.BlockSpec((S, 128), lambda hp, i: (0, 0))],
        out_specs=(pl.BlockSpec((T, 128), lambda hp, i: (i, hp)),
                   pl.BlockSpec((None, None, 8, T), lambda hp, i: (hp, i, 0, 0))),
        scratch_shapes=[pltpu.VMEM((2, S, 128), BF16), pltpu.VMEM((2, n_t, 128, T), BF16),
                        pltpu.VMEM((8, T), F32), pltpu.VMEM((8, T), F32), pltpu.VMEM((128, T), F32)],
        compiler_params=_params(dimension_semantics=("arbitrary", "arbitrary")),
    )(qkv, qkv, qkv, big_f)


def _attention_bwd(qkv, datt, att, lse, big_f):
    S = qkv.shape[0]
    T = min(T_ATT, S)
    n_t = S // T

    def body(q_ref, do_ref, o_ref, lse_ref, k_ref, v_ref, fk_ref,
             dq_ref, dk_ref, dv_ref, cs_ref, dfk_ref, dfq_ref, stat_sc, dqt_sc, qaug_sc):
        hp = pl.program_id(0)
        j = pl.program_id(1)
        lane = lax.broadcasted_iota(jnp.int32, (1, 128), 1)
        sub = lax.broadcasted_iota(jnp.int32, (128, 1), 0)
        head_sel = (lane < HEAD_DIM, lane >= HEAD_DIM)
        head_sel_t = (sub < HEAD_DIM, sub >= HEAD_DIM)
        spare = (HEAD_DIM, 0)
        zero = jnp.zeros((), BF16)
        one = jnp.ones((), BF16)

        def bias_lanes(first, pieces):
            hi, mid, lo = pieces
            return lambda rest: jnp.where(lane == first, hi, jnp.where(lane == first + 1, mid,
                                                                        jnp.where(lane == first + 2, lo, rest)))

        @pl.when(j == 0)
        def _():
            dqt_sc[...] = jnp.zeros_like(dqt_sc)
            cs_ref[...] = jnp.zeros_like(cs_ref)
            dfq_ref[...] = jnp.zeros_like(dfq_ref)

            def prep(i, carry):
                rows = pl.ds(pl.multiple_of(i * T, T), T)
                q = q_ref[rows, :]
                do = do_ref[rows, :]
                prod = o_ref[rows, :] * do.astype(F32)
                d_a = jnp.sum(jnp.where(head_sel[0], prod, 0.0), axis=1, keepdims=True)
                d_b = jnp.sum(jnp.where(head_sel[0], 0.0, prod), axis=1, keepdims=True)
                delta_t = jnp.where(head_sel[0], d_a, d_b).T
                stat_sc[i, 0:1, :] = delta_t[0:1, :]
                stat_sc[i, 1:2, :] = delta_t[HEAD_DIM:HEAD_DIM + 1, :]
                lse = lse_ref[i]
                lse_cols = jnp.where(head_sel_t[0], lse[0:1, :], lse[1:2, :]).T
                for h in range(2):
                    neg_lse = -lse_cols[:, h * HEAD_DIM:h * HEAD_DIM + 1]
                    ones = jnp.where((lane >= spare[h]) & (lane < spare[h] + 3), one, zero)
                    qaug_sc[h, rows, :] = jnp.where(head_sel[h], q, bias_lanes(spare[h] + 3, _split3(neg_lse))(ones))
                return carry

            lax.fori_loop(0, n_t, prep, 0)

        k = k_ref[...]
        v = v_ref[...]
        fk = fk_ref[...]
        kt = k.astype(F32).T
        heads = []
        for h in range(2):
            fkh = jnp.sum(jnp.where(lane == 2 * hp + h, fk, 0.0), axis=1, keepdims=True)
            ones = jnp.where((lane >= spare[h] + 3) & (lane < spare[h] + 6), one, zero)
            kaug = jnp.where(head_sel[h], k, bias_lanes(spare[h], _split3(-fkh))(ones))
            heads.append((kaug, jnp.where(head_sel[h], v, zero), jnp.where(head_sel_t[h], kt, 0.0).astype(BF16)))

        def step(i, masked, acc):
            dk_acc, dv_acc, dfa, dfb = acc
            rows = pl.ds(pl.multiple_of(i * T, T), T)
            do = do_ref[rows, :]
            stat = stat_sc[i]
            dqt = dqt_sc[i]
            dfs = [dfa, dfb]
            for h in range(2):
                kaug, vh, kth = heads[h]
                arg = _dot_nt(kaug, qaug_sc[h, rows, :])
                if masked:
                    rr = lax.broadcasted_iota(jnp.int32, (T, T), 0)
                    cc = lax.broadcasted_iota(jnp.int32, (T, T), 1)
                    arg = jnp.where(rr <= cc, arg, NEG)
                p_t = jnp.exp(arg)
                ds_t = p_t * (_dot_nt(vh, do) - stat[h:h + 1, :])
                ds_bf = ds_t.astype(BF16)
                dv_acc = dv_acc + _dot(p_t.astype(BF16), jnp.where(head_sel[h], do, zero))
                dk_acc = dk_acc + _dot(ds_bf, jnp.where(head_sel[h], q_ref[rows, :], zero))
                dqt = dqt + _dot(kth, ds_bf)
                dfs[h] = dfs[h] + jnp.sum(ds_t, axis=1, keepdims=True)
                dfq_ref[i, h:h + 1, :] += _colsum(ds_t)
            dqt_sc[i] = dqt
            return dk_acc, dv_acc, dfs[0], dfs[1]

        acc0 = (jnp.zeros((T, 128), F32), jnp.zeros((T, 128), F32), jnp.zeros((T, 1), F32), jnp.zeros((T, 1), F32))
        acc1 = step(j, True, acc0)
        dk_acc, dv_acc, dfa, dfb = lax.fori_loop(j + 1, n_t, lambda i, a: step(i, False, a), acc1)
        dk_ref[...] = dk_acc.astype(BF16)
        dv_ref[...] = dv_acc.astype(BF16)
        dfk_ref[...] = -jnp.where(lane == 0, dfa, jnp.where(lane == 1, dfb, 0.0))
        cs_ref[:, 128:256] = cs_ref[:, 128:256] + _colsum(dk_acc)
        cs_ref[:, 256:384] = cs_ref[:, 256:384] + _colsum(dv_acc)

        @pl.when(j == n_t - 1)
        def _():
            def finish(i, tot):
                dq = dqt_sc[i].T
                dq_ref[pl.ds(pl.multiple_of(i * T, T), T), :] = dq.astype(BF16)
                return tot + _colsum(dq)

            cs_ref[:, 0:128] = lax.fori_loop(0, n_t, finish, jnp.zeros((1, 128), F32))

    pair_rows = lambda hp, j: (hp, 0, 0)
    return pl.pallas_call(
        body, name="attention_bwd", grid=(N_PAIR, n_t),
        out_shape=(jax.ShapeDtypeStruct((S, D_ATT), BF16), jax.ShapeDtypeStruct((S, D_ATT), BF16),
                   jax.ShapeDtypeStruct((S, D_ATT), BF16), jax.ShapeDtypeStruct((N_PAIR, 1, 384), F32),
                   jax.ShapeDtypeStruct((N_PAIR, S, 128), F32),
                   jax.ShapeDtypeStruct((N_PAIR, n_t, 8, T), F32)),
        in_specs=[pl.BlockSpec((S, 128), lambda hp, j: (0, hp)),
                  pl.BlockSpec((S, 128), lambda hp, j: (0, hp)),
                  pl.BlockSpec((S, 128), lambda hp, j: (0, hp)),
                  pl.BlockSpec((None, n_t, 8, T), lambda hp, j: (hp, 0, 0, 0)),
                  pl.BlockSpec((T, 128), lambda hp, j: (j, N_PAIR + hp)),
                  pl.BlockSpec((T, 128), lambda hp, j: (j, 2 * N_PAIR + hp)),
                  pl.BlockSpec((T, 128), lambda hp, j: (j, 0))],
        out_specs=(pl.BlockSpec((S, 128), lambda hp, j: (0, hp)),
                   pl.BlockSpec((T, 128), lambda hp, j: (j, hp)),
                   pl.BlockSpec((T, 128), lambda hp, j: (j, hp)),
                   pl.BlockSpec((None, 1, 384), pair_rows),
                   pl.BlockSpec((None, T, 128), lambda hp, j: (hp, j, 0)),
                   pl.BlockSpec((None, n_t, 8, T), lambda hp, j: (hp, 0, 0, 0))),
        scratch_shapes=[pltpu.VMEM((n_t, 8, T), F32), pltpu.VMEM((n_t, 128, T), F32),
                        pltpu.VMEM((2, S, 128), BF16)],
        compiler_params=_params(dimension_semantics=("arbitrary", "arbitrary")),
    )(qkv, datt, att, lse, qkv, qkv, big_f)


def _window_counts(first_row, n_rows, window):
    t = lax.broadcasted_iota(jnp.int32, (n_rows, 1), 0) + first_row
    return jnp.minimum((t + 1).astype(F32), float(window))


def _middle(x, tgt, att, g, p, gate, w_mix, b_mix, pool_scale, w_out, b_out, ln_g, ln_b):
    S = x.shape[0]
    tm = min(TM_MID, S)
    halo_blocks = tm // POOL_HALO

    def body(x_ref, t_ref, att_ref, g_ref, p_ref, ph_ref, gate_ref, wm_ref, bm_ref, ps_ref, wo_ref, bo_ref,
             lg_ref, lb_ref,
             dh_ref, datt_ref, dg_ref, dpl_ref, gwo_ref, gwm_ref, vec_ref, loss_ref):
        i = pl.program_id(0)

        @pl.when(i == 0)
        def _():
            gwo_ref[...] = jnp.zeros_like(gwo_ref)
            gwm_ref[...] = jnp.zeros_like(gwm_ref)
            vec_ref[...] = jnp.zeros_like(vec_ref)
            loss_ref[...] = jnp.zeros_like(loss_ref)

        pc = p_ref[...]
        halo = jnp.where(i > 0, ph_ref[...], 0.0)
        pe = jnp.concatenate([halo, pc], axis=0)
        pooled_parts = []
        for gi, w in enumerate(POOL_WINDOWS):
            cur = pe[:, gi * POOL_GROUP:(gi + 1) * POOL_GROUP]
            span = 1
            while span < w:
                cur = cur + pltpu.roll(cur, span, 0)
                span *= 2
            wsum = cur[POOL_HALO:, :]
            mean = wsum / _window_counts(i * tm, tm, w)
            pooled_parts.append(mean - pc[:, gi * POOL_GROUP:(gi + 1) * POOL_GROUP])
        pooled_bf =[v.astype(BF16) for v in pooled_parts]
        mixed = jnp.concatenate([_dot(pooled_bf[gi], wm_ref[gi]) for gi in range(4)], axis=1) + bm_ref[...]
        ps = ps_ref[...]
        pool_out = mixed * ps
        gv = g_ref[...]
        sig = _sigmoid(gv)
        silu = gv * sig
        att = att_ref[...]
        y = jnp.concatenate([att * silu[:, :D_ATT], pool_out * silu[:, D_ATT:]], axis=1)
        y_bf = y.astype(BF16)
        wo = wo_ref[...]
        yo = _dot(y_bf, wo) + bo_ref[...]
        gate = gate_ref[...]
        h = ALPHA * x_ref[...] + gate * yo
        mu = jnp.mean(h, axis=1, keepdims=True)
        hc = h - mu
        var = jnp.mean(hc * hc, axis=1, keepdims=True)
        rstd = lax.rsqrt(var + LN_EPS)
        yhat = hc * rstd
        lg = lg_ref[...]
        out = yhat * lg + lb_ref[...]
        err = out - t_ref[...]
        loss_ref[...] += 0.5 * jnp.sum(jnp.mean(err * err, axis=1, keepdims=True), axis=0, keepdims=True)

        dout = err * (1.0 / D)
        g_ln_b = _colsum(dout)
        g_ln_g = _colsum(dout * yhat)
        dyh = dout * lg
        dh = rstd * (dyh - jnp.mean(dyh, axis=1, keepdims=True)
                     - yhat * jnp.mean(dyh * yhat, axis=1, keepdims=True))
        dh_ref[...] = dh
        d_gate = _colsum(dh * yo)
        dyo = gate * dh
        g_b_out = _colsum(dyo)
        dyo_bf = dyo.astype(BF16)
        gwo_ref[...] += _dot_tn(y_bf, dyo_bf)
        dy = _dot_nt(dyo_bf, wo)
        dsilu = sig * (1.0 + gv * (1.0 - sig))
        dy_a = dy[:, :D_ATT]
        dy_p = dy[:, D_ATT:]
        datt_ref[...] = (dy_a * silu[:, :D_ATT]).astype(BF16)
        dpo = dy_p * silu[:, D_ATT:]
        dg = jnp.concatenate([dy_a * att * dsilu[:, :D_ATT], dy_p * pool_out * dsilu[:, D_ATT:]], axis=1)
        dg_ref[...] = dg.astype(BF16)
        g_dg = _colsum(dg)
        g_ps = _colsum(dpo * mixed)
        dmixed = dpo * ps
        g_bm = _colsum(dmixed)
        dmixed_bf = dmixed.astype(BF16)
        dpl = []
        for gi in range(4):
            dm = dmixed_bf[:, gi * POOL_GROUP:(gi + 1) * POOL_GROUP]
            gwm_ref[gi] += _dot_tn(pooled_bf[gi], dm)
            dpl.append(_dot_nt(dm, wm_ref[gi]))
        dpl_ref[...] = jnp.concatenate(dpl, axis=1)
        vec_ref[0:1, :] += g_ln_g
        vec_ref[1:2, :] += g_ln_b
        vec_ref[2:3, :] += d_gate
        vec_ref[3:4, :] += g_b_out
        vec_ref[4:5, :] += g_dg
        vec_ref[5:6, 0:D_POOL] += g_ps
        vec_ref[6:7, 0:D_POOL] += g_bm

    row = lambda w: pl.BlockSpec((tm, w), lambda i: (i, 0))
    full2 = lambda a: pl.BlockSpec(a.shape, lambda i: (0, 0))
    full3 = lambda a: pl.BlockSpec(a.shape, lambda i: (0, 0, 0))
    return pl.pallas_call(
        body, name="middle", grid=(S // tm,),
        out_shape=(jax.ShapeDtypeStruct((S, D), F32),
                   jax.ShapeDtypeStruct((S, D_ATT), BF16),
                   jax.ShapeDtypeStruct((S, D), BF16),
                   jax.ShapeDtypeStruct((S, D_POOL), F32),
                   jax.ShapeDtypeStruct((D, D), F32),
                   jax.ShapeDtypeStruct((4, POOL_GROUP, POOL_GROUP), F32),
                   jax.ShapeDtypeStruct((8, D), F32),
                   jax.ShapeDtypeStruct((1, 1), F32)),
        in_specs=[row(D), row(D), row(D_ATT), row(D), row(D_POOL),
                  pl.BlockSpec((POOL_HALO, D_POOL), lambda i: (jnp.maximum(i * halo_blocks - 1, 0), 0)),
                  full2(gate), full3(w_mix), full2(b_mix), full2(pool_scale), full2(w_out), full2(b_out),
                  full2(ln_g), full2(ln_b)],
        out_specs=(row(D), row(D_ATT), row(D), row(D_POOL),
                   pl.BlockSpec((D, D), lambda i: (0, 0)),
                   pl.BlockSpec((4, POOL_GROUP, POOL_GROUP), lambda i: (0, 0, 0)),
                   pl.BlockSpec((8, D), lambda i: (0, 0)),
                   pl.BlockSpec((1, 1), lambda i: (0, 0))),
        compiler_params=_params(dimension_semantics=("arbitrary",)),
    )(x, tgt, att, g, p, p, gate, w_mix, b_mix, pool_scale, w_out, b_out, ln_g, ln_b)


def _tail(dpl, dfk8, f):
    S = dpl.shape[0]
    tm = min(TM_TAIL, S)
    n_t = S // tm
    halo_blocks = tm // POOL_HALO
    last_halo = S // POOL_HALO - 1

    def body(d_ref, dn_ref, dfk_ref, f_ref, dp_ref, df_ref, cs_ref, carry):
        s = pl.program_id(0)
        i = n_t - 1 - s

        @pl.when(s == 0)
        def _():
            carry[...] = jnp.zeros_like(carry)
            cs_ref[...] = jnp.zeros_like(cs_ref)

        dc = d_ref[...]
        nxt = jnp.where(s > 0, dn_ref[...], 0.0)
        de = jnp.concatenate([dc, nxt], axis=0)
        n_e = tm + POOL_HALO
        parts = []
        for gi, w in enumerate(POOL_WINDOWS):
            cur = de[:, gi * POOL_GROUP:(gi + 1) * POOL_GROUP] / _window_counts(i * tm, n_e, w)
            span = 1
            while span < w:
                cur = cur + pltpu.roll(cur, n_e - span, 0)
                span *= 2
            parts.append(cur[:tm, :] - dc[:, gi * POOL_GROUP:(gi + 1) * POOL_GROUP])
        dp = jnp.concatenate(parts, axis=1)
        dp_ref[...] = dp.astype(BF16)
        cs_ref[0:1, :] += _colsum(dp)

        r = lax.broadcasted_iota(jnp.int32, (tm, tm), 0)
        c = lax.broadcasted_iota(jnp.int32, (tm, tm), 1)
        tri = (r <= c).astype(F32)
        dlogf = jnp.dot(tri, dfk_ref[...], preferred_element_type=F32, precision=lax.Precision.HIGHEST) + carry[...]
        carry[...] = dlogf[0:1, :]
        df = dlogf * _sigmoid(-f_ref[...])
        df_ref[...] = df.astype(BF16)
        cs_ref[1:2, 0:128] += _colsum(df)

    rev = lambda w: pl.BlockSpec((tm, w), lambda s: (n_t - 1 - s, 0))
    return pl.pallas_call(
        body, name="tail", grid=(n_t,),
        out_shape=(jax.ShapeDtypeStruct((S, D_POOL), BF16), jax.ShapeDtypeStruct((S, 128), BF16),
                   jax.ShapeDtypeStruct((8, D_POOL), F32)),
        in_specs=[rev(D_POOL),
                  pl.BlockSpec((POOL_HALO, D_POOL),
                               lambda s: (jnp.minimum((n_t - s) * halo_blocks, last_halo), 0)),
                  rev(128), rev(128)],
        out_specs=(rev(D_POOL), rev(128), pl.BlockSpec((8, D_POOL), lambda s: (0, 0))),
        scratch_shapes=[pltpu.VMEM((1, 128), F32)],
        compiler_params=_params(dimension_semantics=("arbitrary",)),
    )(dpl, dpl, dfk8, f)


PIECES = ((O_QKV, D_ATT), (O_QKV + D_ATT, D_ATT), (O_QKV + 2 * D_ATT, D_ATT), (O_F, 128), (O_P, D_POOL), (O_G, D))


def _grad_w_in(u, pieces):
    S = u.shape[0]
    tm = min(TM_GW, S)
    n_t = S // tm

    def body(u_ref, *rest):
        piece_refs, out_ref, acc, sem = rest[:6], rest[6], rest[7], rest[8]
        i = pl.program_id(0)

        @pl.when(i == 0)
        def _():
            acc[...] = jnp.zeros_like(acc)

        u_t = u_ref[...]
        for (off, w), ref in zip(PIECES, piece_refs):
            acc[:, off:off + w] += _dot_tn(u_t, ref[...])

        @pl.when(i == n_t - 1)
        def _():
            cp = pltpu.make_async_copy(acc, out_ref, sem)
            cp.start()
            cp.wait()

    return pl.pallas_call(
        body, name="grad_w_in", grid=(n_t,),
        out_shape=jax.ShapeDtypeStruct((D, D_PAD), F32),
        in_specs=[pl.BlockSpec((tm, D), lambda i: (i, 0))]
        + [pl.BlockSpec((tm, w), lambda i: (i, 0)) for _, w in PIECES],
        out_specs=pl.BlockSpec(memory_space=pl.ANY),
        scratch_shapes=[pltpu.VMEM((D, D_PAD), F32), pltpu.SemaphoreType.DMA],
        compiler_params=_params(dimension_semantics=("arbitrary",)),
    )(u, *pieces)


def _grad_x(pieces, wt_pad, dh, x, scale):
    S = x.shape[0]
    tm = min(TM_DU, S)

    def body(*refs):
        piece_refs = refs[:6]
        w_ref, dh_ref, x_ref, sc_ref, gx_ref, vec_ref = refs[6:]

        @pl.when(pl.program_id(0) == 0)
        def _():
            vec_ref[...] = jnp.zeros_like(vec_ref)

        du = jnp.zeros((tm, D), F32)
        for (off, w), ref in zip(PIECES, piece_refs):
            du = du + _dot(ref[...], w_ref[off:off + w, :])
        xv = x_ref[...]
        gx_ref[...] = ALPHA * dh_ref[...] + du * (1.0 + sc_ref[...])
        vec_ref[0:1, :] += _colsum(du)
        vec_ref[1:2, :] += _colsum(du * xv)

    row = lambda w: pl.BlockSpec((tm, w), lambda i: (i, 0))
    return pl.pallas_call(
        body, name="grad_x", grid=(S // tm,),
        out_shape=(jax.ShapeDtypeStruct((S, D), F32), jax.ShapeDtypeStruct((8, D), F32)),
        in_specs=[row(w) for _, w in PIECES]
        + [pl.BlockSpec(wt_pad.shape, lambda i: (0, 0)), row(D), row(D), pl.BlockSpec((1, D), lambda i: (0, 0))],
        out_specs=(row(D), pl.BlockSpec((8, D), lambda i: (0, 0))),
        compiler_params=_params(dimension_semantics=("arbitrary",)),
    )(*pieces, wt_pad, dh, x, scale)


def _grad_ada(c_all, dada_all, dada_cols):
    def body(c_ref, dall_ref, dcol_ref, gw_ref, gb_ref):
        rows = lax.broadcasted_iota(jnp.int32, (8, 1), 0)
        cm = jnp.zeros((8, D), F32)
        dm = jnp.zeros((8, 3 * D), F32)
        for r in range(8):
            cm = jnp.where(rows == r, c_ref[r], cm)
            dm = jnp.where(rows == r, dall_ref[r], dm)
        act = cm * _sigmoid(cm)
        pad = jnp.zeros((8, D), F32)
        lhs = jnp.concatenate([act, pad], axis=0).astype(BF16)
        rhs = jnp.concatenate([dcol_ref[...], jnp.zeros((8, SHARD_ADA), F32)], axis=0).astype(BF16)
        gw_ref[...] = _dot_tn(lhs, rhs)
        gb_ref[...] = _colsum(dm)

    vm = pl.BlockSpec(memory_space=pltpu.VMEM)
    return pl.pallas_call(
        body, name="grad_ada",
        out_shape=(jax.ShapeDtypeStruct((D, SHARD_ADA), F32), jax.ShapeDtypeStruct((1, 3 * D), F32)),
        in_specs=[vm, vm, vm], out_specs=(vm, vm),
        compiler_params=_params(),
    )(c_all, dada_all, dada_cols)


def _adamw_math(w, g, m, v):
    m = ADAM_B1 * m + (1.0 - ADAM_B1) * g
    v = ADAM_B2 * v + (1.0 - ADAM_B2) * (g * g)
    m_hat = m / (1.0 - ADAM_B1 ** ADAM_STEP)
    v_hat = v / (1.0 - ADAM_B2 ** ADAM_STEP)
    delta = -ADAM_LR * (m_hat / (jnp.sqrt(v_hat) + ADAM_EPS) + ADAM_WD * w)
    return delta, m, v


def _adamw(groups, n_steps):
    n = len(groups)

    def body(*refs):
        ins, outs = refs[:4 * n], refs[4 * n:]
        for t in range(n):
            w, g, m, v = (r[...] for r in ins[4 * t:4 * t + 4])
            d, m2, v2 = _adamw_math(w, g, m, v)
            outs[3 * t][...] = d
            outs[3 * t + 1][...] = m2
            outs[3 * t + 2][...] = v2

    in_specs, out_specs, out_shape, args = [], [], [], []
    for (w, g, m, v) in groups:
        rest = w.shape[1:]
        spec = pl.BlockSpec((w.shape[0] // n_steps,) + rest, lambda i, nd=len(rest): (i,) + (0,) * nd)
        in_specs += [spec] * 4
        out_specs += [spec] * 3
        out_shape += [jax.ShapeDtypeStruct(w.shape, F32)] * 3
        args += [w, g, m, v]
    return pl.pallas_call(
        body, name="adamw_%d_%d" % (n, n_steps), grid=(n_steps,),
        out_shape=tuple(out_shape), in_specs=in_specs, out_specs=tuple(out_specs),
        compiler_params=_params(dimension_semantics=("arbitrary",)),
    )(*args)


def _pack_small(parts):
    rows = []
    used = 0
    for name, (first, n_rows) in SMALL_SEGS.items():
        if first > used:
            rows.append(jnp.zeros((first - used, 128), F32))
        flat = parts[name].reshape(-1)
        flat = jnp.pad(flat, (0, n_rows * 128 - flat.shape[0]))
        rows.append(flat.reshape(n_rows, 128))
        used = first + n_rows
    rows.append(jnp.zeros((SMALL_ROWS - used, 128), F32))
    return jnp.concatenate(rows, axis=0)


def _unpack_small(buf, name, shape):
    first, n_rows = SMALL_SEGS[name]
    n = int(np.prod(shape))
    return buf[first:first + n_rows].reshape(-1)[:n].reshape(shape)


def _pad_in(v):
    r = v.shape[0]
    z = jnp.zeros((r, O_P - O_F - N_HEADS), v.dtype)
    return jnp.concatenate([v[:, :3 * D_ATT + N_HEADS], z, v[:, 3 * D_ATT + N_HEADS:]], axis=1)


def _unpad_in(v):
    return jnp.concatenate([v[:, :O_F + N_HEADS], v[:, O_P:]], axis=1)


def _shards_in(v):
    gap = O_P - (O_F + N_HEADS)
    parts = []
    for a in range(N_CHIPS):
        lo, hi = a * SHARD_IN, (a + 1) * SHARD_IN
        cut = O_F + N_HEADS
        if hi <= cut:
            parts.append(v[:, lo:hi])
        elif lo >= cut:
            parts.append(v[:, lo + gap:hi + gap])
        else:
            parts.append(jnp.concatenate([v[:, lo:cut], v[:, cut + gap:hi + gap]], axis=1))
    return jnp.stack(parts, axis=0)


def kernel(x, c, w_ada, b_ada, w_in, b_in, w_pool_mix, b_pool_mix, pool_scale, w_out, b_out, ln_g, ln_b, loss_target, m_w_ada, m_b_ada, m_w_in, m_b_in, m_w_pool_mix, m_b_pool_mix, m_pool_scale, m_w_out, m_b_out, m_ln_g, m_ln_b, v_w_ada, v_b_ada, v_w_in, v_b_in, v_w_pool_mix, v_b_pool_mix, v_pool_scale, v_w_out, v_b_out, v_ln_g, v_ln_b):
    S = x.shape[1]
    T = min(T_ATT, S)
    n_t = S // T
    chip = 2 * lax.axis_index("x") + lax.axis_index("y")
    x2 = x[0]
    tgt = loss_target[0]
    q_scale = jnp.concatenate([jnp.full((1, D_ATT), Q_SCALE, F32), jnp.ones((1, D_PAD - D_ATT), F32)], axis=1)

    to_cols = lambda a: jnp.transpose(a, (2, 0, 1))
    from_cols = lambda a: jnp.transpose(a, (1, 2, 0))
    c_all, ada4, wt_in_all, w_out_all = _gather_and_ada(
        c, w_ada[0], b_ada.reshape(4, 1, SHARD_ADA), to_cols(w_in).reshape(SHARD_IN, D).astype(BF16),
        w_out[0].astype(BF16))
    ada = ada4[:, 0, :].reshape(1, 3 * D)
    shift, scale, gate = ada[:, :D], ada[:, D:2 * D], ada[:, 2 * D:]
    wt_full = wt_in_all.reshape(D_IN, D)
    n_real = 3 * D_ATT + N_HEADS
    wt_pad = jnp.concatenate([wt_full[:D_ATT] * jnp.asarray(Q_SCALE, BF16), wt_full[D_ATT:n_real],
                              jnp.zeros((O_P - n_real, D), BF16), wt_full[n_real:]], axis=0)
    b_pad = _pad_in(b_in) * q_scale
    w_out_full = w_out_all.reshape(D, D)
    w_mix_bf = w_pool_mix[0].astype(BF16)

    u, qkv, f, p, g = _in_proj(x2, shift, scale, wt_pad, b_pad)
    big_f = _forget_cumsum(f)
    att, lse = _attention_fwd(qkv, big_f)

    dh, datt, dg, dpl, gw_out, gw_mix, vec, loss_part = _middle(
        x2, tgt, att, g, p, gate, w_mix_bf, b_pool_mix.reshape(1, D_POOL), pool_scale, w_out_full, b_out, ln_g, ln_b)
    dq, dk, dv, cs_att, dfk, dfq = _attention_bwd(qkv, datt, att, lse, big_f)
    dfk8 = jnp.transpose(dfk[:, :, 0:2], (1, 0, 2)).reshape(S, N_HEADS)
    dfk8 = dfk8 + jnp.transpose(dfq[:, :, 0:2, :], (1, 3, 0, 2)).reshape(S, N_HEADS)
    dfk8 = jnp.pad(dfk8, ((0, 0), (0, 128 - N_HEADS)))
    dp, df, cs_tail = _tail(dpl, dfk8, f)
    pieces = (dq, dk, dv, df, dp, dg)
    gw_pad = _grad_w_in(u, pieces)
    grad_x, vec_x = _grad_x(pieces, wt_pad, dh, x2, scale)

    cs_qkv = jnp.transpose(cs_att.reshape(N_PAIR, 3, 128), (1, 0, 2)).reshape(1, 3 * D_ATT)
    gb_pad = jnp.concatenate([cs_qkv, cs_tail[1:2, 0:128], cs_tail[0:1, :], vec[4:5, :]], axis=1) * q_scale
    dada = jnp.concatenate([vec_x[0:1, :], vec_x[1:2, :], vec[2:3, :]], axis=1)
    small = _pack_small({
        "b_in": gb_pad, "w_pool_mix": gw_mix, "b_pool_mix": vec[6:7, :D_POOL], "pool_scale": vec[5:6, :D_POOL],
        "b_out": vec[3:4, :], "ln_g": vec[0:1, :], "ln_b": vec[1:2, :], "loss": loss_part,
        "b_ada": jnp.zeros((1, 3 * D), F32)})

    g_w_in, g_w_out, small_sum, dada_all = _reduce_all(
        _shards_in(gw_pad), _shards_in(q_scale), gw_out.reshape(N_CHIPS, SHARD_OUT, D),
        jnp.ones((N_CHIPS, 1, D), F32), small[:SMALL_REDUCED_ROWS], dada)
    dada_cols = lax.dynamic_slice(dada_all[:, 0, :], (0, chip * SHARD_ADA), (8, SHARD_ADA))
    g_w_ada, g_b_ada = _grad_ada(c_all, dada_all, dada_cols)
    loss = _unpack_small(small_sum, "loss", (1,))[0]

    grads_small = jnp.concatenate([small_sum, g_b_ada.reshape(24, 128)], axis=0)
    no_param = jnp.zeros((1,), F32)
    small_w = {"b_in": _pad_in(b_in), "w_pool_mix": w_pool_mix, "b_pool_mix": b_pool_mix, "pool_scale": pool_scale,
               "b_out": b_out, "ln_g": ln_g, "ln_b": ln_b, "loss": no_param, "b_ada": b_ada}
    small_m = {"b_in": _pad_in(m_b_in), "w_pool_mix": m_w_pool_mix, "b_pool_mix": m_b_pool_mix,
               "pool_scale": m_pool_scale, "b_out": m_b_out, "ln_g": m_ln_g, "ln_b": m_ln_b, "loss": no_param,
               "b_ada": m_b_ada}
    small_v = {"b_in": _pad_in(v_b_in), "w_pool_mix": v_w_pool_mix, "b_pool_mix": v_b_pool_mix,
               "pool_scale": v_pool_scale, "b_out": v_b_out, "ln_g": v_ln_g, "ln_b": v_ln_b, "loss": no_param,
               "b_ada": v_b_ada}
    big = _adamw([(w_ada[0], g_w_ada, m_w_ada[0], v_w_ada[0]),
                  (w_out[0], g_w_out, m_w_out[0], v_w_out[0])], 8)
    g_w_in_cols = to_cols(g_w_in[None])
    big_in = _adamw([(to_cols(w_in), g_w_in_cols, to_cols(m_w_in), to_cols(v_w_in))], 14)
    sm = _adamw([(_pack_small(small_w), grads_small, _pack_small(small_m), _pack_small(small_v))], 1)

    names = ["w_ada", "b_ada", "w_in", "b_in", "w_pool_mix", "b_pool_mix", "pool_scale", "w_out", "b_out",
             "ln_g", "ln_b"]
    shapes = {"b_ada": (1, 3 * D), "b_in": (1, D_PAD), "w_pool_mix": (1, 4, POOL_GROUP, POOL_GROUP),
              "b_pool_mix": (1, 4, POOL_GROUP), "pool_scale": (1, D_POOL), "b_out": (1, D), "ln_g": (1, D),
              "ln_b": (1, D)}
    big_idx = {"w_ada": 0, "w_out": 1}

    def leaf(kind, name):
        if name == "w_in":
            return from_cols(g_w_in_cols if kind == 0 else big_in[kind - 1])
        if name in big_idx:
            if kind == 0:
                return (g_w_ada, g_w_out)[big_idx[name]][None]
            return big[3 * big_idx[name] + kind - 1][None]
        buf = grads_small if kind == 0 else sm[kind - 1]
        val = _unpack_small(buf, name, shapes[name])
        if name == "b_in":
            val = _unpad_in(val)
        return val

    outs = [loss, grad_x[None]]
    for kind in range(4):
        outs += [leaf(kind, n) for n in names]
    return tuple(outs)
```

```python
import functools

import numpy as np
import jax
import jax.numpy as jnp
from jax import lax
from jax.experimental import pallas as pl
from jax.experimental.pallas import tpu as pltpu

F32 = jnp.float32
BF16 = jnp.bfloat16
MESH = pl.DeviceIdType.MESH

D = 1024
D_ATT = 512
D_POOL = 512
N_HEADS = 8
HEAD_DIM = 64
N_PAIR = N_HEADS // 2
POOL_WINDOWS = (2, 4, 8, 16)
POOL_GROUP = 128
POOL_HALO = 16
LN_EPS = 1e-5
ALPHA = 2.0 ** 0.25
D_IN = 3 * D_ATT + N_HEADS + D_POOL + D_ATT + D_POOL
N_CHIPS = 4
SHARD_IN = D_IN // N_CHIPS
SHARD_ADA = 3 * D // N_CHIPS
SHARD_OUT = D // N_CHIPS

O_QKV, O_F, O_P, O_G, D_PAD = 0, 1536, 1664, 2176, 3200
Q_SCALE = HEAD_DIM ** -0.5

ADAM_LR, ADAM_B1, ADAM_B2, ADAM_EPS, ADAM_WD, ADAM_STEP = 0.001, 0.9, 0.999, 1e-08, 0.01, 10

NEG = -1e30

VMEM_LIMIT = 56 * 1024 * 1024

TM_PROJ = 512
T_ATT = 512
TM_MID = 256
TM_GW = 512
TM_DU = 512

REL7 = [(0, 0, 1), (0, 1, 0), (0, 1, 1), (1, 0, 0), (1, 0, 1), (1, 1, 0), (1, 1, 1)]
REL3 = [(0, 1), (1, 0), (1, 1)]

SMALL_SEGS = {}
_row = 0
for _name, _n in (("b_in", 3200), ("w_pool_mix", 65536), ("b_pool_mix", 512), ("pool_scale", 512),
                  ("b_out", 1024), ("ln_g", 1024), ("ln_b", 1024), ("loss", 1)):
    _rows = -(-_n // 1024) * 8
    SMALL_SEGS[_name] = (_row, _rows)
    _row += _rows
SMALL_REDUCED_ROWS = -(-_row // 16) * 16
SMALL_SEGS["b_ada"] = (SMALL_REDUCED_ROWS, 24)
SMALL_ROWS = SMALL_REDUCED_ROWS + 24


def _params(**kw):
    return pltpu.CompilerParams(vmem_limit_bytes=VMEM_LIMIT, **kw)


def _flip(v, d):
    return v if d == 0 else 1 - v


def _dot(a, b):
    return jnp.dot(a, b, preferred_element_type=F32)


def _dot_nt(a, b):
    return lax.dot_general(a, b, (((1,), (1,)), ((), ())), preferred_element_type=F32)


def _dot_tn(a, b):
    return lax.dot_general(a, b, (((0,), (0,)), ((), ())), preferred_element_type=F32)


def _sigmoid(v):
    return 1.0 / (1.0 + jnp.exp(-v))


def _colsum(v):
    return jnp.sum(v, axis=0, keepdims=True)


def _gather_and_ada(c, w_ada, b_ada4, w_in_sh, w_out_sh):
    def body(c_ref, w_ref, b_ref, win_ref, wout_ref, call_ref, ada_ref, win_all, wout_all,
             cslab, sbuf, rbuf, cs_sem, cr_sem, as_sem, ar_sem, own_sem, s_sem, r_sem, fs_sem, fr_sem):
        x, y, cc = lax.axis_index("x"), lax.axis_index("y"), lax.axis_index("c")
        me = 4 * x + 2 * y + cc
        chip = 2 * x + y
        sib = (x, y, 1 - cc)
        srcs = (win_ref, wout_ref)
        dsts = (win_all, wout_all)

        def half(t, which):
            if t == 0:
                return (slice(None), pl.ds(pl.multiple_of(which * (D // 2), D // 2), D // 2))
            return (pl.ds(pl.multiple_of(which * (SHARD_OUT // 2), SHARD_OUT // 2), SHARD_OUT // 2), slice(None))

        own = [pltpu.make_async_copy(srcs[t], dsts[t].at[chip], own_sem.at[t]) for t in range(2)]
        for cp in own:
            cp.start()
        first = []
        for k, (dx, dy) in enumerate(REL3):
            peer = (_flip(x, dx), _flip(y, dy), cc)
            for t in range(2):
                cp = pltpu.make_async_remote_copy(
                    src_ref=srcs[t].at[half(t, cc)], dst_ref=dsts[t].at[(chip,) + half(t, cc)],
                    send_sem=s_sem.at[2 * k + t], recv_sem=r_sem.at[2 * k + t],
                    device_id=peer, device_id_type=MESH)
                cp.start()
                first.append(cp)

        cslab[...] = jnp.broadcast_to(c_ref[...], (8, D))
        call_ref[me] = cslab[...]
        gathers = []
        for k, (dx, dy, dc) in enumerate(REL7):
            cp = pltpu.make_async_remote_copy(
                src_ref=cslab, dst_ref=call_ref.at[me], send_sem=cs_sem.at[k], recv_sem=cr_sem.at[k],
                device_id=(_flip(x, dx), _flip(y, dy), _flip(cc, dc)), device_id_type=MESH)
            cp.start()
            gathers.append(cp)
        for cp in gathers:
            cp.wait()
        slab_row = lax.broadcasted_iota(jnp.int32, (8, 1), 0)
        mat = jnp.zeros((8, D), F32)
        for r in range(8):
            mat = jnp.where(slab_row == r, call_ref[r], mat)
        act = (mat * _sigmoid(mat)).astype(BF16)
        part = _dot(act, w_ref[...].astype(BF16))
        sends = []
        for k, (dx, dy) in enumerate(REL3):
            px, py = _flip(x, dx), _flip(y, dy)
            r = 4 * px + 2 * py + cc
            piece = _colsum(jnp.where(slab_row == r, part, 0.0))
            sbuf[k] = jnp.broadcast_to(piece, (8, SHARD_ADA))
            cp = pltpu.make_async_remote_copy(
                src_ref=sbuf.at[k], dst_ref=rbuf.at[k], send_sem=as_sem.at[k], recv_sem=ar_sem.at[k],
                device_id=(px, py, cc), device_id_type=MESH)
            cp.start()
            sends.append(cp)
        own_piece = _colsum(jnp.where(slab_row == me, part, 0.0))
        ada_ref[chip] = jnp.broadcast_to(own_piece, (8, SHARD_ADA)) + b_ref[chip]
        for k, (dx, dy) in enumerate(REL3):
            sends[k].wait()
            a = 2 * _flip(x, dx) + _flip(y, dy)
            ada_ref[a] = rbuf[k] + b_ref[a]

        passed = []
        for k, (dx, dy) in enumerate(REL3):
            a = 2 * _flip(x, dx) + _flip(y, dy)
            for t in range(2):
                landed = dsts[t].at[(a,) + half(t, cc)]
                pltpu.make_async_remote_copy(
                    src_ref=landed, dst_ref=landed, send_sem=s_sem.at[2 * k + t], recv_sem=r_sem.at[2 * k + t],
                    device_id=sib, device_id_type=MESH).wait_recv()
                cp = pltpu.make_async_remote_copy(
                    src_ref=landed, dst_ref=landed, send_sem=fs_sem.at[2 * k + t], recv_sem=fr_sem.at[2 * k + t],
                    device_id=sib, device_id_type=MESH)
                cp.start()
                passed.append(cp)
        for k, (dx, dy) in enumerate(REL3):
            a = 2 * _flip(x, dx) + _flip(y, dy)
            for t in range(2):
                other = dsts[t].at[(a,) + half(t, 1 - cc)]
                pltpu.make_async_remote_copy(
                    src_ref=other, dst_ref=other, send_sem=fs_sem.at[2 * k + t], recv_sem=fr_sem.at[2 * k + t],
                    device_id=sib, device_id_type=MESH).wait_recv()
        for cp in first + passed:
            cp.wait_send()
        for cp in own:
            cp.wait()

    vm = pl.BlockSpec(memory_space=pltpu.VMEM)
    return pl.pallas_call(
        body, name="gather_and_ada",
        out_shape=(jax.ShapeDtypeStruct((8, 8, D), F32), jax.ShapeDtypeStruct((4, 8, SHARD_ADA), F32),
                   jax.ShapeDtypeStruct((N_CHIPS, SHARD_IN, D), BF16),
                   jax.ShapeDtypeStruct((N_CHIPS, SHARD_OUT, D), BF16)),
        in_specs=[vm] * 5, out_specs=(vm,) * 4,
        scratch_shapes=[pltpu.VMEM((8, D), F32), pltpu.VMEM((3, 8, SHARD_ADA), F32),
                        pltpu.VMEM((3, 8, SHARD_ADA), F32),
                        pltpu.SemaphoreType.DMA((7,)), pltpu.SemaphoreType.DMA((7,)),
                        pltpu.SemaphoreType.DMA((3,)), pltpu.SemaphoreType.DMA((3,)),
                        pltpu.SemaphoreType.DMA((2,)), pltpu.SemaphoreType.DMA((6,)),
                        pltpu.SemaphoreType.DMA((6,)), pltpu.SemaphoreType.DMA((6,)),
                        pltpu.SemaphoreType.DMA((6,))],
        compiler_params=_params(),
    )(c, w_ada, b_ada4, w_in_sh, w_out_sh)


def _scatter_stages(pos, g_ref, sc_ref, out_ref, sib_buf, send_buf, ici_buf, sem1, sem2s, sem2r, sem3):
    x, y, cc, chip, sib = pos
    RH = g_ref.shape[1] // 2
    mine = pl.ds(pl.multiple_of(cc * RH, RH), RH)
    theirs = pl.ds(pl.multiple_of((1 - cc) * RH, RH), RH)
    cp1 = pltpu.make_async_remote_copy(
        src_ref=g_ref.at[:, theirs, :], dst_ref=sib_buf, send_sem=sem1.at[0], recv_sem=sem1.at[1],
        device_id=sib, device_id_type=MESH)
    sends = []
    for k, (dx, dy) in enumerate(REL3):
        px, py = _flip(x, dx), _flip(y, dy)
        sends.append(pltpu.make_async_remote_copy(
            src_ref=send_buf.at[2 * px + py], dst_ref=ici_buf.at[chip],
            send_sem=sem2s.at[k], recv_sem=sem2r.at[k], device_id=(px, py, cc), device_id_type=MESH))
    cp3 = pltpu.make_async_remote_copy(
        src_ref=out_ref.at[mine, :], dst_ref=out_ref.at[mine, :], send_sem=sem3.at[0], recv_sem=sem3.at[1],
        device_id=sib, device_id_type=MESH)

    def finish1():
        cp1.wait()
        for a in range(N_CHIPS):
            both = g_ref[a, mine, :] + sib_buf[a]
            sib_buf[a] = both
            send_buf[a] = both.astype(BF16)

    def start2():
        for cp in sends:
            cp.start()
        ici_buf[chip] = send_buf[chip]

    def finish2():
        for cp in sends:
            cp.wait()
        own = sib_buf[chip]
        parts = [jnp.where(chip == a, own, ici_buf[a].astype(F32)) for a in range(N_CHIPS)]
        out_ref[mine, :] = ((parts[0] + parts[1]) + (parts[2] + parts[3])) * sc_ref[chip]

    return [(cp1.start, finish1), (start2, finish2), (cp3.start, cp3.wait)]


def _all_reduce_stages(pos, g_ref, out_ref, sib_buf, ici_buf, sem1, sem2s, sem2r, sem3):
    x, y, cc, chip, sib = pos
    RH = g_ref.shape[0] // 2
    mine = pl.ds(pl.multiple_of(cc * RH, 8), RH)
    theirs = pl.ds(pl.multiple_of((1 - cc) * RH, 8), RH)
    cp1 = pltpu.make_async_remote_copy(
        src_ref=g_ref.at[theirs, :], dst_ref=sib_buf, send_sem=sem1.at[0], recv_sem=sem1.at[1],
        device_id=sib, device_id_type=MESH)
    sends = []
    for k, (dx, dy) in enumerate(REL3):
        px, py = _flip(x, dx), _flip(y, dy)
        sends.append(pltpu.make_async_remote_copy(
            src_ref=sib_buf, dst_ref=ici_buf.at[chip],
            send_sem=sem2s.at[k], recv_sem=sem2r.at[k], device_id=(px, py, cc), device_id_type=MESH))
    cp3 = pltpu.make_async_remote_copy(
        src_ref=out_ref.at[mine, :], dst_ref=out_ref.at[mine, :], send_sem=sem3.at[0], recv_sem=sem3.at[1],
        device_id=sib, device_id_type=MESH)

    def finish1():
        cp1.wait()
        sib_buf[...] = g_ref[mine, :] + sib_buf[...]

    def start2():
        for cp in sends:
            cp.start()
        ici_buf[chip] = sib_buf[...]

    def finish2():
        for cp in sends:
            cp.wait()
        out_ref[mine, :] = (ici_buf[0] + ici_buf[1]) + (ici_buf[2] + ici_buf[3])

    return [(cp1.start, finish1), (start2, finish2), (cp3.start, cp3.wait)]


def _reduce_all(g4_in, sc_in, g4_out, sc_out, small, dada):
    shapes = [g4_in.shape[1:], g4_out.shape[1:]]
    R = small.shape[0]
    W = dada.shape[1]
    n_sem = 4

    def body(gin_ref, scin_ref, gout_ref, scout_ref, sm_ref, d_ref, oin_ref, oout_ref, osm_ref, dall_ref, *scratch):
        x, y, cc = lax.axis_index("x"), lax.axis_index("y"), lax.axis_index("c")
        me = 4 * x + 2 * y + cc
        pos = (x, y, cc, 2 * x + y, (x, y, 1 - cc))
        dslab, ds_sem, dr_sem = scratch[0:3]
        rest = scratch[3:]
        in_bufs, rest = rest[0:3 + n_sem], rest[3 + n_sem:]
        out_bufs, rest = rest[0:3 + n_sem], rest[3 + n_sem:]
        sm_bufs = rest
        dslab[...] = jnp.broadcast_to(d_ref[...], (8, W))
        dall_ref[me] = dslab[...]
        gathers = []
        for k, (dx, dy, dc) in enumerate(REL7):
            cp = pltpu.make_async_remote_copy(
                src_ref=dslab, dst_ref=dall_ref.at[me], send_sem=ds_sem.at[k], recv_sem=dr_sem.at[k],
                device_id=(_flip(x, dx), _flip(y, dy), _flip(cc, dc)), device_id_type=MESH)
            cp.start()
            gathers.append(cp)
        plans = [_scatter_stages(pos, gin_ref, scin_ref, oin_ref, *in_bufs),
                 _scatter_stages(pos, gout_ref, scout_ref, oout_ref, *out_bufs),
                 _all_reduce_stages(pos, sm_ref, osm_ref, *sm_bufs)]
        for stage in range(3):
            for plan in plans:
                plan[stage][0]()
            for plan in plans:
                plan[stage][1]()
        for cp in gathers:
            cp.wait()

    def sems():
        return [pltpu.SemaphoreType.DMA((2,)), pltpu.SemaphoreType.DMA((3,)),
                pltpu.SemaphoreType.DMA((3,)), pltpu.SemaphoreType.DMA((2,))]

    scratch = [pltpu.VMEM((8, W), F32), pltpu.SemaphoreType.DMA((7,)), pltpu.SemaphoreType.DMA((7,))]
    for r, c in shapes:
        scratch += [pltpu.VMEM((N_CHIPS, r // 2, c), F32), pltpu.VMEM((N_CHIPS, r // 2, c), BF16),
                    pltpu.VMEM((N_CHIPS, r // 2, c), BF16)] + sems()
    scratch += [pltpu.VMEM((R // 2, 128), F32), pltpu.VMEM((N_CHIPS, R // 2, 128), F32)] + sems()
    vm = pl.BlockSpec(memory_space=pltpu.VMEM)
    return pl.pallas_call(
        body, name="reduce_all",
        out_shape=(jax.ShapeDtypeStruct(g4_in.shape[1:], F32), jax.ShapeDtypeStruct(g4_out.shape[1:], F32),
                   jax.ShapeDtypeStruct((R, 128), F32), jax.ShapeDtypeStruct((8, 8, W), F32)),
        in_specs=[vm] * 6, out_specs=(vm,) * 4,
        scratch_shapes=scratch,
        compiler_params=_params(),
    )(g4_in, sc_in, g4_out, sc_out, small, dada)


def _in_proj(x, shift, scale, wt_pad, b_pad):
    S = x.shape[0]
    tm = min(TM_PROJ, S)

    def body(x_ref, sh_ref, sc_ref, w_ref, b_ref, u_ref, qkv_ref, f_ref, p_ref, g_ref):
        u = (x_ref[...] * (1.0 + sc_ref[...]) + sh_ref[...]).astype(BF16)
        u_ref[...] = u
        qkv_ref[...] = (_dot_nt(u, w_ref[O_QKV:O_F, :]) + b_ref[:, O_QKV:O_F]).astype(BF16)
        f_ref[...] = _dot_nt(u, w_ref[O_F:O_P, :]) + b_ref[:, O_F:O_P]
        p_ref[...] = _dot_nt(u, w_ref[O_P:O_G, :]) + b_ref[:, O_P:O_G]
        g_ref[...] = _dot_nt(u, w_ref[O_G:D_PAD, :]) + b_ref[:, O_G:D_PAD]

    row = lambda w: pl.BlockSpec((tm, w), lambda i: (i, 0))
    full = lambda a: pl.BlockSpec(a.shape, lambda i: (0, 0))
    return pl.pallas_call(
        body, name="in_proj", grid=(S // tm,),
        out_shape=(jax.ShapeDtypeStruct((S, D), BF16), jax.ShapeDtypeStruct((S, 3 * D_ATT), BF16),
                   jax.ShapeDtypeStruct((S, 128), F32), jax.ShapeDtypeStruct((S, D_POOL), F32),
                   jax.ShapeDtypeStruct((S, D), F32)),
        in_specs=[row(D), full(shift), full(scale), full(wt_pad), full(b_pad)],
        out_specs=(row(D), row(3 * D_ATT), row(128), row(D_POOL), row(D)),
        compiler_params=_params(dimension_semantics=("arbitrary",)),
    )(x, shift, scale, wt_pad, b_pad)


def _forget_cumsum(f):
    S = f.shape[0]
    tm = min(T_ATT, S)

    def body(f_ref, out_ref, carry):
        @pl.when(pl.program_id(0) == 0)
        def _():
            carry[...] = jnp.zeros_like(carry)
        v = f_ref[...]
        logf = jnp.minimum(v, 0.0) - jnp.log(1.0 + jnp.exp(-jnp.abs(v)))
        r = lax.broadcasted_iota(jnp.int32, (tm, tm), 0)
        c = lax.broadcasted_iota(jnp.int32, (tm, tm), 1)
        tri = (r <= c).astype(F32)
        rows8 = logf.T[0:8, :]
        cum8 = jnp.dot(rows8, tri, preferred_element_type=F32, precision=lax.Precision.HIGHEST) + carry[...]
        out_ref[...] = jnp.concatenate([cum8, jnp.zeros((128 - 8, tm), F32)], axis=0).T
        last = lax.broadcasted_iota(jnp.int32, (1, tm), 1) == tm - 1
        carry[...] = jnp.sum(jnp.where(last, cum8, 0.0), axis=1, keepdims=True)

    return pl.pallas_call(
        body, name="forget_cumsum", grid=(S // tm,),
        out_shape=jax.ShapeDtypeStruct((S, 128), F32),
        in_specs=[pl.BlockSpec((tm, 128), lambda i: (i, 0))],
        out_specs=pl.BlockSpec((tm, 128), lambda i: (i, 0)),
        scratch_shapes=[pltpu.VMEM((8, 1), F32)],
        compiler_params=_params(dimension_semantics=("arbitrary",)),
    )(f)


def _split3(v):
    hi = v.astype(BF16)
    rest = v - hi.astype(F32)
    mid = rest.astype(BF16)
    lo = (rest - mid.astype(F32)).astype(BF16)
    return hi, mid, lo


def _attention_fwd(qkv, big_f):
    S = qkv.shape[0]
    T = min(T_ATT, S)
    n_t = S // T

    def body(q_ref, k_ref, v_ref, f_ref, o_ref, lse_ref, kaug_sc, vt_sc, m_sc, l_sc, acc_sc):
        hp = pl.program_id(0)
        i = pl.program_id(1)
        lane = lax.broadcasted_iota(jnp.int32, (1, 128), 1)
        sub = lax.broadcasted_iota(jnp.int32, (128, 1), 0)
        head_sel = (lane < HEAD_DIM, lane >= HEAD_DIM)
        head_sel_t = (sub < HEAD_DIM, sub >= HEAD_DIM)
        spare = (HEAD_DIM, 0)
        zero = jnp.zeros((), BF16)

        @pl.when(i == 0)
        def _():
            def prep(jt, carry):
                rows = pl.ds(pl.multiple_of(jt * T, T), T)
                k = k_ref[rows, :]
                ft = f_ref[rows, :]
                vt = v_ref[rows, :].astype(F32).T
                for h in range(2):
                    fh = jnp.sum(jnp.where(lane == 2 * hp + h, ft, 0.0), axis=1, keepdims=True)
                    hi, mid, lo = _split3(-fh)
                    b = spare[h]
                    bias = jnp.where(lane == b, hi, jnp.where(lane == b + 1, mid, jnp.where(lane == b + 2, lo, zero)))
                    kaug_sc[h, rows, :] = jnp.where(head_sel[h], k, bias)
                    vt_sc[h, jt] = jnp.where(head_sel_t[h], vt, 0.0).astype(BF16)
                return carry

            lax.fori_loop(0, n_t, prep, 0)

        q = q_ref[...]
        q_heads = []
        for h in range(2):
            ones = jnp.where((lane >= spare[h]) & (lane < spare[h] + 3), jnp.ones((), BF16), zero)
            q_heads.append(jnp.where(head_sel[h], q, ones))
        m_sc[...] = jnp.full((8, T), NEG, F32)
        l_sc[...] = jnp.zeros((8, T), F32)
        acc_sc[...] = jnp.zeros((128, T), F32)

        def update(j, k_lo, n_k, q_lo, masked):
            rows = pl.ds(pl.multiple_of(j * T + k_lo, n_k), n_k)
            n_q = T - q_lo
            alphas, pvs = [], []
            for h in range(2):
                s_t = _dot_nt(kaug_sc[h, rows, :], q_heads[h][q_lo:, :])
                if masked:
                    rr = lax.broadcasted_iota(jnp.int32, (n_k, n_q), 0) + k_lo
                    cc = lax.broadcasted_iota(jnp.int32, (n_k, n_q), 1) + q_lo
                    s_t = jnp.where(rr <= cc, s_t, NEG)
                m_prev = m_sc[h:h + 1, q_lo:]
                m_new = jnp.maximum(m_prev, jnp.max(s_t, axis=0, keepdims=True))
                alpha = jnp.exp(m_prev - m_new)
                p_t = jnp.exp(s_t - m_new)
                l_sc[h:h + 1, q_lo:] = alpha * l_sc[h:h + 1, q_lo:] + jnp.sum(p_t, axis=0, keepdims=True)
                m_sc[h:h + 1, q_lo:] = m_new
                alphas.append(alpha)
                pvs.append(_dot(vt_sc[h, j, :, k_lo:k_lo + n_k], p_t.astype(BF16)))
            acc_sc[:, q_lo:] = (acc_sc[:, q_lo:] * jnp.where(head_sel_t[0], alphas[0], alphas[1])
                                + (pvs[0] + pvs[1]))

        def off_diagonal(j, carry):
            update(j, 0, T, 0, False)
            return carry

        lax.fori_loop(0, i, off_diagonal, 0)
        half = T // 2
        update(i, 0, half, 0, True)
        update(i, half, half, half, True)
        l = l_sc[...]
        o_ref[...] = (acc_sc[...] / jnp.where(head_sel_t[0], l[0:1, :], l[1:2, :])).T
        is_head = lax.broadcasted_iota(jnp.int32, (8, 1), 0) < 2
        lse_ref[...] = jnp.where(is_head, m_sc[...] + jnp.log(jnp.where(is_head, l, 1.0)), 0.0)

    return pl.pallas_call(
        body, name="attention_fwd", grid=(N_PAIR, n_t),
        out_shape=(jax.ShapeDtypeStruct((S, D_ATT), F32), jax.ShapeDtypeStruct((N_PAIR, n_t, 8, T), F32)),
        in_specs=[pl.BlockSpec((T, 128), lambda hp, i: (i, hp)),
                  pl.BlockSpec((S, 128), lambda hp, i: (0, N_PAIR + hp)),
                  pl.BlockSpec((S, 128), lambda hp, i: (0, 2 * N_PAIR + hp)),
                  pl.BlockSpec((S, 128), lambda hp, i: (0, 0))],
        out_specs=(pl.BlockSpec((T, 128), lambda hp, i: (i, hp)),
                   pl.BlockSpec((None, None, 8, T), lambda hp, i: (hp, i, 0, 0))),
        scratch_shapes=[pltpu.VMEM((2, S, 128), BF16), pltpu.VMEM((2, n_t, 128, T), BF16),
                        pltpu.VMEM((8, T), F32), pltpu.VMEM((8, T), F32), pltpu.VMEM((128, T), F32)],
        compiler_params=_params(dimension_semantics=("arbitrary", "arbitrary")),
    )(qkv, qkv, qkv, big_f)


def _attention_bwd(qkv, datt, att, lse, big_f):
    S = qkv.shape[0]
    T = min(T_ATT, S)
    n_t = S // T

    def body(q_ref, do_ref, o_ref, lse_ref, k_ref, v_ref, fk_ref,
             dq_ref, dk_ref, dv_ref, cs_ref, dfk_ref, dfq_ref, stat_sc, dqt_sc, qaug_sc):
        hp = pl.program_id(0)
        j = pl.program_id(1)
        lane = lax.broadcasted_iota(jnp.int32, (1, 128), 1)
        sub = lax.broadcasted_iota(jnp.int32, (128, 1), 0)
        head_sel = (lane < HEAD_DIM, lane >= HEAD_DIM)
        head_sel_t = (sub < HEAD_DIM, sub >= HEAD_DIM)
        spare = (HEAD_DIM, 0)
        zero = jnp.zeros((), BF16)
        one = jnp.ones((), BF16)

        def bias_lanes(first, pieces):
            hi, mid, lo = pieces
            return lambda rest: jnp.where(lane == first, hi, jnp.where(lane == first + 1, mid,
                                                                        jnp.where(lane == first + 2, lo, rest)))

        @pl.when(j == 0)
        def _():
            dqt_sc[...] = jnp.zeros_like(dqt_sc)
            cs_ref[...] = jnp.zeros_like(cs_ref)
            dfq_ref[...] = jnp.zeros_like(dfq_ref)

            def prep(i, carry):
                rows = pl.ds(pl.multiple_of(i * T, T), T)
                q = q_ref[rows, :]
                do = do_ref[rows, :]
                prod = o_ref[rows, :] * do.astype(F32)
                d_a = jnp.sum(jnp.where(head_sel[0], prod, 0.0), axis=1, keepdims=True)
                d_b = jnp.sum(jnp.where(head_sel[0], 0.0, prod), axis=1, keepdims=True)
                delta_t = jnp.where(head_sel[0], d_a, d_b).T
                stat_sc[i, 0:1, :] = delta_t[0:1, :]
                stat_sc[i, 1:2, :] = delta_t[HEAD_DIM:HEAD_DIM + 1, :]
                lse = lse_ref[i]
                lse_cols = jnp.where(head_sel_t[0], lse[0:1, :], lse[1:2, :]).T
                for h in range(2):
                    neg_lse = -lse_cols[:, h * HEAD_DIM:h * HEAD_DIM + 1]
                    ones = jnp.where((lane >= spare[h]) & (lane < spare[h] + 3), one, zero)
                    qaug_sc[h, rows, :] = jnp.where(head_sel[h], q, bias_lanes(spare[h] + 3, _split3(neg_lse))(ones))
                return carry

            lax.fori_loop(0, n_t, prep, 0)

        k = k_ref[...]
        v = v_ref[...]
        fk = fk_ref[...]
        kt = k.astype(F32).T
        heads = []
        for h in range(2):
            fkh = jnp.sum(jnp.where(lane == 2 * hp + h, fk, 0.0), axis=1, keepdims=True)
            ones = jnp.where((lane >= spare[h] + 3) & (lane < spare[h] + 6), one, zero)
            kaug = jnp.where(head_sel[h], k, bias_lanes(spare[h], _split3(-fkh))(ones))
            heads.append((kaug, jnp.where(head_sel[h], v, zero), jnp.where(head_sel_t[h], kt, 0.0).astype(BF16)))

        def block(i, k_lo, n_k, q_lo, masked):
            n_q = T - q_lo
            rows = pl.ds(pl.multiple_of(i * T + q_lo, n_q), n_q)
            q = q_ref[rows, :]
            do = do_ref[rows, :]
            stat = stat_sc[i]
            dk = jnp.zeros((n_k, 128), F32)
            dv = jnp.zeros((n_k, 128), F32)
            dqt = jnp.zeros((128, n_q), F32)
            dfs = []
            for h in range(2):
                kaug, vh, kth = heads[h]
                arg = _dot_nt(kaug[k_lo:k_lo + n_k, :], qaug_sc[h, rows, :])
                if masked:
                    rr = lax.broadcasted_iota(jnp.int32, (n_k, n_q), 0) + k_lo
                    cc = lax.broadcasted_iota(jnp.int32, (n_k, n_q), 1) + q_lo
                    arg = jnp.where(rr <= cc, arg, NEG)
                p_t = jnp.exp(arg)
                ds_t = p_t * (_dot_nt(vh[k_lo:k_lo + n_k, :], do) - stat[h:h + 1, q_lo:])
                ds_bf = ds_t.astype(BF16)
                dv = dv + _dot(p_t.astype(BF16), jnp.where(head_sel[h], do, zero))
                dk = dk + _dot(ds_bf, jnp.where(head_sel[h], q, zero))
                dqt = dqt + _dot(kth[:, k_lo:k_lo + n_k], ds_bf)
                dfs.append(jnp.sum(ds_t, axis=1, keepdims=True))
                dfq_ref[i, h:h + 1, q_lo:] += _colsum(ds_t)
            dqt_sc[i, :, q_lo:] += dqt
            return dk, dv, dfs[0], dfs[1]

        def off_diagonal(i, acc):
            return tuple(a + b for a, b in zip(acc, block(i, 0, T, 0, False)))

        half = T // 2
        early = block(j, 0, half, 0, True)
        late = block(j, half, half, half, True)
        acc1 = tuple(jnp.concatenate([a, b], axis=0) for a, b in zip(early, late))
        dk_acc, dv_acc, dfa, dfb = lax.fori_loop(j + 1, n_t, off_diagonal, acc1)
        dk_ref[...] = dk_acc.astype(BF16)
        dv_ref[...] = dv_acc.astype(BF16)
        dfk_ref[...] = -jnp.where(lane == 0, dfa, jnp.where(lane == 1, dfb, 0.0))
        cs_ref[:, 128:256] = cs_ref[:, 128:256] + _colsum(dk_acc)
        cs_ref[:, 256:384] = cs_ref[:, 256:384] + _colsum(dv_acc)

        @pl.when(j == n_t - 1)
        def _():
            def finish(i, tot):
                dq = dqt_sc[i].T
                dq_ref[pl.ds(pl.multiple_of(i * T, T), T), :] = dq.astype(BF16)
                return tot + _colsum(dq)

            cs_ref[:, 0:128] = lax.fori_loop(0, n_t, finish, jnp.zeros((1, 128), F32))

    pair_rows = lambda hp, j: (hp, 0, 0)
    return pl.pallas_call(
        body, name="attention_bwd", grid=(N_PAIR, n_t),
        out_shape=(jax.ShapeDtypeStruct((S, D_ATT), BF16), jax.ShapeDtypeStruct((S, D_ATT), BF16),
                   jax.ShapeDtypeStruct((S, D_ATT), BF16), jax.ShapeDtypeStruct((N_PAIR, 1, 384), F32),
                   jax.ShapeDtypeStruct((N_PAIR, S, 128), F32),
                   jax.ShapeDtypeStruct((N_PAIR, n_t, 8, T), F32)),
        in_specs=[pl.BlockSpec((S, 128), lambda hp, j: (0, hp)),
                  pl.BlockSpec((S, 128), lambda hp, j: (0, hp)),
                  pl.BlockSpec((S, 128), lambda hp, j: (0, hp)),
                  pl.BlockSpec((None, n_t, 8, T), lambda hp, j: (hp, 0, 0, 0)),
                  pl.BlockSpec((T, 128), lambda hp, j: (j, N_PAIR + hp)),
                  pl.BlockSpec((T, 128), lambda hp, j: (j, 2 * N_PAIR + hp)),
                  pl.BlockSpec((T, 128), lambda hp, j: (j, 0))],
        out_specs=(pl.BlockSpec((S, 128), lambda hp, j: (0, hp)),
                   pl.BlockSpec((T, 128), lambda hp, j: (j, hp)),
                   pl.BlockSpec((T, 128), lambda hp, j: (j, hp)),
                   pl.BlockSpec((None, 1, 384), pair_rows),
                   pl.BlockSpec((None, T, 128), lambda hp, j: (hp, j, 0)),
                   pl.BlockSpec((None, n_t, 8, T), lambda hp, j: (hp, 0, 0, 0))),
        scratch_shapes=[pltpu.VMEM((n_t, 8, T), F32), pltpu.VMEM((n_t, 128, T), F32),
                        pltpu.VMEM((2, S, 128), BF16)],
        compiler_params=_params(dimension_semantics=("arbitrary", "arbitrary")),
    )(qkv, datt, att, lse, qkv, qkv, big_f)


def _window_counts(first_row, n_rows, window):
    t = lax.broadcasted_iota(jnp.int32, (n_rows, 1), 0) + first_row
    return jnp.minimum((t + 1).astype(F32), float(window))


def _middle(x, tgt, att, g, p, gate, w_mix, b_mix, pool_scale, w_out, b_out, ln_g, ln_b):
    S = x.shape[0]
    tm = min(TM_MID, S)
    halo_blocks = tm // POOL_HALO

    def body(x_ref, t_ref, att_ref, g_ref, p_ref, ph_ref, gate_ref, wm_ref, bm_ref, ps_ref, wo_ref, bo_ref,
             lg_ref, lb_ref,
             dh_ref, datt_ref, dg_ref, dpl_ref, gwo_ref, gwm_ref, vec_ref, loss_ref):
        i = pl.program_id(0)

        @pl.when(i == 0)
        def _():
            gwo_ref[...] = jnp.zeros_like(gwo_ref)
            gwm_ref[...] = jnp.zeros_like(gwm_ref)
            vec_ref[...] = jnp.zeros_like(vec_ref)
            loss_ref[...] = jnp.zeros_like(loss_ref)

        pc = p_ref[...]
        halo = jnp.where(i > 0, ph_ref[...], 0.0)
        pe = jnp.concatenate([halo, pc], axis=0)
        pooled_parts = []
        for gi, w in enumerate(POOL_WINDOWS):
            cur = pe[:, gi * POOL_GROUP:(gi + 1) * POOL_GROUP]
            span = 1
            while span < w:
                cur = cur + pltpu.roll(cur, span, 0)
                span *= 2
            wsum = cur[POOL_HALO:, :]
            mean = wsum / _window_counts(i * tm, tm, w)
            pooled_parts.append(mean - pc[:, gi * POOL_GROUP:(gi + 1) * POOL_GROUP])
        pooled_bf =[v.astype(BF16) for v in pooled_parts]
        mixed = jnp.concatenate([_dot(pooled_bf[gi], wm_ref[gi]) for gi in range(4)], axis=1) + bm_ref[...]
        ps = ps_ref[...]
        pool_out = mixed * ps
        gv = g_ref[...]
        sig = _sigmoid(gv)
        silu = gv * sig
        att = att_ref[...]
        y = jnp.concatenate([att * silu[:, :D_ATT], pool_out * silu[:, D_ATT:]], axis=1)
        y_bf = y.astype(BF16)
        wo = wo_ref[...]
        yo = _dot(y_bf, wo) + bo_ref[...]
        gate = gate_ref[...]
        h = ALPHA * x_ref[...] + gate * yo
        mu = jnp.mean(h, axis=1, keepdims=True)
        hc = h - mu
        var = jnp.mean(hc * hc, axis=1, keepdims=True)
        rstd = lax.rsqrt(var + LN_EPS)
        yhat = hc * rstd
        lg = lg_ref[...]
        out = yhat * lg + lb_ref[...]
        err = out - t_ref[...]
        loss_ref[...] += 0.5 * jnp.sum(jnp.mean(err * err, axis=1, keepdims=True), axis=0, keepdims=True)

        dout = err * (1.0 / D)
        g_ln_b = _colsum(dout)
        g_ln_g = _colsum(dout * yhat)
        dyh = dout * lg
        dh = rstd * (dyh - jnp.mean(dyh, axis=1, keepdims=True)
                     - yhat * jnp.mean(dyh * yhat, axis=1, keepdims=True))
        dh_ref[...] = dh
        d_gate = _colsum(dh * yo)
        dyo = gate * dh
        g_b_out = _colsum(dyo)
        dyo_bf = dyo.astype(BF16)
        gwo_ref[...] += _dot_tn(y_bf, dyo_bf)
        dy = _dot_nt(dyo_bf, wo)
        dsilu = sig * (1.0 + gv * (1.0 - sig))
        dy_a = dy[:, :D_ATT]
        dy_p = dy[:, D_ATT:]
        datt_ref[...] = (dy_a * silu[:, :D_ATT]).astype(BF16)
        dpo = dy_p * silu[:, D_ATT:]
        dg = jnp.concatenate([dy_a * att * dsilu[:, :D_ATT], dy_p * pool_out * dsilu[:, D_ATT:]], axis=1)
        dg_ref[...] = dg.astype(BF16)
        g_dg = _colsum(dg)
        g_ps = _colsum(dpo * mixed)
        dmixed = dpo * ps
        g_bm = _colsum(dmixed)
        dmixed_bf = dmixed.astype(BF16)
        dpl = []
        for gi in range(4):
            dm = dmixed_bf[:, gi * POOL_GROUP:(gi + 1) * POOL_GROUP]
            gwm_ref[gi] += _dot_tn(pooled_bf[gi], dm)
            dpl.append(_dot_nt(dm, wm_ref[gi]))
        dpl_ref[...] = jnp.concatenate(dpl, axis=1)
        vec_ref[0:1, :] += g_ln_g
        vec_ref[1:2, :] += g_ln_b
        vec_ref[2:3, :] += d_gate
        vec_ref[3:4, :] += g_b_out
        vec_ref[4:5, :] += g_dg
        vec_ref[5:6, 0:D_POOL] += g_ps
        vec_ref[6:7, 0:D_POOL] += g_bm

    row = lambda w: pl.BlockSpec((tm, w), lambda i: (i, 0))
    full2 = lambda a: pl.BlockSpec(a.shape, lambda i: (0, 0))
    full3 = lambda a: pl.BlockSpec(a.shape, lambda i: (0, 0, 0))
    return pl.pallas_call(
        body, name="middle", grid=(S // tm,),
        out_shape=(jax.ShapeDtypeStruct((S, D), F32),
                   jax.ShapeDtypeStruct((S, D_ATT), BF16),
                   jax.ShapeDtypeStruct((S, D), BF16),
                   jax.ShapeDtypeStruct((S, D_POOL), F32),
                   jax.ShapeDtypeStruct((D, D), F32),
                   jax.ShapeDtypeStruct((4, POOL_GROUP, POOL_GROUP), F32),
                   jax.ShapeDtypeStruct((8, D), F32),
                   jax.ShapeDtypeStruct((1, 1), F32)),
        in_specs=[row(D), row(D), row(D_ATT), row(D), row(D_POOL),
                  pl.BlockSpec((POOL_HALO, D_POOL), lambda i: (jnp.maximum(i * halo_blocks - 1, 0), 0)),
                  full2(gate), full3(w_mix), full2(b_mix), full2(pool_scale), full2(w_out), full2(b_out),
                  full2(ln_g), full2(ln_b)],
        out_specs=(row(D), row(D_ATT), row(D), row(D_POOL),
                   pl.BlockSpec((D, D), lambda i: (0, 0)),
                   pl.BlockSpec((4, POOL_GROUP, POOL_GROUP), lambda i: (0, 0, 0)),
                   pl.BlockSpec((8, D), lambda i: (0, 0)),
                   pl.BlockSpec((1, 1), lambda i: (0, 0))),
        compiler_params=_params(dimension_semantics=("arbitrary",)),
    )(x, tgt, att, g, p, p, gate, w_mix, b_mix, pool_scale, w_out, b_out, ln_g, ln_b)


def _tail(dpl, dfk, dfq, f):
    S = dpl.shape[0]
    tm = min(T_ATT, S)
    n_t = S // tm
    halo_blocks = tm // POOL_HALO
    last_halo = S // POOL_HALO - 1

    def body(d_ref, dn_ref, dfk_ref, dfq_ref, f_ref, dp_ref, df_ref, cs_ref, carry):
        s = pl.program_id(0)
        i = n_t - 1 - s

        @pl.when(s == 0)
        def _():
            carry[...] = jnp.zeros_like(carry)
            cs_ref[...] = jnp.zeros_like(cs_ref)

        dc = d_ref[...]
        nxt = jnp.where(s > 0, dn_ref[...], 0.0)
        de = jnp.concatenate([dc, nxt], axis=0)
        n_e = tm + POOL_HALO
        parts = []
        for gi, w in enumerate(POOL_WINDOWS):
            cur = de[:, gi * POOL_GROUP:(gi + 1) * POOL_GROUP] / _window_counts(i * tm, n_e, w)
            span = 1
            while span < w:
                cur = cur + pltpu.roll(cur, n_e - span, 0)
                span *= 2
            parts.append(cur[:tm, :] - dc[:, gi * POOL_GROUP:(gi + 1) * POOL_GROUP])
        dp = jnp.concatenate(parts, axis=1)
        dp_ref[...] = dp.astype(BF16)
        cs_ref[0:1, :] += _colsum(dp)

        r = lax.broadcasted_iota(jnp.int32, (tm, tm), 0)
        c = lax.broadcasted_iota(jnp.int32, (tm, tm), 1)
        tri = (r >= c).astype(F32)
        k_cols = dfk_ref[0]
        rows8 = dfq_ref[0]
        for hp in range(1, N_PAIR):
            k_cols = k_cols + pltpu.roll(dfk_ref[hp], 2 * hp, 1)
            rows8 = rows8 + pltpu.roll(dfq_ref[hp], 2 * hp, 0)
        rows8 = rows8 + k_cols.T[0:8, :]
        dlogf8 = jnp.dot(rows8, tri, preferred_element_type=F32, precision=lax.Precision.HIGHEST) + carry[...]
        first = lax.broadcasted_iota(jnp.int32, (1, tm), 1) == 0
        carry[...] = jnp.sum(jnp.where(first, dlogf8, 0.0), axis=1, keepdims=True)
        dlogf = jnp.concatenate([dlogf8, jnp.zeros((128 - 8, tm), F32)], axis=0).T
        df = dlogf * _sigmoid(-f_ref[...])
        df_ref[...] = df.astype(BF16)
        cs_ref[1:2, 0:128] += _colsum(df)

    rev = lambda w: pl.BlockSpec((tm, w), lambda s: (n_t - 1 - s, 0))
    return pl.pallas_call(
        body, name="tail", grid=(n_t,),
        out_shape=(jax.ShapeDtypeStruct((S, D_POOL), BF16), jax.ShapeDtypeStruct((S, 128), BF16),
                   jax.ShapeDtypeStruct((8, D_POOL), F32)),
        in_specs=[rev(D_POOL),
                  pl.BlockSpec((POOL_HALO, D_POOL),
                               lambda s: (jnp.minimum((n_t - s) * halo_blocks, last_halo), 0)),
                  pl.BlockSpec((N_PAIR, tm, 128), lambda s: (0, n_t - 1 - s, 0)),
                  pl.BlockSpec((N_PAIR, None, 8, tm), lambda s: (0, n_t - 1 - s, 0, 0)),
                  rev(128)],
        out_specs=(rev(D_POOL), rev(128), pl.BlockSpec((8, D_POOL), lambda s: (0, 0))),
        scratch_shapes=[pltpu.VMEM((8, 1), F32)],
        compiler_params=_params(dimension_semantics=("arbitrary",)),
    )(dpl, dpl, dfk, dfq, f)


PIECES = ((O_QKV, D_ATT), (O_QKV + D_ATT, D_ATT), (O_QKV + 2 * D_ATT, D_ATT), (O_F, 128), (O_P, D_POOL), (O_G, D))


def _grad_w_in(u, pieces):
    S = u.shape[0]
    tm = min(TM_GW, S)
    n_t = S // tm

    def body(u_ref, *rest):
        piece_refs, out_ref, acc, sem = rest[:6], rest[6], rest[7], rest[8]
        i = pl.program_id(0)

        @pl.when(i == 0)
        def _():
            acc[...] = jnp.zeros_like(acc)

        u_t = u_ref[...]
        for (off, w), ref in zip(PIECES, piece_refs):
            acc[:, off:off + w] += _dot_tn(u_t, ref[...])

        @pl.when(i == n_t - 1)
        def _():
            cp = pltpu.make_async_copy(acc, out_ref, sem)
            cp.start()
            cp.wait()

    return pl.pallas_call(
        body, name="grad_w_in", grid=(n_t,),
        out_shape=jax.ShapeDtypeStruct((D, D_PAD), F32),
        in_specs=[pl.BlockSpec((tm, D), lambda i: (i, 0))]
        + [pl.BlockSpec((tm, w), lambda i: (i, 0)) for _, w in PIECES],
        out_specs=pl.BlockSpec(memory_space=pl.ANY),
        scratch_shapes=[pltpu.VMEM((D, D_PAD), F32), pltpu.SemaphoreType.DMA],
        compiler_params=_params(dimension_semantics=("arbitrary",)),
    )(u, *pieces)


def _grad_x(pieces, wt_pad, dh, x, scale):
    S = x.shape[0]
    tm = min(TM_DU, S)

    def body(*refs):
        piece_refs = refs[:6]
        w_ref, dh_ref, x_ref, sc_ref, gx_ref, vec_ref = refs[6:]

        @pl.when(pl.program_id(0) == 0)
        def _():
            vec_ref[...] = jnp.zeros_like(vec_ref)

        du = jnp.zeros((tm, D), F32)
        for (off, w), ref in zip(PIECES, piece_refs):
            du = du + _dot(ref[...], w_ref[off:off + w, :])
        xv = x_ref[...]
        gx_ref[...] = ALPHA * dh_ref[...] + du * (1.0 + sc_ref[...])
        vec_ref[0:1, :] += _colsum(du)
        vec_ref[1:2, :] += _colsum(du * xv)

    row = lambda w: pl.BlockSpec((tm, w), lambda i: (i, 0))
    return pl.pallas_call(
        body, name="grad_x", grid=(S // tm,),
        out_shape=(jax.ShapeDtypeStruct((S, D), F32), jax.ShapeDtypeStruct((8, D), F32)),
        in_specs=[row(w) for _, w in PIECES]
        + [pl.BlockSpec(wt_pad.shape, lambda i: (0, 0)), row(D), row(D), pl.BlockSpec((1, D), lambda i: (0, 0))],
        out_specs=(row(D), pl.BlockSpec((8, D), lambda i: (0, 0))),
        compiler_params=_params(dimension_semantics=("arbitrary",)),
    )(*pieces, wt_pad, dh, x, scale)


def _grad_ada(c_all, dada_all, dada_cols):
    def body(c_ref, dall_ref, dcol_ref, gw_ref, gb_ref):
        rows = lax.broadcasted_iota(jnp.int32, (8, 1), 0)
        cm = jnp.zeros((8, D), F32)
        dm = jnp.zeros((8, 3 * D), F32)
        for r in range(8):
            cm = jnp.where(rows == r, c_ref[r], cm)
            dm = jnp.where(rows == r, dall_ref[r], dm)
        act = cm * _sigmoid(cm)
        pad = jnp.zeros((8, D), F32)
        lhs = jnp.concatenate([act, pad], axis=0).astype(BF16)
        rhs = jnp.concatenate([dcol_ref[...], jnp.zeros((8, SHARD_ADA), F32)], axis=0).astype(BF16)
        gw_ref[...] = _dot_tn(lhs, rhs)
        gb_ref[...] = _colsum(dm)

    vm = pl.BlockSpec(memory_space=pltpu.VMEM)
    return pl.pallas_call(
        body, name="grad_ada",
        out_shape=(jax.ShapeDtypeStruct((D, SHARD_ADA), F32), jax.ShapeDtypeStruct((1, 3 * D), F32)),
        in_specs=[vm, vm, vm], out_specs=(vm, vm),
        compiler_params=_params(),
    )(c_all, dada_all, dada_cols)


def _adamw_math(w, g, m, v):
    m = ADAM_B1 * m + (1.0 - ADAM_B1) * g
    v = ADAM_B2 * v + (1.0 - ADAM_B2) * (g * g)
    m_hat = m / (1.0 - ADAM_B1 ** ADAM_STEP)
    v_hat = v / (1.0 - ADAM_B2 ** ADAM_STEP)
    delta = -ADAM_LR * (m_hat / (jnp.sqrt(v_hat) + ADAM_EPS) + ADAM_WD * w)
    return delta, m, v


def _adamw(groups, n_steps):
    n = len(groups)

    def body(*refs):
        ins, outs = refs[:4 * n], refs[4 * n:]
        for t in range(n):
            w, g, m, v = (r[...] for r in ins[4 * t:4 * t + 4])
            d, m2, v2 = _adamw_math(w, g, m, v)
            outs[3 * t][...] = d
            outs[3 * t + 1][...] = m2
            outs[3 * t + 2][...] = v2

    in_specs, out_specs, out_shape, args = [], [], [], []
    for (w, g, m, v) in groups:
        rest = w.shape[1:]
        spec = pl.BlockSpec((w.shape[0] // n_steps,) + rest, lambda i, nd=len(rest): (i,) + (0,) * nd)
        in_specs += [spec] * 4
        out_specs += [spec] * 3
        out_shape += [jax.ShapeDtypeStruct(w.shape, F32)] * 3
        args += [w, g, m, v]
    return pl.pallas_call(
        body, name="adamw_%d_%d" % (n, n_steps), grid=(n_steps,),
        out_shape=tuple(out_shape), in_specs=in_specs, out_specs=tuple(out_specs),
        compiler_params=_params(dimension_semantics=("arbitrary",)),
    )(*args)


def _pack_small(parts):
    rows = []
    used = 0
    for name, (first, n_rows) in SMALL_SEGS.items():
        if first > used:
            rows.append(jnp.zeros((first - used, 128), F32))
        flat = parts[name].reshape(-1)
        flat = jnp.pad(flat, (0, n_rows * 128 - flat.shape[0]))
        rows.append(flat.reshape(n_rows, 128))
        used = first + n_rows
    rows.append(jnp.zeros((SMALL_ROWS - used, 128), F32))
    return jnp.concatenate(rows, axis=0)


def _unpack_small(buf, name, shape):
    first, n_rows = SMALL_SEGS[name]
    n = int(np.prod(shape))
    return buf[first:first + n_rows].reshape(-1)[:n].reshape(shape)


def _pad_in(v):
    r = v.shape[0]
    z = jnp.zeros((r, O_P - O_F - N_HEADS), v.dtype)
    return jnp.concatenate([v[:, :3 * D_ATT + N_HEADS], z, v[:, 3 * D_ATT + N_HEADS:]], axis=1)


def _unpad_in(v):
    return jnp.concatenate([v[:, :O_F + N_HEADS], v[:, O_P:]], axis=1)


def _shards_in(v):
    gap = O_P - (O_F + N_HEADS)
    parts = []
    for a in range(N_CHIPS):
        lo, hi = a * SHARD_IN, (a + 1) * SHARD_IN
        cut = O_F + N_HEADS
        if hi <= cut:
            parts.append(v[:, lo:hi])
        elif lo >= cut:
            parts.append(v[:, lo + gap:hi + gap])
        else:
            parts.append(jnp.concatenate([v[:, lo:cut], v[:, cut + gap:hi + gap]], axis=1))
    return jnp.stack(parts, axis=0)


def kernel(x, c, w_ada, b_ada, w_in, b_in, w_pool_mix, b_pool_mix, pool_scale, w_out, b_out, ln_g, ln_b, loss_target, m_w_ada, m_b_ada, m_w_in, m_b_in, m_w_pool_mix, m_b_pool_mix, m_pool_scale, m_w_out, m_b_out, m_ln_g, m_ln_b, v_w_ada, v_b_ada, v_w_in, v_b_in, v_w_pool_mix, v_b_pool_mix, v_pool_scale, v_w_out, v_b_out, v_ln_g, v_ln_b):
    S = x.shape[1]
    T = min(T_ATT, S)
    n_t = S // T
    chip = 2 * lax.axis_index("x") + lax.axis_index("y")
    x2 = x[0]
    tgt = loss_target[0]
    q_scale = jnp.concatenate([jnp.full((1, D_ATT), Q_SCALE, F32), jnp.ones((1, D_PAD - D_ATT), F32)], axis=1)

    to_cols = lambda a: jnp.transpose(a, (2, 0, 1))
    from_cols = lambda a: jnp.transpose(a, (1, 2, 0))
    c_all, ada4, wt_in_all, w_out_all = _gather_and_ada(
        c, w_ada[0], b_ada.reshape(4, 1, SHARD_ADA), to_cols(w_in).reshape(SHARD_IN, D).astype(BF16),
        w_out[0].astype(BF16))
    ada = ada4[:, 0, :].reshape(1, 3 * D)
    shift, scale, gate = ada[:, :D], ada[:, D:2 * D], ada[:, 2 * D:]
    wt_full = wt_in_all.reshape(D_IN, D)
    n_real = 3 * D_ATT + N_HEADS
    wt_pad = jnp.concatenate([wt_full[:D_ATT] * jnp.asarray(Q_SCALE, BF16), wt_full[D_ATT:n_real],
                              jnp.zeros((O_P - n_real, D), BF16), wt_full[n_real:]], axis=0)
    b_pad = _pad_in(b_in) * q_scale
    w_out_full = w_out_all.reshape(D, D)
    w_mix_bf = w_pool_mix[0].astype(BF16)

    u, qkv, f, p, g = _in_proj(x2, shift, scale, wt_pad, b_pad)
    big_f = _forget_cumsum(f)
    att, lse = _attention_fwd(qkv, big_f)

    dh, datt, dg, dpl, gw_out, gw_mix, vec, loss_part = _middle(
        x2, tgt, att, g, p, gate, w_mix_bf, b_pool_mix.reshape(1, D_POOL), pool_scale, w_out_full, b_out, ln_g, ln_b)
    dq, dk, dv, cs_att, dfk, dfq = _attention_bwd(qkv, datt, att, lse, big_f)
    dp, df, cs_tail = _tail(dpl, dfk, dfq, f)
    pieces = (dq, dk, dv, df, dp, dg)
    gw_pad = _grad_w_in(u, pieces)
    grad_x, vec_x = _grad_x(pieces, wt_pad, dh, x2, scale)

    cs_qkv = jnp.transpose(cs_att.reshape(N_PAIR, 3, 128), (1, 0, 2)).reshape(1, 3 * D_ATT)
    gb_pad = jnp.concatenate([cs_qkv, cs_tail[1:2, 0:128], cs_tail[0:1, :], vec[4:5, :]], axis=1) * q_scale
    dada = jnp.concatenate([vec_x[0:1, :], vec_x[1:2, :], vec[2:3, :]], axis=1)
    small = _pack_small({
        "b_in": gb_pad, "w_pool_mix": gw_mix, "b_pool_mix": vec[6:7, :D_POOL], "pool_scale": vec[5:6, :D_POOL],
        "b_out": vec[3:4, :], "ln_g": vec[0:1, :], "ln_b": vec[1:2, :], "loss": loss_part,
        "b_ada": jnp.zeros((1, 3 * D), F32)})

    g_w_in, g_w_out, small_sum, dada_all = _reduce_all(
        _shards_in(gw_pad), _shards_in(q_scale), gw_out.reshape(N_CHIPS, SHARD_OUT, D),
        jnp.ones((N_CHIPS, 1, D), F32), small[:SMALL_REDUCED_ROWS], dada)
    dada_cols = lax.dynamic_slice(dada_all[:, 0, :], (0, chip * SHARD_ADA), (8, SHARD_ADA))
    g_w_ada, g_b_ada = _grad_ada(c_all, dada_all, dada_cols)
    loss = _unpack_small(small_sum, "loss", (1,))[0]

    grads_small = jnp.concatenate([small_sum, g_b_ada.reshape(24, 128)], axis=0)
    no_param = jnp.zeros((1,), F32)
    small_w = {"b_in": _pad_in(b_in), "w_pool_mix": w_pool_mix, "b_pool_mix": b_pool_mix, "pool_scale": pool_scale,
               "b_out": b_out, "ln_g": ln_g, "ln_b": ln_b, "loss": no_param, "b_ada": b_ada}
    small_m = {"b_in": _pad_in(m_b_in), "w_pool_mix": m_w_pool_mix, "b_pool_mix": m_b_pool_mix,
               "pool_scale": m_pool_scale, "b_out": m_b_out, "ln_g": m_ln_g, "ln_b": m_ln_b, "loss": no_param,
               "b_ada": m_b_ada}
    small_v = {"b_in": _pad_in(v_b_in), "w_pool_mix": v_w_pool_mix, "b_pool_mix": v_b_pool_mix,
               "pool_scale": v_pool_scale, "b_out": v_b_out, "ln_g": v_ln_g, "ln_b": v_ln_b, "loss": no_param,
               "b_ada": v_b_ada}
    big = _adamw([(w_ada[0], g_w_ada, m_w_ada[0], v_w_ada[0]),
                  (w_out[0], g_w_out, m_w_out[0], v_w_out[0])], 8)
    g_w_in_cols = to_cols(g_w_in[None])
    big_in = _adamw([(to_cols(w_in), g_w_in_cols, to_cols(m_w_in), to_cols(v_w_in))], 14)
    sm = _adamw([(_pack_small(small_w), grads_small, _pack_small(small_m), _pack_small(small_v))], 1)

    names = ["w_ada", "b_ada", "w_in", "b_in", "w_pool_mix", "b_pool_mix", "pool_scale", "w_out", "b_out",
             "ln_g", "ln_b"]
    shapes = {"b_ada": (1, 3 * D), "b_in": (1, D_PAD), "w_pool_mix": (1, 4, POOL_GROUP, POOL_GROUP),
              "b_pool_mix": (1, 4, POOL_GROUP), "pool_scale": (1, D_POOL), "b_out": (1, D), "ln_g": (1, D),
              "ln_b": (1, D)}
    big_idx = {"w_ada": 0, "w_out": 1}

    def leaf(kind, name):
        if name == "w_in":
            return from_cols(g_w_in_cols if kind == 0 else big_in[kind - 1])
        if name in big_idx:
            if kind == 0:
                return (g_w_ada, g_w_out)[big_idx[name]][None]
            return big[3 * big_idx[name] + kind - 1][None]
        buf = grads_small if kind == 0 else sm[kind - 1]
        val = _unpack_small(buf, name, shapes[name])
        if name == "b_in":
            val = _unpad_in(val)
        return val

    outs = [loss, grad_x[None]]
    for kind in range(4):
        outs += [leaf(kind, n) for n in names]
    return tuple(outs)
```

```python
import functools

import numpy as np
import jax
import jax.numpy as jnp
from jax import lax
from jax.experimental import pallas as pl
from jax.experimental.pallas import tpu as pltpu

F32 = jnp.float32
BF16 = jnp.bfloat16
MESH = pl.DeviceIdType.MESH

D = 1024
D_ATT = 512
D_POOL = 512
N_HEADS = 8
HEAD_DIM = 64
N_PAIR = N_HEADS // 2
POOL_WINDOWS = (2, 4, 8, 16)
POOL_GROUP = 128
POOL_HALO = 16
LN_EPS = 1e-5
ALPHA = 2.0 ** 0.25
D_IN = 3 * D_ATT + N_HEADS + D_POOL + D_ATT + D_POOL
N_CHIPS = 4
SHARD_IN = D_IN // N_CHIPS
SHARD_ADA = 3 * D // N_CHIPS
SHARD_OUT = D // N_CHIPS

O_QKV, O_F, O_P, O_G, D_PAD = 0, 1536, 1664, 2176, 3200
Q_SCALE = HEAD_DIM ** -0.5

ADAM_LR, ADAM_B1, ADAM_B2, ADAM_EPS, ADAM_WD, ADAM_STEP = 0.001, 0.9, 0.999, 1e-08, 0.01, 10

NEG = -1e30

VMEM_LIMIT = 56 * 1024 * 1024

TM_PROJ = 512
T_ATT = 512
TM_MID = 256
TM_GW = 1024
TM_DU = 512

REL7 = [(0, 0, 1), (0, 1, 0), (0, 1, 1), (1, 0, 0), (1, 0, 1), (1, 1, 0), (1, 1, 1)]
REL3 = [(0, 1), (1, 0), (1, 1)]

SMALL_SEGS = {}
_row = 0
for _name, _n in (("b_in", 3200), ("w_pool_mix", 65536), ("b_pool_mix", 512), ("pool_scale", 512),
                  ("b_out", 1024), ("ln_g", 1024), ("ln_b", 1024), ("loss", 1)):
    _rows = -(-_n // 1024) * 8
    SMALL_SEGS[_name] = (_row, _rows)
    _row += _rows
SMALL_REDUCED_ROWS = -(-_row // 16) * 16
SMALL_SEGS["b_ada"] = (SMALL_REDUCED_ROWS, 24)
SMALL_ROWS = SMALL_REDUCED_ROWS + 24


def _params(**kw):
    return pltpu.CompilerParams(vmem_limit_bytes=VMEM_LIMIT, **kw)


def _flip(v, d):
    return v if d == 0 else 1 - v


def _dot(a, b):
    return jnp.dot(a, b, preferred_element_type=F32)


def _dot_nt(a, b):
    return lax.dot_general(a, b, (((1,), (1,)), ((), ())), preferred_element_type=F32)


def _dot_tn(a, b):
    return lax.dot_general(a, b, (((0,), (0,)), ((), ())), preferred_element_type=F32)


def _sigmoid(v):
    return 1.0 / (1.0 + jnp.exp(-v))


def _colsum(v):
    return jnp.sum(v, axis=0, keepdims=True)


def _gather_and_ada(c, w_ada, b_ada4, w_in_sh, w_out_sh):
    def body(c_ref, w_ref, b_ref, win_ref, wout_ref, call_ref, ada_ref, win_all, wout_all,
             cslab, sbuf, rbuf, cs_sem, cr_sem, as_sem, ar_sem, own_sem, s_sem, r_sem, fs_sem, fr_sem):
        x, y, cc = lax.axis_index("x"), lax.axis_index("y"), lax.axis_index("c")
        me = 4 * x + 2 * y + cc
        chip = 2 * x + y
        sib = (x, y, 1 - cc)
        srcs = (win_ref, wout_ref)
        dsts = (win_all, wout_all)

        def half(t, which):
            if t == 0:
                return (slice(None), pl.ds(pl.multiple_of(which * (D // 2), D // 2), D // 2))
            return (pl.ds(pl.multiple_of(which * (SHARD_OUT // 2), SHARD_OUT // 2), SHARD_OUT // 2), slice(None))

        own = [pltpu.make_async_copy(srcs[t], dsts[t].at[chip], own_sem.at[t]) for t in range(2)]
        for cp in own:
            cp.start()
        first = []
        for k, (dx, dy) in enumerate(REL3):
            peer = (_flip(x, dx), _flip(y, dy), cc)
            for t in range(2):
                cp = pltpu.make_async_remote_copy(
                    src_ref=srcs[t].at[half(t, cc)], dst_ref=dsts[t].at[(chip,) + half(t, cc)],
                    send_sem=s_sem.at[2 * k + t], recv_sem=r_sem.at[2 * k + t],
                    device_id=peer, device_id_type=MESH)
                cp.start()
                first.append(cp)

        cslab[...] = jnp.broadcast_to(c_ref[...], (8, D))
        call_ref[me] = cslab[...]
        gathers = []
        for k, (dx, dy, dc) in enumerate(REL7):
            cp = pltpu.make_async_remote_copy(
                src_ref=cslab, dst_ref=call_ref.at[me], send_sem=cs_sem.at[k], recv_sem=cr_sem.at[k],
                device_id=(_flip(x, dx), _flip(y, dy), _flip(cc, dc)), device_id_type=MESH)
            cp.start()
            gathers.append(cp)
        for cp in gathers:
            cp.wait()
        slab_row = lax.broadcasted_iota(jnp.int32, (8, 1), 0)
        mat = jnp.zeros((8, D), F32)
        for r in range(8):
            mat = jnp.where(slab_row == r, call_ref[r], mat)
        act = (mat * _sigmoid(mat)).astype(BF16)
        part = _dot(act, w_ref[...].astype(BF16))
        sends = []
        for k, (dx, dy) in enumerate(REL3):
            px, py = _flip(x, dx), _flip(y, dy)
            r = 4 * px + 2 * py + cc
            piece = _colsum(jnp.where(slab_row == r, part, 0.0))
            sbuf[k] = jnp.broadcast_to(piece, (8, SHARD_ADA))
            cp = pltpu.make_async_remote_copy(
                src_ref=sbuf.at[k], dst_ref=rbuf.at[k], send_sem=as_sem.at[k], recv_sem=ar_sem.at[k],
                device_id=(px, py, cc), device_id_type=MESH)
            cp.start()
            sends.append(cp)
        own_piece = _colsum(jnp.where(slab_row == me, part, 0.0))
        ada_ref[chip] = jnp.broadcast_to(own_piece, (8, SHARD_ADA)) + b_ref[chip]
        for k, (dx, dy) in enumerate(REL3):
            sends[k].wait()
            a = 2 * _flip(x, dx) + _flip(y, dy)
            ada_ref[a] = rbuf[k] + b_ref[a]

        passed = []
        for k, (dx, dy) in enumerate(REL3):
            a = 2 * _flip(x, dx) + _flip(y, dy)
            for t in range(2):
                landed = dsts[t].at[(a,) + half(t, cc)]
                pltpu.make_async_remote_copy(
                    src_ref=landed, dst_ref=landed, send_sem=s_sem.at[2 * k + t], recv_sem=r_sem.at[2 * k + t],
                    device_id=sib, device_id_type=MESH).wait_recv()
                cp = pltpu.make_async_remote_copy(
                    src_ref=landed, dst_ref=landed, send_sem=fs_sem.at[2 * k + t], recv_sem=fr_sem.at[2 * k + t],
                    device_id=sib, device_id_type=MESH)
                cp.start()
                passed.append(cp)
        for k, (dx, dy) in enumerate(REL3):
            a = 2 * _flip(x, dx) + _flip(y, dy)
            for t in range(2):
                other = dsts[t].at[(a,) + half(t, 1 - cc)]
                pltpu.make_async_remote_copy(
                    src_ref=other, dst_ref=other, send_sem=fs_sem.at[2 * k + t], recv_sem=fr_sem.at[2 * k + t],
                    device_id=sib, device_id_type=MESH).wait_recv()
        for cp in first + passed:
            cp.wait_send()
        for cp in own:
            cp.wait()

    vm = pl.BlockSpec(memory_space=pltpu.VMEM)
    return pl.pallas_call(
        body, name="gather_and_ada",
        out_shape=(jax.ShapeDtypeStruct((8, 8, D), F32), jax.ShapeDtypeStruct((4, 8, SHARD_ADA), F32),
                   jax.ShapeDtypeStruct((N_CHIPS, SHARD_IN, D), BF16),
                   jax.ShapeDtypeStruct((N_CHIPS, SHARD_OUT, D), BF16)),
        in_specs=[vm] * 5, out_specs=(vm,) * 4,
        scratch_shapes=[pltpu.VMEM((8, D), F32), pltpu.VMEM((3, 8, SHARD_ADA), F32),
                        pltpu.VMEM((3, 8, SHARD_ADA), F32),
                        pltpu.SemaphoreType.DMA((7,)), pltpu.SemaphoreType.DMA((7,)),
                        pltpu.SemaphoreType.DMA((3,)), pltpu.SemaphoreType.DMA((3,)),
                        pltpu.SemaphoreType.DMA((2,)), pltpu.SemaphoreType.DMA((6,)),
                        pltpu.SemaphoreType.DMA((6,)), pltpu.SemaphoreType.DMA((6,)),
                        pltpu.SemaphoreType.DMA((6,))],
        compiler_params=_params(),
    )(c, w_ada, b_ada4, w_in_sh, w_out_sh)


def _scatter_stages(pos, g_ref, sc_ref, out_ref, sib_buf, send_buf, ici_buf, sem1, sem2s, sem2r, sem3):
    x, y, cc, chip, sib = pos
    RH = g_ref.shape[1] // 2
    mine = pl.ds(pl.multiple_of(cc * RH, RH), RH)
    theirs = pl.ds(pl.multiple_of((1 - cc) * RH, RH), RH)
    cp1 = pltpu.make_async_remote_copy(
        src_ref=g_ref.at[:, theirs, :], dst_ref=sib_buf, send_sem=sem1.at[0], recv_sem=sem1.at[1],
        device_id=sib, device_id_type=MESH)
    sends = []
    for k, (dx, dy) in enumerate(REL3):
        px, py = _flip(x, dx), _flip(y, dy)
        sends.append(pltpu.make_async_remote_copy(
            src_ref=send_buf.at[2 * px + py], dst_ref=ici_buf.at[chip],
            send_sem=sem2s.at[k], recv_sem=sem2r.at[k], device_id=(px, py, cc), device_id_type=MESH))
    cp3 = pltpu.make_async_remote_copy(
        src_ref=out_ref.at[mine, :], dst_ref=out_ref.at[mine, :], send_sem=sem3.at[0], recv_sem=sem3.at[1],
        device_id=sib, device_id_type=MESH)

    def finish1():
        cp1.wait()
        for a in range(N_CHIPS):
            both = g_ref[a, mine, :] + sib_buf[a]
            sib_buf[a] = both
            send_buf[a] = both.astype(BF16)

    def start2():
        for cp in sends:
            cp.start()
        ici_buf[chip] = send_buf[chip]

    def finish2():
        for cp in sends:
            cp.wait()
        own = sib_buf[chip]
        parts = [jnp.where(chip == a, own, ici_buf[a].astype(F32)) for a in range(N_CHIPS)]
        out_ref[mine, :] = ((parts[0] + parts[1]) + (parts[2] + parts[3])) * sc_ref[chip]

    return [(cp1.start, finish1), (start2, finish2), (cp3.start, cp3.wait)]


def _all_reduce_stages(pos, g_ref, out_ref, sib_buf, ici_buf, sem1, sem2s, sem2r, sem3):
    x, y, cc, chip, sib = pos
    RH = g_ref.shape[0] // 2
    mine = pl.ds(pl.multiple_of(cc * RH, 8), RH)
    theirs = pl.ds(pl.multiple_of((1 - cc) * RH, 8), RH)
    cp1 = pltpu.make_async_remote_copy(
        src_ref=g_ref.at[theirs, :], dst_ref=sib_buf, send_sem=sem1.at[0], recv_sem=sem1.at[1],
        device_id=sib, device_id_type=MESH)
    sends = []
    for k, (dx, dy) in enumerate(REL3):
        px, py = _flip(x, dx), _flip(y, dy)
        sends.append(pltpu.make_async_remote_copy(
            src_ref=sib_buf, dst_ref=ici_buf.at[chip],
            send_sem=sem2s.at[k], recv_sem=sem2r.at[k], device_id=(px, py, cc), device_id_type=MESH))
    cp3 = pltpu.make_async_remote_copy(
        src_ref=out_ref.at[mine, :], dst_ref=out_ref.at[mine, :], send_sem=sem3.at[0], recv_sem=sem3.at[1],
        device_id=sib, device_id_type=MESH)

    def finish1():
        cp1.wait()
        sib_buf[...] = g_ref[mine, :] + sib_buf[...]

    def start2():
        for cp in sends:
            cp.start()
        ici_buf[chip] = sib_buf[...]

    def finish2():
        for cp in sends:
            cp.wait()
        out_ref[mine, :] = (ici_buf[0] + ici_buf[1]) + (ici_buf[2] + ici_buf[3])

    return [(cp1.start, finish1), (start2, finish2), (cp3.start, cp3.wait)]


def _reduce_all(g4_in, sc_in, g4_out, sc_out, small, dada):
    shapes = [g4_in.shape[1:], g4_out.shape[1:]]
    R = small.shape[0]
    W = dada.shape[1]
    n_sem = 4

    def body(gin_ref, scin_ref, gout_ref, scout_ref, sm_ref, d_ref, oin_ref, oout_ref, osm_ref, dall_ref, *scratch):
        x, y, cc = lax.axis_index("x"), lax.axis_index("y"), lax.axis_index("c")
        me = 4 * x + 2 * y + cc
        pos = (x, y, cc, 2 * x + y, (x, y, 1 - cc))
        dslab, ds_sem, dr_sem = scratch[0:3]
        rest = scratch[3:]
        in_bufs, rest = rest[0:3 + n_sem], rest[3 + n_sem:]
        out_bufs, rest = rest[0:3 + n_sem], rest[3 + n_sem:]
        sm_bufs = rest
        dslab[...] = jnp.broadcast_to(d_ref[...], (8, W))
        dall_ref[me] = dslab[...]
        gathers = []
        for k, (dx, dy, dc) in enumerate(REL7):
            cp = pltpu.make_async_remote_copy(
                src_ref=dslab, dst_ref=dall_ref.at[me], send_sem=ds_sem.at[k], recv_sem=dr_sem.at[k],
                device_id=(_flip(x, dx), _flip(y, dy), _flip(cc, dc)), device_id_type=MESH)
            cp.start()
            gathers.append(cp)
        plans = [_scatter_stages(pos, gin_ref, scin_ref, oin_ref, *in_bufs),
                 _scatter_stages(pos, gout_ref, scout_ref, oout_ref, *out_bufs),
                 _all_reduce_stages(pos, sm_ref, osm_ref, *sm_bufs)]
        for stage in range(3):
            for plan in plans:
                plan[stage][0]()
            for plan in plans:
                plan[stage][1]()
        for cp in gathers:
            cp.wait()

    def sems():
        return [pltpu.SemaphoreType.DMA((2,)), pltpu.SemaphoreType.DMA((3,)),
                pltpu.SemaphoreType.DMA((3,)), pltpu.SemaphoreType.DMA((2,))]

    scratch = [pltpu.VMEM((8, W), F32), pltpu.SemaphoreType.DMA((7,)), pltpu.SemaphoreType.DMA((7,))]
    for r, c in shapes:
        scratch += [pltpu.VMEM((N_CHIPS, r // 2, c), F32), pltpu.VMEM((N_CHIPS, r // 2, c), BF16),
                    pltpu.VMEM((N_CHIPS, r // 2, c), BF16)] + sems()
    scratch += [pltpu.VMEM((R // 2, 128), F32), pltpu.VMEM((N_CHIPS, R // 2, 128), F32)] + sems()
    vm = pl.BlockSpec(memory_space=pltpu.VMEM)
    return pl.pallas_call(
        body, name="reduce_all",
        out_shape=(jax.ShapeDtypeStruct(g4_in.shape[1:], F32), jax.ShapeDtypeStruct(g4_out.shape[1:], F32),
                   jax.ShapeDtypeStruct((R, 128), F32), jax.ShapeDtypeStruct((8, 8, W), F32)),
        in_specs=[vm] * 6, out_specs=(vm,) * 4,
        scratch_shapes=scratch,
        compiler_params=_params(),
    )(g4_in, sc_in, g4_out, sc_out, small, dada)


def _in_proj(x, shift, scale, wt_pad, b_pad):
    S = x.shape[0]
    tm = min(TM_PROJ, S)

    def body(x_ref, sh_ref, sc_ref, w_ref, b_ref, u_ref, qkv_ref, f_ref, p_ref, g_ref):
        u = (x_ref[...] * (1.0 + sc_ref[...]) + sh_ref[...]).astype(BF16)
        u_ref[...] = u
        qkv_ref[...] = (_dot_nt(u, w_ref[O_QKV:O_F, :]) + b_ref[:, O_QKV:O_F]).astype(BF16)
        f_ref[...] = _dot_nt(u, w_ref[O_F:O_P, :]) + b_ref[:, O_F:O_P]
        p_ref[...] = _dot_nt(u, w_ref[O_P:O_G, :]) + b_ref[:, O_P:O_G]
        g_ref[...] = _dot_nt(u, w_ref[O_G:D_PAD, :]) + b_ref[:, O_G:D_PAD]

    row = lambda w: pl.BlockSpec((tm, w), lambda i: (i, 0))
    full = lambda a: pl.BlockSpec(a.shape, lambda i: (0, 0))
    return pl.pallas_call(
        body, name="in_proj", grid=(S // tm,),
        out_shape=(jax.ShapeDtypeStruct((S, D), BF16), jax.ShapeDtypeStruct((S, 3 * D_ATT), BF16),
                   jax.ShapeDtypeStruct((S, 128), F32), jax.ShapeDtypeStruct((S, D_POOL), F32),
                   jax.ShapeDtypeStruct((S, D), F32)),
        in_specs=[row(D), full(shift), full(scale), full(wt_pad), full(b_pad)],
        out_specs=(row(D), row(3 * D_ATT), row(128), row(D_POOL), row(D)),
        compiler_params=_params(dimension_semantics=("arbitrary",)),
    )(x, shift, scale, wt_pad, b_pad)


def _forget_cumsum(f):
    S = f.shape[0]
    tm = min(T_ATT, S)

    def body(f_ref, out_ref, carry):
        @pl.when(pl.program_id(0) == 0)
        def _():
            carry[...] = jnp.zeros_like(carry)
        v = f_ref[...]
        logf = jnp.minimum(v, 0.0) - jnp.log(1.0 + jnp.exp(-jnp.abs(v)))
        r = lax.broadcasted_iota(jnp.int32, (tm, tm), 0)
        c = lax.broadcasted_iota(jnp.int32, (tm, tm), 1)
        tri = (r <= c).astype(F32)
        rows8 = logf.T[0:8, :]
        cum8 = jnp.dot(rows8, tri, preferred_element_type=F32, precision=lax.Precision.HIGHEST) + carry[...]
        out_ref[...] = jnp.concatenate([cum8, jnp.zeros((128 - 8, tm), F32)], axis=0).T
        last = lax.broadcasted_iota(jnp.int32, (1, tm), 1) == tm - 1
        carry[...] = jnp.sum(jnp.where(last, cum8, 0.0), axis=1, keepdims=True)

    return pl.pallas_call(
        body, name="forget_cumsum", grid=(S // tm,),
        out_shape=jax.ShapeDtypeStruct((S, 128), F32),
        in_specs=[pl.BlockSpec((tm, 128), lambda i: (i, 0))],
        out_specs=pl.BlockSpec((tm, 128), lambda i: (i, 0)),
        scratch_shapes=[pltpu.VMEM((8, 1), F32)],
        compiler_params=_params(dimension_semantics=("arbitrary",)),
    )(f)


def _split3(v):
    hi = v.astype(BF16)
    rest = v - hi.astype(F32)
    mid = rest.astype(BF16)
    lo = (rest - mid.astype(F32)).astype(BF16)
    return hi, mid, lo


def _attention_fwd(qkv, big_f):
    S = qkv.shape[0]
    T = min(T_ATT, S)
    n_t = S // T

    def body(q_ref, k_ref, v_ref, f_ref, o_ref, lse_ref, kaug_sc, vt_sc, m_sc, l_sc, acc_sc):
        hp = pl.program_id(0)
        i = pl.program_id(1)
        lane = lax.broadcasted_iota(jnp.int32, (1, 128), 1)
        sub = lax.broadcasted_iota(jnp.int32, (128, 1), 0)
        head_sel = (lane < HEAD_DIM, lane >= HEAD_DIM)
        head_sel_t = (sub < HEAD_DIM, sub >= HEAD_DIM)
        spare = (HEAD_DIM, 0)
        zero = jnp.zeros((), BF16)

        @pl.when(i == 0)
        def _():
            def prep(jt, carry):
                rows = pl.ds(pl.multiple_of(jt * T, T), T)
                k = k_ref[rows, :]
                ft = f_ref[rows, :]
                vt = v_ref[rows, :].astype(F32).T
                for h in range(2):
                    fh = jnp.sum(jnp.where(lane == 2 * hp + h, ft, 0.0), axis=1, keepdims=True)
                    hi, mid, lo = _split3(-fh)
                    b = spare[h]
                    bias = jnp.where(lane == b, hi, jnp.where(lane == b + 1, mid, jnp.where(lane == b + 2, lo, zero)))
                    kaug_sc[h, rows, :] = jnp.where(head_sel[h], k, bias)
                    vt_sc[h, jt] = jnp.where(head_sel_t[h], vt, 0.0).astype(BF16)
                return carry

            lax.fori_loop(0, n_t, prep, 0)

        q = q_ref[...]
        q_heads = []
        for h in range(2):
            ones = jnp.where((lane >= spare[h]) & (lane < spare[h] + 3), jnp.ones((), BF16), zero)
            q_heads.append(jnp.where(head_sel[h], q, ones))
        m_sc[...] = jnp.full((8, T), NEG, F32)
        l_sc[...] = jnp.zeros((8, T), F32)
        acc_sc[...] = jnp.zeros((128, T), F32)

        def update(j, k_lo, n_k, q_lo, masked):
            rows = pl.ds(pl.multiple_of(j * T + k_lo, n_k), n_k)
            n_q = T - q_lo
            alphas, pvs = [], []
            for h in range(2):
                s_t = _dot_nt(kaug_sc[h, rows, :], q_heads[h][q_lo:, :])
                if masked:
                    rr = lax.broadcasted_iota(jnp.int32, (n_k, n_q), 0) + k_lo
                    cc = lax.broadcasted_iota(jnp.int32, (n_k, n_q), 1) + q_lo
                    s_t = jnp.where(rr <= cc, s_t, NEG)
                m_prev = m_sc[h:h + 1, q_lo:]
                m_new = jnp.maximum(m_prev, jnp.max(s_t, axis=0, keepdims=True))
                alpha = jnp.exp(m_prev - m_new)
                p_t = jnp.exp(s_t - m_new)
                l_sc[h:h + 1, q_lo:] = alpha * l_sc[h:h + 1, q_lo:] + jnp.sum(p_t, axis=0, keepdims=True)
                m_sc[h:h + 1, q_lo:] = m_new
                alphas.append(alpha)
                pvs.append(_dot(vt_sc[h, j, :, k_lo:k_lo + n_k], p_t.astype(BF16)))
            acc_sc[:, q_lo:] = (acc_sc[:, q_lo:] * jnp.where(head_sel_t[0], alphas[0], alphas[1])
                                + (pvs[0] + pvs[1]))

        def two_off_diagonal(jj, carry):
            update(2 * jj, 0, T, 0, False)
            update(2 * jj + 1, 0, T, 0, False)
            return carry

        lax.fori_loop(0, i // 2, two_off_diagonal, 0)

        @pl.when(i % 2 == 1)
        def _():
            update(i - 1, 0, T, 0, False)

        update(i, 0, T, 0, True)
        l = l_sc[...]
        o_ref[...] = (acc_sc[...] / jnp.where(head_sel_t[0], l[0:1, :], l[1:2, :])).T
        is_head = lax.broadcasted_iota(jnp.int32, (8, 1), 0) < 2
        lse_ref[...] = jnp.where(is_head, m_sc[...] + jnp.log(jnp.where(is_head, l, 1.0)), 0.0)

    return pl.pallas_call(
        body, name="attention_fwd", grid=(N_PAIR, n_t),
        out_shape=(jax.ShapeDtypeStruct((S, D_ATT), F32), jax.ShapeDtypeStruct((N_PAIR, n_t, 8, T), F32)),
        in_specs=[pl.BlockSpec((T, 128), lambda hp, i: (i, hp)),
                  pl.BlockSpec((S, 128), lambda hp, i: (0, N_PAIR + hp)),
                  pl.BlockSpec((S, 128), lambda hp, i: (0, 2 * N_PAIR + hp)),
                  pl.BlockSpec((S, 128), lambda hp, i: (0, 0))],
        out_specs=(pl.BlockSpec((T, 128), lambda hp, i: (i, hp)),
                   pl.BlockSpec((None, None, 8, T), lambda hp, i: (hp, i, 0, 0))),
        scratch_shapes=[pltpu.VMEM((2, S, 128), BF16), pltpu.VMEM((2, n_t, 128, T), BF16),
                        pltpu.VMEM((8, T), F32), pltpu.VMEM((8, T), F32), pltpu.VMEM((128, T), F32)],
        compiler_params=_params(dimension_semantics=("arbitrary", "arbitrary")),
    )(qkv, qkv, qkv, big_f)


def _attention_bwd(qkv, datt, att, lse, big_f):
    S = qkv.shape[0]
    T = min(T_ATT, S)
    n_t = S // T

    def body(q_ref, do_ref, o_ref, lse_ref, k_ref, v_ref, fk_ref,
             dq_ref, dk_ref, dv_ref, cs_ref, dfk_ref, dfq_ref, stat_sc, dqt_sc, qaug_sc):
        hp = pl.program_id(0)
        j = pl.program_id(1)
        lane = lax.broadcasted_iota(jnp.int32, (1, 128), 1)
        sub = lax.broadcasted_iota(jnp.int32, (128, 1), 0)
        head_sel = (lane < HEAD_DIM, lane >= HEAD_DIM)
        head_sel_t = (sub < HEAD_DIM, sub >= HEAD_DIM)
        spare = (HEAD_DIM, 0)
        zero = jnp.zeros((), BF16)
        one = jnp.ones((), BF16)

        def bias_lanes(first, pieces):
            hi, mid, lo = pieces
            return lambda rest: jnp.where(lane == first, hi, jnp.where(lane == first + 1, mid,
                                                                        jnp.where(lane == first + 2, lo, rest)))

        @pl.when(j == 0)
        def _():
            dqt_sc[...] = jnp.zeros_like(dqt_sc)
            cs_ref[...] = jnp.zeros_like(cs_ref)
            dfq_ref[...] = jnp.zeros_like(dfq_ref)

            def prep(i, carry):
                rows = pl.ds(pl.multiple_of(i * T, T), T)
                q = q_ref[rows, :]
                do = do_ref[rows, :]
                prod = o_ref[rows, :] * do.astype(F32)
                d_a = jnp.sum(jnp.where(head_sel[0], prod, 0.0), axis=1, keepdims=True)
                d_b = jnp.sum(jnp.where(head_sel[0], 0.0, prod), axis=1, keepdims=True)
                delta_t = jnp.where(head_sel[0], d_a, d_b).T
                stat_sc[i, 0:1, :] = delta_t[0:1, :]
                stat_sc[i, 1:2, :] = delta_t[HEAD_DIM:HEAD_DIM + 1, :]
                lse = lse_ref[i]
                lse_cols = jnp.where(head_sel_t[0], lse[0:1, :], lse[1:2, :]).T
                for h in range(2):
                    neg_lse = -lse_cols[:, h * HEAD_DIM:h * HEAD_DIM + 1]
                    ones = jnp.where((lane >= spare[h]) & (lane < spare[h] + 3), one, zero)
                    qaug_sc[h, rows, :] = jnp.where(head_sel[h], q, bias_lanes(spare[h] + 3, _split3(neg_lse))(ones))
                return carry

            lax.fori_loop(0, n_t, prep, 0)

        k = k_ref[...]
        v = v_ref[...]
        fk = fk_ref[...]
        kt = k.astype(F32).T
        heads = []
        for h in range(2):
            fkh = jnp.sum(jnp.where(lane == 2 * hp + h, fk, 0.0), axis=1, keepdims=True)
            ones = jnp.where((lane >= spare[h] + 3) & (lane < spare[h] + 6), one, zero)
            kaug = jnp.where(head_sel[h], k, bias_lanes(spare[h], _split3(-fkh))(ones))
            heads.append((kaug, jnp.where(head_sel[h], v, zero), jnp.where(head_sel_t[h], kt, 0.0).astype(BF16)))

        def block(i, k_lo, n_k, q_lo, masked):
            n_q = T - q_lo
            rows = pl.ds(pl.multiple_of(i * T + q_lo, n_q), n_q)
            q = q_ref[rows, :]
            do = do_ref[rows, :]
            stat = stat_sc[i]
            dk = jnp.zeros((n_k, 128), F32)
            dv = jnp.zeros((n_k, 128), F32)
            dqt = jnp.zeros((128, n_q), F32)
            dfs = []
            for h in range(2):
                kaug, vh, kth = heads[h]
                arg = _dot_nt(kaug[k_lo:k_lo + n_k, :], qaug_sc[h, rows, :])
                if masked:
                    rr = lax.broadcasted_iota(jnp.int32, (n_k, n_q), 0) + k_lo
                    cc = lax.broadcasted_iota(jnp.int32, (n_k, n_q), 1) + q_lo
                    arg = jnp.where(rr <= cc, arg, NEG)
                p_t = jnp.exp(arg)
                ds_t = p_t * (_dot_nt(vh[k_lo:k_lo + n_k, :], do) - stat[h:h + 1, q_lo:])
                ds_bf = ds_t.astype(BF16)
                dv = dv + _dot(p_t.astype(BF16), jnp.where(head_sel[h], do, zero))
                dk = dk + _dot(ds_bf, jnp.where(head_sel[h], q, zero))
                dqt = dqt + _dot(kth[:, k_lo:k_lo + n_k], ds_bf)
                dfs.append(jnp.sum(ds_t, axis=1, keepdims=True))
                dfq_ref[i, h:h + 1, q_lo:] += _colsum(ds_t)
            dqt_sc[i, :, q_lo:] += dqt
            return dk, dv, dfs[0], dfs[1]

        def off_diagonal(i, acc):
            return tuple(a + b for a, b in zip(acc, block(i, 0, T, 0, False)))

        half = T // 2
        early = block(j, 0, half, 0, True)
        late = block(j, half, half, half, True)
        acc1 = tuple(jnp.concatenate([a, b], axis=0) for a, b in zip(early, late))
        dk_acc, dv_acc, dfa, dfb = lax.fori_loop(j + 1, n_t, off_diagonal, acc1)
        dk_ref[...] = dk_acc.astype(BF16)
        dv_ref[...] = dv_acc.astype(BF16)
        dfk_ref[...] = -jnp.where(lane == 0, dfa, jnp.where(lane == 1, dfb, 0.0))
        cs_ref[:, 128:256] = cs_ref[:, 128:256] + _colsum(dk_acc)
        cs_ref[:, 256:384] = cs_ref[:, 256:384] + _colsum(dv_acc)

        @pl.when(j == n_t - 1)
        def _():
            def finish(i, tot):
                dq = dqt_sc[i].T
                dq_ref[pl.ds(pl.multiple_of(i * T, T), T), :] = dq.astype(BF16)
                return tot + _colsum(dq)

            cs_ref[:, 0:128] = lax.fori_loop(0, n_t, finish, jnp.zeros((1, 128), F32))

    pair_rows = lambda hp, j: (hp, 0, 0)
    return pl.pallas_call(
        body, name="attention_bwd", grid=(N_PAIR, n_t),
        out_shape=(jax.ShapeDtypeStruct((S, D_ATT), BF16), jax.ShapeDtypeStruct((S, D_ATT), BF16),
                   jax.ShapeDtypeStruct((S, D_ATT), BF16), jax.ShapeDtypeStruct((N_PAIR, 1, 384), F32),
                   jax.ShapeDtypeStruct((N_PAIR, S, 128), F32),
                   jax.ShapeDtypeStruct((N_PAIR, n_t, 8, T), F32)),
        in_specs=[pl.BlockSpec((S, 128), lambda hp, j: (0, hp)),
                  pl.BlockSpec((S, 128), lambda hp, j: (0, hp)),
                  pl.BlockSpec((S, 128), lambda hp, j: (0, hp)),
                  pl.BlockSpec((None, n_t, 8, T), lambda hp, j: (hp, 0, 0, 0)),
                  pl.BlockSpec((T, 128), lambda hp, j: (j, N_PAIR + hp)),
                  pl.BlockSpec((T, 128), lambda hp, j: (j, 2 * N_PAIR + hp)),
                  pl.BlockSpec((T, 128), lambda hp, j: (j, 0))],
        out_specs=(pl.BlockSpec((S, 128), lambda hp, j: (0, hp)),
                   pl.BlockSpec((T, 128), lambda hp, j: (j, hp)),
                   pl.BlockSpec((T, 128), lambda hp, j: (j, hp)),
                   pl.BlockSpec((None, 1, 384), pair_rows),
                   pl.BlockSpec((None, T, 128), lambda hp, j: (hp, j, 0)),
                   pl.BlockSpec((None, n_t, 8, T), lambda hp, j: (hp, 0, 0, 0))),
        scratch_shapes=[pltpu.VMEM((n_t, 8, T), F32), pltpu.VMEM((n_t, 128, T), F32),
                        pltpu.VMEM((2, S, 128), BF16)],
        compiler_params=_params(dimension_semantics=("arbitrary", "arbitrary")),
    )(qkv, datt, att, lse, qkv, qkv, big_f)


def _window_counts(first_row, n_rows, window):
    t = lax.broadcasted_iota(jnp.int32, (n_rows, 1), 0) + first_row
    return jnp.minimum((t + 1).astype(F32), float(window))


def _middle(x, tgt, att, g, p, gate, w_mix, b_mix, pool_scale, w_out, b_out, ln_g, ln_b):
    S = x.shape[0]
    tm = min(TM_MID, S)
    halo_blocks = tm // POOL_HALO

    def body(x_ref, t_ref, att_ref, g_ref, p_ref, ph_ref, gate_ref, wm_ref, bm_ref, ps_ref, wo_ref, bo_ref,
             lg_ref, lb_ref,
             dh_ref, datt_ref, dg_ref, dpl_ref, gwo_ref, gwm_ref, vec_ref, loss_ref):
        i = pl.program_id(0)

        @pl.when(i == 0)
        def _():
            gwo_ref[...] = jnp.zeros_like(gwo_ref)
            gwm_ref[...] = jnp.zeros_like(gwm_ref)
            vec_ref[...] = jnp.zeros_like(vec_ref)
            loss_ref[...] = jnp.zeros_like(loss_ref)

        pc = p_ref[...]
        halo = jnp.where(i > 0, ph_ref[...], 0.0)
        pe = jnp.concatenate([halo, pc], axis=0)
        pooled_parts = []
        for gi, w in enumerate(POOL_WINDOWS):
            cur = pe[:, gi * POOL_GROUP:(gi + 1) * POOL_GROUP]
            span = 1
            while span < w:
                cur = cur + pltpu.roll(cur, span, 0)
                span *= 2
            wsum = cur[POOL_HALO:, :]
            mean = wsum / _window_counts(i * tm, tm, w)
            pooled_parts.append(mean - pc[:, gi * POOL_GROUP:(gi + 1) * POOL_GROUP])
        pooled_bf =[v.astype(BF16) for v in pooled_parts]
        mixed = jnp.concatenate([_dot(pooled_bf[gi], wm_ref[gi]) for gi in range(4)], axis=1) + bm_ref[...]
        ps = ps_ref[...]
        pool_out = mixed * ps
        gv = g_ref[...]
        sig = _sigmoid(gv)
        silu = gv * sig
        att = att_ref[...]
        y = jnp.concatenate([att * silu[:, :D_ATT], pool_out * silu[:, D_ATT:]], axis=1)
        y_bf = y.astype(BF16)
        wo = wo_ref[...]
        yo = _dot(y_bf, wo) + bo_ref[...]
        gate = gate_ref[...]
        h = ALPHA * x_ref[...] + gate * yo
        mu = jnp.mean(h, axis=1, keepdims=True)
        hc = h - mu
        var = jnp.mean(hc * hc, axis=1, keepdims=True)
        rstd = lax.rsqrt(var + LN_EPS)
        yhat = hc * rstd
        lg = lg_ref[...]
        out = yhat * lg + lb_ref[...]
        err = out - t_ref[...]
        loss_ref[...] += 0.5 * jnp.sum(jnp.mean(err * err, axis=1, keepdims=True), axis=0, keepdims=True)

        dout = err * (1.0 / D)
        g_ln_b = _colsum(dout)
        g_ln_g = _colsum(dout * yhat)
        dyh = dout * lg
        dh = rstd * (dyh - jnp.mean(dyh, axis=1, keepdims=True)
                     - yhat * jnp.mean(dyh * yhat, axis=1, keepdims=True))
        dh_ref[...] = dh
        d_gate = _colsum(dh * yo)
        dyo = gate * dh
        g_b_out = _colsum(dyo)
        dyo_bf = dyo.astype(BF16)
        gwo_ref[...] += _dot_tn(y_bf, dyo_bf)
        dy = _dot_nt(dyo_bf, wo)
        dsilu = sig * (1.0 + gv * (1.0 - sig))
        dy_a = dy[:, :D_ATT]
        dy_p = dy[:, D_ATT:]
        datt_ref[...] = (dy_a * silu[:, :D_ATT]).astype(BF16)
        dpo = dy_p * silu[:, D_ATT:]
        dg = jnp.concatenate([dy_a * att * dsilu[:, :D_ATT], dy_p * pool_out * dsilu[:, D_ATT:]], axis=1)
        dg_ref[...] = dg.astype(BF16)
        g_dg = _colsum(dg)
        g_ps = _colsum(dpo * mixed)
        dmixed = dpo * ps
        g_bm = _colsum(dmixed)
        dmixed_bf = dmixed.astype(BF16)
        dpl = []
        for gi in range(4):
            dm = dmixed_bf[:, gi * POOL_GROUP:(gi + 1) * POOL_GROUP]
            gwm_ref[gi] += _dot_tn(pooled_bf[gi], dm)
            dpl.append(_dot_nt(dm, wm_ref[gi]))
        dpl_ref[...] = jnp.concatenate(dpl, axis=1)
        vec_ref[0:1, :] += g_ln_g
        vec_ref[1:2, :] += g_ln_b
        vec_ref[2:3, :] += d_gate
        vec_ref[3:4, :] += g_b_out
        vec_ref[4:5, :] += g_dg
        vec_ref[5:6, 0:D_POOL] += g_ps
        vec_ref[6:7, 0:D_POOL] += g_bm

    row = lambda w: pl.BlockSpec((tm, w), lambda i: (i, 0))
    full2 = lambda a: pl.BlockSpec(a.shape, lambda i: (0, 0))
    full3 = lambda a: pl.BlockSpec(a.shape, lambda i: (0, 0, 0))
    return pl.pallas_call(
        body, name="middle", grid=(S // tm,),
        out_shape=(jax.ShapeDtypeStruct((S, D), F32),
                   jax.ShapeDtypeStruct((S, D_ATT), BF16),
                   jax.ShapeDtypeStruct((S, D), BF16),
                   jax.ShapeDtypeStruct((S, D_POOL), F32),
                   jax.ShapeDtypeStruct((D, D), F32),
                   jax.ShapeDtypeStruct((4, POOL_GROUP, POOL_GROUP), F32),
                   jax.ShapeDtypeStruct((8, D), F32),
                   jax.ShapeDtypeStruct((1, 1), F32)),
        in_specs=[row(D), row(D), row(D_ATT), row(D), row(D_POOL),
                  pl.BlockSpec((POOL_HALO, D_POOL), lambda i: (jnp.maximum(i * halo_blocks - 1, 0), 0)),
                  full2(gate), full3(w_mix), full2(b_mix), full2(pool_scale), full2(w_out), full2(b_out),
                  full2(ln_g), full2(ln_b)],
        out_specs=(row(D), row(D_ATT), row(D), row(D_POOL),
                   pl.BlockSpec((D, D), lambda i: (0, 0)),
                   pl.BlockSpec((4, POOL_GROUP, POOL_GROUP), lambda i: (0, 0, 0)),
                   pl.BlockSpec((8, D), lambda i: (0, 0)),
                   pl.BlockSpec((1, 1), lambda i: (0, 0))),
        compiler_params=_params(dimension_semantics=("arbitrary",)),
    )(x, tgt, att, g, p, p, gate, w_mix, b_mix, pool_scale, w_out, b_out, ln_g, ln_b)


def _tail(dpl, dfk, dfq, f):
    S = dpl.shape[0]
    tm = min(T_ATT, S)
    n_t = S // tm
    halo_blocks = tm // POOL_HALO
    last_halo = S // POOL_HALO - 1

    def body(d_ref, dn_ref, dfk_ref, dfq_ref, f_ref, dp_ref, df_ref, cs_ref, carry):
        s = pl.program_id(0)
        i = n_t - 1 - s

        @pl.when(s == 0)
        def _():
            carry[...] = jnp.zeros_like(carry)
            cs_ref[...] = jnp.zeros_like(cs_ref)

        dc = d_ref[...]
        nxt = jnp.where(s > 0, dn_ref[...], 0.0)
        de = jnp.concatenate([dc, nxt], axis=0)
        n_e = tm + POOL_HALO
        parts = []
        for gi, w in enumerate(POOL_WINDOWS):
            cur = de[:, gi * POOL_GROUP:(gi + 1) * POOL_GROUP] / _window_counts(i * tm, n_e, w)
            span = 1
            while span < w:
                cur = cur + pltpu.roll(cur, n_e - span, 0)
                span *= 2
            parts.append(cur[:tm, :] - dc[:, gi * POOL_GROUP:(gi + 1) * POOL_GROUP])
        dp = jnp.concatenate(parts, axis=1)
        dp_ref[...] = dp.astype(BF16)
        cs_ref[0:1, :] += _colsum(dp)

        r = lax.broadcasted_iota(jnp.int32, (tm, tm), 0)
        c = lax.broadcasted_iota(jnp.int32, (tm, tm), 1)
        tri = (r >= c).astype(F32)
        k_cols = dfk_ref[0]
        rows8 = dfq_ref[0]
        for hp in range(1, N_PAIR):
            k_cols = k_cols + pltpu.roll(dfk_ref[hp], 2 * hp, 1)
            rows8 = rows8 + pltpu.roll(dfq_ref[hp], 2 * hp, 0)
        rows8 = rows8 + k_cols.T[0:8, :]
        dlogf8 = jnp.dot(rows8, tri, preferred_element_type=F32, precision=lax.Precision.HIGHEST) + carry[...]
        first = lax.broadcasted_iota(jnp.int32, (1, tm), 1) == 0
        carry[...] = jnp.sum(jnp.where(first, dlogf8, 0.0), axis=1, keepdims=True)
        dlogf = jnp.concatenate([dlogf8, jnp.zeros((128 - 8, tm), F32)], axis=0).T
        df = dlogf * _sigmoid(-f_ref[...])
        df_ref[...] = df.astype(BF16)
        cs_ref[1:2, 0:128] += _colsum(df)

    rev = lambda w: pl.BlockSpec((tm, w), lambda s: (n_t - 1 - s, 0))
    return pl.pallas_call(
        body, name="tail", grid=(n_t,),
        out_shape=(jax.ShapeDtypeStruct((S, D_POOL), BF16), jax.ShapeDtypeStruct((S, 128), BF16),
                   jax.ShapeDtypeStruct((8, D_POOL), F32)),
        in_specs=[rev(D_POOL),
                  pl.BlockSpec((POOL_HALO, D_POOL),
                               lambda s: (jnp.minimum((n_t - s) * halo_blocks, last_halo), 0)),
                  pl.BlockSpec((N_PAIR, tm, 128), lambda s: (0, n_t - 1 - s, 0)),
                  pl.BlockSpec((N_PAIR, None, 8, tm), lambda s: (0, n_t - 1 - s, 0, 0)),
                  rev(128)],
        out_specs=(rev(D_POOL), rev(128), pl.BlockSpec((8, D_POOL), lambda s: (0, 0))),
        scratch_shapes=[pltpu.VMEM((8, 1), F32)],
        compiler_params=_params(dimension_semantics=("arbitrary",)),
    )(dpl, dpl, dfk, dfq, f)


PIECES = ((O_QKV, D_ATT), (O_QKV + D_ATT, D_ATT), (O_QKV + 2 * D_ATT, D_ATT), (O_F, 128), (O_P, D_POOL), (O_G, D))


def _grad_w_in(u, pieces):
    S = u.shape[0]
    tm = min(TM_GW, S)
    n_t = S // tm

    def body(u_ref, *rest):
        piece_refs, out_ref, acc, sem = rest[:6], rest[6], rest[7], rest[8]
        i = pl.program_id(0)

        @pl.when(i == 0)
        def _():
            acc[...] = jnp.zeros_like(acc)

        u_t = u_ref[...]
        for (off, w), ref in zip(PIECES, piece_refs):
            acc[:, off:off + w] += _dot_tn(u_t, ref[...])

        @pl.when(i == n_t - 1)
        def _():
            cp = pltpu.make_async_copy(acc, out_ref, sem)
            cp.start()
            cp.wait()

    return pl.pallas_call(
        body, name="grad_w_in", grid=(n_t,),
        out_shape=jax.ShapeDtypeStruct((D, D_PAD), F32),
        in_specs=[pl.BlockSpec((tm, D), lambda i: (i, 0))]
        + [pl.BlockSpec((tm, w), lambda i: (i, 0)) for _, w in PIECES],
        out_specs=pl.BlockSpec(memory_space=pl.ANY),
        scratch_shapes=[pltpu.VMEM((D, D_PAD), F32), pltpu.SemaphoreType.DMA],
        compiler_params=_params(dimension_semantics=("arbitrary",)),
    )(u, *pieces)


def _grad_x(pieces, wt_pad, dh, x, scale):
    S = x.shape[0]
    tm = min(TM_DU, S)

    def body(*refs):
        piece_refs = refs[:6]
        w_ref, dh_ref, x_ref, sc_ref, gx_ref, vec_ref = refs[6:]

        @pl.when(pl.program_id(0) == 0)
        def _():
            vec_ref[...] = jnp.zeros_like(vec_ref)

        du = jnp.zeros((tm, D), F32)
        for (off, w), ref in zip(PIECES, piece_refs):
            du = du + _dot(ref[...], w_ref[off:off + w, :])
        xv = x_ref[...]
        gx_ref[...] = ALPHA * dh_ref[...] + du * (1.0 + sc_ref[...])
        vec_ref[0:1, :] += _colsum(du)
        vec_ref[1:2, :] += _colsum(du * xv)

    row = lambda w: pl.BlockSpec((tm, w), lambda i: (i, 0))
    return pl.pallas_call(
        body, name="grad_x", grid=(S // tm,),
        out_shape=(jax.ShapeDtypeStruct((S, D), F32), jax.ShapeDtypeStruct((8, D), F32)),
        in_specs=[row(w) for _, w in PIECES]
        + [pl.BlockSpec(wt_pad.shape, lambda i: (0, 0)), row(D), row(D), pl.BlockSpec((1, D), lambda i: (0, 0))],
        out_specs=(row(D), pl.BlockSpec((8, D), lambda i: (0, 0))),
        compiler_params=_params(dimension_semantics=("arbitrary",)),
    )(*pieces, wt_pad, dh, x, scale)


def _grad_ada(c_all, dada_all, dada_cols):
    def body(c_ref, dall_ref, dcol_ref, gw_ref, gb_ref):
        rows = lax.broadcasted_iota(jnp.int32, (8, 1), 0)
        cm = jnp.zeros((8, D), F32)
        dm = jnp.zeros((8, 3 * D), F32)
        for r in range(8):
            cm = jnp.where(rows == r, c_ref[r], cm)
            dm = jnp.where(rows == r, dall_ref[r], dm)
        act = cm * _sigmoid(cm)
        pad = jnp.zeros((8, D), F32)
        lhs = jnp.concatenate([act, pad], axis=0).astype(BF16)
        rhs = jnp.concatenate([dcol_ref[...], jnp.zeros((8, SHARD_ADA), F32)], axis=0).astype(BF16)
        gw_ref[...] = _dot_tn(lhs, rhs)
        gb_ref[...] = _colsum(dm)

    vm = pl.BlockSpec(memory_space=pltpu.VMEM)
    return pl.pallas_call(
        body, name="grad_ada",
        out_shape=(jax.ShapeDtypeStruct((D, SHARD_ADA), F32), jax.ShapeDtypeStruct((1, 3 * D), F32)),
        in_specs=[vm, vm, vm], out_specs=(vm, vm),
        compiler_params=_params(),
    )(c_all, dada_all, dada_cols)


def _adamw_math(w, g, m, v):
    m = ADAM_B1 * m + (1.0 - ADAM_B1) * g
    v = ADAM_B2 * v + (1.0 - ADAM_B2) * (g * g)
    m_hat = m / (1.0 - ADAM_B1 ** ADAM_STEP)
    v_hat = v / (1.0 - ADAM_B2 ** ADAM_STEP)
    delta = -ADAM_LR * (m_hat / (jnp.sqrt(v_hat) + ADAM_EPS) + ADAM_WD * w)
    return delta, m, v


def _adamw(groups, n_steps):
    n = len(groups)

    def body(*refs):
        ins, outs = refs[:4 * n], refs[4 * n:]
        for t in range(n):
            w, g, m, v = (r[...] for r in ins[4 * t:4 * t + 4])
            d, m2, v2 = _adamw_math(w, g, m, v)
            outs[3 * t][...] = d
            outs[3 * t + 1][...] = m2
            outs[3 * t + 2][...] = v2

    in_specs, out_specs, out_shape, args = [], [], [], []
    for (w, g, m, v) in groups:
        rest = w.shape[1:]
        spec = pl.BlockSpec((w.shape[0] // n_steps,) + rest, lambda i, nd=len(rest): (i,) + (0,) * nd)
        in_specs += [spec] * 4
        out_specs += [spec] * 3
        out_shape += [jax.ShapeDtypeStruct(w.shape, F32)] * 3
        args += [w, g, m, v]
    return pl.pallas_call(
        body, name="adamw_%d_%d" % (n, n_steps), grid=(n_steps,),
        out_shape=tuple(out_shape), in_specs=in_specs, out_specs=tuple(out_specs),
        compiler_params=_params(dimension_semantics=("arbitrary",)),
    )(*args)


def _pack_small(parts):
    rows = []
    used = 0
    for name, (first, n_rows) in SMALL_SEGS.items():
        if first > used:
            rows.append(jnp.zeros((first - used, 128), F32))
        flat = parts[name].reshape(-1)
        flat = jnp.pad(flat, (0, n_rows * 128 - flat.shape[0]))
        rows.append(flat.reshape(n_rows, 128))
        used = first + n_rows
    rows.append(jnp.zeros((SMALL_ROWS - used, 128), F32))
    return jnp.concatenate(rows, axis=0)


def _unpack_small(buf, name, shape):
    first, n_rows = SMALL_SEGS[name]
    n = int(np.prod(shape))
    return buf[first:first + n_rows].reshape(-1)[:n].reshape(shape)


def _pad_in(v):
    r = v.shape[0]
    z = jnp.zeros((r, O_P - O_F - N_HEADS), v.dtype)
    return jnp.concatenate([v[:, :3 * D_ATT + N_HEADS], z, v[:, 3 * D_ATT + N_HEADS:]], axis=1)


def _unpad_in(v):
    return jnp.concatenate([v[:, :O_F + N_HEADS], v[:, O_P:]], axis=1)


def _shards_in(v):
    gap = O_P - (O_F + N_HEADS)
    parts = []
    for a in range(N_CHIPS):
        lo, hi = a * SHARD_IN, (a + 1) * SHARD_IN
        cut = O_F + N_HEADS
        if hi <= cut:
            parts.append(v[:, lo:hi])
        elif lo >= cut:
            parts.append(v[:, lo + gap:hi + gap])
        else:
            parts.append(jnp.concatenate([v[:, lo:cut], v[:, cut + gap:hi + gap]], axis=1))
    return jnp.stack(parts, axis=0)


def kernel(x, c, w_ada, b_ada, w_in, b_in, w_pool_mix, b_pool_mix, pool_scale, w_out, b_out, ln_g, ln_b, loss_target, m_w_ada, m_b_ada, m_w_in, m_b_in, m_w_pool_mix, m_b_pool_mix, m_pool_scale, m_w_out, m_b_out, m_ln_g, m_ln_b, v_w_ada, v_b_ada, v_w_in, v_b_in, v_w_pool_mix, v_b_pool_mix, v_pool_scale, v_w_out, v_b_out, v_ln_g, v_ln_b):
    S = x.shape[1]
    T = min(T_ATT, S)
    n_t = S // T
    chip = 2 * lax.axis_index("x") + lax.axis_index("y")
    x2 = x[0]
    tgt = loss_target[0]
    q_scale = jnp.concatenate([jnp.full((1, D_ATT), Q_SCALE, F32), jnp.ones((1, D_PAD - D_ATT), F32)], axis=1)

    to_cols = lambda a: jnp.transpose(a, (2, 0, 1))
    from_cols = lambda a: jnp.transpose(a, (1, 2, 0))
    c_all, ada4, wt_in_all, w_out_all = _gather_and_ada(
        c, w_ada[0], b_ada.reshape(4, 1, SHARD_ADA), to_cols(w_in).reshape(SHARD_IN, D).astype(BF16),
        w_out[0].astype(BF16))
    ada = ada4[:, 0, :].reshape(1, 3 * D)
    shift, scale, gate = ada[:, :D], ada[:, D:2 * D], ada[:, 2 * D:]
    wt_full = wt_in_all.reshape(D_IN, D)
    n_real = 3 * D_ATT + N_HEADS
    wt_pad = jnp.concatenate([wt_full[:D_ATT] * jnp.asarray(Q_SCALE, BF16), wt_full[D_ATT:n_real],
                              jnp.zeros((O_P - n_real, D), BF16), wt_full[n_real:]], axis=0)
    b_pad = _pad_in(b_in) * q_scale
    w_out_full = w_out_all.reshape(D, D)
    w_mix_bf = w_pool_mix[0].astype(BF16)

    u, qkv, f, p, g = _in_proj(x2, shift, scale, wt_pad, b_pad)
    big_f = _forget_cumsum(f)
    att, lse = _attention_fwd(qkv, big_f)

    dh, datt, dg, dpl, gw_out, gw_mix, vec, loss_part = _middle(
        x2, tgt, att, g, p, gate, w_mix_bf, b_pool_mix.reshape(1, D_POOL), pool_scale, w_out_full, b_out, ln_g, ln_b)
    dq, dk, dv, cs_att, dfk, dfq = _attention_bwd(qkv, datt, att, lse, big_f)
    dp, df, cs_tail = _tail(dpl, dfk, dfq, f)
    pieces = (dq, dk, dv, df, dp, dg)
    gw_pad = _grad_w_in(u, pieces)
    grad_x, vec_x = _grad_x(pieces, wt_pad, dh, x2, scale)

    cs_qkv = jnp.transpose(cs_att.reshape(N_PAIR, 3, 128), (1, 0, 2)).reshape(1, 3 * D_ATT)
    gb_pad = jnp.concatenate([cs_qkv, cs_tail[1:2, 0:128], cs_tail[0:1, :], vec[4:5, :]], axis=1) * q_scale
    dada = jnp.concatenate([vec_x[0:1, :], vec_x[1:2, :], vec[2:3, :]], axis=1)
    small = _pack_small({
        "b_in": gb_pad, "w_pool_mix": gw_mix, "b_pool_mix": vec[6:7, :D_POOL], "pool_scale": vec[5:6, :D_POOL],
        "b_out": vec[3:4, :], "ln_g": vec[0:1, :], "ln_b": vec[1:2, :], "loss": loss_part,
        "b_ada": jnp.zeros((1, 3 * D), F32)})

    g_w_in, g_w_out, small_sum, dada_all = _reduce_all(
        _shards_in(gw_pad), _shards_in(q_scale), gw_out.reshape(N_CHIPS, SHARD_OUT, D),
        jnp.ones((N_CHIPS, 1, D), F32), small[:SMALL_REDUCED_ROWS], dada)
    dada_cols = lax.dynamic_slice(dada_all[:, 0, :], (0, chip * SHARD_ADA), (8, SHARD_ADA))
    g_w_ada, g_b_ada = _grad_ada(c_all, dada_all, dada_cols)
    loss = _unpack_small(small_sum, "loss", (1,))[0]

    grads_small = jnp.concatenate([small_sum, g_b_ada.reshape(24, 128)], axis=0)
    no_param = jnp.zeros((1,), F32)
    small_w = {"b_in": _pad_in(b_in), "w_pool_mix": w_pool_mix, "b_pool_mix": b_pool_mix, "pool_scale": pool_scale,
               "b_out": b_out, "ln_g": ln_g, "ln_b": ln_b, "loss": no_param, "b_ada": b_ada}
    small_m = {"b_in": _pad_in(m_b_in), "w_pool_mix": m_w_pool_mix, "b_pool_mix": m_b_pool_mix,
               "pool_scale": m_pool_scale, "b_out": m_b_out, "ln_g": m_ln_g, "ln_b": m_ln_b, "loss": no_param,
               "b_ada": m_b_ada}
    small_v = {"b_in": _pad_in(v_b_in), "w_pool_mix": v_w_pool_mix, "b_pool_mix": v_b_pool_mix,
               "pool_scale": v_pool_scale, "b_out": v_b_out, "ln_g": v_ln_g, "ln_b": v_ln_b, "loss": no_param,
               "b_ada": v_b_ada}
    big = _adamw([(w_ada[0], g_w_ada, m_w_ada[0], v_w_ada[0]),
                  (w_out[0], g_w_out, m_w_out[0], v_w_out[0])], 8)
    g_w_in_cols = to_cols(g_w_in[None])
    big_in = _adamw([(to_cols(w_in), g_w_in_cols, to_cols(m_w_in), to_cols(v_w_in))], 14)
    sm = _adamw([(_pack_small(small_w), grads_small, _pack_small(small_m), _pack_small(small_v))], 1)

    names = ["w_ada", "b_ada", "w_in", "b_in", "w_pool_mix", "b_pool_mix", "pool_scale", "w_out", "b_out",
             "ln_g", "ln_b"]
    shapes = {"b_ada": (1, 3 * D), "b_in": (1, D_PAD), "w_pool_mix": (1, 4, POOL_GROUP, POOL_GROUP),
              "b_pool_mix": (1, 4, POOL_GROUP), "pool_scale": (1, D_POOL), "b_out": (1, D), "ln_g": (1, D),
              "ln_b": (1, D)}
    big_idx = {"w_ada": 0, "w_out": 1}

    def leaf(kind, name):
        if name == "w_in":
            return from_cols(g_w_in_cols if kind == 0 else big_in[kind - 1])
        if name in big_idx:
            if kind == 0:
                return (g_w_ada, g_w_out)[big_idx[name]][None]
            return big[3 * big_idx[name] + kind - 1][None]
        buf = grads_small if kind == 0 else sm[kind - 1]
        val = _unpack_small(buf, name, shapes[name])
        if name == "b_in":
            val = _unpad_in(val)
        return val

    outs = [loss, grad_x[None]]
    for kind in range(4):
        outs += [leaf(kind, n) for n in names]
    return tuple(outs)
```

```python
import functools

import numpy as np
import jax
import jax.numpy as jnp
from jax import lax
from jax.experimental import pallas as pl
from jax.experimental.pallas import tpu as pltpu

F32 = jnp.float32
BF16 = jnp.bfloat16
MESH = pl.DeviceIdType.MESH

D = 1024
D_ATT = 512
D_POOL = 512
N_HEADS = 8
HEAD_DIM = 64
N_PAIR = N_HEADS // 2
POOL_WINDOWS = (2, 4, 8, 16)
POOL_GROUP = 128
POOL_HALO = 16
LN_EPS = 1e-5
ALPHA = 2.0 ** 0.25
D_IN = 3 * D_ATT + N_HEADS + D_POOL + D_ATT + D_POOL
N_CHIPS = 4
SHARD_IN = D_IN // N_CHIPS
SHARD_ADA = 3 * D // N_CHIPS
SHARD_OUT = D // N_CHIPS

O_QKV, O_F, O_P, O_G, D_PAD = 0, 1536, 1664, 2176, 3200
Q_SCALE = HEAD_DIM ** -0.5

ADAM_LR, ADAM_B1, ADAM_B2, ADAM_EPS, ADAM_WD, ADAM_STEP = 0.001, 0.9, 0.999, 1e-08, 0.01, 10

NEG = -1e30

VMEM_LIMIT = 56 * 1024 * 1024

TM_PROJ = 512
T_ATT = 512
TM_MID = 256
TM_GW = 1024
TM_DU = 512

REL7 = [(0, 0, 1), (0, 1, 0), (0, 1, 1), (1, 0, 0), (1, 0, 1), (1, 1, 0), (1, 1, 1)]
REL3 = [(0, 1), (1, 0), (1, 1)]

SMALL_SEGS = {}
_row = 0
for _name, _n in (("b_in", 3200), ("w_pool_mix", 65536), ("b_pool_mix", 512), ("pool_scale", 512),
                  ("b_out", 1024), ("ln_g", 1024), ("ln_b", 1024), ("loss", 1)):
    _rows = -(-_n // 1024) * 8
    SMALL_SEGS[_name] = (_row, _rows)
    _row += _rows
SMALL_REDUCED_ROWS = -(-_row // 16) * 16
SMALL_SEGS["b_ada"] = (SMALL_REDUCED_ROWS, 24)
SMALL_ROWS = SMALL_REDUCED_ROWS + 24


def _params(**kw):
    return pltpu.CompilerParams(vmem_limit_bytes=VMEM_LIMIT, **kw)


def _flip(v, d):
    return v if d == 0 else 1 - v


def _dot(a, b):
    return jnp.dot(a, b, preferred_element_type=F32)


def _dot_nt(a, b):
    return lax.dot_general(a, b, (((1,), (1,)), ((), ())), preferred_element_type=F32)


def _dot_tn(a, b):
    return lax.dot_general(a, b, (((0,), (0,)), ((), ())), preferred_element_type=F32)


def _sigmoid(v):
    return 1.0 / (1.0 + jnp.exp(-v))


def _colsum(v):
    return jnp.sum(v, axis=0, keepdims=True)


def _gather_stages(pos, src_ref, dst_ref, half, own_sem, s_sem, r_sem, fs_sem, fr_sem):
    x, y, cc, chip, sib = pos
    own = pltpu.make_async_copy(src_ref, dst_ref.at[chip], own_sem)
    first, landed, others = [], [], []
    for k, (dx, dy) in enumerate(REL3):
        px, py = _flip(x, dx), _flip(y, dy)
        first.append(pltpu.make_async_remote_copy(
            src_ref=src_ref.at[half(cc)], dst_ref=dst_ref.at[(chip,) + half(cc)],
            send_sem=s_sem.at[k], recv_sem=r_sem.at[k], device_id=(px, py, cc), device_id_type=MESH))
        landed.append(dst_ref.at[(2 * px + py,) + half(cc)])
        others.append(dst_ref.at[(2 * px + py,) + half(1 - cc)])
    passed = [pltpu.make_async_remote_copy(src_ref=landed[k], dst_ref=landed[k], send_sem=fs_sem.at[k],
                                           recv_sem=fr_sem.at[k], device_id=sib, device_id_type=MESH)
              for k in range(3)]

    def start():
        own.start()
        for cp in first:
            cp.start()

    def forward():
        for k in range(3):
            pltpu.make_async_remote_copy(src_ref=landed[k], dst_ref=landed[k], send_sem=s_sem.at[k],
                                         recv_sem=r_sem.at[k], device_id=sib, device_id_type=MESH).wait_recv()
            passed[k].start()

    def finish():
        for k in range(3):
            pltpu.make_async_remote_copy(src_ref=others[k], dst_ref=others[k], send_sem=fs_sem.at[k],
                                         recv_sem=fr_sem.at[k], device_id=sib, device_id_type=MESH).wait_recv()
        for cp in first + passed:
            cp.wait_send()
        own.wait()

    return start, forward, finish


def _gather_scratch():
    return [pltpu.SemaphoreType.DMA, pltpu.SemaphoreType.DMA((3,)), pltpu.SemaphoreType.DMA((3,)),
            pltpu.SemaphoreType.DMA((3,)), pltpu.SemaphoreType.DMA((3,))]


def _gather_and_ada(c, w_ada, b_ada4, w_in_sh):
    def body(c_ref, w_ref, b_ref, win_ref, call_ref, ada_ref, win_all,
             cslab, sbuf, rbuf, cs_sem, cr_sem, as_sem, ar_sem, *gather_sems):
        x, y, cc = lax.axis_index("x"), lax.axis_index("y"), lax.axis_index("c")
        me = 4 * x + 2 * y + cc
        chip = 2 * x + y
        lane_half = lambda which: (slice(None), pl.ds(pl.multiple_of(which * (D // 2), D // 2), D // 2))
        start, forward, finish = _gather_stages((x, y, cc, chip, (x, y, 1 - cc)), win_ref, win_all, lane_half,
                                                *gather_sems)
        start()

        cslab[...] = jnp.broadcast_to(c_ref[...], (8, D))
        call_ref[me] = cslab[...]
        gathers = []
        for k, (dx, dy, dc) in enumerate(REL7):
            cp = pltpu.make_async_remote_copy(
                src_ref=cslab, dst_ref=call_ref.at[me], send_sem=cs_sem.at[k], recv_sem=cr_sem.at[k],
                device_id=(_flip(x, dx), _flip(y, dy), _flip(cc, dc)), device_id_type=MESH)
            cp.start()
            gathers.append(cp)
        for cp in gathers:
            cp.wait()
        slab_row = lax.broadcasted_iota(jnp.int32, (8, 1), 0)
        mat = jnp.zeros((8, D), F32)
        for r in range(8):
            mat = jnp.where(slab_row == r, call_ref[r], mat)
        act = (mat * _sigmoid(mat)).astype(BF16)
        part = _dot(act, w_ref[...].astype(BF16))
        sends = []
        for k, (dx, dy) in enumerate(REL3):
            px, py = _flip(x, dx), _flip(y, dy)
            r = 4 * px + 2 * py + cc
            piece = _colsum(jnp.where(slab_row == r, part, 0.0))
            sbuf[k] = jnp.broadcast_to(piece, (8, SHARD_ADA))
            cp = pltpu.make_async_remote_copy(
                src_ref=sbuf.at[k], dst_ref=rbuf.at[k], send_sem=as_sem.at[k], recv_sem=ar_sem.at[k],
                device_id=(px, py, cc), device_id_type=MESH)
            cp.start()
            sends.append(cp)
        own_piece = _colsum(jnp.where(slab_row == me, part, 0.0))
        ada_ref[chip] = jnp.broadcast_to(own_piece, (8, SHARD_ADA)) + b_ref[chip]
        for k, (dx, dy) in enumerate(REL3):
            sends[k].wait()
            a = 2 * _flip(x, dx) + _flip(y, dy)
            ada_ref[a] = rbuf[k] + b_ref[a]

        forward()
        finish()

    vm = pl.BlockSpec(memory_space=pltpu.VMEM)
    return pl.pallas_call(
        body, name="gather_and_ada",
        out_shape=(jax.ShapeDtypeStruct((8, 8, D), F32), jax.ShapeDtypeStruct((4, 8, SHARD_ADA), F32),
                   jax.ShapeDtypeStruct((N_CHIPS, SHARD_IN, D), BF16)),
        in_specs=[vm] * 4, out_specs=(vm,) * 3,
        scratch_shapes=[pltpu.VMEM((8, D), F32), pltpu.VMEM((3, 8, SHARD_ADA), F32),
                        pltpu.VMEM((3, 8, SHARD_ADA), F32),
                        pltpu.SemaphoreType.DMA((7,)), pltpu.SemaphoreType.DMA((7,)),
                        pltpu.SemaphoreType.DMA((3,)), pltpu.SemaphoreType.DMA((3,))] + _gather_scratch(),
        compiler_params=_params(),
    )(c, w_ada, b_ada4, w_in_sh)


def _scatter_stages(pos, g_ref, sc_ref, out_ref, sib_buf, send_buf, ici_buf, sem1, sem2s, sem2r, sem3):
    x, y, cc, chip, sib = pos
    RH = g_ref.shape[1] // 2
    mine = pl.ds(pl.multiple_of(cc * RH, RH), RH)
    theirs = pl.ds(pl.multiple_of((1 - cc) * RH, RH), RH)
    cp1 = pltpu.make_async_remote_copy(
        src_ref=g_ref.at[:, theirs, :], dst_ref=sib_buf, send_sem=sem1.at[0], recv_sem=sem1.at[1],
        device_id=sib, device_id_type=MESH)
    sends = []
    for k, (dx, dy) in enumerate(REL3):
        px, py = _flip(x, dx), _flip(y, dy)
        sends.append(pltpu.make_async_remote_copy(
            src_ref=send_buf.at[2 * px + py], dst_ref=ici_buf.at[chip],
            send_sem=sem2s.at[k], recv_sem=sem2r.at[k], device_id=(px, py, cc), device_id_type=MESH))
    cp3 = pltpu.make_async_remote_copy(
        src_ref=out_ref.at[mine, :], dst_ref=out_ref.at[mine, :], send_sem=sem3.at[0], recv_sem=sem3.at[1],
        device_id=sib, device_id_type=MESH)

    def finish1():
        cp1.wait()
        for a in range(N_CHIPS):
            both = g_ref[a, mine, :] + sib_buf[a]
            sib_buf[a] = both
            send_buf[a] = both.astype(BF16)

    def start2():
        for cp in sends:
            cp.start()
        ici_buf[chip] = send_buf[chip]

    def finish2():
        for cp in sends:
            cp.wait()
        own = sib_buf[chip]
        parts = [jnp.where(chip == a, own, ici_buf[a].astype(F32)) for a in range(N_CHIPS)]
        out_ref[mine, :] = ((parts[0] + parts[1]) + (parts[2] + parts[3])) * sc_ref[chip]

    return [(cp1.start, finish1), (start2, finish2), (cp3.start, cp3.wait)]


def _all_reduce_stages(pos, g_ref, out_ref, sib_buf, ici_buf, sem1, sem2s, sem2r, sem3):
    x, y, cc, chip, sib = pos
    RH = g_ref.shape[0] // 2
    mine = pl.ds(pl.multiple_of(cc * RH, 8), RH)
    theirs = pl.ds(pl.multiple_of((1 - cc) * RH, 8), RH)
    cp1 = pltpu.make_async_remote_copy(
        src_ref=g_ref.at[theirs, :], dst_ref=sib_buf, send_sem=sem1.at[0], recv_sem=sem1.at[1],
        device_id=sib, device_id_type=MESH)
    sends = []
    for k, (dx, dy) in enumerate(REL3):
        px, py = _flip(x, dx), _flip(y, dy)
        sends.append(pltpu.make_async_remote_copy(
            src_ref=sib_buf, dst_ref=ici_buf.at[chip],
            send_sem=sem2s.at[k], recv_sem=sem2r.at[k], device_id=(px, py, cc), device_id_type=MESH))
    cp3 = pltpu.make_async_remote_copy(
        src_ref=out_ref.at[mine, :], dst_ref=out_ref.at[mine, :], send_sem=sem3.at[0], recv_sem=sem3.at[1],
        device_id=sib, device_id_type=MESH)

    def finish1():
        cp1.wait()
        sib_buf[...] = g_ref[mine, :] + sib_buf[...]

    def start2():
        for cp in sends:
            cp.start()
        ici_buf[chip] = sib_buf[...]

    def finish2():
        for cp in sends:
            cp.wait()
        out_ref[mine, :] = (ici_buf[0] + ici_buf[1]) + (ici_buf[2] + ici_buf[3])

    return [(cp1.start, finish1), (start2, finish2), (cp3.start, cp3.wait)]


def _stage_sems():
    return [pltpu.SemaphoreType.DMA((2,)), pltpu.SemaphoreType.DMA((3,)),
            pltpu.SemaphoreType.DMA((3,)), pltpu.SemaphoreType.DMA((2,))]


def _scatter_scratch(r, c):
    return [pltpu.VMEM((N_CHIPS, r // 2, c), F32), pltpu.VMEM((N_CHIPS, r // 2, c), BF16),
            pltpu.VMEM((N_CHIPS, r // 2, c), BF16)] + _stage_sems()


def _reduce_all(g4_in, sc_in, small, dada):
    R = small.shape[0]
    W = dada.shape[1]
    n_in = len(_scatter_scratch(*g4_in.shape[1:]))

    def body(gin_ref, scin_ref, sm_ref, d_ref, oin_ref, osm_ref, dall_ref, *scratch):
        x, y, cc = lax.axis_index("x"), lax.axis_index("y"), lax.axis_index("c")
        me = 4 * x + 2 * y + cc
        pos = (x, y, cc, 2 * x + y, (x, y, 1 - cc))
        dslab, ds_sem, dr_sem = scratch[0:3]
        in_bufs, sm_bufs = scratch[3:3 + n_in], scratch[3 + n_in:]
        dslab[...] = jnp.broadcast_to(d_ref[...], (8, W))
        dall_ref[me] = dslab[...]
        gathers = []
        for k, (dx, dy, dc) in enumerate(REL7):
            cp = pltpu.make_async_remote_copy(
                src_ref=dslab, dst_ref=dall_ref.at[me], send_sem=ds_sem.at[k], recv_sem=dr_sem.at[k],
                device_id=(_flip(x, dx), _flip(y, dy), _flip(cc, dc)), device_id_type=MESH)
            cp.start()
            gathers.append(cp)
        plans = [_scatter_stages(pos, gin_ref, scin_ref, oin_ref, *in_bufs),
                 _all_reduce_stages(pos, sm_ref, osm_ref, *sm_bufs)]
        for stage in range(3):
            for plan in plans:
                plan[stage][0]()
            for plan in plans:
                plan[stage][1]()
        for cp in gathers:
            cp.wait()

    scratch = [pltpu.VMEM((8, W), F32), pltpu.SemaphoreType.DMA((7,)), pltpu.SemaphoreType.DMA((7,))]
    scratch += _scatter_scratch(*g4_in.shape[1:])
    scratch += [pltpu.VMEM((R // 2, 128), F32), pltpu.VMEM((N_CHIPS, R // 2, 128), F32)] + _stage_sems()
    vm = pl.BlockSpec(memory_space=pltpu.VMEM)
    return pl.pallas_call(
        body, name="reduce_all",
        out_shape=(jax.ShapeDtypeStruct(g4_in.shape[1:], F32),
                   jax.ShapeDtypeStruct((R, 128), F32), jax.ShapeDtypeStruct((8, 8, W), F32)),
        in_specs=[vm] * 4, out_specs=(vm,) * 3,
        scratch_shapes=scratch,
        compiler_params=_params(),
    )(g4_in, sc_in, small, dada)


def _in_proj(x, shift, scale, wt_pad, b_pad, w_out_sh):
    S = x.shape[0]
    tm = min(TM_PROJ, S)
    n_steps = S // tm
    assert n_steps >= 3

    def body(x_ref, sh_ref, sc_ref, w_ref, b_ref, wo_ref, u_ref, qkv_ref, f_ref, p_ref, g_ref, wo_all,
             wo_buf, *gather_sems):
        i = pl.program_id(0)
        xx, yy, cc = lax.axis_index("x"), lax.axis_index("y"), lax.axis_index("c")
        row_half = lambda which: (pl.ds(pl.multiple_of(which * (SHARD_OUT // 2), SHARD_OUT // 2), SHARD_OUT // 2),
                                  slice(None))
        start, forward, finish = _gather_stages((xx, yy, cc, 2 * xx + yy, (xx, yy, 1 - cc)), wo_ref, wo_buf,
                                                row_half, *gather_sems)
        pl.when(i == 0)(start)
        pl.when(i == n_steps // 2)(forward)

        @pl.when(i == n_steps - 1)
        def _():
            finish()
            wo_all[...] = wo_buf[...]

        u = (x_ref[...] * (1.0 + sc_ref[...]) + sh_ref[...]).astype(BF16)
        u_ref[...] = u
        qkv_ref[...] = (_dot_nt(u, w_ref[O_QKV:O_F, :]) + b_ref[:, O_QKV:O_F]).astype(BF16)
        f_ref[...] = _dot_nt(u, w_ref[O_F:O_P, :]) + b_ref[:, O_F:O_P]
        p_ref[...] = _dot_nt(u, w_ref[O_P:O_G, :]) + b_ref[:, O_P:O_G]
        g_ref[...] = _dot_nt(u, w_ref[O_G:D_PAD, :]) + b_ref[:, O_G:D_PAD]

    row = lambda w: pl.BlockSpec((tm, w), lambda i: (i, 0))
    full = lambda a: pl.BlockSpec(a.shape, lambda i: (0, 0))
    vm = pl.BlockSpec(memory_space=pltpu.VMEM)
    return pl.pallas_call(
        body, name="in_proj", grid=(n_steps,),
        out_shape=(jax.ShapeDtypeStruct((S, D), BF16), jax.ShapeDtypeStruct((S, 3 * D_ATT), BF16),
                   jax.ShapeDtypeStruct((S, 128), F32), jax.ShapeDtypeStruct((S, D_POOL), F32),
                   jax.ShapeDtypeStruct((S, D), F32), jax.ShapeDtypeStruct((N_CHIPS,) + w_out_sh.shape, BF16)),
        in_specs=[row(D), full(shift), full(scale), full(wt_pad), full(b_pad), vm],
        out_specs=(row(D), row(3 * D_ATT), row(128), row(D_POOL), row(D), vm),
        scratch_shapes=[pltpu.VMEM((N_CHIPS,) + w_out_sh.shape, BF16)] + _gather_scratch(),
        compiler_params=_params(dimension_semantics=("arbitrary",)),
    )(x, shift, scale, wt_pad, b_pad, w_out_sh)


def _forget_cumsum(f):
    S = f.shape[0]
    tm = min(T_ATT, S)

    def body(f_ref, out_ref, carry):
        @pl.when(pl.program_id(0) == 0)
        def _():
            carry[...] = jnp.zeros_like(carry)
        v = f_ref[...]
        logf = jnp.minimum(v, 0.0) - jnp.log(1.0 + jnp.exp(-jnp.abs(v)))
        r = lax.broadcasted_iota(jnp.int32, (tm, tm), 0)
        c = lax.broadcasted_iota(jnp.int32, (tm, tm), 1)
        tri = (r <= c).astype(F32)
        rows8 = logf.T[0:8, :]
        cum8 = jnp.dot(rows8, tri, preferred_element_type=F32, precision=lax.Precision.HIGHEST) + carry[...]
        out_ref[...] = jnp.concatenate([cum8, jnp.zeros((128 - 8, tm), F32)], axis=0).T
        last = lax.broadcasted_iota(jnp.int32, (1, tm), 1) == tm - 1
        carry[...] = jnp.sum(jnp.where(last, cum8, 0.0), axis=1, keepdims=True)

    return pl.pallas_call(
        body, name="forget_cumsum", grid=(S // tm,),
        out_shape=jax.ShapeDtypeStruct((S, 128), F32),
        in_specs=[pl.BlockSpec((tm, 128), lambda i: (i, 0))],
        out_specs=pl.BlockSpec((tm, 128), lambda i: (i, 0)),
        scratch_shapes=[pltpu.VMEM((8, 1), F32)],
        compiler_params=_params(dimension_semantics=("arbitrary",)),
    )(f)


def _split3(v):
    hi = v.astype(BF16)
    rest = v - hi.astype(F32)
    mid = rest.astype(BF16)
    lo = (rest - mid.astype(F32)).astype(BF16)
    return hi, mid, lo


def _attention_fwd(qkv, big_f):
    S = qkv.shape[0]
    T = min(T_ATT, S)
    n_t = S // T

    def body(q_ref, k_ref, v_ref, f_ref, o_ref, lse_ref, kaug_sc, vt_sc, m_sc, l_sc, acc_sc):
        hp = pl.program_id(0)
        i = pl.program_id(1)
        lane = lax.broadcasted_iota(jnp.int32, (1, 128), 1)
        sub = lax.broadcasted_iota(jnp.int32, (128, 1), 0)
        head_sel = (lane < HEAD_DIM, lane >= HEAD_DIM)
        head_sel_t = (sub < HEAD_DIM, sub >= HEAD_DIM)
        spare = (HEAD_DIM, 0)
        zero = jnp.zeros((), BF16)

        @pl.when(i == 0)
        def _():
            def prep(jt, carry):
                rows = pl.ds(pl.multiple_of(jt * T, T), T)
                k = k_ref[rows, :]
                ft = f_ref[rows, :]
                vt = v_ref[rows, :].astype(F32).T
                for h in range(2):
                    fh = jnp.sum(jnp.where(lane == 2 * hp + h, ft, 0.0), axis=1, keepdims=True)
                    hi, mid, lo = _split3(-fh)
                    b = spare[h]
                    bias = jnp.where(lane == b, hi, jnp.where(lane == b + 1, mid, jnp.where(lane == b + 2, lo, zero)))
                    kaug_sc[h, rows, :] = jnp.where(head_sel[h], k, bias)
                    vt_sc[h, jt] = jnp.where(head_sel_t[h], vt, 0.0).astype(BF16)
                return carry

            lax.fori_loop(0, n_t, prep, 0)

        q = q_ref[...]
        q_heads = []
        for h in range(2):
            ones = jnp.where((lane >= spare[h]) & (lane < spare[h] + 3), jnp.ones((), BF16), zero)
            q_heads.append(jnp.where(head_sel[h], q, ones))
        m_sc[...] = jnp.full((8, T), NEG, F32)
        l_sc[...] = jnp.zeros((8, T), F32)
        acc_sc[...] = jnp.zeros((128, T), F32)

        def update(j, k_lo, n_k, q_lo, masked):
            rows = pl.ds(pl.multiple_of(j * T + k_lo, n_k), n_k)
            n_q = T - q_lo
            alphas, pvs = [], []
            for h in range(2):
                s_t = _dot_nt(kaug_sc[h, rows, :], q_heads[h][q_lo:, :])
                if masked:
                    rr = lax.broadcasted_iota(jnp.int32, (n_k, n_q), 0) + k_lo
                    cc = lax.broadcasted_iota(jnp.int32, (n_k, n_q), 1) + q_lo
                    s_t = jnp.where(rr <= cc, s_t, NEG)
                m_prev = m_sc[h:h + 1, q_lo:]
                m_new = jnp.maximum(m_prev, jnp.max(s_t, axis=0, keepdims=True))
                alpha = jnp.exp(m_prev - m_new)
                p_t = jnp.exp(s_t - m_new)
                l_sc[h:h + 1, q_lo:] = alpha * l_sc[h:h + 1, q_lo:] + jnp.sum(p_t, axis=0, keepdims=True)
                m_sc[h:h + 1, q_lo:] = m_new
                alphas.append(alpha)
                pvs.append(_dot(vt_sc[h, j, :, k_lo:k_lo + n_k], p_t.astype(BF16)))
            acc_sc[:, q_lo:] = (acc_sc[:, q_lo:] * jnp.where(head_sel_t[0], alphas[0], alphas[1])
                                + (pvs[0] + pvs[1]))

        def two_off_diagonal(jj, carry):
            update(2 * jj, 0, T, 0, False)
            update(2 * jj + 1, 0, T, 0, False)
            return carry

        lax.fori_loop(0, i // 2, two_off_diagonal, 0)

        @pl.when(i % 2 == 1)
        def _():
            update(i - 1, 0, T, 0, False)

        update(i, 0, T, 0, True)
        l = l_sc[...]
        o_ref[...] = (acc_sc[...] / jnp.where(head_sel_t[0], l[0:1, :], l[1:2, :])).T
        is_head = lax.broadcasted_iota(jnp.int32, (8, 1), 0) < 2
        lse_ref[...] = jnp.where(is_head, m_sc[...] + jnp.log(jnp.where(is_head, l, 1.0)), 0.0)

    return pl.pallas_call(
        body, name="attention_fwd", grid=(N_PAIR, n_t),
        out_shape=(jax.ShapeDtypeStruct((S, D_ATT), F32), jax.ShapeDtypeStruct((N_PAIR, n_t, 8, T), F32)),
        in_specs=[pl.BlockSpec((T, 128), lambda hp, i: (i, hp)),
                  pl.BlockSpec((S, 128), lambda hp, i: (0, N_PAIR + hp)),
                  pl.BlockSpec((S, 128), lambda hp, i: (0, 2 * N_PAIR + hp)),
                  pl.BlockSpec((S, 128), lambda hp, i: (0, 0))],
        out_specs=(pl.BlockSpec((T, 128), lambda hp, i: (i, hp)),
                   pl.BlockSpec((None, None, 8, T), lambda hp, i: (hp, i, 0, 0))),
        scratch_shapes=[pltpu.VMEM((2, S, 128), BF16), pltpu.VMEM((2, n_t, 128, T), BF16),
                        pltpu.VMEM((8, T), F32), pltpu.VMEM((8, T), F32), pltpu.VMEM((128, T), F32)],
        compiler_params=_params(dimension_semantics=("arbitrary", "arbitrary")),
    )(qkv, qkv, qkv, big_f)


def _attention_bwd(qkv, datt, att, lse, big_f, gw_out4, sc_out):
    S = qkv.shape[0]
    T = min(T_ATT, S)
    n_t = S // T
    n_steps = N_PAIR * n_t
    marks = (0, n_steps // 8, n_steps // 2, n_steps // 2 + n_steps // 8)

    def body(q_ref, do_ref, o_ref, lse_ref, k_ref, v_ref, fk_ref, gout_ref, scout_ref,
             dq_ref, dk_ref, dv_ref, cs_ref, dfk_ref, dfq_ref, oout_ref, stat_sc, dqt_sc, qaug_sc,
             out_buf, *red_bufs):
        hp = pl.program_id(0)
        j = pl.program_id(1)
        x, y, cc = lax.axis_index("x"), lax.axis_index("y"), lax.axis_index("c")
        plan = _scatter_stages((x, y, cc, 2 * x + y, (x, y, 1 - cc)), gout_ref, scout_ref, out_buf, *red_bufs)
        step = hp * n_t + j
        for n, mark in enumerate(marks):
            @pl.when(step == mark)
            def _(n=n):
                if n > 0:
                    plan[n - 1][1]()
                if n < 3:
                    plan[n][0]()
                else:
                    oout_ref[...] = out_buf[...]

        lane = lax.broadcasted_iota(jnp.int32, (1, 128), 1)
        sub = lax.broadcasted_iota(jnp.int32, (128, 1), 0)
        head_sel = (lane < HEAD_DIM, lane >= HEAD_DIM)
        head_sel_t = (sub < HEAD_DIM, sub >= HEAD_DIM)
        spare = (HEAD_DIM, 0)
        zero = jnp.zeros((), BF16)
        one = jnp.ones((), BF16)

        def bias_lanes(first, pieces):
            hi, mid, lo = pieces
            return lambda rest: jnp.where(lane == first, hi, jnp.where(lane == first + 1, mid,
                                                                        jnp.where(lane == first + 2, lo, rest)))

        @pl.when(j == 0)
        def _():
            dqt_sc[...] = jnp.zeros_like(dqt_sc)
            cs_ref[...] = jnp.zeros_like(cs_ref)
            dfq_ref[...] = jnp.zeros_like(dfq_ref)

            def prep(i, carry):
                rows = pl.ds(pl.multiple_of(i * T, T), T)
                q = q_ref[rows, :]
                do = do_ref[rows, :]
                prod = o_ref[rows, :] * do.astype(F32)
                d_a = jnp.sum(jnp.where(head_sel[0], prod, 0.0), axis=1, keepdims=True)
                d_b = jnp.sum(jnp.where(head_sel[0], 0.0, prod), axis=1, keepdims=True)
                delta_t = jnp.where(head_sel[0], d_a, d_b).T
                stat_sc[i, 0:1, :] = delta_t[0:1, :]
                stat_sc[i, 1:2, :] = delta_t[HEAD_DIM:HEAD_DIM + 1, :]
                lse = lse_ref[i]
                lse_cols = jnp.where(head_sel_t[0], lse[0:1, :], lse[1:2, :]).T
                for h in range(2):
                    neg_lse = -lse_cols[:, h * HEAD_DIM:h * HEAD_DIM + 1]
                    ones = jnp.where((lane >= spare[h]) & (lane < spare[h] + 3), one, zero)
                    qaug_sc[h, rows, :] = jnp.where(head_sel[h], q, bias_lanes(spare[h] + 3, _split3(neg_lse))(ones))
                return carry

            lax.fori_loop(0, n_t, prep, 0)

        k = k_ref[...]
        v = v_ref[...]
        fk = fk_ref[...]
        kt = k.astype(F32).T
        heads = []
        for h in range(2):
            fkh = jnp.sum(jnp.where(lane == 2 * hp + h, fk, 0.0), axis=1, keepdims=True)
            ones = jnp.where((lane >= spare[h] + 3) & (lane < spare[h] + 6), one, zero)
            kaug = jnp.where(head_sel[h], k, bias_lanes(spare[h], _split3(-fkh))(ones))
            heads.append((kaug, jnp.where(head_sel[h], v, zero), jnp.where(head_sel_t[h], kt, 0.0).astype(BF16)))

        def block(i, k_lo, n_k, q_lo, masked):
            n_q = T - q_lo
            rows = pl.ds(pl.multiple_of(i * T + q_lo, n_q), n_q)
            q = q_ref[rows, :]
            do = do_ref[rows, :]
            stat = stat_sc[i]
            dk = jnp.zeros((n_k, 128), F32)
            dv = jnp.zeros((n_k, 128), F32)
            dqt = jnp.zeros((128, n_q), F32)
            dfs = []
            for h in range(2):
                kaug, vh, kth = heads[h]
                arg = _dot_nt(kaug[k_lo:k_lo + n_k, :], qaug_sc[h, rows, :])
                if masked:
                    rr = lax.broadcasted_iota(jnp.int32, (n_k, n_q), 0) + k_lo
                    cc = lax.broadcasted_iota(jnp.int32, (n_k, n_q), 1) + q_lo
                    arg = jnp.where(rr <= cc, arg, NEG)
                p_t = jnp.exp(arg)
                ds_t = p_t * (_dot_nt(vh[k_lo:k_lo + n_k, :], do) - stat[h:h + 1, q_lo:])
                ds_bf = ds_t.astype(BF16)
                dv = dv + _dot(p_t.astype(BF16), jnp.where(head_sel[h], do, zero))
                dk = dk + _dot(ds_bf, jnp.where(head_sel[h], q, zero))
                dqt = dqt + _dot(kth[:, k_lo:k_lo + n_k], ds_bf)
                dfs.append(jnp.sum(ds_t, axis=1, keepdims=True))
                dfq_ref[i, h:h + 1, q_lo:] += _colsum(ds_t)
            dqt_sc[i, :, q_lo:] += dqt
            return dk, dv, dfs[0], dfs[1]

        def off_diagonal(i, acc):
            return tuple(a + b for a, b in zip(acc, block(i, 0, T, 0, False)))

        half = T // 2
        early = block(j, 0, half, 0, True)
        late = block(j, half, half, half, True)
        acc1 = tuple(jnp.concatenate([a, b], axis=0) for a, b in zip(early, late))
        dk_acc, dv_acc, dfa, dfb = lax.fori_loop(j + 1, n_t, off_diagonal, acc1)
        dk_ref[...] = dk_acc.astype(BF16)
        dv_ref[...] = dv_acc.astype(BF16)
        dfk_ref[...] = -jnp.where(lane == 0, dfa, jnp.where(lane == 1, dfb, 0.0))
        cs_ref[:, 128:256] = cs_ref[:, 128:256] + _colsum(dk_acc)
        cs_ref[:, 256:384] = cs_ref[:, 256:384] + _colsum(dv_acc)

        @pl.when(j == n_t - 1)
        def _():
            def finish(i, tot):
                dq = dqt_sc[i].T
                dq_ref[pl.ds(pl.multiple_of(i * T, T), T), :] = dq.astype(BF16)
                return tot + _colsum(dq)

            cs_ref[:, 0:128] = lax.fori_loop(0, n_t, finish, jnp.zeros((1, 128), F32))

    pair_rows = lambda hp, j: (hp, 0, 0)
    vm = pl.BlockSpec(memory_space=pltpu.VMEM)
    _, r_out, c_out = gw_out4.shape
    return pl.pallas_call(
        body, name="attention_bwd", grid=(N_PAIR, n_t),
        out_shape=(jax.ShapeDtypeStruct((S, D_ATT), BF16), jax.ShapeDtypeStruct((S, D_ATT), BF16),
                   jax.ShapeDtypeStruct((S, D_ATT), BF16), jax.ShapeDtypeStruct((N_PAIR, 1, 384), F32),
                   jax.ShapeDtypeStruct((N_PAIR, S, 128), F32),
                   jax.ShapeDtypeStruct((N_PAIR, n_t, 8, T), F32),
                   jax.ShapeDtypeStruct((r_out, c_out), F32)),
        in_specs=[pl.BlockSpec((S, 128), lambda hp, j: (0, hp)),
                  pl.BlockSpec((S, 128), lambda hp, j: (0, hp)),
                  pl.BlockSpec((S, 128), lambda hp, j: (0, hp)),
                  pl.BlockSpec((None, n_t, 8, T), lambda hp, j: (hp, 0, 0, 0)),
                  pl.BlockSpec((T, 128), lambda hp, j: (j, N_PAIR + hp)),
                  pl.BlockSpec((T, 128), lambda hp, j: (j, 2 * N_PAIR + hp)),
                  pl.BlockSpec((T, 128), lambda hp, j: (j, 0)),
                  vm, vm],
        out_specs=(pl.BlockSpec((S, 128), lambda hp, j: (0, hp)),
                   pl.BlockSpec((T, 128), lambda hp, j: (j, hp)),
                   pl.BlockSpec((T, 128), lambda hp, j: (j, hp)),
                   pl.BlockSpec((None, 1, 384), pair_rows),
                   pl.BlockSpec((None, T, 128), lambda hp, j: (hp, j, 0)),
                   pl.BlockSpec((None, n_t, 8, T), lambda hp, j: (hp, 0, 0, 0)),
                   vm),
        scratch_shapes=[pltpu.VMEM((n_t, 8, T), F32), pltpu.VMEM((n_t, 128, T), F32),
                        pltpu.VMEM((2, S, 128), BF16), pltpu.VMEM((r_out, c_out), F32)]
        + _scatter_scratch(r_out, c_out),
        compiler_params=_params(dimension_semantics=("arbitrary", "arbitrary")),
    )(qkv, datt, att, lse, qkv, qkv, big_f, gw_out4, sc_out)


def _window_counts(first_row, n_rows, window):
    t = lax.broadcasted_iota(jnp.int32, (n_rows, 1), 0) + first_row
    return jnp.minimum((t + 1).astype(F32), float(window))


def _middle(x, tgt, att, g, p, gate, w_mix, b_mix, pool_scale, w_out, b_out, ln_g, ln_b):
    S = x.shape[0]
    tm = min(TM_MID, S)
    halo_blocks = tm // POOL_HALO

    def body(x_ref, t_ref, att_ref, g_ref, p_ref, ph_ref, gate_ref, wm_ref, bm_ref, ps_ref, wo_ref, bo_ref,
             lg_ref, lb_ref,
             dh_ref, datt_ref, dg_ref, dpl_ref, gwo_ref, gwm_ref, vec_ref, loss_ref):
        i = pl.program_id(0)

        @pl.when(i == 0)
        def _():
            gwo_ref[...] = jnp.zeros_like(gwo_ref)
            gwm_ref[...] = jnp.zeros_like(gwm_ref)
            vec_ref[...] = jnp.zeros_like(vec_ref)
            loss_ref[...] = jnp.zeros_like(loss_ref)

        pc = p_ref[...]
        halo = jnp.where(i > 0, ph_ref[...], 0.0)
        pe = jnp.concatenate([halo, pc], axis=0)
        pooled_parts = []
        for gi, w in enumerate(POOL_WINDOWS):
            cur = pe[:, gi * POOL_GROUP:(gi + 1) * POOL_GROUP]
            span = 1
            while span < w:
                cur = cur + pltpu.roll(cur, span, 0)
                span *= 2
            wsum = cur[POOL_HALO:, :]
            mean = wsum / _window_counts(i * tm, tm, w)
            pooled_parts.append(mean - pc[:, gi * POOL_GROUP:(gi + 1) * POOL_GROUP])
        pooled_bf =[v.astype(BF16) for v in pooled_parts]
        mixed = jnp.concatenate([_dot(pooled_bf[gi], wm_ref[gi]) for gi in range(4)], axis=1) + bm_ref[...]
        ps = ps_ref[...]
        pool_out = mixed * ps
        gv = g_ref[...]
        sig = _sigmoid(gv)
        silu = gv * sig
        att = att_ref[...]
        y = jnp.concatenate([att * silu[:, :D_ATT], pool_out * silu[:, D_ATT:]], axis=1)
        y_bf = y.astype(BF16)
        wo = wo_ref[...]
        yo = _dot(y_bf, wo) + bo_ref[...]
        gate = gate_ref[...]
        h = ALPHA * x_ref[...] + gate * yo
        mu = jnp.mean(h, axis=1, keepdims=True)
        hc = h - mu
        var = jnp.mean(hc * hc, axis=1, keepdims=True)
        rstd = lax.rsqrt(var + LN_EPS)
        yhat = hc * rstd
        lg = lg_ref[...]
        out = yhat * lg + lb_ref[...]
        err = out - t_ref[...]
        loss_ref[...] += 0.5 * jnp.sum(jnp.mean(err * err, axis=1, keepdims=True), axis=0, keepdims=True)

        dout = err * (1.0 / D)
        g_ln_b = _colsum(dout)
        g_ln_g = _colsum(dout * yhat)
        dyh = dout * lg
        dh = rstd * (dyh - jnp.mean(dyh, axis=1, keepdims=True)
                     - yhat * jnp.mean(dyh * yhat, axis=1, keepdims=True))
        dh_ref[...] = dh
        d_gate = _colsum(dh * yo)
        dyo = gate * dh
        g_b_out = _colsum(dyo)
        dyo_bf = dyo.astype(BF16)
        gwo_ref[...] += _dot_tn(y_bf, dyo_bf)
        dy = _dot_nt(dyo_bf, wo)
        dsilu = sig * (1.0 + gv * (1.0 - sig))
        dy_a = dy[:, :D_ATT]
        dy_p = dy[:, D_ATT:]
        datt_ref[...] = (dy_a * silu[:, :D_ATT]).astype(BF16)
        dpo = dy_p * silu[:, D_ATT:]
        dg = jnp.concatenate([dy_a * att * dsilu[:, :D_ATT], dy_p * pool_out * dsilu[:, D_ATT:]], axis=1)
        dg_ref[...] = dg.astype(BF16)
        g_dg = _colsum(dg)
        g_ps = _colsum(dpo * mixed)
        dmixed = dpo * ps
        g_bm = _colsum(dmixed)
        dmixed_bf = dmixed.astype(BF16)
        dpl = []
        for gi in range(4):
            dm = dmixed_bf[:, gi * POOL_GROUP:(gi + 1) * POOL_GROUP]
            gwm_ref[gi] += _dot_tn(pooled_bf[gi], dm)
            dpl.append(_dot_nt(dm, wm_ref[gi]))
        dpl_ref[...] = jnp.concatenate(dpl, axis=1)
        vec_ref[0:1, :] += g_ln_g
        vec_ref[1:2, :] += g_ln_b
        vec_ref[2:3, :] += d_gate
        vec_ref[3:4, :] += g_b_out
        vec_ref[4:5, :] += g_dg
        vec_ref[5:6, 0:D_POOL] += g_ps
        vec_ref[6:7, 0:D_POOL] += g_bm

    row = lambda w: pl.BlockSpec((tm, w), lambda i: (i, 0))
    full2 = lambda a: pl.BlockSpec(a.shape, lambda i: (0, 0))
    full3 = lambda a: pl.BlockSpec(a.shape, lambda i: (0, 0, 0))
    return pl.pallas_call(
        body, name="middle", grid=(S // tm,),
        out_shape=(jax.ShapeDtypeStruct((S, D), F32),
                   jax.ShapeDtypeStruct((S, D_ATT), BF16),
                   jax.ShapeDtypeStruct((S, D), BF16),
                   jax.ShapeDtypeStruct((S, D_POOL), F32),
                   jax.ShapeDtypeStruct((D, D), F32),
                   jax.ShapeDtypeStruct((4, POOL_GROUP, POOL_GROUP), F32),
                   jax.ShapeDtypeStruct((8, D), F32),
                   jax.ShapeDtypeStruct((1, 1), F32)),
        in_specs=[row(D), row(D), row(D_ATT), row(D), row(D_POOL),
                  pl.BlockSpec((POOL_HALO, D_POOL), lambda i: (jnp.maximum(i * halo_blocks - 1, 0), 0)),
                  full2(gate), full3(w_mix), full2(b_mix), full2(pool_scale), full2(w_out), full2(b_out),
                  full2(ln_g), full2(ln_b)],
        out_specs=(row(D), row(D_ATT), row(D), row(D_POOL),
                   pl.BlockSpec((D, D), lambda i: (0, 0)),
                   pl.BlockSpec((4, POOL_GROUP, POOL_GROUP), lambda i: (0, 0, 0)),
                   pl.BlockSpec((8, D), lambda i: (0, 0)),
                   pl.BlockSpec((1, 1), lambda i: (0, 0))),
        compiler_params=_params(dimension_semantics=("arbitrary",)),
    )(x, tgt, att, g, p, p, gate, w_mix, b_mix, pool_scale, w_out, b_out, ln_g, ln_b)


def _tail(dpl, dfk, dfq, f):
    S = dpl.shape[0]
    tm = min(T_ATT, S)
    n_t = S // tm
    halo_blocks = tm // POOL_HALO
    last_halo = S // POOL_HALO - 1

    def body(d_ref, dn_ref, dfk_ref, dfq_ref, f_ref, dp_ref, df_ref, cs_ref, carry):
        s = pl.program_id(0)
        i = n_t - 1 - s

        @pl.when(s == 0)
        def _():
            carry[...] = jnp.zeros_like(carry)
            cs_ref[...] = jnp.zeros_like(cs_ref)

        dc = d_ref[...]
        nxt = jnp.where(s > 0, dn_ref[...], 0.0)
        de = jnp.concatenate([dc, nxt], axis=0)
        n_e = tm + POOL_HALO
        parts = []
        for gi, w in enumerate(POOL_WINDOWS):
            cur = de[:, gi * POOL_GROUP:(gi + 1) * POOL_GROUP] / _window_counts(i * tm, n_e, w)
            span = 1
            while span < w:
                cur = cur + pltpu.roll(cur, n_e - span, 0)
                span *= 2
            parts.append(cur[:tm, :] - dc[:, gi * POOL_GROUP:(gi + 1) * POOL_GROUP])
        dp = jnp.concatenate(parts, axis=1)
        dp_ref[...] = dp.astype(BF16)
        cs_ref[0:1, :] += _colsum(dp)

        r = lax.broadcasted_iota(jnp.int32, (tm, tm), 0)
        c = lax.broadcasted_iota(jnp.int32, (tm, tm), 1)
        tri = (r >= c).astype(F32)
        k_cols = dfk_ref[0]
        rows8 = dfq_ref[0]
        for hp in range(1, N_PAIR):
            k_cols = k_cols + pltpu.roll(dfk_ref[hp], 2 * hp, 1)
            rows8 = rows8 + pltpu.roll(dfq_ref[hp], 2 * hp, 0)
        rows8 = rows8 + k_cols.T[0:8, :]
        dlogf8 = jnp.dot(rows8, tri, preferred_element_type=F32, precision=lax.Precision.HIGHEST) + carry[...]
        first = lax.broadcasted_iota(jnp.int32, (1, tm), 1) == 0
        carry[...] = jnp.sum(jnp.where(first, dlogf8, 0.0), axis=1, keepdims=True)
        dlogf = jnp.concatenate([dlogf8, jnp.zeros((128 - 8, tm), F32)], axis=0).T
        df = dlogf * _sigmoid(-f_ref[...])
        df_ref[...] = df.astype(BF16)
        cs_ref[1:2, 0:128] += _colsum(df)

    rev = lambda w: pl.BlockSpec((tm, w), lambda s: (n_t - 1 - s, 0))
    return pl.pallas_call(
        body, name="tail", grid=(n_t,),
        out_shape=(jax.ShapeDtypeStruct((S, D_POOL), BF16), jax.ShapeDtypeStruct((S, 128), BF16),
                   jax.ShapeDtypeStruct((8, D_POOL), F32)),
        in_specs=[rev(D_POOL),
                  pl.BlockSpec((POOL_HALO, D_POOL),
                               lambda s: (jnp.minimum((n_t - s) * halo_blocks, last_halo), 0)),
                  pl.BlockSpec((N_PAIR, tm, 128), lambda s: (0, n_t - 1 - s, 0)),
                  pl.BlockSpec((N_PAIR, None, 8, tm), lambda s: (0, n_t - 1 - s, 0, 0)),
                  rev(128)],
        out_specs=(rev(D_POOL), rev(128), pl.BlockSpec((8, D_POOL), lambda s: (0, 0))),
        scratch_shapes=[pltpu.VMEM((8, 1), F32)],
        compiler_params=_params(dimension_semantics=("arbitrary",)),
    )(dpl, dpl, dfk, dfq, f)


PIECES = ((O_QKV, D_ATT), (O_QKV + D_ATT, D_ATT), (O_QKV + 2 * D_ATT, D_ATT), (O_F, 128), (O_P, D_POOL), (O_G, D))


def _grad_w_in(u, pieces):
    S = u.shape[0]
    tm = min(TM_GW, S)
    n_t = S // tm

    def body(u_ref, *rest):
        piece_refs, out_ref, acc, sem = rest[:6], rest[6], rest[7], rest[8]
        i = pl.program_id(0)

        @pl.when(i == 0)
        def _():
            acc[...] = jnp.zeros_like(acc)

        u_t = u_ref[...]
        for (off, w), ref in zip(PIECES, piece_refs):
            acc[:, off:off + w] += _dot_tn(u_t, ref[...])

        @pl.when(i == n_t - 1)
        def _():
            cp = pltpu.make_async_copy(acc, out_ref, sem)
            cp.start()
            cp.wait()

    return pl.pallas_call(
        body, name="grad_w_in", grid=(n_t,),
        out_shape=jax.ShapeDtypeStruct((D, D_PAD), F32),
        in_specs=[pl.BlockSpec((tm, D), lambda i: (i, 0))]
        + [pl.BlockSpec((tm, w), lambda i: (i, 0)) for _, w in PIECES],
        out_specs=pl.BlockSpec(memory_space=pl.ANY),
        scratch_shapes=[pltpu.VMEM((D, D_PAD), F32), pltpu.SemaphoreType.DMA],
        compiler_params=_params(dimension_semantics=("arbitrary",)),
    )(u, *pieces)


def _grad_x(pieces, wt_pad, dh, x, scale):
    S = x.shape[0]
    tm = min(TM_DU, S)

    def body(*refs):
        piece_refs = refs[:6]
        w_ref, dh_ref, x_ref, sc_ref, gx_ref, vec_ref = refs[6:]

        @pl.when(pl.program_id(0) == 0)
        def _():
            vec_ref[...] = jnp.zeros_like(vec_ref)

        du = jnp.zeros((tm, D), F32)
        for (off, w), ref in zip(PIECES, piece_refs):
            du = du + _dot(ref[...], w_ref[off:off + w, :])
        xv = x_ref[...]
        gx_ref[...] = ALPHA * dh_ref[...] + du * (1.0 + sc_ref[...])
        vec_ref[0:1, :] += _colsum(du)
        vec_ref[1:2, :] += _colsum(du * xv)

    row = lambda w: pl.BlockSpec((tm, w), lambda i: (i, 0))
    return pl.pallas_call(
        body, name="grad_x", grid=(S // tm,),
        out_shape=(jax.ShapeDtypeStruct((S, D), F32), jax.ShapeDtypeStruct((8, D), F32)),
        in_specs=[row(w) for _, w in PIECES]
        + [pl.BlockSpec(wt_pad.shape, lambda i: (0, 0)), row(D), row(D), pl.BlockSpec((1, D), lambda i: (0, 0))],
        out_specs=(row(D), pl.BlockSpec((8, D), lambda i: (0, 0))),
        compiler_params=_params(dimension_semantics=("arbitrary",)),
    )(*pieces, wt_pad, dh, x, scale)


def _grad_ada(c_all, dada_all, dada_cols):
    def body(c_ref, dall_ref, dcol_ref, gw_ref, gb_ref):
        rows = lax.broadcasted_iota(jnp.int32, (8, 1), 0)
        cm = jnp.zeros((8, D), F32)
        dm = jnp.zeros((8, 3 * D), F32)
        for r in range(8):
            cm = jnp.where(rows == r, c_ref[r], cm)
            dm = jnp.where(rows == r, dall_ref[r], dm)
        act = cm * _sigmoid(cm)
        pad = jnp.zeros((8, D), F32)
        lhs = jnp.concatenate([act, pad], axis=0).astype(BF16)
        rhs = jnp.concatenate([dcol_ref[...], jnp.zeros((8, SHARD_ADA), F32)], axis=0).astype(BF16)
        gw_ref[...] = _dot_tn(lhs, rhs)
        gb_ref[...] = _colsum(dm)

    vm = pl.BlockSpec(memory_space=pltpu.VMEM)
    return pl.pallas_call(
        body, name="grad_ada",
        out_shape=(jax.ShapeDtypeStruct((D, SHARD_ADA), F32), jax.ShapeDtypeStruct((1, 3 * D), F32)),
        in_specs=[vm, vm, vm], out_specs=(vm, vm),
        compiler_params=_params(),
    )(c_all, dada_all, dada_cols)


def _adamw_math(w, g, m, v):
    m = ADAM_B1 * m + (1.0 - ADAM_B1) * g
    v = ADAM_B2 * v + (1.0 - ADAM_B2) * (g * g)
    m_hat = m / (1.0 - ADAM_B1 ** ADAM_STEP)
    v_hat = v / (1.0 - ADAM_B2 ** ADAM_STEP)
    delta = -ADAM_LR * (m_hat / (jnp.sqrt(v_hat) + ADAM_EPS) + ADAM_WD * w)
    return delta, m, v


def _adamw(groups, n_steps):
    n = len(groups)

    def body(*refs):
        ins, outs = refs[:4 * n], refs[4 * n:]
        for t in range(n):
            w, g, m, v = (r[...] for r in ins[4 * t:4 * t + 4])
            d, m2, v2 = _adamw_math(w, g, m, v)
            outs[3 * t][...] = d
            outs[3 * t + 1][...] = m2
            outs[3 * t + 2][...] = v2

    in_specs, out_specs, out_shape, args = [], [], [], []
    for (w, g, m, v) in groups:
        rest = w.shape[1:]
        spec = pl.BlockSpec((w.shape[0] // n_steps,) + rest, lambda i, nd=len(rest): (i,) + (0,) * nd)
        in_specs += [spec] * 4
        out_specs += [spec] * 3
        out_shape += [jax.ShapeDtypeStruct(w.shape, F32)] * 3
        args += [w, g, m, v]
    return pl.pallas_call(
        body, name="adamw_%d_%d" % (n, n_steps), grid=(n_steps,),
        out_shape=tuple(out_shape), in_specs=in_specs, out_specs=tuple(out_specs),
        compiler_params=_params(dimension_semantics=("arbitrary",)),
    )(*args)


def _pack_small(parts):
    rows = []
    used = 0
    for name, (first, n_rows) in SMALL_SEGS.items():
        if first > used:
            rows.append(jnp.zeros((first - used, 128), F32))
        flat = parts[name].reshape(-1)
        flat = jnp.pad(flat, (0, n_rows * 128 - flat.shape[0]))
        rows.append(flat.reshape(n_rows, 128))
        used = first + n_rows
    rows.append(jnp.zeros((SMALL_ROWS - used, 128), F32))
    return jnp.concatenate(rows, axis=0)


def _unpack_small(buf, name, shape):
    first, n_rows = SMALL_SEGS[name]
    n = int(np.prod(shape))
    return buf[first:first + n_rows].reshape(-1)[:n].reshape(shape)


def _pad_in(v):
    r = v.shape[0]
    z = jnp.zeros((r, O_P - O_F - N_HEADS), v.dtype)
    return jnp.concatenate([v[:, :3 * D_ATT + N_HEADS], z, v[:, 3 * D_ATT + N_HEADS:]], axis=1)


def _unpad_in(v):
    return jnp.concatenate([v[:, :O_F + N_HEADS], v[:, O_P:]], axis=1)


def _shards_in(v):
    gap = O_P - (O_F + N_HEADS)
    parts = []
    for a in range(N_CHIPS):
        lo, hi = a * SHARD_IN, (a + 1) * SHARD_IN
        cut = O_F + N_HEADS
        if hi <= cut:
            parts.append(v[:, lo:hi])
        elif lo >= cut:
            parts.append(v[:, lo + gap:hi + gap])
        else:
            parts.append(jnp.concatenate([v[:, lo:cut], v[:, cut + gap:hi + gap]], axis=1))
    return jnp.stack(parts, axis=0)


def kernel(x, c, w_ada, b_ada, w_in, b_in, w_pool_mix, b_pool_mix, pool_scale, w_out, b_out, ln_g, ln_b, loss_target, m_w_ada, m_b_ada, m_w_in, m_b_in, m_w_pool_mix, m_b_pool_mix, m_pool_scale, m_w_out, m_b_out, m_ln_g, m_ln_b, v_w_ada, v_b_ada, v_w_in, v_b_in, v_w_pool_mix, v_b_pool_mix, v_pool_scale, v_w_out, v_b_out, v_ln_g, v_ln_b):
    S = x.shape[1]
    T = min(T_ATT, S)
    n_t = S // T
    chip = 2 * lax.axis_index("x") + lax.axis_index("y")
    x2 = x[0]
    tgt = loss_target[0]
    q_scale = jnp.concatenate([jnp.full((1, D_ATT), Q_SCALE, F32), jnp.ones((1, D_PAD - D_ATT), F32)], axis=1)

    to_cols = lambda a: jnp.transpose(a, (2, 0, 1))
    from_cols = lambda a: jnp.transpose(a, (1, 2, 0))
    c_all, ada4, wt_in_all = _gather_and_ada(
        c, w_ada[0], b_ada.reshape(4, 1, SHARD_ADA), to_cols(w_in).reshape(SHARD_IN, D).astype(BF16))
    ada = ada4[:, 0, :].reshape(1, 3 * D)
    shift, scale, gate = ada[:, :D], ada[:, D:2 * D], ada[:, 2 * D:]
    wt_full = wt_in_all.reshape(D_IN, D)
    n_real = 3 * D_ATT + N_HEADS
    wt_pad = jnp.concatenate([wt_full[:D_ATT] * jnp.asarray(Q_SCALE, BF16), wt_full[D_ATT:n_real],
                              jnp.zeros((O_P - n_real, D), BF16), wt_full[n_real:]], axis=0)
    b_pad = _pad_in(b_in) * q_scale
    w_mix_bf = w_pool_mix[0].astype(BF16)

    u, qkv, f, p, g, w_out_all = _in_proj(x2, shift, scale, wt_pad, b_pad, w_out[0].astype(BF16))
    w_out_full = w_out_all.reshape(D, D)
    big_f = _forget_cumsum(f)
    att, lse = _attention_fwd(qkv, big_f)

    dh, datt, dg, dpl, gw_out, gw_mix, vec, loss_part = _middle(
        x2, tgt, att, g, p, gate, w_mix_bf, b_pool_mix.reshape(1, D_POOL), pool_scale, w_out_full, b_out, ln_g, ln_b)
    dq, dk, dv, cs_att, dfk, dfq, g_w_out = _attention_bwd(
        qkv, datt, att, lse, big_f, gw_out.reshape(N_CHIPS, SHARD_OUT, D), jnp.ones((N_CHIPS, 1, D), F32))
    dp, df, cs_tail = _tail(dpl, dfk, dfq, f)
    pieces = (dq, dk, dv, df, dp, dg)
    gw_pad = _grad_w_in(u, pieces)
    grad_x, vec_x = _grad_x(pieces, wt_pad, dh, x2, scale)

    cs_qkv = jnp.transpose(cs_att.reshape(N_PAIR, 3, 128), (1, 0, 2)).reshape(1, 3 * D_ATT)
    gb_pad = jnp.concatenate([cs_qkv, cs_tail[1:2, 0:128], cs_tail[0:1, :], vec[4:5, :]], axis=1) * q_scale
    dada = jnp.concatenate([vec_x[0:1, :], vec_x[1:2, :], vec[2:3, :]], axis=1)
    small = _pack_small({
        "b_in": gb_pad, "w_pool_mix": gw_mix, "b_pool_mix": vec[6:7, :D_POOL], "pool_scale": vec[5:6, :D_POOL],
        "b_out": vec[3:4, :], "ln_g": vec[0:1, :], "ln_b": vec[1:2, :], "loss": loss_part,
        "b_ada": jnp.zeros((1, 3 * D), F32)})

    g_w_in, small_sum, dada_all = _reduce_all(
        _shards_in(gw_pad), _shards_in(q_scale), small[:SMALL_REDUCED_ROWS], dada)
    dada_cols = lax.dynamic_slice(dada_all[:, 0, :], (0, chip * SHARD_ADA), (8, SHARD_ADA))
    g_w_ada, g_b_ada = _grad_ada(c_all, dada_all, dada_cols)
    loss = _unpack_small(small_sum, "loss", (1,))[0]

    grads_small = jnp.concatenate([small_sum, g_b_ada.reshape(24, 128)], axis=0)
    no_param = jnp.zeros((1,), F32)
    small_w = {"b_in": _pad_in(b_in), "w_pool_mix": w_pool_mix, "b_pool_mix": b_pool_mix, "pool_scale": pool_scale,
               "b_out": b_out, "ln_g": ln_g, "ln_b": ln_b, "loss": no_param, "b_ada": b_ada}
    small_m = {"b_in": _pad_in(m_b_in), "w_pool_mix": m_w_pool_mix, "b_pool_mix": m_b_pool_mix,
               "pool_scale": m_pool_scale, "b_out": m_b_out, "ln_g": m_ln_g, "ln_b": m_ln_b, "loss": no_param,
               "b_ada": m_b_ada}
    small_v = {"b_in": _pad_in(v_b_in), "w_pool_mix": v_w_pool_mix, "b_pool_mix": v_b_pool_mix,
               "pool_scale": v_pool_scale, "b_out": v_b_out, "ln_g": v_ln_g, "ln_b": v_ln_b, "loss": no_param,
               "b_ada": v_b_ada}
    big = _adamw([(w_ada[0], g_w_ada, m_w_ada[0], v_w_ada[0]),
                  (w_out[0], g_w_out, m_w_out[0], v_w_out[0])], 8)
    g_w_in_cols = to_cols(g_w_in[None])
    big_in = _adamw([(to_cols(w_in), g_w_in_cols, to_cols(m_w_in), to_cols(v_w_in))], 14)
    sm = _adamw([(_pack_small(small_w), grads_small, _pack_small(small_m), _pack_small(small_v))], 1)

    names = ["w_ada", "b_ada", "w_in", "b_in", "w_pool_mix", "b_pool_mix", "pool_scale", "w_out", "b_out",
             "ln_g", "ln_b"]
    shapes = {"b_ada": (1, 3 * D), "b_in": (1, D_PAD), "w_pool_mix": (1, 4, POOL_GROUP, POOL_GROUP),
              "b_pool_mix": (1, 4, POOL_GROUP), "pool_scale": (1, D_POOL), "b_out": (1, D), "ln_g": (1, D),
              "ln_b": (1, D)}
    big_idx = {"w_ada": 0, "w_out": 1}

    def leaf(kind, name):
        if name == "w_in":
            return from_cols(g_w_in_cols if kind == 0 else big_in[kind - 1])
        if name in big_idx:
            if kind == 0:
                return (g_w_ada, g_w_out)[big_idx[name]][None]
            return big[3 * big_idx[name] + kind - 1][None]
        buf = grads_small if kind == 0 else sm[kind - 1]
        val = _unpack_small(buf, name, shapes[name])
        if name == "b_in":
            val = _unpad_in(val)
        return val

    outs = [loss, grad_x[None]]
    for kind in range(4):
        outs += [leaf(kind, n) for n in names]
    return tuple(outs)
```

```python
import functools

import numpy as np
import jax
import jax.numpy as jnp
from jax import lax
from jax.experimental import pallas as pl
from jax.experimental.pallas import tpu as pltpu

F32 = jnp.float32
BF16 = jnp.bfloat16
MESH = pl.DeviceIdType.MESH

D = 1024
D_ATT = 512
D_POOL = 512
N_HEADS = 8
HEAD_DIM = 64
N_PAIR = N_HEADS // 2
POOL_WINDOWS = (2, 4, 8, 16)
POOL_GROUP = 128
POOL_HALO = 16
LN_EPS = 1e-5
ALPHA = 2.0 ** 0.25
D_IN = 3 * D_ATT + N_HEADS + D_POOL + D_ATT + D_POOL
N_CHIPS = 4
SHARD_IN = D_IN // N_CHIPS
SHARD_ADA = 3 * D // N_CHIPS
SHARD_OUT = D // N_CHIPS

O_QKV, O_F, O_P, O_G, D_PAD = 0, 1536, 1664, 2176, 3200
Q_SCALE = HEAD_DIM ** -0.5

ADAM_LR, ADAM_B1, ADAM_B2, ADAM_EPS, ADAM_WD, ADAM_STEP = 0.001, 0.9, 0.999, 1e-08, 0.01, 10

NEG = -1e30

VMEM_LIMIT = 56 * 1024 * 1024

TM_PROJ = 512
T_ATT = 512
TM_MID = 256
TM_GW = 1024
TM_DU = 512

REL7 = [(0, 0, 1), (0, 1, 0), (0, 1, 1), (1, 0, 0), (1, 0, 1), (1, 1, 0), (1, 1, 1)]
REL3 = [(0, 1), (1, 0), (1, 1)]

SMALL_SEGS = {}
_row = 0
for _name, _n in (("b_in", 3200), ("w_pool_mix", 65536), ("b_pool_mix", 512), ("pool_scale", 512),
                  ("b_out", 1024), ("ln_g", 1024), ("ln_b", 1024), ("loss", 1)):
    _rows = -(-_n // 1024) * 8
    SMALL_SEGS[_name] = (_row, _rows)
    _row += _rows
SMALL_REDUCED_ROWS = -(-_row // 16) * 16
SMALL_SEGS["b_ada"] = (SMALL_REDUCED_ROWS, 24)
SMALL_ROWS = SMALL_REDUCED_ROWS + 24


def _params(**kw):
    return pltpu.CompilerParams(vmem_limit_bytes=VMEM_LIMIT, **kw)


def _flip(v, d):
    return v if d == 0 else 1 - v


def _dot(a, b):
    return jnp.dot(a, b, preferred_element_type=F32)


def _dot_nt(a, b):
    return lax.dot_general(a, b, (((1,), (1,)), ((), ())), preferred_element_type=F32)


def _dot_tn(a, b):
    return lax.dot_general(a, b, (((0,), (0,)), ((), ())), preferred_element_type=F32)


def _sigmoid(v):
    return 1.0 / (1.0 + jnp.exp(-v))


def _colsum(v):
    return jnp.sum(v, axis=0, keepdims=True)


def _gather_stages(pos, src_ref, dst_ref, half, own_sem, s_sem, r_sem, fs_sem, fr_sem):
    x, y, cc, chip, sib = pos
    own = pltpu.make_async_copy(src_ref, dst_ref.at[chip], own_sem)
    first, landed, others = [], [], []
    for k, (dx, dy) in enumerate(REL3):
        px, py = _flip(x, dx), _flip(y, dy)
        first.append(pltpu.make_async_remote_copy(
            src_ref=src_ref.at[half(cc)], dst_ref=dst_ref.at[(chip,) + half(cc)],
            send_sem=s_sem.at[k], recv_sem=r_sem.at[k], device_id=(px, py, cc), device_id_type=MESH))
        landed.append(dst_ref.at[(2 * px + py,) + half(cc)])
        others.append(dst_ref.at[(2 * px + py,) + half(1 - cc)])
    passed = [pltpu.make_async_remote_copy(src_ref=landed[k], dst_ref=landed[k], send_sem=fs_sem.at[k],
                                           recv_sem=fr_sem.at[k], device_id=sib, device_id_type=MESH)
              for k in range(3)]

    def start():
        own.start()
        for cp in first:
            cp.start()

    def forward():
        for k in range(3):
            pltpu.make_async_remote_copy(src_ref=landed[k], dst_ref=landed[k], send_sem=s_sem.at[k],
                                         recv_sem=r_sem.at[k], device_id=sib, device_id_type=MESH).wait_recv()
            passed[k].start()

    def finish():
        for k in range(3):
            pltpu.make_async_remote_copy(src_ref=others[k], dst_ref=others[k], send_sem=fs_sem.at[k],
                                         recv_sem=fr_sem.at[k], device_id=sib, device_id_type=MESH).wait_recv()
        for cp in first + passed:
            cp.wait_send()
        own.wait()

    return start, forward, finish


def _gather_scratch():
    return [pltpu.SemaphoreType.DMA, pltpu.SemaphoreType.DMA((3,)), pltpu.SemaphoreType.DMA((3,)),
            pltpu.SemaphoreType.DMA((3,)), pltpu.SemaphoreType.DMA((3,))]


def _gather_and_ada(c, w_ada, b_ada4, w_in_sh):
    def body(c_ref, w_ref, b_ref, win_ref, call_ref, ada_ref, win_all,
             cslab, sbuf, rbuf, cs_sem, cr_sem, as_sem, ar_sem, *gather_sems):
        x, y, cc = lax.axis_index("x"), lax.axis_index("y"), lax.axis_index("c")
        me = 4 * x + 2 * y + cc
        chip = 2 * x + y
        lane_half = lambda which: (slice(None), pl.ds(pl.multiple_of(which * (D // 2), D // 2), D // 2))
        start, forward, finish = _gather_stages((x, y, cc, chip, (x, y, 1 - cc)), win_ref, win_all, lane_half,
                                                *gather_sems)
        start()

        cslab[...] = jnp.broadcast_to(c_ref[...], (8, D))
        call_ref[me] = cslab[...]
        gathers = []
        for k, (dx, dy, dc) in enumerate(REL7):
            cp = pltpu.make_async_remote_copy(
                src_ref=cslab, dst_ref=call_ref.at[me], send_sem=cs_sem.at[k], recv_sem=cr_sem.at[k],
                device_id=(_flip(x, dx), _flip(y, dy), _flip(cc, dc)), device_id_type=MESH)
            cp.start()
            gathers.append(cp)
        for cp in gathers:
            cp.wait()
        slab_row = lax.broadcasted_iota(jnp.int32, (8, 1), 0)
        mat = jnp.zeros((8, D), F32)
        for r in range(8):
            mat = jnp.where(slab_row == r, call_ref[r], mat)
        act = (mat * _sigmoid(mat)).astype(BF16)
        part = _dot(act, w_ref[...].astype(BF16))
        sends = []
        for k, (dx, dy) in enumerate(REL3):
            px, py = _flip(x, dx), _flip(y, dy)
            r = 4 * px + 2 * py + cc
            piece = _colsum(jnp.where(slab_row == r, part, 0.0))
            sbuf[k] = jnp.broadcast_to(piece, (8, SHARD_ADA))
            cp = pltpu.make_async_remote_copy(
                src_ref=sbuf.at[k], dst_ref=rbuf.at[k], send_sem=as_sem.at[k], recv_sem=ar_sem.at[k],
                device_id=(px, py, cc), device_id_type=MESH)
            cp.start()
            sends.append(cp)
        own_piece = _colsum(jnp.where(slab_row == me, part, 0.0))
        ada_ref[chip] = jnp.broadcast_to(own_piece, (8, SHARD_ADA)) + b_ref[chip]
        for k, (dx, dy) in enumerate(REL3):
            sends[k].wait()
            a = 2 * _flip(x, dx) + _flip(y, dy)
            ada_ref[a] = rbuf[k] + b_ref[a]

        forward()
        finish()

    vm = pl.BlockSpec(memory_space=pltpu.VMEM)
    return pl.pallas_call(
        body, name="gather_and_ada",
        out_shape=(jax.ShapeDtypeStruct((8, 8, D), F32), jax.ShapeDtypeStruct((4, 8, SHARD_ADA), F32),
                   jax.ShapeDtypeStruct((N_CHIPS, SHARD_IN, D), BF16)),
        in_specs=[vm] * 4, out_specs=(vm,) * 3,
        scratch_shapes=[pltpu.VMEM((8, D), F32), pltpu.VMEM((3, 8, SHARD_ADA), F32),
                        pltpu.VMEM((3, 8, SHARD_ADA), F32),
                        pltpu.SemaphoreType.DMA((7,)), pltpu.SemaphoreType.DMA((7,)),
                        pltpu.SemaphoreType.DMA((3,)), pltpu.SemaphoreType.DMA((3,))] + _gather_scratch(),
        compiler_params=_params(),
    )(c, w_ada, b_ada4, w_in_sh)


def _scatter_stages(pos, g_ref, sc_ref, out_ref, sib_buf, send_buf, ici_buf, sem1, sem2s, sem2r, sem3, own=None):
    x, y, cc, chip, sib = pos
    RH = g_ref.shape[1] // 2
    mine = pl.ds(pl.multiple_of(cc * RH, RH), RH)
    theirs = pl.ds(pl.multiple_of((1 - cc) * RH, RH), RH)
    cp1 = pltpu.make_async_remote_copy(
        src_ref=g_ref.at[:, theirs, :], dst_ref=sib_buf, send_sem=sem1.at[0], recv_sem=sem1.at[1],
        device_id=sib, device_id_type=MESH)
    stage_own = None if own is None else pltpu.make_async_copy(g_ref.at[:, mine, :], own[0], own[1])
    sends = []
    for k, (dx, dy) in enumerate(REL3):
        px, py = _flip(x, dx), _flip(y, dy)
        sends.append(pltpu.make_async_remote_copy(
            src_ref=send_buf.at[2 * px + py], dst_ref=ici_buf.at[chip],
            send_sem=sem2s.at[k], recv_sem=sem2r.at[k], device_id=(px, py, cc), device_id_type=MESH))
    cp3 = pltpu.make_async_remote_copy(
        src_ref=out_ref.at[mine, :], dst_ref=out_ref.at[mine, :], send_sem=sem3.at[0], recv_sem=sem3.at[1],
        device_id=sib, device_id_type=MESH)

    def start1():
        cp1.start()
        if stage_own is not None:
            stage_own.start()

    def finish1():
        cp1.wait()
        if stage_own is not None:
            stage_own.wait()
        for a in range(N_CHIPS):
            both = (g_ref[a, mine, :] if own is None else own[0][a]) + sib_buf[a]
            sib_buf[a] = both
            send_buf[a] = both.astype(BF16)

    def start2():
        for cp in sends:
            cp.start()
        ici_buf[chip] = send_buf[chip]

    def finish2():
        for cp in sends:
            cp.wait()
        mine_f32 = sib_buf[chip]
        parts = [jnp.where(chip == a, mine_f32, ici_buf[a].astype(F32)) for a in range(N_CHIPS)]
        out_ref[mine, :] = ((parts[0] + parts[1]) + (parts[2] + parts[3])) * sc_ref[chip]

    return [(start1, finish1), (start2, finish2), (cp3.start, cp3.wait)]


def _all_reduce_stages(pos, g_ref, out_ref, sib_buf, ici_buf, sem1, sem2s, sem2r, sem3):
    x, y, cc, chip, sib = pos
    RH = g_ref.shape[0] // 2
    mine = pl.ds(pl.multiple_of(cc * RH, 8), RH)
    theirs = pl.ds(pl.multiple_of((1 - cc) * RH, 8), RH)
    cp1 = pltpu.make_async_remote_copy(
        src_ref=g_ref.at[theirs, :], dst_ref=sib_buf, send_sem=sem1.at[0], recv_sem=sem1.at[1],
        device_id=sib, device_id_type=MESH)
    sends = []
    for k, (dx, dy) in enumerate(REL3):
        px, py = _flip(x, dx), _flip(y, dy)
        sends.append(pltpu.make_async_remote_copy(
            src_ref=sib_buf, dst_ref=ici_buf.at[chip],
            send_sem=sem2s.at[k], recv_sem=sem2r.at[k], device_id=(px, py, cc), device_id_type=MESH))
    cp3 = pltpu.make_async_remote_copy(
        src_ref=out_ref.at[mine, :], dst_ref=out_ref.at[mine, :], send_sem=sem3.at[0], recv_sem=sem3.at[1],
        device_id=sib, device_id_type=MESH)

    def finish1():
        cp1.wait()
        sib_buf[...] = g_ref[mine, :] + sib_buf[...]

    def start2():
        for cp in sends:
            cp.start()
        ici_buf[chip] = sib_buf[...]

    def finish2():
        for cp in sends:
            cp.wait()
        out_ref[mine, :] = (ici_buf[0] + ici_buf[1]) + (ici_buf[2] + ici_buf[3])

    return [(cp1.start, finish1), (start2, finish2), (cp3.start, cp3.wait)]


def _stage_sems():
    return [pltpu.SemaphoreType.DMA((2,)), pltpu.SemaphoreType.DMA((3,)),
            pltpu.SemaphoreType.DMA((3,)), pltpu.SemaphoreType.DMA((2,))]


def _scatter_scratch(r, c):
    return [pltpu.VMEM((N_CHIPS, r // 2, c), F32), pltpu.VMEM((N_CHIPS, r // 2, c), BF16),
            pltpu.VMEM((N_CHIPS, r // 2, c), BF16)] + _stage_sems()


def _reduce_small(small, dada):
    R = small.shape[0]
    W = dada.shape[1]

    def body(sm_ref, d_ref, osm_ref, dall_ref, dslab, ds_sem, dr_sem, *sm_bufs):
        x, y, cc = lax.axis_index("x"), lax.axis_index("y"), lax.axis_index("c")
        me = 4 * x + 2 * y + cc
        pos = (x, y, cc, 2 * x + y, (x, y, 1 - cc))
        dslab[...] = jnp.broadcast_to(d_ref[...], (8, W))
        dall_ref[me] = dslab[...]
        gathers = []
        for k, (dx, dy, dc) in enumerate(REL7):
            cp = pltpu.make_async_remote_copy(
                src_ref=dslab, dst_ref=dall_ref.at[me], send_sem=ds_sem.at[k], recv_sem=dr_sem.at[k],
                device_id=(_flip(x, dx), _flip(y, dy), _flip(cc, dc)), device_id_type=MESH)
            cp.start()
            gathers.append(cp)
        for start, finish in _all_reduce_stages(pos, sm_ref, osm_ref, *sm_bufs):
            start()
            finish()
        for cp in gathers:
            cp.wait()

    scratch = [pltpu.VMEM((8, W), F32), pltpu.SemaphoreType.DMA((7,)), pltpu.SemaphoreType.DMA((7,))]
    scratch += [pltpu.VMEM((R // 2, 128), F32), pltpu.VMEM((N_CHIPS, R // 2, 128), F32)] + _stage_sems()
    vm = pl.BlockSpec(memory_space=pltpu.VMEM)
    return pl.pallas_call(
        body, name="reduce_small",
        out_shape=(jax.ShapeDtypeStruct((R, 128), F32), jax.ShapeDtypeStruct((8, 8, W), F32)),
        in_specs=[vm] * 2, out_specs=(vm,) * 2,
        scratch_shapes=scratch,
        compiler_params=_params(),
    )(small, dada)


def _in_proj(x, shift, scale, wt_pad, b_pad, w_out_sh):
    S = x.shape[0]
    tm = min(TM_PROJ, S)
    n_steps = S // tm
    assert n_steps >= 3

    def body(x_ref, sh_ref, sc_ref, w_ref, b_ref, wo_ref, u_ref, qkv_ref, f_ref, p_ref, g_ref, wo_all,
             wo_buf, *gather_sems):
        i = pl.program_id(0)
        xx, yy, cc = lax.axis_index("x"), lax.axis_index("y"), lax.axis_index("c")
        row_half = lambda which: (pl.ds(pl.multiple_of(which * (SHARD_OUT // 2), SHARD_OUT // 2), SHARD_OUT // 2),
                                  slice(None))
        start, forward, finish = _gather_stages((xx, yy, cc, 2 * xx + yy, (xx, yy, 1 - cc)), wo_ref, wo_buf,
                                                row_half, *gather_sems)
        pl.when(i == 0)(start)
        pl.when(i == n_steps // 2)(forward)

        @pl.when(i == n_steps - 1)
        def _():
            finish()
            wo_all[...] = wo_buf[...]

        u = (x_ref[...] * (1.0 + sc_ref[...]) + sh_ref[...]).astype(BF16)
        u_ref[...] = u
        qkv_ref[...] = (_dot_nt(u, w_ref[O_QKV:O_F, :]) + b_ref[:, O_QKV:O_F]).astype(BF16)
        f_ref[...] = _dot_nt(u, w_ref[O_F:O_P, :]) + b_ref[:, O_F:O_P]
        p_ref[...] = _dot_nt(u, w_ref[O_P:O_G, :]) + b_ref[:, O_P:O_G]
        g_ref[...] = _dot_nt(u, w_ref[O_G:D_PAD, :]) + b_ref[:, O_G:D_PAD]

    row = lambda w: pl.BlockSpec((tm, w), lambda i: (i, 0))
    full = lambda a: pl.BlockSpec(a.shape, lambda i: (0, 0))
    vm = pl.BlockSpec(memory_space=pltpu.VMEM)
    return pl.pallas_call(
        body, name="in_proj", grid=(n_steps,),
        out_shape=(jax.ShapeDtypeStruct((S, D), BF16), jax.ShapeDtypeStruct((S, 3 * D_ATT), BF16),
                   jax.ShapeDtypeStruct((S, 128), F32), jax.ShapeDtypeStruct((S, D_POOL), F32),
                   jax.ShapeDtypeStruct((S, D), F32), jax.ShapeDtypeStruct((N_CHIPS,) + w_out_sh.shape, BF16)),
        in_specs=[row(D), full(shift), full(scale), full(wt_pad), full(b_pad), vm],
        out_specs=(row(D), row(3 * D_ATT), row(128), row(D_POOL), row(D), vm),
        scratch_shapes=[pltpu.VMEM((N_CHIPS,) + w_out_sh.shape, BF16)] + _gather_scratch(),
        compiler_params=_params(dimension_semantics=("arbitrary",)),
    )(x, shift, scale, wt_pad, b_pad, w_out_sh)


def _forget_cumsum(f):
    S = f.shape[0]
    tm = min(T_ATT, S)

    def body(f_ref, out_ref, carry):
        @pl.when(pl.program_id(0) == 0)
        def _():
            carry[...] = jnp.zeros_like(carry)
        v = f_ref[...]
        logf = jnp.minimum(v, 0.0) - jnp.log(1.0 + jnp.exp(-jnp.abs(v)))
        r = lax.broadcasted_iota(jnp.int32, (tm, tm), 0)
        c = lax.broadcasted_iota(jnp.int32, (tm, tm), 1)
        tri = (r <= c).astype(F32)
        rows8 = logf.T[0:8, :]
        cum8 = jnp.dot(rows8, tri, preferred_element_type=F32, precision=lax.Precision.HIGHEST) + carry[...]
        out_ref[...] = jnp.concatenate([cum8, jnp.zeros((128 - 8, tm), F32)], axis=0).T
        last = lax.broadcasted_iota(jnp.int32, (1, tm), 1) == tm - 1
        carry[...] = jnp.sum(jnp.where(last, cum8, 0.0), axis=1, keepdims=True)

    return pl.pallas_call(
        body, name="forget_cumsum", grid=(S // tm,),
        out_shape=jax.ShapeDtypeStruct((S, 128), F32),
        in_specs=[pl.BlockSpec((tm, 128), lambda i: (i, 0))],
        out_specs=pl.BlockSpec((tm, 128), lambda i: (i, 0)),
        scratch_shapes=[pltpu.VMEM((8, 1), F32)],
        compiler_params=_params(dimension_semantics=("arbitrary",)),
    )(f)


def _split3(v):
    hi = v.astype(BF16)
    rest = v - hi.astype(F32)
    mid = rest.astype(BF16)
    lo = (rest - mid.astype(F32)).astype(BF16)
    return hi, mid, lo


def _attention_fwd(qkv, big_f):
    S = qkv.shape[0]
    T = min(T_ATT, S)
    n_t = S // T

    def body(q_ref, k_ref, v_ref, f_ref, o_ref, lse_ref, kaug_sc, vt_sc, m_sc, l_sc, acc_sc):
        hp = pl.program_id(0)
        i = pl.program_id(1)
        lane = lax.broadcasted_iota(jnp.int32, (1, 128), 1)
        sub = lax.broadcasted_iota(jnp.int32, (128, 1), 0)
        head_sel = (lane < HEAD_DIM, lane >= HEAD_DIM)
        head_sel_t = (sub < HEAD_DIM, sub >= HEAD_DIM)
        spare = (HEAD_DIM, 0)
        zero = jnp.zeros((), BF16)

        @pl.when(i == 0)
        def _():
            def prep(jt, carry):
                rows = pl.ds(pl.multiple_of(jt * T, T), T)
                k = k_ref[rows, :]
                ft = f_ref[rows, :]
                vt = v_ref[rows, :].astype(F32).T
                for h in range(2):
                    fh = jnp.sum(jnp.where(lane == 2 * hp + h, ft, 0.0), axis=1, keepdims=True)
                    hi, mid, lo = _split3(-fh)
                    b = spare[h]
                    bias = jnp.where(lane == b, hi, jnp.where(lane == b + 1, mid, jnp.where(lane == b + 2, lo, zero)))
                    kaug_sc[h, rows, :] = jnp.where(head_sel[h], k, bias)
                    vt_sc[h, jt] = jnp.where(head_sel_t[h], vt, 0.0).astype(BF16)
                return carry

            lax.fori_loop(0, n_t, prep, 0)

        q = q_ref[...]
        q_heads = []
        for h in range(2):
            ones = jnp.where((lane >= spare[h]) & (lane < spare[h] + 3), jnp.ones((), BF16), zero)
            q_heads.append(jnp.where(head_sel[h], q, ones))
        m_sc[...] = jnp.full((8, T), NEG, F32)
        l_sc[...] = jnp.zeros((8, T), F32)
        acc_sc[...] = jnp.zeros((128, T), F32)

        def update(j, k_lo, n_k, q_lo, masked):
            rows = pl.ds(pl.multiple_of(j * T + k_lo, n_k), n_k)
            n_q = T - q_lo
            alphas, pvs = [], []
            for h in range(2):
                s_t = _dot_nt(kaug_sc[h, rows, :], q_heads[h][q_lo:, :])
                if masked:
                    rr = lax.broadcasted_iota(jnp.int32, (n_k, n_q), 0) + k_lo
                    cc = lax.broadcasted_iota(jnp.int32, (n_k, n_q), 1) + q_lo
                    s_t = jnp.where(rr <= cc, s_t, NEG)
                m_prev = m_sc[h:h + 1, q_lo:]
                m_new = jnp.maximum(m_prev, jnp.max(s_t, axis=0, keepdims=True))
                alpha = jnp.exp(m_prev - m_new)
                p_t = jnp.exp(s_t - m_new)
                l_sc[h:h + 1, q_lo:] = alpha * l_sc[h:h + 1, q_lo:] + jnp.sum(p_t, axis=0, keepdims=True)
                m_sc[h:h + 1, q_lo:] = m_new
                alphas.append(alpha)
                pvs.append(_dot(vt_sc[h, j, :, k_lo:k_lo + n_k], p_t.astype(BF16)))
            acc_sc[:, q_lo:] = (acc_sc[:, q_lo:] * jnp.where(head_sel_t[0], alphas[0], alphas[1])
                                + (pvs[0] + pvs[1]))

        def two_off_diagonal(jj, carry):
            update(2 * jj, 0, T, 0, False)
            update(2 * jj + 1, 0, T, 0, False)
            return carry

        lax.fori_loop(0, i // 2, two_off_diagonal, 0)

        @pl.when(i % 2 == 1)
        def _():
            update(i - 1, 0, T, 0, False)

        update(i, 0, T, 0, True)
        l = l_sc[...]
        o_ref[...] = (acc_sc[...] / jnp.where(head_sel_t[0], l[0:1, :], l[1:2, :])).T
        is_head = lax.broadcasted_iota(jnp.int32, (8, 1), 0) < 2
        lse_ref[...] = jnp.where(is_head, m_sc[...] + jnp.log(jnp.where(is_head, l, 1.0)), 0.0)

    return pl.pallas_call(
        body, name="attention_fwd", grid=(N_PAIR, n_t),
        out_shape=(jax.ShapeDtypeStruct((S, D_ATT), F32), jax.ShapeDtypeStruct((N_PAIR, n_t, 8, T), F32)),
        in_specs=[pl.BlockSpec((T, 128), lambda hp, i: (i, hp)),
                  pl.BlockSpec((S, 128), lambda hp, i: (0, N_PAIR + hp)),
                  pl.BlockSpec((S, 128), lambda hp, i: (0, 2 * N_PAIR + hp)),
                  pl.BlockSpec((S, 128), lambda hp, i: (0, 0))],
        out_specs=(pl.BlockSpec((T, 128), lambda hp, i: (i, hp)),
                   pl.BlockSpec((None, None, 8, T), lambda hp, i: (hp, i, 0, 0))),
        scratch_shapes=[pltpu.VMEM((2, S, 128), BF16), pltpu.VMEM((2, n_t, 128, T), BF16),
                        pltpu.VMEM((8, T), F32), pltpu.VMEM((8, T), F32), pltpu.VMEM((128, T), F32)],
        compiler_params=_params(dimension_semantics=("arbitrary", "arbitrary")),
    )(qkv, qkv, qkv, big_f)


def _attention_bwd(qkv, datt, att, lse, big_f, gw_out4, sc_out):
    S = qkv.shape[0]
    T = min(T_ATT, S)
    n_t = S // T
    n_steps = N_PAIR * n_t
    marks = (0, n_steps // 8, n_steps // 2, n_steps // 2 + n_steps // 8)

    def body(q_ref, do_ref, o_ref, lse_ref, k_ref, v_ref, fk_ref, gout_ref, scout_ref,
             dq_ref, dk_ref, dv_ref, cs_ref, dfk_ref, dfq_ref, oout_ref, stat_sc, dqt_sc, qaug_sc,
             out_buf, *red_bufs):
        hp = pl.program_id(0)
        j = pl.program_id(1)
        x, y, cc = lax.axis_index("x"), lax.axis_index("y"), lax.axis_index("c")
        plan = _scatter_stages((x, y, cc, 2 * x + y, (x, y, 1 - cc)), gout_ref, scout_ref, out_buf, *red_bufs)
        step = hp * n_t + j
        for n, mark in enumerate(marks):
            @pl.when(step == mark)
            def _(n=n):
                if n > 0:
                    plan[n - 1][1]()
                if n < 3:
                    plan[n][0]()
                else:
                    oout_ref[...] = out_buf[...]

        lane = lax.broadcasted_iota(jnp.int32, (1, 128), 1)
        sub = lax.broadcasted_iota(jnp.int32, (128, 1), 0)
        head_sel = (lane < HEAD_DIM, lane >= HEAD_DIM)
        head_sel_t = (sub < HEAD_DIM, sub >= HEAD_DIM)
        spare = (HEAD_DIM, 0)
        zero = jnp.zeros((), BF16)
        one = jnp.ones((), BF16)

        def bias_lanes(first, pieces):
            hi, mid, lo = pieces
            return lambda rest: jnp.where(lane == first, hi, jnp.where(lane == first + 1, mid,
                                                                        jnp.where(lane == first + 2, lo, rest)))

        @pl.when(j == 0)
        def _():
            dqt_sc[...] = jnp.zeros_like(dqt_sc)
            cs_ref[...] = jnp.zeros_like(cs_ref)
            dfq_ref[...] = jnp.zeros_like(dfq_ref)

            def prep(i, carry):
                rows = pl.ds(pl.multiple_of(i * T, T), T)
                q = q_ref[rows, :]
                do = do_ref[rows, :]
                prod = o_ref[rows, :] * do.astype(F32)
                d_a = jnp.sum(jnp.where(head_sel[0], prod, 0.0), axis=1, keepdims=True)
                d_b = jnp.sum(jnp.where(head_sel[0], 0.0, prod), axis=1, keepdims=True)
                delta_t = jnp.where(head_sel[0], d_a, d_b).T
                stat_sc[i, 0:1, :] = delta_t[0:1, :]
                stat_sc[i, 1:2, :] = delta_t[HEAD_DIM:HEAD_DIM + 1, :]
                lse = lse_ref[i]
                lse_cols = jnp.where(head_sel_t[0], lse[0:1, :], lse[1:2, :]).T
                for h in range(2):
                    neg_lse = -lse_cols[:, h * HEAD_DIM:h * HEAD_DIM + 1]
                    ones = jnp.where((lane >= spare[h]) & (lane < spare[h] + 3), one, zero)
                    qaug_sc[h, rows, :] = jnp.where(head_sel[h], q, bias_lanes(spare[h] + 3, _split3(neg_lse))(ones))
                return carry

            lax.fori_loop(0, n_t, prep, 0)

        k = k_ref[...]
        v = v_ref[...]
        fk = fk_ref[...]
        kt = k.astype(F32).T
        heads = []
        for h in range(2):
            fkh = jnp.sum(jnp.where(lane == 2 * hp + h, fk, 0.0), axis=1, keepdims=True)
            ones = jnp.where((lane >= spare[h] + 3) & (lane < spare[h] + 6), one, zero)
            kaug = jnp.where(head_sel[h], k, bias_lanes(spare[h], _split3(-fkh))(ones))
            heads.append((kaug, jnp.where(head_sel[h], v, zero), jnp.where(head_sel_t[h], kt, 0.0).astype(BF16)))

        def block(i, k_lo, n_k, q_lo, masked):
            n_q = T - q_lo
            rows = pl.ds(pl.multiple_of(i * T + q_lo, n_q), n_q)
            q = q_ref[rows, :]
            do = do_ref[rows, :]
            stat = stat_sc[i]
            dk = jnp.zeros((n_k, 128), F32)
            dv = jnp.zeros((n_k, 128), F32)
            dqt = jnp.zeros((128, n_q), F32)
            dfs = []
            for h in range(2):
                kaug, vh, kth = heads[h]
                arg = _dot_nt(kaug[k_lo:k_lo + n_k, :], qaug_sc[h, rows, :])
                if masked:
                    rr = lax.broadcasted_iota(jnp.int32, (n_k, n_q), 0) + k_lo
                    cc = lax.broadcasted_iota(jnp.int32, (n_k, n_q), 1) + q_lo
                    arg = jnp.where(rr <= cc, arg, NEG)
                p_t = jnp.exp(arg)
                ds_t = p_t * (_dot_nt(vh[k_lo:k_lo + n_k, :], do) - stat[h:h + 1, q_lo:])
                ds_bf = ds_t.astype(BF16)
                dv = dv + _dot(p_t.astype(BF16), jnp.where(head_sel[h], do, zero))
                dk = dk + _dot(ds_bf, jnp.where(head_sel[h], q, zero))
                dqt = dqt + _dot(kth[:, k_lo:k_lo + n_k], ds_bf)
                dfs.append(jnp.sum(ds_t, axis=1, keepdims=True))
                dfq_ref[i, h:h + 1, q_lo:] += _colsum(ds_t)
            dqt_sc[i, :, q_lo:] += dqt
            return dk, dv, dfs[0], dfs[1]

        def off_diagonal(i, acc):
            return tuple(a + b for a, b in zip(acc, block(i, 0, T, 0, False)))

        half = T // 2
        early = block(j, 0, half, 0, True)
        late = block(j, half, half, half, True)
        acc1 = tuple(jnp.concatenate([a, b], axis=0) for a, b in zip(early, late))
        dk_acc, dv_acc, dfa, dfb = lax.fori_loop(j + 1, n_t, off_diagonal, acc1)
        dk_ref[...] = dk_acc.astype(BF16)
        dv_ref[...] = dv_acc.astype(BF16)
        dfk_ref[...] = -jnp.where(lane == 0, dfa, jnp.where(lane == 1, dfb, 0.0))
        cs_ref[:, 128:256] = cs_ref[:, 128:256] + _colsum(dk_acc)
        cs_ref[:, 256:384] = cs_ref[:, 256:384] + _colsum(dv_acc)

        @pl.when(j == n_t - 1)
        def _():
            def finish(i, tot):
                dq = dqt_sc[i].T
                dq_ref[pl.ds(pl.multiple_of(i * T, T), T), :] = dq.astype(BF16)
                return tot + _colsum(dq)

            cs_ref[:, 0:128] = lax.fori_loop(0, n_t, finish, jnp.zeros((1, 128), F32))

    pair_rows = lambda hp, j: (hp, 0, 0)
    vm = pl.BlockSpec(memory_space=pltpu.VMEM)
    _, r_out, c_out = gw_out4.shape
    return pl.pallas_call(
        body, name="attention_bwd", grid=(N_PAIR, n_t),
        out_shape=(jax.ShapeDtypeStruct((S, D_ATT), BF16), jax.ShapeDtypeStruct((S, D_ATT), BF16),
                   jax.ShapeDtypeStruct((S, D_ATT), BF16), jax.ShapeDtypeStruct((N_PAIR, 1, 384), F32),
                   jax.ShapeDtypeStruct((N_PAIR, S, 128), F32),
                   jax.ShapeDtypeStruct((N_PAIR, n_t, 8, T), F32),
                   jax.ShapeDtypeStruct((r_out, c_out), F32)),
        in_specs=[pl.BlockSpec((S, 128), lambda hp, j: (0, hp)),
                  pl.BlockSpec((S, 128), lambda hp, j: (0, hp)),
                  pl.BlockSpec((S, 128), lambda hp, j: (0, hp)),
                  pl.BlockSpec((None, n_t, 8, T), lambda hp, j: (hp, 0, 0, 0)),
                  pl.BlockSpec((T, 128), lambda hp, j: (j, N_PAIR + hp)),
                  pl.BlockSpec((T, 128), lambda hp, j: (j, 2 * N_PAIR + hp)),
                  pl.BlockSpec((T, 128), lambda hp, j: (j, 0)),
                  vm, vm],
        out_specs=(pl.BlockSpec((S, 128), lambda hp, j: (0, hp)),
                   pl.BlockSpec((T, 128), lambda hp, j: (j, hp)),
                   pl.BlockSpec((T, 128), lambda hp, j: (j, hp)),
                   pl.BlockSpec((None, 1, 384), pair_rows),
                   pl.BlockSpec((None, T, 128), lambda hp, j: (hp, j, 0)),
                   pl.BlockSpec((None, n_t, 8, T), lambda hp, j: (hp, 0, 0, 0)),
                   vm),
        scratch_shapes=[pltpu.VMEM((n_t, 8, T), F32), pltpu.VMEM((n_t, 128, T), F32),
                        pltpu.VMEM((2, S, 128), BF16), pltpu.VMEM((r_out, c_out), F32)]
        + _scatter_scratch(r_out, c_out),
        compiler_params=_params(dimension_semantics=("arbitrary", "arbitrary")),
    )(qkv, datt, att, lse, qkv, qkv, big_f, gw_out4, sc_out)


def _window_counts(first_row, n_rows, window):
    t = lax.broadcasted_iota(jnp.int32, (n_rows, 1), 0) + first_row
    return jnp.minimum((t + 1).astype(F32), float(window))


def _middle(x, tgt, att, g, p, gate, w_mix, b_mix, pool_scale, w_out, b_out, ln_g, ln_b):
    S = x.shape[0]
    tm = min(TM_MID, S)
    halo_blocks = tm // POOL_HALO

    def body(x_ref, t_ref, att_ref, g_ref, p_ref, ph_ref, gate_ref, wm_ref, bm_ref, ps_ref, wo_ref, bo_ref,
             lg_ref, lb_ref,
             dh_ref, datt_ref, dg_ref, dpl_ref, gwo_ref, gwm_ref, vec_ref, loss_ref):
        i = pl.program_id(0)

        @pl.when(i == 0)
        def _():
            gwo_ref[...] = jnp.zeros_like(gwo_ref)
            gwm_ref[...] = jnp.zeros_like(gwm_ref)
            vec_ref[...] = jnp.zeros_like(vec_ref)
            loss_ref[...] = jnp.zeros_like(loss_ref)

        pc = p_ref[...]
        halo = jnp.where(i > 0, ph_ref[...], 0.0)
        pe = jnp.concatenate([halo, pc], axis=0)
        pooled_parts = []
        for gi, w in enumerate(POOL_WINDOWS):
            cur = pe[:, gi * POOL_GROUP:(gi + 1) * POOL_GROUP]
            span = 1
            while span < w:
                cur = cur + pltpu.roll(cur, span, 0)
                span *= 2
            wsum = cur[POOL_HALO:, :]
            mean = wsum / _window_counts(i * tm, tm, w)
            pooled_parts.append(mean - pc[:, gi * POOL_GROUP:(gi + 1) * POOL_GROUP])
        pooled_bf =[v.astype(BF16) for v in pooled_parts]
        mixed = jnp.concatenate([_dot(pooled_bf[gi], wm_ref[gi]) for gi in range(4)], axis=1) + bm_ref[...]
        ps = ps_ref[...]
        pool_out = mixed * ps
        gv = g_ref[...]
        sig = _sigmoid(gv)
        silu = gv * sig
        att = att_ref[...]
        y = jnp.concatenate([att * silu[:, :D_ATT], pool_out * silu[:, D_ATT:]], axis=1)
        y_bf = y.astype(BF16)
        wo = wo_ref[...]
        yo = _dot(y_bf, wo) + bo_ref[...]
        gate = gate_ref[...]
        h = ALPHA * x_ref[...] + gate * yo
        mu = jnp.mean(h, axis=1, keepdims=True)
        hc = h - mu
        var = jnp.mean(hc * hc, axis=1, keepdims=True)
        rstd = lax.rsqrt(var + LN_EPS)
        yhat = hc * rstd
        lg = lg_ref[...]
        out = yhat * lg + lb_ref[...]
        err = out - t_ref[...]
        loss_ref[...] += 0.5 * jnp.sum(jnp.mean(err * err, axis=1, keepdims=True), axis=0, keepdims=True)

        dout = err * (1.0 / D)
        g_ln_b = _colsum(dout)
        g_ln_g = _colsum(dout * yhat)
        dyh = dout * lg
        dh = rstd * (dyh - jnp.mean(dyh, axis=1, keepdims=True)
                     - yhat * jnp.mean(dyh * yhat, axis=1, keepdims=True))
        dh_ref[...] = dh
        d_gate = _colsum(dh * yo)
        dyo = gate * dh
        g_b_out = _colsum(dyo)
        dyo_bf = dyo.astype(BF16)
        gwo_ref[...] += _dot_tn(y_bf, dyo_bf)
        dy = _dot_nt(dyo_bf, wo)
        dsilu = sig * (1.0 + gv * (1.0 - sig))
        dy_a = dy[:, :D_ATT]
        dy_p = dy[:, D_ATT:]
        datt_ref[...] = (dy_a * silu[:, :D_ATT]).astype(BF16)
        dpo = dy_p * silu[:, D_ATT:]
        dg = jnp.concatenate([dy_a * att * dsilu[:, :D_ATT], dy_p * pool_out * dsilu[:, D_ATT:]], axis=1)
        dg_ref[...] = dg.astype(BF16)
        g_dg = _colsum(dg)
        g_ps = _colsum(dpo * mixed)
        dmixed = dpo * ps
        g_bm = _colsum(dmixed)
        dmixed_bf = dmixed.astype(BF16)
        dpl = []
        for gi in range(4):
            dm = dmixed_bf[:, gi * POOL_GROUP:(gi + 1) * POOL_GROUP]
            gwm_ref[gi] += _dot_tn(pooled_bf[gi], dm)
            dpl.append(_dot_nt(dm, wm_ref[gi]))
        dpl_ref[...] = jnp.concatenate(dpl, axis=1)
        vec_ref[0:1, :] += g_ln_g
        vec_ref[1:2, :] += g_ln_b
        vec_ref[2:3, :] += d_gate
        vec_ref[3:4, :] += g_b_out
        vec_ref[4:5, :] += g_dg
        vec_ref[5:6, 0:D_POOL] += g_ps
        vec_ref[6:7, 0:D_POOL] += g_bm

    row = lambda w: pl.BlockSpec((tm, w), lambda i: (i, 0))
    full2 = lambda a: pl.BlockSpec(a.shape, lambda i: (0, 0))
    full3 = lambda a: pl.BlockSpec(a.shape, lambda i: (0, 0, 0))
    return pl.pallas_call(
        body, name="middle", grid=(S // tm,),
        out_shape=(jax.ShapeDtypeStruct((S, D), F32),
                   jax.ShapeDtypeStruct((S, D_ATT), BF16),
                   jax.ShapeDtypeStruct((S, D), BF16),
                   jax.ShapeDtypeStruct((S, D_POOL), F32),
                   jax.ShapeDtypeStruct((D, D), F32),
                   jax.ShapeDtypeStruct((4, POOL_GROUP, POOL_GROUP), F32),
                   jax.ShapeDtypeStruct((8, D), F32),
                   jax.ShapeDtypeStruct((1, 1), F32)),
        in_specs=[row(D), row(D), row(D_ATT), row(D), row(D_POOL),
                  pl.BlockSpec((POOL_HALO, D_POOL), lambda i: (jnp.maximum(i * halo_blocks - 1, 0), 0)),
                  full2(gate), full3(w_mix), full2(b_mix), full2(pool_scale), full2(w_out), full2(b_out),
                  full2(ln_g), full2(ln_b)],
        out_specs=(row(D), row(D_ATT), row(D), row(D_POOL),
                   pl.BlockSpec((D, D), lambda i: (0, 0)),
                   pl.BlockSpec((4, POOL_GROUP, POOL_GROUP), lambda i: (0, 0, 0)),
                   pl.BlockSpec((8, D), lambda i: (0, 0)),
                   pl.BlockSpec((1, 1), lambda i: (0, 0))),
        compiler_params=_params(dimension_semantics=("arbitrary",)),
    )(x, tgt, att, g, p, p, gate, w_mix, b_mix, pool_scale, w_out, b_out, ln_g, ln_b)


def _tail(dpl, dfk, dfq, f):
    S = dpl.shape[0]
    tm = min(T_ATT, S)
    n_t = S // tm
    halo_blocks = tm // POOL_HALO
    last_halo = S // POOL_HALO - 1

    def body(d_ref, dn_ref, dfk_ref, dfq_ref, f_ref, dp_ref, df_ref, cs_ref, carry):
        s = pl.program_id(0)
        i = n_t - 1 - s

        @pl.when(s == 0)
        def _():
            carry[...] = jnp.zeros_like(carry)
            cs_ref[...] = jnp.zeros_like(cs_ref)

        dc = d_ref[...]
        nxt = jnp.where(s > 0, dn_ref[...], 0.0)
        de = jnp.concatenate([dc, nxt], axis=0)
        n_e = tm + POOL_HALO
        parts = []
        for gi, w in enumerate(POOL_WINDOWS):
            cur = de[:, gi * POOL_GROUP:(gi + 1) * POOL_GROUP] / _window_counts(i * tm, n_e, w)
            span = 1
            while span < w:
                cur = cur + pltpu.roll(cur, n_e - span, 0)
                span *= 2
            parts.append(cur[:tm, :] - dc[:, gi * POOL_GROUP:(gi + 1) * POOL_GROUP])
        dp = jnp.concatenate(parts, axis=1)
        dp_ref[...] = dp.astype(BF16)
        cs_ref[0:1, :] += _colsum(dp)

        r = lax.broadcasted_iota(jnp.int32, (tm, tm), 0)
        c = lax.broadcasted_iota(jnp.int32, (tm, tm), 1)
        tri = (r >= c).astype(F32)
        k_cols = dfk_ref[0]
        rows8 = dfq_ref[0]
        for hp in range(1, N_PAIR):
            k_cols = k_cols + pltpu.roll(dfk_ref[hp], 2 * hp, 1)
            rows8 = rows8 + pltpu.roll(dfq_ref[hp], 2 * hp, 0)
        rows8 = rows8 + k_cols.T[0:8, :]
        dlogf8 = jnp.dot(rows8, tri, preferred_element_type=F32, precision=lax.Precision.HIGHEST) + carry[...]
        first = lax.broadcasted_iota(jnp.int32, (1, tm), 1) == 0
        carry[...] = jnp.sum(jnp.where(first, dlogf8, 0.0), axis=1, keepdims=True)
        dlogf = jnp.concatenate([dlogf8, jnp.zeros((128 - 8, tm), F32)], axis=0).T
        df = dlogf * _sigmoid(-f_ref[...])
        df_ref[...] = df.astype(BF16)
        cs_ref[1:2, 0:128] += _colsum(df)

    rev = lambda w: pl.BlockSpec((tm, w), lambda s: (n_t - 1 - s, 0))
    return pl.pallas_call(
        body, name="tail", grid=(n_t,),
        out_shape=(jax.ShapeDtypeStruct((S, D_POOL), BF16), jax.ShapeDtypeStruct((S, 128), BF16),
                   jax.ShapeDtypeStruct((8, D_POOL), F32)),
        in_specs=[rev(D_POOL),
                  pl.BlockSpec((POOL_HALO, D_POOL),
                               lambda s: (jnp.minimum((n_t - s) * halo_blocks, last_halo), 0)),
                  pl.BlockSpec((N_PAIR, tm, 128), lambda s: (0, n_t - 1 - s, 0)),
                  pl.BlockSpec((N_PAIR, None, 8, tm), lambda s: (0, n_t - 1 - s, 0, 0)),
                  rev(128)],
        out_specs=(rev(D_POOL), rev(128), pl.BlockSpec((8, D_POOL), lambda s: (0, 0))),
        scratch_shapes=[pltpu.VMEM((8, 1), F32)],
        compiler_params=_params(dimension_semantics=("arbitrary",)),
    )(dpl, dpl, dfk, dfq, f)


PIECES = ((O_QKV, D_ATT), (O_QKV + D_ATT, D_ATT), (O_QKV + 2 * D_ATT, D_ATT), (O_F, 128), (O_P, D_POOL), (O_G, D))


def _grad_w_in(u, pieces):
    S = u.shape[0]
    tm = min(TM_GW, S)
    n_t = S // tm

    def body(u_ref, *rest):
        piece_refs, out_ref, acc, sem = rest[:6], rest[6], rest[7], rest[8]
        i = pl.program_id(0)

        @pl.when(i == 0)
        def _():
            acc[...] = jnp.zeros_like(acc)

        u_t = u_ref[...]
        for (off, w), ref in zip(PIECES, piece_refs):
            acc[:, off:off + w] += _dot_tn(u_t, ref[...])

        @pl.when(i == n_t - 1)
        def _():
            cp = pltpu.make_async_copy(acc, out_ref, sem)
            cp.start()
            cp.wait()

    return pl.pallas_call(
        body, name="grad_w_in", grid=(n_t,),
        out_shape=jax.ShapeDtypeStruct((D, D_PAD), F32),
        in_specs=[pl.BlockSpec((tm, D), lambda i: (i, 0))]
        + [pl.BlockSpec((tm, w), lambda i: (i, 0)) for _, w in PIECES],
        out_specs=pl.BlockSpec(memory_space=pl.ANY),
        scratch_shapes=[pltpu.VMEM((D, D_PAD), F32), pltpu.SemaphoreType.DMA],
        compiler_params=_params(dimension_semantics=("arbitrary",)),
    )(u, *pieces)


def _grad_x(pieces, wt_pad, dh, x, scale, g4_in, sc_in):
    S = x.shape[0]
    tm = min(TM_DU, S)
    n_steps = S // tm
    assert n_steps >= 4
    n_red = len(_scatter_scratch(*g4_in.shape[1:]))

    def body(*refs):
        piece_refs = refs[:6]
        w_hbm, dh_ref, x_ref, sc_ref, g4_ref, sc4_ref, gx_ref, vec_ref, gout_ref = refs[6:15]
        w_ref, w_sem, own_buf, own_sem, out_buf = refs[15:20]
        red_bufs = refs[20:20 + n_red]
        i = pl.program_id(0)
        xx, yy, cc = lax.axis_index("x"), lax.axis_index("y"), lax.axis_index("c")
        plan = _scatter_stages((xx, yy, cc, 2 * xx + yy, (xx, yy, 1 - cc)), g4_ref, sc4_ref, out_buf, *red_bufs,
                               own=(own_buf, own_sem))

        @pl.when(i == 0)
        def _():
            vec_ref[...] = jnp.zeros_like(vec_ref)
            plan[0][0]()
            weights = pltpu.make_async_copy(w_hbm, w_ref, w_sem)
            weights.start()
            weights.wait()

        @pl.when(i == 2)
        def _():
            plan[0][1]()
            plan[1][0]()

        du = jnp.zeros((tm, D), F32)
        for (off, w), ref in zip(PIECES, piece_refs):
            du = du + _dot(ref[...], w_ref[off:off + w, :])
        xv = x_ref[...]
        gx_ref[...] = ALPHA * dh_ref[...] + du * (1.0 + sc_ref[...])
        vec_ref[0:1, :] += _colsum(du)
        vec_ref[1:2, :] += _colsum(du * xv)

        @pl.when(i == n_steps - 1)
        def _():
            plan[1][1]()
            plan[2][0]()
            plan[2][1]()
            gout_ref[...] = out_buf[...]

    row = lambda w: pl.BlockSpec((tm, w), lambda i: (i, 0))
    vm = pl.BlockSpec(memory_space=pltpu.VMEM)
    _, r_in, c_in = g4_in.shape
    return pl.pallas_call(
        body, name="grad_x", grid=(n_steps,),
        out_shape=(jax.ShapeDtypeStruct((S, D), F32), jax.ShapeDtypeStruct((8, D), F32),
                   jax.ShapeDtypeStruct((r_in, c_in), F32)),
        in_specs=[row(w) for _, w in PIECES]
        + [pl.BlockSpec(memory_space=pl.ANY), row(D), row(D), pl.BlockSpec((1, D), lambda i: (0, 0)),
           pl.BlockSpec(memory_space=pl.ANY), vm],
        out_specs=(row(D), pl.BlockSpec((8, D), lambda i: (0, 0)), vm),
        scratch_shapes=[pltpu.VMEM(wt_pad.shape, BF16), pltpu.SemaphoreType.DMA,
                        pltpu.VMEM((N_CHIPS, r_in // 2, c_in), F32), pltpu.SemaphoreType.DMA,
                        pltpu.VMEM((r_in, c_in), F32)] + _scatter_scratch(r_in, c_in),
        compiler_params=_params(dimension_semantics=("arbitrary",)),
    )(*pieces, wt_pad, dh, x, scale, g4_in, sc_in)


def _grad_ada(c_all, dada_all, dada_cols):
    def body(c_ref, dall_ref, dcol_ref, gw_ref, gb_ref):
        rows = lax.broadcasted_iota(jnp.int32, (8, 1), 0)
        cm = jnp.zeros((8, D), F32)
        dm = jnp.zeros((8, 3 * D), F32)
        for r in range(8):
            cm = jnp.where(rows == r, c_ref[r], cm)
            dm = jnp.where(rows == r, dall_ref[r], dm)
        act = cm * _sigmoid(cm)
        pad = jnp.zeros((8, D), F32)
        lhs = jnp.concatenate([act, pad], axis=0).astype(BF16)
        rhs = jnp.concatenate([dcol_ref[...], jnp.zeros((8, SHARD_ADA), F32)], axis=0).astype(BF16)
        gw_ref[...] = _dot_tn(lhs, rhs)
        gb_ref[...] = _colsum(dm)

    vm = pl.BlockSpec(memory_space=pltpu.VMEM)
    return pl.pallas_call(
        body, name="grad_ada",
        out_shape=(jax.ShapeDtypeStruct((D, SHARD_ADA), F32), jax.ShapeDtypeStruct((1, 3 * D), F32)),
        in_specs=[vm, vm, vm], out_specs=(vm, vm),
        compiler_params=_params(),
    )(c_all, dada_all, dada_cols)


def _adamw_math(w, g, m, v):
    m = ADAM_B1 * m + (1.0 - ADAM_B1) * g
    v = ADAM_B2 * v + (1.0 - ADAM_B2) * (g * g)
    m_hat = m / (1.0 - ADAM_B1 ** ADAM_STEP)
    v_hat = v / (1.0 - ADAM_B2 ** ADAM_STEP)
    delta = -ADAM_LR * (m_hat / (jnp.sqrt(v_hat) + ADAM_EPS) + ADAM_WD * w)
    return delta, m, v


def _adamw(groups, n_steps):
    n = len(groups)

    def body(*refs):
        ins, outs = refs[:4 * n], refs[4 * n:]
        for t in range(n):
            w, g, m, v = (r[...] for r in ins[4 * t:4 * t + 4])
            d, m2, v2 = _adamw_math(w, g, m, v)
            outs[3 * t][...] = d
            outs[3 * t + 1][...] = m2
            outs[3 * t + 2][...] = v2

    in_specs, out_specs, out_shape, args = [], [], [], []
    for (w, g, m, v) in groups:
        rest = w.shape[1:]
        spec = pl.BlockSpec((w.shape[0] // n_steps,) + rest, lambda i, nd=len(rest): (i,) + (0,) * nd)
        in_specs += [spec] * 4
        out_specs += [spec] * 3
        out_shape += [jax.ShapeDtypeStruct(w.shape, F32)] * 3
        args += [w, g, m, v]
    return pl.pallas_call(
        body, name="adamw_%d_%d" % (n, n_steps), grid=(n_steps,),
        out_shape=tuple(out_shape), in_specs=in_specs, out_specs=tuple(out_specs),
        compiler_params=_params(dimension_semantics=("arbitrary",)),
    )(*args)


def _pack_small(parts):
    rows = []
    used = 0
    for name, (first, n_rows) in SMALL_SEGS.items():
        if first > used:
            rows.append(jnp.zeros((first - used, 128), F32))
        flat = parts[name].reshape(-1)
        flat = jnp.pad(flat, (0, n_rows * 128 - flat.shape[0]))
        rows.append(flat.reshape(n_rows, 128))
        used = first + n_rows
    rows.append(jnp.zeros((SMALL_ROWS - used, 128), F32))
    return jnp.concatenate(rows, axis=0)


def _unpack_small(buf, name, shape):
    first, n_rows = SMALL_SEGS[name]
    n = int(np.prod(shape))
    return buf[first:first + n_rows].reshape(-1)[:n].reshape(shape)


def _pad_in(v):
    r = v.shape[0]
    z = jnp.zeros((r, O_P - O_F - N_HEADS), v.dtype)
    return jnp.concatenate([v[:, :3 * D_ATT + N_HEADS], z, v[:, 3 * D_ATT + N_HEADS:]], axis=1)


def _unpad_in(v):
    return jnp.concatenate([v[:, :O_F + N_HEADS], v[:, O_P:]], axis=1)


def _shards_in(v):
    gap = O_P - (O_F + N_HEADS)
    parts = []
    for a in range(N_CHIPS):
        lo, hi = a * SHARD_IN, (a + 1) * SHARD_IN
        cut = O_F + N_HEADS
        if hi <= cut:
            parts.append(v[:, lo:hi])
        elif lo >= cut:
            parts.append(v[:, lo + gap:hi + gap])
        else:
            parts.append(jnp.concatenate([v[:, lo:cut], v[:, cut + gap:hi + gap]], axis=1))
    return jnp.stack(parts, axis=0)


def kernel(x, c, w_ada, b_ada, w_in, b_in, w_pool_mix, b_pool_mix, pool_scale, w_out, b_out, ln_g, ln_b, loss_target, m_w_ada, m_b_ada, m_w_in, m_b_in, m_w_pool_mix, m_b_pool_mix, m_pool_scale, m_w_out, m_b_out, m_ln_g, m_ln_b, v_w_ada, v_b_ada, v_w_in, v_b_in, v_w_pool_mix, v_b_pool_mix, v_pool_scale, v_w_out, v_b_out, v_ln_g, v_ln_b):
    S = x.shape[1]
    T = min(T_ATT, S)
    n_t = S // T
    chip = 2 * lax.axis_index("x") + lax.axis_index("y")
    x2 = x[0]
    tgt = loss_target[0]
    q_scale = jnp.concatenate([jnp.full((1, D_ATT), Q_SCALE, F32), jnp.ones((1, D_PAD - D_ATT), F32)], axis=1)

    to_cols = lambda a: jnp.transpose(a, (2, 0, 1))
    from_cols = lambda a: jnp.transpose(a, (1, 2, 0))
    c_all, ada4, wt_in_all = _gather_and_ada(
        c, w_ada[0], b_ada.reshape(4, 1, SHARD_ADA), to_cols(w_in).reshape(SHARD_IN, D).astype(BF16))
    ada = ada4[:, 0, :].reshape(1, 3 * D)
    shift, scale, gate = ada[:, :D], ada[:, D:2 * D], ada[:, 2 * D:]
    wt_full = wt_in_all.reshape(D_IN, D)
    n_real = 3 * D_ATT + N_HEADS
    wt_pad = jnp.concatenate([wt_full[:D_ATT] * jnp.asarray(Q_SCALE, BF16), wt_full[D_ATT:n_real],
                              jnp.zeros((O_P - n_real, D), BF16), wt_full[n_real:]], axis=0)
    b_pad = _pad_in(b_in) * q_scale
    w_mix_bf = w_pool_mix[0].astype(BF16)

    u, qkv, f, p, g, w_out_all = _in_proj(x2, shift, scale, wt_pad, b_pad, w_out[0].astype(BF16))
    w_out_full = w_out_all.reshape(D, D)
    big_f = _forget_cumsum(f)
    att, lse = _attention_fwd(qkv, big_f)

    dh, datt, dg, dpl, gw_out, gw_mix, vec, loss_part = _middle(
        x2, tgt, att, g, p, gate, w_mix_bf, b_pool_mix.reshape(1, D_POOL), pool_scale, w_out_full, b_out, ln_g, ln_b)
    dq, dk, dv, cs_att, dfk, dfq, g_w_out = _attention_bwd(
        qkv, datt, att, lse, big_f, gw_out.reshape(N_CHIPS, SHARD_OUT, D), jnp.ones((N_CHIPS, 1, D), F32))
    dp, df, cs_tail = _tail(dpl, dfk, dfq, f)
    pieces = (dq, dk, dv, df, dp, dg)
    gw_pad = _grad_w_in(u, pieces)
    grad_x, vec_x, g_w_in = _grad_x(pieces, wt_pad, dh, x2, scale, _shards_in(gw_pad), _shards_in(q_scale))

    cs_qkv = jnp.transpose(cs_att.reshape(N_PAIR, 3, 128), (1, 0, 2)).reshape(1, 3 * D_ATT)
    gb_pad = jnp.concatenate([cs_qkv, cs_tail[1:2, 0:128], cs_tail[0:1, :], vec[4:5, :]], axis=1) * q_scale
    dada = jnp.concatenate([vec_x[0:1, :], vec_x[1:2, :], vec[2:3, :]], axis=1)
    small = _pack_small({
        "b_in": gb_pad, "w_pool_mix": gw_mix, "b_pool_mix": vec[6:7, :D_POOL], "pool_scale": vec[5:6, :D_POOL],
        "b_out": vec[3:4, :], "ln_g": vec[0:1, :], "ln_b": vec[1:2, :], "loss": loss_part,
        "b_ada": jnp.zeros((1, 3 * D), F32)})

    small_sum, dada_all = _reduce_small(small[:SMALL_REDUCED_ROWS], dada)
    dada_cols = lax.dynamic_slice(dada_all[:, 0, :], (0, chip * SHARD_ADA), (8, SHARD_ADA))
    g_w_ada, g_b_ada = _grad_ada(c_all, dada_all, dada_cols)
    loss = _unpack_small(small_sum, "loss", (1,))[0]

    grads_small = jnp.concatenate([small_sum, g_b_ada.reshape(24, 128)], axis=0)
    no_param = jnp.zeros((1,), F32)
    small_w = {"b_in": _pad_in(b_in), "w_pool_mix": w_pool_mix, "b_pool_mix": b_pool_mix, "pool_scale": pool_scale,
               "b_out": b_out, "ln_g": ln_g, "ln_b": ln_b, "loss": no_param, "b_ada": b_ada}
    small_m = {"b_in": _pad_in(m_b_in), "w_pool_mix": m_w_pool_mix, "b_pool_mix": m_b_pool_mix,
               "pool_scale": m_pool_scale, "b_out": m_b_out, "ln_g": m_ln_g, "ln_b": m_ln_b, "loss": no_param,
               "b_ada": m_b_ada}
    small_v = {"b_in": _pad_in(v_b_in), "w_pool_mix": v_w_pool_mix, "b_pool_mix": v_b_pool_mix,
               "pool_scale": v_pool_scale, "b_out": v_b_out, "ln_g": v_ln_g, "ln_b": v_ln_b, "loss": no_param,
               "b_ada": v_b_ada}
    big = _adamw([(w_ada[0], g_w_ada, m_w_ada[0], v_w_ada[0]),
                  (w_out[0], g_w_out, m_w_out[0], v_w_out[0])], 8)
    g_w_in_cols = to_cols(g_w_in[None])
    big_in = _adamw([(to_cols(w_in), g_w_in_cols, to_cols(m_w_in), to_cols(v_w_in))], 14)
    sm = _adamw([(_pack_small(small_w), grads_small, _pack_small(small_m), _pack_small(small_v))], 1)

    names = ["w_ada", "b_ada", "w_in", "b_in", "w_pool_mix", "b_pool_mix", "pool_scale", "w_out", "b_out",
             "ln_g", "ln_b"]
    shapes = {"b_ada": (1, 3 * D), "b_in": (1, D_PAD), "w_pool_mix": (1, 4, POOL_GROUP, POOL_GROUP),
              "b_pool_mix": (1, 4, POOL_GROUP), "pool_scale": (1, D_POOL), "b_out": (1, D), "ln_g": (1, D),
              "ln_b": (1, D)}
    big_idx = {"w_ada": 0, "w_out": 1}

    def leaf(kind, name):
        if name == "w_in":
            return from_cols(g_w_in_cols if kind == 0 else big_in[kind - 1])
        if name in big_idx:
            if kind == 0:
                return (g_w_ada, g_w_out)[big_idx[name]][None]
            return big[3 * big_idx[name] + kind - 1][None]
        buf = grads_small if kind == 0 else sm[kind - 1]
        val = _unpack_small(buf, name, shapes[name])
        if name == "b_in":
            val = _unpad_in(val)
        return val

    outs = [loss, grad_x[None]]
    for kind in range(4):
        outs += [leaf(kind, n) for n in names]
    return tuple(outs)
```

```python
import functools

import numpy as np
import jax
import jax.numpy as jnp
from jax import lax
from jax.experimental import pallas as pl
from jax.experimental.pallas import tpu as pltpu

F32 = jnp.float32
BF16 = jnp.bfloat16
MESH = pl.DeviceIdType.MESH

D = 1024
D_ATT = 512
D_POOL = 512
N_HEADS = 8
HEAD_DIM = 64
N_PAIR = N_HEADS // 2
POOL_WINDOWS = (2, 4, 8, 16)
POOL_GROUP = 128
POOL_HALO = 16
LN_EPS = 1e-5
ALPHA = 2.0 ** 0.25
D_IN = 3 * D_ATT + N_HEADS + D_POOL + D_ATT + D_POOL
N_CHIPS = 4
SHARD_IN = D_IN // N_CHIPS
SHARD_ADA = 3 * D // N_CHIPS
SHARD_OUT = D // N_CHIPS

O_QKV, O_F, O_P, O_G, D_PAD = 0, 1536, 1664, 2176, 3200
Q_SCALE = HEAD_DIM ** -0.5

ADAM_LR, ADAM_B1, ADAM_B2, ADAM_EPS, ADAM_WD, ADAM_STEP = 0.001, 0.9, 0.999, 1e-08, 0.01, 10

NEG = -1e30

VMEM_LIMIT = 56 * 1024 * 1024

TM_PROJ = 512
T_ATT = 512
TM_MID = 256
TM_GW = 1024
TM_DU = 512

REL7 = [(0, 0, 1), (0, 1, 0), (0, 1, 1), (1, 0, 0), (1, 0, 1), (1, 1, 0), (1, 1, 1)]
REL3 = [(0, 1), (1, 0), (1, 1)]

SMALL_SEGS = {}
_row = 0
for _name, _n in (("b_in", 3200), ("w_pool_mix", 65536), ("b_pool_mix", 512), ("pool_scale", 512),
                  ("b_out", 1024), ("ln_g", 1024), ("ln_b", 1024), ("loss", 1)):
    _rows = -(-_n // 1024) * 8
    SMALL_SEGS[_name] = (_row, _rows)
    _row += _rows
SMALL_REDUCED_ROWS = -(-_row // 16) * 16
SMALL_SEGS["b_ada"] = (SMALL_REDUCED_ROWS, 24)
SMALL_ROWS = SMALL_REDUCED_ROWS + 24


def _params(**kw):
    return pltpu.CompilerParams(vmem_limit_bytes=VMEM_LIMIT, **kw)


def _flip(v, d):
    return v if d == 0 else 1 - v


def _dot(a, b):
    return jnp.dot(a, b, preferred_element_type=F32)


def _dot_nt(a, b):
    return lax.dot_general(a, b, (((1,), (1,)), ((), ())), preferred_element_type=F32)


def _dot_tn(a, b):
    return lax.dot_general(a, b, (((0,), (0,)), ((), ())), preferred_element_type=F32)


def _sigmoid(v):
    return 1.0 / (1.0 + jnp.exp(-v))


def _colsum(v):
    return jnp.sum(v, axis=0, keepdims=True)


def _gather_stages(pos, src_ref, dst_ref, half, own_sem, s_sem, r_sem, fs_sem, fr_sem):
    x, y, cc, chip, sib = pos
    own = pltpu.make_async_copy(src_ref, dst_ref.at[chip], own_sem)
    first, landed, others = [], [], []
    for k, (dx, dy) in enumerate(REL3):
        px, py = _flip(x, dx), _flip(y, dy)
        first.append(pltpu.make_async_remote_copy(
            src_ref=src_ref.at[half(cc)], dst_ref=dst_ref.at[(chip,) + half(cc)],
            send_sem=s_sem.at[k], recv_sem=r_sem.at[k], device_id=(px, py, cc), device_id_type=MESH))
        landed.append(dst_ref.at[(2 * px + py,) + half(cc)])
        others.append(dst_ref.at[(2 * px + py,) + half(1 - cc)])
    passed = [pltpu.make_async_remote_copy(src_ref=landed[k], dst_ref=landed[k], send_sem=fs_sem.at[k],
                                           recv_sem=fr_sem.at[k], device_id=sib, device_id_type=MESH)
              for k in range(3)]

    def start():
        own.start()
        for cp in first:
            cp.start()

    def forward():
        for k in range(3):
            pltpu.make_async_remote_copy(src_ref=landed[k], dst_ref=landed[k], send_sem=s_sem.at[k],
                                         recv_sem=r_sem.at[k], device_id=sib, device_id_type=MESH).wait_recv()
            passed[k].start()

    def finish():
        for k in range(3):
            pltpu.make_async_remote_copy(src_ref=others[k], dst_ref=others[k], send_sem=fs_sem.at[k],
                                         recv_sem=fr_sem.at[k], device_id=sib, device_id_type=MESH).wait_recv()
        for cp in first + passed:
            cp.wait_send()
        own.wait()

    return start, forward, finish


def _gather_scratch():
    return [pltpu.SemaphoreType.DMA, pltpu.SemaphoreType.DMA((3,)), pltpu.SemaphoreType.DMA((3,)),
            pltpu.SemaphoreType.DMA((3,)), pltpu.SemaphoreType.DMA((3,))]


def _gather_and_ada(c, w_ada, b_ada4, w_in_sh):
    def body(c_ref, w_ref, b_ref, win_ref, call_ref, ada_ref, win_all,
             cslab, sbuf, rbuf, cs_sem, cr_sem, as_sem, ar_sem, *gather_sems):
        x, y, cc = lax.axis_index("x"), lax.axis_index("y"), lax.axis_index("c")
        me = 4 * x + 2 * y + cc
        chip = 2 * x + y
        lane_half = lambda which: (slice(None), pl.ds(pl.multiple_of(which * (D // 2), D // 2), D // 2))
        start, forward, finish = _gather_stages((x, y, cc, chip, (x, y, 1 - cc)), win_ref, win_all, lane_half,
                                                *gather_sems)
        start()

        cslab[...] = jnp.broadcast_to(c_ref[...], (8, D))
        call_ref[me] = cslab[...]
        gathers = []
        for k, (dx, dy, dc) in enumerate(REL7):
            cp = pltpu.make_async_remote_copy(
                src_ref=cslab, dst_ref=call_ref.at[me], send_sem=cs_sem.at[k], recv_sem=cr_sem.at[k],
                device_id=(_flip(x, dx), _flip(y, dy), _flip(cc, dc)), device_id_type=MESH)
            cp.start()
            gathers.append(cp)
        for cp in gathers:
            cp.wait()
        slab_row = lax.broadcasted_iota(jnp.int32, (8, 1), 0)
        mat = jnp.zeros((8, D), F32)
        for r in range(8):
            mat = jnp.where(slab_row == r, call_ref[r], mat)
        act = (mat * _sigmoid(mat)).astype(BF16)
        part = _dot(act, w_ref[...].astype(BF16))
        sends = []
        for k, (dx, dy) in enumerate(REL3):
            px, py = _flip(x, dx), _flip(y, dy)
            r = 4 * px + 2 * py + cc
            piece = _colsum(jnp.where(slab_row == r, part, 0.0))
            sbuf[k] = jnp.broadcast_to(piece, (8, SHARD_ADA))
            cp = pltpu.make_async_remote_copy(
                src_ref=sbuf.at[k], dst_ref=rbuf.at[k], send_sem=as_sem.at[k], recv_sem=ar_sem.at[k],
                device_id=(px, py, cc), device_id_type=MESH)
            cp.start()
            sends.append(cp)
        own_piece = _colsum(jnp.where(slab_row == me, part, 0.0))
        ada_ref[chip] = jnp.broadcast_to(own_piece, (8, SHARD_ADA)) + b_ref[chip]
        for k, (dx, dy) in enumerate(REL3):
            sends[k].wait()
            a = 2 * _flip(x, dx) + _flip(y, dy)
            ada_ref[a] = rbuf[k] + b_ref[a]

        forward()
        finish()

    vm = pl.BlockSpec(memory_space=pltpu.VMEM)
    return pl.pallas_call(
        body, name="gather_and_ada",
        out_shape=(jax.ShapeDtypeStruct((8, 8, D), F32), jax.ShapeDtypeStruct((4, 8, SHARD_ADA), F32),
                   jax.ShapeDtypeStruct((N_CHIPS, SHARD_IN, D), BF16)),
        in_specs=[vm] * 4, out_specs=(vm,) * 3,
        scratch_shapes=[pltpu.VMEM((8, D), F32), pltpu.VMEM((3, 8, SHARD_ADA), F32),
                        pltpu.VMEM((3, 8, SHARD_ADA), F32),
                        pltpu.SemaphoreType.DMA((7,)), pltpu.SemaphoreType.DMA((7,)),
                        pltpu.SemaphoreType.DMA((3,)), pltpu.SemaphoreType.DMA((3,))] + _gather_scratch(),
        compiler_params=_params(),
    )(c, w_ada, b_ada4, w_in_sh)


def _scatter_stages(pos, g_ref, sc_ref, out_ref, sib_buf, send_buf, ici_buf, sem1, sem2s, sem2r, sem3):
    x, y, cc, chip, sib = pos
    RH = g_ref.shape[1] // 2
    mine = pl.ds(pl.multiple_of(cc * RH, RH), RH)
    theirs = pl.ds(pl.multiple_of((1 - cc) * RH, RH), RH)
    cp1 = pltpu.make_async_remote_copy(
        src_ref=g_ref.at[:, theirs, :], dst_ref=sib_buf, send_sem=sem1.at[0], recv_sem=sem1.at[1],
        device_id=sib, device_id_type=MESH)
    sends = []
    for k, (dx, dy) in enumerate(REL3):
        px, py = _flip(x, dx), _flip(y, dy)
        sends.append(pltpu.make_async_remote_copy(
            src_ref=send_buf.at[2 * px + py], dst_ref=ici_buf.at[chip],
            send_sem=sem2s.at[k], recv_sem=sem2r.at[k], device_id=(px, py, cc), device_id_type=MESH))
    cp3 = pltpu.make_async_remote_copy(
        src_ref=out_ref.at[mine, :], dst_ref=out_ref.at[mine, :], send_sem=sem3.at[0], recv_sem=sem3.at[1],
        device_id=sib, device_id_type=MESH)

    def finish1():
        cp1.wait()
        for a in range(N_CHIPS):
            both = g_ref[a, mine, :] + sib_buf[a]
            sib_buf[a] = both
            send_buf[a] = both.astype(BF16)

    def start2():
        for cp in sends:
            cp.start()
        ici_buf[chip] = send_buf[chip]

    def finish2():
        for cp in sends:
            cp.wait()
        own = sib_buf[chip]
        parts = [jnp.where(chip == a, own, ici_buf[a].astype(F32)) for a in range(N_CHIPS)]
        out_ref[mine, :] = ((parts[0] + parts[1]) + (parts[2] + parts[3])) * sc_ref[chip]

    return [(cp1.start, finish1), (start2, finish2), (cp3.start, cp3.wait)]


def _all_reduce_stages(pos, g_ref, out_ref, sib_buf, ici_buf, sem1, sem2s, sem2r, sem3):
    x, y, cc, chip, sib = pos
    RH = g_ref.shape[0] // 2
    mine = pl.ds(pl.multiple_of(cc * RH, 8), RH)
    theirs = pl.ds(pl.multiple_of((1 - cc) * RH, 8), RH)
    cp1 = pltpu.make_async_remote_copy(
        src_ref=g_ref.at[theirs, :], dst_ref=sib_buf, send_sem=sem1.at[0], recv_sem=sem1.at[1],
        device_id=sib, device_id_type=MESH)
    sends = []
    for k, (dx, dy) in enumerate(REL3):
        px, py = _flip(x, dx), _flip(y, dy)
        sends.append(pltpu.make_async_remote_copy(
            src_ref=sib_buf, dst_ref=ici_buf.at[chip],
            send_sem=sem2s.at[k], recv_sem=sem2r.at[k], device_id=(px, py, cc), device_id_type=MESH))
    cp3 = pltpu.make_async_remote_copy(
        src_ref=out_ref.at[mine, :], dst_ref=out_ref.at[mine, :], send_sem=sem3.at[0], recv_sem=sem3.at[1],
        device_id=sib, device_id_type=MESH)

    def finish1():
        cp1.wait()
        sib_buf[...] = g_ref[mine, :] + sib_buf[...]

    def start2():
        for cp in sends:
            cp.start()
        ici_buf[chip] = sib_buf[...]

    def finish2():
        for cp in sends:
            cp.wait()
        out_ref[mine, :] = (ici_buf[0] + ici_buf[1]) + (ici_buf[2] + ici_buf[3])

    return [(cp1.start, finish1), (start2, finish2), (cp3.start, cp3.wait)]


def _stage_sems():
    return [pltpu.SemaphoreType.DMA((2,)), pltpu.SemaphoreType.DMA((3,)),
            pltpu.SemaphoreType.DMA((3,)), pltpu.SemaphoreType.DMA((2,))]


def _scatter_scratch(r, c):
    return [pltpu.VMEM((N_CHIPS, r // 2, c), F32), pltpu.VMEM((N_CHIPS, r // 2, c), BF16),
            pltpu.VMEM((N_CHIPS, r // 2, c), BF16)] + _stage_sems()


def _reduce_all(g4_in, sc_in, small, dada):
    R = small.shape[0]
    W = dada.shape[1]
    n_in = len(_scatter_scratch(*g4_in.shape[1:]))

    def body(gin_ref, scin_ref, sm_ref, d_ref, oin_ref, osm_ref, dall_ref, *scratch):
        x, y, cc = lax.axis_index("x"), lax.axis_index("y"), lax.axis_index("c")
        me = 4 * x + 2 * y + cc
        pos = (x, y, cc, 2 * x + y, (x, y, 1 - cc))
        dslab, ds_sem, dr_sem = scratch[0:3]
        in_bufs, sm_bufs = scratch[3:3 + n_in], scratch[3 + n_in:]
        dslab[...] = jnp.broadcast_to(d_ref[...], (8, W))
        dall_ref[me] = dslab[...]
        gathers = []
        for k, (dx, dy, dc) in enumerate(REL7):
            cp = pltpu.make_async_remote_copy(
                src_ref=dslab, dst_ref=dall_ref.at[me], send_sem=ds_sem.at[k], recv_sem=dr_sem.at[k],
                device_id=(_flip(x, dx), _flip(y, dy), _flip(cc, dc)), device_id_type=MESH)
            cp.start()
            gathers.append(cp)
        plans = [_scatter_stages(pos, gin_ref, scin_ref, oin_ref, *in_bufs),
                 _all_reduce_stages(pos, sm_ref, osm_ref, *sm_bufs)]
        for stage in range(3):
            for plan in plans:
                plan[stage][0]()
            for plan in plans:
                plan[stage][1]()
        for cp in gathers:
            cp.wait()

    scratch = [pltpu.VMEM((8, W), F32), pltpu.SemaphoreType.DMA((7,)), pltpu.SemaphoreType.DMA((7,))]
    scratch += _scatter_scratch(*g4_in.shape[1:])
    scratch += [pltpu.VMEM((R // 2, 128), F32), pltpu.VMEM((N_CHIPS, R // 2, 128), F32)] + _stage_sems()
    vm = pl.BlockSpec(memory_space=pltpu.VMEM)
    return pl.pallas_call(
        body, name="reduce_all",
        out_shape=(jax.ShapeDtypeStruct(g4_in.shape[1:], F32),
                   jax.ShapeDtypeStruct((R, 128), F32), jax.ShapeDtypeStruct((8, 8, W), F32)),
        in_specs=[vm] * 4, out_specs=(vm,) * 3,
        scratch_shapes=scratch,
        compiler_params=_params(),
    )(g4_in, sc_in, small, dada)


def _in_proj(x, shift, scale, wt_pad, b_pad, w_out_sh):
    S = x.shape[0]
    tm = min(TM_PROJ, S)
    n_steps = S // tm
    assert n_steps >= 3

    def body(x_ref, sh_ref, sc_ref, w_ref, b_ref, wo_ref, u_ref, qkv_ref, f_ref, p_ref, g_ref, wo_all,
             wo_buf, *gather_sems):
        i = pl.program_id(0)
        xx, yy, cc = lax.axis_index("x"), lax.axis_index("y"), lax.axis_index("c")
        row_half = lambda which: (pl.ds(pl.multiple_of(which * (SHARD_OUT // 2), SHARD_OUT // 2), SHARD_OUT // 2),
                                  slice(None))
        start, forward, finish = _gather_stages((xx, yy, cc, 2 * xx + yy, (xx, yy, 1 - cc)), wo_ref, wo_buf,
                                                row_half, *gather_sems)
        pl.when(i == 0)(start)
        pl.when(i == n_steps // 2)(forward)

        @pl.when(i == n_steps - 1)
        def _():
            finish()
            wo_all[...] = wo_buf[...]

        u = (x_ref[...] * (1.0 + sc_ref[...]) + sh_ref[...]).astype(BF16)
        u_ref[...] = u
        qkv_ref[...] = (_dot_nt(u, w_ref[O_QKV:O_F, :]) + b_ref[:, O_QKV:O_F]).astype(BF16)
        f_ref[...] = _dot_nt(u, w_ref[O_F:O_P, :]) + b_ref[:, O_F:O_P]
        p_ref[...] = _dot_nt(u, w_ref[O_P:O_G, :]) + b_ref[:, O_P:O_G]
        g_ref[...] = _dot_nt(u, w_ref[O_G:D_PAD, :]) + b_ref[:, O_G:D_PAD]

    row = lambda w: pl.BlockSpec((tm, w), lambda i: (i, 0))
    full = lambda a: pl.BlockSpec(a.shape, lambda i: (0, 0))
    vm = pl.BlockSpec(memory_space=pltpu.VMEM)
    return pl.pallas_call(
        body, name="in_proj", grid=(n_steps,),
        out_shape=(jax.ShapeDtypeStruct((S, D), BF16), jax.ShapeDtypeStruct((S, 3 * D_ATT), BF16),
                   jax.ShapeDtypeStruct((S, 128), F32), jax.ShapeDtypeStruct((S, D_POOL), F32),
                   jax.ShapeDtypeStruct((S, D), F32), jax.ShapeDtypeStruct((N_CHIPS,) + w_out_sh.shape, BF16)),
        in_specs=[row(D), full(shift), full(scale), full(wt_pad), full(b_pad), vm],
        out_specs=(row(D), row(3 * D_ATT), row(128), row(D_POOL), row(D), vm),
        scratch_shapes=[pltpu.VMEM((N_CHIPS,) + w_out_sh.shape, BF16)] + _gather_scratch(),
        compiler_params=_params(dimension_semantics=("arbitrary",)),
    )(x, shift, scale, wt_pad, b_pad, w_out_sh)


def _forget_cumsum(f):
    S = f.shape[0]
    tm = min(T_ATT, S)

    def body(f_ref, out_ref, carry):
        @pl.when(pl.program_id(0) == 0)
        def _():
            carry[...] = jnp.zeros_like(carry)
        v = f_ref[...]
        logf = jnp.minimum(v, 0.0) - jnp.log(1.0 + jnp.exp(-jnp.abs(v)))
        r = lax.broadcasted_iota(jnp.int32, (tm, tm), 0)
        c = lax.broadcasted_iota(jnp.int32, (tm, tm), 1)
        tri = (r <= c).astype(F32)
        rows8 = logf.T[0:8, :]
        cum8 = jnp.dot(rows8, tri, preferred_element_type=F32, precision=lax.Precision.HIGHEST) + carry[...]
        out_ref[...] = jnp.concatenate([cum8, jnp.zeros((128 - 8, tm), F32)], axis=0).T
        last = lax.broadcasted_iota(jnp.int32, (1, tm), 1) == tm - 1
        carry[...] = jnp.sum(jnp.where(last, cum8, 0.0), axis=1, keepdims=True)

    return pl.pallas_call(
        body, name="forget_cumsum", grid=(S // tm,),
        out_shape=jax.ShapeDtypeStruct((S, 128), F32),
        in_specs=[pl.BlockSpec((tm, 128), lambda i: (i, 0))],
        out_specs=pl.BlockSpec((tm, 128), lambda i: (i, 0)),
        scratch_shapes=[pltpu.VMEM((8, 1), F32)],
        compiler_params=_params(dimension_semantics=("arbitrary",)),
    )(f)


def _split3(v):
    hi = v.astype(BF16)
    rest = v - hi.astype(F32)
    mid = rest.astype(BF16)
    lo = (rest - mid.astype(F32)).astype(BF16)
    return hi, mid, lo


def _attention_fwd(qkv, big_f):
    S = qkv.shape[0]
    T = min(T_ATT, S)
    n_t = S // T

    def body(q_ref, k_ref, v_ref, f_ref, o_ref, lse_ref, kaug_sc, vt_sc, m_sc, l_sc, acc_sc):
        hp = pl.program_id(0)
        i = pl.program_id(1)
        lane = lax.broadcasted_iota(jnp.int32, (1, 128), 1)
        sub = lax.broadcasted_iota(jnp.int32, (128, 1), 0)
        head_sel = (lane < HEAD_DIM, lane >= HEAD_DIM)
        head_sel_t = (sub < HEAD_DIM, sub >= HEAD_DIM)
        spare = (HEAD_DIM, 0)
        zero = jnp.zeros((), BF16)

        @pl.when(i == 0)
        def _():
            def prep(jt, carry):
                rows = pl.ds(pl.multiple_of(jt * T, T), T)
                k = k_ref[rows, :]
                ft = f_ref[rows, :]
                vt = v_ref[rows, :].astype(F32).T
                for h in range(2):
                    fh = jnp.sum(jnp.where(lane == 2 * hp + h, ft, 0.0), axis=1, keepdims=True)
                    hi, mid, lo = _split3(-fh)
                    b = spare[h]
                    bias = jnp.where(lane == b, hi, jnp.where(lane == b + 1, mid, jnp.where(lane == b + 2, lo, zero)))
                    kaug_sc[h, rows, :] = jnp.where(head_sel[h], k, bias)
                    vt_sc[h, jt] = jnp.where(head_sel_t[h], vt, 0.0).astype(BF16)
                return carry

            lax.fori_loop(0, n_t, prep, 0)

        q = q_ref[...]
        q_heads = []
        for h in range(2):
            ones = jnp.where((lane >= spare[h]) & (lane < spare[h] + 3), jnp.ones((), BF16), zero)
            q_heads.append(jnp.where(head_sel[h], q, ones))
        m_sc[...] = jnp.full((8, T), NEG, F32)
        l_sc[...] = jnp.zeros((8, T), F32)
        acc_sc[...] = jnp.zeros((128, T), F32)

        def update(j, k_lo, n_k, q_lo, masked):
            rows = pl.ds(pl.multiple_of(j * T + k_lo, n_k), n_k)
            n_q = T - q_lo
            alphas, pvs = [], []
            for h in range(2):
                s_t = _dot_nt(kaug_sc[h, rows, :], q_heads[h][q_lo:, :])
                if masked:
                    rr = lax.broadcasted_iota(jnp.int32, (n_k, n_q), 0) + k_lo
                    cc = lax.broadcasted_iota(jnp.int32, (n_k, n_q), 1) + q_lo
                    s_t = jnp.where(rr <= cc, s_t, NEG)
                m_prev = m_sc[h:h + 1, q_lo:]
                m_new = jnp.maximum(m_prev, jnp.max(s_t, axis=0, keepdims=True))
                alpha = jnp.exp(m_prev - m_new)
                p_t = jnp.exp(s_t - m_new)
                l_sc[h:h + 1, q_lo:] = alpha * l_sc[h:h + 1, q_lo:] + jnp.sum(p_t, axis=0, keepdims=True)
                m_sc[h:h + 1, q_lo:] = m_new
                alphas.append(alpha)
                pvs.append(_dot(vt_sc[h, j, :, k_lo:k_lo + n_k], p_t.astype(BF16)))
            acc_sc[:, q_lo:] = (acc_sc[:, q_lo:] * jnp.where(head_sel_t[0], alphas[0], alphas[1])
                                + (pvs[0] + pvs[1]))

        def two_off_diagonal(jj, carry):
            update(2 * jj, 0, T, 0, False)
            update(2 * jj + 1, 0, T, 0, False)
            return carry

        lax.fori_loop(0, i // 2, two_off_diagonal, 0)

        @pl.when(i % 2 == 1)
        def _():
            update(i - 1, 0, T, 0, False)

        update(i, 0, T, 0, True)
        l = l_sc[...]
        o_ref[...] = (acc_sc[...] / jnp.where(head_sel_t[0], l[0:1, :], l[1:2, :])).T
        is_head = lax.broadcasted_iota(jnp.int32, (8, 1), 0) < 2
        lse_ref[...] = jnp.where(is_head, m_sc[...] + jnp.log(jnp.where(is_head, l, 1.0)), 0.0)

    return pl.pallas_call(
        body, name="attention_fwd", grid=(N_PAIR, n_t),
        out_shape=(jax.ShapeDtypeStruct((S, D_ATT), F32), jax.ShapeDtypeStruct((N_PAIR, n_t, 8, T), F32)),
        in_specs=[pl.BlockSpec((T, 128), lambda hp, i: (i, hp)),
                  pl.BlockSpec((S, 128), lambda hp, i: (0, N_PAIR + hp)),
                  pl.BlockSpec((S, 128), lambda hp, i: (0, 2 * N_PAIR + hp)),
                  pl.BlockSpec((S, 128), lambda hp, i: (0, 0))],
        out_specs=(pl.BlockSpec((T, 128), lambda hp, i: (i, hp)),
                   pl.BlockSpec((None, None, 8, T), lambda hp, i: (hp, i, 0, 0))),
        scratch_shapes=[pltpu.VMEM((2, S, 128), BF16), pltpu.VMEM((2, n_t, 128, T), BF16),
                        pltpu.VMEM((8, T), F32), pltpu.VMEM((8, T), F32), pltpu.VMEM((128, T), F32)],
        compiler_params=_params(dimension_semantics=("arbitrary", "arbitrary")),
    )(qkv, qkv, qkv, big_f)


def _attention_bwd(qkv, datt, att, lse, big_f, gw_out4, sc_out):
    S = qkv.shape[0]
    T = min(T_ATT, S)
    n_t = S // T
    n_steps = N_PAIR * n_t
    marks = (0, n_steps // 8, n_steps // 2, n_steps // 2 + n_steps // 8)

    def body(q_ref, do_ref, o_ref, lse_ref, k_ref, v_ref, fk_ref, gout_ref, scout_ref,
             dq_ref, dk_ref, dv_ref, cs_ref, dfk_ref, dfq_ref, oout_ref, stat_sc, dqt_sc, qaug_sc,
             out_buf, *red_bufs):
        hp = pl.program_id(0)
        j = pl.program_id(1)
        x, y, cc = lax.axis_index("x"), lax.axis_index("y"), lax.axis_index("c")
        plan = _scatter_stages((x, y, cc, 2 * x + y, (x, y, 1 - cc)), gout_ref, scout_ref, out_buf, *red_bufs)
        step = hp * n_t + j
        for n, mark in enumerate(marks):
            @pl.when(step == mark)
            def _(n=n):
                if n > 0:
                    plan[n - 1][1]()
                if n < 3:
                    plan[n][0]()
                else:
                    oout_ref[...] = out_buf[...]

        lane = lax.broadcasted_iota(jnp.int32, (1, 128), 1)
        sub = lax.broadcasted_iota(jnp.int32, (128, 1), 0)
        head_sel = (lane < HEAD_DIM, lane >= HEAD_DIM)
        head_sel_t = (sub < HEAD_DIM, sub >= HEAD_DIM)
        spare = (HEAD_DIM, 0)
        zero = jnp.zeros((), BF16)
        one = jnp.ones((), BF16)

        def bias_lanes(first, pieces):
            hi, mid, lo = pieces
            return lambda rest: jnp.where(lane == first, hi, jnp.where(lane == first + 1, mid,
                                                                        jnp.where(lane == first + 2, lo, rest)))

        @pl.when(j == 0)
        def _():
            dqt_sc[...] = jnp.zeros_like(dqt_sc)
            cs_ref[...] = jnp.zeros_like(cs_ref)
            dfq_ref[...] = jnp.zeros_like(dfq_ref)

            def prep(i, carry):
                rows = pl.ds(pl.multiple_of(i * T, T), T)
                q = q_ref[rows, :]
                do = do_ref[rows, :]
                prod = o_ref[rows, :] * do.astype(F32)
                d_a = jnp.sum(jnp.where(head_sel[0], prod, 0.0), axis=1, keepdims=True)
                d_b = jnp.sum(jnp.where(head_sel[0], 0.0, prod), axis=1, keepdims=True)
                delta_t = jnp.where(head_sel[0], d_a, d_b).T
                stat_sc[i, 0:1, :] = delta_t[0:1, :]
                stat_sc[i, 1:2, :] = delta_t[HEAD_DIM:HEAD_DIM + 1, :]
                lse = lse_ref[i]
                lse_cols = jnp.where(head_sel_t[0], lse[0:1, :], lse[1:2, :]).T
                for h in range(2):
                    neg_lse = -lse_cols[:, h * HEAD_DIM:h * HEAD_DIM + 1]
                    ones = jnp.where((lane >= spare[h]) & (lane < spare[h] + 3), one, zero)
                    qaug_sc[h, rows, :] = jnp.where(head_sel[h], q, bias_lanes(spare[h] + 3, _split3(neg_lse))(ones))
                return carry

            lax.fori_loop(0, n_t, prep, 0)

        k = k_ref[...]
        v = v_ref[...]
        fk = fk_ref[...]
        kt = k.astype(F32).T
        heads = []
        for h in range(2):
            fkh = jnp.sum(jnp.where(lane == 2 * hp + h, fk, 0.0), axis=1, keepdims=True)
            ones = jnp.where((lane >= spare[h] + 3) & (lane < spare[h] + 6), one, zero)
            kaug = jnp.where(head_sel[h], k, bias_lanes(spare[h], _split3(-fkh))(ones))
            heads.append((kaug, jnp.where(head_sel[h], v, zero), jnp.where(head_sel_t[h], kt, 0.0).astype(BF16)))

        def block(i, k_lo, n_k, q_lo, masked):
            n_q = T - q_lo
            rows = pl.ds(pl.multiple_of(i * T + q_lo, n_q), n_q)
            q = q_ref[rows, :]
            do = do_ref[rows, :]
            stat = stat_sc[i]
            dk = jnp.zeros((n_k, 128), F32)
            dv = jnp.zeros((n_k, 128), F32)
            dqt = jnp.zeros((128, n_q), F32)
            dfs = []
            for h in range(2):
                kaug, vh, kth = heads[h]
                arg = _dot_nt(kaug[k_lo:k_lo + n_k, :], qaug_sc[h, rows, :])
                if masked:
                    rr = lax.broadcasted_iota(jnp.int32, (n_k, n_q), 0) + k_lo
                    cc = lax.broadcasted_iota(jnp.int32, (n_k, n_q), 1) + q_lo
                    arg = jnp.where(rr <= cc, arg, NEG)
                p_t = jnp.exp(arg)
                ds_t = p_t * (_dot_nt(vh[k_lo:k_lo + n_k, :], do) - stat[h:h + 1, q_lo:])
                ds_bf = ds_t.astype(BF16)
                dv = dv + _dot(p_t.astype(BF16), jnp.where(head_sel[h], do, zero))
                dk = dk + _dot(ds_bf, jnp.where(head_sel[h], q, zero))
                dqt = dqt + _dot(kth[:, k_lo:k_lo + n_k], ds_bf)
                dfs.append(jnp.sum(ds_t, axis=1, keepdims=True))
                dfq_ref[i, h:h + 1, q_lo:] += _colsum(ds_t)
            dqt_sc[i, :, q_lo:] += dqt
            return dk, dv, dfs[0], dfs[1]

        def off_diagonal(i, acc):
            return tuple(a + b for a, b in zip(acc, block(i, 0, T, 0, False)))

        half = T // 2
        early = block(j, 0, half, 0, True)
        late = block(j, half, half, half, True)
        acc1 = tuple(jnp.concatenate([a, b], axis=0) for a, b in zip(early, late))
        n_off = n_t - 1 - j
        acc2 = lax.fori_loop(0, n_off // 2,
                             lambda ii, a: off_diagonal(j + 2 + 2 * ii, off_diagonal(j + 1 + 2 * ii, a)), acc1)
        dk_acc, dv_acc, dfa, dfb = lax.fori_loop(0, n_off % 2, lambda _, a: off_diagonal(n_t - 1, a), acc2)
        dk_ref[...] = dk_acc.astype(BF16)
        dv_ref[...] = dv_acc.astype(BF16)
        dfk_ref[...] = -jnp.where(lane == 0, dfa, jnp.where(lane == 1, dfb, 0.0))
        cs_ref[:, 128:256] = cs_ref[:, 128:256] + _colsum(dk_acc)
        cs_ref[:, 256:384] = cs_ref[:, 256:384] + _colsum(dv_acc)

        @pl.when(j == n_t - 1)
        def _():
            def finish(i, tot):
                dq = dqt_sc[i].T
                dq_ref[pl.ds(pl.multiple_of(i * T, T), T), :] = dq.astype(BF16)
                return tot + _colsum(dq)

            cs_ref[:, 0:128] = lax.fori_loop(0, n_t, finish, jnp.zeros((1, 128), F32))

    pair_rows = lambda hp, j: (hp, 0, 0)
    vm = pl.BlockSpec(memory_space=pltpu.VMEM)
    _, r_out, c_out = gw_out4.shape
    return pl.pallas_call(
        body, name="attention_bwd", grid=(N_PAIR, n_t),
        out_shape=(jax.ShapeDtypeStruct((S, D_ATT), BF16), jax.ShapeDtypeStruct((S, D_ATT), BF16),
                   jax.ShapeDtypeStruct((S, D_ATT), BF16), jax.ShapeDtypeStruct((N_PAIR, 1, 384), F32),
                   jax.ShapeDtypeStruct((N_PAIR, S, 128), F32),
                   jax.ShapeDtypeStruct((N_PAIR, n_t, 8, T), F32),
                   jax.ShapeDtypeStruct((r_out, c_out), F32)),
        in_specs=[pl.BlockSpec((S, 128), lambda hp, j: (0, hp)),
                  pl.BlockSpec((S, 128), lambda hp, j: (0, hp)),
                  pl.BlockSpec((S, 128), lambda hp, j: (0, hp)),
                  pl.BlockSpec((None, n_t, 8, T), lambda hp, j: (hp, 0, 0, 0)),
                  pl.BlockSpec((T, 128), lambda hp, j: (j, N_PAIR + hp)),
                  pl.BlockSpec((T, 128), lambda hp, j: (j, 2 * N_PAIR + hp)),
                  pl.BlockSpec((T, 128), lambda hp, j: (j, 0)),
                  vm, vm],
        out_specs=(pl.BlockSpec((S, 128), lambda hp, j: (0, hp)),
                   pl.BlockSpec((T, 128), lambda hp, j: (j, hp)),
                   pl.BlockSpec((T, 128), lambda hp, j: (j, hp)),
                   pl.BlockSpec((None, 1, 384), pair_rows),
                   pl.BlockSpec((None, T, 128), lambda hp, j: (hp, j, 0)),
                   pl.BlockSpec((None, n_t, 8, T), lambda hp, j: (hp, 0, 0, 0)),
                   vm),
        scratch_shapes=[pltpu.VMEM((n_t, 8, T), F32), pltpu.VMEM((n_t, 128, T), F32),
                        pltpu.VMEM((2, S, 128), BF16), pltpu.VMEM((r_out, c_out), F32)]
        + _scatter_scratch(r_out, c_out),
        compiler_params=_params(dimension_semantics=("arbitrary", "arbitrary")),
    )(qkv, datt, att, lse, qkv, qkv, big_f, gw_out4, sc_out)


def _window_counts(first_row, n_rows, window):
    t = lax.broadcasted_iota(jnp.int32, (n_rows, 1), 0) + first_row
    return jnp.minimum((t + 1).astype(F32), float(window))


def _middle(x, tgt, att, g, p, gate, w_mix, b_mix, pool_scale, w_out, b_out, ln_g, ln_b):
    S = x.shape[0]
    tm = min(TM_MID, S)
    halo_blocks = tm // POOL_HALO

    def body(x_ref, t_ref, att_ref, g_ref, p_ref, ph_ref, gate_ref, wm_ref, bm_ref, ps_ref, wo_ref, bo_ref,
             lg_ref, lb_ref,
             dh_ref, datt_ref, dg_ref, dpl_ref, gwo_ref, gwm_ref, vec_ref, loss_ref):
        i = pl.program_id(0)

        @pl.when(i == 0)
        def _():
            gwo_ref[...] = jnp.zeros_like(gwo_ref)
            gwm_ref[...] = jnp.zeros_like(gwm_ref)
            vec_ref[...] = jnp.zeros_like(vec_ref)
            loss_ref[...] = jnp.zeros_like(loss_ref)

        pc = p_ref[...]
        halo = jnp.where(i > 0, ph_ref[...], 0.0)
        pe = jnp.concatenate([halo, pc], axis=0)
        pooled_parts = []
        for gi, w in enumerate(POOL_WINDOWS):
            cur = pe[:, gi * POOL_GROUP:(gi + 1) * POOL_GROUP]
            span = 1
            while span < w:
                cur = cur + pltpu.roll(cur, span, 0)
                span *= 2
            wsum = cur[POOL_HALO:, :]
            mean = wsum / _window_counts(i * tm, tm, w)
            pooled_parts.append(mean - pc[:, gi * POOL_GROUP:(gi + 1) * POOL_GROUP])
        pooled_bf =[v.astype(BF16) for v in pooled_parts]
        mixed = jnp.concatenate([_dot(pooled_bf[gi], wm_ref[gi]) for gi in range(4)], axis=1) + bm_ref[...]
        ps = ps_ref[...]
        pool_out = mixed * ps
        gv = g_ref[...]
        sig = _sigmoid(gv)
        silu = gv * sig
        att = att_ref[...]
        y = jnp.concatenate([att * silu[:, :D_ATT], pool_out * silu[:, D_ATT:]], axis=1)
        y_bf = y.astype(BF16)
        wo = wo_ref[...]
        yo = _dot(y_bf, wo) + bo_ref[...]
        gate = gate_ref[...]
        h = ALPHA * x_ref[...] + gate * yo
        mu = jnp.mean(h, axis=1, keepdims=True)
        hc = h - mu
        var = jnp.mean(hc * hc, axis=1, keepdims=True)
        rstd = lax.rsqrt(var + LN_EPS)
        yhat = hc * rstd
        lg = lg_ref[...]
        out = yhat * lg + lb_ref[...]
        err = out - t_ref[...]
        loss_ref[...] += 0.5 * jnp.sum(jnp.mean(err * err, axis=1, keepdims=True), axis=0, keepdims=True)

        dout = err * (1.0 / D)
        g_ln_b = _colsum(dout)
        g_ln_g = _colsum(dout * yhat)
        dyh = dout * lg
        dh = rstd * (dyh - jnp.mean(dyh, axis=1, keepdims=True)
                     - yhat * jnp.mean(dyh * yhat, axis=1, keepdims=True))
        dh_ref[...] = dh
        d_gate = _colsum(dh * yo)
        dyo = gate * dh
        g_b_out = _colsum(dyo)
        dyo_bf = dyo.astype(BF16)
        gwo_ref[...] += _dot_tn(y_bf, dyo_bf)
        dy = _dot_nt(dyo_bf, wo)
        dsilu = sig * (1.0 + gv * (1.0 - sig))
        dy_a = dy[:, :D_ATT]
        dy_p = dy[:, D_ATT:]
        datt_ref[...] = (dy_a * silu[:, :D_ATT]).astype(BF16)
        dpo = dy_p * silu[:, D_ATT:]
        dg = jnp.concatenate([dy_a * att * dsilu[:, :D_ATT], dy_p * pool_out * dsilu[:, D_ATT:]], axis=1)
        dg_ref[...] = dg.astype(BF16)
        g_dg = _colsum(dg)
        g_ps = _colsum(dpo * mixed)
        dmixed = dpo * ps
        g_bm = _colsum(dmixed)
        dmixed_bf = dmixed.astype(BF16)
        dpl = []
        for gi in range(4):
            dm = dmixed_bf[:, gi * POOL_GROUP:(gi + 1) * POOL_GROUP]
            gwm_ref[gi] += _dot_tn(pooled_bf[gi], dm)
            dpl.append(_dot_nt(dm, wm_ref[gi]))
        dpl_ref[...] = jnp.concatenate(dpl, axis=1)
        vec_ref[0:1, :] += g_ln_g
        vec_ref[1:2, :] += g_ln_b
        vec_ref[2:3, :] += d_gate
        vec_ref[3:4, :] += g_b_out
        vec_ref[4:5, :] += g_dg
        vec_ref[5:6, 0:D_POOL] += g_ps
        vec_ref[6:7, 0:D_POOL] += g_bm

    row = lambda w: pl.BlockSpec((tm, w), lambda i: (i, 0))
    full2 = lambda a: pl.BlockSpec(a.shape, lambda i: (0, 0))
    full3 = lambda a: pl.BlockSpec(a.shape, lambda i: (0, 0, 0))
    return pl.pallas_call(
        body, name="middle", grid=(S // tm,),
        out_shape=(jax.ShapeDtypeStruct((S, D), F32),
                   jax.ShapeDtypeStruct((S, D_ATT), BF16),
                   jax.ShapeDtypeStruct((S, D), BF16),
                   jax.ShapeDtypeStruct((S, D_POOL), F32),
                   jax.ShapeDtypeStruct((D, D), F32),
                   jax.ShapeDtypeStruct((4, POOL_GROUP, POOL_GROUP), F32),
                   jax.ShapeDtypeStruct((8, D), F32),
                   jax.ShapeDtypeStruct((1, 1), F32)),
        in_specs=[row(D), row(D), row(D_ATT), row(D), row(D_POOL),
                  pl.BlockSpec((POOL_HALO, D_POOL), lambda i: (jnp.maximum(i * halo_blocks - 1, 0), 0)),
                  full2(gate), full3(w_mix), full2(b_mix), full2(pool_scale), full2(w_out), full2(b_out),
                  full2(ln_g), full2(ln_b)],
        out_specs=(row(D), row(D_ATT), row(D), row(D_POOL),
                   pl.BlockSpec((D, D), lambda i: (0, 0)),
                   pl.BlockSpec((4, POOL_GROUP, POOL_GROUP), lambda i: (0, 0, 0)),
                   pl.BlockSpec((8, D), lambda i: (0, 0)),
                   pl.BlockSpec((1, 1), lambda i: (0, 0))),
        compiler_params=_params(dimension_semantics=("arbitrary",)),
    )(x, tgt, att, g, p, p, gate, w_mix, b_mix, pool_scale, w_out, b_out, ln_g, ln_b)


def _tail(dpl, dfk, dfq, f):
    S = dpl.shape[0]
    tm = min(T_ATT, S)
    n_t = S // tm
    halo_blocks = tm // POOL_HALO
    last_halo = S // POOL_HALO - 1

    def body(d_ref, dn_ref, dfk_ref, dfq_ref, f_ref, dp_ref, df_ref, cs_ref, carry):
        s = pl.program_id(0)
        i = n_t - 1 - s

        @pl.when(s == 0)
        def _():
            carry[...] = jnp.zeros_like(carry)
            cs_ref[...] = jnp.zeros_like(cs_ref)

        dc = d_ref[...]
        nxt = jnp.where(s > 0, dn_ref[...], 0.0)
        de = jnp.concatenate([dc, nxt], axis=0)
        n_e = tm + POOL_HALO
        parts = []
        for gi, w in enumerate(POOL_WINDOWS):
            cur = de[:, gi * POOL_GROUP:(gi + 1) * POOL_GROUP] / _window_counts(i * tm, n_e, w)
            span = 1
            while span < w:
                cur = cur + pltpu.roll(cur, n_e - span, 0)
                span *= 2
            parts.append(cur[:tm, :] - dc[:, gi * POOL_GROUP:(gi + 1) * POOL_GROUP])
        dp = jnp.concatenate(parts, axis=1)
        dp_ref[...] = dp.astype(BF16)
        cs_ref[0:1, :] += _colsum(dp)

        r = lax.broadcasted_iota(jnp.int32, (tm, tm), 0)
        c = lax.broadcasted_iota(jnp.int32, (tm, tm), 1)
        tri = (r >= c).astype(F32)
        k_cols = dfk_ref[0]
        rows8 = dfq_ref[0]
        for hp in range(1, N_PAIR):
            k_cols = k_cols + pltpu.roll(dfk_ref[hp], 2 * hp, 1)
            rows8 = rows8 + pltpu.roll(dfq_ref[hp], 2 * hp, 0)
        rows8 = rows8 + k_cols.T[0:8, :]
        dlogf8 = jnp.dot(rows8, tri, preferred_element_type=F32, precision=lax.Precision.HIGHEST) + carry[...]
        first = lax.broadcasted_iota(jnp.int32, (1, tm), 1) == 0
        carry[...] = jnp.sum(jnp.where(first, dlogf8, 0.0), axis=1, keepdims=True)
        dlogf = jnp.concatenate([dlogf8, jnp.zeros((128 - 8, tm), F32)], axis=0).T
        df = dlogf * _sigmoid(-f_ref[...])
        df_ref[...] = df.astype(BF16)
        cs_ref[1:2, 0:128] += _colsum(df)

    rev = lambda w: pl.BlockSpec((tm, w), lambda s: (n_t - 1 - s, 0))
    return pl.pallas_call(
        body, name="tail", grid=(n_t,),
        out_shape=(jax.ShapeDtypeStruct((S, D_POOL), BF16), jax.ShapeDtypeStruct((S, 128), BF16),
                   jax.ShapeDtypeStruct((8, D_POOL), F32)),
        in_specs=[rev(D_POOL),
                  pl.BlockSpec((POOL_HALO, D_POOL),
                               lambda s: (jnp.minimum((n_t - s) * halo_blocks, last_halo), 0)),
                  pl.BlockSpec((N_PAIR, tm, 128), lambda s: (0, n_t - 1 - s, 0)),
                  pl.BlockSpec((N_PAIR, None, 8, tm), lambda s: (0, n_t - 1 - s, 0, 0)),
                  rev(128)],
        out_specs=(rev(D_POOL), rev(128), pl.BlockSpec((8, D_POOL), lambda s: (0, 0))),
        scratch_shapes=[pltpu.VMEM((8, 1), F32)],
        compiler_params=_params(dimension_semantics=("arbitrary",)),
    )(dpl, dpl, dfk, dfq, f)


PIECES = ((O_QKV, D_ATT), (O_QKV + D_ATT, D_ATT), (O_QKV + 2 * D_ATT, D_ATT), (O_F, 128), (O_P, D_POOL), (O_G, D))


def _grad_w_in(u, pieces):
    S = u.shape[0]
    tm = min(TM_GW, S)
    n_t = S // tm

    def body(u_ref, *rest):
        piece_refs, out_ref, acc, sem = rest[:6], rest[6], rest[7], rest[8]
        i = pl.program_id(0)

        @pl.when(i == 0)
        def _():
            acc[...] = jnp.zeros_like(acc)

        u_t = u_ref[...]
        for (off, w), ref in zip(PIECES, piece_refs):
            acc[:, off:off + w] += _dot_tn(u_t, ref[...])

        @pl.when(i == n_t - 1)
        def _():
            cp = pltpu.make_async_copy(acc, out_ref, sem)
            cp.start()
            cp.wait()

    return pl.pallas_call(
        body, name="grad_w_in", grid=(n_t,),
        out_shape=jax.ShapeDtypeStruct((D, D_PAD), F32),
        in_specs=[pl.BlockSpec((tm, D), lambda i: (i, 0))]
        + [pl.BlockSpec((tm, w), lambda i: (i, 0)) for _, w in PIECES],
        out_specs=pl.BlockSpec(memory_space=pl.ANY),
        scratch_shapes=[pltpu.VMEM((D, D_PAD), F32), pltpu.SemaphoreType.DMA],
        compiler_params=_params(dimension_semantics=("arbitrary",)),
    )(u, *pieces)


def _grad_x(pieces, wt_pad, dh, x, scale):
    S = x.shape[0]
    tm = min(TM_DU, S)

    def body(*refs):
        piece_refs = refs[:6]
        w_ref, dh_ref, x_ref, sc_ref, gx_ref, vec_ref = refs[6:]

        @pl.when(pl.program_id(0) == 0)
        def _():
            vec_ref[...] = jnp.zeros_like(vec_ref)

        du = jnp.zeros((tm, D), F32)
        for (off, w), ref in zip(PIECES, piece_refs):
            du = du + _dot(ref[...], w_ref[off:off + w, :])
        xv = x_ref[...]
        gx_ref[...] = ALPHA * dh_ref[...] + du * (1.0 + sc_ref[...])
        vec_ref[0:1, :] += _colsum(du)
        vec_ref[1:2, :] += _colsum(du * xv)

    row = lambda w: pl.BlockSpec((tm, w), lambda i: (i, 0))
    return pl.pallas_call(
        body, name="grad_x", grid=(S // tm,),
        out_shape=(jax.ShapeDtypeStruct((S, D), F32), jax.ShapeDtypeStruct((8, D), F32)),
        in_specs=[row(w) for _, w in PIECES]
        + [pl.BlockSpec(wt_pad.shape, lambda i: (0, 0)), row(D), row(D), pl.BlockSpec((1, D), lambda i: (0, 0))],
        out_specs=(row(D), pl.BlockSpec((8, D), lambda i: (0, 0))),
        compiler_params=_params(dimension_semantics=("arbitrary",)),
    )(*pieces, wt_pad, dh, x, scale)


def _grad_ada(c_all, dada_all, dada_cols):
    def body(c_ref, dall_ref, dcol_ref, gw_ref, gb_ref):
        rows = lax.broadcasted_iota(jnp.int32, (8, 1), 0)
        cm = jnp.zeros((8, D), F32)
        dm = jnp.zeros((8, 3 * D), F32)
        for r in range(8):
            cm = jnp.where(rows == r, c_ref[r], cm)
            dm = jnp.where(rows == r, dall_ref[r], dm)
        act = cm * _sigmoid(cm)
        pad = jnp.zeros((8, D), F32)
        lhs = jnp.concatenate([act, pad], axis=0).astype(BF16)
        rhs = jnp.concatenate([dcol_ref[...], jnp.zeros((8, SHARD_ADA), F32)], axis=0).astype(BF16)
        gw_ref[...] = _dot_tn(lhs, rhs)
        gb_ref[...] = _colsum(dm)

    vm = pl.BlockSpec(memory_space=pltpu.VMEM)
    return pl.pallas_call(
        body, name="grad_ada",
        out_shape=(jax.ShapeDtypeStruct((D, SHARD_ADA), F32), jax.ShapeDtypeStruct((1, 3 * D), F32)),
        in_specs=[vm, vm, vm], out_specs=(vm, vm),
        compiler_params=_params(),
    )(c_all, dada_all, dada_cols)


def _adamw_math(w, g, m, v):
    m = ADAM_B1 * m + (1.0 - ADAM_B1) * g
    v = ADAM_B2 * v + (1.0 - ADAM_B2) * (g * g)
    m_hat = m / (1.0 - ADAM_B1 ** ADAM_STEP)
    v_hat = v / (1.0 - ADAM_B2 ** ADAM_STEP)
    delta = -ADAM_LR * (m_hat / (jnp.sqrt(v_hat) + ADAM_EPS) + ADAM_WD * w)
    return delta, m, v


def _adamw(groups, n_steps):
    n = len(groups)

    def body(*refs):
        ins, outs = refs[:4 * n], refs[4 * n:]
        for t in range(n):
            w, g, m, v = (r[...] for r in ins[4 * t:4 * t + 4])
            d, m2, v2 = _adamw_math(w, g, m, v)
            outs[3 * t][...] = d
            outs[3 * t + 1][...] = m2
            outs[3 * t + 2][...] = v2

    in_specs, out_specs, out_shape, args = [], [], [], []
    for (w, g, m, v) in groups:
        rest = w.shape[1:]
        spec = pl.BlockSpec((w.shape[0] // n_steps,) + rest, lambda i, nd=len(rest): (i,) + (0,) * nd)
        in_specs += [spec] * 4
        out_specs += [spec] * 3
        out_shape += [jax.ShapeDtypeStruct(w.shape, F32)] * 3
        args += [w, g, m, v]
    return pl.pallas_call(
        body, name="adamw_%d_%d" % (n, n_steps), grid=(n_steps,),
        out_shape=tuple(out_shape), in_specs=in_specs, out_specs=tuple(out_specs),
        compiler_params=_params(dimension_semantics=("arbitrary",)),
    )(*args)


def _pack_small(parts):
    rows = []
    used = 0
    for name, (first, n_rows) in SMALL_SEGS.items():
        if first > used:
            rows.append(jnp.zeros((first - used, 128), F32))
        flat = parts[name].reshape(-1)
        flat = jnp.pad(flat, (0, n_rows * 128 - flat.shape[0]))
        rows.append(flat.reshape(n_rows, 128))
        used = first + n_rows
    rows.append(jnp.zeros((SMALL_ROWS - used, 128), F32))
    return jnp.concatenate(rows, axis=0)


def _unpack_small(buf, name, shape):
    first, n_rows = SMALL_SEGS[name]
    n = int(np.prod(shape))
    return buf[first:first + n_rows].reshape(-1)[:n].reshape(shape)


def _pad_in(v):
    r = v.shape[0]
    z = jnp.zeros((r, O_P - O_F - N_HEADS), v.dtype)
    return jnp.concatenate([v[:, :3 * D_ATT + N_HEADS], z, v[:, 3 * D_ATT + N_HEADS:]], axis=1)


def _unpad_in(v):
    return jnp.concatenate([v[:, :O_F + N_HEADS], v[:, O_P:]], axis=1)


def _shards_in(v):
    gap = O_P - (O_F + N_HEADS)
    parts = []
    for a in range(N_CHIPS):
        lo, hi = a * SHARD_IN, (a + 1) * SHARD_IN
        cut = O_F + N_HEADS
        if hi <= cut:
            parts.append(v[:, lo:hi])
        elif lo >= cut:
            parts.append(v[:, lo + gap:hi + gap])
        else:
            parts.append(jnp.concatenate([v[:, lo:cut], v[:, cut + gap:hi + gap]], axis=1))
    return jnp.stack(parts, axis=0)


def kernel(x, c, w_ada, b_ada, w_in, b_in, w_pool_mix, b_pool_mix, pool_scale, w_out, b_out, ln_g, ln_b, loss_target, m_w_ada, m_b_ada, m_w_in, m_b_in, m_w_pool_mix, m_b_pool_mix, m_pool_scale, m_w_out, m_b_out, m_ln_g, m_ln_b, v_w_ada, v_b_ada, v_w_in, v_b_in, v_w_pool_mix, v_b_pool_mix, v_pool_scale, v_w_out, v_b_out, v_ln_g, v_ln_b):
    S = x.shape[1]
    T = min(T_ATT, S)
    n_t = S // T
    chip = 2 * lax.axis_index("x") + lax.axis_index("y")
    x2 = x[0]
    tgt = loss_target[0]
    q_scale = jnp.concatenate([jnp.full((1, D_ATT), Q_SCALE, F32), jnp.ones((1, D_PAD - D_ATT), F32)], axis=1)

    to_cols = lambda a: jnp.transpose(a, (2, 0, 1))
    from_cols = lambda a: jnp.transpose(a, (1, 2, 0))
    c_all, ada4, wt_in_all = _gather_and_ada(
        c, w_ada[0], b_ada.reshape(4, 1, SHARD_ADA), to_cols(w_in).reshape(SHARD_IN, D).astype(BF16))
    ada = ada4[:, 0, :].reshape(1, 3 * D)
    shift, scale, gate = ada[:, :D], ada[:, D:2 * D], ada[:, 2 * D:]
    wt_full = wt_in_all.reshape(D_IN, D)
    n_real = 3 * D_ATT + N_HEADS
    wt_pad = jnp.concatenate([wt_full[:D_ATT] * jnp.asarray(Q_SCALE, BF16), wt_full[D_ATT:n_real],
                              jnp.zeros((O_P - n_real, D), BF16), wt_full[n_real:]], axis=0)
    b_pad = _pad_in(b_in) * q_scale
    w_mix_bf = w_pool_mix[0].astype(BF16)

    u, qkv, f, p, g, w_out_all = _in_proj(x2, shift, scale, wt_pad, b_pad, w_out[0].astype(BF16))
    w_out_full = w_out_all.reshape(D, D)
    big_f = _forget_cumsum(f)
    att, lse = _attention_fwd(qkv, big_f)

    dh, datt, dg, dpl, gw_out, gw_mix, vec, loss_part = _middle(
        x2, tgt, att, g, p, gate, w_mix_bf, b_pool_mix.reshape(1, D_POOL), pool_scale, w_out_full, b_out, ln_g, ln_b)
    dq, dk, dv, cs_att, dfk, dfq, g_w_out = _attention_bwd(
        qkv, datt, att, lse, big_f, gw_out.reshape(N_CHIPS, SHARD_OUT, D), jnp.ones((N_CHIPS, 1, D), F32))
    dp, df, cs_tail = _tail(dpl, dfk, dfq, f)
    pieces = (dq, dk, dv, df, dp, dg)
    gw_pad = _grad_w_in(u, pieces)
    grad_x, vec_x = _grad_x(pieces, wt_pad, dh, x2, scale)

    cs_qkv = jnp.transpose(cs_att.reshape(N_PAIR, 3, 128), (1, 0, 2)).reshape(1, 3 * D_ATT)
    gb_pad = jnp.concatenate([cs_qkv, cs_tail[1:2, 0:128], cs_tail[0:1, :], vec[4:5, :]], axis=1) * q_scale
    dada = jnp.concatenate([vec_x[0:1, :], vec_x[1:2, :], vec[2:3, :]], axis=1)
    small = _pack_small({
        "b_in": gb_pad, "w_pool_mix": gw_mix, "b_pool_mix": vec[6:7, :D_POOL], "pool_scale": vec[5:6, :D_POOL],
        "b_out": vec[3:4, :], "ln_g": vec[0:1, :], "ln_b": vec[1:2, :], "loss": loss_part,
        "b_ada": jnp.zeros((1, 3 * D), F32)})

    g_w_in, small_sum, dada_all = _reduce_all(
        _shards_in(gw_pad), _shards_in(q_scale), small[:SMALL_REDUCED_ROWS], dada)
    dada_cols = lax.dynamic_slice(dada_all[:, 0, :], (0, chip * SHARD_ADA), (8, SHARD_ADA))
    g_w_ada, g_b_ada = _grad_ada(c_all, dada_all, dada_cols)
    loss = _unpack_small(small_sum, "loss", (1,))[0]

    grads_small = jnp.concatenate([small_sum, g_b_ada.reshape(24, 128)], axis=0)
    no_param = jnp.zeros((1,), F32)
    small_w = {"b_in": _pad_in(b_in), "w_pool_mix": w_pool_mix, "b_pool_mix": b_pool_mix, "pool_scale": pool_scale,
               "b_out": b_out, "ln_g": ln_g, "ln_b": ln_b, "loss": no_param, "b_ada": b_ada}
    small_m = {"b_in": _pad_in(m_b_in), "w_pool_mix": m_w_pool_mix, "b_pool_mix": m_b_pool_mix,
               "pool_scale": m_pool_scale, "b_out": m_b_out, "ln_g": m_ln_g, "ln_b": m_ln_b, "loss": no_param,
               "b_ada": m_b_ada}
    small_v = {"b_in": _pad_in(v_b_in), "w_pool_mix": v_w_pool_mix, "b_pool_mix": v_b_pool_mix,
               "pool_scale": v_pool_scale, "b_out": v_b_out, "ln_g": v_ln_g, "ln_b": v_ln_b, "loss": no_param,
               "b_ada": v_b_ada}
    big = _adamw([(w_ada[0], g_w_ada, m_w_ada[0], v_w_ada[0]),
                  (w_out[0], g_w_out, m_w_out[0], v_w_out[0])], 8)
    g_w_in_cols = to_cols(g_w_in[None])
    big_in = _adamw([(to_cols(w_in), g_w_in_cols, to_cols(m_w_in), to_cols(v_w_in))], 14)
    sm = _adamw([(_pack_small(small_w), grads_small, _pack_small(small_m), _pack_small(small_v))], 1)

    names = ["w_ada", "b_ada", "w_in", "b_in", "w_pool_mix", "b_pool_mix", "pool_scale", "w_out", "b_out",
             "ln_g", "ln_b"]
    shapes = {"b_ada": (1, 3 * D), "b_in": (1, D_PAD), "w_pool_mix": (1, 4, POOL_GROUP, POOL_GROUP),
              "b_pool_mix": (1, 4, POOL_GROUP), "pool_scale": (1, D_POOL), "b_out": (1, D), "ln_g": (1, D),
              "ln_b": (1, D)}
    big_idx = {"w_ada": 0, "w_out": 1}

    def leaf(kind, name):
        if name == "w_in":
            return from_cols(g_w_in_cols if kind == 0 else big_in[kind - 1])
        if name in big_idx:
            if kind == 0:
                return (g_w_ada, g_w_out)[big_idx[name]][None]
            return big[3 * big_idx[name] + kind - 1][None]
        buf = grads_small if kind == 0 else sm[kind - 1]
        val = _unpack_small(buf, name, shapes[name])
        if name == "b_in":
            val = _unpad_in(val)
        return val

    outs = [loss, grad_x[None]]
    for kind in range(4):
        outs += [leaf(kind, n) for n in names]
    return tuple(outs)
```

```python
import functools

import numpy as np
import jax
import jax.numpy as jnp
from jax import lax
from jax.experimental import pallas as pl
from jax.experimental.pallas import tpu as pltpu

F32 = jnp.float32
BF16 = jnp.bfloat16
MESH = pl.DeviceIdType.MESH

D = 1024
D_ATT = 512
D_POOL = 512
N_HEADS = 8
HEAD_DIM = 64
N_PAIR = N_HEADS // 2
POOL_WINDOWS = (2, 4, 8, 16)
POOL_GROUP = 128
POOL_HALO = 16
LN_EPS = 1e-5
ALPHA = 2.0 ** 0.25
D_IN = 3 * D_ATT + N_HEADS + D_POOL + D_ATT + D_POOL
N_CHIPS = 4
SHARD_IN = D_IN // N_CHIPS
SHARD_ADA = 3 * D // N_CHIPS
SHARD_OUT = D // N_CHIPS

O_QKV, O_F, O_P, O_G, D_PAD = 0, 1536, 1664, 2176, 3200
Q_SCALE = HEAD_DIM ** -0.5

ADAM_LR, ADAM_B1, ADAM_B2, ADAM_EPS, ADAM_WD, ADAM_STEP = 0.001, 0.9, 0.999, 1e-08, 0.01, 10

NEG = -1e30

VMEM_LIMIT = 56 * 1024 * 1024

TM_PROJ = 512
T_ATT = 512
TM_MID = 256
TM_GW = 1024
TM_DU = 512

REL7 = [(0, 0, 1), (0, 1, 0), (0, 1, 1), (1, 0, 0), (1, 0, 1), (1, 1, 0), (1, 1, 1)]
REL3 = [(0, 1), (1, 0), (1, 1)]

SMALL_SEGS = {}
_row = 0
for _name, _n in (("b_in", 3200), ("w_pool_mix", 65536), ("b_pool_mix", 512), ("pool_scale", 512),
                  ("b_out", 1024), ("ln_g", 1024), ("ln_b", 1024), ("loss", 1)):
    _rows = -(-_n // 1024) * 8
    SMALL_SEGS[_name] = (_row, _rows)
    _row += _rows
SMALL_REDUCED_ROWS = -(-_row // 16) * 16
SMALL_SEGS["b_ada"] = (SMALL_REDUCED_ROWS, 24)
SMALL_ROWS = SMALL_REDUCED_ROWS + 24


def _params(**kw):
    return pltpu.CompilerParams(vmem_limit_bytes=VMEM_LIMIT, **kw)


def _flip(v, d):
    return v if d == 0 else 1 - v


def _dot(a, b):
    return jnp.dot(a, b, preferred_element_type=F32)


def _dot_nt(a, b):
    return lax.dot_general(a, b, (((1,), (1,)), ((), ())), preferred_element_type=F32)


def _dot_tn(a, b):
    return lax.dot_general(a, b, (((0,), (0,)), ((), ())), preferred_element_type=F32)


def _sigmoid(v):
    return 1.0 / (1.0 + jnp.exp(-v))


def _colsum(v):
    return jnp.sum(v, axis=0, keepdims=True)


def _gather_stages(pos, src_ref, dst_ref, half, own_sem, s_sem, r_sem, fs_sem, fr_sem):
    x, y, cc, chip, sib = pos
    own = pltpu.make_async_copy(src_ref, dst_ref.at[chip], own_sem)
    first, landed, others = [], [], []
    for k, (dx, dy) in enumerate(REL3):
        px, py = _flip(x, dx), _flip(y, dy)
        first.append(pltpu.make_async_remote_copy(
            src_ref=src_ref.at[half(cc)], dst_ref=dst_ref.at[(chip,) + half(cc)],
            send_sem=s_sem.at[k], recv_sem=r_sem.at[k], device_id=(px, py, cc), device_id_type=MESH))
        landed.append(dst_ref.at[(2 * px + py,) + half(cc)])
        others.append(dst_ref.at[(2 * px + py,) + half(1 - cc)])
    passed = [pltpu.make_async_remote_copy(src_ref=landed[k], dst_ref=landed[k], send_sem=fs_sem.at[k],
                                           recv_sem=fr_sem.at[k], device_id=sib, device_id_type=MESH)
              for k in range(3)]

    def start():
        own.start()
        for cp in first:
            cp.start()

    def forward():
        for k in range(3):
            pltpu.make_async_remote_copy(src_ref=landed[k], dst_ref=landed[k], send_sem=s_sem.at[k],
                                         recv_sem=r_sem.at[k], device_id=sib, device_id_type=MESH).wait_recv()
            passed[k].start()

    def finish():
        for k in range(3):
            pltpu.make_async_remote_copy(src_ref=others[k], dst_ref=others[k], send_sem=fs_sem.at[k],
                                         recv_sem=fr_sem.at[k], device_id=sib, device_id_type=MESH).wait_recv()
        for cp in first + passed:
            cp.wait_send()
        own.wait()

    return start, forward, finish


def _gather_scratch():
    return [pltpu.SemaphoreType.DMA, pltpu.SemaphoreType.DMA((3,)), pltpu.SemaphoreType.DMA((3,)),
            pltpu.SemaphoreType.DMA((3,)), pltpu.SemaphoreType.DMA((3,))]


def _gather_and_ada(c, w_ada, b_ada4, w_in_sh):
    def body(c_ref, w_ref, b_ref, win_ref, call_ref, ada_ref, win_all,
             cslab, sbuf, rbuf, cs_sem, cr_sem, as_sem, ar_sem, *gather_sems):
        x, y, cc = lax.axis_index("x"), lax.axis_index("y"), lax.axis_index("c")
        me = 4 * x + 2 * y + cc
        chip = 2 * x + y
        lane_half = lambda which: (slice(None), pl.ds(pl.multiple_of(which * (D // 2), D // 2), D // 2))
        start, forward, finish = _gather_stages((x, y, cc, chip, (x, y, 1 - cc)), win_ref, win_all, lane_half,
                                                *gather_sems)
        start()

        cslab[...] = jnp.broadcast_to(c_ref[...], (8, D))
        call_ref[me] = cslab[...]
        gathers = []
        for k, (dx, dy, dc) in enumerate(REL7):
            cp = pltpu.make_async_remote_copy(
                src_ref=cslab, dst_ref=call_ref.at[me], send_sem=cs_sem.at[k], recv_sem=cr_sem.at[k],
                device_id=(_flip(x, dx), _flip(y, dy), _flip(cc, dc)), device_id_type=MESH)
            cp.start()
            gathers.append(cp)
        for cp in gathers:
            cp.wait()
        slab_row = lax.broadcasted_iota(jnp.int32, (8, 1), 0)
        mat = jnp.zeros((8, D), F32)
        for r in range(8):
            mat = jnp.where(slab_row == r, call_ref[r], mat)
        act = (mat * _sigmoid(mat)).astype(BF16)
        part = _dot(act, w_ref[...].astype(BF16))
        sends = []
        for k, (dx, dy) in enumerate(REL3):
            px, py = _flip(x, dx), _flip(y, dy)
            r = 4 * px + 2 * py + cc
            piece = _colsum(jnp.where(slab_row == r, part, 0.0))
            sbuf[k] = jnp.broadcast_to(piece, (8, SHARD_ADA))
            cp = pltpu.make_async_remote_copy(
                src_ref=sbuf.at[k], dst_ref=rbuf.at[k], send_sem=as_sem.at[k], recv_sem=ar_sem.at[k],
                device_id=(px, py, cc), device_id_type=MESH)
            cp.start()
            sends.append(cp)
        own_piece = _colsum(jnp.where(slab_row == me, part, 0.0))
        ada_ref[chip] = jnp.broadcast_to(own_piece, (8, SHARD_ADA)) + b_ref[chip]
        for k, (dx, dy) in enumerate(REL3):
            sends[k].wait()
            a = 2 * _flip(x, dx) + _flip(y, dy)
            ada_ref[a] = rbuf[k] + b_ref[a]

        forward()
        finish()

    vm = pl.BlockSpec(memory_space=pltpu.VMEM)
    return pl.pallas_call(
        body, name="gather_and_ada",
        out_shape=(jax.ShapeDtypeStruct((8, 8, D), F32), jax.ShapeDtypeStruct((4, 8, SHARD_ADA), F32),
                   jax.ShapeDtypeStruct((N_CHIPS, SHARD_IN, D), BF16)),
        in_specs=[vm] * 4, out_specs=(vm,) * 3,
        scratch_shapes=[pltpu.VMEM((8, D), F32), pltpu.VMEM((3, 8, SHARD_ADA), F32),
                        pltpu.VMEM((3, 8, SHARD_ADA), F32),
                        pltpu.SemaphoreType.DMA((7,)), pltpu.SemaphoreType.DMA((7,)),
                        pltpu.SemaphoreType.DMA((3,)), pltpu.SemaphoreType.DMA((3,))] + _gather_scratch(),
        compiler_params=_params(),
    )(c, w_ada, b_ada4, w_in_sh)


def _scatter_stages(pos, g_ref, sc_ref, out_ref, sib_buf, send_buf, ici_buf, sem1, sem2s, sem2r, sem3, part=(0, 1)):
    x, y, cc, chip, sib = pos
    q, n_parts = part
    RH = g_ref.shape[1] // 2 // n_parts
    mine = pl.ds(pl.multiple_of((cc * n_parts + q) * RH, RH), RH)
    theirs = pl.ds(pl.multiple_of(((1 - cc) * n_parts + q) * RH, RH), RH)
    cp1 = pltpu.make_async_remote_copy(
        src_ref=g_ref.at[:, theirs, :], dst_ref=sib_buf, send_sem=sem1.at[0], recv_sem=sem1.at[1],
        device_id=sib, device_id_type=MESH)
    sends = []
    for k, (dx, dy) in enumerate(REL3):
        px, py = _flip(x, dx), _flip(y, dy)
        sends.append(pltpu.make_async_remote_copy(
            src_ref=send_buf.at[2 * px + py], dst_ref=ici_buf.at[chip],
            send_sem=sem2s.at[k], recv_sem=sem2r.at[k], device_id=(px, py, cc), device_id_type=MESH))
    cp3 = pltpu.make_async_remote_copy(
        src_ref=out_ref.at[mine, :], dst_ref=out_ref.at[mine, :], send_sem=sem3.at[0], recv_sem=sem3.at[1],
        device_id=sib, device_id_type=MESH)

    def finish1():
        cp1.wait()
        for a in range(N_CHIPS):
            both = g_ref[a, mine, :] + sib_buf[a]
            sib_buf[a] = both
            send_buf[a] = both.astype(BF16)

    def start2():
        for cp in sends:
            cp.start()
        ici_buf[chip] = send_buf[chip]

    def finish2():
        for cp in sends:
            cp.wait()
        own = sib_buf[chip]
        parts = [jnp.where(chip == a, own, ici_buf[a].astype(F32)) for a in range(N_CHIPS)]
        out_ref[mine, :] = ((parts[0] + parts[1]) + (parts[2] + parts[3])) * sc_ref[chip]

    return [(cp1.start, finish1), (start2, finish2), (cp3.start, cp3.wait)]


def _all_reduce_stages(pos, g_ref, out_ref, sib_buf, ici_buf, sem1, sem2s, sem2r, sem3):
    x, y, cc, chip, sib = pos
    RH = g_ref.shape[0] // 2
    mine = pl.ds(pl.multiple_of(cc * RH, 8), RH)
    theirs = pl.ds(pl.multiple_of((1 - cc) * RH, 8), RH)
    cp1 = pltpu.make_async_remote_copy(
        src_ref=g_ref.at[theirs, :], dst_ref=sib_buf, send_sem=sem1.at[0], recv_sem=sem1.at[1],
        device_id=sib, device_id_type=MESH)
    sends = []
    for k, (dx, dy) in enumerate(REL3):
        px, py = _flip(x, dx), _flip(y, dy)
        sends.append(pltpu.make_async_remote_copy(
            src_ref=sib_buf, dst_ref=ici_buf.at[chip],
            send_sem=sem2s.at[k], recv_sem=sem2r.at[k], device_id=(px, py, cc), device_id_type=MESH))
    cp3 = pltpu.make_async_remote_copy(
        src_ref=out_ref.at[mine, :], dst_ref=out_ref.at[mine, :], send_sem=sem3.at[0], recv_sem=sem3.at[1],
        device_id=sib, device_id_type=MESH)

    def finish1():
        cp1.wait()
        sib_buf[...] = g_ref[mine, :] + sib_buf[...]

    def start2():
        for cp in sends:
            cp.start()
        ici_buf[chip] = sib_buf[...]

    def finish2():
        for cp in sends:
            cp.wait()
        out_ref[mine, :] = (ici_buf[0] + ici_buf[1]) + (ici_buf[2] + ici_buf[3])

    return [(cp1.start, finish1), (start2, finish2), (cp3.start, cp3.wait)]


def _stage_sems():
    return [pltpu.SemaphoreType.DMA((2,)), pltpu.SemaphoreType.DMA((3,)),
            pltpu.SemaphoreType.DMA((3,)), pltpu.SemaphoreType.DMA((2,))]


def _scatter_scratch(r, c):
    return [pltpu.VMEM((N_CHIPS, r // 2, c), F32), pltpu.VMEM((N_CHIPS, r // 2, c), BF16),
            pltpu.VMEM((N_CHIPS, r // 2, c), BF16)] + _stage_sems()


def _reduce_all(g4_in, sc_in, small, dada):
    R = small.shape[0]
    W = dada.shape[1]
    r_in, c_in = g4_in.shape[1:]
    n_in = len(_scatter_scratch(r_in // 2, c_in))

    def body(gin_ref, scin_ref, sm_ref, d_ref, oin_ref, osm_ref, dall_ref, *scratch):
        x, y, cc = lax.axis_index("x"), lax.axis_index("y"), lax.axis_index("c")
        me = 4 * x + 2 * y + cc
        pos = (x, y, cc, 2 * x + y, (x, y, 1 - cc))
        dslab, ds_sem, dr_sem = scratch[0:3]
        a_bufs, b_bufs, sm_bufs = scratch[3:3 + n_in], scratch[3 + n_in:3 + 2 * n_in], scratch[3 + 2 * n_in:]
        dslab[...] = jnp.broadcast_to(d_ref[...], (8, W))
        dall_ref[me] = dslab[...]
        gathers = []
        for k, (dx, dy, dc) in enumerate(REL7):
            cp = pltpu.make_async_remote_copy(
                src_ref=dslab, dst_ref=dall_ref.at[me], send_sem=ds_sem.at[k], recv_sem=dr_sem.at[k],
                device_id=(_flip(x, dx), _flip(y, dy), _flip(cc, dc)), device_id_type=MESH)
            cp.start()
            gathers.append(cp)
        first = _scatter_stages(pos, gin_ref, scin_ref, oin_ref, *a_bufs, part=(0, 2))
        second = _scatter_stages(pos, gin_ref, scin_ref, oin_ref, *b_bufs, part=(1, 2))
        little = _all_reduce_stages(pos, sm_ref, osm_ref, *sm_bufs)
        for plan in (first, second, little):
            plan[0][0]()
        first[0][1]()
        first[1][0]()
        little[0][1]()
        little[1][0]()
        second[0][1]()
        second[1][0]()
        first[1][1]()
        first[2][0]()
        second[1][1]()
        second[2][0]()
        little[1][1]()
        little[2][0]()
        for plan in (first, second, little):
            plan[2][1]()
        for cp in gathers:
            cp.wait()

    scratch = [pltpu.VMEM((8, W), F32), pltpu.SemaphoreType.DMA((7,)), pltpu.SemaphoreType.DMA((7,))]
    scratch += _scatter_scratch(r_in // 2, c_in) + _scatter_scratch(r_in // 2, c_in)
    scratch += [pltpu.VMEM((R // 2, 128), F32), pltpu.VMEM((N_CHIPS, R // 2, 128), F32)] + _stage_sems()
    vm = pl.BlockSpec(memory_space=pltpu.VMEM)
    return pl.pallas_call(
        body, name="reduce_all",
        out_shape=(jax.ShapeDtypeStruct(g4_in.shape[1:], F32),
                   jax.ShapeDtypeStruct((R, 128), F32), jax.ShapeDtypeStruct((8, 8, W), F32)),
        in_specs=[vm] * 4, out_specs=(vm,) * 3,
        scratch_shapes=scratch,
        compiler_params=_params(),
    )(g4_in, sc_in, small, dada)


def _in_proj(x, shift, scale, wt_pad, b_pad, w_out_sh):
    S = x.shape[0]
    tm = min(TM_PROJ, S)
    n_steps = S // tm
    assert n_steps >= 3

    def body(x_ref, sh_ref, sc_ref, w_ref, b_ref, wo_ref, u_ref, qkv_ref, f_ref, p_ref, g_ref, wo_all,
             wo_buf, *gather_sems):
        i = pl.program_id(0)
        xx, yy, cc = lax.axis_index("x"), lax.axis_index("y"), lax.axis_index("c")
        row_half = lambda which: (pl.ds(pl.multiple_of(which * (SHARD_OUT // 2), SHARD_OUT // 2), SHARD_OUT // 2),
                                  slice(None))
        start, forward, finish = _gather_stages((xx, yy, cc, 2 * xx + yy, (xx, yy, 1 - cc)), wo_ref, wo_buf,
                                                row_half, *gather_sems)
        pl.when(i == 0)(start)
        pl.when(i == n_steps // 2)(forward)

        @pl.when(i == n_steps - 1)
        def _():
            finish()
            wo_all[...] = wo_buf[...]

        u = (x_ref[...] * (1.0 + sc_ref[...]) + sh_ref[...]).astype(BF16)
        u_ref[...] = u
        qkv_ref[...] = (_dot_nt(u, w_ref[O_QKV:O_F, :]) + b_ref[:, O_QKV:O_F]).astype(BF16)
        f_ref[...] = _dot_nt(u, w_ref[O_F:O_P, :]) + b_ref[:, O_F:O_P]
        p_ref[...] = _dot_nt(u, w_ref[O_P:O_G, :]) + b_ref[:, O_P:O_G]
        g_ref[...] = _dot_nt(u, w_ref[O_G:D_PAD, :]) + b_ref[:, O_G:D_PAD]

    row = lambda w: pl.BlockSpec((tm, w), lambda i: (i, 0))
    full = lambda a: pl.BlockSpec(a.shape, lambda i: (0, 0))
    vm = pl.BlockSpec(memory_space=pltpu.VMEM)
    return pl.pallas_call(
        body, name="in_proj", grid=(n_steps,),
        out_shape=(jax.ShapeDtypeStruct((S, D), BF16), jax.ShapeDtypeStruct((S, 3 * D_ATT), BF16),
                   jax.ShapeDtypeStruct((S, 128), F32), jax.ShapeDtypeStruct((S, D_POOL), F32),
                   jax.ShapeDtypeStruct((S, D), F32), jax.ShapeDtypeStruct((N_CHIPS,) + w_out_sh.shape, BF16)),
        in_specs=[row(D), full(shift), full(scale), full(wt_pad), full(b_pad), vm],
        out_specs=(row(D), row(3 * D_ATT), row(128), row(D_POOL), row(D), vm),
        scratch_shapes=[pltpu.VMEM((N_CHIPS,) + w_out_sh.shape, BF16)] + _gather_scratch(),
        compiler_params=_params(dimension_semantics=("arbitrary",)),
    )(x, shift, scale, wt_pad, b_pad, w_out_sh)


def _forget_cumsum(f):
    S = f.shape[0]
    tm = min(T_ATT, S)

    def body(f_ref, out_ref, carry):
        @pl.when(pl.program_id(0) == 0)
        def _():
            carry[...] = jnp.zeros_like(carry)
        v = f_ref[...]
        logf = jnp.minimum(v, 0.0) - jnp.log(1.0 + jnp.exp(-jnp.abs(v)))
        r = lax.broadcasted_iota(jnp.int32, (tm, tm), 0)
        c = lax.broadcasted_iota(jnp.int32, (tm, tm), 1)
        tri = (r <= c).astype(F32)
        rows8 = logf.T[0:8, :]
        cum8 = jnp.dot(rows8, tri, preferred_element_type=F32, precision=lax.Precision.HIGHEST) + carry[...]
        out_ref[...] = jnp.concatenate([cum8, jnp.zeros((128 - 8, tm), F32)], axis=0).T
        last = lax.broadcasted_iota(jnp.int32, (1, tm), 1) == tm - 1
        carry[...] = jnp.sum(jnp.where(last, cum8, 0.0), axis=1, keepdims=True)

    return pl.pallas_call(
        body, name="forget_cumsum", grid=(S // tm,),
        out_shape=jax.ShapeDtypeStruct((S, 128), F32),
        in_specs=[pl.BlockSpec((tm, 128), lambda i: (i, 0))],
        out_specs=pl.BlockSpec((tm, 128), lambda i: (i, 0)),
        scratch_shapes=[pltpu.VMEM((8, 1), F32)],
        compiler_params=_params(dimension_semantics=("arbitrary",)),
    )(f)


def _split3(v):
    hi = v.astype(BF16)
    rest = v - hi.astype(F32)
    mid = rest.astype(BF16)
    lo = (rest - mid.astype(F32)).astype(BF16)
    return hi, mid, lo


def _attention_fwd(qkv, big_f):
    S = qkv.shape[0]
    T = min(T_ATT, S)
    n_t = S // T

    def body(q_ref, k_ref, v_ref, f_ref, o_ref, lse_ref, kaug_sc, vt_sc, m_sc, l_sc, acc_sc):
        hp = pl.program_id(0)
        i = pl.program_id(1)
        lane = lax.broadcasted_iota(jnp.int32, (1, 128), 1)
        sub = lax.broadcasted_iota(jnp.int32, (128, 1), 0)
        head_sel = (lane < HEAD_DIM, lane >= HEAD_DIM)
        head_sel_t = (sub < HEAD_DIM, sub >= HEAD_DIM)
        spare = (HEAD_DIM, 0)
        zero = jnp.zeros((), BF16)

        @pl.when(i == 0)
        def _():
            def prep(jt, carry):
                rows = pl.ds(pl.multiple_of(jt * T, T), T)
                k = k_ref[rows, :]
                ft = f_ref[rows, :]
                vt = v_ref[rows, :].astype(F32).T
                for h in range(2):
                    fh = jnp.sum(jnp.where(lane == 2 * hp + h, ft, 0.0), axis=1, keepdims=True)
                    hi, mid, lo = _split3(-fh)
                    b = spare[h]
                    bias = jnp.where(lane == b, hi, jnp.where(lane == b + 1, mid, jnp.where(lane == b + 2, lo, zero)))
                    kaug_sc[h, rows, :] = jnp.where(head_sel[h], k, bias)
                    vt_sc[h, jt] = jnp.where(head_sel_t[h], vt, 0.0).astype(BF16)
                return carry

            lax.fori_loop(0, n_t, prep, 0)

        q = q_ref[...]
        q_heads = []
        for h in range(2):
            ones = jnp.where((lane >= spare[h]) & (lane < spare[h] + 3), jnp.ones((), BF16), zero)
            q_heads.append(jnp.where(head_sel[h], q, ones))
        m_sc[...] = jnp.full((8, T), NEG, F32)
        l_sc[...] = jnp.zeros((8, T), F32)
        acc_sc[...] = jnp.zeros((128, T), F32)

        def update(j, k_lo, n_k, q_lo, masked):
            rows = pl.ds(pl.multiple_of(j * T + k_lo, n_k), n_k)
            n_q = T - q_lo
            alphas, pvs = [], []
            for h in range(2):
                s_t = _dot_nt(kaug_sc[h, rows, :], q_heads[h][q_lo:, :])
                if masked:
                    rr = lax.broadcasted_iota(jnp.int32, (n_k, n_q), 0) + k_lo
                    cc = lax.broadcasted_iota(jnp.int32, (n_k, n_q), 1) + q_lo
                    s_t = jnp.where(rr <= cc, s_t, NEG)
                m_prev = m_sc[h:h + 1, q_lo:]
                m_new = jnp.maximum(m_prev, jnp.max(s_t, axis=0, keepdims=True))
                alpha = jnp.exp(m_prev - m_new)
                p_t = jnp.exp(s_t - m_new)
                l_sc[h:h + 1, q_lo:] = alpha * l_sc[h:h + 1, q_lo:] + jnp.sum(p_t, axis=0, keepdims=True)
                m_sc[h:h + 1, q_lo:] = m_new
                alphas.append(alpha)
                pvs.append(_dot(vt_sc[h, j, :, k_lo:k_lo + n_k], p_t.astype(BF16)))
            acc_sc[:, q_lo:] = (acc_sc[:, q_lo:] * jnp.where(head_sel_t[0], alphas[0], alphas[1])
                                + (pvs[0] + pvs[1]))

        def two_off_diagonal(jj, carry):
            update(2 * jj, 0, T, 0, False)
            update(2 * jj + 1, 0, T, 0, False)
            return carry

        lax.fori_loop(0, i // 2, two_off_diagonal, 0)

        @pl.when(i % 2 == 1)
        def _():
            update(i - 1, 0, T, 0, False)

        update(i, 0, T, 0, True)
        l = l_sc[...]
        o_ref[...] = (acc_sc[...] / jnp.where(head_sel_t[0], l[0:1, :], l[1:2, :])).T
        is_head = lax.broadcasted_iota(jnp.int32, (8, 1), 0) < 2
        lse_ref[...] = jnp.where(is_head, m_sc[...] + jnp.log(jnp.where(is_head, l, 1.0)), 0.0)

    return pl.pallas_call(
        body, name="attention_fwd", grid=(N_PAIR, n_t),
        out_shape=(jax.ShapeDtypeStruct((S, D_ATT), F32), jax.ShapeDtypeStruct((N_PAIR, n_t, 8, T), F32)),
        in_specs=[pl.BlockSpec((T, 128), lambda hp, i: (i, hp)),
                  pl.BlockSpec((S, 128), lambda hp, i: (0, N_PAIR + hp)),
                  pl.BlockSpec((S, 128), lambda hp, i: (0, 2 * N_PAIR + hp)),
                  pl.BlockSpec((S, 128), lambda hp, i: (0, 0))],
        out_specs=(pl.BlockSpec((T, 128), lambda hp, i: (i, hp)),
                   pl.BlockSpec((None, None, 8, T), lambda hp, i: (hp, i, 0, 0))),
        scratch_shapes=[pltpu.VMEM((2, S, 128), BF16), pltpu.VMEM((2, n_t, 128, T), BF16),
                        pltpu.VMEM((8, T), F32), pltpu.VMEM((8, T), F32), pltpu.VMEM((128, T), F32)],
        compiler_params=_params(dimension_semantics=("arbitrary", "arbitrary")),
    )(qkv, qkv, qkv, big_f)


def _attention_bwd(qkv, datt, att, lse, big_f, gw_out4, sc_out):
    S = qkv.shape[0]
    T = min(T_ATT, S)
    n_t = S // T
    n_steps = N_PAIR * n_t
    marks = (0, n_steps // 8, n_steps // 2, n_steps // 2 + n_steps // 8)

    def body(q_ref, do_ref, o_ref, lse_ref, k_ref, v_ref, fk_ref, gout_ref, scout_ref,
             dq_ref, dk_ref, dv_ref, cs_ref, dfk_ref, dfq_ref, oout_ref, stat_sc, dqt_sc, qaug_sc,
             out_buf, *red_bufs):
        hp = pl.program_id(0)
        j = pl.program_id(1)
        x, y, cc = lax.axis_index("x"), lax.axis_index("y"), lax.axis_index("c")
        plan = _scatter_stages((x, y, cc, 2 * x + y, (x, y, 1 - cc)), gout_ref, scout_ref, out_buf, *red_bufs)
        step = hp * n_t + j
        for n, mark in enumerate(marks):
            @pl.when(step == mark)
            def _(n=n):
                if n > 0:
                    plan[n - 1][1]()
                if n < 3:
                    plan[n][0]()
                else:
                    oout_ref[...] = out_buf[...]

        lane = lax.broadcasted_iota(jnp.int32, (1, 128), 1)
        sub = lax.broadcasted_iota(jnp.int32, (128, 1), 0)
        head_sel = (lane < HEAD_DIM, lane >= HEAD_DIM)
        head_sel_t = (sub < HEAD_DIM, sub >= HEAD_DIM)
        spare = (HEAD_DIM, 0)
        zero = jnp.zeros((), BF16)
        one = jnp.ones((), BF16)

        def bias_lanes(first, pieces):
            hi, mid, lo = pieces
            return lambda rest: jnp.where(lane == first, hi, jnp.where(lane == first + 1, mid,
                                                                        jnp.where(lane == first + 2, lo, rest)))

        @pl.when(j == 0)
        def _():
            dqt_sc[...] = jnp.zeros_like(dqt_sc)
            cs_ref[...] = jnp.zeros_like(cs_ref)
            dfq_ref[...] = jnp.zeros_like(dfq_ref)

            def prep(i, carry):
                rows = pl.ds(pl.multiple_of(i * T, T), T)
                q = q_ref[rows, :]
                do = do_ref[rows, :]
                prod = o_ref[rows, :] * do.astype(F32)
                d_a = jnp.sum(jnp.where(head_sel[0], prod, 0.0), axis=1, keepdims=True)
                d_b = jnp.sum(jnp.where(head_sel[0], 0.0, prod), axis=1, keepdims=True)
                delta_t = jnp.where(head_sel[0], d_a, d_b).T
                stat_sc[i, 0:1, :] = delta_t[0:1, :]
                stat_sc[i, 1:2, :] = delta_t[HEAD_DIM:HEAD_DIM + 1, :]
                lse = lse_ref[i]
                lse_cols = jnp.where(head_sel_t[0], lse[0:1, :], lse[1:2, :]).T
                for h in range(2):
                    neg_lse = -lse_cols[:, h * HEAD_DIM:h * HEAD_DIM + 1]
                    ones = jnp.where((lane >= spare[h]) & (lane < spare[h] + 3), one, zero)
                    qaug_sc[h, rows, :] = jnp.where(head_sel[h], q, bias_lanes(spare[h] + 3, _split3(neg_lse))(ones))
                return carry

            lax.fori_loop(0, n_t, prep, 0)

        k = k_ref[...]
        v = v_ref[...]
        fk = fk_ref[...]
        kt = k.astype(F32).T
        heads = []
        for h in range(2):
            fkh = jnp.sum(jnp.where(lane == 2 * hp + h, fk, 0.0), axis=1, keepdims=True)
            ones = jnp.where((lane >= spare[h] + 3) & (lane < spare[h] + 6), one, zero)
            kaug = jnp.where(head_sel[h], k, bias_lanes(spare[h], _split3(-fkh))(ones))
            heads.append((kaug, jnp.where(head_sel[h], v, zero), jnp.where(head_sel_t[h], kt, 0.0).astype(BF16)))

        def block(i, k_lo, n_k, q_lo, masked):
            n_q = T - q_lo
            rows = pl.ds(pl.multiple_of(i * T + q_lo, n_q), n_q)
            q = q_ref[rows, :]
            do = do_ref[rows, :]
            stat = stat_sc[i]
            dk = jnp.zeros((n_k, 128), F32)
            dv = jnp.zeros((n_k, 128), F32)
            dqt = jnp.zeros((128, n_q), F32)
            dfs = []
            for h in range(2):
                kaug, vh, kth = heads[h]
                arg = _dot_nt(kaug[k_lo:k_lo + n_k, :], qaug_sc[h, rows, :])
                if masked:
                    rr = lax.broadcasted_iota(jnp.int32, (n_k, n_q), 0) + k_lo
                    cc = lax.broadcasted_iota(jnp.int32, (n_k, n_q), 1) + q_lo
                    arg = jnp.where(rr <= cc, arg, NEG)
                p_t = jnp.exp(arg)
                ds_t = p_t * (_dot_nt(vh[k_lo:k_lo + n_k, :], do) - stat[h:h + 1, q_lo:])
                ds_bf = ds_t.astype(BF16)
                dv = dv + _dot(p_t.astype(BF16), jnp.where(head_sel[h], do, zero))
                dk = dk + _dot(ds_bf, jnp.where(head_sel[h], q, zero))
                dqt = dqt + _dot(kth[:, k_lo:k_lo + n_k], ds_bf)
                dfs.append(jnp.sum(ds_t, axis=1, keepdims=True))
                dfq_ref[i, h:h + 1, q_lo:] += _colsum(ds_t)
            dqt_sc[i, :, q_lo:] += dqt
            return dk, dv, dfs[0], dfs[1]

        def off_diagonal(i, acc):
            return tuple(a + b for a, b in zip(acc, block(i, 0, T, 0, False)))

        half = T // 2
        early = block(j, 0, half, 0, True)
        late = block(j, half, half, half, True)
        acc1 = tuple(jnp.concatenate([a, b], axis=0) for a, b in zip(early, late))
        n_off = n_t - 1 - j
        acc2 = lax.fori_loop(0, n_off // 2,
                             lambda ii, a: off_diagonal(j + 2 + 2 * ii, off_diagonal(j + 1 + 2 * ii, a)), acc1)
        dk_acc, dv_acc, dfa, dfb = lax.fori_loop(0, n_off % 2, lambda _, a: off_diagonal(n_t - 1, a), acc2)
        dk_ref[...] = dk_acc.astype(BF16)
        dv_ref[...] = dv_acc.astype(BF16)
        dfk_ref[...] = -jnp.where(lane == 0, dfa, jnp.where(lane == 1, dfb, 0.0))
        cs_ref[:, 128:256] = cs_ref[:, 128:256] + _colsum(dk_acc)
        cs_ref[:, 256:384] = cs_ref[:, 256:384] + _colsum(dv_acc)

        @pl.when(j == n_t - 1)
        def _():
            def finish(i, tot):
                dq = dqt_sc[i].T
                dq_ref[pl.ds(pl.multiple_of(i * T, T), T), :] = dq.astype(BF16)
                return tot + _colsum(dq)

            cs_ref[:, 0:128] = lax.fori_loop(0, n_t, finish, jnp.zeros((1, 128), F32))

    pair_rows = lambda hp, j: (hp, 0, 0)
    vm = pl.BlockSpec(memory_space=pltpu.VMEM)
    _, r_out, c_out = gw_out4.shape
    return pl.pallas_call(
        body, name="attention_bwd", grid=(N_PAIR, n_t),
        out_shape=(jax.ShapeDtypeStruct((S, D_ATT), BF16), jax.ShapeDtypeStruct((S, D_ATT), BF16),
                   jax.ShapeDtypeStruct((S, D_ATT), BF16), jax.ShapeDtypeStruct((N_PAIR, 1, 384), F32),
                   jax.ShapeDtypeStruct((N_PAIR, S, 128), F32),
                   jax.ShapeDtypeStruct((N_PAIR, n_t, 8, T), F32),
                   jax.ShapeDtypeStruct((r_out, c_out), F32)),
        in_specs=[pl.BlockSpec((S, 128), lambda hp, j: (0, hp)),
                  pl.BlockSpec((S, 128), lambda hp, j: (0, hp)),
                  pl.BlockSpec((S, 128), lambda hp, j: (0, hp)),
                  pl.BlockSpec((None, n_t, 8, T), lambda hp, j: (hp, 0, 0, 0)),
                  pl.BlockSpec((T, 128), lambda hp, j: (j, N_PAIR + hp)),
                  pl.BlockSpec((T, 128), lambda hp, j: (j, 2 * N_PAIR + hp)),
                  pl.BlockSpec((T, 128), lambda hp, j: (j, 0)),
                  vm, vm],
        out_specs=(pl.BlockSpec((S, 128), lambda hp, j: (0, hp)),
                   pl.BlockSpec((T, 128), lambda hp, j: (j, hp)),
                   pl.BlockSpec((T, 128), lambda hp, j: (j, hp)),
                   pl.BlockSpec((None, 1, 384), pair_rows),
                   pl.BlockSpec((None, T, 128), lambda hp, j: (hp, j, 0)),
                   pl.BlockSpec((None, n_t, 8, T), lambda hp, j: (hp, 0, 0, 0)),
                   vm),
        scratch_shapes=[pltpu.VMEM((n_t, 8, T), F32), pltpu.VMEM((n_t, 128, T), F32),
                        pltpu.VMEM((2, S, 128), BF16), pltpu.VMEM((r_out, c_out), F32)]
        + _scatter_scratch(r_out, c_out),
        compiler_params=_params(dimension_semantics=("arbitrary", "arbitrary")),
    )(qkv, datt, att, lse, qkv, qkv, big_f, gw_out4, sc_out)


def _window_counts(first_row, n_rows, window):
    t = lax.broadcasted_iota(jnp.int32, (n_rows, 1), 0) + first_row
    return jnp.minimum((t + 1).astype(F32), float(window))


def _middle(x, tgt, att, g, p, gate, w_mix, b_mix, pool_scale, w_out, b_out, ln_g, ln_b):
    S = x.shape[0]
    tm = min(TM_MID, S)
    halo_blocks = tm // POOL_HALO

    def body(x_ref, t_ref, att_ref, g_ref, p_ref, ph_ref, gate_ref, wm_ref, bm_ref, ps_ref, wo_ref, bo_ref,
             lg_ref, lb_ref,
             dh_ref, datt_ref, dg_ref, dpl_ref, gwo_ref, gwm_ref, vec_ref, loss_ref):
        i = pl.program_id(0)

        @pl.when(i == 0)
        def _():
            gwo_ref[...] = jnp.zeros_like(gwo_ref)
            gwm_ref[...] = jnp.zeros_like(gwm_ref)
            vec_ref[...] = jnp.zeros_like(vec_ref)
            loss_ref[...] = jnp.zeros_like(loss_ref)

        pc = p_ref[...]
        halo = jnp.where(i > 0, ph_ref[...], 0.0)
        pe = jnp.concatenate([halo, pc], axis=0)
        pooled_parts = []
        for gi, w in enumerate(POOL_WINDOWS):
            cur = pe[:, gi * POOL_GROUP:(gi + 1) * POOL_GROUP]
            span = 1
            while span < w:
                cur = cur + pltpu.roll(cur, span, 0)
                span *= 2
            wsum = cur[POOL_HALO:, :]
            mean = wsum / _window_counts(i * tm, tm, w)
            pooled_parts.append(mean - pc[:, gi * POOL_GROUP:(gi + 1) * POOL_GROUP])
        pooled_bf =[v.astype(BF16) for v in pooled_parts]
        mixed = jnp.concatenate([_dot(pooled_bf[gi], wm_ref[gi]) for gi in range(4)], axis=1) + bm_ref[...]
        ps = ps_ref[...]
        pool_out = mixed * ps
        gv = g_ref[...]
        sig = _sigmoid(gv)
        silu = gv * sig
        att = att_ref[...]
        y = jnp.concatenate([att * silu[:, :D_ATT], pool_out * silu[:, D_ATT:]], axis=1)
        y_bf = y.astype(BF16)
        wo = wo_ref[...]
        yo = _dot(y_bf, wo) + bo_ref[...]
        gate = gate_ref[...]
        h = ALPHA * x_ref[...] + gate * yo
        mu = jnp.mean(h, axis=1, keepdims=True)
        hc = h - mu
        var = jnp.mean(hc * hc, axis=1, keepdims=True)
        rstd = lax.rsqrt(var + LN_EPS)
        yhat = hc * rstd
        lg = lg_ref[...]
        out = yhat * lg + lb_ref[...]
        err = out - t_ref[...]
        loss_ref[...] += 0.5 * jnp.sum(jnp.mean(err * err, axis=1, keepdims=True), axis=0, keepdims=True)

        dout = err * (1.0 / D)
        g_ln_b = _colsum(dout)
        g_ln_g = _colsum(dout * yhat)
        dyh = dout * lg
        dh = rstd * (dyh - jnp.mean(dyh, axis=1, keepdims=True)
                     - yhat * jnp.mean(dyh * yhat, axis=1, keepdims=True))
        dh_ref[...] = dh
        d_gate = _colsum(dh * yo)
        dyo = gate * dh
        g_b_out = _colsum(dyo)
        dyo_bf = dyo.astype(BF16)
        gwo_ref[...] += _dot_tn(y_bf, dyo_bf)
        dy = _dot_nt(dyo_bf, wo)
        dsilu = sig * (1.0 + gv * (1.0 - sig))
        dy_a = dy[:, :D_ATT]
        dy_p = dy[:, D_ATT:]
        datt_ref[...] = (dy_a * silu[:, :D_ATT]).astype(BF16)
        dpo = dy_p * silu[:, D_ATT:]
        dg = jnp.concatenate([dy_a * att * dsilu[:, :D_ATT], dy_p * pool_out * dsilu[:, D_ATT:]], axis=1)
        dg_ref[...] = dg.astype(BF16)
        g_dg = _colsum(dg)
        g_ps = _colsum(dpo * mixed)
        dmixed = dpo * ps
        g_bm = _colsum(dmixed)
        dmixed_bf = dmixed.astype(BF16)
        dpl = []
        for gi in range(4):
            dm = dmixed_bf[:, gi * POOL_GROUP:(gi + 1) * POOL_GROUP]
            gwm_ref[gi] += _dot_tn(pooled_bf[gi], dm)
            dpl.append(_dot_nt(dm, wm_ref[gi]))
        dpl_ref[...] = jnp.concatenate(dpl, axis=1)
        vec_ref[0:1, :] += g_ln_g
        vec_ref[1:2, :] += g_ln_b
        vec_ref[2:3, :] += d_gate
        vec_ref[3:4, :] += g_b_out
        vec_ref[4:5, :] += g_dg
        vec_ref[5:6, 0:D_POOL] += g_ps
        vec_ref[6:7, 0:D_POOL] += g_bm

    row = lambda w: pl.BlockSpec((tm, w), lambda i: (i, 0))
    full2 = lambda a: pl.BlockSpec(a.shape, lambda i: (0, 0))
    full3 = lambda a: pl.BlockSpec(a.shape, lambda i: (0, 0, 0))
    return pl.pallas_call(
        body, name="middle", grid=(S // tm,),
        out_shape=(jax.ShapeDtypeStruct((S, D), F32),
                   jax.ShapeDtypeStruct((S, D_ATT), BF16),
                   jax.ShapeDtypeStruct((S, D), BF16),
                   jax.ShapeDtypeStruct((S, D_POOL), F32),
                   jax.ShapeDtypeStruct((D, D), F32),
                   jax.ShapeDtypeStruct((4, POOL_GROUP, POOL_GROUP), F32),
                   jax.ShapeDtypeStruct((8, D), F32),
                   jax.ShapeDtypeStruct((1, 1), F32)),
        in_specs=[row(D), row(D), row(D_ATT), row(D), row(D_POOL),
                  pl.BlockSpec((POOL_HALO, D_POOL), lambda i: (jnp.maximum(i * halo_blocks - 1, 0), 0)),
                  full2(gate), full3(w_mix), full2(b_mix), full2(pool_scale), full2(w_out), full2(b_out),
                  full2(ln_g), full2(ln_b)],
        out_specs=(row(D), row(D_ATT), row(D), row(D_POOL),
                   pl.BlockSpec((D, D), lambda i: (0, 0)),
                   pl.BlockSpec((4, POOL_GROUP, POOL_GROUP), lambda i: (0, 0, 0)),
                   pl.BlockSpec((8, D), lambda i: (0, 0)),
                   pl.BlockSpec((1, 1), lambda i: (0, 0))),
        compiler_params=_params(dimension_semantics=("arbitrary",)),
    )(x, tgt, att, g, p, p, gate, w_mix, b_mix, pool_scale, w_out, b_out, ln_g, ln_b)


def _tail(dpl, dfk, dfq, f):
    S = dpl.shape[0]
    tm = min(T_ATT, S)
    n_t = S // tm
    halo_blocks = tm // POOL_HALO
    last_halo = S // POOL_HALO - 1

    def body(d_ref, dn_ref, dfk_ref, dfq_ref, f_ref, dp_ref, df_ref, cs_ref, carry):
        s = pl.program_id(0)
        i = n_t - 1 - s

        @pl.when(s == 0)
        def _():
            carry[...] = jnp.zeros_like(carry)
            cs_ref[...] = jnp.zeros_like(cs_ref)

        dc = d_ref[...]
        nxt = jnp.where(s > 0, dn_ref[...], 0.0)
        de = jnp.concatenate([dc, nxt], axis=0)
        n_e = tm + POOL_HALO
        parts = []
        for gi, w in enumerate(POOL_WINDOWS):
            cur = de[:, gi * POOL_GROUP:(gi + 1) * POOL_GROUP] / _window_counts(i * tm, n_e, w)
            span = 1
            while span < w:
                cur = cur + pltpu.roll(cur, n_e - span, 0)
                span *= 2
            parts.append(cur[:tm, :] - dc[:, gi * POOL_GROUP:(gi + 1) * POOL_GROUP])
        dp = jnp.concatenate(parts, axis=1)
        dp_ref[...] = dp.astype(BF16)
        cs_ref[0:1, :] += _colsum(dp)

        r = lax.broadcasted_iota(jnp.int32, (tm, tm), 0)
        c = lax.broadcasted_iota(jnp.int32, (tm, tm), 1)
        tri = (r >= c).astype(F32)
        k_cols = dfk_ref[0]
        rows8 = dfq_ref[0]
        for hp in range(1, N_PAIR):
            k_cols = k_cols + pltpu.roll(dfk_ref[hp], 2 * hp, 1)
            rows8 = rows8 + pltpu.roll(dfq_ref[hp], 2 * hp, 0)
        rows8 = rows8 + k_cols.T[0:8, :]
        dlogf8 = jnp.dot(rows8, tri, preferred_element_type=F32, precision=lax.Precision.HIGHEST) + carry[...]
        first = lax.broadcasted_iota(jnp.int32, (1, tm), 1) == 0
        carry[...] = jnp.sum(jnp.where(first, dlogf8, 0.0), axis=1, keepdims=True)
        dlogf = jnp.concatenate([dlogf8, jnp.zeros((128 - 8, tm), F32)], axis=0).T
        df = dlogf * _sigmoid(-f_ref[...])
        df_ref[...] = df.astype(BF16)
        cs_ref[1:2, 0:128] += _colsum(df)

    rev = lambda w: pl.BlockSpec((tm, w), lambda s: (n_t - 1 - s, 0))
    return pl.pallas_call(
        body, name="tail", grid=(n_t,),
        out_shape=(jax.ShapeDtypeStruct((S, D_POOL), BF16), jax.ShapeDtypeStruct((S, 128), BF16),
                   jax.ShapeDtypeStruct((8, D_POOL), F32)),
        in_specs=[rev(D_POOL),
                  pl.BlockSpec((POOL_HALO, D_POOL),
                               lambda s: (jnp.minimum((n_t - s) * halo_blocks, last_halo), 0)),
                  pl.BlockSpec((N_PAIR, tm, 128), lambda s: (0, n_t - 1 - s, 0)),
                  pl.BlockSpec((N_PAIR, None, 8, tm), lambda s: (0, n_t - 1 - s, 0, 0)),
                  rev(128)],
        out_specs=(rev(D_POOL), rev(128), pl.BlockSpec((8, D_POOL), lambda s: (0, 0))),
        scratch_shapes=[pltpu.VMEM((8, 1), F32)],
        compiler_params=_params(dimension_semantics=("arbitrary",)),
    )(dpl, dpl, dfk, dfq, f)


PIECES = ((O_QKV, D_ATT), (O_QKV + D_ATT, D_ATT), (O_QKV + 2 * D_ATT, D_ATT), (O_F, 128), (O_P, D_POOL), (O_G, D))


def _grad_w_in(u, pieces):
    S = u.shape[0]
    tm = min(TM_GW, S)
    n_t = S // tm

    def body(u_ref, *rest):
        piece_refs, out_ref, acc, sem = rest[:6], rest[6], rest[7], rest[8]
        i = pl.program_id(0)

        @pl.when(i == 0)
        def _():
            acc[...] = jnp.zeros_like(acc)

        u_t = u_ref[...]
        for (off, w), ref in zip(PIECES, piece_refs):
            acc[:, off:off + w] += _dot_tn(u_t, ref[...])

        @pl.when(i == n_t - 1)
        def _():
            cp = pltpu.make_async_copy(acc, out_ref, sem)
            cp.start()
            cp.wait()

    return pl.pallas_call(
        body, name="grad_w_in", grid=(n_t,),
        out_shape=jax.ShapeDtypeStruct((D, D_PAD), F32),
        in_specs=[pl.BlockSpec((tm, D), lambda i: (i, 0))]
        + [pl.BlockSpec((tm, w), lambda i: (i, 0)) for _, w in PIECES],
        out_specs=pl.BlockSpec(memory_space=pl.ANY),
        scratch_shapes=[pltpu.VMEM((D, D_PAD), F32), pltpu.SemaphoreType.DMA],
        compiler_params=_params(dimension_semantics=("arbitrary",)),
    )(u, *pieces)


def _grad_x(pieces, wt_pad, dh, x, scale):
    S = x.shape[0]
    tm = min(TM_DU, S)

    def body(*refs):
        piece_refs = refs[:6]
        w_ref, dh_ref, x_ref, sc_ref, gx_ref, vec_ref = refs[6:]

        @pl.when(pl.program_id(0) == 0)
        def _():
            vec_ref[...] = jnp.zeros_like(vec_ref)

        du = jnp.zeros((tm, D), F32)
        for (off, w), ref in zip(PIECES, piece_refs):
            du = du + _dot(ref[...], w_ref[off:off + w, :])
        xv = x_ref[...]
        gx_ref[...] = ALPHA * dh_ref[...] + du * (1.0 + sc_ref[...])
        vec_ref[0:1, :] += _colsum(du)
        vec_ref[1:2, :] += _colsum(du * xv)

    row = lambda w: pl.BlockSpec((tm, w), lambda i: (i, 0))
    return pl.pallas_call(
        body, name="grad_x", grid=(S // tm,),
        out_shape=(jax.ShapeDtypeStruct((S, D), F32), jax.ShapeDtypeStruct((8, D), F32)),
        in_specs=[row(w) for _, w in PIECES]
        + [pl.BlockSpec(wt_pad.shape, lambda i: (0, 0)), row(D), row(D), pl.BlockSpec((1, D), lambda i: (0, 0))],
        out_specs=(row(D), pl.BlockSpec((8, D), lambda i: (0, 0))),
        compiler_params=_params(dimension_semantics=("arbitrary",)),
    )(*pieces, wt_pad, dh, x, scale)


def _grad_ada(c_all, dada_all, dada_cols):
    def body(c_ref, dall_ref, dcol_ref, gw_ref, gb_ref):
        rows = lax.broadcasted_iota(jnp.int32, (8, 1), 0)
        cm = jnp.zeros((8, D), F32)
        dm = jnp.zeros((8, 3 * D), F32)
        for r in range(8):
            cm = jnp.where(rows == r, c_ref[r], cm)
            dm = jnp.where(rows == r, dall_ref[r], dm)
        act = cm * _sigmoid(cm)
        pad = jnp.zeros((8, D), F32)
        lhs = jnp.concatenate([act, pad], axis=0).astype(BF16)
        rhs = jnp.concatenate([dcol_ref[...], jnp.zeros((8, SHARD_ADA), F32)], axis=0).astype(BF16)
        gw_ref[...] = _dot_tn(lhs, rhs)
        gb_ref[...] = _colsum(dm)

    vm = pl.BlockSpec(memory_space=pltpu.VMEM)
    return pl.pallas_call(
        body, name="grad_ada",
        out_shape=(jax.ShapeDtypeStruct((D, SHARD_ADA), F32), jax.ShapeDtypeStruct((1, 3 * D), F32)),
        in_specs=[vm, vm, vm], out_specs=(vm, vm),
        compiler_params=_params(),
    )(c_all, dada_all, dada_cols)


def _adamw_math(w, g, m, v):
    m = ADAM_B1 * m + (1.0 - ADAM_B1) * g
    v = ADAM_B2 * v + (1.0 - ADAM_B2) * (g * g)
    m_hat = m / (1.0 - ADAM_B1 ** ADAM_STEP)
    v_hat = v / (1.0 - ADAM_B2 ** ADAM_STEP)
    delta = -ADAM_LR * (m_hat / (jnp.sqrt(v_hat) + ADAM_EPS) + ADAM_WD * w)
    return delta, m, v


def _adamw(groups, n_steps):
    n = len(groups)

    def body(*refs):
        ins, outs = refs[:4 * n], refs[4 * n:]
        for t in range(n):
            w, g, m, v = (r[...] for r in ins[4 * t:4 * t + 4])
            d, m2, v2 = _adamw_math(w, g, m, v)
            outs[3 * t][...] = d
            outs[3 * t + 1][...] = m2
            outs[3 * t + 2][...] = v2

    in_specs, out_specs, out_shape, args = [], [], [], []
    for (w, g, m, v) in groups:
        rest = w.shape[1:]
        spec = pl.BlockSpec((w.shape[0] // n_steps,) + rest, lambda i, nd=len(rest): (i,) + (0,) * nd)
        in_specs += [spec] * 4
        out_specs += [spec] * 3
        out_shape += [jax.ShapeDtypeStruct(w.shape, F32)] * 3
        args += [w, g, m, v]
    return pl.pallas_call(
        body, name="adamw_%d_%d" % (n, n_steps), grid=(n_steps,),
        out_shape=tuple(out_shape), in_specs=in_specs, out_specs=tuple(out_specs),
        compiler_params=_params(dimension_semantics=("arbitrary",)),
    )(*args)


def _pack_small(parts):
    rows = []
    used = 0
    for name, (first, n_rows) in SMALL_SEGS.items():
        if first > used:
            rows.append(jnp.zeros((first - used, 128), F32))
        flat = parts[name].reshape(-1)
        flat = jnp.pad(flat, (0, n_rows * 128 - flat.shape[0]))
        rows.append(flat.reshape(n_rows, 128))
        used = first + n_rows
    rows.append(jnp.zeros((SMALL_ROWS - used, 128), F32))
    return jnp.concatenate(rows, axis=0)


def _unpack_small(buf, name, shape):
    first, n_rows = SMALL_SEGS[name]
    n = int(np.prod(shape))
    return buf[first:first + n_rows].reshape(-1)[:n].reshape(shape)


def _pad_in(v):
    r = v.shape[0]
    z = jnp.zeros((r, O_P - O_F - N_HEADS), v.dtype)
    return jnp.concatenate([v[:, :3 * D_ATT + N_HEADS], z, v[:, 3 * D_ATT + N_HEADS:]], axis=1)


def _unpad_in(v):
    return jnp.concatenate([v[:, :O_F + N_HEADS], v[:, O_P:]], axis=1)


def _shards_in(v):
    gap = O_P - (O_F + N_HEADS)
    parts = []
    for a in range(N_CHIPS):
        lo, hi = a * SHARD_IN, (a + 1) * SHARD_IN
        cut = O_F + N_HEADS
        if hi <= cut:
            parts.append(v[:, lo:hi])
        elif lo >= cut:
            parts.append(v[:, lo + gap:hi + gap])
        else:
            parts.append(jnp.concatenate([v[:, lo:cut], v[:, cut + gap:hi + gap]], axis=1))
    return jnp.stack(parts, axis=0)


def kernel(x, c, w_ada, b_ada, w_in, b_in, w_pool_mix, b_pool_mix, pool_scale, w_out, b_out, ln_g, ln_b, loss_target, m_w_ada, m_b_ada, m_w_in, m_b_in, m_w_pool_mix, m_b_pool_mix, m_pool_scale, m_w_out, m_b_out, m_ln_g, m_ln_b, v_w_ada, v_b_ada, v_w_in, v_b_in, v_w_pool_mix, v_b_pool_mix, v_pool_scale, v_w_out, v_b_out, v_ln_g, v_ln_b):
    S = x.shape[1]
    T = min(T_ATT, S)
    n_t = S // T
    chip = 2 * lax.axis_index("x") + lax.axis_index("y")
    x2 = x[0]
    tgt = loss_target[0]
    q_scale = jnp.concatenate([jnp.full((1, D_ATT), Q_SCALE, F32), jnp.ones((1, D_PAD - D_ATT), F32)], axis=1)

    to_cols = lambda a: jnp.transpose(a, (2, 0, 1))
    from_cols = lambda a: jnp.transpose(a, (1, 2, 0))
    c_all, ada4, wt_in_all = _gather_and_ada(
        c, w_ada[0], b_ada.reshape(4, 1, SHARD_ADA), to_cols(w_in).reshape(SHARD_IN, D).astype(BF16))
    ada = ada4[:, 0, :].reshape(1, 3 * D)
    shift, scale, gate = ada[:, :D], ada[:, D:2 * D], ada[:, 2 * D:]
    wt_full = wt_in_all.reshape(D_IN, D)
    n_real = 3 * D_ATT + N_HEADS
    wt_pad = jnp.concatenate([wt_full[:D_ATT] * jnp.asarray(Q_SCALE, BF16), wt_full[D_ATT:n_real],
                              jnp.zeros((O_P - n_real, D), BF16), wt_full[n_real:]], axis=0)
    b_pad = _pad_in(b_in) * q_scale
    w_mix_bf = w_pool_mix[0].astype(BF16)

    u, qkv, f, p, g, w_out_all = _in_proj(x2, shift, scale, wt_pad, b_pad, w_out[0].astype(BF16))
    w_out_full = w_out_all.reshape(D, D)
    big_f = _forget_cumsum(f)
    att, lse = _attention_fwd(qkv, big_f)

    dh, datt, dg, dpl, gw_out, gw_mix, vec, loss_part = _middle(
        x2, tgt, att, g, p, gate, w_mix_bf, b_pool_mix.reshape(1, D_POOL), pool_scale, w_out_full, b_out, ln_g, ln_b)
    dq, dk, dv, cs_att, dfk, dfq, g_w_out = _attention_bwd(
        qkv, datt, att, lse, big_f, gw_out.reshape(N_CHIPS, SHARD_OUT, D), jnp.ones((N_CHIPS, 1, D), F32))
    dp, df, cs_tail = _tail(dpl, dfk, dfq, f)
    pieces = (dq, dk, dv, df, dp, dg)
    gw_pad = _grad_w_in(u, pieces)
    grad_x, vec_x = _grad_x(pieces, wt_pad, dh, x2, scale)

    cs_qkv = jnp.transpose(cs_att.reshape(N_PAIR, 3, 128), (1, 0, 2)).reshape(1, 3 * D_ATT)
    gb_pad = jnp.concatenate([cs_qkv, cs_tail[1:2, 0:128], cs_tail[0:1, :], vec[4:5, :]], axis=1) * q_scale
    dada = jnp.concatenate([vec_x[0:1, :], vec_x[1:2, :], vec[2:3, :]], axis=1)
    small = _pack_small({
        "b_in": gb_pad, "w_pool_mix": gw_mix, "b_pool_mix": vec[6:7, :D_POOL], "pool_scale": vec[5:6, :D_POOL],
        "b_out": vec[3:4, :], "ln_g": vec[0:1, :], "ln_b": vec[1:2, :], "loss": loss_part,
        "b_ada": jnp.zeros((1, 3 * D), F32)})

    g_w_in, small_sum, dada_all = _reduce_all(
        _shards_in(gw_pad), _shards_in(q_scale), small[:SMALL_REDUCED_ROWS], dada)
    dada_cols = lax.dynamic_slice(dada_all[:, 0, :], (0, chip * SHARD_ADA), (8, SHARD_ADA))
    g_w_ada, g_b_ada = _grad_ada(c_all, dada_all, dada_cols)
    loss = _unpack_small(small_sum, "loss", (1,))[0]

    grads_small = jnp.concatenate([small_sum, g_b_ada.reshape(24, 128)], axis=0)
    no_param = jnp.zeros((1,), F32)
    small_w = {"b_in": _pad_in(b_in), "w_pool_mix": w_pool_mix, "b_pool_mix": b_pool_mix, "pool_scale": pool_scale,
               "b_out": b_out, "ln_g": ln_g, "ln_b": ln_b, "loss": no_param, "b_ada": b_ada}
    small_m = {"b_in": _pad_in(m_b_in), "w_pool_mix": m_w_pool_mix, "b_pool_mix": m_b_pool_mix,
               "pool_scale": m_pool_scale, "b_out": m_b_out, "ln_g": m_ln_g, "ln_b": m_ln_b, "loss": no_param,
               "b_ada": m_b_ada}
    small_v = {"b_in": _pad_in(v_b_in), "w_pool_mix": v_w_pool_mix, "b_pool_mix": v_b_pool_mix,
               "pool_scale": v_pool_scale, "b_out": v_b_out, "ln_g": v_ln_g, "ln_b": v_ln_b, "loss": no_param,
               "b_ada": v_b_ada}
    big = _adamw([(w_ada[0], g_w_ada, m_w_ada[0], v_w_ada[0]),
                  (w_out[0], g_w_out, m_w_out[0], v_w_out[0])], 8)
    g_w_in_cols = to_cols(g_w_in[None])
    big_in = _adamw([(to_cols(w_in), g_w_in_cols, to_cols(m_w_in), to_cols(v_w_in))], 14)
    sm = _adamw([(_pack_small(small_w), grads_small, _pack_small(small_m), _pack_small(small_v))], 1)

    names = ["w_ada", "b_ada", "w_in", "b_in", "w_pool_mix", "b_pool_mix", "pool_scale", "w_out", "b_out",
             "ln_g", "ln_b"]
    shapes = {"b_ada": (1, 3 * D), "b_in": (1, D_PAD), "w_pool_mix": (1, 4, POOL_GROUP, POOL_GROUP),
              "b_pool_mix": (1, 4, POOL_GROUP), "pool_scale": (1, D_POOL), "b_out": (1, D), "ln_g": (1, D),
              "ln_b": (1, D)}
    big_idx = {"w_ada": 0, "w_out": 1}

    def leaf(kind, name):
        if name == "w_in":
            return from_cols(g_w_in_cols if kind == 0 else big_in[kind - 1])
        if name in big_idx:
            if kind == 0:
                return (g_w_ada, g_w_out)[big_idx[name]][None]
            return big[3 * big_idx[name] + kind - 1][None]
        buf = grads_small if kind == 0 else sm[kind - 1]
        val = _unpack_small(buf, name, shapes[name])
        if name == "b_in":
            val = _unpad_in(val)
        return val

    outs = [loss, grad_x[None]]
    for kind in range(4):
        outs += [leaf(kind, n) for n in names]
    return tuple(outs)
```

```python
import functools

import numpy as np
import jax
import jax.numpy as jnp
from jax import lax
from jax.experimental import pallas as pl
from jax.experimental.pallas import tpu as pltpu

F32 = jnp.float32
BF16 = jnp.bfloat16
MESH = pl.DeviceIdType.MESH

D = 1024
D_ATT = 512
D_POOL = 512
N_HEADS = 8
HEAD_DIM = 64
N_PAIR = N_HEADS // 2
POOL_WINDOWS = (2, 4, 8, 16)
POOL_GROUP = 128
POOL_HALO = 16
LN_EPS = 1e-5
ALPHA = 2.0 ** 0.25
D_IN = 3 * D_ATT + N_HEADS + D_POOL + D_ATT + D_POOL
N_CHIPS = 4
SHARD_IN = D_IN // N_CHIPS
SHARD_ADA = 3 * D // N_CHIPS
SHARD_OUT = D // N_CHIPS

O_QKV, O_F, O_P, O_G, D_PAD = 0, 1536, 1664, 2176, 3200
Q_SCALE = HEAD_DIM ** -0.5

ADAM_LR, ADAM_B1, ADAM_B2, ADAM_EPS, ADAM_WD, ADAM_STEP = 0.001, 0.9, 0.999, 1e-08, 0.01, 10

NEG = -1e30

VMEM_LIMIT = 56 * 1024 * 1024

TM_PROJ = 512
T_ATT = 512
TM_MID = 256
TM_GW = 1024
TM_DU = 512

REL7 = [(0, 0, 1), (0, 1, 0), (0, 1, 1), (1, 0, 0), (1, 0, 1), (1, 1, 0), (1, 1, 1)]
REL3 = [(0, 1), (1, 0), (1, 1)]

SMALL_SEGS = {}
_row = 0
for _name, _n in (("b_in", D_IN), ("w_pool_mix", 65536), ("b_pool_mix", 512), ("pool_scale", 512),
                  ("b_out", 1024), ("ln_g", 1024), ("ln_b", 1024), ("loss", 1)):
    _rows = -(-_n // 1024) * 8
    SMALL_SEGS[_name] = (_row, _rows)
    _row += _rows
SMALL_ROWS = -(-_row // 16) * 16


def _params(**kw):
    return pltpu.CompilerParams(vmem_limit_bytes=VMEM_LIMIT, **kw)


def _flip(v, d):
    return v if d == 0 else 1 - v


def _dot(a, b):
    return jnp.dot(a, b, preferred_element_type=F32)


def _dot_nt(a, b):
    return lax.dot_general(a, b, (((1,), (1,)), ((), ())), preferred_element_type=F32)


def _dot_tn(a, b):
    return lax.dot_general(a, b, (((0,), (0,)), ((), ())), preferred_element_type=F32)


def _sigmoid(v):
    return 1.0 / (1.0 + jnp.exp(-v))


def _colsum(v):
    return jnp.sum(v, axis=0, keepdims=True)


def _gather_stages(pos, src_ref, dst_ref, half, own_sem, s_sem, r_sem, fs_sem, fr_sem):
    x, y, cc, chip, sib = pos
    own = pltpu.make_async_copy(src_ref, dst_ref.at[chip], own_sem)
    first, landed, others = [], [], []
    for k, (dx, dy) in enumerate(REL3):
        px, py = _flip(x, dx), _flip(y, dy)
        first.append(pltpu.make_async_remote_copy(
            src_ref=src_ref.at[half(cc)], dst_ref=dst_ref.at[(chip,) + half(cc)],
            send_sem=s_sem.at[k], recv_sem=r_sem.at[k], device_id=(px, py, cc), device_id_type=MESH))
        landed.append(dst_ref.at[(2 * px + py,) + half(cc)])
        others.append(dst_ref.at[(2 * px + py,) + half(1 - cc)])
    passed = [pltpu.make_async_remote_copy(src_ref=landed[k], dst_ref=landed[k], send_sem=fs_sem.at[k],
                                           recv_sem=fr_sem.at[k], device_id=sib, device_id_type=MESH)
              for k in range(3)]

    def start():
        own.start()
        for cp in first:
            cp.start()

    def forward():
        for k in range(3):
            pltpu.make_async_remote_copy(src_ref=landed[k], dst_ref=landed[k], send_sem=s_sem.at[k],
                                         recv_sem=r_sem.at[k], device_id=sib, device_id_type=MESH).wait_recv()
            passed[k].start()

    def finish():
        for k in range(3):
            pltpu.make_async_remote_copy(src_ref=others[k], dst_ref=others[k], send_sem=fs_sem.at[k],
                                         recv_sem=fr_sem.at[k], device_id=sib, device_id_type=MESH).wait_recv()
        for cp in first + passed:
            cp.wait_send()
        own.wait()

    return start, forward, finish


def _gather_scratch():
    return [pltpu.SemaphoreType.DMA, pltpu.SemaphoreType.DMA((3,)), pltpu.SemaphoreType.DMA((3,)),
            pltpu.SemaphoreType.DMA((3,)), pltpu.SemaphoreType.DMA((3,))]


def _gather_and_ada(c, w_ada, b_ada4, w_in_sh):
    def body(c_ref, w_ref, b_ref, win_ref, call_ref, ada_ref, win_all,
             cslab, sbuf, rbuf, cs_sem, cr_sem, as_sem, ar_sem, *gather_sems):
        x, y, cc = lax.axis_index("x"), lax.axis_index("y"), lax.axis_index("c")
        me = 4 * x + 2 * y + cc
        chip = 2 * x + y
        lane_half = lambda which: (slice(None), pl.ds(pl.multiple_of(which * (D // 2), D // 2), D // 2))
        start, forward, finish = _gather_stages((x, y, cc, chip, (x, y, 1 - cc)), win_ref, win_all, lane_half,
                                                *gather_sems)
        start()

        cslab[...] = jnp.broadcast_to(c_ref[...], (8, D))
        call_ref[me] = cslab[...]
        gathers = []
        for k, (dx, dy, dc) in enumerate(REL7):
            cp = pltpu.make_async_remote_copy(
                src_ref=cslab, dst_ref=call_ref.at[me], send_sem=cs_sem.at[k], recv_sem=cr_sem.at[k],
                device_id=(_flip(x, dx), _flip(y, dy), _flip(cc, dc)), device_id_type=MESH)
            cp.start()
            gathers.append(cp)
        for cp in gathers:
            cp.wait()
        slab_row = lax.broadcasted_iota(jnp.int32, (8, 1), 0)
        mat = jnp.zeros((8, D), F32)
        for r in range(8):
            mat = jnp.where(slab_row == r, call_ref[r], mat)
        act = (mat * _sigmoid(mat)).astype(BF16)
        part = _dot(act, w_ref[...].astype(BF16))
        sends = []
        for k, (dx, dy) in enumerate(REL3):
            px, py = _flip(x, dx), _flip(y, dy)
            r = 4 * px + 2 * py + cc
            piece = _colsum(jnp.where(slab_row == r, part, 0.0))
            sbuf[k] = jnp.broadcast_to(piece, (8, SHARD_ADA))
            cp = pltpu.make_async_remote_copy(
                src_ref=sbuf.at[k], dst_ref=rbuf.at[k], send_sem=as_sem.at[k], recv_sem=ar_sem.at[k],
                device_id=(px, py, cc), device_id_type=MESH)
            cp.start()
            sends.append(cp)
        own_piece = _colsum(jnp.where(slab_row == me, part, 0.0))
        ada_ref[chip] = jnp.broadcast_to(own_piece, (8, SHARD_ADA)) + b_ref[chip]
        for k, (dx, dy) in enumerate(REL3):
            sends[k].wait()
            a = 2 * _flip(x, dx) + _flip(y, dy)
            ada_ref[a] = rbuf[k] + b_ref[a]

        forward()
        finish()

    vm = pl.BlockSpec(memory_space=pltpu.VMEM)
    return pl.pallas_call(
        body, name="gather_and_ada",
        out_shape=(jax.ShapeDtypeStruct((8, 8, D), F32), jax.ShapeDtypeStruct((4, 8, SHARD_ADA), F32),
                   jax.ShapeDtypeStruct((N_CHIPS, SHARD_IN, D), BF16)),
        in_specs=[vm] * 4, out_specs=(vm,) * 3,
        scratch_shapes=[pltpu.VMEM((8, D), F32), pltpu.VMEM((3, 8, SHARD_ADA), F32),
                        pltpu.VMEM((3, 8, SHARD_ADA), F32),
                        pltpu.SemaphoreType.DMA((7,)), pltpu.SemaphoreType.DMA((7,)),
                        pltpu.SemaphoreType.DMA((3,)), pltpu.SemaphoreType.DMA((3,))] + _gather_scratch(),
        compiler_params=_params(),
    )(c, w_ada, b_ada4, w_in_sh)


def _scatter_stages(pos, g_ref, sc_ref, out_ref, sib_buf, send_buf, ici_buf, sem1, sem2s, sem2r, sem3, part=(0, 1)):
    x, y, cc, chip, sib = pos
    q, n_parts = part
    RH = g_ref.shape[1] // 2 // n_parts
    mine = pl.ds(pl.multiple_of((cc * n_parts + q) * RH, RH), RH)
    theirs = pl.ds(pl.multiple_of(((1 - cc) * n_parts + q) * RH, RH), RH)
    cp1 = pltpu.make_async_remote_copy(
        src_ref=g_ref.at[:, theirs, :], dst_ref=sib_buf, send_sem=sem1.at[0], recv_sem=sem1.at[1],
        device_id=sib, device_id_type=MESH)
    sends = []
    for k, (dx, dy) in enumerate(REL3):
        px, py = _flip(x, dx), _flip(y, dy)
        sends.append(pltpu.make_async_remote_copy(
            src_ref=send_buf.at[2 * px + py], dst_ref=ici_buf.at[chip],
            send_sem=sem2s.at[k], recv_sem=sem2r.at[k], device_id=(px, py, cc), device_id_type=MESH))
    cp3 = pltpu.make_async_remote_copy(
        src_ref=out_ref.at[mine, :], dst_ref=out_ref.at[mine, :], send_sem=sem3.at[0], recv_sem=sem3.at[1],
        device_id=sib, device_id_type=MESH)

    def finish1():
        cp1.wait()
        for a in range(N_CHIPS):
            both = g_ref[a, mine, :] + sib_buf[a]
            sib_buf[a] = both
            send_buf[a] = both.astype(BF16)

    def start2():
        for cp in sends:
            cp.start()
        ici_buf[chip] = send_buf[chip]

    def finish2():
        for cp in sends:
            cp.wait()
        own = sib_buf[chip]
        parts = [jnp.where(chip == a, own, ici_buf[a].astype(F32)) for a in range(N_CHIPS)]
        out_ref[mine, :] = ((parts[0] + parts[1]) + (parts[2] + parts[3])) * sc_ref[chip]

    return [(cp1.start, finish1), (start2, finish2), (cp3.start, cp3.wait)]


def _all_reduce_stages(pos, g_ref, out_ref, sib_buf, ici_buf, sem1, sem2s, sem2r, sem3):
    x, y, cc, chip, sib = pos
    RH = g_ref.shape[0] // 2
    mine = pl.ds(pl.multiple_of(cc * RH, 8), RH)
    theirs = pl.ds(pl.multiple_of((1 - cc) * RH, 8), RH)
    cp1 = pltpu.make_async_remote_copy(
        src_ref=g_ref.at[theirs, :], dst_ref=sib_buf, send_sem=sem1.at[0], recv_sem=sem1.at[1],
        device_id=sib, device_id_type=MESH)
    sends = []
    for k, (dx, dy) in enumerate(REL3):
        px, py = _flip(x, dx), _flip(y, dy)
        sends.append(pltpu.make_async_remote_copy(
            src_ref=sib_buf, dst_ref=ici_buf.at[chip],
            send_sem=sem2s.at[k], recv_sem=sem2r.at[k], device_id=(px, py, cc), device_id_type=MESH))
    cp3 = pltpu.make_async_remote_copy(
        src_ref=out_ref.at[mine, :], dst_ref=out_ref.at[mine, :], send_sem=sem3.at[0], recv_sem=sem3.at[1],
        device_id=sib, device_id_type=MESH)

    def finish1():
        cp1.wait()
        sib_buf[...] = g_ref[mine, :] + sib_buf[...]

    def start2():
        for cp in sends:
            cp.start()
        ici_buf[chip] = sib_buf[...]

    def finish2():
        for cp in sends:
            cp.wait()
        out_ref[mine, :] = (ici_buf[0] + ici_buf[1]) + (ici_buf[2] + ici_buf[3])

    return [(cp1.start, finish1), (start2, finish2), (cp3.start, cp3.wait)]


def _stage_sems():
    return [pltpu.SemaphoreType.DMA((2,)), pltpu.SemaphoreType.DMA((3,)),
            pltpu.SemaphoreType.DMA((3,)), pltpu.SemaphoreType.DMA((2,))]


def _scatter_scratch(r, c):
    return [pltpu.VMEM((N_CHIPS, r // 2, c), F32), pltpu.VMEM((N_CHIPS, r // 2, c), BF16),
            pltpu.VMEM((N_CHIPS, r // 2, c), BF16)] + _stage_sems()


def _reduce_all(g4_in, sc_in, small, dada):
    R = small.shape[0]
    W = dada.shape[1]
    r_in, c_in = g4_in.shape[1:]
    n_in = len(_scatter_scratch(r_in // 2, c_in))

    def body(gin_ref, scin_ref, sm_ref, d_ref, oin_ref, osm_ref, dall_ref, *scratch):
        x, y, cc = lax.axis_index("x"), lax.axis_index("y"), lax.axis_index("c")
        me = 4 * x + 2 * y + cc
        pos = (x, y, cc, 2 * x + y, (x, y, 1 - cc))
        dslab, ds_sem, dr_sem = scratch[0:3]
        a_bufs, b_bufs, sm_bufs = scratch[3:3 + n_in], scratch[3 + n_in:3 + 2 * n_in], scratch[3 + 2 * n_in:]
        dslab[...] = jnp.broadcast_to(d_ref[...], (8, W))
        dall_ref[me] = dslab[...]
        gathers = []
        for k, (dx, dy, dc) in enumerate(REL7):
            cp = pltpu.make_async_remote_copy(
                src_ref=dslab, dst_ref=dall_ref.at[me], send_sem=ds_sem.at[k], recv_sem=dr_sem.at[k],
                device_id=(_flip(x, dx), _flip(y, dy), _flip(cc, dc)), device_id_type=MESH)
            cp.start()
            gathers.append(cp)
        first = _scatter_stages(pos, gin_ref, scin_ref, oin_ref, *a_bufs, part=(0, 2))
        second = _scatter_stages(pos, gin_ref, scin_ref, oin_ref, *b_bufs, part=(1, 2))
        little = _all_reduce_stages(pos, sm_ref, osm_ref, *sm_bufs)
        for plan in (first, second, little):
            plan[0][0]()
        first[0][1]()
        first[1][0]()
        little[0][1]()
        little[1][0]()
        second[0][1]()
        second[1][0]()
        first[1][1]()
        first[2][0]()
        second[1][1]()
        second[2][0]()
        little[1][1]()
        little[2][0]()
        for plan in (first, second, little):
            plan[2][1]()
        for cp in gathers:
            cp.wait()

    scratch = [pltpu.VMEM((8, W), F32), pltpu.SemaphoreType.DMA((7,)), pltpu.SemaphoreType.DMA((7,))]
    scratch += _scatter_scratch(r_in // 2, c_in) + _scatter_scratch(r_in // 2, c_in)
    scratch += [pltpu.VMEM((R // 2, 128), F32), pltpu.VMEM((N_CHIPS, R // 2, 128), F32)] + _stage_sems()
    vm = pl.BlockSpec(memory_space=pltpu.VMEM)
    return pl.pallas_call(
        body, name="reduce_all",
        out_shape=(jax.ShapeDtypeStruct(g4_in.shape[1:], F32),
                   jax.ShapeDtypeStruct((R, 128), F32), jax.ShapeDtypeStruct((8, 8, W), F32)),
        in_specs=[vm] * 4, out_specs=(vm,) * 3,
        scratch_shapes=scratch,
        compiler_params=_params(),
    )(g4_in, sc_in, small, dada)


def _in_proj(x, shift, scale, wt_pad, b_pad, w_out_sh):
    S = x.shape[0]
    tm = min(TM_PROJ, S)
    n_steps = S // tm
    assert n_steps >= 3

    def body(x_ref, sh_ref, sc_ref, w_ref, b_ref, wo_ref, u_ref, qkv_ref, f_ref, p_ref, g_ref, wo_all,
             wo_buf, *gather_sems):
        i = pl.program_id(0)
        xx, yy, cc = lax.axis_index("x"), lax.axis_index("y"), lax.axis_index("c")
        row_half = lambda which: (pl.ds(pl.multiple_of(which * (SHARD_OUT // 2), SHARD_OUT // 2), SHARD_OUT // 2),
                                  slice(None))
        start, forward, finish = _gather_stages((xx, yy, cc, 2 * xx + yy, (xx, yy, 1 - cc)), wo_ref, wo_buf,
                                                row_half, *gather_sems)
        pl.when(i == 0)(start)
        pl.when(i == n_steps // 2)(forward)

        @pl.when(i == n_steps - 1)
        def _():
            finish()
            wo_all[...] = wo_buf[...]

        u = (x_ref[...] * (1.0 + sc_ref[...]) + sh_ref[...]).astype(BF16)
        u_ref[...] = u
        qkv_ref[...] = (_dot_nt(u, w_ref[O_QKV:O_F, :]) + b_ref[:, O_QKV:O_F]).astype(BF16)
        f_ref[...] = _dot_nt(u, w_ref[O_F:O_P, :]) + b_ref[:, O_F:O_P]
        p_ref[...] = _dot_nt(u, w_ref[O_P:O_G, :]) + b_ref[:, O_P:O_G]
        g_ref[...] = _dot_nt(u, w_ref[O_G:D_PAD, :]) + b_ref[:, O_G:D_PAD]

    row = lambda w: pl.BlockSpec((tm, w), lambda i: (i, 0))
    full = lambda a: pl.BlockSpec(a.shape, lambda i: (0, 0))
    vm = pl.BlockSpec(memory_space=pltpu.VMEM)
    return pl.pallas_call(
        body, name="in_proj", grid=(n_steps,),
        out_shape=(jax.ShapeDtypeStruct((S, D), BF16), jax.ShapeDtypeStruct((S, 3 * D_ATT), BF16),
                   jax.ShapeDtypeStruct((S, 128), F32), jax.ShapeDtypeStruct((S, D_POOL), F32),
                   jax.ShapeDtypeStruct((S, D), F32), jax.ShapeDtypeStruct((N_CHIPS,) + w_out_sh.shape, BF16)),
        in_specs=[row(D), full(shift), full(scale), full(wt_pad), full(b_pad), vm],
        out_specs=(row(D), row(3 * D_ATT), row(128), row(D_POOL), row(D), vm),
        scratch_shapes=[pltpu.VMEM((N_CHIPS,) + w_out_sh.shape, BF16)] + _gather_scratch(),
        compiler_params=_params(dimension_semantics=("arbitrary",)),
    )(x, shift, scale, wt_pad, b_pad, w_out_sh)


def _forget_cumsum(f):
    S = f.shape[0]
    tm = min(T_ATT, S)

    def body(f_ref, out_ref, carry):
        @pl.when(pl.program_id(0) == 0)
        def _():
            carry[...] = jnp.zeros_like(carry)
        v = f_ref[...]
        logf = jnp.minimum(v, 0.0) - jnp.log(1.0 + jnp.exp(-jnp.abs(v)))
        r = lax.broadcasted_iota(jnp.int32, (tm, tm), 0)
        c = lax.broadcasted_iota(jnp.int32, (tm, tm), 1)
        tri = (r <= c).astype(F32)
        rows8 = logf.T[0:8, :]
        cum8 = jnp.dot(rows8, tri, preferred_element_type=F32, precision=lax.Precision.HIGHEST) + carry[...]
        out_ref[...] = jnp.concatenate([cum8, jnp.zeros((128 - 8, tm), F32)], axis=0).T
        last = lax.broadcasted_iota(jnp.int32, (1, tm), 1) == tm - 1
        carry[...] = jnp.sum(jnp.where(last, cum8, 0.0), axis=1, keepdims=True)

    return pl.pallas_call(
        body, name="forget_cumsum", grid=(S // tm,),
        out_shape=jax.ShapeDtypeStruct((S, 128), F32),
        in_specs=[pl.BlockSpec((tm, 128), lambda i: (i, 0))],
        out_specs=pl.BlockSpec((tm, 128), lambda i: (i, 0)),
        scratch_shapes=[pltpu.VMEM((8, 1), F32)],
        compiler_params=_params(dimension_semantics=("arbitrary",)),
    )(f)


def _split3(v):
    hi = v.astype(BF16)
    rest = v - hi.astype(F32)
    mid = rest.astype(BF16)
    lo = (rest - mid.astype(F32)).astype(BF16)
    return hi, mid, lo


def _attention_fwd(qkv, big_f):
    S = qkv.shape[0]
    T = min(T_ATT, S)
    n_t = S // T

    def body(q_ref, k_ref, v_ref, f_ref, o_ref, lse_ref, kaug_sc, vt_sc, m_sc, l_sc, acc_sc):
        hp = pl.program_id(0)
        i = pl.program_id(1)
        lane = lax.broadcasted_iota(jnp.int32, (1, 128), 1)
        sub = lax.broadcasted_iota(jnp.int32, (128, 1), 0)
        head_sel = (lane < HEAD_DIM, lane >= HEAD_DIM)
        head_sel_t = (sub < HEAD_DIM, sub >= HEAD_DIM)
        spare = (HEAD_DIM, 0)
        zero = jnp.zeros((), BF16)

        @pl.when(i == 0)
        def _():
            def prep(jt, carry):
                rows = pl.ds(pl.multiple_of(jt * T, T), T)
                k = k_ref[rows, :]
                ft = f_ref[rows, :]
                vt = v_ref[rows, :].astype(F32).T
                for h in range(2):
                    fh = jnp.sum(jnp.where(lane == 2 * hp + h, ft, 0.0), axis=1, keepdims=True)
                    hi, mid, lo = _split3(-fh)
                    b = spare[h]
                    bias = jnp.where(lane == b, hi, jnp.where(lane == b + 1, mid, jnp.where(lane == b + 2, lo, zero)))
                    kaug_sc[h, rows, :] = jnp.where(head_sel[h], k, bias)
                    vt_sc[h, jt] = jnp.where(head_sel_t[h], vt, 0.0).astype(BF16)
                return carry

            lax.fori_loop(0, n_t, prep, 0)

        q = q_ref[...]
        q_heads = []
        for h in range(2):
            ones = jnp.where((lane >= spare[h]) & (lane < spare[h] + 3), jnp.ones((), BF16), zero)
            q_heads.append(jnp.where(head_sel[h], q, ones))
        m_sc[...] = jnp.full((8, T), NEG, F32)
        l_sc[...] = jnp.zeros((8, T), F32)
        acc_sc[...] = jnp.zeros((128, T), F32)

        def update(j, k_lo, n_k, q_lo, masked):
            rows = pl.ds(pl.multiple_of(j * T + k_lo, n_k), n_k)
            n_q = T - q_lo
            alphas, pvs = [], []
            for h in range(2):
                s_t = _dot_nt(kaug_sc[h, rows, :], q_heads[h][q_lo:, :])
                if masked:
                    rr = lax.broadcasted_iota(jnp.int32, (n_k, n_q), 0) + k_lo
                    cc = lax.broadcasted_iota(jnp.int32, (n_k, n_q), 1) + q_lo
                    s_t = jnp.where(rr <= cc, s_t, NEG)
                m_prev = m_sc[h:h + 1, q_lo:]
                m_new = jnp.maximum(m_prev, jnp.max(s_t, axis=0, keepdims=True))
                alpha = jnp.exp(m_prev - m_new)
                p_t = jnp.exp(s_t - m_new)
                l_sc[h:h + 1, q_lo:] = alpha * l_sc[h:h + 1, q_lo:] + jnp.sum(p_t, axis=0, keepdims=True)
                m_sc[h:h + 1, q_lo:] = m_new
                alphas.append(alpha)
                pvs.append(_dot(vt_sc[h, j, :, k_lo:k_lo + n_k], p_t.astype(BF16)))
            acc_sc[:, q_lo:] = (acc_sc[:, q_lo:] * jnp.where(head_sel_t[0], alphas[0], alphas[1])
                                + (pvs[0] + pvs[1]))

        def two_off_diagonal(jj, carry):
            update(2 * jj, 0, T, 0, False)
            update(2 * jj + 1, 0, T, 0, False)
            return carry

        lax.fori_loop(0, i // 2, two_off_diagonal, 0)

        @pl.when(i % 2 == 1)
        def _():
            update(i - 1, 0, T, 0, False)

        update(i, 0, T, 0, True)
        l = l_sc[...]
        o_ref[...] = (acc_sc[...] / jnp.where(head_sel_t[0], l[0:1, :], l[1:2, :])).T
        is_head = lax.broadcasted_iota(jnp.int32, (8, 1), 0) < 2
        lse_ref[...] = jnp.where(is_head, m_sc[...] + jnp.log(jnp.where(is_head, l, 1.0)), 0.0)

    return pl.pallas_call(
        body, name="attention_fwd", grid=(N_PAIR, n_t),
        out_shape=(jax.ShapeDtypeStruct((S, D_ATT), F32), jax.ShapeDtypeStruct((N_PAIR, n_t, 8, T), F32)),
        in_specs=[pl.BlockSpec((T, 128), lambda hp, i: (i, hp)),
                  pl.BlockSpec((S, 128), lambda hp, i: (0, N_PAIR + hp)),
                  pl.BlockSpec((S, 128), lambda hp, i: (0, 2 * N_PAIR + hp)),
                  pl.BlockSpec((S, 128), lambda hp, i: (0, 0))],
        out_specs=(pl.BlockSpec((T, 128), lambda hp, i: (i, hp)),
                   pl.BlockSpec((None, None, 8, T), lambda hp, i: (hp, i, 0, 0))),
        scratch_shapes=[pltpu.VMEM((2, S, 128), BF16), pltpu.VMEM((2, n_t, 128, T), BF16),
                        pltpu.VMEM((8, T), F32), pltpu.VMEM((8, T), F32), pltpu.VMEM((128, T), F32)],
        compiler_params=_params(dimension_semantics=("arbitrary", "arbitrary")),
    )(qkv, qkv, qkv, big_f)


def _attention_bwd(qkv, datt, att, lse, big_f, gw_out4, sc_out):
    S = qkv.shape[0]
    T = min(T_ATT, S)
    n_t = S // T
    n_steps = N_PAIR * n_t
    marks = (0, n_steps // 8, n_steps // 2, n_steps // 2 + n_steps // 8)

    def body(q_ref, do_ref, o_ref, lse_ref, k_ref, v_ref, fk_ref, gout_ref, scout_ref,
             dq_ref, dk_ref, dv_ref, cs_ref, dfk_ref, dfq_ref, oout_ref, stat_sc, dqt_sc, qaug_sc,
             out_buf, *red_bufs):
        hp = pl.program_id(0)
        j = pl.program_id(1)
        x, y, cc = lax.axis_index("x"), lax.axis_index("y"), lax.axis_index("c")
        plan = _scatter_stages((x, y, cc, 2 * x + y, (x, y, 1 - cc)), gout_ref, scout_ref, out_buf, *red_bufs)
        step = hp * n_t + j
        for n, mark in enumerate(marks):
            @pl.when(step == mark)
            def _(n=n):
                if n > 0:
                    plan[n - 1][1]()
                if n < 3:
                    plan[n][0]()
                else:
                    oout_ref[...] = out_buf[...]

        lane = lax.broadcasted_iota(jnp.int32, (1, 128), 1)
        sub = lax.broadcasted_iota(jnp.int32, (128, 1), 0)
        head_sel = (lane < HEAD_DIM, lane >= HEAD_DIM)
        head_sel_t = (sub < HEAD_DIM, sub >= HEAD_DIM)
        spare = (HEAD_DIM, 0)
        zero = jnp.zeros((), BF16)
        one = jnp.ones((), BF16)

        def bias_lanes(first, pieces):
            hi, mid, lo = pieces
            return lambda rest: jnp.where(lane == first, hi, jnp.where(lane == first + 1, mid,
                                                                        jnp.where(lane == first + 2, lo, rest)))

        @pl.when(j == 0)
        def _():
            dqt_sc[...] = jnp.zeros_like(dqt_sc)
            cs_ref[...] = jnp.zeros_like(cs_ref)
            dfq_ref[...] = jnp.zeros_like(dfq_ref)

            def prep(i, carry):
                rows = pl.ds(pl.multiple_of(i * T, T), T)
                q = q_ref[rows, :]
                do = do_ref[rows, :]
                prod = o_ref[rows, :] * do.astype(F32)
                d_a = jnp.sum(jnp.where(head_sel[0], prod, 0.0), axis=1, keepdims=True)
                d_b = jnp.sum(jnp.where(head_sel[0], 0.0, prod), axis=1, keepdims=True)
                delta_t = jnp.where(head_sel[0], d_a, d_b).T
                stat_sc[i, 0:1, :] = delta_t[0:1, :]
                stat_sc[i, 1:2, :] = delta_t[HEAD_DIM:HEAD_DIM + 1, :]
                lse = lse_ref[i]
                lse_cols = jnp.where(head_sel_t[0], lse[0:1, :], lse[1:2, :]).T
                for h in range(2):
                    neg_lse = -lse_cols[:, h * HEAD_DIM:h * HEAD_DIM + 1]
                    ones = jnp.where((lane >= spare[h]) & (lane < spare[h] + 3), one, zero)
                    qaug_sc[h, rows, :] = jnp.where(head_sel[h], q, bias_lanes(spare[h] + 3, _split3(neg_lse))(ones))
                return carry

            lax.fori_loop(0, n_t, prep, 0)

        k = k_ref[...]
        v = v_ref[...]
        fk = fk_ref[...]
        kt = k.astype(F32).T
        heads = []
        for h in range(2):
            fkh = jnp.sum(jnp.where(lane == 2 * hp + h, fk, 0.0), axis=1, keepdims=True)
            ones = jnp.where((lane >= spare[h] + 3) & (lane < spare[h] + 6), one, zero)
            kaug = jnp.where(head_sel[h], k, bias_lanes(spare[h], _split3(-fkh))(ones))
            heads.append((kaug, jnp.where(head_sel[h], v, zero), jnp.where(head_sel_t[h], kt, 0.0).astype(BF16)))

        def block(i, k_lo, n_k, q_lo, masked):
            n_q = T - q_lo
            rows = pl.ds(pl.multiple_of(i * T + q_lo, n_q), n_q)
            q = q_ref[rows, :]
            do = do_ref[rows, :]
            stat = stat_sc[i]
            dk = jnp.zeros((n_k, 128), F32)
            dv = jnp.zeros((n_k, 128), F32)
            dqt = jnp.zeros((128, n_q), F32)
            dfs = []
            for h in range(2):
                kaug, vh, kth = heads[h]
                arg = _dot_nt(kaug[k_lo:k_lo + n_k, :], qaug_sc[h, rows, :])
                if masked:
                    rr = lax.broadcasted_iota(jnp.int32, (n_k, n_q), 0) + k_lo
                    cc = lax.broadcasted_iota(jnp.int32, (n_k, n_q), 1) + q_lo
                    arg = jnp.where(rr <= cc, arg, NEG)
                p_t = jnp.exp(arg)
                ds_t = p_t * (_dot_nt(vh[k_lo:k_lo + n_k, :], do) - stat[h:h + 1, q_lo:])
                ds_bf = ds_t.astype(BF16)
                dv = dv + _dot(p_t.astype(BF16), jnp.where(head_sel[h], do, zero))
                dk = dk + _dot(ds_bf, jnp.where(head_sel[h], q, zero))
                dqt = dqt + _dot(kth[:, k_lo:k_lo + n_k], ds_bf)
                dfs.append(jnp.sum(ds_t, axis=1, keepdims=True))
                dfq_ref[i, h:h + 1, q_lo:] += _colsum(ds_t)
            dqt_sc[i, :, q_lo:] += dqt
            return dk, dv, dfs[0], dfs[1]

        def off_diagonal(i, acc):
            return tuple(a + b for a, b in zip(acc, block(i, 0, T, 0, False)))

        half = T // 2
        early = block(j, 0, half, 0, True)
        late = block(j, half, half, half, True)
        acc1 = tuple(jnp.concatenate([a, b], axis=0) for a, b in zip(early, late))
        n_off = n_t - 1 - j
        acc2 = lax.fori_loop(0, n_off // 2,
                             lambda ii, a: off_diagonal(j + 2 + 2 * ii, off_diagonal(j + 1 + 2 * ii, a)), acc1)
        dk_acc, dv_acc, dfa, dfb = lax.fori_loop(0, n_off % 2, lambda _, a: off_diagonal(n_t - 1, a), acc2)
        dk_ref[...] = dk_acc.astype(BF16)
        dv_ref[...] = dv_acc.astype(BF16)
        dfk_ref[...] = -jnp.where(lane == 0, dfa, jnp.where(lane == 1, dfb, 0.0))
        cs_ref[:, 128:256] = cs_ref[:, 128:256] + _colsum(dk_acc)
        cs_ref[:, 256:384] = cs_ref[:, 256:384] + _colsum(dv_acc)

        @pl.when(j == n_t - 1)
        def _():
            def finish(i, tot):
                dq = dqt_sc[i].T
                dq_ref[pl.ds(pl.multiple_of(i * T, T), T), :] = dq.astype(BF16)
                return tot + _colsum(dq)

            cs_ref[:, 0:128] = lax.fori_loop(0, n_t, finish, jnp.zeros((1, 128), F32))

    pair_rows = lambda hp, j: (hp, 0, 0)
    vm = pl.BlockSpec(memory_space=pltpu.VMEM)
    _, r_out, c_out = gw_out4.shape
    return pl.pallas_call(
        body, name="attention_bwd", grid=(N_PAIR, n_t),
        out_shape=(jax.ShapeDtypeStruct((S, D_ATT), BF16), jax.ShapeDtypeStruct((S, D_ATT), BF16),
                   jax.ShapeDtypeStruct((S, D_ATT), BF16), jax.ShapeDtypeStruct((N_PAIR, 1, 384), F32),
                   jax.ShapeDtypeStruct((N_PAIR, S, 128), F32),
                   jax.ShapeDtypeStruct((N_PAIR, n_t, 8, T), F32),
                   jax.ShapeDtypeStruct((r_out, c_out), F32)),
        in_specs=[pl.BlockSpec((S, 128), lambda hp, j: (0, hp)),
                  pl.BlockSpec((S, 128), lambda hp, j: (0, hp)),
                  pl.BlockSpec((S, 128), lambda hp, j: (0, hp)),
                  pl.BlockSpec((None, n_t, 8, T), lambda hp, j: (hp, 0, 0, 0)),
                  pl.BlockSpec((T, 128), lambda hp, j: (j, N_PAIR + hp)),
                  pl.BlockSpec((T, 128), lambda hp, j: (j, 2 * N_PAIR + hp)),
                  pl.BlockSpec((T, 128), lambda hp, j: (j, 0)),
                  vm, vm],
        out_specs=(pl.BlockSpec((S, 128), lambda hp, j: (0, hp)),
                   pl.BlockSpec((T, 128), lambda hp, j: (j, hp)),
                   pl.BlockSpec((T, 128), lambda hp, j: (j, hp)),
                   pl.BlockSpec((None, 1, 384), pair_rows),
                   pl.BlockSpec((None, T, 128), lambda hp, j: (hp, j, 0)),
                   pl.BlockSpec((None, n_t, 8, T), lambda hp, j: (hp, 0, 0, 0)),
                   vm),
        scratch_shapes=[pltpu.VMEM((n_t, 8, T), F32), pltpu.VMEM((n_t, 128, T), F32),
                        pltpu.VMEM((2, S, 128), BF16), pltpu.VMEM((r_out, c_out), F32)]
        + _scatter_scratch(r_out, c_out),
        compiler_params=_params(dimension_semantics=("arbitrary", "arbitrary")),
    )(qkv, datt, att, lse, qkv, qkv, big_f, gw_out4, sc_out)


def _window_counts(first_row, n_rows, window):
    t = lax.broadcasted_iota(jnp.int32, (n_rows, 1), 0) + first_row
    return jnp.minimum((t + 1).astype(F32), float(window))


def _middle(x, tgt, att, g, p, gate, w_mix, b_mix, pool_scale, w_out, b_out, ln_g, ln_b):
    S = x.shape[0]
    tm = min(TM_MID, S)
    halo_blocks = tm // POOL_HALO

    def body(x_ref, t_ref, att_ref, g_ref, p_ref, ph_ref, gate_ref, wm_ref, bm_ref, ps_ref, wo_ref, bo_ref,
             lg_ref, lb_ref,
             dh_ref, datt_ref, dg_ref, dpl_ref, gwo_ref, gwm_ref, vec_ref, loss_ref):
        i = pl.program_id(0)

        @pl.when(i == 0)
        def _():
            gwo_ref[...] = jnp.zeros_like(gwo_ref)
            gwm_ref[...] = jnp.zeros_like(gwm_ref)
            vec_ref[...] = jnp.zeros_like(vec_ref)
            loss_ref[...] = jnp.zeros_like(loss_ref)

        pc = p_ref[...]
        halo = jnp.where(i > 0, ph_ref[...], 0.0)
        pe = jnp.concatenate([halo, pc], axis=0)
        pooled_parts = []
        for gi, w in enumerate(POOL_WINDOWS):
            cur = pe[:, gi * POOL_GROUP:(gi + 1) * POOL_GROUP]
            span = 1
            while span < w:
                cur = cur + pltpu.roll(cur, span, 0)
                span *= 2
            wsum = cur[POOL_HALO:, :]
            mean = wsum / _window_counts(i * tm, tm, w)
            pooled_parts.append(mean - pc[:, gi * POOL_GROUP:(gi + 1) * POOL_GROUP])
        pooled_bf =[v.astype(BF16) for v in pooled_parts]
        mixed = jnp.concatenate([_dot(pooled_bf[gi], wm_ref[gi]) for gi in range(4)], axis=1) + bm_ref[...]
        ps = ps_ref[...]
        pool_out = mixed * ps
        gv = g_ref[...]
        sig = _sigmoid(gv)
        silu = gv * sig
        att = att_ref[...]
        y = jnp.concatenate([att * silu[:, :D_ATT], pool_out * silu[:, D_ATT:]], axis=1)
        y_bf = y.astype(BF16)
        wo = wo_ref[...]
        yo = _dot(y_bf, wo) + bo_ref[...]
        gate = gate_ref[...]
        h = ALPHA * x_ref[...] + gate * yo
        mu = jnp.mean(h, axis=1, keepdims=True)
        hc = h - mu
        var = jnp.mean(hc * hc, axis=1, keepdims=True)
        rstd = lax.rsqrt(var + LN_EPS)
        yhat = hc * rstd
        lg = lg_ref[...]
        out = yhat * lg + lb_ref[...]
        err = out - t_ref[...]
        loss_ref[...] += 0.5 * jnp.sum(jnp.mean(err * err, axis=1, keepdims=True), axis=0, keepdims=True)

        dout = err * (1.0 / D)
        g_ln_b = _colsum(dout)
        g_ln_g = _colsum(dout * yhat)
        dyh = dout * lg
        dh = rstd * (dyh - jnp.mean(dyh, axis=1, keepdims=True)
                     - yhat * jnp.mean(dyh * yhat, axis=1, keepdims=True))
        dh_ref[...] = dh
        d_gate = _colsum(dh * yo)
        dyo = gate * dh
        g_b_out = _colsum(dyo)
        dyo_bf = dyo.astype(BF16)
        gwo_ref[...] += _dot_tn(y_bf, dyo_bf)
        dy = _dot_nt(dyo_bf, wo)
        dsilu = sig * (1.0 + gv * (1.0 - sig))
        dy_a = dy[:, :D_ATT]
        dy_p = dy[:, D_ATT:]
        datt_ref[...] = (dy_a * silu[:, :D_ATT]).astype(BF16)
        dpo = dy_p * silu[:, D_ATT:]
        dg = jnp.concatenate([dy_a * att * dsilu[:, :D_ATT], dy_p * pool_out * dsilu[:, D_ATT:]], axis=1)
        dg_ref[...] = dg.astype(BF16)
        g_dg = _colsum(dg)
        g_ps = _colsum(dpo * mixed)
        dmixed = dpo * ps
        g_bm = _colsum(dmixed)
        dmixed_bf = dmixed.astype(BF16)
        dpl = []
        for gi in range(4):
            dm = dmixed_bf[:, gi * POOL_GROUP:(gi + 1) * POOL_GROUP]
            gwm_ref[gi] += _dot_tn(pooled_bf[gi], dm)
            dpl.append(_dot_nt(dm, wm_ref[gi]))
        dpl_ref[...] = jnp.concatenate(dpl, axis=1)
        vec_ref[0:1, :] += g_ln_g
        vec_ref[1:2, :] += g_ln_b
        vec_ref[2:3, :] += d_gate
        vec_ref[3:4, :] += g_b_out
        vec_ref[4:5, :] += g_dg
        vec_ref[5:6, 0:D_POOL] += g_ps
        vec_ref[6:7, 0:D_POOL] += g_bm

    row = lambda w: pl.BlockSpec((tm, w), lambda i: (i, 0))
    full2 = lambda a: pl.BlockSpec(a.shape, lambda i: (0, 0))
    full3 = lambda a: pl.BlockSpec(a.shape, lambda i: (0, 0, 0))
    return pl.pallas_call(
        body, name="middle", grid=(S // tm,),
        out_shape=(jax.ShapeDtypeStruct((S, D), F32),
                   jax.ShapeDtypeStruct((S, D_ATT), BF16),
                   jax.ShapeDtypeStruct((S, D), BF16),
                   jax.ShapeDtypeStruct((S, D_POOL), F32),
                   jax.ShapeDtypeStruct((D, D), F32),
                   jax.ShapeDtypeStruct((4, POOL_GROUP, POOL_GROUP), F32),
                   jax.ShapeDtypeStruct((8, D), F32),
                   jax.ShapeDtypeStruct((1, 1), F32)),
        in_specs=[row(D), row(D), row(D_ATT), row(D), row(D_POOL),
                  pl.BlockSpec((POOL_HALO, D_POOL), lambda i: (jnp.maximum(i * halo_blocks - 1, 0), 0)),
                  full2(gate), full3(w_mix), full2(b_mix), full2(pool_scale), full2(w_out), full2(b_out),
                  full2(ln_g), full2(ln_b)],
        out_specs=(row(D), row(D_ATT), row(D), row(D_POOL),
                   pl.BlockSpec((D, D), lambda i: (0, 0)),
                   pl.BlockSpec((4, POOL_GROUP, POOL_GROUP), lambda i: (0, 0, 0)),
                   pl.BlockSpec((8, D), lambda i: (0, 0)),
                   pl.BlockSpec((1, 1), lambda i: (0, 0))),
        compiler_params=_params(dimension_semantics=("arbitrary",)),
    )(x, tgt, att, g, p, p, gate, w_mix, b_mix, pool_scale, w_out, b_out, ln_g, ln_b)


def _tail(dpl, dfk, dfq, f):
    S = dpl.shape[0]
    tm = min(T_ATT, S)
    n_t = S // tm
    halo_blocks = tm // POOL_HALO
    last_halo = S // POOL_HALO - 1

    def body(d_ref, dn_ref, dfk_ref, dfq_ref, f_ref, dp_ref, df_ref, cs_ref, carry):
        s = pl.program_id(0)
        i = n_t - 1 - s

        @pl.when(s == 0)
        def _():
            carry[...] = jnp.zeros_like(carry)
            cs_ref[...] = jnp.zeros_like(cs_ref)

        dc = d_ref[...]
        nxt = jnp.where(s > 0, dn_ref[...], 0.0)
        de = jnp.concatenate([dc, nxt], axis=0)
        n_e = tm + POOL_HALO
        parts = []
        for gi, w in enumerate(POOL_WINDOWS):
            cur = de[:, gi * POOL_GROUP:(gi + 1) * POOL_GROUP] / _window_counts(i * tm, n_e, w)
            span = 1
            while span < w:
                cur = cur + pltpu.roll(cur, n_e - span, 0)
                span *= 2
            parts.append(cur[:tm, :] - dc[:, gi * POOL_GROUP:(gi + 1) * POOL_GROUP])
        dp = jnp.concatenate(parts, axis=1)
        dp_ref[...] = dp.astype(BF16)
        cs_ref[0:1, :] += _colsum(dp)

        r = lax.broadcasted_iota(jnp.int32, (tm, tm), 0)
        c = lax.broadcasted_iota(jnp.int32, (tm, tm), 1)
        tri = (r >= c).astype(F32)
        k_cols = dfk_ref[0]
        rows8 = dfq_ref[0]
        for hp in range(1, N_PAIR):
            k_cols = k_cols + pltpu.roll(dfk_ref[hp], 2 * hp, 1)
            rows8 = rows8 + pltpu.roll(dfq_ref[hp], 2 * hp, 0)
        rows8 = rows8 + k_cols.T[0:8, :]
        dlogf8 = jnp.dot(rows8, tri, preferred_element_type=F32, precision=lax.Precision.HIGHEST) + carry[...]
        first = lax.broadcasted_iota(jnp.int32, (1, tm), 1) == 0
        carry[...] = jnp.sum(jnp.where(first, dlogf8, 0.0), axis=1, keepdims=True)
        dlogf = jnp.concatenate([dlogf8, jnp.zeros((128 - 8, tm), F32)], axis=0).T
        df = dlogf * _sigmoid(-f_ref[...])
        df_ref[...] = df.astype(BF16)
        cs_ref[1:2, 0:128] += _colsum(df)

    rev = lambda w: pl.BlockSpec((tm, w), lambda s: (n_t - 1 - s, 0))
    return pl.pallas_call(
        body, name="tail", grid=(n_t,),
        out_shape=(jax.ShapeDtypeStruct((S, D_POOL), BF16), jax.ShapeDtypeStruct((S, 128), BF16),
                   jax.ShapeDtypeStruct((8, D_POOL), F32)),
        in_specs=[rev(D_POOL),
                  pl.BlockSpec((POOL_HALO, D_POOL),
                               lambda s: (jnp.minimum((n_t - s) * halo_blocks, last_halo), 0)),
                  pl.BlockSpec((N_PAIR, tm, 128), lambda s: (0, n_t - 1 - s, 0)),
                  pl.BlockSpec((N_PAIR, None, 8, tm), lambda s: (0, n_t - 1 - s, 0, 0)),
                  rev(128)],
        out_specs=(rev(D_POOL), rev(128), pl.BlockSpec((8, D_POOL), lambda s: (0, 0))),
        scratch_shapes=[pltpu.VMEM((8, 1), F32)],
        compiler_params=_params(dimension_semantics=("arbitrary",)),
    )(dpl, dpl, dfk, dfq, f)


PIECES = ((O_QKV, D_ATT), (O_QKV + D_ATT, D_ATT), (O_QKV + 2 * D_ATT, D_ATT), (O_F, 128), (O_P, D_POOL), (O_G, D))


def _grad_w_in(u, pieces):
    S = u.shape[0]
    tm = min(TM_GW, S)
    n_t = S // tm

    def body(u_ref, *rest):
        piece_refs, out_ref, acc, sem = rest[:6], rest[6], rest[7], rest[8]
        i = pl.program_id(0)

        @pl.when(i == 0)
        def _():
            acc[...] = jnp.zeros_like(acc)

        u_t = u_ref[...]
        for (off, w), ref in zip(PIECES, piece_refs):
            acc[:, off:off + w] += _dot_tn(u_t, ref[...])

        @pl.when(i == n_t - 1)
        def _():
            cp = pltpu.make_async_copy(acc, out_ref, sem)
            cp.start()
            cp.wait()

    return pl.pallas_call(
        body, name="grad_w_in", grid=(n_t,),
        out_shape=jax.ShapeDtypeStruct((D, D_PAD), F32),
        in_specs=[pl.BlockSpec((tm, D), lambda i: (i, 0))]
        + [pl.BlockSpec((tm, w), lambda i: (i, 0)) for _, w in PIECES],
        out_specs=pl.BlockSpec(memory_space=pl.ANY),
        scratch_shapes=[pltpu.VMEM((D, D_PAD), F32), pltpu.SemaphoreType.DMA],
        compiler_params=_params(dimension_semantics=("arbitrary",)),
    )(u, *pieces)


def _grad_x(pieces, wt_pad, dh, x, scale):
    S = x.shape[0]
    tm = min(TM_DU, S)

    def body(*refs):
        piece_refs = refs[:6]
        w_ref, dh_ref, x_ref, sc_ref, gx_ref, vec_ref = refs[6:]

        @pl.when(pl.program_id(0) == 0)
        def _():
            vec_ref[...] = jnp.zeros_like(vec_ref)

        du = jnp.zeros((tm, D), F32)
        for (off, w), ref in zip(PIECES, piece_refs):
            du = du + _dot(ref[...], w_ref[off:off + w, :])
        xv = x_ref[...]
        gx_ref[...] = ALPHA * dh_ref[...] + du * (1.0 + sc_ref[...])
        vec_ref[0:1, :] += _colsum(du)
        vec_ref[1:2, :] += _colsum(du * xv)

    row = lambda w: pl.BlockSpec((tm, w), lambda i: (i, 0))
    return pl.pallas_call(
        body, name="grad_x", grid=(S // tm,),
        out_shape=(jax.ShapeDtypeStruct((S, D), F32), jax.ShapeDtypeStruct((8, D), F32)),
        in_specs=[row(w) for _, w in PIECES]
        + [pl.BlockSpec(wt_pad.shape, lambda i: (0, 0)), row(D), row(D), pl.BlockSpec((1, D), lambda i: (0, 0))],
        out_specs=(row(D), pl.BlockSpec((8, D), lambda i: (0, 0))),
        compiler_params=_params(dimension_semantics=("arbitrary",)),
    )(*pieces, wt_pad, dh, x, scale)


def _grad_ada(c_all, dada_all, dada_cols):
    def body(c_ref, dall_ref, dcol_ref, gw_ref, gb_ref):
        rows = lax.broadcasted_iota(jnp.int32, (8, 1), 0)
        cm = jnp.zeros((8, D), F32)
        dm = jnp.zeros((8, 3 * D), F32)
        for r in range(8):
            cm = jnp.where(rows == r, c_ref[r], cm)
            dm = jnp.where(rows == r, dall_ref[r], dm)
        act = cm * _sigmoid(cm)
        pad = jnp.zeros((8, D), F32)
        lhs = jnp.concatenate([act, pad], axis=0).astype(BF16)
        rhs = jnp.concatenate([dcol_ref[...], jnp.zeros((8, SHARD_ADA), F32)], axis=0).astype(BF16)
        gw_ref[...] = _dot_tn(lhs, rhs)
        gb_ref[...] = _colsum(dm)

    vm = pl.BlockSpec(memory_space=pltpu.VMEM)
    return pl.pallas_call(
        body, name="grad_ada",
        out_shape=(jax.ShapeDtypeStruct((D, SHARD_ADA), F32), jax.ShapeDtypeStruct((1, 3 * D), F32)),
        in_specs=[vm, vm, vm], out_specs=(vm, vm),
        compiler_params=_params(),
    )(c_all, dada_all, dada_cols)


def _adamw_math(w, g, m, v):
    m = ADAM_B1 * m + (1.0 - ADAM_B1) * g
    v = ADAM_B2 * v + (1.0 - ADAM_B2) * (g * g)
    m_hat = m / (1.0 - ADAM_B1 ** ADAM_STEP)
    v_hat = v / (1.0 - ADAM_B2 ** ADAM_STEP)
    delta = -ADAM_LR * (m_hat / (jnp.sqrt(v_hat) + ADAM_EPS) + ADAM_WD * w)
    return delta, m, v


def _adamw(groups, n_steps):
    n = len(groups)

    def body(*refs):
        ins, outs = refs[:4 * n], refs[4 * n:]
        for t in range(n):
            w, g, m, v = (r[...] for r in ins[4 * t:4 * t + 4])
            d, m2, v2 = _adamw_math(w, g, m, v)
            outs[3 * t][...] = d
            outs[3 * t + 1][...] = m2
            outs[3 * t + 2][...] = v2

    in_specs, out_specs, out_shape, args = [], [], [], []
    for (w, g, m, v) in groups:
        rest = w.shape[1:]
        spec = pl.BlockSpec((w.shape[0] // n_steps,) + rest, lambda i, nd=len(rest): (i,) + (0,) * nd)
        in_specs += [spec] * 4
        out_specs += [spec] * 3
        out_shape += [jax.ShapeDtypeStruct(w.shape, F32)] * 3
        args += [w, g, m, v]
    return pl.pallas_call(
        body, name="adamw_%d_%d" % (n, n_steps), grid=(n_steps,),
        out_shape=tuple(out_shape), in_specs=in_specs, out_specs=tuple(out_specs),
        compiler_params=_params(dimension_semantics=("arbitrary",)),
    )(*args)


def _adamw_small(small_sum, g_b_ada, params):
    n = len(params)

    def body(gs_ref, gba_ref, *refs):
        ins, outs = refs[:3 * n], refs[3 * n:]
        for t, (name, w0, _, _) in enumerate(params):
            w_ref, m_ref, v_ref = ins[3 * t:3 * t + 3]
            first = SMALL_SEGS[name][0] if name in SMALL_SEGS else None
            if w0.shape[0] > 1:
                pieces = [((slice(None), slice(None)), gs_ref[first:first + w0.shape[0], :])]
            else:
                pieces = []
                for r in range(-(-w0.shape[1] // 128)):
                    lanes = slice(128 * r, min(128 * r + 128, w0.shape[1]))
                    g = gba_ref[0:1, lanes] if first is None else gs_ref[first + r:first + r + 1, 0:lanes.stop - lanes.start]
                    pieces.append(((slice(0, 1), lanes), g))
            for where, g in pieces:
                d, m2, v2 = _adamw_math(w_ref[where], g, m_ref[where], v_ref[where])
                for ref, val in zip(outs[4 * t:4 * t + 4], (g, d, m2, v2)):
                    ref[where] = val

    vm = pl.BlockSpec(memory_space=pltpu.VMEM)
    args = [small_sum, g_b_ada]
    out_shape = []
    for _, w, m, v in params:
        args += [w, m, v]
        out_shape += [jax.ShapeDtypeStruct(w.shape, F32)] * 4
    return pl.pallas_call(
        body, name="adamw_small",
        out_shape=tuple(out_shape), in_specs=[vm] * len(args), out_specs=(vm,) * len(out_shape),
        compiler_params=_params(),
    )(*args)


def _pack_small(parts):
    rows = []
    used = 0
    for name, (first, n_rows) in SMALL_SEGS.items():
        if first > used:
            rows.append(jnp.zeros((first - used, 128), F32))
        flat = parts[name].reshape(-1)
        flat = jnp.pad(flat, (0, n_rows * 128 - flat.shape[0]))
        rows.append(flat.reshape(n_rows, 128))
        used = first + n_rows
    rows.append(jnp.zeros((SMALL_ROWS - used, 128), F32))
    return jnp.concatenate(rows, axis=0)


def _unpack_small(buf, name, shape):
    first, n_rows = SMALL_SEGS[name]
    n = int(np.prod(shape))
    return buf[first:first + n_rows].reshape(-1)[:n].reshape(shape)


def _pad_in(v):
    r = v.shape[0]
    z = jnp.zeros((r, O_P - O_F - N_HEADS), v.dtype)
    return jnp.concatenate([v[:, :3 * D_ATT + N_HEADS], z, v[:, 3 * D_ATT + N_HEADS:]], axis=1)


def _unpad_in(v):
    return jnp.concatenate([v[:, :O_F + N_HEADS], v[:, O_P:]], axis=1)


def _shards_in(v):
    gap = O_P - (O_F + N_HEADS)
    parts = []
    for a in range(N_CHIPS):
        lo, hi = a * SHARD_IN, (a + 1) * SHARD_IN
        cut = O_F + N_HEADS
        if hi <= cut:
            parts.append(v[:, lo:hi])
        elif lo >= cut:
            parts.append(v[:, lo + gap:hi + gap])
        else:
            parts.append(jnp.concatenate([v[:, lo:cut], v[:, cut + gap:hi + gap]], axis=1))
    return jnp.stack(parts, axis=0)


def kernel(x, c, w_ada, b_ada, w_in, b_in, w_pool_mix, b_pool_mix, pool_scale, w_out, b_out, ln_g, ln_b, loss_target, m_w_ada, m_b_ada, m_w_in, m_b_in, m_w_pool_mix, m_b_pool_mix, m_pool_scale, m_w_out, m_b_out, m_ln_g, m_ln_b, v_w_ada, v_b_ada, v_w_in, v_b_in, v_w_pool_mix, v_b_pool_mix, v_pool_scale, v_w_out, v_b_out, v_ln_g, v_ln_b):
    S = x.shape[1]
    T = min(T_ATT, S)
    n_t = S // T
    chip = 2 * lax.axis_index("x") + lax.axis_index("y")
    x2 = x[0]
    tgt = loss_target[0]
    q_scale = jnp.concatenate([jnp.full((1, D_ATT), Q_SCALE, F32), jnp.ones((1, D_PAD - D_ATT), F32)], axis=1)

    to_cols = lambda a: jnp.transpose(a, (2, 0, 1))
    from_cols = lambda a: jnp.transpose(a, (1, 2, 0))
    c_all, ada4, wt_in_all = _gather_and_ada(
        c, w_ada[0], b_ada.reshape(4, 1, SHARD_ADA), to_cols(w_in).reshape(SHARD_IN, D).astype(BF16))
    ada = ada4[:, 0, :].reshape(1, 3 * D)
    shift, scale, gate = ada[:, :D], ada[:, D:2 * D], ada[:, 2 * D:]
    wt_full = wt_in_all.reshape(D_IN, D)
    n_real = 3 * D_ATT + N_HEADS
    wt_pad = jnp.concatenate([wt_full[:D_ATT] * jnp.asarray(Q_SCALE, BF16), wt_full[D_ATT:n_real],
                              jnp.zeros((O_P - n_real, D), BF16), wt_full[n_real:]], axis=0)
    b_pad = _pad_in(b_in) * q_scale
    w_mix_bf = w_pool_mix[0].astype(BF16)

    u, qkv, f, p, g, w_out_all = _in_proj(x2, shift, scale, wt_pad, b_pad, w_out[0].astype(BF16))
    w_out_full = w_out_all.reshape(D, D)
    big_f = _forget_cumsum(f)
    att, lse = _attention_fwd(qkv, big_f)

    dh, datt, dg, dpl, gw_out, gw_mix, vec, loss_part = _middle(
        x2, tgt, att, g, p, gate, w_mix_bf, b_pool_mix.reshape(1, D_POOL), pool_scale, w_out_full, b_out, ln_g, ln_b)
    dq, dk, dv, cs_att, dfk, dfq, g_w_out = _attention_bwd(
        qkv, datt, att, lse, big_f, gw_out.reshape(N_CHIPS, SHARD_OUT, D), jnp.ones((N_CHIPS, 1, D), F32))
    dp, df, cs_tail = _tail(dpl, dfk, dfq, f)
    pieces = (dq, dk, dv, df, dp, dg)
    gw_pad = _grad_w_in(u, pieces)
    grad_x, vec_x = _grad_x(pieces, wt_pad, dh, x2, scale)

    cs_qkv = jnp.transpose(cs_att.reshape(N_PAIR, 3, 128), (1, 0, 2)).reshape(1, 3 * D_ATT)
    gb_pad = jnp.concatenate([cs_qkv, cs_tail[1:2, 0:128], cs_tail[0:1, :], vec[4:5, :]], axis=1) * q_scale
    dada = jnp.concatenate([vec_x[0:1, :], vec_x[1:2, :], vec[2:3, :]], axis=1)
    small = _pack_small({
        "b_in": _unpad_in(gb_pad), "w_pool_mix": gw_mix, "b_pool_mix": vec[6:7, :D_POOL],
        "pool_scale": vec[5:6, :D_POOL], "b_out": vec[3:4, :], "ln_g": vec[0:1, :], "ln_b": vec[1:2, :],
        "loss": loss_part})

    g_w_in, small_sum, dada_all = _reduce_all(
        _shards_in(gw_pad), _shards_in(q_scale), small, dada)
    dada_cols = lax.dynamic_slice(dada_all[:, 0, :], (0, chip * SHARD_ADA), (8, SHARD_ADA))
    g_w_ada, g_b_ada = _grad_ada(c_all, dada_all, dada_cols)
    loss = _unpack_small(small_sum, "loss", (1,))[0]

    big = _adamw([(w_ada[0], g_w_ada, m_w_ada[0], v_w_ada[0]),
                  (w_out[0], g_w_out, m_w_out[0], v_w_out[0])], 8)
    g_w_in_cols = to_cols(g_w_in[None])
    big_in = _adamw([(to_cols(w_in), g_w_in_cols, to_cols(m_w_in), to_cols(v_w_in))], 14)
    tiles = lambda a: a.reshape(4 * POOL_GROUP, POOL_GROUP)
    flat = lambda a: a.reshape(1, D_POOL)
    small_params = [("b_ada", b_ada, m_b_ada, v_b_ada), ("b_in", b_in, m_b_in, v_b_in),
                    ("w_pool_mix", tiles(w_pool_mix), tiles(m_w_pool_mix), tiles(v_w_pool_mix)),
                    ("b_pool_mix", flat(b_pool_mix), flat(m_b_pool_mix), flat(v_b_pool_mix)),
                    ("pool_scale", pool_scale, m_pool_scale, v_pool_scale), ("b_out", b_out, m_b_out, v_b_out),
                    ("ln_g", ln_g, m_ln_g, v_ln_g), ("ln_b", ln_b, m_ln_b, v_ln_b)]
    sm = _adamw_small(small_sum, g_b_ada, small_params)
    sm_idx = {p[0]: n for n, p in enumerate(small_params)}
    shapes = {"w_pool_mix": (1, 4, POOL_GROUP, POOL_GROUP), "b_pool_mix": (1, 4, POOL_GROUP)}

    names = ["w_ada", "b_ada", "w_in", "b_in", "w_pool_mix", "b_pool_mix", "pool_scale", "w_out", "b_out",
             "ln_g", "ln_b"]
    big_idx = {"w_ada": 0, "w_out": 1}

    def leaf(kind, name):
        if name == "w_in":
            return from_cols(g_w_in_cols if kind == 0 else big_in[kind - 1])
        if name in big_idx:
            if kind == 0:
                return (g_w_ada, g_w_out)[big_idx[name]][None]
            return big[3 * big_idx[name] + kind - 1][None]
        val = sm[4 * sm_idx[name] + kind]
        return val.reshape(shapes[name]) if name in shapes else val

    outs = [loss, grad_x[None]]
    for kind in range(4):
        outs += [leaf(kind, n) for n in names]
    return tuple(outs)
```

```python
import functools

import numpy as np
import jax
import jax.numpy as jnp
from jax import lax
from jax.experimental import pallas as pl
from jax.experimental.pallas import tpu as pltpu

F32 = jnp.float32
BF16 = jnp.bfloat16
MESH = pl.DeviceIdType.MESH

D = 1024
D_ATT = 512
D_POOL = 512
N_HEADS = 8
HEAD_DIM = 64
N_PAIR = N_HEADS // 2
POOL_WINDOWS = (2, 4, 8, 16)
POOL_GROUP = 128
POOL_HALO = 16
LN_EPS = 1e-5
ALPHA = 2.0 ** 0.25
D_IN = 3 * D_ATT + N_HEADS + D_POOL + D_ATT + D_POOL
N_CHIPS = 4
SHARD_IN = D_IN // N_CHIPS
SHARD_ADA = 3 * D // N_CHIPS
SHARD_OUT = D // N_CHIPS

O_QKV, O_F, O_P, O_G, D_PAD = 0, 1536, 1664, 2176, 3200
Q_SCALE = HEAD_DIM ** -0.5

ADAM_LR, ADAM_B1, ADAM_B2, ADAM_EPS, ADAM_WD, ADAM_STEP = 0.001, 0.9, 0.999, 1e-08, 0.01, 10

NEG = -1e30

VMEM_LIMIT = 56 * 1024 * 1024

TM_PROJ = 512
T_ATT = 512
TM_MID = 256
TM_GW = 1024
TM_DU = 512

REL7 = [(0, 0, 1), (0, 1, 0), (0, 1, 1), (1, 0, 0), (1, 0, 1), (1, 1, 0), (1, 1, 1)]
REL3 = [(0, 1), (1, 0), (1, 1)]

SMALL_SEGS = {}
_row = 0
for _name, _n in (("b_in", D_IN), ("w_pool_mix", 65536), ("b_pool_mix", 512), ("pool_scale", 512),
                  ("b_out", 1024), ("ln_g", 1024), ("ln_b", 1024), ("loss", 1)):
    _rows = -(-_n // 1024) * 8
    SMALL_SEGS[_name] = (_row, _rows)
    _row += _rows
SMALL_ROWS = -(-_row // 16) * 16


def _params(**kw):
    return pltpu.CompilerParams(vmem_limit_bytes=VMEM_LIMIT, **kw)


def _flip(v, d):
    return v if d == 0 else 1 - v


def _dot(a, b):
    return jnp.dot(a, b, preferred_element_type=F32)


def _dot_nt(a, b):
    return lax.dot_general(a, b, (((1,), (1,)), ((), ())), preferred_element_type=F32)


def _dot_tn(a, b):
    return lax.dot_general(a, b, (((0,), (0,)), ((), ())), preferred_element_type=F32)


def _sigmoid(v):
    return 1.0 / (1.0 + jnp.exp(-v))


def _colsum(v):
    return jnp.sum(v, axis=0, keepdims=True)


def _gather_stages(pos, src_ref, dst_ref, half, own_sem, s_sem, r_sem, fs_sem, fr_sem):
    x, y, cc, chip, sib = pos
    own = pltpu.make_async_copy(src_ref, dst_ref.at[chip], own_sem)
    first, landed, others = [], [], []
    for k, (dx, dy) in enumerate(REL3):
        px, py = _flip(x, dx), _flip(y, dy)
        first.append(pltpu.make_async_remote_copy(
            src_ref=src_ref.at[half(cc)], dst_ref=dst_ref.at[(chip,) + half(cc)],
            send_sem=s_sem.at[k], recv_sem=r_sem.at[k], device_id=(px, py, cc), device_id_type=MESH))
        landed.append(dst_ref.at[(2 * px + py,) + half(cc)])
        others.append(dst_ref.at[(2 * px + py,) + half(1 - cc)])
    passed = [pltpu.make_async_remote_copy(src_ref=landed[k], dst_ref=landed[k], send_sem=fs_sem.at[k],
                                           recv_sem=fr_sem.at[k], device_id=sib, device_id_type=MESH)
              for k in range(3)]

    def start():
        own.start()
        for cp in first:
            cp.start()

    def forward():
        for k in range(3):
            pltpu.make_async_remote_copy(src_ref=landed[k], dst_ref=landed[k], send_sem=s_sem.at[k],
                                         recv_sem=r_sem.at[k], device_id=sib, device_id_type=MESH).wait_recv()
            passed[k].start()

    def finish():
        for k in range(3):
            pltpu.make_async_remote_copy(src_ref=others[k], dst_ref=others[k], send_sem=fs_sem.at[k],
                                         recv_sem=fr_sem.at[k], device_id=sib, device_id_type=MESH).wait_recv()
        for cp in first + passed:
            cp.wait_send()
        own.wait()

    return start, forward, finish


def _gather_scratch():
    return [pltpu.SemaphoreType.DMA, pltpu.SemaphoreType.DMA((3,)), pltpu.SemaphoreType.DMA((3,)),
            pltpu.SemaphoreType.DMA((3,)), pltpu.SemaphoreType.DMA((3,))]


def _gather_and_ada(c, w_ada, b_ada4, w_in_sh):
    def body(c_ref, w_ref, b_ref, win_ref, call_ref, ada_ref, wt_pad_ref,
             win_all, cslab, sbuf, rbuf, cs_sem, cr_sem, as_sem, ar_sem, *gather_sems):
        x, y, cc = lax.axis_index("x"), lax.axis_index("y"), lax.axis_index("c")
        me = 4 * x + 2 * y + cc
        chip = 2 * x + y
        lane_half = lambda which: (slice(None), pl.ds(pl.multiple_of(which * (D // 2), D // 2), D // 2))
        start, forward, finish = _gather_stages((x, y, cc, chip, (x, y, 1 - cc)), win_ref, win_all, lane_half,
                                                *gather_sems)
        start()

        cslab[...] = jnp.broadcast_to(c_ref[...], (8, D))
        call_ref[me] = cslab[...]
        gathers = []
        for k, (dx, dy, dc) in enumerate(REL7):
            cp = pltpu.make_async_remote_copy(
                src_ref=cslab, dst_ref=call_ref.at[me], send_sem=cs_sem.at[k], recv_sem=cr_sem.at[k],
                device_id=(_flip(x, dx), _flip(y, dy), _flip(cc, dc)), device_id_type=MESH)
            cp.start()
            gathers.append(cp)
        for cp in gathers:
            cp.wait()
        slab_row = lax.broadcasted_iota(jnp.int32, (8, 1), 0)
        mat = jnp.zeros((8, D), F32)
        for r in range(8):
            mat = jnp.where(slab_row == r, call_ref[r], mat)
        act = (mat * _sigmoid(mat)).astype(BF16)
        part = _dot(act, w_ref[...].astype(BF16))
        sends = []
        for k, (dx, dy) in enumerate(REL3):
            px, py = _flip(x, dx), _flip(y, dy)
            r = 4 * px + 2 * py + cc
            piece = _colsum(jnp.where(slab_row == r, part, 0.0))
            sbuf[k] = jnp.broadcast_to(piece, (8, SHARD_ADA))
            cp = pltpu.make_async_remote_copy(
                src_ref=sbuf.at[k], dst_ref=rbuf.at[k], send_sem=as_sem.at[k], recv_sem=ar_sem.at[k],
                device_id=(px, py, cc), device_id_type=MESH)
            cp.start()
            sends.append(cp)
        own_piece = _colsum(jnp.where(slab_row == me, part, 0.0))
        ada_ref[chip] = jnp.broadcast_to(own_piece, (8, SHARD_ADA)) + b_ref[chip]
        for k, (dx, dy) in enumerate(REL3):
            sends[k].wait()
            a = 2 * _flip(x, dx) + _flip(y, dy)
            ada_ref[a] = rbuf[k] + b_ref[a]

        forward()
        finish()
        n_real = 3 * D_ATT + N_HEADS
        for a in range(N_CHIPS):
            lo, hi = a * SHARD_IN, (a + 1) * SHARD_IN
            for s0, s1 in ((lo, min(hi, D_ATT)), (max(lo, D_ATT), min(hi, n_real)), (max(lo, n_real), hi)):
                if s0 < s1:
                    rows = win_all[a, s0 - lo:s1 - lo, :]
                    if s1 <= D_ATT:
                        rows = rows * jnp.asarray(Q_SCALE, BF16)
                    shift = O_P - n_real if s0 >= n_real else 0
                    wt_pad_ref[s0 + shift:s1 + shift, :] = rows
        wt_pad_ref[n_real:O_P, :] = jnp.zeros((O_P - n_real, D), BF16)

    vm = pl.BlockSpec(memory_space=pltpu.VMEM)
    return pl.pallas_call(
        body, name="gather_and_ada",
        out_shape=(jax.ShapeDtypeStruct((8, 8, D), F32), jax.ShapeDtypeStruct((4, 8, SHARD_ADA), F32),
                   jax.ShapeDtypeStruct((D_PAD, D), BF16)),
        in_specs=[vm] * 4, out_specs=(vm,) * 3,
        scratch_shapes=[pltpu.VMEM((N_CHIPS, SHARD_IN, D), BF16),
                        pltpu.VMEM((8, D), F32), pltpu.VMEM((3, 8, SHARD_ADA), F32),
                        pltpu.VMEM((3, 8, SHARD_ADA), F32),
                        pltpu.SemaphoreType.DMA((7,)), pltpu.SemaphoreType.DMA((7,)),
                        pltpu.SemaphoreType.DMA((3,)), pltpu.SemaphoreType.DMA((3,))] + _gather_scratch(),
        compiler_params=_params(),
    )(c, w_ada, b_ada4, w_in_sh)


def _scatter_stages(pos, g_ref, sc_ref, out_ref, sib_buf, send_buf, ici_buf, sem1, sem2s, sem2r, sem3, part=(0, 1)):
    x, y, cc, chip, sib = pos
    q, n_parts = part
    RH = g_ref.shape[1] // 2 // n_parts
    mine = pl.ds(pl.multiple_of((cc * n_parts + q) * RH, RH), RH)
    theirs = pl.ds(pl.multiple_of(((1 - cc) * n_parts + q) * RH, RH), RH)
    cp1 = pltpu.make_async_remote_copy(
        src_ref=g_ref.at[:, theirs, :], dst_ref=sib_buf, send_sem=sem1.at[0], recv_sem=sem1.at[1],
        device_id=sib, device_id_type=MESH)
    sends = []
    for k, (dx, dy) in enumerate(REL3):
        px, py = _flip(x, dx), _flip(y, dy)
        sends.append(pltpu.make_async_remote_copy(
            src_ref=send_buf.at[2 * px + py], dst_ref=ici_buf.at[chip],
            send_sem=sem2s.at[k], recv_sem=sem2r.at[k], device_id=(px, py, cc), device_id_type=MESH))
    cp3 = pltpu.make_async_remote_copy(
        src_ref=out_ref.at[mine, :], dst_ref=out_ref.at[mine, :], send_sem=sem3.at[0], recv_sem=sem3.at[1],
        device_id=sib, device_id_type=MESH)

    def finish1():
        cp1.wait()
        for a in range(N_CHIPS):
            both = g_ref[a, mine, :] + sib_buf[a]
            sib_buf[a] = both
            send_buf[a] = both.astype(BF16)

    def start2():
        for cp in sends:
            cp.start()
        ici_buf[chip] = send_buf[chip]

    def finish2():
        for cp in sends:
            cp.wait()
        own = sib_buf[chip]
        parts = [jnp.where(chip == a, own, ici_buf[a].astype(F32)) for a in range(N_CHIPS)]
        out_ref[mine, :] = ((parts[0] + parts[1]) + (parts[2] + parts[3])) * sc_ref[chip]

    return [(cp1.start, finish1), (start2, finish2), (cp3.start, cp3.wait)]


def _all_reduce_stages(pos, g_ref, out_ref, sib_buf, ici_buf, sem1, sem2s, sem2r, sem3):
    x, y, cc, chip, sib = pos
    RH = g_ref.shape[0] // 2
    mine = pl.ds(pl.multiple_of(cc * RH, 8), RH)
    theirs = pl.ds(pl.multiple_of((1 - cc) * RH, 8), RH)
    cp1 = pltpu.make_async_remote_copy(
        src_ref=g_ref.at[theirs, :], dst_ref=sib_buf, send_sem=sem1.at[0], recv_sem=sem1.at[1],
        device_id=sib, device_id_type=MESH)
    sends = []
    for k, (dx, dy) in enumerate(REL3):
        px, py = _flip(x, dx), _flip(y, dy)
        sends.append(pltpu.make_async_remote_copy(
            src_ref=sib_buf, dst_ref=ici_buf.at[chip],
            send_sem=sem2s.at[k], recv_sem=sem2r.at[k], device_id=(px, py, cc), device_id_type=MESH))
    cp3 = pltpu.make_async_remote_copy(
        src_ref=out_ref.at[mine, :], dst_ref=out_ref.at[mine, :], send_sem=sem3.at[0], recv_sem=sem3.at[1],
        device_id=sib, device_id_type=MESH)

    def finish1():
        cp1.wait()
        sib_buf[...] = g_ref[mine, :] + sib_buf[...]

    def start2():
        for cp in sends:
            cp.start()
        ici_buf[chip] = sib_buf[...]

    def finish2():
        for cp in sends:
            cp.wait()
        out_ref[mine, :] = (ici_buf[0] + ici_buf[1]) + (ici_buf[2] + ici_buf[3])

    return [(cp1.start, finish1), (start2, finish2), (cp3.start, cp3.wait)]


def _stage_sems():
    return [pltpu.SemaphoreType.DMA((2,)), pltpu.SemaphoreType.DMA((3,)),
            pltpu.SemaphoreType.DMA((3,)), pltpu.SemaphoreType.DMA((2,))]


def _scatter_scratch(r, c):
    return [pltpu.VMEM((N_CHIPS, r // 2, c), F32), pltpu.VMEM((N_CHIPS, r // 2, c), BF16),
            pltpu.VMEM((N_CHIPS, r // 2, c), BF16)] + _stage_sems()


def _reduce_all(g4_in, sc_in, small, dada):
    R = small.shape[0]
    W = dada.shape[1]
    r_in, c_in = g4_in.shape[1:]
    n_in = len(_scatter_scratch(r_in // 2, c_in))

    def body(gin_ref, scin_ref, sm_ref, d_ref, oin_ref, osm_ref, dall_ref, *scratch):
        x, y, cc = lax.axis_index("x"), lax.axis_index("y"), lax.axis_index("c")
        me = 4 * x + 2 * y + cc
        pos = (x, y, cc, 2 * x + y, (x, y, 1 - cc))
        dslab, ds_sem, dr_sem = scratch[0:3]
        a_bufs, b_bufs, sm_bufs = scratch[3:3 + n_in], scratch[3 + n_in:3 + 2 * n_in], scratch[3 + 2 * n_in:]
        dslab[...] = jnp.broadcast_to(d_ref[...], (8, W))
        dall_ref[me] = dslab[...]
        gathers = []
        for k, (dx, dy, dc) in enumerate(REL7):
            cp = pltpu.make_async_remote_copy(
                src_ref=dslab, dst_ref=dall_ref.at[me], send_sem=ds_sem.at[k], recv_sem=dr_sem.at[k],
                device_id=(_flip(x, dx), _flip(y, dy), _flip(cc, dc)), device_id_type=MESH)
            cp.start()
            gathers.append(cp)
        first = _scatter_stages(pos, gin_ref, scin_ref, oin_ref, *a_bufs, part=(0, 2))
        second = _scatter_stages(pos, gin_ref, scin_ref, oin_ref, *b_bufs, part=(1, 2))
        little = _all_reduce_stages(pos, sm_ref, osm_ref, *sm_bufs)
        for plan in (first, second, little):
            plan[0][0]()
        first[0][1]()
        first[1][0]()
        little[0][1]()
        little[1][0]()
        second[0][1]()
        second[1][0]()
        first[1][1]()
        first[2][0]()
        second[1][1]()
        second[2][0]()
        little[1][1]()
        little[2][0]()
        for plan in (first, second, little):
            plan[2][1]()
        for cp in gathers:
            cp.wait()

    scratch = [pltpu.VMEM((8, W), F32), pltpu.SemaphoreType.DMA((7,)), pltpu.SemaphoreType.DMA((7,))]
    scratch += _scatter_scratch(r_in // 2, c_in) + _scatter_scratch(r_in // 2, c_in)
    scratch += [pltpu.VMEM((R // 2, 128), F32), pltpu.VMEM((N_CHIPS, R // 2, 128), F32)] + _stage_sems()
    vm = pl.BlockSpec(memory_space=pltpu.VMEM)
    return pl.pallas_call(
        body, name="reduce_all",
        out_shape=(jax.ShapeDtypeStruct(g4_in.shape[1:], F32),
                   jax.ShapeDtypeStruct((R, 128), F32), jax.ShapeDtypeStruct((8, 8, W), F32)),
        in_specs=[vm] * 4, out_specs=(vm,) * 3,
        scratch_shapes=scratch,
        compiler_params=_params(),
    )(g4_in, sc_in, small, dada)


def _in_proj(x, shift, scale, wt_pad, b_pad, w_out_sh):
    S = x.shape[0]
    tm = min(TM_PROJ, S)
    n_steps = S // tm
    assert n_steps >= 3

    def body(x_ref, sh_ref, sc_ref, w_ref, b_ref, wo_ref, u_ref, qkv_ref, f_ref, p_ref, g_ref, wo_all,
             wo_buf, *gather_sems):
        i = pl.program_id(0)
        xx, yy, cc = lax.axis_index("x"), lax.axis_index("y"), lax.axis_index("c")
        row_half = lambda which: (pl.ds(pl.multiple_of(which * (SHARD_OUT // 2), SHARD_OUT // 2), SHARD_OUT // 2),
                                  slice(None))
        start, forward, finish = _gather_stages((xx, yy, cc, 2 * xx + yy, (xx, yy, 1 - cc)), wo_ref, wo_buf,
                                                row_half, *gather_sems)
        pl.when(i == 0)(start)
        pl.when(i == n_steps // 2)(forward)

        @pl.when(i == n_steps - 1)
        def _():
            finish()
            wo_all[...] = wo_buf[...]

        u = (x_ref[...] * (1.0 + sc_ref[...]) + sh_ref[...]).astype(BF16)
        u_ref[...] = u
        qkv_ref[...] = (_dot_nt(u, w_ref[O_QKV:O_F, :]) + b_ref[:, O_QKV:O_F]).astype(BF16)
        f_ref[...] = _dot_nt(u, w_ref[O_F:O_P, :]) + b_ref[:, O_F:O_P]
        p_ref[...] = _dot_nt(u, w_ref[O_P:O_G, :]) + b_ref[:, O_P:O_G]
        g_ref[...] = _dot_nt(u, w_ref[O_G:D_PAD, :]) + b_ref[:, O_G:D_PAD]

    row = lambda w: pl.BlockSpec((tm, w), lambda i: (i, 0))
    full = lambda a: pl.BlockSpec(a.shape, lambda i: (0, 0))
    vm = pl.BlockSpec(memory_space=pltpu.VMEM)
    return pl.pallas_call(
        body, name="in_proj", grid=(n_steps,),
        out_shape=(jax.ShapeDtypeStruct((S, D), BF16), jax.ShapeDtypeStruct((S, 3 * D_ATT), BF16),
                   jax.ShapeDtypeStruct((S, 128), F32), jax.ShapeDtypeStruct((S, D_POOL), F32),
                   jax.ShapeDtypeStruct((S, D), F32), jax.ShapeDtypeStruct((N_CHIPS,) + w_out_sh.shape, BF16)),
        in_specs=[row(D), full(shift), full(scale), full(wt_pad), full(b_pad), vm],
        out_specs=(row(D), row(3 * D_ATT), row(128), row(D_POOL), row(D), vm),
        scratch_shapes=[pltpu.VMEM((N_CHIPS,) + w_out_sh.shape, BF16)] + _gather_scratch(),
        compiler_params=_params(dimension_semantics=("arbitrary",)),
    )(x, shift, scale, wt_pad, b_pad, w_out_sh)


def _forget_cumsum(f):
    S = f.shape[0]
    tm = min(T_ATT, S)

    def body(f_ref, out_ref, carry):
        @pl.when(pl.program_id(0) == 0)
        def _():
            carry[...] = jnp.zeros_like(carry)
        v = f_ref[...]
        logf = jnp.minimum(v, 0.0) - jnp.log(1.0 + jnp.exp(-jnp.abs(v)))
        r = lax.broadcasted_iota(jnp.int32, (tm, tm), 0)
        c = lax.broadcasted_iota(jnp.int32, (tm, tm), 1)
        tri = (r <= c).astype(F32)
        rows8 = logf.T[0:8, :]
        cum8 = jnp.dot(rows8, tri, preferred_element_type=F32, precision=lax.Precision.HIGHEST) + carry[...]
        out_ref[...] = jnp.concatenate([cum8, jnp.zeros((128 - 8, tm), F32)], axis=0).T
        last = lax.broadcasted_iota(jnp.int32, (1, tm), 1) == tm - 1
        carry[...] = jnp.sum(jnp.where(last, cum8, 0.0), axis=1, keepdims=True)

    return pl.pallas_call(
        body, name="forget_cumsum", grid=(S // tm,),
        out_shape=jax.ShapeDtypeStruct((S, 128), F32),
        in_specs=[pl.BlockSpec((tm, 128), lambda i: (i, 0))],
        out_specs=pl.BlockSpec((tm, 128), lambda i: (i, 0)),
        scratch_shapes=[pltpu.VMEM((8, 1), F32)],
        compiler_params=_params(dimension_semantics=("arbitrary",)),
    )(f)


def _split3(v):
    hi = v.astype(BF16)
    rest = v - hi.astype(F32)
    mid = rest.astype(BF16)
    lo = (rest - mid.astype(F32)).astype(BF16)
    return hi, mid, lo


def _attention_fwd(qkv, big_f):
    S = qkv.shape[0]
    T = min(T_ATT, S)
    n_t = S // T

    def body(q_ref, k_ref, v_ref, f_ref, o_ref, lse_ref, kaug_sc, vt_sc, m_sc, l_sc, acc_sc):
        hp = pl.program_id(0)
        i = pl.program_id(1)
        lane = lax.broadcasted_iota(jnp.int32, (1, 128), 1)
        sub = lax.broadcasted_iota(jnp.int32, (128, 1), 0)
        head_sel = (lane < HEAD_DIM, lane >= HEAD_DIM)
        head_sel_t = (sub < HEAD_DIM, sub >= HEAD_DIM)
        spare = (HEAD_DIM, 0)
        zero = jnp.zeros((), BF16)

        @pl.when(i == 0)
        def _():
            def prep(jt, carry):
                rows = pl.ds(pl.multiple_of(jt * T, T), T)
                k = k_ref[rows, :]
                ft = f_ref[rows, :]
                vt = v_ref[rows, :].astype(F32).T
                for h in range(2):
                    fh = jnp.sum(jnp.where(lane == 2 * hp + h, ft, 0.0), axis=1, keepdims=True)
                    hi, mid, lo = _split3(-fh)
                    b = spare[h]
                    bias = jnp.where(lane == b, hi, jnp.where(lane == b + 1, mid, jnp.where(lane == b + 2, lo, zero)))
                    kaug_sc[h, rows, :] = jnp.where(head_sel[h], k, bias)
                    vt_sc[h, jt] = jnp.where(head_sel_t[h], vt, 0.0).astype(BF16)
                return carry

            lax.fori_loop(0, n_t, prep, 0)

        q = q_ref[...]
        q_heads = []
        for h in range(2):
            ones = jnp.where((lane >= spare[h]) & (lane < spare[h] + 3), jnp.ones((), BF16), zero)
            q_heads.append(jnp.where(head_sel[h], q, ones))
        m_sc[...] = jnp.full((8, T), NEG, F32)
        l_sc[...] = jnp.zeros((8, T), F32)
        acc_sc[...] = jnp.zeros((128, T), F32)

        def update(j, k_lo, n_k, q_lo, masked):
            rows = pl.ds(pl.multiple_of(j * T + k_lo, n_k), n_k)
            n_q = T - q_lo
            alphas, pvs = [], []
            for h in range(2):
                s_t = _dot_nt(kaug_sc[h, rows, :], q_heads[h][q_lo:, :])
                if masked:
                    rr = lax.broadcasted_iota(jnp.int32, (n_k, n_q), 0) + k_lo
                    cc = lax.broadcasted_iota(jnp.int32, (n_k, n_q), 1) + q_lo
                    s_t = jnp.where(rr <= cc, s_t, NEG)
                m_prev = m_sc[h:h + 1, q_lo:]
                m_new = jnp.maximum(m_prev, jnp.max(s_t, axis=0, keepdims=True))
                alpha = jnp.exp(m_prev - m_new)
                p_t = jnp.exp(s_t - m_new)
                l_sc[h:h + 1, q_lo:] = alpha * l_sc[h:h + 1, q_lo:] + jnp.sum(p_t, axis=0, keepdims=True)
                m_sc[h:h + 1, q_lo:] = m_new
                alphas.append(alpha)
                pvs.append(_dot(vt_sc[h, j, :, k_lo:k_lo + n_k], p_t.astype(BF16)))
            acc_sc[:, q_lo:] = (acc_sc[:, q_lo:] * jnp.where(head_sel_t[0], alphas[0], alphas[1])
                                + (pvs[0] + pvs[1]))

        def two_off_diagonal(jj, carry):
            update(2 * jj, 0, T, 0, False)
            update(2 * jj + 1, 0, T, 0, False)
            return carry

        lax.fori_loop(0, i // 2, two_off_diagonal, 0)

        @pl.when(i % 2 == 1)
        def _():
            update(i - 1, 0, T, 0, False)

        update(i, 0, T, 0, True)
        l = l_sc[...]
        o_ref[...] = (acc_sc[...] / jnp.where(head_sel_t[0], l[0:1, :], l[1:2, :])).T
        is_head = lax.broadcasted_iota(jnp.int32, (8, 1), 0) < 2
        lse_ref[...] = jnp.where(is_head, m_sc[...] + jnp.log(jnp.where(is_head, l, 1.0)), 0.0)

    return pl.pallas_call(
        body, name="attention_fwd", grid=(N_PAIR, n_t),
        out_shape=(jax.ShapeDtypeStruct((S, D_ATT), F32), jax.ShapeDtypeStruct((N_PAIR, n_t, 8, T), F32)),
        in_specs=[pl.BlockSpec((T, 128), lambda hp, i: (i, hp)),
                  pl.BlockSpec((S, 128), lambda hp, i: (0, N_PAIR + hp)),
                  pl.BlockSpec((S, 128), lambda hp, i: (0, 2 * N_PAIR + hp)),
                  pl.BlockSpec((S, 128), lambda hp, i: (0, 0))],
        out_specs=(pl.BlockSpec((T, 128), lambda hp, i: (i, hp)),
                   pl.BlockSpec((None, None, 8, T), lambda hp, i: (hp, i, 0, 0))),
        scratch_shapes=[pltpu.VMEM((2, S, 128), BF16), pltpu.VMEM((2, n_t, 128, T), BF16),
                        pltpu.VMEM((8, T), F32), pltpu.VMEM((8, T), F32), pltpu.VMEM((128, T), F32)],
        compiler_params=_params(dimension_semantics=("arbitrary", "arbitrary")),
    )(qkv, qkv, qkv, big_f)


def _attention_bwd(qkv, datt, att, lse, big_f, gw_out4, sc_out):
    S = qkv.shape[0]
    T = min(T_ATT, S)
    n_t = S // T
    n_steps = N_PAIR * n_t
    marks = (0, n_steps // 8, n_steps // 2, n_steps // 2 + n_steps // 8)

    def body(q_ref, do_ref, o_ref, lse_ref, k_ref, v_ref, fk_ref, gout_ref, scout_ref,
             dq_ref, dk_ref, dv_ref, cs_ref, dfk_ref, dfq_ref, oout_ref, stat_sc, dqt_sc, qaug_sc,
             out_buf, *red_bufs):
        hp = pl.program_id(0)
        j = pl.program_id(1)
        x, y, cc = lax.axis_index("x"), lax.axis_index("y"), lax.axis_index("c")
        plan = _scatter_stages((x, y, cc, 2 * x + y, (x, y, 1 - cc)), gout_ref, scout_ref, out_buf, *red_bufs)
        step = hp * n_t + j
        for n, mark in enumerate(marks):
            @pl.when(step == mark)
            def _(n=n):
                if n > 0:
                    plan[n - 1][1]()
                if n < 3:
                    plan[n][0]()
                else:
                    oout_ref[...] = out_buf[...]

        lane = lax.broadcasted_iota(jnp.int32, (1, 128), 1)
        sub = lax.broadcasted_iota(jnp.int32, (128, 1), 0)
        head_sel = (lane < HEAD_DIM, lane >= HEAD_DIM)
        head_sel_t = (sub < HEAD_DIM, sub >= HEAD_DIM)
        spare = (HEAD_DIM, 0)
        zero = jnp.zeros((), BF16)
        one = jnp.ones((), BF16)

        def bias_lanes(first, pieces):
            hi, mid, lo = pieces
            return lambda rest: jnp.where(lane == first, hi, jnp.where(lane == first + 1, mid,
                                                                        jnp.where(lane == first + 2, lo, rest)))

        @pl.when(j == 0)
        def _():
            dqt_sc[...] = jnp.zeros_like(dqt_sc)
            cs_ref[...] = jnp.zeros_like(cs_ref)
            dfq_ref[...] = jnp.zeros_like(dfq_ref)

            def prep(i, carry):
                rows = pl.ds(pl.multiple_of(i * T, T), T)
                q = q_ref[rows, :]
                do = do_ref[rows, :]
                prod = o_ref[rows, :] * do.astype(F32)
                d_a = jnp.sum(jnp.where(head_sel[0], prod, 0.0), axis=1, keepdims=True)
                d_b = jnp.sum(jnp.where(head_sel[0], 0.0, prod), axis=1, keepdims=True)
                delta_t = jnp.where(head_sel[0], d_a, d_b).T
                stat_sc[i, 0:1, :] = delta_t[0:1, :]
                stat_sc[i, 1:2, :] = delta_t[HEAD_DIM:HEAD_DIM + 1, :]
                lse = lse_ref[i]
                lse_cols = jnp.where(head_sel_t[0], lse[0:1, :], lse[1:2, :]).T
                for h in range(2):
                    neg_lse = -lse_cols[:, h * HEAD_DIM:h * HEAD_DIM + 1]
                    ones = jnp.where((lane >= spare[h]) & (lane < spare[h] + 3), one, zero)
                    qaug_sc[h, rows, :] = jnp.where(head_sel[h], q, bias_lanes(spare[h] + 3, _split3(neg_lse))(ones))
                return carry

            lax.fori_loop(0, n_t, prep, 0)

        k = k_ref[...]
        v = v_ref[...]
        fk = fk_ref[...]
        kt = k.astype(F32).T
        heads = []
        for h in range(2):
            fkh = jnp.sum(jnp.where(lane == 2 * hp + h, fk, 0.0), axis=1, keepdims=True)
            ones = jnp.where((lane >= spare[h] + 3) & (lane < spare[h] + 6), one, zero)
            kaug = jnp.where(head_sel[h], k, bias_lanes(spare[h], _split3(-fkh))(ones))
            heads.append((kaug, jnp.where(head_sel[h], v, zero), jnp.where(head_sel_t[h], kt, 0.0).astype(BF16)))

        def block(i, k_lo, n_k, q_lo, masked):
            n_q = T - q_lo
            rows = pl.ds(pl.multiple_of(i * T + q_lo, n_q), n_q)
            q = q_ref[rows, :]
            do = do_ref[rows, :]
            stat = stat_sc[i]
            dk = jnp.zeros((n_k, 128), F32)
            dv = jnp.zeros((n_k, 128), F32)
            dqt = jnp.zeros((128, n_q), F32)
            dfs = []
            for h in range(2):
                kaug, vh, kth = heads[h]
                arg = _dot_nt(kaug[k_lo:k_lo + n_k, :], qaug_sc[h, rows, :])
                if masked:
                    rr = lax.broadcasted_iota(jnp.int32, (n_k, n_q), 0) + k_lo
                    cc = lax.broadcasted_iota(jnp.int32, (n_k, n_q), 1) + q_lo
                    arg = jnp.where(rr <= cc, arg, NEG)
                p_t = jnp.exp(arg)
                ds_t = p_t * (_dot_nt(vh[k_lo:k_lo + n_k, :], do) - stat[h:h + 1, q_lo:])
                ds_bf = ds_t.astype(BF16)
                dv = dv + _dot(p_t.astype(BF16), jnp.where(head_sel[h], do, zero))
                dk = dk + _dot(ds_bf, jnp.where(head_sel[h], q, zero))
                dqt = dqt + _dot(kth[:, k_lo:k_lo + n_k], ds_bf)
                dfs.append(jnp.sum(ds_t, axis=1, keepdims=True))
                dfq_ref[i, h:h + 1, q_lo:] += _colsum(ds_t)
            dqt_sc[i, :, q_lo:] += dqt
            return dk, dv, dfs[0], dfs[1]

        def off_diagonal(i, acc):
            return tuple(a + b for a, b in zip(acc, block(i, 0, T, 0, False)))

        half = T // 2
        early = block(j, 0, half, 0, True)
        late = block(j, half, half, half, True)
        acc1 = tuple(jnp.concatenate([a, b], axis=0) for a, b in zip(early, late))
        n_off = n_t - 1 - j
        acc2 = lax.fori_loop(0, n_off // 2,
                             lambda ii, a: off_diagonal(j + 2 + 2 * ii, off_diagonal(j + 1 + 2 * ii, a)), acc1)
        dk_acc, dv_acc, dfa, dfb = lax.fori_loop(0, n_off % 2, lambda _, a: off_diagonal(n_t - 1, a), acc2)
        dk_ref[...] = dk_acc.astype(BF16)
        dv_ref[...] = dv_acc.astype(BF16)
        dfk_ref[...] = -jnp.where(lane == 0, dfa, jnp.where(lane == 1, dfb, 0.0))
        cs_ref[:, 128:256] = cs_ref[:, 128:256] + _colsum(dk_acc)
        cs_ref[:, 256:384] = cs_ref[:, 256:384] + _colsum(dv_acc)

        @pl.when(j == n_t - 1)
        def _():
            def finish(i, tot):
                dq = dqt_sc[i].T
                dq_ref[pl.ds(pl.multiple_of(i * T, T), T), :] = dq.astype(BF16)
                return tot + _colsum(dq)

            cs_ref[:, 0:128] = lax.fori_loop(0, n_t, finish, jnp.zeros((1, 128), F32))

    pair_rows = lambda hp, j: (hp, 0, 0)
    vm = pl.BlockSpec(memory_space=pltpu.VMEM)
    _, r_out, c_out = gw_out4.shape
    return pl.pallas_call(
        body, name="attention_bwd", grid=(N_PAIR, n_t),
        out_shape=(jax.ShapeDtypeStruct((S, D_ATT), BF16), jax.ShapeDtypeStruct((S, D_ATT), BF16),
                   jax.ShapeDtypeStruct((S, D_ATT), BF16), jax.ShapeDtypeStruct((N_PAIR, 1, 384), F32),
                   jax.ShapeDtypeStruct((N_PAIR, S, 128), F32),
                   jax.ShapeDtypeStruct((N_PAIR, n_t, 8, T), F32),
                   jax.ShapeDtypeStruct((r_out, c_out), F32)),
        in_specs=[pl.BlockSpec((S, 128), lambda hp, j: (0, hp)),
                  pl.BlockSpec((S, 128), lambda hp, j: (0, hp)),
                  pl.BlockSpec((S, 128), lambda hp, j: (0, hp)),
                  pl.BlockSpec((None, n_t, 8, T), lambda hp, j: (hp, 0, 0, 0)),
                  pl.BlockSpec((T, 128), lambda hp, j: (j, N_PAIR + hp)),
                  pl.BlockSpec((T, 128), lambda hp, j: (j, 2 * N_PAIR + hp)),
                  pl.BlockSpec((T, 128), lambda hp, j: (j, 0)),
                  vm, vm],
        out_specs=(pl.BlockSpec((S, 128), lambda hp, j: (0, hp)),
                   pl.BlockSpec((T, 128), lambda hp, j: (j, hp)),
                   pl.BlockSpec((T, 128), lambda hp, j: (j, hp)),
                   pl.BlockSpec((None, 1, 384), pair_rows),
                   pl.BlockSpec((None, T, 128), lambda hp, j: (hp, j, 0)),
                   pl.BlockSpec((None, n_t, 8, T), lambda hp, j: (hp, 0, 0, 0)),
                   vm),
        scratch_shapes=[pltpu.VMEM((n_t, 8, T), F32), pltpu.VMEM((n_t, 128, T), F32),
                        pltpu.VMEM((2, S, 128), BF16), pltpu.VMEM((r_out, c_out), F32)]
        + _scatter_scratch(r_out, c_out),
        compiler_params=_params(dimension_semantics=("arbitrary", "arbitrary")),
    )(qkv, datt, att, lse, qkv, qkv, big_f, gw_out4, sc_out)


def _window_counts(first_row, n_rows, window):
    t = lax.broadcasted_iota(jnp.int32, (n_rows, 1), 0) + first_row
    return jnp.minimum((t + 1).astype(F32), float(window))


def _middle(x, tgt, att, g, p, gate, w_mix, b_mix, pool_scale, w_out, b_out, ln_g, ln_b):
    S = x.shape[0]
    tm = min(TM_MID, S)
    halo_blocks = tm // POOL_HALO

    def body(x_ref, t_ref, att_ref, g_ref, p_ref, ph_ref, gate_ref, wm_ref, bm_ref, ps_ref, wo_ref, bo_ref,
             lg_ref, lb_ref,
             dh_ref, datt_ref, dg_ref, dpl_ref, gwo_ref, gwm_ref, vec_ref, loss_ref):
        i = pl.program_id(0)

        @pl.when(i == 0)
        def _():
            gwo_ref[...] = jnp.zeros_like(gwo_ref)
            gwm_ref[...] = jnp.zeros_like(gwm_ref)
            vec_ref[...] = jnp.zeros_like(vec_ref)
            loss_ref[...] = jnp.zeros_like(loss_ref)

        pc = p_ref[...]
        halo = jnp.where(i > 0, ph_ref[...], 0.0)
        pe = jnp.concatenate([halo, pc], axis=0)
        pooled_parts = []
        for gi, w in enumerate(POOL_WINDOWS):
            cur = pe[:, gi * POOL_GROUP:(gi + 1) * POOL_GROUP]
            span = 1
            while span < w:
                cur = cur + pltpu.roll(cur, span, 0)
                span *= 2
            wsum = cur[POOL_HALO:, :]
            mean = wsum / _window_counts(i * tm, tm, w)
            pooled_parts.append(mean - pc[:, gi * POOL_GROUP:(gi + 1) * POOL_GROUP])
        pooled_bf =[v.astype(BF16) for v in pooled_parts]
        mixed = jnp.concatenate([_dot(pooled_bf[gi], wm_ref[gi]) for gi in range(4)], axis=1) + bm_ref[...]
        ps = ps_ref[...]
        pool_out = mixed * ps
        gv = g_ref[...]
        sig = _sigmoid(gv)
        silu = gv * sig
        att = att_ref[...]
        y = jnp.concatenate([att * silu[:, :D_ATT], pool_out * silu[:, D_ATT:]], axis=1)
        y_bf = y.astype(BF16)
        wo = wo_ref[...]
        yo = _dot(y_bf, wo) + bo_ref[...]
        gate = gate_ref[...]
        h = ALPHA * x_ref[...] + gate * yo
        mu = jnp.mean(h, axis=1, keepdims=True)
        hc = h - mu
        var = jnp.mean(hc * hc, axis=1, keepdims=True)
        rstd = lax.rsqrt(var + LN_EPS)
        yhat = hc * rstd
        lg = lg_ref[...]
        out = yhat * lg + lb_ref[...]
        err = out - t_ref[...]
        loss_ref[...] += 0.5 * jnp.sum(jnp.mean(err * err, axis=1, keepdims=True), axis=0, keepdims=True)

        dout = err * (1.0 / D)
        g_ln_b = _colsum(dout)
        g_ln_g = _colsum(dout * yhat)
        dyh = dout * lg
        dh = rstd * (dyh - jnp.mean(dyh, axis=1, keepdims=True)
                     - yhat * jnp.mean(dyh * yhat, axis=1, keepdims=True))
        dh_ref[...] = dh
        d_gate = _colsum(dh * yo)
        dyo = gate * dh
        g_b_out = _colsum(dyo)
        dyo_bf = dyo.astype(BF16)
        gwo_ref[...] += _dot_tn(y_bf, dyo_bf)
        dy = _dot_nt(dyo_bf, wo)
        dsilu = sig * (1.0 + gv * (1.0 - sig))
        dy_a = dy[:, :D_ATT]
        dy_p = dy[:, D_ATT:]
        datt_ref[...] = (dy_a * silu[:, :D_ATT]).astype(BF16)
        dpo = dy_p * silu[:, D_ATT:]
        dg = jnp.concatenate([dy_a * att * dsilu[:, :D_ATT], dy_p * pool_out * dsilu[:, D_ATT:]], axis=1)
        dg_ref[...] = dg.astype(BF16)
        g_dg = _colsum(dg)
        g_ps = _colsum(dpo * mixed)
        dmixed = dpo * ps
        g_bm = _colsum(dmixed)
        dmixed_bf = dmixed.astype(BF16)
        dpl = []
        for gi in range(4):
            dm = dmixed_bf[:, gi * POOL_GROUP:(gi + 1) * POOL_GROUP]
            gwm_ref[gi] += _dot_tn(pooled_bf[gi], dm)
            dpl.append(_dot_nt(dm, wm_ref[gi]))
        dpl_ref[...] = jnp.concatenate(dpl, axis=1)
        vec_ref[0:1, :] += g_ln_g
        vec_ref[1:2, :] += g_ln_b
        vec_ref[2:3, :] += d_gate
        vec_ref[3:4, :] += g_b_out
        vec_ref[4:5, :] += g_dg
        vec_ref[5:6, 0:D_POOL] += g_ps
        vec_ref[6:7, 0:D_POOL] += g_bm

    row = lambda w: pl.BlockSpec((tm, w), lambda i: (i, 0))
    full2 = lambda a: pl.BlockSpec(a.shape, lambda i: (0, 0))
    full3 = lambda a: pl.BlockSpec(a.shape, lambda i: (0, 0, 0))
    return pl.pallas_call(
        body, name="middle", grid=(S // tm,),
        out_shape=(jax.ShapeDtypeStruct((S, D), F32),
                   jax.ShapeDtypeStruct((S, D_ATT), BF16),
                   jax.ShapeDtypeStruct((S, D), BF16),
                   jax.ShapeDtypeStruct((S, D_POOL), F32),
                   jax.ShapeDtypeStruct((D, D), F32),
                   jax.ShapeDtypeStruct((4, POOL_GROUP, POOL_GROUP), F32),
                   jax.ShapeDtypeStruct((8, D), F32),
                   jax.ShapeDtypeStruct((1, 1), F32)),
        in_specs=[row(D), row(D), row(D_ATT), row(D), row(D_POOL),
                  pl.BlockSpec((POOL_HALO, D_POOL), lambda i: (jnp.maximum(i * halo_blocks - 1, 0), 0)),
                  full2(gate), full3(w_mix), full2(b_mix), full2(pool_scale), full2(w_out), full2(b_out),
                  full2(ln_g), full2(ln_b)],
        out_specs=(row(D), row(D_ATT), row(D), row(D_POOL),
                   pl.BlockSpec((D, D), lambda i: (0, 0)),
                   pl.BlockSpec((4, POOL_GROUP, POOL_GROUP), lambda i: (0, 0, 0)),
                   pl.BlockSpec((8, D), lambda i: (0, 0)),
                   pl.BlockSpec((1, 1), lambda i: (0, 0))),
        compiler_params=_params(dimension_semantics=("arbitrary",)),
    )(x, tgt, att, g, p, p, gate, w_mix, b_mix, pool_scale, w_out, b_out, ln_g, ln_b)


def _tail(dpl, dfk, dfq, f):
    S = dpl.shape[0]
    tm = min(T_ATT, S)
    n_t = S // tm
    halo_blocks = tm // POOL_HALO
    last_halo = S // POOL_HALO - 1

    def body(d_ref, dn_ref, dfk_ref, dfq_ref, f_ref, dp_ref, df_ref, cs_ref, carry):
        s = pl.program_id(0)
        i = n_t - 1 - s

        @pl.when(s == 0)
        def _():
            carry[...] = jnp.zeros_like(carry)
            cs_ref[...] = jnp.zeros_like(cs_ref)

        dc = d_ref[...]
        nxt = jnp.where(s > 0, dn_ref[...], 0.0)
        de = jnp.concatenate([dc, nxt], axis=0)
        n_e = tm + POOL_HALO
        parts = []
        for gi, w in enumerate(POOL_WINDOWS):
            cur = de[:, gi * POOL_GROUP:(gi + 1) * POOL_GROUP] / _window_counts(i * tm, n_e, w)
            span = 1
            while span < w:
                cur = cur + pltpu.roll(cur, n_e - span, 0)
                span *= 2
            parts.append(cur[:tm, :] - dc[:, gi * POOL_GROUP:(gi + 1) * POOL_GROUP])
        dp = jnp.concatenate(parts, axis=1)
        dp_ref[...] = dp.astype(BF16)
        cs_ref[0:1, :] += _colsum(dp)

        r = lax.broadcasted_iota(jnp.int32, (tm, tm), 0)
        c = lax.broadcasted_iota(jnp.int32, (tm, tm), 1)
        tri = (r >= c).astype(F32)
        k_cols = dfk_ref[0]
        rows8 = dfq_ref[0]
        for hp in range(1, N_PAIR):
            k_cols = k_cols + pltpu.roll(dfk_ref[hp], 2 * hp, 1)
            rows8 = rows8 + pltpu.roll(dfq_ref[hp], 2 * hp, 0)
        rows8 = rows8 + k_cols.T[0:8, :]
        dlogf8 = jnp.dot(rows8, tri, preferred_element_type=F32, precision=lax.Precision.HIGHEST) + carry[...]
        first = lax.broadcasted_iota(jnp.int32, (1, tm), 1) == 0
        carry[...] = jnp.sum(jnp.where(first, dlogf8, 0.0), axis=1, keepdims=True)
        dlogf = jnp.concatenate([dlogf8, jnp.zeros((128 - 8, tm), F32)], axis=0).T
        df = dlogf * _sigmoid(-f_ref[...])
        df_ref[...] = df.astype(BF16)
        cs_ref[1:2, 0:128] += _colsum(df)

    rev = lambda w: pl.BlockSpec((tm, w), lambda s: (n_t - 1 - s, 0))
    return pl.pallas_call(
        body, name="tail", grid=(n_t,),
        out_shape=(jax.ShapeDtypeStruct((S, D_POOL), BF16), jax.ShapeDtypeStruct((S, 128), BF16),
                   jax.ShapeDtypeStruct((8, D_POOL), F32)),
        in_specs=[rev(D_POOL),
                  pl.BlockSpec((POOL_HALO, D_POOL),
                               lambda s: (jnp.minimum((n_t - s) * halo_blocks, last_halo), 0)),
                  pl.BlockSpec((N_PAIR, tm, 128), lambda s: (0, n_t - 1 - s, 0)),
                  pl.BlockSpec((N_PAIR, None, 8, tm), lambda s: (0, n_t - 1 - s, 0, 0)),
                  rev(128)],
        out_specs=(rev(D_POOL), rev(128), pl.BlockSpec((8, D_POOL), lambda s: (0, 0))),
        scratch_shapes=[pltpu.VMEM((8, 1), F32)],
        compiler_params=_params(dimension_semantics=("arbitrary",)),
    )(dpl, dpl, dfk, dfq, f)


PIECES = ((O_QKV, D_ATT), (O_QKV + D_ATT, D_ATT), (O_QKV + 2 * D_ATT, D_ATT), (O_F, 128), (O_P, D_POOL), (O_G, D))


def _grad_w_in(u, pieces):
    S = u.shape[0]
    tm = min(TM_GW, S)
    n_t = S // tm

    def body(u_ref, *rest):
        piece_refs, out_ref, acc, sem = rest[:6], rest[6], rest[7], rest[8]
        i = pl.program_id(0)

        @pl.when(i == 0)
        def _():
            acc[...] = jnp.zeros_like(acc)

        u_t = u_ref[...]
        for (off, w), ref in zip(PIECES, piece_refs):
            acc[:, off:off + w] += _dot_tn(u_t, ref[...])

        @pl.when(i == n_t - 1)
        def _():
            cp = pltpu.make_async_copy(acc, out_ref, sem)
            cp.start()
            cp.wait()

    return pl.pallas_call(
        body, name="grad_w_in", grid=(n_t,),
        out_shape=jax.ShapeDtypeStruct((D, D_PAD), F32),
        in_specs=[pl.BlockSpec((tm, D), lambda i: (i, 0))]
        + [pl.BlockSpec((tm, w), lambda i: (i, 0)) for _, w in PIECES],
        out_specs=pl.BlockSpec(memory_space=pl.ANY),
        scratch_shapes=[pltpu.VMEM((D, D_PAD), F32), pltpu.SemaphoreType.DMA],
        compiler_params=_params(dimension_semantics=("arbitrary",)),
    )(u, *pieces)


def _grad_x(pieces, wt_pad, dh, x, scale):
    S = x.shape[0]
    tm = min(TM_DU, S)

    def body(*refs):
        piece_refs = refs[:6]
        w_ref, dh_ref, x_ref, sc_ref, gx_ref, vec_ref = refs[6:]

        @pl.when(pl.program_id(0) == 0)
        def _():
            vec_ref[...] = jnp.zeros_like(vec_ref)

        du = jnp.zeros((tm, D), F32)
        for (off, w), ref in zip(PIECES, piece_refs):
            du = du + _dot(ref[...], w_ref[off:off + w, :])
        xv = x_ref[...]
        gx_ref[...] = ALPHA * dh_ref[...] + du * (1.0 + sc_ref[...])
        vec_ref[0:1, :] += _colsum(du)
        vec_ref[1:2, :] += _colsum(du * xv)

    row = lambda w: pl.BlockSpec((tm, w), lambda i: (i, 0))
    return pl.pallas_call(
        body, name="grad_x", grid=(S // tm,),
        out_shape=(jax.ShapeDtypeStruct((S, D), F32), jax.ShapeDtypeStruct((8, D), F32)),
        in_specs=[row(w) for _, w in PIECES]
        + [pl.BlockSpec(wt_pad.shape, lambda i: (0, 0)), row(D), row(D), pl.BlockSpec((1, D), lambda i: (0, 0))],
        out_specs=(row(D), pl.BlockSpec((8, D), lambda i: (0, 0))),
        compiler_params=_params(dimension_semantics=("arbitrary",)),
    )(*pieces, wt_pad, dh, x, scale)


def _grad_ada(c_all, dada_all, dada_cols):
    def body(c_ref, dall_ref, dcol_ref, gw_ref, gb_ref):
        rows = lax.broadcasted_iota(jnp.int32, (8, 1), 0)
        cm = jnp.zeros((8, D), F32)
        dm = jnp.zeros((8, 3 * D), F32)
        for r in range(8):
            cm = jnp.where(rows == r, c_ref[r], cm)
            dm = jnp.where(rows == r, dall_ref[r], dm)
        act = cm * _sigmoid(cm)
        pad = jnp.zeros((8, D), F32)
        lhs = jnp.concatenate([act, pad], axis=0).astype(BF16)
        rhs = jnp.concatenate([dcol_ref[...], jnp.zeros((8, SHARD_ADA), F32)], axis=0).astype(BF16)
        gw_ref[...] = _dot_tn(lhs, rhs)
        gb_ref[...] = _colsum(dm)

    vm = pl.BlockSpec(memory_space=pltpu.VMEM)
    return pl.pallas_call(
        body, name="grad_ada",
        out_shape=(jax.ShapeDtypeStruct((D, SHARD_ADA), F32), jax.ShapeDtypeStruct((1, 3 * D), F32)),
        in_specs=[vm, vm, vm], out_specs=(vm, vm),
        compiler_params=_params(),
    )(c_all, dada_all, dada_cols)


def _adamw_math(w, g, m, v):
    m = ADAM_B1 * m + (1.0 - ADAM_B1) * g
    v = ADAM_B2 * v + (1.0 - ADAM_B2) * (g * g)
    m_hat = m / (1.0 - ADAM_B1 ** ADAM_STEP)
    v_hat = v / (1.0 - ADAM_B2 ** ADAM_STEP)
    delta = -ADAM_LR * (m_hat / (jnp.sqrt(v_hat) + ADAM_EPS) + ADAM_WD * w)
    return delta, m, v


def _adamw(groups, n_steps):
    n = len(groups)

    def body(*refs):
        ins, outs = refs[:4 * n], refs[4 * n:]
        for t in range(n):
            w, g, m, v = (r[...] for r in ins[4 * t:4 * t + 4])
            d, m2, v2 = _adamw_math(w, g, m, v)
            outs[3 * t][...] = d
            outs[3 * t + 1][...] = m2
            outs[3 * t + 2][...] = v2

    in_specs, out_specs, out_shape, args = [], [], [], []
    for (w, g, m, v) in groups:
        rest = w.shape[1:]
        spec = pl.BlockSpec((w.shape[0] // n_steps,) + rest, lambda i, nd=len(rest): (i,) + (0,) * nd)
        in_specs += [spec] * 4
        out_specs += [spec] * 3
        out_shape += [jax.ShapeDtypeStruct(w.shape, F32)] * 3
        args += [w, g, m, v]
    return pl.pallas_call(
        body, name="adamw_%d_%d" % (n, n_steps), grid=(n_steps,),
        out_shape=tuple(out_shape), in_specs=in_specs, out_specs=tuple(out_specs),
        compiler_params=_params(dimension_semantics=("arbitrary",)),
    )(*args)


def _adamw_small(small_sum, g_b_ada, params):
    n = len(params)

    def body(gs_ref, gba_ref, *refs):
        ins, outs = refs[:3 * n], refs[3 * n:]
        for t, (name, w0, _, _) in enumerate(params):
            w_ref, m_ref, v_ref = ins[3 * t:3 * t + 3]
            first = SMALL_SEGS[name][0] if name in SMALL_SEGS else None
            if w0.shape[0] > 1:
                pieces = [((slice(None), slice(None)), gs_ref[first:first + w0.shape[0], :])]
            else:
                pieces = []
                for r in range(-(-w0.shape[1] // 128)):
                    lanes = slice(128 * r, min(128 * r + 128, w0.shape[1]))
                    g = gba_ref[0:1, lanes] if first is None else gs_ref[first + r:first + r + 1, 0:lanes.stop - lanes.start]
                    pieces.append(((slice(0, 1), lanes), g))
            for where, g in pieces:
                d, m2, v2 = _adamw_math(w_ref[where], g, m_ref[where], v_ref[where])
                for ref, val in zip(outs[4 * t:4 * t + 4], (g, d, m2, v2)):
                    ref[where] = val

    vm = pl.BlockSpec(memory_space=pltpu.VMEM)
    args = [small_sum, g_b_ada]
    out_shape = []
    for _, w, m, v in params:
        args += [w, m, v]
        out_shape += [jax.ShapeDtypeStruct(w.shape, F32)] * 4
    return pl.pallas_call(
        body, name="adamw_small",
        out_shape=tuple(out_shape), in_specs=[vm] * len(args), out_specs=(vm,) * len(out_shape),
        compiler_params=_params(),
    )(*args)


def _pack_small(parts):
    rows = []
    used = 0
    for name, (first, n_rows) in SMALL_SEGS.items():
        if first > used:
            rows.append(jnp.zeros((first - used, 128), F32))
        flat = parts[name].reshape(-1)
        flat = jnp.pad(flat, (0, n_rows * 128 - flat.shape[0]))
        rows.append(flat.reshape(n_rows, 128))
        used = first + n_rows
    rows.append(jnp.zeros((SMALL_ROWS - used, 128), F32))
    return jnp.concatenate(rows, axis=0)


def _unpack_small(buf, name, shape):
    first, n_rows = SMALL_SEGS[name]
    n = int(np.prod(shape))
    return buf[first:first + n_rows].reshape(-1)[:n].reshape(shape)


def _pad_in(v):
    r = v.shape[0]
    z = jnp.zeros((r, O_P - O_F - N_HEADS), v.dtype)
    return jnp.concatenate([v[:, :3 * D_ATT + N_HEADS], z, v[:, 3 * D_ATT + N_HEADS:]], axis=1)


def _unpad_in(v):
    return jnp.concatenate([v[:, :O_F + N_HEADS], v[:, O_P:]], axis=1)


def _shards_in(v):
    gap = O_P - (O_F + N_HEADS)
    parts = []
    for a in range(N_CHIPS):
        lo, hi = a * SHARD_IN, (a + 1) * SHARD_IN
        cut = O_F + N_HEADS
        if hi <= cut:
            parts.append(v[:, lo:hi])
        elif lo >= cut:
            parts.append(v[:, lo + gap:hi + gap])
        else:
            parts.append(jnp.concatenate([v[:, lo:cut], v[:, cut + gap:hi + gap]], axis=1))
    return jnp.stack(parts, axis=0)


def kernel(x, c, w_ada, b_ada, w_in, b_in, w_pool_mix, b_pool_mix, pool_scale, w_out, b_out, ln_g, ln_b, loss_target, m_w_ada, m_b_ada, m_w_in, m_b_in, m_w_pool_mix, m_b_pool_mix, m_pool_scale, m_w_out, m_b_out, m_ln_g, m_ln_b, v_w_ada, v_b_ada, v_w_in, v_b_in, v_w_pool_mix, v_b_pool_mix, v_pool_scale, v_w_out, v_b_out, v_ln_g, v_ln_b):
    S = x.shape[1]
    T = min(T_ATT, S)
    n_t = S // T
    chip = 2 * lax.axis_index("x") + lax.axis_index("y")
    x2 = x[0]
    tgt = loss_target[0]
    q_scale = jnp.concatenate([jnp.full((1, D_ATT), Q_SCALE, F32), jnp.ones((1, D_PAD - D_ATT), F32)], axis=1)

    to_cols = lambda a: jnp.transpose(a, (2, 0, 1))
    from_cols = lambda a: jnp.transpose(a, (1, 2, 0))
    c_all, ada4, wt_pad = _gather_and_ada(
        c, w_ada[0], b_ada.reshape(4, 1, SHARD_ADA), to_cols(w_in).reshape(SHARD_IN, D).astype(BF16))
    ada = ada4[:, 0, :].reshape(1, 3 * D)
    shift, scale, gate = ada[:, :D], ada[:, D:2 * D], ada[:, 2 * D:]
    b_pad = _pad_in(b_in) * q_scale
    w_mix_bf = w_pool_mix[0].astype(BF16)

    u, qkv, f, p, g, w_out_all = _in_proj(x2, shift, scale, wt_pad, b_pad, w_out[0].astype(BF16))
    w_out_full = w_out_all.reshape(D, D)
    big_f = _forget_cumsum(f)
    att, lse = _attention_fwd(qkv, big_f)

    dh, datt, dg, dpl, gw_out, gw_mix, vec, loss_part = _middle(
        x2, tgt, att, g, p, gate, w_mix_bf, b_pool_mix.reshape(1, D_POOL), pool_scale, w_out_full, b_out, ln_g, ln_b)
    dq, dk, dv, cs_att, dfk, dfq, g_w_out = _attention_bwd(
        qkv, datt, att, lse, big_f, gw_out.reshape(N_CHIPS, SHARD_OUT, D), jnp.ones((N_CHIPS, 1, D), F32))
    dp, df, cs_tail = _tail(dpl, dfk, dfq, f)
    pieces = (dq, dk, dv, df, dp, dg)
    gw_pad = _grad_w_in(u, pieces)
    grad_x, vec_x = _grad_x(pieces, wt_pad, dh, x2, scale)

    cs_qkv = jnp.transpose(cs_att.reshape(N_PAIR, 3, 128), (1, 0, 2)).reshape(1, 3 * D_ATT)
    gb_pad = jnp.concatenate([cs_qkv, cs_tail[1:2, 0:128], cs_tail[0:1, :], vec[4:5, :]], axis=1) * q_scale
    dada = jnp.concatenate([vec_x[0:1, :], vec_x[1:2, :], vec[2:3, :]], axis=1)
    small = _pack_small({
        "b_in": _unpad_in(gb_pad), "w_pool_mix": gw_mix, "b_pool_mix": vec[6:7, :D_POOL],
        "pool_scale": vec[5:6, :D_POOL], "b_out": vec[3:4, :], "ln_g": vec[0:1, :], "ln_b": vec[1:2, :],
        "loss": loss_part})

    g_w_in, small_sum, dada_all = _reduce_all(
        _shards_in(gw_pad), _shards_in(q_scale), small, dada)
    dada_cols = lax.dynamic_slice(dada_all[:, 0, :], (0, chip * SHARD_ADA), (8, SHARD_ADA))
    g_w_ada, g_b_ada = _grad_ada(c_all, dada_all, dada_cols)
    loss = _unpack_small(small_sum, "loss", (1,))[0]

    big = _adamw([(w_ada[0], g_w_ada, m_w_ada[0], v_w_ada[0]),
                  (w_out[0], g_w_out, m_w_out[0], v_w_out[0])], 8)
    g_w_in_cols = to_cols(g_w_in[None])
    big_in = _adamw([(to_cols(w_in), g_w_in_cols, to_cols(m_w_in), to_cols(v_w_in))], 14)
    tiles = lambda a: a.reshape(4 * POOL_GROUP, POOL_GROUP)
    flat = lambda a: a.reshape(1, D_POOL)
    small_params = [("b_ada", b_ada, m_b_ada, v_b_ada), ("b_in", b_in, m_b_in, v_b_in),
                    ("w_pool_mix", tiles(w_pool_mix), tiles(m_w_pool_mix), tiles(v_w_pool_mix)),
                    ("b_pool_mix", flat(b_pool_mix), flat(m_b_pool_mix), flat(v_b_pool_mix)),
                    ("pool_scale", pool_scale, m_pool_scale, v_pool_scale), ("b_out", b_out, m_b_out, v_b_out),
                    ("ln_g", ln_g, m_ln_g, v_ln_g), ("ln_b", ln_b, m_ln_b, v_ln_b)]
    sm = _adamw_small(small_sum, g_b_ada, small_params)
    sm_idx = {p[0]: n for n, p in enumerate(small_params)}
    shapes = {"w_pool_mix": (1, 4, POOL_GROUP, POOL_GROUP), "b_pool_mix": (1, 4, POOL_GROUP)}

    names = ["w_ada", "b_ada", "w_in", "b_in", "w_pool_mix", "b_pool_mix", "pool_scale", "w_out", "b_out",
             "ln_g", "ln_b"]
    big_idx = {"w_ada": 0, "w_out": 1}

    def leaf(kind, name):
        if name == "w_in":
            return from_cols(g_w_in_cols if kind == 0 else big_in[kind - 1])
        if name in big_idx:
            if kind == 0:
                return (g_w_ada, g_w_out)[big_idx[name]][None]
            return big[3 * big_idx[name] + kind - 1][None]
        val = sm[4 * sm_idx[name] + kind]
        return val.reshape(shapes[name]) if name in shapes else val

    outs = [loss, grad_x[None]]
    for kind in range(4):
        outs += [leaf(kind, n) for n in names]
    return tuple(outs)
```

```python
import functools

import numpy as np
import jax
import jax.numpy as jnp
from jax import lax
from jax.experimental import pallas as pl
from jax.experimental.pallas import tpu as pltpu

F32 = jnp.float32
BF16 = jnp.bfloat16
MESH = pl.DeviceIdType.MESH

D = 1024
D_ATT = 512
D_POOL = 512
N_HEADS = 8
HEAD_DIM = 64
N_PAIR = N_HEADS // 2
POOL_WINDOWS = (2, 4, 8, 16)
POOL_GROUP = 128
POOL_HALO = 16
LN_EPS = 1e-5
ALPHA = 2.0 ** 0.25
D_IN = 3 * D_ATT + N_HEADS + D_POOL + D_ATT + D_POOL
N_CHIPS = 4
SHARD_IN = D_IN // N_CHIPS
SHARD_ADA = 3 * D // N_CHIPS
SHARD_OUT = D // N_CHIPS

O_QKV, O_F, O_P, O_G, D_PAD = 0, 1536, 1664, 2176, 3200
Q_SCALE = HEAD_DIM ** -0.5

ADAM_LR, ADAM_B1, ADAM_B2, ADAM_EPS, ADAM_WD, ADAM_STEP = 0.001, 0.9, 0.999, 1e-08, 0.01, 10

NEG = -1e30

VMEM_LIMIT = 56 * 1024 * 1024

TM_PROJ = 512
T_ATT = 512
TM_MID = 256
TM_GW = 1024
TM_DU = 512

REL7 = [(0, 0, 1), (0, 1, 0), (0, 1, 1), (1, 0, 0), (1, 0, 1), (1, 1, 0), (1, 1, 1)]
REL3 = [(0, 1), (1, 0), (1, 1)]

SMALL_SEGS = {}
_row = 0
for _name, _n in (("b_in", D_IN), ("w_pool_mix", 65536), ("b_pool_mix", 512), ("pool_scale", 512),
                  ("b_out", 1024), ("ln_g", 1024), ("ln_b", 1024), ("loss", 1)):
    _rows = -(-_n // 1024) * 8
    SMALL_SEGS[_name] = (_row, _rows)
    _row += _rows
SMALL_ROWS = -(-_row // 16) * 16


def _params(**kw):
    return pltpu.CompilerParams(vmem_limit_bytes=VMEM_LIMIT, **kw)


def _flip(v, d):
    return v if d == 0 else 1 - v


def _dot(a, b):
    return jnp.dot(a, b, preferred_element_type=F32)


def _dot_nt(a, b):
    return lax.dot_general(a, b, (((1,), (1,)), ((), ())), preferred_element_type=F32)


def _dot_tn(a, b):
    return lax.dot_general(a, b, (((0,), (0,)), ((), ())), preferred_element_type=F32)


def _sigmoid(v):
    return 1.0 / (1.0 + jnp.exp(-v))


def _colsum(v):
    return jnp.sum(v, axis=0, keepdims=True)


def _gather_stages(pos, src_ref, dst_ref, half, own_sem, s_sem, r_sem, fs_sem, fr_sem):
    x, y, cc, chip, sib = pos
    own = pltpu.make_async_copy(src_ref, dst_ref.at[chip], own_sem)
    first, landed, others = [], [], []
    for k, (dx, dy) in enumerate(REL3):
        px, py = _flip(x, dx), _flip(y, dy)
        first.append(pltpu.make_async_remote_copy(
            src_ref=src_ref.at[half(cc)], dst_ref=dst_ref.at[(chip,) + half(cc)],
            send_sem=s_sem.at[k], recv_sem=r_sem.at[k], device_id=(px, py, cc), device_id_type=MESH))
        landed.append(dst_ref.at[(2 * px + py,) + half(cc)])
        others.append(dst_ref.at[(2 * px + py,) + half(1 - cc)])
    passed = [pltpu.make_async_remote_copy(src_ref=landed[k], dst_ref=landed[k], send_sem=fs_sem.at[k],
                                           recv_sem=fr_sem.at[k], device_id=sib, device_id_type=MESH)
              for k in range(3)]

    def start():
        own.start()
        for cp in first:
            cp.start()

    def forward():
        for k in range(3):
            pltpu.make_async_remote_copy(src_ref=landed[k], dst_ref=landed[k], send_sem=s_sem.at[k],
                                         recv_sem=r_sem.at[k], device_id=sib, device_id_type=MESH).wait_recv()
            passed[k].start()

    def finish():
        for k in range(3):
            pltpu.make_async_remote_copy(src_ref=others[k], dst_ref=others[k], send_sem=fs_sem.at[k],
                                         recv_sem=fr_sem.at[k], device_id=sib, device_id_type=MESH).wait_recv()
        for cp in first + passed:
            cp.wait_send()
        own.wait()

    return start, forward, finish


def _gather_scratch():
    return [pltpu.SemaphoreType.DMA, pltpu.SemaphoreType.DMA((3,)), pltpu.SemaphoreType.DMA((3,)),
            pltpu.SemaphoreType.DMA((3,)), pltpu.SemaphoreType.DMA((3,))]


def _gather_and_ada(c, w_ada, b_ada4, w_in_sh):
    def body(c_ref, w_ref, b_ref, win_ref, call_ref, ada_ref, wt_pad_ref,
             win_all, cslab, sbuf, rbuf, cs_sem, cr_sem, as_sem, ar_sem, *gather_sems):
        x, y, cc = lax.axis_index("x"), lax.axis_index("y"), lax.axis_index("c")
        me = 4 * x + 2 * y + cc
        chip = 2 * x + y
        lane_half = lambda which: (slice(None), pl.ds(pl.multiple_of(which * (D // 2), D // 2), D // 2))
        start, forward, finish = _gather_stages((x, y, cc, chip, (x, y, 1 - cc)), win_ref, win_all, lane_half,
                                                *gather_sems)
        start()

        cslab[...] = jnp.broadcast_to(c_ref[...], (8, D))
        call_ref[me] = cslab[...]
        gathers = []
        for k, (dx, dy, dc) in enumerate(REL7):
            cp = pltpu.make_async_remote_copy(
                src_ref=cslab, dst_ref=call_ref.at[me], send_sem=cs_sem.at[k], recv_sem=cr_sem.at[k],
                device_id=(_flip(x, dx), _flip(y, dy), _flip(cc, dc)), device_id_type=MESH)
            cp.start()
            gathers.append(cp)
        for cp in gathers:
            cp.wait()
        slab_row = lax.broadcasted_iota(jnp.int32, (8, 1), 0)
        mat = jnp.zeros((8, D), F32)
        for r in range(8):
            mat = jnp.where(slab_row == r, call_ref[r], mat)
        act = (mat * _sigmoid(mat)).astype(BF16)
        part = _dot(act, w_ref[...].astype(BF16))
        sends = []
        for k, (dx, dy) in enumerate(REL3):
            px, py = _flip(x, dx), _flip(y, dy)
            r = 4 * px + 2 * py + cc
            piece = _colsum(jnp.where(slab_row == r, part, 0.0))
            sbuf[k] = jnp.broadcast_to(piece, (8, SHARD_ADA))
            cp = pltpu.make_async_remote_copy(
                src_ref=sbuf.at[k], dst_ref=rbuf.at[k], send_sem=as_sem.at[k], recv_sem=ar_sem.at[k],
                device_id=(px, py, cc), device_id_type=MESH)
            cp.start()
            sends.append(cp)
        own_piece = _colsum(jnp.where(slab_row == me, part, 0.0))
        ada_ref[chip] = jnp.broadcast_to(own_piece, (8, SHARD_ADA)) + b_ref[chip]
        for k, (dx, dy) in enumerate(REL3):
            sends[k].wait()
            a = 2 * _flip(x, dx) + _flip(y, dy)
            ada_ref[a] = rbuf[k] + b_ref[a]

        forward()
        finish()
        n_real = 3 * D_ATT + N_HEADS
        for a in range(N_CHIPS):
            lo, hi = a * SHARD_IN, (a + 1) * SHARD_IN
            for s0, s1 in ((lo, min(hi, D_ATT)), (max(lo, D_ATT), min(hi, n_real)), (max(lo, n_real), hi)):
                if s0 < s1:
                    rows = win_all[a, s0 - lo:s1 - lo, :]
                    if s1 <= D_ATT:
                        rows = rows * jnp.asarray(Q_SCALE, BF16)
                    shift = O_P - n_real if s0 >= n_real else 0
                    wt_pad_ref[s0 + shift:s1 + shift, :] = rows
        wt_pad_ref[n_real:O_P, :] = jnp.zeros((O_P - n_real, D), BF16)

    vm = pl.BlockSpec(memory_space=pltpu.VMEM)
    return pl.pallas_call(
        body, name="gather_and_ada",
        out_shape=(jax.ShapeDtypeStruct((8, 8, D), F32), jax.ShapeDtypeStruct((4, 8, SHARD_ADA), F32),
                   jax.ShapeDtypeStruct((D_PAD, D), BF16)),
        in_specs=[vm] * 4, out_specs=(vm,) * 3,
        scratch_shapes=[pltpu.VMEM((N_CHIPS, SHARD_IN, D), BF16),
                        pltpu.VMEM((8, D), F32), pltpu.VMEM((3, 8, SHARD_ADA), F32),
                        pltpu.VMEM((3, 8, SHARD_ADA), F32),
                        pltpu.SemaphoreType.DMA((7,)), pltpu.SemaphoreType.DMA((7,)),
                        pltpu.SemaphoreType.DMA((3,)), pltpu.SemaphoreType.DMA((3,))] + _gather_scratch(),
        compiler_params=_params(),
    )(c, w_ada, b_ada4, w_in_sh)


def _shard_cols():
    cut, gap = O_F + N_HEADS, O_P - (O_F + N_HEADS)
    out = []
    for a in range(N_CHIPS):
        lo, hi = a * SHARD_IN, (a + 1) * SHARD_IN
        out.append(([(lo, min(hi, cut))] if lo < cut else []) + ([(max(lo, cut) + gap, hi + gap)] if hi > cut else []))
    return out


def _scatter_stages(pos, g_ref, sc_ref, out_ref, sib_buf, send_buf, ici_buf, sem1, sem2s, sem2r, sem3, part=(0, 1),
                    cols=None, own_buf=None):
    x, y, cc, chip, sib = pos
    q, n_parts = part
    RH = (g_ref.shape[1] if cols is None else g_ref.shape[0]) // 2 // n_parts
    mine = pl.ds(pl.multiple_of((cc * n_parts + q) * RH, RH), RH)
    theirs = pl.ds(pl.multiple_of(((1 - cc) * n_parts + q) * RH, RH), RH)
    cp1 = pltpu.make_async_remote_copy(
        src_ref=g_ref.at[:, theirs, :] if cols is None else g_ref.at[theirs, :], dst_ref=sib_buf,
        send_sem=sem1.at[0], recv_sem=sem1.at[1], device_id=sib, device_id_type=MESH)
    sends = []
    for k, (dx, dy) in enumerate(REL3):
        px, py = _flip(x, dx), _flip(y, dy)
        sends.append(pltpu.make_async_remote_copy(
            src_ref=send_buf.at[2 * px + py], dst_ref=ici_buf.at[chip],
            send_sem=sem2s.at[k], recv_sem=sem2r.at[k], device_id=(px, py, cc), device_id_type=MESH))
    cp3 = pltpu.make_async_remote_copy(
        src_ref=out_ref.at[mine, :], dst_ref=out_ref.at[mine, :], send_sem=sem3.at[0], recv_sem=sem3.at[1],
        device_id=sib, device_id_type=MESH)

    def finish1():
        cp1.wait()
        if cols is None:
            for a in range(N_CHIPS):
                both = g_ref[a, mine, :] + sib_buf[a]
                sib_buf[a] = both
                send_buf[a] = both.astype(BF16)
        else:
            both = g_ref[mine, :] + sib_buf[...]
            for a, pieces in enumerate(cols):
                at = 0
                for lo, hi in pieces:
                    own_buf[a, :, at:at + hi - lo] = both[:, lo:hi]
                    send_buf[a, :, at:at + hi - lo] = both[:, lo:hi].astype(BF16)
                    at += hi - lo

    def start2():
        for cp in sends:
            cp.start()
        ici_buf[chip] = send_buf[chip]

    def finish2():
        for cp in sends:
            cp.wait()
        own = (sib_buf if cols is None else own_buf)[chip]
        parts = [jnp.where(chip == a, own, ici_buf[a].astype(F32)) for a in range(N_CHIPS)]
        out_ref[mine, :] = ((parts[0] + parts[1]) + (parts[2] + parts[3])) * sc_ref[chip]

    return [(cp1.start, finish1), (start2, finish2), (cp3.start, cp3.wait)]


def _all_reduce_stages(pos, g_ref, out_ref, sib_buf, ici_buf, sem1, sem2s, sem2r, sem3):
    x, y, cc, chip, sib = pos
    RH = g_ref.shape[0] // 2
    mine = pl.ds(pl.multiple_of(cc * RH, 8), RH)
    theirs = pl.ds(pl.multiple_of((1 - cc) * RH, 8), RH)
    cp1 = pltpu.make_async_remote_copy(
        src_ref=g_ref.at[theirs, :], dst_ref=sib_buf, send_sem=sem1.at[0], recv_sem=sem1.at[1],
        device_id=sib, device_id_type=MESH)
    sends = []
    for k, (dx, dy) in enumerate(REL3):
        px, py = _flip(x, dx), _flip(y, dy)
        sends.append(pltpu.make_async_remote_copy(
            src_ref=sib_buf, dst_ref=ici_buf.at[chip],
            send_sem=sem2s.at[k], recv_sem=sem2r.at[k], device_id=(px, py, cc), device_id_type=MESH))
    cp3 = pltpu.make_async_remote_copy(
        src_ref=out_ref.at[mine, :], dst_ref=out_ref.at[mine, :], send_sem=sem3.at[0], recv_sem=sem3.at[1],
        device_id=sib, device_id_type=MESH)

    def finish1():
        cp1.wait()
        sib_buf[...] = g_ref[mine, :] + sib_buf[...]

    def start2():
        for cp in sends:
            cp.start()
        ici_buf[chip] = sib_buf[...]

    def finish2():
        for cp in sends:
            cp.wait()
        out_ref[mine, :] = (ici_buf[0] + ici_buf[1]) + (ici_buf[2] + ici_buf[3])

    return [(cp1.start, finish1), (start2, finish2), (cp3.start, cp3.wait)]


def _stage_sems():
    return [pltpu.SemaphoreType.DMA((2,)), pltpu.SemaphoreType.DMA((3,)),
            pltpu.SemaphoreType.DMA((3,)), pltpu.SemaphoreType.DMA((2,))]


def _scatter_scratch(r, c):
    return [pltpu.VMEM((N_CHIPS, r // 2, c), F32), pltpu.VMEM((N_CHIPS, r // 2, c), BF16),
            pltpu.VMEM((N_CHIPS, r // 2, c), BF16)] + _stage_sems()


def _reduce_all(gw_pad, sc_in, small, dada):
    R = small.shape[0]
    W = dada.shape[1]
    r_in, p_in = gw_pad.shape
    c_in = SHARD_IN
    chunk = r_in // 4

    def chunk_scratch():
        return ([pltpu.VMEM((chunk, p_in), F32), pltpu.VMEM((N_CHIPS, chunk, c_in), BF16),
                 pltpu.VMEM((N_CHIPS, chunk, c_in), BF16)] + _stage_sems()
                + [pltpu.VMEM((N_CHIPS, chunk, c_in), F32)])

    n_in = len(chunk_scratch())

    def body(gin_ref, scin_ref, sm_ref, d_ref, oin_ref, osm_ref, dall_ref, *scratch):
        x, y, cc = lax.axis_index("x"), lax.axis_index("y"), lax.axis_index("c")
        me = 4 * x + 2 * y + cc
        pos = (x, y, cc, 2 * x + y, (x, y, 1 - cc))
        dslab, ds_sem, dr_sem = scratch[0:3]
        a_bufs, b_bufs, sm_bufs = scratch[3:3 + n_in], scratch[3 + n_in:3 + 2 * n_in], scratch[3 + 2 * n_in:]
        dslab[...] = jnp.broadcast_to(d_ref[...], (8, W))
        dall_ref[me] = dslab[...]
        gathers = []
        for k, (dx, dy, dc) in enumerate(REL7):
            cp = pltpu.make_async_remote_copy(
                src_ref=dslab, dst_ref=dall_ref.at[me], send_sem=ds_sem.at[k], recv_sem=dr_sem.at[k],
                device_id=(_flip(x, dx), _flip(y, dy), _flip(cc, dc)), device_id_type=MESH)
            cp.start()
            gathers.append(cp)
        cols = _shard_cols()
        first = _scatter_stages(pos, gin_ref, scin_ref, oin_ref, *a_bufs[:-1], part=(0, 2), cols=cols,
                                own_buf=a_bufs[-1])
        second = _scatter_stages(pos, gin_ref, scin_ref, oin_ref, *b_bufs[:-1], part=(1, 2), cols=cols,
                                 own_buf=b_bufs[-1])
        little = _all_reduce_stages(pos, sm_ref, osm_ref, *sm_bufs)
        for plan in (first, second, little):
            plan[0][0]()
        first[0][1]()
        first[1][0]()
        little[0][1]()
        little[1][0]()
        second[0][1]()
        second[1][0]()
        first[1][1]()
        first[2][0]()
        second[1][1]()
        second[2][0]()
        little[1][1]()
        little[2][0]()
        for plan in (first, second, little):
            plan[2][1]()
        for cp in gathers:
            cp.wait()

    scratch = [pltpu.VMEM((8, W), F32), pltpu.SemaphoreType.DMA((7,)), pltpu.SemaphoreType.DMA((7,))]
    scratch += chunk_scratch() + chunk_scratch()
    scratch += [pltpu.VMEM((R // 2, 128), F32), pltpu.VMEM((N_CHIPS, R // 2, 128), F32)] + _stage_sems()
    vm = pl.BlockSpec(memory_space=pltpu.VMEM)
    return pl.pallas_call(
        body, name="reduce_all",
        out_shape=(jax.ShapeDtypeStruct((r_in, c_in), F32),
                   jax.ShapeDtypeStruct((R, 128), F32), jax.ShapeDtypeStruct((8, 8, W), F32)),
        in_specs=[vm] * 4, out_specs=(vm,) * 3,
        scratch_shapes=scratch,
        compiler_params=_params(),
    )(gw_pad, sc_in, small, dada)


def _in_proj(x, shift, scale, wt_pad, b_pad, w_out_sh):
    S = x.shape[0]
    tm = min(TM_PROJ, S)
    n_steps = S // tm
    assert n_steps >= 3

    def body(x_ref, sh_ref, sc_ref, w_ref, b_ref, wo_ref, u_ref, qkv_ref, f_ref, p_ref, g_ref, wo_all,
             wo_buf, *gather_sems):
        i = pl.program_id(0)
        xx, yy, cc = lax.axis_index("x"), lax.axis_index("y"), lax.axis_index("c")
        row_half = lambda which: (pl.ds(pl.multiple_of(which * (SHARD_OUT // 2), SHARD_OUT // 2), SHARD_OUT // 2),
                                  slice(None))
        start, forward, finish = _gather_stages((xx, yy, cc, 2 * xx + yy, (xx, yy, 1 - cc)), wo_ref, wo_buf,
                                                row_half, *gather_sems)
        pl.when(i == 0)(start)
        pl.when(i == n_steps // 2)(forward)

        @pl.when(i == n_steps - 1)
        def _():
            finish()
            wo_all[...] = wo_buf[...]

        u = (x_ref[...] * (1.0 + sc_ref[...]) + sh_ref[...]).astype(BF16)
        u_ref[...] = u
        qkv_ref[...] = (_dot_nt(u, w_ref[O_QKV:O_F, :]) + b_ref[:, O_QKV:O_F]).astype(BF16)
        f_ref[...] = _dot_nt(u, w_ref[O_F:O_P, :]) + b_ref[:, O_F:O_P]
        p_ref[...] = _dot_nt(u, w_ref[O_P:O_G, :]) + b_ref[:, O_P:O_G]
        g_ref[...] = _dot_nt(u, w_ref[O_G:D_PAD, :]) + b_ref[:, O_G:D_PAD]

    row = lambda w: pl.BlockSpec((tm, w), lambda i: (i, 0))
    full = lambda a: pl.BlockSpec(a.shape, lambda i: (0, 0))
    vm = pl.BlockSpec(memory_space=pltpu.VMEM)
    return pl.pallas_call(
        body, name="in_proj", grid=(n_steps,),
        out_shape=(jax.ShapeDtypeStruct((S, D), BF16), jax.ShapeDtypeStruct((S, 3 * D_ATT), BF16),
                   jax.ShapeDtypeStruct((S, 128), F32), jax.ShapeDtypeStruct((S, D_POOL), F32),
                   jax.ShapeDtypeStruct((S, D), F32), jax.ShapeDtypeStruct((N_CHIPS,) + w_out_sh.shape, BF16)),
        in_specs=[row(D), full(shift), full(scale), full(wt_pad), full(b_pad), vm],
        out_specs=(row(D), row(3 * D_ATT), row(128), row(D_POOL), row(D), vm),
        scratch_shapes=[pltpu.VMEM((N_CHIPS,) + w_out_sh.shape, BF16)] + _gather_scratch(),
        compiler_params=_params(dimension_semantics=("arbitrary",)),
    )(x, shift, scale, wt_pad, b_pad, w_out_sh)


def _forget_cumsum(f):
    S = f.shape[0]
    tm = min(T_ATT, S)

    def body(f_ref, out_ref, carry):
        @pl.when(pl.program_id(0) == 0)
        def _():
            carry[...] = jnp.zeros_like(carry)
        v = f_ref[...]
        logf = jnp.minimum(v, 0.0) - jnp.log(1.0 + jnp.exp(-jnp.abs(v)))
        r = lax.broadcasted_iota(jnp.int32, (tm, tm), 0)
        c = lax.broadcasted_iota(jnp.int32, (tm, tm), 1)
        tri = (r <= c).astype(F32)
        rows8 = logf.T[0:8, :]
        cum8 = jnp.dot(rows8, tri, preferred_element_type=F32, precision=lax.Precision.HIGHEST) + carry[...]
        out_ref[...] = jnp.concatenate([cum8, jnp.zeros((128 - 8, tm), F32)], axis=0).T
        last = lax.broadcasted_iota(jnp.int32, (1, tm), 1) == tm - 1
        carry[...] = jnp.sum(jnp.where(last, cum8, 0.0), axis=1, keepdims=True)

    return pl.pallas_call(
        body, name="forget_cumsum", grid=(S // tm,),
        out_shape=jax.ShapeDtypeStruct((S, 128), F32),
        in_specs=[pl.BlockSpec((tm, 128), lambda i: (i, 0))],
        out_specs=pl.BlockSpec((tm, 128), lambda i: (i, 0)),
        scratch_shapes=[pltpu.VMEM((8, 1), F32)],
        compiler_params=_params(dimension_semantics=("arbitrary",)),
    )(f)


def _split3(v):
    hi = v.astype(BF16)
    rest = v - hi.astype(F32)
    mid = rest.astype(BF16)
    lo = (rest - mid.astype(F32)).astype(BF16)
    return hi, mid, lo


def _attention_fwd(qkv, big_f):
    S = qkv.shape[0]
    T = min(T_ATT, S)
    n_t = S // T

    def body(q_ref, k_ref, v_ref, f_ref, o_ref, lse_ref, kaug_sc, vt_sc, m_sc, l_sc, acc_sc):
        hp = pl.program_id(0)
        i = pl.program_id(1)
        lane = lax.broadcasted_iota(jnp.int32, (1, 128), 1)
        sub = lax.broadcasted_iota(jnp.int32, (128, 1), 0)
        head_sel = (lane < HEAD_DIM, lane >= HEAD_DIM)
        head_sel_t = (sub < HEAD_DIM, sub >= HEAD_DIM)
        spare = (HEAD_DIM, 0)
        zero = jnp.zeros((), BF16)

        @pl.when(i == 0)
        def _():
            def prep(jt, carry):
                rows = pl.ds(pl.multiple_of(jt * T, T), T)
                k = k_ref[rows, :]
                ft = f_ref[rows, :]
                vt = v_ref[rows, :].astype(F32).T
                for h in range(2):
                    fh = jnp.sum(jnp.where(lane == 2 * hp + h, ft, 0.0), axis=1, keepdims=True)
                    hi, mid, lo = _split3(-fh)
                    b = spare[h]
                    bias = jnp.where(lane == b, hi, jnp.where(lane == b + 1, mid, jnp.where(lane == b + 2, lo, zero)))
                    kaug_sc[h, rows, :] = jnp.where(head_sel[h], k, bias)
                    vt_sc[h, jt] = jnp.where(head_sel_t[h], vt, 0.0).astype(BF16)
                return carry

            lax.fori_loop(0, n_t, prep, 0)

        q = q_ref[...]
        q_heads = []
        for h in range(2):
            ones = jnp.where((lane >= spare[h]) & (lane < spare[h] + 3), jnp.ones((), BF16), zero)
            q_heads.append(jnp.where(head_sel[h], q, ones))
        m_sc[...] = jnp.full((8, T), NEG, F32)
        l_sc[...] = jnp.zeros((8, T), F32)
        acc_sc[...] = jnp.zeros((128, T), F32)

        def update(j, k_lo, n_k, q_lo, masked):
            rows = pl.ds(pl.multiple_of(j * T + k_lo, n_k), n_k)
            n_q = T - q_lo
            alphas, pvs = [], []
            for h in range(2):
                s_t = _dot_nt(kaug_sc[h, rows, :], q_heads[h][q_lo:, :])
                if masked:
                    rr = lax.broadcasted_iota(jnp.int32, (n_k, n_q), 0) + k_lo
                    cc = lax.broadcasted_iota(jnp.int32, (n_k, n_q), 1) + q_lo
                    s_t = jnp.where(rr <= cc, s_t, NEG)
                m_prev = m_sc[h:h + 1, q_lo:]
                m_new = jnp.maximum(m_prev, jnp.max(s_t, axis=0, keepdims=True))
                alpha = jnp.exp(m_prev - m_new)
                p_t = jnp.exp(s_t - m_new)
                l_sc[h:h + 1, q_lo:] = alpha * l_sc[h:h + 1, q_lo:] + jnp.sum(p_t, axis=0, keepdims=True)
                m_sc[h:h + 1, q_lo:] = m_new
                alphas.append(alpha)
                pvs.append(_dot(vt_sc[h, j, :, k_lo:k_lo + n_k], p_t.astype(BF16)))
            acc_sc[:, q_lo:] = (acc_sc[:, q_lo:] * jnp.where(head_sel_t[0], alphas[0], alphas[1])
                                + (pvs[0] + pvs[1]))

        def two_off_diagonal(jj, carry):
            update(2 * jj, 0, T, 0, False)
            update(2 * jj + 1, 0, T, 0, False)
            return carry

        lax.fori_loop(0, i // 2, two_off_diagonal, 0)

        @pl.when(i % 2 == 1)
        def _():
            update(i - 1, 0, T, 0, False)

        update(i, 0, T, 0, True)
        l = l_sc[...]
        o_ref[...] = (acc_sc[...] / jnp.where(head_sel_t[0], l[0:1, :], l[1:2, :])).T
        is_head = lax.broadcasted_iota(jnp.int32, (8, 1), 0) < 2
        lse_ref[...] = jnp.where(is_head, m_sc[...] + jnp.log(jnp.where(is_head, l, 1.0)), 0.0)

    return pl.pallas_call(
        body, name="attention_fwd", grid=(N_PAIR, n_t),
        out_shape=(jax.ShapeDtypeStruct((S, D_ATT), F32), jax.ShapeDtypeStruct((N_PAIR, n_t, 8, T), F32)),
        in_specs=[pl.BlockSpec((T, 128), lambda hp, i: (i, hp)),
                  pl.BlockSpec((S, 128), lambda hp, i: (0, N_PAIR + hp)),
                  pl.BlockSpec((S, 128), lambda hp, i: (0, 2 * N_PAIR + hp)),
                  pl.BlockSpec((S, 128), lambda hp, i: (0, 0))],
        out_specs=(pl.BlockSpec((T, 128), lambda hp, i: (i, hp)),
                   pl.BlockSpec((None, None, 8, T), lambda hp, i: (hp, i, 0, 0))),
        scratch_shapes=[pltpu.VMEM((2, S, 128), BF16), pltpu.VMEM((2, n_t, 128, T), BF16),
                        pltpu.VMEM((8, T), F32), pltpu.VMEM((8, T), F32), pltpu.VMEM((128, T), F32)],
        compiler_params=_params(dimension_semantics=("arbitrary", "arbitrary")),
    )(qkv, qkv, qkv, big_f)


def _attention_bwd(qkv, datt, att, lse, big_f, gw_out4, sc_out):
    S = qkv.shape[0]
    T = min(T_ATT, S)
    n_t = S // T
    n_steps = N_PAIR * n_t
    marks = (0, n_steps // 8, n_steps // 2, n_steps // 2 + n_steps // 8)

    def body(q_ref, do_ref, o_ref, lse_ref, k_ref, v_ref, fk_ref, gout_ref, scout_ref,
             dq_ref, dk_ref, dv_ref, cs_ref, dfk_ref, dfq_ref, oout_ref, stat_sc, dqt_sc, qaug_sc,
             out_buf, *red_bufs):
        hp = pl.program_id(0)
        j = pl.program_id(1)
        x, y, cc = lax.axis_index("x"), lax.axis_index("y"), lax.axis_index("c")
        plan = _scatter_stages((x, y, cc, 2 * x + y, (x, y, 1 - cc)), gout_ref, scout_ref, out_buf, *red_bufs)
        step = hp * n_t + j
        for n, mark in enumerate(marks):
            @pl.when(step == mark)
            def _(n=n):
                if n > 0:
                    plan[n - 1][1]()
                if n < 3:
                    plan[n][0]()
                else:
                    oout_ref[...] = out_buf[...]

        lane = lax.broadcasted_iota(jnp.int32, (1, 128), 1)
        sub = lax.broadcasted_iota(jnp.int32, (128, 1), 0)
        head_sel = (lane < HEAD_DIM, lane >= HEAD_DIM)
        head_sel_t = (sub < HEAD_DIM, sub >= HEAD_DIM)
        spare = (HEAD_DIM, 0)
        zero = jnp.zeros((), BF16)
        one = jnp.ones((), BF16)

        def bias_lanes(first, pieces):
            hi, mid, lo = pieces
            return lambda rest: jnp.where(lane == first, hi, jnp.where(lane == first + 1, mid,
                                                                        jnp.where(lane == first + 2, lo, rest)))

        @pl.when(j == 0)
        def _():
            dqt_sc[...] = jnp.zeros_like(dqt_sc)
            cs_ref[...] = jnp.zeros_like(cs_ref)
            dfq_ref[...] = jnp.zeros_like(dfq_ref)

            def prep(i, carry):
                rows = pl.ds(pl.multiple_of(i * T, T), T)
                q = q_ref[rows, :]
                do = do_ref[rows, :]
                prod = o_ref[rows, :] * do.astype(F32)
                d_a = jnp.sum(jnp.where(head_sel[0], prod, 0.0), axis=1, keepdims=True)
                d_b = jnp.sum(jnp.where(head_sel[0], 0.0, prod), axis=1, keepdims=True)
                delta_t = jnp.where(head_sel[0], d_a, d_b).T
                stat_sc[i, 0:1, :] = delta_t[0:1, :]
                stat_sc[i, 1:2, :] = delta_t[HEAD_DIM:HEAD_DIM + 1, :]
                lse = lse_ref[i]
                lse_cols = jnp.where(head_sel_t[0], lse[0:1, :], lse[1:2, :]).T
                for h in range(2):
                    neg_lse = -lse_cols[:, h * HEAD_DIM:h * HEAD_DIM + 1]
                    ones = jnp.where((lane >= spare[h]) & (lane < spare[h] + 3), one, zero)
                    qaug_sc[h, rows, :] = jnp.where(head_sel[h], q, bias_lanes(spare[h] + 3, _split3(neg_lse))(ones))
                return carry

            lax.fori_loop(0, n_t, prep, 0)

        k = k_ref[...]
        v = v_ref[...]
        fk = fk_ref[...]
        kt = k.astype(F32).T
        heads = []
        for h in range(2):
            fkh = jnp.sum(jnp.where(lane == 2 * hp + h, fk, 0.0), axis=1, keepdims=True)
            ones = jnp.where((lane >= spare[h] + 3) & (lane < spare[h] + 6), one, zero)
            kaug = jnp.where(head_sel[h], k, bias_lanes(spare[h], _split3(-fkh))(ones))
            heads.append((kaug, jnp.where(head_sel[h], v, zero), jnp.where(head_sel_t[h], kt, 0.0).astype(BF16)))

        def block(i, k_lo, n_k, q_lo, masked):
            n_q = T - q_lo
            rows = pl.ds(pl.multiple_of(i * T + q_lo, n_q), n_q)
            q = q_ref[rows, :]
            do = do_ref[rows, :]
            stat = stat_sc[i]
            dk = jnp.zeros((n_k, 128), F32)
            dv = jnp.zeros((n_k, 128), F32)
            dqt = jnp.zeros((128, n_q), F32)
            dfs = []
            for h in range(2):
                kaug, vh, kth = heads[h]
                arg = _dot_nt(kaug[k_lo:k_lo + n_k, :], qaug_sc[h, rows, :])
                if masked:
                    rr = lax.broadcasted_iota(jnp.int32, (n_k, n_q), 0) + k_lo
                    cc = lax.broadcasted_iota(jnp.int32, (n_k, n_q), 1) + q_lo
                    arg = jnp.where(rr <= cc, arg, NEG)
                p_t = jnp.exp(arg)
                ds_t = p_t * (_dot_nt(vh[k_lo:k_lo + n_k, :], do) - stat[h:h + 1, q_lo:])
                ds_bf = ds_t.astype(BF16)
                dv = dv + _dot(p_t.astype(BF16), jnp.where(head_sel[h], do, zero))
                dk = dk + _dot(ds_bf, jnp.where(head_sel[h], q, zero))
                dqt = dqt + _dot(kth[:, k_lo:k_lo + n_k], ds_bf)
                dfs.append(jnp.sum(ds_t, axis=1, keepdims=True))
                dfq_ref[i, h:h + 1, q_lo:] += _colsum(ds_t)
            dqt_sc[i, :, q_lo:] += dqt
            return dk, dv, dfs[0], dfs[1]

        def off_diagonal(i, acc):
            return tuple(a + b for a, b in zip(acc, block(i, 0, T, 0, False)))

        half = T // 2
        early = block(j, 0, half, 0, True)
        late = block(j, half, half, half, True)
        acc1 = tuple(jnp.concatenate([a, b], axis=0) for a, b in zip(early, late))
        n_off = n_t - 1 - j
        acc2 = lax.fori_loop(0, n_off // 2,
                             lambda ii, a: off_diagonal(j + 2 + 2 * ii, off_diagonal(j + 1 + 2 * ii, a)), acc1)
        dk_acc, dv_acc, dfa, dfb = lax.fori_loop(0, n_off % 2, lambda _, a: off_diagonal(n_t - 1, a), acc2)
        dk_ref[...] = dk_acc.astype(BF16)
        dv_ref[...] = dv_acc.astype(BF16)
        dfk_ref[...] = -jnp.where(lane == 0, dfa, jnp.where(lane == 1, dfb, 0.0))
        cs_ref[:, 128:256] = cs_ref[:, 128:256] + _colsum(dk_acc)
        cs_ref[:, 256:384] = cs_ref[:, 256:384] + _colsum(dv_acc)

        @pl.when(j == n_t - 1)
        def _():
            def finish(i, tot):
                dq = dqt_sc[i].T
                dq_ref[pl.ds(pl.multiple_of(i * T, T), T), :] = dq.astype(BF16)
                return tot + _colsum(dq)

            cs_ref[:, 0:128] = lax.fori_loop(0, n_t, finish, jnp.zeros((1, 128), F32))

    pair_rows = lambda hp, j: (hp, 0, 0)
    vm = pl.BlockSpec(memory_space=pltpu.VMEM)
    _, r_out, c_out = gw_out4.shape
    return pl.pallas_call(
        body, name="attention_bwd", grid=(N_PAIR, n_t),
        out_shape=(jax.ShapeDtypeStruct((S, D_ATT), BF16), jax.ShapeDtypeStruct((S, D_ATT), BF16),
                   jax.ShapeDtypeStruct((S, D_ATT), BF16), jax.ShapeDtypeStruct((N_PAIR, 1, 384), F32),
                   jax.ShapeDtypeStruct((N_PAIR, S, 128), F32),
                   jax.ShapeDtypeStruct((N_PAIR, n_t, 8, T), F32),
                   jax.ShapeDtypeStruct((r_out, c_out), F32)),
        in_specs=[pl.BlockSpec((S, 128), lambda hp, j: (0, hp)),
                  pl.BlockSpec((S, 128), lambda hp, j: (0, hp)),
                  pl.BlockSpec((S, 128), lambda hp, j: (0, hp)),
                  pl.BlockSpec((None, n_t, 8, T), lambda hp, j: (hp, 0, 0, 0)),
                  pl.BlockSpec((T, 128), lambda hp, j: (j, N_PAIR + hp)),
                  pl.BlockSpec((T, 128), lambda hp, j: (j, 2 * N_PAIR + hp)),
                  pl.BlockSpec((T, 128), lambda hp, j: (j, 0)),
                  vm, vm],
        out_specs=(pl.BlockSpec((S, 128), lambda hp, j: (0, hp)),
                   pl.BlockSpec((T, 128), lambda hp, j: (j, hp)),
                   pl.BlockSpec((T, 128), lambda hp, j: (j, hp)),
                   pl.BlockSpec((None, 1, 384), pair_rows),
                   pl.BlockSpec((None, T, 128), lambda hp, j: (hp, j, 0)),
                   pl.BlockSpec((None, n_t, 8, T), lambda hp, j: (hp, 0, 0, 0)),
                   vm),
        scratch_shapes=[pltpu.VMEM((n_t, 8, T), F32), pltpu.VMEM((n_t, 128, T), F32),
                        pltpu.VMEM((2, S, 128), BF16), pltpu.VMEM((r_out, c_out), F32)]
        + _scatter_scratch(r_out, c_out),
        compiler_params=_params(dimension_semantics=("arbitrary", "arbitrary")),
    )(qkv, datt, att, lse, qkv, qkv, big_f, gw_out4, sc_out)


def _window_counts(first_row, n_rows, window):
    t = lax.broadcasted_iota(jnp.int32, (n_rows, 1), 0) + first_row
    return jnp.minimum((t + 1).astype(F32), float(window))


def _middle(x, tgt, att, g, p, gate, w_mix, b_mix, pool_scale, w_out, b_out, ln_g, ln_b):
    S = x.shape[0]
    tm = min(TM_MID, S)
    halo_blocks = tm // POOL_HALO

    def body(x_ref, t_ref, att_ref, g_ref, p_ref, ph_ref, gate_ref, wm_ref, bm_ref, ps_ref, wo_ref, bo_ref,
             lg_ref, lb_ref,
             dh_ref, datt_ref, dg_ref, dpl_ref, gwo_ref, gwm_ref, vec_ref, loss_ref):
        i = pl.program_id(0)

        @pl.when(i == 0)
        def _():
            gwo_ref[...] = jnp.zeros_like(gwo_ref)
            gwm_ref[...] = jnp.zeros_like(gwm_ref)
            vec_ref[...] = jnp.zeros_like(vec_ref)
            loss_ref[...] = jnp.zeros_like(loss_ref)

        pc = p_ref[...]
        halo = jnp.where(i > 0, ph_ref[...], 0.0)
        pe = jnp.concatenate([halo, pc], axis=0)
        pooled_parts = []
        for gi, w in enumerate(POOL_WINDOWS):
            cur = pe[:, gi * POOL_GROUP:(gi + 1) * POOL_GROUP]
            span = 1
            while span < w:
                cur = cur + pltpu.roll(cur, span, 0)
                span *= 2
            wsum = cur[POOL_HALO:, :]
            mean = wsum / _window_counts(i * tm, tm, w)
            pooled_parts.append(mean - pc[:, gi * POOL_GROUP:(gi + 1) * POOL_GROUP])
        pooled_bf =[v.astype(BF16) for v in pooled_parts]
        mixed = jnp.concatenate([_dot(pooled_bf[gi], wm_ref[gi]) for gi in range(4)], axis=1) + bm_ref[...]
        ps = ps_ref[...]
        pool_out = mixed * ps
        gv = g_ref[...]
        sig = _sigmoid(gv)
        silu = gv * sig
        att = att_ref[...]
        y = jnp.concatenate([att * silu[:, :D_ATT], pool_out * silu[:, D_ATT:]], axis=1)
        y_bf = y.astype(BF16)
        wo = wo_ref[...]
        yo = _dot(y_bf, wo) + bo_ref[...]
        gate = gate_ref[...]
        h = ALPHA * x_ref[...] + gate * yo
        mu = jnp.mean(h, axis=1, keepdims=True)
        hc = h - mu
        var = jnp.mean(hc * hc, axis=1, keepdims=True)
        rstd = lax.rsqrt(var + LN_EPS)
        yhat = hc * rstd
        lg = lg_ref[...]
        out = yhat * lg + lb_ref[...]
        err = out - t_ref[...]
        loss_ref[...] += 0.5 * jnp.sum(jnp.mean(err * err, axis=1, keepdims=True), axis=0, keepdims=True)

        dout = err * (1.0 / D)
        g_ln_b = _colsum(dout)
        g_ln_g = _colsum(dout * yhat)
        dyh = dout * lg
        dh = rstd * (dyh - jnp.mean(dyh, axis=1, keepdims=True)
                     - yhat * jnp.mean(dyh * yhat, axis=1, keepdims=True))
        dh_ref[...] = dh
        d_gate = _colsum(dh * yo)
        dyo = gate * dh
        g_b_out = _colsum(dyo)
        dyo_bf = dyo.astype(BF16)
        gwo_ref[...] += _dot_tn(y_bf, dyo_bf)
        dy = _dot_nt(dyo_bf, wo)
        dsilu = sig * (1.0 + gv * (1.0 - sig))
        dy_a = dy[:, :D_ATT]
        dy_p = dy[:, D_ATT:]
        datt_ref[...] = (dy_a * silu[:, :D_ATT]).astype(BF16)
        dpo = dy_p * silu[:, D_ATT:]
        dg = jnp.concatenate([dy_a * att * dsilu[:, :D_ATT], dy_p * pool_out * dsilu[:, D_ATT:]], axis=1)
        dg_ref[...] = dg.astype(BF16)
        g_dg = _colsum(dg)
        g_ps = _colsum(dpo * mixed)
        dmixed = dpo * ps
        g_bm = _colsum(dmixed)
        dmixed_bf = dmixed.astype(BF16)
        dpl = []
        for gi in range(4):
            dm = dmixed_bf[:, gi * POOL_GROUP:(gi + 1) * POOL_GROUP]
            gwm_ref[gi] += _dot_tn(pooled_bf[gi], dm)
            dpl.append(_dot_nt(dm, wm_ref[gi]))
        dpl_ref[...] = jnp.concatenate(dpl, axis=1)
        vec_ref[0:1, :] += g_ln_g
        vec_ref[1:2, :] += g_ln_b
        vec_ref[2:3, :] += d_gate
        vec_ref[3:4, :] += g_b_out
        vec_ref[4:5, :] += g_dg
        vec_ref[5:6, 0:D_POOL] += g_ps
        vec_ref[6:7, 0:D_POOL] += g_bm

    row = lambda w: pl.BlockSpec((tm, w), lambda i: (i, 0))
    full2 = lambda a: pl.BlockSpec(a.shape, lambda i: (0, 0))
    full3 = lambda a: pl.BlockSpec(a.shape, lambda i: (0, 0, 0))
    return pl.pallas_call(
        body, name="middle", grid=(S // tm,),
        out_shape=(jax.ShapeDtypeStruct((S, D), F32),
                   jax.ShapeDtypeStruct((S, D_ATT), BF16),
                   jax.ShapeDtypeStruct((S, D), BF16),
                   jax.ShapeDtypeStruct((S, D_POOL), F32),
                   jax.ShapeDtypeStruct((D, D), F32),
                   jax.ShapeDtypeStruct((4, POOL_GROUP, POOL_GROUP), F32),
                   jax.ShapeDtypeStruct((8, D), F32),
                   jax.ShapeDtypeStruct((1, 1), F32)),
        in_specs=[row(D), row(D), row(D_ATT), row(D), row(D_POOL),
                  pl.BlockSpec((POOL_HALO, D_POOL), lambda i: (jnp.maximum(i * halo_blocks - 1, 0), 0)),
                  full2(gate), full3(w_mix), full2(b_mix), full2(pool_scale), full2(w_out), full2(b_out),
                  full2(ln_g), full2(ln_b)],
        out_specs=(row(D), row(D_ATT), row(D), row(D_POOL),
                   pl.BlockSpec((D, D), lambda i: (0, 0)),
                   pl.BlockSpec((4, POOL_GROUP, POOL_GROUP), lambda i: (0, 0, 0)),
                   pl.BlockSpec((8, D), lambda i: (0, 0)),
                   pl.BlockSpec((1, 1), lambda i: (0, 0))),
        compiler_params=_params(dimension_semantics=("arbitrary",)),
    )(x, tgt, att, g, p, p, gate, w_mix, b_mix, pool_scale, w_out, b_out, ln_g, ln_b)


def _tail(dpl, dfk, dfq, f):
    S = dpl.shape[0]
    tm = min(T_ATT, S)
    n_t = S // tm
    halo_blocks = tm // POOL_HALO
    last_halo = S // POOL_HALO - 1

    def body(d_ref, dn_ref, dfk_ref, dfq_ref, f_ref, dp_ref, df_ref, cs_ref, carry):
        s = pl.program_id(0)
        i = n_t - 1 - s

        @pl.when(s == 0)
        def _():
            carry[...] = jnp.zeros_like(carry)
            cs_ref[...] = jnp.zeros_like(cs_ref)

        dc = d_ref[...]
        nxt = jnp.where(s > 0, dn_ref[...], 0.0)
        de = jnp.concatenate([dc, nxt], axis=0)
        n_e = tm + POOL_HALO
        parts = []
        for gi, w in enumerate(POOL_WINDOWS):
            cur = de[:, gi * POOL_GROUP:(gi + 1) * POOL_GROUP] / _window_counts(i * tm, n_e, w)
            span = 1
            while span < w:
                cur = cur + pltpu.roll(cur, n_e - span, 0)
                span *= 2
            parts.append(cur[:tm, :] - dc[:, gi * POOL_GROUP:(gi + 1) * POOL_GROUP])
        dp = jnp.concatenate(parts, axis=1)
        dp_ref[...] = dp.astype(BF16)
        cs_ref[0:1, :] += _colsum(dp)

        r = lax.broadcasted_iota(jnp.int32, (tm, tm), 0)
        c = lax.broadcasted_iota(jnp.int32, (tm, tm), 1)
        tri = (r >= c).astype(F32)
        k_cols = dfk_ref[0]
        rows8 = dfq_ref[0]
        for hp in range(1, N_PAIR):
            k_cols = k_cols + pltpu.roll(dfk_ref[hp], 2 * hp, 1)
            rows8 = rows8 + pltpu.roll(dfq_ref[hp], 2 * hp, 0)
        rows8 = rows8 + k_cols.T[0:8, :]
        dlogf8 = jnp.dot(rows8, tri, preferred_element_type=F32, precision=lax.Precision.HIGHEST) + carry[...]
        first = lax.broadcasted_iota(jnp.int32, (1, tm), 1) == 0
        carry[...] = jnp.sum(jnp.where(first, dlogf8, 0.0), axis=1, keepdims=True)
        dlogf = jnp.concatenate([dlogf8, jnp.zeros((128 - 8, tm), F32)], axis=0).T
        df = dlogf * _sigmoid(-f_ref[...])
        df_ref[...] = df.astype(BF16)
        cs_ref[1:2, 0:128] += _colsum(df)

    rev = lambda w: pl.BlockSpec((tm, w), lambda s: (n_t - 1 - s, 0))
    return pl.pallas_call(
        body, name="tail", grid=(n_t,),
        out_shape=(jax.ShapeDtypeStruct((S, D_POOL), BF16), jax.ShapeDtypeStruct((S, 128), BF16),
                   jax.ShapeDtypeStruct((8, D_POOL), F32)),
        in_specs=[rev(D_POOL),
                  pl.BlockSpec((POOL_HALO, D_POOL),
                               lambda s: (jnp.minimum((n_t - s) * halo_blocks, last_halo), 0)),
                  pl.BlockSpec((N_PAIR, tm, 128), lambda s: (0, n_t - 1 - s, 0)),
                  pl.BlockSpec((N_PAIR, None, 8, tm), lambda s: (0, n_t - 1 - s, 0, 0)),
                  rev(128)],
        out_specs=(rev(D_POOL), rev(128), pl.BlockSpec((8, D_POOL), lambda s: (0, 0))),
        scratch_shapes=[pltpu.VMEM((8, 1), F32)],
        compiler_params=_params(dimension_semantics=("arbitrary",)),
    )(dpl, dpl, dfk, dfq, f)


PIECES = ((O_QKV, D_ATT), (O_QKV + D_ATT, D_ATT), (O_QKV + 2 * D_ATT, D_ATT), (O_F, 128), (O_P, D_POOL), (O_G, D))


def _grad_w_in(u, pieces):
    S = u.shape[0]
    tm = min(TM_GW, S)
    n_t = S // tm

    def body(u_ref, *rest):
        piece_refs, out_ref, acc, sem = rest[:6], rest[6], rest[7], rest[8]
        i = pl.program_id(0)

        @pl.when(i == 0)
        def _():
            acc[...] = jnp.zeros_like(acc)

        u_t = u_ref[...]
        for (off, w), ref in zip(PIECES, piece_refs):
            acc[:, off:off + w] += _dot_tn(u_t, ref[...])

        @pl.when(i == n_t - 1)
        def _():
            cp = pltpu.make_async_copy(acc, out_ref, sem)
            cp.start()
            cp.wait()

    return pl.pallas_call(
        body, name="grad_w_in", grid=(n_t,),
        out_shape=jax.ShapeDtypeStruct((D, D_PAD), F32),
        in_specs=[pl.BlockSpec((tm, D), lambda i: (i, 0))]
        + [pl.BlockSpec((tm, w), lambda i: (i, 0)) for _, w in PIECES],
        out_specs=pl.BlockSpec(memory_space=pl.ANY),
        scratch_shapes=[pltpu.VMEM((D, D_PAD), F32), pltpu.SemaphoreType.DMA],
        compiler_params=_params(dimension_semantics=("arbitrary",)),
    )(u, *pieces)


def _grad_x(pieces, wt_pad, dh, x, scale):
    S = x.shape[0]
    tm = min(TM_DU, S)

    def body(*refs):
        piece_refs = refs[:6]
        w_ref, dh_ref, x_ref, sc_ref, gx_ref, vec_ref = refs[6:]

        @pl.when(pl.program_id(0) == 0)
        def _():
            vec_ref[...] = jnp.zeros_like(vec_ref)

        du = jnp.zeros((tm, D), F32)
        for (off, w), ref in zip(PIECES, piece_refs):
            du = du + _dot(ref[...], w_ref[off:off + w, :])
        xv = x_ref[...]
        gx_ref[...] = ALPHA * dh_ref[...] + du * (1.0 + sc_ref[...])
        vec_ref[0:1, :] += _colsum(du)
        vec_ref[1:2, :] += _colsum(du * xv)

    row = lambda w: pl.BlockSpec((tm, w), lambda i: (i, 0))
    return pl.pallas_call(
        body, name="grad_x", grid=(S // tm,),
        out_shape=(jax.ShapeDtypeStruct((S, D), F32), jax.ShapeDtypeStruct((8, D), F32)),
        in_specs=[row(w) for _, w in PIECES]
        + [pl.BlockSpec(wt_pad.shape, lambda i: (0, 0)), row(D), row(D), pl.BlockSpec((1, D), lambda i: (0, 0))],
        out_specs=(row(D), pl.BlockSpec((8, D), lambda i: (0, 0))),
        compiler_params=_params(dimension_semantics=("arbitrary",)),
    )(*pieces, wt_pad, dh, x, scale)


def _grad_ada(c_all, dada_all, dada_cols):
    def body(c_ref, dall_ref, dcol_ref, gw_ref, gb_ref):
        rows = lax.broadcasted_iota(jnp.int32, (8, 1), 0)
        cm = jnp.zeros((8, D), F32)
        dm = jnp.zeros((8, 3 * D), F32)
        for r in range(8):
            cm = jnp.where(rows == r, c_ref[r], cm)
            dm = jnp.where(rows == r, dall_ref[r], dm)
        act = cm * _sigmoid(cm)
        pad = jnp.zeros((8, D), F32)
        lhs = jnp.concatenate([act, pad], axis=0).astype(BF16)
        rhs = jnp.concatenate([dcol_ref[...], jnp.zeros((8, SHARD_ADA), F32)], axis=0).astype(BF16)
        gw_ref[...] = _dot_tn(lhs, rhs)
        gb_ref[...] = _colsum(dm)

    vm = pl.BlockSpec(memory_space=pltpu.VMEM)
    return pl.pallas_call(
        body, name="grad_ada",
        out_shape=(jax.ShapeDtypeStruct((D, SHARD_ADA), F32), jax.ShapeDtypeStruct((1, 3 * D), F32)),
        in_specs=[vm, vm, vm], out_specs=(vm, vm),
        compiler_params=_params(),
    )(c_all, dada_all, dada_cols)


def _adamw_math(w, g, m, v):
    m = ADAM_B1 * m + (1.0 - ADAM_B1) * g
    v = ADAM_B2 * v + (1.0 - ADAM_B2) * (g * g)
    m_hat = m / (1.0 - ADAM_B1 ** ADAM_STEP)
    v_hat = v / (1.0 - ADAM_B2 ** ADAM_STEP)
    delta = -ADAM_LR * (m_hat / (jnp.sqrt(v_hat) + ADAM_EPS) + ADAM_WD * w)
    return delta, m, v


def _adamw(groups, n_steps):
    n = len(groups)

    def body(*refs):
        ins, outs = refs[:4 * n], refs[4 * n:]
        for t in range(n):
            w, g, m, v = (r[...] for r in ins[4 * t:4 * t + 4])
            d, m2, v2 = _adamw_math(w, g, m, v)
            outs[3 * t][...] = d
            outs[3 * t + 1][...] = m2
            outs[3 * t + 2][...] = v2

    in_specs, out_specs, out_shape, args = [], [], [], []
    for (w, g, m, v) in groups:
        rest = w.shape[1:]
        spec = pl.BlockSpec((w.shape[0] // n_steps,) + rest, lambda i, nd=len(rest): (i,) + (0,) * nd)
        in_specs += [spec] * 4
        out_specs += [spec] * 3
        out_shape += [jax.ShapeDtypeStruct(w.shape, F32)] * 3
        args += [w, g, m, v]
    return pl.pallas_call(
        body, name="adamw_%d_%d" % (n, n_steps), grid=(n_steps,),
        out_shape=tuple(out_shape), in_specs=in_specs, out_specs=tuple(out_specs),
        compiler_params=_params(dimension_semantics=("arbitrary",)),
    )(*args)


def _adamw_small(small_sum, g_b_ada, params):
    n = len(params)

    def body(gs_ref, gba_ref, *refs):
        ins, outs = refs[:3 * n], refs[3 * n:]
        for t, (name, w0, _, _) in enumerate(params):
            w_ref, m_ref, v_ref = ins[3 * t:3 * t + 3]
            first = SMALL_SEGS[name][0] if name in SMALL_SEGS else None
            if w0.shape[0] > 1:
                pieces = [((slice(None), slice(None)), gs_ref[first:first + w0.shape[0], :])]
            else:
                pieces = []
                for r in range(-(-w0.shape[1] // 128)):
                    lanes = slice(128 * r, min(128 * r + 128, w0.shape[1]))
                    g = gba_ref[0:1, lanes] if first is None else gs_ref[first + r:first + r + 1, 0:lanes.stop - lanes.start]
                    pieces.append(((slice(0, 1), lanes), g))
            for where, g in pieces:
                d, m2, v2 = _adamw_math(w_ref[where], g, m_ref[where], v_ref[where])
                for ref, val in zip(outs[4 * t:4 * t + 4], (g, d, m2, v2)):
                    ref[where] = val

    vm = pl.BlockSpec(memory_space=pltpu.VMEM)
    args = [small_sum, g_b_ada]
    out_shape = []
    for _, w, m, v in params:
        args += [w, m, v]
        out_shape += [jax.ShapeDtypeStruct(w.shape, F32)] * 4
    return pl.pallas_call(
        body, name="adamw_small",
        out_shape=tuple(out_shape), in_specs=[vm] * len(args), out_specs=(vm,) * len(out_shape),
        compiler_params=_params(),
    )(*args)


def _pack_small(parts):
    rows = []
    used = 0
    for name, (first, n_rows) in SMALL_SEGS.items():
        if first > used:
            rows.append(jnp.zeros((first - used, 128), F32))
        flat = parts[name].reshape(-1)
        flat = jnp.pad(flat, (0, n_rows * 128 - flat.shape[0]))
        rows.append(flat.reshape(n_rows, 128))
        used = first + n_rows
    rows.append(jnp.zeros((SMALL_ROWS - used, 128), F32))
    return jnp.concatenate(rows, axis=0)


def _unpack_small(buf, name, shape):
    first, n_rows = SMALL_SEGS[name]
    n = int(np.prod(shape))
    return buf[first:first + n_rows].reshape(-1)[:n].reshape(shape)


def _pad_in(v):
    r = v.shape[0]
    z = jnp.zeros((r, O_P - O_F - N_HEADS), v.dtype)
    return jnp.concatenate([v[:, :3 * D_ATT + N_HEADS], z, v[:, 3 * D_ATT + N_HEADS:]], axis=1)


def _unpad_in(v):
    return jnp.concatenate([v[:, :O_F + N_HEADS], v[:, O_P:]], axis=1)


def _shards_in(v):
    gap = O_P - (O_F + N_HEADS)
    parts = []
    for a in range(N_CHIPS):
        lo, hi = a * SHARD_IN, (a + 1) * SHARD_IN
        cut = O_F + N_HEADS
        if hi <= cut:
            parts.append(v[:, lo:hi])
        elif lo >= cut:
            parts.append(v[:, lo + gap:hi + gap])
        else:
            parts.append(jnp.concatenate([v[:, lo:cut], v[:, cut + gap:hi + gap]], axis=1))
    return jnp.stack(parts, axis=0)


def kernel(x, c, w_ada, b_ada, w_in, b_in, w_pool_mix, b_pool_mix, pool_scale, w_out, b_out, ln_g, ln_b, loss_target, m_w_ada, m_b_ada, m_w_in, m_b_in, m_w_pool_mix, m_b_pool_mix, m_pool_scale, m_w_out, m_b_out, m_ln_g, m_ln_b, v_w_ada, v_b_ada, v_w_in, v_b_in, v_w_pool_mix, v_b_pool_mix, v_pool_scale, v_w_out, v_b_out, v_ln_g, v_ln_b):
    S = x.shape[1]
    T = min(T_ATT, S)
    n_t = S // T
    chip = 2 * lax.axis_index("x") + lax.axis_index("y")
    x2 = x[0]
    tgt = loss_target[0]
    q_scale = jnp.concatenate([jnp.full((1, D_ATT), Q_SCALE, F32), jnp.ones((1, D_PAD - D_ATT), F32)], axis=1)

    to_cols = lambda a: jnp.transpose(a, (2, 0, 1))
    from_cols = lambda a: jnp.transpose(a, (1, 2, 0))
    c_all, ada4, wt_pad = _gather_and_ada(
        c, w_ada[0], b_ada.reshape(4, 1, SHARD_ADA), to_cols(w_in).reshape(SHARD_IN, D).astype(BF16))
    ada = ada4[:, 0, :].reshape(1, 3 * D)
    shift, scale, gate = ada[:, :D], ada[:, D:2 * D], ada[:, 2 * D:]
    b_pad = _pad_in(b_in) * q_scale
    w_mix_bf = w_pool_mix[0].astype(BF16)

    u, qkv, f, p, g, w_out_all = _in_proj(x2, shift, scale, wt_pad, b_pad, w_out[0].astype(BF16))
    w_out_full = w_out_all.reshape(D, D)
    big_f = _forget_cumsum(f)
    att, lse = _attention_fwd(qkv, big_f)

    dh, datt, dg, dpl, gw_out, gw_mix, vec, loss_part = _middle(
        x2, tgt, att, g, p, gate, w_mix_bf, b_pool_mix.reshape(1, D_POOL), pool_scale, w_out_full, b_out, ln_g, ln_b)
    dq, dk, dv, cs_att, dfk, dfq, g_w_out = _attention_bwd(
        qkv, datt, att, lse, big_f, gw_out.reshape(N_CHIPS, SHARD_OUT, D), jnp.ones((N_CHIPS, 1, D), F32))
    dp, df, cs_tail = _tail(dpl, dfk, dfq, f)
    pieces = (dq, dk, dv, df, dp, dg)
    gw_pad = _grad_w_in(u, pieces)
    grad_x, vec_x = _grad_x(pieces, wt_pad, dh, x2, scale)

    cs_qkv = jnp.transpose(cs_att.reshape(N_PAIR, 3, 128), (1, 0, 2)).reshape(1, 3 * D_ATT)
    gb_pad = jnp.concatenate([cs_qkv, cs_tail[1:2, 0:128], cs_tail[0:1, :], vec[4:5, :]], axis=1) * q_scale
    dada = jnp.concatenate([vec_x[0:1, :], vec_x[1:2, :], vec[2:3, :]], axis=1)
    small = _pack_small({
        "b_in": _unpad_in(gb_pad), "w_pool_mix": gw_mix, "b_pool_mix": vec[6:7, :D_POOL],
        "pool_scale": vec[5:6, :D_POOL], "b_out": vec[3:4, :], "ln_g": vec[0:1, :], "ln_b": vec[1:2, :],
        "loss": loss_part})

    g_w_in, small_sum, dada_all = _reduce_all(
        gw_pad, _shards_in(q_scale), small, dada)
    dada_cols = lax.dynamic_slice(dada_all[:, 0, :], (0, chip * SHARD_ADA), (8, SHARD_ADA))
    g_w_ada, g_b_ada = _grad_ada(c_all, dada_all, dada_cols)
    loss = _unpack_small(small_sum, "loss", (1,))[0]

    big = _adamw([(w_ada[0], g_w_ada, m_w_ada[0], v_w_ada[0]),
                  (w_out[0], g_w_out, m_w_out[0], v_w_out[0])], 8)
    g_w_in_cols = to_cols(g_w_in[None])
    big_in = _adamw([(to_cols(w_in), g_w_in_cols, to_cols(m_w_in), to_cols(v_w_in))], 14)
    tiles = lambda a: a.reshape(4 * POOL_GROUP, POOL_GROUP)
    flat = lambda a: a.reshape(1, D_POOL)
    small_params = [("b_ada", b_ada, m_b_ada, v_b_ada), ("b_in", b_in, m_b_in, v_b_in),
                    ("w_pool_mix", tiles(w_pool_mix), tiles(m_w_pool_mix), tiles(v_w_pool_mix)),
                    ("b_pool_mix", flat(b_pool_mix), flat(m_b_pool_mix), flat(v_b_pool_mix)),
                    ("pool_scale", pool_scale, m_pool_scale, v_pool_scale), ("b_out", b_out, m_b_out, v_b_out),
                    ("ln_g", ln_g, m_ln_g, v_ln_g), ("ln_b", ln_b, m_ln_b, v_ln_b)]
    sm = _adamw_small(small_sum, g_b_ada, small_params)
    sm_idx = {p[0]: n for n, p in enumerate(small_params)}
    shapes = {"w_pool_mix": (1, 4, POOL_GROUP, POOL_GROUP), "b_pool_mix": (1, 4, POOL_GROUP)}

    names = ["w_ada", "b_ada", "w_in", "b_in", "w_pool_mix", "b_pool_mix", "pool_scale", "w_out", "b_out",
             "ln_g", "ln_b"]
    big_idx = {"w_ada": 0, "w_out": 1}

    def leaf(kind, name):
        if name == "w_in":
            return from_cols(g_w_in_cols if kind == 0 else big_in[kind - 1])
        if name in big_idx:
            if kind == 0:
                return (g_w_ada, g_w_out)[big_idx[name]][None]
            return big[3 * big_idx[name] + kind - 1][None]
        val = sm[4 * sm_idx[name] + kind]
        return val.reshape(shapes[name]) if name in shapes else val

    outs = [loss, grad_x[None]]
    for kind in range(4):
        outs += [leaf(kind, n) for n in names]
    return tuple(outs)
```

```python
import functools

import numpy as np
import jax
import jax.numpy as jnp
from jax import lax
from jax.experimental import pallas as pl
from jax.experimental.pallas import tpu as pltpu

F32 = jnp.float32
BF16 = jnp.bfloat16
MESH = pl.DeviceIdType.MESH

D = 1024
D_ATT = 512
D_POOL = 512
N_HEADS = 8
HEAD_DIM = 64
N_PAIR = N_HEADS // 2
POOL_WINDOWS = (2, 4, 8, 16)
POOL_GROUP = 128
POOL_HALO = 16
LN_EPS = 1e-5
ALPHA = 2.0 ** 0.25
D_IN = 3 * D_ATT + N_HEADS + D_POOL + D_ATT + D_POOL
N_CHIPS = 4
SHARD_IN = D_IN // N_CHIPS
SHARD_ADA = 3 * D // N_CHIPS
SHARD_OUT = D // N_CHIPS

O_QKV, O_F, O_P, O_G, D_PAD = 0, 1536, 1664, 2176, 3200
Q_SCALE = HEAD_DIM ** -0.5

ADAM_LR, ADAM_B1, ADAM_B2, ADAM_EPS, ADAM_WD, ADAM_STEP = 0.001, 0.9, 0.999, 1e-08, 0.01, 10

NEG = -1e30

VMEM_LIMIT = 56 * 1024 * 1024

TM_PROJ = 512
T_ATT = 512
TM_MID = 256
TM_GW = 1024
TM_DU = 512

REL7 = [(0, 0, 1), (0, 1, 0), (0, 1, 1), (1, 0, 0), (1, 0, 1), (1, 1, 0), (1, 1, 1)]
REL3 = [(0, 1), (1, 0), (1, 1)]

SMALL_SEGS = {}
_row = 0
for _name, _n in (("b_in", D_IN), ("w_pool_mix", 65536), ("b_pool_mix", 512), ("pool_scale", 512),
                  ("b_out", 1024), ("ln_g", 1024), ("ln_b", 1024), ("loss", 1)):
    _rows = -(-_n // 1024) * 8
    SMALL_SEGS[_name] = (_row, _rows)
    _row += _rows
SMALL_ROWS = -(-_row // 16) * 16


def _params(**kw):
    return pltpu.CompilerParams(vmem_limit_bytes=VMEM_LIMIT, **kw)


def _flip(v, d):
    return v if d == 0 else 1 - v


def _dot(a, b):
    return jnp.dot(a, b, preferred_element_type=F32)


def _dot_nt(a, b):
    return lax.dot_general(a, b, (((1,), (1,)), ((), ())), preferred_element_type=F32)


def _dot_tn(a, b):
    return lax.dot_general(a, b, (((0,), (0,)), ((), ())), preferred_element_type=F32)


def _sigmoid(v):
    return 1.0 / (1.0 + jnp.exp(-v))


def _colsum(v):
    return jnp.sum(v, axis=0, keepdims=True)


def _gather_stages(pos, src_ref, dst_ref, half, own_sem, s_sem, r_sem, fs_sem, fr_sem):
    x, y, cc, chip, sib = pos
    own = pltpu.make_async_copy(src_ref, dst_ref.at[chip], own_sem)
    first, landed, others = [], [], []
    for k, (dx, dy) in enumerate(REL3):
        px, py = _flip(x, dx), _flip(y, dy)
        first.append(pltpu.make_async_remote_copy(
            src_ref=src_ref.at[half(cc)], dst_ref=dst_ref.at[(chip,) + half(cc)],
            send_sem=s_sem.at[k], recv_sem=r_sem.at[k], device_id=(px, py, cc), device_id_type=MESH))
        landed.append(dst_ref.at[(2 * px + py,) + half(cc)])
        others.append(dst_ref.at[(2 * px + py,) + half(1 - cc)])
    passed = [pltpu.make_async_remote_copy(src_ref=landed[k], dst_ref=landed[k], send_sem=fs_sem.at[k],
                                           recv_sem=fr_sem.at[k], device_id=sib, device_id_type=MESH)
              for k in range(3)]

    def start():
        own.start()
        for cp in first:
            cp.start()

    def forward():
        for k in range(3):
            pltpu.make_async_remote_copy(src_ref=landed[k], dst_ref=landed[k], send_sem=s_sem.at[k],
                                         recv_sem=r_sem.at[k], device_id=sib, device_id_type=MESH).wait_recv()
            passed[k].start()

    def finish():
        for k in range(3):
            pltpu.make_async_remote_copy(src_ref=others[k], dst_ref=others[k], send_sem=fs_sem.at[k],
                                         recv_sem=fr_sem.at[k], device_id=sib, device_id_type=MESH).wait_recv()
        for cp in first + passed:
            cp.wait_send()
        own.wait()

    return start, forward, finish


def _gather_scratch():
    return [pltpu.SemaphoreType.DMA, pltpu.SemaphoreType.DMA((3,)), pltpu.SemaphoreType.DMA((3,)),
            pltpu.SemaphoreType.DMA((3,)), pltpu.SemaphoreType.DMA((3,))]


def _gather_and_ada(c, w_ada, b_ada4, w_in_sh):
    def body(c_ref, w_ref, b_ref, win_ref, call_ref, ada_ref, wt_pad_ref,
             win_all, cslab, sbuf, rbuf, cs_sem, cr_sem, as_sem, ar_sem, *gather_sems):
        x, y, cc = lax.axis_index("x"), lax.axis_index("y"), lax.axis_index("c")
        me = 4 * x + 2 * y + cc
        chip = 2 * x + y
        lane_half = lambda which: (slice(None), pl.ds(pl.multiple_of(which * (D // 2), D // 2), D // 2))
        start, forward, finish = _gather_stages((x, y, cc, chip, (x, y, 1 - cc)), win_ref, win_all, lane_half,
                                                *gather_sems)
        start()

        cslab[...] = jnp.broadcast_to(c_ref[...], (8, D))
        call_ref[me] = cslab[...]
        gathers = []
        for k, (dx, dy, dc) in enumerate(REL7):
            cp = pltpu.make_async_remote_copy(
                src_ref=cslab, dst_ref=call_ref.at[me], send_sem=cs_sem.at[k], recv_sem=cr_sem.at[k],
                device_id=(_flip(x, dx), _flip(y, dy), _flip(cc, dc)), device_id_type=MESH)
            cp.start()
            gathers.append(cp)
        for cp in gathers:
            cp.wait()
        slab_row = lax.broadcasted_iota(jnp.int32, (8, 1), 0)
        mat = jnp.zeros((8, D), F32)
        for r in range(8):
            mat = jnp.where(slab_row == r, call_ref[r], mat)
        act = (mat * _sigmoid(mat)).astype(BF16)
        part = _dot(act, w_ref[...].astype(BF16))
        sends = []
        for k, (dx, dy) in enumerate(REL3):
            px, py = _flip(x, dx), _flip(y, dy)
            r = 4 * px + 2 * py + cc
            piece = _colsum(jnp.where(slab_row == r, part, 0.0))
            sbuf[k] = jnp.broadcast_to(piece, (8, SHARD_ADA))
            cp = pltpu.make_async_remote_copy(
                src_ref=sbuf.at[k], dst_ref=rbuf.at[k], send_sem=as_sem.at[k], recv_sem=ar_sem.at[k],
                device_id=(px, py, cc), device_id_type=MESH)
            cp.start()
            sends.append(cp)
        own_piece = _colsum(jnp.where(slab_row == me, part, 0.0))
        ada_ref[chip] = jnp.broadcast_to(own_piece, (8, SHARD_ADA)) + b_ref[chip]
        for k, (dx, dy) in enumerate(REL3):
            sends[k].wait()
            a = 2 * _flip(x, dx) + _flip(y, dy)
            ada_ref[a] = rbuf[k] + b_ref[a]

        forward()
        finish()
        n_real = 3 * D_ATT + N_HEADS
        for a in range(N_CHIPS):
            lo, hi = a * SHARD_IN, (a + 1) * SHARD_IN
            for s0, s1 in ((lo, min(hi, D_ATT)), (max(lo, D_ATT), min(hi, n_real)), (max(lo, n_real), hi)):
                if s0 < s1:
                    rows = win_all[a, s0 - lo:s1 - lo, :]
                    if s1 <= D_ATT:
                        rows = rows * jnp.asarray(Q_SCALE, BF16)
                    shift = O_P - n_real if s0 >= n_real else 0
                    wt_pad_ref[s0 + shift:s1 + shift, :] = rows
        wt_pad_ref[n_real:O_P, :] = jnp.zeros((O_P - n_real, D), BF16)

    vm = pl.BlockSpec(memory_space=pltpu.VMEM)
    return pl.pallas_call(
        body, name="gather_and_ada",
        out_shape=(jax.ShapeDtypeStruct((8, 8, D), F32), jax.ShapeDtypeStruct((4, 8, SHARD_ADA), F32),
                   jax.ShapeDtypeStruct((D_PAD, D), BF16)),
        in_specs=[vm] * 4, out_specs=(vm,) * 3,
        scratch_shapes=[pltpu.VMEM((N_CHIPS, SHARD_IN, D), BF16),
                        pltpu.VMEM((8, D), F32), pltpu.VMEM((3, 8, SHARD_ADA), F32),
                        pltpu.VMEM((3, 8, SHARD_ADA), F32),
                        pltpu.SemaphoreType.DMA((7,)), pltpu.SemaphoreType.DMA((7,)),
                        pltpu.SemaphoreType.DMA((3,)), pltpu.SemaphoreType.DMA((3,))] + _gather_scratch(),
        compiler_params=_params(),
    )(c, w_ada, b_ada4, w_in_sh)


def _shard_cols():
    cut, gap = O_F + N_HEADS, O_P - (O_F + N_HEADS)
    out = []
    for a in range(N_CHIPS):
        lo, hi = a * SHARD_IN, (a + 1) * SHARD_IN
        out.append(([(lo, min(hi, cut))] if lo < cut else []) + ([(max(lo, cut) + gap, hi + gap)] if hi > cut else []))
    return out


def _scatter_stages(pos, g_ref, sc_ref, out_ref, sib_buf, send_buf, ici_buf, sem1, sem2s, sem2r, sem3, part=(0, 1),
                    cols=None, own_buf=None):
    x, y, cc, chip, sib = pos
    q, n_parts = part
    RH = (g_ref.shape[1] if cols is None else g_ref.shape[0]) // 2 // n_parts
    mine = pl.ds(pl.multiple_of((cc * n_parts + q) * RH, RH), RH)
    theirs = pl.ds(pl.multiple_of(((1 - cc) * n_parts + q) * RH, RH), RH)
    cp1 = pltpu.make_async_remote_copy(
        src_ref=g_ref.at[:, theirs, :] if cols is None else g_ref.at[theirs, :], dst_ref=sib_buf,
        send_sem=sem1.at[0], recv_sem=sem1.at[1], device_id=sib, device_id_type=MESH)
    sends = []
    for k, (dx, dy) in enumerate(REL3):
        px, py = _flip(x, dx), _flip(y, dy)
        sends.append(pltpu.make_async_remote_copy(
            src_ref=send_buf.at[2 * px + py], dst_ref=ici_buf.at[chip],
            send_sem=sem2s.at[k], recv_sem=sem2r.at[k], device_id=(px, py, cc), device_id_type=MESH))
    cp3 = pltpu.make_async_remote_copy(
        src_ref=out_ref.at[mine, :], dst_ref=out_ref.at[mine, :], send_sem=sem3.at[0], recv_sem=sem3.at[1],
        device_id=sib, device_id_type=MESH)

    def finish1():
        cp1.wait()
        if cols is None:
            for a in range(N_CHIPS):
                both = g_ref[a, mine, :] + sib_buf[a]
                sib_buf[a] = both
                send_buf[a] = both.astype(BF16)
        else:
            both = g_ref[mine, :] + sib_buf[...]
            for a, pieces in enumerate(cols):
                at = 0
                for lo, hi in pieces:
                    own_buf[a, :, at:at + hi - lo] = both[:, lo:hi]
                    send_buf[a, :, at:at + hi - lo] = both[:, lo:hi].astype(BF16)
                    at += hi - lo

    def start2():
        for cp in sends:
            cp.start()
        ici_buf[chip] = send_buf[chip]

    def finish2():
        for cp in sends:
            cp.wait()
        own = (sib_buf if cols is None else own_buf)[chip]
        parts = [jnp.where(chip == a, own, ici_buf[a].astype(F32)) for a in range(N_CHIPS)]
        out_ref[mine, 0:own.shape[1]] = ((parts[0] + parts[1]) + (parts[2] + parts[3])) * sc_ref[chip]

    return [(cp1.start, finish1), (start2, finish2), (cp3.start, cp3.wait)]


def _all_reduce_stages(pos, g_ref, out_ref, sib_buf, ici_buf, sem1, sem2s, sem2r, sem3):
    x, y, cc, chip, sib = pos
    RH = g_ref.shape[0] // 2
    mine = pl.ds(pl.multiple_of(cc * RH, 8), RH)
    theirs = pl.ds(pl.multiple_of((1 - cc) * RH, 8), RH)
    cp1 = pltpu.make_async_remote_copy(
        src_ref=g_ref.at[theirs, :], dst_ref=sib_buf, send_sem=sem1.at[0], recv_sem=sem1.at[1],
        device_id=sib, device_id_type=MESH)
    sends = []
    for k, (dx, dy) in enumerate(REL3):
        px, py = _flip(x, dx), _flip(y, dy)
        sends.append(pltpu.make_async_remote_copy(
            src_ref=sib_buf, dst_ref=ici_buf.at[chip],
            send_sem=sem2s.at[k], recv_sem=sem2r.at[k], device_id=(px, py, cc), device_id_type=MESH))
    cp3 = pltpu.make_async_remote_copy(
        src_ref=out_ref.at[mine, :], dst_ref=out_ref.at[mine, :], send_sem=sem3.at[0], recv_sem=sem3.at[1],
        device_id=sib, device_id_type=MESH)

    def finish1():
        cp1.wait()
        sib_buf[...] = g_ref[mine, :] + sib_buf[...]

    def start2():
        for cp in sends:
            cp.start()
        ici_buf[chip] = sib_buf[...]

    def finish2():
        for cp in sends:
            cp.wait()
        out_ref[mine, :] = (ici_buf[0] + ici_buf[1]) + (ici_buf[2] + ici_buf[3])

    return [(cp1.start, finish1), (start2, finish2), (cp3.start, cp3.wait)]


def _stage_sems():
    return [pltpu.SemaphoreType.DMA((2,)), pltpu.SemaphoreType.DMA((3,)),
            pltpu.SemaphoreType.DMA((3,)), pltpu.SemaphoreType.DMA((2,))]


def _scatter_scratch(r, c):
    return [pltpu.VMEM((N_CHIPS, r // 2, c), F32), pltpu.VMEM((N_CHIPS, r // 2, c), BF16),
            pltpu.VMEM((N_CHIPS, r // 2, c), BF16)] + _stage_sems()


def _reduce_all(gw_pad, sc_in, small, dada):
    R = small.shape[0]
    W = dada.shape[1]
    r_in, p_in = gw_pad.shape
    c_in = SHARD_IN
    chunk = r_in // 4

    def chunk_scratch():
        return ([pltpu.VMEM((chunk, p_in), F32), pltpu.VMEM((N_CHIPS, chunk, c_in), BF16),
                 pltpu.VMEM((N_CHIPS, chunk, c_in), BF16)] + _stage_sems()
                + [pltpu.VMEM((N_CHIPS, chunk, c_in), F32)])

    n_in = len(chunk_scratch())

    c_wide = -(-c_in // 128) * 128

    def body(gin_ref, scin_ref, sm_ref, d_ref, ocols_ref, osm_ref, dall_ref, oin_ref, *scratch):
        x, y, cc = lax.axis_index("x"), lax.axis_index("y"), lax.axis_index("c")
        me = 4 * x + 2 * y + cc
        pos = (x, y, cc, 2 * x + y, (x, y, 1 - cc))
        oin_ref[:, c_in:c_wide] = jnp.zeros((r_in, c_wide - c_in), F32)
        dslab, ds_sem, dr_sem = scratch[0:3]
        a_bufs, b_bufs, sm_bufs = scratch[3:3 + n_in], scratch[3 + n_in:3 + 2 * n_in], scratch[3 + 2 * n_in:]
        dslab[...] = jnp.broadcast_to(d_ref[...], (8, W))
        dall_ref[me] = dslab[...]
        gathers = []
        for k, (dx, dy, dc) in enumerate(REL7):
            cp = pltpu.make_async_remote_copy(
                src_ref=dslab, dst_ref=dall_ref.at[me], send_sem=ds_sem.at[k], recv_sem=dr_sem.at[k],
                device_id=(_flip(x, dx), _flip(y, dy), _flip(cc, dc)), device_id_type=MESH)
            cp.start()
            gathers.append(cp)
        cols = _shard_cols()
        first = _scatter_stages(pos, gin_ref, scin_ref, oin_ref, *a_bufs[:-1], part=(0, 2), cols=cols,
                                own_buf=a_bufs[-1])
        second = _scatter_stages(pos, gin_ref, scin_ref, oin_ref, *b_bufs[:-1], part=(1, 2), cols=cols,
                                 own_buf=b_bufs[-1])
        little = _all_reduce_stages(pos, sm_ref, osm_ref, *sm_bufs)
        for plan in (first, second, little):
            plan[0][0]()
        first[0][1]()
        first[1][0]()
        little[0][1]()
        little[1][0]()
        second[0][1]()
        second[1][0]()
        first[1][1]()
        first[2][0]()
        second[1][1]()
        second[2][0]()
        little[1][1]()
        little[2][0]()
        for plan in (first, second, little):
            plan[2][1]()
        ocols_ref[...] = oin_ref[...].T[0:c_in, :][:, None, :]
        for cp in gathers:
            cp.wait()

    scratch = [pltpu.VMEM((r_in, c_wide), F32),
               pltpu.VMEM((8, W), F32), pltpu.SemaphoreType.DMA((7,)), pltpu.SemaphoreType.DMA((7,))]
    scratch += chunk_scratch() + chunk_scratch()
    scratch += [pltpu.VMEM((R // 2, 128), F32), pltpu.VMEM((N_CHIPS, R // 2, 128), F32)] + _stage_sems()
    vm = pl.BlockSpec(memory_space=pltpu.VMEM)
    return pl.pallas_call(
        body, name="reduce_all",
        out_shape=(jax.ShapeDtypeStruct((c_in, 1, r_in), F32),
                   jax.ShapeDtypeStruct((R, 128), F32), jax.ShapeDtypeStruct((8, 8, W), F32)),
        in_specs=[vm] * 4, out_specs=(vm,) * 3,
        scratch_shapes=scratch,
        compiler_params=_params(),
    )(gw_pad, sc_in, small, dada)


def _in_proj(x, shift, scale, wt_pad, b_pad, w_out_sh):
    S = x.shape[0]
    tm = min(TM_PROJ, S)
    n_steps = S // tm
    assert n_steps >= 3

    def body(x_ref, sh_ref, sc_ref, w_ref, b_ref, wo_ref, u_ref, qkv_ref, f_ref, p_ref, g_ref, wo_all,
             wo_buf, *gather_sems):
        i = pl.program_id(0)
        xx, yy, cc = lax.axis_index("x"), lax.axis_index("y"), lax.axis_index("c")
        row_half = lambda which: (pl.ds(pl.multiple_of(which * (SHARD_OUT // 2), SHARD_OUT // 2), SHARD_OUT // 2),
                                  slice(None))
        start, forward, finish = _gather_stages((xx, yy, cc, 2 * xx + yy, (xx, yy, 1 - cc)), wo_ref, wo_buf,
                                                row_half, *gather_sems)
        pl.when(i == 0)(start)
        pl.when(i == n_steps // 2)(forward)

        @pl.when(i == n_steps - 1)
        def _():
            finish()
            wo_all[...] = wo_buf[...]

        u = (x_ref[...] * (1.0 + sc_ref[...]) + sh_ref[...]).astype(BF16)
        u_ref[...] = u
        qkv_ref[...] = (_dot_nt(u, w_ref[O_QKV:O_F, :]) + b_ref[:, O_QKV:O_F]).astype(BF16)
        f_ref[...] = _dot_nt(u, w_ref[O_F:O_P, :]) + b_ref[:, O_F:O_P]
        p_ref[...] = _dot_nt(u, w_ref[O_P:O_G, :]) + b_ref[:, O_P:O_G]
        g_ref[...] = _dot_nt(u, w_ref[O_G:D_PAD, :]) + b_ref[:, O_G:D_PAD]

    row = lambda w: pl.BlockSpec((tm, w), lambda i: (i, 0))
    full = lambda a: pl.BlockSpec(a.shape, lambda i: (0, 0))
    vm = pl.BlockSpec(memory_space=pltpu.VMEM)
    return pl.pallas_call(
        body, name="in_proj", grid=(n_steps,),
        out_shape=(jax.ShapeDtypeStruct((S, D), BF16), jax.ShapeDtypeStruct((S, 3 * D_ATT), BF16),
                   jax.ShapeDtypeStruct((S, 128), F32), jax.ShapeDtypeStruct((S, D_POOL), F32),
                   jax.ShapeDtypeStruct((S, D), F32), jax.ShapeDtypeStruct((N_CHIPS,) + w_out_sh.shape, BF16)),
        in_specs=[row(D), full(shift), full(scale), full(wt_pad), full(b_pad), vm],
        out_specs=(row(D), row(3 * D_ATT), row(128), row(D_POOL), row(D), vm),
        scratch_shapes=[pltpu.VMEM((N_CHIPS,) + w_out_sh.shape, BF16)] + _gather_scratch(),
        compiler_params=_params(dimension_semantics=("arbitrary",)),
    )(x, shift, scale, wt_pad, b_pad, w_out_sh)


def _forget_cumsum(f):
    S = f.shape[0]
    tm = min(T_ATT, S)

    def body(f_ref, out_ref, carry):
        @pl.when(pl.program_id(0) == 0)
        def _():
            carry[...] = jnp.zeros_like(carry)
        v = f_ref[...]
        logf = jnp.minimum(v, 0.0) - jnp.log(1.0 + jnp.exp(-jnp.abs(v)))
        r = lax.broadcasted_iota(jnp.int32, (tm, tm), 0)
        c = lax.broadcasted_iota(jnp.int32, (tm, tm), 1)
        tri = (r <= c).astype(F32)
        rows8 = logf.T[0:8, :]
        cum8 = jnp.dot(rows8, tri, preferred_element_type=F32, precision=lax.Precision.HIGHEST) + carry[...]
        out_ref[...] = jnp.concatenate([cum8, jnp.zeros((128 - 8, tm), F32)], axis=0).T
        last = lax.broadcasted_iota(jnp.int32, (1, tm), 1) == tm - 1
        carry[...] = jnp.sum(jnp.where(last, cum8, 0.0), axis=1, keepdims=True)

    return pl.pallas_call(
        body, name="forget_cumsum", grid=(S // tm,),
        out_shape=jax.ShapeDtypeStruct((S, 128), F32),
        in_specs=[pl.BlockSpec((tm, 128), lambda i: (i, 0))],
        out_specs=pl.BlockSpec((tm, 128), lambda i: (i, 0)),
        scratch_shapes=[pltpu.VMEM((8, 1), F32)],
        compiler_params=_params(dimension_semantics=("arbitrary",)),
    )(f)


def _split3(v):
    hi = v.astype(BF16)
    rest = v - hi.astype(F32)
    mid = rest.astype(BF16)
    lo = (rest - mid.astype(F32)).astype(BF16)
    return hi, mid, lo


def _attention_fwd(qkv, big_f):
    S = qkv.shape[0]
    T = min(T_ATT, S)
    n_t = S // T

    def body(q_ref, k_ref, v_ref, f_ref, o_ref, lse_ref, kaug_sc, vt_sc, m_sc, l_sc, acc_sc):
        hp = pl.program_id(0)
        i = pl.program_id(1)
        lane = lax.broadcasted_iota(jnp.int32, (1, 128), 1)
        sub = lax.broadcasted_iota(jnp.int32, (128, 1), 0)
        head_sel = (lane < HEAD_DIM, lane >= HEAD_DIM)
        head_sel_t = (sub < HEAD_DIM, sub >= HEAD_DIM)
        spare = (HEAD_DIM, 0)
        zero = jnp.zeros((), BF16)

        @pl.when(i == 0)
        def _():
            def prep(jt, carry):
                rows = pl.ds(pl.multiple_of(jt * T, T), T)
                k = k_ref[rows, :]
                ft = f_ref[rows, :]
                vt = v_ref[rows, :].astype(F32).T
                for h in range(2):
                    fh = jnp.sum(jnp.where(lane == 2 * hp + h, ft, 0.0), axis=1, keepdims=True)
                    hi, mid, lo = _split3(-fh)
                    b = spare[h]
                    bias = jnp.where(lane == b, hi, jnp.where(lane == b + 1, mid, jnp.where(lane == b + 2, lo, zero)))
                    kaug_sc[h, rows, :] = jnp.where(head_sel[h], k, bias)
                    vt_sc[h, jt] = jnp.where(head_sel_t[h], vt, 0.0).astype(BF16)
                return carry

            lax.fori_loop(0, n_t, prep, 0)

        q = q_ref[...]
        q_heads = []
        for h in range(2):
            ones = jnp.where((lane >= spare[h]) & (lane < spare[h] + 3), jnp.ones((), BF16), zero)
            q_heads.append(jnp.where(head_sel[h], q, ones))
        m_sc[...] = jnp.full((8, T), NEG, F32)
        l_sc[...] = jnp.zeros((8, T), F32)
        acc_sc[...] = jnp.zeros((128, T), F32)

        def update(j, k_lo, n_k, q_lo, masked):
            rows = pl.ds(pl.multiple_of(j * T + k_lo, n_k), n_k)
            n_q = T - q_lo
            alphas, pvs = [], []
            for h in range(2):
                s_t = _dot_nt(kaug_sc[h, rows, :], q_heads[h][q_lo:, :])
                if masked:
                    rr = lax.broadcasted_iota(jnp.int32, (n_k, n_q), 0) + k_lo
                    cc = lax.broadcasted_iota(jnp.int32, (n_k, n_q), 1) + q_lo
                    s_t = jnp.where(rr <= cc, s_t, NEG)
                m_prev = m_sc[h:h + 1, q_lo:]
                m_new = jnp.maximum(m_prev, jnp.max(s_t, axis=0, keepdims=True))
                alpha = jnp.exp(m_prev - m_new)
                p_t = jnp.exp(s_t - m_new)
                l_sc[h:h + 1, q_lo:] = alpha * l_sc[h:h + 1, q_lo:] + jnp.sum(p_t, axis=0, keepdims=True)
                m_sc[h:h + 1, q_lo:] = m_new
                alphas.append(alpha)
                pvs.append(_dot(vt_sc[h, j, :, k_lo:k_lo + n_k], p_t.astype(BF16)))
            acc_sc[:, q_lo:] = (acc_sc[:, q_lo:] * jnp.where(head_sel_t[0], alphas[0], alphas[1])
                                + (pvs[0] + pvs[1]))

        def two_off_diagonal(jj, carry):
            update(2 * jj, 0, T, 0, False)
            update(2 * jj + 1, 0, T, 0, False)
            return carry

        lax.fori_loop(0, i // 2, two_off_diagonal, 0)

        @pl.when(i % 2 == 1)
        def _():
            update(i - 1, 0, T, 0, False)

        update(i, 0, T, 0, True)
        l = l_sc[...]
        o_ref[...] = (acc_sc[...] / jnp.where(head_sel_t[0], l[0:1, :], l[1:2, :])).T
        is_head = lax.broadcasted_iota(jnp.int32, (8, 1), 0) < 2
        lse_ref[...] = jnp.where(is_head, m_sc[...] + jnp.log(jnp.where(is_head, l, 1.0)), 0.0)

    return pl.pallas_call(
        body, name="attention_fwd", grid=(N_PAIR, n_t),
        out_shape=(jax.ShapeDtypeStruct((S, D_ATT), F32), jax.ShapeDtypeStruct((N_PAIR, n_t, 8, T), F32)),
        in_specs=[pl.BlockSpec((T, 128), lambda hp, i: (i, hp)),
                  pl.BlockSpec((S, 128), lambda hp, i: (0, N_PAIR + hp)),
                  pl.BlockSpec((S, 128), lambda hp, i: (0, 2 * N_PAIR + hp)),
                  pl.BlockSpec((S, 128), lambda hp, i: (0, 0))],
        out_specs=(pl.BlockSpec((T, 128), lambda hp, i: (i, hp)),
                   pl.BlockSpec((None, None, 8, T), lambda hp, i: (hp, i, 0, 0))),
        scratch_shapes=[pltpu.VMEM((2, S, 128), BF16), pltpu.VMEM((2, n_t, 128, T), BF16),
                        pltpu.VMEM((8, T), F32), pltpu.VMEM((8, T), F32), pltpu.VMEM((128, T), F32)],
        compiler_params=_params(dimension_semantics=("arbitrary", "arbitrary")),
    )(qkv, qkv, qkv, big_f)


def _attention_bwd(qkv, datt, att, lse, big_f, gw_out4, sc_out):
    S = qkv.shape[0]
    T = min(T_ATT, S)
    n_t = S // T
    n_steps = N_PAIR * n_t
    marks = (0, n_steps // 8, n_steps // 2, n_steps // 2 + n_steps // 8)

    def body(q_ref, do_ref, o_ref, lse_ref, k_ref, v_ref, fk_ref, gout_ref, scout_ref,
             dq_ref, dk_ref, dv_ref, cs_ref, dfk_ref, dfq_ref, oout_ref, stat_sc, dqt_sc, qaug_sc,
             out_buf, *red_bufs):
        hp = pl.program_id(0)
        j = pl.program_id(1)
        x, y, cc = lax.axis_index("x"), lax.axis_index("y"), lax.axis_index("c")
        plan = _scatter_stages((x, y, cc, 2 * x + y, (x, y, 1 - cc)), gout_ref, scout_ref, out_buf, *red_bufs)
        step = hp * n_t + j
        for n, mark in enumerate(marks):
            @pl.when(step == mark)
            def _(n=n):
                if n > 0:
                    plan[n - 1][1]()
                if n < 3:
                    plan[n][0]()
                else:
                    oout_ref[...] = out_buf[...]

        lane = lax.broadcasted_iota(jnp.int32, (1, 128), 1)
        sub = lax.broadcasted_iota(jnp.int32, (128, 1), 0)
        head_sel = (lane < HEAD_DIM, lane >= HEAD_DIM)
        head_sel_t = (sub < HEAD_DIM, sub >= HEAD_DIM)
        spare = (HEAD_DIM, 0)
        zero = jnp.zeros((), BF16)
        one = jnp.ones((), BF16)

        def bias_lanes(first, pieces):
            hi, mid, lo = pieces
            return lambda rest: jnp.where(lane == first, hi, jnp.where(lane == first + 1, mid,
                                                                        jnp.where(lane == first + 2, lo, rest)))

        @pl.when(j == 0)
        def _():
            dqt_sc[...] = jnp.zeros_like(dqt_sc)
            cs_ref[...] = jnp.zeros_like(cs_ref)
            dfq_ref[...] = jnp.zeros_like(dfq_ref)

            def prep(i, carry):
                rows = pl.ds(pl.multiple_of(i * T, T), T)
                q = q_ref[rows, :]
                do = do_ref[rows, :]
                prod = o_ref[rows, :] * do.astype(F32)
                d_a = jnp.sum(jnp.where(head_sel[0], prod, 0.0), axis=1, keepdims=True)
                d_b = jnp.sum(jnp.where(head_sel[0], 0.0, prod), axis=1, keepdims=True)
                delta_t = jnp.where(head_sel[0], d_a, d_b).T
                stat_sc[i, 0:1, :] = delta_t[0:1, :]
                stat_sc[i, 1:2, :] = delta_t[HEAD_DIM:HEAD_DIM + 1, :]
                lse = lse_ref[i]
                lse_cols = jnp.where(head_sel_t[0], lse[0:1, :], lse[1:2, :]).T
                for h in range(2):
                    neg_lse = -lse_cols[:, h * HEAD_DIM:h * HEAD_DIM + 1]
                    ones = jnp.where((lane >= spare[h]) & (lane < spare[h] + 3), one, zero)
                    qaug_sc[h, rows, :] = jnp.where(head_sel[h], q, bias_lanes(spare[h] + 3, _split3(neg_lse))(ones))
                return carry

            lax.fori_loop(0, n_t, prep, 0)

        k = k_ref[...]
        v = v_ref[...]
        fk = fk_ref[...]
        kt = k.astype(F32).T
        heads = []
        for h in range(2):
            fkh = jnp.sum(jnp.where(lane == 2 * hp + h, fk, 0.0), axis=1, keepdims=True)
            ones = jnp.where((lane >= spare[h] + 3) & (lane < spare[h] + 6), one, zero)
            kaug = jnp.where(head_sel[h], k, bias_lanes(spare[h], _split3(-fkh))(ones))
            heads.append((kaug, jnp.where(head_sel[h], v, zero), jnp.where(head_sel_t[h], kt, 0.0).astype(BF16)))

        def block(i, k_lo, n_k, q_lo, masked):
            n_q = T - q_lo
            rows = pl.ds(pl.multiple_of(i * T + q_lo, n_q), n_q)
            q = q_ref[rows, :]
            do = do_ref[rows, :]
            stat = stat_sc[i]
            dk = jnp.zeros((n_k, 128), F32)
            dv = jnp.zeros((n_k, 128), F32)
            dqt = jnp.zeros((128, n_q), F32)
            dfs = []
            for h in range(2):
                kaug, vh, kth = heads[h]
                arg = _dot_nt(kaug[k_lo:k_lo + n_k, :], qaug_sc[h, rows, :])
                if masked:
                    rr = lax.broadcasted_iota(jnp.int32, (n_k, n_q), 0) + k_lo
                    cc = lax.broadcasted_iota(jnp.int32, (n_k, n_q), 1) + q_lo
                    arg = jnp.where(rr <= cc, arg, NEG)
                p_t = jnp.exp(arg)
                ds_t = p_t * (_dot_nt(vh[k_lo:k_lo + n_k, :], do) - stat[h:h + 1, q_lo:])
                ds_bf = ds_t.astype(BF16)
                dv = dv + _dot(p_t.astype(BF16), jnp.where(head_sel[h], do, zero))
                dk = dk + _dot(ds_bf, jnp.where(head_sel[h], q, zero))
                dqt = dqt + _dot(kth[:, k_lo:k_lo + n_k], ds_bf)
                dfs.append(jnp.sum(ds_t, axis=1, keepdims=True))
                dfq_ref[i, h:h + 1, q_lo:] += _colsum(ds_t)
            dqt_sc[i, :, q_lo:] += dqt
            return dk, dv, dfs[0], dfs[1]

        def off_diagonal(i, acc):
            return tuple(a + b for a, b in zip(acc, block(i, 0, T, 0, False)))

        half = T // 2
        early = block(j, 0, half, 0, True)
        late = block(j, half, half, half, True)
        acc1 = tuple(jnp.concatenate([a, b], axis=0) for a, b in zip(early, late))
        n_off = n_t - 1 - j
        acc2 = lax.fori_loop(0, n_off // 2,
                             lambda ii, a: off_diagonal(j + 2 + 2 * ii, off_diagonal(j + 1 + 2 * ii, a)), acc1)
        dk_acc, dv_acc, dfa, dfb = lax.fori_loop(0, n_off % 2, lambda _, a: off_diagonal(n_t - 1, a), acc2)
        dk_ref[...] = dk_acc.astype(BF16)
        dv_ref[...] = dv_acc.astype(BF16)
        dfk_ref[...] = -jnp.where(lane == 0, dfa, jnp.where(lane == 1, dfb, 0.0))
        cs_ref[:, 128:256] = cs_ref[:, 128:256] + _colsum(dk_acc)
        cs_ref[:, 256:384] = cs_ref[:, 256:384] + _colsum(dv_acc)

        @pl.when(j == n_t - 1)
        def _():
            def finish(i, tot):
                dq = dqt_sc[i].T
                dq_ref[pl.ds(pl.multiple_of(i * T, T), T), :] = dq.astype(BF16)
                return tot + _colsum(dq)

            cs_ref[:, 0:128] = lax.fori_loop(0, n_t, finish, jnp.zeros((1, 128), F32))

    pair_rows = lambda hp, j: (hp, 0, 0)
    vm = pl.BlockSpec(memory_space=pltpu.VMEM)
    _, r_out, c_out = gw_out4.shape
    return pl.pallas_call(
        body, name="attention_bwd", grid=(N_PAIR, n_t),
        out_shape=(jax.ShapeDtypeStruct((S, D_ATT), BF16), jax.ShapeDtypeStruct((S, D_ATT), BF16),
                   jax.ShapeDtypeStruct((S, D_ATT), BF16), jax.ShapeDtypeStruct((N_PAIR, 1, 384), F32),
                   jax.ShapeDtypeStruct((N_PAIR, S, 128), F32),
                   jax.ShapeDtypeStruct((N_PAIR, n_t, 8, T), F32),
                   jax.ShapeDtypeStruct((r_out, c_out), F32)),
        in_specs=[pl.BlockSpec((S, 128), lambda hp, j: (0, hp)),
                  pl.BlockSpec((S, 128), lambda hp, j: (0, hp)),
                  pl.BlockSpec((S, 128), lambda hp, j: (0, hp)),
                  pl.BlockSpec((None, n_t, 8, T), lambda hp, j: (hp, 0, 0, 0)),
                  pl.BlockSpec((T, 128), lambda hp, j: (j, N_PAIR + hp)),
                  pl.BlockSpec((T, 128), lambda hp, j: (j, 2 * N_PAIR + hp)),
                  pl.BlockSpec((T, 128), lambda hp, j: (j, 0)),
                  vm, vm],
        out_specs=(pl.BlockSpec((S, 128), lambda hp, j: (0, hp)),
                   pl.BlockSpec((T, 128), lambda hp, j: (j, hp)),
                   pl.BlockSpec((T, 128), lambda hp, j: (j, hp)),
                   pl.BlockSpec((None, 1, 384), pair_rows),
                   pl.BlockSpec((None, T, 128), lambda hp, j: (hp, j, 0)),
                   pl.BlockSpec((None, n_t, 8, T), lambda hp, j: (hp, 0, 0, 0)),
                   vm),
        scratch_shapes=[pltpu.VMEM((n_t, 8, T), F32), pltpu.VMEM((n_t, 128, T), F32),
                        pltpu.VMEM((2, S, 128), BF16), pltpu.VMEM((r_out, c_out), F32)]
        + _scatter_scratch(r_out, c_out),
        compiler_params=_params(dimension_semantics=("arbitrary", "arbitrary")),
    )(qkv, datt, att, lse, qkv, qkv, big_f, gw_out4, sc_out)


def _window_counts(first_row, n_rows, window):
    t = lax.broadcasted_iota(jnp.int32, (n_rows, 1), 0) + first_row
    return jnp.minimum((t + 1).astype(F32), float(window))


def _middle(x, tgt, att, g, p, gate, w_mix, b_mix, pool_scale, w_out, b_out, ln_g, ln_b):
    S = x.shape[0]
    tm = min(TM_MID, S)
    halo_blocks = tm // POOL_HALO

    def body(x_ref, t_ref, att_ref, g_ref, p_ref, ph_ref, gate_ref, wm_ref, bm_ref, ps_ref, wo_ref, bo_ref,
             lg_ref, lb_ref,
             dh_ref, datt_ref, dg_ref, dpl_ref, gwo_ref, gwm_ref, vec_ref, loss_ref):
        i = pl.program_id(0)

        @pl.when(i == 0)
        def _():
            gwo_ref[...] = jnp.zeros_like(gwo_ref)
            gwm_ref[...] = jnp.zeros_like(gwm_ref)
            vec_ref[...] = jnp.zeros_like(vec_ref)
            loss_ref[...] = jnp.zeros_like(loss_ref)

        pc = p_ref[...]
        halo = jnp.where(i > 0, ph_ref[...], 0.0)
        pe = jnp.concatenate([halo, pc], axis=0)
        pooled_parts = []
        for gi, w in enumerate(POOL_WINDOWS):
            cur = pe[:, gi * POOL_GROUP:(gi + 1) * POOL_GROUP]
            span = 1
            while span < w:
                cur = cur + pltpu.roll(cur, span, 0)
                span *= 2
            wsum = cur[POOL_HALO:, :]
            mean = wsum / _window_counts(i * tm, tm, w)
            pooled_parts.append(mean - pc[:, gi * POOL_GROUP:(gi + 1) * POOL_GROUP])
        pooled_bf =[v.astype(BF16) for v in pooled_parts]
        mixed = jnp.concatenate([_dot(pooled_bf[gi], wm_ref[gi]) for gi in range(4)], axis=1) + bm_ref[...]
        ps = ps_ref[...]
        pool_out = mixed * ps
        gv = g_ref[...]
        sig = _sigmoid(gv)
        silu = gv * sig
        att = att_ref[...]
        y = jnp.concatenate([att * silu[:, :D_ATT], pool_out * silu[:, D_ATT:]], axis=1)
        y_bf = y.astype(BF16)
        wo = wo_ref[...]
        yo = _dot(y_bf, wo) + bo_ref[...]
        gate = gate_ref[...]
        h = ALPHA * x_ref[...] + gate * yo
        mu = jnp.mean(h, axis=1, keepdims=True)
        hc = h - mu
        var = jnp.mean(hc * hc, axis=1, keepdims=True)
        rstd = lax.rsqrt(var + LN_EPS)
        yhat = hc * rstd
        lg = lg_ref[...]
        out = yhat * lg + lb_ref[...]
        err = out - t_ref[...]
        loss_ref[...] += 0.5 * jnp.sum(jnp.mean(err * err, axis=1, keepdims=True), axis=0, keepdims=True)

        dout = err * (1.0 / D)
        g_ln_b = _colsum(dout)
        g_ln_g = _colsum(dout * yhat)
        dyh = dout * lg
        dh = rstd * (dyh - jnp.mean(dyh, axis=1, keepdims=True)
                     - yhat * jnp.mean(dyh * yhat, axis=1, keepdims=True))
        dh_ref[...] = dh
        d_gate = _colsum(dh * yo)
        dyo = gate * dh
        g_b_out = _colsum(dyo)
        dyo_bf = dyo.astype(BF16)
        gwo_ref[...] += _dot_tn(y_bf, dyo_bf)
        dy = _dot_nt(dyo_bf, wo)
        dsilu = sig * (1.0 + gv * (1.0 - sig))
        dy_a = dy[:, :D_ATT]
        dy_p = dy[:, D_ATT:]
        datt_ref[...] = (dy_a * silu[:, :D_ATT]).astype(BF16)
        dpo = dy_p * silu[:, D_ATT:]
        dg = jnp.concatenate([dy_a * att * dsilu[:, :D_ATT], dy_p * pool_out * dsilu[:, D_ATT:]], axis=1)
        dg_ref[...] = dg.astype(BF16)
        g_dg = _colsum(dg)
        g_ps = _colsum(dpo * mixed)
        dmixed = dpo * ps
        g_bm = _colsum(dmixed)
        dmixed_bf = dmixed.astype(BF16)
        dpl = []
        for gi in range(4):
            dm = dmixed_bf[:, gi * POOL_GROUP:(gi + 1) * POOL_GROUP]
            gwm_ref[gi] += _dot_tn(pooled_bf[gi], dm)
            dpl.append(_dot_nt(dm, wm_ref[gi]))
        dpl_ref[...] = jnp.concatenate(dpl, axis=1)
        vec_ref[0:1, :] += g_ln_g
        vec_ref[1:2, :] += g_ln_b
        vec_ref[2:3, :] += d_gate
        vec_ref[3:4, :] += g_b_out
        vec_ref[4:5, :] += g_dg
        vec_ref[5:6, 0:D_POOL] += g_ps
        vec_ref[6:7, 0:D_POOL] += g_bm

    row = lambda w: pl.BlockSpec((tm, w), lambda i: (i, 0))
    full2 = lambda a: pl.BlockSpec(a.shape, lambda i: (0, 0))
    full3 = lambda a: pl.BlockSpec(a.shape, lambda i: (0, 0, 0))
    return pl.pallas_call(
        body, name="middle", grid=(S // tm,),
        out_shape=(jax.ShapeDtypeStruct((S, D), F32),
                   jax.ShapeDtypeStruct((S, D_ATT), BF16),
                   jax.ShapeDtypeStruct((S, D), BF16),
                   jax.ShapeDtypeStruct((S, D_POOL), F32),
                   jax.ShapeDtypeStruct((D, D), F32),
                   jax.ShapeDtypeStruct((4, POOL_GROUP, POOL_GROUP), F32),
                   jax.ShapeDtypeStruct((8, D), F32),
                   jax.ShapeDtypeStruct((1, 1), F32)),
        in_specs=[row(D), row(D), row(D_ATT), row(D), row(D_POOL),
                  pl.BlockSpec((POOL_HALO, D_POOL), lambda i: (jnp.maximum(i * halo_blocks - 1, 0), 0)),
                  full2(gate), full3(w_mix), full2(b_mix), full2(pool_scale), full2(w_out), full2(b_out),
                  full2(ln_g), full2(ln_b)],
        out_specs=(row(D), row(D_ATT), row(D), row(D_POOL),
                   pl.BlockSpec((D, D), lambda i: (0, 0)),
                   pl.BlockSpec((4, POOL_GROUP, POOL_GROUP), lambda i: (0, 0, 0)),
                   pl.BlockSpec((8, D), lambda i: (0, 0)),
                   pl.BlockSpec((1, 1), lambda i: (0, 0))),
        compiler_params=_params(dimension_semantics=("arbitrary",)),
    )(x, tgt, att, g, p, p, gate, w_mix, b_mix, pool_scale, w_out, b_out, ln_g, ln_b)


def _tail(dpl, dfk, dfq, f):
    S = dpl.shape[0]
    tm = min(T_ATT, S)
    n_t = S // tm
    halo_blocks = tm // POOL_HALO
    last_halo = S // POOL_HALO - 1

    def body(d_ref, dn_ref, dfk_ref, dfq_ref, f_ref, dp_ref, df_ref, cs_ref, carry):
        s = pl.program_id(0)
        i = n_t - 1 - s

        @pl.when(s == 0)
        def _():
            carry[...] = jnp.zeros_like(carry)
            cs_ref[...] = jnp.zeros_like(cs_ref)

        dc = d_ref[...]
        nxt = jnp.where(s > 0, dn_ref[...], 0.0)
        de = jnp.concatenate([dc, nxt], axis=0)
        n_e = tm + POOL_HALO
        parts = []
        for gi, w in enumerate(POOL_WINDOWS):
            cur = de[:, gi * POOL_GROUP:(gi + 1) * POOL_GROUP] / _window_counts(i * tm, n_e, w)
            span = 1
            while span < w:
                cur = cur + pltpu.roll(cur, n_e - span, 0)
                span *= 2
            parts.append(cur[:tm, :] - dc[:, gi * POOL_GROUP:(gi + 1) * POOL_GROUP])
        dp = jnp.concatenate(parts, axis=1)
        dp_ref[...] = dp.astype(BF16)
        cs_ref[0:1, :] += _colsum(dp)

        r = lax.broadcasted_iota(jnp.int32, (tm, tm), 0)
        c = lax.broadcasted_iota(jnp.int32, (tm, tm), 1)
        tri = (r >= c).astype(F32)
        k_cols = dfk_ref[0]
        rows8 = dfq_ref[0]
        for hp in range(1, N_PAIR):
            k_cols = k_cols + pltpu.roll(dfk_ref[hp], 2 * hp, 1)
            rows8 = rows8 + pltpu.roll(dfq_ref[hp], 2 * hp, 0)
        rows8 = rows8 + k_cols.T[0:8, :]
        dlogf8 = jnp.dot(rows8, tri, preferred_element_type=F32, precision=lax.Precision.HIGHEST) + carry[...]
        first = lax.broadcasted_iota(jnp.int32, (1, tm), 1) == 0
        carry[...] = jnp.sum(jnp.where(first, dlogf8, 0.0), axis=1, keepdims=True)
        dlogf = jnp.concatenate([dlogf8, jnp.zeros((128 - 8, tm), F32)], axis=0).T
        df = dlogf * _sigmoid(-f_ref[...])
        df_ref[...] = df.astype(BF16)
        cs_ref[1:2, 0:128] += _colsum(df)

    rev = lambda w: pl.BlockSpec((tm, w), lambda s: (n_t - 1 - s, 0))
    return pl.pallas_call(
        body, name="tail", grid=(n_t,),
        out_shape=(jax.ShapeDtypeStruct((S, D_POOL), BF16), jax.ShapeDtypeStruct((S, 128), BF16),
                   jax.ShapeDtypeStruct((8, D_POOL), F32)),
        in_specs=[rev(D_POOL),
                  pl.BlockSpec((POOL_HALO, D_POOL),
                               lambda s: (jnp.minimum((n_t - s) * halo_blocks, last_halo), 0)),
                  pl.BlockSpec((N_PAIR, tm, 128), lambda s: (0, n_t - 1 - s, 0)),
                  pl.BlockSpec((N_PAIR, None, 8, tm), lambda s: (0, n_t - 1 - s, 0, 0)),
                  rev(128)],
        out_specs=(rev(D_POOL), rev(128), pl.BlockSpec((8, D_POOL), lambda s: (0, 0))),
        scratch_shapes=[pltpu.VMEM((8, 1), F32)],
        compiler_params=_params(dimension_semantics=("arbitrary",)),
    )(dpl, dpl, dfk, dfq, f)


PIECES = ((O_QKV, D_ATT), (O_QKV + D_ATT, D_ATT), (O_QKV + 2 * D_ATT, D_ATT), (O_F, 128), (O_P, D_POOL), (O_G, D))


def _grad_w_in(u, pieces):
    S = u.shape[0]
    tm = min(TM_GW, S)
    n_t = S // tm

    def body(u_ref, *rest):
        piece_refs, out_ref, acc, sem = rest[:6], rest[6], rest[7], rest[8]
        i = pl.program_id(0)

        @pl.when(i == 0)
        def _():
            acc[...] = jnp.zeros_like(acc)

        u_t = u_ref[...]
        for (off, w), ref in zip(PIECES, piece_refs):
            acc[:, off:off + w] += _dot_tn(u_t, ref[...])

        @pl.when(i == n_t - 1)
        def _():
            cp = pltpu.make_async_copy(acc, out_ref, sem)
            cp.start()
            cp.wait()

    return pl.pallas_call(
        body, name="grad_w_in", grid=(n_t,),
        out_shape=jax.ShapeDtypeStruct((D, D_PAD), F32),
        in_specs=[pl.BlockSpec((tm, D), lambda i: (i, 0))]
        + [pl.BlockSpec((tm, w), lambda i: (i, 0)) for _, w in PIECES],
        out_specs=pl.BlockSpec(memory_space=pl.ANY),
        scratch_shapes=[pltpu.VMEM((D, D_PAD), F32), pltpu.SemaphoreType.DMA],
        compiler_params=_params(dimension_semantics=("arbitrary",)),
    )(u, *pieces)


def _grad_x(pieces, wt_pad, dh, x, scale):
    S = x.shape[0]
    tm = min(TM_DU, S)

    def body(*refs):
        piece_refs = refs[:6]
        w_ref, dh_ref, x_ref, sc_ref, gx_ref, vec_ref = refs[6:]

        @pl.when(pl.program_id(0) == 0)
        def _():
            vec_ref[...] = jnp.zeros_like(vec_ref)

        du = jnp.zeros((tm, D), F32)
        for (off, w), ref in zip(PIECES, piece_refs):
            du = du + _dot(ref[...], w_ref[off:off + w, :])
        xv = x_ref[...]
        gx_ref[...] = ALPHA * dh_ref[...] + du * (1.0 + sc_ref[...])
        vec_ref[0:1, :] += _colsum(du)
        vec_ref[1:2, :] += _colsum(du * xv)

    row = lambda w: pl.BlockSpec((tm, w), lambda i: (i, 0))
    return pl.pallas_call(
        body, name="grad_x", grid=(S // tm,),
        out_shape=(jax.ShapeDtypeStruct((S, D), F32), jax.ShapeDtypeStruct((8, D), F32)),
        in_specs=[row(w) for _, w in PIECES]
        + [pl.BlockSpec(wt_pad.shape, lambda i: (0, 0)), row(D), row(D), pl.BlockSpec((1, D), lambda i: (0, 0))],
        out_specs=(row(D), pl.BlockSpec((8, D), lambda i: (0, 0))),
        compiler_params=_params(dimension_semantics=("arbitrary",)),
    )(*pieces, wt_pad, dh, x, scale)


def _grad_ada(c_all, dada_all, dada_cols):
    def body(c_ref, dall_ref, dcol_ref, gw_ref, gb_ref):
        rows = lax.broadcasted_iota(jnp.int32, (8, 1), 0)
        cm = jnp.zeros((8, D), F32)
        dm = jnp.zeros((8, 3 * D), F32)
        for r in range(8):
            cm = jnp.where(rows == r, c_ref[r], cm)
            dm = jnp.where(rows == r, dall_ref[r], dm)
        act = cm * _sigmoid(cm)
        pad = jnp.zeros((8, D), F32)
        lhs = jnp.concatenate([act, pad], axis=0).astype(BF16)
        rhs = jnp.concatenate([dcol_ref[...], jnp.zeros((8, SHARD_ADA), F32)], axis=0).astype(BF16)
        gw_ref[...] = _dot_tn(lhs, rhs)
        gb_ref[...] = _colsum(dm)

    vm = pl.BlockSpec(memory_space=pltpu.VMEM)
    return pl.pallas_call(
        body, name="grad_ada",
        out_shape=(jax.ShapeDtypeStruct((D, SHARD_ADA), F32), jax.ShapeDtypeStruct((1, 3 * D), F32)),
        in_specs=[vm, vm, vm], out_specs=(vm, vm),
        compiler_params=_params(),
    )(c_all, dada_all, dada_cols)


def _adamw_math(w, g, m, v):
    m = ADAM_B1 * m + (1.0 - ADAM_B1) * g
    v = ADAM_B2 * v + (1.0 - ADAM_B2) * (g * g)
    m_hat = m / (1.0 - ADAM_B1 ** ADAM_STEP)
    v_hat = v / (1.0 - ADAM_B2 ** ADAM_STEP)
    delta = -ADAM_LR * (m_hat / (jnp.sqrt(v_hat) + ADAM_EPS) + ADAM_WD * w)
    return delta, m, v


def _adamw(groups, n_steps):
    n = len(groups)

    def body(*refs):
        ins, outs = refs[:4 * n], refs[4 * n:]
        for t in range(n):
            w, g, m, v = (r[...] for r in ins[4 * t:4 * t + 4])
            d, m2, v2 = _adamw_math(w, g, m, v)
            outs[3 * t][...] = d
            outs[3 * t + 1][...] = m2
            outs[3 * t + 2][...] = v2

    in_specs, out_specs, out_shape, args = [], [], [], []
    for (w, g, m, v) in groups:
        rest = w.shape[1:]
        spec = pl.BlockSpec((w.shape[0] // n_steps,) + rest, lambda i, nd=len(rest): (i,) + (0,) * nd)
        in_specs += [spec] * 4
        out_specs += [spec] * 3
        out_shape += [jax.ShapeDtypeStruct(w.shape, F32)] * 3
        args += [w, g, m, v]
    return pl.pallas_call(
        body, name="adamw_%d_%d" % (n, n_steps), grid=(n_steps,),
        out_shape=tuple(out_shape), in_specs=in_specs, out_specs=tuple(out_specs),
        compiler_params=_params(dimension_semantics=("arbitrary",)),
    )(*args)


def _adamw_small(small_sum, g_b_ada, params):
    n = len(params)

    def body(gs_ref, gba_ref, *refs):
        ins, outs = refs[:3 * n], refs[3 * n:]
        for t, (name, w0, _, _) in enumerate(params):
            w_ref, m_ref, v_ref = ins[3 * t:3 * t + 3]
            first = SMALL_SEGS[name][0] if name in SMALL_SEGS else None
            if w0.shape[0] > 1:
                pieces = [((slice(None), slice(None)), gs_ref[first:first + w0.shape[0], :])]
            else:
                pieces = []
                for r in range(-(-w0.shape[1] // 128)):
                    lanes = slice(128 * r, min(128 * r + 128, w0.shape[1]))
                    g = gba_ref[0:1, lanes] if first is None else gs_ref[first + r:first + r + 1, 0:lanes.stop - lanes.start]
                    pieces.append(((slice(0, 1), lanes), g))
            for where, g in pieces:
                d, m2, v2 = _adamw_math(w_ref[where], g, m_ref[where], v_ref[where])
                for ref, val in zip(outs[4 * t:4 * t + 4], (g, d, m2, v2)):
                    ref[where] = val

    vm = pl.BlockSpec(memory_space=pltpu.VMEM)
    args = [small_sum, g_b_ada]
    out_shape = []
    for _, w, m, v in params:
        args += [w, m, v]
        out_shape += [jax.ShapeDtypeStruct(w.shape, F32)] * 4
    return pl.pallas_call(
        body, name="adamw_small",
        out_shape=tuple(out_shape), in_specs=[vm] * len(args), out_specs=(vm,) * len(out_shape),
        compiler_params=_params(),
    )(*args)


def _pack_small(parts):
    rows = []
    used = 0
    for name, (first, n_rows) in SMALL_SEGS.items():
        if first > used:
            rows.append(jnp.zeros((first - used, 128), F32))
        flat = parts[name].reshape(-1)
        flat = jnp.pad(flat, (0, n_rows * 128 - flat.shape[0]))
        rows.append(flat.reshape(n_rows, 128))
        used = first + n_rows
    rows.append(jnp.zeros((SMALL_ROWS - used, 128), F32))
    return jnp.concatenate(rows, axis=0)


def _unpack_small(buf, name, shape):
    first, n_rows = SMALL_SEGS[name]
    n = int(np.prod(shape))
    return buf[first:first + n_rows].reshape(-1)[:n].reshape(shape)


def _pad_in(v):
    r = v.shape[0]
    z = jnp.zeros((r, O_P - O_F - N_HEADS), v.dtype)
    return jnp.concatenate([v[:, :3 * D_ATT + N_HEADS], z, v[:, 3 * D_ATT + N_HEADS:]], axis=1)


def _unpad_in(v):
    return jnp.concatenate([v[:, :O_F + N_HEADS], v[:, O_P:]], axis=1)


def _shards_in(v):
    gap = O_P - (O_F + N_HEADS)
    parts = []
    for a in range(N_CHIPS):
        lo, hi = a * SHARD_IN, (a + 1) * SHARD_IN
        cut = O_F + N_HEADS
        if hi <= cut:
            parts.append(v[:, lo:hi])
        elif lo >= cut:
            parts.append(v[:, lo + gap:hi + gap])
        else:
            parts.append(jnp.concatenate([v[:, lo:cut], v[:, cut + gap:hi + gap]], axis=1))
    return jnp.stack(parts, axis=0)


def kernel(x, c, w_ada, b_ada, w_in, b_in, w_pool_mix, b_pool_mix, pool_scale, w_out, b_out, ln_g, ln_b, loss_target, m_w_ada, m_b_ada, m_w_in, m_b_in, m_w_pool_mix, m_b_pool_mix, m_pool_scale, m_w_out, m_b_out, m_ln_g, m_ln_b, v_w_ada, v_b_ada, v_w_in, v_b_in, v_w_pool_mix, v_b_pool_mix, v_pool_scale, v_w_out, v_b_out, v_ln_g, v_ln_b):
    S = x.shape[1]
    T = min(T_ATT, S)
    n_t = S // T
    chip = 2 * lax.axis_index("x") + lax.axis_index("y")
    x2 = x[0]
    tgt = loss_target[0]
    q_scale = jnp.concatenate([jnp.full((1, D_ATT), Q_SCALE, F32), jnp.ones((1, D_PAD - D_ATT), F32)], axis=1)

    to_cols = lambda a: jnp.transpose(a, (2, 0, 1))
    from_cols = lambda a: jnp.transpose(a, (1, 2, 0))
    c_all, ada4, wt_pad = _gather_and_ada(
        c, w_ada[0], b_ada.reshape(4, 1, SHARD_ADA), to_cols(w_in).reshape(SHARD_IN, D).astype(BF16))
    ada = ada4[:, 0, :].reshape(1, 3 * D)
    shift, scale, gate = ada[:, :D], ada[:, D:2 * D], ada[:, 2 * D:]
    b_pad = _pad_in(b_in) * q_scale
    w_mix_bf = w_pool_mix[0].astype(BF16)

    u, qkv, f, p, g, w_out_all = _in_proj(x2, shift, scale, wt_pad, b_pad, w_out[0].astype(BF16))
    w_out_full = w_out_all.reshape(D, D)
    big_f = _forget_cumsum(f)
    att, lse = _attention_fwd(qkv, big_f)

    dh, datt, dg, dpl, gw_out, gw_mix, vec, loss_part = _middle(
        x2, tgt, att, g, p, gate, w_mix_bf, b_pool_mix.reshape(1, D_POOL), pool_scale, w_out_full, b_out, ln_g, ln_b)
    dq, dk, dv, cs_att, dfk, dfq, g_w_out = _attention_bwd(
        qkv, datt, att, lse, big_f, gw_out.reshape(N_CHIPS, SHARD_OUT, D), jnp.ones((N_CHIPS, 1, D), F32))
    dp, df, cs_tail = _tail(dpl, dfk, dfq, f)
    pieces = (dq, dk, dv, df, dp, dg)
    gw_pad = _grad_w_in(u, pieces)
    grad_x, vec_x = _grad_x(pieces, wt_pad, dh, x2, scale)

    cs_qkv = jnp.transpose(cs_att.reshape(N_PAIR, 3, 128), (1, 0, 2)).reshape(1, 3 * D_ATT)
    gb_pad = jnp.concatenate([cs_qkv, cs_tail[1:2, 0:128], cs_tail[0:1, :], vec[4:5, :]], axis=1) * q_scale
    dada = jnp.concatenate([vec_x[0:1, :], vec_x[1:2, :], vec[2:3, :]], axis=1)
    small = _pack_small({
        "b_in": _unpad_in(gb_pad), "w_pool_mix": gw_mix, "b_pool_mix": vec[6:7, :D_POOL],
        "pool_scale": vec[5:6, :D_POOL], "b_out": vec[3:4, :], "ln_g": vec[0:1, :], "ln_b": vec[1:2, :],
        "loss": loss_part})

    g_w_in, small_sum, dada_all = _reduce_all(
        gw_pad, _shards_in(q_scale), small, dada)
    dada_cols = lax.dynamic_slice(dada_all[:, 0, :], (0, chip * SHARD_ADA), (8, SHARD_ADA))
    g_w_ada, g_b_ada = _grad_ada(c_all, dada_all, dada_cols)
    loss = _unpack_small(small_sum, "loss", (1,))[0]

    big = _adamw([(w_ada[0], g_w_ada, m_w_ada[0], v_w_ada[0]),
                  (w_out[0], g_w_out, m_w_out[0], v_w_out[0])], 8)
    g_w_in_cols = g_w_in
    big_in = _adamw([(to_cols(w_in), g_w_in_cols, to_cols(m_w_in), to_cols(v_w_in))], 14)
    tiles = lambda a: a.reshape(4 * POOL_GROUP, POOL_GROUP)
    flat = lambda a: a.reshape(1, D_POOL)
    small_params = [("b_ada", b_ada, m_b_ada, v_b_ada), ("b_in", b_in, m_b_in, v_b_in),
                    ("w_pool_mix", tiles(w_pool_mix), tiles(m_w_pool_mix), tiles(v_w_pool_mix)),
                    ("b_pool_mix", flat(b_pool_mix), flat(m_b_pool_mix), flat(v_b_pool_mix)),
                    ("pool_scale", pool_scale, m_pool_scale, v_pool_scale), ("b_out", b_out, m_b_out, v_b_out),
                    ("ln_g", ln_g, m_ln_g, v_ln_g), ("ln_b", ln_b, m_ln_b, v_ln_b)]
    sm = _adamw_small(small_sum, g_b_ada, small_params)
    sm_idx = {p[0]: n for n, p in enumerate(small_params)}
    shapes = {"w_pool_mix": (1, 4, POOL_GROUP, POOL_GROUP), "b_pool_mix": (1, 4, POOL_GROUP)}

    names = ["w_ada", "b_ada", "w_in", "b_in", "w_pool_mix", "b_pool_mix", "pool_scale", "w_out", "b_out",
             "ln_g", "ln_b"]
    big_idx = {"w_ada": 0, "w_out": 1}

    def leaf(kind, name):
        if name == "w_in":
            return from_cols(g_w_in_cols if kind == 0 else big_in[kind - 1])
        if name in big_idx:
            if kind == 0:
                return (g_w_ada, g_w_out)[big_idx[name]][None]
            return big[3 * big_idx[name] + kind - 1][None]
        val = sm[4 * sm_idx[name] + kind]
        return val.reshape(shapes[name]) if name in shapes else val

    outs = [loss, grad_x[None]]
    for kind in range(4):
        outs += [leaf(kind, n) for n in names]
    return tuple(outs)
```

```python
import functools

import numpy as np
import jax
import jax.numpy as jnp
from jax import lax
from jax.experimental import pallas as pl
from jax.experimental.pallas import tpu as pltpu

F32 = jnp.float32
BF16 = jnp.bfloat16
MESH = pl.DeviceIdType.MESH

D = 1024
D_ATT = 512
D_POOL = 512
N_HEADS = 8
HEAD_DIM = 64
N_PAIR = N_HEADS // 2
POOL_WINDOWS = (2, 4, 8, 16)
POOL_GROUP = 128
POOL_HALO = 16
LN_EPS = 1e-5
ALPHA = 2.0 ** 0.25
D_IN = 3 * D_ATT + N_HEADS + D_POOL + D_ATT + D_POOL
N_CHIPS = 4
SHARD_IN = D_IN // N_CHIPS
SHARD_ADA = 3 * D // N_CHIPS
SHARD_OUT = D // N_CHIPS

O_QKV, O_F, O_P, O_G, D_PAD = 0, 1536, 1664, 2176, 3200
Q_SCALE = HEAD_DIM ** -0.5

ADAM_LR, ADAM_B1, ADAM_B2, ADAM_EPS, ADAM_WD, ADAM_STEP = 0.001, 0.9, 0.999, 1e-08, 0.01, 10

NEG = -1e30

VMEM_LIMIT = 56 * 1024 * 1024

TM_PROJ = 1024
T_ATT = 512
TM_MID = 256
TM_GW = 1024
TM_DU = 512

REL7 = [(0, 0, 1), (0, 1, 0), (0, 1, 1), (1, 0, 0), (1, 0, 1), (1, 1, 0), (1, 1, 1)]
REL3 = [(0, 1), (1, 0), (1, 1)]

SMALL_SEGS = {}
_row = 0
for _name, _n in (("b_in", D_IN), ("w_pool_mix", 65536), ("b_pool_mix", 512), ("pool_scale", 512),
                  ("b_out", 1024), ("ln_g", 1024), ("ln_b", 1024), ("loss", 1)):
    _rows = -(-_n // 1024) * 8
    SMALL_SEGS[_name] = (_row, _rows)
    _row += _rows
SMALL_ROWS = -(-_row // 16) * 16


def _params(**kw):
    return pltpu.CompilerParams(vmem_limit_bytes=VMEM_LIMIT, **kw)


def _flip(v, d):
    return v if d == 0 else 1 - v


def _dot(a, b):
    return jnp.dot(a, b, preferred_element_type=F32)


def _dot_nt(a, b):
    return lax.dot_general(a, b, (((1,), (1,)), ((), ())), preferred_element_type=F32)


def _dot_tn(a, b):
    return lax.dot_general(a, b, (((0,), (0,)), ((), ())), preferred_element_type=F32)


def _sigmoid(v):
    return 1.0 / (1.0 + jnp.exp(-v))


def _colsum(v):
    return jnp.sum(v, axis=0, keepdims=True)


def _gather_stages(pos, src_ref, dst_ref, half, own_sem, s_sem, r_sem, fs_sem, fr_sem):
    x, y, cc, chip, sib = pos
    own = pltpu.make_async_copy(src_ref, dst_ref.at[chip], own_sem)
    first, landed, others = [], [], []
    for k, (dx, dy) in enumerate(REL3):
        px, py = _flip(x, dx), _flip(y, dy)
        first.append(pltpu.make_async_remote_copy(
            src_ref=src_ref.at[half(cc)], dst_ref=dst_ref.at[(chip,) + half(cc)],
            send_sem=s_sem.at[k], recv_sem=r_sem.at[k], device_id=(px, py, cc), device_id_type=MESH))
        landed.append(dst_ref.at[(2 * px + py,) + half(cc)])
        others.append(dst_ref.at[(2 * px + py,) + half(1 - cc)])
    passed = [pltpu.make_async_remote_copy(src_ref=landed[k], dst_ref=landed[k], send_sem=fs_sem.at[k],
                                           recv_sem=fr_sem.at[k], device_id=sib, device_id_type=MESH)
              for k in range(3)]

    def start():
        own.start()
        for cp in first:
            cp.start()

    def forward():
        for k in range(3):
            pltpu.make_async_remote_copy(src_ref=landed[k], dst_ref=landed[k], send_sem=s_sem.at[k],
                                         recv_sem=r_sem.at[k], device_id=sib, device_id_type=MESH).wait_recv()
            passed[k].start()

    def finish():
        for k in range(3):
            pltpu.make_async_remote_copy(src_ref=others[k], dst_ref=others[k], send_sem=fs_sem.at[k],
                                         recv_sem=fr_sem.at[k], device_id=sib, device_id_type=MESH).wait_recv()
        for cp in first + passed:
            cp.wait_send()
        own.wait()

    return start, forward, finish


def _gather_scratch():
    return [pltpu.SemaphoreType.DMA, pltpu.SemaphoreType.DMA((3,)), pltpu.SemaphoreType.DMA((3,)),
            pltpu.SemaphoreType.DMA((3,)), pltpu.SemaphoreType.DMA((3,))]


def _gather_and_ada(c, w_ada, b_ada4, w_in_sh):
    def body(c_ref, w_ref, b_ref, win_ref, call_ref, ada_ref, wt_pad_ref,
             win_all, cslab, sbuf, rbuf, cs_sem, cr_sem, as_sem, ar_sem, *gather_sems):
        x, y, cc = lax.axis_index("x"), lax.axis_index("y"), lax.axis_index("c")
        me = 4 * x + 2 * y + cc
        chip = 2 * x + y
        lane_half = lambda which: (slice(None), pl.ds(pl.multiple_of(which * (D // 2), D // 2), D // 2))
        start, forward, finish = _gather_stages((x, y, cc, chip, (x, y, 1 - cc)), win_ref, win_all, lane_half,
                                                *gather_sems)
        start()

        cslab[...] = jnp.broadcast_to(c_ref[...], (8, D))
        call_ref[me] = cslab[...]
        gathers = []
        for k, (dx, dy, dc) in enumerate(REL7):
            cp = pltpu.make_async_remote_copy(
                src_ref=cslab, dst_ref=call_ref.at[me], send_sem=cs_sem.at[k], recv_sem=cr_sem.at[k],
                device_id=(_flip(x, dx), _flip(y, dy), _flip(cc, dc)), device_id_type=MESH)
            cp.start()
            gathers.append(cp)
        for cp in gathers:
            cp.wait()
        slab_row = lax.broadcasted_iota(jnp.int32, (8, 1), 0)
        mat = jnp.zeros((8, D), F32)
        for r in range(8):
            mat = jnp.where(slab_row == r, call_ref[r], mat)
        act = (mat * _sigmoid(mat)).astype(BF16)
        part = _dot(act, w_ref[...].astype(BF16))
        sends = []
        for k, (dx, dy) in enumerate(REL3):
            px, py = _flip(x, dx), _flip(y, dy)
            r = 4 * px + 2 * py + cc
            piece = _colsum(jnp.where(slab_row == r, part, 0.0))
            sbuf[k] = jnp.broadcast_to(piece, (8, SHARD_ADA))
            cp = pltpu.make_async_remote_copy(
                src_ref=sbuf.at[k], dst_ref=rbuf.at[k], send_sem=as_sem.at[k], recv_sem=ar_sem.at[k],
                device_id=(px, py, cc), device_id_type=MESH)
            cp.start()
            sends.append(cp)
        own_piece = _colsum(jnp.where(slab_row == me, part, 0.0))
        ada_ref[chip] = jnp.broadcast_to(own_piece, (8, SHARD_ADA)) + b_ref[chip]
        for k, (dx, dy) in enumerate(REL3):
            sends[k].wait()
            a = 2 * _flip(x, dx) + _flip(y, dy)
            ada_ref[a] = rbuf[k] + b_ref[a]

        forward()
        finish()
        n_real = 3 * D_ATT + N_HEADS
        for a in range(N_CHIPS):
            lo, hi = a * SHARD_IN, (a + 1) * SHARD_IN
            for s0, s1 in ((lo, min(hi, D_ATT)), (max(lo, D_ATT), min(hi, n_real)), (max(lo, n_real), hi)):
                if s0 < s1:
                    rows = win_all[a, s0 - lo:s1 - lo, :]
                    if s1 <= D_ATT:
                        rows = rows * jnp.asarray(Q_SCALE, BF16)
                    shift = O_P - n_real if s0 >= n_real else 0
                    wt_pad_ref[s0 + shift:s1 + shift, :] = rows
        wt_pad_ref[n_real:O_P, :] = jnp.zeros((O_P - n_real, D), BF16)

    vm = pl.BlockSpec(memory_space=pltpu.VMEM)
    return pl.pallas_call(
        body, name="gather_and_ada",
        out_shape=(jax.ShapeDtypeStruct((8, 8, D), F32), jax.ShapeDtypeStruct((4, 8, SHARD_ADA), F32),
                   jax.ShapeDtypeStruct((D_PAD, D), BF16)),
        in_specs=[vm] * 4, out_specs=(vm,) * 3,
        scratch_shapes=[pltpu.VMEM((N_CHIPS, SHARD_IN, D), BF16),
                        pltpu.VMEM((8, D), F32), pltpu.VMEM((3, 8, SHARD_ADA), F32),
                        pltpu.VMEM((3, 8, SHARD_ADA), F32),
                        pltpu.SemaphoreType.DMA((7,)), pltpu.SemaphoreType.DMA((7,)),
                        pltpu.SemaphoreType.DMA((3,)), pltpu.SemaphoreType.DMA((3,))] + _gather_scratch(),
        compiler_params=_params(),
    )(c, w_ada, b_ada4, w_in_sh)


def _shard_cols():
    cut, gap = O_F + N_HEADS, O_P - (O_F + N_HEADS)
    out = []
    for a in range(N_CHIPS):
        lo, hi = a * SHARD_IN, (a + 1) * SHARD_IN
        out.append(([(lo, min(hi, cut))] if lo < cut else []) + ([(max(lo, cut) + gap, hi + gap)] if hi > cut else []))
    return out


def _scatter_stages(pos, g_ref, sc_ref, out_ref, sib_buf, send_buf, ici_buf, sem1, sem2s, sem2r, sem3, part=(0, 1),
                    cols=None, own_buf=None):
    x, y, cc, chip, sib = pos
    q, n_parts = part
    RH = (g_ref.shape[1] if cols is None else g_ref.shape[0]) // 2 // n_parts
    mine = pl.ds(pl.multiple_of((cc * n_parts + q) * RH, RH), RH)
    theirs = pl.ds(pl.multiple_of(((1 - cc) * n_parts + q) * RH, RH), RH)
    cp1 = pltpu.make_async_remote_copy(
        src_ref=g_ref.at[:, theirs, :] if cols is None else g_ref.at[theirs, :], dst_ref=sib_buf,
        send_sem=sem1.at[0], recv_sem=sem1.at[1], device_id=sib, device_id_type=MESH)
    sends = []
    for k, (dx, dy) in enumerate(REL3):
        px, py = _flip(x, dx), _flip(y, dy)
        sends.append(pltpu.make_async_remote_copy(
            src_ref=send_buf.at[2 * px + py], dst_ref=ici_buf.at[chip],
            send_sem=sem2s.at[k], recv_sem=sem2r.at[k], device_id=(px, py, cc), device_id_type=MESH))
    cp3 = pltpu.make_async_remote_copy(
        src_ref=out_ref.at[mine, :], dst_ref=out_ref.at[mine, :], send_sem=sem3.at[0], recv_sem=sem3.at[1],
        device_id=sib, device_id_type=MESH)

    def finish1():
        cp1.wait()
        if cols is None:
            for a in range(N_CHIPS):
                both = g_ref[a, mine, :] + sib_buf[a]
                sib_buf[a] = both
                send_buf[a] = both.astype(BF16)
        else:
            both = g_ref[mine, :] + sib_buf[...]
            for a, pieces in enumerate(cols):
                at = 0
                for lo, hi in pieces:
                    own_buf[a, :, at:at + hi - lo] = both[:, lo:hi]
                    send_buf[a, :, at:at + hi - lo] = both[:, lo:hi].astype(BF16)
                    at += hi - lo

    def start2():
        for cp in sends:
            cp.start()
        ici_buf[chip] = send_buf[chip]

    def finish2():
        for cp in sends:
            cp.wait()
        own = (sib_buf if cols is None else own_buf)[chip]
        parts = [jnp.where(chip == a, own, ici_buf[a].astype(F32)) for a in range(N_CHIPS)]
        out_ref[mine, 0:own.shape[1]] = ((parts[0] + parts[1]) + (parts[2] + parts[3])) * sc_ref[chip]

    return [(cp1.start, finish1), (start2, finish2), (cp3.start, cp3.wait)]


def _all_reduce_stages(pos, g_ref, out_ref, sib_buf, ici_buf, sem1, sem2s, sem2r, sem3):
    x, y, cc, chip, sib = pos
    RH = g_ref.shape[0] // 2
    mine = pl.ds(pl.multiple_of(cc * RH, 8), RH)
    theirs = pl.ds(pl.multiple_of((1 - cc) * RH, 8), RH)
    cp1 = pltpu.make_async_remote_copy(
        src_ref=g_ref.at[theirs, :], dst_ref=sib_buf, send_sem=sem1.at[0], recv_sem=sem1.at[1],
        device_id=sib, device_id_type=MESH)
    sends = []
    for k, (dx, dy) in enumerate(REL3):
        px, py = _flip(x, dx), _flip(y, dy)
        sends.append(pltpu.make_async_remote_copy(
            src_ref=sib_buf, dst_ref=ici_buf.at[chip],
            send_sem=sem2s.at[k], recv_sem=sem2r.at[k], device_id=(px, py, cc), device_id_type=MESH))
    cp3 = pltpu.make_async_remote_copy(
        src_ref=out_ref.at[mine, :], dst_ref=out_ref.at[mine, :], send_sem=sem3.at[0], recv_sem=sem3.at[1],
        device_id=sib, device_id_type=MESH)

    def finish1():
        cp1.wait()
        sib_buf[...] = g_ref[mine, :] + sib_buf[...]

    def start2():
        for cp in sends:
            cp.start()
        ici_buf[chip] = sib_buf[...]

    def finish2():
        for cp in sends:
            cp.wait()
        out_ref[mine, :] = (ici_buf[0] + ici_buf[1]) + (ici_buf[2] + ici_buf[3])

    return [(cp1.start, finish1), (start2, finish2), (cp3.start, cp3.wait)]


def _stage_sems():
    return [pltpu.SemaphoreType.DMA((2,)), pltpu.SemaphoreType.DMA((3,)),
            pltpu.SemaphoreType.DMA((3,)), pltpu.SemaphoreType.DMA((2,))]


def _scatter_scratch(r, c):
    return [pltpu.VMEM((N_CHIPS, r // 2, c), F32), pltpu.VMEM((N_CHIPS, r // 2, c), BF16),
            pltpu.VMEM((N_CHIPS, r // 2, c), BF16)] + _stage_sems()


def _reduce_all(gw_pad, sc_in, small, dada):
    R = small.shape[0]
    W = dada.shape[1]
    r_in, p_in = gw_pad.shape
    c_in = SHARD_IN
    chunk = r_in // 4

    def chunk_scratch():
        return ([pltpu.VMEM((chunk, p_in), F32), pltpu.VMEM((N_CHIPS, chunk, c_in), BF16),
                 pltpu.VMEM((N_CHIPS, chunk, c_in), BF16)] + _stage_sems()
                + [pltpu.VMEM((N_CHIPS, chunk, c_in), F32)])

    n_in = len(chunk_scratch())

    c_wide = -(-c_in // 128) * 128

    def body(gin_ref, scin_ref, sm_ref, d_ref, ocols_ref, osm_ref, dall_ref, oin_ref, *scratch):
        x, y, cc = lax.axis_index("x"), lax.axis_index("y"), lax.axis_index("c")
        me = 4 * x + 2 * y + cc
        pos = (x, y, cc, 2 * x + y, (x, y, 1 - cc))
        oin_ref[:, c_in:c_wide] = jnp.zeros((r_in, c_wide - c_in), F32)
        dslab, ds_sem, dr_sem = scratch[0:3]
        a_bufs, b_bufs, sm_bufs = scratch[3:3 + n_in], scratch[3 + n_in:3 + 2 * n_in], scratch[3 + 2 * n_in:]
        dslab[...] = jnp.broadcast_to(d_ref[...], (8, W))
        dall_ref[me] = dslab[...]
        gathers = []
        for k, (dx, dy, dc) in enumerate(REL7):
            cp = pltpu.make_async_remote_copy(
                src_ref=dslab, dst_ref=dall_ref.at[me], send_sem=ds_sem.at[k], recv_sem=dr_sem.at[k],
                device_id=(_flip(x, dx), _flip(y, dy), _flip(cc, dc)), device_id_type=MESH)
            cp.start()
            gathers.append(cp)
        cols = _shard_cols()
        first = _scatter_stages(pos, gin_ref, scin_ref, oin_ref, *a_bufs[:-1], part=(0, 2), cols=cols,
                                own_buf=a_bufs[-1])
        second = _scatter_stages(pos, gin_ref, scin_ref, oin_ref, *b_bufs[:-1], part=(1, 2), cols=cols,
                                 own_buf=b_bufs[-1])
        little = _all_reduce_stages(pos, sm_ref, osm_ref, *sm_bufs)
        for plan in (first, second, little):
            plan[0][0]()
        first[0][1]()
        first[1][0]()
        little[0][1]()
        little[1][0]()
        second[0][1]()
        second[1][0]()
        first[1][1]()
        first[2][0]()
        second[1][1]()
        second[2][0]()
        little[1][1]()
        little[2][0]()
        for plan in (first, second, little):
            plan[2][1]()
        ocols_ref[...] = oin_ref[...].T[0:c_in, :][:, None, :]
        for cp in gathers:
            cp.wait()

    scratch = [pltpu.VMEM((r_in, c_wide), F32),
               pltpu.VMEM((8, W), F32), pltpu.SemaphoreType.DMA((7,)), pltpu.SemaphoreType.DMA((7,))]
    scratch += chunk_scratch() + chunk_scratch()
    scratch += [pltpu.VMEM((R // 2, 128), F32), pltpu.VMEM((N_CHIPS, R // 2, 128), F32)] + _stage_sems()
    vm = pl.BlockSpec(memory_space=pltpu.VMEM)
    return pl.pallas_call(
        body, name="reduce_all",
        out_shape=(jax.ShapeDtypeStruct((c_in, 1, r_in), F32),
                   jax.ShapeDtypeStruct((R, 128), F32), jax.ShapeDtypeStruct((8, 8, W), F32)),
        in_specs=[vm] * 4, out_specs=(vm,) * 3,
        scratch_shapes=scratch,
        compiler_params=_params(),
    )(gw_pad, sc_in, small, dada)


def _in_proj(x, shift, scale, wt_pad, b_pad, w_out_sh):
    S = x.shape[0]
    tm = min(TM_PROJ, S)
    n_steps = S // tm
    assert n_steps >= 3

    def body(x_ref, sh_ref, sc_ref, w_ref, b_ref, wo_ref, u_ref, qkv_ref, f_ref, p_ref, g_ref, wo_all,
             wo_buf, *gather_sems):
        i = pl.program_id(0)
        xx, yy, cc = lax.axis_index("x"), lax.axis_index("y"), lax.axis_index("c")
        row_half = lambda which: (pl.ds(pl.multiple_of(which * (SHARD_OUT // 2), SHARD_OUT // 2), SHARD_OUT // 2),
                                  slice(None))
        start, forward, finish = _gather_stages((xx, yy, cc, 2 * xx + yy, (xx, yy, 1 - cc)), wo_ref, wo_buf,
                                                row_half, *gather_sems)
        pl.when(i == 0)(start)
        pl.when(i == n_steps // 2)(forward)

        @pl.when(i == n_steps - 1)
        def _():
            finish()
            wo_all[...] = wo_buf[...]

        u = (x_ref[...] * (1.0 + sc_ref[...]) + sh_ref[...]).astype(BF16)
        u_ref[...] = u
        qkv_ref[...] = (_dot_nt(u, w_ref[O_QKV:O_F, :]) + b_ref[:, O_QKV:O_F]).astype(BF16)
        f_ref[...] = _dot_nt(u, w_ref[O_F:O_P, :]) + b_ref[:, O_F:O_P]
        p_ref[...] = _dot_nt(u, w_ref[O_P:O_G, :]) + b_ref[:, O_P:O_G]
        g_ref[...] = _dot_nt(u, w_ref[O_G:D_PAD, :]) + b_ref[:, O_G:D_PAD]

    row = lambda w: pl.BlockSpec((tm, w), lambda i: (i, 0))
    full = lambda a: pl.BlockSpec(a.shape, lambda i: (0, 0))
    vm = pl.BlockSpec(memory_space=pltpu.VMEM)
    return pl.pallas_call(
        body, name="in_proj", grid=(n_steps,),
        out_shape=(jax.ShapeDtypeStruct((S, D), BF16), jax.ShapeDtypeStruct((S, 3 * D_ATT), BF16),
                   jax.ShapeDtypeStruct((S, 128), F32), jax.ShapeDtypeStruct((S, D_POOL), F32),
                   jax.ShapeDtypeStruct((S, D), F32), jax.ShapeDtypeStruct((N_CHIPS,) + w_out_sh.shape, BF16)),
        in_specs=[row(D), full(shift), full(scale), full(wt_pad), full(b_pad), vm],
        out_specs=(row(D), row(3 * D_ATT), row(128), row(D_POOL), row(D), vm),
        scratch_shapes=[pltpu.VMEM((N_CHIPS,) + w_out_sh.shape, BF16)] + _gather_scratch(),
        compiler_params=_params(dimension_semantics=("arbitrary",)),
    )(x, shift, scale, wt_pad, b_pad, w_out_sh)


def _forget_cumsum(f):
    S = f.shape[0]
    tm = min(T_ATT, S)

    def body(f_ref, out_ref, carry):
        @pl.when(pl.program_id(0) == 0)
        def _():
            carry[...] = jnp.zeros_like(carry)
        v = f_ref[...]
        logf = jnp.minimum(v, 0.0) - jnp.log(1.0 + jnp.exp(-jnp.abs(v)))
        r = lax.broadcasted_iota(jnp.int32, (tm, tm), 0)
        c = lax.broadcasted_iota(jnp.int32, (tm, tm), 1)
        tri = (r <= c).astype(F32)
        rows8 = logf.T[0:8, :]
        cum8 = jnp.dot(rows8, tri, preferred_element_type=F32, precision=lax.Precision.HIGHEST) + carry[...]
        out_ref[...] = jnp.concatenate([cum8, jnp.zeros((128 - 8, tm), F32)], axis=0).T
        last = lax.broadcasted_iota(jnp.int32, (1, tm), 1) == tm - 1
        carry[...] = jnp.sum(jnp.where(last, cum8, 0.0), axis=1, keepdims=True)

    return pl.pallas_call(
        body, name="forget_cumsum", grid=(S // tm,),
        out_shape=jax.ShapeDtypeStruct((S, 128), F32),
        in_specs=[pl.BlockSpec((tm, 128), lambda i: (i, 0))],
        out_specs=pl.BlockSpec((tm, 128), lambda i: (i, 0)),
        scratch_shapes=[pltpu.VMEM((8, 1), F32)],
        compiler_params=_params(dimension_semantics=("arbitrary",)),
    )(f)


def _split3(v):
    hi = v.astype(BF16)
    rest = v - hi.astype(F32)
    mid = rest.astype(BF16)
    lo = (rest - mid.astype(F32)).astype(BF16)
    return hi, mid, lo


def _attention_fwd(qkv, big_f):
    S = qkv.shape[0]
    T = min(T_ATT, S)
    n_t = S // T

    def body(q_ref, k_ref, v_ref, f_ref, o_ref, lse_ref, kaug_sc, vt_sc, m_sc, l_sc, acc_sc):
        hp = pl.program_id(0)
        i = pl.program_id(1)
        lane = lax.broadcasted_iota(jnp.int32, (1, 128), 1)
        sub = lax.broadcasted_iota(jnp.int32, (128, 1), 0)
        head_sel = (lane < HEAD_DIM, lane >= HEAD_DIM)
        head_sel_t = (sub < HEAD_DIM, sub >= HEAD_DIM)
        spare = (HEAD_DIM, 0)
        zero = jnp.zeros((), BF16)

        @pl.when(i == 0)
        def _():
            def prep(jt, carry):
                rows = pl.ds(pl.multiple_of(jt * T, T), T)
                k = k_ref[rows, :]
                ft = f_ref[rows, :]
                vt = v_ref[rows, :].astype(F32).T
                for h in range(2):
                    fh = jnp.sum(jnp.where(lane == 2 * hp + h, ft, 0.0), axis=1, keepdims=True)
                    hi, mid, lo = _split3(-fh)
                    b = spare[h]
                    bias = jnp.where(lane == b, hi, jnp.where(lane == b + 1, mid, jnp.where(lane == b + 2, lo, zero)))
                    kaug_sc[h, rows, :] = jnp.where(head_sel[h], k, bias)
                    vt_sc[h, jt] = jnp.where(head_sel_t[h], vt, 0.0).astype(BF16)
                return carry

            lax.fori_loop(0, n_t, prep, 0)

        q = q_ref[...]
        q_heads = []
        for h in range(2):
            ones = jnp.where((lane >= spare[h]) & (lane < spare[h] + 3), jnp.ones((), BF16), zero)
            q_heads.append(jnp.where(head_sel[h], q, ones))
        m_sc[...] = jnp.full((8, T), NEG, F32)
        l_sc[...] = jnp.zeros((8, T), F32)
        acc_sc[...] = jnp.zeros((128, T), F32)

        def update(j, k_lo, n_k, q_lo, masked):
            rows = pl.ds(pl.multiple_of(j * T + k_lo, n_k), n_k)
            n_q = T - q_lo
            alphas, pvs = [], []
            for h in range(2):
                s_t = _dot_nt(kaug_sc[h, rows, :], q_heads[h][q_lo:, :])
                if masked:
                    rr = lax.broadcasted_iota(jnp.int32, (n_k, n_q), 0) + k_lo
                    cc = lax.broadcasted_iota(jnp.int32, (n_k, n_q), 1) + q_lo
                    s_t = jnp.where(rr <= cc, s_t, NEG)
                m_prev = m_sc[h:h + 1, q_lo:]
                m_new = jnp.maximum(m_prev, jnp.max(s_t, axis=0, keepdims=True))
                alpha = jnp.exp(m_prev - m_new)
                p_t = jnp.exp(s_t - m_new)
                l_sc[h:h + 1, q_lo:] = alpha * l_sc[h:h + 1, q_lo:] + jnp.sum(p_t, axis=0, keepdims=True)
                m_sc[h:h + 1, q_lo:] = m_new
                alphas.append(alpha)
                pvs.append(_dot(vt_sc[h, j, :, k_lo:k_lo + n_k], p_t.astype(BF16)))
            acc_sc[:, q_lo:] = (acc_sc[:, q_lo:] * jnp.where(head_sel_t[0], alphas[0], alphas[1])
                                + (pvs[0] + pvs[1]))

        def two_off_diagonal(jj, carry):
            update(2 * jj, 0, T, 0, False)
            update(2 * jj + 1, 0, T, 0, False)
            return carry

        lax.fori_loop(0, i // 2, two_off_diagonal, 0)

        @pl.when(i % 2 == 1)
        def _():
            update(i - 1, 0, T, 0, False)

        update(i, 0, T, 0, True)
        l = l_sc[...]
        o_ref[...] = (acc_sc[...] / jnp.where(head_sel_t[0], l[0:1, :], l[1:2, :])).T
        is_head = lax.broadcasted_iota(jnp.int32, (8, 1), 0) < 2
        lse_ref[...] = jnp.where(is_head, m_sc[...] + jnp.log(jnp.where(is_head, l, 1.0)), 0.0)

    return pl.pallas_call(
        body, name="attention_fwd", grid=(N_PAIR, n_t),
        out_shape=(jax.ShapeDtypeStruct((S, D_ATT), F32), jax.ShapeDtypeStruct((N_PAIR, n_t, 8, T), F32)),
        in_specs=[pl.BlockSpec((T, 128), lambda hp, i: (i, hp)),
                  pl.BlockSpec((S, 128), lambda hp, i: (0, N_PAIR + hp)),
                  pl.BlockSpec((S, 128), lambda hp, i: (0, 2 * N_PAIR + hp)),
                  pl.BlockSpec((S, 128), lambda hp, i: (0, 0))],
        out_specs=(pl.BlockSpec((T, 128), lambda hp, i: (i, hp)),
                   pl.BlockSpec((None, None, 8, T), lambda hp, i: (hp, i, 0, 0))),
        scratch_shapes=[pltpu.VMEM((2, S, 128), BF16), pltpu.VMEM((2, n_t, 128, T), BF16),
                        pltpu.VMEM((8, T), F32), pltpu.VMEM((8, T), F32), pltpu.VMEM((128, T), F32)],
        compiler_params=_params(dimension_semantics=("arbitrary", "arbitrary")),
    )(qkv, qkv, qkv, big_f)


def _attention_bwd(qkv, datt, att, lse, big_f, gw_out4, sc_out):
    S = qkv.shape[0]
    T = min(T_ATT, S)
    n_t = S // T
    n_steps = N_PAIR * n_t
    marks = (0, n_steps // 8, n_steps // 2, n_steps // 2 + n_steps // 8)

    def body(q_ref, do_ref, o_ref, lse_ref, k_ref, v_ref, fk_ref, gout_ref, scout_ref,
             dq_ref, dk_ref, dv_ref, cs_ref, dfk_ref, dfq_ref, oout_ref, stat_sc, dqt_sc, qaug_sc,
             out_buf, *red_bufs):
        hp = pl.program_id(0)
        j = pl.program_id(1)
        x, y, cc = lax.axis_index("x"), lax.axis_index("y"), lax.axis_index("c")
        plan = _scatter_stages((x, y, cc, 2 * x + y, (x, y, 1 - cc)), gout_ref, scout_ref, out_buf, *red_bufs)
        step = hp * n_t + j
        for n, mark in enumerate(marks):
            @pl.when(step == mark)
            def _(n=n):
                if n > 0:
                    plan[n - 1][1]()
                if n < 3:
                    plan[n][0]()
                else:
                    oout_ref[...] = out_buf[...]

        lane = lax.broadcasted_iota(jnp.int32, (1, 128), 1)
        sub = lax.broadcasted_iota(jnp.int32, (128, 1), 0)
        head_sel = (lane < HEAD_DIM, lane >= HEAD_DIM)
        head_sel_t = (sub < HEAD_DIM, sub >= HEAD_DIM)
        spare = (HEAD_DIM, 0)
        zero = jnp.zeros((), BF16)
        one = jnp.ones((), BF16)

        def bias_lanes(first, pieces):
            hi, mid, lo = pieces
            return lambda rest: jnp.where(lane == first, hi, jnp.where(lane == first + 1, mid,
                                                                        jnp.where(lane == first + 2, lo, rest)))

        @pl.when(j == 0)
        def _():
            dqt_sc[...] = jnp.zeros_like(dqt_sc)
            cs_ref[...] = jnp.zeros_like(cs_ref)
            dfq_ref[...] = jnp.zeros_like(dfq_ref)

            def prep(i, carry):
                rows = pl.ds(pl.multiple_of(i * T, T), T)
                q = q_ref[rows, :]
                do = do_ref[rows, :]
                prod = o_ref[rows, :] * do.astype(F32)
                d_a = jnp.sum(jnp.where(head_sel[0], prod, 0.0), axis=1, keepdims=True)
                d_b = jnp.sum(jnp.where(head_sel[0], 0.0, prod), axis=1, keepdims=True)
                delta_t = jnp.where(head_sel[0], d_a, d_b).T
                stat_sc[i, 0:1, :] = delta_t[0:1, :]
                stat_sc[i, 1:2, :] = delta_t[HEAD_DIM:HEAD_DIM + 1, :]
                lse = lse_ref[i]
                lse_cols = jnp.where(head_sel_t[0], lse[0:1, :], lse[1:2, :]).T
                for h in range(2):
                    neg_lse = -lse_cols[:, h * HEAD_DIM:h * HEAD_DIM + 1]
                    ones = jnp.where((lane >= spare[h]) & (lane < spare[h] + 3), one, zero)
                    qaug_sc[h, rows, :] = jnp.where(head_sel[h], q, bias_lanes(spare[h] + 3, _split3(neg_lse))(ones))
                return carry

            lax.fori_loop(0, n_t, prep, 0)

        k = k_ref[...]
        v = v_ref[...]
        fk = fk_ref[...]
        kt = k.astype(F32).T
        heads = []
        for h in range(2):
            fkh = jnp.sum(jnp.where(lane == 2 * hp + h, fk, 0.0), axis=1, keepdims=True)
            ones = jnp.where((lane >= spare[h] + 3) & (lane < spare[h] + 6), one, zero)
            kaug = jnp.where(head_sel[h], k, bias_lanes(spare[h], _split3(-fkh))(ones))
            heads.append((kaug, jnp.where(head_sel[h], v, zero), jnp.where(head_sel_t[h], kt, 0.0).astype(BF16)))

        def block(i, k_lo, n_k, q_lo, masked):
            n_q = T - q_lo
            rows = pl.ds(pl.multiple_of(i * T + q_lo, n_q), n_q)
            q = q_ref[rows, :]
            do = do_ref[rows, :]
            stat = stat_sc[i]
            dk = jnp.zeros((n_k, 128), F32)
            dv = jnp.zeros((n_k, 128), F32)
            dqt = jnp.zeros((128, n_q), F32)
            dfs = []
            for h in range(2):
                kaug, vh, kth = heads[h]
                arg = _dot_nt(kaug[k_lo:k_lo + n_k, :], qaug_sc[h, rows, :])
                if masked:
                    rr = lax.broadcasted_iota(jnp.int32, (n_k, n_q), 0) + k_lo
                    cc = lax.broadcasted_iota(jnp.int32, (n_k, n_q), 1) + q_lo
                    arg = jnp.where(rr <= cc, arg, NEG)
                p_t = jnp.exp(arg)
                ds_t = p_t * (_dot_nt(vh[k_lo:k_lo + n_k, :], do) - stat[h:h + 1, q_lo:])
                ds_bf = ds_t.astype(BF16)
                dv = dv + _dot(p_t.astype(BF16), jnp.where(head_sel[h], do, zero))
                dk = dk + _dot(ds_bf, jnp.where(head_sel[h], q, zero))
                dqt = dqt + _dot(kth[:, k_lo:k_lo + n_k], ds_bf)
                dfs.append(jnp.sum(ds_t, axis=1, keepdims=True))
                dfq_ref[i, h:h + 1, q_lo:] += _colsum(ds_t)
            dqt_sc[i, :, q_lo:] += dqt
            return dk, dv, dfs[0], dfs[1]

        def off_diagonal(i, acc):
            return tuple(a + b for a, b in zip(acc, block(i, 0, T, 0, False)))

        half = T // 2
        early = block(j, 0, half, 0, True)
        late = block(j, half, half, half, True)
        acc1 = tuple(jnp.concatenate([a, b], axis=0) for a, b in zip(early, late))
        n_off = n_t - 1 - j
        acc2 = lax.fori_loop(0, n_off // 2,
                             lambda ii, a: off_diagonal(j + 2 + 2 * ii, off_diagonal(j + 1 + 2 * ii, a)), acc1)
        dk_acc, dv_acc, dfa, dfb = lax.fori_loop(0, n_off % 2, lambda _, a: off_diagonal(n_t - 1, a), acc2)
        dk_ref[...] = dk_acc.astype(BF16)
        dv_ref[...] = dv_acc.astype(BF16)
        dfk_ref[...] = -jnp.where(lane == 0, dfa, jnp.where(lane == 1, dfb, 0.0))
        cs_ref[:, 128:256] = cs_ref[:, 128:256] + _colsum(dk_acc)
        cs_ref[:, 256:384] = cs_ref[:, 256:384] + _colsum(dv_acc)

        @pl.when(j == n_t - 1)
        def _():
            def finish(i, tot):
                dq = dqt_sc[i].T
                dq_ref[pl.ds(pl.multiple_of(i * T, T), T), :] = dq.astype(BF16)
                return tot + _colsum(dq)

            cs_ref[:, 0:128] = lax.fori_loop(0, n_t, finish, jnp.zeros((1, 128), F32))

    pair_rows = lambda hp, j: (hp, 0, 0)
    vm = pl.BlockSpec(memory_space=pltpu.VMEM)
    _, r_out, c_out = gw_out4.shape
    return pl.pallas_call(
        body, name="attention_bwd", grid=(N_PAIR, n_t),
        out_shape=(jax.ShapeDtypeStruct((S, D_ATT), BF16), jax.ShapeDtypeStruct((S, D_ATT), BF16),
                   jax.ShapeDtypeStruct((S, D_ATT), BF16), jax.ShapeDtypeStruct((N_PAIR, 1, 384), F32),
                   jax.ShapeDtypeStruct((N_PAIR, S, 128), F32),
                   jax.ShapeDtypeStruct((N_PAIR, n_t, 8, T), F32),
                   jax.ShapeDtypeStruct((r_out, c_out), F32)),
        in_specs=[pl.BlockSpec((S, 128), lambda hp, j: (0, hp)),
                  pl.BlockSpec((S, 128), lambda hp, j: (0, hp)),
                  pl.BlockSpec((S, 128), lambda hp, j: (0, hp)),
                  pl.BlockSpec((None, n_t, 8, T), lambda hp, j: (hp, 0, 0, 0)),
                  pl.BlockSpec((T, 128), lambda hp, j: (j, N_PAIR + hp)),
                  pl.BlockSpec((T, 128), lambda hp, j: (j, 2 * N_PAIR + hp)),
                  pl.BlockSpec((T, 128), lambda hp, j: (j, 0)),
                  vm, vm],
        out_specs=(pl.BlockSpec((S, 128), lambda hp, j: (0, hp)),
                   pl.BlockSpec((T, 128), lambda hp, j: (j, hp)),
                   pl.BlockSpec((T, 128), lambda hp, j: (j, hp)),
                   pl.BlockSpec((None, 1, 384), pair_rows),
                   pl.BlockSpec((None, T, 128), lambda hp, j: (hp, j, 0)),
                   pl.BlockSpec((None, n_t, 8, T), lambda hp, j: (hp, 0, 0, 0)),
                   vm),
        scratch_shapes=[pltpu.VMEM((n_t, 8, T), F32), pltpu.VMEM((n_t, 128, T), F32),
                        pltpu.VMEM((2, S, 128), BF16), pltpu.VMEM((r_out, c_out), F32)]
        + _scatter_scratch(r_out, c_out),
        compiler_params=_params(dimension_semantics=("arbitrary", "arbitrary")),
    )(qkv, datt, att, lse, qkv, qkv, big_f, gw_out4, sc_out)


def _window_counts(first_row, n_rows, window):
    t = lax.broadcasted_iota(jnp.int32, (n_rows, 1), 0) + first_row
    return jnp.minimum((t + 1).astype(F32), float(window))


def _middle(x, tgt, att, g, p, gate, w_mix, b_mix, pool_scale, w_out, b_out, ln_g, ln_b):
    S = x.shape[0]
    tm = min(TM_MID, S)
    halo_blocks = tm // POOL_HALO

    def body(x_ref, t_ref, att_ref, g_ref, p_ref, ph_ref, gate_ref, wm_ref, bm_ref, ps_ref, wo_ref, bo_ref,
             lg_ref, lb_ref,
             dh_ref, datt_ref, dg_ref, dpl_ref, gwo_ref, gwm_ref, vec_ref, loss_ref):
        i = pl.program_id(0)

        @pl.when(i == 0)
        def _():
            gwo_ref[...] = jnp.zeros_like(gwo_ref)
            gwm_ref[...] = jnp.zeros_like(gwm_ref)
            vec_ref[...] = jnp.zeros_like(vec_ref)
            loss_ref[...] = jnp.zeros_like(loss_ref)

        pc = p_ref[...]
        halo = jnp.where(i > 0, ph_ref[...], 0.0)
        pe = jnp.concatenate([halo, pc], axis=0)
        pooled_parts = []
        for gi, w in enumerate(POOL_WINDOWS):
            cur = pe[:, gi * POOL_GROUP:(gi + 1) * POOL_GROUP]
            span = 1
            while span < w:
                cur = cur + pltpu.roll(cur, span, 0)
                span *= 2
            wsum = cur[POOL_HALO:, :]
            mean = wsum / _window_counts(i * tm, tm, w)
            pooled_parts.append(mean - pc[:, gi * POOL_GROUP:(gi + 1) * POOL_GROUP])
        pooled_bf =[v.astype(BF16) for v in pooled_parts]
        mixed = jnp.concatenate([_dot(pooled_bf[gi], wm_ref[gi]) for gi in range(4)], axis=1) + bm_ref[...]
        ps = ps_ref[...]
        pool_out = mixed * ps
        gv = g_ref[...]
        sig = _sigmoid(gv)
        silu = gv * sig
        att = att_ref[...]
        y = jnp.concatenate([att * silu[:, :D_ATT], pool_out * silu[:, D_ATT:]], axis=1)
        y_bf = y.astype(BF16)
        wo = wo_ref[...]
        yo = _dot(y_bf, wo) + bo_ref[...]
        gate = gate_ref[...]
        h = ALPHA * x_ref[...] + gate * yo
        mu = jnp.mean(h, axis=1, keepdims=True)
        hc = h - mu
        var = jnp.mean(hc * hc, axis=1, keepdims=True)
        rstd = lax.rsqrt(var + LN_EPS)
        yhat = hc * rstd
        lg = lg_ref[...]
        out = yhat * lg + lb_ref[...]
        err = out - t_ref[...]
        loss_ref[...] += 0.5 * jnp.sum(jnp.mean(err * err, axis=1, keepdims=True), axis=0, keepdims=True)

        dout = err * (1.0 / D)
        g_ln_b = _colsum(dout)
        g_ln_g = _colsum(dout * yhat)
        dyh = dout * lg
        dh = rstd * (dyh - jnp.mean(dyh, axis=1, keepdims=True)
                     - yhat * jnp.mean(dyh * yhat, axis=1, keepdims=True))
        dh_ref[...] = dh
        d_gate = _colsum(dh * yo)
        dyo = gate * dh
        g_b_out = _colsum(dyo)
        dyo_bf = dyo.astype(BF16)
        gwo_ref[...] += _dot_tn(y_bf, dyo_bf)
        dy = _dot_nt(dyo_bf, wo)
        dsilu = sig * (1.0 + gv * (1.0 - sig))
        dy_a = dy[:, :D_ATT]
        dy_p = dy[:, D_ATT:]
        datt_ref[...] = (dy_a * silu[:, :D_ATT]).astype(BF16)
        dpo = dy_p * silu[:, D_ATT:]
        dg = jnp.concatenate([dy_a * att * dsilu[:, :D_ATT], dy_p * pool_out * dsilu[:, D_ATT:]], axis=1)
        dg_ref[...] = dg.astype(BF16)
        g_dg = _colsum(dg)
        g_ps = _colsum(dpo * mixed)
        dmixed = dpo * ps
        g_bm = _colsum(dmixed)
        dmixed_bf = dmixed.astype(BF16)
        dpl = []
        for gi in range(4):
            dm = dmixed_bf[:, gi * POOL_GROUP:(gi + 1) * POOL_GROUP]
            gwm_ref[gi] += _dot_tn(pooled_bf[gi], dm)
            dpl.append(_dot_nt(dm, wm_ref[gi]))
        dpl_ref[...] = jnp.concatenate(dpl, axis=1)
        vec_ref[0:1, :] += g_ln_g
        vec_ref[1:2, :] += g_ln_b
        vec_ref[2:3, :] += d_gate
        vec_ref[3:4, :] += g_b_out
        vec_ref[4:5, :] += g_dg
        vec_ref[5:6, 0:D_POOL] += g_ps
        vec_ref[6:7, 0:D_POOL] += g_bm

    row = lambda w: pl.BlockSpec((tm, w), lambda i: (i, 0))
    full2 = lambda a: pl.BlockSpec(a.shape, lambda i: (0, 0))
    full3 = lambda a: pl.BlockSpec(a.shape, lambda i: (0, 0, 0))
    return pl.pallas_call(
        body, name="middle", grid=(S // tm,),
        out_shape=(jax.ShapeDtypeStruct((S, D), F32),
                   jax.ShapeDtypeStruct((S, D_ATT), BF16),
                   jax.ShapeDtypeStruct((S, D), BF16),
                   jax.ShapeDtypeStruct((S, D_POOL), F32),
                   jax.ShapeDtypeStruct((D, D), F32),
                   jax.ShapeDtypeStruct((4, POOL_GROUP, POOL_GROUP), F32),
                   jax.ShapeDtypeStruct((8, D), F32),
                   jax.ShapeDtypeStruct((1, 1), F32)),
        in_specs=[row(D), row(D), row(D_ATT), row(D), row(D_POOL),
                  pl.BlockSpec((POOL_HALO, D_POOL), lambda i: (jnp.maximum(i * halo_blocks - 1, 0), 0)),
                  full2(gate), full3(w_mix), full2(b_mix), full2(pool_scale), full2(w_out), full2(b_out),
                  full2(ln_g), full2(ln_b)],
        out_specs=(row(D), row(D_ATT), row(D), row(D_POOL),
                   pl.BlockSpec((D, D), lambda i: (0, 0)),
                   pl.BlockSpec((4, POOL_GROUP, POOL_GROUP), lambda i: (0, 0, 0)),
                   pl.BlockSpec((8, D), lambda i: (0, 0)),
                   pl.BlockSpec((1, 1), lambda i: (0, 0))),
        compiler_params=_params(dimension_semantics=("arbitrary",)),
    )(x, tgt, att, g, p, p, gate, w_mix, b_mix, pool_scale, w_out, b_out, ln_g, ln_b)


def _tail(dpl, dfk, dfq, f):
    S = dpl.shape[0]
    tm = min(T_ATT, S)
    n_t = S // tm
    halo_blocks = tm // POOL_HALO
    last_halo = S // POOL_HALO - 1

    def body(d_ref, dn_ref, dfk_ref, dfq_ref, f_ref, dp_ref, df_ref, cs_ref, carry):
        s = pl.program_id(0)
        i = n_t - 1 - s

        @pl.when(s == 0)
        def _():
            carry[...] = jnp.zeros_like(carry)
            cs_ref[...] = jnp.zeros_like(cs_ref)

        dc = d_ref[...]
        nxt = jnp.where(s > 0, dn_ref[...], 0.0)
        de = jnp.concatenate([dc, nxt], axis=0)
        n_e = tm + POOL_HALO
        parts = []
        for gi, w in enumerate(POOL_WINDOWS):
            cur = de[:, gi * POOL_GROUP:(gi + 1) * POOL_GROUP] / _window_counts(i * tm, n_e, w)
            span = 1
            while span < w:
                cur = cur + pltpu.roll(cur, n_e - span, 0)
                span *= 2
            parts.append(cur[:tm, :] - dc[:, gi * POOL_GROUP:(gi + 1) * POOL_GROUP])
        dp = jnp.concatenate(parts, axis=1)
        dp_ref[...] = dp.astype(BF16)
        cs_ref[0:1, :] += _colsum(dp)

        r = lax.broadcasted_iota(jnp.int32, (tm, tm), 0)
        c = lax.broadcasted_iota(jnp.int32, (tm, tm), 1)
        tri = (r >= c).astype(F32)
        k_cols = dfk_ref[0]
        rows8 = dfq_ref[0]
        for hp in range(1, N_PAIR):
            k_cols = k_cols + pltpu.roll(dfk_ref[hp], 2 * hp, 1)
            rows8 = rows8 + pltpu.roll(dfq_ref[hp], 2 * hp, 0)
        rows8 = rows8 + k_cols.T[0:8, :]
        dlogf8 = jnp.dot(rows8, tri, preferred_element_type=F32, precision=lax.Precision.HIGHEST) + carry[...]
        first = lax.broadcasted_iota(jnp.int32, (1, tm), 1) == 0
        carry[...] = jnp.sum(jnp.where(first, dlogf8, 0.0), axis=1, keepdims=True)
        dlogf = jnp.concatenate([dlogf8, jnp.zeros((128 - 8, tm), F32)], axis=0).T
        df = dlogf * _sigmoid(-f_ref[...])
        df_ref[...] = df.astype(BF16)
        cs_ref[1:2, 0:128] += _colsum(df)

    rev = lambda w: pl.BlockSpec((tm, w), lambda s: (n_t - 1 - s, 0))
    return pl.pallas_call(
        body, name="tail", grid=(n_t,),
        out_shape=(jax.ShapeDtypeStruct((S, D_POOL), BF16), jax.ShapeDtypeStruct((S, 128), BF16),
                   jax.ShapeDtypeStruct((8, D_POOL), F32)),
        in_specs=[rev(D_POOL),
                  pl.BlockSpec((POOL_HALO, D_POOL),
                               lambda s: (jnp.minimum((n_t - s) * halo_blocks, last_halo), 0)),
                  pl.BlockSpec((N_PAIR, tm, 128), lambda s: (0, n_t - 1 - s, 0)),
                  pl.BlockSpec((N_PAIR, None, 8, tm), lambda s: (0, n_t - 1 - s, 0, 0)),
                  rev(128)],
        out_specs=(rev(D_POOL), rev(128), pl.BlockSpec((8, D_POOL), lambda s: (0, 0))),
        scratch_shapes=[pltpu.VMEM((8, 1), F32)],
        compiler_params=_params(dimension_semantics=("arbitrary",)),
    )(dpl, dpl, dfk, dfq, f)


PIECES = ((O_QKV, D_ATT), (O_QKV + D_ATT, D_ATT), (O_QKV + 2 * D_ATT, D_ATT), (O_F, 128), (O_P, D_POOL), (O_G, D))


def _grad_w_in(u, pieces):
    S = u.shape[0]
    tm = min(TM_GW, S)
    n_t = S // tm

    def body(u_ref, *rest):
        piece_refs, out_ref, acc, sem = rest[:6], rest[6], rest[7], rest[8]
        i = pl.program_id(0)

        @pl.when(i == 0)
        def _():
            acc[...] = jnp.zeros_like(acc)

        u_t = u_ref[...]
        for (off, w), ref in zip(PIECES, piece_refs):
            acc[:, off:off + w] += _dot_tn(u_t, ref[...])

        @pl.when(i == n_t - 1)
        def _():
            cp = pltpu.make_async_copy(acc, out_ref, sem)
            cp.start()
            cp.wait()

    return pl.pallas_call(
        body, name="grad_w_in", grid=(n_t,),
        out_shape=jax.ShapeDtypeStruct((D, D_PAD), F32),
        in_specs=[pl.BlockSpec((tm, D), lambda i: (i, 0))]
        + [pl.BlockSpec((tm, w), lambda i: (i, 0)) for _, w in PIECES],
        out_specs=pl.BlockSpec(memory_space=pl.ANY),
        scratch_shapes=[pltpu.VMEM((D, D_PAD), F32), pltpu.SemaphoreType.DMA],
        compiler_params=_params(dimension_semantics=("arbitrary",)),
    )(u, *pieces)


def _grad_x(pieces, wt_pad, dh, x, scale):
    S = x.shape[0]
    tm = min(TM_DU, S)

    def body(*refs):
        piece_refs = refs[:6]
        w_ref, dh_ref, x_ref, sc_ref, gx_ref, vec_ref = refs[6:]

        @pl.when(pl.program_id(0) == 0)
        def _():
            vec_ref[...] = jnp.zeros_like(vec_ref)

        du = jnp.zeros((tm, D), F32)
        for (off, w), ref in zip(PIECES, piece_refs):
            du = du + _dot(ref[...], w_ref[off:off + w, :])
        xv = x_ref[...]
        gx_ref[...] = ALPHA * dh_ref[...] + du * (1.0 + sc_ref[...])
        vec_ref[0:1, :] += _colsum(du)
        vec_ref[1:2, :] += _colsum(du * xv)

    row = lambda w: pl.BlockSpec((tm, w), lambda i: (i, 0))
    return pl.pallas_call(
        body, name="grad_x", grid=(S // tm,),
        out_shape=(jax.ShapeDtypeStruct((S, D), F32), jax.ShapeDtypeStruct((8, D), F32)),
        in_specs=[row(w) for _, w in PIECES]
        + [pl.BlockSpec(wt_pad.shape, lambda i: (0, 0)), row(D), row(D), pl.BlockSpec((1, D), lambda i: (0, 0))],
        out_specs=(row(D), pl.BlockSpec((8, D), lambda i: (0, 0))),
        compiler_params=_params(dimension_semantics=("arbitrary",)),
    )(*pieces, wt_pad, dh, x, scale)


def _grad_ada(c_all, dada_all, dada_cols):
    def body(c_ref, dall_ref, dcol_ref, gw_ref, gb_ref):
        rows = lax.broadcasted_iota(jnp.int32, (8, 1), 0)
        cm = jnp.zeros((8, D), F32)
        dm = jnp.zeros((8, 3 * D), F32)
        for r in range(8):
            cm = jnp.where(rows == r, c_ref[r], cm)
            dm = jnp.where(rows == r, dall_ref[r], dm)
        act = cm * _sigmoid(cm)
        pad = jnp.zeros((8, D), F32)
        lhs = jnp.concatenate([act, pad], axis=0).astype(BF16)
        rhs = jnp.concatenate([dcol_ref[...], jnp.zeros((8, SHARD_ADA), F32)], axis=0).astype(BF16)
        gw_ref[...] = _dot_tn(lhs, rhs)
        gb_ref[...] = _colsum(dm)

    vm = pl.BlockSpec(memory_space=pltpu.VMEM)
    return pl.pallas_call(
        body, name="grad_ada",
        out_shape=(jax.ShapeDtypeStruct((D, SHARD_ADA), F32), jax.ShapeDtypeStruct((1, 3 * D), F32)),
        in_specs=[vm, vm, vm], out_specs=(vm, vm),
        compiler_params=_params(),
    )(c_all, dada_all, dada_cols)


def _adamw_math(w, g, m, v):
    m = ADAM_B1 * m + (1.0 - ADAM_B1) * g
    v = ADAM_B2 * v + (1.0 - ADAM_B2) * (g * g)
    m_hat = m / (1.0 - ADAM_B1 ** ADAM_STEP)
    v_hat = v / (1.0 - ADAM_B2 ** ADAM_STEP)
    delta = -ADAM_LR * (m_hat / (jnp.sqrt(v_hat) + ADAM_EPS) + ADAM_WD * w)
    return delta, m, v


def _adamw(groups, n_steps):
    n = len(groups)

    def body(*refs):
        ins, outs = refs[:4 * n], refs[4 * n:]
        for t in range(n):
            w, g, m, v = (r[...] for r in ins[4 * t:4 * t + 4])
            d, m2, v2 = _adamw_math(w, g, m, v)
            outs[3 * t][...] = d
            outs[3 * t + 1][...] = m2
            outs[3 * t + 2][...] = v2

    in_specs, out_specs, out_shape, args = [], [], [], []
    for (w, g, m, v) in groups:
        rest = w.shape[1:]
        spec = pl.BlockSpec((w.shape[0] // n_steps,) + rest, lambda i, nd=len(rest): (i,) + (0,) * nd)
        in_specs += [spec] * 4
        out_specs += [spec] * 3
        out_shape += [jax.ShapeDtypeStruct(w.shape, F32)] * 3
        args += [w, g, m, v]
    return pl.pallas_call(
        body, name="adamw_%d_%d" % (n, n_steps), grid=(n_steps,),
        out_shape=tuple(out_shape), in_specs=in_specs, out_specs=tuple(out_specs),
        compiler_params=_params(dimension_semantics=("arbitrary",)),
    )(*args)


def _adamw_small(small_sum, g_b_ada, params):
    n = len(params)

    def body(gs_ref, gba_ref, *refs):
        ins, outs = refs[:3 * n], refs[3 * n:]
        for t, (name, w0, _, _) in enumerate(params):
            w_ref, m_ref, v_ref = ins[3 * t:3 * t + 3]
            first = SMALL_SEGS[name][0] if name in SMALL_SEGS else None
            if w0.shape[0] > 1:
                pieces = [((slice(None), slice(None)), gs_ref[first:first + w0.shape[0], :])]
            else:
                pieces = []
                for r in range(-(-w0.shape[1] // 128)):
                    lanes = slice(128 * r, min(128 * r + 128, w0.shape[1]))
                    g = gba_ref[0:1, lanes] if first is None else gs_ref[first + r:first + r + 1, 0:lanes.stop - lanes.start]
                    pieces.append(((slice(0, 1), lanes), g))
            for where, g in pieces:
                d, m2, v2 = _adamw_math(w_ref[where], g, m_ref[where], v_ref[where])
                for ref, val in zip(outs[4 * t:4 * t + 4], (g, d, m2, v2)):
                    ref[where] = val

    vm = pl.BlockSpec(memory_space=pltpu.VMEM)
    args = [small_sum, g_b_ada]
    out_shape = []
    for _, w, m, v in params:
        args += [w, m, v]
        out_shape += [jax.ShapeDtypeStruct(w.shape, F32)] * 4
    return pl.pallas_call(
        body, name="adamw_small",
        out_shape=tuple(out_shape), in_specs=[vm] * len(args), out_specs=(vm,) * len(out_shape),
        compiler_params=_params(),
    )(*args)


def _pack_small(parts):
    rows = []
    used = 0
    for name, (first, n_rows) in SMALL_SEGS.items():
        if first > used:
            rows.append(jnp.zeros((first - used, 128), F32))
        flat = parts[name].reshape(-1)
        flat = jnp.pad(flat, (0, n_rows * 128 - flat.shape[0]))
        rows.append(flat.reshape(n_rows, 128))
        used = first + n_rows
    rows.append(jnp.zeros((SMALL_ROWS - used, 128), F32))
    return jnp.concatenate(rows, axis=0)


def _unpack_small(buf, name, shape):
    first, n_rows = SMALL_SEGS[name]
    n = int(np.prod(shape))
    return buf[first:first + n_rows].reshape(-1)[:n].reshape(shape)


def _pad_in(v):
    r = v.shape[0]
    z = jnp.zeros((r, O_P - O_F - N_HEADS), v.dtype)
    return jnp.concatenate([v[:, :3 * D_ATT + N_HEADS], z, v[:, 3 * D_ATT + N_HEADS:]], axis=1)


def _unpad_in(v):
    return jnp.concatenate([v[:, :O_F + N_HEADS], v[:, O_P:]], axis=1)


def _shards_in(v):
    gap = O_P - (O_F + N_HEADS)
    parts = []
    for a in range(N_CHIPS):
        lo, hi = a * SHARD_IN, (a + 1) * SHARD_IN
        cut = O_F + N_HEADS
        if hi <= cut:
            parts.append(v[:, lo:hi])
        elif lo >= cut:
            parts.append(v[:, lo + gap:hi + gap])
        else:
            parts.append(jnp.concatenate([v[:, lo:cut], v[:, cut + gap:hi + gap]], axis=1))
    return jnp.stack(parts, axis=0)


def kernel(x, c, w_ada, b_ada, w_in, b_in, w_pool_mix, b_pool_mix, pool_scale, w_out, b_out, ln_g, ln_b, loss_target, m_w_ada, m_b_ada, m_w_in, m_b_in, m_w_pool_mix, m_b_pool_mix, m_pool_scale, m_w_out, m_b_out, m_ln_g, m_ln_b, v_w_ada, v_b_ada, v_w_in, v_b_in, v_w_pool_mix, v_b_pool_mix, v_pool_scale, v_w_out, v_b_out, v_ln_g, v_ln_b):
    S = x.shape[1]
    T = min(T_ATT, S)
    n_t = S // T
    chip = 2 * lax.axis_index("x") + lax.axis_index("y")
    x2 = x[0]
    tgt = loss_target[0]
    q_scale = jnp.concatenate([jnp.full((1, D_ATT), Q_SCALE, F32), jnp.ones((1, D_PAD - D_ATT), F32)], axis=1)

    to_cols = lambda a: jnp.transpose(a, (2, 0, 1))
    from_cols = lambda a: jnp.transpose(a, (1, 2, 0))
    c_all, ada4, wt_pad = _gather_and_ada(
        c, w_ada[0], b_ada.reshape(4, 1, SHARD_ADA), to_cols(w_in).reshape(SHARD_IN, D).astype(BF16))
    ada = ada4[:, 0, :].reshape(1, 3 * D)
    shift, scale, gate = ada[:, :D], ada[:, D:2 * D], ada[:, 2 * D:]
    b_pad = _pad_in(b_in) * q_scale
    w_mix_bf = w_pool_mix[0].astype(BF16)

    u, qkv, f, p, g, w_out_all = _in_proj(x2, shift, scale, wt_pad, b_pad, w_out[0].astype(BF16))
    w_out_full = w_out_all.reshape(D, D)
    big_f = _forget_cumsum(f)
    att, lse = _attention_fwd(qkv, big_f)

    dh, datt, dg, dpl, gw_out, gw_mix, vec, loss_part = _middle(
        x2, tgt, att, g, p, gate, w_mix_bf, b_pool_mix.reshape(1, D_POOL), pool_scale, w_out_full, b_out, ln_g, ln_b)
    dq, dk, dv, cs_att, dfk, dfq, g_w_out = _attention_bwd(
        qkv, datt, att, lse, big_f, gw_out.reshape(N_CHIPS, SHARD_OUT, D), jnp.ones((N_CHIPS, 1, D), F32))
    dp, df, cs_tail = _tail(dpl, dfk, dfq, f)
    pieces = (dq, dk, dv, df, dp, dg)
    gw_pad = _grad_w_in(u, pieces)
    grad_x, vec_x = _grad_x(pieces, wt_pad, dh, x2, scale)

    cs_qkv = jnp.transpose(cs_att.reshape(N_PAIR, 3, 128), (1, 0, 2)).reshape(1, 3 * D_ATT)
    gb_pad = jnp.concatenate([cs_qkv, cs_tail[1:2, 0:128], cs_tail[0:1, :], vec[4:5, :]], axis=1) * q_scale
    dada = jnp.concatenate([vec_x[0:1, :], vec_x[1:2, :], vec[2:3, :]], axis=1)
    small = _pack_small({
        "b_in": _unpad_in(gb_pad), "w_pool_mix": gw_mix, "b_pool_mix": vec[6:7, :D_POOL],
        "pool_scale": vec[5:6, :D_POOL], "b_out": vec[3:4, :], "ln_g": vec[0:1, :], "ln_b": vec[1:2, :],
        "loss": loss_part})

    g_w_in, small_sum, dada_all = _reduce_all(
        gw_pad, _shards_in(q_scale), small, dada)
    dada_cols = lax.dynamic_slice(dada_all[:, 0, :], (0, chip * SHARD_ADA), (8, SHARD_ADA))
    g_w_ada, g_b_ada = _grad_ada(c_all, dada_all, dada_cols)
    loss = _unpack_small(small_sum, "loss", (1,))[0]

    big = _adamw([(w_ada[0], g_w_ada, m_w_ada[0], v_w_ada[0]),
                  (w_out[0], g_w_out, m_w_out[0], v_w_out[0])], 8)
    g_w_in_cols = g_w_in
    big_in = _adamw([(to_cols(w_in), g_w_in_cols, to_cols(m_w_in), to_cols(v_w_in))], 14)
    tiles = lambda a: a.reshape(4 * POOL_GROUP, POOL_GROUP)
    flat = lambda a: a.reshape(1, D_POOL)
    small_params = [("b_ada", b_ada, m_b_ada, v_b_ada), ("b_in", b_in, m_b_in, v_b_in),
                    ("w_pool_mix", tiles(w_pool_mix), tiles(m_w_pool_mix), tiles(v_w_pool_mix)),
                    ("b_pool_mix", flat(b_pool_mix), flat(m_b_pool_mix), flat(v_b_pool_mix)),
                    ("pool_scale", pool_scale, m_pool_scale, v_pool_scale), ("b_out", b_out, m_b_out, v_b_out),
                    ("ln_g", ln_g, m_ln_g, v_ln_g), ("ln_b", ln_b, m_ln_b, v_ln_b)]
    sm = _adamw_small(small_sum, g_b_ada, small_params)
    sm_idx = {p[0]: n for n, p in enumerate(small_params)}
    shapes = {"w_pool_mix": (1, 4, POOL_GROUP, POOL_GROUP), "b_pool_mix": (1, 4, POOL_GROUP)}

    names = ["w_ada", "b_ada", "w_in", "b_in", "w_pool_mix", "b_pool_mix", "pool_scale", "w_out", "b_out",
             "ln_g", "ln_b"]
    big_idx = {"w_ada": 0, "w_out": 1}

    def leaf(kind, name):
        if name == "w_in":
            return from_cols(g_w_in_cols if kind == 0 else big_in[kind - 1])
        if name in big_idx:
            if kind == 0:
                return (g_w_ada, g_w_out)[big_idx[name]][None]
            return big[3 * big_idx[name] + kind - 1][None]
        val = sm[4 * sm_idx[name] + kind]
        return val.reshape(shapes[name]) if name in shapes else val

    outs = [loss, grad_x[None]]
    for kind in range(4):
        outs += [leaf(kind, n) for n in names]
    return tuple(outs)
```

```python
import functools

import numpy as np
import jax
import jax.numpy as jnp
from jax import lax
from jax.experimental import pallas as pl
from jax.experimental.pallas import tpu as pltpu

F32 = jnp.float32
BF16 = jnp.bfloat16
MESH = pl.DeviceIdType.MESH

D = 1024
D_ATT = 512
D_POOL = 512
N_HEADS = 8
HEAD_DIM = 64
N_PAIR = N_HEADS // 2
POOL_WINDOWS = (2, 4, 8, 16)
POOL_GROUP = 128
POOL_HALO = 16
LN_EPS = 1e-5
ALPHA = 2.0 ** 0.25
D_IN = 3 * D_ATT + N_HEADS + D_POOL + D_ATT + D_POOL
N_CHIPS = 4
SHARD_IN = D_IN // N_CHIPS
SHARD_ADA = 3 * D // N_CHIPS
SHARD_OUT = D // N_CHIPS

O_QKV, O_F, O_P, O_G, D_PAD = 0, 1536, 1664, 2176, 3200
Q_SCALE = HEAD_DIM ** -0.5

ADAM_LR, ADAM_B1, ADAM_B2, ADAM_EPS, ADAM_WD, ADAM_STEP = 0.001, 0.9, 0.999, 1e-08, 0.01, 10

NEG = -1e30

VMEM_LIMIT = 56 * 1024 * 1024

TM_PROJ = 512
T_ATT = 512
TM_MID = 256
TM_GW = 1024
TM_DU = 512

REL7 = [(0, 0, 1), (0, 1, 0), (0, 1, 1), (1, 0, 0), (1, 0, 1), (1, 1, 0), (1, 1, 1)]
REL3 = [(0, 1), (1, 0), (1, 1)]

SMALL_SEGS = {}
_row = 0
for _name, _n in (("b_in", D_IN), ("w_pool_mix", 65536), ("b_pool_mix", 512), ("pool_scale", 512),
                  ("b_out", 1024), ("ln_g", 1024), ("ln_b", 1024), ("loss", 1)):
    _rows = -(-_n // 1024) * 8
    SMALL_SEGS[_name] = (_row, _rows)
    _row += _rows
SMALL_ROWS = -(-_row // 16) * 16


def _params(**kw):
    return pltpu.CompilerParams(vmem_limit_bytes=VMEM_LIMIT, **kw)


def _flip(v, d):
    return v if d == 0 else 1 - v


def _dot(a, b):
    return jnp.dot(a, b, preferred_element_type=F32)


def _dot_nt(a, b):
    return lax.dot_general(a, b, (((1,), (1,)), ((), ())), preferred_element_type=F32)


def _dot_tn(a, b):
    return lax.dot_general(a, b, (((0,), (0,)), ((), ())), preferred_element_type=F32)


def _sigmoid(v):
    return 1.0 / (1.0 + jnp.exp(-v))


def _colsum(v):
    return jnp.sum(v, axis=0, keepdims=True)


def _gather_stages(pos, src_ref, dst_ref, half, own_sem, s_sem, r_sem, fs_sem, fr_sem):
    x, y, cc, chip, sib = pos
    own = pltpu.make_async_copy(src_ref, dst_ref.at[chip], own_sem)
    first, landed, others = [], [], []
    for k, (dx, dy) in enumerate(REL3):
        px, py = _flip(x, dx), _flip(y, dy)
        first.append(pltpu.make_async_remote_copy(
            src_ref=src_ref.at[half(cc)], dst_ref=dst_ref.at[(chip,) + half(cc)],
            send_sem=s_sem.at[k], recv_sem=r_sem.at[k], device_id=(px, py, cc), device_id_type=MESH))
        landed.append(dst_ref.at[(2 * px + py,) + half(cc)])
        others.append(dst_ref.at[(2 * px + py,) + half(1 - cc)])
    passed = [pltpu.make_async_remote_copy(src_ref=landed[k], dst_ref=landed[k], send_sem=fs_sem.at[k],
                                           recv_sem=fr_sem.at[k], device_id=sib, device_id_type=MESH)
              for k in range(3)]

    def start(finish_src=None):
        for cp in first:
            cp.start()
        if finish_src is not None:
            finish_src()
        own.start()

    def forward():
        for k in range(3):
            pltpu.make_async_remote_copy(src_ref=landed[k], dst_ref=landed[k], send_sem=s_sem.at[k],
                                         recv_sem=r_sem.at[k], device_id=sib, device_id_type=MESH).wait_recv()
            passed[k].start()

    def finish():
        for k in range(3):
            pltpu.make_async_remote_copy(src_ref=others[k], dst_ref=others[k], send_sem=fs_sem.at[k],
                                         recv_sem=fr_sem.at[k], device_id=sib, device_id_type=MESH).wait_recv()
        for cp in first + passed:
            cp.wait_send()
        own.wait()

    return start, forward, finish


def _gather_scratch():
    return [pltpu.SemaphoreType.DMA, pltpu.SemaphoreType.DMA((3,)), pltpu.SemaphoreType.DMA((3,)),
            pltpu.SemaphoreType.DMA((3,)), pltpu.SemaphoreType.DMA((3,))]


def _gather_and_ada(c, w_ada, b_ada4, w_in_sh):
    def body(c_ref, w_ref, b_ref, win_ref, call_ref, ada_ref, wt_pad_ref,
             win_all, win_bf, cslab, sbuf, rbuf, cs_sem, cr_sem, as_sem, ar_sem, *gather_sems):
        x, y, cc = lax.axis_index("x"), lax.axis_index("y"), lax.axis_index("c")
        me = 4 * x + 2 * y + cc
        chip = 2 * x + y
        lane_half = lambda which: (slice(None), pl.ds(pl.multiple_of(which * (D // 2), D // 2), D // 2))
        def round_half(which):
            for h in range(2):
                @pl.when(which == h)
                def _():
                    lanes = slice(h * (D // 2), (h + 1) * (D // 2))
                    win_bf[:, lanes] = win_ref[:, 0, lanes].astype(BF16)

        start, forward, finish = _gather_stages((x, y, cc, chip, (x, y, 1 - cc)), win_bf, win_all, lane_half,
                                                *gather_sems)
        round_half(cc)
        start(lambda: round_half(1 - cc))

        cslab[...] = jnp.broadcast_to(c_ref[...], (8, D))
        call_ref[me] = cslab[...]
        gathers = []
        for k, (dx, dy, dc) in enumerate(REL7):
            cp = pltpu.make_async_remote_copy(
                src_ref=cslab, dst_ref=call_ref.at[me], send_sem=cs_sem.at[k], recv_sem=cr_sem.at[k],
                device_id=(_flip(x, dx), _flip(y, dy), _flip(cc, dc)), device_id_type=MESH)
            cp.start()
            gathers.append(cp)
        for cp in gathers:
            cp.wait()
        slab_row = lax.broadcasted_iota(jnp.int32, (8, 1), 0)
        mat = jnp.zeros((8, D), F32)
        for r in range(8):
            mat = jnp.where(slab_row == r, call_ref[r], mat)
        act = (mat * _sigmoid(mat)).astype(BF16)
        part = _dot(act, w_ref[...].astype(BF16))
        sends = []
        for k, (dx, dy) in enumerate(REL3):
            px, py = _flip(x, dx), _flip(y, dy)
            r = 4 * px + 2 * py + cc
            piece = _colsum(jnp.where(slab_row == r, part, 0.0))
            sbuf[k] = jnp.broadcast_to(piece, (8, SHARD_ADA))
            cp = pltpu.make_async_remote_copy(
                src_ref=sbuf.at[k], dst_ref=rbuf.at[k], send_sem=as_sem.at[k], recv_sem=ar_sem.at[k],
                device_id=(px, py, cc), device_id_type=MESH)
            cp.start()
            sends.append(cp)
        own_piece = _colsum(jnp.where(slab_row == me, part, 0.0))
        ada_ref[chip] = jnp.broadcast_to(own_piece, (8, SHARD_ADA)) + b_ref[chip]
        for k, (dx, dy) in enumerate(REL3):
            sends[k].wait()
            a = 2 * _flip(x, dx) + _flip(y, dy)
            ada_ref[a] = rbuf[k] + b_ref[a]

        forward()
        finish()
        n_real = 3 * D_ATT + N_HEADS
        for a in range(N_CHIPS):
            lo, hi = a * SHARD_IN, (a + 1) * SHARD_IN
            for s0, s1 in ((lo, min(hi, D_ATT)), (max(lo, D_ATT), min(hi, n_real)), (max(lo, n_real), hi)):
                if s0 < s1:
                    rows = win_all[a, s0 - lo:s1 - lo, :]
                    if s1 <= D_ATT:
                        rows = rows * jnp.asarray(Q_SCALE, BF16)
                    shift = O_P - n_real if s0 >= n_real else 0
                    wt_pad_ref[s0 + shift:s1 + shift, :] = rows
        wt_pad_ref[n_real:O_P, :] = jnp.zeros((O_P - n_real, D), BF16)

    vm = pl.BlockSpec(memory_space=pltpu.VMEM)
    return pl.pallas_call(
        body, name="gather_and_ada",
        out_shape=(jax.ShapeDtypeStruct((8, 8, D), F32), jax.ShapeDtypeStruct((4, 8, SHARD_ADA), F32),
                   jax.ShapeDtypeStruct((D_PAD, D), BF16)),
        in_specs=[vm] * 4, out_specs=(vm,) * 3,
        scratch_shapes=[pltpu.VMEM((N_CHIPS, SHARD_IN, D), BF16), pltpu.VMEM((SHARD_IN, D), BF16),
                        pltpu.VMEM((8, D), F32), pltpu.VMEM((3, 8, SHARD_ADA), F32),
                        pltpu.VMEM((3, 8, SHARD_ADA), F32),
                        pltpu.SemaphoreType.DMA((7,)), pltpu.SemaphoreType.DMA((7,)),
                        pltpu.SemaphoreType.DMA((3,)), pltpu.SemaphoreType.DMA((3,))] + _gather_scratch(),
        compiler_params=_params(),
    )(c, w_ada, b_ada4, w_in_sh)


def _shard_cols():
    cut, gap = O_F + N_HEADS, O_P - (O_F + N_HEADS)
    out = []
    for a in range(N_CHIPS):
        lo, hi = a * SHARD_IN, (a + 1) * SHARD_IN
        out.append(([(lo, min(hi, cut))] if lo < cut else []) + ([(max(lo, cut) + gap, hi + gap)] if hi > cut else []))
    return out


def _scatter_stages(pos, g_ref, sc_ref, out_ref, sib_buf, send_buf, ici_buf, sem1, sem2s, sem2r, sem3, part=(0, 1),
                    cols=None, own_buf=None):
    x, y, cc, chip, sib = pos
    q, n_parts = part
    RH = (g_ref.shape[1] if cols is None else g_ref.shape[0]) // 2 // n_parts
    mine = pl.ds(pl.multiple_of((cc * n_parts + q) * RH, RH), RH)
    theirs = pl.ds(pl.multiple_of(((1 - cc) * n_parts + q) * RH, RH), RH)
    cp1 = pltpu.make_async_remote_copy(
        src_ref=g_ref.at[:, theirs, :] if cols is None else g_ref.at[theirs, :], dst_ref=sib_buf,
        send_sem=sem1.at[0], recv_sem=sem1.at[1], device_id=sib, device_id_type=MESH)
    sends = []
    for k, (dx, dy) in enumerate(REL3):
        px, py = _flip(x, dx), _flip(y, dy)
        sends.append(pltpu.make_async_remote_copy(
            src_ref=send_buf.at[2 * px + py], dst_ref=ici_buf.at[chip],
            send_sem=sem2s.at[k], recv_sem=sem2r.at[k], device_id=(px, py, cc), device_id_type=MESH))
    cp3 = pltpu.make_async_remote_copy(
        src_ref=out_ref.at[mine, :], dst_ref=out_ref.at[mine, :], send_sem=sem3.at[0], recv_sem=sem3.at[1],
        device_id=sib, device_id_type=MESH)

    def finish1():
        cp1.wait()
        if cols is None:
            for a in range(N_CHIPS):
                both = g_ref[a, mine, :] + sib_buf[a]
                sib_buf[a] = both
                send_buf[a] = both.astype(BF16)
        else:
            both = g_ref[mine, :] + sib_buf[...]
            for a, pieces in enumerate(cols):
                at = 0
                for lo, hi in pieces:
                    own_buf[a, :, at:at + hi - lo] = both[:, lo:hi]
                    send_buf[a, :, at:at + hi - lo] = both[:, lo:hi].astype(BF16)
                    at += hi - lo

    def start2():
        for cp in sends:
            cp.start()
        ici_buf[chip] = send_buf[chip]

    def finish2():
        for cp in sends:
            cp.wait()
        own = (sib_buf if cols is None else own_buf)[chip]
        parts = [jnp.where(chip == a, own, ici_buf[a].astype(F32)) for a in range(N_CHIPS)]
        out_ref[mine, 0:own.shape[1]] = ((parts[0] + parts[1]) + (parts[2] + parts[3])) * sc_ref[chip]

    return [(cp1.start, finish1), (start2, finish2), (cp3.start, cp3.wait)]


def _all_reduce_stages(pos, g_ref, out_ref, sib_buf, ici_buf, sem1, sem2s, sem2r, sem3):
    x, y, cc, chip, sib = pos
    RH = g_ref.shape[0] // 2
    mine = pl.ds(pl.multiple_of(cc * RH, 8), RH)
    theirs = pl.ds(pl.multiple_of((1 - cc) * RH, 8), RH)
    cp1 = pltpu.make_async_remote_copy(
        src_ref=g_ref.at[theirs, :], dst_ref=sib_buf, send_sem=sem1.at[0], recv_sem=sem1.at[1],
        device_id=sib, device_id_type=MESH)
    sends = []
    for k, (dx, dy) in enumerate(REL3):
        px, py = _flip(x, dx), _flip(y, dy)
        sends.append(pltpu.make_async_remote_copy(
            src_ref=sib_buf, dst_ref=ici_buf.at[chip],
            send_sem=sem2s.at[k], recv_sem=sem2r.at[k], device_id=(px, py, cc), device_id_type=MESH))
    cp3 = pltpu.make_async_remote_copy(
        src_ref=out_ref.at[mine, :], dst_ref=out_ref.at[mine, :], send_sem=sem3.at[0], recv_sem=sem3.at[1],
        device_id=sib, device_id_type=MESH)

    def finish1():
        cp1.wait()
        sib_buf[...] = g_ref[mine, :] + sib_buf[...]

    def start2():
        for cp in sends:
            cp.start()
        ici_buf[chip] = sib_buf[...]

    def finish2():
        for cp in sends:
            cp.wait()
        out_ref[mine, :] = (ici_buf[0] + ici_buf[1]) + (ici_buf[2] + ici_buf[3])

    return [(cp1.start, finish1), (start2, finish2), (cp3.start, cp3.wait)]


def _stage_sems():
    return [pltpu.SemaphoreType.DMA((2,)), pltpu.SemaphoreType.DMA((3,)),
            pltpu.SemaphoreType.DMA((3,)), pltpu.SemaphoreType.DMA((2,))]


def _scatter_scratch(r, c):
    return [pltpu.VMEM((N_CHIPS, r // 2, c), F32), pltpu.VMEM((N_CHIPS, r // 2, c), BF16),
            pltpu.VMEM((N_CHIPS, r // 2, c), BF16)] + _stage_sems()


def _reduce_all(gw_pad, sc_in, small, dada):
    R = small.shape[0]
    W = dada.shape[1]
    r_in, p_in = gw_pad.shape
    c_in = SHARD_IN
    chunk = r_in // 4

    def chunk_scratch():
        return ([pltpu.VMEM((chunk, p_in), F32), pltpu.VMEM((N_CHIPS, chunk, c_in), BF16),
                 pltpu.VMEM((N_CHIPS, chunk, c_in), BF16)] + _stage_sems()
                + [pltpu.VMEM((N_CHIPS, chunk, c_in), F32)])

    n_in = len(chunk_scratch())

    c_wide = -(-c_in // 128) * 128

    def body(gin_ref, scin_ref, sm_ref, d_ref, ocols_ref, osm_ref, dall_ref, oin_ref, *scratch):
        x, y, cc = lax.axis_index("x"), lax.axis_index("y"), lax.axis_index("c")
        me = 4 * x + 2 * y + cc
        pos = (x, y, cc, 2 * x + y, (x, y, 1 - cc))
        oin_ref[:, c_in:c_wide] = jnp.zeros((r_in, c_wide - c_in), F32)
        dslab, ds_sem, dr_sem = scratch[0:3]
        a_bufs, b_bufs, sm_bufs = scratch[3:3 + n_in], scratch[3 + n_in:3 + 2 * n_in], scratch[3 + 2 * n_in:]
        dslab[...] = jnp.broadcast_to(d_ref[...], (8, W))
        dall_ref[me] = dslab[...]
        gathers = []
        for k, (dx, dy, dc) in enumerate(REL7):
            cp = pltpu.make_async_remote_copy(
                src_ref=dslab, dst_ref=dall_ref.at[me], send_sem=ds_sem.at[k], recv_sem=dr_sem.at[k],
                device_id=(_flip(x, dx), _flip(y, dy), _flip(cc, dc)), device_id_type=MESH)
            cp.start()
            gathers.append(cp)
        cols = _shard_cols()
        first = _scatter_stages(pos, gin_ref, scin_ref, oin_ref, *a_bufs[:-1], part=(0, 2), cols=cols,
                                own_buf=a_bufs[-1])
        second = _scatter_stages(pos, gin_ref, scin_ref, oin_ref, *b_bufs[:-1], part=(1, 2), cols=cols,
                                 own_buf=b_bufs[-1])
        little = _all_reduce_stages(pos, sm_ref, osm_ref, *sm_bufs)
        for plan in (first, second, little):
            plan[0][0]()
        first[0][1]()
        first[1][0]()
        little[0][1]()
        little[1][0]()
        second[0][1]()
        second[1][0]()
        first[1][1]()
        first[2][0]()
        second[1][1]()
        second[2][0]()
        little[1][1]()
        little[2][0]()
        for plan in (first, second, little):
            plan[2][1]()
        ocols_ref[...] = oin_ref[...].T[0:c_in, :][:, None, :]
        for cp in gathers:
            cp.wait()

    scratch = [pltpu.VMEM((r_in, c_wide), F32),
               pltpu.VMEM((8, W), F32), pltpu.SemaphoreType.DMA((7,)), pltpu.SemaphoreType.DMA((7,))]
    scratch += chunk_scratch() + chunk_scratch()
    scratch += [pltpu.VMEM((R // 2, 128), F32), pltpu.VMEM((N_CHIPS, R // 2, 128), F32)] + _stage_sems()
    vm = pl.BlockSpec(memory_space=pltpu.VMEM)
    return pl.pallas_call(
        body, name="reduce_all",
        out_shape=(jax.ShapeDtypeStruct((c_in, 1, r_in), F32),
                   jax.ShapeDtypeStruct((R, 128), F32), jax.ShapeDtypeStruct((8, 8, W), F32)),
        in_specs=[vm] * 4, out_specs=(vm,) * 3,
        scratch_shapes=scratch,
        compiler_params=_params(),
    )(gw_pad, sc_in, small, dada)


def _in_proj(x, shift, scale, wt_pad, b_pad, w_out_sh):
    S = x.shape[0]
    tm = min(TM_PROJ, S)
    n_steps = S // tm
    assert n_steps >= 3

    def body(x_ref, sh_ref, sc_ref, w_ref, b_ref, wo_ref, u_ref, qkv_ref, f_ref, p_ref, g_ref, wo_all,
             wo_buf, *gather_sems):
        i = pl.program_id(0)
        xx, yy, cc = lax.axis_index("x"), lax.axis_index("y"), lax.axis_index("c")
        row_half = lambda which: (pl.ds(pl.multiple_of(which * (SHARD_OUT // 2), SHARD_OUT // 2), SHARD_OUT // 2),
                                  slice(None))
        start, forward, finish = _gather_stages((xx, yy, cc, 2 * xx + yy, (xx, yy, 1 - cc)), wo_ref, wo_buf,
                                                row_half, *gather_sems)
        pl.when(i == 0)(start)
        pl.when(i == n_steps // 2)(forward)

        @pl.when(i == n_steps - 1)
        def _():
            finish()
            wo_all[...] = wo_buf[...]

        u = (x_ref[...] * (1.0 + sc_ref[...]) + sh_ref[...]).astype(BF16)
        u_ref[...] = u
        qkv_ref[...] = (_dot_nt(u, w_ref[O_QKV:O_F, :]) + b_ref[:, O_QKV:O_F]).astype(BF16)
        f_ref[...] = _dot_nt(u, w_ref[O_F:O_P, :]) + b_ref[:, O_F:O_P]
        p_ref[...] = _dot_nt(u, w_ref[O_P:O_G, :]) + b_ref[:, O_P:O_G]
        g_ref[...] = _dot_nt(u, w_ref[O_G:D_PAD, :]) + b_ref[:, O_G:D_PAD]

    row = lambda w: pl.BlockSpec((tm, w), lambda i: (i, 0))
    full = lambda a: pl.BlockSpec(a.shape, lambda i: (0, 0))
    vm = pl.BlockSpec(memory_space=pltpu.VMEM)
    return pl.pallas_call(
        body, name="in_proj", grid=(n_steps,),
        out_shape=(jax.ShapeDtypeStruct((S, D), BF16), jax.ShapeDtypeStruct((S, 3 * D_ATT), BF16),
                   jax.ShapeDtypeStruct((S, 128), F32), jax.ShapeDtypeStruct((S, D_POOL), F32),
                   jax.ShapeDtypeStruct((S, D), F32), jax.ShapeDtypeStruct((N_CHIPS,) + w_out_sh.shape, BF16)),
        in_specs=[row(D), full(shift), full(scale), full(wt_pad), full(b_pad), vm],
        out_specs=(row(D), row(3 * D_ATT), row(128), row(D_POOL), row(D), vm),
        scratch_shapes=[pltpu.VMEM((N_CHIPS,) + w_out_sh.shape, BF16)] + _gather_scratch(),
        compiler_params=_params(dimension_semantics=("arbitrary",)),
    )(x, shift, scale, wt_pad, b_pad, w_out_sh)


def _forget_cumsum(f):
    S = f.shape[0]
    tm = min(T_ATT, S)

    def body(f_ref, out_ref, carry):
        @pl.when(pl.program_id(0) == 0)
        def _():
            carry[...] = jnp.zeros_like(carry)
        v = f_ref[...]
        logf = jnp.minimum(v, 0.0) - jnp.log(1.0 + jnp.exp(-jnp.abs(v)))
        r = lax.broadcasted_iota(jnp.int32, (tm, tm), 0)
        c = lax.broadcasted_iota(jnp.int32, (tm, tm), 1)
        tri = (r <= c).astype(F32)
        rows8 = logf.T[0:8, :]
        cum8 = jnp.dot(rows8, tri, preferred_element_type=F32, precision=lax.Precision.HIGHEST) + carry[...]
        out_ref[...] = jnp.concatenate([cum8, jnp.zeros((128 - 8, tm), F32)], axis=0).T
        last = lax.broadcasted_iota(jnp.int32, (1, tm), 1) == tm - 1
        carry[...] = jnp.sum(jnp.where(last, cum8, 0.0), axis=1, keepdims=True)

    return pl.pallas_call(
        body, name="forget_cumsum", grid=(S // tm,),
        out_shape=jax.ShapeDtypeStruct((S, 128), F32),
        in_specs=[pl.BlockSpec((tm, 128), lambda i: (i, 0))],
        out_specs=pl.BlockSpec((tm, 128), lambda i: (i, 0)),
        scratch_shapes=[pltpu.VMEM((8, 1), F32)],
        compiler_params=_params(dimension_semantics=("arbitrary",)),
    )(f)


def _split3(v):
    hi = v.astype(BF16)
    rest = v - hi.astype(F32)
    mid = rest.astype(BF16)
    lo = (rest - mid.astype(F32)).astype(BF16)
    return hi, mid, lo


def _attention_fwd(qkv, big_f):
    S = qkv.shape[0]
    T = min(T_ATT, S)
    n_t = S // T

    def body(q_ref, k_ref, v_ref, f_ref, o_ref, lse_ref, kaug_sc, vt_sc, m_sc, l_sc, acc_sc):
        hp = pl.program_id(0)
        i = pl.program_id(1)
        lane = lax.broadcasted_iota(jnp.int32, (1, 128), 1)
        sub = lax.broadcasted_iota(jnp.int32, (128, 1), 0)
        head_sel = (lane < HEAD_DIM, lane >= HEAD_DIM)
        head_sel_t = (sub < HEAD_DIM, sub >= HEAD_DIM)
        spare = (HEAD_DIM, 0)
        zero = jnp.zeros((), BF16)

        @pl.when(i == 0)
        def _():
            def prep(jt, carry):
                rows = pl.ds(pl.multiple_of(jt * T, T), T)
                k = k_ref[rows, :]
                ft = f_ref[rows, :]
                vt = v_ref[rows, :].astype(F32).T
                for h in range(2):
                    fh = jnp.sum(jnp.where(lane == 2 * hp + h, ft, 0.0), axis=1, keepdims=True)
                    hi, mid, lo = _split3(-fh)
                    b = spare[h]
                    bias = jnp.where(lane == b, hi, jnp.where(lane == b + 1, mid, jnp.where(lane == b + 2, lo, zero)))
                    kaug_sc[h, rows, :] = jnp.where(head_sel[h], k, bias)
                    vt_sc[h, jt] = jnp.where(head_sel_t[h], vt, 0.0).astype(BF16)
                return carry

            lax.fori_loop(0, n_t, prep, 0)

        q = q_ref[...]
        q_heads = []
        for h in range(2):
            ones = jnp.where((lane >= spare[h]) & (lane < spare[h] + 3), jnp.ones((), BF16), zero)
            q_heads.append(jnp.where(head_sel[h], q, ones))
        m_sc[...] = jnp.full((8, T), NEG, F32)
        l_sc[...] = jnp.zeros((8, T), F32)
        acc_sc[...] = jnp.zeros((128, T), F32)

        def update(j, k_lo, n_k, q_lo, masked):
            rows = pl.ds(pl.multiple_of(j * T + k_lo, n_k), n_k)
            n_q = T - q_lo
            alphas, pvs = [], []
            for h in range(2):
                s_t = _dot_nt(kaug_sc[h, rows, :], q_heads[h][q_lo:, :])
                if masked:
                    rr = lax.broadcasted_iota(jnp.int32, (n_k, n_q), 0) + k_lo
                    cc = lax.broadcasted_iota(jnp.int32, (n_k, n_q), 1) + q_lo
                    s_t = jnp.where(rr <= cc, s_t, NEG)
                m_prev = m_sc[h:h + 1, q_lo:]
                m_new = jnp.maximum(m_prev, jnp.max(s_t, axis=0, keepdims=True))
                alpha = jnp.exp(m_prev - m_new)
                p_t = jnp.exp(s_t - m_new)
                l_sc[h:h + 1, q_lo:] = alpha * l_sc[h:h + 1, q_lo:] + jnp.sum(p_t, axis=0, keepdims=True)
                m_sc[h:h + 1, q_lo:] = m_new
                alphas.append(alpha)
                pvs.append(_dot(vt_sc[h, j, :, k_lo:k_lo + n_k], p_t.astype(BF16)))
            acc_sc[:, q_lo:] = (acc_sc[:, q_lo:] * jnp.where(head_sel_t[0], alphas[0], alphas[1])
                                + (pvs[0] + pvs[1]))

        def two_off_diagonal(jj, carry):
            update(2 * jj, 0, T, 0, False)
            update(2 * jj + 1, 0, T, 0, False)
            return carry

        lax.fori_loop(0, i // 2, two_off_diagonal, 0)

        @pl.when(i % 2 == 1)
        def _():
            update(i - 1, 0, T, 0, False)

        update(i, 0, T, 0, True)
        l = l_sc[...]
        o_ref[...] = (acc_sc[...] / jnp.where(head_sel_t[0], l[0:1, :], l[1:2, :])).T
        is_head = lax.broadcasted_iota(jnp.int32, (8, 1), 0) < 2
        lse_ref[...] = jnp.where(is_head, m_sc[...] + jnp.log(jnp.where(is_head, l, 1.0)), 0.0)

    return pl.pallas_call(
        body, name="attention_fwd", grid=(N_PAIR, n_t),
        out_shape=(jax.ShapeDtypeStruct((S, D_ATT), F32), jax.ShapeDtypeStruct((N_PAIR, n_t, 8, T), F32)),
        in_specs=[pl.BlockSpec((T, 128), lambda hp, i: (i, hp)),
                  pl.BlockSpec((S, 128), lambda hp, i: (0, N_PAIR + hp)),
                  pl.BlockSpec((S, 128), lambda hp, i: (0, 2 * N_PAIR + hp)),
                  pl.BlockSpec((S, 128), lambda hp, i: (0, 0))],
        out_specs=(pl.BlockSpec((T, 128), lambda hp, i: (i, hp)),
                   pl.BlockSpec((None, None, 8, T), lambda hp, i: (hp, i, 0, 0))),
        scratch_shapes=[pltpu.VMEM((2, S, 128), BF16), pltpu.VMEM((2, n_t, 128, T), BF16),
                        pltpu.VMEM((8, T), F32), pltpu.VMEM((8, T), F32), pltpu.VMEM((128, T), F32)],
        compiler_params=_params(dimension_semantics=("arbitrary", "arbitrary")),
    )(qkv, qkv, qkv, big_f)


def _attention_bwd(qkv, datt, att, lse, big_f, gw_out4, sc_out):
    S = qkv.shape[0]
    T = min(T_ATT, S)
    n_t = S // T
    n_steps = N_PAIR * n_t
    marks = (0, n_steps // 8, n_steps // 2, n_steps // 2 + n_steps // 8)

    def body(q_ref, do_ref, o_ref, lse_ref, k_ref, v_ref, fk_ref, gout_ref, scout_ref,
             dq_ref, dk_ref, dv_ref, cs_ref, dfk_ref, dfq_ref, oout_ref, stat_sc, dqt_sc, qaug_sc,
             out_buf, *red_bufs):
        hp = pl.program_id(0)
        j = pl.program_id(1)
        x, y, cc = lax.axis_index("x"), lax.axis_index("y"), lax.axis_index("c")
        plan = _scatter_stages((x, y, cc, 2 * x + y, (x, y, 1 - cc)), gout_ref, scout_ref, out_buf, *red_bufs)
        step = hp * n_t + j
        for n, mark in enumerate(marks):
            @pl.when(step == mark)
            def _(n=n):
                if n > 0:
                    plan[n - 1][1]()
                if n < 3:
                    plan[n][0]()
                else:
                    oout_ref[...] = out_buf[...]

        lane = lax.broadcasted_iota(jnp.int32, (1, 128), 1)
        sub = lax.broadcasted_iota(jnp.int32, (128, 1), 0)
        head_sel = (lane < HEAD_DIM, lane >= HEAD_DIM)
        head_sel_t = (sub < HEAD_DIM, sub >= HEAD_DIM)
        spare = (HEAD_DIM, 0)
        zero = jnp.zeros((), BF16)
        one = jnp.ones((), BF16)

        def bias_lanes(first, pieces):
            hi, mid, lo = pieces
            return lambda rest: jnp.where(lane == first, hi, jnp.where(lane == first + 1, mid,
                                                                        jnp.where(lane == first + 2, lo, rest)))

        @pl.when(j == 0)
        def _():
            dqt_sc[...] = jnp.zeros_like(dqt_sc)
            cs_ref[...] = jnp.zeros_like(cs_ref)
            dfq_ref[...] = jnp.zeros_like(dfq_ref)

            def prep(i, carry):
                rows = pl.ds(pl.multiple_of(i * T, T), T)
                q = q_ref[rows, :]
                do = do_ref[rows, :]
                prod = o_ref[rows, :] * do.astype(F32)
                d_a = jnp.sum(jnp.where(head_sel[0], prod, 0.0), axis=1, keepdims=True)
                d_b = jnp.sum(jnp.where(head_sel[0], 0.0, prod), axis=1, keepdims=True)
                delta_t = jnp.where(head_sel[0], d_a, d_b).T
                stat_sc[i, 0:1, :] = delta_t[0:1, :]
                stat_sc[i, 1:2, :] = delta_t[HEAD_DIM:HEAD_DIM + 1, :]
                lse = lse_ref[i]
                lse_cols = jnp.where(head_sel_t[0], lse[0:1, :], lse[1:2, :]).T
                for h in range(2):
                    neg_lse = -lse_cols[:, h * HEAD_DIM:h * HEAD_DIM + 1]
                    ones = jnp.where((lane >= spare[h]) & (lane < spare[h] + 3), one, zero)
                    qaug_sc[h, rows, :] = jnp.where(head_sel[h], q, bias_lanes(spare[h] + 3, _split3(neg_lse))(ones))
                return carry

            lax.fori_loop(0, n_t, prep, 0)

        k = k_ref[...]
        v = v_ref[...]
        fk = fk_ref[...]
        kt = k.astype(F32).T
        heads = []
        for h in range(2):
            fkh = jnp.sum(jnp.where(lane == 2 * hp + h, fk, 0.0), axis=1, keepdims=True)
            ones = jnp.where((lane >= spare[h] + 3) & (lane < spare[h] + 6), one, zero)
            kaug = jnp.where(head_sel[h], k, bias_lanes(spare[h], _split3(-fkh))(ones))
            heads.append((kaug, jnp.where(head_sel[h], v, zero), jnp.where(head_sel_t[h], kt, 0.0).astype(BF16)))

        def block(i, k_lo, n_k, q_lo, masked):
            n_q = T - q_lo
            rows = pl.ds(pl.multiple_of(i * T + q_lo, n_q), n_q)
            q = q_ref[rows, :]
            do = do_ref[rows, :]
            stat = stat_sc[i]
            dk = jnp.zeros((n_k, 128), F32)
            dv = jnp.zeros((n_k, 128), F32)
            dqt = jnp.zeros((128, n_q), F32)
            dfs = []
            for h in range(2):
                kaug, vh, kth = heads[h]
                arg = _dot_nt(kaug[k_lo:k_lo + n_k, :], qaug_sc[h, rows, :])
                if masked:
                    rr = lax.broadcasted_iota(jnp.int32, (n_k, n_q), 0) + k_lo
                    cc = lax.broadcasted_iota(jnp.int32, (n_k, n_q), 1) + q_lo
                    arg = jnp.where(rr <= cc, arg, NEG)
                p_t = jnp.exp(arg)
                ds_t = p_t * (_dot_nt(vh[k_lo:k_lo + n_k, :], do) - stat[h:h + 1, q_lo:])
                ds_bf = ds_t.astype(BF16)
                dv = dv + _dot(p_t.astype(BF16), jnp.where(head_sel[h], do, zero))
                dk = dk + _dot(ds_bf, jnp.where(head_sel[h], q, zero))
                dqt = dqt + _dot(kth[:, k_lo:k_lo + n_k], ds_bf)
                dfs.append(jnp.sum(ds_t, axis=1, keepdims=True))
                dfq_ref[i, h:h + 1, q_lo:] += _colsum(ds_t)
            dqt_sc[i, :, q_lo:] += dqt
            return dk, dv, dfs[0], dfs[1]

        def off_diagonal(i, acc):
            return tuple(a + b for a, b in zip(acc, block(i, 0, T, 0, False)))

        half = T // 2
        early = block(j, 0, half, 0, True)
        late = block(j, half, half, half, True)
        acc1 = tuple(jnp.concatenate([a, b], axis=0) for a, b in zip(early, late))
        n_off = n_t - 1 - j
        acc2 = lax.fori_loop(0, n_off // 2,
                             lambda ii, a: off_diagonal(j + 2 + 2 * ii, off_diagonal(j + 1 + 2 * ii, a)), acc1)
        dk_acc, dv_acc, dfa, dfb = lax.fori_loop(0, n_off % 2, lambda _, a: off_diagonal(n_t - 1, a), acc2)
        dk_ref[...] = dk_acc.astype(BF16)
        dv_ref[...] = dv_acc.astype(BF16)
        dfk_ref[...] = -jnp.where(lane == 0, dfa, jnp.where(lane == 1, dfb, 0.0))
        cs_ref[:, 128:256] = cs_ref[:, 128:256] + _colsum(dk_acc)
        cs_ref[:, 256:384] = cs_ref[:, 256:384] + _colsum(dv_acc)

        @pl.when(j == n_t - 1)
        def _():
            def finish(i, tot):
                dq = dqt_sc[i].T
                dq_ref[pl.ds(pl.multiple_of(i * T, T), T), :] = dq.astype(BF16)
                return tot + _colsum(dq)

            cs_ref[:, 0:128] = lax.fori_loop(0, n_t, finish, jnp.zeros((1, 128), F32))

    pair_rows = lambda hp, j: (hp, 0, 0)
    vm = pl.BlockSpec(memory_space=pltpu.VMEM)
    _, r_out, c_out = gw_out4.shape
    return pl.pallas_call(
        body, name="attention_bwd", grid=(N_PAIR, n_t),
        out_shape=(jax.ShapeDtypeStruct((S, D_ATT), BF16), jax.ShapeDtypeStruct((S, D_ATT), BF16),
                   jax.ShapeDtypeStruct((S, D_ATT), BF16), jax.ShapeDtypeStruct((N_PAIR, 1, 384), F32),
                   jax.ShapeDtypeStruct((N_PAIR, S, 128), F32),
                   jax.ShapeDtypeStruct((N_PAIR, n_t, 8, T), F32),
                   jax.ShapeDtypeStruct((r_out, c_out), F32)),
        in_specs=[pl.BlockSpec((S, 128), lambda hp, j: (0, hp)),
                  pl.BlockSpec((S, 128), lambda hp, j: (0, hp)),
                  pl.BlockSpec((S, 128), lambda hp, j: (0, hp)),
                  pl.BlockSpec((None, n_t, 8, T), lambda hp, j: (hp, 0, 0, 0)),
                  pl.BlockSpec((T, 128), lambda hp, j: (j, N_PAIR + hp)),
                  pl.BlockSpec((T, 128), lambda hp, j: (j, 2 * N_PAIR + hp)),
                  pl.BlockSpec((T, 128), lambda hp, j: (j, 0)),
                  vm, vm],
        out_specs=(pl.BlockSpec((S, 128), lambda hp, j: (0, hp)),
                   pl.BlockSpec((T, 128), lambda hp, j: (j, hp)),
                   pl.BlockSpec((T, 128), lambda hp, j: (j, hp)),
                   pl.BlockSpec((None, 1, 384), pair_rows),
                   pl.BlockSpec((None, T, 128), lambda hp, j: (hp, j, 0)),
                   pl.BlockSpec((None, n_t, 8, T), lambda hp, j: (hp, 0, 0, 0)),
                   vm),
        scratch_shapes=[pltpu.VMEM((n_t, 8, T), F32), pltpu.VMEM((n_t, 128, T), F32),
                        pltpu.VMEM((2, S, 128), BF16), pltpu.VMEM((r_out, c_out), F32)]
        + _scatter_scratch(r_out, c_out),
        compiler_params=_params(dimension_semantics=("arbitrary", "arbitrary")),
    )(qkv, datt, att, lse, qkv, qkv, big_f, gw_out4, sc_out)


def _window_counts(first_row, n_rows, window):
    t = lax.broadcasted_iota(jnp.int32, (n_rows, 1), 0) + first_row
    return jnp.minimum((t + 1).astype(F32), float(window))


def _middle(x, tgt, att, g, p, gate, w_mix, b_mix, pool_scale, w_out, b_out, ln_g, ln_b):
    S = x.shape[0]
    tm = min(TM_MID, S)
    halo_blocks = tm // POOL_HALO

    def body(x_ref, t_ref, att_ref, g_ref, p_ref, ph_ref, gate_ref, wm_ref, bm_ref, ps_ref, wo_ref, bo_ref,
             lg_ref, lb_ref,
             dh_ref, datt_ref, dg_ref, dpl_ref, gwo_ref, gwm_ref, vec_ref, loss_ref):
        i = pl.program_id(0)

        @pl.when(i == 0)
        def _():
            gwo_ref[...] = jnp.zeros_like(gwo_ref)
            gwm_ref[...] = jnp.zeros_like(gwm_ref)
            vec_ref[...] = jnp.zeros_like(vec_ref)
            loss_ref[...] = jnp.zeros_like(loss_ref)

        pc = p_ref[...]
        halo = jnp.where(i > 0, ph_ref[...], 0.0)
        pe = jnp.concatenate([halo, pc], axis=0)
        pooled_parts = []
        for gi, w in enumerate(POOL_WINDOWS):
            cur = pe[:, gi * POOL_GROUP:(gi + 1) * POOL_GROUP]
            span = 1
            while span < w:
                cur = cur + pltpu.roll(cur, span, 0)
                span *= 2
            wsum = cur[POOL_HALO:, :]
            mean = wsum / _window_counts(i * tm, tm, w)
            pooled_parts.append(mean - pc[:, gi * POOL_GROUP:(gi + 1) * POOL_GROUP])
        pooled_bf =[v.astype(BF16) for v in pooled_parts]
        mixed = jnp.concatenate([_dot(pooled_bf[gi], wm_ref[gi]) for gi in range(4)], axis=1) + bm_ref[...]
        ps = ps_ref[...]
        pool_out = mixed * ps
        gv = g_ref[...]
        sig = _sigmoid(gv)
        silu = gv * sig
        att = att_ref[...]
        y = jnp.concatenate([att * silu[:, :D_ATT], pool_out * silu[:, D_ATT:]], axis=1)
        y_bf = y.astype(BF16)
        wo = wo_ref[...]
        yo = _dot(y_bf, wo) + bo_ref[...]
        gate = gate_ref[...]
        h = ALPHA * x_ref[...] + gate * yo
        mu = jnp.mean(h, axis=1, keepdims=True)
        hc = h - mu
        var = jnp.mean(hc * hc, axis=1, keepdims=True)
        rstd = lax.rsqrt(var + LN_EPS)
        yhat = hc * rstd
        lg = lg_ref[...]
        out = yhat * lg + lb_ref[...]
        err = out - t_ref[...]
        loss_ref[...] += 0.5 * jnp.sum(jnp.mean(err * err, axis=1, keepdims=True), axis=0, keepdims=True)

        dout = err * (1.0 / D)
        g_ln_b = _colsum(dout)
        g_ln_g = _colsum(dout * yhat)
        dyh = dout * lg
        dh = rstd * (dyh - jnp.mean(dyh, axis=1, keepdims=True)
                     - yhat * jnp.mean(dyh * yhat, axis=1, keepdims=True))
        dh_ref[...] = dh
        d_gate = _colsum(dh * yo)
        dyo = gate * dh
        g_b_out = _colsum(dyo)
        dyo_bf = dyo.astype(BF16)
        gwo_ref[...] += _dot_tn(y_bf, dyo_bf)
        dy = _dot_nt(dyo_bf, wo)
        dsilu = sig * (1.0 + gv * (1.0 - sig))
        dy_a = dy[:, :D_ATT]
        dy_p = dy[:, D_ATT:]
        datt_ref[...] = (dy_a * silu[:, :D_ATT]).astype(BF16)
        dpo = dy_p * silu[:, D_ATT:]
        dg = jnp.concatenate([dy_a * att * dsilu[:, :D_ATT], dy_p * pool_out * dsilu[:, D_ATT:]], axis=1)
        dg_ref[...] = dg.astype(BF16)
        g_dg = _colsum(dg)
        g_ps = _colsum(dpo * mixed)
        dmixed = dpo * ps
        g_bm = _colsum(dmixed)
        dmixed_bf = dmixed.astype(BF16)
        dpl = []
        for gi in range(4):
            dm = dmixed_bf[:, gi * POOL_GROUP:(gi + 1) * POOL_GROUP]
            gwm_ref[gi] += _dot_tn(pooled_bf[gi], dm)
            dpl.append(_dot_nt(dm, wm_ref[gi]))
        dpl_ref[...] = jnp.concatenate(dpl, axis=1)
        vec_ref[0:1, :] += g_ln_g
        vec_ref[1:2, :] += g_ln_b
        vec_ref[2:3, :] += d_gate
        vec_ref[3:4, :] += g_b_out
        vec_ref[4:5, :] += g_dg
        vec_ref[5:6, 0:D_POOL] += g_ps
        vec_ref[6:7, 0:D_POOL] += g_bm

    row = lambda w: pl.BlockSpec((tm, w), lambda i: (i, 0))
    full2 = lambda a: pl.BlockSpec(a.shape, lambda i: (0, 0))
    full3 = lambda a: pl.BlockSpec(a.shape, lambda i: (0, 0, 0))
    return pl.pallas_call(
        body, name="middle", grid=(S // tm,),
        out_shape=(jax.ShapeDtypeStruct((S, D), F32),
                   jax.ShapeDtypeStruct((S, D_ATT), BF16),
                   jax.ShapeDtypeStruct((S, D), BF16),
                   jax.ShapeDtypeStruct((S, D_POOL), F32),
                   jax.ShapeDtypeStruct((D, D), F32),
                   jax.ShapeDtypeStruct((4, POOL_GROUP, POOL_GROUP), F32),
                   jax.ShapeDtypeStruct((8, D), F32),
                   jax.ShapeDtypeStruct((1, 1), F32)),
        in_specs=[row(D), row(D), row(D_ATT), row(D), row(D_POOL),
                  pl.BlockSpec((POOL_HALO, D_POOL), lambda i: (jnp.maximum(i * halo_blocks - 1, 0), 0)),
                  full2(gate), full3(w_mix), full2(b_mix), full2(pool_scale), full2(w_out), full2(b_out),
                  full2(ln_g), full2(ln_b)],
        out_specs=(row(D), row(D_ATT), row(D), row(D_POOL),
                   pl.BlockSpec((D, D), lambda i: (0, 0)),
                   pl.BlockSpec((4, POOL_GROUP, POOL_GROUP), lambda i: (0, 0, 0)),
                   pl.BlockSpec((8, D), lambda i: (0, 0)),
                   pl.BlockSpec((1, 1), lambda i: (0, 0))),
        compiler_params=_params(dimension_semantics=("arbitrary",)),
    )(x, tgt, att, g, p, p, gate, w_mix, b_mix, pool_scale, w_out, b_out, ln_g, ln_b)


def _tail(dpl, dfk, dfq, f):
    S = dpl.shape[0]
    tm = min(T_ATT, S)
    n_t = S // tm
    halo_blocks = tm // POOL_HALO
    last_halo = S // POOL_HALO - 1

    def body(d_ref, dn_ref, dfk_ref, dfq_ref, f_ref, dp_ref, df_ref, cs_ref, carry):
        s = pl.program_id(0)
        i = n_t - 1 - s

        @pl.when(s == 0)
        def _():
            carry[...] = jnp.zeros_like(carry)
            cs_ref[...] = jnp.zeros_like(cs_ref)

        dc = d_ref[...]
        nxt = jnp.where(s > 0, dn_ref[...], 0.0)
        de = jnp.concatenate([dc, nxt], axis=0)
        n_e = tm + POOL_HALO
        parts = []
        for gi, w in enumerate(POOL_WINDOWS):
            cur = de[:, gi * POOL_GROUP:(gi + 1) * POOL_GROUP] / _window_counts(i * tm, n_e, w)
            span = 1
            while span < w:
                cur = cur + pltpu.roll(cur, n_e - span, 0)
                span *= 2
            parts.append(cur[:tm, :] - dc[:, gi * POOL_GROUP:(gi + 1) * POOL_GROUP])
        dp = jnp.concatenate(parts, axis=1)
        dp_ref[...] = dp.astype(BF16)
        cs_ref[0:1, :] += _colsum(dp)

        r = lax.broadcasted_iota(jnp.int32, (tm, tm), 0)
        c = lax.broadcasted_iota(jnp.int32, (tm, tm), 1)
        tri = (r >= c).astype(F32)
        k_cols = dfk_ref[0]
        rows8 = dfq_ref[0]
        for hp in range(1, N_PAIR):
            k_cols = k_cols + pltpu.roll(dfk_ref[hp], 2 * hp, 1)
            rows8 = rows8 + pltpu.roll(dfq_ref[hp], 2 * hp, 0)
        rows8 = rows8 + k_cols.T[0:8, :]
        dlogf8 = jnp.dot(rows8, tri, preferred_element_type=F32, precision=lax.Precision.HIGHEST) + carry[...]
        first = lax.broadcasted_iota(jnp.int32, (1, tm), 1) == 0
        carry[...] = jnp.sum(jnp.where(first, dlogf8, 0.0), axis=1, keepdims=True)
        dlogf = jnp.concatenate([dlogf8, jnp.zeros((128 - 8, tm), F32)], axis=0).T
        df = dlogf * _sigmoid(-f_ref[...])
        df_ref[...] = df.astype(BF16)
        cs_ref[1:2, 0:128] += _colsum(df)

    rev = lambda w: pl.BlockSpec((tm, w), lambda s: (n_t - 1 - s, 0))
    return pl.pallas_call(
        body, name="tail", grid=(n_t,),
        out_shape=(jax.ShapeDtypeStruct((S, D_POOL), BF16), jax.ShapeDtypeStruct((S, 128), BF16),
                   jax.ShapeDtypeStruct((8, D_POOL), F32)),
        in_specs=[rev(D_POOL),
                  pl.BlockSpec((POOL_HALO, D_POOL),
                               lambda s: (jnp.minimum((n_t - s) * halo_blocks, last_halo), 0)),
                  pl.BlockSpec((N_PAIR, tm, 128), lambda s: (0, n_t - 1 - s, 0)),
                  pl.BlockSpec((N_PAIR, None, 8, tm), lambda s: (0, n_t - 1 - s, 0, 0)),
                  rev(128)],
        out_specs=(rev(D_POOL), rev(128), pl.BlockSpec((8, D_POOL), lambda s: (0, 0))),
        scratch_shapes=[pltpu.VMEM((8, 1), F32)],
        compiler_params=_params(dimension_semantics=("arbitrary",)),
    )(dpl, dpl, dfk, dfq, f)


PIECES = ((O_QKV, D_ATT), (O_QKV + D_ATT, D_ATT), (O_QKV + 2 * D_ATT, D_ATT), (O_F, 128), (O_P, D_POOL), (O_G, D))


def _grad_w_in(u, pieces):
    S = u.shape[0]
    tm = min(TM_GW, S)
    n_t = S // tm

    def body(u_ref, *rest):
        piece_refs, out_ref, acc, sem = rest[:6], rest[6], rest[7], rest[8]
        i = pl.program_id(0)

        @pl.when(i == 0)
        def _():
            acc[...] = jnp.zeros_like(acc)

        u_t = u_ref[...]
        for (off, w), ref in zip(PIECES, piece_refs):
            acc[:, off:off + w] += _dot_tn(u_t, ref[...])

        @pl.when(i == n_t - 1)
        def _():
            cp = pltpu.make_async_copy(acc, out_ref, sem)
            cp.start()
            cp.wait()

    return pl.pallas_call(
        body, name="grad_w_in", grid=(n_t,),
        out_shape=jax.ShapeDtypeStruct((D, D_PAD), F32),
        in_specs=[pl.BlockSpec((tm, D), lambda i: (i, 0))]
        + [pl.BlockSpec((tm, w), lambda i: (i, 0)) for _, w in PIECES],
        out_specs=pl.BlockSpec(memory_space=pl.ANY),
        scratch_shapes=[pltpu.VMEM((D, D_PAD), F32), pltpu.SemaphoreType.DMA],
        compiler_params=_params(dimension_semantics=("arbitrary",)),
    )(u, *pieces)


def _grad_x(pieces, wt_pad, dh, x, scale):
    S = x.shape[0]
    tm = min(TM_DU, S)

    def body(*refs):
        piece_refs = refs[:6]
        w_ref, dh_ref, x_ref, sc_ref, gx_ref, vec_ref = refs[6:]

        @pl.when(pl.program_id(0) == 0)
        def _():
            vec_ref[...] = jnp.zeros_like(vec_ref)

        du = jnp.zeros((tm, D), F32)
        for (off, w), ref in zip(PIECES, piece_refs):
            du = du + _dot(ref[...], w_ref[off:off + w, :])
        xv = x_ref[...]
        gx_ref[...] = ALPHA * dh_ref[...] + du * (1.0 + sc_ref[...])
        vec_ref[0:1, :] += _colsum(du)
        vec_ref[1:2, :] += _colsum(du * xv)

    row = lambda w: pl.BlockSpec((tm, w), lambda i: (i, 0))
    return pl.pallas_call(
        body, name="grad_x", grid=(S // tm,),
        out_shape=(jax.ShapeDtypeStruct((S, D), F32), jax.ShapeDtypeStruct((8, D), F32)),
        in_specs=[row(w) for _, w in PIECES]
        + [pl.BlockSpec(wt_pad.shape, lambda i: (0, 0)), row(D), row(D), pl.BlockSpec((1, D), lambda i: (0, 0))],
        out_specs=(row(D), pl.BlockSpec((8, D), lambda i: (0, 0))),
        compiler_params=_params(dimension_semantics=("arbitrary",)),
    )(*pieces, wt_pad, dh, x, scale)


def _grad_ada(c_all, dada_all, dada_cols):
    def body(c_ref, dall_ref, dcol_ref, gw_ref, gb_ref):
        rows = lax.broadcasted_iota(jnp.int32, (8, 1), 0)
        cm = jnp.zeros((8, D), F32)
        dm = jnp.zeros((8, 3 * D), F32)
        for r in range(8):
            cm = jnp.where(rows == r, c_ref[r], cm)
            dm = jnp.where(rows == r, dall_ref[r], dm)
        act = cm * _sigmoid(cm)
        pad = jnp.zeros((8, D), F32)
        lhs = jnp.concatenate([act, pad], axis=0).astype(BF16)
        rhs = jnp.concatenate([dcol_ref[...], jnp.zeros((8, SHARD_ADA), F32)], axis=0).astype(BF16)
        gw_ref[...] = _dot_tn(lhs, rhs)
        gb_ref[...] = _colsum(dm)

    vm = pl.BlockSpec(memory_space=pltpu.VMEM)
    return pl.pallas_call(
        body, name="grad_ada",
        out_shape=(jax.ShapeDtypeStruct((D, SHARD_ADA), F32), jax.ShapeDtypeStruct((1, 3 * D), F32)),
        in_specs=[vm, vm, vm], out_specs=(vm, vm),
        compiler_params=_params(),
    )(c_all, dada_all, dada_cols)


def _adamw_math(w, g, m, v):
    m = ADAM_B1 * m + (1.0 - ADAM_B1) * g
    v = ADAM_B2 * v + (1.0 - ADAM_B2) * (g * g)
    m_hat = m / (1.0 - ADAM_B1 ** ADAM_STEP)
    v_hat = v / (1.0 - ADAM_B2 ** ADAM_STEP)
    delta = -ADAM_LR * (m_hat / (jnp.sqrt(v_hat) + ADAM_EPS) + ADAM_WD * w)
    return delta, m, v


def _adamw(groups, n_steps):
    n = len(groups)

    def body(*refs):
        ins, outs = refs[:4 * n], refs[4 * n:]
        for t in range(n):
            w, g, m, v = (r[...] for r in ins[4 * t:4 * t + 4])
            d, m2, v2 = _adamw_math(w, g, m, v)
            outs[3 * t][...] = d
            outs[3 * t + 1][...] = m2
            outs[3 * t + 2][...] = v2

    in_specs, out_specs, out_shape, args = [], [], [], []
    for (w, g, m, v) in groups:
        rest = w.shape[1:]
        spec = pl.BlockSpec((w.shape[0] // n_steps,) + rest, lambda i, nd=len(rest): (i,) + (0,) * nd)
        in_specs += [spec] * 4
        out_specs += [spec] * 3
        out_shape += [jax.ShapeDtypeStruct(w.shape, F32)] * 3
        args += [w, g, m, v]
    return pl.pallas_call(
        body, name="adamw_%d_%d" % (n, n_steps), grid=(n_steps,),
        out_shape=tuple(out_shape), in_specs=in_specs, out_specs=tuple(out_specs),
        compiler_params=_params(dimension_semantics=("arbitrary",)),
    )(*args)


def _adamw_small(small_sum, g_b_ada, params):
    n = len(params)

    def body(gs_ref, gba_ref, *refs):
        ins, outs = refs[:3 * n], refs[3 * n:]
        for t, (name, w0, _, _) in enumerate(params):
            w_ref, m_ref, v_ref = ins[3 * t:3 * t + 3]
            first = SMALL_SEGS[name][0] if name in SMALL_SEGS else None
            if w0.shape[0] > 1:
                pieces = [((slice(None), slice(None)), gs_ref[first:first + w0.shape[0], :])]
            else:
                pieces = []
                for r in range(-(-w0.shape[1] // 128)):
                    lanes = slice(128 * r, min(128 * r + 128, w0.shape[1]))
                    g = gba_ref[0:1, lanes] if first is None else gs_ref[first + r:first + r + 1, 0:lanes.stop - lanes.start]
                    pieces.append(((slice(0, 1), lanes), g))
            for where, g in pieces:
                d, m2, v2 = _adamw_math(w_ref[where], g, m_ref[where], v_ref[where])
                for ref, val in zip(outs[4 * t:4 * t + 4], (g, d, m2, v2)):
                    ref[where] = val

    vm = pl.BlockSpec(memory_space=pltpu.VMEM)
    args = [small_sum, g_b_ada]
    out_shape = []
    for _, w, m, v in params:
        args += [w, m, v]
        out_shape += [jax.ShapeDtypeStruct(w.shape, F32)] * 4
    return pl.pallas_call(
        body, name="adamw_small",
        out_shape=tuple(out_shape), in_specs=[vm] * len(args), out_specs=(vm,) * len(out_shape),
        compiler_params=_params(),
    )(*args)


def _pack_small(parts):
    rows = []
    used = 0
    for name, (first, n_rows) in SMALL_SEGS.items():
        if first > used:
            rows.append(jnp.zeros((first - used, 128), F32))
        flat = parts[name].reshape(-1)
        flat = jnp.pad(flat, (0, n_rows * 128 - flat.shape[0]))
        rows.append(flat.reshape(n_rows, 128))
        used = first + n_rows
    rows.append(jnp.zeros((SMALL_ROWS - used, 128), F32))
    return jnp.concatenate(rows, axis=0)


def _unpack_small(buf, name, shape):
    first, n_rows = SMALL_SEGS[name]
    n = int(np.prod(shape))
    return buf[first:first + n_rows].reshape(-1)[:n].reshape(shape)


def _pad_in(v):
    r = v.shape[0]
    z = jnp.zeros((r, O_P - O_F - N_HEADS), v.dtype)
    return jnp.concatenate([v[:, :3 * D_ATT + N_HEADS], z, v[:, 3 * D_ATT + N_HEADS:]], axis=1)


def _unpad_in(v):
    return jnp.concatenate([v[:, :O_F + N_HEADS], v[:, O_P:]], axis=1)


def _shards_in(v):
    gap = O_P - (O_F + N_HEADS)
    parts = []
    for a in range(N_CHIPS):
        lo, hi = a * SHARD_IN, (a + 1) * SHARD_IN
        cut = O_F + N_HEADS
        if hi <= cut:
            parts.append(v[:, lo:hi])
        elif lo >= cut:
            parts.append(v[:, lo + gap:hi + gap])
        else:
            parts.append(jnp.concatenate([v[:, lo:cut], v[:, cut + gap:hi + gap]], axis=1))
    return jnp.stack(parts, axis=0)


def kernel(x, c, w_ada, b_ada, w_in, b_in, w_pool_mix, b_pool_mix, pool_scale, w_out, b_out, ln_g, ln_b, loss_target, m_w_ada, m_b_ada, m_w_in, m_b_in, m_w_pool_mix, m_b_pool_mix, m_pool_scale, m_w_out, m_b_out, m_ln_g, m_ln_b, v_w_ada, v_b_ada, v_w_in, v_b_in, v_w_pool_mix, v_b_pool_mix, v_pool_scale, v_w_out, v_b_out, v_ln_g, v_ln_b):
    S = x.shape[1]
    T = min(T_ATT, S)
    n_t = S // T
    chip = 2 * lax.axis_index("x") + lax.axis_index("y")
    x2 = x[0]
    tgt = loss_target[0]
    q_scale = jnp.concatenate([jnp.full((1, D_ATT), Q_SCALE, F32), jnp.ones((1, D_PAD - D_ATT), F32)], axis=1)

    to_cols = lambda a: jnp.transpose(a, (2, 0, 1))
    from_cols = lambda a: jnp.transpose(a, (1, 2, 0))
    c_all, ada4, wt_pad = _gather_and_ada(
        c, w_ada[0], b_ada.reshape(4, 1, SHARD_ADA), to_cols(w_in))
    ada = ada4[:, 0, :].reshape(1, 3 * D)
    shift, scale, gate = ada[:, :D], ada[:, D:2 * D], ada[:, 2 * D:]
    b_pad = _pad_in(b_in) * q_scale
    w_mix_bf = w_pool_mix[0].astype(BF16)

    u, qkv, f, p, g, w_out_all = _in_proj(x2, shift, scale, wt_pad, b_pad, w_out[0].astype(BF16))
    w_out_full = w_out_all.reshape(D, D)
    big_f = _forget_cumsum(f)
    att, lse = _attention_fwd(qkv, big_f)

    dh, datt, dg, dpl, gw_out, gw_mix, vec, loss_part = _middle(
        x2, tgt, att, g, p, gate, w_mix_bf, b_pool_mix.reshape(1, D_POOL), pool_scale, w_out_full, b_out, ln_g, ln_b)
    dq, dk, dv, cs_att, dfk, dfq, g_w_out = _attention_bwd(
        qkv, datt, att, lse, big_f, gw_out.reshape(N_CHIPS, SHARD_OUT, D), jnp.ones((N_CHIPS, 1, D), F32))
    dp, df, cs_tail = _tail(dpl, dfk, dfq, f)
    pieces = (dq, dk, dv, df, dp, dg)
    gw_pad = _grad_w_in(u, pieces)
    grad_x, vec_x = _grad_x(pieces, wt_pad, dh, x2, scale)

    cs_qkv = jnp.transpose(cs_att.reshape(N_PAIR, 3, 128), (1, 0, 2)).reshape(1, 3 * D_ATT)
    gb_pad = jnp.concatenate([cs_qkv, cs_tail[1:2, 0:128], cs_tail[0:1, :], vec[4:5, :]], axis=1) * q_scale
    dada = jnp.concatenate([vec_x[0:1, :], vec_x[1:2, :], vec[2:3, :]], axis=1)
    small = _pack_small({
        "b_in": _unpad_in(gb_pad), "w_pool_mix": gw_mix, "b_pool_mix": vec[6:7, :D_POOL],
        "pool_scale": vec[5:6, :D_POOL], "b_out": vec[3:4, :], "ln_g": vec[0:1, :], "ln_b": vec[1:2, :],
        "loss": loss_part})

    g_w_in, small_sum, dada_all = _reduce_all(
        gw_pad, _shards_in(q_scale), small, dada)
    dada_cols = lax.dynamic_slice(dada_all[:, 0, :], (0, chip * SHARD_ADA), (8, SHARD_ADA))
    g_w_ada, g_b_ada = _grad_ada(c_all, dada_all, dada_cols)
    loss = _unpack_small(small_sum, "loss", (1,))[0]

    big = _adamw([(w_ada[0], g_w_ada, m_w_ada[0], v_w_ada[0]),
                  (w_out[0], g_w_out, m_w_out[0], v_w_out[0])], 4)
    g_w_in_cols = g_w_in
    big_in = _adamw([(to_cols(w_in), g_w_in_cols, to_cols(m_w_in), to_cols(v_w_in))], 5)
    tiles = lambda a: a.reshape(4 * POOL_GROUP, POOL_GROUP)
    flat = lambda a: a.reshape(1, D_POOL)
    small_params = [("b_ada", b_ada, m_b_ada, v_b_ada), ("b_in", b_in, m_b_in, v_b_in),
                    ("w_pool_mix", tiles(w_pool_mix), tiles(m_w_pool_mix), tiles(v_w_pool_mix)),
                    ("b_pool_mix", flat(b_pool_mix), flat(m_b_pool_mix), flat(v_b_pool_mix)),
                    ("pool_scale", pool_scale, m_pool_scale, v_pool_scale), ("b_out", b_out, m_b_out, v_b_out),
                    ("ln_g", ln_g, m_ln_g, v_ln_g), ("ln_b", ln_b, m_ln_b, v_ln_b)]
    sm = _adamw_small(small_sum, g_b_ada, small_params)
    sm_idx = {p[0]: n for n, p in enumerate(small_params)}
    shapes = {"w_pool_mix": (1, 4, POOL_GROUP, POOL_GROUP), "b_pool_mix": (1, 4, POOL_GROUP)}

    names = ["w_ada", "b_ada", "w_in", "b_in", "w_pool_mix", "b_pool_mix", "pool_scale", "w_out", "b_out",
             "ln_g", "ln_b"]
    big_idx = {"w_ada": 0, "w_out": 1}

    def leaf(kind, name):
        if name == "w_in":
            return from_cols(g_w_in_cols if kind == 0 else big_in[kind - 1])
        if name in big_idx:
            if kind == 0:
                return (g_w_ada, g_w_out)[big_idx[name]][None]
            return big[3 * big_idx[name] + kind - 1][None]
        val = sm[4 * sm_idx[name] + kind]
        return val.reshape(shapes[name]) if name in shapes else val

    outs = [loss, grad_x[None]]
    for kind in range(4):
        outs += [leaf(kind, n) for n in names]
    return tuple(outs)
```

```python
import functools

import numpy as np
import jax
import jax.numpy as jnp
from jax import lax
from jax.experimental import pallas as pl
from jax.experimental.pallas import tpu as pltpu

F32 = jnp.float32
BF16 = jnp.bfloat16
MESH = pl.DeviceIdType.MESH

D = 1024
D_ATT = 512
D_POOL = 512
N_HEADS = 8
HEAD_DIM = 64
N_PAIR = N_HEADS // 2
POOL_WINDOWS = (2, 4, 8, 16)
POOL_GROUP = 128
POOL_HALO = 16
LN_EPS = 1e-5
ALPHA = 2.0 ** 0.25
D_IN = 3 * D_ATT + N_HEADS + D_POOL + D_ATT + D_POOL
N_CHIPS = 4
SHARD_IN = D_IN // N_CHIPS
SHARD_ADA = 3 * D // N_CHIPS
SHARD_OUT = D // N_CHIPS

O_QKV, O_F, O_P, O_G, D_PAD = 0, 1536, 1664, 2176, 3200
Q_SCALE = HEAD_DIM ** -0.5

ADAM_LR, ADAM_B1, ADAM_B2, ADAM_EPS, ADAM_WD, ADAM_STEP = 0.001, 0.9, 0.999, 1e-08, 0.01, 10

NEG = -1e30

VMEM_LIMIT = 56 * 1024 * 1024

TM_PROJ = 512
T_ATT = 512
TM_MID = 256
TM_GW = 1024
TM_DU = 512

REL7 = [(0, 0, 1), (0, 1, 0), (0, 1, 1), (1, 0, 0), (1, 0, 1), (1, 1, 0), (1, 1, 1)]
REL3 = [(0, 1), (1, 0), (1, 1)]

SMALL_SEGS = {}
_row = 0
for _name, _n in (("b_in", D_IN), ("w_pool_mix", 65536), ("b_pool_mix", 512), ("pool_scale", 512),
                  ("b_out", 1024), ("ln_g", 1024), ("ln_b", 1024), ("loss", 1)):
    _rows = -(-_n // 1024) * 8
    SMALL_SEGS[_name] = (_row, _rows)
    _row += _rows
SMALL_ROWS = -(-_row // 16) * 16


def _params(**kw):
    return pltpu.CompilerParams(vmem_limit_bytes=VMEM_LIMIT, **kw)


def _flip(v, d):
    return v if d == 0 else 1 - v


def _dot(a, b):
    return jnp.dot(a, b, preferred_element_type=F32)


def _dot_nt(a, b):
    return lax.dot_general(a, b, (((1,), (1,)), ((), ())), preferred_element_type=F32)


def _dot_tn(a, b):
    return lax.dot_general(a, b, (((0,), (0,)), ((), ())), preferred_element_type=F32)


def _sigmoid(v):
    return 1.0 / (1.0 + jnp.exp(-v))


def _colsum(v):
    return jnp.sum(v, axis=0, keepdims=True)


def _gather_stages(pos, src_ref, dst_ref, half, own_sem, s_sem, r_sem, fs_sem, fr_sem):
    x, y, cc, chip, sib = pos
    own = pltpu.make_async_copy(src_ref, dst_ref.at[chip], own_sem)
    first, landed, others = [], [], []
    for k, (dx, dy) in enumerate(REL3):
        px, py = _flip(x, dx), _flip(y, dy)
        first.append(pltpu.make_async_remote_copy(
            src_ref=src_ref.at[half(cc)], dst_ref=dst_ref.at[(chip,) + half(cc)],
            send_sem=s_sem.at[k], recv_sem=r_sem.at[k], device_id=(px, py, cc), device_id_type=MESH))
        landed.append(dst_ref.at[(2 * px + py,) + half(cc)])
        others.append(dst_ref.at[(2 * px + py,) + half(1 - cc)])
    passed = [pltpu.make_async_remote_copy(src_ref=landed[k], dst_ref=landed[k], send_sem=fs_sem.at[k],
                                           recv_sem=fr_sem.at[k], device_id=sib, device_id_type=MESH)
              for k in range(3)]

    def start(finish_src=None):
        for cp in first:
            cp.start()
        if finish_src is not None:
            finish_src()
        own.start()

    def forward():
        for k in range(3):
            pltpu.make_async_remote_copy(src_ref=landed[k], dst_ref=landed[k], send_sem=s_sem.at[k],
                                         recv_sem=r_sem.at[k], device_id=sib, device_id_type=MESH).wait_recv()
            passed[k].start()

    def finish():
        for k in range(3):
            pltpu.make_async_remote_copy(src_ref=others[k], dst_ref=others[k], send_sem=fs_sem.at[k],
                                         recv_sem=fr_sem.at[k], device_id=sib, device_id_type=MESH).wait_recv()
        for cp in first + passed:
            cp.wait_send()
        own.wait()

    return start, forward, finish


def _gather_scratch():
    return [pltpu.SemaphoreType.DMA, pltpu.SemaphoreType.DMA((3,)), pltpu.SemaphoreType.DMA((3,)),
            pltpu.SemaphoreType.DMA((3,)), pltpu.SemaphoreType.DMA((3,))]


def _gather_and_ada(c, w_ada, b_ada4, w_in_sh):
    def body(c_ref, w_ref, b_ref, win_ref, call_ref, ada_ref, wt_pad_ref,
             win_all, win_bf, cslab, sbuf, rbuf, cs_sem, cr_sem, as_sem, ar_sem, *gather_sems):
        x, y, cc = lax.axis_index("x"), lax.axis_index("y"), lax.axis_index("c")
        me = 4 * x + 2 * y + cc
        chip = 2 * x + y
        lane_half = lambda which: (slice(None), pl.ds(pl.multiple_of(which * (D // 2), D // 2), D // 2))
        def round_half(which):
            for h in range(2):
                @pl.when(which == h)
                def _():
                    lanes = slice(h * (D // 2), (h + 1) * (D // 2))
                    win_bf[:, lanes] = win_ref[:, 0, lanes].astype(BF16)

        start, forward, finish = _gather_stages((x, y, cc, chip, (x, y, 1 - cc)), win_bf, win_all, lane_half,
                                                *gather_sems)
        round_half(cc)
        start(lambda: round_half(1 - cc))

        cslab[...] = jnp.broadcast_to(c_ref[...], (8, D))
        call_ref[me] = cslab[...]
        gathers = []
        for k, (dx, dy, dc) in enumerate(REL7):
            cp = pltpu.make_async_remote_copy(
                src_ref=cslab, dst_ref=call_ref.at[me], send_sem=cs_sem.at[k], recv_sem=cr_sem.at[k],
                device_id=(_flip(x, dx), _flip(y, dy), _flip(cc, dc)), device_id_type=MESH)
            cp.start()
            gathers.append(cp)
        for cp in gathers:
            cp.wait()
        slab_row = lax.broadcasted_iota(jnp.int32, (8, 1), 0)
        mat = jnp.zeros((8, D), F32)
        for r in range(8):
            mat = jnp.where(slab_row == r, call_ref[r], mat)
        act = (mat * _sigmoid(mat)).astype(BF16)
        part = _dot(act, w_ref[...].astype(BF16))
        sends = []
        for k, (dx, dy) in enumerate(REL3):
            px, py = _flip(x, dx), _flip(y, dy)
            r = 4 * px + 2 * py + cc
            piece = _colsum(jnp.where(slab_row == r, part, 0.0))
            sbuf[k] = jnp.broadcast_to(piece, (8, SHARD_ADA))
            cp = pltpu.make_async_remote_copy(
                src_ref=sbuf.at[k], dst_ref=rbuf.at[k], send_sem=as_sem.at[k], recv_sem=ar_sem.at[k],
                device_id=(px, py, cc), device_id_type=MESH)
            cp.start()
            sends.append(cp)
        own_piece = _colsum(jnp.where(slab_row == me, part, 0.0))
        ada_ref[chip] = jnp.broadcast_to(own_piece, (8, SHARD_ADA)) + b_ref[chip]
        for k, (dx, dy) in enumerate(REL3):
            sends[k].wait()
            a = 2 * _flip(x, dx) + _flip(y, dy)
            ada_ref[a] = rbuf[k] + b_ref[a]

        forward()
        finish()
        n_real = 3 * D_ATT + N_HEADS
        for a in range(N_CHIPS):
            lo, hi = a * SHARD_IN, (a + 1) * SHARD_IN
            for s0, s1 in ((lo, min(hi, D_ATT)), (max(lo, D_ATT), min(hi, n_real)), (max(lo, n_real), hi)):
                if s0 < s1:
                    rows = win_all[a, s0 - lo:s1 - lo, :]
                    if s1 <= D_ATT:
                        rows = rows * jnp.asarray(Q_SCALE, BF16)
                    shift = O_P - n_real if s0 >= n_real else 0
                    wt_pad_ref[s0 + shift:s1 + shift, :] = rows
        wt_pad_ref[n_real:O_P, :] = jnp.zeros((O_P - n_real, D), BF16)

    vm = pl.BlockSpec(memory_space=pltpu.VMEM)
    return pl.pallas_call(
        body, name="gather_and_ada",
        out_shape=(jax.ShapeDtypeStruct((8, 8, D), F32), jax.ShapeDtypeStruct((4, 8, SHARD_ADA), F32),
                   jax.ShapeDtypeStruct((D_PAD, D), BF16)),
        in_specs=[vm] * 4, out_specs=(vm,) * 3,
        scratch_shapes=[pltpu.VMEM((N_CHIPS, SHARD_IN, D), BF16), pltpu.VMEM((SHARD_IN, D), BF16),
                        pltpu.VMEM((8, D), F32), pltpu.VMEM((3, 8, SHARD_ADA), F32),
                        pltpu.VMEM((3, 8, SHARD_ADA), F32),
                        pltpu.SemaphoreType.DMA((7,)), pltpu.SemaphoreType.DMA((7,)),
                        pltpu.SemaphoreType.DMA((3,)), pltpu.SemaphoreType.DMA((3,))] + _gather_scratch(),
        compiler_params=_params(),
    )(c, w_ada, b_ada4, w_in_sh)


def _shard_cols():
    cut, gap = O_F + N_HEADS, O_P - (O_F + N_HEADS)
    out = []
    for a in range(N_CHIPS):
        lo, hi = a * SHARD_IN, (a + 1) * SHARD_IN
        out.append(([(lo, min(hi, cut))] if lo < cut else []) + ([(max(lo, cut) + gap, hi + gap)] if hi > cut else []))
    return out


def _scatter_stages(pos, g_ref, sc_ref, out_ref, sib_buf, send_buf, ici_buf, sem1, sem2s, sem2r, sem3, part=(0, 1),
                    cols=None, own_buf=None):
    x, y, cc, chip, sib = pos
    q, n_parts = part
    RH = (g_ref.shape[1] if cols is None else g_ref.shape[0]) // 2 // n_parts
    mine = pl.ds(pl.multiple_of((cc * n_parts + q) * RH, RH), RH)
    theirs = pl.ds(pl.multiple_of(((1 - cc) * n_parts + q) * RH, RH), RH)
    cp1 = pltpu.make_async_remote_copy(
        src_ref=g_ref.at[:, theirs, :] if cols is None else g_ref.at[theirs, :], dst_ref=sib_buf,
        send_sem=sem1.at[0], recv_sem=sem1.at[1], device_id=sib, device_id_type=MESH)
    sends = []
    for k, (dx, dy) in enumerate(REL3):
        px, py = _flip(x, dx), _flip(y, dy)
        sends.append(pltpu.make_async_remote_copy(
            src_ref=send_buf.at[2 * px + py], dst_ref=ici_buf.at[chip],
            send_sem=sem2s.at[k], recv_sem=sem2r.at[k], device_id=(px, py, cc), device_id_type=MESH))
    cp3 = pltpu.make_async_remote_copy(
        src_ref=out_ref.at[mine, :], dst_ref=out_ref.at[mine, :], send_sem=sem3.at[0], recv_sem=sem3.at[1],
        device_id=sib, device_id_type=MESH)

    def finish1():
        cp1.wait()
        if cols is None:
            for a in range(N_CHIPS):
                both = g_ref[a, mine, :] + sib_buf[a]
                sib_buf[a] = both
                send_buf[a] = both.astype(BF16)
        else:
            both = g_ref[mine, :] + sib_buf[...]
            for a, pieces in enumerate(cols):
                at = 0
                for lo, hi in pieces:
                    own_buf[a, :, at:at + hi - lo] = both[:, lo:hi]
                    send_buf[a, :, at:at + hi - lo] = both[:, lo:hi].astype(BF16)
                    at += hi - lo

    def start2():
        for cp in sends:
            cp.start()
        ici_buf[chip] = send_buf[chip]

    def finish2():
        for cp in sends:
            cp.wait()
        own = (sib_buf if cols is None else own_buf)[chip]
        parts = [jnp.where(chip == a, own, ici_buf[a].astype(F32)) for a in range(N_CHIPS)]
        out_ref[mine, 0:own.shape[1]] = ((parts[0] + parts[1]) + (parts[2] + parts[3])) * sc_ref[chip]

    return [(cp1.start, finish1), (start2, finish2), (cp3.start, cp3.wait)]


def _all_reduce_stages(pos, g_ref, out_ref, sib_buf, ici_buf, sem1, sem2s, sem2r, sem3):
    x, y, cc, chip, sib = pos
    RH = g_ref.shape[0] // 2
    mine = pl.ds(pl.multiple_of(cc * RH, 8), RH)
    theirs = pl.ds(pl.multiple_of((1 - cc) * RH, 8), RH)
    cp1 = pltpu.make_async_remote_copy(
        src_ref=g_ref.at[theirs, :], dst_ref=sib_buf, send_sem=sem1.at[0], recv_sem=sem1.at[1],
        device_id=sib, device_id_type=MESH)
    sends = []
    for k, (dx, dy) in enumerate(REL3):
        px, py = _flip(x, dx), _flip(y, dy)
        sends.append(pltpu.make_async_remote_copy(
            src_ref=sib_buf, dst_ref=ici_buf.at[chip],
            send_sem=sem2s.at[k], recv_sem=sem2r.at[k], device_id=(px, py, cc), device_id_type=MESH))
    cp3 = pltpu.make_async_remote_copy(
        src_ref=out_ref.at[mine, :], dst_ref=out_ref.at[mine, :], send_sem=sem3.at[0], recv_sem=sem3.at[1],
        device_id=sib, device_id_type=MESH)

    def finish1():
        cp1.wait()
        sib_buf[...] = g_ref[mine, :] + sib_buf[...]

    def start2():
        for cp in sends:
            cp.start()
        ici_buf[chip] = sib_buf[...]

    def finish2():
        for cp in sends:
            cp.wait()
        out_ref[mine, :] = (ici_buf[0] + ici_buf[1]) + (ici_buf[2] + ici_buf[3])

    return [(cp1.start, finish1), (start2, finish2), (cp3.start, cp3.wait)]


def _stage_sems():
    return [pltpu.SemaphoreType.DMA((2,)), pltpu.SemaphoreType.DMA((3,)),
            pltpu.SemaphoreType.DMA((3,)), pltpu.SemaphoreType.DMA((2,))]


def _scatter_scratch(r, c):
    return [pltpu.VMEM((N_CHIPS, r // 2, c), F32), pltpu.VMEM((N_CHIPS, r // 2, c), BF16),
            pltpu.VMEM((N_CHIPS, r // 2, c), BF16)] + _stage_sems()


def _reduce_all(gw_pad, sc_in, small, dada, c_all):
    R = small.shape[0]
    W = dada.shape[1]
    r_in, p_in = gw_pad.shape
    c_in = SHARD_IN
    chunk = r_in // 4

    def chunk_scratch():
        return ([pltpu.VMEM((chunk, p_in), F32), pltpu.VMEM((N_CHIPS, chunk, c_in), BF16),
                 pltpu.VMEM((N_CHIPS, chunk, c_in), BF16)] + _stage_sems()
                + [pltpu.VMEM((N_CHIPS, chunk, c_in), F32)])

    n_in = len(chunk_scratch())

    c_wide = -(-c_in // 128) * 128

    def body(gin_ref, scin_ref, sm_ref, d_ref, c_ref, ocols_ref, osm_ref, gwa_ref, gba_ref, oin_ref, dall_ref,
             *scratch):
        x, y, cc = lax.axis_index("x"), lax.axis_index("y"), lax.axis_index("c")
        me = 4 * x + 2 * y + cc
        chip = 2 * x + y
        pos = (x, y, cc, chip, (x, y, 1 - cc))
        oin_ref[:, c_in:c_wide] = jnp.zeros((r_in, c_wide - c_in), F32)
        dslab, ds_sem, dr_sem = scratch[0:3]
        a_bufs, b_bufs, sm_bufs = scratch[3:3 + n_in], scratch[3 + n_in:3 + 2 * n_in], scratch[3 + 2 * n_in:]
        dslab[...] = jnp.broadcast_to(d_ref[...], (8, W))
        dall_ref[me] = dslab[...]
        gathers = []
        for k, (dx, dy, dc) in enumerate(REL7):
            cp = pltpu.make_async_remote_copy(
                src_ref=dslab, dst_ref=dall_ref.at[me], send_sem=ds_sem.at[k], recv_sem=dr_sem.at[k],
                device_id=(_flip(x, dx), _flip(y, dy), _flip(cc, dc)), device_id_type=MESH)
            cp.start()
            gathers.append(cp)
        cols = _shard_cols()
        first = _scatter_stages(pos, gin_ref, scin_ref, oin_ref, *a_bufs[:-1], part=(0, 2), cols=cols,
                                own_buf=a_bufs[-1])
        second = _scatter_stages(pos, gin_ref, scin_ref, oin_ref, *b_bufs[:-1], part=(1, 2), cols=cols,
                                 own_buf=b_bufs[-1])
        little = _all_reduce_stages(pos, sm_ref, osm_ref, *sm_bufs)
        for plan in (first, second, little):
            plan[0][0]()
        for cp in gathers:
            cp.wait()
        slab_row = lax.broadcasted_iota(jnp.int32, (8, 1), 0)
        cm = jnp.zeros((8, D), F32)
        dm = jnp.zeros((8, W), F32)
        for r in range(8):
            cm = jnp.where(slab_row == r, c_ref[r], cm)
            dm = jnp.where(slab_row == r, dall_ref[r], dm)
        act = cm * _sigmoid(cm)
        dcol = dm[:, 0:SHARD_ADA]
        for a in range(1, N_CHIPS):
            dcol = jnp.where(chip == a, dm[:, a * SHARD_ADA:(a + 1) * SHARD_ADA], dcol)
        lhs = jnp.concatenate([act, jnp.zeros((8, D), F32)], axis=0).astype(BF16)
        rhs = jnp.concatenate([dcol, jnp.zeros((8, SHARD_ADA), F32)], axis=0).astype(BF16)
        gwa_ref[...] = _dot_tn(lhs, rhs)
        gba_ref[...] = _colsum(dm)
        first[0][1]()
        first[1][0]()
        little[0][1]()
        little[1][0]()
        second[0][1]()
        second[1][0]()
        first[1][1]()
        first[2][0]()
        second[1][1]()
        second[2][0]()
        little[1][1]()
        little[2][0]()
        for plan in (first, second, little):
            plan[2][1]()
        ocols_ref[...] = oin_ref[...].T[0:c_in, :][:, None, :]

    scratch = [pltpu.VMEM((r_in, c_wide), F32), pltpu.VMEM((8, 8, W), F32),
               pltpu.VMEM((8, W), F32), pltpu.SemaphoreType.DMA((7,)), pltpu.SemaphoreType.DMA((7,))]
    scratch += chunk_scratch() + chunk_scratch()
    scratch += [pltpu.VMEM((R // 2, 128), F32), pltpu.VMEM((N_CHIPS, R // 2, 128), F32)] + _stage_sems()
    vm = pl.BlockSpec(memory_space=pltpu.VMEM)
    return pl.pallas_call(
        body, name="reduce_all",
        out_shape=(jax.ShapeDtypeStruct((c_in, 1, r_in), F32), jax.ShapeDtypeStruct((R, 128), F32),
                   jax.ShapeDtypeStruct((D, SHARD_ADA), F32), jax.ShapeDtypeStruct((1, W), F32)),
        in_specs=[vm] * 5, out_specs=(vm,) * 4,
        scratch_shapes=scratch,
        compiler_params=_params(),
    )(gw_pad, sc_in, small, dada, c_all)


def _in_proj(x, shift, scale, wt_pad, b_pad, w_out_sh):
    S = x.shape[0]
    tm = min(TM_PROJ, S)
    n_steps = S // tm
    assert n_steps >= 3

    def body(x_ref, sh_ref, sc_ref, w_ref, b_ref, wo_ref, u_ref, qkv_ref, f_ref, p_ref, g_ref, wo_all,
             wo_buf, *gather_sems):
        i = pl.program_id(0)
        xx, yy, cc = lax.axis_index("x"), lax.axis_index("y"), lax.axis_index("c")
        row_half = lambda which: (pl.ds(pl.multiple_of(which * (SHARD_OUT // 2), SHARD_OUT // 2), SHARD_OUT // 2),
                                  slice(None))
        start, forward, finish = _gather_stages((xx, yy, cc, 2 * xx + yy, (xx, yy, 1 - cc)), wo_ref, wo_buf,
                                                row_half, *gather_sems)
        pl.when(i == 0)(start)
        pl.when(i == n_steps // 2)(forward)

        @pl.when(i == n_steps - 1)
        def _():
            finish()
            wo_all[...] = wo_buf[...]

        u = (x_ref[...] * (1.0 + sc_ref[...]) + sh_ref[...]).astype(BF16)
        u_ref[...] = u
        qkv_ref[...] = (_dot_nt(u, w_ref[O_QKV:O_F, :]) + b_ref[:, O_QKV:O_F]).astype(BF16)
        f_ref[...] = _dot_nt(u, w_ref[O_F:O_P, :]) + b_ref[:, O_F:O_P]
        p_ref[...] = _dot_nt(u, w_ref[O_P:O_G, :]) + b_ref[:, O_P:O_G]
        g_ref[...] = _dot_nt(u, w_ref[O_G:D_PAD, :]) + b_ref[:, O_G:D_PAD]

    row = lambda w: pl.BlockSpec((tm, w), lambda i: (i, 0))
    full = lambda a: pl.BlockSpec(a.shape, lambda i: (0, 0))
    vm = pl.BlockSpec(memory_space=pltpu.VMEM)
    return pl.pallas_call(
        body, name="in_proj", grid=(n_steps,),
        out_shape=(jax.ShapeDtypeStruct((S, D), BF16), jax.ShapeDtypeStruct((S, 3 * D_ATT), BF16),
                   jax.ShapeDtypeStruct((S, 128), F32), jax.ShapeDtypeStruct((S, D_POOL), F32),
                   jax.ShapeDtypeStruct((S, D), F32), jax.ShapeDtypeStruct((N_CHIPS,) + w_out_sh.shape, BF16)),
        in_specs=[row(D), full(shift), full(scale), full(wt_pad), full(b_pad), vm],
        out_specs=(row(D), row(3 * D_ATT), row(128), row(D_POOL), row(D), vm),
        scratch_shapes=[pltpu.VMEM((N_CHIPS,) + w_out_sh.shape, BF16)] + _gather_scratch(),
        compiler_params=_params(dimension_semantics=("arbitrary",)),
    )(x, shift, scale, wt_pad, b_pad, w_out_sh)


def _forget_cumsum(f):
    S = f.shape[0]
    tm = min(T_ATT, S)

    def body(f_ref, out_ref, carry):
        @pl.when(pl.program_id(0) == 0)
        def _():
            carry[...] = jnp.zeros_like(carry)
        v = f_ref[...]
        logf = jnp.minimum(v, 0.0) - jnp.log(1.0 + jnp.exp(-jnp.abs(v)))
        r = lax.broadcasted_iota(jnp.int32, (tm, tm), 0)
        c = lax.broadcasted_iota(jnp.int32, (tm, tm), 1)
        tri = (r <= c).astype(F32)
        rows8 = logf.T[0:8, :]
        cum8 = jnp.dot(rows8, tri, preferred_element_type=F32, precision=lax.Precision.HIGHEST) + carry[...]
        out_ref[...] = jnp.concatenate([cum8, jnp.zeros((128 - 8, tm), F32)], axis=0).T
        last = lax.broadcasted_iota(jnp.int32, (1, tm), 1) == tm - 1
        carry[...] = jnp.sum(jnp.where(last, cum8, 0.0), axis=1, keepdims=True)

    return pl.pallas_call(
        body, name="forget_cumsum", grid=(S // tm,),
        out_shape=jax.ShapeDtypeStruct((S, 128), F32),
        in_specs=[pl.BlockSpec((tm, 128), lambda i: (i, 0))],
        out_specs=pl.BlockSpec((tm, 128), lambda i: (i, 0)),
        scratch_shapes=[pltpu.VMEM((8, 1), F32)],
        compiler_params=_params(dimension_semantics=("arbitrary",)),
    )(f)


def _split3(v):
    hi = v.astype(BF16)
    rest = v - hi.astype(F32)
    mid = rest.astype(BF16)
    lo = (rest - mid.astype(F32)).astype(BF16)
    return hi, mid, lo


def _attention_fwd(qkv, big_f):
    S = qkv.shape[0]
    T = min(T_ATT, S)
    n_t = S // T

    def body(q_ref, k_ref, v_ref, f_ref, o_ref, lse_ref, kaug_sc, vt_sc, m_sc, l_sc, acc_sc):
        hp = pl.program_id(0)
        i = pl.program_id(1)
        lane = lax.broadcasted_iota(jnp.int32, (1, 128), 1)
        sub = lax.broadcasted_iota(jnp.int32, (128, 1), 0)
        head_sel = (lane < HEAD_DIM, lane >= HEAD_DIM)
        head_sel_t = (sub < HEAD_DIM, sub >= HEAD_DIM)
        spare = (HEAD_DIM, 0)
        zero = jnp.zeros((), BF16)

        @pl.when(i == 0)
        def _():
            def prep(jt, carry):
                rows = pl.ds(pl.multiple_of(jt * T, T), T)
                k = k_ref[rows, :]
                ft = f_ref[rows, :]
                vt = v_ref[rows, :].astype(F32).T
                for h in range(2):
                    fh = jnp.sum(jnp.where(lane == 2 * hp + h, ft, 0.0), axis=1, keepdims=True)
                    hi, mid, lo = _split3(-fh)
                    b = spare[h]
                    bias = jnp.where(lane == b, hi, jnp.where(lane == b + 1, mid, jnp.where(lane == b + 2, lo, zero)))
                    kaug_sc[h, rows, :] = jnp.where(head_sel[h], k, bias)
                    vt_sc[h, jt] = jnp.where(head_sel_t[h], vt, 0.0).astype(BF16)
                return carry

            lax.fori_loop(0, n_t, prep, 0)

        q = q_ref[...]
        q_heads = []
        for h in range(2):
            ones = jnp.where((lane >= spare[h]) & (lane < spare[h] + 3), jnp.ones((), BF16), zero)
            q_heads.append(jnp.where(head_sel[h], q, ones))
        m_sc[...] = jnp.full((8, T), NEG, F32)
        l_sc[...] = jnp.zeros((8, T), F32)
        acc_sc[...] = jnp.zeros((128, T), F32)

        def update(j, k_lo, n_k, q_lo, masked):
            rows = pl.ds(pl.multiple_of(j * T + k_lo, n_k), n_k)
            n_q = T - q_lo
            alphas, pvs = [], []
            for h in range(2):
                s_t = _dot_nt(kaug_sc[h, rows, :], q_heads[h][q_lo:, :])
                if masked:
                    rr = lax.broadcasted_iota(jnp.int32, (n_k, n_q), 0) + k_lo
                    cc = lax.broadcasted_iota(jnp.int32, (n_k, n_q), 1) + q_lo
                    s_t = jnp.where(rr <= cc, s_t, NEG)
                m_prev = m_sc[h:h + 1, q_lo:]
                m_new = jnp.maximum(m_prev, jnp.max(s_t, axis=0, keepdims=True))
                alpha = jnp.exp(m_prev - m_new)
                p_t = jnp.exp(s_t - m_new)
                l_sc[h:h + 1, q_lo:] = alpha * l_sc[h:h + 1, q_lo:] + jnp.sum(p_t, axis=0, keepdims=True)
                m_sc[h:h + 1, q_lo:] = m_new
                alphas.append(alpha)
                pvs.append(_dot(vt_sc[h, j, :, k_lo:k_lo + n_k], p_t.astype(BF16)))
            acc_sc[:, q_lo:] = (acc_sc[:, q_lo:] * jnp.where(head_sel_t[0], alphas[0], alphas[1])
                                + (pvs[0] + pvs[1]))

        def two_off_diagonal(jj, carry):
            update(2 * jj, 0, T, 0, False)
            update(2 * jj + 1, 0, T, 0, False)
            return carry

        lax.fori_loop(0, i // 2, two_off_diagonal, 0)

        @pl.when(i % 2 == 1)
        def _():
            update(i - 1, 0, T, 0, False)

        update(i, 0, T, 0, True)
        l = l_sc[...]
        o_ref[...] = (acc_sc[...] / jnp.where(head_sel_t[0], l[0:1, :], l[1:2, :])).T
        is_head = lax.broadcasted_iota(jnp.int32, (8, 1), 0) < 2
        lse_ref[...] = jnp.where(is_head, m_sc[...] + jnp.log(jnp.where(is_head, l, 1.0)), 0.0)

    return pl.pallas_call(
        body, name="attention_fwd", grid=(N_PAIR, n_t),
        out_shape=(jax.ShapeDtypeStruct((S, D_ATT), F32), jax.ShapeDtypeStruct((N_PAIR, n_t, 8, T), F32)),
        in_specs=[pl.BlockSpec((T, 128), lambda hp, i: (i, hp)),
                  pl.BlockSpec((S, 128), lambda hp, i: (0, N_PAIR + hp)),
                  pl.BlockSpec((S, 128), lambda hp, i: (0, 2 * N_PAIR + hp)),
                  pl.BlockSpec((S, 128), lambda hp, i: (0, 0))],
        out_specs=(pl.BlockSpec((T, 128), lambda hp, i: (i, hp)),
                   pl.BlockSpec((None, None, 8, T), lambda hp, i: (hp, i, 0, 0))),
        scratch_shapes=[pltpu.VMEM((2, S, 128), BF16), pltpu.VMEM((2, n_t, 128, T), BF16),
                        pltpu.VMEM((8, T), F32), pltpu.VMEM((8, T), F32), pltpu.VMEM((128, T), F32)],
        compiler_params=_params(dimension_semantics=("arbitrary", "arbitrary")),
    )(qkv, qkv, qkv, big_f)


def _attention_bwd(qkv, datt, att, lse, big_f, gw_out4, sc_out):
    S = qkv.shape[0]
    T = min(T_ATT, S)
    n_t = S // T
    n_steps = N_PAIR * n_t
    marks = (0, n_steps // 8, n_steps // 2, n_steps // 2 + n_steps // 8)

    def body(q_ref, do_ref, o_ref, lse_ref, k_ref, v_ref, fk_ref, gout_ref, scout_ref,
             dq_ref, dk_ref, dv_ref, cs_ref, dfk_ref, dfq_ref, oout_ref, stat_sc, dqt_sc, qaug_sc,
             out_buf, *red_bufs):
        hp = pl.program_id(0)
        j = pl.program_id(1)
        x, y, cc = lax.axis_index("x"), lax.axis_index("y"), lax.axis_index("c")
        plan = _scatter_stages((x, y, cc, 2 * x + y, (x, y, 1 - cc)), gout_ref, scout_ref, out_buf, *red_bufs)
        step = hp * n_t + j
        for n, mark in enumerate(marks):
            @pl.when(step == mark)
            def _(n=n):
                if n > 0:
                    plan[n - 1][1]()
                if n < 3:
                    plan[n][0]()
                else:
                    oout_ref[...] = out_buf[...]

        lane = lax.broadcasted_iota(jnp.int32, (1, 128), 1)
        sub = lax.broadcasted_iota(jnp.int32, (128, 1), 0)
        head_sel = (lane < HEAD_DIM, lane >= HEAD_DIM)
        head_sel_t = (sub < HEAD_DIM, sub >= HEAD_DIM)
        spare = (HEAD_DIM, 0)
        zero = jnp.zeros((), BF16)
        one = jnp.ones((), BF16)

        def bias_lanes(first, pieces):
            hi, mid, lo = pieces
            return lambda rest: jnp.where(lane == first, hi, jnp.where(lane == first + 1, mid,
                                                                        jnp.where(lane == first + 2, lo, rest)))

        @pl.when(j == 0)
        def _():
            dqt_sc[...] = jnp.zeros_like(dqt_sc)
            cs_ref[...] = jnp.zeros_like(cs_ref)
            dfq_ref[...] = jnp.zeros_like(dfq_ref)

            def prep(i, carry):
                rows = pl.ds(pl.multiple_of(i * T, T), T)
                q = q_ref[rows, :]
                do = do_ref[rows, :]
                prod = o_ref[rows, :] * do.astype(F32)
                d_a = jnp.sum(jnp.where(head_sel[0], prod, 0.0), axis=1, keepdims=True)
                d_b = jnp.sum(jnp.where(head_sel[0], 0.0, prod), axis=1, keepdims=True)
                delta_t = jnp.where(head_sel[0], d_a, d_b).T
                stat_sc[i, 0:1, :] = delta_t[0:1, :]
                stat_sc[i, 1:2, :] = delta_t[HEAD_DIM:HEAD_DIM + 1, :]
                lse = lse_ref[i]
                lse_cols = jnp.where(head_sel_t[0], lse[0:1, :], lse[1:2, :]).T
                for h in range(2):
                    neg_lse = -lse_cols[:, h * HEAD_DIM:h * HEAD_DIM + 1]
                    ones = jnp.where((lane >= spare[h]) & (lane < spare[h] + 3), one, zero)
                    qaug_sc[h, rows, :] = jnp.where(head_sel[h], q, bias_lanes(spare[h] + 3, _split3(neg_lse))(ones))
                return carry

            lax.fori_loop(0, n_t, prep, 0)

        k = k_ref[...]
        v = v_ref[...]
        fk = fk_ref[...]
        kt = k.astype(F32).T
        heads = []
        for h in range(2):
            fkh = jnp.sum(jnp.where(lane == 2 * hp + h, fk, 0.0), axis=1, keepdims=True)
            ones = jnp.where((lane >= spare[h] + 3) & (lane < spare[h] + 6), one, zero)
            kaug = jnp.where(head_sel[h], k, bias_lanes(spare[h], _split3(-fkh))(ones))
            heads.append((kaug, jnp.where(head_sel[h], v, zero), jnp.where(head_sel_t[h], kt, 0.0).astype(BF16)))

        def block(i, k_lo, n_k, q_lo, masked):
            n_q = T - q_lo
            rows = pl.ds(pl.multiple_of(i * T + q_lo, n_q), n_q)
            q = q_ref[rows, :]
            do = do_ref[rows, :]
            stat = stat_sc[i]
            dk = jnp.zeros((n_k, 128), F32)
            dv = jnp.zeros((n_k, 128), F32)
            dqt = jnp.zeros((128, n_q), F32)
            dfs = []
            for h in range(2):
                kaug, vh, kth = heads[h]
                arg = _dot_nt(kaug[k_lo:k_lo + n_k, :], qaug_sc[h, rows, :])
                if masked:
                    rr = lax.broadcasted_iota(jnp.int32, (n_k, n_q), 0) + k_lo
                    cc = lax.broadcasted_iota(jnp.int32, (n_k, n_q), 1) + q_lo
                    arg = jnp.where(rr <= cc, arg, NEG)
                p_t = jnp.exp(arg)
                ds_t = p_t * (_dot_nt(vh[k_lo:k_lo + n_k, :], do) - stat[h:h + 1, q_lo:])
                ds_bf = ds_t.astype(BF16)
                dv = dv + _dot(p_t.astype(BF16), jnp.where(head_sel[h], do, zero))
                dk = dk + _dot(ds_bf, jnp.where(head_sel[h], q, zero))
                dqt = dqt + _dot(kth[:, k_lo:k_lo + n_k], ds_bf)
                dfs.append(jnp.sum(ds_t, axis=1, keepdims=True))
                dfq_ref[i, h:h + 1, q_lo:] += _colsum(ds_t)
            dqt_sc[i, :, q_lo:] += dqt
            return dk, dv, dfs[0], dfs[1]

        def off_diagonal(i, acc):
            return tuple(a + b for a, b in zip(acc, block(i, 0, T, 0, False)))

        half = T // 2
        early = block(j, 0, half, 0, True)
        late = block(j, half, half, half, True)
        acc1 = tuple(jnp.concatenate([a, b], axis=0) for a, b in zip(early, late))
        n_off = n_t - 1 - j
        acc2 = lax.fori_loop(0, n_off // 2,
                             lambda ii, a: off_diagonal(j + 2 + 2 * ii, off_diagonal(j + 1 + 2 * ii, a)), acc1)
        dk_acc, dv_acc, dfa, dfb = lax.fori_loop(0, n_off % 2, lambda _, a: off_diagonal(n_t - 1, a), acc2)
        dk_ref[...] = dk_acc.astype(BF16)
        dv_ref[...] = dv_acc.astype(BF16)
        dfk_ref[...] = -jnp.where(lane == 0, dfa, jnp.where(lane == 1, dfb, 0.0))
        cs_ref[:, 128:256] = cs_ref[:, 128:256] + _colsum(dk_acc)
        cs_ref[:, 256:384] = cs_ref[:, 256:384] + _colsum(dv_acc)

        @pl.when(j == n_t - 1)
        def _():
            def finish(i, tot):
                dq = dqt_sc[i].T
                dq_ref[pl.ds(pl.multiple_of(i * T, T), T), :] = dq.astype(BF16)
                return tot + _colsum(dq)

            cs_ref[:, 0:128] = lax.fori_loop(0, n_t, finish, jnp.zeros((1, 128), F32))

    pair_rows = lambda hp, j: (hp, 0, 0)
    vm = pl.BlockSpec(memory_space=pltpu.VMEM)
    _, r_out, c_out = gw_out4.shape
    return pl.pallas_call(
        body, name="attention_bwd", grid=(N_PAIR, n_t),
        out_shape=(jax.ShapeDtypeStruct((S, D_ATT), BF16), jax.ShapeDtypeStruct((S, D_ATT), BF16),
                   jax.ShapeDtypeStruct((S, D_ATT), BF16), jax.ShapeDtypeStruct((N_PAIR, 1, 384), F32),
                   jax.ShapeDtypeStruct((N_PAIR, S, 128), F32),
                   jax.ShapeDtypeStruct((N_PAIR, n_t, 8, T), F32),
                   jax.ShapeDtypeStruct((r_out, c_out), F32)),
        in_specs=[pl.BlockSpec((S, 128), lambda hp, j: (0, hp)),
                  pl.BlockSpec((S, 128), lambda hp, j: (0, hp)),
                  pl.BlockSpec((S, 128), lambda hp, j: (0, hp)),
                  pl.BlockSpec((None, n_t, 8, T), lambda hp, j: (hp, 0, 0, 0)),
                  pl.BlockSpec((T, 128), lambda hp, j: (j, N_PAIR + hp)),
                  pl.BlockSpec((T, 128), lambda hp, j: (j, 2 * N_PAIR + hp)),
                  pl.BlockSpec((T, 128), lambda hp, j: (j, 0)),
                  vm, vm],
        out_specs=(pl.BlockSpec((S, 128), lambda hp, j: (0, hp)),
                   pl.BlockSpec((T, 128), lambda hp, j: (j, hp)),
                   pl.BlockSpec((T, 128), lambda hp, j: (j, hp)),
                   pl.BlockSpec((None, 1, 384), pair_rows),
                   pl.BlockSpec((None, T, 128), lambda hp, j: (hp, j, 0)),
                   pl.BlockSpec((None, n_t, 8, T), lambda hp, j: (hp, 0, 0, 0)),
                   vm),
        scratch_shapes=[pltpu.VMEM((n_t, 8, T), F32), pltpu.VMEM((n_t, 128, T), F32),
                        pltpu.VMEM((2, S, 128), BF16), pltpu.VMEM((r_out, c_out), F32)]
        + _scatter_scratch(r_out, c_out),
        compiler_params=_params(dimension_semantics=("arbitrary", "arbitrary")),
    )(qkv, datt, att, lse, qkv, qkv, big_f, gw_out4, sc_out)


def _window_counts(first_row, n_rows, window):
    t = lax.broadcasted_iota(jnp.int32, (n_rows, 1), 0) + first_row
    return jnp.minimum((t + 1).astype(F32), float(window))


def _middle(x, tgt, att, g, p, gate, w_mix, b_mix, pool_scale, w_out, b_out, ln_g, ln_b):
    S = x.shape[0]
    tm = min(TM_MID, S)
    halo_blocks = tm // POOL_HALO

    def body(x_ref, t_ref, att_ref, g_ref, p_ref, ph_ref, gate_ref, wm_ref, bm_ref, ps_ref, wo_ref, bo_ref,
             lg_ref, lb_ref,
             dh_ref, datt_ref, dg_ref, dpl_ref, gwo_ref, gwm_ref, vec_ref, loss_ref):
        i = pl.program_id(0)

        @pl.when(i == 0)
        def _():
            gwo_ref[...] = jnp.zeros_like(gwo_ref)
            gwm_ref[...] = jnp.zeros_like(gwm_ref)
            vec_ref[...] = jnp.zeros_like(vec_ref)
            loss_ref[...] = jnp.zeros_like(loss_ref)

        pc = p_ref[...]
        halo = jnp.where(i > 0, ph_ref[...], 0.0)
        pe = jnp.concatenate([halo, pc], axis=0)
        pooled_parts = []
        for gi, w in enumerate(POOL_WINDOWS):
            cur = pe[:, gi * POOL_GROUP:(gi + 1) * POOL_GROUP]
            span = 1
            while span < w:
                cur = cur + pltpu.roll(cur, span, 0)
                span *= 2
            wsum = cur[POOL_HALO:, :]
            mean = wsum / _window_counts(i * tm, tm, w)
            pooled_parts.append(mean - pc[:, gi * POOL_GROUP:(gi + 1) * POOL_GROUP])
        pooled_bf =[v.astype(BF16) for v in pooled_parts]
        mixed = jnp.concatenate([_dot(pooled_bf[gi], wm_ref[gi]) for gi in range(4)], axis=1) + bm_ref[...]
        ps = ps_ref[...]
        pool_out = mixed * ps
        gv = g_ref[...]
        sig = _sigmoid(gv)
        silu = gv * sig
        att = att_ref[...]
        y = jnp.concatenate([att * silu[:, :D_ATT], pool_out * silu[:, D_ATT:]], axis=1)
        y_bf = y.astype(BF16)
        wo = wo_ref[...]
        yo = _dot(y_bf, wo) + bo_ref[...]
        gate = gate_ref[...]
        h = ALPHA * x_ref[...] + gate * yo
        mu = jnp.mean(h, axis=1, keepdims=True)
        hc = h - mu
        var = jnp.mean(hc * hc, axis=1, keepdims=True)
        rstd = lax.rsqrt(var + LN_EPS)
        yhat = hc * rstd
        lg = lg_ref[...]
        out = yhat * lg + lb_ref[...]
        err = out - t_ref[...]
        loss_ref[...] += 0.5 * jnp.sum(jnp.mean(err * err, axis=1, keepdims=True), axis=0, keepdims=True)

        dout = err * (1.0 / D)
        g_ln_b = _colsum(dout)
        g_ln_g = _colsum(dout * yhat)
        dyh = dout * lg
        dh = rstd * (dyh - jnp.mean(dyh, axis=1, keepdims=True)
                     - yhat * jnp.mean(dyh * yhat, axis=1, keepdims=True))
        dh_ref[...] = dh
        d_gate = _colsum(dh * yo)
        dyo = gate * dh
        g_b_out = _colsum(dyo)
        dyo_bf = dyo.astype(BF16)
        gwo_ref[...] += _dot_tn(y_bf, dyo_bf)
        dy = _dot_nt(dyo_bf, wo)
        dsilu = sig * (1.0 + gv * (1.0 - sig))
        dy_a = dy[:, :D_ATT]
        dy_p = dy[:, D_ATT:]
        datt_ref[...] = (dy_a * silu[:, :D_ATT]).astype(BF16)
        dpo = dy_p * silu[:, D_ATT:]
        dg = jnp.concatenate([dy_a * att * dsilu[:, :D_ATT], dy_p * pool_out * dsilu[:, D_ATT:]], axis=1)
        dg_ref[...] = dg.astype(BF16)
        g_dg = _colsum(dg)
        g_ps = _colsum(dpo * mixed)
        dmixed = dpo * ps
        g_bm = _colsum(dmixed)
        dmixed_bf = dmixed.astype(BF16)
        dpl = []
        for gi in range(4):
            dm = dmixed_bf[:, gi * POOL_GROUP:(gi + 1) * POOL_GROUP]
            gwm_ref[gi] += _dot_tn(pooled_bf[gi], dm)
            dpl.append(_dot_nt(dm, wm_ref[gi]))
        dpl_ref[...] = jnp.concatenate(dpl, axis=1)
        vec_ref[0:1, :] += g_ln_g
        vec_ref[1:2, :] += g_ln_b
        vec_ref[2:3, :] += d_gate
        vec_ref[3:4, :] += g_b_out
        vec_ref[4:5, :] += g_dg
        vec_ref[5:6, 0:D_POOL] += g_ps
        vec_ref[6:7, 0:D_POOL] += g_bm

    row = lambda w: pl.BlockSpec((tm, w), lambda i: (i, 0))
    full2 = lambda a: pl.BlockSpec(a.shape, lambda i: (0, 0))
    full3 = lambda a: pl.BlockSpec(a.shape, lambda i: (0, 0, 0))
    return pl.pallas_call(
        body, name="middle", grid=(S // tm,),
        out_shape=(jax.ShapeDtypeStruct((S, D), F32),
                   jax.ShapeDtypeStruct((S, D_ATT), BF16),
                   jax.ShapeDtypeStruct((S, D), BF16),
                   jax.ShapeDtypeStruct((S, D_POOL), F32),
                   jax.ShapeDtypeStruct((D, D), F32),
                   jax.ShapeDtypeStruct((4, POOL_GROUP, POOL_GROUP), F32),
                   jax.ShapeDtypeStruct((8, D), F32),
                   jax.ShapeDtypeStruct((1, 1), F32)),
        in_specs=[row(D), row(D), row(D_ATT), row(D), row(D_POOL),
                  pl.BlockSpec((POOL_HALO, D_POOL), lambda i: (jnp.maximum(i * halo_blocks - 1, 0), 0)),
                  full2(gate), full3(w_mix), full2(b_mix), full2(pool_scale), full2(w_out), full2(b_out),
                  full2(ln_g), full2(ln_b)],
        out_specs=(row(D), row(D_ATT), row(D), row(D_POOL),
                   pl.BlockSpec((D, D), lambda i: (0, 0)),
                   pl.BlockSpec((4, POOL_GROUP, POOL_GROUP), lambda i: (0, 0, 0)),
                   pl.BlockSpec((8, D), lambda i: (0, 0)),
                   pl.BlockSpec((1, 1), lambda i: (0, 0))),
        compiler_params=_params(dimension_semantics=("arbitrary",)),
    )(x, tgt, att, g, p, p, gate, w_mix, b_mix, pool_scale, w_out, b_out, ln_g, ln_b)


def _tail(dpl, dfk, dfq, f):
    S = dpl.shape[0]
    tm = min(T_ATT, S)
    n_t = S // tm
    halo_blocks = tm // POOL_HALO
    last_halo = S // POOL_HALO - 1

    def body(d_ref, dn_ref, dfk_ref, dfq_ref, f_ref, dp_ref, df_ref, cs_ref, carry):
        s = pl.program_id(0)
        i = n_t - 1 - s

        @pl.when(s == 0)
        def _():
            carry[...] = jnp.zeros_like(carry)
            cs_ref[...] = jnp.zeros_like(cs_ref)

        dc = d_ref[...]
        nxt = jnp.where(s > 0, dn_ref[...], 0.0)
        de = jnp.concatenate([dc, nxt], axis=0)
        n_e = tm + POOL_HALO
        parts = []
        for gi, w in enumerate(POOL_WINDOWS):
            cur = de[:, gi * POOL_GROUP:(gi + 1) * POOL_GROUP] / _window_counts(i * tm, n_e, w)
            span = 1
            while span < w:
                cur = cur + pltpu.roll(cur, n_e - span, 0)
                span *= 2
            parts.append(cur[:tm, :] - dc[:, gi * POOL_GROUP:(gi + 1) * POOL_GROUP])
        dp = jnp.concatenate(parts, axis=1)
        dp_ref[...] = dp.astype(BF16)
        cs_ref[0:1, :] += _colsum(dp)

        r = lax.broadcasted_iota(jnp.int32, (tm, tm), 0)
        c = lax.broadcasted_iota(jnp.int32, (tm, tm), 1)
        tri = (r >= c).astype(F32)
        k_cols = dfk_ref[0]
        rows8 = dfq_ref[0]
        for hp in range(1, N_PAIR):
            k_cols = k_cols + pltpu.roll(dfk_ref[hp], 2 * hp, 1)
            rows8 = rows8 + pltpu.roll(dfq_ref[hp], 2 * hp, 0)
        rows8 = rows8 + k_cols.T[0:8, :]
        dlogf8 = jnp.dot(rows8, tri, preferred_element_type=F32, precision=lax.Precision.HIGHEST) + carry[...]
        first = lax.broadcasted_iota(jnp.int32, (1, tm), 1) == 0
        carry[...] = jnp.sum(jnp.where(first, dlogf8, 0.0), axis=1, keepdims=True)
        dlogf = jnp.concatenate([dlogf8, jnp.zeros((128 - 8, tm), F32)], axis=0).T
        df = dlogf * _sigmoid(-f_ref[...])
        df_ref[...] = df.astype(BF16)
        cs_ref[1:2, 0:128] += _colsum(df)

    rev = lambda w: pl.BlockSpec((tm, w), lambda s: (n_t - 1 - s, 0))
    return pl.pallas_call(
        body, name="tail", grid=(n_t,),
        out_shape=(jax.ShapeDtypeStruct((S, D_POOL), BF16), jax.ShapeDtypeStruct((S, 128), BF16),
                   jax.ShapeDtypeStruct((8, D_POOL), F32)),
        in_specs=[rev(D_POOL),
                  pl.BlockSpec((POOL_HALO, D_POOL),
                               lambda s: (jnp.minimum((n_t - s) * halo_blocks, last_halo), 0)),
                  pl.BlockSpec((N_PAIR, tm, 128), lambda s: (0, n_t - 1 - s, 0)),
                  pl.BlockSpec((N_PAIR, None, 8, tm), lambda s: (0, n_t - 1 - s, 0, 0)),
                  rev(128)],
        out_specs=(rev(D_POOL), rev(128), pl.BlockSpec((8, D_POOL), lambda s: (0, 0))),
        scratch_shapes=[pltpu.VMEM((8, 1), F32)],
        compiler_params=_params(dimension_semantics=("arbitrary",)),
    )(dpl, dpl, dfk, dfq, f)


PIECES = ((O_QKV, D_ATT), (O_QKV + D_ATT, D_ATT), (O_QKV + 2 * D_ATT, D_ATT), (O_F, 128), (O_P, D_POOL), (O_G, D))


def _grad_w_in(u, pieces):
    S = u.shape[0]
    tm = min(TM_GW, S)
    n_t = S // tm

    def body(u_ref, *rest):
        piece_refs, out_ref, acc, sem = rest[:6], rest[6], rest[7], rest[8]
        i = pl.program_id(0)

        @pl.when(i == 0)
        def _():
            acc[...] = jnp.zeros_like(acc)

        u_t = u_ref[...]
        for (off, w), ref in zip(PIECES, piece_refs):
            acc[:, off:off + w] += _dot_tn(u_t, ref[...])

        @pl.when(i == n_t - 1)
        def _():
            cp = pltpu.make_async_copy(acc, out_ref, sem)
            cp.start()
            cp.wait()

    return pl.pallas_call(
        body, name="grad_w_in", grid=(n_t,),
        out_shape=jax.ShapeDtypeStruct((D, D_PAD), F32),
        in_specs=[pl.BlockSpec((tm, D), lambda i: (i, 0))]
        + [pl.BlockSpec((tm, w), lambda i: (i, 0)) for _, w in PIECES],
        out_specs=pl.BlockSpec(memory_space=pl.ANY),
        scratch_shapes=[pltpu.VMEM((D, D_PAD), F32), pltpu.SemaphoreType.DMA],
        compiler_params=_params(dimension_semantics=("arbitrary",)),
    )(u, *pieces)


def _grad_x(pieces, wt_pad, dh, x, scale):
    S = x.shape[0]
    tm = min(TM_DU, S)

    def body(*refs):
        piece_refs = refs[:6]
        w_ref, dh_ref, x_ref, sc_ref, gx_ref, vec_ref = refs[6:]

        @pl.when(pl.program_id(0) == 0)
        def _():
            vec_ref[...] = jnp.zeros_like(vec_ref)

        du = jnp.zeros((tm, D), F32)
        for (off, w), ref in zip(PIECES, piece_refs):
            du = du + _dot(ref[...], w_ref[off:off + w, :])
        xv = x_ref[...]
        gx_ref[...] = ALPHA * dh_ref[...] + du * (1.0 + sc_ref[...])
        vec_ref[0:1, :] += _colsum(du)
        vec_ref[1:2, :] += _colsum(du * xv)

    row = lambda w: pl.BlockSpec((tm, w), lambda i: (i, 0))
    return pl.pallas_call(
        body, name="grad_x", grid=(S // tm,),
        out_shape=(jax.ShapeDtypeStruct((S, D), F32), jax.ShapeDtypeStruct((8, D), F32)),
        in_specs=[row(w) for _, w in PIECES]
        + [pl.BlockSpec(wt_pad.shape, lambda i: (0, 0)), row(D), row(D), pl.BlockSpec((1, D), lambda i: (0, 0))],
        out_specs=(row(D), pl.BlockSpec((8, D), lambda i: (0, 0))),
        compiler_params=_params(dimension_semantics=("arbitrary",)),
    )(*pieces, wt_pad, dh, x, scale)


def _adamw_math(w, g, m, v):
    m = ADAM_B1 * m + (1.0 - ADAM_B1) * g
    v = ADAM_B2 * v + (1.0 - ADAM_B2) * (g * g)
    m_hat = m / (1.0 - ADAM_B1 ** ADAM_STEP)
    v_hat = v / (1.0 - ADAM_B2 ** ADAM_STEP)
    delta = -ADAM_LR * (m_hat / (jnp.sqrt(v_hat) + ADAM_EPS) + ADAM_WD * w)
    return delta, m, v


def _adamw(groups, n_steps):
    n = len(groups)

    def body(*refs):
        ins, outs = refs[:4 * n], refs[4 * n:]
        for t in range(n):
            w, g, m, v = (r[...] for r in ins[4 * t:4 * t + 4])
            d, m2, v2 = _adamw_math(w, g, m, v)
            outs[3 * t][...] = d
            outs[3 * t + 1][...] = m2
            outs[3 * t + 2][...] = v2

    in_specs, out_specs, out_shape, args = [], [], [], []
    for (w, g, m, v) in groups:
        rest = w.shape[1:]
        spec = pl.BlockSpec((w.shape[0] // n_steps,) + rest, lambda i, nd=len(rest): (i,) + (0,) * nd)
        in_specs += [spec] * 4
        out_specs += [spec] * 3
        out_shape += [jax.ShapeDtypeStruct(w.shape, F32)] * 3
        args += [w, g, m, v]
    return pl.pallas_call(
        body, name="adamw_%d_%d" % (n, n_steps), grid=(n_steps,),
        out_shape=tuple(out_shape), in_specs=in_specs, out_specs=tuple(out_specs),
        compiler_params=_params(dimension_semantics=("arbitrary",)),
    )(*args)


def _adamw_small(small_sum, g_b_ada, params):
    n = len(params)

    def body(gs_ref, gba_ref, *refs):
        ins, outs = refs[:3 * n], refs[3 * n:]
        for t, (name, w0, _, _) in enumerate(params):
            w_ref, m_ref, v_ref = ins[3 * t:3 * t + 3]
            first = SMALL_SEGS[name][0] if name in SMALL_SEGS else None
            if w0.shape[0] > 1:
                pieces = [((slice(None), slice(None)), gs_ref[first:first + w0.shape[0], :])]
            else:
                pieces = []
                for r in range(-(-w0.shape[1] // 128)):
                    lanes = slice(128 * r, min(128 * r + 128, w0.shape[1]))
                    g = gba_ref[0:1, lanes] if first is None else gs_ref[first + r:first + r + 1, 0:lanes.stop - lanes.start]
                    pieces.append(((slice(0, 1), lanes), g))
            for where, g in pieces:
                d, m2, v2 = _adamw_math(w_ref[where], g, m_ref[where], v_ref[where])
                for ref, val in zip(outs[4 * t:4 * t + 4], (g, d, m2, v2)):
                    ref[where] = val

    vm = pl.BlockSpec(memory_space=pltpu.VMEM)
    args = [small_sum, g_b_ada]
    out_shape = []
    for _, w, m, v in params:
        args += [w, m, v]
        out_shape += [jax.ShapeDtypeStruct(w.shape, F32)] * 4
    return pl.pallas_call(
        body, name="adamw_small",
        out_shape=tuple(out_shape), in_specs=[vm] * len(args), out_specs=(vm,) * len(out_shape),
        compiler_params=_params(),
    )(*args)


def _pack_small(parts):
    rows = []
    used = 0
    for name, (first, n_rows) in SMALL_SEGS.items():
        if first > used:
            rows.append(jnp.zeros((first - used, 128), F32))
        flat = parts[name].reshape(-1)
        flat = jnp.pad(flat, (0, n_rows * 128 - flat.shape[0]))
        rows.append(flat.reshape(n_rows, 128))
        used = first + n_rows
    rows.append(jnp.zeros((SMALL_ROWS - used, 128), F32))
    return jnp.concatenate(rows, axis=0)


def _unpack_small(buf, name, shape):
    first, n_rows = SMALL_SEGS[name]
    n = int(np.prod(shape))
    return buf[first:first + n_rows].reshape(-1)[:n].reshape(shape)


def _pad_in(v):
    r = v.shape[0]
    z = jnp.zeros((r, O_P - O_F - N_HEADS), v.dtype)
    return jnp.concatenate([v[:, :3 * D_ATT + N_HEADS], z, v[:, 3 * D_ATT + N_HEADS:]], axis=1)


def _unpad_in(v):
    return jnp.concatenate([v[:, :O_F + N_HEADS], v[:, O_P:]], axis=1)


def _shards_in(v):
    gap = O_P - (O_F + N_HEADS)
    parts = []
    for a in range(N_CHIPS):
        lo, hi = a * SHARD_IN, (a + 1) * SHARD_IN
        cut = O_F + N_HEADS
        if hi <= cut:
            parts.append(v[:, lo:hi])
        elif lo >= cut:
            parts.append(v[:, lo + gap:hi + gap])
        else:
            parts.append(jnp.concatenate([v[:, lo:cut], v[:, cut + gap:hi + gap]], axis=1))
    return jnp.stack(parts, axis=0)


def kernel(x, c, w_ada, b_ada, w_in, b_in, w_pool_mix, b_pool_mix, pool_scale, w_out, b_out, ln_g, ln_b, loss_target, m_w_ada, m_b_ada, m_w_in, m_b_in, m_w_pool_mix, m_b_pool_mix, m_pool_scale, m_w_out, m_b_out, m_ln_g, m_ln_b, v_w_ada, v_b_ada, v_w_in, v_b_in, v_w_pool_mix, v_b_pool_mix, v_pool_scale, v_w_out, v_b_out, v_ln_g, v_ln_b):
    S = x.shape[1]
    T = min(T_ATT, S)
    n_t = S // T
    x2 = x[0]
    tgt = loss_target[0]
    q_scale = jnp.concatenate([jnp.full((1, D_ATT), Q_SCALE, F32), jnp.ones((1, D_PAD - D_ATT), F32)], axis=1)

    to_cols = lambda a: jnp.transpose(a, (2, 0, 1))
    from_cols = lambda a: jnp.transpose(a, (1, 2, 0))
    c_all, ada4, wt_pad = _gather_and_ada(
        c, w_ada[0], b_ada.reshape(4, 1, SHARD_ADA), to_cols(w_in))
    ada = ada4[:, 0, :].reshape(1, 3 * D)
    shift, scale, gate = ada[:, :D], ada[:, D:2 * D], ada[:, 2 * D:]
    b_pad = _pad_in(b_in) * q_scale
    w_mix_bf = w_pool_mix[0].astype(BF16)

    u, qkv, f, p, g, w_out_all = _in_proj(x2, shift, scale, wt_pad, b_pad, w_out[0].astype(BF16))
    w_out_full = w_out_all.reshape(D, D)
    big_f = _forget_cumsum(f)
    att, lse = _attention_fwd(qkv, big_f)

    dh, datt, dg, dpl, gw_out, gw_mix, vec, loss_part = _middle(
        x2, tgt, att, g, p, gate, w_mix_bf, b_pool_mix.reshape(1, D_POOL), pool_scale, w_out_full, b_out, ln_g, ln_b)
    dq, dk, dv, cs_att, dfk, dfq, g_w_out = _attention_bwd(
        qkv, datt, att, lse, big_f, gw_out.reshape(N_CHIPS, SHARD_OUT, D), jnp.ones((N_CHIPS, 1, D), F32))
    dp, df, cs_tail = _tail(dpl, dfk, dfq, f)
    pieces = (dq, dk, dv, df, dp, dg)
    gw_pad = _grad_w_in(u, pieces)
    grad_x, vec_x = _grad_x(pieces, wt_pad, dh, x2, scale)

    cs_qkv = jnp.transpose(cs_att.reshape(N_PAIR, 3, 128), (1, 0, 2)).reshape(1, 3 * D_ATT)
    gb_pad = jnp.concatenate([cs_qkv, cs_tail[1:2, 0:128], cs_tail[0:1, :], vec[4:5, :]], axis=1) * q_scale
    dada = jnp.concatenate([vec_x[0:1, :], vec_x[1:2, :], vec[2:3, :]], axis=1)
    small = _pack_small({
        "b_in": _unpad_in(gb_pad), "w_pool_mix": gw_mix, "b_pool_mix": vec[6:7, :D_POOL],
        "pool_scale": vec[5:6, :D_POOL], "b_out": vec[3:4, :], "ln_g": vec[0:1, :], "ln_b": vec[1:2, :],
        "loss": loss_part})

    g_w_in, small_sum, g_w_ada, g_b_ada = _reduce_all(
        gw_pad, _shards_in(q_scale), small, dada, c_all)
    loss = _unpack_small(small_sum, "loss", (1,))[0]

    big = _adamw([(w_ada[0], g_w_ada, m_w_ada[0], v_w_ada[0]),
                  (w_out[0], g_w_out, m_w_out[0], v_w_out[0])], 4)
    g_w_in_cols = g_w_in
    big_in = _adamw([(to_cols(w_in), g_w_in_cols, to_cols(m_w_in), to_cols(v_w_in))], 5)
    tiles = lambda a: a.reshape(4 * POOL_GROUP, POOL_GROUP)
    flat = lambda a: a.reshape(1, D_POOL)
    small_params = [("b_ada", b_ada, m_b_ada, v_b_ada), ("b_in", b_in, m_b_in, v_b_in),
                    ("w_pool_mix", tiles(w_pool_mix), tiles(m_w_pool_mix), tiles(v_w_pool_mix)),
                    ("b_pool_mix", flat(b_pool_mix), flat(m_b_pool_mix), flat(v_b_pool_mix)),
                    ("pool_scale", pool_scale, m_pool_scale, v_pool_scale), ("b_out", b_out, m_b_out, v_b_out),
                    ("ln_g", ln_g, m_ln_g, v_ln_g), ("ln_b", ln_b, m_ln_b, v_ln_b)]
    sm = _adamw_small(small_sum, g_b_ada, small_params)
    sm_idx = {p[0]: n for n, p in enumerate(small_params)}
    shapes = {"w_pool_mix": (1, 4, POOL_GROUP, POOL_GROUP), "b_pool_mix": (1, 4, POOL_GROUP)}

    names = ["w_ada", "b_ada", "w_in", "b_in", "w_pool_mix", "b_pool_mix", "pool_scale", "w_out", "b_out",
             "ln_g", "ln_b"]
    big_idx = {"w_ada": 0, "w_out": 1}

    def leaf(kind, name):
        if name == "w_in":
            return from_cols(g_w_in_cols if kind == 0 else big_in[kind - 1])
        if name in big_idx:
            if kind == 0:
                return (g_w_ada, g_w_out)[big_idx[name]][None]
            return big[3 * big_idx[name] + kind - 1][None]
        val = sm[4 * sm_idx[name] + kind]
        return val.reshape(shapes[name]) if name in shapes else val

    outs = [loss, grad_x[None]]
    for kind in range(4):
        outs += [leaf(kind, n) for n in names]
    return tuple(outs)
```

```python
import functools

import numpy as np
import jax
import jax.numpy as jnp
from jax import lax
from jax.experimental import pallas as pl
from jax.experimental.pallas import tpu as pltpu

F32 = jnp.float32
BF16 = jnp.bfloat16
MESH = pl.DeviceIdType.MESH

D = 1024
D_ATT = 512
D_POOL = 512
N_HEADS = 8
HEAD_DIM = 64
N_PAIR = N_HEADS // 2
POOL_WINDOWS = (2, 4, 8, 16)
POOL_GROUP = 128
POOL_HALO = 16
LN_EPS = 1e-5
ALPHA = 2.0 ** 0.25
D_IN = 3 * D_ATT + N_HEADS + D_POOL + D_ATT + D_POOL
N_CHIPS = 4
SHARD_IN = D_IN // N_CHIPS
SHARD_ADA = 3 * D // N_CHIPS
SHARD_OUT = D // N_CHIPS

O_QKV, O_F, O_P, O_G, D_PAD = 0, 1536, 1664, 2176, 3200
Q_SCALE = HEAD_DIM ** -0.5

ADAM_LR, ADAM_B1, ADAM_B2, ADAM_EPS, ADAM_WD, ADAM_STEP = 0.001, 0.9, 0.999, 1e-08, 0.01, 10

NEG = -1e30

VMEM_LIMIT = 56 * 1024 * 1024

TM_PROJ = 512
T_ATT = 512
TM_MID = 256
TM_GW = 1024
TM_DU = 512

REL7 = [(0, 0, 1), (0, 1, 0), (0, 1, 1), (1, 0, 0), (1, 0, 1), (1, 1, 0), (1, 1, 1)]
REL3 = [(0, 1), (1, 0), (1, 1)]

SMALL_SEGS = {}
_row = 0
for _name, _n in (("b_in", D_IN), ("w_pool_mix", 65536), ("b_pool_mix", 512), ("pool_scale", 512),
                  ("b_out", 1024), ("ln_g", 1024), ("ln_b", 1024), ("loss", 1)):
    _rows = -(-_n // 1024) * 8
    SMALL_SEGS[_name] = (_row, _rows)
    _row += _rows
SMALL_ROWS = -(-_row // 16) * 16


def _params(**kw):
    return pltpu.CompilerParams(vmem_limit_bytes=VMEM_LIMIT, **kw)


def _flip(v, d):
    return v if d == 0 else 1 - v


def _dot(a, b):
    return jnp.dot(a, b, preferred_element_type=F32)


def _dot_nt(a, b):
    return lax.dot_general(a, b, (((1,), (1,)), ((), ())), preferred_element_type=F32)


def _dot_tn(a, b):
    return lax.dot_general(a, b, (((0,), (0,)), ((), ())), preferred_element_type=F32)


def _sigmoid(v):
    return 1.0 / (1.0 + jnp.exp(-v))


def _colsum(v):
    return jnp.sum(v, axis=0, keepdims=True)


def _gather_stages(pos, src_ref, dst_ref, half, own_sem, s_sem, r_sem, fs_sem, fr_sem):
    x, y, cc, chip, sib = pos
    own = pltpu.make_async_copy(src_ref, dst_ref.at[chip], own_sem)
    first, landed, others = [], [], []
    for k, (dx, dy) in enumerate(REL3):
        px, py = _flip(x, dx), _flip(y, dy)
        first.append(pltpu.make_async_remote_copy(
            src_ref=src_ref.at[half(cc)], dst_ref=dst_ref.at[(chip,) + half(cc)],
            send_sem=s_sem.at[k], recv_sem=r_sem.at[k], device_id=(px, py, cc), device_id_type=MESH))
        landed.append(dst_ref.at[(2 * px + py,) + half(cc)])
        others.append(dst_ref.at[(2 * px + py,) + half(1 - cc)])
    passed = [pltpu.make_async_remote_copy(src_ref=landed[k], dst_ref=landed[k], send_sem=fs_sem.at[k],
                                           recv_sem=fr_sem.at[k], device_id=sib, device_id_type=MESH)
              for k in range(3)]

    def start(finish_src=None):
        for cp in first:
            cp.start()
        if finish_src is not None:
            finish_src()
        own.start()

    def forward():
        for k in range(3):
            pltpu.make_async_remote_copy(src_ref=landed[k], dst_ref=landed[k], send_sem=s_sem.at[k],
                                         recv_sem=r_sem.at[k], device_id=sib, device_id_type=MESH).wait_recv()
            passed[k].start()

    def finish():
        for k in range(3):
            pltpu.make_async_remote_copy(src_ref=others[k], dst_ref=others[k], send_sem=fs_sem.at[k],
                                         recv_sem=fr_sem.at[k], device_id=sib, device_id_type=MESH).wait_recv()
        for cp in first + passed:
            cp.wait_send()
        own.wait()

    return start, forward, finish


def _gather_scratch():
    return [pltpu.SemaphoreType.DMA, pltpu.SemaphoreType.DMA((3,)), pltpu.SemaphoreType.DMA((3,)),
            pltpu.SemaphoreType.DMA((3,)), pltpu.SemaphoreType.DMA((3,))]


def _gather_and_ada(c, w_ada, b_ada4, w_in_sh):
    def body(c_ref, w_ref, b_ref, win_ref, call_ref, ada_ref, wt_pad_ref,
             win_all, win_bf, cslab, sbuf, rbuf, cs_sem, cr_sem, as_sem, ar_sem, *gather_sems):
        x, y, cc = lax.axis_index("x"), lax.axis_index("y"), lax.axis_index("c")
        me = 4 * x + 2 * y + cc
        chip = 2 * x + y
        lane_half = lambda which: (slice(None), pl.ds(pl.multiple_of(which * (D // 2), D // 2), D // 2))
        def round_half(which):
            for h in range(2):
                @pl.when(which == h)
                def _():
                    lanes = slice(h * (D // 2), (h + 1) * (D // 2))
                    win_bf[:, lanes] = win_ref[:, 0, lanes].astype(BF16)

        start, forward, finish = _gather_stages((x, y, cc, chip, (x, y, 1 - cc)), win_bf, win_all, lane_half,
                                                *gather_sems)
        round_half(cc)
        start(lambda: round_half(1 - cc))

        cslab[...] = jnp.broadcast_to(c_ref[...], (8, D))
        call_ref[me] = cslab[...]
        gathers = []
        for k, (dx, dy, dc) in enumerate(REL7):
            cp = pltpu.make_async_remote_copy(
                src_ref=cslab, dst_ref=call_ref.at[me], send_sem=cs_sem.at[k], recv_sem=cr_sem.at[k],
                device_id=(_flip(x, dx), _flip(y, dy), _flip(cc, dc)), device_id_type=MESH)
            cp.start()
            gathers.append(cp)
        for cp in gathers:
            cp.wait()
        slab_row = lax.broadcasted_iota(jnp.int32, (8, 1), 0)
        mat = jnp.zeros((8, D), F32)
        for r in range(8):
            mat = jnp.where(slab_row == r, call_ref[r], mat)
        act = (mat * _sigmoid(mat)).astype(BF16)
        part = _dot(act, w_ref[...].astype(BF16))
        sends = []
        for k, (dx, dy) in enumerate(REL3):
            px, py = _flip(x, dx), _flip(y, dy)
            r = 4 * px + 2 * py + cc
            piece = _colsum(jnp.where(slab_row == r, part, 0.0))
            sbuf[k] = jnp.broadcast_to(piece, (8, SHARD_ADA))
            cp = pltpu.make_async_remote_copy(
                src_ref=sbuf.at[k], dst_ref=rbuf.at[k], send_sem=as_sem.at[k], recv_sem=ar_sem.at[k],
                device_id=(px, py, cc), device_id_type=MESH)
            cp.start()
            sends.append(cp)
        own_piece = _colsum(jnp.where(slab_row == me, part, 0.0))
        ada_ref[chip] = jnp.broadcast_to(own_piece, (8, SHARD_ADA)) + b_ref[chip]
        for k, (dx, dy) in enumerate(REL3):
            sends[k].wait()
            a = 2 * _flip(x, dx) + _flip(y, dy)
            ada_ref[a] = rbuf[k] + b_ref[a]

        forward()
        finish()
        n_real = 3 * D_ATT + N_HEADS
        for a in range(N_CHIPS):
            lo, hi = a * SHARD_IN, (a + 1) * SHARD_IN
            for s0, s1 in ((lo, min(hi, D_ATT)), (max(lo, D_ATT), min(hi, n_real)), (max(lo, n_real), hi)):
                if s0 < s1:
                    rows = win_all[a, s0 - lo:s1 - lo, :]
                    if s1 <= D_ATT:
                        rows = rows * jnp.asarray(Q_SCALE, BF16)
                    shift = O_P - n_real if s0 >= n_real else 0
                    wt_pad_ref[s0 + shift:s1 + shift, :] = rows
        wt_pad_ref[n_real:O_P, :] = jnp.zeros((O_P - n_real, D), BF16)

    vm = pl.BlockSpec(memory_space=pltpu.VMEM)
    return pl.pallas_call(
        body, name="gather_and_ada",
        out_shape=(jax.ShapeDtypeStruct((8, 8, D), F32), jax.ShapeDtypeStruct((4, 8, SHARD_ADA), F32),
                   jax.ShapeDtypeStruct((D_PAD, D), BF16)),
        in_specs=[vm] * 4, out_specs=(vm,) * 3,
        scratch_shapes=[pltpu.VMEM((N_CHIPS, SHARD_IN, D), BF16), pltpu.VMEM((SHARD_IN, D), BF16),
                        pltpu.VMEM((8, D), F32), pltpu.VMEM((3, 8, SHARD_ADA), F32),
                        pltpu.VMEM((3, 8, SHARD_ADA), F32),
                        pltpu.SemaphoreType.DMA((7,)), pltpu.SemaphoreType.DMA((7,)),
                        pltpu.SemaphoreType.DMA((3,)), pltpu.SemaphoreType.DMA((3,))] + _gather_scratch(),
        compiler_params=_params(),
    )(c, w_ada, b_ada4, w_in_sh)


def _shard_cols():
    cut, gap = O_F + N_HEADS, O_P - (O_F + N_HEADS)
    out = []
    for a in range(N_CHIPS):
        lo, hi = a * SHARD_IN, (a + 1) * SHARD_IN
        out.append(([(lo, min(hi, cut))] if lo < cut else []) + ([(max(lo, cut) + gap, hi + gap)] if hi > cut else []))
    return out


def _scatter_stages(pos, g_ref, sc_ref, out_ref, sib_buf, send_buf, ici_buf, sem1, sem2s, sem2r, sem3, part=(0, 1),
                    cols=None, own_buf=None):
    x, y, cc, chip, sib = pos
    q, n_parts = part
    RH = (g_ref.shape[1] if cols is None else g_ref.shape[0]) // 2 // n_parts
    mine = pl.ds(pl.multiple_of((cc * n_parts + q) * RH, RH), RH)
    theirs = pl.ds(pl.multiple_of(((1 - cc) * n_parts + q) * RH, RH), RH)
    cp1 = pltpu.make_async_remote_copy(
        src_ref=g_ref.at[:, theirs, :] if cols is None else g_ref.at[theirs, :], dst_ref=sib_buf,
        send_sem=sem1.at[0], recv_sem=sem1.at[1], device_id=sib, device_id_type=MESH)
    sends = []
    for k, (dx, dy) in enumerate(REL3):
        px, py = _flip(x, dx), _flip(y, dy)
        sends.append(pltpu.make_async_remote_copy(
            src_ref=send_buf.at[2 * px + py], dst_ref=ici_buf.at[chip],
            send_sem=sem2s.at[k], recv_sem=sem2r.at[k], device_id=(px, py, cc), device_id_type=MESH))
    cp3 = pltpu.make_async_remote_copy(
        src_ref=out_ref.at[mine, :], dst_ref=out_ref.at[mine, :], send_sem=sem3.at[0], recv_sem=sem3.at[1],
        device_id=sib, device_id_type=MESH)

    def finish1():
        cp1.wait()
        if cols is None:
            for a in range(N_CHIPS):
                both = g_ref[a, mine, :] + sib_buf[a]
                sib_buf[a] = both
                send_buf[a] = both.astype(BF16)
        else:
            both = g_ref[mine, :] + sib_buf[...]
            for a, pieces in enumerate(cols):
                at = 0
                for lo, hi in pieces:
                    own_buf[a, :, at:at + hi - lo] = both[:, lo:hi]
                    send_buf[a, :, at:at + hi - lo] = both[:, lo:hi].astype(BF16)
                    at += hi - lo

    def start2():
        for cp in sends:
            cp.start()
        ici_buf[chip] = send_buf[chip]

    def finish2():
        for cp in sends:
            cp.wait()
        own = (sib_buf if cols is None else own_buf)[chip]
        parts = [jnp.where(chip == a, own, ici_buf[a].astype(F32)) for a in range(N_CHIPS)]
        out_ref[mine, 0:own.shape[1]] = ((parts[0] + parts[1]) + (parts[2] + parts[3])) * sc_ref[chip]

    return [(cp1.start, finish1), (start2, finish2), (cp3.start, cp3.wait)]


def _all_reduce_stages(pos, g_ref, out_ref, sib_buf, ici_buf, sem1, sem2s, sem2r, sem3):
    x, y, cc, chip, sib = pos
    RH = g_ref.shape[0] // 2
    mine = pl.ds(pl.multiple_of(cc * RH, 8), RH)
    theirs = pl.ds(pl.multiple_of((1 - cc) * RH, 8), RH)
    cp1 = pltpu.make_async_remote_copy(
        src_ref=g_ref.at[theirs, :], dst_ref=sib_buf, send_sem=sem1.at[0], recv_sem=sem1.at[1],
        device_id=sib, device_id_type=MESH)
    sends = []
    for k, (dx, dy) in enumerate(REL3):
        px, py = _flip(x, dx), _flip(y, dy)
        sends.append(pltpu.make_async_remote_copy(
            src_ref=sib_buf, dst_ref=ici_buf.at[chip],
            send_sem=sem2s.at[k], recv_sem=sem2r.at[k], device_id=(px, py, cc), device_id_type=MESH))
    cp3 = pltpu.make_async_remote_copy(
        src_ref=out_ref.at[mine, :], dst_ref=out_ref.at[mine, :], send_sem=sem3.at[0], recv_sem=sem3.at[1],
        device_id=sib, device_id_type=MESH)

    def finish1():
        cp1.wait()
        sib_buf[...] = g_ref[mine, :] + sib_buf[...]

    def start2():
        for cp in sends:
            cp.start()
        ici_buf[chip] = sib_buf[...]

    def finish2():
        for cp in sends:
            cp.wait()
        out_ref[mine, :] = (ici_buf[0] + ici_buf[1]) + (ici_buf[2] + ici_buf[3])

    return [(cp1.start, finish1), (start2, finish2), (cp3.start, cp3.wait)]


def _stage_sems():
    return [pltpu.SemaphoreType.DMA((2,)), pltpu.SemaphoreType.DMA((3,)),
            pltpu.SemaphoreType.DMA((3,)), pltpu.SemaphoreType.DMA((2,))]


def _scatter_scratch(r, c):
    return [pltpu.VMEM((N_CHIPS, r // 2, c), F32), pltpu.VMEM((N_CHIPS, r // 2, c), BF16),
            pltpu.VMEM((N_CHIPS, r // 2, c), BF16)] + _stage_sems()


def _reduce_all(gw_pad, sc_in, small, dada, c_all):
    R = small.shape[0]
    W = dada.shape[1]
    r_in, p_in = gw_pad.shape
    c_in = SHARD_IN
    chunk = r_in // 4

    def chunk_scratch():
        return ([pltpu.VMEM((chunk, p_in), F32), pltpu.VMEM((N_CHIPS, chunk, c_in), BF16),
                 pltpu.VMEM((N_CHIPS, chunk, c_in), BF16)] + _stage_sems()
                + [pltpu.VMEM((N_CHIPS, chunk, c_in), F32)])

    n_in = len(chunk_scratch())

    c_wide = -(-c_in // 128) * 128

    def body(gin_ref, scin_ref, sm_ref, d_ref, c_ref, ocols_ref, osm_ref, gwa_ref, gba_ref, oin_ref, dall_ref,
             *scratch):
        x, y, cc = lax.axis_index("x"), lax.axis_index("y"), lax.axis_index("c")
        me = 4 * x + 2 * y + cc
        chip = 2 * x + y
        pos = (x, y, cc, chip, (x, y, 1 - cc))
        oin_ref[:, c_in:c_wide] = jnp.zeros((r_in, c_wide - c_in), F32)
        dslab, ds_sem, dr_sem = scratch[0:3]
        a_bufs, b_bufs, sm_bufs = scratch[3:3 + n_in], scratch[3 + n_in:3 + 2 * n_in], scratch[3 + 2 * n_in:]
        dslab[...] = jnp.broadcast_to(d_ref[...], (8, W))
        dall_ref[me] = dslab[...]
        gathers = []
        for k, (dx, dy, dc) in enumerate(REL7):
            cp = pltpu.make_async_remote_copy(
                src_ref=dslab, dst_ref=dall_ref.at[me], send_sem=ds_sem.at[k], recv_sem=dr_sem.at[k],
                device_id=(_flip(x, dx), _flip(y, dy), _flip(cc, dc)), device_id_type=MESH)
            cp.start()
            gathers.append(cp)
        cols = _shard_cols()
        first = _scatter_stages(pos, gin_ref, scin_ref, oin_ref, *a_bufs[:-1], part=(0, 2), cols=cols,
                                own_buf=a_bufs[-1])
        second = _scatter_stages(pos, gin_ref, scin_ref, oin_ref, *b_bufs[:-1], part=(1, 2), cols=cols,
                                 own_buf=b_bufs[-1])
        little = _all_reduce_stages(pos, sm_ref, osm_ref, *sm_bufs)
        for plan in (first, second, little):
            plan[0][0]()
        first[0][1]()
        first[1][0]()
        little[0][1]()
        little[1][0]()
        second[0][1]()
        second[1][0]()
        for cp in gathers:
            cp.wait()
        slab_row = lax.broadcasted_iota(jnp.int32, (8, 1), 0)
        cm = jnp.zeros((8, D), F32)
        dm = jnp.zeros((8, W), F32)
        for r in range(8):
            cm = jnp.where(slab_row == r, c_ref[r], cm)
            dm = jnp.where(slab_row == r, dall_ref[r], dm)
        act = cm * _sigmoid(cm)
        dcol = dm[:, 0:SHARD_ADA]
        for a in range(1, N_CHIPS):
            dcol = jnp.where(chip == a, dm[:, a * SHARD_ADA:(a + 1) * SHARD_ADA], dcol)
        lhs = jnp.concatenate([act, jnp.zeros((8, D), F32)], axis=0).astype(BF16)
        rhs = jnp.concatenate([dcol, jnp.zeros((8, SHARD_ADA), F32)], axis=0).astype(BF16)
        gwa_ref[...] = _dot_tn(lhs, rhs)
        gba_ref[...] = _colsum(dm)
        first[1][1]()
        first[2][0]()
        second[1][1]()
        second[2][0]()
        little[1][1]()
        little[2][0]()
        for plan in (first, second, little):
            plan[2][1]()
        ocols_ref[...] = oin_ref[...].T[0:c_in, :][:, None, :]

    scratch = [pltpu.VMEM((r_in, c_wide), F32), pltpu.VMEM((8, 8, W), F32),
               pltpu.VMEM((8, W), F32), pltpu.SemaphoreType.DMA((7,)), pltpu.SemaphoreType.DMA((7,))]
    scratch += chunk_scratch() + chunk_scratch()
    scratch += [pltpu.VMEM((R // 2, 128), F32), pltpu.VMEM((N_CHIPS, R // 2, 128), F32)] + _stage_sems()
    vm = pl.BlockSpec(memory_space=pltpu.VMEM)
    return pl.pallas_call(
        body, name="reduce_all",
        out_shape=(jax.ShapeDtypeStruct((c_in, 1, r_in), F32), jax.ShapeDtypeStruct((R, 128), F32),
                   jax.ShapeDtypeStruct((D, SHARD_ADA), F32), jax.ShapeDtypeStruct((1, W), F32)),
        in_specs=[vm] * 5, out_specs=(vm,) * 4,
        scratch_shapes=scratch,
        compiler_params=_params(),
    )(gw_pad, sc_in, small, dada, c_all)


def _in_proj(x, shift, scale, wt_pad, b_pad, w_out_sh):
    S = x.shape[0]
    tm = min(TM_PROJ, S)
    n_steps = S // tm
    assert n_steps >= 3

    def body(x_ref, sh_ref, sc_ref, w_ref, b_ref, wo_ref, u_ref, qkv_ref, f_ref, p_ref, g_ref, wo_all,
             wo_buf, *gather_sems):
        i = pl.program_id(0)
        xx, yy, cc = lax.axis_index("x"), lax.axis_index("y"), lax.axis_index("c")
        row_half = lambda which: (pl.ds(pl.multiple_of(which * (SHARD_OUT // 2), SHARD_OUT // 2), SHARD_OUT // 2),
                                  slice(None))
        start, forward, finish = _gather_stages((xx, yy, cc, 2 * xx + yy, (xx, yy, 1 - cc)), wo_ref, wo_buf,
                                                row_half, *gather_sems)
        pl.when(i == 0)(start)
        pl.when(i == n_steps // 2)(forward)

        @pl.when(i == n_steps - 1)
        def _():
            finish()
            wo_all[...] = wo_buf[...]

        u = (x_ref[...] * (1.0 + sc_ref[...]) + sh_ref[...]).astype(BF16)
        u_ref[...] = u
        qkv_ref[...] = (_dot_nt(u, w_ref[O_QKV:O_F, :]) + b_ref[:, O_QKV:O_F]).astype(BF16)
        f_ref[...] = _dot_nt(u, w_ref[O_F:O_P, :]) + b_ref[:, O_F:O_P]
        p_ref[...] = _dot_nt(u, w_ref[O_P:O_G, :]) + b_ref[:, O_P:O_G]
        g_ref[...] = _dot_nt(u, w_ref[O_G:D_PAD, :]) + b_ref[:, O_G:D_PAD]

    row = lambda w: pl.BlockSpec((tm, w), lambda i: (i, 0))
    full = lambda a: pl.BlockSpec(a.shape, lambda i: (0, 0))
    vm = pl.BlockSpec(memory_space=pltpu.VMEM)
    return pl.pallas_call(
        body, name="in_proj", grid=(n_steps,),
        out_shape=(jax.ShapeDtypeStruct((S, D), BF16), jax.ShapeDtypeStruct((S, 3 * D_ATT), BF16),
                   jax.ShapeDtypeStruct((S, 128), F32), jax.ShapeDtypeStruct((S, D_POOL), F32),
                   jax.ShapeDtypeStruct((S, D), F32), jax.ShapeDtypeStruct((N_CHIPS,) + w_out_sh.shape, BF16)),
        in_specs=[row(D), full(shift), full(scale), full(wt_pad), full(b_pad), vm],
        out_specs=(row(D), row(3 * D_ATT), row(128), row(D_POOL), row(D), vm),
        scratch_shapes=[pltpu.VMEM((N_CHIPS,) + w_out_sh.shape, BF16)] + _gather_scratch(),
        compiler_params=_params(dimension_semantics=("arbitrary",)),
    )(x, shift, scale, wt_pad, b_pad, w_out_sh)


def _forget_cumsum(f):
    S = f.shape[0]
    tm = min(T_ATT, S)

    def body(f_ref, out_ref, carry):
        @pl.when(pl.program_id(0) == 0)
        def _():
            carry[...] = jnp.zeros_like(carry)
        v = f_ref[...]
        logf = jnp.minimum(v, 0.0) - jnp.log(1.0 + jnp.exp(-jnp.abs(v)))
        r = lax.broadcasted_iota(jnp.int32, (tm, tm), 0)
        c = lax.broadcasted_iota(jnp.int32, (tm, tm), 1)
        tri = (r <= c).astype(F32)
        rows8 = logf.T[0:8, :]
        cum8 = jnp.dot(rows8, tri, preferred_element_type=F32, precision=lax.Precision.HIGHEST) + carry[...]
        out_ref[...] = jnp.concatenate([cum8, jnp.zeros((128 - 8, tm), F32)], axis=0).T
        last = lax.broadcasted_iota(jnp.int32, (1, tm), 1) == tm - 1
        carry[...] = jnp.sum(jnp.where(last, cum8, 0.0), axis=1, keepdims=True)

    return pl.pallas_call(
        body, name="forget_cumsum", grid=(S // tm,),
        out_shape=jax.ShapeDtypeStruct((S, 128), F32),
        in_specs=[pl.BlockSpec((tm, 128), lambda i: (i, 0))],
        out_specs=pl.BlockSpec((tm, 128), lambda i: (i, 0)),
        scratch_shapes=[pltpu.VMEM((8, 1), F32)],
        compiler_params=_params(dimension_semantics=("arbitrary",)),
    )(f)


def _split3(v):
    hi = v.astype(BF16)
    rest = v - hi.astype(F32)
    mid = rest.astype(BF16)
    lo = (rest - mid.astype(F32)).astype(BF16)
    return hi, mid, lo


def _attention_fwd(qkv, big_f):
    S = qkv.shape[0]
    T = min(T_ATT, S)
    n_t = S // T

    def body(q_ref, k_ref, v_ref, f_ref, o_ref, lse_ref, kaug_sc, vt_sc, m_sc, l_sc, acc_sc):
        hp = pl.program_id(0)
        i = pl.program_id(1)
        lane = lax.broadcasted_iota(jnp.int32, (1, 128), 1)
        sub = lax.broadcasted_iota(jnp.int32, (128, 1), 0)
        head_sel = (lane < HEAD_DIM, lane >= HEAD_DIM)
        head_sel_t = (sub < HEAD_DIM, sub >= HEAD_DIM)
        spare = (HEAD_DIM, 0)
        zero = jnp.zeros((), BF16)

        @pl.when(i == 0)
        def _():
            def prep(jt, carry):
                rows = pl.ds(pl.multiple_of(jt * T, T), T)
                k = k_ref[rows, :]
                ft = f_ref[rows, :]
                vt = v_ref[rows, :].astype(F32).T
                for h in range(2):
                    fh = jnp.sum(jnp.where(lane == 2 * hp + h, ft, 0.0), axis=1, keepdims=True)
                    hi, mid, lo = _split3(-fh)
                    b = spare[h]
                    bias = jnp.where(lane == b, hi, jnp.where(lane == b + 1, mid, jnp.where(lane == b + 2, lo, zero)))
                    kaug_sc[h, rows, :] = jnp.where(head_sel[h], k, bias)
                    vt_sc[h, jt] = jnp.where(head_sel_t[h], vt, 0.0).astype(BF16)
                return carry

            lax.fori_loop(0, n_t, prep, 0)

        q = q_ref[...]
        q_heads = []
        for h in range(2):
            ones = jnp.where((lane >= spare[h]) & (lane < spare[h] + 3), jnp.ones((), BF16), zero)
            q_heads.append(jnp.where(head_sel[h], q, ones))
        m_sc[...] = jnp.full((8, T), NEG, F32)
        l_sc[...] = jnp.zeros((8, T), F32)
        acc_sc[...] = jnp.zeros((128, T), F32)

        def update(j, k_lo, n_k, q_lo, masked):
            rows = pl.ds(pl.multiple_of(j * T + k_lo, n_k), n_k)
            n_q = T - q_lo
            alphas, pvs = [], []
            for h in range(2):
                s_t = _dot_nt(kaug_sc[h, rows, :], q_heads[h][q_lo:, :])
                if masked:
                    rr = lax.broadcasted_iota(jnp.int32, (n_k, n_q), 0) + k_lo
                    cc = lax.broadcasted_iota(jnp.int32, (n_k, n_q), 1) + q_lo
                    s_t = jnp.where(rr <= cc, s_t, NEG)
                m_prev = m_sc[h:h + 1, q_lo:]
                m_new = jnp.maximum(m_prev, jnp.max(s_t, axis=0, keepdims=True))
                alpha = jnp.exp(m_prev - m_new)
                p_t = jnp.exp(s_t - m_new)
                l_sc[h:h + 1, q_lo:] = alpha * l_sc[h:h + 1, q_lo:] + jnp.sum(p_t, axis=0, keepdims=True)
                m_sc[h:h + 1, q_lo:] = m_new
                alphas.append(alpha)
                pvs.append(_dot(vt_sc[h, j, :, k_lo:k_lo + n_k], p_t.astype(BF16)))
            acc_sc[:, q_lo:] = (acc_sc[:, q_lo:] * jnp.where(head_sel_t[0], alphas[0], alphas[1])
                                + (pvs[0] + pvs[1]))

        def two_off_diagonal(jj, carry):
            update(2 * jj, 0, T, 0, False)
            update(2 * jj + 1, 0, T, 0, False)
            return carry

        lax.fori_loop(0, i // 2, two_off_diagonal, 0)

        @pl.when(i % 2 == 1)
        def _():
            update(i - 1, 0, T, 0, False)

        update(i, 0, T, 0, True)
        l = l_sc[...]
        o_ref[...] = (acc_sc[...] / jnp.where(head_sel_t[0], l[0:1, :], l[1:2, :])).T
        is_head = lax.broadcasted_iota(jnp.int32, (8, 1), 0) < 2
        lse_ref[...] = jnp.where(is_head, m_sc[...] + jnp.log(jnp.where(is_head, l, 1.0)), 0.0)

    return pl.pallas_call(
        body, name="attention_fwd", grid=(N_PAIR, n_t),
        out_shape=(jax.ShapeDtypeStruct((S, D_ATT), F32), jax.ShapeDtypeStruct((N_PAIR, n_t, 8, T), F32)),
        in_specs=[pl.BlockSpec((T, 128), lambda hp, i: (i, hp)),
                  pl.BlockSpec((S, 128), lambda hp, i: (0, N_PAIR + hp)),
                  pl.BlockSpec((S, 128), lambda hp, i: (0, 2 * N_PAIR + hp)),
                  pl.BlockSpec((S, 128), lambda hp, i: (0, 0))],
        out_specs=(pl.BlockSpec((T, 128), lambda hp, i: (i, hp)),
                   pl.BlockSpec((None, None, 8, T), lambda hp, i: (hp, i, 0, 0))),
        scratch_shapes=[pltpu.VMEM((2, S, 128), BF16), pltpu.VMEM((2, n_t, 128, T), BF16),
                        pltpu.VMEM((8, T), F32), pltpu.VMEM((8, T), F32), pltpu.VMEM((128, T), F32)],
        compiler_params=_params(dimension_semantics=("arbitrary", "arbitrary")),
    )(qkv, qkv, qkv, big_f)


def _attention_bwd(qkv, datt, att, lse, big_f, gw_out4, sc_out):
    S = qkv.shape[0]
    T = min(T_ATT, S)
    n_t = S // T
    n_steps = N_PAIR * n_t
    marks = (0, n_steps // 8, n_steps // 2, n_steps // 2 + n_steps // 8)

    def body(q_ref, do_ref, o_ref, lse_ref, k_ref, v_ref, fk_ref, gout_ref, scout_ref,
             dq_ref, dk_ref, dv_ref, cs_ref, dfk_ref, dfq_ref, oout_ref, stat_sc, dqt_sc, qaug_sc,
             out_buf, *red_bufs):
        hp = pl.program_id(0)
        j = pl.program_id(1)
        x, y, cc = lax.axis_index("x"), lax.axis_index("y"), lax.axis_index("c")
        plan = _scatter_stages((x, y, cc, 2 * x + y, (x, y, 1 - cc)), gout_ref, scout_ref, out_buf, *red_bufs)
        step = hp * n_t + j
        for n, mark in enumerate(marks):
            @pl.when(step == mark)
            def _(n=n):
                if n > 0:
                    plan[n - 1][1]()
                if n < 3:
                    plan[n][0]()
                else:
                    oout_ref[...] = out_buf[...]

        lane = lax.broadcasted_iota(jnp.int32, (1, 128), 1)
        sub = lax.broadcasted_iota(jnp.int32, (128, 1), 0)
        head_sel = (lane < HEAD_DIM, lane >= HEAD_DIM)
        head_sel_t = (sub < HEAD_DIM, sub >= HEAD_DIM)
        spare = (HEAD_DIM, 0)
        zero = jnp.zeros((), BF16)
        one = jnp.ones((), BF16)

        def bias_lanes(first, pieces):
            hi, mid, lo = pieces
            return lambda rest: jnp.where(lane == first, hi, jnp.where(lane == first + 1, mid,
                                                                        jnp.where(lane == first + 2, lo, rest)))

        @pl.when(j == 0)
        def _():
            dqt_sc[...] = jnp.zeros_like(dqt_sc)
            cs_ref[...] = jnp.zeros_like(cs_ref)
            dfq_ref[...] = jnp.zeros_like(dfq_ref)

            def prep(i, carry):
                rows = pl.ds(pl.multiple_of(i * T, T), T)
                q = q_ref[rows, :]
                do = do_ref[rows, :]
                prod = o_ref[rows, :] * do.astype(F32)
                d_a = jnp.sum(jnp.where(head_sel[0], prod, 0.0), axis=1, keepdims=True)
                d_b = jnp.sum(jnp.where(head_sel[0], 0.0, prod), axis=1, keepdims=True)
                delta_t = jnp.where(head_sel[0], d_a, d_b).T
                stat_sc[i, 0:1, :] = delta_t[0:1, :]
                stat_sc[i, 1:2, :] = delta_t[HEAD_DIM:HEAD_DIM + 1, :]
                lse = lse_ref[i]
                lse_cols = jnp.where(head_sel_t[0], lse[0:1, :], lse[1:2, :]).T
                for h in range(2):
                    neg_lse = -lse_cols[:, h * HEAD_DIM:h * HEAD_DIM + 1]
                    ones = jnp.where((lane >= spare[h]) & (lane < spare[h] + 3), one, zero)
                    qaug_sc[h, rows, :] = jnp.where(head_sel[h], q, bias_lanes(spare[h] + 3, _split3(neg_lse))(ones))
                return carry

            lax.fori_loop(0, n_t, prep, 0)

        k = k_ref[...]
        v = v_ref[...]
        fk = fk_ref[...]
        kt = k.astype(F32).T
        heads = []
        for h in range(2):
            fkh = jnp.sum(jnp.where(lane == 2 * hp + h, fk, 0.0), axis=1, keepdims=True)
            ones = jnp.where((lane >= spare[h] + 3) & (lane < spare[h] + 6), one, zero)
            kaug = jnp.where(head_sel[h], k, bias_lanes(spare[h], _split3(-fkh))(ones))
            heads.append((kaug, jnp.where(head_sel[h], v, zero), jnp.where(head_sel_t[h], kt, 0.0).astype(BF16)))

        def block(i, k_lo, n_k, q_lo, masked):
            n_q = T - q_lo
            rows = pl.ds(pl.multiple_of(i * T + q_lo, n_q), n_q)
            q = q_ref[rows, :]
            do = do_ref[rows, :]
            stat = stat_sc[i]
            dk = jnp.zeros((n_k, 128), F32)
            dv = jnp.zeros((n_k, 128), F32)
            dqt = jnp.zeros((128, n_q), F32)
            dfs = []
            for h in range(2):
                kaug, vh, kth = heads[h]
                arg = _dot_nt(kaug[k_lo:k_lo + n_k, :], qaug_sc[h, rows, :])
                if masked:
                    rr = lax.broadcasted_iota(jnp.int32, (n_k, n_q), 0) + k_lo
                    cc = lax.broadcasted_iota(jnp.int32, (n_k, n_q), 1) + q_lo
                    arg = jnp.where(rr <= cc, arg, NEG)
                p_t = jnp.exp(arg)
                ds_t = p_t * (_dot_nt(vh[k_lo:k_lo + n_k, :], do) - stat[h:h + 1, q_lo:])
                ds_bf = ds_t.astype(BF16)
                dv = dv + _dot(p_t.astype(BF16), jnp.where(head_sel[h], do, zero))
                dk = dk + _dot(ds_bf, jnp.where(head_sel[h], q, zero))
                dqt = dqt + _dot(kth[:, k_lo:k_lo + n_k], ds_bf)
                dfs.append(jnp.sum(ds_t, axis=1, keepdims=True))
                dfq_ref[i, h:h + 1, q_lo:] += _colsum(ds_t)
            dqt_sc[i, :, q_lo:] += dqt
            return dk, dv, dfs[0], dfs[1]

        def off_diagonal(i, acc):
            return tuple(a + b for a, b in zip(acc, block(i, 0, T, 0, False)))

        half = T // 2
        early = block(j, 0, half, 0, True)
        late = block(j, half, half, half, True)
        acc1 = tuple(jnp.concatenate([a, b], axis=0) for a, b in zip(early, late))
        n_off = n_t - 1 - j
        acc2 = lax.fori_loop(0, n_off // 2,
                             lambda ii, a: off_diagonal(j + 2 + 2 * ii, off_diagonal(j + 1 + 2 * ii, a)), acc1)
        dk_acc, dv_acc, dfa, dfb = lax.fori_loop(0, n_off % 2, lambda _, a: off_diagonal(n_t - 1, a), acc2)
        dk_ref[...] = dk_acc.astype(BF16)
        dv_ref[...] = dv_acc.astype(BF16)
        dfk_ref[...] = -jnp.where(lane == 0, dfa, jnp.where(lane == 1, dfb, 0.0))
        cs_ref[:, 128:256] = cs_ref[:, 128:256] + _colsum(dk_acc)
        cs_ref[:, 256:384] = cs_ref[:, 256:384] + _colsum(dv_acc)

        @pl.when(j == n_t - 1)
        def _():
            def finish(i, tot):
                dq = dqt_sc[i].T
                dq_ref[pl.ds(pl.multiple_of(i * T, T), T), :] = dq.astype(BF16)
                return tot + _colsum(dq)

            cs_ref[:, 0:128] = lax.fori_loop(0, n_t, finish, jnp.zeros((1, 128), F32))

    pair_rows = lambda hp, j: (hp, 0, 0)
    vm = pl.BlockSpec(memory_space=pltpu.VMEM)
    _, r_out, c_out = gw_out4.shape
    return pl.pallas_call(
        body, name="attention_bwd", grid=(N_PAIR, n_t),
        out_shape=(jax.ShapeDtypeStruct((S, D_ATT), BF16), jax.ShapeDtypeStruct((S, D_ATT), BF16),
                   jax.ShapeDtypeStruct((S, D_ATT), BF16), jax.ShapeDtypeStruct((N_PAIR, 1, 384), F32),
                   jax.ShapeDtypeStruct((N_PAIR, S, 128), F32),
                   jax.ShapeDtypeStruct((N_PAIR, n_t, 8, T), F32),
                   jax.ShapeDtypeStruct((r_out, c_out), F32)),
        in_specs=[pl.BlockSpec((S, 128), lambda hp, j: (0, hp)),
                  pl.BlockSpec((S, 128), lambda hp, j: (0, hp)),
                  pl.BlockSpec((S, 128), lambda hp, j: (0, hp)),
                  pl.BlockSpec((None, n_t, 8, T), lambda hp, j: (hp, 0, 0, 0)),
                  pl.BlockSpec((T, 128), lambda hp, j: (j, N_PAIR + hp)),
                  pl.BlockSpec((T, 128), lambda hp, j: (j, 2 * N_PAIR + hp)),
                  pl.BlockSpec((T, 128), lambda hp, j: (j, 0)),
                  vm, vm],
        out_specs=(pl.BlockSpec((S, 128), lambda hp, j: (0, hp)),
                   pl.BlockSpec((T, 128), lambda hp, j: (j, hp)),
                   pl.BlockSpec((T, 128), lambda hp, j: (j, hp)),
                   pl.BlockSpec((None, 1, 384), pair_rows),
                   pl.BlockSpec((None, T, 128), lambda hp, j: (hp, j, 0)),
                   pl.BlockSpec((None, n_t, 8, T), lambda hp, j: (hp, 0, 0, 0)),
                   vm),
        scratch_shapes=[pltpu.VMEM((n_t, 8, T), F32), pltpu.VMEM((n_t, 128, T), F32),
                        pltpu.VMEM((2, S, 128), BF16), pltpu.VMEM((r_out, c_out), F32)]
        + _scatter_scratch(r_out, c_out),
        compiler_params=_params(dimension_semantics=("arbitrary", "arbitrary")),
    )(qkv, datt, att, lse, qkv, qkv, big_f, gw_out4, sc_out)


def _window_counts(first_row, n_rows, window):
    t = lax.broadcasted_iota(jnp.int32, (n_rows, 1), 0) + first_row
    return jnp.minimum((t + 1).astype(F32), float(window))


def _middle(x, tgt, att, g, p, gate, w_mix, b_mix, pool_scale, w_out, b_out, ln_g, ln_b):
    S = x.shape[0]
    tm = min(TM_MID, S)
    halo_blocks = tm // POOL_HALO

    def body(x_ref, t_ref, att_ref, g_ref, p_ref, ph_ref, gate_ref, wm_ref, bm_ref, ps_ref, wo_ref, bo_ref,
             lg_ref, lb_ref,
             dh_ref, datt_ref, dg_ref, dpl_ref, gwo_ref, gwm_ref, vec_ref, loss_ref):
        i = pl.program_id(0)

        @pl.when(i == 0)
        def _():
            gwo_ref[...] = jnp.zeros_like(gwo_ref)
            gwm_ref[...] = jnp.zeros_like(gwm_ref)
            vec_ref[...] = jnp.zeros_like(vec_ref)
            loss_ref[...] = jnp.zeros_like(loss_ref)

        pc = p_ref[...]
        halo = jnp.where(i > 0, ph_ref[...], 0.0)
        pe = jnp.concatenate([halo, pc], axis=0)
        pooled_parts = []
        for gi, w in enumerate(POOL_WINDOWS):
            cur = pe[:, gi * POOL_GROUP:(gi + 1) * POOL_GROUP]
            span = 1
            while span < w:
                cur = cur + pltpu.roll(cur, span, 0)
                span *= 2
            wsum = cur[POOL_HALO:, :]
            mean = wsum / _window_counts(i * tm, tm, w)
            pooled_parts.append(mean - pc[:, gi * POOL_GROUP:(gi + 1) * POOL_GROUP])
        pooled_bf =[v.astype(BF16) for v in pooled_parts]
        mixed = jnp.concatenate([_dot(pooled_bf[gi], wm_ref[gi]) for gi in range(4)], axis=1) + bm_ref[...]
        ps = ps_ref[...]
        pool_out = mixed * ps
        gv = g_ref[...]
        sig = _sigmoid(gv)
        silu = gv * sig
        att = att_ref[...]
        y = jnp.concatenate([att * silu[:, :D_ATT], pool_out * silu[:, D_ATT:]], axis=1)
        y_bf = y.astype(BF16)
        wo = wo_ref[...]
        yo = _dot(y_bf, wo) + bo_ref[...]
        gate = gate_ref[...]
        h = ALPHA * x_ref[...] + gate * yo
        mu = jnp.mean(h, axis=1, keepdims=True)
        hc = h - mu
        var = jnp.mean(hc * hc, axis=1, keepdims=True)
        rstd = lax.rsqrt(var + LN_EPS)
        yhat = hc * rstd
        lg = lg_ref[...]
        out = yhat * lg + lb_ref[...]
        err = out - t_ref[...]
        loss_ref[...] += 0.5 * jnp.sum(jnp.mean(err * err, axis=1, keepdims=True), axis=0, keepdims=True)

        dout = err * (1.0 / D)
        g_ln_b = _colsum(dout)
        g_ln_g = _colsum(dout * yhat)
        dyh = dout * lg
        dh = rstd * (dyh - jnp.mean(dyh, axis=1, keepdims=True)
                     - yhat * jnp.mean(dyh * yhat, axis=1, keepdims=True))
        dh_ref[...] = dh
        d_gate = _colsum(dh * yo)
        dyo = gate * dh
        g_b_out = _colsum(dyo)
        dyo_bf = dyo.astype(BF16)
        gwo_ref[...] += _dot_tn(y_bf, dyo_bf)
        dy = _dot_nt(dyo_bf, wo)
        dsilu = sig * (1.0 + gv * (1.0 - sig))
        dy_a = dy[:, :D_ATT]
        dy_p = dy[:, D_ATT:]
        datt_ref[...] = (dy_a * silu[:, :D_ATT]).astype(BF16)
        dpo = dy_p * silu[:, D_ATT:]
        dg = jnp.concatenate([dy_a * att * dsilu[:, :D_ATT], dy_p * pool_out * dsilu[:, D_ATT:]], axis=1)
        dg_ref[...] = dg.astype(BF16)
        g_dg = _colsum(dg)
        g_ps = _colsum(dpo * mixed)
        dmixed = dpo * ps
        g_bm = _colsum(dmixed)
        dmixed_bf = dmixed.astype(BF16)
        dpl = []
        for gi in range(4):
            dm = dmixed_bf[:, gi * POOL_GROUP:(gi + 1) * POOL_GROUP]
            gwm_ref[gi] += _dot_tn(pooled_bf[gi], dm)
            dpl.append(_dot_nt(dm, wm_ref[gi]))
        dpl_ref[...] = jnp.concatenate(dpl, axis=1)
        vec_ref[0:1, :] += g_ln_g
        vec_ref[1:2, :] += g_ln_b
        vec_ref[2:3, :] += d_gate
        vec_ref[3:4, :] += g_b_out
        vec_ref[4:5, :] += g_dg
        vec_ref[5:6, 0:D_POOL] += g_ps
        vec_ref[6:7, 0:D_POOL] += g_bm

    row = lambda w: pl.BlockSpec((tm, w), lambda i: (i, 0))
    full2 = lambda a: pl.BlockSpec(a.shape, lambda i: (0, 0))
    full3 = lambda a: pl.BlockSpec(a.shape, lambda i: (0, 0, 0))
    return pl.pallas_call(
        body, name="middle", grid=(S // tm,),
        out_shape=(jax.ShapeDtypeStruct((S, D), F32),
                   jax.ShapeDtypeStruct((S, D_ATT), BF16),
                   jax.ShapeDtypeStruct((S, D), BF16),
                   jax.ShapeDtypeStruct((S, D_POOL), F32),
                   jax.ShapeDtypeStruct((D, D), F32),
                   jax.ShapeDtypeStruct((4, POOL_GROUP, POOL_GROUP), F32),
                   jax.ShapeDtypeStruct((8, D), F32),
                   jax.ShapeDtypeStruct((1, 1), F32)),
        in_specs=[row(D), row(D), row(D_ATT), row(D), row(D_POOL),
                  pl.BlockSpec((POOL_HALO, D_POOL), lambda i: (jnp.maximum(i * halo_blocks - 1, 0), 0)),
                  full2(gate), full3(w_mix), full2(b_mix), full2(pool_scale), full2(w_out), full2(b_out),
                  full2(ln_g), full2(ln_b)],
        out_specs=(row(D), row(D_ATT), row(D), row(D_POOL),
                   pl.BlockSpec((D, D), lambda i: (0, 0)),
                   pl.BlockSpec((4, POOL_GROUP, POOL_GROUP), lambda i: (0, 0, 0)),
                   pl.BlockSpec((8, D), lambda i: (0, 0)),
                   pl.BlockSpec((1, 1), lambda i: (0, 0))),
        compiler_params=_params(dimension_semantics=("arbitrary",)),
    )(x, tgt, att, g, p, p, gate, w_mix, b_mix, pool_scale, w_out, b_out, ln_g, ln_b)


def _tail(dpl, dfk, dfq, f):
    S = dpl.shape[0]
    tm = min(T_ATT, S)
    n_t = S // tm
    halo_blocks = tm // POOL_HALO
    last_halo = S // POOL_HALO - 1

    def body(d_ref, dn_ref, dfk_ref, dfq_ref, f_ref, dp_ref, df_ref, cs_ref, carry):
        s = pl.program_id(0)
        i = n_t - 1 - s

        @pl.when(s == 0)
        def _():
            carry[...] = jnp.zeros_like(carry)
            cs_ref[...] = jnp.zeros_like(cs_ref)

        dc = d_ref[...]
        nxt = jnp.where(s > 0, dn_ref[...], 0.0)
        de = jnp.concatenate([dc, nxt], axis=0)
        n_e = tm + POOL_HALO
        parts = []
        for gi, w in enumerate(POOL_WINDOWS):
            cur = de[:, gi * POOL_GROUP:(gi + 1) * POOL_GROUP] / _window_counts(i * tm, n_e, w)
            span = 1
            while span < w:
                cur = cur + pltpu.roll(cur, n_e - span, 0)
                span *= 2
            parts.append(cur[:tm, :] - dc[:, gi * POOL_GROUP:(gi + 1) * POOL_GROUP])
        dp = jnp.concatenate(parts, axis=1)
        dp_ref[...] = dp.astype(BF16)
        cs_ref[0:1, :] += _colsum(dp)

        r = lax.broadcasted_iota(jnp.int32, (tm, tm), 0)
        c = lax.broadcasted_iota(jnp.int32, (tm, tm), 1)
        tri = (r >= c).astype(F32)
        k_cols = dfk_ref[0]
        rows8 = dfq_ref[0]
        for hp in range(1, N_PAIR):
            k_cols = k_cols + pltpu.roll(dfk_ref[hp], 2 * hp, 1)
            rows8 = rows8 + pltpu.roll(dfq_ref[hp], 2 * hp, 0)
        rows8 = rows8 + k_cols.T[0:8, :]
        dlogf8 = jnp.dot(rows8, tri, preferred_element_type=F32, precision=lax.Precision.HIGHEST) + carry[...]
        first = lax.broadcasted_iota(jnp.int32, (1, tm), 1) == 0
        carry[...] = jnp.sum(jnp.where(first, dlogf8, 0.0), axis=1, keepdims=True)
        dlogf = jnp.concatenate([dlogf8, jnp.zeros((128 - 8, tm), F32)], axis=0).T
        df = dlogf * _sigmoid(-f_ref[...])
        df_ref[...] = df.astype(BF16)
        cs_ref[1:2, 0:128] += _colsum(df)

    rev = lambda w: pl.BlockSpec((tm, w), lambda s: (n_t - 1 - s, 0))
    return pl.pallas_call(
        body, name="tail", grid=(n_t,),
        out_shape=(jax.ShapeDtypeStruct((S, D_POOL), BF16), jax.ShapeDtypeStruct((S, 128), BF16),
                   jax.ShapeDtypeStruct((8, D_POOL), F32)),
        in_specs=[rev(D_POOL),
                  pl.BlockSpec((POOL_HALO, D_POOL),
                               lambda s: (jnp.minimum((n_t - s) * halo_blocks, last_halo), 0)),
                  pl.BlockSpec((N_PAIR, tm, 128), lambda s: (0, n_t - 1 - s, 0)),
                  pl.BlockSpec((N_PAIR, None, 8, tm), lambda s: (0, n_t - 1 - s, 0, 0)),
                  rev(128)],
        out_specs=(rev(D_POOL), rev(128), pl.BlockSpec((8, D_POOL), lambda s: (0, 0))),
        scratch_shapes=[pltpu.VMEM((8, 1), F32)],
        compiler_params=_params(dimension_semantics=("arbitrary",)),
    )(dpl, dpl, dfk, dfq, f)


PIECES = ((O_QKV, D_ATT), (O_QKV + D_ATT, D_ATT), (O_QKV + 2 * D_ATT, D_ATT), (O_F, 128), (O_P, D_POOL), (O_G, D))


def _grad_w_in(u, pieces):
    S = u.shape[0]
    tm = min(TM_GW, S)
    n_t = S // tm

    def body(u_ref, *rest):
        piece_refs, out_ref, acc, sem = rest[:6], rest[6], rest[7], rest[8]
        i = pl.program_id(0)

        @pl.when(i == 0)
        def _():
            acc[...] = jnp.zeros_like(acc)

        u_t = u_ref[...]
        for (off, w), ref in zip(PIECES, piece_refs):
            acc[:, off:off + w] += _dot_tn(u_t, ref[...])

        @pl.when(i == n_t - 1)
        def _():
            cp = pltpu.make_async_copy(acc, out_ref, sem)
            cp.start()
            cp.wait()

    return pl.pallas_call(
        body, name="grad_w_in", grid=(n_t,),
        out_shape=jax.ShapeDtypeStruct((D, D_PAD), F32),
        in_specs=[pl.BlockSpec((tm, D), lambda i: (i, 0))]
        + [pl.BlockSpec((tm, w), lambda i: (i, 0)) for _, w in PIECES],
        out_specs=pl.BlockSpec(memory_space=pl.ANY),
        scratch_shapes=[pltpu.VMEM((D, D_PAD), F32), pltpu.SemaphoreType.DMA],
        compiler_params=_params(dimension_semantics=("arbitrary",)),
    )(u, *pieces)


def _grad_x(pieces, wt_pad, dh, x, scale):
    S = x.shape[0]
    tm = min(TM_DU, S)

    def body(*refs):
        piece_refs = refs[:6]
        w_ref, dh_ref, x_ref, sc_ref, gx_ref, vec_ref = refs[6:]

        @pl.when(pl.program_id(0) == 0)
        def _():
            vec_ref[...] = jnp.zeros_like(vec_ref)

        du = jnp.zeros((tm, D), F32)
        for (off, w), ref in zip(PIECES, piece_refs):
            du = du + _dot(ref[...], w_ref[off:off + w, :])
        xv = x_ref[...]
        gx_ref[...] = ALPHA * dh_ref[...] + du * (1.0 + sc_ref[...])
        vec_ref[0:1, :] += _colsum(du)
        vec_ref[1:2, :] += _colsum(du * xv)

    row = lambda w: pl.BlockSpec((tm, w), lambda i: (i, 0))
    return pl.pallas_call(
        body, name="grad_x", grid=(S // tm,),
        out_shape=(jax.ShapeDtypeStruct((S, D), F32), jax.ShapeDtypeStruct((8, D), F32)),
        in_specs=[row(w) for _, w in PIECES]
        + [pl.BlockSpec(wt_pad.shape, lambda i: (0, 0)), row(D), row(D), pl.BlockSpec((1, D), lambda i: (0, 0))],
        out_specs=(row(D), pl.BlockSpec((8, D), lambda i: (0, 0))),
        compiler_params=_params(dimension_semantics=("arbitrary",)),
    )(*pieces, wt_pad, dh, x, scale)


def _adamw_math(w, g, m, v):
    m = ADAM_B1 * m + (1.0 - ADAM_B1) * g
    v = ADAM_B2 * v + (1.0 - ADAM_B2) * (g * g)
    m_hat = m / (1.0 - ADAM_B1 ** ADAM_STEP)
    v_hat = v / (1.0 - ADAM_B2 ** ADAM_STEP)
    delta = -ADAM_LR * (m_hat / (jnp.sqrt(v_hat) + ADAM_EPS) + ADAM_WD * w)
    return delta, m, v


def _adamw(groups, n_steps):
    n = len(groups)

    def body(*refs):
        ins, outs = refs[:4 * n], refs[4 * n:]
        for t in range(n):
            w, g, m, v = (r[...] for r in ins[4 * t:4 * t + 4])
            d, m2, v2 = _adamw_math(w, g, m, v)
            outs[4 * t][...] = d
            outs[4 * t + 1][...] = m2
            outs[4 * t + 2][...] = v2
            outs[4 * t + 3][...] = g

    in_specs, out_specs, out_shape, args = [], [], [], []
    for (w, g, m, v) in groups:
        rest = w.shape[1:]
        spec = pl.BlockSpec((w.shape[0] // n_steps,) + rest, lambda i, nd=len(rest): (i,) + (0,) * nd)
        in_specs += [spec] * 4
        out_specs += [spec] * 4
        out_shape += [jax.ShapeDtypeStruct(w.shape, F32)] * 4
        args += [w, g, m, v]
    return pl.pallas_call(
        body, name="adamw_%d_%d" % (n, n_steps), grid=(n_steps,),
        out_shape=tuple(out_shape), in_specs=in_specs, out_specs=tuple(out_specs),
        compiler_params=_params(dimension_semantics=("arbitrary",)),
    )(*args)


def _adamw_small(small_sum, g_b_ada, params):
    n = len(params)

    def body(gs_ref, gba_ref, *refs):
        ins, outs = refs[:3 * n], refs[3 * n:]
        for t, (name, w0, _, _) in enumerate(params):
            w_ref, m_ref, v_ref = ins[3 * t:3 * t + 3]
            first = SMALL_SEGS[name][0] if name in SMALL_SEGS else None
            if w0.shape[0] > 1:
                pieces = [((slice(None), slice(None)), gs_ref[first:first + w0.shape[0], :])]
            else:
                pieces = []
                for r in range(-(-w0.shape[1] // 128)):
                    lanes = slice(128 * r, min(128 * r + 128, w0.shape[1]))
                    g = gba_ref[0:1, lanes] if first is None else gs_ref[first + r:first + r + 1, 0:lanes.stop - lanes.start]
                    pieces.append(((slice(0, 1), lanes), g))
            for where, g in pieces:
                d, m2, v2 = _adamw_math(w_ref[where], g, m_ref[where], v_ref[where])
                for ref, val in zip(outs[4 * t:4 * t + 4], (g, d, m2, v2)):
                    ref[where] = val

    vm = pl.BlockSpec(memory_space=pltpu.VMEM)
    args = [small_sum, g_b_ada]
    out_shape = []
    for _, w, m, v in params:
        args += [w, m, v]
        out_shape += [jax.ShapeDtypeStruct(w.shape, F32)] * 4
    return pl.pallas_call(
        body, name="adamw_small",
        out_shape=tuple(out_shape), in_specs=[vm] * len(args), out_specs=(vm,) * len(out_shape),
        compiler_params=_params(),
    )(*args)


def _pack_small(parts):
    rows = []
    used = 0
    for name, (first, n_rows) in SMALL_SEGS.items():
        if first > used:
            rows.append(jnp.zeros((first - used, 128), F32))
        flat = parts[name].reshape(-1)
        flat = jnp.pad(flat, (0, n_rows * 128 - flat.shape[0]))
        rows.append(flat.reshape(n_rows, 128))
        used = first + n_rows
    rows.append(jnp.zeros((SMALL_ROWS - used, 128), F32))
    return jnp.concatenate(rows, axis=0)


def _unpack_small(buf, name, shape):
    first, n_rows = SMALL_SEGS[name]
    n = int(np.prod(shape))
    return buf[first:first + n_rows].reshape(-1)[:n].reshape(shape)


def _pad_in(v):
    r = v.shape[0]
    z = jnp.zeros((r, O_P - O_F - N_HEADS), v.dtype)
    return jnp.concatenate([v[:, :3 * D_ATT + N_HEADS], z, v[:, 3 * D_ATT + N_HEADS:]], axis=1)


def _unpad_in(v):
    return jnp.concatenate([v[:, :O_F + N_HEADS], v[:, O_P:]], axis=1)


def _shards_in(v):
    gap = O_P - (O_F + N_HEADS)
    parts = []
    for a in range(N_CHIPS):
        lo, hi = a * SHARD_IN, (a + 1) * SHARD_IN
        cut = O_F + N_HEADS
        if hi <= cut:
            parts.append(v[:, lo:hi])
        elif lo >= cut:
            parts.append(v[:, lo + gap:hi + gap])
        else:
            parts.append(jnp.concatenate([v[:, lo:cut], v[:, cut + gap:hi + gap]], axis=1))
    return jnp.stack(parts, axis=0)


def kernel(x, c, w_ada, b_ada, w_in, b_in, w_pool_mix, b_pool_mix, pool_scale, w_out, b_out, ln_g, ln_b, loss_target, m_w_ada, m_b_ada, m_w_in, m_b_in, m_w_pool_mix, m_b_pool_mix, m_pool_scale, m_w_out, m_b_out, m_ln_g, m_ln_b, v_w_ada, v_b_ada, v_w_in, v_b_in, v_w_pool_mix, v_b_pool_mix, v_pool_scale, v_w_out, v_b_out, v_ln_g, v_ln_b):
    S = x.shape[1]
    T = min(T_ATT, S)
    n_t = S // T
    x2 = x[0]
    tgt = loss_target[0]
    q_scale = jnp.concatenate([jnp.full((1, D_ATT), Q_SCALE, F32), jnp.ones((1, D_PAD - D_ATT), F32)], axis=1)

    to_cols = lambda a: jnp.transpose(a, (2, 0, 1))
    from_cols = lambda a: jnp.transpose(a, (1, 2, 0))
    c_all, ada4, wt_pad = _gather_and_ada(
        c, w_ada[0], b_ada.reshape(4, 1, SHARD_ADA), to_cols(w_in))
    ada = ada4[:, 0, :].reshape(1, 3 * D)
    shift, scale, gate = ada[:, :D], ada[:, D:2 * D], ada[:, 2 * D:]
    b_pad = _pad_in(b_in) * q_scale
    w_mix_bf = w_pool_mix[0].astype(BF16)

    u, qkv, f, p, g, w_out_all = _in_proj(x2, shift, scale, wt_pad, b_pad, w_out[0].astype(BF16))
    w_out_full = w_out_all.reshape(D, D)
    big_f = _forget_cumsum(f)
    att, lse = _attention_fwd(qkv, big_f)

    dh, datt, dg, dpl, gw_out, gw_mix, vec, loss_part = _middle(
        x2, tgt, att, g, p, gate, w_mix_bf, b_pool_mix.reshape(1, D_POOL), pool_scale, w_out_full, b_out, ln_g, ln_b)
    dq, dk, dv, cs_att, dfk, dfq, g_w_out = _attention_bwd(
        qkv, datt, att, lse, big_f, gw_out.reshape(N_CHIPS, SHARD_OUT, D), jnp.ones((N_CHIPS, 1, D), F32))
    dp, df, cs_tail = _tail(dpl, dfk, dfq, f)
    pieces = (dq, dk, dv, df, dp, dg)
    gw_pad = _grad_w_in(u, pieces)
    grad_x, vec_x = _grad_x(pieces, wt_pad, dh, x2, scale)

    cs_qkv = jnp.transpose(cs_att.reshape(N_PAIR, 3, 128), (1, 0, 2)).reshape(1, 3 * D_ATT)
    gb_pad = jnp.concatenate([cs_qkv, cs_tail[1:2, 0:128], cs_tail[0:1, :], vec[4:5, :]], axis=1) * q_scale
    dada = jnp.concatenate([vec_x[0:1, :], vec_x[1:2, :], vec[2:3, :]], axis=1)
    small = _pack_small({
        "b_in": _unpad_in(gb_pad), "w_pool_mix": gw_mix, "b_pool_mix": vec[6:7, :D_POOL],
        "pool_scale": vec[5:6, :D_POOL], "b_out": vec[3:4, :], "ln_g": vec[0:1, :], "ln_b": vec[1:2, :],
        "loss": loss_part})

    g_w_in, small_sum, g_w_ada, g_b_ada = _reduce_all(
        gw_pad, _shards_in(q_scale), small, dada, c_all)
    loss = _unpack_small(small_sum, "loss", (1,))[0]

    big = _adamw([(w_ada[0], g_w_ada, m_w_ada[0], v_w_ada[0]),
                  (w_out[0], g_w_out, m_w_out[0], v_w_out[0])], 4)
    big_in = _adamw([(to_cols(w_in), g_w_in, to_cols(m_w_in), to_cols(v_w_in))], 5)
    tiles = lambda a: a.reshape(4 * POOL_GROUP, POOL_GROUP)
    flat = lambda a: a.reshape(1, D_POOL)
    small_params = [("b_ada", b_ada, m_b_ada, v_b_ada), ("b_in", b_in, m_b_in, v_b_in),
                    ("w_pool_mix", tiles(w_pool_mix), tiles(m_w_pool_mix), tiles(v_w_pool_mix)),
                    ("b_pool_mix", flat(b_pool_mix), flat(m_b_pool_mix), flat(v_b_pool_mix)),
                    ("pool_scale", pool_scale, m_pool_scale, v_pool_scale), ("b_out", b_out, m_b_out, v_b_out),
                    ("ln_g", ln_g, m_ln_g, v_ln_g), ("ln_b", ln_b, m_ln_b, v_ln_b)]
    sm = _adamw_small(small_sum, g_b_ada, small_params)
    sm_idx = {p[0]: n for n, p in enumerate(small_params)}
    shapes = {"w_pool_mix": (1, 4, POOL_GROUP, POOL_GROUP), "b_pool_mix": (1, 4, POOL_GROUP)}

    names = ["w_ada", "b_ada", "w_in", "b_in", "w_pool_mix", "b_pool_mix", "pool_scale", "w_out", "b_out",
             "ln_g", "ln_b"]
    big_idx = {"w_ada": 0, "w_out": 1}

    def leaf(kind, name):
        if name == "w_in":
            return from_cols(big_in[(kind - 1) % 4])
        if name in big_idx:
            return big[4 * big_idx[name] + (kind - 1) % 4][None]
        val = sm[4 * sm_idx[name] + kind]
        return val.reshape(shapes[name]) if name in shapes else val

    outs = [loss, grad_x[None]]
    for kind in range(4):
        outs += [leaf(kind, n) for n in names]
    return tuple(outs)
```

```python
import functools

import numpy as np
import jax
import jax.numpy as jnp
from jax import lax
from jax.experimental import pallas as pl
from jax.experimental.pallas import tpu as pltpu

F32 = jnp.float32
BF16 = jnp.bfloat16
MESH = pl.DeviceIdType.MESH

D = 1024
D_ATT = 512
D_POOL = 512
N_HEADS = 8
HEAD_DIM = 64
N_PAIR = N_HEADS // 2
POOL_WINDOWS = (2, 4, 8, 16)
POOL_GROUP = 128
POOL_HALO = 16
LN_EPS = 1e-5
ALPHA = 2.0 ** 0.25
D_IN = 3 * D_ATT + N_HEADS + D_POOL + D_ATT + D_POOL
N_CHIPS = 4
SHARD_IN = D_IN // N_CHIPS
SHARD_ADA = 3 * D // N_CHIPS
SHARD_OUT = D // N_CHIPS

O_QKV, O_F, O_P, O_G, D_PAD = 0, 1536, 1664, 2176, 3200
Q_SCALE = HEAD_DIM ** -0.5

ADAM_LR, ADAM_B1, ADAM_B2, ADAM_EPS, ADAM_WD, ADAM_STEP = 0.001, 0.9, 0.999, 1e-08, 0.01, 10

NEG = -1e30

VMEM_LIMIT = 56 * 1024 * 1024

TM_PROJ = 512
T_ATT = 512
TM_MID = 256
TM_GW = 1024
TM_DU = 512
N_CHUNK = 4

REL7 = [(0, 0, 1), (0, 1, 0), (0, 1, 1), (1, 0, 0), (1, 0, 1), (1, 1, 0), (1, 1, 1)]
REL3 = [(0, 1), (1, 0), (1, 1)]

SMALL_SEGS = {}
_row = 0
for _name, _n in (("b_in", D_IN), ("w_pool_mix", 65536), ("b_pool_mix", 512), ("pool_scale", 512),
                  ("b_out", 1024), ("ln_g", 1024), ("ln_b", 1024), ("loss", 1)):
    _rows = -(-_n // 1024) * 8
    SMALL_SEGS[_name] = (_row, _rows)
    _row += _rows
SMALL_ROWS = -(-_row // 16) * 16


def _params(**kw):
    return pltpu.CompilerParams(vmem_limit_bytes=VMEM_LIMIT, **kw)


def _flip(v, d):
    return v if d == 0 else 1 - v


def _dot(a, b):
    return jnp.dot(a, b, preferred_element_type=F32)


def _dot_nt(a, b):
    return lax.dot_general(a, b, (((1,), (1,)), ((), ())), preferred_element_type=F32)


def _dot_tn(a, b):
    return lax.dot_general(a, b, (((0,), (0,)), ((), ())), preferred_element_type=F32)


def _sigmoid(v):
    return 1.0 / (1.0 + jnp.exp(-v))


def _colsum(v):
    return jnp.sum(v, axis=0, keepdims=True)


def _gather_stages(pos, src_ref, dst_ref, half, own_sem, s_sem, r_sem, fs_sem, fr_sem):
    x, y, cc, chip, sib = pos
    own = pltpu.make_async_copy(src_ref, dst_ref.at[chip], own_sem)
    first, landed, others = [], [], []
    for k, (dx, dy) in enumerate(REL3):
        px, py = _flip(x, dx), _flip(y, dy)
        first.append(pltpu.make_async_remote_copy(
            src_ref=src_ref.at[half(cc)], dst_ref=dst_ref.at[(chip,) + half(cc)],
            send_sem=s_sem.at[k], recv_sem=r_sem.at[k], device_id=(px, py, cc), device_id_type=MESH))
        landed.append(dst_ref.at[(2 * px + py,) + half(cc)])
        others.append(dst_ref.at[(2 * px + py,) + half(1 - cc)])
    passed = [pltpu.make_async_remote_copy(src_ref=landed[k], dst_ref=landed[k], send_sem=fs_sem.at[k],
                                           recv_sem=fr_sem.at[k], device_id=sib, device_id_type=MESH)
              for k in range(3)]

    def start(finish_src=None):
        for cp in first:
            cp.start()
        if finish_src is not None:
            finish_src()
        own.start()

    def forward():
        for k in range(3):
            pltpu.make_async_remote_copy(src_ref=landed[k], dst_ref=landed[k], send_sem=s_sem.at[k],
                                         recv_sem=r_sem.at[k], device_id=sib, device_id_type=MESH).wait_recv()
            passed[k].start()

    def finish():
        for k in range(3):
            pltpu.make_async_remote_copy(src_ref=others[k], dst_ref=others[k], send_sem=fs_sem.at[k],
                                         recv_sem=fr_sem.at[k], device_id=sib, device_id_type=MESH).wait_recv()
        for cp in first + passed:
            cp.wait_send()
        own.wait()

    return start, forward, finish


def _gather_scratch():
    return [pltpu.SemaphoreType.DMA, pltpu.SemaphoreType.DMA((3,)), pltpu.SemaphoreType.DMA((3,)),
            pltpu.SemaphoreType.DMA((3,)), pltpu.SemaphoreType.DMA((3,))]


def _gather_and_ada(c, w_ada, b_ada4, w_in_sh):
    def body(c_ref, w_ref, b_ref, win_ref, call_ref, ada_ref, wt_pad_ref,
             win_all, win_bf, cslab, sbuf, rbuf, cs_sem, cr_sem, as_sem, ar_sem, *gather_sems):
        x, y, cc = lax.axis_index("x"), lax.axis_index("y"), lax.axis_index("c")
        me = 4 * x + 2 * y + cc
        chip = 2 * x + y
        lane_half = lambda which: (slice(None), pl.ds(pl.multiple_of(which * (D // 2), D // 2), D // 2))
        def round_half(which):
            for h in range(2):
                @pl.when(which == h)
                def _():
                    lanes = slice(h * (D // 2), (h + 1) * (D // 2))
                    win_bf[:, lanes] = win_ref[:, 0, lanes].astype(BF16)

        start, forward, finish = _gather_stages((x, y, cc, chip, (x, y, 1 - cc)), win_bf, win_all, lane_half,
                                                *gather_sems)
        round_half(cc)
        start(lambda: round_half(1 - cc))

        cslab[...] = jnp.broadcast_to(c_ref[...], (8, D))
        call_ref[me] = cslab[...]
        gathers = []
        for k, (dx, dy, dc) in enumerate(REL7):
            cp = pltpu.make_async_remote_copy(
                src_ref=cslab, dst_ref=call_ref.at[me], send_sem=cs_sem.at[k], recv_sem=cr_sem.at[k],
                device_id=(_flip(x, dx), _flip(y, dy), _flip(cc, dc)), device_id_type=MESH)
            cp.start()
            gathers.append(cp)
        for cp in gathers:
            cp.wait()
        slab_row = lax.broadcasted_iota(jnp.int32, (8, 1), 0)
        mat = jnp.zeros((8, D), F32)
        for r in range(8):
            mat = jnp.where(slab_row == r, call_ref[r], mat)
        act = (mat * _sigmoid(mat)).astype(BF16)
        part = _dot(act, w_ref[...].astype(BF16))
        sends = []
        for k, (dx, dy) in enumerate(REL3):
            px, py = _flip(x, dx), _flip(y, dy)
            r = 4 * px + 2 * py + cc
            piece = _colsum(jnp.where(slab_row == r, part, 0.0))
            sbuf[k] = jnp.broadcast_to(piece, (8, SHARD_ADA))
            cp = pltpu.make_async_remote_copy(
                src_ref=sbuf.at[k], dst_ref=rbuf.at[k], send_sem=as_sem.at[k], recv_sem=ar_sem.at[k],
                device_id=(px, py, cc), device_id_type=MESH)
            cp.start()
            sends.append(cp)
        own_piece = _colsum(jnp.where(slab_row == me, part, 0.0))
        ada_ref[chip] = jnp.broadcast_to(own_piece, (8, SHARD_ADA)) + b_ref[chip]
        for k, (dx, dy) in enumerate(REL3):
            sends[k].wait()
            a = 2 * _flip(x, dx) + _flip(y, dy)
            ada_ref[a] = rbuf[k] + b_ref[a]

        forward()
        finish()
        n_real = 3 * D_ATT + N_HEADS
        for a in range(N_CHIPS):
            lo, hi = a * SHARD_IN, (a + 1) * SHARD_IN
            for s0, s1 in ((lo, min(hi, D_ATT)), (max(lo, D_ATT), min(hi, n_real)), (max(lo, n_real), hi)):
                if s0 < s1:
                    rows = win_all[a, s0 - lo:s1 - lo, :]
                    if s1 <= D_ATT:
                        rows = rows * jnp.asarray(Q_SCALE, BF16)
                    shift = O_P - n_real if s0 >= n_real else 0
                    wt_pad_ref[s0 + shift:s1 + shift, :] = rows
        wt_pad_ref[n_real:O_P, :] = jnp.zeros((O_P - n_real, D), BF16)

    vm = pl.BlockSpec(memory_space=pltpu.VMEM)
    return pl.pallas_call(
        body, name="gather_and_ada",
        out_shape=(jax.ShapeDtypeStruct((8, 8, D), F32), jax.ShapeDtypeStruct((4, 8, SHARD_ADA), F32),
                   jax.ShapeDtypeStruct((D_PAD, D), BF16)),
        in_specs=[vm] * 4, out_specs=(vm,) * 3,
        scratch_shapes=[pltpu.VMEM((N_CHIPS, SHARD_IN, D), BF16), pltpu.VMEM((SHARD_IN, D), BF16),
                        pltpu.VMEM((8, D), F32), pltpu.VMEM((3, 8, SHARD_ADA), F32),
                        pltpu.VMEM((3, 8, SHARD_ADA), F32),
                        pltpu.SemaphoreType.DMA((7,)), pltpu.SemaphoreType.DMA((7,)),
                        pltpu.SemaphoreType.DMA((3,)), pltpu.SemaphoreType.DMA((3,))] + _gather_scratch(),
        compiler_params=_params(),
    )(c, w_ada, b_ada4, w_in_sh)


def _shard_cols():
    cut, gap = O_F + N_HEADS, O_P - (O_F + N_HEADS)
    out = []
    for a in range(N_CHIPS):
        lo, hi = a * SHARD_IN, (a + 1) * SHARD_IN
        out.append(([(lo, min(hi, cut))] if lo < cut else []) + ([(max(lo, cut) + gap, hi + gap)] if hi > cut else []))
    return out


def _scatter_stages(pos, g_ref, sc_ref, out_ref, sib_buf, send_buf, ici_buf, sem1, sem2s, sem2r, sem3, part=(0, 1),
                    cols=None, own_buf=None):
    x, y, cc, chip, sib = pos
    q, n_parts = part
    RH = (g_ref.shape[1] if cols is None else g_ref.shape[0]) // 2 // n_parts
    mine = pl.ds(pl.multiple_of((cc * n_parts + q) * RH, RH), RH)
    theirs = pl.ds(pl.multiple_of(((1 - cc) * n_parts + q) * RH, RH), RH)
    cp1 = pltpu.make_async_remote_copy(
        src_ref=g_ref.at[:, theirs, :] if cols is None else g_ref.at[theirs, :], dst_ref=sib_buf,
        send_sem=sem1.at[0], recv_sem=sem1.at[1], device_id=sib, device_id_type=MESH)
    sends = []
    for k, (dx, dy) in enumerate(REL3):
        px, py = _flip(x, dx), _flip(y, dy)
        sends.append(pltpu.make_async_remote_copy(
            src_ref=send_buf.at[2 * px + py], dst_ref=ici_buf.at[chip],
            send_sem=sem2s.at[k], recv_sem=sem2r.at[k], device_id=(px, py, cc), device_id_type=MESH))
    cp3 = pltpu.make_async_remote_copy(
        src_ref=out_ref.at[mine, :], dst_ref=out_ref.at[mine, :], send_sem=sem3.at[0], recv_sem=sem3.at[1],
        device_id=sib, device_id_type=MESH)

    def finish1():
        cp1.wait()
        if cols is None:
            for a in range(N_CHIPS):
                both = g_ref[a, mine, :] + sib_buf[a]
                sib_buf[a] = both
                send_buf[a] = both.astype(BF16)
        else:
            both = g_ref[mine, :] + sib_buf[...]
            for a, pieces in enumerate(cols):
                at = 0
                for lo, hi in pieces:
                    own_buf[a, :, at:at + hi - lo] = both[:, lo:hi]
                    send_buf[a, :, at:at + hi - lo] = both[:, lo:hi].astype(BF16)
                    at += hi - lo

    def start2():
        for cp in sends:
            cp.start()
        ici_buf[chip] = send_buf[chip]

    def finish2():
        for cp in sends:
            cp.wait()
        own = (sib_buf if cols is None else own_buf)[chip]
        parts = [jnp.where(chip == a, own, ici_buf[a].astype(F32)) for a in range(N_CHIPS)]
        out_ref[mine, 0:own.shape[1]] = ((parts[0] + parts[1]) + (parts[2] + parts[3])) * sc_ref[chip]

    return [(cp1.start, finish1), (start2, finish2), (cp3.start, cp3.wait)]


def _all_reduce_stages(pos, g_ref, out_ref, sib_buf, ici_buf, sem1, sem2s, sem2r, sem3):
    x, y, cc, chip, sib = pos
    RH = g_ref.shape[0] // 2
    mine = pl.ds(pl.multiple_of(cc * RH, 8), RH)
    theirs = pl.ds(pl.multiple_of((1 - cc) * RH, 8), RH)
    cp1 = pltpu.make_async_remote_copy(
        src_ref=g_ref.at[theirs, :], dst_ref=sib_buf, send_sem=sem1.at[0], recv_sem=sem1.at[1],
        device_id=sib, device_id_type=MESH)
    sends = []
    for k, (dx, dy) in enumerate(REL3):
        px, py = _flip(x, dx), _flip(y, dy)
        sends.append(pltpu.make_async_remote_copy(
            src_ref=sib_buf, dst_ref=ici_buf.at[chip],
            send_sem=sem2s.at[k], recv_sem=sem2r.at[k], device_id=(px, py, cc), device_id_type=MESH))
    cp3 = pltpu.make_async_remote_copy(
        src_ref=out_ref.at[mine, :], dst_ref=out_ref.at[mine, :], send_sem=sem3.at[0], recv_sem=sem3.at[1],
        device_id=sib, device_id_type=MESH)

    def finish1():
        cp1.wait()
        sib_buf[...] = g_ref[mine, :] + sib_buf[...]

    def start2():
        for cp in sends:
            cp.start()
        ici_buf[chip] = sib_buf[...]

    def finish2():
        for cp in sends:
            cp.wait()
        out_ref[mine, :] = (ici_buf[0] + ici_buf[1]) + (ici_buf[2] + ici_buf[3])

    return [(cp1.start, finish1), (start2, finish2), (cp3.start, cp3.wait)]


def _stage_sems():
    return [pltpu.SemaphoreType.DMA((2,)), pltpu.SemaphoreType.DMA((3,)),
            pltpu.SemaphoreType.DMA((3,)), pltpu.SemaphoreType.DMA((2,))]


def _scatter_scratch(r, c):
    return [pltpu.VMEM((N_CHIPS, r // 2, c), F32), pltpu.VMEM((N_CHIPS, r // 2, c), BF16),
            pltpu.VMEM((N_CHIPS, r // 2, c), BF16)] + _stage_sems()


def _reduce_all(gw_pad, sc_in, small, dada, c_all):
    R = small.shape[0]
    W = dada.shape[1]
    r_in, p_in = gw_pad.shape
    c_in = SHARD_IN
    chunk = r_in // 2 // N_CHUNK

    def chunk_scratch():
        return ([pltpu.VMEM((chunk, p_in), F32), pltpu.VMEM((N_CHIPS, chunk, c_in), BF16),
                 pltpu.VMEM((N_CHIPS, chunk, c_in), BF16)] + _stage_sems()
                + [pltpu.VMEM((N_CHIPS, chunk, c_in), F32)])

    n_in = len(chunk_scratch())

    c_wide = -(-c_in // 128) * 128

    def body(gin_ref, scin_ref, sm_ref, d_ref, c_ref, ocols_ref, osm_ref, gwa_ref, gba_ref, oin_ref, dall_ref,
             *scratch):
        x, y, cc = lax.axis_index("x"), lax.axis_index("y"), lax.axis_index("c")
        me = 4 * x + 2 * y + cc
        chip = 2 * x + y
        pos = (x, y, cc, chip, (x, y, 1 - cc))
        oin_ref[:, c_in:c_wide] = jnp.zeros((r_in, c_wide - c_in), F32)
        dslab, ds_sem, dr_sem = scratch[0:3]
        chunk_bufs = [scratch[3 + q * n_in:3 + (q + 1) * n_in] for q in range(N_CHUNK)]
        sm_bufs = scratch[3 + N_CHUNK * n_in:]
        dslab[...] = jnp.broadcast_to(d_ref[...], (8, W))
        dall_ref[me] = dslab[...]
        gathers = []
        for k, (dx, dy, dc) in enumerate(REL7):
            cp = pltpu.make_async_remote_copy(
                src_ref=dslab, dst_ref=dall_ref.at[me], send_sem=ds_sem.at[k], recv_sem=dr_sem.at[k],
                device_id=(_flip(x, dx), _flip(y, dy), _flip(cc, dc)), device_id_type=MESH)
            cp.start()
            gathers.append(cp)
        cols = _shard_cols()
        plans = [_scatter_stages(pos, gin_ref, scin_ref, oin_ref, *bufs[:-1], part=(q, N_CHUNK), cols=cols,
                                 own_buf=bufs[-1]) for q, bufs in enumerate(chunk_bufs)]
        little = _all_reduce_stages(pos, sm_ref, osm_ref, *sm_bufs)
        for plan in plans[:2] + [little]:
            plan[0][0]()
        for q, plan in enumerate(plans):
            plan[0][1]()
            if q + 2 < N_CHUNK:
                plans[q + 2][0][0]()
            plan[1][0]()
            if q == 0:
                little[0][1]()
                little[1][0]()
        for cp in gathers:
            cp.wait()
        slab_row = lax.broadcasted_iota(jnp.int32, (8, 1), 0)
        cm = jnp.zeros((8, D), F32)
        dm = jnp.zeros((8, W), F32)
        for r in range(8):
            cm = jnp.where(slab_row == r, c_ref[r], cm)
            dm = jnp.where(slab_row == r, dall_ref[r], dm)
        act = cm * _sigmoid(cm)
        dcol = dm[:, 0:SHARD_ADA]
        for a in range(1, N_CHIPS):
            dcol = jnp.where(chip == a, dm[:, a * SHARD_ADA:(a + 1) * SHARD_ADA], dcol)
        lhs = jnp.concatenate([act, jnp.zeros((8, D), F32)], axis=0).astype(BF16)
        rhs = jnp.concatenate([dcol, jnp.zeros((8, SHARD_ADA), F32)], axis=0).astype(BF16)
        gwa_ref[...] = _dot_tn(lhs, rhs)
        gba_ref[...] = _colsum(dm)
        for plan in plans + [little]:
            plan[1][1]()
            plan[2][0]()
        for q, plan in enumerate(plans):
            plan[2][1]()
            for base in (q * chunk, (N_CHUNK + q) * chunk):
                ocols_ref[:, :, base:base + chunk] = oin_ref[base:base + chunk, :].T[0:c_in, :][:, None, :]
        little[2][1]()

    scratch = [pltpu.VMEM((r_in, c_wide), F32), pltpu.VMEM((8, 8, W), F32),
               pltpu.VMEM((8, W), F32), pltpu.SemaphoreType.DMA((7,)), pltpu.SemaphoreType.DMA((7,))]
    for _ in range(N_CHUNK):
        scratch += chunk_scratch()
    scratch += [pltpu.VMEM((R // 2, 128), F32), pltpu.VMEM((N_CHIPS, R // 2, 128), F32)] + _stage_sems()
    vm = pl.BlockSpec(memory_space=pltpu.VMEM)
    return pl.pallas_call(
        body, name="reduce_all",
        out_shape=(jax.ShapeDtypeStruct((c_in, 1, r_in), F32), jax.ShapeDtypeStruct((R, 128), F32),
                   jax.ShapeDtypeStruct((D, SHARD_ADA), F32), jax.ShapeDtypeStruct((1, W), F32)),
        in_specs=[vm] * 5, out_specs=(vm,) * 4,
        scratch_shapes=scratch,
        compiler_params=_params(),
    )(gw_pad, sc_in, small, dada, c_all)


def _in_proj(x, shift, scale, wt_pad, b_pad, w_out_sh):
    S = x.shape[0]
    tm = min(TM_PROJ, S)
    n_steps = S // tm
    assert n_steps >= 3

    def body(x_ref, sh_ref, sc_ref, w_ref, b_ref, wo_ref, u_ref, qkv_ref, f_ref, p_ref, g_ref, wo_all,
             wo_buf, *gather_sems):
        i = pl.program_id(0)
        xx, yy, cc = lax.axis_index("x"), lax.axis_index("y"), lax.axis_index("c")
        row_half = lambda which: (pl.ds(pl.multiple_of(which * (SHARD_OUT // 2), SHARD_OUT // 2), SHARD_OUT // 2),
                                  slice(None))
        start, forward, finish = _gather_stages((xx, yy, cc, 2 * xx + yy, (xx, yy, 1 - cc)), wo_ref, wo_buf,
                                                row_half, *gather_sems)
        pl.when(i == 0)(start)
        pl.when(i == n_steps // 2)(forward)

        @pl.when(i == n_steps - 1)
        def _():
            finish()
            wo_all[...] = wo_buf[...]

        u = (x_ref[...] * (1.0 + sc_ref[...]) + sh_ref[...]).astype(BF16)
        u_ref[...] = u
        qkv_ref[...] = (_dot_nt(u, w_ref[O_QKV:O_F, :]) + b_ref[:, O_QKV:O_F]).astype(BF16)
        f_ref[...] = _dot_nt(u, w_ref[O_F:O_P, :]) + b_ref[:, O_F:O_P]
        p_ref[...] = _dot_nt(u, w_ref[O_P:O_G, :]) + b_ref[:, O_P:O_G]
        g_ref[...] = _dot_nt(u, w_ref[O_G:D_PAD, :]) + b_ref[:, O_G:D_PAD]

    row = lambda w: pl.BlockSpec((tm, w), lambda i: (i, 0))
    full = lambda a: pl.BlockSpec(a.shape, lambda i: (0, 0))
    vm = pl.BlockSpec(memory_space=pltpu.VMEM)
    return pl.pallas_call(
        body, name="in_proj", grid=(n_steps,),
        out_shape=(jax.ShapeDtypeStruct((S, D), BF16), jax.ShapeDtypeStruct((S, 3 * D_ATT), BF16),
                   jax.ShapeDtypeStruct((S, 128), F32), jax.ShapeDtypeStruct((S, D_POOL), F32),
                   jax.ShapeDtypeStruct((S, D), F32), jax.ShapeDtypeStruct((N_CHIPS,) + w_out_sh.shape, BF16)),
        in_specs=[row(D), full(shift), full(scale), full(wt_pad), full(b_pad), vm],
        out_specs=(row(D), row(3 * D_ATT), row(128), row(D_POOL), row(D), vm),
        scratch_shapes=[pltpu.VMEM((N_CHIPS,) + w_out_sh.shape, BF16)] + _gather_scratch(),
        compiler_params=_params(dimension_semantics=("arbitrary",)),
    )(x, shift, scale, wt_pad, b_pad, w_out_sh)


def _forget_cumsum(f):
    S = f.shape[0]
    tm = min(T_ATT, S)

    def body(f_ref, out_ref, carry):
        @pl.when(pl.program_id(0) == 0)
        def _():
            carry[...] = jnp.zeros_like(carry)
        v = f_ref[...]
        logf = jnp.minimum(v, 0.0) - jnp.log(1.0 + jnp.exp(-jnp.abs(v)))
        r = lax.broadcasted_iota(jnp.int32, (tm, tm), 0)
        c = lax.broadcasted_iota(jnp.int32, (tm, tm), 1)
        tri = (r <= c).astype(F32)
        rows8 = logf.T[0:8, :]
        cum8 = jnp.dot(rows8, tri, preferred_element_type=F32, precision=lax.Precision.HIGHEST) + carry[...]
        out_ref[...] = jnp.concatenate([cum8, jnp.zeros((128 - 8, tm), F32)], axis=0).T
        last = lax.broadcasted_iota(jnp.int32, (1, tm), 1) == tm - 1
        carry[...] = jnp.sum(jnp.where(last, cum8, 0.0), axis=1, keepdims=True)

    return pl.pallas_call(
        body, name="forget_cumsum", grid=(S // tm,),
        out_shape=jax.ShapeDtypeStruct((S, 128), F32),
        in_specs=[pl.BlockSpec((tm, 128), lambda i: (i, 0))],
        out_specs=pl.BlockSpec((tm, 128), lambda i: (i, 0)),
        scratch_shapes=[pltpu.VMEM((8, 1), F32)],
        compiler_params=_params(dimension_semantics=("arbitrary",)),
    )(f)


def _split3(v):
    hi = v.astype(BF16)
    rest = v - hi.astype(F32)
    mid = rest.astype(BF16)
    lo = (rest - mid.astype(F32)).astype(BF16)
    return hi, mid, lo


def _attention_fwd(qkv, big_f):
    S = qkv.shape[0]
    T = min(T_ATT, S)
    n_t = S // T

    def body(q_ref, k_ref, v_ref, f_ref, o_ref, lse_ref, kaug_sc, vt_sc, m_sc, l_sc, acc_sc):
        hp = pl.program_id(0)
        i = pl.program_id(1)
        lane = lax.broadcasted_iota(jnp.int32, (1, 128), 1)
        sub = lax.broadcasted_iota(jnp.int32, (128, 1), 0)
        head_sel = (lane < HEAD_DIM, lane >= HEAD_DIM)
        head_sel_t = (sub < HEAD_DIM, sub >= HEAD_DIM)
        spare = (HEAD_DIM, 0)
        zero = jnp.zeros((), BF16)

        @pl.when(i == 0)
        def _():
            def prep(jt, carry):
                rows = pl.ds(pl.multiple_of(jt * T, T), T)
                k = k_ref[rows, :]
                ft = f_ref[rows, :]
                vt = v_ref[rows, :].astype(F32).T
                for h in range(2):
                    fh = jnp.sum(jnp.where(lane == 2 * hp + h, ft, 0.0), axis=1, keepdims=True)
                    hi, mid, lo = _split3(-fh)
                    b = spare[h]
                    bias = jnp.where(lane == b, hi, jnp.where(lane == b + 1, mid, jnp.where(lane == b + 2, lo, zero)))
                    kaug_sc[h, rows, :] = jnp.where(head_sel[h], k, bias)
                    vt_sc[h, jt] = jnp.where(head_sel_t[h], vt, 0.0).astype(BF16)
                return carry

            lax.fori_loop(0, n_t, prep, 0)

        q = q_ref[...]
        q_heads = []
        for h in range(2):
            ones = jnp.where((lane >= spare[h]) & (lane < spare[h] + 3), jnp.ones((), BF16), zero)
            q_heads.append(jnp.where(head_sel[h], q, ones))
        m_sc[...] = jnp.full((8, T), NEG, F32)
        l_sc[...] = jnp.zeros((8, T), F32)
        acc_sc[...] = jnp.zeros((128, T), F32)

        def update(j, k_lo, n_k, q_lo, masked):
            rows = pl.ds(pl.multiple_of(j * T + k_lo, n_k), n_k)
            n_q = T - q_lo
            alphas, pvs = [], []
            for h in range(2):
                s_t = _dot_nt(kaug_sc[h, rows, :], q_heads[h][q_lo:, :])
                if masked:
                    rr = lax.broadcasted_iota(jnp.int32, (n_k, n_q), 0) + k_lo
                    cc = lax.broadcasted_iota(jnp.int32, (n_k, n_q), 1) + q_lo
                    s_t = jnp.where(rr <= cc, s_t, NEG)
                m_prev = m_sc[h:h + 1, q_lo:]
                m_new = jnp.maximum(m_prev, jnp.max(s_t, axis=0, keepdims=True))
                alpha = jnp.exp(m_prev - m_new)
                p_t = jnp.exp(s_t - m_new)
                l_sc[h:h + 1, q_lo:] = alpha * l_sc[h:h + 1, q_lo:] + jnp.sum(p_t, axis=0, keepdims=True)
                m_sc[h:h + 1, q_lo:] = m_new
                alphas.append(alpha)
                pvs.append(_dot(vt_sc[h, j, :, k_lo:k_lo + n_k], p_t.astype(BF16)))
            acc_sc[:, q_lo:] = (acc_sc[:, q_lo:] * jnp.where(head_sel_t[0], alphas[0], alphas[1])
                                + (pvs[0] + pvs[1]))

        def two_off_diagonal(jj, carry):
            update(2 * jj, 0, T, 0, False)
            update(2 * jj + 1, 0, T, 0, False)
            return carry

        lax.fori_loop(0, i // 2, two_off_diagonal, 0)

        @pl.when(i % 2 == 1)
        def _():
            update(i - 1, 0, T, 0, False)

        update(i, 0, T, 0, True)
        l = l_sc[...]
        o_ref[...] = (acc_sc[...] / jnp.where(head_sel_t[0], l[0:1, :], l[1:2, :])).T
        is_head = lax.broadcasted_iota(jnp.int32, (8, 1), 0) < 2
        lse_ref[...] = jnp.where(is_head, m_sc[...] + jnp.log(jnp.where(is_head, l, 1.0)), 0.0)

    return pl.pallas_call(
        body, name="attention_fwd", grid=(N_PAIR, n_t),
        out_shape=(jax.ShapeDtypeStruct((S, D_ATT), F32), jax.ShapeDtypeStruct((N_PAIR, n_t, 8, T), F32)),
        in_specs=[pl.BlockSpec((T, 128), lambda hp, i: (i, hp)),
                  pl.BlockSpec((S, 128), lambda hp, i: (0, N_PAIR + hp)),
                  pl.BlockSpec((S, 128), lambda hp, i: (0, 2 * N_PAIR + hp)),
                  pl.BlockSpec((S, 128), lambda hp, i: (0, 0))],
        out_specs=(pl.BlockSpec((T, 128), lambda hp, i: (i, hp)),
                   pl.BlockSpec((None, None, 8, T), lambda hp, i: (hp, i, 0, 0))),
        scratch_shapes=[pltpu.VMEM((2, S, 128), BF16), pltpu.VMEM((2, n_t, 128, T), BF16),
                        pltpu.VMEM((8, T), F32), pltpu.VMEM((8, T), F32), pltpu.VMEM((128, T), F32)],
        compiler_params=_params(dimension_semantics=("arbitrary", "arbitrary")),
    )(qkv, qkv, qkv, big_f)


def _attention_bwd(qkv, datt, att, lse, big_f, gw_out4, sc_out):
    S = qkv.shape[0]
    T = min(T_ATT, S)
    n_t = S // T
    n_steps = N_PAIR * n_t
    marks = (0, n_steps // 8, n_steps // 2, n_steps // 2 + n_steps // 8)

    def body(q_ref, do_ref, o_ref, lse_ref, k_ref, v_ref, fk_ref, gout_ref, scout_ref,
             dq_ref, dk_ref, dv_ref, cs_ref, dfk_ref, dfq_ref, oout_ref, stat_sc, dqt_sc, qaug_sc,
             out_buf, *red_bufs):
        hp = pl.program_id(0)
        j = pl.program_id(1)
        x, y, cc = lax.axis_index("x"), lax.axis_index("y"), lax.axis_index("c")
        plan = _scatter_stages((x, y, cc, 2 * x + y, (x, y, 1 - cc)), gout_ref, scout_ref, out_buf, *red_bufs)
        step = hp * n_t + j
        for n, mark in enumerate(marks):
            @pl.when(step == mark)
            def _(n=n):
                if n > 0:
                    plan[n - 1][1]()
                if n < 3:
                    plan[n][0]()
                else:
                    oout_ref[...] = out_buf[...]

        lane = lax.broadcasted_iota(jnp.int32, (1, 128), 1)
        sub = lax.broadcasted_iota(jnp.int32, (128, 1), 0)
        head_sel = (lane < HEAD_DIM, lane >= HEAD_DIM)
        head_sel_t = (sub < HEAD_DIM, sub >= HEAD_DIM)
        spare = (HEAD_DIM, 0)
        zero = jnp.zeros((), BF16)
        one = jnp.ones((), BF16)

        def bias_lanes(first, pieces):
            hi, mid, lo = pieces
            return lambda rest: jnp.where(lane == first, hi, jnp.where(lane == first + 1, mid,
                                                                        jnp.where(lane == first + 2, lo, rest)))

        @pl.when(j == 0)
        def _():
            dqt_sc[...] = jnp.zeros_like(dqt_sc)
            cs_ref[...] = jnp.zeros_like(cs_ref)
            dfq_ref[...] = jnp.zeros_like(dfq_ref)

            def prep(i, carry):
                rows = pl.ds(pl.multiple_of(i * T, T), T)
                q = q_ref[rows, :]
                do = do_ref[rows, :]
                prod = o_ref[rows, :] * do.astype(F32)
                d_a = jnp.sum(jnp.where(head_sel[0], prod, 0.0), axis=1, keepdims=True)
                d_b = jnp.sum(jnp.where(head_sel[0], 0.0, prod), axis=1, keepdims=True)
                delta_t = jnp.where(head_sel[0], d_a, d_b).T
                stat_sc[i, 0:1, :] = delta_t[0:1, :]
                stat_sc[i, 1:2, :] = delta_t[HEAD_DIM:HEAD_DIM + 1, :]
                lse = lse_ref[i]
                lse_cols = jnp.where(head_sel_t[0], lse[0:1, :], lse[1:2, :]).T
                for h in range(2):
                    neg_lse = -lse_cols[:, h * HEAD_DIM:h * HEAD_DIM + 1]
                    ones = jnp.where((lane >= spare[h]) & (lane < spare[h] + 3), one, zero)
                    qaug_sc[h, rows, :] = jnp.where(head_sel[h], q, bias_lanes(spare[h] + 3, _split3(neg_lse))(ones))
                return carry

            lax.fori_loop(0, n_t, prep, 0)

        k = k_ref[...]
        v = v_ref[...]
        fk = fk_ref[...]
        kt = k.astype(F32).T
        heads = []
        for h in range(2):
            fkh = jnp.sum(jnp.where(lane == 2 * hp + h, fk, 0.0), axis=1, keepdims=True)
            ones = jnp.where((lane >= spare[h] + 3) & (lane < spare[h] + 6), one, zero)
            kaug = jnp.where(head_sel[h], k, bias_lanes(spare[h], _split3(-fkh))(ones))
            heads.append((kaug, jnp.where(head_sel[h], v, zero), jnp.where(head_sel_t[h], kt, 0.0).astype(BF16)))

        def block(i, k_lo, n_k, q_lo, masked):
            n_q = T - q_lo
            rows = pl.ds(pl.multiple_of(i * T + q_lo, n_q), n_q)
            q = q_ref[rows, :]
            do = do_ref[rows, :]
            stat = stat_sc[i]
            dk = jnp.zeros((n_k, 128), F32)
            dv = jnp.zeros((n_k, 128), F32)
            dqt = jnp.zeros((128, n_q), F32)
            dfs = []
            for h in range(2):
                kaug, vh, kth = heads[h]
                arg = _dot_nt(kaug[k_lo:k_lo + n_k, :], qaug_sc[h, rows, :])
                if masked:
                    rr = lax.broadcasted_iota(jnp.int32, (n_k, n_q), 0) + k_lo
                    cc = lax.broadcasted_iota(jnp.int32, (n_k, n_q), 1) + q_lo
                    arg = jnp.where(rr <= cc, arg, NEG)
                p_t = jnp.exp(arg)
                ds_t = p_t * (_dot_nt(vh[k_lo:k_lo + n_k, :], do) - stat[h:h + 1, q_lo:])
                ds_bf = ds_t.astype(BF16)
                dv = dv + _dot(p_t.astype(BF16), jnp.where(head_sel[h], do, zero))
                dk = dk + _dot(ds_bf, jnp.where(head_sel[h], q, zero))
                dqt = dqt + _dot(kth[:, k_lo:k_lo + n_k], ds_bf)
                dfs.append(jnp.sum(ds_t, axis=1, keepdims=True))
                dfq_ref[i, h:h + 1, q_lo:] += _colsum(ds_t)
            dqt_sc[i, :, q_lo:] += dqt
            return dk, dv, dfs[0], dfs[1]

        def off_diagonal(i, acc):
            return tuple(a + b for a, b in zip(acc, block(i, 0, T, 0, False)))

        half = T // 2
        early = block(j, 0, half, 0, True)
        late = block(j, half, half, half, True)
        acc1 = tuple(jnp.concatenate([a, b], axis=0) for a, b in zip(early, late))
        n_off = n_t - 1 - j
        acc2 = lax.fori_loop(0, n_off // 2,
                             lambda ii, a: off_diagonal(j + 2 + 2 * ii, off_diagonal(j + 1 + 2 * ii, a)), acc1)
        dk_acc, dv_acc, dfa, dfb = lax.fori_loop(0, n_off % 2, lambda _, a: off_diagonal(n_t - 1, a), acc2)
        dk_ref[...] = dk_acc.astype(BF16)
        dv_ref[...] = dv_acc.astype(BF16)
        dfk_ref[...] = -jnp.where(lane == 0, dfa, jnp.where(lane == 1, dfb, 0.0))
        cs_ref[:, 128:256] = cs_ref[:, 128:256] + _colsum(dk_acc)
        cs_ref[:, 256:384] = cs_ref[:, 256:384] + _colsum(dv_acc)

        @pl.when(j == n_t - 1)
        def _():
            def finish(i, tot):
                dq = dqt_sc[i].T
                dq_ref[pl.ds(pl.multiple_of(i * T, T), T), :] = dq.astype(BF16)
                return tot + _colsum(dq)

            cs_ref[:, 0:128] = lax.fori_loop(0, n_t, finish, jnp.zeros((1, 128), F32))

    pair_rows = lambda hp, j: (hp, 0, 0)
    vm = pl.BlockSpec(memory_space=pltpu.VMEM)
    _, r_out, c_out = gw_out4.shape
    return pl.pallas_call(
        body, name="attention_bwd", grid=(N_PAIR, n_t),
        out_shape=(jax.ShapeDtypeStruct((S, D_ATT), BF16), jax.ShapeDtypeStruct((S, D_ATT), BF16),
                   jax.ShapeDtypeStruct((S, D_ATT), BF16), jax.ShapeDtypeStruct((N_PAIR, 1, 384), F32),
                   jax.ShapeDtypeStruct((N_PAIR, S, 128), F32),
                   jax.ShapeDtypeStruct((N_PAIR, n_t, 8, T), F32),
                   jax.ShapeDtypeStruct((r_out, c_out), F32)),
        in_specs=[pl.BlockSpec((S, 128), lambda hp, j: (0, hp)),
                  pl.BlockSpec((S, 128), lambda hp, j: (0, hp)),
                  pl.BlockSpec((S, 128), lambda hp, j: (0, hp)),
                  pl.BlockSpec((None, n_t, 8, T), lambda hp, j: (hp, 0, 0, 0)),
                  pl.BlockSpec((T, 128), lambda hp, j: (j, N_PAIR + hp)),
                  pl.BlockSpec((T, 128), lambda hp, j: (j, 2 * N_PAIR + hp)),
                  pl.BlockSpec((T, 128), lambda hp, j: (j, 0)),
                  vm, vm],
        out_specs=(pl.BlockSpec((S, 128), lambda hp, j: (0, hp)),
                   pl.BlockSpec((T, 128), lambda hp, j: (j, hp)),
                   pl.BlockSpec((T, 128), lambda hp, j: (j, hp)),
                   pl.BlockSpec((None, 1, 384), pair_rows),
                   pl.BlockSpec((None, T, 128), lambda hp, j: (hp, j, 0)),
                   pl.BlockSpec((None, n_t, 8, T), lambda hp, j: (hp, 0, 0, 0)),
                   vm),
        scratch_shapes=[pltpu.VMEM((n_t, 8, T), F32), pltpu.VMEM((n_t, 128, T), F32),
                        pltpu.VMEM((2, S, 128), BF16), pltpu.VMEM((r_out, c_out), F32)]
        + _scatter_scratch(r_out, c_out),
        compiler_params=_params(dimension_semantics=("arbitrary", "arbitrary")),
    )(qkv, datt, att, lse, qkv, qkv, big_f, gw_out4, sc_out)


def _window_counts(first_row, n_rows, window):
    t = lax.broadcasted_iota(jnp.int32, (n_rows, 1), 0) + first_row
    return jnp.minimum((t + 1).astype(F32), float(window))


def _middle(x, tgt, att, g, p, gate, w_mix, b_mix, pool_scale, w_out, b_out, ln_g, ln_b):
    S = x.shape[0]
    tm = min(TM_MID, S)
    halo_blocks = tm // POOL_HALO

    def body(x_ref, t_ref, att_ref, g_ref, p_ref, ph_ref, gate_ref, wm_ref, bm_ref, ps_ref, wo_ref, bo_ref,
             lg_ref, lb_ref,
             dh_ref, datt_ref, dg_ref, dpl_ref, gwo_ref, gwm_ref, vec_ref, loss_ref):
        i = pl.program_id(0)

        @pl.when(i == 0)
        def _():
            gwo_ref[...] = jnp.zeros_like(gwo_ref)
            gwm_ref[...] = jnp.zeros_like(gwm_ref)
            vec_ref[...] = jnp.zeros_like(vec_ref)
            loss_ref[...] = jnp.zeros_like(loss_ref)

        pc = p_ref[...]
        halo = jnp.where(i > 0, ph_ref[...], 0.0)
        pe = jnp.concatenate([halo, pc], axis=0)
        pooled_parts = []
        for gi, w in enumerate(POOL_WINDOWS):
            cur = pe[:, gi * POOL_GROUP:(gi + 1) * POOL_GROUP]
            span = 1
            while span < w:
                cur = cur + pltpu.roll(cur, span, 0)
                span *= 2
            wsum = cur[POOL_HALO:, :]
            mean = wsum / _window_counts(i * tm, tm, w)
            pooled_parts.append(mean - pc[:, gi * POOL_GROUP:(gi + 1) * POOL_GROUP])
        pooled_bf =[v.astype(BF16) for v in pooled_parts]
        mixed = jnp.concatenate([_dot(pooled_bf[gi], wm_ref[gi]) for gi in range(4)], axis=1) + bm_ref[...]
        ps = ps_ref[...]
        pool_out = mixed * ps
        gv = g_ref[...]
        sig = _sigmoid(gv)
        silu = gv * sig
        att = att_ref[...]
        y = jnp.concatenate([att * silu[:, :D_ATT], pool_out * silu[:, D_ATT:]], axis=1)
        y_bf = y.astype(BF16)
        wo = wo_ref[...]
        yo = _dot(y_bf, wo) + bo_ref[...]
        gate = gate_ref[...]
        h = ALPHA * x_ref[...] + gate * yo
        mu = jnp.mean(h, axis=1, keepdims=True)
        hc = h - mu
        var = jnp.mean(hc * hc, axis=1, keepdims=True)
        rstd = lax.rsqrt(var + LN_EPS)
        yhat = hc * rstd
        lg = lg_ref[...]
        out = yhat * lg + lb_ref[...]
        err = out - t_ref[...]
        loss_ref[...] += 0.5 * jnp.sum(jnp.mean(err * err, axis=1, keepdims=True), axis=0, keepdims=True)

        dout = err * (1.0 / D)
        g_ln_b = _colsum(dout)
        g_ln_g = _colsum(dout * yhat)
        dyh = dout * lg
        dh = rstd * (dyh - jnp.mean(dyh, axis=1, keepdims=True)
                     - yhat * jnp.mean(dyh * yhat, axis=1, keepdims=True))
        dh_ref[...] = dh
        d_gate = _colsum(dh * yo)
        dyo = gate * dh
        g_b_out = _colsum(dyo)
        dyo_bf = dyo.astype(BF16)
        gwo_ref[...] += _dot_tn(y_bf, dyo_bf)
        dy = _dot_nt(dyo_bf, wo)
        dsilu = sig * (1.0 + gv * (1.0 - sig))
        dy_a = dy[:, :D_ATT]
        dy_p = dy[:, D_ATT:]
        datt_ref[...] = (dy_a * silu[:, :D_ATT]).astype(BF16)
        dpo = dy_p * silu[:, D_ATT:]
        dg = jnp.concatenate([dy_a * att * dsilu[:, :D_ATT], dy_p * pool_out * dsilu[:, D_ATT:]], axis=1)
        dg_ref[...] = dg.astype(BF16)
        g_dg = _colsum(dg)
        g_ps = _colsum(dpo * mixed)
        dmixed = dpo * ps
        g_bm = _colsum(dmixed)
        dmixed_bf = dmixed.astype(BF16)
        dpl = []
        for gi in range(4):
            dm = dmixed_bf[:, gi * POOL_GROUP:(gi + 1) * POOL_GROUP]
            gwm_ref[gi] += _dot_tn(pooled_bf[gi], dm)
            dpl.append(_dot_nt(dm, wm_ref[gi]))
        dpl_ref[...] = jnp.concatenate(dpl, axis=1)
        vec_ref[0:1, :] += g_ln_g
        vec_ref[1:2, :] += g_ln_b
        vec_ref[2:3, :] += d_gate
        vec_ref[3:4, :] += g_b_out
        vec_ref[4:5, :] += g_dg
        vec_ref[5:6, 0:D_POOL] += g_ps
        vec_ref[6:7, 0:D_POOL] += g_bm

    row = lambda w: pl.BlockSpec((tm, w), lambda i: (i, 0))
    full2 = lambda a: pl.BlockSpec(a.shape, lambda i: (0, 0))
    full3 = lambda a: pl.BlockSpec(a.shape, lambda i: (0, 0, 0))
    return pl.pallas_call(
        body, name="middle", grid=(S // tm,),
        out_shape=(jax.ShapeDtypeStruct((S, D), F32),
                   jax.ShapeDtypeStruct((S, D_ATT), BF16),
                   jax.ShapeDtypeStruct((S, D), BF16),
                   jax.ShapeDtypeStruct((S, D_POOL), F32),
                   jax.ShapeDtypeStruct((D, D), F32),
                   jax.ShapeDtypeStruct((4, POOL_GROUP, POOL_GROUP), F32),
                   jax.ShapeDtypeStruct((8, D), F32),
                   jax.ShapeDtypeStruct((1, 1), F32)),
        in_specs=[row(D), row(D), row(D_ATT), row(D), row(D_POOL),
                  pl.BlockSpec((POOL_HALO, D_POOL), lambda i: (jnp.maximum(i * halo_blocks - 1, 0), 0)),
                  full2(gate), full3(w_mix), full2(b_mix), full2(pool_scale), full2(w_out), full2(b_out),
                  full2(ln_g), full2(ln_b)],
        out_specs=(row(D), row(D_ATT), row(D), row(D_POOL),
                   pl.BlockSpec((D, D), lambda i: (0, 0)),
                   pl.BlockSpec((4, POOL_GROUP, POOL_GROUP), lambda i: (0, 0, 0)),
                   pl.BlockSpec((8, D), lambda i: (0, 0)),
                   pl.BlockSpec((1, 1), lambda i: (0, 0))),
        compiler_params=_params(dimension_semantics=("arbitrary",)),
    )(x, tgt, att, g, p, p, gate, w_mix, b_mix, pool_scale, w_out, b_out, ln_g, ln_b)


def _tail(dpl, dfk, dfq, f):
    S = dpl.shape[0]
    tm = min(T_ATT, S)
    n_t = S // tm
    halo_blocks = tm // POOL_HALO
    last_halo = S // POOL_HALO - 1

    def body(d_ref, dn_ref, dfk_ref, dfq_ref, f_ref, dp_ref, df_ref, cs_ref, carry):
        s = pl.program_id(0)
        i = n_t - 1 - s

        @pl.when(s == 0)
        def _():
            carry[...] = jnp.zeros_like(carry)
            cs_ref[...] = jnp.zeros_like(cs_ref)

        dc = d_ref[...]
        nxt = jnp.where(s > 0, dn_ref[...], 0.0)
        de = jnp.concatenate([dc, nxt], axis=0)
        n_e = tm + POOL_HALO
        parts = []
        for gi, w in enumerate(POOL_WINDOWS):
            cur = de[:, gi * POOL_GROUP:(gi + 1) * POOL_GROUP] / _window_counts(i * tm, n_e, w)
            span = 1
            while span < w:
                cur = cur + pltpu.roll(cur, n_e - span, 0)
                span *= 2
            parts.append(cur[:tm, :] - dc[:, gi * POOL_GROUP:(gi + 1) * POOL_GROUP])
        dp = jnp.concatenate(parts, axis=1)
        dp_ref[...] = dp.astype(BF16)
        cs_ref[0:1, :] += _colsum(dp)

        r = lax.broadcasted_iota(jnp.int32, (tm, tm), 0)
        c = lax.broadcasted_iota(jnp.int32, (tm, tm), 1)
        tri = (r >= c).astype(F32)
        k_cols = dfk_ref[0]
        rows8 = dfq_ref[0]
        for hp in range(1, N_PAIR):
            k_cols = k_cols + pltpu.roll(dfk_ref[hp], 2 * hp, 1)
            rows8 = rows8 + pltpu.roll(dfq_ref[hp], 2 * hp, 0)
        rows8 = rows8 + k_cols.T[0:8, :]
        dlogf8 = jnp.dot(rows8, tri, preferred_element_type=F32, precision=lax.Precision.HIGHEST) + carry[...]
        first = lax.broadcasted_iota(jnp.int32, (1, tm), 1) == 0
        carry[...] = jnp.sum(jnp.where(first, dlogf8, 0.0), axis=1, keepdims=True)
        dlogf = jnp.concatenate([dlogf8, jnp.zeros((128 - 8, tm), F32)], axis=0).T
        df = dlogf * _sigmoid(-f_ref[...])
        df_ref[...] = df.astype(BF16)
        cs_ref[1:2, 0:128] += _colsum(df)

    rev = lambda w: pl.BlockSpec((tm, w), lambda s: (n_t - 1 - s, 0))
    return pl.pallas_call(
        body, name="tail", grid=(n_t,),
        out_shape=(jax.ShapeDtypeStruct((S, D_POOL), BF16), jax.ShapeDtypeStruct((S, 128), BF16),
                   jax.ShapeDtypeStruct((8, D_POOL), F32)),
        in_specs=[rev(D_POOL),
                  pl.BlockSpec((POOL_HALO, D_POOL),
                               lambda s: (jnp.minimum((n_t - s) * halo_blocks, last_halo), 0)),
                  pl.BlockSpec((N_PAIR, tm, 128), lambda s: (0, n_t - 1 - s, 0)),
                  pl.BlockSpec((N_PAIR, None, 8, tm), lambda s: (0, n_t - 1 - s, 0, 0)),
                  rev(128)],
        out_specs=(rev(D_POOL), rev(128), pl.BlockSpec((8, D_POOL), lambda s: (0, 0))),
        scratch_shapes=[pltpu.VMEM((8, 1), F32)],
        compiler_params=_params(dimension_semantics=("arbitrary",)),
    )(dpl, dpl, dfk, dfq, f)


PIECES = ((O_QKV, D_ATT), (O_QKV + D_ATT, D_ATT), (O_QKV + 2 * D_ATT, D_ATT), (O_F, 128), (O_P, D_POOL), (O_G, D))


def _grad_w_in(u, pieces):
    S = u.shape[0]
    tm = min(TM_GW, S)
    n_t = S // tm

    def body(u_ref, *rest):
        piece_refs, out_ref, acc, sem = rest[:6], rest[6], rest[7], rest[8]
        i = pl.program_id(0)

        @pl.when(i == 0)
        def _():
            acc[...] = jnp.zeros_like(acc)

        u_t = u_ref[...]
        for (off, w), ref in zip(PIECES, piece_refs):
            acc[:, off:off + w] += _dot_tn(u_t, ref[...])

        @pl.when(i == n_t - 1)
        def _():
            cp = pltpu.make_async_copy(acc, out_ref, sem)
            cp.start()
            cp.wait()

    return pl.pallas_call(
        body, name="grad_w_in", grid=(n_t,),
        out_shape=jax.ShapeDtypeStruct((D, D_PAD), F32),
        in_specs=[pl.BlockSpec((tm, D), lambda i: (i, 0))]
        + [pl.BlockSpec((tm, w), lambda i: (i, 0)) for _, w in PIECES],
        out_specs=pl.BlockSpec(memory_space=pl.ANY),
        scratch_shapes=[pltpu.VMEM((D, D_PAD), F32), pltpu.SemaphoreType.DMA],
        compiler_params=_params(dimension_semantics=("arbitrary",)),
    )(u, *pieces)


def _grad_x(pieces, wt_pad, dh, x, scale):
    S = x.shape[0]
    tm = min(TM_DU, S)

    def body(*refs):
        piece_refs = refs[:6]
        w_ref, dh_ref, x_ref, sc_ref, gx_ref, vec_ref = refs[6:]

        @pl.when(pl.program_id(0) == 0)
        def _():
            vec_ref[...] = jnp.zeros_like(vec_ref)

        du = jnp.zeros((tm, D), F32)
        for (off, w), ref in zip(PIECES, piece_refs):
            du = du + _dot(ref[...], w_ref[off:off + w, :])
        xv = x_ref[...]
        gx_ref[...] = ALPHA * dh_ref[...] + du * (1.0 + sc_ref[...])
        vec_ref[0:1, :] += _colsum(du)
        vec_ref[1:2, :] += _colsum(du * xv)

    row = lambda w: pl.BlockSpec((tm, w), lambda i: (i, 0))
    return pl.pallas_call(
        body, name="grad_x", grid=(S // tm,),
        out_shape=(jax.ShapeDtypeStruct((S, D), F32), jax.ShapeDtypeStruct((8, D), F32)),
        in_specs=[row(w) for _, w in PIECES]
        + [pl.BlockSpec(wt_pad.shape, lambda i: (0, 0)), row(D), row(D), pl.BlockSpec((1, D), lambda i: (0, 0))],
        out_specs=(row(D), pl.BlockSpec((8, D), lambda i: (0, 0))),
        compiler_params=_params(dimension_semantics=("arbitrary",)),
    )(*pieces, wt_pad, dh, x, scale)


def _adamw_math(w, g, m, v):
    m = ADAM_B1 * m + (1.0 - ADAM_B1) * g
    v = ADAM_B2 * v + (1.0 - ADAM_B2) * (g * g)
    m_hat = m / (1.0 - ADAM_B1 ** ADAM_STEP)
    v_hat = v / (1.0 - ADAM_B2 ** ADAM_STEP)
    delta = -ADAM_LR * (m_hat / (jnp.sqrt(v_hat) + ADAM_EPS) + ADAM_WD * w)
    return delta, m, v


def _adamw(groups, n_steps):
    n = len(groups)

    def body(*refs):
        ins, outs = refs[:4 * n], refs[4 * n:]
        for t in range(n):
            w, g, m, v = (r[...] for r in ins[4 * t:4 * t + 4])
            d, m2, v2 = _adamw_math(w, g, m, v)
            outs[4 * t][...] = d
            outs[4 * t + 1][...] = m2
            outs[4 * t + 2][...] = v2
            outs[4 * t + 3][...] = g

    in_specs, out_specs, out_shape, args = [], [], [], []
    for (w, g, m, v) in groups:
        rest = w.shape[1:]
        spec = pl.BlockSpec((w.shape[0] // n_steps,) + rest, lambda i, nd=len(rest): (i,) + (0,) * nd)
        in_specs += [spec] * 4
        out_specs += [spec] * 4
        out_shape += [jax.ShapeDtypeStruct(w.shape, F32)] * 4
        args += [w, g, m, v]
    return pl.pallas_call(
        body, name="adamw_%d_%d" % (n, n_steps), grid=(n_steps,),
        out_shape=tuple(out_shape), in_specs=in_specs, out_specs=tuple(out_specs),
        compiler_params=_params(dimension_semantics=("arbitrary",)),
    )(*args)


def _adamw_small(small_sum, g_b_ada, params):
    n = len(params)

    def body(gs_ref, gba_ref, *refs):
        ins, outs = refs[:3 * n], refs[3 * n:]
        for t, (name, w0, _, _) in enumerate(params):
            w_ref, m_ref, v_ref = ins[3 * t:3 * t + 3]
            first = SMALL_SEGS[name][0] if name in SMALL_SEGS else None
            if w0.shape[0] > 1:
                pieces = [((slice(None), slice(None)), gs_ref[first:first + w0.shape[0], :])]
            else:
                pieces = []
                for r in range(-(-w0.shape[1] // 128)):
                    lanes = slice(128 * r, min(128 * r + 128, w0.shape[1]))
                    g = gba_ref[0:1, lanes] if first is None else gs_ref[first + r:first + r + 1, 0:lanes.stop - lanes.start]
                    pieces.append(((slice(0, 1), lanes), g))
            for where, g in pieces:
                d, m2, v2 = _adamw_math(w_ref[where], g, m_ref[where], v_ref[where])
                for ref, val in zip(outs[4 * t:4 * t + 4], (g, d, m2, v2)):
                    ref[where] = val

    vm = pl.BlockSpec(memory_space=pltpu.VMEM)
    args = [small_sum, g_b_ada]
    out_shape = []
    for _, w, m, v in params:
        args += [w, m, v]
        out_shape += [jax.ShapeDtypeStruct(w.shape, F32)] * 4
    return pl.pallas_call(
        body, name="adamw_small",
        out_shape=tuple(out_shape), in_specs=[vm] * len(args), out_specs=(vm,) * len(out_shape),
        compiler_params=_params(),
    )(*args)


def _pack_small(parts):
    rows = []
    used = 0
    for name, (first, n_rows) in SMALL_SEGS.items():
        if first > used:
            rows.append(jnp.zeros((first - used, 128), F32))
        flat = parts[name].reshape(-1)
        flat = jnp.pad(flat, (0, n_rows * 128 - flat.shape[0]))
        rows.append(flat.reshape(n_rows, 128))
        used = first + n_rows
    rows.append(jnp.zeros((SMALL_ROWS - used, 128), F32))
    return jnp.concatenate(rows, axis=0)


def _unpack_small(buf, name, shape):
    first, n_rows = SMALL_SEGS[name]
    n = int(np.prod(shape))
    return buf[first:first + n_rows].reshape(-1)[:n].reshape(shape)


def _pad_in(v):
    r = v.shape[0]
    z = jnp.zeros((r, O_P - O_F - N_HEADS), v.dtype)
    return jnp.concatenate([v[:, :3 * D_ATT + N_HEADS], z, v[:, 3 * D_ATT + N_HEADS:]], axis=1)


def _unpad_in(v):
    return jnp.concatenate([v[:, :O_F + N_HEADS], v[:, O_P:]], axis=1)


def _shards_in(v):
    gap = O_P - (O_F + N_HEADS)
    parts = []
    for a in range(N_CHIPS):
        lo, hi = a * SHARD_IN, (a + 1) * SHARD_IN
        cut = O_F + N_HEADS
        if hi <= cut:
            parts.append(v[:, lo:hi])
        elif lo >= cut:
            parts.append(v[:, lo + gap:hi + gap])
        else:
            parts.append(jnp.concatenate([v[:, lo:cut], v[:, cut + gap:hi + gap]], axis=1))
    return jnp.stack(parts, axis=0)


def kernel(x, c, w_ada, b_ada, w_in, b_in, w_pool_mix, b_pool_mix, pool_scale, w_out, b_out, ln_g, ln_b, loss_target, m_w_ada, m_b_ada, m_w_in, m_b_in, m_w_pool_mix, m_b_pool_mix, m_pool_scale, m_w_out, m_b_out, m_ln_g, m_ln_b, v_w_ada, v_b_ada, v_w_in, v_b_in, v_w_pool_mix, v_b_pool_mix, v_pool_scale, v_w_out, v_b_out, v_ln_g, v_ln_b):
    S = x.shape[1]
    T = min(T_ATT, S)
    n_t = S // T
    x2 = x[0]
    tgt = loss_target[0]
    q_scale = jnp.concatenate([jnp.full((1, D_ATT), Q_SCALE, F32), jnp.ones((1, D_PAD - D_ATT), F32)], axis=1)

    to_cols = lambda a: jnp.transpose(a, (2, 0, 1))
    from_cols = lambda a: jnp.transpose(a, (1, 2, 0))
    c_all, ada4, wt_pad = _gather_and_ada(
        c, w_ada[0], b_ada.reshape(4, 1, SHARD_ADA), to_cols(w_in))
    ada = ada4[:, 0, :].reshape(1, 3 * D)
    shift, scale, gate = ada[:, :D], ada[:, D:2 * D], ada[:, 2 * D:]
    b_pad = _pad_in(b_in) * q_scale
    w_mix_bf = w_pool_mix[0].astype(BF16)

    u, qkv, f, p, g, w_out_all = _in_proj(x2, shift, scale, wt_pad, b_pad, w_out[0].astype(BF16))
    w_out_full = w_out_all.reshape(D, D)
    big_f = _forget_cumsum(f)
    att, lse = _attention_fwd(qkv, big_f)

    dh, datt, dg, dpl, gw_out, gw_mix, vec, loss_part = _middle(
        x2, tgt, att, g, p, gate, w_mix_bf, b_pool_mix.reshape(1, D_POOL), pool_scale, w_out_full, b_out, ln_g, ln_b)
    dq, dk, dv, cs_att, dfk, dfq, g_w_out = _attention_bwd(
        qkv, datt, att, lse, big_f, gw_out.reshape(N_CHIPS, SHARD_OUT, D), jnp.ones((N_CHIPS, 1, D), F32))
    dp, df, cs_tail = _tail(dpl, dfk, dfq, f)
    pieces = (dq, dk, dv, df, dp, dg)
    gw_pad = _grad_w_in(u, pieces)
    grad_x, vec_x = _grad_x(pieces, wt_pad, dh, x2, scale)

    cs_qkv = jnp.transpose(cs_att.reshape(N_PAIR, 3, 128), (1, 0, 2)).reshape(1, 3 * D_ATT)
    gb_pad = jnp.concatenate([cs_qkv, cs_tail[1:2, 0:128], cs_tail[0:1, :], vec[4:5, :]], axis=1) * q_scale
    dada = jnp.concatenate([vec_x[0:1, :], vec_x[1:2, :], vec[2:3, :]], axis=1)
    small = _pack_small({
        "b_in": _unpad_in(gb_pad), "w_pool_mix": gw_mix, "b_pool_mix": vec[6:7, :D_POOL],
        "pool_scale": vec[5:6, :D_POOL], "b_out": vec[3:4, :], "ln_g": vec[0:1, :], "ln_b": vec[1:2, :],
        "loss": loss_part})

    g_w_in, small_sum, g_w_ada, g_b_ada = _reduce_all(
        gw_pad, _shards_in(q_scale), small, dada, c_all)
    loss = _unpack_small(small_sum, "loss", (1,))[0]

    big = _adamw([(w_ada[0], g_w_ada, m_w_ada[0], v_w_ada[0]),
                  (w_out[0], g_w_out, m_w_out[0], v_w_out[0])], 4)
    big_in = _adamw([(to_cols(w_in), g_w_in, to_cols(m_w_in), to_cols(v_w_in))], 5)
    tiles = lambda a: a.reshape(4 * POOL_GROUP, POOL_GROUP)
    flat = lambda a: a.reshape(1, D_POOL)
    small_params = [("b_ada", b_ada, m_b_ada, v_b_ada), ("b_in", b_in, m_b_in, v_b_in),
                    ("w_pool_mix", tiles(w_pool_mix), tiles(m_w_pool_mix), tiles(v_w_pool_mix)),
                    ("b_pool_mix", flat(b_pool_mix), flat(m_b_pool_mix), flat(v_b_pool_mix)),
                    ("pool_scale", pool_scale, m_pool_scale, v_pool_scale), ("b_out", b_out, m_b_out, v_b_out),
                    ("ln_g", ln_g, m_ln_g, v_ln_g), ("ln_b", ln_b, m_ln_b, v_ln_b)]
    sm = _adamw_small(small_sum, g_b_ada, small_params)
    sm_idx = {p[0]: n for n, p in enumerate(small_params)}
    shapes = {"w_pool_mix": (1, 4, POOL_GROUP, POOL_GROUP), "b_pool_mix": (1, 4, POOL_GROUP)}

    names = ["w_ada", "b_ada", "w_in", "b_in", "w_pool_mix", "b_pool_mix", "pool_scale", "w_out", "b_out",
             "ln_g", "ln_b"]
    big_idx = {"w_ada": 0, "w_out": 1}

    def leaf(kind, name):
        if name == "w_in":
            return from_cols(big_in[(kind - 1) % 4])
        if name in big_idx:
            return big[4 * big_idx[name] + (kind - 1) % 4][None]
        val = sm[4 * sm_idx[name] + kind]
        return val.reshape(shapes[name]) if name in shapes else val

    outs = [loss, grad_x[None]]
    for kind in range(4):
        outs += [leaf(kind, n) for n in names]
    return tuple(outs)
```

```python
import functools

import numpy as np
import jax
import jax.numpy as jnp
from jax import lax
from jax.experimental import pallas as pl
from jax.experimental.pallas import tpu as pltpu

F32 = jnp.float32
BF16 = jnp.bfloat16
MESH = pl.DeviceIdType.MESH

D = 1024
D_ATT = 512
D_POOL = 512
N_HEADS = 8
HEAD_DIM = 64
N_PAIR = N_HEADS // 2
POOL_WINDOWS = (2, 4, 8, 16)
POOL_GROUP = 128
POOL_HALO = 16
LN_EPS = 1e-5
ALPHA = 2.0 ** 0.25
D_IN = 3 * D_ATT + N_HEADS + D_POOL + D_ATT + D_POOL
N_CHIPS = 4
SHARD_IN = D_IN // N_CHIPS
SHARD_ADA = 3 * D // N_CHIPS
SHARD_OUT = D // N_CHIPS

O_QKV, O_F, O_P, O_G, D_PAD = 0, 1536, 1664, 2176, 3200
Q_SCALE = HEAD_DIM ** -0.5

ADAM_LR, ADAM_B1, ADAM_B2, ADAM_EPS, ADAM_WD, ADAM_STEP = 0.001, 0.9, 0.999, 1e-08, 0.01, 10

NEG = -1e30

VMEM_LIMIT = 56 * 1024 * 1024

TM_PROJ = 512
T_ATT = 512
TM_MID = 512
TM_GW = 1024
TM_DU = 512

REL7 = [(0, 0, 1), (0, 1, 0), (0, 1, 1), (1, 0, 0), (1, 0, 1), (1, 1, 0), (1, 1, 1)]
REL3 = [(0, 1), (1, 0), (1, 1)]

SMALL_SEGS = {}
_row = 0
for _name, _n in (("b_in", D_IN), ("w_pool_mix", 65536), ("b_pool_mix", 512), ("pool_scale", 512),
                  ("b_out", 1024), ("ln_g", 1024), ("ln_b", 1024), ("loss", 1)):
    _rows = -(-_n // 1024) * 8
    SMALL_SEGS[_name] = (_row, _rows)
    _row += _rows
SMALL_ROWS = -(-_row // 16) * 16


def _params(**kw):
    return pltpu.CompilerParams(vmem_limit_bytes=VMEM_LIMIT, **kw)


def _flip(v, d):
    return v if d == 0 else 1 - v


def _dot(a, b):
    return jnp.dot(a, b, preferred_element_type=F32)


def _dot_nt(a, b):
    return lax.dot_general(a, b, (((1,), (1,)), ((), ())), preferred_element_type=F32)


def _dot_tn(a, b):
    return lax.dot_general(a, b, (((0,), (0,)), ((), ())), preferred_element_type=F32)


def _sigmoid(v):
    return 1.0 / (1.0 + jnp.exp(-v))


def _colsum(v):
    return jnp.sum(v, axis=0, keepdims=True)


def _gather_stages(pos, src_ref, dst_ref, half, own_sem, s_sem, r_sem, fs_sem, fr_sem):
    x, y, cc, chip, sib = pos
    own = pltpu.make_async_copy(src_ref, dst_ref.at[chip], own_sem)
    first, landed, others = [], [], []
    for k, (dx, dy) in enumerate(REL3):
        px, py = _flip(x, dx), _flip(y, dy)
        first.append(pltpu.make_async_remote_copy(
            src_ref=src_ref.at[half(cc)], dst_ref=dst_ref.at[(chip,) + half(cc)],
            send_sem=s_sem.at[k], recv_sem=r_sem.at[k], device_id=(px, py, cc), device_id_type=MESH))
        landed.append(dst_ref.at[(2 * px + py,) + half(cc)])
        others.append(dst_ref.at[(2 * px + py,) + half(1 - cc)])
    passed = [pltpu.make_async_remote_copy(src_ref=landed[k], dst_ref=landed[k], send_sem=fs_sem.at[k],
                                           recv_sem=fr_sem.at[k], device_id=sib, device_id_type=MESH)
              for k in range(3)]

    def start(finish_src=None):
        for cp in first:
            cp.start()
        if finish_src is not None:
            finish_src()
        own.start()

    def forward():
        for k in range(3):
            pltpu.make_async_remote_copy(src_ref=landed[k], dst_ref=landed[k], send_sem=s_sem.at[k],
                                         recv_sem=r_sem.at[k], device_id=sib, device_id_type=MESH).wait_recv()
            passed[k].start()

    def finish():
        for k in range(3):
            pltpu.make_async_remote_copy(src_ref=others[k], dst_ref=others[k], send_sem=fs_sem.at[k],
                                         recv_sem=fr_sem.at[k], device_id=sib, device_id_type=MESH).wait_recv()
        for cp in first + passed:
            cp.wait_send()
        own.wait()

    return start, forward, finish


def _gather_scratch():
    return [pltpu.SemaphoreType.DMA, pltpu.SemaphoreType.DMA((3,)), pltpu.SemaphoreType.DMA((3,)),
            pltpu.SemaphoreType.DMA((3,)), pltpu.SemaphoreType.DMA((3,))]


def _gather_and_ada(c, w_ada, b_ada4, w_in_sh):
    def body(c_ref, w_ref, b_ref, win_ref, call_ref, ada_ref, wt_pad_ref,
             win_all, win_bf, cslab, sbuf, rbuf, cs_sem, cr_sem, as_sem, ar_sem, *gather_sems):
        x, y, cc = lax.axis_index("x"), lax.axis_index("y"), lax.axis_index("c")
        me = 4 * x + 2 * y + cc
        chip = 2 * x + y
        lane_half = lambda which: (slice(None), pl.ds(pl.multiple_of(which * (D // 2), D // 2), D // 2))
        def round_half(which):
            for h in range(2):
                @pl.when(which == h)
                def _():
                    lanes = slice(h * (D // 2), (h + 1) * (D // 2))
                    win_bf[:, lanes] = win_ref[:, 0, lanes].astype(BF16)

        start, forward, finish = _gather_stages((x, y, cc, chip, (x, y, 1 - cc)), win_bf, win_all, lane_half,
                                                *gather_sems)
        round_half(cc)
        start(lambda: round_half(1 - cc))

        cslab[...] = jnp.broadcast_to(c_ref[...], (8, D))
        call_ref[me] = cslab[...]
        gathers = []
        for k, (dx, dy, dc) in enumerate(REL7):
            cp = pltpu.make_async_remote_copy(
                src_ref=cslab, dst_ref=call_ref.at[me], send_sem=cs_sem.at[k], recv_sem=cr_sem.at[k],
                device_id=(_flip(x, dx), _flip(y, dy), _flip(cc, dc)), device_id_type=MESH)
            cp.start()
            gathers.append(cp)
        for cp in gathers:
            cp.wait()
        slab_row = lax.broadcasted_iota(jnp.int32, (8, 1), 0)
        mat = jnp.zeros((8, D), F32)
        for r in range(8):
            mat = jnp.where(slab_row == r, call_ref[r], mat)
        act = (mat * _sigmoid(mat)).astype(BF16)
        part = _dot(act, w_ref[...].astype(BF16))
        sends = []
        for k, (dx, dy) in enumerate(REL3):
            px, py = _flip(x, dx), _flip(y, dy)
            r = 4 * px + 2 * py + cc
            piece = _colsum(jnp.where(slab_row == r, part, 0.0))
            sbuf[k] = jnp.broadcast_to(piece, (8, SHARD_ADA))
            cp = pltpu.make_async_remote_copy(
                src_ref=sbuf.at[k], dst_ref=rbuf.at[k], send_sem=as_sem.at[k], recv_sem=ar_sem.at[k],
                device_id=(px, py, cc), device_id_type=MESH)
            cp.start()
            sends.append(cp)
        own_piece = _colsum(jnp.where(slab_row == me, part, 0.0))
        ada_ref[chip] = jnp.broadcast_to(own_piece, (8, SHARD_ADA)) + b_ref[chip]
        for k, (dx, dy) in enumerate(REL3):
            sends[k].wait()
            a = 2 * _flip(x, dx) + _flip(y, dy)
            ada_ref[a] = rbuf[k] + b_ref[a]

        forward()
        finish()
        n_real = 3 * D_ATT + N_HEADS
        for a in range(N_CHIPS):
            lo, hi = a * SHARD_IN, (a + 1) * SHARD_IN
            for s0, s1 in ((lo, min(hi, D_ATT)), (max(lo, D_ATT), min(hi, n_real)), (max(lo, n_real), hi)):
                if s0 < s1:
                    rows = win_all[a, s0 - lo:s1 - lo, :]
                    if s1 <= D_ATT:
                        rows = rows * jnp.asarray(Q_SCALE, BF16)
                    shift = O_P - n_real if s0 >= n_real else 0
                    wt_pad_ref[s0 + shift:s1 + shift, :] = rows
        wt_pad_ref[n_real:O_P, :] = jnp.zeros((O_P - n_real, D), BF16)

    vm = pl.BlockSpec(memory_space=pltpu.VMEM)
    return pl.pallas_call(
        body, name="gather_and_ada",
        out_shape=(jax.ShapeDtypeStruct((8, 8, D), F32), jax.ShapeDtypeStruct((4, 8, SHARD_ADA), F32),
                   jax.ShapeDtypeStruct((D_PAD, D), BF16)),
        in_specs=[vm] * 4, out_specs=(vm,) * 3,
        scratch_shapes=[pltpu.VMEM((N_CHIPS, SHARD_IN, D), BF16), pltpu.VMEM((SHARD_IN, D), BF16),
                        pltpu.VMEM((8, D), F32), pltpu.VMEM((3, 8, SHARD_ADA), F32),
                        pltpu.VMEM((3, 8, SHARD_ADA), F32),
                        pltpu.SemaphoreType.DMA((7,)), pltpu.SemaphoreType.DMA((7,)),
                        pltpu.SemaphoreType.DMA((3,)), pltpu.SemaphoreType.DMA((3,))] + _gather_scratch(),
        compiler_params=_params(),
    )(c, w_ada, b_ada4, w_in_sh)


def _shard_cols():
    cut, gap = O_F + N_HEADS, O_P - (O_F + N_HEADS)
    out = []
    for a in range(N_CHIPS):
        lo, hi = a * SHARD_IN, (a + 1) * SHARD_IN
        out.append(([(lo, min(hi, cut))] if lo < cut else []) + ([(max(lo, cut) + gap, hi + gap)] if hi > cut else []))
    return out


def _scatter_stages(pos, g_ref, sc_ref, out_ref, sib_buf, send_buf, ici_buf, sem1, sem2s, sem2r, sem3, part=(0, 1),
                    cols=None, own_buf=None):
    x, y, cc, chip, sib = pos
    q, n_parts = part
    RH = (g_ref.shape[1] if cols is None else g_ref.shape[0]) // 2 // n_parts
    mine = pl.ds(pl.multiple_of((cc * n_parts + q) * RH, RH), RH)
    theirs = pl.ds(pl.multiple_of(((1 - cc) * n_parts + q) * RH, RH), RH)
    cp1 = pltpu.make_async_remote_copy(
        src_ref=g_ref.at[:, theirs, :] if cols is None else g_ref.at[theirs, :], dst_ref=sib_buf,
        send_sem=sem1.at[0], recv_sem=sem1.at[1], device_id=sib, device_id_type=MESH)
    sends = []
    for k, (dx, dy) in enumerate(REL3):
        px, py = _flip(x, dx), _flip(y, dy)
        sends.append(pltpu.make_async_remote_copy(
            src_ref=send_buf.at[2 * px + py], dst_ref=ici_buf.at[chip],
            send_sem=sem2s.at[k], recv_sem=sem2r.at[k], device_id=(px, py, cc), device_id_type=MESH))
    cp3 = pltpu.make_async_remote_copy(
        src_ref=out_ref.at[mine, :], dst_ref=out_ref.at[mine, :], send_sem=sem3.at[0], recv_sem=sem3.at[1],
        device_id=sib, device_id_type=MESH)

    def finish1():
        cp1.wait()
        if cols is None:
            for a in range(N_CHIPS):
                both = g_ref[a, mine, :] + sib_buf[a]
                sib_buf[a] = both
                send_buf[a] = both.astype(BF16)
        else:
            both = g_ref[mine, :] + sib_buf[...]
            for a, pieces in enumerate(cols):
                at = 0
                for lo, hi in pieces:
                    own_buf[a, :, at:at + hi - lo] = both[:, lo:hi]
                    send_buf[a, :, at:at + hi - lo] = both[:, lo:hi].astype(BF16)
                    at += hi - lo

    def start2():
        for cp in sends:
            cp.start()
        ici_buf[chip] = send_buf[chip]

    def finish2():
        for cp in sends:
            cp.wait()
        own = (sib_buf if cols is None else own_buf)[chip]
        parts = [jnp.where(chip == a, own, ici_buf[a].astype(F32)) for a in range(N_CHIPS)]
        out_ref[mine, 0:own.shape[1]] = ((parts[0] + parts[1]) + (parts[2] + parts[3])) * sc_ref[chip]

    return [(cp1.start, finish1), (start2, finish2), (cp3.start, cp3.wait)]


def _all_reduce_stages(pos, g_ref, out_ref, sib_buf, ici_buf, sem1, sem2s, sem2r, sem3):
    x, y, cc, chip, sib = pos
    RH = g_ref.shape[0] // 2
    mine = pl.ds(pl.multiple_of(cc * RH, 8), RH)
    theirs = pl.ds(pl.multiple_of((1 - cc) * RH, 8), RH)
    cp1 = pltpu.make_async_remote_copy(
        src_ref=g_ref.at[theirs, :], dst_ref=sib_buf, send_sem=sem1.at[0], recv_sem=sem1.at[1],
        device_id=sib, device_id_type=MESH)
    sends = []
    for k, (dx, dy) in enumerate(REL3):
        px, py = _flip(x, dx), _flip(y, dy)
        sends.append(pltpu.make_async_remote_copy(
            src_ref=sib_buf, dst_ref=ici_buf.at[chip],
            send_sem=sem2s.at[k], recv_sem=sem2r.at[k], device_id=(px, py, cc), device_id_type=MESH))
    cp3 = pltpu.make_async_remote_copy(
        src_ref=out_ref.at[mine, :], dst_ref=out_ref.at[mine, :], send_sem=sem3.at[0], recv_sem=sem3.at[1],
        device_id=sib, device_id_type=MESH)

    def finish1():
        cp1.wait()
        sib_buf[...] = g_ref[mine, :] + sib_buf[...]

    def start2():
        for cp in sends:
            cp.start()
        ici_buf[chip] = sib_buf[...]

    def finish2():
        for cp in sends:
            cp.wait()
        out_ref[mine, :] = (ici_buf[0] + ici_buf[1]) + (ici_buf[2] + ici_buf[3])

    return [(cp1.start, finish1), (start2, finish2), (cp3.start, cp3.wait)]


def _stage_sems():
    return [pltpu.SemaphoreType.DMA((2,)), pltpu.SemaphoreType.DMA((3,)),
            pltpu.SemaphoreType.DMA((3,)), pltpu.SemaphoreType.DMA((2,))]


def _scatter_scratch(r, c):
    return [pltpu.VMEM((N_CHIPS, r // 2, c), F32), pltpu.VMEM((N_CHIPS, r // 2, c), BF16),
            pltpu.VMEM((N_CHIPS, r // 2, c), BF16)] + _stage_sems()


def _reduce_all(gw_pad, sc_in, small, dada, c_all):
    R = small.shape[0]
    W = dada.shape[1]
    r_in, p_in = gw_pad.shape
    c_in = SHARD_IN
    chunk = r_in // 4

    def chunk_scratch():
        return ([pltpu.VMEM((chunk, p_in), F32), pltpu.VMEM((N_CHIPS, chunk, c_in), BF16),
                 pltpu.VMEM((N_CHIPS, chunk, c_in), BF16)] + _stage_sems()
                + [pltpu.VMEM((N_CHIPS, chunk, c_in), F32)])

    n_in = len(chunk_scratch())

    c_wide = -(-c_in // 128) * 128

    def body(gin_ref, scin_ref, sm_ref, d_ref, c_ref, ocols_ref, osm_ref, gwa_ref, gba_ref, oin_ref, dall_ref,
             *scratch):
        x, y, cc = lax.axis_index("x"), lax.axis_index("y"), lax.axis_index("c")
        me = 4 * x + 2 * y + cc
        chip = 2 * x + y
        pos = (x, y, cc, chip, (x, y, 1 - cc))
        oin_ref[:, c_in:c_wide] = jnp.zeros((r_in, c_wide - c_in), F32)
        dslab, ds_sem, dr_sem = scratch[0:3]
        a_bufs, b_bufs, sm_bufs = scratch[3:3 + n_in], scratch[3 + n_in:3 + 2 * n_in], scratch[3 + 2 * n_in:]
        dslab[...] = jnp.broadcast_to(d_ref[...], (8, W))
        dall_ref[me] = dslab[...]
        gathers = []
        for k, (dx, dy, dc) in enumerate(REL7):
            cp = pltpu.make_async_remote_copy(
                src_ref=dslab, dst_ref=dall_ref.at[me], send_sem=ds_sem.at[k], recv_sem=dr_sem.at[k],
                device_id=(_flip(x, dx), _flip(y, dy), _flip(cc, dc)), device_id_type=MESH)
            cp.start()
            gathers.append(cp)
        cols = _shard_cols()
        first = _scatter_stages(pos, gin_ref, scin_ref, oin_ref, *a_bufs[:-1], part=(0, 2), cols=cols,
                                own_buf=a_bufs[-1])
        second = _scatter_stages(pos, gin_ref, scin_ref, oin_ref, *b_bufs[:-1], part=(1, 2), cols=cols,
                                 own_buf=b_bufs[-1])
        little = _all_reduce_stages(pos, sm_ref, osm_ref, *sm_bufs)
        for plan in (first, second, little):
            plan[0][0]()
        first[0][1]()
        first[1][0]()
        little[0][1]()
        little[1][0]()
        second[0][1]()
        second[1][0]()
        for cp in gathers:
            cp.wait()
        slab_row = lax.broadcasted_iota(jnp.int32, (8, 1), 0)
        cm = jnp.zeros((8, D), F32)
        dm = jnp.zeros((8, W), F32)
        for r in range(8):
            cm = jnp.where(slab_row == r, c_ref[r], cm)
            dm = jnp.where(slab_row == r, dall_ref[r], dm)
        act = cm * _sigmoid(cm)
        dcol = dm[:, 0:SHARD_ADA]
        for a in range(1, N_CHIPS):
            dcol = jnp.where(chip == a, dm[:, a * SHARD_ADA:(a + 1) * SHARD_ADA], dcol)
        lhs = jnp.concatenate([act, jnp.zeros((8, D), F32)], axis=0).astype(BF16)
        rhs = jnp.concatenate([dcol, jnp.zeros((8, SHARD_ADA), F32)], axis=0).astype(BF16)
        gwa_ref[...] = _dot_tn(lhs, rhs)
        gba_ref[...] = _colsum(dm)
        first[1][1]()
        first[2][0]()
        second[1][1]()
        second[2][0]()
        little[1][1]()
        little[2][0]()
        for plan in (first, second, little):
            plan[2][1]()
        ocols_ref[...] = oin_ref[...].T[0:c_in, :][:, None, :]

    scratch = [pltpu.VMEM((r_in, c_wide), F32), pltpu.VMEM((8, 8, W), F32),
               pltpu.VMEM((8, W), F32), pltpu.SemaphoreType.DMA((7,)), pltpu.SemaphoreType.DMA((7,))]
    scratch += chunk_scratch() + chunk_scratch()
    scratch += [pltpu.VMEM((R // 2, 128), F32), pltpu.VMEM((N_CHIPS, R // 2, 128), F32)] + _stage_sems()
    vm = pl.BlockSpec(memory_space=pltpu.VMEM)
    return pl.pallas_call(
        body, name="reduce_all",
        out_shape=(jax.ShapeDtypeStruct((c_in, 1, r_in), F32), jax.ShapeDtypeStruct((R, 128), F32),
                   jax.ShapeDtypeStruct((D, SHARD_ADA), F32), jax.ShapeDtypeStruct((1, W), F32)),
        in_specs=[vm] * 5, out_specs=(vm,) * 4,
        scratch_shapes=scratch,
        compiler_params=_params(),
    )(gw_pad, sc_in, small, dada, c_all)


def _in_proj(x, shift, scale, wt_pad, b_pad, w_out_sh):
    S = x.shape[0]
    tm = min(TM_PROJ, S)
    n_steps = S // tm
    assert n_steps >= 3

    def body(x_ref, sh_ref, sc_ref, w_ref, b_ref, wo_ref, u_ref, qkv_ref, f_ref, p_ref, g_ref, wo_all,
             wo_buf, *gather_sems):
        i = pl.program_id(0)
        xx, yy, cc = lax.axis_index("x"), lax.axis_index("y"), lax.axis_index("c")
        row_half = lambda which: (pl.ds(pl.multiple_of(which * (SHARD_OUT // 2), SHARD_OUT // 2), SHARD_OUT // 2),
                                  slice(None))
        start, forward, finish = _gather_stages((xx, yy, cc, 2 * xx + yy, (xx, yy, 1 - cc)), wo_ref, wo_buf,
                                                row_half, *gather_sems)
        pl.when(i == 0)(start)
        pl.when(i == n_steps // 2)(forward)

        @pl.when(i == n_steps - 1)
        def _():
            finish()
            wo_all[...] = wo_buf[...]

        u = (x_ref[...] * (1.0 + sc_ref[...]) + sh_ref[...]).astype(BF16)
        u_ref[...] = u
        qkv_ref[...] = (_dot_nt(u, w_ref[O_QKV:O_F, :]) + b_ref[:, O_QKV:O_F]).astype(BF16)
        f_ref[...] = _dot_nt(u, w_ref[O_F:O_P, :]) + b_ref[:, O_F:O_P]
        p_ref[...] = _dot_nt(u, w_ref[O_P:O_G, :]) + b_ref[:, O_P:O_G]
        g_ref[...] = _dot_nt(u, w_ref[O_G:D_PAD, :]) + b_ref[:, O_G:D_PAD]

    row = lambda w: pl.BlockSpec((tm, w), lambda i: (i, 0))
    full = lambda a: pl.BlockSpec(a.shape, lambda i: (0, 0))
    vm = pl.BlockSpec(memory_space=pltpu.VMEM)
    return pl.pallas_call(
        body, name="in_proj", grid=(n_steps,),
        out_shape=(jax.ShapeDtypeStruct((S, D), BF16), jax.ShapeDtypeStruct((S, 3 * D_ATT), BF16),
                   jax.ShapeDtypeStruct((S, 128), F32), jax.ShapeDtypeStruct((S, D_POOL), F32),
                   jax.ShapeDtypeStruct((S, D), F32), jax.ShapeDtypeStruct((N_CHIPS,) + w_out_sh.shape, BF16)),
        in_specs=[row(D), full(shift), full(scale), full(wt_pad), full(b_pad), vm],
        out_specs=(row(D), row(3 * D_ATT), row(128), row(D_POOL), row(D), vm),
        scratch_shapes=[pltpu.VMEM((N_CHIPS,) + w_out_sh.shape, BF16)] + _gather_scratch(),
        compiler_params=_params(dimension_semantics=("arbitrary",)),
    )(x, shift, scale, wt_pad, b_pad, w_out_sh)


def _forget_cumsum(f):
    S = f.shape[0]
    tm = min(T_ATT, S)

    def body(f_ref, out_ref, carry):
        @pl.when(pl.program_id(0) == 0)
        def _():
            carry[...] = jnp.zeros_like(carry)
        v = f_ref[...]
        logf = jnp.minimum(v, 0.0) - jnp.log(1.0 + jnp.exp(-jnp.abs(v)))
        r = lax.broadcasted_iota(jnp.int32, (tm, tm), 0)
        c = lax.broadcasted_iota(jnp.int32, (tm, tm), 1)
        tri = (r <= c).astype(F32)
        rows8 = logf.T[0:8, :]
        cum8 = jnp.dot(rows8, tri, preferred_element_type=F32, precision=lax.Precision.HIGHEST) + carry[...]
        out_ref[...] = jnp.concatenate([cum8, jnp.zeros((128 - 8, tm), F32)], axis=0).T
        last = lax.broadcasted_iota(jnp.int32, (1, tm), 1) == tm - 1
        carry[...] = jnp.sum(jnp.where(last, cum8, 0.0), axis=1, keepdims=True)

    return pl.pallas_call(
        body, name="forget_cumsum", grid=(S // tm,),
        out_shape=jax.ShapeDtypeStruct((S, 128), F32),
        in_specs=[pl.BlockSpec((tm, 128), lambda i: (i, 0))],
        out_specs=pl.BlockSpec((tm, 128), lambda i: (i, 0)),
        scratch_shapes=[pltpu.VMEM((8, 1), F32)],
        compiler_params=_params(dimension_semantics=("arbitrary",)),
    )(f)


def _split3(v):
    hi = v.astype(BF16)
    rest = v - hi.astype(F32)
    mid = rest.astype(BF16)
    lo = (rest - mid.astype(F32)).astype(BF16)
    return hi, mid, lo


def _attention_fwd(qkv, big_f):
    S = qkv.shape[0]
    T = min(T_ATT, S)
    n_t = S // T

    def body(q_ref, k_ref, v_ref, f_ref, o_ref, lse_ref, kaug_sc, vt_sc, m_sc, l_sc, acc_sc):
        hp = pl.program_id(0)
        i = pl.program_id(1)
        lane = lax.broadcasted_iota(jnp.int32, (1, 128), 1)
        sub = lax.broadcasted_iota(jnp.int32, (128, 1), 0)
        head_sel = (lane < HEAD_DIM, lane >= HEAD_DIM)
        head_sel_t = (sub < HEAD_DIM, sub >= HEAD_DIM)
        spare = (HEAD_DIM, 0)
        zero = jnp.zeros((), BF16)

        @pl.when(i == 0)
        def _():
            def prep(jt, carry):
                rows = pl.ds(pl.multiple_of(jt * T, T), T)
                k = k_ref[rows, :]
                ft = f_ref[rows, :]
                vt = v_ref[rows, :].astype(F32).T
                for h in range(2):
                    fh = jnp.sum(jnp.where(lane == 2 * hp + h, ft, 0.0), axis=1, keepdims=True)
                    hi, mid, lo = _split3(-fh)
                    b = spare[h]
                    bias = jnp.where(lane == b, hi, jnp.where(lane == b + 1, mid, jnp.where(lane == b + 2, lo, zero)))
                    kaug_sc[h, rows, :] = jnp.where(head_sel[h], k, bias)
                    vt_sc[h, jt] = jnp.where(head_sel_t[h], vt, 0.0).astype(BF16)
                return carry

            lax.fori_loop(0, n_t, prep, 0)

        q = q_ref[...]
        q_heads = []
        for h in range(2):
            ones = jnp.where((lane >= spare[h]) & (lane < spare[h] + 3), jnp.ones((), BF16), zero)
            q_heads.append(jnp.where(head_sel[h], q, ones))
        m_sc[...] = jnp.full((8, T), NEG, F32)
        l_sc[...] = jnp.zeros((8, T), F32)
        acc_sc[...] = jnp.zeros((128, T), F32)

        def update(j, k_lo, n_k, q_lo, masked):
            rows = pl.ds(pl.multiple_of(j * T + k_lo, n_k), n_k)
            n_q = T - q_lo
            alphas, pvs = [], []
            for h in range(2):
                s_t = _dot_nt(kaug_sc[h, rows, :], q_heads[h][q_lo:, :])
                if masked:
                    rr = lax.broadcasted_iota(jnp.int32, (n_k, n_q), 0) + k_lo
                    cc = lax.broadcasted_iota(jnp.int32, (n_k, n_q), 1) + q_lo
                    s_t = jnp.where(rr <= cc, s_t, NEG)
                m_prev = m_sc[h:h + 1, q_lo:]
                m_new = jnp.maximum(m_prev, jnp.max(s_t, axis=0, keepdims=True))
                alpha = jnp.exp(m_prev - m_new)
                p_t = jnp.exp(s_t - m_new)
                l_sc[h:h + 1, q_lo:] = alpha * l_sc[h:h + 1, q_lo:] + jnp.sum(p_t, axis=0, keepdims=True)
                m_sc[h:h + 1, q_lo:] = m_new
                alphas.append(alpha)
                pvs.append(_dot(vt_sc[h, j, :, k_lo:k_lo + n_k], p_t.astype(BF16)))
            acc_sc[:, q_lo:] = (acc_sc[:, q_lo:] * jnp.where(head_sel_t[0], alphas[0], alphas[1])
                                + (pvs[0] + pvs[1]))

        def two_off_diagonal(jj, carry):
            update(2 * jj, 0, T, 0, False)
            update(2 * jj + 1, 0, T, 0, False)
            return carry

        lax.fori_loop(0, i // 2, two_off_diagonal, 0)

        @pl.when(i % 2 == 1)
        def _():
            update(i - 1, 0, T, 0, False)

        update(i, 0, T, 0, True)
        l = l_sc[...]
        o_ref[...] = (acc_sc[...] / jnp.where(head_sel_t[0], l[0:1, :], l[1:2, :])).T
        is_head = lax.broadcasted_iota(jnp.int32, (8, 1), 0) < 2
        lse_ref[...] = jnp.where(is_head, m_sc[...] + jnp.log(jnp.where(is_head, l, 1.0)), 0.0)

    return pl.pallas_call(
        body, name="attention_fwd", grid=(N_PAIR, n_t),
        out_shape=(jax.ShapeDtypeStruct((S, D_ATT), F32), jax.ShapeDtypeStruct((N_PAIR, n_t, 8, T), F32)),
        in_specs=[pl.BlockSpec((T, 128), lambda hp, i: (i, hp)),
                  pl.BlockSpec((S, 128), lambda hp, i: (0, N_PAIR + hp)),
                  pl.BlockSpec((S, 128), lambda hp, i: (0, 2 * N_PAIR + hp)),
                  pl.BlockSpec((S, 128), lambda hp, i: (0, 0))],
        out_specs=(pl.BlockSpec((T, 128), lambda hp, i: (i, hp)),
                   pl.BlockSpec((None, None, 8, T), lambda hp, i: (hp, i, 0, 0))),
        scratch_shapes=[pltpu.VMEM((2, S, 128), BF16), pltpu.VMEM((2, n_t, 128, T), BF16),
                        pltpu.VMEM((8, T), F32), pltpu.VMEM((8, T), F32), pltpu.VMEM((128, T), F32)],
        compiler_params=_params(dimension_semantics=("arbitrary", "arbitrary")),
    )(qkv, qkv, qkv, big_f)


def _attention_bwd(qkv, datt, att, lse, big_f, gw_out4, sc_out):
    S = qkv.shape[0]
    T = min(T_ATT, S)
    n_t = S // T
    n_steps = N_PAIR * n_t
    marks = (0, n_steps // 8, n_steps // 2, n_steps // 2 + n_steps // 8)

    def body(q_ref, do_ref, o_ref, lse_ref, k_ref, v_ref, fk_ref, gout_ref, scout_ref,
             dq_ref, dk_ref, dv_ref, cs_ref, dfk_ref, dfq_ref, oout_ref, stat_sc, dqt_sc, qaug_sc,
             out_buf, *red_bufs):
        hp = pl.program_id(0)
        j = pl.program_id(1)
        x, y, cc = lax.axis_index("x"), lax.axis_index("y"), lax.axis_index("c")
        plan = _scatter_stages((x, y, cc, 2 * x + y, (x, y, 1 - cc)), gout_ref, scout_ref, out_buf, *red_bufs)
        step = hp * n_t + j
        for n, mark in enumerate(marks):
            @pl.when(step == mark)
            def _(n=n):
                if n > 0:
                    plan[n - 1][1]()
                if n < 3:
                    plan[n][0]()
                else:
                    oout_ref[...] = out_buf[...]

        lane = lax.broadcasted_iota(jnp.int32, (1, 128), 1)
        sub = lax.broadcasted_iota(jnp.int32, (128, 1), 0)
        head_sel = (lane < HEAD_DIM, lane >= HEAD_DIM)
        head_sel_t = (sub < HEAD_DIM, sub >= HEAD_DIM)
        spare = (HEAD_DIM, 0)
        zero = jnp.zeros((), BF16)
        one = jnp.ones((), BF16)

        def bias_lanes(first, pieces):
            hi, mid, lo = pieces
            return lambda rest: jnp.where(lane == first, hi, jnp.where(lane == first + 1, mid,
                                                                        jnp.where(lane == first + 2, lo, rest)))

        @pl.when(j == 0)
        def _():
            dqt_sc[...] = jnp.zeros_like(dqt_sc)
            cs_ref[...] = jnp.zeros_like(cs_ref)
            dfq_ref[...] = jnp.zeros_like(dfq_ref)

            def prep(i, carry):
                rows = pl.ds(pl.multiple_of(i * T, T), T)
                q = q_ref[rows, :]
                do = do_ref[rows, :]
                prod = o_ref[rows, :] * do.astype(F32)
                d_a = jnp.sum(jnp.where(head_sel[0], prod, 0.0), axis=1, keepdims=True)
                d_b = jnp.sum(jnp.where(head_sel[0], 0.0, prod), axis=1, keepdims=True)
                delta_t = jnp.where(head_sel[0], d_a, d_b).T
                stat_sc[i, 0:1, :] = delta_t[0:1, :]
                stat_sc[i, 1:2, :] = delta_t[HEAD_DIM:HEAD_DIM + 1, :]
                lse = lse_ref[i]
                lse_cols = jnp.where(head_sel_t[0], lse[0:1, :], lse[1:2, :]).T
                for h in range(2):
                    neg_lse = -lse_cols[:, h * HEAD_DIM:h * HEAD_DIM + 1]
                    ones = jnp.where((lane >= spare[h]) & (lane < spare[h] + 3), one, zero)
                    qaug_sc[h, rows, :] = jnp.where(head_sel[h], q, bias_lanes(spare[h] + 3, _split3(neg_lse))(ones))
                return carry

            lax.fori_loop(0, n_t, prep, 0)

        k = k_ref[...]
        v = v_ref[...]
        fk = fk_ref[...]
        kt = k.astype(F32).T
        heads = []
        for h in range(2):
            fkh = jnp.sum(jnp.where(lane == 2 * hp + h, fk, 0.0), axis=1, keepdims=True)
            ones = jnp.where((lane >= spare[h] + 3) & (lane < spare[h] + 6), one, zero)
            kaug = jnp.where(head_sel[h], k, bias_lanes(spare[h], _split3(-fkh))(ones))
            heads.append((kaug, jnp.where(head_sel[h], v, zero), jnp.where(head_sel_t[h], kt, 0.0).astype(BF16)))

        def block(i, k_lo, n_k, q_lo, masked):
            n_q = T - q_lo
            rows = pl.ds(pl.multiple_of(i * T + q_lo, n_q), n_q)
            q = q_ref[rows, :]
            do = do_ref[rows, :]
            stat = stat_sc[i]
            dk = jnp.zeros((n_k, 128), F32)
            dv = jnp.zeros((n_k, 128), F32)
            dqt = jnp.zeros((128, n_q), F32)
            dfs = []
            for h in range(2):
                kaug, vh, kth = heads[h]
                arg = _dot_nt(kaug[k_lo:k_lo + n_k, :], qaug_sc[h, rows, :])
                if masked:
                    rr = lax.broadcasted_iota(jnp.int32, (n_k, n_q), 0) + k_lo
                    cc = lax.broadcasted_iota(jnp.int32, (n_k, n_q), 1) + q_lo
                    arg = jnp.where(rr <= cc, arg, NEG)
                p_t = jnp.exp(arg)
                ds_t = p_t * (_dot_nt(vh[k_lo:k_lo + n_k, :], do) - stat[h:h + 1, q_lo:])
                ds_bf = ds_t.astype(BF16)
                dv = dv + _dot(p_t.astype(BF16), jnp.where(head_sel[h], do, zero))
                dk = dk + _dot(ds_bf, jnp.where(head_sel[h], q, zero))
                dqt = dqt + _dot(kth[:, k_lo:k_lo + n_k], ds_bf)
                dfs.append(jnp.sum(ds_t, axis=1, keepdims=True))
                dfq_ref[i, h:h + 1, q_lo:] += _colsum(ds_t)
            dqt_sc[i, :, q_lo:] += dqt
            return dk, dv, dfs[0], dfs[1]

        def off_diagonal(i, acc):
            return tuple(a + b for a, b in zip(acc, block(i, 0, T, 0, False)))

        half = T // 2
        early = block(j, 0, half, 0, True)
        late = block(j, half, half, half, True)
        acc1 = tuple(jnp.concatenate([a, b], axis=0) for a, b in zip(early, late))
        n_off = n_t - 1 - j
        acc2 = lax.fori_loop(0, n_off // 2,
                             lambda ii, a: off_diagonal(j + 2 + 2 * ii, off_diagonal(j + 1 + 2 * ii, a)), acc1)
        dk_acc, dv_acc, dfa, dfb = lax.fori_loop(0, n_off % 2, lambda _, a: off_diagonal(n_t - 1, a), acc2)
        dk_ref[...] = dk_acc.astype(BF16)
        dv_ref[...] = dv_acc.astype(BF16)
        dfk_ref[...] = -jnp.where(lane == 0, dfa, jnp.where(lane == 1, dfb, 0.0))
        cs_ref[:, 128:256] = cs_ref[:, 128:256] + _colsum(dk_acc)
        cs_ref[:, 256:384] = cs_ref[:, 256:384] + _colsum(dv_acc)

        @pl.when(j == n_t - 1)
        def _():
            def finish(i, tot):
                dq = dqt_sc[i].T
                dq_ref[pl.ds(pl.multiple_of(i * T, T), T), :] = dq.astype(BF16)
                return tot + _colsum(dq)

            cs_ref[:, 0:128] = lax.fori_loop(0, n_t, finish, jnp.zeros((1, 128), F32))

    pair_rows = lambda hp, j: (hp, 0, 0)
    vm = pl.BlockSpec(memory_space=pltpu.VMEM)
    _, r_out, c_out = gw_out4.shape
    return pl.pallas_call(
        body, name="attention_bwd", grid=(N_PAIR, n_t),
        out_shape=(jax.ShapeDtypeStruct((S, D_ATT), BF16), jax.ShapeDtypeStruct((S, D_ATT), BF16),
                   jax.ShapeDtypeStruct((S, D_ATT), BF16), jax.ShapeDtypeStruct((N_PAIR, 1, 384), F32),
                   jax.ShapeDtypeStruct((N_PAIR, S, 128), F32),
                   jax.ShapeDtypeStruct((N_PAIR, n_t, 8, T), F32),
                   jax.ShapeDtypeStruct((r_out, c_out), F32)),
        in_specs=[pl.BlockSpec((S, 128), lambda hp, j: (0, hp)),
                  pl.BlockSpec((S, 128), lambda hp, j: (0, hp)),
                  pl.BlockSpec((S, 128), lambda hp, j: (0, hp)),
                  pl.BlockSpec((None, n_t, 8, T), lambda hp, j: (hp, 0, 0, 0)),
                  pl.BlockSpec((T, 128), lambda hp, j: (j, N_PAIR + hp)),
                  pl.BlockSpec((T, 128), lambda hp, j: (j, 2 * N_PAIR + hp)),
                  pl.BlockSpec((T, 128), lambda hp, j: (j, 0)),
                  vm, vm],
        out_specs=(pl.BlockSpec((S, 128), lambda hp, j: (0, hp)),
                   pl.BlockSpec((T, 128), lambda hp, j: (j, hp)),
                   pl.BlockSpec((T, 128), lambda hp, j: (j, hp)),
                   pl.BlockSpec((None, 1, 384), pair_rows),
                   pl.BlockSpec((None, T, 128), lambda hp, j: (hp, j, 0)),
                   pl.BlockSpec((None, n_t, 8, T), lambda hp, j: (hp, 0, 0, 0)),
                   vm),
        scratch_shapes=[pltpu.VMEM((n_t, 8, T), F32), pltpu.VMEM((n_t, 128, T), F32),
                        pltpu.VMEM((2, S, 128), BF16), pltpu.VMEM((r_out, c_out), F32)]
        + _scatter_scratch(r_out, c_out),
        compiler_params=_params(dimension_semantics=("arbitrary", "arbitrary")),
    )(qkv, datt, att, lse, qkv, qkv, big_f, gw_out4, sc_out)


def _window_counts(first_row, n_rows, window):
    t = lax.broadcasted_iota(jnp.int32, (n_rows, 1), 0) + first_row
    return jnp.minimum((t + 1).astype(F32), float(window))


def _middle(x, tgt, att, g, p, gate, w_mix, b_mix, pool_scale, w_out, b_out, ln_g, ln_b):
    S = x.shape[0]
    tm = min(TM_MID, S)
    halo_blocks = tm // POOL_HALO

    def body(x_ref, t_ref, att_ref, g_ref, p_ref, ph_ref, gate_ref, wm_ref, bm_ref, ps_ref, wo_ref, bo_ref,
             lg_ref, lb_ref,
             dh_ref, datt_ref, dg_ref, dpl_ref, gwo_ref, gwm_ref, vec_ref, loss_ref):
        i = pl.program_id(0)

        @pl.when(i == 0)
        def _():
            gwo_ref[...] = jnp.zeros_like(gwo_ref)
            gwm_ref[...] = jnp.zeros_like(gwm_ref)
            vec_ref[...] = jnp.zeros_like(vec_ref)
            loss_ref[...] = jnp.zeros_like(loss_ref)

        pc = p_ref[...]
        halo = jnp.where(i > 0, ph_ref[...], 0.0)
        pe = jnp.concatenate([halo, pc], axis=0)
        pooled_parts = []
        for gi, w in enumerate(POOL_WINDOWS):
            cur = pe[:, gi * POOL_GROUP:(gi + 1) * POOL_GROUP]
            span = 1
            while span < w:
                cur = cur + pltpu.roll(cur, span, 0)
                span *= 2
            wsum = cur[POOL_HALO:, :]
            mean = wsum / _window_counts(i * tm, tm, w)
            pooled_parts.append(mean - pc[:, gi * POOL_GROUP:(gi + 1) * POOL_GROUP])
        pooled_bf =[v.astype(BF16) for v in pooled_parts]
        mixed = jnp.concatenate([_dot(pooled_bf[gi], wm_ref[gi]) for gi in range(4)], axis=1) + bm_ref[...]
        ps = ps_ref[...]
        pool_out = mixed * ps
        gv = g_ref[...]
        sig = _sigmoid(gv)
        silu = gv * sig
        att = att_ref[...]
        y = jnp.concatenate([att * silu[:, :D_ATT], pool_out * silu[:, D_ATT:]], axis=1)
        y_bf = y.astype(BF16)
        wo = wo_ref[...]
        yo = _dot(y_bf, wo) + bo_ref[...]
        gate = gate_ref[...]
        h = ALPHA * x_ref[...] + gate * yo
        mu = jnp.mean(h, axis=1, keepdims=True)
        hc = h - mu
        var = jnp.mean(hc * hc, axis=1, keepdims=True)
        rstd = lax.rsqrt(var + LN_EPS)
        yhat = hc * rstd
        lg = lg_ref[...]
        out = yhat * lg + lb_ref[...]
        err = out - t_ref[...]
        loss_ref[...] += 0.5 * jnp.sum(jnp.mean(err * err, axis=1, keepdims=True), axis=0, keepdims=True)

        dout = err * (1.0 / D)
        g_ln_b = _colsum(dout)
        g_ln_g = _colsum(dout * yhat)
        dyh = dout * lg
        dh = rstd * (dyh - jnp.mean(dyh, axis=1, keepdims=True)
                     - yhat * jnp.mean(dyh * yhat, axis=1, keepdims=True))
        dh_ref[...] = dh
        d_gate = _colsum(dh * yo)
        dyo = gate * dh
        g_b_out = _colsum(dyo)
        dyo_bf = dyo.astype(BF16)
        gwo_ref[...] += _dot_tn(y_bf, dyo_bf)
        dy = _dot_nt(dyo_bf, wo)
        dsilu = sig * (1.0 + gv * (1.0 - sig))
        dy_a = dy[:, :D_ATT]
        dy_p = dy[:, D_ATT:]
        datt_ref[...] = (dy_a * silu[:, :D_ATT]).astype(BF16)
        dpo = dy_p * silu[:, D_ATT:]
        dg = jnp.concatenate([dy_a * att * dsilu[:, :D_ATT], dy_p * pool_out * dsilu[:, D_ATT:]], axis=1)
        dg_ref[...] = dg.astype(BF16)
        g_dg = _colsum(dg)
        g_ps = _colsum(dpo * mixed)
        dmixed = dpo * ps
        g_bm = _colsum(dmixed)
        dmixed_bf = dmixed.astype(BF16)
        dpl = []
        for gi in range(4):
            dm = dmixed_bf[:, gi * POOL_GROUP:(gi + 1) * POOL_GROUP]
            gwm_ref[gi] += _dot_tn(pooled_bf[gi], dm)
            dpl.append(_dot_nt(dm, wm_ref[gi]))
        dpl_ref[...] = jnp.concatenate(dpl, axis=1)
        vec_ref[0:1, :] += g_ln_g
        vec_ref[1:2, :] += g_ln_b
        vec_ref[2:3, :] += d_gate
        vec_ref[3:4, :] += g_b_out
        vec_ref[4:5, :] += g_dg
        vec_ref[5:6, 0:D_POOL] += g_ps
        vec_ref[6:7, 0:D_POOL] += g_bm

    row = lambda w: pl.BlockSpec((tm, w), lambda i: (i, 0))
    full2 = lambda a: pl.BlockSpec(a.shape, lambda i: (0, 0))
    full3 = lambda a: pl.BlockSpec(a.shape, lambda i: (0, 0, 0))
    return pl.pallas_call(
        body, name="middle", grid=(S // tm,),
        out_shape=(jax.ShapeDtypeStruct((S, D), F32),
                   jax.ShapeDtypeStruct((S, D_ATT), BF16),
                   jax.ShapeDtypeStruct((S, D), BF16),
                   jax.ShapeDtypeStruct((S, D_POOL), F32),
                   jax.ShapeDtypeStruct((D, D), F32),
                   jax.ShapeDtypeStruct((4, POOL_GROUP, POOL_GROUP), F32),
                   jax.ShapeDtypeStruct((8, D), F32),
                   jax.ShapeDtypeStruct((1, 1), F32)),
        in_specs=[row(D), row(D), row(D_ATT), row(D), row(D_POOL),
                  pl.BlockSpec((POOL_HALO, D_POOL), lambda i: (jnp.maximum(i * halo_blocks - 1, 0), 0)),
                  full2(gate), full3(w_mix), full2(b_mix), full2(pool_scale), full2(w_out), full2(b_out),
                  full2(ln_g), full2(ln_b)],
        out_specs=(row(D), row(D_ATT), row(D), row(D_POOL),
                   pl.BlockSpec((D, D), lambda i: (0, 0)),
                   pl.BlockSpec((4, POOL_GROUP, POOL_GROUP), lambda i: (0, 0, 0)),
                   pl.BlockSpec((8, D), lambda i: (0, 0)),
                   pl.BlockSpec((1, 1), lambda i: (0, 0))),
        compiler_params=_params(dimension_semantics=("arbitrary",)),
    )(x, tgt, att, g, p, p, gate, w_mix, b_mix, pool_scale, w_out, b_out, ln_g, ln_b)


def _tail(dpl, dfk, dfq, f):
    S = dpl.shape[0]
    tm = min(T_ATT, S)
    n_t = S // tm
    halo_blocks = tm // POOL_HALO
    last_halo = S // POOL_HALO - 1

    def body(d_ref, dn_ref, dfk_ref, dfq_ref, f_ref, dp_ref, df_ref, cs_ref, carry):
        s = pl.program_id(0)
        i = n_t - 1 - s

        @pl.when(s == 0)
        def _():
            carry[...] = jnp.zeros_like(carry)
            cs_ref[...] = jnp.zeros_like(cs_ref)

        dc = d_ref[...]
        nxt = jnp.where(s > 0, dn_ref[...], 0.0)
        de = jnp.concatenate([dc, nxt], axis=0)
        n_e = tm + POOL_HALO
        parts = []
        for gi, w in enumerate(POOL_WINDOWS):
            cur = de[:, gi * POOL_GROUP:(gi + 1) * POOL_GROUP] / _window_counts(i * tm, n_e, w)
            span = 1
            while span < w:
                cur = cur + pltpu.roll(cur, n_e - span, 0)
                span *= 2
            parts.append(cur[:tm, :] - dc[:, gi * POOL_GROUP:(gi + 1) * POOL_GROUP])
        dp = jnp.concatenate(parts, axis=1)
        dp_ref[...] = dp.astype(BF16)
        cs_ref[0:1, :] += _colsum(dp)

        r = lax.broadcasted_iota(jnp.int32, (tm, tm), 0)
        c = lax.broadcasted_iota(jnp.int32, (tm, tm), 1)
        tri = (r >= c).astype(F32)
        k_cols = dfk_ref[0]
        rows8 = dfq_ref[0]
        for hp in range(1, N_PAIR):
            k_cols = k_cols + pltpu.roll(dfk_ref[hp], 2 * hp, 1)
            rows8 = rows8 + pltpu.roll(dfq_ref[hp], 2 * hp, 0)
        rows8 = rows8 + k_cols.T[0:8, :]
        dlogf8 = jnp.dot(rows8, tri, preferred_element_type=F32, precision=lax.Precision.HIGHEST) + carry[...]
        first = lax.broadcasted_iota(jnp.int32, (1, tm), 1) == 0
        carry[...] = jnp.sum(jnp.where(first, dlogf8, 0.0), axis=1, keepdims=True)
        dlogf = jnp.concatenate([dlogf8, jnp.zeros((128 - 8, tm), F32)], axis=0).T
        df = dlogf * _sigmoid(-f_ref[...])
        df_ref[...] = df.astype(BF16)
        cs_ref[1:2, 0:128] += _colsum(df)

    rev = lambda w: pl.BlockSpec((tm, w), lambda s: (n_t - 1 - s, 0))
    return pl.pallas_call(
        body, name="tail", grid=(n_t,),
        out_shape=(jax.ShapeDtypeStruct((S, D_POOL), BF16), jax.ShapeDtypeStruct((S, 128), BF16),
                   jax.ShapeDtypeStruct((8, D_POOL), F32)),
        in_specs=[rev(D_POOL),
                  pl.BlockSpec((POOL_HALO, D_POOL),
                               lambda s: (jnp.minimum((n_t - s) * halo_blocks, last_halo), 0)),
                  pl.BlockSpec((N_PAIR, tm, 128), lambda s: (0, n_t - 1 - s, 0)),
                  pl.BlockSpec((N_PAIR, None, 8, tm), lambda s: (0, n_t - 1 - s, 0, 0)),
                  rev(128)],
        out_specs=(rev(D_POOL), rev(128), pl.BlockSpec((8, D_POOL), lambda s: (0, 0))),
        scratch_shapes=[pltpu.VMEM((8, 1), F32)],
        compiler_params=_params(dimension_semantics=("arbitrary",)),
    )(dpl, dpl, dfk, dfq, f)


PIECES = ((O_QKV, D_ATT), (O_QKV + D_ATT, D_ATT), (O_QKV + 2 * D_ATT, D_ATT), (O_F, 128), (O_P, D_POOL), (O_G, D))


def _grad_w_in(u, pieces):
    S = u.shape[0]
    tm = min(TM_GW, S)
    n_t = S // tm

    def body(u_ref, *rest):
        piece_refs, out_ref, acc, sem = rest[:6], rest[6], rest[7], rest[8]
        i = pl.program_id(0)

        @pl.when(i == 0)
        def _():
            acc[...] = jnp.zeros_like(acc)

        u_t = u_ref[...]
        for (off, w), ref in zip(PIECES, piece_refs):
            acc[:, off:off + w] += _dot_tn(u_t, ref[...])

        @pl.when(i == n_t - 1)
        def _():
            cp = pltpu.make_async_copy(acc, out_ref, sem)
            cp.start()
            cp.wait()

    return pl.pallas_call(
        body, name="grad_w_in", grid=(n_t,),
        out_shape=jax.ShapeDtypeStruct((D, D_PAD), F32),
        in_specs=[pl.BlockSpec((tm, D), lambda i: (i, 0))]
        + [pl.BlockSpec((tm, w), lambda i: (i, 0)) for _, w in PIECES],
        out_specs=pl.BlockSpec(memory_space=pl.ANY),
        scratch_shapes=[pltpu.VMEM((D, D_PAD), F32), pltpu.SemaphoreType.DMA],
        compiler_params=_params(dimension_semantics=("arbitrary",)),
    )(u, *pieces)


def _grad_x(pieces, wt_pad, dh, x, scale):
    S = x.shape[0]
    tm = min(TM_DU, S)

    def body(*refs):
        piece_refs = refs[:6]
        w_ref, dh_ref, x_ref, sc_ref, gx_ref, vec_ref = refs[6:]

        @pl.when(pl.program_id(0) == 0)
        def _():
            vec_ref[...] = jnp.zeros_like(vec_ref)

        du = jnp.zeros((tm, D), F32)
        for (off, w), ref in zip(PIECES, piece_refs):
            du = du + _dot(ref[...], w_ref[off:off + w, :])
        xv = x_ref[...]
        gx_ref[...] = ALPHA * dh_ref[...] + du * (1.0 + sc_ref[...])
        vec_ref[0:1, :] += _colsum(du)
        vec_ref[1:2, :] += _colsum(du * xv)

    row = lambda w: pl.BlockSpec((tm, w), lambda i: (i, 0))
    return pl.pallas_call(
        body, name="grad_x", grid=(S // tm,),
        out_shape=(jax.ShapeDtypeStruct((S, D), F32), jax.ShapeDtypeStruct((8, D), F32)),
        in_specs=[row(w) for _, w in PIECES]
        + [pl.BlockSpec(wt_pad.shape, lambda i: (0, 0)), row(D), row(D), pl.BlockSpec((1, D), lambda i: (0, 0))],
        out_specs=(row(D), pl.BlockSpec((8, D), lambda i: (0, 0))),
        compiler_params=_params(dimension_semantics=("arbitrary",)),
    )(*pieces, wt_pad, dh, x, scale)


def _adamw_math(w, g, m, v):
    m = ADAM_B1 * m + (1.0 - ADAM_B1) * g
    v = ADAM_B2 * v + (1.0 - ADAM_B2) * (g * g)
    m_hat = m / (1.0 - ADAM_B1 ** ADAM_STEP)
    v_hat = v / (1.0 - ADAM_B2 ** ADAM_STEP)
    delta = -ADAM_LR * (m_hat / (jnp.sqrt(v_hat) + ADAM_EPS) + ADAM_WD * w)
    return delta, m, v


def _adamw(groups, n_steps):
    n = len(groups)

    def body(*refs):
        ins, outs = refs[:4 * n], refs[4 * n:]
        for t in range(n):
            w, g, m, v = (r[...] for r in ins[4 * t:4 * t + 4])
            d, m2, v2 = _adamw_math(w, g, m, v)
            outs[4 * t][...] = d
            outs[4 * t + 1][...] = m2
            outs[4 * t + 2][...] = v2
            outs[4 * t + 3][...] = g

    in_specs, out_specs, out_shape, args = [], [], [], []
    for (w, g, m, v) in groups:
        rest = w.shape[1:]
        spec = pl.BlockSpec((w.shape[0] // n_steps,) + rest, lambda i, nd=len(rest): (i,) + (0,) * nd)
        in_specs += [spec] * 4
        out_specs += [spec] * 4
        out_shape += [jax.ShapeDtypeStruct(w.shape, F32)] * 4
        args += [w, g, m, v]
    return pl.pallas_call(
        body, name="adamw_%d_%d" % (n, n_steps), grid=(n_steps,),
        out_shape=tuple(out_shape), in_specs=in_specs, out_specs=tuple(out_specs),
        compiler_params=_params(dimension_semantics=("arbitrary",)),
    )(*args)


def _adamw_small(small_sum, g_b_ada, params):
    n = len(params)

    def body(gs_ref, gba_ref, *refs):
        ins, outs = refs[:3 * n], refs[3 * n:]
        for t, (name, w0, _, _) in enumerate(params):
            w_ref, m_ref, v_ref = ins[3 * t:3 * t + 3]
            first = SMALL_SEGS[name][0] if name in SMALL_SEGS else None
            if w0.shape[0] > 1:
                pieces = [((slice(None), slice(None)), gs_ref[first:first + w0.shape[0], :])]
            else:
                pieces = []
                for r in range(-(-w0.shape[1] // 128)):
                    lanes = slice(128 * r, min(128 * r + 128, w0.shape[1]))
                    g = gba_ref[0:1, lanes] if first is None else gs_ref[first + r:first + r + 1, 0:lanes.stop - lanes.start]
                    pieces.append(((slice(0, 1), lanes), g))
            for where, g in pieces:
                d, m2, v2 = _adamw_math(w_ref[where], g, m_ref[where], v_ref[where])
                for ref, val in zip(outs[4 * t:4 * t + 4], (g, d, m2, v2)):
                    ref[where] = val

    vm = pl.BlockSpec(memory_space=pltpu.VMEM)
    args = [small_sum, g_b_ada]
    out_shape = []
    for _, w, m, v in params:
        args += [w, m, v]
        out_shape += [jax.ShapeDtypeStruct(w.shape, F32)] * 4
    return pl.pallas_call(
        body, name="adamw_small",
        out_shape=tuple(out_shape), in_specs=[vm] * len(args), out_specs=(vm,) * len(out_shape),
        compiler_params=_params(),
    )(*args)


def _pack_small(parts):
    rows = []
    used = 0
    for name, (first, n_rows) in SMALL_SEGS.items():
        if first > used:
            rows.append(jnp.zeros((first - used, 128), F32))
        flat = parts[name].reshape(-1)
        flat = jnp.pad(flat, (0, n_rows * 128 - flat.shape[0]))
        rows.append(flat.reshape(n_rows, 128))
        used = first + n_rows
    rows.append(jnp.zeros((SMALL_ROWS - used, 128), F32))
    return jnp.concatenate(rows, axis=0)


def _unpack_small(buf, name, shape):
    first, n_rows = SMALL_SEGS[name]
    n = int(np.prod(shape))
    return buf[first:first + n_rows].reshape(-1)[:n].reshape(shape)


def _pad_in(v):
    r = v.shape[0]
    z = jnp.zeros((r, O_P - O_F - N_HEADS), v.dtype)
    return jnp.concatenate([v[:, :3 * D_ATT + N_HEADS], z, v[:, 3 * D_ATT + N_HEADS:]], axis=1)


def _unpad_in(v):
    return jnp.concatenate([v[:, :O_F + N_HEADS], v[:, O_P:]], axis=1)


def _shards_in(v):
    gap = O_P - (O_F + N_HEADS)
    parts = []
    for a in range(N_CHIPS):
        lo, hi = a * SHARD_IN, (a + 1) * SHARD_IN
        cut = O_F + N_HEADS
        if hi <= cut:
            parts.append(v[:, lo:hi])
        elif lo >= cut:
            parts.append(v[:, lo + gap:hi + gap])
        else:
            parts.append(jnp.concatenate([v[:, lo:cut], v[:, cut + gap:hi + gap]], axis=1))
    return jnp.stack(parts, axis=0)


def kernel(x, c, w_ada, b_ada, w_in, b_in, w_pool_mix, b_pool_mix, pool_scale, w_out, b_out, ln_g, ln_b, loss_target, m_w_ada, m_b_ada, m_w_in, m_b_in, m_w_pool_mix, m_b_pool_mix, m_pool_scale, m_w_out, m_b_out, m_ln_g, m_ln_b, v_w_ada, v_b_ada, v_w_in, v_b_in, v_w_pool_mix, v_b_pool_mix, v_pool_scale, v_w_out, v_b_out, v_ln_g, v_ln_b):
    S = x.shape[1]
    T = min(T_ATT, S)
    n_t = S // T
    x2 = x[0]
    tgt = loss_target[0]
    q_scale = jnp.concatenate([jnp.full((1, D_ATT), Q_SCALE, F32), jnp.ones((1, D_PAD - D_ATT), F32)], axis=1)

    to_cols = lambda a: jnp.transpose(a, (2, 0, 1))
    from_cols = lambda a: jnp.transpose(a, (1, 2, 0))
    c_all, ada4, wt_pad = _gather_and_ada(
        c, w_ada[0], b_ada.reshape(4, 1, SHARD_ADA), to_cols(w_in))
    ada = ada4[:, 0, :].reshape(1, 3 * D)
    shift, scale, gate = ada[:, :D], ada[:, D:2 * D], ada[:, 2 * D:]
    b_pad = _pad_in(b_in) * q_scale
    w_mix_bf = w_pool_mix[0].astype(BF16)

    u, qkv, f, p, g, w_out_all = _in_proj(x2, shift, scale, wt_pad, b_pad, w_out[0].astype(BF16))
    w_out_full = w_out_all.reshape(D, D)
    big_f = _forget_cumsum(f)
    att, lse = _attention_fwd(qkv, big_f)

    dh, datt, dg, dpl, gw_out, gw_mix, vec, loss_part = _middle(
        x2, tgt, att, g, p, gate, w_mix_bf, b_pool_mix.reshape(1, D_POOL), pool_scale, w_out_full, b_out, ln_g, ln_b)
    dq, dk, dv, cs_att, dfk, dfq, g_w_out = _attention_bwd(
        qkv, datt, att, lse, big_f, gw_out.reshape(N_CHIPS, SHARD_OUT, D), jnp.ones((N_CHIPS, 1, D), F32))
    dp, df, cs_tail = _tail(dpl, dfk, dfq, f)
    pieces = (dq, dk, dv, df, dp, dg)
    gw_pad = _grad_w_in(u, pieces)
    grad_x, vec_x = _grad_x(pieces, wt_pad, dh, x2, scale)

    cs_qkv = jnp.transpose(cs_att.reshape(N_PAIR, 3, 128), (1, 0, 2)).reshape(1, 3 * D_ATT)
    gb_pad = jnp.concatenate([cs_qkv, cs_tail[1:2, 0:128], cs_tail[0:1, :], vec[4:5, :]], axis=1) * q_scale
    dada = jnp.concatenate([vec_x[0:1, :], vec_x[1:2, :], vec[2:3, :]], axis=1)
    small = _pack_small({
        "b_in": _unpad_in(gb_pad), "w_pool_mix": gw_mix, "b_pool_mix": vec[6:7, :D_POOL],
        "pool_scale": vec[5:6, :D_POOL], "b_out": vec[3:4, :], "ln_g": vec[0:1, :], "ln_b": vec[1:2, :],
        "loss": loss_part})

    g_w_in, small_sum, g_w_ada, g_b_ada = _reduce_all(
        gw_pad, _shards_in(q_scale), small, dada, c_all)
    loss = _unpack_small(small_sum, "loss", (1,))[0]

    big = _adamw([(w_ada[0], g_w_ada, m_w_ada[0], v_w_ada[0]),
                  (w_out[0], g_w_out, m_w_out[0], v_w_out[0])], 4)
    big_in = _adamw([(to_cols(w_in), g_w_in, to_cols(m_w_in), to_cols(v_w_in))], 5)
    tiles = lambda a: a.reshape(4 * POOL_GROUP, POOL_GROUP)
    flat = lambda a: a.reshape(1, D_POOL)
    small_params = [("b_ada", b_ada, m_b_ada, v_b_ada), ("b_in", b_in, m_b_in, v_b_in),
                    ("w_pool_mix", tiles(w_pool_mix), tiles(m_w_pool_mix), tiles(v_w_pool_mix)),
                    ("b_pool_mix", flat(b_pool_mix), flat(m_b_pool_mix), flat(v_b_pool_mix)),
                    ("pool_scale", pool_scale, m_pool_scale, v_pool_scale), ("b_out", b_out, m_b_out, v_b_out),
                    ("ln_g", ln_g, m_ln_g, v_ln_g), ("ln_b", ln_b, m_ln_b, v_ln_b)]
    sm = _adamw_small(small_sum, g_b_ada, small_params)
    sm_idx = {p[0]: n for n, p in enumerate(small_params)}
    shapes = {"w_pool_mix": (1, 4, POOL_GROUP, POOL_GROUP), "b_pool_mix": (1, 4, POOL_GROUP)}

    names = ["w_ada", "b_ada", "w_in", "b_in", "w_pool_mix", "b_pool_mix", "pool_scale", "w_out", "b_out",
             "ln_g", "ln_b"]
    big_idx = {"w_ada": 0, "w_out": 1}

    def leaf(kind, name):
        if name == "w_in":
            return from_cols(big_in[(kind - 1) % 4])
        if name in big_idx:
            return big[4 * big_idx[name] + (kind - 1) % 4][None]
        val = sm[4 * sm_idx[name] + kind]
        return val.reshape(shapes[name]) if name in shapes else val

    outs = [loss, grad_x[None]]
    for kind in range(4):
        outs += [leaf(kind, n) for n in names]
    return tuple(outs)
```

```python
import functools

import numpy as np
import jax
import jax.numpy as jnp
from jax import lax
from jax.experimental import pallas as pl
from jax.experimental.pallas import tpu as pltpu

F32 = jnp.float32
BF16 = jnp.bfloat16
MESH = pl.DeviceIdType.MESH

D = 1024
D_ATT = 512
D_POOL = 512
N_HEADS = 8
HEAD_DIM = 64
N_PAIR = N_HEADS // 2
POOL_WINDOWS = (2, 4, 8, 16)
POOL_GROUP = 128
POOL_HALO = 16
LN_EPS = 1e-5
ALPHA = 2.0 ** 0.25
D_IN = 3 * D_ATT + N_HEADS + D_POOL + D_ATT + D_POOL
N_CHIPS = 4
SHARD_IN = D_IN // N_CHIPS
SHARD_ADA = 3 * D // N_CHIPS
SHARD_OUT = D // N_CHIPS

O_QKV, O_F, O_P, O_G, D_PAD = 0, 1536, 1664, 2176, 3200
Q_SCALE = HEAD_DIM ** -0.5

ADAM_LR, ADAM_B1, ADAM_B2, ADAM_EPS, ADAM_WD, ADAM_STEP = 0.001, 0.9, 0.999, 1e-08, 0.01, 10

NEG = -1e30

VMEM_LIMIT = 56 * 1024 * 1024

TM_PROJ = 512
T_ATT = 512
TM_MID = 512
TM_GW = 1024
TM_DU = 512

REL7 = [(0, 0, 1), (0, 1, 0), (0, 1, 1), (1, 0, 0), (1, 0, 1), (1, 1, 0), (1, 1, 1)]
REL3 = [(0, 1), (1, 0), (1, 1)]

SMALL_SEGS = {}
_row = 0
for _name, _n in (("b_in", D_IN), ("w_pool_mix", 65536), ("b_pool_mix", 512), ("pool_scale", 512),
                  ("b_out", 1024), ("ln_g", 1024), ("ln_b", 1024), ("loss", 1)):
    _rows = -(-_n // 1024) * 8
    SMALL_SEGS[_name] = (_row, _rows)
    _row += _rows
SMALL_ROWS = -(-_row // 16) * 16


def _params(**kw):
    return pltpu.CompilerParams(vmem_limit_bytes=VMEM_LIMIT, **kw)


def _flip(v, d):
    return v if d == 0 else 1 - v


def _dot(a, b):
    return jnp.dot(a, b, preferred_element_type=F32)


def _dot_nt(a, b):
    return lax.dot_general(a, b, (((1,), (1,)), ((), ())), preferred_element_type=F32)


def _dot_tn(a, b):
    return lax.dot_general(a, b, (((0,), (0,)), ((), ())), preferred_element_type=F32)


def _sigmoid(v):
    return 1.0 / (1.0 + jnp.exp(-v))


def _colsum(v):
    return jnp.sum(v, axis=0, keepdims=True)


def _gather_stages(pos, src_ref, dst_ref, half, own_sem, s_sem, r_sem, fs_sem, fr_sem):
    x, y, cc, chip, sib = pos
    own = pltpu.make_async_copy(src_ref, dst_ref.at[chip], own_sem)
    first, landed, others = [], [], []
    for k, (dx, dy) in enumerate(REL3):
        px, py = _flip(x, dx), _flip(y, dy)
        first.append(pltpu.make_async_remote_copy(
            src_ref=src_ref.at[half(cc)], dst_ref=dst_ref.at[(chip,) + half(cc)],
            send_sem=s_sem.at[k], recv_sem=r_sem.at[k], device_id=(px, py, cc), device_id_type=MESH))
        landed.append(dst_ref.at[(2 * px + py,) + half(cc)])
        others.append(dst_ref.at[(2 * px + py,) + half(1 - cc)])
    passed = [pltpu.make_async_remote_copy(src_ref=landed[k], dst_ref=landed[k], send_sem=fs_sem.at[k],
                                           recv_sem=fr_sem.at[k], device_id=sib, device_id_type=MESH)
              for k in range(3)]

    def start(finish_src=None):
        for cp in first:
            cp.start()
        if finish_src is not None:
            finish_src()
        own.start()

    def forward():
        for k in range(3):
            pltpu.make_async_remote_copy(src_ref=landed[k], dst_ref=landed[k], send_sem=s_sem.at[k],
                                         recv_sem=r_sem.at[k], device_id=sib, device_id_type=MESH).wait_recv()
            passed[k].start()

    def finish():
        for k in range(3):
            pltpu.make_async_remote_copy(src_ref=others[k], dst_ref=others[k], send_sem=fs_sem.at[k],
                                         recv_sem=fr_sem.at[k], device_id=sib, device_id_type=MESH).wait_recv()
        for cp in first + passed:
            cp.wait_send()
        own.wait()

    return start, forward, finish


def _gather_scratch():
    return [pltpu.SemaphoreType.DMA, pltpu.SemaphoreType.DMA((3,)), pltpu.SemaphoreType.DMA((3,)),
            pltpu.SemaphoreType.DMA((3,)), pltpu.SemaphoreType.DMA((3,))]


def _gather_and_ada(c, w_ada, b_ada4, w_in_sh):
    def body(c_ref, w_ref, b_ref, win_ref, call_ref, shift_ref, scale_ref, gate_ref, wt_pad_ref,
             win_all, win_bf, ada_ref, cslab, sbuf, rbuf, cs_sem, cr_sem, as_sem, ar_sem, *gather_sems):
        x, y, cc = lax.axis_index("x"), lax.axis_index("y"), lax.axis_index("c")
        me = 4 * x + 2 * y + cc
        chip = 2 * x + y
        lane_half = lambda which: (slice(None), pl.ds(pl.multiple_of(which * (D // 2), D // 2), D // 2))
        def round_half(which):
            for h in range(2):
                @pl.when(which == h)
                def _():
                    lanes = slice(h * (D // 2), (h + 1) * (D // 2))
                    win_bf[:, lanes] = win_ref[:, 0, lanes].astype(BF16)

        start, forward, finish = _gather_stages((x, y, cc, chip, (x, y, 1 - cc)), win_bf, win_all, lane_half,
                                                *gather_sems)
        round_half(cc)
        start(lambda: round_half(1 - cc))

        cslab[...] = jnp.broadcast_to(c_ref[...], (8, D))
        call_ref[me] = cslab[...]
        gathers = []
        for k, (dx, dy, dc) in enumerate(REL7):
            cp = pltpu.make_async_remote_copy(
                src_ref=cslab, dst_ref=call_ref.at[me], send_sem=cs_sem.at[k], recv_sem=cr_sem.at[k],
                device_id=(_flip(x, dx), _flip(y, dy), _flip(cc, dc)), device_id_type=MESH)
            cp.start()
            gathers.append(cp)
        for cp in gathers:
            cp.wait()
        slab_row = lax.broadcasted_iota(jnp.int32, (8, 1), 0)
        mat = jnp.zeros((8, D), F32)
        for r in range(8):
            mat = jnp.where(slab_row == r, call_ref[r], mat)
        act = (mat * _sigmoid(mat)).astype(BF16)
        part = _dot(act, w_ref[...].astype(BF16))
        sends = []
        for k, (dx, dy) in enumerate(REL3):
            px, py = _flip(x, dx), _flip(y, dy)
            r = 4 * px + 2 * py + cc
            piece = _colsum(jnp.where(slab_row == r, part, 0.0))
            sbuf[k] = jnp.broadcast_to(piece, (8, SHARD_ADA))
            cp = pltpu.make_async_remote_copy(
                src_ref=sbuf.at[k], dst_ref=rbuf.at[k], send_sem=as_sem.at[k], recv_sem=ar_sem.at[k],
                device_id=(px, py, cc), device_id_type=MESH)
            cp.start()
            sends.append(cp)
        own_piece = _colsum(jnp.where(slab_row == me, part, 0.0))
        ada_ref[chip] = jnp.broadcast_to(own_piece, (8, SHARD_ADA)) + b_ref[chip]
        for k, (dx, dy) in enumerate(REL3):
            sends[k].wait()
            a = 2 * _flip(x, dx) + _flip(y, dy)
            ada_ref[a] = rbuf[k] + b_ref[a]
        ada = jnp.concatenate([ada_ref[a][0:1, :] for a in range(N_CHIPS)], axis=1)
        shift_ref[...] = ada[:, 0:D]
        scale_ref[...] = ada[:, D:2 * D]
        gate_ref[...] = ada[:, 2 * D:3 * D]

        forward()
        finish()
        n_real = 3 * D_ATT + N_HEADS
        for a in range(N_CHIPS):
            lo, hi = a * SHARD_IN, (a + 1) * SHARD_IN
            for s0, s1 in ((lo, min(hi, D_ATT)), (max(lo, D_ATT), min(hi, n_real)), (max(lo, n_real), hi)):
                if s0 < s1:
                    rows = win_all[a, s0 - lo:s1 - lo, :]
                    if s1 <= D_ATT:
                        rows = rows * jnp.asarray(Q_SCALE, BF16)
                    shift = O_P - n_real if s0 >= n_real else 0
                    wt_pad_ref[s0 + shift:s1 + shift, :] = rows
        wt_pad_ref[n_real:O_P, :] = jnp.zeros((O_P - n_real, D), BF16)

    vm = pl.BlockSpec(memory_space=pltpu.VMEM)
    return pl.pallas_call(
        body, name="gather_and_ada",
        out_shape=(jax.ShapeDtypeStruct((8, 8, D), F32),) + (jax.ShapeDtypeStruct((1, D), F32),) * 3
        + (jax.ShapeDtypeStruct((D_PAD, D), BF16),),
        in_specs=[vm] * 4, out_specs=(vm,) * 5,
        scratch_shapes=[pltpu.VMEM((N_CHIPS, SHARD_IN, D), BF16), pltpu.VMEM((SHARD_IN, D), BF16),
                        pltpu.VMEM((N_CHIPS, 8, SHARD_ADA), F32), pltpu.VMEM((8, D), F32), pltpu.VMEM((3, 8, SHARD_ADA), F32),
                        pltpu.VMEM((3, 8, SHARD_ADA), F32),
                        pltpu.SemaphoreType.DMA((7,)), pltpu.SemaphoreType.DMA((7,)),
                        pltpu.SemaphoreType.DMA((3,)), pltpu.SemaphoreType.DMA((3,))] + _gather_scratch(),
        compiler_params=_params(),
    )(c, w_ada, b_ada4, w_in_sh)


def _shard_cols():
    cut, gap = O_F + N_HEADS, O_P - (O_F + N_HEADS)
    out = []
    for a in range(N_CHIPS):
        lo, hi = a * SHARD_IN, (a + 1) * SHARD_IN
        out.append(([(lo, min(hi, cut))] if lo < cut else []) + ([(max(lo, cut) + gap, hi + gap)] if hi > cut else []))
    return out


def _scatter_stages(pos, g_ref, sc_ref, out_ref, sib_buf, send_buf, ici_buf, sem1, sem2s, sem2r, sem3, part=(0, 1),
                    cols=None, own_buf=None):
    x, y, cc, chip, sib = pos
    q, n_parts = part
    RH = (g_ref.shape[1] if cols is None else g_ref.shape[0]) // 2 // n_parts
    mine = pl.ds(pl.multiple_of((cc * n_parts + q) * RH, RH), RH)
    theirs = pl.ds(pl.multiple_of(((1 - cc) * n_parts + q) * RH, RH), RH)
    cp1 = pltpu.make_async_remote_copy(
        src_ref=g_ref.at[:, theirs, :] if cols is None else g_ref.at[theirs, :], dst_ref=sib_buf,
        send_sem=sem1.at[0], recv_sem=sem1.at[1], device_id=sib, device_id_type=MESH)
    sends = []
    for k, (dx, dy) in enumerate(REL3):
        px, py = _flip(x, dx), _flip(y, dy)
        sends.append(pltpu.make_async_remote_copy(
            src_ref=send_buf.at[2 * px + py], dst_ref=ici_buf.at[chip],
            send_sem=sem2s.at[k], recv_sem=sem2r.at[k], device_id=(px, py, cc), device_id_type=MESH))
    cp3 = pltpu.make_async_remote_copy(
        src_ref=out_ref.at[mine, :], dst_ref=out_ref.at[mine, :], send_sem=sem3.at[0], recv_sem=sem3.at[1],
        device_id=sib, device_id_type=MESH)

    def finish1():
        cp1.wait()
        if cols is None:
            for a in range(N_CHIPS):
                both = g_ref[a, mine, :] + sib_buf[a]
                sib_buf[a] = both
                send_buf[a] = both.astype(BF16)
        else:
            both = g_ref[mine, :] + sib_buf[...]
            for a, pieces in enumerate(cols):
                at = 0
                for lo, hi in pieces:
                    own_buf[a, :, at:at + hi - lo] = both[:, lo:hi]
                    send_buf[a, :, at:at + hi - lo] = both[:, lo:hi].astype(BF16)
                    at += hi - lo

    def start2():
        for cp in sends:
            cp.start()
        ici_buf[chip] = send_buf[chip]

    def finish2():
        for cp in sends:
            cp.wait()
        own = (sib_buf if cols is None else own_buf)[chip]
        parts = [jnp.where(chip == a, own, ici_buf[a].astype(F32)) for a in range(N_CHIPS)]
        total = (parts[0] + parts[1]) + (parts[2] + parts[3])
        out_ref[mine, 0:own.shape[1]] = total if sc_ref is None else total * sc_ref[chip]

    return [(cp1.start, finish1), (start2, finish2), (cp3.start, cp3.wait)]


def _all_reduce_stages(pos, g_ref, out_ref, sib_buf, ici_buf, sem1, sem2s, sem2r, sem3):
    x, y, cc, chip, sib = pos
    RH = g_ref.shape[0] // 2
    mine = pl.ds(pl.multiple_of(cc * RH, 8), RH)
    theirs = pl.ds(pl.multiple_of((1 - cc) * RH, 8), RH)
    cp1 = pltpu.make_async_remote_copy(
        src_ref=g_ref.at[theirs, :], dst_ref=sib_buf, send_sem=sem1.at[0], recv_sem=sem1.at[1],
        device_id=sib, device_id_type=MESH)
    sends = []
    for k, (dx, dy) in enumerate(REL3):
        px, py = _flip(x, dx), _flip(y, dy)
        sends.append(pltpu.make_async_remote_copy(
            src_ref=sib_buf, dst_ref=ici_buf.at[chip],
            send_sem=sem2s.at[k], recv_sem=sem2r.at[k], device_id=(px, py, cc), device_id_type=MESH))
    cp3 = pltpu.make_async_remote_copy(
        src_ref=out_ref.at[mine, :], dst_ref=out_ref.at[mine, :], send_sem=sem3.at[0], recv_sem=sem3.at[1],
        device_id=sib, device_id_type=MESH)

    def finish1():
        cp1.wait()
        sib_buf[...] = g_ref[mine, :] + sib_buf[...]

    def start2():
        for cp in sends:
            cp.start()
        ici_buf[chip] = sib_buf[...]

    def finish2():
        for cp in sends:
            cp.wait()
        out_ref[mine, :] = (ici_buf[0] + ici_buf[1]) + (ici_buf[2] + ici_buf[3])

    return [(cp1.start, finish1), (start2, finish2), (cp3.start, cp3.wait)]


def _stage_sems():
    return [pltpu.SemaphoreType.DMA((2,)), pltpu.SemaphoreType.DMA((3,)),
            pltpu.SemaphoreType.DMA((3,)), pltpu.SemaphoreType.DMA((2,))]


def _scatter_scratch(r, c):
    return [pltpu.VMEM((N_CHIPS, r // 2, c), F32), pltpu.VMEM((N_CHIPS, r // 2, c), BF16),
            pltpu.VMEM((N_CHIPS, r // 2, c), BF16)] + _stage_sems()


def _reduce_all(gw_pad, sc_in, small, dada, c_all):
    R = small.shape[0]
    W = dada.shape[1]
    r_in, p_in = gw_pad.shape
    c_in = SHARD_IN
    chunk = r_in // 4

    def chunk_scratch():
        return ([pltpu.VMEM((chunk, p_in), F32), pltpu.VMEM((N_CHIPS, chunk, c_in), BF16),
                 pltpu.VMEM((N_CHIPS, chunk, c_in), BF16)] + _stage_sems()
                + [pltpu.VMEM((N_CHIPS, chunk, c_in), F32)])

    n_in = len(chunk_scratch())

    c_wide = -(-c_in // 128) * 128

    def body(gin_ref, scin_ref, sm_ref, d_ref, c_ref, ocols_ref, osm_ref, gwa_ref, gba_ref, loss_ref, oin_ref,
             dall_ref, *scratch):
        x, y, cc = lax.axis_index("x"), lax.axis_index("y"), lax.axis_index("c")
        me = 4 * x + 2 * y + cc
        chip = 2 * x + y
        pos = (x, y, cc, chip, (x, y, 1 - cc))
        oin_ref[:, c_in:c_wide] = jnp.zeros((r_in, c_wide - c_in), F32)
        dslab, ds_sem, dr_sem = scratch[0:3]
        a_bufs, b_bufs, sm_bufs = scratch[3:3 + n_in], scratch[3 + n_in:3 + 2 * n_in], scratch[3 + 2 * n_in:]
        dslab[...] = jnp.broadcast_to(d_ref[...], (8, W))
        dall_ref[me] = dslab[...]
        gathers = []
        for k, (dx, dy, dc) in enumerate(REL7):
            cp = pltpu.make_async_remote_copy(
                src_ref=dslab, dst_ref=dall_ref.at[me], send_sem=ds_sem.at[k], recv_sem=dr_sem.at[k],
                device_id=(_flip(x, dx), _flip(y, dy), _flip(cc, dc)), device_id_type=MESH)
            cp.start()
            gathers.append(cp)
        cols = _shard_cols()
        first = _scatter_stages(pos, gin_ref, scin_ref, oin_ref, *a_bufs[:-1], part=(0, 2), cols=cols,
                                own_buf=a_bufs[-1])
        second = _scatter_stages(pos, gin_ref, scin_ref, oin_ref, *b_bufs[:-1], part=(1, 2), cols=cols,
                                 own_buf=b_bufs[-1])
        little = _all_reduce_stages(pos, sm_ref, osm_ref, *sm_bufs)
        for plan in (first, second, little):
            plan[0][0]()
        first[0][1]()
        first[1][0]()
        little[0][1]()
        little[1][0]()
        second[0][1]()
        second[1][0]()
        for cp in gathers:
            cp.wait()
        slab_row = lax.broadcasted_iota(jnp.int32, (8, 1), 0)
        cm = jnp.zeros((8, D), F32)
        dm = jnp.zeros((8, W), F32)
        for r in range(8):
            cm = jnp.where(slab_row == r, c_ref[r], cm)
            dm = jnp.where(slab_row == r, dall_ref[r], dm)
        act = cm * _sigmoid(cm)
        dcol = dm[:, 0:SHARD_ADA]
        for a in range(1, N_CHIPS):
            dcol = jnp.where(chip == a, dm[:, a * SHARD_ADA:(a + 1) * SHARD_ADA], dcol)
        lhs = jnp.concatenate([act, jnp.zeros((8, D), F32)], axis=0).astype(BF16)
        rhs = jnp.concatenate([dcol, jnp.zeros((8, SHARD_ADA), F32)], axis=0).astype(BF16)
        gwa_ref[...] = _dot_tn(lhs, rhs)
        gba_ref[...] = _colsum(dm)
        first[1][1]()
        first[2][0]()
        second[1][1]()
        second[2][0]()
        little[1][1]()
        little[2][0]()
        for plan in (first, second, little):
            plan[2][1]()
        loss_row = SMALL_SEGS["loss"][0]
        loss_ref[...] = osm_ref[loss_row:loss_row + 1, 0:1]
        ocols_ref[...] = oin_ref[...].T[0:c_in, :][:, None, :]

    scratch = [pltpu.VMEM((r_in, c_wide), F32), pltpu.VMEM((8, 8, W), F32),
               pltpu.VMEM((8, W), F32), pltpu.SemaphoreType.DMA((7,)), pltpu.SemaphoreType.DMA((7,))]
    scratch += chunk_scratch() + chunk_scratch()
    scratch += [pltpu.VMEM((R // 2, 128), F32), pltpu.VMEM((N_CHIPS, R // 2, 128), F32)] + _stage_sems()
    vm = pl.BlockSpec(memory_space=pltpu.VMEM)
    return pl.pallas_call(
        body, name="reduce_all",
        out_shape=(jax.ShapeDtypeStruct((c_in, 1, r_in), F32), jax.ShapeDtypeStruct((R, 128), F32),
                   jax.ShapeDtypeStruct((D, SHARD_ADA), F32), jax.ShapeDtypeStruct((1, W), F32),
                   jax.ShapeDtypeStruct((1, 1), F32)),
        in_specs=[vm] * 5, out_specs=(vm,) * 5,
        scratch_shapes=scratch,
        compiler_params=_params(),
    )(gw_pad, sc_in, small, dada, c_all)


def _in_proj(x, shift, scale, wt_pad, b_pad, w_out_sh):
    S = x.shape[0]
    tm = min(TM_PROJ, S)
    n_steps = S // tm
    assert n_steps >= 3

    def body(x_ref, sh_ref, sc_ref, w_ref, b_ref, wo_ref, u_ref, qkv_ref, f_ref, p_ref, g_ref, wo_all,
             wo_buf, wo_bf, *gather_sems):
        i = pl.program_id(0)
        xx, yy, cc = lax.axis_index("x"), lax.axis_index("y"), lax.axis_index("c")
        row_half = lambda which: (pl.ds(pl.multiple_of(which * (SHARD_OUT // 2), SHARD_OUT // 2), SHARD_OUT // 2),
                                  slice(None))
        start, forward, finish = _gather_stages((xx, yy, cc, 2 * xx + yy, (xx, yy, 1 - cc)), wo_bf, wo_buf,
                                                row_half, *gather_sems)

        @pl.when(i == 0)
        def _():
            wo_bf[...] = wo_ref[...].astype(BF16)
            start()

        pl.when(i == n_steps // 2)(forward)

        @pl.when(i == n_steps - 1)
        def _():
            finish()
            wo_all[...] = wo_buf[...]

        u = (x_ref[...] * (1.0 + sc_ref[...]) + sh_ref[...]).astype(BF16)
        u_ref[...] = u
        qkv_ref[...] = (_dot_nt(u, w_ref[O_QKV:O_F, :]) + b_ref[:, O_QKV:O_F]).astype(BF16)
        f_ref[...] = _dot_nt(u, w_ref[O_F:O_P, :]) + b_ref[:, O_F:O_P]
        p_ref[...] = _dot_nt(u, w_ref[O_P:O_G, :]) + b_ref[:, O_P:O_G]
        g_ref[...] = _dot_nt(u, w_ref[O_G:D_PAD, :]) + b_ref[:, O_G:D_PAD]

    row = lambda w: pl.BlockSpec((tm, w), lambda i: (i, 0))
    full = lambda a: pl.BlockSpec(a.shape, lambda i: (0, 0))
    vm = pl.BlockSpec(memory_space=pltpu.VMEM)
    return pl.pallas_call(
        body, name="in_proj", grid=(n_steps,),
        out_shape=(jax.ShapeDtypeStruct((S, D), BF16), jax.ShapeDtypeStruct((S, 3 * D_ATT), BF16),
                   jax.ShapeDtypeStruct((S, 128), F32), jax.ShapeDtypeStruct((S, D_POOL), F32),
                   jax.ShapeDtypeStruct((S, D), F32), jax.ShapeDtypeStruct((N_CHIPS,) + w_out_sh.shape, BF16)),
        in_specs=[row(D), full(shift), full(scale), full(wt_pad), full(b_pad), vm],
        out_specs=(row(D), row(3 * D_ATT), row(128), row(D_POOL), row(D), vm),
        scratch_shapes=[pltpu.VMEM((N_CHIPS,) + w_out_sh.shape, BF16), pltpu.VMEM(w_out_sh.shape, BF16)]
        + _gather_scratch(),
        compiler_params=_params(dimension_semantics=("arbitrary",)),
    )(x, shift, scale, wt_pad, b_pad, w_out_sh)


def _forget_cumsum(f):
    S = f.shape[0]
    tm = min(T_ATT, S)

    def body(f_ref, out_ref, carry):
        @pl.when(pl.program_id(0) == 0)
        def _():
            carry[...] = jnp.zeros_like(carry)
        v = f_ref[...]
        logf = jnp.minimum(v, 0.0) - jnp.log(1.0 + jnp.exp(-jnp.abs(v)))
        r = lax.broadcasted_iota(jnp.int32, (tm, tm), 0)
        c = lax.broadcasted_iota(jnp.int32, (tm, tm), 1)
        tri = (r <= c).astype(F32)
        rows8 = logf.T[0:8, :]
        cum8 = jnp.dot(rows8, tri, preferred_element_type=F32, precision=lax.Precision.HIGHEST) + carry[...]
        out_ref[...] = jnp.concatenate([cum8, jnp.zeros((128 - 8, tm), F32)], axis=0).T
        last = lax.broadcasted_iota(jnp.int32, (1, tm), 1) == tm - 1
        carry[...] = jnp.sum(jnp.where(last, cum8, 0.0), axis=1, keepdims=True)

    return pl.pallas_call(
        body, name="forget_cumsum", grid=(S // tm,),
        out_shape=jax.ShapeDtypeStruct((S, 128), F32),
        in_specs=[pl.BlockSpec((tm, 128), lambda i: (i, 0))],
        out_specs=pl.BlockSpec((tm, 128), lambda i: (i, 0)),
        scratch_shapes=[pltpu.VMEM((8, 1), F32)],
        compiler_params=_params(dimension_semantics=("arbitrary",)),
    )(f)


def _split3(v):
    hi = v.astype(BF16)
    rest = v - hi.astype(F32)
    mid = rest.astype(BF16)
    lo = (rest - mid.astype(F32)).astype(BF16)
    return hi, mid, lo


def _attention_fwd(qkv, big_f):
    S = qkv.shape[0]
    T = min(T_ATT, S)
    n_t = S // T

    def body(q_ref, k_ref, v_ref, f_ref, o_ref, lse_ref, kaug_sc, vt_sc, m_sc, l_sc, acc_sc):
        hp = pl.program_id(0)
        i = pl.program_id(1)
        lane = lax.broadcasted_iota(jnp.int32, (1, 128), 1)
        sub = lax.broadcasted_iota(jnp.int32, (128, 1), 0)
        head_sel = (lane < HEAD_DIM, lane >= HEAD_DIM)
        head_sel_t = (sub < HEAD_DIM, sub >= HEAD_DIM)
        spare = (HEAD_DIM, 0)
        zero = jnp.zeros((), BF16)

        @pl.when(i == 0)
        def _():
            def prep(jt, carry):
                rows = pl.ds(pl.multiple_of(jt * T, T), T)
                k = k_ref[rows, :]
                ft = f_ref[rows, :]
                vt = v_ref[rows, :].astype(F32).T
                for h in range(2):
                    fh = jnp.sum(jnp.where(lane == 2 * hp + h, ft, 0.0), axis=1, keepdims=True)
                    hi, mid, lo = _split3(-fh)
                    b = spare[h]
                    bias = jnp.where(lane == b, hi, jnp.where(lane == b + 1, mid, jnp.where(lane == b + 2, lo, zero)))
                    kaug_sc[h, rows, :] = jnp.where(head_sel[h], k, bias)
                    vt_sc[h, jt] = jnp.where(head_sel_t[h], vt, 0.0).astype(BF16)
                return carry

            lax.fori_loop(0, n_t, prep, 0)

        q = q_ref[...]
        q_heads = []
        for h in range(2):
            ones = jnp.where((lane >= spare[h]) & (lane < spare[h] + 3), jnp.ones((), BF16), zero)
            q_heads.append(jnp.where(head_sel[h], q, ones))
        m_sc[...] = jnp.full((8, T), NEG, F32)
        l_sc[...] = jnp.zeros((8, T), F32)
        acc_sc[...] = jnp.zeros((128, T), F32)

        def update(j, k_lo, n_k, q_lo, masked):
            rows = pl.ds(pl.multiple_of(j * T + k_lo, n_k), n_k)
            n_q = T - q_lo
            alphas, pvs = [], []
            for h in range(2):
                s_t = _dot_nt(kaug_sc[h, rows, :], q_heads[h][q_lo:, :])
                if masked:
                    rr = lax.broadcasted_iota(jnp.int32, (n_k, n_q), 0) + k_lo
                    cc = lax.broadcasted_iota(jnp.int32, (n_k, n_q), 1) + q_lo
                    s_t = jnp.where(rr <= cc, s_t, NEG)
                m_prev = m_sc[h:h + 1, q_lo:]
                m_new = jnp.maximum(m_prev, jnp.max(s_t, axis=0, keepdims=True))
                alpha = jnp.exp(m_prev - m_new)
                p_t = jnp.exp(s_t - m_new)
                l_sc[h:h + 1, q_lo:] = alpha * l_sc[h:h + 1, q_lo:] + jnp.sum(p_t, axis=0, keepdims=True)
                m_sc[h:h + 1, q_lo:] = m_new
                alphas.append(alpha)
                pvs.append(_dot(vt_sc[h, j, :, k_lo:k_lo + n_k], p_t.astype(BF16)))
            acc_sc[:, q_lo:] = (acc_sc[:, q_lo:] * jnp.where(head_sel_t[0], alphas[0], alphas[1])
                                + (pvs[0] + pvs[1]))

        def two_off_diagonal(jj, carry):
            update(2 * jj, 0, T, 0, False)
            update(2 * jj + 1, 0, T, 0, False)
            return carry

        lax.fori_loop(0, i // 2, two_off_diagonal, 0)

        @pl.when(i % 2 == 1)
        def _():
            update(i - 1, 0, T, 0, False)

        update(i, 0, T, 0, True)
        l = l_sc[...]
        o_ref[...] = (acc_sc[...] / jnp.where(head_sel_t[0], l[0:1, :], l[1:2, :])).T
        is_head = lax.broadcasted_iota(jnp.int32, (8, 1), 0) < 2
        lse_ref[...] = jnp.where(is_head, m_sc[...] + jnp.log(jnp.where(is_head, l, 1.0)), 0.0)

    return pl.pallas_call(
        body, name="attention_fwd", grid=(N_PAIR, n_t),
        out_shape=(jax.ShapeDtypeStruct((S, D_ATT), F32), jax.ShapeDtypeStruct((N_PAIR, n_t, 8, T), F32)),
        in_specs=[pl.BlockSpec((T, 128), lambda hp, i: (i, hp)),
                  pl.BlockSpec((S, 128), lambda hp, i: (0, N_PAIR + hp)),
                  pl.BlockSpec((S, 128), lambda hp, i: (0, 2 * N_PAIR + hp)),
                  pl.BlockSpec((S, 128), lambda hp, i: (0, 0))],
        out_specs=(pl.BlockSpec((T, 128), lambda hp, i: (i, hp)),
                   pl.BlockSpec((None, None, 8, T), lambda hp, i: (hp, i, 0, 0))),
        scratch_shapes=[pltpu.VMEM((2, S, 128), BF16), pltpu.VMEM((2, n_t, 128, T), BF16),
                        pltpu.VMEM((8, T), F32), pltpu.VMEM((8, T), F32), pltpu.VMEM((128, T), F32)],
        compiler_params=_params(dimension_semantics=("arbitrary", "arbitrary")),
    )(qkv, qkv, qkv, big_f)


def _attention_bwd(qkv, datt, att, lse, big_f, gw_out4):
    S = qkv.shape[0]
    T = min(T_ATT, S)
    n_t = S // T
    n_steps = N_PAIR * n_t
    marks = (0, n_steps // 8, n_steps // 2, n_steps // 2 + n_steps // 8)

    def body(q_ref, do_ref, o_ref, lse_ref, k_ref, v_ref, fk_ref, gout_ref,
             dq_ref, dk_ref, dv_ref, cs_ref, dfk_ref, dfq_ref, oout_ref, stat_sc, dqt_sc, qaug_sc,
             out_buf, *red_bufs):
        hp = pl.program_id(0)
        j = pl.program_id(1)
        x, y, cc = lax.axis_index("x"), lax.axis_index("y"), lax.axis_index("c")
        plan = _scatter_stages((x, y, cc, 2 * x + y, (x, y, 1 - cc)), gout_ref, None, out_buf, *red_bufs)
        step = hp * n_t + j
        for n, mark in enumerate(marks):
            @pl.when(step == mark)
            def _(n=n):
                if n > 0:
                    plan[n - 1][1]()
                if n < 3:
                    plan[n][0]()
                else:
                    oout_ref[...] = out_buf[...]

        lane = lax.broadcasted_iota(jnp.int32, (1, 128), 1)
        sub = lax.broadcasted_iota(jnp.int32, (128, 1), 0)
        head_sel = (lane < HEAD_DIM, lane >= HEAD_DIM)
        head_sel_t = (sub < HEAD_DIM, sub >= HEAD_DIM)
        spare = (HEAD_DIM, 0)
        zero = jnp.zeros((), BF16)
        one = jnp.ones((), BF16)

        def bias_lanes(first, pieces):
            hi, mid, lo = pieces
            return lambda rest: jnp.where(lane == first, hi, jnp.where(lane == first + 1, mid,
                                                                        jnp.where(lane == first + 2, lo, rest)))

        @pl.when(j == 0)
        def _():
            dqt_sc[...] = jnp.zeros_like(dqt_sc)
            cs_ref[...] = jnp.zeros_like(cs_ref)
            dfq_ref[...] = jnp.zeros_like(dfq_ref)

            def prep(i, carry):
                rows = pl.ds(pl.multiple_of(i * T, T), T)
                q = q_ref[rows, :]
                do = do_ref[rows, :]
                prod = o_ref[rows, :] * do.astype(F32)
                d_a = jnp.sum(jnp.where(head_sel[0], prod, 0.0), axis=1, keepdims=True)
                d_b = jnp.sum(jnp.where(head_sel[0], 0.0, prod), axis=1, keepdims=True)
                delta_t = jnp.where(head_sel[0], d_a, d_b).T
                stat_sc[i, 0:1, :] = delta_t[0:1, :]
                stat_sc[i, 1:2, :] = delta_t[HEAD_DIM:HEAD_DIM + 1, :]
                lse = lse_ref[i]
                lse_cols = jnp.where(head_sel_t[0], lse[0:1, :], lse[1:2, :]).T
                for h in range(2):
                    neg_lse = -lse_cols[:, h * HEAD_DIM:h * HEAD_DIM + 1]
                    ones = jnp.where((lane >= spare[h]) & (lane < spare[h] + 3), one, zero)
                    qaug_sc[h, rows, :] = jnp.where(head_sel[h], q, bias_lanes(spare[h] + 3, _split3(neg_lse))(ones))
                return carry

            lax.fori_loop(0, n_t, prep, 0)

        k = k_ref[...]
        v = v_ref[...]
        fk = fk_ref[...]
        kt = k.astype(F32).T
        heads = []
        for h in range(2):
            fkh = jnp.sum(jnp.where(lane == 2 * hp + h, fk, 0.0), axis=1, keepdims=True)
            ones = jnp.where((lane >= spare[h] + 3) & (lane < spare[h] + 6), one, zero)
            kaug = jnp.where(head_sel[h], k, bias_lanes(spare[h], _split3(-fkh))(ones))
            heads.append((kaug, jnp.where(head_sel[h], v, zero), jnp.where(head_sel_t[h], kt, 0.0).astype(BF16)))

        def block(i, k_lo, n_k, q_lo, masked):
            n_q = T - q_lo
            rows = pl.ds(pl.multiple_of(i * T + q_lo, n_q), n_q)
            q = q_ref[rows, :]
            do = do_ref[rows, :]
            stat = stat_sc[i]
            dk = jnp.zeros((n_k, 128), F32)
            dv = jnp.zeros((n_k, 128), F32)
            dqt = jnp.zeros((128, n_q), F32)
            dfs = []
            for h in range(2):
                kaug, vh, kth = heads[h]
                arg = _dot_nt(kaug[k_lo:k_lo + n_k, :], qaug_sc[h, rows, :])
                if masked:
                    rr = lax.broadcasted_iota(jnp.int32, (n_k, n_q), 0) + k_lo
                    cc = lax.broadcasted_iota(jnp.int32, (n_k, n_q), 1) + q_lo
                    arg = jnp.where(rr <= cc, arg, NEG)
                p_t = jnp.exp(arg)
                ds_t = p_t * (_dot_nt(vh[k_lo:k_lo + n_k, :], do) - stat[h:h + 1, q_lo:])
                ds_bf = ds_t.astype(BF16)
                dv = dv + _dot(p_t.astype(BF16), jnp.where(head_sel[h], do, zero))
                dk = dk + _dot(ds_bf, jnp.where(head_sel[h], q, zero))
                dqt = dqt + _dot(kth[:, k_lo:k_lo + n_k], ds_bf)
                dfs.append(jnp.sum(ds_t, axis=1, keepdims=True))
                dfq_ref[i, h:h + 1, q_lo:] += _colsum(ds_t)
            dqt_sc[i, :, q_lo:] += dqt
            return dk, dv, dfs[0], dfs[1]

        def off_diagonal(i, acc):
            return tuple(a + b for a, b in zip(acc, block(i, 0, T, 0, False)))

        half = T // 2
        early = block(j, 0, half, 0, True)
        late = block(j, half, half, half, True)
        acc1 = tuple(jnp.concatenate([a, b], axis=0) for a, b in zip(early, late))
        n_off = n_t - 1 - j
        acc2 = lax.fori_loop(0, n_off // 2,
                             lambda ii, a: off_diagonal(j + 2 + 2 * ii, off_diagonal(j + 1 + 2 * ii, a)), acc1)
        dk_acc, dv_acc, dfa, dfb = lax.fori_loop(0, n_off % 2, lambda _, a: off_diagonal(n_t - 1, a), acc2)
        dk_ref[...] = dk_acc.astype(BF16)
        dv_ref[...] = dv_acc.astype(BF16)
        dfk_ref[...] = -jnp.where(lane == 0, dfa, jnp.where(lane == 1, dfb, 0.0))
        cs_ref[:, 128:256] = cs_ref[:, 128:256] + _colsum(dk_acc)
        cs_ref[:, 256:384] = cs_ref[:, 256:384] + _colsum(dv_acc)

        @pl.when(j == n_t - 1)
        def _():
            def finish(i, tot):
                dq = dqt_sc[i].T
                dq_ref[pl.ds(pl.multiple_of(i * T, T), T), :] = dq.astype(BF16)
                return tot + _colsum(dq)

            cs_ref[:, 0:128] = lax.fori_loop(0, n_t, finish, jnp.zeros((1, 128), F32))

    pair_rows = lambda hp, j: (hp, 0, 0)
    vm = pl.BlockSpec(memory_space=pltpu.VMEM)
    _, r_out, c_out = gw_out4.shape
    return pl.pallas_call(
        body, name="attention_bwd", grid=(N_PAIR, n_t),
        out_shape=(jax.ShapeDtypeStruct((S, D_ATT), BF16), jax.ShapeDtypeStruct((S, D_ATT), BF16),
                   jax.ShapeDtypeStruct((S, D_ATT), BF16), jax.ShapeDtypeStruct((N_PAIR, 1, 384), F32),
                   jax.ShapeDtypeStruct((N_PAIR, S, 128), F32),
                   jax.ShapeDtypeStruct((N_PAIR, n_t, 8, T), F32),
                   jax.ShapeDtypeStruct((r_out, c_out), F32)),
        in_specs=[pl.BlockSpec((S, 128), lambda hp, j: (0, hp)),
                  pl.BlockSpec((S, 128), lambda hp, j: (0, hp)),
                  pl.BlockSpec((S, 128), lambda hp, j: (0, hp)),
                  pl.BlockSpec((None, n_t, 8, T), lambda hp, j: (hp, 0, 0, 0)),
                  pl.BlockSpec((T, 128), lambda hp, j: (j, N_PAIR + hp)),
                  pl.BlockSpec((T, 128), lambda hp, j: (j, 2 * N_PAIR + hp)),
                  pl.BlockSpec((T, 128), lambda hp, j: (j, 0)),
                  vm],
        out_specs=(pl.BlockSpec((S, 128), lambda hp, j: (0, hp)),
                   pl.BlockSpec((T, 128), lambda hp, j: (j, hp)),
                   pl.BlockSpec((T, 128), lambda hp, j: (j, hp)),
                   pl.BlockSpec((None, 1, 384), pair_rows),
                   pl.BlockSpec((None, T, 128), lambda hp, j: (hp, j, 0)),
                   pl.BlockSpec((None, n_t, 8, T), lambda hp, j: (hp, 0, 0, 0)),
                   vm),
        scratch_shapes=[pltpu.VMEM((n_t, 8, T), F32), pltpu.VMEM((n_t, 128, T), F32),
                        pltpu.VMEM((2, S, 128), BF16), pltpu.VMEM((r_out, c_out), F32)]
        + _scatter_scratch(r_out, c_out),
        compiler_params=_params(dimension_semantics=("arbitrary", "arbitrary")),
    )(qkv, datt, att, lse, qkv, qkv, big_f, gw_out4)


def _window_counts(first_row, n_rows, window):
    t = lax.broadcasted_iota(jnp.int32, (n_rows, 1), 0) + first_row
    return jnp.minimum((t + 1).astype(F32), float(window))


def _middle(x, tgt, att, g, p, gate, w_mix, b_mix, pool_scale, w_out, b_out, ln_g, ln_b):
    S = x.shape[0]
    tm = min(TM_MID, S)
    halo_blocks = tm // POOL_HALO

    def body(x_ref, t_ref, att_ref, g_ref, p_ref, ph_ref, gate_ref, wm_ref, bm_ref, ps_ref, wo_ref, bo_ref,
             lg_ref, lb_ref,
             dh_ref, datt_ref, dg_ref, dpl_ref, gwo_ref, gwm_ref, vec_ref, loss_ref):
        i = pl.program_id(0)

        @pl.when(i == 0)
        def _():
            gwo_ref[...] = jnp.zeros_like(gwo_ref)
            gwm_ref[...] = jnp.zeros_like(gwm_ref)
            vec_ref[...] = jnp.zeros_like(vec_ref)
            loss_ref[...] = jnp.zeros_like(loss_ref)

        pc = p_ref[...]
        halo = jnp.where(i > 0, ph_ref[...], 0.0)
        pe = jnp.concatenate([halo, pc], axis=0)
        pooled_parts = []
        for gi, w in enumerate(POOL_WINDOWS):
            cur = pe[:, gi * POOL_GROUP:(gi + 1) * POOL_GROUP]
            span = 1
            while span < w:
                cur = cur + pltpu.roll(cur, span, 0)
                span *= 2
            wsum = cur[POOL_HALO:, :]
            mean = wsum / _window_counts(i * tm, tm, w)
            pooled_parts.append(mean - pc[:, gi * POOL_GROUP:(gi + 1) * POOL_GROUP])
        pooled_bf =[v.astype(BF16) for v in pooled_parts]
        wm = [wm_ref[gi].astype(BF16) for gi in range(4)]
        mixed = jnp.concatenate([_dot(pooled_bf[gi], wm[gi]) for gi in range(4)], axis=1) + bm_ref[...]
        ps = ps_ref[...]
        pool_out = mixed * ps
        gv = g_ref[...]
        sig = _sigmoid(gv)
        silu = gv * sig
        att = att_ref[...]
        y = jnp.concatenate([att * silu[:, :D_ATT], pool_out * silu[:, D_ATT:]], axis=1)
        y_bf = y.astype(BF16)
        wo = wo_ref[...]
        yo = _dot(y_bf, wo) + bo_ref[...]
        gate = gate_ref[...]
        h = ALPHA * x_ref[...] + gate * yo
        mu = jnp.mean(h, axis=1, keepdims=True)
        hc = h - mu
        var = jnp.mean(hc * hc, axis=1, keepdims=True)
        rstd = lax.rsqrt(var + LN_EPS)
        yhat = hc * rstd
        lg = lg_ref[...]
        out = yhat * lg + lb_ref[...]
        err = out - t_ref[...]
        loss_ref[...] += 0.5 * jnp.sum(jnp.mean(err * err, axis=1, keepdims=True), axis=0, keepdims=True)

        dout = err * (1.0 / D)
        g_ln_b = _colsum(dout)
        g_ln_g = _colsum(dout * yhat)
        dyh = dout * lg
        dh = rstd * (dyh - jnp.mean(dyh, axis=1, keepdims=True)
                     - yhat * jnp.mean(dyh * yhat, axis=1, keepdims=True))
        dh_ref[...] = dh
        d_gate = _colsum(dh * yo)
        dyo = gate * dh
        g_b_out = _colsum(dyo)
        dyo_bf = dyo.astype(BF16)
        gwo_ref[...] += _dot_tn(y_bf, dyo_bf)
        dy = _dot_nt(dyo_bf, wo)
        dsilu = sig * (1.0 + gv * (1.0 - sig))
        dy_a = dy[:, :D_ATT]
        dy_p = dy[:, D_ATT:]
        datt_ref[...] = (dy_a * silu[:, :D_ATT]).astype(BF16)
        dpo = dy_p * silu[:, D_ATT:]
        dg = jnp.concatenate([dy_a * att * dsilu[:, :D_ATT], dy_p * pool_out * dsilu[:, D_ATT:]], axis=1)
        dg_ref[...] = dg.astype(BF16)
        g_dg = _colsum(dg)
        g_ps = _colsum(dpo * mixed)
        dmixed = dpo * ps
        g_bm = _colsum(dmixed)
        dmixed_bf = dmixed.astype(BF16)
        dpl = []
        for gi in range(4):
            dm = dmixed_bf[:, gi * POOL_GROUP:(gi + 1) * POOL_GROUP]
            gwm_ref[gi] += _dot_tn(pooled_bf[gi], dm)
            dpl.append(_dot_nt(dm, wm[gi]))
        dpl_ref[...] = jnp.concatenate(dpl, axis=1)
        vec_ref[0:1, :] += g_ln_g
        vec_ref[1:2, :] += g_ln_b
        vec_ref[2:3, :] += d_gate
        vec_ref[3:4, :] += g_b_out
        vec_ref[4:5, :] += g_dg
        vec_ref[5:6, 0:D_POOL] += g_ps
        vec_ref[6:7, 0:D_POOL] += g_bm

    row = lambda w: pl.BlockSpec((tm, w), lambda i: (i, 0))
    full2 = lambda a: pl.BlockSpec(a.shape, lambda i: (0, 0))
    full3 = lambda a: pl.BlockSpec(a.shape, lambda i: (0, 0, 0))
    return pl.pallas_call(
        body, name="middle", grid=(S // tm,),
        out_shape=(jax.ShapeDtypeStruct((S, D), F32),
                   jax.ShapeDtypeStruct((S, D_ATT), BF16),
                   jax.ShapeDtypeStruct((S, D), BF16),
                   jax.ShapeDtypeStruct((S, D_POOL), F32),
                   jax.ShapeDtypeStruct((D, D), F32),
                   jax.ShapeDtypeStruct((4, POOL_GROUP, POOL_GROUP), F32),
                   jax.ShapeDtypeStruct((8, D), F32),
                   jax.ShapeDtypeStruct((1, 1), F32)),
        in_specs=[row(D), row(D), row(D_ATT), row(D), row(D_POOL),
                  pl.BlockSpec((POOL_HALO, D_POOL), lambda i: (jnp.maximum(i * halo_blocks - 1, 0), 0)),
                  full2(gate), full3(w_mix), full2(b_mix), full2(pool_scale), full2(w_out), full2(b_out),
                  full2(ln_g), full2(ln_b)],
        out_specs=(row(D), row(D_ATT), row(D), row(D_POOL),
                   pl.BlockSpec((D, D), lambda i: (0, 0)),
                   pl.BlockSpec((4, POOL_GROUP, POOL_GROUP), lambda i: (0, 0, 0)),
                   pl.BlockSpec((8, D), lambda i: (0, 0)),
                   pl.BlockSpec((1, 1), lambda i: (0, 0))),
        compiler_params=_params(dimension_semantics=("arbitrary",)),
    )(x, tgt, att, g, p, p, gate, w_mix, b_mix, pool_scale, w_out, b_out, ln_g, ln_b)


def _tail(dpl, dfk, dfq, f):
    S = dpl.shape[0]
    tm = min(T_ATT, S)
    n_t = S // tm
    halo_blocks = tm // POOL_HALO
    last_halo = S // POOL_HALO - 1

    def body(d_ref, dn_ref, dfk_ref, dfq_ref, f_ref, dp_ref, df_ref, cs_ref, carry):
        s = pl.program_id(0)
        i = n_t - 1 - s

        @pl.when(s == 0)
        def _():
            carry[...] = jnp.zeros_like(carry)
            cs_ref[...] = jnp.zeros_like(cs_ref)

        dc = d_ref[...]
        nxt = jnp.where(s > 0, dn_ref[...], 0.0)
        de = jnp.concatenate([dc, nxt], axis=0)
        n_e = tm + POOL_HALO
        parts = []
        for gi, w in enumerate(POOL_WINDOWS):
            cur = de[:, gi * POOL_GROUP:(gi + 1) * POOL_GROUP] / _window_counts(i * tm, n_e, w)
            span = 1
            while span < w:
                cur = cur + pltpu.roll(cur, n_e - span, 0)
                span *= 2
            parts.append(cur[:tm, :] - dc[:, gi * POOL_GROUP:(gi + 1) * POOL_GROUP])
        dp = jnp.concatenate(parts, axis=1)
        dp_ref[...] = dp.astype(BF16)
        cs_ref[0:1, :] += _colsum(dp)

        r = lax.broadcasted_iota(jnp.int32, (tm, tm), 0)
        c = lax.broadcasted_iota(jnp.int32, (tm, tm), 1)
        tri = (r >= c).astype(F32)
        k_cols = dfk_ref[0]
        rows8 = dfq_ref[0]
        for hp in range(1, N_PAIR):
            k_cols = k_cols + pltpu.roll(dfk_ref[hp], 2 * hp, 1)
            rows8 = rows8 + pltpu.roll(dfq_ref[hp], 2 * hp, 0)
        rows8 = rows8 + k_cols.T[0:8, :]
        dlogf8 = jnp.dot(rows8, tri, preferred_element_type=F32, precision=lax.Precision.HIGHEST) + carry[...]
        first = lax.broadcasted_iota(jnp.int32, (1, tm), 1) == 0
        carry[...] = jnp.sum(jnp.where(first, dlogf8, 0.0), axis=1, keepdims=True)
        dlogf = jnp.concatenate([dlogf8, jnp.zeros((128 - 8, tm), F32)], axis=0).T
        df = dlogf * _sigmoid(-f_ref[...])
        df_ref[...] = df.astype(BF16)
        cs_ref[1:2, 0:128] += _colsum(df)

    rev = lambda w: pl.BlockSpec((tm, w), lambda s: (n_t - 1 - s, 0))
    return pl.pallas_call(
        body, name="tail", grid=(n_t,),
        out_shape=(jax.ShapeDtypeStruct((S, D_POOL), BF16), jax.ShapeDtypeStruct((S, 128), BF16),
                   jax.ShapeDtypeStruct((8, D_POOL), F32)),
        in_specs=[rev(D_POOL),
                  pl.BlockSpec((POOL_HALO, D_POOL),
                               lambda s: (jnp.minimum((n_t - s) * halo_blocks, last_halo), 0)),
                  pl.BlockSpec((N_PAIR, tm, 128), lambda s: (0, n_t - 1 - s, 0)),
                  pl.BlockSpec((N_PAIR, None, 8, tm), lambda s: (0, n_t - 1 - s, 0, 0)),
                  rev(128)],
        out_specs=(rev(D_POOL), rev(128), pl.BlockSpec((8, D_POOL), lambda s: (0, 0))),
        scratch_shapes=[pltpu.VMEM((8, 1), F32)],
        compiler_params=_params(dimension_semantics=("arbitrary",)),
    )(dpl, dpl, dfk, dfq, f)


PIECES = ((O_QKV, D_ATT), (O_QKV + D_ATT, D_ATT), (O_QKV + 2 * D_ATT, D_ATT), (O_F, 128), (O_P, D_POOL), (O_G, D))


def _grad_w_in(u, pieces):
    S = u.shape[0]
    tm = min(TM_GW, S)
    n_t = S // tm

    def body(u_ref, *rest):
        piece_refs, out_ref, acc, sem = rest[:6], rest[6], rest[7], rest[8]
        i = pl.program_id(0)

        @pl.when(i == 0)
        def _():
            acc[...] = jnp.zeros_like(acc)

        u_t = u_ref[...]
        for (off, w), ref in zip(PIECES, piece_refs):
            acc[:, off:off + w] += _dot_tn(u_t, ref[...])

        @pl.when(i == n_t - 1)
        def _():
            cp = pltpu.make_async_copy(acc, out_ref, sem)
            cp.start()
            cp.wait()

    return pl.pallas_call(
        body, name="grad_w_in", grid=(n_t,),
        out_shape=jax.ShapeDtypeStruct((D, D_PAD), F32),
        in_specs=[pl.BlockSpec((tm, D), lambda i: (i, 0))]
        + [pl.BlockSpec((tm, w), lambda i: (i, 0)) for _, w in PIECES],
        out_specs=pl.BlockSpec(memory_space=pl.ANY),
        scratch_shapes=[pltpu.VMEM((D, D_PAD), F32), pltpu.SemaphoreType.DMA],
        compiler_params=_params(dimension_semantics=("arbitrary",)),
    )(u, *pieces)


def _grad_x(pieces, wt_pad, dh, x, scale):
    S = x.shape[0]
    tm = min(TM_DU, S)

    def body(*refs):
        piece_refs = refs[:6]
        w_ref, dh_ref, x_ref, sc_ref, gx_ref, vec_ref = refs[6:]

        @pl.when(pl.program_id(0) == 0)
        def _():
            vec_ref[...] = jnp.zeros_like(vec_ref)

        du = jnp.zeros((tm, D), F32)
        for (off, w), ref in zip(PIECES, piece_refs):
            du = du + _dot(ref[...], w_ref[off:off + w, :])
        xv = x_ref[...]
        gx_ref[...] = ALPHA * dh_ref[...] + du * (1.0 + sc_ref[...])
        vec_ref[0:1, :] += _colsum(du)
        vec_ref[1:2, :] += _colsum(du * xv)

    row = lambda w: pl.BlockSpec((tm, w), lambda i: (i, 0))
    return pl.pallas_call(
        body, name="grad_x", grid=(S // tm,),
        out_shape=(jax.ShapeDtypeStruct((S, D), F32), jax.ShapeDtypeStruct((8, D), F32)),
        in_specs=[row(w) for _, w in PIECES]
        + [pl.BlockSpec(wt_pad.shape, lambda i: (0, 0)), row(D), row(D), pl.BlockSpec((1, D), lambda i: (0, 0))],
        out_specs=(row(D), pl.BlockSpec((8, D), lambda i: (0, 0))),
        compiler_params=_params(dimension_semantics=("arbitrary",)),
    )(*pieces, wt_pad, dh, x, scale)


def _adamw_math(w, g, m, v):
    m = ADAM_B1 * m + (1.0 - ADAM_B1) * g
    v = ADAM_B2 * v + (1.0 - ADAM_B2) * (g * g)
    m_hat = m / (1.0 - ADAM_B1 ** ADAM_STEP)
    v_hat = v / (1.0 - ADAM_B2 ** ADAM_STEP)
    delta = -ADAM_LR * (m_hat / (jnp.sqrt(v_hat) + ADAM_EPS) + ADAM_WD * w)
    return delta, m, v


def _adamw(groups, n_steps):
    n = len(groups)

    def body(*refs):
        ins, outs = refs[:4 * n], refs[4 * n:]
        for t in range(n):
            w, g, m, v = (r[...] for r in ins[4 * t:4 * t + 4])
            d, m2, v2 = _adamw_math(w, g, m, v)
            outs[4 * t][...] = d
            outs[4 * t + 1][...] = m2
            outs[4 * t + 2][...] = v2
            outs[4 * t + 3][...] = g

    in_specs, out_specs, out_shape, args = [], [], [], []
    for (w, g, m, v) in groups:
        rest = w.shape[1:]
        spec = pl.BlockSpec((w.shape[0] // n_steps,) + rest, lambda i, nd=len(rest): (i,) + (0,) * nd)
        in_specs += [spec] * 4
        out_specs += [spec] * 4
        out_shape += [jax.ShapeDtypeStruct(w.shape, F32)] * 4
        args += [w, g, m, v]
    return pl.pallas_call(
        body, name="adamw_%d_%d" % (n, n_steps), grid=(n_steps,),
        out_shape=tuple(out_shape), in_specs=in_specs, out_specs=tuple(out_specs),
        compiler_params=_params(dimension_semantics=("arbitrary",)),
    )(*args)


def _adamw_small(small_sum, g_b_ada, params):
    n = len(params)

    def body(gs_ref, gba_ref, *refs):
        ins, outs = refs[:3 * n], refs[3 * n:]
        for t, (name, w0, _, _) in enumerate(params):
            w_ref, m_ref, v_ref = ins[3 * t:3 * t + 3]
            first = SMALL_SEGS[name][0] if name in SMALL_SEGS else None
            if w0.shape[0] > 1:
                pieces = [((slice(None), slice(None)), gs_ref[first:first + w0.shape[0], :])]
            else:
                pieces = []
                for r in range(-(-w0.shape[1] // 128)):
                    lanes = slice(128 * r, min(128 * r + 128, w0.shape[1]))
                    g = gba_ref[0:1, lanes] if first is None else gs_ref[first + r:first + r + 1, 0:lanes.stop - lanes.start]
                    pieces.append(((slice(0, 1), lanes), g))
            for where, g in pieces:
                d, m2, v2 = _adamw_math(w_ref[where], g, m_ref[where], v_ref[where])
                for ref, val in zip(outs[4 * t:4 * t + 4], (g, d, m2, v2)):
                    ref[where] = val

    vm = pl.BlockSpec(memory_space=pltpu.VMEM)
    args = [small_sum, g_b_ada]
    out_shape = []
    for _, w, m, v in params:
        args += [w, m, v]
        out_shape += [jax.ShapeDtypeStruct(w.shape, F32)] * 4
    return pl.pallas_call(
        body, name="adamw_small",
        out_shape=tuple(out_shape), in_specs=[vm] * len(args), out_specs=(vm,) * len(out_shape),
        compiler_params=_params(),
    )(*args)


def _pack_small(parts):
    rows = []
    used = 0
    for name, (first, n_rows) in SMALL_SEGS.items():
        if first > used:
            rows.append(jnp.zeros((first - used, 128), F32))
        flat = parts[name].reshape(-1)
        flat = jnp.pad(flat, (0, n_rows * 128 - flat.shape[0]))
        rows.append(flat.reshape(n_rows, 128))
        used = first + n_rows
    rows.append(jnp.zeros((SMALL_ROWS - used, 128), F32))
    return jnp.concatenate(rows, axis=0)


def _pad_in(v):
    r = v.shape[0]
    z = jnp.zeros((r, O_P - O_F - N_HEADS), v.dtype)
    return jnp.concatenate([v[:, :3 * D_ATT + N_HEADS], z, v[:, 3 * D_ATT + N_HEADS:]], axis=1)


def _unpad_in(v):
    return jnp.concatenate([v[:, :O_F + N_HEADS], v[:, O_P:]], axis=1)


def _shards_in(v):
    gap = O_P - (O_F + N_HEADS)
    parts = []
    for a in range(N_CHIPS):
        lo, hi = a * SHARD_IN, (a + 1) * SHARD_IN
        cut = O_F + N_HEADS
        if hi <= cut:
            parts.append(v[:, lo:hi])
        elif lo >= cut:
            parts.append(v[:, lo + gap:hi + gap])
        else:
            parts.append(jnp.concatenate([v[:, lo:cut], v[:, cut + gap:hi + gap]], axis=1))
    return jnp.stack(parts, axis=0)


def kernel(x, c, w_ada, b_ada, w_in, b_in, w_pool_mix, b_pool_mix, pool_scale, w_out, b_out, ln_g, ln_b, loss_target, m_w_ada, m_b_ada, m_w_in, m_b_in, m_w_pool_mix, m_b_pool_mix, m_pool_scale, m_w_out, m_b_out, m_ln_g, m_ln_b, v_w_ada, v_b_ada, v_w_in, v_b_in, v_w_pool_mix, v_b_pool_mix, v_pool_scale, v_w_out, v_b_out, v_ln_g, v_ln_b):
    S = x.shape[1]
    T = min(T_ATT, S)
    n_t = S // T
    x2 = x[0]
    tgt = loss_target[0]
    q_scale = jnp.concatenate([jnp.full((1, D_ATT), Q_SCALE, F32), jnp.ones((1, D_PAD - D_ATT), F32)], axis=1)

    to_cols = lambda a: jnp.transpose(a, (2, 0, 1))
    from_cols = lambda a: jnp.transpose(a, (1, 2, 0))
    c_all, shift, scale, gate, wt_pad = _gather_and_ada(
        c, w_ada[0], b_ada.reshape(4, 1, SHARD_ADA), to_cols(w_in))
    b_pad = _pad_in(b_in) * q_scale

    u, qkv, f, p, g, w_out_all = _in_proj(x2, shift, scale, wt_pad, b_pad, w_out[0])
    w_out_full = w_out_all.reshape(D, D)
    big_f = _forget_cumsum(f)
    att, lse = _attention_fwd(qkv, big_f)

    dh, datt, dg, dpl, gw_out, gw_mix, vec, loss_part = _middle(
        x2, tgt, att, g, p, gate, w_pool_mix[0], b_pool_mix.reshape(1, D_POOL), pool_scale, w_out_full, b_out, ln_g, ln_b)
    dq, dk, dv, cs_att, dfk, dfq, g_w_out = _attention_bwd(
        qkv, datt, att, lse, big_f, gw_out.reshape(N_CHIPS, SHARD_OUT, D))
    dp, df, cs_tail = _tail(dpl, dfk, dfq, f)
    pieces = (dq, dk, dv, df, dp, dg)
    gw_pad = _grad_w_in(u, pieces)
    grad_x, vec_x = _grad_x(pieces, wt_pad, dh, x2, scale)

    cs_qkv = jnp.transpose(cs_att.reshape(N_PAIR, 3, 128), (1, 0, 2)).reshape(1, 3 * D_ATT)
    gb_pad = jnp.concatenate([cs_qkv, cs_tail[1:2, 0:128], cs_tail[0:1, :], vec[4:5, :]], axis=1) * q_scale
    dada = jnp.concatenate([vec_x[0:1, :], vec_x[1:2, :], vec[2:3, :]], axis=1)
    small = _pack_small({
        "b_in": _unpad_in(gb_pad), "w_pool_mix": gw_mix, "b_pool_mix": vec[6:7, :D_POOL],
        "pool_scale": vec[5:6, :D_POOL], "b_out": vec[3:4, :], "ln_g": vec[0:1, :], "ln_b": vec[1:2, :],
        "loss": loss_part})

    g_w_in, small_sum, g_w_ada, g_b_ada, loss = _reduce_all(
        gw_pad, _shards_in(q_scale), small, dada, c_all)

    big = _adamw([(w_ada[0], g_w_ada, m_w_ada[0], v_w_ada[0]),
                  (w_out[0], g_w_out, m_w_out[0], v_w_out[0])], 4)
    big_in = _adamw([(to_cols(w_in), g_w_in, to_cols(m_w_in), to_cols(v_w_in))], 5)
    tiles = lambda a: a.reshape(4 * POOL_GROUP, POOL_GROUP)
    flat = lambda a: a.reshape(1, D_POOL)
    small_params = [("b_ada", b_ada, m_b_ada, v_b_ada), ("b_in", b_in, m_b_in, v_b_in),
                    ("w_pool_mix", tiles(w_pool_mix), tiles(m_w_pool_mix), tiles(v_w_pool_mix)),
                    ("b_pool_mix", flat(b_pool_mix), flat(m_b_pool_mix), flat(v_b_pool_mix)),
                    ("pool_scale", pool_scale, m_pool_scale, v_pool_scale), ("b_out", b_out, m_b_out, v_b_out),
                    ("ln_g", ln_g, m_ln_g, v_ln_g), ("ln_b", ln_b, m_ln_b, v_ln_b)]
    sm = _adamw_small(small_sum, g_b_ada, small_params)
    sm_idx = {p[0]: n for n, p in enumerate(small_params)}
    shapes = {"w_pool_mix": (1, 4, POOL_GROUP, POOL_GROUP), "b_pool_mix": (1, 4, POOL_GROUP)}

    names = ["w_ada", "b_ada", "w_in", "b_in", "w_pool_mix", "b_pool_mix", "pool_scale", "w_out", "b_out",
             "ln_g", "ln_b"]
    big_idx = {"w_ada": 0, "w_out": 1}

    def leaf(kind, name):
        if name == "w_in":
            return from_cols(big_in[(kind - 1) % 4])
        if name in big_idx:
            return big[4 * big_idx[name] + (kind - 1) % 4][None]
        val = sm[4 * sm_idx[name] + kind]
        return val.reshape(shapes[name]) if name in shapes else val

    outs = [loss.reshape(()), grad_x[None]]
    for kind in range(4):
        outs += [leaf(kind, n) for n in names]
    return tuple(outs)
```

```python
import functools

import numpy as np
import jax
import jax.numpy as jnp
from jax import lax
from jax.experimental import pallas as pl
from jax.experimental.pallas import tpu as pltpu

F32 = jnp.float32
BF16 = jnp.bfloat16
MESH = pl.DeviceIdType.MESH

D = 1024
D_ATT = 512
D_POOL = 512
N_HEADS = 8
HEAD_DIM = 64
N_PAIR = N_HEADS // 2
POOL_WINDOWS = (2, 4, 8, 16)
POOL_GROUP = 128
POOL_HALO = 16
LN_EPS = 1e-5
ALPHA = 2.0 ** 0.25
D_IN = 3 * D_ATT + N_HEADS + D_POOL + D_ATT + D_POOL
N_CHIPS = 4
SHARD_IN = D_IN // N_CHIPS
SHARD_ADA = 3 * D // N_CHIPS
SHARD_OUT = D // N_CHIPS

O_QKV, O_F, O_P, O_G, D_PAD = 0, 1536, 1664, 2176, 3200
Q_SCALE = HEAD_DIM ** -0.5

ADAM_LR, ADAM_B1, ADAM_B2, ADAM_EPS, ADAM_WD, ADAM_STEP = 0.001, 0.9, 0.999, 1e-08, 0.01, 10

NEG = -1e30

VMEM_LIMIT = 56 * 1024 * 1024

TM_PROJ = 512
T_ATT = 512
TM_MID = 512
TM_GW = 1024
TM_DU = 512

REL7 = [(0, 0, 1), (0, 1, 0), (0, 1, 1), (1, 0, 0), (1, 0, 1), (1, 1, 0), (1, 1, 1)]
REL3 = [(0, 1), (1, 0), (1, 1)]

SMALL_SEGS = {}
_row = 0
for _name, _n in (("b_in", D_IN), ("w_pool_mix", 65536), ("b_pool_mix", 512), ("pool_scale", 512),
                  ("b_out", 1024), ("ln_g", 1024), ("ln_b", 1024), ("loss", 1)):
    _rows = -(-_n // 1024) * 8
    SMALL_SEGS[_name] = (_row, _rows)
    _row += _rows
SMALL_ROWS = -(-_row // 16) * 16


def _params(**kw):
    return pltpu.CompilerParams(vmem_limit_bytes=VMEM_LIMIT, **kw)


def _flip(v, d):
    return v if d == 0 else 1 - v


def _dot(a, b):
    return jnp.dot(a, b, preferred_element_type=F32)


def _dot_nt(a, b):
    return lax.dot_general(a, b, (((1,), (1,)), ((), ())), preferred_element_type=F32)


def _dot_tn(a, b):
    return lax.dot_general(a, b, (((0,), (0,)), ((), ())), preferred_element_type=F32)


def _sigmoid(v):
    return 1.0 / (1.0 + jnp.exp(-v))


def _colsum(v):
    return jnp.sum(v, axis=0, keepdims=True)


def _gather_stages(pos, src_ref, dst_ref, half, own_sem, s_sem, r_sem, fs_sem, fr_sem):
    x, y, cc, chip, sib = pos
    own = pltpu.make_async_copy(src_ref, dst_ref.at[chip], own_sem)
    first, landed, others = [], [], []
    for k, (dx, dy) in enumerate(REL3):
        px, py = _flip(x, dx), _flip(y, dy)
        first.append(pltpu.make_async_remote_copy(
            src_ref=src_ref.at[half(cc)], dst_ref=dst_ref.at[(chip,) + half(cc)],
            send_sem=s_sem.at[k], recv_sem=r_sem.at[k], device_id=(px, py, cc), device_id_type=MESH))
        landed.append(dst_ref.at[(2 * px + py,) + half(cc)])
        others.append(dst_ref.at[(2 * px + py,) + half(1 - cc)])
    passed = [pltpu.make_async_remote_copy(src_ref=landed[k], dst_ref=landed[k], send_sem=fs_sem.at[k],
                                           recv_sem=fr_sem.at[k], device_id=sib, device_id_type=MESH)
              for k in range(3)]

    def start(finish_src=None):
        for cp in first:
            cp.start()
        if finish_src is not None:
            finish_src()
        own.start()

    def forward():
        for k in range(3):
            pltpu.make_async_remote_copy(src_ref=landed[k], dst_ref=landed[k], send_sem=s_sem.at[k],
                                         recv_sem=r_sem.at[k], device_id=sib, device_id_type=MESH).wait_recv()
            passed[k].start()

    def finish():
        for k in range(3):
            pltpu.make_async_remote_copy(src_ref=others[k], dst_ref=others[k], send_sem=fs_sem.at[k],
                                         recv_sem=fr_sem.at[k], device_id=sib, device_id_type=MESH).wait_recv()
        for cp in first + passed:
            cp.wait_send()
        own.wait()

    return start, forward, finish


def _gather_scratch():
    return [pltpu.SemaphoreType.DMA, pltpu.SemaphoreType.DMA((3,)), pltpu.SemaphoreType.DMA((3,)),
            pltpu.SemaphoreType.DMA((3,)), pltpu.SemaphoreType.DMA((3,))]


def _gather_and_ada(c, w_ada, b_ada4, w_in_sh):
    def body(c_ref, w_ref, b_ref, win_ref, call_ref, shift_ref, scale_ref, gate_ref, wt_pad_ref,
             win_all, win_bf, ada_ref, cslab, sbuf, rbuf, cs_sem, cr_sem, as_sem, ar_sem, *gather_sems):
        x, y, cc = lax.axis_index("x"), lax.axis_index("y"), lax.axis_index("c")
        me = 4 * x + 2 * y + cc
        chip = 2 * x + y
        lane_half = lambda which: (slice(None), pl.ds(pl.multiple_of(which * (D // 2), D // 2), D // 2))
        def round_half(which):
            for h in range(2):
                @pl.when(which == h)
                def _():
                    lanes = slice(h * (D // 2), (h + 1) * (D // 2))
                    win_bf[:, lanes] = win_ref[:, 0, lanes].astype(BF16)

        start, forward, finish = _gather_stages((x, y, cc, chip, (x, y, 1 - cc)), win_bf, win_all, lane_half,
                                                *gather_sems)
        round_half(cc)
        start(lambda: round_half(1 - cc))

        cslab[...] = jnp.broadcast_to(c_ref[...], (8, D))
        call_ref[me] = cslab[...]
        gathers = []
        for k, (dx, dy, dc) in enumerate(REL7):
            cp = pltpu.make_async_remote_copy(
                src_ref=cslab, dst_ref=call_ref.at[me], send_sem=cs_sem.at[k], recv_sem=cr_sem.at[k],
                device_id=(_flip(x, dx), _flip(y, dy), _flip(cc, dc)), device_id_type=MESH)
            cp.start()
            gathers.append(cp)
        for cp in gathers:
            cp.wait()
        slab_row = lax.broadcasted_iota(jnp.int32, (8, 1), 0)
        mat = jnp.zeros((8, D), F32)
        for r in range(8):
            mat = jnp.where(slab_row == r, call_ref[r], mat)
        act = (mat * _sigmoid(mat)).astype(BF16)
        part = _dot(act, w_ref[...].astype(BF16))
        sends = []
        for k, (dx, dy) in enumerate(REL3):
            px, py = _flip(x, dx), _flip(y, dy)
            r = 4 * px + 2 * py + cc
            piece = _colsum(jnp.where(slab_row == r, part, 0.0))
            sbuf[k] = jnp.broadcast_to(piece, (8, SHARD_ADA))
            cp = pltpu.make_async_remote_copy(
                src_ref=sbuf.at[k], dst_ref=rbuf.at[k], send_sem=as_sem.at[k], recv_sem=ar_sem.at[k],
                device_id=(px, py, cc), device_id_type=MESH)
            cp.start()
            sends.append(cp)
        own_piece = _colsum(jnp.where(slab_row == me, part, 0.0))
        ada_ref[chip] = jnp.broadcast_to(own_piece, (8, SHARD_ADA)) + b_ref[chip]
        for k, (dx, dy) in enumerate(REL3):
            sends[k].wait()
            a = 2 * _flip(x, dx) + _flip(y, dy)
            ada_ref[a] = rbuf[k] + b_ref[a]
        ada = jnp.concatenate([ada_ref[a][0:1, :] for a in range(N_CHIPS)], axis=1)
        shift_ref[...] = ada[:, 0:D]
        scale_ref[...] = ada[:, D:2 * D]
        gate_ref[...] = ada[:, 2 * D:3 * D]

        forward()
        finish()
        n_real = 3 * D_ATT + N_HEADS
        for a in range(N_CHIPS):
            lo, hi = a * SHARD_IN, (a + 1) * SHARD_IN
            for s0, s1 in ((lo, min(hi, D_ATT)), (max(lo, D_ATT), min(hi, n_real)), (max(lo, n_real), hi)):
                if s0 < s1:
                    rows = win_all[a, s0 - lo:s1 - lo, :]
                    if s1 <= D_ATT:
                        rows = rows * jnp.asarray(Q_SCALE, BF16)
                    shift = O_P - n_real if s0 >= n_real else 0
                    wt_pad_ref[s0 + shift:s1 + shift, :] = rows
        wt_pad_ref[n_real:O_P, :] = jnp.zeros((O_P - n_real, D), BF16)

    vm = pl.BlockSpec(memory_space=pltpu.VMEM)
    return pl.pallas_call(
        body, name="gather_and_ada",
        out_shape=(jax.ShapeDtypeStruct((8, 8, D), F32),) + (jax.ShapeDtypeStruct((1, D), F32),) * 3
        + (jax.ShapeDtypeStruct((D_PAD, D), BF16),),
        in_specs=[vm] * 4, out_specs=(vm,) * 5,
        scratch_shapes=[pltpu.VMEM((N_CHIPS, SHARD_IN, D), BF16), pltpu.VMEM((SHARD_IN, D), BF16),
                        pltpu.VMEM((N_CHIPS, 8, SHARD_ADA), F32), pltpu.VMEM((8, D), F32), pltpu.VMEM((3, 8, SHARD_ADA), F32),
                        pltpu.VMEM((3, 8, SHARD_ADA), F32),
                        pltpu.SemaphoreType.DMA((7,)), pltpu.SemaphoreType.DMA((7,)),
                        pltpu.SemaphoreType.DMA((3,)), pltpu.SemaphoreType.DMA((3,))] + _gather_scratch(),
        compiler_params=_params(),
    )(c, w_ada, b_ada4, w_in_sh)


def _shard_cols():
    cut, gap = O_F + N_HEADS, O_P - (O_F + N_HEADS)
    out = []
    for a in range(N_CHIPS):
        lo, hi = a * SHARD_IN, (a + 1) * SHARD_IN
        out.append(([(lo, min(hi, cut))] if lo < cut else []) + ([(max(lo, cut) + gap, hi + gap)] if hi > cut else []))
    return out


def _scatter_stages(pos, g_ref, sc_ref, out_ref, sib_buf, send_buf, ici_buf, sem1, sem2s, sem2r, sem3, part=(0, 1),
                    cols=None, own_buf=None):
    x, y, cc, chip, sib = pos
    q, n_parts = part
    RH = (g_ref.shape[1] if cols is None else g_ref.shape[0]) // 2 // n_parts
    mine = pl.ds(pl.multiple_of((cc * n_parts + q) * RH, RH), RH)
    theirs = pl.ds(pl.multiple_of(((1 - cc) * n_parts + q) * RH, RH), RH)
    cp1 = pltpu.make_async_remote_copy(
        src_ref=g_ref.at[:, theirs, :] if cols is None else g_ref.at[theirs, :], dst_ref=sib_buf,
        send_sem=sem1.at[0], recv_sem=sem1.at[1], device_id=sib, device_id_type=MESH)
    sends = []
    for k, (dx, dy) in enumerate(REL3):
        px, py = _flip(x, dx), _flip(y, dy)
        sends.append(pltpu.make_async_remote_copy(
            src_ref=send_buf.at[2 * px + py], dst_ref=ici_buf.at[chip],
            send_sem=sem2s.at[k], recv_sem=sem2r.at[k], device_id=(px, py, cc), device_id_type=MESH))
    cp3 = pltpu.make_async_remote_copy(
        src_ref=out_ref.at[mine, :], dst_ref=out_ref.at[mine, :], send_sem=sem3.at[0], recv_sem=sem3.at[1],
        device_id=sib, device_id_type=MESH)

    def finish1():
        cp1.wait()
        if cols is None:
            for a in range(N_CHIPS):
                both = g_ref[a, mine, :] + sib_buf[a]
                sib_buf[a] = both
                send_buf[a] = both.astype(BF16)
        else:
            both = g_ref[mine, :] + sib_buf[...]
            for a, pieces in enumerate(cols):
                at = 0
                for lo, hi in pieces:
                    own_buf[a, :, at:at + hi - lo] = both[:, lo:hi]
                    send_buf[a, :, at:at + hi - lo] = both[:, lo:hi].astype(BF16)
                    at += hi - lo

    def start2():
        for cp in sends:
            cp.start()
        ici_buf[chip] = send_buf[chip]

    def finish2():
        for cp in sends:
            cp.wait()
        own = (sib_buf if cols is None else own_buf)[chip]
        parts = [jnp.where(chip == a, own, ici_buf[a].astype(F32)) for a in range(N_CHIPS)]
        total = (parts[0] + parts[1]) + (parts[2] + parts[3])
        out_ref[mine, 0:own.shape[1]] = total if sc_ref is None else total * sc_ref[chip]

    return [(cp1.start, finish1), (start2, finish2), (cp3.start, cp3.wait)]


def _all_reduce_stages(pos, g_ref, out_ref, sib_buf, ici_buf, sem1, sem2s, sem2r, sem3):
    x, y, cc, chip, sib = pos
    RH = g_ref.shape[0] // 2
    mine = pl.ds(pl.multiple_of(cc * RH, 8), RH)
    theirs = pl.ds(pl.multiple_of((1 - cc) * RH, 8), RH)
    cp1 = pltpu.make_async_remote_copy(
        src_ref=g_ref.at[theirs, :], dst_ref=sib_buf, send_sem=sem1.at[0], recv_sem=sem1.at[1],
        device_id=sib, device_id_type=MESH)
    sends = []
    for k, (dx, dy) in enumerate(REL3):
        px, py = _flip(x, dx), _flip(y, dy)
        sends.append(pltpu.make_async_remote_copy(
            src_ref=sib_buf, dst_ref=ici_buf.at[chip],
            send_sem=sem2s.at[k], recv_sem=sem2r.at[k], device_id=(px, py, cc), device_id_type=MESH))
    cp3 = pltpu.make_async_remote_copy(
        src_ref=out_ref.at[mine, :], dst_ref=out_ref.at[mine, :], send_sem=sem3.at[0], recv_sem=sem3.at[1],
        device_id=sib, device_id_type=MESH)

    def finish1():
        cp1.wait()
        sib_buf[...] = g_ref[mine, :] + sib_buf[...]

    def start2():
        for cp in sends:
            cp.start()
        ici_buf[chip] = sib_buf[...]

    def finish2():
        for cp in sends:
            cp.wait()
        out_ref[mine, :] = (ici_buf[0] + ici_buf[1]) + (ici_buf[2] + ici_buf[3])

    return [(cp1.start, finish1), (start2, finish2), (cp3.start, cp3.wait)]


def _stage_sems():
    return [pltpu.SemaphoreType.DMA((2,)), pltpu.SemaphoreType.DMA((3,)),
            pltpu.SemaphoreType.DMA((3,)), pltpu.SemaphoreType.DMA((2,))]


def _scatter_scratch(r, c):
    return [pltpu.VMEM((N_CHIPS, r // 2, c), F32), pltpu.VMEM((N_CHIPS, r // 2, c), BF16),
            pltpu.VMEM((N_CHIPS, r // 2, c), BF16)] + _stage_sems()


def _reduce_all(gw_pad, sc_in, small, dada, c_all):
    R = small.shape[0]
    W = dada.shape[1]
    r_in, p_in = gw_pad.shape
    c_in = SHARD_IN
    chunk = r_in // 4

    def chunk_scratch():
        return ([pltpu.VMEM((chunk, p_in), F32), pltpu.VMEM((N_CHIPS, chunk, c_in), BF16),
                 pltpu.VMEM((N_CHIPS, chunk, c_in), BF16)] + _stage_sems()
                + [pltpu.VMEM((N_CHIPS, chunk, c_in), F32)])

    n_in = len(chunk_scratch())

    c_wide = -(-c_in // 128) * 128

    def body(gin_ref, scin_ref, sm_ref, d_ref, c_ref, ocols_ref, osm_ref, gwa_ref, gba_ref, loss_ref, oin_ref,
             dall_ref, *scratch):
        x, y, cc = lax.axis_index("x"), lax.axis_index("y"), lax.axis_index("c")
        me = 4 * x + 2 * y + cc
        chip = 2 * x + y
        pos = (x, y, cc, chip, (x, y, 1 - cc))
        oin_ref[:, c_in:c_wide] = jnp.zeros((r_in, c_wide - c_in), F32)
        dslab, ds_sem, dr_sem = scratch[0:3]
        a_bufs, b_bufs, sm_bufs = scratch[3:3 + n_in], scratch[3 + n_in:3 + 2 * n_in], scratch[3 + 2 * n_in:]
        dslab[...] = jnp.broadcast_to(d_ref[...], (8, W))
        dall_ref[me] = dslab[...]
        gathers = []
        for k, (dx, dy, dc) in enumerate(REL7):
            cp = pltpu.make_async_remote_copy(
                src_ref=dslab, dst_ref=dall_ref.at[me], send_sem=ds_sem.at[k], recv_sem=dr_sem.at[k],
                device_id=(_flip(x, dx), _flip(y, dy), _flip(cc, dc)), device_id_type=MESH)
            cp.start()
            gathers.append(cp)
        cols = _shard_cols()
        first = _scatter_stages(pos, gin_ref, scin_ref, oin_ref, *a_bufs[:-1], part=(0, 2), cols=cols,
                                own_buf=a_bufs[-1])
        second = _scatter_stages(pos, gin_ref, scin_ref, oin_ref, *b_bufs[:-1], part=(1, 2), cols=cols,
                                 own_buf=b_bufs[-1])
        little = _all_reduce_stages(pos, sm_ref, osm_ref, *sm_bufs)
        for plan in (first, second, little):
            plan[0][0]()
        first[0][1]()
        first[1][0]()
        little[0][1]()
        little[1][0]()
        second[0][1]()
        second[1][0]()
        for cp in gathers:
            cp.wait()
        slab_row = lax.broadcasted_iota(jnp.int32, (8, 1), 0)
        cm = jnp.zeros((8, D), F32)
        dm = jnp.zeros((8, W), F32)
        for r in range(8):
            cm = jnp.where(slab_row == r, c_ref[r], cm)
            dm = jnp.where(slab_row == r, dall_ref[r], dm)
        act = cm * _sigmoid(cm)
        dcol = dm[:, 0:SHARD_ADA]
        for a in range(1, N_CHIPS):
            dcol = jnp.where(chip == a, dm[:, a * SHARD_ADA:(a + 1) * SHARD_ADA], dcol)
        lhs = jnp.concatenate([act, jnp.zeros((8, D), F32)], axis=0).astype(BF16)
        rhs = jnp.concatenate([dcol, jnp.zeros((8, SHARD_ADA), F32)], axis=0).astype(BF16)
        gwa_ref[...] = _dot_tn(lhs, rhs)
        gba_ref[...] = _colsum(dm)
        first[1][1]()
        first[2][0]()
        second[1][1]()
        second[2][0]()
        little[1][1]()
        little[2][0]()
        for plan in (first, second, little):
            plan[2][1]()
        loss_row = SMALL_SEGS["loss"][0]
        loss_ref[...] = osm_ref[loss_row:loss_row + 1, 0:1]
        ocols_ref[...] = oin_ref[...].T[0:c_in, :][:, None, :]

    scratch = [pltpu.VMEM((r_in, c_wide), F32), pltpu.VMEM((8, 8, W), F32),
               pltpu.VMEM((8, W), F32), pltpu.SemaphoreType.DMA((7,)), pltpu.SemaphoreType.DMA((7,))]
    scratch += chunk_scratch() + chunk_scratch()
    scratch += [pltpu.VMEM((R // 2, 128), F32), pltpu.VMEM((N_CHIPS, R // 2, 128), F32)] + _stage_sems()
    vm = pl.BlockSpec(memory_space=pltpu.VMEM)
    return pl.pallas_call(
        body, name="reduce_all",
        out_shape=(jax.ShapeDtypeStruct((c_in, 1, r_in), F32), jax.ShapeDtypeStruct((R, 128), F32),
                   jax.ShapeDtypeStruct((D, SHARD_ADA), F32), jax.ShapeDtypeStruct((1, W), F32),
                   jax.ShapeDtypeStruct((1, 1), F32)),
        in_specs=[vm] * 5, out_specs=(vm,) * 5,
        scratch_shapes=scratch,
        compiler_params=_params(),
    )(gw_pad, sc_in, small, dada, c_all)


def _in_proj(x, shift, scale, wt_pad, b_pad, w_out_sh):
    S = x.shape[0]
    tm = min(TM_PROJ, S)
    n_steps = S // tm
    assert n_steps >= 3

    def body(x_ref, sh_ref, sc_ref, w_ref, b_ref, wo_ref, u_ref, qkv_ref, f_ref, p_ref, g_ref, wo_all,
             wo_buf, wo_bf, *gather_sems):
        i = pl.program_id(0)
        xx, yy, cc = lax.axis_index("x"), lax.axis_index("y"), lax.axis_index("c")
        row_half = lambda which: (pl.ds(pl.multiple_of(which * (SHARD_OUT // 2), SHARD_OUT // 2), SHARD_OUT // 2),
                                  slice(None))
        start, forward, finish = _gather_stages((xx, yy, cc, 2 * xx + yy, (xx, yy, 1 - cc)), wo_bf, wo_buf,
                                                row_half, *gather_sems)

        @pl.when(i == 0)
        def _():
            wo_bf[...] = wo_ref[...].astype(BF16)
            start()

        pl.when(i == n_steps // 2)(forward)

        @pl.when(i == n_steps - 1)
        def _():
            finish()
            wo_all[...] = wo_buf[...]

        u = (x_ref[...] * (1.0 + sc_ref[...]) + sh_ref[...]).astype(BF16)
        u_ref[...] = u
        qkv_ref[...] = (_dot_nt(u, w_ref[O_QKV:O_F, :]) + b_ref[:, O_QKV:O_F]).astype(BF16)
        f_ref[...] = _dot_nt(u, w_ref[O_F:O_P, :]) + b_ref[:, O_F:O_P]
        p_ref[...] = _dot_nt(u, w_ref[O_P:O_G, :]) + b_ref[:, O_P:O_G]
        g_ref[...] = _dot_nt(u, w_ref[O_G:D_PAD, :]) + b_ref[:, O_G:D_PAD]

    row = lambda w: pl.BlockSpec((tm, w), lambda i: (i, 0))
    full = lambda a: pl.BlockSpec(a.shape, lambda i: (0, 0))
    vm = pl.BlockSpec(memory_space=pltpu.VMEM)
    return pl.pallas_call(
        body, name="in_proj", grid=(n_steps,),
        out_shape=(jax.ShapeDtypeStruct((S, D), BF16), jax.ShapeDtypeStruct((S, 3 * D_ATT), BF16),
                   jax.ShapeDtypeStruct((S, 128), F32), jax.ShapeDtypeStruct((S, D_POOL), F32),
                   jax.ShapeDtypeStruct((S, D), F32), jax.ShapeDtypeStruct((N_CHIPS,) + w_out_sh.shape, BF16)),
        in_specs=[row(D), full(shift), full(scale), full(wt_pad), full(b_pad), vm],
        out_specs=(row(D), row(3 * D_ATT), row(128), row(D_POOL), row(D), vm),
        scratch_shapes=[pltpu.VMEM((N_CHIPS,) + w_out_sh.shape, BF16), pltpu.VMEM(w_out_sh.shape, BF16)]
        + _gather_scratch(),
        compiler_params=_params(dimension_semantics=("arbitrary",)),
    )(x, shift, scale, wt_pad, b_pad, w_out_sh)


def _forget_cumsum(f):
    S = f.shape[0]
    tm = min(T_ATT, S)

    def body(f_ref, out_ref, carry):
        @pl.when(pl.program_id(0) == 0)
        def _():
            carry[...] = jnp.zeros_like(carry)
        v = f_ref[...]
        logf = jnp.minimum(v, 0.0) - jnp.log(1.0 + jnp.exp(-jnp.abs(v)))
        r = lax.broadcasted_iota(jnp.int32, (tm, tm), 0)
        c = lax.broadcasted_iota(jnp.int32, (tm, tm), 1)
        tri = (r <= c).astype(F32)
        rows8 = logf.T[0:8, :]
        cum8 = jnp.dot(rows8, tri, preferred_element_type=F32, precision=lax.Precision.HIGHEST) + carry[...]
        out_ref[...] = jnp.concatenate([cum8, jnp.zeros((128 - 8, tm), F32)], axis=0).T
        last = lax.broadcasted_iota(jnp.int32, (1, tm), 1) == tm - 1
        carry[...] = jnp.sum(jnp.where(last, cum8, 0.0), axis=1, keepdims=True)

    return pl.pallas_call(
        body, name="forget_cumsum", grid=(S // tm,),
        out_shape=jax.ShapeDtypeStruct((S, 128), F32),
        in_specs=[pl.BlockSpec((tm, 128), lambda i: (i, 0))],
        out_specs=pl.BlockSpec((tm, 128), lambda i: (i, 0)),
        scratch_shapes=[pltpu.VMEM((8, 1), F32)],
        compiler_params=_params(dimension_semantics=("arbitrary",)),
    )(f)


def _split3(v):
    hi = v.astype(BF16)
    rest = v - hi.astype(F32)
    mid = rest.astype(BF16)
    lo = (rest - mid.astype(F32)).astype(BF16)
    return hi, mid, lo


def _attention_fwd(qkv, big_f):
    S = qkv.shape[0]
    T = min(T_ATT, S)
    n_t = S // T

    def body(q_ref, k_ref, v_ref, f_ref, o_ref, lse_ref, kaug_sc, vt_sc, m_sc, l_sc, acc_sc):
        hp = pl.program_id(0)
        i = pl.program_id(1)
        lane = lax.broadcasted_iota(jnp.int32, (1, 128), 1)
        sub = lax.broadcasted_iota(jnp.int32, (128, 1), 0)
        head_sel = (lane < HEAD_DIM, lane >= HEAD_DIM)
        head_sel_t = (sub < HEAD_DIM, sub >= HEAD_DIM)
        spare = (HEAD_DIM, 0)
        zero = jnp.zeros((), BF16)

        @pl.when(i == 0)
        def _():
            def prep(jt, carry):
                rows = pl.ds(pl.multiple_of(jt * T, T), T)
                k = k_ref[rows, :]
                ft = f_ref[rows, :]
                vt = v_ref[rows, :].astype(F32).T
                for h in range(2):
                    fh = jnp.sum(jnp.where(lane == 2 * hp + h, ft, 0.0), axis=1, keepdims=True)
                    hi, mid, lo = _split3(-fh)
                    b = spare[h]
                    bias = jnp.where(lane == b, hi, jnp.where(lane == b + 1, mid, jnp.where(lane == b + 2, lo, zero)))
                    kaug_sc[h, rows, :] = jnp.where(head_sel[h], k, bias)
                    vt_sc[h, jt] = jnp.where(head_sel_t[h], vt, 0.0).astype(BF16)
                return carry

            lax.fori_loop(0, n_t, prep, 0)

        q = q_ref[...]
        q_heads = []
        for h in range(2):
            ones = jnp.where((lane >= spare[h]) & (lane < spare[h] + 3), jnp.ones((), BF16), zero)
            q_heads.append(jnp.where(head_sel[h], q, ones))
        m_sc[...] = jnp.full((8, T), NEG, F32)
        l_sc[...] = jnp.zeros((8, T), F32)
        acc_sc[...] = jnp.zeros((128, T), F32)

        def update(j, k_lo, n_k, q_lo, masked):
            rows = pl.ds(pl.multiple_of(j * T + k_lo, n_k), n_k)
            n_q = T - q_lo
            alphas, pvs = [], []
            for h in range(2):
                s_t = _dot_nt(kaug_sc[h, rows, :], q_heads[h][q_lo:, :])
                if masked:
                    rr = lax.broadcasted_iota(jnp.int32, (n_k, n_q), 0) + k_lo
                    cc = lax.broadcasted_iota(jnp.int32, (n_k, n_q), 1) + q_lo
                    s_t = jnp.where(rr <= cc, s_t, NEG)
                m_prev = m_sc[h:h + 1, q_lo:]
                m_new = jnp.maximum(m_prev, jnp.max(s_t, axis=0, keepdims=True))
                alpha = jnp.exp(m_prev - m_new)
                p_t = jnp.exp(s_t - m_new)
                l_sc[h:h + 1, q_lo:] = alpha * l_sc[h:h + 1, q_lo:] + jnp.sum(p_t, axis=0, keepdims=True)
                m_sc[h:h + 1, q_lo:] = m_new
                alphas.append(alpha)
                pvs.append(_dot(vt_sc[h, j, :, k_lo:k_lo + n_k], p_t.astype(BF16)))
            acc_sc[:, q_lo:] = (acc_sc[:, q_lo:] * jnp.where(head_sel_t[0], alphas[0], alphas[1])
                                + (pvs[0] + pvs[1]))

        def two_off_diagonal(jj, carry):
            update(2 * jj, 0, T, 0, False)
            update(2 * jj + 1, 0, T, 0, False)
            return carry

        lax.fori_loop(0, i // 2, two_off_diagonal, 0)

        @pl.when(i % 2 == 1)
        def _():
            update(i - 1, 0, T, 0, False)
            update(i, 0, T, 0, True)

        @pl.when(i % 2 == 0)
        def _():
            update(i, 0, T, 0, True)

        l = l_sc[...]
        o_ref[...] = (acc_sc[...] / jnp.where(head_sel_t[0], l[0:1, :], l[1:2, :])).T
        is_head = lax.broadcasted_iota(jnp.int32, (8, 1), 0) < 2
        lse_ref[...] = jnp.where(is_head, m_sc[...] + jnp.log(jnp.where(is_head, l, 1.0)), 0.0)

    return pl.pallas_call(
        body, name="attention_fwd", grid=(N_PAIR, n_t),
        out_shape=(jax.ShapeDtypeStruct((S, D_ATT), F32), jax.ShapeDtypeStruct((N_PAIR, n_t, 8, T), F32)),
        in_specs=[pl.BlockSpec((T, 128), lambda hp, i: (i, hp)),
                  pl.BlockSpec((S, 128), lambda hp, i: (0, N_PAIR + hp)),
                  pl.BlockSpec((S, 128), lambda hp, i: (0, 2 * N_PAIR + hp)),
                  pl.BlockSpec((S, 128), lambda hp, i: (0, 0))],
        out_specs=(pl.BlockSpec((T, 128), lambda hp, i: (i, hp)),
                   pl.BlockSpec((None, None, 8, T), lambda hp, i: (hp, i, 0, 0))),
        scratch_shapes=[pltpu.VMEM((2, S, 128), BF16), pltpu.VMEM((2, n_t, 128, T), BF16),
                        pltpu.VMEM((8, T), F32), pltpu.VMEM((8, T), F32), pltpu.VMEM((128, T), F32)],
        compiler_params=_params(dimension_semantics=("arbitrary", "arbitrary")),
    )(qkv, qkv, qkv, big_f)


def _attention_bwd(qkv, datt, att, lse, big_f, gw_out4):
    S = qkv.shape[0]
    T = min(T_ATT, S)
    n_t = S // T
    n_steps = N_PAIR * n_t
    marks = (0, n_steps // 8, n_steps // 2, n_steps // 2 + n_steps // 8)

    def body(q_ref, do_ref, o_ref, lse_ref, k_ref, v_ref, fk_ref, gout_ref,
             dq_ref, dk_ref, dv_ref, cs_ref, dfk_ref, dfq_ref, oout_ref, stat_sc, dqt_sc, qaug_sc,
             out_buf, *red_bufs):
        hp = pl.program_id(0)
        j = pl.program_id(1)
        x, y, cc = lax.axis_index("x"), lax.axis_index("y"), lax.axis_index("c")
        plan = _scatter_stages((x, y, cc, 2 * x + y, (x, y, 1 - cc)), gout_ref, None, out_buf, *red_bufs)
        step = hp * n_t + j
        for n, mark in enumerate(marks):
            @pl.when(step == mark)
            def _(n=n):
                if n > 0:
                    plan[n - 1][1]()
                if n < 3:
                    plan[n][0]()
                else:
                    oout_ref[...] = out_buf[...]

        lane = lax.broadcasted_iota(jnp.int32, (1, 128), 1)
        sub = lax.broadcasted_iota(jnp.int32, (128, 1), 0)
        head_sel = (lane < HEAD_DIM, lane >= HEAD_DIM)
        head_sel_t = (sub < HEAD_DIM, sub >= HEAD_DIM)
        spare = (HEAD_DIM, 0)
        zero = jnp.zeros((), BF16)
        one = jnp.ones((), BF16)

        def bias_lanes(first, pieces):
            hi, mid, lo = pieces
            return lambda rest: jnp.where(lane == first, hi, jnp.where(lane == first + 1, mid,
                                                                        jnp.where(lane == first + 2, lo, rest)))

        @pl.when(j == 0)
        def _():
            dqt_sc[...] = jnp.zeros_like(dqt_sc)
            cs_ref[...] = jnp.zeros_like(cs_ref)
            dfq_ref[...] = jnp.zeros_like(dfq_ref)

            def prep(i, carry):
                rows = pl.ds(pl.multiple_of(i * T, T), T)
                q = q_ref[rows, :]
                do = do_ref[rows, :]
                prod = o_ref[rows, :] * do.astype(F32)
                d_a = jnp.sum(jnp.where(head_sel[0], prod, 0.0), axis=1, keepdims=True)
                d_b = jnp.sum(jnp.where(head_sel[0], 0.0, prod), axis=1, keepdims=True)
                delta_t = jnp.where(head_sel[0], d_a, d_b).T
                stat_sc[i, 0:1, :] = delta_t[0:1, :]
                stat_sc[i, 1:2, :] = delta_t[HEAD_DIM:HEAD_DIM + 1, :]
                lse = lse_ref[i]
                lse_cols = jnp.where(head_sel_t[0], lse[0:1, :], lse[1:2, :]).T
                for h in range(2):
                    neg_lse = -lse_cols[:, h * HEAD_DIM:h * HEAD_DIM + 1]
                    ones = jnp.where((lane >= spare[h]) & (lane < spare[h] + 3), one, zero)
                    qaug_sc[h, rows, :] = jnp.where(head_sel[h], q, bias_lanes(spare[h] + 3, _split3(neg_lse))(ones))
                return carry

            lax.fori_loop(0, n_t, prep, 0)

        k = k_ref[...]
        v = v_ref[...]
        fk = fk_ref[...]
        kt = k.astype(F32).T
        heads = []
        for h in range(2):
            fkh = jnp.sum(jnp.where(lane == 2 * hp + h, fk, 0.0), axis=1, keepdims=True)
            ones = jnp.where((lane >= spare[h] + 3) & (lane < spare[h] + 6), one, zero)
            kaug = jnp.where(head_sel[h], k, bias_lanes(spare[h], _split3(-fkh))(ones))
            heads.append((kaug, jnp.where(head_sel[h], v, zero), jnp.where(head_sel_t[h], kt, 0.0).astype(BF16)))

        def block(i, k_lo, n_k, q_lo, masked):
            n_q = T - q_lo
            rows = pl.ds(pl.multiple_of(i * T + q_lo, n_q), n_q)
            q = q_ref[rows, :]
            do = do_ref[rows, :]
            stat = stat_sc[i]
            dk = jnp.zeros((n_k, 128), F32)
            dv = jnp.zeros((n_k, 128), F32)
            dqt = jnp.zeros((128, n_q), F32)
            dfs = []
            for h in range(2):
                kaug, vh, kth = heads[h]
                arg = _dot_nt(kaug[k_lo:k_lo + n_k, :], qaug_sc[h, rows, :])
                if masked:
                    rr = lax.broadcasted_iota(jnp.int32, (n_k, n_q), 0) + k_lo
                    cc = lax.broadcasted_iota(jnp.int32, (n_k, n_q), 1) + q_lo
                    arg = jnp.where(rr <= cc, arg, NEG)
                p_t = jnp.exp(arg)
                ds_t = p_t * (_dot_nt(vh[k_lo:k_lo + n_k, :], do) - stat[h:h + 1, q_lo:])
                ds_bf = ds_t.astype(BF16)
                dv = dv + _dot(p_t.astype(BF16), jnp.where(head_sel[h], do, zero))
                dk = dk + _dot(ds_bf, jnp.where(head_sel[h], q, zero))
                dqt = dqt + _dot(kth[:, k_lo:k_lo + n_k], ds_bf)
                dfs.append(jnp.sum(ds_t, axis=1, keepdims=True))
                dfq_ref[i, h:h + 1, q_lo:] += _colsum(ds_t)
            dqt_sc[i, :, q_lo:] += dqt
            return dk, dv, dfs[0], dfs[1]

        def off_diagonal(i, acc):
            return tuple(a + b for a, b in zip(acc, block(i, 0, T, 0, False)))

        half = T // 2
        early = block(j, 0, half, 0, True)
        late = block(j, half, half, half, True)
        acc1 = tuple(jnp.concatenate([a, b], axis=0) for a, b in zip(early, late))
        n_off = n_t - 1 - j
        acc2 = lax.fori_loop(0, n_off // 2,
                             lambda ii, a: off_diagonal(j + 2 + 2 * ii, off_diagonal(j + 1 + 2 * ii, a)), acc1)
        dk_acc, dv_acc, dfa, dfb = lax.fori_loop(0, n_off % 2, lambda _, a: off_diagonal(n_t - 1, a), acc2)
        dk_ref[...] = dk_acc.astype(BF16)
        dv_ref[...] = dv_acc.astype(BF16)
        dfk_ref[...] = -jnp.where(lane == 0, dfa, jnp.where(lane == 1, dfb, 0.0))
        cs_ref[:, 128:256] = cs_ref[:, 128:256] + _colsum(dk_acc)
        cs_ref[:, 256:384] = cs_ref[:, 256:384] + _colsum(dv_acc)

        @pl.when(j == n_t - 1)
        def _():
            def finish(i, tot):
                dq = dqt_sc[i].T
                dq_ref[pl.ds(pl.multiple_of(i * T, T), T), :] = dq.astype(BF16)
                return tot + _colsum(dq)

            cs_ref[:, 0:128] = lax.fori_loop(0, n_t, finish, jnp.zeros((1, 128), F32))

    pair_rows = lambda hp, j: (hp, 0, 0)
    vm = pl.BlockSpec(memory_space=pltpu.VMEM)
    _, r_out, c_out = gw_out4.shape
    return pl.pallas_call(
        body, name="attention_bwd", grid=(N_PAIR, n_t),
        out_shape=(jax.ShapeDtypeStruct((S, D_ATT), BF16), jax.ShapeDtypeStruct((S, D_ATT), BF16),
                   jax.ShapeDtypeStruct((S, D_ATT), BF16), jax.ShapeDtypeStruct((N_PAIR, 1, 384), F32),
                   jax.ShapeDtypeStruct((N_PAIR, S, 128), F32),
                   jax.ShapeDtypeStruct((N_PAIR, n_t, 8, T), F32),
                   jax.ShapeDtypeStruct((r_out, c_out), F32)),
        in_specs=[pl.BlockSpec((S, 128), lambda hp, j: (0, hp)),
                  pl.BlockSpec((S, 128), lambda hp, j: (0, hp)),
                  pl.BlockSpec((S, 128), lambda hp, j: (0, hp)),
                  pl.BlockSpec((None, n_t, 8, T), lambda hp, j: (hp, 0, 0, 0)),
                  pl.BlockSpec((T, 128), lambda hp, j: (j, N_PAIR + hp)),
                  pl.BlockSpec((T, 128), lambda hp, j: (j, 2 * N_PAIR + hp)),
                  pl.BlockSpec((T, 128), lambda hp, j: (j, 0)),
                  vm],
        out_specs=(pl.BlockSpec((S, 128), lambda hp, j: (0, hp)),
                   pl.BlockSpec((T, 128), lambda hp, j: (j, hp)),
                   pl.BlockSpec((T, 128), lambda hp, j: (j, hp)),
                   pl.BlockSpec((None, 1, 384), pair_rows),
                   pl.BlockSpec((None, T, 128), lambda hp, j: (hp, j, 0)),
                   pl.BlockSpec((None, n_t, 8, T), lambda hp, j: (hp, 0, 0, 0)),
                   vm),
        scratch_shapes=[pltpu.VMEM((n_t, 8, T), F32), pltpu.VMEM((n_t, 128, T), F32),
                        pltpu.VMEM((2, S, 128), BF16), pltpu.VMEM((r_out, c_out), F32)]
        + _scatter_scratch(r_out, c_out),
        compiler_params=_params(dimension_semantics=("arbitrary", "arbitrary")),
    )(qkv, datt, att, lse, qkv, qkv, big_f, gw_out4)


def _window_counts(first_row, n_rows, window):
    t = lax.broadcasted_iota(jnp.int32, (n_rows, 1), 0) + first_row
    return jnp.minimum((t + 1).astype(F32), float(window))


def _middle(x, tgt, att, g, p, gate, w_mix, b_mix, pool_scale, w_out, b_out, ln_g, ln_b):
    S = x.shape[0]
    tm = min(TM_MID, S)
    halo_blocks = tm // POOL_HALO

    def body(x_ref, t_ref, att_ref, g_ref, p_ref, ph_ref, gate_ref, wm_ref, bm_ref, ps_ref, wo_ref, bo_ref,
             lg_ref, lb_ref,
             dh_ref, datt_ref, dg_ref, dpl_ref, gwo_ref, gwm_ref, vec_ref, loss_ref):
        i = pl.program_id(0)

        @pl.when(i == 0)
        def _():
            gwo_ref[...] = jnp.zeros_like(gwo_ref)
            gwm_ref[...] = jnp.zeros_like(gwm_ref)
            vec_ref[...] = jnp.zeros_like(vec_ref)
            loss_ref[...] = jnp.zeros_like(loss_ref)

        pc = p_ref[...]
        halo = jnp.where(i > 0, ph_ref[...], 0.0)
        pe = jnp.concatenate([halo, pc], axis=0)
        pooled_parts = []
        for gi, w in enumerate(POOL_WINDOWS):
            cur = pe[:, gi * POOL_GROUP:(gi + 1) * POOL_GROUP]
            span = 1
            while span < w:
                cur = cur + pltpu.roll(cur, span, 0)
                span *= 2
            wsum = cur[POOL_HALO:, :]
            mean = wsum / _window_counts(i * tm, tm, w)
            pooled_parts.append(mean - pc[:, gi * POOL_GROUP:(gi + 1) * POOL_GROUP])
        pooled_bf =[v.astype(BF16) for v in pooled_parts]
        wm = [wm_ref[gi].astype(BF16) for gi in range(4)]
        mixed = jnp.concatenate([_dot(pooled_bf[gi], wm[gi]) for gi in range(4)], axis=1) + bm_ref[...]
        ps = ps_ref[...]
        pool_out = mixed * ps
        gv = g_ref[...]
        sig = _sigmoid(gv)
        silu = gv * sig
        att = att_ref[...]
        y = jnp.concatenate([att * silu[:, :D_ATT], pool_out * silu[:, D_ATT:]], axis=1)
        y_bf = y.astype(BF16)
        wo = wo_ref[...]
        yo = _dot(y_bf, wo) + bo_ref[...]
        gate = gate_ref[...]
        h = ALPHA * x_ref[...] + gate * yo
        mu = jnp.mean(h, axis=1, keepdims=True)
        hc = h - mu
        var = jnp.mean(hc * hc, axis=1, keepdims=True)
        rstd = lax.rsqrt(var + LN_EPS)
        yhat = hc * rstd
        lg = lg_ref[...]
        out = yhat * lg + lb_ref[...]
        err = out - t_ref[...]
        loss_ref[...] += 0.5 * jnp.sum(jnp.mean(err * err, axis=1, keepdims=True), axis=0, keepdims=True)

        dout = err * (1.0 / D)
        g_ln_b = _colsum(dout)
        g_ln_g = _colsum(dout * yhat)
        dyh = dout * lg
        dh = rstd * (dyh - jnp.mean(dyh, axis=1, keepdims=True)
                     - yhat * jnp.mean(dyh * yhat, axis=1, keepdims=True))
        dh_ref[...] = dh
        d_gate = _colsum(dh * yo)
        dyo = gate * dh
        g_b_out = _colsum(dyo)
        dyo_bf = dyo.astype(BF16)
        gwo_ref[...] += _dot_tn(y_bf, dyo_bf)
        dy = _dot_nt(dyo_bf, wo)
        dsilu = sig * (1.0 + gv * (1.0 - sig))
        dy_a = dy[:, :D_ATT]
        dy_p = dy[:, D_ATT:]
        datt_ref[...] = (dy_a * silu[:, :D_ATT]).astype(BF16)
        dpo = dy_p * silu[:, D_ATT:]
        dg = jnp.concatenate([dy_a * att * dsilu[:, :D_ATT], dy_p * pool_out * dsilu[:, D_ATT:]], axis=1)
        dg_ref[...] = dg.astype(BF16)
        g_dg = _colsum(dg)
        g_ps = _colsum(dpo * mixed)
        dmixed = dpo * ps
        g_bm = _colsum(dmixed)
        dmixed_bf = dmixed.astype(BF16)
        dpl = []
        for gi in range(4):
            dm = dmixed_bf[:, gi * POOL_GROUP:(gi + 1) * POOL_GROUP]
            gwm_ref[gi] += _dot_tn(pooled_bf[gi], dm)
            dpl.append(_dot_nt(dm, wm[gi]))
        dpl_ref[...] = jnp.concatenate(dpl, axis=1)
        vec_ref[0:1, :] += g_ln_g
        vec_ref[1:2, :] += g_ln_b
        vec_ref[2:3, :] += d_gate
        vec_ref[3:4, :] += g_b_out
        vec_ref[4:5, :] += g_dg
        vec_ref[5:6, 0:D_POOL] += g_ps
        vec_ref[6:7, 0:D_POOL] += g_bm

    row = lambda w: pl.BlockSpec((tm, w), lambda i: (i, 0))
    full2 = lambda a: pl.BlockSpec(a.shape, lambda i: (0, 0))
    full3 = lambda a: pl.BlockSpec(a.shape, lambda i: (0, 0, 0))
    return pl.pallas_call(
        body, name="middle", grid=(S // tm,),
        out_shape=(jax.ShapeDtypeStruct((S, D), F32),
                   jax.ShapeDtypeStruct((S, D_ATT), BF16),
                   jax.ShapeDtypeStruct((S, D), BF16),
                   jax.ShapeDtypeStruct((S, D_POOL), F32),
                   jax.ShapeDtypeStruct((D, D), F32),
                   jax.ShapeDtypeStruct((4, POOL_GROUP, POOL_GROUP), F32),
                   jax.ShapeDtypeStruct((8, D), F32),
                   jax.ShapeDtypeStruct((1, 1), F32)),
        in_specs=[row(D), row(D), row(D_ATT), row(D), row(D_POOL),
                  pl.BlockSpec((POOL_HALO, D_POOL), lambda i: (jnp.maximum(i * halo_blocks - 1, 0), 0)),
                  full2(gate), full3(w_mix), full2(b_mix), full2(pool_scale), full2(w_out), full2(b_out),
                  full2(ln_g), full2(ln_b)],
        out_specs=(row(D), row(D_ATT), row(D), row(D_POOL),
                   pl.BlockSpec((D, D), lambda i: (0, 0)),
                   pl.BlockSpec((4, POOL_GROUP, POOL_GROUP), lambda i: (0, 0, 0)),
                   pl.BlockSpec((8, D), lambda i: (0, 0)),
                   pl.BlockSpec((1, 1), lambda i: (0, 0))),
        compiler_params=_params(dimension_semantics=("arbitrary",)),
    )(x, tgt, att, g, p, p, gate, w_mix, b_mix, pool_scale, w_out, b_out, ln_g, ln_b)


def _tail(dpl, dfk, dfq, f):
    S = dpl.shape[0]
    tm = min(T_ATT, S)
    n_t = S // tm
    halo_blocks = tm // POOL_HALO
    last_halo = S // POOL_HALO - 1

    def body(d_ref, dn_ref, dfk_ref, dfq_ref, f_ref, dp_ref, df_ref, cs_ref, carry):
        s = pl.program_id(0)
        i = n_t - 1 - s

        @pl.when(s == 0)
        def _():
            carry[...] = jnp.zeros_like(carry)
            cs_ref[...] = jnp.zeros_like(cs_ref)

        dc = d_ref[...]
        nxt = jnp.where(s > 0, dn_ref[...], 0.0)
        de = jnp.concatenate([dc, nxt], axis=0)
        n_e = tm + POOL_HALO
        parts = []
        for gi, w in enumerate(POOL_WINDOWS):
            cur = de[:, gi * POOL_GROUP:(gi + 1) * POOL_GROUP] / _window_counts(i * tm, n_e, w)
            span = 1
            while span < w:
                cur = cur + pltpu.roll(cur, n_e - span, 0)
                span *= 2
            parts.append(cur[:tm, :] - dc[:, gi * POOL_GROUP:(gi + 1) * POOL_GROUP])
        dp = jnp.concatenate(parts, axis=1)
        dp_ref[...] = dp.astype(BF16)
        cs_ref[0:1, :] += _colsum(dp)

        r = lax.broadcasted_iota(jnp.int32, (tm, tm), 0)
        c = lax.broadcasted_iota(jnp.int32, (tm, tm), 1)
        tri = (r >= c).astype(F32)
        k_cols = dfk_ref[0]
        rows8 = dfq_ref[0]
        for hp in range(1, N_PAIR):
            k_cols = k_cols + pltpu.roll(dfk_ref[hp], 2 * hp, 1)
            rows8 = rows8 + pltpu.roll(dfq_ref[hp], 2 * hp, 0)
        rows8 = rows8 + k_cols.T[0:8, :]
        dlogf8 = jnp.dot(rows8, tri, preferred_element_type=F32, precision=lax.Precision.HIGHEST) + carry[...]
        first = lax.broadcasted_iota(jnp.int32, (1, tm), 1) == 0
        carry[...] = jnp.sum(jnp.where(first, dlogf8, 0.0), axis=1, keepdims=True)
        dlogf = jnp.concatenate([dlogf8, jnp.zeros((128 - 8, tm), F32)], axis=0).T
        df = dlogf * _sigmoid(-f_ref[...])
        df_ref[...] = df.astype(BF16)
        cs_ref[1:2, 0:128] += _colsum(df)

    rev = lambda w: pl.BlockSpec((tm, w), lambda s: (n_t - 1 - s, 0))
    return pl.pallas_call(
        body, name="tail", grid=(n_t,),
        out_shape=(jax.ShapeDtypeStruct((S, D_POOL), BF16), jax.ShapeDtypeStruct((S, 128), BF16),
                   jax.ShapeDtypeStruct((8, D_POOL), F32)),
        in_specs=[rev(D_POOL),
                  pl.BlockSpec((POOL_HALO, D_POOL),
                               lambda s: (jnp.minimum((n_t - s) * halo_blocks, last_halo), 0)),
                  pl.BlockSpec((N_PAIR, tm, 128), lambda s: (0, n_t - 1 - s, 0)),
                  pl.BlockSpec((N_PAIR, None, 8, tm), lambda s: (0, n_t - 1 - s, 0, 0)),
                  rev(128)],
        out_specs=(rev(D_POOL), rev(128), pl.BlockSpec((8, D_POOL), lambda s: (0, 0))),
        scratch_shapes=[pltpu.VMEM((8, 1), F32)],
        compiler_params=_params(dimension_semantics=("arbitrary",)),
    )(dpl, dpl, dfk, dfq, f)


PIECES = ((O_QKV, D_ATT), (O_QKV + D_ATT, D_ATT), (O_QKV + 2 * D_ATT, D_ATT), (O_F, 128), (O_P, D_POOL), (O_G, D))


def _grad_w_in(u, pieces):
    S = u.shape[0]
    tm = min(TM_GW, S)
    n_t = S // tm

    def body(u_ref, *rest):
        piece_refs, out_ref, acc, sem = rest[:6], rest[6], rest[7], rest[8]
        i = pl.program_id(0)

        @pl.when(i == 0)
        def _():
            acc[...] = jnp.zeros_like(acc)

        u_t = u_ref[...]
        for (off, w), ref in zip(PIECES, piece_refs):
            acc[:, off:off + w] += _dot_tn(u_t, ref[...])

        @pl.when(i == n_t - 1)
        def _():
            cp = pltpu.make_async_copy(acc, out_ref, sem)
            cp.start()
            cp.wait()

    return pl.pallas_call(
        body, name="grad_w_in", grid=(n_t,),
        out_shape=jax.ShapeDtypeStruct((D, D_PAD), F32),
        in_specs=[pl.BlockSpec((tm, D), lambda i: (i, 0))]
        + [pl.BlockSpec((tm, w), lambda i: (i, 0)) for _, w in PIECES],
        out_specs=pl.BlockSpec(memory_space=pl.ANY),
        scratch_shapes=[pltpu.VMEM((D, D_PAD), F32), pltpu.SemaphoreType.DMA],
        compiler_params=_params(dimension_semantics=("arbitrary",)),
    )(u, *pieces)


def _grad_x(pieces, wt_pad, dh, x, scale):
    S = x.shape[0]
    tm = min(TM_DU, S)

    def body(*refs):
        piece_refs = refs[:6]
        w_ref, dh_ref, x_ref, sc_ref, gx_ref, vec_ref = refs[6:]

        @pl.when(pl.program_id(0) == 0)
        def _():
            vec_ref[...] = jnp.zeros_like(vec_ref)

        du = jnp.zeros((tm, D), F32)
        for (off, w), ref in zip(PIECES, piece_refs):
            du = du + _dot(ref[...], w_ref[off:off + w, :])
        xv = x_ref[...]
        gx_ref[...] = ALPHA * dh_ref[...] + du * (1.0 + sc_ref[...])
        vec_ref[0:1, :] += _colsum(du)
        vec_ref[1:2, :] += _colsum(du * xv)

    row = lambda w: pl.BlockSpec((tm, w), lambda i: (i, 0))
    return pl.pallas_call(
        body, name="grad_x", grid=(S // tm,),
        out_shape=(jax.ShapeDtypeStruct((S, D), F32), jax.ShapeDtypeStruct((8, D), F32)),
        in_specs=[row(w) for _, w in PIECES]
        + [pl.BlockSpec(wt_pad.shape, lambda i: (0, 0)), row(D), row(D), pl.BlockSpec((1, D), lambda i: (0, 0))],
        out_specs=(row(D), pl.BlockSpec((8, D), lambda i: (0, 0))),
        compiler_params=_params(dimension_semantics=("arbitrary",)),
    )(*pieces, wt_pad, dh, x, scale)


def _adamw_math(w, g, m, v):
    m = ADAM_B1 * m + (1.0 - ADAM_B1) * g
    v = ADAM_B2 * v + (1.0 - ADAM_B2) * (g * g)
    m_hat = m / (1.0 - ADAM_B1 ** ADAM_STEP)
    v_hat = v / (1.0 - ADAM_B2 ** ADAM_STEP)
    delta = -ADAM_LR * (m_hat / (jnp.sqrt(v_hat) + ADAM_EPS) + ADAM_WD * w)
    return delta, m, v


def _adamw(groups, n_steps):
    n = len(groups)

    def body(*refs):
        ins, outs = refs[:4 * n], refs[4 * n:]
        for t in range(n):
            w, g, m, v = (r[...] for r in ins[4 * t:4 * t + 4])
            d, m2, v2 = _adamw_math(w, g, m, v)
            outs[4 * t][...] = d
            outs[4 * t + 1][...] = m2
            outs[4 * t + 2][...] = v2
            outs[4 * t + 3][...] = g

    in_specs, out_specs, out_shape, args = [], [], [], []
    for (w, g, m, v) in groups:
        rest = w.shape[1:]
        spec = pl.BlockSpec((w.shape[0] // n_steps,) + rest, lambda i, nd=len(rest): (i,) + (0,) * nd)
        in_specs += [spec] * 4
        out_specs += [spec] * 4
        out_shape += [jax.ShapeDtypeStruct(w.shape, F32)] * 4
        args += [w, g, m, v]
    return pl.pallas_call(
        body, name="adamw_%d_%d" % (n, n_steps), grid=(n_steps,),
        out_shape=tuple(out_shape), in_specs=in_specs, out_specs=tuple(out_specs),
        compiler_params=_params(dimension_semantics=("arbitrary",)),
    )(*args)


def _adamw_small(small_sum, g_b_ada, params):
    n = len(params)

    def body(gs_ref, gba_ref, *refs):
        ins, outs = refs[:3 * n], refs[3 * n:]
        for t, (name, w0, _, _) in enumerate(params):
            w_ref, m_ref, v_ref = ins[3 * t:3 * t + 3]
            first = SMALL_SEGS[name][0] if name in SMALL_SEGS else None
            if w0.shape[0] > 1:
                pieces = [((slice(None), slice(None)), gs_ref[first:first + w0.shape[0], :])]
            else:
                pieces = []
                for r in range(-(-w0.shape[1] // 128)):
                    lanes = slice(128 * r, min(128 * r + 128, w0.shape[1]))
                    g = gba_ref[0:1, lanes] if first is None else gs_ref[first + r:first + r + 1, 0:lanes.stop - lanes.start]
                    pieces.append(((slice(0, 1), lanes), g))
            for where, g in pieces:
                d, m2, v2 = _adamw_math(w_ref[where], g, m_ref[where], v_ref[where])
                for ref, val in zip(outs[4 * t:4 * t + 4], (g, d, m2, v2)):
                    ref[where] = val

    vm = pl.BlockSpec(memory_space=pltpu.VMEM)
    args = [small_sum, g_b_ada]
    out_shape = []
    for _, w, m, v in params:
        args += [w, m, v]
        out_shape += [jax.ShapeDtypeStruct(w.shape, F32)] * 4
    return pl.pallas_call(
        body, name="adamw_small",
        out_shape=tuple(out_shape), in_specs=[vm] * len(args), out_specs=(vm,) * len(out_shape),
        compiler_params=_params(),
    )(*args)


def _pack_small(parts):
    rows = []
    used = 0
    for name, (first, n_rows) in SMALL_SEGS.items():
        if first > used:
            rows.append(jnp.zeros((first - used, 128), F32))
        flat = parts[name].reshape(-1)
        flat = jnp.pad(flat, (0, n_rows * 128 - flat.shape[0]))
        rows.append(flat.reshape(n_rows, 128))
        used = first + n_rows
    rows.append(jnp.zeros((SMALL_ROWS - used, 128), F32))
    return jnp.concatenate(rows, axis=0)


def _pad_in(v):
    r = v.shape[0]
    z = jnp.zeros((r, O_P - O_F - N_HEADS), v.dtype)
    return jnp.concatenate([v[:, :3 * D_ATT + N_HEADS], z, v[:, 3 * D_ATT + N_HEADS:]], axis=1)


def _unpad_in(v):
    return jnp.concatenate([v[:, :O_F + N_HEADS], v[:, O_P:]], axis=1)


def _shards_in(v):
    gap = O_P - (O_F + N_HEADS)
    parts = []
    for a in range(N_CHIPS):
        lo, hi = a * SHARD_IN, (a + 1) * SHARD_IN
        cut = O_F + N_HEADS
        if hi <= cut:
            parts.append(v[:, lo:hi])
        elif lo >= cut:
            parts.append(v[:, lo + gap:hi + gap])
        else:
            parts.append(jnp.concatenate([v[:, lo:cut], v[:, cut + gap:hi + gap]], axis=1))
    return jnp.stack(parts, axis=0)


def kernel(x, c, w_ada, b_ada, w_in, b_in, w_pool_mix, b_pool_mix, pool_scale, w_out, b_out, ln_g, ln_b, loss_target, m_w_ada, m_b_ada, m_w_in, m_b_in, m_w_pool_mix, m_b_pool_mix, m_pool_scale, m_w_out, m_b_out, m_ln_g, m_ln_b, v_w_ada, v_b_ada, v_w_in, v_b_in, v_w_pool_mix, v_b_pool_mix, v_pool_scale, v_w_out, v_b_out, v_ln_g, v_ln_b):
    S = x.shape[1]
    T = min(T_ATT, S)
    n_t = S // T
    x2 = x[0]
    tgt = loss_target[0]
    q_scale = jnp.concatenate([jnp.full((1, D_ATT), Q_SCALE, F32), jnp.ones((1, D_PAD - D_ATT), F32)], axis=1)

    to_cols = lambda a: jnp.transpose(a, (2, 0, 1))
    from_cols = lambda a: jnp.transpose(a, (1, 2, 0))
    c_all, shift, scale, gate, wt_pad = _gather_and_ada(
        c, w_ada[0], b_ada.reshape(4, 1, SHARD_ADA), to_cols(w_in))
    b_pad = _pad_in(b_in) * q_scale

    u, qkv, f, p, g, w_out_all = _in_proj(x2, shift, scale, wt_pad, b_pad, w_out[0])
    w_out_full = w_out_all.reshape(D, D)
    big_f = _forget_cumsum(f)
    att, lse = _attention_fwd(qkv, big_f)

    dh, datt, dg, dpl, gw_out, gw_mix, vec, loss_part = _middle(
        x2, tgt, att, g, p, gate, w_pool_mix[0], b_pool_mix.reshape(1, D_POOL), pool_scale, w_out_full, b_out, ln_g, ln_b)
    dq, dk, dv, cs_att, dfk, dfq, g_w_out = _attention_bwd(
        qkv, datt, att, lse, big_f, gw_out.reshape(N_CHIPS, SHARD_OUT, D))
    dp, df, cs_tail = _tail(dpl, dfk, dfq, f)
    pieces = (dq, dk, dv, df, dp, dg)
    gw_pad = _grad_w_in(u, pieces)
    grad_x, vec_x = _grad_x(pieces, wt_pad, dh, x2, scale)

    cs_qkv = jnp.transpose(cs_att.reshape(N_PAIR, 3, 128), (1, 0, 2)).reshape(1, 3 * D_ATT)
    gb_pad = jnp.concatenate([cs_qkv, cs_tail[1:2, 0:128], cs_tail[0:1, :], vec[4:5, :]], axis=1) * q_scale
    dada = jnp.concatenate([vec_x[0:1, :], vec_x[1:2, :], vec[2:3, :]], axis=1)
    small = _pack_small({
        "b_in": _unpad_in(gb_pad), "w_pool_mix": gw_mix, "b_pool_mix": vec[6:7, :D_POOL],
        "pool_scale": vec[5:6, :D_POOL], "b_out": vec[3:4, :], "ln_g": vec[0:1, :], "ln_b": vec[1:2, :],
        "loss": loss_part})

    g_w_in, small_sum, g_w_ada, g_b_ada, loss = _reduce_all(
        gw_pad, _shards_in(q_scale), small, dada, c_all)

    big = _adamw([(w_ada[0], g_w_ada, m_w_ada[0], v_w_ada[0]),
                  (w_out[0], g_w_out, m_w_out[0], v_w_out[0])], 2)
    big_in = _adamw([(to_cols(w_in), g_w_in, to_cols(m_w_in), to_cols(v_w_in))], 2)
    tiles = lambda a: a.reshape(4 * POOL_GROUP, POOL_GROUP)
    flat = lambda a: a.reshape(1, D_POOL)
    small_params = [("b_ada", b_ada, m_b_ada, v_b_ada), ("b_in", b_in, m_b_in, v_b_in),
                    ("w_pool_mix", tiles(w_pool_mix), tiles(m_w_pool_mix), tiles(v_w_pool_mix)),
                    ("b_pool_mix", flat(b_pool_mix), flat(m_b_pool_mix), flat(v_b_pool_mix)),
                    ("pool_scale", pool_scale, m_pool_scale, v_pool_scale), ("b_out", b_out, m_b_out, v_b_out),
                    ("ln_g", ln_g, m_ln_g, v_ln_g), ("ln_b", ln_b, m_ln_b, v_ln_b)]
    sm = _adamw_small(small_sum, g_b_ada, small_params)
    sm_idx = {p[0]: n for n, p in enumerate(small_params)}
    shapes = {"w_pool_mix": (1, 4, POOL_GROUP, POOL_GROUP), "b_pool_mix": (1, 4, POOL_GROUP)}

    names = ["w_ada", "b_ada", "w_in", "b_in", "w_pool_mix", "b_pool_mix", "pool_scale", "w_out", "b_out",
             "ln_g", "ln_b"]
    big_idx = {"w_ada": 0, "w_out": 1}

    def leaf(kind, name):
        if name == "w_in":
            return from_cols(big_in[(kind - 1) % 4])
        if name in big_idx:
            return big[4 * big_idx[name] + (kind - 1) % 4][None]
        val = sm[4 * sm_idx[name] + kind]
        return val.reshape(shapes[name]) if name in shapes else val

    outs = [loss.reshape(()), grad_x[None]]
    for kind in range(4):
        outs += [leaf(kind, n) for n in names]
    return tuple(outs)
```

```python
import functools

import numpy as np
import jax
import jax.numpy as jnp
from jax import lax
from jax.experimental import pallas as pl
from jax.experimental.pallas import tpu as pltpu

F32 = jnp.float32
BF16 = jnp.bfloat16
MESH = pl.DeviceIdType.MESH

D = 1024
D_ATT = 512
D_POOL = 512
N_HEADS = 8
HEAD_DIM = 64
N_PAIR = N_HEADS // 2
POOL_WINDOWS = (2, 4, 8, 16)
POOL_GROUP = 128
POOL_HALO = 16
LN_EPS = 1e-5
ALPHA = 2.0 ** 0.25
D_IN = 3 * D_ATT + N_HEADS + D_POOL + D_ATT + D_POOL
N_CHIPS = 4
SHARD_IN = D_IN // N_CHIPS
SHARD_ADA = 3 * D // N_CHIPS
SHARD_OUT = D // N_CHIPS

O_QKV, O_F, O_P, O_G, D_PAD = 0, 1536, 1664, 2176, 3200
Q_SCALE = HEAD_DIM ** -0.5

ADAM_LR, ADAM_B1, ADAM_B2, ADAM_EPS, ADAM_WD, ADAM_STEP = 0.001, 0.9, 0.999, 1e-08, 0.01, 10

NEG = -1e30

VMEM_LIMIT = 56 * 1024 * 1024

TM_PROJ = 512
T_ATT = 512
TM_MID = 512
TM_GW = 1024
TM_DU = 512

REL7 = [(0, 0, 1), (0, 1, 0), (0, 1, 1), (1, 0, 0), (1, 0, 1), (1, 1, 0), (1, 1, 1)]
REL3 = [(0, 1), (1, 0), (1, 1)]

SMALL_SEGS = {}
_row = 0
for _name, _n in (("b_in", D_IN), ("w_pool_mix", 65536), ("b_pool_mix", 512), ("pool_scale", 512),
                  ("b_out", 1024), ("ln_g", 1024), ("ln_b", 1024), ("loss", 1)):
    _rows = -(-_n // 1024) * 8
    SMALL_SEGS[_name] = (_row, _rows)
    _row += _rows
SMALL_ROWS = -(-_row // 16) * 16


def _params(**kw):
    return pltpu.CompilerParams(vmem_limit_bytes=VMEM_LIMIT, **kw)


def _flip(v, d):
    return v if d == 0 else 1 - v


def _dot(a, b):
    return jnp.dot(a, b, preferred_element_type=F32)


def _dot_nt(a, b):
    return lax.dot_general(a, b, (((1,), (1,)), ((), ())), preferred_element_type=F32)


def _dot_tn(a, b):
    return lax.dot_general(a, b, (((0,), (0,)), ((), ())), preferred_element_type=F32)


def _sigmoid(v):
    return 1.0 / (1.0 + jnp.exp(-v))


def _colsum(v):
    return jnp.sum(v, axis=0, keepdims=True)


def _gather_stages(pos, src_ref, dst_ref, half, own_sem, s_sem, r_sem, fs_sem, fr_sem):
    x, y, cc, chip, sib = pos
    own = pltpu.make_async_copy(src_ref, dst_ref.at[chip], own_sem)
    first, landed, others = [], [], []
    for k, (dx, dy) in enumerate(REL3):
        px, py = _flip(x, dx), _flip(y, dy)
        first.append(pltpu.make_async_remote_copy(
            src_ref=src_ref.at[half(cc)], dst_ref=dst_ref.at[(chip,) + half(cc)],
            send_sem=s_sem.at[k], recv_sem=r_sem.at[k], device_id=(px, py, cc), device_id_type=MESH))
        landed.append(dst_ref.at[(2 * px + py,) + half(cc)])
        others.append(dst_ref.at[(2 * px + py,) + half(1 - cc)])
    passed = [pltpu.make_async_remote_copy(src_ref=landed[k], dst_ref=landed[k], send_sem=fs_sem.at[k],
                                           recv_sem=fr_sem.at[k], device_id=sib, device_id_type=MESH)
              for k in range(3)]

    def start(finish_src=None):
        for cp in first:
            cp.start()
        if finish_src is not None:
            finish_src()
        own.start()

    def forward():
        for k in range(3):
            pltpu.make_async_remote_copy(src_ref=landed[k], dst_ref=landed[k], send_sem=s_sem.at[k],
                                         recv_sem=r_sem.at[k], device_id=sib, device_id_type=MESH).wait_recv()
            passed[k].start()

    def finish():
        for k in range(3):
            pltpu.make_async_remote_copy(src_ref=others[k], dst_ref=others[k], send_sem=fs_sem.at[k],
                                         recv_sem=fr_sem.at[k], device_id=sib, device_id_type=MESH).wait_recv()
        for cp in first + passed:
            cp.wait_send()
        own.wait()

    return start, forward, finish


def _gather_scratch():
    return [pltpu.SemaphoreType.DMA, pltpu.SemaphoreType.DMA((3,)), pltpu.SemaphoreType.DMA((3,)),
            pltpu.SemaphoreType.DMA((3,)), pltpu.SemaphoreType.DMA((3,))]


def _gather_and_ada(c, w_ada, b_ada4, w_in_sh):
    def body(c_ref, w_ref, b_ref, win_ref, call_ref, shift_ref, scale_ref, gate_ref, wt_pad_ref,
             win_all, win_bf, ada_ref, cslab, sbuf, rbuf, cs_sem, cr_sem, as_sem, ar_sem, *gather_sems):
        x, y, cc = lax.axis_index("x"), lax.axis_index("y"), lax.axis_index("c")
        me = 4 * x + 2 * y + cc
        chip = 2 * x + y
        lane_half = lambda which: (slice(None), pl.ds(pl.multiple_of(which * (D // 2), D // 2), D // 2))
        def round_half(which):
            for h in range(2):
                @pl.when(which == h)
                def _():
                    lanes = slice(h * (D // 2), (h + 1) * (D // 2))
                    win_bf[:, lanes] = win_ref[:, 0, lanes].astype(BF16)

        start, forward, finish = _gather_stages((x, y, cc, chip, (x, y, 1 - cc)), win_bf, win_all, lane_half,
                                                *gather_sems)
        round_half(cc)
        start(lambda: round_half(1 - cc))

        cslab[...] = jnp.broadcast_to(c_ref[...], (8, D))
        call_ref[me] = cslab[...]
        gathers = []
        for k, (dx, dy, dc) in enumerate(REL7):
            cp = pltpu.make_async_remote_copy(
                src_ref=cslab, dst_ref=call_ref.at[me], send_sem=cs_sem.at[k], recv_sem=cr_sem.at[k],
                device_id=(_flip(x, dx), _flip(y, dy), _flip(cc, dc)), device_id_type=MESH)
            cp.start()
            gathers.append(cp)
        for cp in gathers:
            cp.wait()
        slab_row = lax.broadcasted_iota(jnp.int32, (8, 1), 0)
        mat = jnp.zeros((8, D), F32)
        for r in range(8):
            mat = jnp.where(slab_row == r, call_ref[r], mat)
        act = (mat * _sigmoid(mat)).astype(BF16)
        part = _dot(act, w_ref[...].astype(BF16))
        sends = []
        for k, (dx, dy) in enumerate(REL3):
            px, py = _flip(x, dx), _flip(y, dy)
            r = 4 * px + 2 * py + cc
            piece = _colsum(jnp.where(slab_row == r, part, 0.0))
            sbuf[k] = jnp.broadcast_to(piece, (8, SHARD_ADA))
            cp = pltpu.make_async_remote_copy(
                src_ref=sbuf.at[k], dst_ref=rbuf.at[k], send_sem=as_sem.at[k], recv_sem=ar_sem.at[k],
                device_id=(px, py, cc), device_id_type=MESH)
            cp.start()
            sends.append(cp)
        own_piece = _colsum(jnp.where(slab_row == me, part, 0.0))
        ada_ref[chip] = jnp.broadcast_to(own_piece, (8, SHARD_ADA)) + b_ref[chip]
        for k, (dx, dy) in enumerate(REL3):
            sends[k].wait()
            a = 2 * _flip(x, dx) + _flip(y, dy)
            ada_ref[a] = rbuf[k] + b_ref[a]
        ada = jnp.concatenate([ada_ref[a][0:1, :] for a in range(N_CHIPS)], axis=1)
        shift_ref[...] = ada[:, 0:D]
        scale_ref[...] = ada[:, D:2 * D]
        gate_ref[...] = ada[:, 2 * D:3 * D]

        forward()
        finish()
        n_real = 3 * D_ATT + N_HEADS
        for a in range(N_CHIPS):
            lo, hi = a * SHARD_IN, (a + 1) * SHARD_IN
            for s0, s1 in ((lo, min(hi, D_ATT)), (max(lo, D_ATT), min(hi, n_real)), (max(lo, n_real), hi)):
                if s0 < s1:
                    rows = win_all[a, s0 - lo:s1 - lo, :]
                    if s1 <= D_ATT:
                        rows = rows * jnp.asarray(Q_SCALE, BF16)
                    shift = O_P - n_real if s0 >= n_real else 0
                    wt_pad_ref[s0 + shift:s1 + shift, :] = rows
        wt_pad_ref[n_real:O_P, :] = jnp.zeros((O_P - n_real, D), BF16)

    vm = pl.BlockSpec(memory_space=pltpu.VMEM)
    return pl.pallas_call(
        body, name="gather_and_ada",
        out_shape=(jax.ShapeDtypeStruct((8, 8, D), F32),) + (jax.ShapeDtypeStruct((1, D), F32),) * 3
        + (jax.ShapeDtypeStruct((D_PAD, D), BF16),),
        in_specs=[vm] * 4, out_specs=(vm,) * 5,
        scratch_shapes=[pltpu.VMEM((N_CHIPS, SHARD_IN, D), BF16), pltpu.VMEM((SHARD_IN, D), BF16),
                        pltpu.VMEM((N_CHIPS, 8, SHARD_ADA), F32), pltpu.VMEM((8, D), F32), pltpu.VMEM((3, 8, SHARD_ADA), F32),
                        pltpu.VMEM((3, 8, SHARD_ADA), F32),
                        pltpu.SemaphoreType.DMA((7,)), pltpu.SemaphoreType.DMA((7,)),
                        pltpu.SemaphoreType.DMA((3,)), pltpu.SemaphoreType.DMA((3,))] + _gather_scratch(),
        compiler_params=_params(),
    )(c, w_ada, b_ada4, w_in_sh)


def _shard_cols():
    cut, gap = O_F + N_HEADS, O_P - (O_F + N_HEADS)
    out = []
    for a in range(N_CHIPS):
        lo, hi = a * SHARD_IN, (a + 1) * SHARD_IN
        out.append(([(lo, min(hi, cut))] if lo < cut else []) + ([(max(lo, cut) + gap, hi + gap)] if hi > cut else []))
    return out


def _scatter_stages(pos, g_ref, sc_ref, out_ref, sib_buf, send_buf, ici_buf, sem1, sem2s, sem2r, sem3, part=(0, 1),
                    cols=None, own_buf=None):
    x, y, cc, chip, sib = pos
    q, n_parts = part
    RH = (g_ref.shape[1] if cols is None else g_ref.shape[0]) // 2 // n_parts
    mine = pl.ds(pl.multiple_of((cc * n_parts + q) * RH, RH), RH)
    theirs = pl.ds(pl.multiple_of(((1 - cc) * n_parts + q) * RH, RH), RH)
    cp1 = pltpu.make_async_remote_copy(
        src_ref=g_ref.at[:, theirs, :] if cols is None else g_ref.at[theirs, :], dst_ref=sib_buf,
        send_sem=sem1.at[0], recv_sem=sem1.at[1], device_id=sib, device_id_type=MESH)
    sends = []
    for k, (dx, dy) in enumerate(REL3):
        px, py = _flip(x, dx), _flip(y, dy)
        sends.append(pltpu.make_async_remote_copy(
            src_ref=send_buf.at[2 * px + py], dst_ref=ici_buf.at[chip],
            send_sem=sem2s.at[k], recv_sem=sem2r.at[k], device_id=(px, py, cc), device_id_type=MESH))
    cp3 = pltpu.make_async_remote_copy(
        src_ref=out_ref.at[mine, :], dst_ref=out_ref.at[mine, :], send_sem=sem3.at[0], recv_sem=sem3.at[1],
        device_id=sib, device_id_type=MESH)

    def finish1():
        cp1.wait()
        if cols is None:
            for a in range(N_CHIPS):
                both = g_ref[a, mine, :] + sib_buf[a]
                sib_buf[a] = both
                send_buf[a] = both.astype(BF16)
        else:
            both = g_ref[mine, :] + sib_buf[...]
            for a, pieces in enumerate(cols):
                at = 0
                for lo, hi in pieces:
                    own_buf[a, :, at:at + hi - lo] = both[:, lo:hi]
                    send_buf[a, :, at:at + hi - lo] = both[:, lo:hi].astype(BF16)
                    at += hi - lo

    def start2():
        for cp in sends:
            cp.start()
        ici_buf[chip] = send_buf[chip]

    def finish2():
        for cp in sends:
            cp.wait()
        own = (sib_buf if cols is None else own_buf)[chip]
        parts = [jnp.where(chip == a, own, ici_buf[a].astype(F32)) for a in range(N_CHIPS)]
        total = (parts[0] + parts[1]) + (parts[2] + parts[3])
        out_ref[mine, 0:own.shape[1]] = total if sc_ref is None else total * sc_ref[chip]

    return [(cp1.start, finish1), (start2, finish2), (cp3.start, cp3.wait)]


def _all_reduce_stages(pos, g_ref, out_ref, sib_buf, ici_buf, sem1, sem2s, sem2r, sem3):
    x, y, cc, chip, sib = pos
    RH = g_ref.shape[0] // 2
    mine = pl.ds(pl.multiple_of(cc * RH, 8), RH)
    theirs = pl.ds(pl.multiple_of((1 - cc) * RH, 8), RH)
    cp1 = pltpu.make_async_remote_copy(
        src_ref=g_ref.at[theirs, :], dst_ref=sib_buf, send_sem=sem1.at[0], recv_sem=sem1.at[1],
        device_id=sib, device_id_type=MESH)
    sends = []
    for k, (dx, dy) in enumerate(REL3):
        px, py = _flip(x, dx), _flip(y, dy)
        sends.append(pltpu.make_async_remote_copy(
            src_ref=sib_buf, dst_ref=ici_buf.at[chip],
            send_sem=sem2s.at[k], recv_sem=sem2r.at[k], device_id=(px, py, cc), device_id_type=MESH))
    cp3 = pltpu.make_async_remote_copy(
        src_ref=out_ref.at[mine, :], dst_ref=out_ref.at[mine, :], send_sem=sem3.at[0], recv_sem=sem3.at[1],
        device_id=sib, device_id_type=MESH)

    def finish1():
        cp1.wait()
        sib_buf[...] = g_ref[mine, :] + sib_buf[...]

    def start2():
        for cp in sends:
            cp.start()
        ici_buf[chip] = sib_buf[...]

    def finish2():
        for cp in sends:
            cp.wait()
        out_ref[mine, :] = (ici_buf[0] + ici_buf[1]) + (ici_buf[2] + ici_buf[3])

    return [(cp1.start, finish1), (start2, finish2), (cp3.start, cp3.wait)]


def _stage_sems():
    return [pltpu.SemaphoreType.DMA((2,)), pltpu.SemaphoreType.DMA((3,)),
            pltpu.SemaphoreType.DMA((3,)), pltpu.SemaphoreType.DMA((2,))]


def _scatter_scratch(r, c):
    return [pltpu.VMEM((N_CHIPS, r // 2, c), F32), pltpu.VMEM((N_CHIPS, r // 2, c), BF16),
            pltpu.VMEM((N_CHIPS, r // 2, c), BF16)] + _stage_sems()


def _reduce_all(gw_pad, sc_in, small, dada, c_all):
    R = small.shape[0]
    W = dada.shape[1]
    r_in, p_in = gw_pad.shape
    c_in = SHARD_IN
    chunk = r_in // 4

    def chunk_scratch():
        return ([pltpu.VMEM((chunk, p_in), F32), pltpu.VMEM((N_CHIPS, chunk, c_in), BF16),
                 pltpu.VMEM((N_CHIPS, chunk, c_in), BF16)] + _stage_sems()
                + [pltpu.VMEM((N_CHIPS, chunk, c_in), F32)])

    n_in = len(chunk_scratch())

    c_wide = -(-c_in // 128) * 128

    def body(gin_ref, scin_ref, sm_ref, d_ref, c_ref, ocols_ref, osm_ref, gwa_ref, gba_ref, loss_ref, oin_ref,
             dall_ref, *scratch):
        x, y, cc = lax.axis_index("x"), lax.axis_index("y"), lax.axis_index("c")
        me = 4 * x + 2 * y + cc
        chip = 2 * x + y
        pos = (x, y, cc, chip, (x, y, 1 - cc))
        oin_ref[:, c_in:c_wide] = jnp.zeros((r_in, c_wide - c_in), F32)
        dslab, ds_sem, dr_sem = scratch[0:3]
        a_bufs, b_bufs, sm_bufs = scratch[3:3 + n_in], scratch[3 + n_in:3 + 2 * n_in], scratch[3 + 2 * n_in:]
        dslab[...] = jnp.broadcast_to(d_ref[...], (8, W))
        dall_ref[me] = dslab[...]
        gathers = []
        for k, (dx, dy, dc) in enumerate(REL7):
            cp = pltpu.make_async_remote_copy(
                src_ref=dslab, dst_ref=dall_ref.at[me], send_sem=ds_sem.at[k], recv_sem=dr_sem.at[k],
                device_id=(_flip(x, dx), _flip(y, dy), _flip(cc, dc)), device_id_type=MESH)
            cp.start()
            gathers.append(cp)
        cols = _shard_cols()
        first = _scatter_stages(pos, gin_ref, scin_ref, oin_ref, *a_bufs[:-1], part=(0, 2), cols=cols,
                                own_buf=a_bufs[-1])
        second = _scatter_stages(pos, gin_ref, scin_ref, oin_ref, *b_bufs[:-1], part=(1, 2), cols=cols,
                                 own_buf=b_bufs[-1])
        little = _all_reduce_stages(pos, sm_ref, osm_ref, *sm_bufs)
        for plan in (first, second, little):
            plan[0][0]()
        first[0][1]()
        first[1][0]()
        little[0][1]()
        little[1][0]()
        second[0][1]()
        second[1][0]()
        for cp in gathers:
            cp.wait()
        slab_row = lax.broadcasted_iota(jnp.int32, (8, 1), 0)
        cm = jnp.zeros((8, D), F32)
        dm = jnp.zeros((8, W), F32)
        for r in range(8):
            cm = jnp.where(slab_row == r, c_ref[r], cm)
            dm = jnp.where(slab_row == r, dall_ref[r], dm)
        act = cm * _sigmoid(cm)
        dcol = dm[:, 0:SHARD_ADA]
        for a in range(1, N_CHIPS):
            dcol = jnp.where(chip == a, dm[:, a * SHARD_ADA:(a + 1) * SHARD_ADA], dcol)
        lhs = jnp.concatenate([act, jnp.zeros((8, D), F32)], axis=0).astype(BF16)
        rhs = jnp.concatenate([dcol, jnp.zeros((8, SHARD_ADA), F32)], axis=0).astype(BF16)
        gwa_ref[...] = _dot_tn(lhs, rhs)
        gba_ref[...] = _colsum(dm)
        first[1][1]()
        first[2][0]()
        second[1][1]()
        second[2][0]()
        little[1][1]()
        little[2][0]()
        for plan in (first, second, little):
            plan[2][1]()
        loss_row = SMALL_SEGS["loss"][0]
        loss_ref[...] = osm_ref[loss_row:loss_row + 1, 0:1]
        ocols_ref[...] = oin_ref[...].T[0:c_in, :][:, None, :]

    scratch = [pltpu.VMEM((r_in, c_wide), F32), pltpu.VMEM((8, 8, W), F32),
               pltpu.VMEM((8, W), F32), pltpu.SemaphoreType.DMA((7,)), pltpu.SemaphoreType.DMA((7,))]
    scratch += chunk_scratch() + chunk_scratch()
    scratch += [pltpu.VMEM((R // 2, 128), F32), pltpu.VMEM((N_CHIPS, R // 2, 128), F32)] + _stage_sems()
    vm = pl.BlockSpec(memory_space=pltpu.VMEM)
    return pl.pallas_call(
        body, name="reduce_all",
        out_shape=(jax.ShapeDtypeStruct((c_in, 1, r_in), F32), jax.ShapeDtypeStruct((R, 128), F32),
                   jax.ShapeDtypeStruct((D, SHARD_ADA), F32), jax.ShapeDtypeStruct((1, W), F32),
                   jax.ShapeDtypeStruct((1, 1), F32)),
        in_specs=[vm] * 5, out_specs=(vm,) * 5,
        scratch_shapes=scratch,
        compiler_params=_params(),
    )(gw_pad, sc_in, small, dada, c_all)


def _in_proj(x, shift, scale, wt_pad, b_pad, w_out_sh):
    S = x.shape[0]
    tm = min(TM_PROJ, S)
    n_steps = S // tm
    assert n_steps >= 3

    def body(x_ref, sh_ref, sc_ref, w_ref, b_ref, wo_ref, u_ref, qkv_ref, f_ref, p_ref, g_ref, wo_all,
             wo_buf, wo_bf, *gather_sems):
        i = pl.program_id(0)
        xx, yy, cc = lax.axis_index("x"), lax.axis_index("y"), lax.axis_index("c")
        row_half = lambda which: (pl.ds(pl.multiple_of(which * (SHARD_OUT // 2), SHARD_OUT // 2), SHARD_OUT // 2),
                                  slice(None))
        start, forward, finish = _gather_stages((xx, yy, cc, 2 * xx + yy, (xx, yy, 1 - cc)), wo_bf, wo_buf,
                                                row_half, *gather_sems)

        @pl.when(i == 0)
        def _():
            wo_bf[...] = wo_ref[...].astype(BF16)
            start()

        pl.when(i == n_steps // 2)(forward)

        @pl.when(i == n_steps - 1)
        def _():
            finish()
            wo_all[...] = wo_buf[...]

        u = (x_ref[...] * (1.0 + sc_ref[...]) + sh_ref[...]).astype(BF16)
        u_ref[...] = u
        qkv_ref[...] = (_dot_nt(u, w_ref[O_QKV:O_F, :]) + b_ref[:, O_QKV:O_F]).astype(BF16)
        f_ref[...] = _dot_nt(u, w_ref[O_F:O_P, :]) + b_ref[:, O_F:O_P]
        p_ref[...] = _dot_nt(u, w_ref[O_P:O_G, :]) + b_ref[:, O_P:O_G]
        g_ref[...] = _dot_nt(u, w_ref[O_G:D_PAD, :]) + b_ref[:, O_G:D_PAD]

    row = lambda w: pl.BlockSpec((tm, w), lambda i: (i, 0))
    full = lambda a: pl.BlockSpec(a.shape, lambda i: (0, 0))
    vm = pl.BlockSpec(memory_space=pltpu.VMEM)
    return pl.pallas_call(
        body, name="in_proj", grid=(n_steps,),
        out_shape=(jax.ShapeDtypeStruct((S, D), BF16), jax.ShapeDtypeStruct((S, 3 * D_ATT), BF16),
                   jax.ShapeDtypeStruct((S, 128), F32), jax.ShapeDtypeStruct((S, D_POOL), F32),
                   jax.ShapeDtypeStruct((S, D), F32), jax.ShapeDtypeStruct((N_CHIPS,) + w_out_sh.shape, BF16)),
        in_specs=[row(D), full(shift), full(scale), full(wt_pad), full(b_pad), vm],
        out_specs=(row(D), row(3 * D_ATT), row(128), row(D_POOL), row(D), vm),
        scratch_shapes=[pltpu.VMEM((N_CHIPS,) + w_out_sh.shape, BF16), pltpu.VMEM(w_out_sh.shape, BF16)]
        + _gather_scratch(),
        compiler_params=_params(dimension_semantics=("arbitrary",)),
    )(x, shift, scale, wt_pad, b_pad, w_out_sh)


def _forget_cumsum(f):
    S = f.shape[0]
    tm = min(T_ATT, S)

    def body(f_ref, out_ref, carry):
        @pl.when(pl.program_id(0) == 0)
        def _():
            carry[...] = jnp.zeros_like(carry)
        v = f_ref[...]
        logf = jnp.minimum(v, 0.0) - jnp.log(1.0 + jnp.exp(-jnp.abs(v)))
        r = lax.broadcasted_iota(jnp.int32, (tm, tm), 0)
        c = lax.broadcasted_iota(jnp.int32, (tm, tm), 1)
        tri = (r <= c).astype(F32)
        rows8 = logf.T[0:8, :]
        cum8 = jnp.dot(rows8, tri, preferred_element_type=F32, precision=lax.Precision.HIGHEST) + carry[...]
        out_ref[...] = jnp.concatenate([cum8, jnp.zeros((128 - 8, tm), F32)], axis=0).T
        last = lax.broadcasted_iota(jnp.int32, (1, tm), 1) == tm - 1
        carry[...] = jnp.sum(jnp.where(last, cum8, 0.0), axis=1, keepdims=True)

    return pl.pallas_call(
        body, name="forget_cumsum", grid=(S // tm,),
        out_shape=jax.ShapeDtypeStruct((S, 128), F32),
        in_specs=[pl.BlockSpec((tm, 128), lambda i: (i, 0))],
        out_specs=pl.BlockSpec((tm, 128), lambda i: (i, 0)),
        scratch_shapes=[pltpu.VMEM((8, 1), F32)],
        compiler_params=_params(dimension_semantics=("arbitrary",)),
    )(f)


def _split3(v):
    hi = v.astype(BF16)
    rest = v - hi.astype(F32)
    mid = rest.astype(BF16)
    lo = (rest - mid.astype(F32)).astype(BF16)
    return hi, mid, lo


def _attention_fwd(qkv, big_f):
    S = qkv.shape[0]
    T = min(T_ATT, S)
    n_t = S // T

    def body(q_ref, k_ref, v_ref, f_ref, o_ref, lse_ref, kaug_sc, vt_sc, m_sc, l_sc, acc_sc):
        hp = pl.program_id(0)
        i = pl.program_id(1)
        lane = lax.broadcasted_iota(jnp.int32, (1, 128), 1)
        sub = lax.broadcasted_iota(jnp.int32, (128, 1), 0)
        head_sel = (lane < HEAD_DIM, lane >= HEAD_DIM)
        head_sel_t = (sub < HEAD_DIM, sub >= HEAD_DIM)
        spare = (HEAD_DIM, 0)
        zero = jnp.zeros((), BF16)

        @pl.when(i == 0)
        def _():
            def prep(jt, carry):
                rows = pl.ds(pl.multiple_of(jt * T, T), T)
                k = k_ref[rows, :]
                ft = f_ref[rows, :]
                vt = v_ref[rows, :].astype(F32).T
                for h in range(2):
                    fh = jnp.sum(jnp.where(lane == 2 * hp + h, ft, 0.0), axis=1, keepdims=True)
                    hi, mid, lo = _split3(-fh)
                    b = spare[h]
                    bias = jnp.where(lane == b, hi, jnp.where(lane == b + 1, mid, jnp.where(lane == b + 2, lo, zero)))
                    kaug_sc[h, rows, :] = jnp.where(head_sel[h], k, bias)
                    vt_sc[h, jt] = jnp.where(head_sel_t[h], vt, 0.0).astype(BF16)
                return carry

            lax.fori_loop(0, n_t, prep, 0)

        q = q_ref[...]
        q_heads = []
        for h in range(2):
            ones = jnp.where((lane >= spare[h]) & (lane < spare[h] + 3), jnp.ones((), BF16), zero)
            q_heads.append(jnp.where(head_sel[h], q, ones))
        m_sc[...] = jnp.full((8, T), NEG, F32)
        l_sc[...] = jnp.zeros((8, T), F32)
        acc_sc[...] = jnp.zeros((128, T), F32)

        def update(j, k_lo, n_k, q_lo, masked):
            rows = pl.ds(pl.multiple_of(j * T + k_lo, n_k), n_k)
            n_q = T - q_lo
            alphas, pvs = [], []
            for h in range(2):
                s_t = _dot_nt(kaug_sc[h, rows, :], q_heads[h][q_lo:, :])
                if masked:
                    rr = lax.broadcasted_iota(jnp.int32, (n_k, n_q), 0) + k_lo
                    cc = lax.broadcasted_iota(jnp.int32, (n_k, n_q), 1) + q_lo
                    s_t = jnp.where(rr <= cc, s_t, NEG)
                m_prev = m_sc[h:h + 1, q_lo:]
                m_new = jnp.maximum(m_prev, jnp.max(s_t, axis=0, keepdims=True))
                alpha = jnp.exp(m_prev - m_new)
                p_t = jnp.exp(s_t - m_new)
                l_sc[h:h + 1, q_lo:] = alpha * l_sc[h:h + 1, q_lo:] + jnp.sum(p_t, axis=0, keepdims=True)
                m_sc[h:h + 1, q_lo:] = m_new
                alphas.append(alpha)
                pvs.append(_dot(vt_sc[h, j, :, k_lo:k_lo + n_k], p_t.astype(BF16)))
            acc_sc[:, q_lo:] = (acc_sc[:, q_lo:] * jnp.where(head_sel_t[0], alphas[0], alphas[1])
                                + (pvs[0] + pvs[1]))

        def two_off_diagonal(jj, carry):
            update(2 * jj, 0, T, 0, False)
            update(2 * jj + 1, 0, T, 0, False)
            return carry

        lax.fori_loop(0, i // 2, two_off_diagonal, 0)

        @pl.when(i % 2 == 1)
        def _():
            update(i - 1, 0, T, 0, False)
            update(i, 0, T, 0, True)

        @pl.when(i % 2 == 0)
        def _():
            update(i, 0, T, 0, True)

        l = l_sc[...]
        o_ref[...] = (acc_sc[...] / jnp.where(head_sel_t[0], l[0:1, :], l[1:2, :])).T
        is_head = lax.broadcasted_iota(jnp.int32, (8, 1), 0) < 2
        lse_ref[...] = jnp.where(is_head, m_sc[...] + jnp.log(jnp.where(is_head, l, 1.0)), 0.0)

    return pl.pallas_call(
        body, name="attention_fwd", grid=(N_PAIR, n_t),
        out_shape=(jax.ShapeDtypeStruct((S, D_ATT), F32), jax.ShapeDtypeStruct((N_PAIR, n_t, 8, T), F32)),
        in_specs=[pl.BlockSpec((T, 128), lambda hp, i: (i, hp)),
                  pl.BlockSpec((S, 128), lambda hp, i: (0, N_PAIR + hp)),
                  pl.BlockSpec((S, 128), lambda hp, i: (0, 2 * N_PAIR + hp)),
                  pl.BlockSpec((S, 128), lambda hp, i: (0, 0))],
        out_specs=(pl.BlockSpec((T, 128), lambda hp, i: (i, hp)),
                   pl.BlockSpec((None, None, 8, T), lambda hp, i: (hp, i, 0, 0))),
        scratch_shapes=[pltpu.VMEM((2, S, 128), BF16), pltpu.VMEM((2, n_t, 128, T), BF16),
                        pltpu.VMEM((8, T), F32), pltpu.VMEM((8, T), F32), pltpu.VMEM((128, T), F32)],
        compiler_params=_params(dimension_semantics=("arbitrary", "arbitrary")),
    )(qkv, qkv, qkv, big_f)


def _attention_bwd(qkv, datt, att, lse, big_f, gw_out4):
    S = qkv.shape[0]
    T = min(T_ATT, S)
    n_t = S // T
    n_steps = N_PAIR * n_t
    marks = (0, n_steps // 8, n_steps // 2, n_steps // 2 + n_steps // 8)

    def body(q_ref, do_ref, o_ref, lse_ref, k_ref, v_ref, fk_ref, gout_ref,
             dq_ref, dk_ref, dv_ref, cs_ref, dfk_ref, dfq_ref, oout_ref, stat_sc, dqt_sc, qaug_sc,
             out_buf, *red_bufs):
        hp = pl.program_id(0)
        j = pl.program_id(1)
        x, y, cc = lax.axis_index("x"), lax.axis_index("y"), lax.axis_index("c")
        plan = _scatter_stages((x, y, cc, 2 * x + y, (x, y, 1 - cc)), gout_ref, None, out_buf, *red_bufs)
        step = hp * n_t + j
        for n, mark in enumerate(marks):
            @pl.when(step == mark)
            def _(n=n):
                if n > 0:
                    plan[n - 1][1]()
                if n < 3:
                    plan[n][0]()
                else:
                    oout_ref[...] = out_buf[...]

        lane = lax.broadcasted_iota(jnp.int32, (1, 128), 1)
        sub = lax.broadcasted_iota(jnp.int32, (128, 1), 0)
        head_sel = (lane < HEAD_DIM, lane >= HEAD_DIM)
        head_sel_t = (sub < HEAD_DIM, sub >= HEAD_DIM)
        spare = (HEAD_DIM, 0)
        zero = jnp.zeros((), BF16)
        one = jnp.ones((), BF16)

        def bias_lanes(first, pieces):
            hi, mid, lo = pieces
            return lambda rest: jnp.where(lane == first, hi, jnp.where(lane == first + 1, mid,
                                                                        jnp.where(lane == first + 2, lo, rest)))

        @pl.when(j == 0)
        def _():
            dqt_sc[...] = jnp.zeros_like(dqt_sc)
            cs_ref[...] = jnp.zeros_like(cs_ref)
            dfq_ref[...] = jnp.zeros_like(dfq_ref)

            def prep(i, carry):
                rows = pl.ds(pl.multiple_of(i * T, T), T)
                q = q_ref[rows, :]
                do = do_ref[rows, :]
                prod = o_ref[rows, :] * do.astype(F32)
                d_a = jnp.sum(jnp.where(head_sel[0], prod, 0.0), axis=1, keepdims=True)
                d_b = jnp.sum(jnp.where(head_sel[0], 0.0, prod), axis=1, keepdims=True)
                delta_t = jnp.where(head_sel[0], d_a, d_b).T
                stat_sc[i, 0:1, :] = delta_t[0:1, :]
                stat_sc[i, 1:2, :] = delta_t[HEAD_DIM:HEAD_DIM + 1, :]
                lse = lse_ref[i]
                lse_cols = jnp.where(head_sel_t[0], lse[0:1, :], lse[1:2, :]).T
                for h in range(2):
                    neg_lse = -lse_cols[:, h * HEAD_DIM:h * HEAD_DIM + 1]
                    ones = jnp.where((lane >= spare[h]) & (lane < spare[h] + 3), one, zero)
                    qaug_sc[h, rows, :] = jnp.where(head_sel[h], q, bias_lanes(spare[h] + 3, _split3(neg_lse))(ones))
                return carry

            lax.fori_loop(0, n_t, prep, 0)

        k = k_ref[...]
        v = v_ref[...]
        fk = fk_ref[...]
        kt = k.astype(F32).T
        heads = []
        for h in range(2):
            fkh = jnp.sum(jnp.where(lane == 2 * hp + h, fk, 0.0), axis=1, keepdims=True)
            ones = jnp.where((lane >= spare[h] + 3) & (lane < spare[h] + 6), one, zero)
            kaug = jnp.where(head_sel[h], k, bias_lanes(spare[h], _split3(-fkh))(ones))
            heads.append((kaug, jnp.where(head_sel[h], v, zero), jnp.where(head_sel_t[h], kt, 0.0).astype(BF16)))

        def block(i, k_lo, n_k, q_lo, masked):
            n_q = T - q_lo
            rows = pl.ds(pl.multiple_of(i * T + q_lo, n_q), n_q)
            q = q_ref[rows, :]
            do = do_ref[rows, :]
            stat = stat_sc[i]
            dk = jnp.zeros((n_k, 128), F32)
            dv = jnp.zeros((n_k, 128), F32)
            dqt = jnp.zeros((128, n_q), F32)
            dfs = []
            for h in range(2):
                kaug, vh, kth = heads[h]
                arg = _dot_nt(kaug[k_lo:k_lo + n_k, :], qaug_sc[h, rows, :])
                if masked:
                    rr = lax.broadcasted_iota(jnp.int32, (n_k, n_q), 0) + k_lo
                    cc = lax.broadcasted_iota(jnp.int32, (n_k, n_q), 1) + q_lo
                    arg = jnp.where(rr <= cc, arg, NEG)
                p_t = jnp.exp(arg)
                ds_t = p_t * (_dot_nt(vh[k_lo:k_lo + n_k, :], do) - stat[h:h + 1, q_lo:])
                ds_bf = ds_t.astype(BF16)
                dv = dv + _dot(p_t.astype(BF16), jnp.where(head_sel[h], do, zero))
                dk = dk + _dot(ds_bf, jnp.where(head_sel[h], q, zero))
                dqt = dqt + _dot(kth[:, k_lo:k_lo + n_k], ds_bf)
                dfs.append(jnp.sum(ds_t, axis=1, keepdims=True))
                dfq_ref[i, h:h + 1, q_lo:] += _colsum(ds_t)
            dqt_sc[i, :, q_lo:] += dqt
            return dk, dv, dfs[0], dfs[1]

        def off_diagonal(i, acc):
            return tuple(a + b for a, b in zip(acc, block(i, 0, T, 0, False)))

        half = T // 2

        def diagonal():
            early = block(j, 0, half, 0, True)
            late = block(j, half, half, half, True)
            return tuple(jnp.concatenate([a, b], axis=0) for a, b in zip(early, late))

        n_off = n_t - 1 - j
        odd = n_off % 2
        acc1 = lax.cond(odd == 1, lambda: off_diagonal(j + 1, diagonal()), diagonal)
        first = j + 1 + odd
        dk_acc, dv_acc, dfa, dfb = lax.fori_loop(
            0, n_off // 2, lambda ii, a: off_diagonal(first + 1 + 2 * ii, off_diagonal(first + 2 * ii, a)), acc1)
        dk_ref[...] = dk_acc.astype(BF16)
        dv_ref[...] = dv_acc.astype(BF16)
        dfk_ref[...] = -jnp.where(lane == 0, dfa, jnp.where(lane == 1, dfb, 0.0))
        cs_ref[:, 128:256] = cs_ref[:, 128:256] + _colsum(dk_acc)
        cs_ref[:, 256:384] = cs_ref[:, 256:384] + _colsum(dv_acc)

        @pl.when(j == n_t - 1)
        def _():
            def finish(i, tot):
                dq = dqt_sc[i].T
                dq_ref[pl.ds(pl.multiple_of(i * T, T), T), :] = dq.astype(BF16)
                return tot + _colsum(dq)

            cs_ref[:, 0:128] = lax.fori_loop(0, n_t, finish, jnp.zeros((1, 128), F32))

    pair_rows = lambda hp, j: (hp, 0, 0)
    vm = pl.BlockSpec(memory_space=pltpu.VMEM)
    _, r_out, c_out = gw_out4.shape
    return pl.pallas_call(
        body, name="attention_bwd", grid=(N_PAIR, n_t),
        out_shape=(jax.ShapeDtypeStruct((S, D_ATT), BF16), jax.ShapeDtypeStruct((S, D_ATT), BF16),
                   jax.ShapeDtypeStruct((S, D_ATT), BF16), jax.ShapeDtypeStruct((N_PAIR, 1, 384), F32),
                   jax.ShapeDtypeStruct((N_PAIR, S, 128), F32),
                   jax.ShapeDtypeStruct((N_PAIR, n_t, 8, T), F32),
                   jax.ShapeDtypeStruct((r_out, c_out), F32)),
        in_specs=[pl.BlockSpec((S, 128), lambda hp, j: (0, hp)),
                  pl.BlockSpec((S, 128), lambda hp, j: (0, hp)),
                  pl.BlockSpec((S, 128), lambda hp, j: (0, hp)),
                  pl.BlockSpec((None, n_t, 8, T), lambda hp, j: (hp, 0, 0, 0)),
                  pl.BlockSpec((T, 128), lambda hp, j: (j, N_PAIR + hp)),
                  pl.BlockSpec((T, 128), lambda hp, j: (j, 2 * N_PAIR + hp)),
                  pl.BlockSpec((T, 128), lambda hp, j: (j, 0)),
                  vm],
        out_specs=(pl.BlockSpec((S, 128), lambda hp, j: (0, hp)),
                   pl.BlockSpec((T, 128), lambda hp, j: (j, hp)),
                   pl.BlockSpec((T, 128), lambda hp, j: (j, hp)),
                   pl.BlockSpec((None, 1, 384), pair_rows),
                   pl.BlockSpec((None, T, 128), lambda hp, j: (hp, j, 0)),
                   pl.BlockSpec((None, n_t, 8, T), lambda hp, j: (hp, 0, 0, 0)),
                   vm),
        scratch_shapes=[pltpu.VMEM((n_t, 8, T), F32), pltpu.VMEM((n_t, 128, T), F32),
                        pltpu.VMEM((2, S, 128), BF16), pltpu.VMEM((r_out, c_out), F32)]
        + _scatter_scratch(r_out, c_out),
        compiler_params=_params(dimension_semantics=("arbitrary", "arbitrary")),
    )(qkv, datt, att, lse, qkv, qkv, big_f, gw_out4)


def _window_counts(first_row, n_rows, window):
    t = lax.broadcasted_iota(jnp.int32, (n_rows, 1), 0) + first_row
    return jnp.minimum((t + 1).astype(F32), float(window))


def _middle(x, tgt, att, g, p, gate, w_mix, b_mix, pool_scale, w_out, b_out, ln_g, ln_b):
    S = x.shape[0]
    tm = min(TM_MID, S)
    halo_blocks = tm // POOL_HALO

    def body(x_ref, t_ref, att_ref, g_ref, p_ref, ph_ref, gate_ref, wm_ref, bm_ref, ps_ref, wo_ref, bo_ref,
             lg_ref, lb_ref,
             dh_ref, datt_ref, dg_ref, dpl_ref, gwo_ref, gwm_ref, vec_ref, loss_ref):
        i = pl.program_id(0)

        @pl.when(i == 0)
        def _():
            gwo_ref[...] = jnp.zeros_like(gwo_ref)
            gwm_ref[...] = jnp.zeros_like(gwm_ref)
            vec_ref[...] = jnp.zeros_like(vec_ref)
            loss_ref[...] = jnp.zeros_like(loss_ref)

        pc = p_ref[...]
        halo = jnp.where(i > 0, ph_ref[...], 0.0)
        pe = jnp.concatenate([halo, pc], axis=0)
        pooled_parts = []
        for gi, w in enumerate(POOL_WINDOWS):
            cur = pe[:, gi * POOL_GROUP:(gi + 1) * POOL_GROUP]
            span = 1
            while span < w:
                cur = cur + pltpu.roll(cur, span, 0)
                span *= 2
            wsum = cur[POOL_HALO:, :]
            mean = wsum / _window_counts(i * tm, tm, w)
            pooled_parts.append(mean - pc[:, gi * POOL_GROUP:(gi + 1) * POOL_GROUP])
        pooled_bf =[v.astype(BF16) for v in pooled_parts]
        wm = [wm_ref[gi].astype(BF16) for gi in range(4)]
        mixed = jnp.concatenate([_dot(pooled_bf[gi], wm[gi]) for gi in range(4)], axis=1) + bm_ref[...]
        ps = ps_ref[...]
        pool_out = mixed * ps
        gv = g_ref[...]
        sig = _sigmoid(gv)
        silu = gv * sig
        att = att_ref[...]
        y = jnp.concatenate([att * silu[:, :D_ATT], pool_out * silu[:, D_ATT:]], axis=1)
        y_bf = y.astype(BF16)
        wo = wo_ref[...]
        yo = _dot(y_bf, wo) + bo_ref[...]
        gate = gate_ref[...]
        h = ALPHA * x_ref[...] + gate * yo
        mu = jnp.mean(h, axis=1, keepdims=True)
        hc = h - mu
        var = jnp.mean(hc * hc, axis=1, keepdims=True)
        rstd = lax.rsqrt(var + LN_EPS)
        yhat = hc * rstd
        lg = lg_ref[...]
        out = yhat * lg + lb_ref[...]
        err = out - t_ref[...]
        loss_ref[...] += 0.5 * jnp.sum(jnp.mean(err * err, axis=1, keepdims=True), axis=0, keepdims=True)

        dout = err * (1.0 / D)
        g_ln_b = _colsum(dout)
        g_ln_g = _colsum(dout * yhat)
        dyh = dout * lg
        dh = rstd * (dyh - jnp.mean(dyh, axis=1, keepdims=True)
                     - yhat * jnp.mean(dyh * yhat, axis=1, keepdims=True))
        dh_ref[...] = dh
        d_gate = _colsum(dh * yo)
        dyo = gate * dh
        g_b_out = _colsum(dyo)
        dyo_bf = dyo.astype(BF16)
        gwo_ref[...] += _dot_tn(y_bf, dyo_bf)
        dy = _dot_nt(dyo_bf, wo)
        dsilu = sig * (1.0 + gv * (1.0 - sig))
        dy_a = dy[:, :D_ATT]
        dy_p = dy[:, D_ATT:]
        datt_ref[...] = (dy_a * silu[:, :D_ATT]).astype(BF16)
        dpo = dy_p * silu[:, D_ATT:]
        dg = jnp.concatenate([dy_a * att * dsilu[:, :D_ATT], dy_p * pool_out * dsilu[:, D_ATT:]], axis=1)
        dg_ref[...] = dg.astype(BF16)
        g_dg = _colsum(dg)
        g_ps = _colsum(dpo * mixed)
        dmixed = dpo * ps
        g_bm = _colsum(dmixed)
        dmixed_bf = dmixed.astype(BF16)
        dpl = []
        for gi in range(4):
            dm = dmixed_bf[:, gi * POOL_GROUP:(gi + 1) * POOL_GROUP]
            gwm_ref[gi] += _dot_tn(pooled_bf[gi], dm)
            dpl.append(_dot_nt(dm, wm[gi]))
        dpl_ref[...] = jnp.concatenate(dpl, axis=1)
        vec_ref[0:1, :] += g_ln_g
        vec_ref[1:2, :] += g_ln_b
        vec_ref[2:3, :] += d_gate
        vec_ref[3:4, :] += g_b_out
        vec_ref[4:5, :] += g_dg
        vec_ref[5:6, 0:D_POOL] += g_ps
        vec_ref[6:7, 0:D_POOL] += g_bm

    row = lambda w: pl.BlockSpec((tm, w), lambda i: (i, 0))
    full2 = lambda a: pl.BlockSpec(a.shape, lambda i: (0, 0))
    full3 = lambda a: pl.BlockSpec(a.shape, lambda i: (0, 0, 0))
    return pl.pallas_call(
        body, name="middle", grid=(S // tm,),
        out_shape=(jax.ShapeDtypeStruct((S, D), F32),
                   jax.ShapeDtypeStruct((S, D_ATT), BF16),
                   jax.ShapeDtypeStruct((S, D), BF16),
                   jax.ShapeDtypeStruct((S, D_POOL), F32),
                   jax.ShapeDtypeStruct((D, D), F32),
                   jax.ShapeDtypeStruct((4, POOL_GROUP, POOL_GROUP), F32),
                   jax.ShapeDtypeStruct((8, D), F32),
                   jax.ShapeDtypeStruct((1, 1), F32)),
        in_specs=[row(D), row(D), row(D_ATT), row(D), row(D_POOL),
                  pl.BlockSpec((POOL_HALO, D_POOL), lambda i: (jnp.maximum(i * halo_blocks - 1, 0), 0)),
                  full2(gate), full3(w_mix), full2(b_mix), full2(pool_scale), full2(w_out), full2(b_out),
                  full2(ln_g), full2(ln_b)],
        out_specs=(row(D), row(D_ATT), row(D), row(D_POOL),
                   pl.BlockSpec((D, D), lambda i: (0, 0)),
                   pl.BlockSpec((4, POOL_GROUP, POOL_GROUP), lambda i: (0, 0, 0)),
                   pl.BlockSpec((8, D), lambda i: (0, 0)),
                   pl.BlockSpec((1, 1), lambda i: (0, 0))),
        compiler_params=_params(dimension_semantics=("arbitrary",)),
    )(x, tgt, att, g, p, p, gate, w_mix, b_mix, pool_scale, w_out, b_out, ln_g, ln_b)


def _tail(dpl, dfk, dfq, f):
    S = dpl.shape[0]
    tm = min(T_ATT, S)
    n_t = S // tm
    halo_blocks = tm // POOL_HALO
    last_halo = S // POOL_HALO - 1

    def body(d_ref, dn_ref, dfk_ref, dfq_ref, f_ref, dp_ref, df_ref, cs_ref, carry):
        s = pl.program_id(0)
        i = n_t - 1 - s

        @pl.when(s == 0)
        def _():
            carry[...] = jnp.zeros_like(carry)
            cs_ref[...] = jnp.zeros_like(cs_ref)

        dc = d_ref[...]
        nxt = jnp.where(s > 0, dn_ref[...], 0.0)
        de = jnp.concatenate([dc, nxt], axis=0)
        n_e = tm + POOL_HALO
        parts = []
        for gi, w in enumerate(POOL_WINDOWS):
            cur = de[:, gi * POOL_GROUP:(gi + 1) * POOL_GROUP] / _window_counts(i * tm, n_e, w)
            span = 1
            while span < w:
                cur = cur + pltpu.roll(cur, n_e - span, 0)
                span *= 2
            parts.append(cur[:tm, :] - dc[:, gi * POOL_GROUP:(gi + 1) * POOL_GROUP])
        dp = jnp.concatenate(parts, axis=1)
        dp_ref[...] = dp.astype(BF16)
        cs_ref[0:1, :] += _colsum(dp)

        r = lax.broadcasted_iota(jnp.int32, (tm, tm), 0)
        c = lax.broadcasted_iota(jnp.int32, (tm, tm), 1)
        tri = (r >= c).astype(F32)
        k_cols = dfk_ref[0]
        rows8 = dfq_ref[0]
        for hp in range(1, N_PAIR):
            k_cols = k_cols + pltpu.roll(dfk_ref[hp], 2 * hp, 1)
            rows8 = rows8 + pltpu.roll(dfq_ref[hp], 2 * hp, 0)
        rows8 = rows8 + k_cols.T[0:8, :]
        dlogf8 = jnp.dot(rows8, tri, preferred_element_type=F32, precision=lax.Precision.HIGHEST) + carry[...]
        first = lax.broadcasted_iota(jnp.int32, (1, tm), 1) == 0
        carry[...] = jnp.sum(jnp.where(first, dlogf8, 0.0), axis=1, keepdims=True)
        dlogf = jnp.concatenate([dlogf8, jnp.zeros((128 - 8, tm), F32)], axis=0).T
        df = dlogf * _sigmoid(-f_ref[...])
        df_ref[...] = df.astype(BF16)
        cs_ref[1:2, 0:128] += _colsum(df)

    rev = lambda w: pl.BlockSpec((tm, w), lambda s: (n_t - 1 - s, 0))
    return pl.pallas_call(
        body, name="tail", grid=(n_t,),
        out_shape=(jax.ShapeDtypeStruct((S, D_POOL), BF16), jax.ShapeDtypeStruct((S, 128), BF16),
                   jax.ShapeDtypeStruct((8, D_POOL), F32)),
        in_specs=[rev(D_POOL),
                  pl.BlockSpec((POOL_HALO, D_POOL),
                               lambda s: (jnp.minimum((n_t - s) * halo_blocks, last_halo), 0)),
                  pl.BlockSpec((N_PAIR, tm, 128), lambda s: (0, n_t - 1 - s, 0)),
                  pl.BlockSpec((N_PAIR, None, 8, tm), lambda s: (0, n_t - 1 - s, 0, 0)),
                  rev(128)],
        out_specs=(rev(D_POOL), rev(128), pl.BlockSpec((8, D_POOL), lambda s: (0, 0))),
        scratch_shapes=[pltpu.VMEM((8, 1), F32)],
        compiler_params=_params(dimension_semantics=("arbitrary",)),
    )(dpl, dpl, dfk, dfq, f)


PIECES = ((O_QKV, D_ATT), (O_QKV + D_ATT, D_ATT), (O_QKV + 2 * D_ATT, D_ATT), (O_F, 128), (O_P, D_POOL), (O_G, D))


def _grad_w_in(u, pieces):
    S = u.shape[0]
    tm = min(TM_GW, S)
    n_t = S // tm

    def body(u_ref, *rest):
        piece_refs, out_ref, acc, sem = rest[:6], rest[6], rest[7], rest[8]
        i = pl.program_id(0)

        @pl.when(i == 0)
        def _():
            acc[...] = jnp.zeros_like(acc)

        u_t = u_ref[...]
        for (off, w), ref in zip(PIECES, piece_refs):
            acc[:, off:off + w] += _dot_tn(u_t, ref[...])

        @pl.when(i == n_t - 1)
        def _():
            cp = pltpu.make_async_copy(acc, out_ref, sem)
            cp.start()
            cp.wait()

    return pl.pallas_call(
        body, name="grad_w_in", grid=(n_t,),
        out_shape=jax.ShapeDtypeStruct((D, D_PAD), F32),
        in_specs=[pl.BlockSpec((tm, D), lambda i: (i, 0))]
        + [pl.BlockSpec((tm, w), lambda i: (i, 0)) for _, w in PIECES],
        out_specs=pl.BlockSpec(memory_space=pl.ANY),
        scratch_shapes=[pltpu.VMEM((D, D_PAD), F32), pltpu.SemaphoreType.DMA],
        compiler_params=_params(dimension_semantics=("arbitrary",)),
    )(u, *pieces)


def _grad_x(pieces, wt_pad, dh, x, scale):
    S = x.shape[0]
    tm = min(TM_DU, S)

    def body(*refs):
        piece_refs = refs[:6]
        w_ref, dh_ref, x_ref, sc_ref, gx_ref, vec_ref = refs[6:]

        @pl.when(pl.program_id(0) == 0)
        def _():
            vec_ref[...] = jnp.zeros_like(vec_ref)

        du = jnp.zeros((tm, D), F32)
        for (off, w), ref in zip(PIECES, piece_refs):
            du = du + _dot(ref[...], w_ref[off:off + w, :])
        xv = x_ref[...]
        gx_ref[...] = ALPHA * dh_ref[...] + du * (1.0 + sc_ref[...])
        vec_ref[0:1, :] += _colsum(du)
        vec_ref[1:2, :] += _colsum(du * xv)

    row = lambda w: pl.BlockSpec((tm, w), lambda i: (i, 0))
    return pl.pallas_call(
        body, name="grad_x", grid=(S // tm,),
        out_shape=(jax.ShapeDtypeStruct((S, D), F32), jax.ShapeDtypeStruct((8, D), F32)),
        in_specs=[row(w) for _, w in PIECES]
        + [pl.BlockSpec(wt_pad.shape, lambda i: (0, 0)), row(D), row(D), pl.BlockSpec((1, D), lambda i: (0, 0))],
        out_specs=(row(D), pl.BlockSpec((8, D), lambda i: (0, 0))),
        compiler_params=_params(dimension_semantics=("arbitrary",)),
    )(*pieces, wt_pad, dh, x, scale)


def _adamw_math(w, g, m, v):
    m = ADAM_B1 * m + (1.0 - ADAM_B1) * g
    v = ADAM_B2 * v + (1.0 - ADAM_B2) * (g * g)
    m_hat = m / (1.0 - ADAM_B1 ** ADAM_STEP)
    v_hat = v / (1.0 - ADAM_B2 ** ADAM_STEP)
    delta = -ADAM_LR * (m_hat / (jnp.sqrt(v_hat) + ADAM_EPS) + ADAM_WD * w)
    return delta, m, v


def _adamw(groups, n_steps):
    n = len(groups)

    def body(*refs):
        ins, outs = refs[:4 * n], refs[4 * n:]
        for t in range(n):
            w, g, m, v = (r[...] for r in ins[4 * t:4 * t + 4])
            d, m2, v2 = _adamw_math(w, g, m, v)
            outs[4 * t][...] = d
            outs[4 * t + 1][...] = m2
            outs[4 * t + 2][...] = v2
            outs[4 * t + 3][...] = g

    in_specs, out_specs, out_shape, args = [], [], [], []
    for (w, g, m, v) in groups:
        rest = w.shape[1:]
        spec = pl.BlockSpec((w.shape[0] // n_steps,) + rest, lambda i, nd=len(rest): (i,) + (0,) * nd)
        in_specs += [spec] * 4
        out_specs += [spec] * 4
        out_shape += [jax.ShapeDtypeStruct(w.shape, F32)] * 4
        args += [w, g, m, v]
    return pl.pallas_call(
        body, name="adamw_%d_%d" % (n, n_steps), grid=(n_steps,),
        out_shape=tuple(out_shape), in_specs=in_specs, out_specs=tuple(out_specs),
        compiler_params=_params(dimension_semantics=("arbitrary",)),
    )(*args)


def _adamw_small(small_sum, g_b_ada, params):
    n = len(params)

    def body(gs_ref, gba_ref, *refs):
        ins, outs = refs[:3 * n], refs[3 * n:]
        for t, (name, w0, _, _) in enumerate(params):
            w_ref, m_ref, v_ref = ins[3 * t:3 * t + 3]
            first = SMALL_SEGS[name][0] if name in SMALL_SEGS else None
            if w0.shape[0] > 1:
                pieces = [((slice(None), slice(None)), gs_ref[first:first + w0.shape[0], :])]
            else:
                pieces = []
                for r in range(-(-w0.shape[1] // 128)):
                    lanes = slice(128 * r, min(128 * r + 128, w0.shape[1]))
                    g = gba_ref[0:1, lanes] if first is None else gs_ref[first + r:first + r + 1, 0:lanes.stop - lanes.start]
                    pieces.append(((slice(0, 1), lanes), g))
            for where, g in pieces:
                d, m2, v2 = _adamw_math(w_ref[where], g, m_ref[where], v_ref[where])
                for ref, val in zip(outs[4 * t:4 * t + 4], (g, d, m2, v2)):
                    ref[where] = val

    vm = pl.BlockSpec(memory_space=pltpu.VMEM)
    args = [small_sum, g_b_ada]
    out_shape = []
    for _, w, m, v in params:
        args += [w, m, v]
        out_shape += [jax.ShapeDtypeStruct(w.shape, F32)] * 4
    return pl.pallas_call(
        body, name="adamw_small",
        out_shape=tuple(out_shape), in_specs=[vm] * len(args), out_specs=(vm,) * len(out_shape),
        compiler_params=_params(),
    )(*args)


def _pack_small(parts):
    rows = []
    used = 0
    for name, (first, n_rows) in SMALL_SEGS.items():
        if first > used:
            rows.append(jnp.zeros((first - used, 128), F32))
        flat = parts[name].reshape(-1)
        flat = jnp.pad(flat, (0, n_rows * 128 - flat.shape[0]))
        rows.append(flat.reshape(n_rows, 128))
        used = first + n_rows
    rows.append(jnp.zeros((SMALL_ROWS - used, 128), F32))
    return jnp.concatenate(rows, axis=0)


def _pad_in(v):
    r = v.shape[0]
    z = jnp.zeros((r, O_P - O_F - N_HEADS), v.dtype)
    return jnp.concatenate([v[:, :3 * D_ATT + N_HEADS], z, v[:, 3 * D_ATT + N_HEADS:]], axis=1)


def _unpad_in(v):
    return jnp.concatenate([v[:, :O_F + N_HEADS], v[:, O_P:]], axis=1)


def _shards_in(v):
    gap = O_P - (O_F + N_HEADS)
    parts = []
    for a in range(N_CHIPS):
        lo, hi = a * SHARD_IN, (a + 1) * SHARD_IN
        cut = O_F + N_HEADS
        if hi <= cut:
            parts.append(v[:, lo:hi])
        elif lo >= cut:
            parts.append(v[:, lo + gap:hi + gap])
        else:
            parts.append(jnp.concatenate([v[:, lo:cut], v[:, cut + gap:hi + gap]], axis=1))
    return jnp.stack(parts, axis=0)


def kernel(x, c, w_ada, b_ada, w_in, b_in, w_pool_mix, b_pool_mix, pool_scale, w_out, b_out, ln_g, ln_b, loss_target, m_w_ada, m_b_ada, m_w_in, m_b_in, m_w_pool_mix, m_b_pool_mix, m_pool_scale, m_w_out, m_b_out, m_ln_g, m_ln_b, v_w_ada, v_b_ada, v_w_in, v_b_in, v_w_pool_mix, v_b_pool_mix, v_pool_scale, v_w_out, v_b_out, v_ln_g, v_ln_b):
    S = x.shape[1]
    T = min(T_ATT, S)
    n_t = S // T
    x2 = x[0]
    tgt = loss_target[0]
    q_scale = jnp.concatenate([jnp.full((1, D_ATT), Q_SCALE, F32), jnp.ones((1, D_PAD - D_ATT), F32)], axis=1)

    to_cols = lambda a: jnp.transpose(a, (2, 0, 1))
    from_cols = lambda a: jnp.transpose(a, (1, 2, 0))
    c_all, shift, scale, gate, wt_pad = _gather_and_ada(
        c, w_ada[0], b_ada.reshape(4, 1, SHARD_ADA), to_cols(w_in))
    b_pad = _pad_in(b_in) * q_scale

    u, qkv, f, p, g, w_out_all = _in_proj(x2, shift, scale, wt_pad, b_pad, w_out[0])
    w_out_full = w_out_all.reshape(D, D)
    big_f = _forget_cumsum(f)
    att, lse = _attention_fwd(qkv, big_f)

    dh, datt, dg, dpl, gw_out, gw_mix, vec, loss_part = _middle(
        x2, tgt, att, g, p, gate, w_pool_mix[0], b_pool_mix.reshape(1, D_POOL), pool_scale, w_out_full, b_out, ln_g, ln_b)
    dq, dk, dv, cs_att, dfk, dfq, g_w_out = _attention_bwd(
        qkv, datt, att, lse, big_f, gw_out.reshape(N_CHIPS, SHARD_OUT, D))
    dp, df, cs_tail = _tail(dpl, dfk, dfq, f)
    pieces = (dq, dk, dv, df, dp, dg)
    gw_pad = _grad_w_in(u, pieces)
    grad_x, vec_x = _grad_x(pieces, wt_pad, dh, x2, scale)

    cs_qkv = jnp.transpose(cs_att.reshape(N_PAIR, 3, 128), (1, 0, 2)).reshape(1, 3 * D_ATT)
    gb_pad = jnp.concatenate([cs_qkv, cs_tail[1:2, 0:128], cs_tail[0:1, :], vec[4:5, :]], axis=1) * q_scale
    dada = jnp.concatenate([vec_x[0:1, :], vec_x[1:2, :], vec[2:3, :]], axis=1)
    small = _pack_small({
        "b_in": _unpad_in(gb_pad), "w_pool_mix": gw_mix, "b_pool_mix": vec[6:7, :D_POOL],
        "pool_scale": vec[5:6, :D_POOL], "b_out": vec[3:4, :], "ln_g": vec[0:1, :], "ln_b": vec[1:2, :],
        "loss": loss_part})

    g_w_in, small_sum, g_w_ada, g_b_ada, loss = _reduce_all(
        gw_pad, _shards_in(q_scale), small, dada, c_all)

    big = _adamw([(w_ada[0], g_w_ada, m_w_ada[0], v_w_ada[0]),
                  (w_out[0], g_w_out, m_w_out[0], v_w_out[0])], 2)
    big_in = _adamw([(to_cols(w_in), g_w_in, to_cols(m_w_in), to_cols(v_w_in))], 2)
    tiles = lambda a: a.reshape(4 * POOL_GROUP, POOL_GROUP)
    flat = lambda a: a.reshape(1, D_POOL)
    small_params = [("b_ada", b_ada, m_b_ada, v_b_ada), ("b_in", b_in, m_b_in, v_b_in),
                    ("w_pool_mix", tiles(w_pool_mix), tiles(m_w_pool_mix), tiles(v_w_pool_mix)),
                    ("b_pool_mix", flat(b_pool_mix), flat(m_b_pool_mix), flat(v_b_pool_mix)),
                    ("pool_scale", pool_scale, m_pool_scale, v_pool_scale), ("b_out", b_out, m_b_out, v_b_out),
                    ("ln_g", ln_g, m_ln_g, v_ln_g), ("ln_b", ln_b, m_ln_b, v_ln_b)]
    sm = _adamw_small(small_sum, g_b_ada, small_params)
    sm_idx = {p[0]: n for n, p in enumerate(small_params)}
    shapes = {"w_pool_mix": (1, 4, POOL_GROUP, POOL_GROUP), "b_pool_mix": (1, 4, POOL_GROUP)}

    names = ["w_ada", "b_ada", "w_in", "b_in", "w_pool_mix", "b_pool_mix", "pool_scale", "w_out", "b_out",
             "ln_g", "ln_b"]
    big_idx = {"w_ada": 0, "w_out": 1}

    def leaf(kind, name):
        if name == "w_in":
            return from_cols(big_in[(kind - 1) % 4])
        if name in big_idx:
            return big[4 * big_idx[name] + (kind - 1) % 4][None]
        val = sm[4 * sm_idx[name] + kind]
        return val.reshape(shapes[name]) if name in shapes else val

    outs = [loss.reshape(()), grad_x[None]]
    for kind in range(4):
        outs += [leaf(kind, n) for n in names]
    return tuple(outs)
```

```python
import functools

import numpy as np
import jax
import jax.numpy as jnp
from jax import lax
from jax.experimental import pallas as pl
from jax.experimental.pallas import tpu as pltpu

F32 = jnp.float32
BF16 = jnp.bfloat16
MESH = pl.DeviceIdType.MESH

D = 1024
D_ATT = 512
D_POOL = 512
N_HEADS = 8
HEAD_DIM = 64
N_PAIR = N_HEADS // 2
POOL_WINDOWS = (2, 4, 8, 16)
POOL_GROUP = 128
POOL_HALO = 16
LN_EPS = 1e-5
ALPHA = 2.0 ** 0.25
D_IN = 3 * D_ATT + N_HEADS + D_POOL + D_ATT + D_POOL
N_CHIPS = 4
SHARD_IN = D_IN // N_CHIPS
SHARD_ADA = 3 * D // N_CHIPS
SHARD_OUT = D // N_CHIPS

O_QKV, O_F, O_P, O_G, D_PAD = 0, 1536, 1664, 2176, 3200
Q_SCALE = HEAD_DIM ** -0.5

ADAM_LR, ADAM_B1, ADAM_B2, ADAM_EPS, ADAM_WD, ADAM_STEP = 0.001, 0.9, 0.999, 1e-08, 0.01, 10

NEG = -1e30

VMEM_LIMIT = 56 * 1024 * 1024

TM_PROJ = 512
T_ATT = 512
TM_MID = 512
TM_GW = 1024
TM_DU = 512

REL7 = [(0, 0, 1), (0, 1, 0), (0, 1, 1), (1, 0, 0), (1, 0, 1), (1, 1, 0), (1, 1, 1)]
REL3 = [(0, 1), (1, 0), (1, 1)]

SMALL_SEGS = {}
_row = 0
for _name, _n in (("b_in", D_IN), ("w_pool_mix", 65536), ("b_pool_mix", 512), ("pool_scale", 512),
                  ("b_out", 1024), ("ln_g", 1024), ("ln_b", 1024), ("loss", 1)):
    _rows = -(-_n // 1024) * 8
    SMALL_SEGS[_name] = (_row, _rows)
    _row += _rows
SMALL_ROWS = -(-_row // 16) * 16


def _params(**kw):
    return pltpu.CompilerParams(vmem_limit_bytes=VMEM_LIMIT, **kw)


def _flip(v, d):
    return v if d == 0 else 1 - v


def _dot(a, b):
    return jnp.dot(a, b, preferred_element_type=F32)


def _dot_nt(a, b):
    return lax.dot_general(a, b, (((1,), (1,)), ((), ())), preferred_element_type=F32)


def _dot_tn(a, b):
    return lax.dot_general(a, b, (((0,), (0,)), ((), ())), preferred_element_type=F32)


def _sigmoid(v):
    return 1.0 / (1.0 + jnp.exp(-v))


def _colsum(v):
    return jnp.sum(v, axis=0, keepdims=True)


def _gather_stages(pos, src_ref, dst_ref, half, own_sem, s_sem, r_sem, fs_sem, fr_sem):
    x, y, cc, chip, sib = pos
    own = pltpu.make_async_copy(src_ref, dst_ref.at[chip], own_sem)
    first, landed, others = [], [], []
    for k, (dx, dy) in enumerate(REL3):
        px, py = _flip(x, dx), _flip(y, dy)
        first.append(pltpu.make_async_remote_copy(
            src_ref=src_ref.at[half(cc)], dst_ref=dst_ref.at[(chip,) + half(cc)],
            send_sem=s_sem.at[k], recv_sem=r_sem.at[k], device_id=(px, py, cc), device_id_type=MESH))
        landed.append(dst_ref.at[(2 * px + py,) + half(cc)])
        others.append(dst_ref.at[(2 * px + py,) + half(1 - cc)])
    passed = [pltpu.make_async_remote_copy(src_ref=landed[k], dst_ref=landed[k], send_sem=fs_sem.at[k],
                                           recv_sem=fr_sem.at[k], device_id=sib, device_id_type=MESH)
              for k in range(3)]

    def start(finish_src=None):
        for cp in first:
            cp.start()
        if finish_src is not None:
            finish_src()
        own.start()

    def forward():
        for k in range(3):
            pltpu.make_async_remote_copy(src_ref=landed[k], dst_ref=landed[k], send_sem=s_sem.at[k],
                                         recv_sem=r_sem.at[k], device_id=sib, device_id_type=MESH).wait_recv()
            passed[k].start()

    def finish():
        for k in range(3):
            pltpu.make_async_remote_copy(src_ref=others[k], dst_ref=others[k], send_sem=fs_sem.at[k],
                                         recv_sem=fr_sem.at[k], device_id=sib, device_id_type=MESH).wait_recv()
        for cp in first + passed:
            cp.wait_send()
        own.wait()

    return start, forward, finish


def _gather_scratch():
    return [pltpu.SemaphoreType.DMA, pltpu.SemaphoreType.DMA((3,)), pltpu.SemaphoreType.DMA((3,)),
            pltpu.SemaphoreType.DMA((3,)), pltpu.SemaphoreType.DMA((3,))]


def _gather_and_ada(c, w_ada, b_ada4, w_in_sh):
    def body(c_ref, w_ref, b_ref, win_ref, call_ref, shift_ref, scale_ref, gate_ref, wt_pad_ref,
             win_all, win_bf, ada_ref, cslab, sbuf, rbuf, cs_sem, cr_sem, as_sem, ar_sem, *gather_sems):
        x, y, cc = lax.axis_index("x"), lax.axis_index("y"), lax.axis_index("c")
        me = 4 * x + 2 * y + cc
        chip = 2 * x + y
        lane_half = lambda which: (slice(None), pl.ds(pl.multiple_of(which * (D // 2), D // 2), D // 2))
        def round_half(which):
            for h in range(2):
                @pl.when(which == h)
                def _():
                    lanes = slice(h * (D // 2), (h + 1) * (D // 2))
                    win_bf[:, lanes] = win_ref[:, 0, lanes].astype(BF16)

        start, forward, finish = _gather_stages((x, y, cc, chip, (x, y, 1 - cc)), win_bf, win_all, lane_half,
                                                *gather_sems)
        round_half(cc)
        start(lambda: round_half(1 - cc))

        cslab[...] = jnp.broadcast_to(c_ref[...], (8, D))
        call_ref[me] = cslab[...]
        gathers = []
        for k, (dx, dy, dc) in enumerate(REL7):
            cp = pltpu.make_async_remote_copy(
                src_ref=cslab, dst_ref=call_ref.at[me], send_sem=cs_sem.at[k], recv_sem=cr_sem.at[k],
                device_id=(_flip(x, dx), _flip(y, dy), _flip(cc, dc)), device_id_type=MESH)
            cp.start()
            gathers.append(cp)
        for cp in gathers:
            cp.wait()
        slab_row = lax.broadcasted_iota(jnp.int32, (8, 1), 0)
        mat = jnp.zeros((8, D), F32)
        for r in range(8):
            mat = jnp.where(slab_row == r, call_ref[r], mat)
        act = (mat * _sigmoid(mat)).astype(BF16)
        part = _dot(act, w_ref[...].astype(BF16))
        sends = []
        for k, (dx, dy) in enumerate(REL3):
            px, py = _flip(x, dx), _flip(y, dy)
            r = 4 * px + 2 * py + cc
            piece = _colsum(jnp.where(slab_row == r, part, 0.0))
            sbuf[k] = jnp.broadcast_to(piece, (8, SHARD_ADA))
            cp = pltpu.make_async_remote_copy(
                src_ref=sbuf.at[k], dst_ref=rbuf.at[k], send_sem=as_sem.at[k], recv_sem=ar_sem.at[k],
                device_id=(px, py, cc), device_id_type=MESH)
            cp.start()
            sends.append(cp)
        own_piece = _colsum(jnp.where(slab_row == me, part, 0.0))
        ada_ref[chip] = jnp.broadcast_to(own_piece, (8, SHARD_ADA)) + b_ref[chip]
        for k, (dx, dy) in enumerate(REL3):
            sends[k].wait()
            a = 2 * _flip(x, dx) + _flip(y, dy)
            ada_ref[a] = rbuf[k] + b_ref[a]
        ada = jnp.concatenate([ada_ref[a][0:1, :] for a in range(N_CHIPS)], axis=1)
        shift_ref[...] = ada[:, 0:D]
        scale_ref[...] = ada[:, D:2 * D]
        gate_ref[...] = ada[:, 2 * D:3 * D]

        forward()
        finish()
        n_real = 3 * D_ATT + N_HEADS
        for a in range(N_CHIPS):
            lo, hi = a * SHARD_IN, (a + 1) * SHARD_IN
            for s0, s1 in ((lo, min(hi, D_ATT)), (max(lo, D_ATT), min(hi, n_real)), (max(lo, n_real), hi)):
                if s0 < s1:
                    rows = win_all[a, s0 - lo:s1 - lo, :]
                    if s1 <= D_ATT:
                        rows = rows * jnp.asarray(Q_SCALE, BF16)
                    shift = O_P - n_real if s0 >= n_real else 0
                    wt_pad_ref[s0 + shift:s1 + shift, :] = rows
        wt_pad_ref[n_real:O_P, :] = jnp.zeros((O_P - n_real, D), BF16)

    vm = pl.BlockSpec(memory_space=pltpu.VMEM)
    return pl.pallas_call(
        body, name="gather_and_ada",
        out_shape=(jax.ShapeDtypeStruct((8, 8, D), F32),) + (jax.ShapeDtypeStruct((1, D), F32),) * 3
        + (jax.ShapeDtypeStruct((D_PAD, D), BF16),),
        in_specs=[vm] * 4, out_specs=(vm,) * 5,
        scratch_shapes=[pltpu.VMEM((N_CHIPS, SHARD_IN, D), BF16), pltpu.VMEM((SHARD_IN, D), BF16),
                        pltpu.VMEM((N_CHIPS, 8, SHARD_ADA), F32), pltpu.VMEM((8, D), F32), pltpu.VMEM((3, 8, SHARD_ADA), F32),
                        pltpu.VMEM((3, 8, SHARD_ADA), F32),
                        pltpu.SemaphoreType.DMA((7,)), pltpu.SemaphoreType.DMA((7,)),
                        pltpu.SemaphoreType.DMA((3,)), pltpu.SemaphoreType.DMA((3,))] + _gather_scratch(),
        compiler_params=_params(),
    )(c, w_ada, b_ada4, w_in_sh)


def _shard_cols():
    cut, gap = O_F + N_HEADS, O_P - (O_F + N_HEADS)
    out = []
    for a in range(N_CHIPS):
        lo, hi = a * SHARD_IN, (a + 1) * SHARD_IN
        out.append(([(lo, min(hi, cut))] if lo < cut else []) + ([(max(lo, cut) + gap, hi + gap)] if hi > cut else []))
    return out


def _scatter_stages(pos, g_ref, sc_ref, out_ref, sib_buf, send_buf, ici_buf, sem1, sem2s, sem2r, sem3, part=(0, 1),
                    cols=None, own_buf=None, staged=None):
    x, y, cc, chip, sib = pos
    q, n_parts = part
    RH = (g_ref.shape[1] if cols is None else g_ref.shape[0]) // 2 // n_parts
    mine = pl.ds(pl.multiple_of((cc * n_parts + q) * RH, RH), RH)
    theirs = pl.ds(pl.multiple_of(((1 - cc) * n_parts + q) * RH, RH), RH)
    cp1 = pltpu.make_async_remote_copy(
        src_ref=g_ref.at[:, theirs, :] if cols is None else g_ref.at[theirs, :], dst_ref=sib_buf,
        send_sem=sem1.at[0], recv_sem=sem1.at[1], device_id=sib, device_id_type=MESH)
    sends = []
    for k, (dx, dy) in enumerate(REL3):
        px, py = _flip(x, dx), _flip(y, dy)
        sends.append(pltpu.make_async_remote_copy(
            src_ref=send_buf.at[2 * px + py], dst_ref=ici_buf.at[chip],
            send_sem=sem2s.at[k], recv_sem=sem2r.at[k], device_id=(px, py, cc), device_id_type=MESH))
    cp3 = pltpu.make_async_remote_copy(
        src_ref=out_ref.at[mine, :], dst_ref=out_ref.at[mine, :], send_sem=sem3.at[0], recv_sem=sem3.at[1],
        device_id=sib, device_id_type=MESH)

    if staged is not None:
        stage = pltpu.make_async_copy(g_ref.at[mine, :], staged[0], staged[1])

    def start1():
        cp1.start()
        if staged is not None:
            stage.start()

    def finish1():
        cp1.wait()
        if cols is None:
            for a in range(N_CHIPS):
                both = g_ref[a, mine, :] + sib_buf[a]
                sib_buf[a] = both
                send_buf[a] = both.astype(BF16)
        else:
            if staged is not None:
                stage.wait()
            both = (g_ref[mine, :] if staged is None else staged[0][...]) + sib_buf[...]
            for a, pieces in enumerate(cols):
                at = 0
                for lo, hi in pieces:
                    own_buf[a, :, at:at + hi - lo] = both[:, lo:hi]
                    send_buf[a, :, at:at + hi - lo] = both[:, lo:hi].astype(BF16)
                    at += hi - lo

    def start2():
        for cp in sends:
            cp.start()
        ici_buf[chip] = send_buf[chip]

    def finish2():
        for cp in sends:
            cp.wait()
        own = (sib_buf if cols is None else own_buf)[chip]
        parts = [jnp.where(chip == a, own, ici_buf[a].astype(F32)) for a in range(N_CHIPS)]
        total = (parts[0] + parts[1]) + (parts[2] + parts[3])
        out_ref[mine, 0:own.shape[1]] = total if sc_ref is None else total * sc_ref[chip]

    return [(start1, finish1), (start2, finish2), (cp3.start, cp3.wait)]


def _all_reduce_stages(pos, g_ref, out_ref, sib_buf, ici_buf, sem1, sem2s, sem2r, sem3):
    x, y, cc, chip, sib = pos
    RH = g_ref.shape[0] // 2
    mine = pl.ds(pl.multiple_of(cc * RH, 8), RH)
    theirs = pl.ds(pl.multiple_of((1 - cc) * RH, 8), RH)
    cp1 = pltpu.make_async_remote_copy(
        src_ref=g_ref.at[theirs, :], dst_ref=sib_buf, send_sem=sem1.at[0], recv_sem=sem1.at[1],
        device_id=sib, device_id_type=MESH)
    sends = []
    for k, (dx, dy) in enumerate(REL3):
        px, py = _flip(x, dx), _flip(y, dy)
        sends.append(pltpu.make_async_remote_copy(
            src_ref=sib_buf, dst_ref=ici_buf.at[chip],
            send_sem=sem2s.at[k], recv_sem=sem2r.at[k], device_id=(px, py, cc), device_id_type=MESH))
    cp3 = pltpu.make_async_remote_copy(
        src_ref=out_ref.at[mine, :], dst_ref=out_ref.at[mine, :], send_sem=sem3.at[0], recv_sem=sem3.at[1],
        device_id=sib, device_id_type=MESH)

    def finish1():
        cp1.wait()
        sib_buf[...] = g_ref[mine, :] + sib_buf[...]

    def start2():
        for cp in sends:
            cp.start()
        ici_buf[chip] = sib_buf[...]

    def finish2():
        for cp in sends:
            cp.wait()
        out_ref[mine, :] = (ici_buf[0] + ici_buf[1]) + (ici_buf[2] + ici_buf[3])

    return [(cp1.start, finish1), (start2, finish2), (cp3.start, cp3.wait)]


def _stage_sems():
    return [pltpu.SemaphoreType.DMA((2,)), pltpu.SemaphoreType.DMA((3,)),
            pltpu.SemaphoreType.DMA((3,)), pltpu.SemaphoreType.DMA((2,))]


def _scatter_scratch(r, c):
    return [pltpu.VMEM((N_CHIPS, r // 2, c), F32), pltpu.VMEM((N_CHIPS, r // 2, c), BF16),
            pltpu.VMEM((N_CHIPS, r // 2, c), BF16)] + _stage_sems()


def _reduce_all(gw_pad, sc_in, small, dada, c_all):
    R = small.shape[0]
    W = dada.shape[1]
    r_in, p_in = gw_pad.shape
    c_in = SHARD_IN
    chunk = r_in // 4

    def chunk_scratch():
        return ([pltpu.VMEM((chunk, p_in), F32), pltpu.VMEM((N_CHIPS, chunk, c_in), BF16),
                 pltpu.VMEM((N_CHIPS, chunk, c_in), BF16)] + _stage_sems()
                + [pltpu.VMEM((N_CHIPS, chunk, c_in), F32), pltpu.VMEM((chunk, p_in), F32), pltpu.SemaphoreType.DMA])

    n_in = len(chunk_scratch())

    c_wide = -(-c_in // 128) * 128

    def body(gin_ref, scin_ref, sm_ref, d_ref, c_ref, ocols_ref, osm_ref, gwa_ref, gba_ref, loss_ref, oin_ref,
             dall_ref, *scratch):
        x, y, cc = lax.axis_index("x"), lax.axis_index("y"), lax.axis_index("c")
        me = 4 * x + 2 * y + cc
        chip = 2 * x + y
        pos = (x, y, cc, chip, (x, y, 1 - cc))
        oin_ref[:, c_in:c_wide] = jnp.zeros((r_in, c_wide - c_in), F32)
        dslab, ds_sem, dr_sem = scratch[0:3]
        a_bufs, b_bufs, sm_bufs = scratch[3:3 + n_in], scratch[3 + n_in:3 + 2 * n_in], scratch[3 + 2 * n_in:]
        dslab[...] = jnp.broadcast_to(d_ref[...], (8, W))
        dall_ref[me] = dslab[...]
        gathers = []
        for k, (dx, dy, dc) in enumerate(REL7):
            cp = pltpu.make_async_remote_copy(
                src_ref=dslab, dst_ref=dall_ref.at[me], send_sem=ds_sem.at[k], recv_sem=dr_sem.at[k],
                device_id=(_flip(x, dx), _flip(y, dy), _flip(cc, dc)), device_id_type=MESH)
            cp.start()
            gathers.append(cp)
        cols = _shard_cols()
        first = _scatter_stages(pos, gin_ref, scin_ref, oin_ref, *a_bufs[:-3], part=(0, 2), cols=cols,
                                own_buf=a_bufs[-3], staged=a_bufs[-2:])
        second = _scatter_stages(pos, gin_ref, scin_ref, oin_ref, *b_bufs[:-3], part=(1, 2), cols=cols,
                                 own_buf=b_bufs[-3], staged=b_bufs[-2:])
        little = _all_reduce_stages(pos, sm_ref, osm_ref, *sm_bufs)
        for plan in (first, second, little):
            plan[0][0]()
        first[0][1]()
        first[1][0]()
        little[0][1]()
        little[1][0]()
        second[0][1]()
        second[1][0]()
        for cp in gathers:
            cp.wait()
        slab_row = lax.broadcasted_iota(jnp.int32, (8, 1), 0)
        cm = jnp.zeros((8, D), F32)
        dm = jnp.zeros((8, W), F32)
        for r in range(8):
            cm = jnp.where(slab_row == r, c_ref[r], cm)
            dm = jnp.where(slab_row == r, dall_ref[r], dm)
        act = cm * _sigmoid(cm)
        dcol = dm[:, 0:SHARD_ADA]
        for a in range(1, N_CHIPS):
            dcol = jnp.where(chip == a, dm[:, a * SHARD_ADA:(a + 1) * SHARD_ADA], dcol)
        lhs = jnp.concatenate([act, jnp.zeros((8, D), F32)], axis=0).astype(BF16)
        rhs = jnp.concatenate([dcol, jnp.zeros((8, SHARD_ADA), F32)], axis=0).astype(BF16)
        gwa_ref[...] = _dot_tn(lhs, rhs)
        gba_ref[...] = _colsum(dm)
        first[1][1]()
        first[2][0]()
        second[1][1]()
        second[2][0]()
        little[1][1]()
        little[2][0]()
        for plan in (first, second, little):
            plan[2][1]()
        loss_row = SMALL_SEGS["loss"][0]
        loss_ref[...] = osm_ref[loss_row:loss_row + 1, 0:1]
        ocols_ref[...] = oin_ref[...].T[0:c_in, :][:, None, :]

    scratch = [pltpu.VMEM((r_in, c_wide), F32), pltpu.VMEM((8, 8, W), F32),
               pltpu.VMEM((8, W), F32), pltpu.SemaphoreType.DMA((7,)), pltpu.SemaphoreType.DMA((7,))]
    scratch += chunk_scratch() + chunk_scratch()
    scratch += [pltpu.VMEM((R // 2, 128), F32), pltpu.VMEM((N_CHIPS, R // 2, 128), F32)] + _stage_sems()
    vm = pl.BlockSpec(memory_space=pltpu.VMEM)
    return pl.pallas_call(
        body, name="reduce_all",
        out_shape=(jax.ShapeDtypeStruct((c_in, 1, r_in), F32), jax.ShapeDtypeStruct((R, 128), F32),
                   jax.ShapeDtypeStruct((D, SHARD_ADA), F32), jax.ShapeDtypeStruct((1, W), F32),
                   jax.ShapeDtypeStruct((1, 1), F32)),
        in_specs=[pl.BlockSpec(memory_space=pl.ANY)] + [vm] * 4, out_specs=(vm,) * 5,
        scratch_shapes=scratch,
        compiler_params=_params(),
    )(gw_pad, sc_in, small, dada, c_all)


def _in_proj(x, shift, scale, wt_pad, b_pad, w_out_sh):
    S = x.shape[0]
    tm = min(TM_PROJ, S)
    n_steps = S // tm
    assert n_steps >= 3

    def body(x_ref, sh_ref, sc_ref, w_ref, b_ref, wo_ref, u_ref, qkv_ref, f_ref, p_ref, g_ref, wo_all,
             wo_buf, wo_bf, *gather_sems):
        i = pl.program_id(0)
        xx, yy, cc = lax.axis_index("x"), lax.axis_index("y"), lax.axis_index("c")
        row_half = lambda which: (pl.ds(pl.multiple_of(which * (SHARD_OUT // 2), SHARD_OUT // 2), SHARD_OUT // 2),
                                  slice(None))
        start, forward, finish = _gather_stages((xx, yy, cc, 2 * xx + yy, (xx, yy, 1 - cc)), wo_bf, wo_buf,
                                                row_half, *gather_sems)

        @pl.when(i == 0)
        def _():
            wo_bf[...] = wo_ref[...].astype(BF16)
            start()

        pl.when(i == n_steps // 2)(forward)

        @pl.when(i == n_steps - 1)
        def _():
            finish()
            wo_all[...] = wo_buf[...]

        u = (x_ref[...] * (1.0 + sc_ref[...]) + sh_ref[...]).astype(BF16)
        u_ref[...] = u
        qkv_ref[...] = (_dot_nt(u, w_ref[O_QKV:O_F, :]) + b_ref[:, O_QKV:O_F]).astype(BF16)
        f_ref[...] = _dot_nt(u, w_ref[O_F:O_P, :]) + b_ref[:, O_F:O_P]
        p_ref[...] = _dot_nt(u, w_ref[O_P:O_G, :]) + b_ref[:, O_P:O_G]
        g_ref[...] = _dot_nt(u, w_ref[O_G:D_PAD, :]) + b_ref[:, O_G:D_PAD]

    row = lambda w: pl.BlockSpec((tm, w), lambda i: (i, 0))
    full = lambda a: pl.BlockSpec(a.shape, lambda i: (0, 0))
    vm = pl.BlockSpec(memory_space=pltpu.VMEM)
    return pl.pallas_call(
        body, name="in_proj", grid=(n_steps,),
        out_shape=(jax.ShapeDtypeStruct((S, D), BF16), jax.ShapeDtypeStruct((S, 3 * D_ATT), BF16),
                   jax.ShapeDtypeStruct((S, 128), F32), jax.ShapeDtypeStruct((S, D_POOL), F32),
                   jax.ShapeDtypeStruct((S, D), F32), jax.ShapeDtypeStruct((N_CHIPS,) + w_out_sh.shape, BF16)),
        in_specs=[row(D), full(shift), full(scale), full(wt_pad), full(b_pad), vm],
        out_specs=(row(D), row(3 * D_ATT), row(128), row(D_POOL), row(D), vm),
        scratch_shapes=[pltpu.VMEM((N_CHIPS,) + w_out_sh.shape, BF16), pltpu.VMEM(w_out_sh.shape, BF16)]
        + _gather_scratch(),
        compiler_params=_params(dimension_semantics=("arbitrary",)),
    )(x, shift, scale, wt_pad, b_pad, w_out_sh)


def _forget_cumsum(f):
    S = f.shape[0]
    tm = min(T_ATT, S)

    def body(f_ref, out_ref, carry):
        @pl.when(pl.program_id(0) == 0)
        def _():
            carry[...] = jnp.zeros_like(carry)
        v = f_ref[...]
        logf = jnp.minimum(v, 0.0) - jnp.log(1.0 + jnp.exp(-jnp.abs(v)))
        r = lax.broadcasted_iota(jnp.int32, (tm, tm), 0)
        c = lax.broadcasted_iota(jnp.int32, (tm, tm), 1)
        tri = (r <= c).astype(F32)
        rows8 = logf.T[0:8, :]
        cum8 = jnp.dot(rows8, tri, preferred_element_type=F32, precision=lax.Precision.HIGHEST) + carry[...]
        out_ref[...] = jnp.concatenate([cum8, jnp.zeros((128 - 8, tm), F32)], axis=0).T
        last = lax.broadcasted_iota(jnp.int32, (1, tm), 1) == tm - 1
        carry[...] = jnp.sum(jnp.where(last, cum8, 0.0), axis=1, keepdims=True)

    return pl.pallas_call(
        body, name="forget_cumsum", grid=(S // tm,),
        out_shape=jax.ShapeDtypeStruct((S, 128), F32),
        in_specs=[pl.BlockSpec((tm, 128), lambda i: (i, 0))],
        out_specs=pl.BlockSpec((tm, 128), lambda i: (i, 0)),
        scratch_shapes=[pltpu.VMEM((8, 1), F32)],
        compiler_params=_params(dimension_semantics=("arbitrary",)),
    )(f)


def _split3(v):
    hi = v.astype(BF16)
    rest = v - hi.astype(F32)
    mid = rest.astype(BF16)
    lo = (rest - mid.astype(F32)).astype(BF16)
    return hi, mid, lo


def _attention_fwd(qkv, big_f):
    S = qkv.shape[0]
    T = min(T_ATT, S)
    n_t = S // T

    def body(q_ref, k_ref, v_ref, f_ref, o_ref, lse_ref, kaug_sc, vt_sc, m_sc, l_sc, acc_sc):
        hp = pl.program_id(0)
        i = pl.program_id(1)
        lane = lax.broadcasted_iota(jnp.int32, (1, 128), 1)
        sub = lax.broadcasted_iota(jnp.int32, (128, 1), 0)
        head_sel = (lane < HEAD_DIM, lane >= HEAD_DIM)
        head_sel_t = (sub < HEAD_DIM, sub >= HEAD_DIM)
        spare = (HEAD_DIM, 0)
        zero = jnp.zeros((), BF16)

        @pl.when(i == 0)
        def _():
            def prep(jt, carry):
                rows = pl.ds(pl.multiple_of(jt * T, T), T)
                k = k_ref[rows, :]
                ft = f_ref[rows, :]
                vt = v_ref[rows, :].astype(F32).T
                for h in range(2):
                    fh = jnp.sum(jnp.where(lane == 2 * hp + h, ft, 0.0), axis=1, keepdims=True)
                    hi, mid, lo = _split3(-fh)
                    b = spare[h]
                    bias = jnp.where(lane == b, hi, jnp.where(lane == b + 1, mid, jnp.where(lane == b + 2, lo, zero)))
                    kaug_sc[h, rows, :] = jnp.where(head_sel[h], k, bias)
                    vt_sc[h, jt] = jnp.where(head_sel_t[h], vt, 0.0).astype(BF16)
                return carry

            lax.fori_loop(0, n_t, prep, 0)

        q = q_ref[...]
        q_heads = []
        for h in range(2):
            ones = jnp.where((lane >= spare[h]) & (lane < spare[h] + 3), jnp.ones((), BF16), zero)
            q_heads.append(jnp.where(head_sel[h], q, ones))
        m_sc[...] = jnp.full((8, T), NEG, F32)
        l_sc[...] = jnp.zeros((8, T), F32)
        acc_sc[...] = jnp.zeros((128, T), F32)

        def update(j, k_lo, n_k, q_lo, masked):
            rows = pl.ds(pl.multiple_of(j * T + k_lo, n_k), n_k)
            n_q = T - q_lo
            alphas, pvs = [], []
            for h in range(2):
                s_t = _dot_nt(kaug_sc[h, rows, :], q_heads[h][q_lo:, :])
                if masked:
                    rr = lax.broadcasted_iota(jnp.int32, (n_k, n_q), 0) + k_lo
                    cc = lax.broadcasted_iota(jnp.int32, (n_k, n_q), 1) + q_lo
                    s_t = jnp.where(rr <= cc, s_t, NEG)
                m_prev = m_sc[h:h + 1, q_lo:]
                m_new = jnp.maximum(m_prev, jnp.max(s_t, axis=0, keepdims=True))
                alpha = jnp.exp(m_prev - m_new)
                p_t = jnp.exp(s_t - m_new)
                l_sc[h:h + 1, q_lo:] = alpha * l_sc[h:h + 1, q_lo:] + jnp.sum(p_t, axis=0, keepdims=True)
                m_sc[h:h + 1, q_lo:] = m_new
                alphas.append(alpha)
                pvs.append(_dot(vt_sc[h, j, :, k_lo:k_lo + n_k], p_t.astype(BF16)))
            acc_sc[:, q_lo:] = (acc_sc[:, q_lo:] * jnp.where(head_sel_t[0], alphas[0], alphas[1])
                                + (pvs[0] + pvs[1]))

        def two_off_diagonal(jj, carry):
            update(2 * jj, 0, T, 0, False)
            update(2 * jj + 1, 0, T, 0, False)
            return carry

        lax.fori_loop(0, i // 2, two_off_diagonal, 0)

        @pl.when(i % 2 == 1)
        def _():
            update(i - 1, 0, T, 0, False)
            update(i, 0, T, 0, True)

        @pl.when(i % 2 == 0)
        def _():
            update(i, 0, T, 0, True)

        l = l_sc[...]
        o_ref[...] = (acc_sc[...] / jnp.where(head_sel_t[0], l[0:1, :], l[1:2, :])).T
        is_head = lax.broadcasted_iota(jnp.int32, (8, 1), 0) < 2
        lse_ref[...] = jnp.where(is_head, m_sc[...] + jnp.log(jnp.where(is_head, l, 1.0)), 0.0)

    return pl.pallas_call(
        body, name="attention_fwd", grid=(N_PAIR, n_t),
        out_shape=(jax.ShapeDtypeStruct((S, D_ATT), F32), jax.ShapeDtypeStruct((N_PAIR, n_t, 8, T), F32)),
        in_specs=[pl.BlockSpec((T, 128), lambda hp, i: (i, hp)),
                  pl.BlockSpec((S, 128), lambda hp, i: (0, N_PAIR + hp)),
                  pl.BlockSpec((S, 128), lambda hp, i: (0, 2 * N_PAIR + hp)),
                  pl.BlockSpec((S, 128), lambda hp, i: (0, 0))],
        out_specs=(pl.BlockSpec((T, 128), lambda hp, i: (i, hp)),
                   pl.BlockSpec((None, None, 8, T), lambda hp, i: (hp, i, 0, 0))),
        scratch_shapes=[pltpu.VMEM((2, S, 128), BF16), pltpu.VMEM((2, n_t, 128, T), BF16),
                        pltpu.VMEM((8, T), F32), pltpu.VMEM((8, T), F32), pltpu.VMEM((128, T), F32)],
        compiler_params=_params(dimension_semantics=("arbitrary", "arbitrary")),
    )(qkv, qkv, qkv, big_f)


def _attention_bwd(qkv, datt, att, lse, big_f, gw_out4):
    S = qkv.shape[0]
    T = min(T_ATT, S)
    n_t = S // T
    n_steps = N_PAIR * n_t
    marks = (0, n_steps // 8, n_steps // 2, n_steps // 2 + n_steps // 8)

    def body(q_ref, do_ref, o_ref, lse_ref, k_ref, v_ref, fk_ref, gout_ref,
             dq_ref, dk_ref, dv_ref, cs_ref, dfk_ref, dfq_ref, oout_ref, stat_sc, dqt_sc, qaug_sc,
             out_buf, *red_bufs):
        hp = pl.program_id(0)
        j = pl.program_id(1)
        x, y, cc = lax.axis_index("x"), lax.axis_index("y"), lax.axis_index("c")
        plan = _scatter_stages((x, y, cc, 2 * x + y, (x, y, 1 - cc)), gout_ref, None, out_buf, *red_bufs)
        step = hp * n_t + j
        for n, mark in enumerate(marks):
            @pl.when(step == mark)
            def _(n=n):
                if n > 0:
                    plan[n - 1][1]()
                if n < 3:
                    plan[n][0]()
                else:
                    oout_ref[...] = out_buf[...]

        lane = lax.broadcasted_iota(jnp.int32, (1, 128), 1)
        sub = lax.broadcasted_iota(jnp.int32, (128, 1), 0)
        head_sel = (lane < HEAD_DIM, lane >= HEAD_DIM)
        head_sel_t = (sub < HEAD_DIM, sub >= HEAD_DIM)
        spare = (HEAD_DIM, 0)
        zero = jnp.zeros((), BF16)
        one = jnp.ones((), BF16)

        def bias_lanes(first, pieces):
            hi, mid, lo = pieces
            return lambda rest: jnp.where(lane == first, hi, jnp.where(lane == first + 1, mid,
                                                                        jnp.where(lane == first + 2, lo, rest)))

        @pl.when(j == 0)
        def _():
            dqt_sc[...] = jnp.zeros_like(dqt_sc)
            cs_ref[...] = jnp.zeros_like(cs_ref)
            dfq_ref[...] = jnp.zeros_like(dfq_ref)

            def prep(i, carry):
                rows = pl.ds(pl.multiple_of(i * T, T), T)
                q = q_ref[rows, :]
                do = do_ref[rows, :]
                prod = o_ref[rows, :] * do.astype(F32)
                d_a = jnp.sum(jnp.where(head_sel[0], prod, 0.0), axis=1, keepdims=True)
                d_b = jnp.sum(jnp.where(head_sel[0], 0.0, prod), axis=1, keepdims=True)
                delta_t = jnp.where(head_sel[0], d_a, d_b).T
                stat_sc[i, 0:1, :] = delta_t[0:1, :]
                stat_sc[i, 1:2, :] = delta_t[HEAD_DIM:HEAD_DIM + 1, :]
                lse = lse_ref[i]
                lse_cols = jnp.where(head_sel_t[0], lse[0:1, :], lse[1:2, :]).T
                for h in range(2):
                    neg_lse = -lse_cols[:, h * HEAD_DIM:h * HEAD_DIM + 1]
                    ones = jnp.where((lane >= spare[h]) & (lane < spare[h] + 3), one, zero)
                    qaug_sc[h, rows, :] = jnp.where(head_sel[h], q, bias_lanes(spare[h] + 3, _split3(neg_lse))(ones))
                return carry

            lax.fori_loop(0, n_t, prep, 0)

        k = k_ref[...]
        v = v_ref[...]
        fk = fk_ref[...]
        kt = k.astype(F32).T
        heads = []
        for h in range(2):
            fkh = jnp.sum(jnp.where(lane == 2 * hp + h, fk, 0.0), axis=1, keepdims=True)
            ones = jnp.where((lane >= spare[h] + 3) & (lane < spare[h] + 6), one, zero)
            kaug = jnp.where(head_sel[h], k, bias_lanes(spare[h], _split3(-fkh))(ones))
            heads.append((kaug, jnp.where(head_sel[h], v, zero), jnp.where(head_sel_t[h], kt, 0.0).astype(BF16)))

        def block(i, k_lo, n_k, q_lo, masked):
            n_q = T - q_lo
            rows = pl.ds(pl.multiple_of(i * T + q_lo, n_q), n_q)
            q = q_ref[rows, :]
            do = do_ref[rows, :]
            stat = stat_sc[i]
            dk = jnp.zeros((n_k, 128), F32)
            dv = jnp.zeros((n_k, 128), F32)
            dqt = jnp.zeros((128, n_q), F32)
            dfs = []
            for h in range(2):
                kaug, vh, kth = heads[h]
                arg = _dot_nt(kaug[k_lo:k_lo + n_k, :], qaug_sc[h, rows, :])
                if masked:
                    rr = lax.broadcasted_iota(jnp.int32, (n_k, n_q), 0) + k_lo
                    cc = lax.broadcasted_iota(jnp.int32, (n_k, n_q), 1) + q_lo
                    arg = jnp.where(rr <= cc, arg, NEG)
                p_t = jnp.exp(arg)
                ds_t = p_t * (_dot_nt(vh[k_lo:k_lo + n_k, :], do) - stat[h:h + 1, q_lo:])
                ds_bf = ds_t.astype(BF16)
                dv = dv + _dot(p_t.astype(BF16), jnp.where(head_sel[h], do, zero))
                dk = dk + _dot(ds_bf, jnp.where(head_sel[h], q, zero))
                dqt = dqt + _dot(kth[:, k_lo:k_lo + n_k], ds_bf)
                dfs.append(jnp.sum(ds_t, axis=1, keepdims=True))
                dfq_ref[i, h:h + 1, q_lo:] += _colsum(ds_t)
            dqt_sc[i, :, q_lo:] += dqt
            return dk, dv, dfs[0], dfs[1]

        def off_diagonal(i, acc):
            return tuple(a + b for a, b in zip(acc, block(i, 0, T, 0, False)))

        half = T // 2
        early = block(j, 0, half, 0, True)
        late = block(j, half, half, half, True)
        acc1 = tuple(jnp.concatenate([a, b], axis=0) for a, b in zip(early, late))
        n_off = n_t - 1 - j
        acc2 = lax.fori_loop(0, n_off // 2,
                             lambda ii, a: off_diagonal(j + 2 + 2 * ii, off_diagonal(j + 1 + 2 * ii, a)), acc1)
        dk_acc, dv_acc, dfa, dfb = lax.fori_loop(0, n_off % 2, lambda _, a: off_diagonal(n_t - 1, a), acc2)
        dk_ref[...] = dk_acc.astype(BF16)
        dv_ref[...] = dv_acc.astype(BF16)
        dfk_ref[...] = -jnp.where(lane == 0, dfa, jnp.where(lane == 1, dfb, 0.0))
        cs_ref[:, 128:256] = cs_ref[:, 128:256] + _colsum(dk_acc)
        cs_ref[:, 256:384] = cs_ref[:, 256:384] + _colsum(dv_acc)

        @pl.when(j == n_t - 1)
        def _():
            def finish(i, tot):
                dq = dqt_sc[i].T
                dq_ref[pl.ds(pl.multiple_of(i * T, T), T), :] = dq.astype(BF16)
                return tot + _colsum(dq)

            cs_ref[:, 0:128] = lax.fori_loop(0, n_t, finish, jnp.zeros((1, 128), F32))

    pair_rows = lambda hp, j: (hp, 0, 0)
    vm = pl.BlockSpec(memory_space=pltpu.VMEM)
    _, r_out, c_out = gw_out4.shape
    return pl.pallas_call(
        body, name="attention_bwd", grid=(N_PAIR, n_t),
        out_shape=(jax.ShapeDtypeStruct((S, D_ATT), BF16), jax.ShapeDtypeStruct((S, D_ATT), BF16),
                   jax.ShapeDtypeStruct((S, D_ATT), BF16), jax.ShapeDtypeStruct((N_PAIR, 1, 384), F32),
                   jax.ShapeDtypeStruct((N_PAIR, S, 128), F32),
                   jax.ShapeDtypeStruct((N_PAIR, n_t, 8, T), F32),
                   jax.ShapeDtypeStruct((r_out, c_out), F32)),
        in_specs=[pl.BlockSpec((S, 128), lambda hp, j: (0, hp)),
                  pl.BlockSpec((S, 128), lambda hp, j: (0, hp)),
                  pl.BlockSpec((S, 128), lambda hp, j: (0, hp)),
                  pl.BlockSpec((None, n_t, 8, T), lambda hp, j: (hp, 0, 0, 0)),
                  pl.BlockSpec((T, 128), lambda hp, j: (j, N_PAIR + hp)),
                  pl.BlockSpec((T, 128), lambda hp, j: (j, 2 * N_PAIR + hp)),
                  pl.BlockSpec((T, 128), lambda hp, j: (j, 0)),
                  vm],
        out_specs=(pl.BlockSpec((S, 128), lambda hp, j: (0, hp)),
                   pl.BlockSpec((T, 128), lambda hp, j: (j, hp)),
                   pl.BlockSpec((T, 128), lambda hp, j: (j, hp)),
                   pl.BlockSpec((None, 1, 384), pair_rows),
                   pl.BlockSpec((None, T, 128), lambda hp, j: (hp, j, 0)),
                   pl.BlockSpec((None, n_t, 8, T), lambda hp, j: (hp, 0, 0, 0)),
                   vm),
        scratch_shapes=[pltpu.VMEM((n_t, 8, T), F32), pltpu.VMEM((n_t, 128, T), F32),
                        pltpu.VMEM((2, S, 128), BF16), pltpu.VMEM((r_out, c_out), F32)]
        + _scatter_scratch(r_out, c_out),
        compiler_params=_params(dimension_semantics=("arbitrary", "arbitrary")),
    )(qkv, datt, att, lse, qkv, qkv, big_f, gw_out4)


def _window_counts(first_row, n_rows, window):
    t = lax.broadcasted_iota(jnp.int32, (n_rows, 1), 0) + first_row
    return jnp.minimum((t + 1).astype(F32), float(window))


def _middle(x, tgt, att, g, p, gate, w_mix, b_mix, pool_scale, w_out, b_out, ln_g, ln_b):
    S = x.shape[0]
    tm = min(TM_MID, S)
    halo_blocks = tm // POOL_HALO

    def body(x_ref, t_ref, att_ref, g_ref, p_ref, ph_ref, gate_ref, wm_ref, bm_ref, ps_ref, wo_ref, bo_ref,
             lg_ref, lb_ref,
             dh_ref, datt_ref, dg_ref, dpl_ref, gwo_ref, gwm_ref, vec_ref, loss_ref):
        i = pl.program_id(0)

        @pl.when(i == 0)
        def _():
            gwo_ref[...] = jnp.zeros_like(gwo_ref)
            gwm_ref[...] = jnp.zeros_like(gwm_ref)
            vec_ref[...] = jnp.zeros_like(vec_ref)
            loss_ref[...] = jnp.zeros_like(loss_ref)

        pc = p_ref[...]
        halo = jnp.where(i > 0, ph_ref[...], 0.0)
        pe = jnp.concatenate([halo, pc], axis=0)
        pooled_parts = []
        for gi, w in enumerate(POOL_WINDOWS):
            cur = pe[:, gi * POOL_GROUP:(gi + 1) * POOL_GROUP]
            span = 1
            while span < w:
                cur = cur + pltpu.roll(cur, span, 0)
                span *= 2
            wsum = cur[POOL_HALO:, :]
            mean = wsum / _window_counts(i * tm, tm, w)
            pooled_parts.append(mean - pc[:, gi * POOL_GROUP:(gi + 1) * POOL_GROUP])
        pooled_bf =[v.astype(BF16) for v in pooled_parts]
        wm = [wm_ref[gi].astype(BF16) for gi in range(4)]
        mixed = jnp.concatenate([_dot(pooled_bf[gi], wm[gi]) for gi in range(4)], axis=1) + bm_ref[...]
        ps = ps_ref[...]
        pool_out = mixed * ps
        gv = g_ref[...]
        sig = _sigmoid(gv)
        silu = gv * sig
        att = att_ref[...]
        y = jnp.concatenate([att * silu[:, :D_ATT], pool_out * silu[:, D_ATT:]], axis=1)
        y_bf = y.astype(BF16)
        wo = wo_ref[...]
        yo = _dot(y_bf, wo) + bo_ref[...]
        gate = gate_ref[...]
        h = ALPHA * x_ref[...] + gate * yo
        mu = jnp.mean(h, axis=1, keepdims=True)
        hc = h - mu
        var = jnp.mean(hc * hc, axis=1, keepdims=True)
        rstd = lax.rsqrt(var + LN_EPS)
        yhat = hc * rstd
        lg = lg_ref[...]
        out = yhat * lg + lb_ref[...]
        err = out - t_ref[...]
        loss_ref[...] += 0.5 * jnp.sum(jnp.mean(err * err, axis=1, keepdims=True), axis=0, keepdims=True)

        dout = err * (1.0 / D)
        g_ln_b = _colsum(dout)
        g_ln_g = _colsum(dout * yhat)
        dyh = dout * lg
        dh = rstd * (dyh - jnp.mean(dyh, axis=1, keepdims=True)
                     - yhat * jnp.mean(dyh * yhat, axis=1, keepdims=True))
        dh_ref[...] = dh
        d_gate = _colsum(dh * yo)
        dyo = gate * dh
        g_b_out = _colsum(dyo)
        dyo_bf = dyo.astype(BF16)
        gwo_ref[...] += _dot_tn(y_bf, dyo_bf)
        dy = _dot_nt(dyo_bf, wo)
        dsilu = sig * (1.0 + gv * (1.0 - sig))
        dy_a = dy[:, :D_ATT]
        dy_p = dy[:, D_ATT:]
        datt_ref[...] = (dy_a * silu[:, :D_ATT]).astype(BF16)
        dpo = dy_p * silu[:, D_ATT:]
        dg = jnp.concatenate([dy_a * att * dsilu[:, :D_ATT], dy_p * pool_out * dsilu[:, D_ATT:]], axis=1)
        dg_ref[...] = dg.astype(BF16)
        g_dg = _colsum(dg)
        g_ps = _colsum(dpo * mixed)
        dmixed = dpo * ps
        g_bm = _colsum(dmixed)
        dmixed_bf = dmixed.astype(BF16)
        dpl = []
        for gi in range(4):
            dm = dmixed_bf[:, gi * POOL_GROUP:(gi + 1) * POOL_GROUP]
            gwm_ref[gi] += _dot_tn(pooled_bf[gi], dm)
            dpl.append(_dot_nt(dm, wm[gi]))
        dpl_ref[...] = jnp.concatenate(dpl, axis=1)
        vec_ref[0:1, :] += g_ln_g
        vec_ref[1:2, :] += g_ln_b
        vec_ref[2:3, :] += d_gate
        vec_ref[3:4, :] += g_b_out
        vec_ref[4:5, :] += g_dg
        vec_ref[5:6, 0:D_POOL] += g_ps
        vec_ref[6:7, 0:D_POOL] += g_bm

    row = lambda w: pl.BlockSpec((tm, w), lambda i: (i, 0))
    full2 = lambda a: pl.BlockSpec(a.shape, lambda i: (0, 0))
    full3 = lambda a: pl.BlockSpec(a.shape, lambda i: (0, 0, 0))
    return pl.pallas_call(
        body, name="middle", grid=(S // tm,),
        out_shape=(jax.ShapeDtypeStruct((S, D), F32),
                   jax.ShapeDtypeStruct((S, D_ATT), BF16),
                   jax.ShapeDtypeStruct((S, D), BF16),
                   jax.ShapeDtypeStruct((S, D_POOL), F32),
                   jax.ShapeDtypeStruct((D, D), F32),
                   jax.ShapeDtypeStruct((4, POOL_GROUP, POOL_GROUP), F32),
                   jax.ShapeDtypeStruct((8, D), F32),
                   jax.ShapeDtypeStruct((1, 1), F32)),
        in_specs=[row(D), row(D), row(D_ATT), row(D), row(D_POOL),
                  pl.BlockSpec((POOL_HALO, D_POOL), lambda i: (jnp.maximum(i * halo_blocks - 1, 0), 0)),
                  full2(gate), full3(w_mix), full2(b_mix), full2(pool_scale), full2(w_out), full2(b_out),
                  full2(ln_g), full2(ln_b)],
        out_specs=(row(D), row(D_ATT), row(D), row(D_POOL),
                   pl.BlockSpec((D, D), lambda i: (0, 0)),
                   pl.BlockSpec((4, POOL_GROUP, POOL_GROUP), lambda i: (0, 0, 0)),
                   pl.BlockSpec((8, D), lambda i: (0, 0)),
                   pl.BlockSpec((1, 1), lambda i: (0, 0))),
        compiler_params=_params(dimension_semantics=("arbitrary",)),
    )(x, tgt, att, g, p, p, gate, w_mix, b_mix, pool_scale, w_out, b_out, ln_g, ln_b)


def _tail(dpl, dfk, dfq, f):
    S = dpl.shape[0]
    tm = min(T_ATT, S)
    n_t = S // tm
    halo_blocks = tm // POOL_HALO
    last_halo = S // POOL_HALO - 1

    def body(d_ref, dn_ref, dfk_ref, dfq_ref, f_ref, dp_ref, df_ref, cs_ref, carry):
        s = pl.program_id(0)
        i = n_t - 1 - s

        @pl.when(s == 0)
        def _():
            carry[...] = jnp.zeros_like(carry)
            cs_ref[...] = jnp.zeros_like(cs_ref)

        dc = d_ref[...]
        nxt = jnp.where(s > 0, dn_ref[...], 0.0)
        de = jnp.concatenate([dc, nxt], axis=0)
        n_e = tm + POOL_HALO
        parts = []
        for gi, w in enumerate(POOL_WINDOWS):
            cur = de[:, gi * POOL_GROUP:(gi + 1) * POOL_GROUP] / _window_counts(i * tm, n_e, w)
            span = 1
            while span < w:
                cur = cur + pltpu.roll(cur, n_e - span, 0)
                span *= 2
            parts.append(cur[:tm, :] - dc[:, gi * POOL_GROUP:(gi + 1) * POOL_GROUP])
        dp = jnp.concatenate(parts, axis=1)
        dp_ref[...] = dp.astype(BF16)
        cs_ref[0:1, :] += _colsum(dp)

        r = lax.broadcasted_iota(jnp.int32, (tm, tm), 0)
        c = lax.broadcasted_iota(jnp.int32, (tm, tm), 1)
        tri = (r >= c).astype(F32)
        k_cols = dfk_ref[0]
        rows8 = dfq_ref[0]
        for hp in range(1, N_PAIR):
            k_cols = k_cols + pltpu.roll(dfk_ref[hp], 2 * hp, 1)
            rows8 = rows8 + pltpu.roll(dfq_ref[hp], 2 * hp, 0)
        rows8 = rows8 + k_cols.T[0:8, :]
        dlogf8 = jnp.dot(rows8, tri, preferred_element_type=F32, precision=lax.Precision.HIGHEST) + carry[...]
        first = lax.broadcasted_iota(jnp.int32, (1, tm), 1) == 0
        carry[...] = jnp.sum(jnp.where(first, dlogf8, 0.0), axis=1, keepdims=True)
        dlogf = jnp.concatenate([dlogf8, jnp.zeros((128 - 8, tm), F32)], axis=0).T
        df = dlogf * _sigmoid(-f_ref[...])
        df_ref[...] = df.astype(BF16)
        cs_ref[1:2, 0:128] += _colsum(df)

    rev = lambda w: pl.BlockSpec((tm, w), lambda s: (n_t - 1 - s, 0))
    return pl.pallas_call(
        body, name="tail", grid=(n_t,),
        out_shape=(jax.ShapeDtypeStruct((S, D_POOL), BF16), jax.ShapeDtypeStruct((S, 128), BF16),
                   jax.ShapeDtypeStruct((8, D_POOL), F32)),
        in_specs=[rev(D_POOL),
                  pl.BlockSpec((POOL_HALO, D_POOL),
                               lambda s: (jnp.minimum((n_t - s) * halo_blocks, last_halo), 0)),
                  pl.BlockSpec((N_PAIR, tm, 128), lambda s: (0, n_t - 1 - s, 0)),
                  pl.BlockSpec((N_PAIR, None, 8, tm), lambda s: (0, n_t - 1 - s, 0, 0)),
                  rev(128)],
        out_specs=(rev(D_POOL), rev(128), pl.BlockSpec((8, D_POOL), lambda s: (0, 0))),
        scratch_shapes=[pltpu.VMEM((8, 1), F32)],
        compiler_params=_params(dimension_semantics=("arbitrary",)),
    )(dpl, dpl, dfk, dfq, f)


PIECES = ((O_QKV, D_ATT), (O_QKV + D_ATT, D_ATT), (O_QKV + 2 * D_ATT, D_ATT), (O_F, 128), (O_P, D_POOL), (O_G, D))


def _grad_w_in(u, pieces):
    S = u.shape[0]
    tm = min(TM_GW, S)
    n_t = S // tm

    def body(u_ref, *rest):
        piece_refs, out_ref, acc, sem = rest[:6], rest[6], rest[7], rest[8]
        i = pl.program_id(0)

        @pl.when(i == 0)
        def _():
            acc[...] = jnp.zeros_like(acc)

        u_t = u_ref[...]
        for (off, w), ref in zip(PIECES, piece_refs):
            acc[:, off:off + w] += _dot_tn(u_t, ref[...])

        @pl.when(i == n_t - 1)
        def _():
            cp = pltpu.make_async_copy(acc, out_ref, sem)
            cp.start()
            cp.wait()

    return pl.pallas_call(
        body, name="grad_w_in", grid=(n_t,),
        out_shape=jax.ShapeDtypeStruct((D, D_PAD), F32),
        in_specs=[pl.BlockSpec((tm, D), lambda i: (i, 0))]
        + [pl.BlockSpec((tm, w), lambda i: (i, 0)) for _, w in PIECES],
        out_specs=pl.BlockSpec(memory_space=pl.ANY),
        scratch_shapes=[pltpu.VMEM((D, D_PAD), F32), pltpu.SemaphoreType.DMA],
        compiler_params=_params(dimension_semantics=("arbitrary",)),
    )(u, *pieces)


def _grad_x(pieces, wt_pad, dh, x, scale):
    S = x.shape[0]
    tm = min(TM_DU, S)

    def body(*refs):
        piece_refs = refs[:6]
        w_ref, dh_ref, x_ref, sc_ref, gx_ref, vec_ref = refs[6:]

        @pl.when(pl.program_id(0) == 0)
        def _():
            vec_ref[...] = jnp.zeros_like(vec_ref)

        du = jnp.zeros((tm, D), F32)
        for (off, w), ref in zip(PIECES, piece_refs):
            du = du + _dot(ref[...], w_ref[off:off + w, :])
        xv = x_ref[...]
        gx_ref[...] = ALPHA * dh_ref[...] + du * (1.0 + sc_ref[...])
        vec_ref[0:1, :] += _colsum(du)
        vec_ref[1:2, :] += _colsum(du * xv)

    row = lambda w: pl.BlockSpec((tm, w), lambda i: (i, 0))
    return pl.pallas_call(
        body, name="grad_x", grid=(S // tm,),
        out_shape=(jax.ShapeDtypeStruct((S, D), F32), jax.ShapeDtypeStruct((8, D), F32)),
        in_specs=[row(w) for _, w in PIECES]
        + [pl.BlockSpec(wt_pad.shape, lambda i: (0, 0)), row(D), row(D), pl.BlockSpec((1, D), lambda i: (0, 0))],
        out_specs=(row(D), pl.BlockSpec((8, D), lambda i: (0, 0))),
        compiler_params=_params(dimension_semantics=("arbitrary",)),
    )(*pieces, wt_pad, dh, x, scale)


def _adamw_math(w, g, m, v):
    m = ADAM_B1 * m + (1.0 - ADAM_B1) * g
    v = ADAM_B2 * v + (1.0 - ADAM_B2) * (g * g)
    m_hat = m / (1.0 - ADAM_B1 ** ADAM_STEP)
    v_hat = v / (1.0 - ADAM_B2 ** ADAM_STEP)
    delta = -ADAM_LR * (m_hat / (jnp.sqrt(v_hat) + ADAM_EPS) + ADAM_WD * w)
    return delta, m, v


def _adamw(groups, n_steps):
    n = len(groups)

    def body(*refs):
        ins, outs = refs[:4 * n], refs[4 * n:]
        for t in range(n):
            w, g, m, v = (r[...] for r in ins[4 * t:4 * t + 4])
            d, m2, v2 = _adamw_math(w, g, m, v)
            outs[4 * t][...] = d
            outs[4 * t + 1][...] = m2
            outs[4 * t + 2][...] = v2
            outs[4 * t + 3][...] = g

    in_specs, out_specs, out_shape, args = [], [], [], []
    for (w, g, m, v) in groups:
        rest = w.shape[1:]
        spec = pl.BlockSpec((w.shape[0] // n_steps,) + rest, lambda i, nd=len(rest): (i,) + (0,) * nd)
        in_specs += [spec] * 4
        out_specs += [spec] * 4
        out_shape += [jax.ShapeDtypeStruct(w.shape, F32)] * 4
        args += [w, g, m, v]
    return pl.pallas_call(
        body, name="adamw_%d_%d" % (n, n_steps), grid=(n_steps,),
        out_shape=tuple(out_shape), in_specs=in_specs, out_specs=tuple(out_specs),
        compiler_params=_params(dimension_semantics=("arbitrary",)),
    )(*args)


def _adamw_small(small_sum, g_b_ada, params):
    n = len(params)

    def body(gs_ref, gba_ref, *refs):
        ins, outs = refs[:3 * n], refs[3 * n:]
        for t, (name, w0, _, _) in enumerate(params):
            w_ref, m_ref, v_ref = ins[3 * t:3 * t + 3]
            first = SMALL_SEGS[name][0] if name in SMALL_SEGS else None
            if w0.shape[0] > 1:
                pieces = [((slice(None), slice(None)), gs_ref[first:first + w0.shape[0], :])]
            else:
                pieces = []
                for r in range(-(-w0.shape[1] // 128)):
                    lanes = slice(128 * r, min(128 * r + 128, w0.shape[1]))
                    g = gba_ref[0:1, lanes] if first is None else gs_ref[first + r:first + r + 1, 0:lanes.stop - lanes.start]
                    pieces.append(((slice(0, 1), lanes), g))
            for where, g in pieces:
                d, m2, v2 = _adamw_math(w_ref[where], g, m_ref[where], v_ref[where])
                for ref, val in zip(outs[4 * t:4 * t + 4], (g, d, m2, v2)):
                    ref[where] = val

    vm = pl.BlockSpec(memory_space=pltpu.VMEM)
    args = [small_sum, g_b_ada]
    out_shape = []
    for _, w, m, v in params:
        args += [w, m, v]
        out_shape += [jax.ShapeDtypeStruct(w.shape, F32)] * 4
    return pl.pallas_call(
        body, name="adamw_small",
        out_shape=tuple(out_shape), in_specs=[vm] * len(args), out_specs=(vm,) * len(out_shape),
        compiler_params=_params(),
    )(*args)


def _pack_small(parts):
    rows = []
    used = 0
    for name, (first, n_rows) in SMALL_SEGS.items():
        if first > used:
            rows.append(jnp.zeros((first - used, 128), F32))
        flat = parts[name].reshape(-1)
        flat = jnp.pad(flat, (0, n_rows * 128 - flat.shape[0]))
        rows.append(flat.reshape(n_rows, 128))
        used = first + n_rows
    rows.append(jnp.zeros((SMALL_ROWS - used, 128), F32))
    return jnp.concatenate(rows, axis=0)


def _pad_in(v):
    r = v.shape[0]
    z = jnp.zeros((r, O_P - O_F - N_HEADS), v.dtype)
    return jnp.concatenate([v[:, :3 * D_ATT + N_HEADS], z, v[:, 3 * D_ATT + N_HEADS:]], axis=1)


def _unpad_in(v):
    return jnp.concatenate([v[:, :O_F + N_HEADS], v[:, O_P:]], axis=1)


def _shards_in(v):
    gap = O_P - (O_F + N_HEADS)
    parts = []
    for a in range(N_CHIPS):
        lo, hi = a * SHARD_IN, (a + 1) * SHARD_IN
        cut = O_F + N_HEADS
        if hi <= cut:
            parts.append(v[:, lo:hi])
        elif lo >= cut:
            parts.append(v[:, lo + gap:hi + gap])
        else:
            parts.append(jnp.concatenate([v[:, lo:cut], v[:, cut + gap:hi + gap]], axis=1))
    return jnp.stack(parts, axis=0)


def kernel(x, c, w_ada, b_ada, w_in, b_in, w_pool_mix, b_pool_mix, pool_scale, w_out, b_out, ln_g, ln_b, loss_target, m_w_ada, m_b_ada, m_w_in, m_b_in, m_w_pool_mix, m_b_pool_mix, m_pool_scale, m_w_out, m_b_out, m_ln_g, m_ln_b, v_w_ada, v_b_ada, v_w_in, v_b_in, v_w_pool_mix, v_b_pool_mix, v_pool_scale, v_w_out, v_b_out, v_ln_g, v_ln_b):
    S = x.shape[1]
    T = min(T_ATT, S)
    n_t = S // T
    x2 = x[0]
    tgt = loss_target[0]
    q_scale = jnp.concatenate([jnp.full((1, D_ATT), Q_SCALE, F32), jnp.ones((1, D_PAD - D_ATT), F32)], axis=1)

    to_cols = lambda a: jnp.transpose(a, (2, 0, 1))
    from_cols = lambda a: jnp.transpose(a, (1, 2, 0))
    c_all, shift, scale, gate, wt_pad = _gather_and_ada(
        c, w_ada[0], b_ada.reshape(4, 1, SHARD_ADA), to_cols(w_in))
    b_pad = _pad_in(b_in) * q_scale

    u, qkv, f, p, g, w_out_all = _in_proj(x2, shift, scale, wt_pad, b_pad, w_out[0])
    w_out_full = w_out_all.reshape(D, D)
    big_f = _forget_cumsum(f)
    att, lse = _attention_fwd(qkv, big_f)

    dh, datt, dg, dpl, gw_out, gw_mix, vec, loss_part = _middle(
        x2, tgt, att, g, p, gate, w_pool_mix[0], b_pool_mix.reshape(1, D_POOL), pool_scale, w_out_full, b_out, ln_g, ln_b)
    dq, dk, dv, cs_att, dfk, dfq, g_w_out = _attention_bwd(
        qkv, datt, att, lse, big_f, gw_out.reshape(N_CHIPS, SHARD_OUT, D))
    dp, df, cs_tail = _tail(dpl, dfk, dfq, f)
    pieces = (dq, dk, dv, df, dp, dg)
    gw_pad = _grad_w_in(u, pieces)
    grad_x, vec_x = _grad_x(pieces, wt_pad, dh, x2, scale)

    cs_qkv = jnp.transpose(cs_att.reshape(N_PAIR, 3, 128), (1, 0, 2)).reshape(1, 3 * D_ATT)
    gb_pad = jnp.concatenate([cs_qkv, cs_tail[1:2, 0:128], cs_tail[0:1, :], vec[4:5, :]], axis=1) * q_scale
    dada = jnp.concatenate([vec_x[0:1, :], vec_x[1:2, :], vec[2:3, :]], axis=1)
    small = _pack_small({
        "b_in": _unpad_in(gb_pad), "w_pool_mix": gw_mix, "b_pool_mix": vec[6:7, :D_POOL],
        "pool_scale": vec[5:6, :D_POOL], "b_out": vec[3:4, :], "ln_g": vec[0:1, :], "ln_b": vec[1:2, :],
        "loss": loss_part})

    g_w_in, small_sum, g_w_ada, g_b_ada, loss = _reduce_all(
        gw_pad, _shards_in(q_scale), small, dada, c_all)

    big = _adamw([(w_ada[0], g_w_ada, m_w_ada[0], v_w_ada[0]),
                  (w_out[0], g_w_out, m_w_out[0], v_w_out[0])], 2)
    big_in = _adamw([(to_cols(w_in), g_w_in, to_cols(m_w_in), to_cols(v_w_in))], 2)
    tiles = lambda a: a.reshape(4 * POOL_GROUP, POOL_GROUP)
    flat = lambda a: a.reshape(1, D_POOL)
    small_params = [("b_ada", b_ada, m_b_ada, v_b_ada), ("b_in", b_in, m_b_in, v_b_in),
                    ("w_pool_mix", tiles(w_pool_mix), tiles(m_w_pool_mix), tiles(v_w_pool_mix)),
                    ("b_pool_mix", flat(b_pool_mix), flat(m_b_pool_mix), flat(v_b_pool_mix)),
                    ("pool_scale", pool_scale, m_pool_scale, v_pool_scale), ("b_out", b_out, m_b_out, v_b_out),
                    ("ln_g", ln_g, m_ln_g, v_ln_g), ("ln_b", ln_b, m_ln_b, v_ln_b)]
    sm = _adamw_small(small_sum, g_b_ada, small_params)
    sm_idx = {p[0]: n for n, p in enumerate(small_params)}
    shapes = {"w_pool_mix": (1, 4, POOL_GROUP, POOL_GROUP), "b_pool_mix": (1, 4, POOL_GROUP)}

    names = ["w_ada", "b_ada", "w_in", "b_in", "w_pool_mix", "b_pool_mix", "pool_scale", "w_out", "b_out",
             "ln_g", "ln_b"]
    big_idx = {"w_ada": 0, "w_out": 1}

    def leaf(kind, name):
        if name == "w_in":
            return from_cols(big_in[(kind - 1) % 4])
        if name in big_idx:
            return big[4 * big_idx[name] + (kind - 1) % 4][None]
        val = sm[4 * sm_idx[name] + kind]
        return val.reshape(shapes[name]) if name in shapes else val

    outs = [loss.reshape(()), grad_x[None]]
    for kind in range(4):
        outs += [leaf(kind, n) for n in names]
    return tuple(outs)
```

```python
import functools

import numpy as np
import jax
import jax.numpy as jnp
from jax import lax
from jax.experimental import pallas as pl
from jax.experimental.pallas import tpu as pltpu

F32 = jnp.float32
BF16 = jnp.bfloat16
MESH = pl.DeviceIdType.MESH

D = 1024
D_ATT = 512
D_POOL = 512
N_HEADS = 8
HEAD_DIM = 64
N_PAIR = N_HEADS // 2
POOL_WINDOWS = (2, 4, 8, 16)
POOL_GROUP = 128
POOL_HALO = 16
LN_EPS = 1e-5
ALPHA = 2.0 ** 0.25
D_IN = 3 * D_ATT + N_HEADS + D_POOL + D_ATT + D_POOL
N_CHIPS = 4
SHARD_IN = D_IN // N_CHIPS
SHARD_ADA = 3 * D // N_CHIPS
SHARD_OUT = D // N_CHIPS

O_QKV, O_F, O_P, O_G, D_PAD = 0, 1536, 1664, 2176, 3200
Q_SCALE = HEAD_DIM ** -0.5

ADAM_LR, ADAM_B1, ADAM_B2, ADAM_EPS, ADAM_WD, ADAM_STEP = 0.001, 0.9, 0.999, 1e-08, 0.01, 10

NEG = -1e30

VMEM_LIMIT = 56 * 1024 * 1024

TM_PROJ = 512
T_ATT = 512
TM_MID = 512
TM_GW = 1024
TM_DU = 512

REL7 = [(0, 0, 1), (0, 1, 0), (0, 1, 1), (1, 0, 0), (1, 0, 1), (1, 1, 0), (1, 1, 1)]
REL3 = [(0, 1), (1, 0), (1, 1)]

SMALL_SEGS = {}
_row = 0
for _name, _n in (("b_in", D_IN), ("w_pool_mix", 65536), ("b_pool_mix", 512), ("pool_scale", 512),
                  ("b_out", 1024), ("ln_g", 1024), ("ln_b", 1024), ("loss", 1)):
    _rows = -(-_n // 1024) * 8
    SMALL_SEGS[_name] = (_row, _rows)
    _row += _rows
SMALL_ROWS = -(-_row // 16) * 16


def _params(**kw):
    return pltpu.CompilerParams(vmem_limit_bytes=VMEM_LIMIT, **kw)


def _flip(v, d):
    return v if d == 0 else 1 - v


def _dot(a, b):
    return jnp.dot(a, b, preferred_element_type=F32)


def _dot_nt(a, b):
    return lax.dot_general(a, b, (((1,), (1,)), ((), ())), preferred_element_type=F32)


def _dot_tn(a, b):
    return lax.dot_general(a, b, (((0,), (0,)), ((), ())), preferred_element_type=F32)


def _sigmoid(v):
    return 1.0 / (1.0 + jnp.exp(-v))


def _colsum(v):
    return jnp.sum(v, axis=0, keepdims=True)


def _gather_stages(pos, src_ref, dst_ref, half, own_sem, s_sem, r_sem, fs_sem, fr_sem):
    x, y, cc, chip, sib = pos
    own = pltpu.make_async_copy(src_ref, dst_ref.at[chip], own_sem)
    first, landed, others = [], [], []
    for k, (dx, dy) in enumerate(REL3):
        px, py = _flip(x, dx), _flip(y, dy)
        first.append(pltpu.make_async_remote_copy(
            src_ref=src_ref.at[half(cc)], dst_ref=dst_ref.at[(chip,) + half(cc)],
            send_sem=s_sem.at[k], recv_sem=r_sem.at[k], device_id=(px, py, cc), device_id_type=MESH))
        landed.append(dst_ref.at[(2 * px + py,) + half(cc)])
        others.append(dst_ref.at[(2 * px + py,) + half(1 - cc)])
    passed = [pltpu.make_async_remote_copy(src_ref=landed[k], dst_ref=landed[k], send_sem=fs_sem.at[k],
                                           recv_sem=fr_sem.at[k], device_id=sib, device_id_type=MESH)
              for k in range(3)]

    def start(finish_src=None):
        for cp in first:
            cp.start()
        if finish_src is not None:
            finish_src()
        own.start()

    def forward():
        for k in range(3):
            pltpu.make_async_remote_copy(src_ref=landed[k], dst_ref=landed[k], send_sem=s_sem.at[k],
                                         recv_sem=r_sem.at[k], device_id=sib, device_id_type=MESH).wait_recv()
            passed[k].start()

    def finish():
        for k in range(3):
            pltpu.make_async_remote_copy(src_ref=others[k], dst_ref=others[k], send_sem=fs_sem.at[k],
                                         recv_sem=fr_sem.at[k], device_id=sib, device_id_type=MESH).wait_recv()
        for cp in first + passed:
            cp.wait_send()
        own.wait()

    return start, forward, finish


def _gather_scratch():
    return [pltpu.SemaphoreType.DMA, pltpu.SemaphoreType.DMA((3,)), pltpu.SemaphoreType.DMA((3,)),
            pltpu.SemaphoreType.DMA((3,)), pltpu.SemaphoreType.DMA((3,))]


def _gather_and_ada(c, w_ada, b_ada4, w_in_sh):
    def body(c_ref, w_ref, b_ref, win_ref, call_ref, shift_ref, scale_ref, gate_ref, wt_pad_ref,
             win_all, win_bf, ada_ref, cslab, sbuf, rbuf, cs_sem, cr_sem, as_sem, ar_sem, *gather_sems):
        x, y, cc = lax.axis_index("x"), lax.axis_index("y"), lax.axis_index("c")
        me = 4 * x + 2 * y + cc
        chip = 2 * x + y
        lane_half = lambda which: (slice(None), pl.ds(pl.multiple_of(which * (D // 2), D // 2), D // 2))
        def round_half(which):
            for h in range(2):
                @pl.when(which == h)
                def _():
                    lanes = slice(h * (D // 2), (h + 1) * (D // 2))
                    win_bf[:, lanes] = win_ref[:, 0, lanes].astype(BF16)

        start, forward, finish = _gather_stages((x, y, cc, chip, (x, y, 1 - cc)), win_bf, win_all, lane_half,
                                                *gather_sems)
        round_half(cc)
        start(lambda: round_half(1 - cc))

        cslab[...] = jnp.broadcast_to(c_ref[...], (8, D))
        call_ref[me] = cslab[...]
        gathers = []
        for k, (dx, dy, dc) in enumerate(REL7):
            cp = pltpu.make_async_remote_copy(
                src_ref=cslab, dst_ref=call_ref.at[me], send_sem=cs_sem.at[k], recv_sem=cr_sem.at[k],
                device_id=(_flip(x, dx), _flip(y, dy), _flip(cc, dc)), device_id_type=MESH)
            cp.start()
            gathers.append(cp)
        for cp in gathers:
            cp.wait()
        slab_row = lax.broadcasted_iota(jnp.int32, (8, 1), 0)
        mat = jnp.zeros((8, D), F32)
        for r in range(8):
            mat = jnp.where(slab_row == r, call_ref[r], mat)
        act = (mat * _sigmoid(mat)).astype(BF16)
        part = _dot(act, w_ref[...].astype(BF16))
        sends = []
        for k, (dx, dy) in enumerate(REL3):
            px, py = _flip(x, dx), _flip(y, dy)
            r = 4 * px + 2 * py + cc
            piece = _colsum(jnp.where(slab_row == r, part, 0.0))
            sbuf[k] = jnp.broadcast_to(piece, (8, SHARD_ADA))
            cp = pltpu.make_async_remote_copy(
                src_ref=sbuf.at[k], dst_ref=rbuf.at[k], send_sem=as_sem.at[k], recv_sem=ar_sem.at[k],
                device_id=(px, py, cc), device_id_type=MESH)
            cp.start()
            sends.append(cp)
        own_piece = _colsum(jnp.where(slab_row == me, part, 0.0))
        ada_ref[chip] = jnp.broadcast_to(own_piece, (8, SHARD_ADA)) + b_ref[chip]
        for k, (dx, dy) in enumerate(REL3):
            sends[k].wait()
            a = 2 * _flip(x, dx) + _flip(y, dy)
            ada_ref[a] = rbuf[k] + b_ref[a]
        ada = jnp.concatenate([ada_ref[a][0:1, :] for a in range(N_CHIPS)], axis=1)
        shift_ref[...] = ada[:, 0:D]
        scale_ref[...] = ada[:, D:2 * D]
        gate_ref[...] = ada[:, 2 * D:3 * D]

        forward()
        finish()
        n_real = 3 * D_ATT + N_HEADS
        for a in range(N_CHIPS):
            lo, hi = a * SHARD_IN, (a + 1) * SHARD_IN
            for s0, s1 in ((lo, min(hi, D_ATT)), (max(lo, D_ATT), min(hi, n_real)), (max(lo, n_real), hi)):
                if s0 < s1:
                    rows = win_all[a, s0 - lo:s1 - lo, :]
                    if s1 <= D_ATT:
                        rows = rows * jnp.asarray(Q_SCALE, BF16)
                    shift = O_P - n_real if s0 >= n_real else 0
                    wt_pad_ref[s0 + shift:s1 + shift, :] = rows
        wt_pad_ref[n_real:O_P, :] = jnp.zeros((O_P - n_real, D), BF16)

    vm = pl.BlockSpec(memory_space=pltpu.VMEM)
    return pl.pallas_call(
        body, name="gather_and_ada",
        out_shape=(jax.ShapeDtypeStruct((8, 8, D), F32),) + (jax.ShapeDtypeStruct((1, D), F32),) * 3
        + (jax.ShapeDtypeStruct((D_PAD, D), BF16),),
        in_specs=[vm] * 4, out_specs=(vm,) * 5,
        scratch_shapes=[pltpu.VMEM((N_CHIPS, SHARD_IN, D), BF16), pltpu.VMEM((SHARD_IN, D), BF16),
                        pltpu.VMEM((N_CHIPS, 8, SHARD_ADA), F32), pltpu.VMEM((8, D), F32), pltpu.VMEM((3, 8, SHARD_ADA), F32),
                        pltpu.VMEM((3, 8, SHARD_ADA), F32),
                        pltpu.SemaphoreType.DMA((7,)), pltpu.SemaphoreType.DMA((7,)),
                        pltpu.SemaphoreType.DMA((3,)), pltpu.SemaphoreType.DMA((3,))] + _gather_scratch(),
        compiler_params=_params(),
    )(c, w_ada, b_ada4, w_in_sh)


def _shard_cols():
    cut, gap = O_F + N_HEADS, O_P - (O_F + N_HEADS)
    out = []
    for a in range(N_CHIPS):
        lo, hi = a * SHARD_IN, (a + 1) * SHARD_IN
        out.append(([(lo, min(hi, cut))] if lo < cut else []) + ([(max(lo, cut) + gap, hi + gap)] if hi > cut else []))
    return out


def _scatter_stages(pos, g_ref, sc_ref, out_ref, sib_buf, send_buf, ici_buf, sem1, sem2s, sem2r, sem3, part=(0, 1),
                    cols=None, own_buf=None, staged=None):
    x, y, cc, chip, sib = pos
    q, n_parts = part
    RH = (g_ref.shape[1] if cols is None else g_ref.shape[0]) // 2 // n_parts
    mine = pl.ds(pl.multiple_of((cc * n_parts + q) * RH, RH), RH)
    theirs = pl.ds(pl.multiple_of(((1 - cc) * n_parts + q) * RH, RH), RH)
    cp1 = pltpu.make_async_remote_copy(
        src_ref=g_ref.at[:, theirs, :] if cols is None else g_ref.at[theirs, :], dst_ref=sib_buf,
        send_sem=sem1.at[0], recv_sem=sem1.at[1], device_id=sib, device_id_type=MESH)
    sends = []
    for k, (dx, dy) in enumerate(REL3):
        px, py = _flip(x, dx), _flip(y, dy)
        sends.append(pltpu.make_async_remote_copy(
            src_ref=send_buf.at[2 * px + py], dst_ref=ici_buf.at[chip],
            send_sem=sem2s.at[k], recv_sem=sem2r.at[k], device_id=(px, py, cc), device_id_type=MESH))
    cp3 = pltpu.make_async_remote_copy(
        src_ref=out_ref.at[mine, :], dst_ref=out_ref.at[mine, :], send_sem=sem3.at[0], recv_sem=sem3.at[1],
        device_id=sib, device_id_type=MESH)

    if staged is not None:
        stage = pltpu.make_async_copy(g_ref.at[mine, :], staged[0], staged[1])

    def start1():
        cp1.start()
        if staged is not None:
            stage.start()

    def finish1():
        cp1.wait()
        if cols is None:
            for a in range(N_CHIPS):
                both = g_ref[a, mine, :] + sib_buf[a]
                sib_buf[a] = both
                send_buf[a] = both.astype(BF16)
        else:
            if staged is not None:
                stage.wait()
            both = (g_ref[mine, :] if staged is None else staged[0][...]) + sib_buf[...]
            for a, pieces in enumerate(cols):
                at = 0
                for lo, hi in pieces:
                    own_buf[a, :, at:at + hi - lo] = both[:, lo:hi]
                    send_buf[a, :, at:at + hi - lo] = both[:, lo:hi].astype(BF16)
                    at += hi - lo

    def start2():
        for cp in sends:
            cp.start()
        ici_buf[chip] = send_buf[chip]

    def finish2():
        for cp in sends:
            cp.wait()
        own = (sib_buf if cols is None else own_buf)[chip]
        parts = [jnp.where(chip == a, own, ici_buf[a].astype(F32)) for a in range(N_CHIPS)]
        total = (parts[0] + parts[1]) + (parts[2] + parts[3])
        out_ref[mine, 0:own.shape[1]] = total if sc_ref is None else total * sc_ref[chip]

    return [(start1, finish1), (start2, finish2), (cp3.start, cp3.wait)]


def _all_reduce_stages(pos, g_ref, out_ref, sib_buf, ici_buf, sem1, sem2s, sem2r, sem3):
    x, y, cc, chip, sib = pos
    RH = g_ref.shape[0] // 2
    mine = pl.ds(pl.multiple_of(cc * RH, 8), RH)
    theirs = pl.ds(pl.multiple_of((1 - cc) * RH, 8), RH)
    cp1 = pltpu.make_async_remote_copy(
        src_ref=g_ref.at[theirs, :], dst_ref=sib_buf, send_sem=sem1.at[0], recv_sem=sem1.at[1],
        device_id=sib, device_id_type=MESH)
    sends = []
    for k, (dx, dy) in enumerate(REL3):
        px, py = _flip(x, dx), _flip(y, dy)
        sends.append(pltpu.make_async_remote_copy(
            src_ref=sib_buf, dst_ref=ici_buf.at[chip],
            send_sem=sem2s.at[k], recv_sem=sem2r.at[k], device_id=(px, py, cc), device_id_type=MESH))
    cp3 = pltpu.make_async_remote_copy(
        src_ref=out_ref.at[mine, :], dst_ref=out_ref.at[mine, :], send_sem=sem3.at[0], recv_sem=sem3.at[1],
        device_id=sib, device_id_type=MESH)

    def finish1():
        cp1.wait()
        sib_buf[...] = g_ref[mine, :] + sib_buf[...]

    def start2():
        for cp in sends:
            cp.start()
        ici_buf[chip] = sib_buf[...]

    def finish2():
        for cp in sends:
            cp.wait()
        out_ref[mine, :] = (ici_buf[0] + ici_buf[1]) + (ici_buf[2] + ici_buf[3])

    return [(cp1.start, finish1), (start2, finish2), (cp3.start, cp3.wait)]


def _stage_sems():
    return [pltpu.SemaphoreType.DMA((2,)), pltpu.SemaphoreType.DMA((3,)),
            pltpu.SemaphoreType.DMA((3,)), pltpu.SemaphoreType.DMA((2,))]


def _scatter_scratch(r, c):
    return [pltpu.VMEM((N_CHIPS, r // 2, c), F32), pltpu.VMEM((N_CHIPS, r // 2, c), BF16),
            pltpu.VMEM((N_CHIPS, r // 2, c), BF16)] + _stage_sems()


def _reduce_all(gw_pad, sc_in, small, dada, c_all):
    R = small.shape[0]
    W = dada.shape[1]
    r_in, p_in = gw_pad.shape
    c_in = SHARD_IN
    chunk = r_in // 4

    def chunk_scratch():
        return ([pltpu.VMEM((chunk, p_in), F32), pltpu.VMEM((N_CHIPS, chunk, c_in), BF16),
                 pltpu.VMEM((N_CHIPS, chunk, c_in), BF16)] + _stage_sems()
                + [pltpu.VMEM((N_CHIPS, chunk, c_in), F32), pltpu.VMEM((chunk, p_in), F32), pltpu.SemaphoreType.DMA])

    n_in = len(chunk_scratch())

    c_wide = -(-c_in // 128) * 128

    def body(gin_ref, scin_ref, sm_ref, d_ref, c_ref, ocols_ref, osm_ref, gwa_ref, gba_ref, loss_ref, oin_ref,
             dall_ref, *scratch):
        x, y, cc = lax.axis_index("x"), lax.axis_index("y"), lax.axis_index("c")
        me = 4 * x + 2 * y + cc
        chip = 2 * x + y
        pos = (x, y, cc, chip, (x, y, 1 - cc))
        oin_ref[:, c_in:c_wide] = jnp.zeros((r_in, c_wide - c_in), F32)
        dslab, ds_sem, dr_sem = scratch[0:3]
        a_bufs, b_bufs, sm_bufs = scratch[3:3 + n_in], scratch[3 + n_in:3 + 2 * n_in], scratch[3 + 2 * n_in:]
        dslab[...] = jnp.broadcast_to(d_ref[...], (8, W))
        dall_ref[me] = dslab[...]
        gathers = []
        for k, (dx, dy, dc) in enumerate(REL7):
            cp = pltpu.make_async_remote_copy(
                src_ref=dslab, dst_ref=dall_ref.at[me], send_sem=ds_sem.at[k], recv_sem=dr_sem.at[k],
                device_id=(_flip(x, dx), _flip(y, dy), _flip(cc, dc)), device_id_type=MESH)
            cp.start()
            gathers.append(cp)
        cols = _shard_cols()
        first = _scatter_stages(pos, gin_ref, scin_ref, oin_ref, *a_bufs[:-3], part=(0, 2), cols=cols,
                                own_buf=a_bufs[-3], staged=a_bufs[-2:])
        second = _scatter_stages(pos, gin_ref, scin_ref, oin_ref, *b_bufs[:-3], part=(1, 2), cols=cols,
                                 own_buf=b_bufs[-3], staged=b_bufs[-2:])
        little = _all_reduce_stages(pos, sm_ref, osm_ref, *sm_bufs)
        for plan in (first, second, little):
            plan[0][0]()
        first[0][1]()
        first[1][0]()
        little[0][1]()
        little[1][0]()
        second[0][1]()
        second[1][0]()
        for cp in gathers:
            cp.wait()
        slab_row = lax.broadcasted_iota(jnp.int32, (8, 1), 0)
        cm = jnp.zeros((8, D), F32)
        dm = jnp.zeros((8, W), F32)
        for r in range(8):
            cm = jnp.where(slab_row == r, c_ref[r], cm)
            dm = jnp.where(slab_row == r, dall_ref[r], dm)
        act = cm * _sigmoid(cm)
        dcol = dm[:, 0:SHARD_ADA]
        for a in range(1, N_CHIPS):
            dcol = jnp.where(chip == a, dm[:, a * SHARD_ADA:(a + 1) * SHARD_ADA], dcol)
        lhs = jnp.concatenate([act, jnp.zeros((8, D), F32)], axis=0).astype(BF16)
        rhs = jnp.concatenate([dcol, jnp.zeros((8, SHARD_ADA), F32)], axis=0).astype(BF16)
        gwa_ref[...] = _dot_tn(lhs, rhs)
        gba_ref[...] = _colsum(dm)
        first[1][1]()
        first[2][0]()
        second[1][1]()
        second[2][0]()
        little[1][1]()
        little[2][0]()
        for plan in (first, second, little):
            plan[2][1]()
        loss_row = SMALL_SEGS["loss"][0]
        loss_ref[...] = osm_ref[loss_row:loss_row + 1, 0:1]
        ocols_ref[...] = oin_ref[...].T[0:c_in, :][:, None, :]

    scratch = [pltpu.VMEM((r_in, c_wide), F32), pltpu.VMEM((8, 8, W), F32),
               pltpu.VMEM((8, W), F32), pltpu.SemaphoreType.DMA((7,)), pltpu.SemaphoreType.DMA((7,))]
    scratch += chunk_scratch() + chunk_scratch()
    scratch += [pltpu.VMEM((R // 2, 128), F32), pltpu.VMEM((N_CHIPS, R // 2, 128), F32)] + _stage_sems()
    vm = pl.BlockSpec(memory_space=pltpu.VMEM)
    return pl.pallas_call(
        body, name="reduce_all",
        out_shape=(jax.ShapeDtypeStruct((c_in, 1, r_in), F32), jax.ShapeDtypeStruct((R, 128), F32),
                   jax.ShapeDtypeStruct((D, SHARD_ADA), F32), jax.ShapeDtypeStruct((1, W), F32),
                   jax.ShapeDtypeStruct((1, 1), F32)),
        in_specs=[pl.BlockSpec(memory_space=pl.ANY)] + [vm] * 4, out_specs=(vm,) * 5,
        scratch_shapes=scratch,
        compiler_params=_params(),
    )(gw_pad, sc_in, small, dada, c_all)


def _in_proj(x, shift, scale, wt_pad, b_pad, w_out_sh):
    S = x.shape[0]
    tm = min(TM_PROJ, S)
    n_steps = S // tm
    assert n_steps >= 3

    def body(x_ref, sh_ref, sc_ref, w_ref, b_ref, wo_ref, u_ref, qkv_ref, f_ref, p_ref, g_ref, wo_all,
             wo_buf, wo_bf, *gather_sems):
        i = pl.program_id(0)
        xx, yy, cc = lax.axis_index("x"), lax.axis_index("y"), lax.axis_index("c")
        row_half = lambda which: (pl.ds(pl.multiple_of(which * (SHARD_OUT // 2), SHARD_OUT // 2), SHARD_OUT // 2),
                                  slice(None))
        start, forward, finish = _gather_stages((xx, yy, cc, 2 * xx + yy, (xx, yy, 1 - cc)), wo_bf, wo_buf,
                                                row_half, *gather_sems)

        @pl.when(i == 0)
        def _():
            wo_bf[...] = wo_ref[...].astype(BF16)
            start()

        pl.when(i == n_steps // 2)(forward)

        @pl.when(i == n_steps - 1)
        def _():
            finish()
            wo_all[...] = wo_buf[...]

        u = (x_ref[...] * (1.0 + sc_ref[...]) + sh_ref[...]).astype(BF16)
        u_ref[...] = u
        qkv_ref[...] = (_dot_nt(u, w_ref[O_QKV:O_F, :]) + b_ref[:, O_QKV:O_F]).astype(BF16)
        f_ref[...] = _dot_nt(u, w_ref[O_F:O_P, :]) + b_ref[:, O_F:O_P]
        p_ref[...] = _dot_nt(u, w_ref[O_P:O_G, :]) + b_ref[:, O_P:O_G]
        g_ref[...] = _dot_nt(u, w_ref[O_G:D_PAD, :]) + b_ref[:, O_G:D_PAD]

    row = lambda w: pl.BlockSpec((tm, w), lambda i: (i, 0))
    full = lambda a: pl.BlockSpec(a.shape, lambda i: (0, 0))
    vm = pl.BlockSpec(memory_space=pltpu.VMEM)
    return pl.pallas_call(
        body, name="in_proj", grid=(n_steps,),
        out_shape=(jax.ShapeDtypeStruct((S, D), BF16), jax.ShapeDtypeStruct((S, 3 * D_ATT), BF16),
                   jax.ShapeDtypeStruct((S, 128), F32), jax.ShapeDtypeStruct((S, D_POOL), F32),
                   jax.ShapeDtypeStruct((S, D), F32), jax.ShapeDtypeStruct((N_CHIPS,) + w_out_sh.shape, BF16)),
        in_specs=[row(D), full(shift), full(scale), full(wt_pad), full(b_pad), vm],
        out_specs=(row(D), row(3 * D_ATT), row(128), row(D_POOL), row(D), vm),
        scratch_shapes=[pltpu.VMEM((N_CHIPS,) + w_out_sh.shape, BF16), pltpu.VMEM(w_out_sh.shape, BF16)]
        + _gather_scratch(),
        compiler_params=_params(dimension_semantics=("arbitrary",)),
    )(x, shift, scale, wt_pad, b_pad, w_out_sh)


def _forget_cumsum(f):
    S = f.shape[0]
    tm = min(T_ATT, S)

    def body(f_ref, out_ref, carry):
        @pl.when(pl.program_id(0) == 0)
        def _():
            carry[...] = jnp.zeros_like(carry)
        v = f_ref[...]
        logf = jnp.minimum(v, 0.0) - jnp.log(1.0 + jnp.exp(-jnp.abs(v)))
        r = lax.broadcasted_iota(jnp.int32, (tm, tm), 0)
        c = lax.broadcasted_iota(jnp.int32, (tm, tm), 1)
        tri = (r <= c).astype(F32)
        rows8 = logf.T[0:8, :]
        cum8 = jnp.dot(rows8, tri, preferred_element_type=F32, precision=lax.Precision.HIGHEST) + carry[...]
        out_ref[...] = jnp.concatenate([cum8, jnp.zeros((128 - 8, tm), F32)], axis=0).T
        last = lax.broadcasted_iota(jnp.int32, (1, tm), 1) == tm - 1
        carry[...] = jnp.sum(jnp.where(last, cum8, 0.0), axis=1, keepdims=True)

    return pl.pallas_call(
        body, name="forget_cumsum", grid=(S // tm,),
        out_shape=jax.ShapeDtypeStruct((S, 128), F32),
        in_specs=[pl.BlockSpec((tm, 128), lambda i: (i, 0))],
        out_specs=pl.BlockSpec((tm, 128), lambda i: (i, 0)),
        scratch_shapes=[pltpu.VMEM((8, 1), F32)],
        compiler_params=_params(dimension_semantics=("arbitrary",)),
    )(f)


def _split3(v):
    hi = v.astype(BF16)
    rest = v - hi.astype(F32)
    mid = rest.astype(BF16)
    lo = (rest - mid.astype(F32)).astype(BF16)
    return hi, mid, lo


def _attention_fwd(qkv, big_f):
    S = qkv.shape[0]
    T = min(T_ATT, S)
    n_t = S // T

    def body(q_ref, k_ref, v_ref, f_ref, o_ref, lse_ref, kaug_sc, vt_sc, m_sc, l_sc, acc_sc):
        hp = pl.program_id(0)
        i = pl.program_id(1)
        lane = lax.broadcasted_iota(jnp.int32, (1, 128), 1)
        sub = lax.broadcasted_iota(jnp.int32, (128, 1), 0)
        head_sel = (lane < HEAD_DIM, lane >= HEAD_DIM)
        head_sel_t = (sub < HEAD_DIM, sub >= HEAD_DIM)
        spare = (HEAD_DIM, 0)
        zero = jnp.zeros((), BF16)

        @pl.when(i == 0)
        def _():
            def prep(jt, carry):
                rows = pl.ds(pl.multiple_of(jt * T, T), T)
                k = k_ref[rows, :]
                ft = f_ref[rows, :]
                vt = v_ref[rows, :].astype(F32).T
                for h in range(2):
                    fh = jnp.sum(jnp.where(lane == 2 * hp + h, ft, 0.0), axis=1, keepdims=True)
                    hi, mid, lo = _split3(-fh)
                    b = spare[h]
                    bias = jnp.where(lane == b, hi, jnp.where(lane == b + 1, mid, jnp.where(lane == b + 2, lo, zero)))
                    kaug_sc[h, rows, :] = jnp.where(head_sel[h], k, bias)
                    vt_sc[h, jt] = jnp.where(head_sel_t[h], vt, 0.0).astype(BF16)
                return carry

            lax.fori_loop(0, n_t, prep, 0)

        q = q_ref[...]
        q_heads = []
        for h in range(2):
            ones = jnp.where((lane >= spare[h]) & (lane < spare[h] + 3), jnp.ones((), BF16), zero)
            q_heads.append(jnp.where(head_sel[h], q, ones))
        m_sc[...] = jnp.full((8, T), NEG, F32)
        l_sc[...] = jnp.zeros((8, T), F32)
        acc_sc[...] = jnp.zeros((128, T), F32)

        def update(j, k_lo, n_k, q_lo, masked):
            rows = pl.ds(pl.multiple_of(j * T + k_lo, n_k), n_k)
            n_q = T - q_lo
            alphas, pvs = [], []
            for h in range(2):
                s_t = _dot_nt(kaug_sc[h, rows, :], q_heads[h][q_lo:, :])
                if masked:
                    rr = lax.broadcasted_iota(jnp.int32, (n_k, n_q), 0) + k_lo
                    cc = lax.broadcasted_iota(jnp.int32, (n_k, n_q), 1) + q_lo
                    s_t = jnp.where(rr <= cc, s_t, NEG)
                m_prev = m_sc[h:h + 1, q_lo:]
                m_new = jnp.maximum(m_prev, jnp.max(s_t, axis=0, keepdims=True))
                alpha = jnp.exp(m_prev - m_new)
                p_t = jnp.exp(s_t - m_new)
                l_sc[h:h + 1, q_lo:] = alpha * l_sc[h:h + 1, q_lo:] + jnp.sum(p_t, axis=0, keepdims=True)
                m_sc[h:h + 1, q_lo:] = m_new
                alphas.append(alpha)
                pvs.append(_dot(vt_sc[h, j, :, k_lo:k_lo + n_k], p_t.astype(BF16)))
            acc_sc[:, q_lo:] = (acc_sc[:, q_lo:] * jnp.where(head_sel_t[0], alphas[0], alphas[1])
                                + (pvs[0] + pvs[1]))

        def two_off_diagonal(jj, carry):
            update(2 * jj, 0, T, 0, False)
            update(2 * jj + 1, 0, T, 0, False)
            return carry

        lax.fori_loop(0, i // 2, two_off_diagonal, 0)

        @pl.when(i % 2 == 1)
        def _():
            update(i - 1, 0, T, 0, False)
            update(i, 0, T, 0, True)

        @pl.when(i % 2 == 0)
        def _():
            update(i, 0, T, 0, True)

        l = l_sc[...]
        o_ref[...] = (acc_sc[...] / jnp.where(head_sel_t[0], l[0:1, :], l[1:2, :])).T
        is_head = lax.broadcasted_iota(jnp.int32, (8, 1), 0) < 2
        lse_ref[...] = jnp.where(is_head, m_sc[...] + jnp.log(jnp.where(is_head, l, 1.0)), 0.0)

    return pl.pallas_call(
        body, name="attention_fwd", grid=(N_PAIR, n_t),
        out_shape=(jax.ShapeDtypeStruct((S, D_ATT), F32), jax.ShapeDtypeStruct((N_PAIR, n_t, 8, T), F32)),
        in_specs=[pl.BlockSpec((T, 128), lambda hp, i: (i, hp)),
                  pl.BlockSpec((S, 128), lambda hp, i: (0, N_PAIR + hp)),
                  pl.BlockSpec((S, 128), lambda hp, i: (0, 2 * N_PAIR + hp)),
                  pl.BlockSpec((S, 128), lambda hp, i: (0, 0))],
        out_specs=(pl.BlockSpec((T, 128), lambda hp, i: (i, hp)),
                   pl.BlockSpec((None, None, 8, T), lambda hp, i: (hp, i, 0, 0))),
        scratch_shapes=[pltpu.VMEM((2, S, 128), BF16), pltpu.VMEM((2, n_t, 128, T), BF16),
                        pltpu.VMEM((8, T), F32), pltpu.VMEM((8, T), F32), pltpu.VMEM((128, T), F32)],
        compiler_params=_params(dimension_semantics=("arbitrary", "arbitrary")),
    )(qkv, qkv, qkv, big_f)


def _attention_bwd(qkv, datt, att, lse, big_f, gw_out4):
    S = qkv.shape[0]
    T = min(T_ATT, S)
    n_t = S // T
    n_steps = N_PAIR * n_t
    marks = (0, n_steps // 8, n_steps // 2, n_steps // 2 + n_steps // 8)

    def body(q_ref, do_ref, o_ref, lse_ref, k_ref, v_ref, fk_ref, gout_ref,
             dq_ref, dk_ref, dv_ref, cs_ref, dfk_ref, dfq_ref, oout_ref, stat_sc, dqt_sc, qaug_sc,
             out_buf, *red_bufs):
        hp = pl.program_id(0)
        j = pl.program_id(1)
        x, y, cc = lax.axis_index("x"), lax.axis_index("y"), lax.axis_index("c")
        plan = _scatter_stages((x, y, cc, 2 * x + y, (x, y, 1 - cc)), gout_ref, None, out_buf, *red_bufs)
        step = hp * n_t + j
        for n, mark in enumerate(marks):
            @pl.when(step == mark)
            def _(n=n):
                if n > 0:
                    plan[n - 1][1]()
                if n < 3:
                    plan[n][0]()
                else:
                    oout_ref[...] = out_buf[...]

        lane = lax.broadcasted_iota(jnp.int32, (1, 128), 1)
        sub = lax.broadcasted_iota(jnp.int32, (128, 1), 0)
        head_sel = (lane < HEAD_DIM, lane >= HEAD_DIM)
        head_sel_t = (sub < HEAD_DIM, sub >= HEAD_DIM)
        spare = (HEAD_DIM, 0)
        zero = jnp.zeros((), BF16)
        one = jnp.ones((), BF16)

        def bias_lanes(first, pieces):
            hi, mid, lo = pieces
            return lambda rest: jnp.where(lane == first, hi, jnp.where(lane == first + 1, mid,
                                                                        jnp.where(lane == first + 2, lo, rest)))

        @pl.when(j == 0)
        def _():
            dqt_sc[...] = jnp.zeros_like(dqt_sc)
            cs_ref[...] = jnp.zeros_like(cs_ref)
            dfq_ref[...] = jnp.zeros_like(dfq_ref)

            def prep(i, carry):
                rows = pl.ds(pl.multiple_of(i * T, T), T)
                q = q_ref[rows, :]
                do = do_ref[rows, :]
                prod = o_ref[rows, :] * do.astype(F32)
                d_a = jnp.sum(jnp.where(head_sel[0], prod, 0.0), axis=1, keepdims=True)
                d_b = jnp.sum(jnp.where(head_sel[0], 0.0, prod), axis=1, keepdims=True)
                delta_t = jnp.where(head_sel[0], d_a, d_b).T
                stat_sc[i, 0:1, :] = delta_t[0:1, :]
                stat_sc[i, 1:2, :] = delta_t[HEAD_DIM:HEAD_DIM + 1, :]
                lse = lse_ref[i]
                lse_cols = jnp.where(head_sel_t[0], lse[0:1, :], lse[1:2, :]).T
                for h in range(2):
                    neg_lse = -lse_cols[:, h * HEAD_DIM:h * HEAD_DIM + 1]
                    ones = jnp.where((lane >= spare[h]) & (lane < spare[h] + 3), one, zero)
                    qaug_sc[h, rows, :] = jnp.where(head_sel[h], q, bias_lanes(spare[h] + 3, _split3(neg_lse))(ones))
                return carry

            lax.fori_loop(0, n_t, prep, 0)

        k = k_ref[...]
        v = v_ref[...]
        fk = fk_ref[...]
        kt = k.astype(F32).T
        heads = []
        for h in range(2):
            fkh = jnp.sum(jnp.where(lane == 2 * hp + h, fk, 0.0), axis=1, keepdims=True)
            ones = jnp.where((lane >= spare[h] + 3) & (lane < spare[h] + 6), one, zero)
            kaug = jnp.where(head_sel[h], k, bias_lanes(spare[h], _split3(-fkh))(ones))
            heads.append((kaug, jnp.where(head_sel[h], v, zero), jnp.where(head_sel_t[h], kt, 0.0).astype(BF16)))

        def block(i, k_lo, n_k, q_lo, masked):
            n_q = T - q_lo
            rows = pl.ds(pl.multiple_of(i * T + q_lo, n_q), n_q)
            q = q_ref[rows, :]
            do = do_ref[rows, :]
            stat = stat_sc[i]
            dk = jnp.zeros((n_k, 128), F32)
            dv = jnp.zeros((n_k, 128), F32)
            dqt = jnp.zeros((128, n_q), F32)
            dfs = []
            for h in range(2):
                kaug, vh, kth = heads[h]
                arg = _dot_nt(kaug[k_lo:k_lo + n_k, :], qaug_sc[h, rows, :])
                if masked:
                    rr = lax.broadcasted_iota(jnp.int32, (n_k, n_q), 0) + k_lo
                    cc = lax.broadcasted_iota(jnp.int32, (n_k, n_q), 1) + q_lo
                    arg = jnp.where(rr <= cc, arg, NEG)
                p_t = jnp.exp(arg)
                ds_t = p_t * (_dot_nt(vh[k_lo:k_lo + n_k, :], do) - stat[h:h + 1, q_lo:])
                ds_bf = ds_t.astype(BF16)
                dv = dv + _dot(p_t.astype(BF16), jnp.where(head_sel[h], do, zero))
                dk = dk + _dot(ds_bf, jnp.where(head_sel[h], q, zero))
                dqt = dqt + _dot(kth[:, k_lo:k_lo + n_k], ds_bf)
                dfs.append(jnp.sum(ds_t, axis=1, keepdims=True))
                dfq_ref[i, h:h + 1, q_lo:] += _colsum(ds_t)
            dqt_sc[i, :, q_lo:] += dqt
            return dk, dv, dfs[0], dfs[1]

        def off_diagonal(i, acc):
            return tuple(a + b for a, b in zip(acc, block(i, 0, T, 0, False)))

        half = T // 2
        early = block(j, 0, half, 0, True)
        late = block(j, half, half, half, True)
        acc1 = tuple(jnp.concatenate([a, b], axis=0) for a, b in zip(early, late))
        n_off = n_t - 1 - j
        acc2 = lax.fori_loop(0, n_off // 2,
                             lambda ii, a: off_diagonal(j + 2 + 2 * ii, off_diagonal(j + 1 + 2 * ii, a)), acc1)
        dk_acc, dv_acc, dfa, dfb = lax.fori_loop(0, n_off % 2, lambda _, a: off_diagonal(n_t - 1, a), acc2)
        dk_ref[...] = dk_acc.astype(BF16)
        dv_ref[...] = dv_acc.astype(BF16)
        dfk_ref[...] = -jnp.where(lane == 0, dfa, jnp.where(lane == 1, dfb, 0.0))
        cs_ref[:, 128:256] = cs_ref[:, 128:256] + _colsum(dk_acc)
        cs_ref[:, 256:384] = cs_ref[:, 256:384] + _colsum(dv_acc)

        @pl.when(j == n_t - 1)
        def _():
            def finish(i, tot):
                dq = dqt_sc[i].T
                dq_ref[pl.ds(pl.multiple_of(i * T, T), T), :] = dq.astype(BF16)
                return tot + _colsum(dq)

            cs_ref[:, 0:128] = lax.fori_loop(0, n_t, finish, jnp.zeros((1, 128), F32))

    pair_rows = lambda hp, j: (hp, 0, 0)
    vm = pl.BlockSpec(memory_space=pltpu.VMEM)
    _, r_out, c_out = gw_out4.shape
    return pl.pallas_call(
        body, name="attention_bwd", grid=(N_PAIR, n_t),
        out_shape=(jax.ShapeDtypeStruct((S, D_ATT), BF16), jax.ShapeDtypeStruct((S, D_ATT), BF16),
                   jax.ShapeDtypeStruct((S, D_ATT), BF16), jax.ShapeDtypeStruct((N_PAIR, 1, 384), F32),
                   jax.ShapeDtypeStruct((N_PAIR, S, 128), F32),
                   jax.ShapeDtypeStruct((N_PAIR, n_t, 8, T), F32),
                   jax.ShapeDtypeStruct((r_out, c_out), F32)),
        in_specs=[pl.BlockSpec((S, 128), lambda hp, j: (0, hp)),
                  pl.BlockSpec((S, 128), lambda hp, j: (0, hp)),
                  pl.BlockSpec((S, 128), lambda hp, j: (0, hp)),
                  pl.BlockSpec((None, n_t, 8, T), lambda hp, j: (hp, 0, 0, 0)),
                  pl.BlockSpec((T, 128), lambda hp, j: (j, N_PAIR + hp)),
                  pl.BlockSpec((T, 128), lambda hp, j: (j, 2 * N_PAIR + hp)),
                  pl.BlockSpec((T, 128), lambda hp, j: (j, 0)),
                  vm],
        out_specs=(pl.BlockSpec((S, 128), lambda hp, j: (0, hp)),
                   pl.BlockSpec((T, 128), lambda hp, j: (j, hp)),
                   pl.BlockSpec((T, 128), lambda hp, j: (j, hp)),
                   pl.BlockSpec((None, 1, 384), pair_rows),
                   pl.BlockSpec((None, T, 128), lambda hp, j: (hp, j, 0)),
                   pl.BlockSpec((None, n_t, 8, T), lambda hp, j: (hp, 0, 0, 0)),
                   vm),
        scratch_shapes=[pltpu.VMEM((n_t, 8, T), F32), pltpu.VMEM((n_t, 128, T), F32),
                        pltpu.VMEM((2, S, 128), BF16), pltpu.VMEM((r_out, c_out), F32)]
        + _scatter_scratch(r_out, c_out),
        compiler_params=_params(dimension_semantics=("arbitrary", "arbitrary")),
    )(qkv, datt, att, lse, qkv, qkv, big_f, gw_out4)


def _window_counts(first_row, n_rows, window):
    t = lax.broadcasted_iota(jnp.int32, (n_rows, 1), 0) + first_row
    return jnp.minimum((t + 1).astype(F32), float(window))


def _middle(x, tgt, att, g, p, gate, w_mix, b_mix, pool_scale, w_out, b_out, ln_g, ln_b):
    S = x.shape[0]
    tm = min(TM_MID, S)
    halo_blocks = tm // POOL_HALO

    def body(x_ref, t_ref, att_ref, g_ref, p_ref, ph_ref, gate_ref, wm_ref, bm_ref, ps_ref, wo_ref, bo_ref,
             lg_ref, lb_ref,
             dh_ref, datt_ref, dg_ref, dpl_ref, gwo_ref, gwm_ref, vec_ref, loss_ref):
        i = pl.program_id(0)

        @pl.when(i == 0)
        def _():
            gwo_ref[...] = jnp.zeros_like(gwo_ref)
            gwm_ref[...] = jnp.zeros_like(gwm_ref)
            vec_ref[...] = jnp.zeros_like(vec_ref)
            loss_ref[...] = jnp.zeros_like(loss_ref)

        pc = p_ref[...]
        halo = jnp.where(i > 0, ph_ref[...], 0.0)
        pe = jnp.concatenate([halo, pc], axis=0)
        pooled_parts = []
        for gi, w in enumerate(POOL_WINDOWS):
            cur = pe[:, gi * POOL_GROUP:(gi + 1) * POOL_GROUP]
            span = 1
            while span < w:
                cur = cur + pltpu.roll(cur, span, 0)
                span *= 2
            wsum = cur[POOL_HALO:, :]
            mean = wsum / _window_counts(i * tm, tm, w)
            pooled_parts.append(mean - pc[:, gi * POOL_GROUP:(gi + 1) * POOL_GROUP])
        pooled_bf =[v.astype(BF16) for v in pooled_parts]
        wm = [wm_ref[gi].astype(BF16) for gi in range(4)]
        mixed = jnp.concatenate([_dot(pooled_bf[gi], wm[gi]) for gi in range(4)], axis=1) + bm_ref[...]
        ps = ps_ref[...]
        pool_out = mixed * ps
        gv = g_ref[...]
        sig = _sigmoid(gv)
        silu = gv * sig
        att = att_ref[...]
        y = jnp.concatenate([att * silu[:, :D_ATT], pool_out * silu[:, D_ATT:]], axis=1)
        y_bf = y.astype(BF16)
        wo = wo_ref[...]
        yo = _dot(y_bf, wo) + bo_ref[...]
        gate = gate_ref[...]
        h = ALPHA * x_ref[...] + gate * yo
        mu = jnp.mean(h, axis=1, keepdims=True)
        hc = h - mu
        var = jnp.mean(hc * hc, axis=1, keepdims=True)
        rstd = lax.rsqrt(var + LN_EPS)
        yhat = hc * rstd
        lg = lg_ref[...]
        out = yhat * lg + lb_ref[...]
        err = out - t_ref[...]
        loss_ref[...] += 0.5 * jnp.sum(jnp.mean(err * err, axis=1, keepdims=True), axis=0, keepdims=True)

        dout = err * (1.0 / D)
        g_ln_b = _colsum(dout)
        g_ln_g = _colsum(dout * yhat)
        dyh = dout * lg
        dh = rstd * (dyh - jnp.mean(dyh, axis=1, keepdims=True)
                     - yhat * jnp.mean(dyh * yhat, axis=1, keepdims=True))
        dh_ref[...] = dh
        d_gate = _colsum(dh * yo)
        dyo = gate * dh
        g_b_out = _colsum(dyo)
        dyo_bf = dyo.astype(BF16)
        gwo_ref[...] += _dot_tn(y_bf, dyo_bf)
        dy = _dot_nt(dyo_bf, wo)
        dsilu = sig * (1.0 + gv * (1.0 - sig))
        dy_a = dy[:, :D_ATT]
        dy_p = dy[:, D_ATT:]
        datt_ref[...] = (dy_a * silu[:, :D_ATT]).astype(BF16)
        dpo = dy_p * silu[:, D_ATT:]
        dg = jnp.concatenate([dy_a * att * dsilu[:, :D_ATT], dy_p * pool_out * dsilu[:, D_ATT:]], axis=1)
        dg_ref[...] = dg.astype(BF16)
        g_dg = _colsum(dg)
        g_ps = _colsum(dpo * mixed)
        dmixed = dpo * ps
        g_bm = _colsum(dmixed)
        dmixed_bf = dmixed.astype(BF16)
        dpl = []
        for gi in range(4):
            dm = dmixed_bf[:, gi * POOL_GROUP:(gi + 1) * POOL_GROUP]
            gwm_ref[gi] += _dot_tn(pooled_bf[gi], dm)
            dpl.append(_dot_nt(dm, wm[gi]))
        dpl_ref[...] = jnp.concatenate(dpl, axis=1)
        vec_ref[0:1, :] += g_ln_g
        vec_ref[1:2, :] += g_ln_b
        vec_ref[2:3, :] += d_gate
        vec_ref[3:4, :] += g_b_out
        vec_ref[4:5, :] += g_dg
        vec_ref[5:6, 0:D_POOL] += g_ps
        vec_ref[6:7, 0:D_POOL] += g_bm

    row = lambda w: pl.BlockSpec((tm, w), lambda i: (i, 0))
    full2 = lambda a: pl.BlockSpec(a.shape, lambda i: (0, 0))
    full3 = lambda a: pl.BlockSpec(a.shape, lambda i: (0, 0, 0))
    return pl.pallas_call(
        body, name="middle", grid=(S // tm,),
        out_shape=(jax.ShapeDtypeStruct((S, D), F32),
                   jax.ShapeDtypeStruct((S, D_ATT), BF16),
                   jax.ShapeDtypeStruct((S, D), BF16),
                   jax.ShapeDtypeStruct((S, D_POOL), F32),
                   jax.ShapeDtypeStruct((D, D), F32),
                   jax.ShapeDtypeStruct((4, POOL_GROUP, POOL_GROUP), F32),
                   jax.ShapeDtypeStruct((8, D), F32),
                   jax.ShapeDtypeStruct((1, 1), F32)),
        in_specs=[row(D), row(D), row(D_ATT), row(D), row(D_POOL),
                  pl.BlockSpec((POOL_HALO, D_POOL), lambda i: (jnp.maximum(i * halo_blocks - 1, 0), 0)),
                  full2(gate), full3(w_mix), full2(b_mix), full2(pool_scale), full2(w_out), full2(b_out),
                  full2(ln_g), full2(ln_b)],
        out_specs=(row(D), row(D_ATT), row(D), row(D_POOL),
                   pl.BlockSpec((D, D), lambda i: (0, 0)),
                   pl.BlockSpec((4, POOL_GROUP, POOL_GROUP), lambda i: (0, 0, 0)),
                   pl.BlockSpec((8, D), lambda i: (0, 0)),
                   pl.BlockSpec((1, 1), lambda i: (0, 0))),
        compiler_params=_params(dimension_semantics=("arbitrary",)),
    )(x, tgt, att, g, p, p, gate, w_mix, b_mix, pool_scale, w_out, b_out, ln_g, ln_b)


def _tail(dpl, dfk, dfq, f):
    S = dpl.shape[0]
    tm = min(T_ATT, S)
    n_t = S // tm
    halo_blocks = tm // POOL_HALO
    last_halo = S // POOL_HALO - 1

    def body(d_ref, dn_ref, dfk_ref, dfq_ref, f_ref, dp_ref, df_ref, cs_ref, carry):
        s = pl.program_id(0)
        i = n_t - 1 - s

        @pl.when(s == 0)
        def _():
            carry[...] = jnp.zeros_like(carry)
            cs_ref[...] = jnp.zeros_like(cs_ref)

        dc = d_ref[...]
        nxt = jnp.where(s > 0, dn_ref[...], 0.0)
        de = jnp.concatenate([dc, nxt], axis=0)
        n_e = tm + POOL_HALO
        parts = []
        for gi, w in enumerate(POOL_WINDOWS):
            cur = de[:, gi * POOL_GROUP:(gi + 1) * POOL_GROUP] / _window_counts(i * tm, n_e, w)
            span = 1
            while span < w:
                cur = cur + pltpu.roll(cur, n_e - span, 0)
                span *= 2
            parts.append(cur[:tm, :] - dc[:, gi * POOL_GROUP:(gi + 1) * POOL_GROUP])
        dp = jnp.concatenate(parts, axis=1)
        dp_ref[...] = dp.astype(BF16)
        cs_ref[0:1, :] += _colsum(dp)

        r = lax.broadcasted_iota(jnp.int32, (tm, tm), 0)
        c = lax.broadcasted_iota(jnp.int32, (tm, tm), 1)
        tri = (r >= c).astype(F32)
        k_cols = dfk_ref[0]
        rows8 = dfq_ref[0]
        for hp in range(1, N_PAIR):
            k_cols = k_cols + pltpu.roll(dfk_ref[hp], 2 * hp, 1)
            rows8 = rows8 + pltpu.roll(dfq_ref[hp], 2 * hp, 0)
        rows8 = rows8 + k_cols.T[0:8, :]
        dlogf8 = jnp.dot(rows8, tri, preferred_element_type=F32, precision=lax.Precision.HIGHEST) + carry[...]
        first = lax.broadcasted_iota(jnp.int32, (1, tm), 1) == 0
        carry[...] = jnp.sum(jnp.where(first, dlogf8, 0.0), axis=1, keepdims=True)
        dlogf = jnp.concatenate([dlogf8, jnp.zeros((128 - 8, tm), F32)], axis=0).T
        df = dlogf * _sigmoid(-f_ref[...])
        df_ref[...] = df.astype(BF16)
        cs_ref[1:2, 0:128] += _colsum(df)

    rev = lambda w: pl.BlockSpec((tm, w), lambda s: (n_t - 1 - s, 0))
    return pl.pallas_call(
        body, name="tail", grid=(n_t,),
        out_shape=(jax.ShapeDtypeStruct((S, D_POOL), BF16), jax.ShapeDtypeStruct((S, 128), BF16),
                   jax.ShapeDtypeStruct((8, D_POOL), F32)),
        in_specs=[rev(D_POOL),
                  pl.BlockSpec((POOL_HALO, D_POOL),
                               lambda s: (jnp.minimum((n_t - s) * halo_blocks, last_halo), 0)),
                  pl.BlockSpec((N_PAIR, tm, 128), lambda s: (0, n_t - 1 - s, 0)),
                  pl.BlockSpec((N_PAIR, None, 8, tm), lambda s: (0, n_t - 1 - s, 0, 0)),
                  rev(128)],
        out_specs=(rev(D_POOL), rev(128), pl.BlockSpec((8, D_POOL), lambda s: (0, 0))),
        scratch_shapes=[pltpu.VMEM((8, 1), F32)],
        compiler_params=_params(dimension_semantics=("arbitrary",)),
    )(dpl, dpl, dfk, dfq, f)


PIECES = ((O_QKV, D_ATT), (O_QKV + D_ATT, D_ATT), (O_QKV + 2 * D_ATT, D_ATT), (O_F, 128), (O_P, D_POOL), (O_G, D))


def _grad_w_in(u, pieces):
    S = u.shape[0]
    tm = min(TM_GW, S)
    n_t = S // tm

    def body(u_ref, *rest):
        piece_refs, out_ref, acc, sem = rest[:6], rest[6], rest[7], rest[8]
        i = pl.program_id(0)

        @pl.when(i == 0)
        def _():
            acc[...] = jnp.zeros_like(acc)

        last = i == n_t - 1
        copies = [pltpu.make_async_copy(acc.at[:, off:off + w], out_ref.at[:, off:off + w], sem.at[k])
                  for k, (off, w) in enumerate(PIECES)]
        u_t = u_ref[...]
        for k, ((off, w), ref) in enumerate(zip(PIECES, piece_refs)):
            acc[:, off:off + w] += _dot_tn(u_t, ref[...])
            pl.when(last)(copies[k].start)

        @pl.when(last)
        def _():
            for cp in copies:
                cp.wait()

    return pl.pallas_call(
        body, name="grad_w_in", grid=(n_t,),
        out_shape=jax.ShapeDtypeStruct((D, D_PAD), F32),
        in_specs=[pl.BlockSpec((tm, D), lambda i: (i, 0))]
        + [pl.BlockSpec((tm, w), lambda i: (i, 0)) for _, w in PIECES],
        out_specs=pl.BlockSpec(memory_space=pl.ANY),
        scratch_shapes=[pltpu.VMEM((D, D_PAD), F32), pltpu.SemaphoreType.DMA((len(PIECES),))],
        compiler_params=_params(dimension_semantics=("arbitrary",)),
    )(u, *pieces)


def _grad_x(pieces, wt_pad, dh, x, scale):
    S = x.shape[0]
    tm = min(TM_DU, S)

    def body(*refs):
        piece_refs = refs[:6]
        w_ref, dh_ref, x_ref, sc_ref, gx_ref, vec_ref = refs[6:]

        @pl.when(pl.program_id(0) == 0)
        def _():
            vec_ref[...] = jnp.zeros_like(vec_ref)

        du = jnp.zeros((tm, D), F32)
        for (off, w), ref in zip(PIECES, piece_refs):
            du = du + _dot(ref[...], w_ref[off:off + w, :])
        xv = x_ref[...]
        gx_ref[...] = ALPHA * dh_ref[...] + du * (1.0 + sc_ref[...])
        vec_ref[0:1, :] += _colsum(du)
        vec_ref[1:2, :] += _colsum(du * xv)

    row = lambda w: pl.BlockSpec((tm, w), lambda i: (i, 0))
    return pl.pallas_call(
        body, name="grad_x", grid=(S // tm,),
        out_shape=(jax.ShapeDtypeStruct((S, D), F32), jax.ShapeDtypeStruct((8, D), F32)),
        in_specs=[row(w) for _, w in PIECES]
        + [pl.BlockSpec(wt_pad.shape, lambda i: (0, 0)), row(D), row(D), pl.BlockSpec((1, D), lambda i: (0, 0))],
        out_specs=(row(D), pl.BlockSpec((8, D), lambda i: (0, 0))),
        compiler_params=_params(dimension_semantics=("arbitrary",)),
    )(*pieces, wt_pad, dh, x, scale)


def _adamw_math(w, g, m, v):
    m = ADAM_B1 * m + (1.0 - ADAM_B1) * g
    v = ADAM_B2 * v + (1.0 - ADAM_B2) * (g * g)
    m_hat = m / (1.0 - ADAM_B1 ** ADAM_STEP)
    v_hat = v / (1.0 - ADAM_B2 ** ADAM_STEP)
    delta = -ADAM_LR * (m_hat / (jnp.sqrt(v_hat) + ADAM_EPS) + ADAM_WD * w)
    return delta, m, v


def _adamw(groups, n_steps):
    n = len(groups)

    def body(*refs):
        ins, outs = refs[:4 * n], refs[4 * n:]
        for t in range(n):
            w, g, m, v = (r[...] for r in ins[4 * t:4 * t + 4])
            d, m2, v2 = _adamw_math(w, g, m, v)
            outs[4 * t][...] = d
            outs[4 * t + 1][...] = m2
            outs[4 * t + 2][...] = v2
            outs[4 * t + 3][...] = g

    in_specs, out_specs, out_shape, args = [], [], [], []
    for (w, g, m, v) in groups:
        rest = w.shape[1:]
        spec = pl.BlockSpec((w.shape[0] // n_steps,) + rest, lambda i, nd=len(rest): (i,) + (0,) * nd)
        in_specs += [spec] * 4
        out_specs += [spec] * 4
        out_shape += [jax.ShapeDtypeStruct(w.shape, F32)] * 4
        args += [w, g, m, v]
    return pl.pallas_call(
        body, name="adamw_%d_%d" % (n, n_steps), grid=(n_steps,),
        out_shape=tuple(out_shape), in_specs=in_specs, out_specs=tuple(out_specs),
        compiler_params=_params(dimension_semantics=("arbitrary",)),
    )(*args)


def _adamw_small(small_sum, g_b_ada, params):
    n = len(params)

    def body(gs_ref, gba_ref, *refs):
        ins, outs = refs[:3 * n], refs[3 * n:]
        for t, (name, w0, _, _) in enumerate(params):
            w_ref, m_ref, v_ref = ins[3 * t:3 * t + 3]
            first = SMALL_SEGS[name][0] if name in SMALL_SEGS else None
            if w0.shape[0] > 1:
                pieces = [((slice(None), slice(None)), gs_ref[first:first + w0.shape[0], :])]
            else:
                pieces = []
                for r in range(-(-w0.shape[1] // 128)):
                    lanes = slice(128 * r, min(128 * r + 128, w0.shape[1]))
                    g = gba_ref[0:1, lanes] if first is None else gs_ref[first + r:first + r + 1, 0:lanes.stop - lanes.start]
                    pieces.append(((slice(0, 1), lanes), g))
            for where, g in pieces:
                d, m2, v2 = _adamw_math(w_ref[where], g, m_ref[where], v_ref[where])
                for ref, val in zip(outs[4 * t:4 * t + 4], (g, d, m2, v2)):
                    ref[where] = val

    vm = pl.BlockSpec(memory_space=pltpu.VMEM)
    args = [small_sum, g_b_ada]
    out_shape = []
    for _, w, m, v in params:
        args += [w, m, v]
        out_shape += [jax.ShapeDtypeStruct(w.shape, F32)] * 4
    return pl.pallas_call(
        body, name="adamw_small",
        out_shape=tuple(out_shape), in_specs=[vm] * len(args), out_specs=(vm,) * len(out_shape),
        compiler_params=_params(),
    )(*args)


def _pack_small(parts):
    rows = []
    used = 0
    for name, (first, n_rows) in SMALL_SEGS.items():
        if first > used:
            rows.append(jnp.zeros((first - used, 128), F32))
        flat = parts[name].reshape(-1)
        flat = jnp.pad(flat, (0, n_rows * 128 - flat.shape[0]))
        rows.append(flat.reshape(n_rows, 128))
        used = first + n_rows
    rows.append(jnp.zeros((SMALL_ROWS - used, 128), F32))
    return jnp.concatenate(rows, axis=0)


def _pad_in(v):
    r = v.shape[0]
    z = jnp.zeros((r, O_P - O_F - N_HEADS), v.dtype)
    return jnp.concatenate([v[:, :3 * D_ATT + N_HEADS], z, v[:, 3 * D_ATT + N_HEADS:]], axis=1)


def _unpad_in(v):
    return jnp.concatenate([v[:, :O_F + N_HEADS], v[:, O_P:]], axis=1)


def _shards_in(v):
    gap = O_P - (O_F + N_HEADS)
    parts = []
    for a in range(N_CHIPS):
        lo, hi = a * SHARD_IN, (a + 1) * SHARD_IN
        cut = O_F + N_HEADS
        if hi <= cut:
            parts.append(v[:, lo:hi])
        elif lo >= cut:
            parts.append(v[:, lo + gap:hi + gap])
        else:
            parts.append(jnp.concatenate([v[:, lo:cut], v[:, cut + gap:hi + gap]], axis=1))
    return jnp.stack(parts, axis=0)


def kernel(x, c, w_ada, b_ada, w_in, b_in, w_pool_mix, b_pool_mix, pool_scale, w_out, b_out, ln_g, ln_b, loss_target, m_w_ada, m_b_ada, m_w_in, m_b_in, m_w_pool_mix, m_b_pool_mix, m_pool_scale, m_w_out, m_b_out, m_ln_g, m_ln_b, v_w_ada, v_b_ada, v_w_in, v_b_in, v_w_pool_mix, v_b_pool_mix, v_pool_scale, v_w_out, v_b_out, v_ln_g, v_ln_b):
    S = x.shape[1]
    T = min(T_ATT, S)
    n_t = S // T
    x2 = x[0]
    tgt = loss_target[0]
    q_scale = jnp.concatenate([jnp.full((1, D_ATT), Q_SCALE, F32), jnp.ones((1, D_PAD - D_ATT), F32)], axis=1)

    to_cols = lambda a: jnp.transpose(a, (2, 0, 1))
    from_cols = lambda a: jnp.transpose(a, (1, 2, 0))
    c_all, shift, scale, gate, wt_pad = _gather_and_ada(
        c, w_ada[0], b_ada.reshape(4, 1, SHARD_ADA), to_cols(w_in))
    b_pad = _pad_in(b_in) * q_scale

    u, qkv, f, p, g, w_out_all = _in_proj(x2, shift, scale, wt_pad, b_pad, w_out[0])
    w_out_full = w_out_all.reshape(D, D)
    big_f = _forget_cumsum(f)
    att, lse = _attention_fwd(qkv, big_f)

    dh, datt, dg, dpl, gw_out, gw_mix, vec, loss_part = _middle(
        x2, tgt, att, g, p, gate, w_pool_mix[0], b_pool_mix.reshape(1, D_POOL), pool_scale, w_out_full, b_out, ln_g, ln_b)
    dq, dk, dv, cs_att, dfk, dfq, g_w_out = _attention_bwd(
        qkv, datt, att, lse, big_f, gw_out.reshape(N_CHIPS, SHARD_OUT, D))
    dp, df, cs_tail = _tail(dpl, dfk, dfq, f)
    pieces = (dq, dk, dv, df, dp, dg)
    gw_pad = _grad_w_in(u, pieces)
    grad_x, vec_x = _grad_x(pieces, wt_pad, dh, x2, scale)

    cs_qkv = jnp.transpose(cs_att.reshape(N_PAIR, 3, 128), (1, 0, 2)).reshape(1, 3 * D_ATT)
    gb_pad = jnp.concatenate([cs_qkv, cs_tail[1:2, 0:128], cs_tail[0:1, :], vec[4:5, :]], axis=1) * q_scale
    dada = jnp.concatenate([vec_x[0:1, :], vec_x[1:2, :], vec[2:3, :]], axis=1)
    small = _pack_small({
        "b_in": _unpad_in(gb_pad), "w_pool_mix": gw_mix, "b_pool_mix": vec[6:7, :D_POOL],
        "pool_scale": vec[5:6, :D_POOL], "b_out": vec[3:4, :], "ln_g": vec[0:1, :], "ln_b": vec[1:2, :],
        "loss": loss_part})

    g_w_in, small_sum, g_w_ada, g_b_ada, loss = _reduce_all(
        gw_pad, _shards_in(q_scale), small, dada, c_all)

    big = _adamw([(w_ada[0], g_w_ada, m_w_ada[0], v_w_ada[0]),
                  (w_out[0], g_w_out, m_w_out[0], v_w_out[0])], 2)
    big_in = _adamw([(to_cols(w_in), g_w_in, to_cols(m_w_in), to_cols(v_w_in))], 2)
    tiles = lambda a: a.reshape(4 * POOL_GROUP, POOL_GROUP)
    flat = lambda a: a.reshape(1, D_POOL)
    small_params = [("b_ada", b_ada, m_b_ada, v_b_ada), ("b_in", b_in, m_b_in, v_b_in),
                    ("w_pool_mix", tiles(w_pool_mix), tiles(m_w_pool_mix), tiles(v_w_pool_mix)),
                    ("b_pool_mix", flat(b_pool_mix), flat(m_b_pool_mix), flat(v_b_pool_mix)),
                    ("pool_scale", pool_scale, m_pool_scale, v_pool_scale), ("b_out", b_out, m_b_out, v_b_out),
                    ("ln_g", ln_g, m_ln_g, v_ln_g), ("ln_b", ln_b, m_ln_b, v_ln_b)]
    sm = _adamw_small(small_sum, g_b_ada, small_params)
    sm_idx = {p[0]: n for n, p in enumerate(small_params)}
    shapes = {"w_pool_mix": (1, 4, POOL_GROUP, POOL_GROUP), "b_pool_mix": (1, 4, POOL_GROUP)}

    names = ["w_ada", "b_ada", "w_in", "b_in", "w_pool_mix", "b_pool_mix", "pool_scale", "w_out", "b_out",
             "ln_g", "ln_b"]
    big_idx = {"w_ada": 0, "w_out": 1}

    def leaf(kind, name):
        if name == "w_in":
            return from_cols(big_in[(kind - 1) % 4])
        if name in big_idx:
            return big[4 * big_idx[name] + (kind - 1) % 4][None]
        val = sm[4 * sm_idx[name] + kind]
        return val.reshape(shapes[name]) if name in shapes else val

    outs = [loss.reshape(()), grad_x[None]]
    for kind in range(4):
        outs += [leaf(kind, n) for n in names]
    return tuple(outs)
```

```python
import functools

import numpy as np
import jax
import jax.numpy as jnp
from jax import lax
from jax.experimental import pallas as pl
from jax.experimental.pallas import tpu as pltpu

F32 = jnp.float32
BF16 = jnp.bfloat16
MESH = pl.DeviceIdType.MESH

D = 1024
D_ATT = 512
D_POOL = 512
N_HEADS = 8
HEAD_DIM = 64
N_PAIR = N_HEADS // 2
POOL_WINDOWS = (2, 4, 8, 16)
POOL_GROUP = 128
POOL_HALO = 16
LN_EPS = 1e-5
ALPHA = 2.0 ** 0.25
D_IN = 3 * D_ATT + N_HEADS + D_POOL + D_ATT + D_POOL
N_CHIPS = 4
SHARD_IN = D_IN // N_CHIPS
SHARD_ADA = 3 * D // N_CHIPS
SHARD_OUT = D // N_CHIPS

O_QKV, O_F, O_P, O_G, D_PAD = 0, 1536, 1664, 2176, 3200
Q_SCALE = HEAD_DIM ** -0.5

ADAM_LR, ADAM_B1, ADAM_B2, ADAM_EPS, ADAM_WD, ADAM_STEP = 0.001, 0.9, 0.999, 1e-08, 0.01, 10

NEG = -1e30

VMEM_LIMIT = 56 * 1024 * 1024

TM_PROJ = 512
T_ATT = 512
TM_MID = 512
TM_GW = 1024
TM_DU = 512

REL7 = [(0, 0, 1), (0, 1, 0), (0, 1, 1), (1, 0, 0), (1, 0, 1), (1, 1, 0), (1, 1, 1)]
REL3 = [(0, 1), (1, 0), (1, 1)]

SMALL_SEGS = {}
_row = 0
for _name, _n in (("b_in", D_IN), ("w_pool_mix", 65536), ("b_pool_mix", 512), ("pool_scale", 512),
                  ("b_out", 1024), ("ln_g", 1024), ("ln_b", 1024), ("loss", 1)):
    _rows = -(-_n // 1024) * 8
    SMALL_SEGS[_name] = (_row, _rows)
    _row += _rows
SMALL_ROWS = -(-_row // 16) * 16


def _params(**kw):
    return pltpu.CompilerParams(vmem_limit_bytes=VMEM_LIMIT, **kw)


def _flip(v, d):
    return v if d == 0 else 1 - v


def _dot(a, b):
    return jnp.dot(a, b, preferred_element_type=F32)


def _dot_nt(a, b):
    return lax.dot_general(a, b, (((1,), (1,)), ((), ())), preferred_element_type=F32)


def _dot_tn(a, b):
    return lax.dot_general(a, b, (((0,), (0,)), ((), ())), preferred_element_type=F32)


def _sigmoid(v):
    return 1.0 / (1.0 + jnp.exp(-v))


def _colsum(v):
    return jnp.sum(v, axis=0, keepdims=True)


def _gather_stages(pos, src_ref, dst_ref, half, own_sem, s_sem, r_sem, fs_sem, fr_sem):
    x, y, cc, chip, sib = pos
    own = pltpu.make_async_copy(src_ref, dst_ref.at[chip], own_sem)
    first, landed, others = [], [], []
    for k, (dx, dy) in enumerate(REL3):
        px, py = _flip(x, dx), _flip(y, dy)
        first.append(pltpu.make_async_remote_copy(
            src_ref=src_ref.at[half(cc)], dst_ref=dst_ref.at[(chip,) + half(cc)],
            send_sem=s_sem.at[k], recv_sem=r_sem.at[k], device_id=(px, py, cc), device_id_type=MESH))
        landed.append(dst_ref.at[(2 * px + py,) + half(cc)])
        others.append(dst_ref.at[(2 * px + py,) + half(1 - cc)])
    passed = [pltpu.make_async_remote_copy(src_ref=landed[k], dst_ref=landed[k], send_sem=fs_sem.at[k],
                                           recv_sem=fr_sem.at[k], device_id=sib, device_id_type=MESH)
              for k in range(3)]

    def start(finish_src=None):
        for cp in first:
            cp.start()
        if finish_src is not None:
            finish_src()
        own.start()

    def forward():
        for k in range(3):
            pltpu.make_async_remote_copy(src_ref=landed[k], dst_ref=landed[k], send_sem=s_sem.at[k],
                                         recv_sem=r_sem.at[k], device_id=sib, device_id_type=MESH).wait_recv()
            passed[k].start()

    def finish():
        for k in range(3):
            pltpu.make_async_remote_copy(src_ref=others[k], dst_ref=others[k], send_sem=fs_sem.at[k],
                                         recv_sem=fr_sem.at[k], device_id=sib, device_id_type=MESH).wait_recv()
        for cp in first + passed:
            cp.wait_send()
        own.wait()

    return start, forward, finish


def _gather_scratch():
    return [pltpu.SemaphoreType.DMA, pltpu.SemaphoreType.DMA((3,)), pltpu.SemaphoreType.DMA((3,)),
            pltpu.SemaphoreType.DMA((3,)), pltpu.SemaphoreType.DMA((3,))]


def _gather_and_ada(c, w_ada, b_ada4, w_in_sh):
    def body(c_ref, w_ref, b_ref, win_ref, call_ref, shift_ref, scale_ref, gate_ref, wt_pad_ref,
             win_all, win_bf, ada_ref, cslab, sbuf, rbuf, cs_sem, cr_sem, as_sem, ar_sem, *gather_sems):
        x, y, cc = lax.axis_index("x"), lax.axis_index("y"), lax.axis_index("c")
        me = 4 * x + 2 * y + cc
        chip = 2 * x + y
        lane_half = lambda which: (slice(None), pl.ds(pl.multiple_of(which * (D // 2), D // 2), D // 2))
        def round_half(which):
            for h in range(2):
                @pl.when(which == h)
                def _():
                    lanes = slice(h * (D // 2), (h + 1) * (D // 2))
                    win_bf[:, lanes] = win_ref[:, 0, lanes].astype(BF16)

        start, forward, finish = _gather_stages((x, y, cc, chip, (x, y, 1 - cc)), win_bf, win_all, lane_half,
                                                *gather_sems)
        round_half(cc)
        start(lambda: round_half(1 - cc))

        cslab[...] = jnp.broadcast_to(c_ref[...], (8, D))
        call_ref[me] = cslab[...]
        gathers = []
        for k, (dx, dy, dc) in enumerate(REL7):
            cp = pltpu.make_async_remote_copy(
                src_ref=cslab, dst_ref=call_ref.at[me], send_sem=cs_sem.at[k], recv_sem=cr_sem.at[k],
                device_id=(_flip(x, dx), _flip(y, dy), _flip(cc, dc)), device_id_type=MESH)
            cp.start()
            gathers.append(cp)
        for cp in gathers:
            cp.wait()
        slab_row = lax.broadcasted_iota(jnp.int32, (8, 1), 0)
        mat = jnp.zeros((8, D), F32)
        for r in range(8):
            mat = jnp.where(slab_row == r, call_ref[r], mat)
        act = (mat * _sigmoid(mat)).astype(BF16)
        part = _dot(act, w_ref[...].astype(BF16))
        sends = []
        for k, (dx, dy) in enumerate(REL3):
            px, py = _flip(x, dx), _flip(y, dy)
            r = 4 * px + 2 * py + cc
            piece = _colsum(jnp.where(slab_row == r, part, 0.0))
            sbuf[k] = jnp.broadcast_to(piece, (8, SHARD_ADA))
            cp = pltpu.make_async_remote_copy(
                src_ref=sbuf.at[k], dst_ref=rbuf.at[k], send_sem=as_sem.at[k], recv_sem=ar_sem.at[k],
                device_id=(px, py, cc), device_id_type=MESH)
            cp.start()
            sends.append(cp)
        own_piece = _colsum(jnp.where(slab_row == me, part, 0.0))
        ada_ref[chip] = jnp.broadcast_to(own_piece, (8, SHARD_ADA)) + b_ref[chip]
        for k, (dx, dy) in enumerate(REL3):
            sends[k].wait()
            a = 2 * _flip(x, dx) + _flip(y, dy)
            ada_ref[a] = rbuf[k] + b_ref[a]
        ada = jnp.concatenate([ada_ref[a][0:1, :] for a in range(N_CHIPS)], axis=1)
        shift_ref[...] = ada[:, 0:D]
        scale_ref[...] = ada[:, D:2 * D]
        gate_ref[...] = ada[:, 2 * D:3 * D]

        forward()
        finish()
        n_real = 3 * D_ATT + N_HEADS
        for a in range(N_CHIPS):
            lo, hi = a * SHARD_IN, (a + 1) * SHARD_IN
            for s0, s1 in ((lo, min(hi, D_ATT)), (max(lo, D_ATT), min(hi, n_real)), (max(lo, n_real), hi)):
                if s0 < s1:
                    rows = win_all[a, s0 - lo:s1 - lo, :]
                    if s1 <= D_ATT:
                        rows = rows * jnp.asarray(Q_SCALE, BF16)
                    shift = O_P - n_real if s0 >= n_real else 0
                    wt_pad_ref[s0 + shift:s1 + shift, :] = rows
        wt_pad_ref[n_real:O_P, :] = jnp.zeros((O_P - n_real, D), BF16)

    vm = pl.BlockSpec(memory_space=pltpu.VMEM)
    return pl.pallas_call(
        body, name="gather_and_ada",
        out_shape=(jax.ShapeDtypeStruct((8, 8, D), F32),) + (jax.ShapeDtypeStruct((1, D), F32),) * 3
        + (jax.ShapeDtypeStruct((D_PAD, D), BF16),),
        in_specs=[vm] * 4, out_specs=(vm,) * 5,
        scratch_shapes=[pltpu.VMEM((N_CHIPS, SHARD_IN, D), BF16), pltpu.VMEM((SHARD_IN, D), BF16),
                        pltpu.VMEM((N_CHIPS, 8, SHARD_ADA), F32), pltpu.VMEM((8, D), F32), pltpu.VMEM((3, 8, SHARD_ADA), F32),
                        pltpu.VMEM((3, 8, SHARD_ADA), F32),
                        pltpu.SemaphoreType.DMA((7,)), pltpu.SemaphoreType.DMA((7,)),
                        pltpu.SemaphoreType.DMA((3,)), pltpu.SemaphoreType.DMA((3,))] + _gather_scratch(),
        compiler_params=_params(),
    )(c, w_ada, b_ada4, w_in_sh)


def _shard_cols():
    cut, gap = O_F + N_HEADS, O_P - (O_F + N_HEADS)
    out = []
    for a in range(N_CHIPS):
        lo, hi = a * SHARD_IN, (a + 1) * SHARD_IN
        out.append(([(lo, min(hi, cut))] if lo < cut else []) + ([(max(lo, cut) + gap, hi + gap)] if hi > cut else []))
    return out


def _scatter_stages(pos, g_ref, sc_ref, out_ref, sib_buf, send_buf, ici_buf, sem1, sem2s, sem2r, sem3, part=(0, 1),
                    cols=None, own_buf=None, staged=None):
    x, y, cc, chip, sib = pos
    q, n_parts = part
    RH = (g_ref.shape[1] if cols is None else g_ref.shape[0]) // 2 // n_parts
    mine = pl.ds(pl.multiple_of((cc * n_parts + q) * RH, RH), RH)
    theirs = pl.ds(pl.multiple_of(((1 - cc) * n_parts + q) * RH, RH), RH)
    cp1 = pltpu.make_async_remote_copy(
        src_ref=g_ref.at[:, theirs, :] if cols is None else g_ref.at[theirs, :], dst_ref=sib_buf,
        send_sem=sem1.at[0], recv_sem=sem1.at[1], device_id=sib, device_id_type=MESH)
    sends = []
    for k, (dx, dy) in enumerate(REL3):
        px, py = _flip(x, dx), _flip(y, dy)
        sends.append(pltpu.make_async_remote_copy(
            src_ref=send_buf.at[2 * px + py], dst_ref=ici_buf.at[chip],
            send_sem=sem2s.at[k], recv_sem=sem2r.at[k], device_id=(px, py, cc), device_id_type=MESH))
    cp3 = pltpu.make_async_remote_copy(
        src_ref=out_ref.at[mine, :], dst_ref=out_ref.at[mine, :], send_sem=sem3.at[0], recv_sem=sem3.at[1],
        device_id=sib, device_id_type=MESH)

    if staged is not None:
        stage = pltpu.make_async_copy(g_ref.at[mine, :], staged[0], staged[1])

    def start1():
        cp1.start()
        if staged is not None:
            stage.start()

    def finish1():
        cp1.wait()
        if cols is None:
            for a in range(N_CHIPS):
                both = g_ref[a, mine, :] + sib_buf[a]
                sib_buf[a] = both
                send_buf[a] = both.astype(BF16)
        else:
            if staged is not None:
                stage.wait()
            both = (g_ref[mine, :] if staged is None else staged[0][...]) + sib_buf[...]
            for a, pieces in enumerate(cols):
                at = 0
                for lo, hi in pieces:
                    own_buf[a, :, at:at + hi - lo] = both[:, lo:hi]
                    send_buf[a, :, at:at + hi - lo] = both[:, lo:hi].astype(BF16)
                    at += hi - lo

    def start2():
        for cp in sends:
            cp.start()
        ici_buf[chip] = send_buf[chip]

    def finish2():
        for cp in sends:
            cp.wait()
        own = (sib_buf if cols is None else own_buf)[chip]
        parts = [jnp.where(chip == a, own, ici_buf[a].astype(F32)) for a in range(N_CHIPS)]
        total = (parts[0] + parts[1]) + (parts[2] + parts[3])
        out_ref[mine, 0:own.shape[1]] = total if sc_ref is None else total * sc_ref[chip]

    return [(start1, finish1), (start2, finish2), (cp3.start, cp3.wait)]


def _all_reduce_stages(pos, g_ref, out_ref, sib_buf, ici_buf, sem1, sem2s, sem2r, sem3):
    x, y, cc, chip, sib = pos
    RH = g_ref.shape[0] // 2
    mine = pl.ds(pl.multiple_of(cc * RH, 8), RH)
    theirs = pl.ds(pl.multiple_of((1 - cc) * RH, 8), RH)
    cp1 = pltpu.make_async_remote_copy(
        src_ref=g_ref.at[theirs, :], dst_ref=sib_buf, send_sem=sem1.at[0], recv_sem=sem1.at[1],
        device_id=sib, device_id_type=MESH)
    sends = []
    for k, (dx, dy) in enumerate(REL3):
        px, py = _flip(x, dx), _flip(y, dy)
        sends.append(pltpu.make_async_remote_copy(
            src_ref=sib_buf, dst_ref=ici_buf.at[chip],
            send_sem=sem2s.at[k], recv_sem=sem2r.at[k], device_id=(px, py, cc), device_id_type=MESH))
    cp3 = pltpu.make_async_remote_copy(
        src_ref=out_ref.at[mine, :], dst_ref=out_ref.at[mine, :], send_sem=sem3.at[0], recv_sem=sem3.at[1],
        device_id=sib, device_id_type=MESH)

    def finish1():
        cp1.wait()
        sib_buf[...] = g_ref[mine, :] + sib_buf[...]

    def start2():
        for cp in sends:
            cp.start()
        ici_buf[chip] = sib_buf[...]

    def finish2():
        for cp in sends:
            cp.wait()
        out_ref[mine, :] = (ici_buf[0] + ici_buf[1]) + (ici_buf[2] + ici_buf[3])

    return [(cp1.start, finish1), (start2, finish2), (cp3.start, cp3.wait)]


def _stage_sems():
    return [pltpu.SemaphoreType.DMA((2,)), pltpu.SemaphoreType.DMA((3,)),
            pltpu.SemaphoreType.DMA((3,)), pltpu.SemaphoreType.DMA((2,))]


def _scatter_scratch(r, c):
    return [pltpu.VMEM((N_CHIPS, r // 2, c), F32), pltpu.VMEM((N_CHIPS, r // 2, c), BF16),
            pltpu.VMEM((N_CHIPS, r // 2, c), BF16)] + _stage_sems()


def _reduce_all(gw_pad, sc_in, small, dada, c_all):
    R = small.shape[0]
    W = dada.shape[1]
    r_in, p_in = gw_pad.shape
    c_in = SHARD_IN
    chunk = r_in // 4

    def chunk_scratch():
        return ([pltpu.VMEM((chunk, p_in), F32), pltpu.VMEM((N_CHIPS, chunk, c_in), BF16),
                 pltpu.VMEM((N_CHIPS, chunk, c_in), BF16)] + _stage_sems()
                + [pltpu.VMEM((N_CHIPS, chunk, c_in), F32), pltpu.VMEM((chunk, p_in), F32), pltpu.SemaphoreType.DMA])

    n_in = len(chunk_scratch())

    c_wide = -(-c_in // 128) * 128

    def body(gin_ref, scin_ref, sm_ref, d_ref, c_ref, ocols_ref, osm_ref, gwa_ref, gba_ref, loss_ref, oin_ref,
             dall_ref, *scratch):
        x, y, cc = lax.axis_index("x"), lax.axis_index("y"), lax.axis_index("c")
        me = 4 * x + 2 * y + cc
        chip = 2 * x + y
        pos = (x, y, cc, chip, (x, y, 1 - cc))
        oin_ref[:, c_in:c_wide] = jnp.zeros((r_in, c_wide - c_in), F32)
        dslab, ds_sem, dr_sem = scratch[0:3]
        a_bufs, b_bufs, sm_bufs = scratch[3:3 + n_in], scratch[3 + n_in:3 + 2 * n_in], scratch[3 + 2 * n_in:]
        dslab[...] = jnp.broadcast_to(d_ref[...], (8, W))
        dall_ref[me] = dslab[...]
        gathers = []
        for k, (dx, dy, dc) in enumerate(REL7):
            cp = pltpu.make_async_remote_copy(
                src_ref=dslab, dst_ref=dall_ref.at[me], send_sem=ds_sem.at[k], recv_sem=dr_sem.at[k],
                device_id=(_flip(x, dx), _flip(y, dy), _flip(cc, dc)), device_id_type=MESH)
            cp.start()
            gathers.append(cp)
        cols = _shard_cols()
        first = _scatter_stages(pos, gin_ref, scin_ref, oin_ref, *a_bufs[:-3], part=(0, 2), cols=cols,
                                own_buf=a_bufs[-3], staged=a_bufs[-2:])
        second = _scatter_stages(pos, gin_ref, scin_ref, oin_ref, *b_bufs[:-3], part=(1, 2), cols=cols,
                                 own_buf=b_bufs[-3], staged=b_bufs[-2:])
        little = _all_reduce_stages(pos, sm_ref, osm_ref, *sm_bufs)
        for plan in (first, second, little):
            plan[0][0]()
        first[0][1]()
        first[1][0]()
        little[0][1]()
        little[1][0]()
        second[0][1]()
        second[1][0]()
        for cp in gathers:
            cp.wait()
        slab_row = lax.broadcasted_iota(jnp.int32, (8, 1), 0)
        cm = jnp.zeros((8, D), F32)
        dm = jnp.zeros((8, W), F32)
        for r in range(8):
            cm = jnp.where(slab_row == r, c_ref[r], cm)
            dm = jnp.where(slab_row == r, dall_ref[r], dm)
        act = cm * _sigmoid(cm)
        dcol = dm[:, 0:SHARD_ADA]
        for a in range(1, N_CHIPS):
            dcol = jnp.where(chip == a, dm[:, a * SHARD_ADA:(a + 1) * SHARD_ADA], dcol)
        lhs = jnp.concatenate([act, jnp.zeros((8, D), F32)], axis=0).astype(BF16)
        rhs = jnp.concatenate([dcol, jnp.zeros((8, SHARD_ADA), F32)], axis=0).astype(BF16)
        gwa_ref[...] = _dot_tn(lhs, rhs)
        gba_ref[...] = _colsum(dm)
        first[1][1]()
        first[2][0]()
        second[1][1]()
        second[2][0]()
        little[1][1]()
        little[2][0]()
        for plan in (first, second, little):
            plan[2][1]()
        loss_row = SMALL_SEGS["loss"][0]
        loss_ref[...] = osm_ref[loss_row:loss_row + 1, 0:1]
        ocols_ref[...] = oin_ref[...].T[0:c_in, :][:, None, :]

    scratch = [pltpu.VMEM((r_in, c_wide), F32), pltpu.VMEM((8, 8, W), F32),
               pltpu.VMEM((8, W), F32), pltpu.SemaphoreType.DMA((7,)), pltpu.SemaphoreType.DMA((7,))]
    scratch += chunk_scratch() + chunk_scratch()
    scratch += [pltpu.VMEM((R // 2, 128), F32), pltpu.VMEM((N_CHIPS, R // 2, 128), F32)] + _stage_sems()
    vm = pl.BlockSpec(memory_space=pltpu.VMEM)
    return pl.pallas_call(
        body, name="reduce_all",
        out_shape=(jax.ShapeDtypeStruct((c_in, 1, r_in), F32), jax.ShapeDtypeStruct((R, 128), F32),
                   jax.ShapeDtypeStruct((D, SHARD_ADA), F32), jax.ShapeDtypeStruct((1, W), F32),
                   jax.ShapeDtypeStruct((1, 1), F32)),
        in_specs=[pl.BlockSpec(memory_space=pl.ANY)] + [vm] * 4, out_specs=(vm,) * 5,
        scratch_shapes=scratch,
        compiler_params=_params(),
    )(gw_pad, sc_in, small, dada, c_all)


def _in_proj(x, shift, scale, wt_pad, b_pad, w_out_sh):
    S = x.shape[0]
    tm = min(TM_PROJ, S)
    n_steps = S // tm
    assert n_steps >= 3

    def body(x_ref, sh_ref, sc_ref, w_ref, b_ref, wo_ref, u_ref, qkv_ref, f_ref, p_ref, g_ref, wo_all,
             wo_buf, wo_bf, *gather_sems):
        i = pl.program_id(0)
        xx, yy, cc = lax.axis_index("x"), lax.axis_index("y"), lax.axis_index("c")
        row_half = lambda which: (pl.ds(pl.multiple_of(which * (SHARD_OUT // 2), SHARD_OUT // 2), SHARD_OUT // 2),
                                  slice(None))
        start, forward, finish = _gather_stages((xx, yy, cc, 2 * xx + yy, (xx, yy, 1 - cc)), wo_bf, wo_buf,
                                                row_half, *gather_sems)

        @pl.when(i == 0)
        def _():
            wo_bf[...] = wo_ref[...].astype(BF16)
            start()

        pl.when(i == n_steps // 2)(forward)

        @pl.when(i == n_steps - 1)
        def _():
            finish()
            wo_all[...] = wo_buf[...]

        u = (x_ref[...] * (1.0 + sc_ref[...]) + sh_ref[...]).astype(BF16)
        u_ref[...] = u
        qkv_ref[...] = (_dot_nt(u, w_ref[O_QKV:O_F, :]) + b_ref[:, O_QKV:O_F]).astype(BF16)
        f_ref[...] = _dot_nt(u, w_ref[O_F:O_P, :]) + b_ref[:, O_F:O_P]
        p_ref[...] = _dot_nt(u, w_ref[O_P:O_G, :]) + b_ref[:, O_P:O_G]
        g_ref[...] = _dot_nt(u, w_ref[O_G:D_PAD, :]) + b_ref[:, O_G:D_PAD]

    row = lambda w: pl.BlockSpec((tm, w), lambda i: (i, 0))
    full = lambda a: pl.BlockSpec(a.shape, lambda i: (0, 0))
    vm = pl.BlockSpec(memory_space=pltpu.VMEM)
    return pl.pallas_call(
        body, name="in_proj", grid=(n_steps,),
        out_shape=(jax.ShapeDtypeStruct((S, D), BF16), jax.ShapeDtypeStruct((S, 3 * D_ATT), BF16),
                   jax.ShapeDtypeStruct((S, 128), F32), jax.ShapeDtypeStruct((S, D_POOL), F32),
                   jax.ShapeDtypeStruct((S, D), F32), jax.ShapeDtypeStruct((N_CHIPS,) + w_out_sh.shape, BF16)),
        in_specs=[row(D), full(shift), full(scale), full(wt_pad), full(b_pad), vm],
        out_specs=(row(D), row(3 * D_ATT), row(128), row(D_POOL), row(D), vm),
        scratch_shapes=[pltpu.VMEM((N_CHIPS,) + w_out_sh.shape, BF16), pltpu.VMEM(w_out_sh.shape, BF16)]
        + _gather_scratch(),
        compiler_params=_params(dimension_semantics=("arbitrary",)),
    )(x, shift, scale, wt_pad, b_pad, w_out_sh)


def _forget_cumsum(f):
    S = f.shape[0]
    tm = min(T_ATT, S)

    def body(f_ref, out_ref, carry):
        @pl.when(pl.program_id(0) == 0)
        def _():
            carry[...] = jnp.zeros_like(carry)
        v = f_ref[...]
        logf = jnp.minimum(v, 0.0) - jnp.log(1.0 + jnp.exp(-jnp.abs(v)))
        r = lax.broadcasted_iota(jnp.int32, (tm, tm), 0)
        c = lax.broadcasted_iota(jnp.int32, (tm, tm), 1)
        tri = (r <= c).astype(F32)
        rows8 = logf.T[0:8, :]
        cum8 = jnp.dot(rows8, tri, preferred_element_type=F32, precision=lax.Precision.HIGHEST) + carry[...]
        out_ref[...] = jnp.concatenate([cum8, jnp.zeros((128 - 8, tm), F32)], axis=0).T
        last = lax.broadcasted_iota(jnp.int32, (1, tm), 1) == tm - 1
        carry[...] = jnp.sum(jnp.where(last, cum8, 0.0), axis=1, keepdims=True)

    return pl.pallas_call(
        body, name="forget_cumsum", grid=(S // tm,),
        out_shape=jax.ShapeDtypeStruct((S, 128), F32),
        in_specs=[pl.BlockSpec((tm, 128), lambda i: (i, 0))],
        out_specs=pl.BlockSpec((tm, 128), lambda i: (i, 0)),
        scratch_shapes=[pltpu.VMEM((8, 1), F32)],
        compiler_params=_params(dimension_semantics=("arbitrary",)),
    )(f)


def _split3(v):
    hi = v.astype(BF16)
    rest = v - hi.astype(F32)
    mid = rest.astype(BF16)
    lo = (rest - mid.astype(F32)).astype(BF16)
    return hi, mid, lo


def _attention_fwd(qkv, big_f):
    S = qkv.shape[0]
    T = min(T_ATT, S)
    n_t = S // T

    def body(q_ref, k_ref, v_ref, f_ref, o_ref, lse_ref, kaug_sc, vt_sc, m_sc, l_sc, acc_sc):
        hp = pl.program_id(0)
        i = pl.program_id(1)
        lane = lax.broadcasted_iota(jnp.int32, (1, 128), 1)
        sub = lax.broadcasted_iota(jnp.int32, (128, 1), 0)
        head_sel = (lane < HEAD_DIM, lane >= HEAD_DIM)
        head_sel_t = (sub < HEAD_DIM, sub >= HEAD_DIM)
        spare = (HEAD_DIM, 0)
        zero = jnp.zeros((), BF16)

        @pl.when(i == 0)
        def _():
            def prep(jt, carry):
                rows = pl.ds(pl.multiple_of(jt * T, T), T)
                k = k_ref[rows, :]
                ft = f_ref[rows, :]
                vt = v_ref[rows, :].astype(F32).T
                for h in range(2):
                    fh = jnp.sum(jnp.where(lane == 2 * hp + h, ft, 0.0), axis=1, keepdims=True)
                    hi, mid, lo = _split3(-fh)
                    b = spare[h]
                    bias = jnp.where(lane == b, hi, jnp.where(lane == b + 1, mid, jnp.where(lane == b + 2, lo, zero)))
                    kaug_sc[h, rows, :] = jnp.where(head_sel[h], k, bias)
                    vt_sc[h, jt] = jnp.where(head_sel_t[h], vt, 0.0).astype(BF16)
                return carry

            lax.fori_loop(0, n_t, prep, 0)

        q = q_ref[...]
        q_heads = []
        for h in range(2):
            ones = jnp.where((lane >= spare[h]) & (lane < spare[h] + 3), jnp.ones((), BF16), zero)
            q_heads.append(jnp.where(head_sel[h], q, ones))
        m_sc[...] = jnp.full((8, T), NEG, F32)
        l_sc[...] = jnp.zeros((8, T), F32)
        acc_sc[...] = jnp.zeros((128, T), F32)

        def update(j, k_lo, n_k, q_lo, masked):
            rows = pl.ds(pl.multiple_of(j * T + k_lo, n_k), n_k)
            n_q = T - q_lo
            alphas, pvs = [], []
            for h in range(2):
                s_t = _dot_nt(kaug_sc[h, rows, :], q_heads[h][q_lo:, :])
                if masked:
                    rr = lax.broadcasted_iota(jnp.int32, (n_k, n_q), 0) + k_lo
                    cc = lax.broadcasted_iota(jnp.int32, (n_k, n_q), 1) + q_lo
                    s_t = jnp.where(rr <= cc, s_t, NEG)
                m_prev = m_sc[h:h + 1, q_lo:]
                m_new = jnp.maximum(m_prev, jnp.max(s_t, axis=0, keepdims=True))
                alpha = jnp.exp(m_prev - m_new)
                p_t = jnp.exp(s_t - m_new)
                l_sc[h:h + 1, q_lo:] = alpha * l_sc[h:h + 1, q_lo:] + jnp.sum(p_t, axis=0, keepdims=True)
                m_sc[h:h + 1, q_lo:] = m_new
                alphas.append(alpha)
                pvs.append(_dot(vt_sc[h, j, :, k_lo:k_lo + n_k], p_t.astype(BF16)))
            acc_sc[:, q_lo:] = (acc_sc[:, q_lo:] * jnp.where(head_sel_t[0], alphas[0], alphas[1])
                                + (pvs[0] + pvs[1]))

        def two_off_diagonal(jj, carry):
            update(2 * jj, 0, T, 0, False)
            update(2 * jj + 1, 0, T, 0, False)
            return carry

        lax.fori_loop(0, i // 2, two_off_diagonal, 0)

        @pl.when(i % 2 == 1)
        def _():
            update(i - 1, 0, T, 0, False)
            update(i, 0, T, 0, True)

        @pl.when(i % 2 == 0)
        def _():
            update(i, 0, T, 0, True)

        l = l_sc[...]
        o_ref[...] = (acc_sc[...] / jnp.where(head_sel_t[0], l[0:1, :], l[1:2, :])).T
        is_head = lax.broadcasted_iota(jnp.int32, (8, 1), 0) < 2
        lse_ref[...] = jnp.where(is_head, m_sc[...] + jnp.log(jnp.where(is_head, l, 1.0)), 0.0)

    return pl.pallas_call(
        body, name="attention_fwd", grid=(N_PAIR, n_t),
        out_shape=(jax.ShapeDtypeStruct((S, D_ATT), F32), jax.ShapeDtypeStruct((N_PAIR, n_t, 8, T), F32)),
        in_specs=[pl.BlockSpec((T, 128), lambda hp, i: (i, hp)),
                  pl.BlockSpec((S, 128), lambda hp, i: (0, N_PAIR + hp)),
                  pl.BlockSpec((S, 128), lambda hp, i: (0, 2 * N_PAIR + hp)),
                  pl.BlockSpec((S, 128), lambda hp, i: (0, 0))],
        out_specs=(pl.BlockSpec((T, 128), lambda hp, i: (i, hp)),
                   pl.BlockSpec((None, None, 8, T), lambda hp, i: (hp, i, 0, 0))),
        scratch_shapes=[pltpu.VMEM((2, S, 128), BF16), pltpu.VMEM((2, n_t, 128, T), BF16),
                        pltpu.VMEM((8, T), F32), pltpu.VMEM((8, T), F32), pltpu.VMEM((128, T), F32)],
        compiler_params=_params(dimension_semantics=("arbitrary", "arbitrary")),
    )(qkv, qkv, qkv, big_f)


def _attention_bwd(qkv, datt, att, lse, big_f, gw_out4):
    S = qkv.shape[0]
    T = min(T_ATT, S)
    n_t = S // T
    n_steps = N_PAIR * n_t
    marks = (0, n_steps // 8, n_steps // 2, n_steps // 2 + n_steps // 8)

    def body(q_ref, do_ref, o_ref, lse_ref, k_ref, v_ref, fk_ref, gout_ref,
             dq_ref, dk_ref, dv_ref, cs_ref, dfk_ref, dfq_ref, oout_ref, stat_sc, dqt_sc, qaug_sc,
             out_buf, *red_bufs):
        hp = pl.program_id(0)
        j = pl.program_id(1)
        x, y, cc = lax.axis_index("x"), lax.axis_index("y"), lax.axis_index("c")
        plan = _scatter_stages((x, y, cc, 2 * x + y, (x, y, 1 - cc)), gout_ref, None, out_buf, *red_bufs)
        step = hp * n_t + j
        for n, mark in enumerate(marks):
            @pl.when(step == mark)
            def _(n=n):
                if n > 0:
                    plan[n - 1][1]()
                if n < 3:
                    plan[n][0]()
                else:
                    oout_ref[...] = out_buf[...]

        lane = lax.broadcasted_iota(jnp.int32, (1, 128), 1)
        sub = lax.broadcasted_iota(jnp.int32, (128, 1), 0)
        head_sel = (lane < HEAD_DIM, lane >= HEAD_DIM)
        head_sel_t = (sub < HEAD_DIM, sub >= HEAD_DIM)
        spare = (HEAD_DIM, 0)
        zero = jnp.zeros((), BF16)
        one = jnp.ones((), BF16)

        def bias_lanes(first, pieces):
            hi, mid, lo = pieces
            return lambda rest: jnp.where(lane == first, hi, jnp.where(lane == first + 1, mid,
                                                                        jnp.where(lane == first + 2, lo, rest)))

        @pl.when(j == 0)
        def _():
            dqt_sc[...] = jnp.zeros_like(dqt_sc)
            cs_ref[...] = jnp.zeros_like(cs_ref)
            dfq_ref[...] = jnp.zeros_like(dfq_ref)

            def prep(i, carry):
                rows = pl.ds(pl.multiple_of(i * T, T), T)
                q = q_ref[rows, :]
                do = do_ref[rows, :]
                prod = o_ref[rows, :] * do.astype(F32)
                d_a = jnp.sum(jnp.where(head_sel[0], prod, 0.0), axis=1, keepdims=True)
                d_b = jnp.sum(jnp.where(head_sel[0], 0.0, prod), axis=1, keepdims=True)
                delta_t = jnp.where(head_sel[0], d_a, d_b).T
                stat_sc[i, 0:1, :] = delta_t[0:1, :]
                stat_sc[i, 1:2, :] = delta_t[HEAD_DIM:HEAD_DIM + 1, :]
                lse = lse_ref[i]
                lse_cols = jnp.where(head_sel_t[0], lse[0:1, :], lse[1:2, :]).T
                for h in range(2):
                    neg_lse = -lse_cols[:, h * HEAD_DIM:h * HEAD_DIM + 1]
                    ones = jnp.where((lane >= spare[h]) & (lane < spare[h] + 3), one, zero)
                    qaug_sc[h, rows, :] = jnp.where(head_sel[h], q, bias_lanes(spare[h] + 3, _split3(neg_lse))(ones))
                return carry

            lax.fori_loop(0, n_t, prep, 0)

        k = k_ref[...]
        v = v_ref[...]
        fk = fk_ref[...]
        kt = k.astype(F32).T
        heads = []
        for h in range(2):
            fkh = jnp.sum(jnp.where(lane == 2 * hp + h, fk, 0.0), axis=1, keepdims=True)
            ones = jnp.where((lane >= spare[h] + 3) & (lane < spare[h] + 6), one, zero)
            kaug = jnp.where(head_sel[h], k, bias_lanes(spare[h], _split3(-fkh))(ones))
            heads.append((kaug, jnp.where(head_sel[h], v, zero), jnp.where(head_sel_t[h], kt, 0.0).astype(BF16)))

        def block(i, k_lo, n_k, q_lo, masked):
            n_q = T - q_lo
            rows = pl.ds(pl.multiple_of(i * T + q_lo, n_q), n_q)
            q = q_ref[rows, :]
            do = do_ref[rows, :]
            stat = stat_sc[i]
            dk = jnp.zeros((n_k, 128), F32)
            dv = jnp.zeros((n_k, 128), F32)
            dqt = jnp.zeros((128, n_q), F32)
            dfs = []
            for h in range(2):
                kaug, vh, kth = heads[h]
                arg = _dot_nt(kaug[k_lo:k_lo + n_k, :], qaug_sc[h, rows, :])
                if masked:
                    rr = lax.broadcasted_iota(jnp.int32, (n_k, n_q), 0) + k_lo
                    cc = lax.broadcasted_iota(jnp.int32, (n_k, n_q), 1) + q_lo
                    arg = jnp.where(rr <= cc, arg, NEG)
                p_t = jnp.exp(arg)
                ds_t = p_t * (_dot_nt(vh[k_lo:k_lo + n_k, :], do) - stat[h:h + 1, q_lo:])
                ds_bf = ds_t.astype(BF16)
                dv = dv + _dot(p_t.astype(BF16), jnp.where(head_sel[h], do, zero))
                dk = dk + _dot(ds_bf, jnp.where(head_sel[h], q, zero))
                dqt = dqt + _dot(kth[:, k_lo:k_lo + n_k], ds_bf)
                dfs.append(jnp.sum(ds_t, axis=1, keepdims=True))
                dfq_ref[i, h:h + 1, q_lo:] += _colsum(ds_t)
            dqt_sc[i, :, q_lo:] += dqt
            return dk, dv, dfs[0], dfs[1]

        def off_diagonal(i, acc):
            return tuple(a + b for a, b in zip(acc, block(i, 0, T, 0, False)))

        half = T // 2
        early = block(j, 0, half, 0, True)
        late = block(j, half, half, half, True)
        acc1 = tuple(jnp.concatenate([a, b], axis=0) for a, b in zip(early, late))
        n_off = n_t - 1 - j
        acc2 = lax.fori_loop(0, n_off // 2,
                             lambda ii, a: off_diagonal(j + 2 + 2 * ii, off_diagonal(j + 1 + 2 * ii, a)), acc1)
        dk_acc, dv_acc, dfa, dfb = lax.fori_loop(0, n_off % 2, lambda _, a: off_diagonal(n_t - 1, a), acc2)
        dk_ref[...] = dk_acc.astype(BF16)
        dv_ref[...] = dv_acc.astype(BF16)
        dfk_ref[...] = -jnp.where(lane == 0, dfa, jnp.where(lane == 1, dfb, 0.0))
        cs_ref[:, 128:256] = cs_ref[:, 128:256] + _colsum(dk_acc)
        cs_ref[:, 256:384] = cs_ref[:, 256:384] + _colsum(dv_acc)

        @pl.when(j == n_t - 1)
        def _():
            def finish(i, tot):
                dq = dqt_sc[i].T
                dq_ref[pl.ds(pl.multiple_of(i * T, T), T), :] = dq.astype(BF16)
                return tot + _colsum(dq)

            cs_ref[:, 0:128] = lax.fori_loop(0, n_t, finish, jnp.zeros((1, 128), F32))

    pair_rows = lambda hp, j: (hp, 0, 0)
    vm = pl.BlockSpec(memory_space=pltpu.VMEM)
    _, r_out, c_out = gw_out4.shape
    return pl.pallas_call(
        body, name="attention_bwd", grid=(N_PAIR, n_t),
        out_shape=(jax.ShapeDtypeStruct((S, D_ATT), BF16), jax.ShapeDtypeStruct((S, D_ATT), BF16),
                   jax.ShapeDtypeStruct((S, D_ATT), BF16), jax.ShapeDtypeStruct((N_PAIR, 1, 384), F32),
                   jax.ShapeDtypeStruct((N_PAIR, S, 128), F32),
                   jax.ShapeDtypeStruct((N_PAIR, n_t, 8, T), F32),
                   jax.ShapeDtypeStruct((r_out, c_out), F32)),
        in_specs=[pl.BlockSpec((S, 128), lambda hp, j: (0, hp)),
                  pl.BlockSpec((S, 128), lambda hp, j: (0, hp)),
                  pl.BlockSpec((S, 128), lambda hp, j: (0, hp)),
                  pl.BlockSpec((None, n_t, 8, T), lambda hp, j: (hp, 0, 0, 0)),
                  pl.BlockSpec((T, 128), lambda hp, j: (j, N_PAIR + hp)),
                  pl.BlockSpec((T, 128), lambda hp, j: (j, 2 * N_PAIR + hp)),
                  pl.BlockSpec((T, 128), lambda hp, j: (j, 0)),
                  vm],
        out_specs=(pl.BlockSpec((S, 128), lambda hp, j: (0, hp)),
                   pl.BlockSpec((T, 128), lambda hp, j: (j, hp)),
                   pl.BlockSpec((T, 128), lambda hp, j: (j, hp)),
                   pl.BlockSpec((None, 1, 384), pair_rows),
                   pl.BlockSpec((None, T, 128), lambda hp, j: (hp, j, 0)),
                   pl.BlockSpec((None, n_t, 8, T), lambda hp, j: (hp, 0, 0, 0)),
                   vm),
        scratch_shapes=[pltpu.VMEM((n_t, 8, T), F32), pltpu.VMEM((n_t, 128, T), F32),
                        pltpu.VMEM((2, S, 128), BF16), pltpu.VMEM((r_out, c_out), F32)]
        + _scatter_scratch(r_out, c_out),
        compiler_params=_params(dimension_semantics=("arbitrary", "arbitrary")),
    )(qkv, datt, att, lse, qkv, qkv, big_f, gw_out4)


def _window_counts(first_row, n_rows, window):
    t = lax.broadcasted_iota(jnp.int32, (n_rows, 1), 0) + first_row
    return jnp.minimum((t + 1).astype(F32), float(window))


def _middle(x, tgt, att, g, p, gate, w_mix, b_mix, pool_scale, w_out, b_out, ln_g, ln_b):
    S = x.shape[0]
    tm = min(TM_MID, S)
    halo_blocks = tm // POOL_HALO

    def body(x_ref, t_ref, att_ref, g_ref, p_ref, ph_ref, gate_ref, wm_ref, bm_ref, ps_ref, wo_ref, bo_ref,
             lg_ref, lb_ref,
             dh_ref, datt_ref, dg_ref, dpl_ref, gwo_ref, gwm_ref, vec_ref, loss_ref, gwo_acc, gwo_sem):
        i = pl.program_id(0)
        last = i == pl.num_programs(0) - 1
        gwo_out = pltpu.make_async_copy(gwo_acc, gwo_ref, gwo_sem)

        @pl.when(i == 0)
        def _():
            gwo_acc[...] = jnp.zeros_like(gwo_acc)
            gwm_ref[...] = jnp.zeros_like(gwm_ref)
            vec_ref[...] = jnp.zeros_like(vec_ref)
            loss_ref[...] = jnp.zeros_like(loss_ref)

        pc = p_ref[...]
        halo = jnp.where(i > 0, ph_ref[...], 0.0)
        pe = jnp.concatenate([halo, pc], axis=0)
        pooled_parts = []
        for gi, w in enumerate(POOL_WINDOWS):
            cur = pe[:, gi * POOL_GROUP:(gi + 1) * POOL_GROUP]
            span = 1
            while span < w:
                cur = cur + pltpu.roll(cur, span, 0)
                span *= 2
            wsum = cur[POOL_HALO:, :]
            mean = wsum / _window_counts(i * tm, tm, w)
            pooled_parts.append(mean - pc[:, gi * POOL_GROUP:(gi + 1) * POOL_GROUP])
        pooled_bf =[v.astype(BF16) for v in pooled_parts]
        wm = [wm_ref[gi].astype(BF16) for gi in range(4)]
        mixed = jnp.concatenate([_dot(pooled_bf[gi], wm[gi]) for gi in range(4)], axis=1) + bm_ref[...]
        ps = ps_ref[...]
        pool_out = mixed * ps
        gv = g_ref[...]
        sig = _sigmoid(gv)
        silu = gv * sig
        att = att_ref[...]
        y = jnp.concatenate([att * silu[:, :D_ATT], pool_out * silu[:, D_ATT:]], axis=1)
        y_bf = y.astype(BF16)
        wo = wo_ref[...]
        yo = _dot(y_bf, wo) + bo_ref[...]
        gate = gate_ref[...]
        h = ALPHA * x_ref[...] + gate * yo
        mu = jnp.mean(h, axis=1, keepdims=True)
        hc = h - mu
        var = jnp.mean(hc * hc, axis=1, keepdims=True)
        rstd = lax.rsqrt(var + LN_EPS)
        yhat = hc * rstd
        lg = lg_ref[...]
        out = yhat * lg + lb_ref[...]
        err = out - t_ref[...]
        loss_ref[...] += 0.5 * jnp.sum(jnp.mean(err * err, axis=1, keepdims=True), axis=0, keepdims=True)

        dout = err * (1.0 / D)
        g_ln_b = _colsum(dout)
        g_ln_g = _colsum(dout * yhat)
        dyh = dout * lg
        dh = rstd * (dyh - jnp.mean(dyh, axis=1, keepdims=True)
                     - yhat * jnp.mean(dyh * yhat, axis=1, keepdims=True))
        dh_ref[...] = dh
        d_gate = _colsum(dh * yo)
        dyo = gate * dh
        g_b_out = _colsum(dyo)
        dyo_bf = dyo.astype(BF16)
        gwo_acc[...] += _dot_tn(y_bf, dyo_bf)
        pl.when(last)(gwo_out.start)
        dy = _dot_nt(dyo_bf, wo)
        dsilu = sig * (1.0 + gv * (1.0 - sig))
        dy_a = dy[:, :D_ATT]
        dy_p = dy[:, D_ATT:]
        datt_ref[...] = (dy_a * silu[:, :D_ATT]).astype(BF16)
        dpo = dy_p * silu[:, D_ATT:]
        dg = jnp.concatenate([dy_a * att * dsilu[:, :D_ATT], dy_p * pool_out * dsilu[:, D_ATT:]], axis=1)
        dg_ref[...] = dg.astype(BF16)
        g_dg = _colsum(dg)
        g_ps = _colsum(dpo * mixed)
        dmixed = dpo * ps
        g_bm = _colsum(dmixed)
        dmixed_bf = dmixed.astype(BF16)
        dpl = []
        for gi in range(4):
            dm = dmixed_bf[:, gi * POOL_GROUP:(gi + 1) * POOL_GROUP]
            gwm_ref[gi] += _dot_tn(pooled_bf[gi], dm)
            dpl.append(_dot_nt(dm, wm[gi]))
        dpl_ref[...] = jnp.concatenate(dpl, axis=1)
        vec_ref[0:1, :] += g_ln_g
        vec_ref[1:2, :] += g_ln_b
        vec_ref[2:3, :] += d_gate
        vec_ref[3:4, :] += g_b_out
        vec_ref[4:5, :] += g_dg
        vec_ref[5:6, 0:D_POOL] += g_ps
        vec_ref[6:7, 0:D_POOL] += g_bm
        pl.when(last)(gwo_out.wait)

    row = lambda w: pl.BlockSpec((tm, w), lambda i: (i, 0))
    full2 = lambda a: pl.BlockSpec(a.shape, lambda i: (0, 0))
    full3 = lambda a: pl.BlockSpec(a.shape, lambda i: (0, 0, 0))
    return pl.pallas_call(
        body, name="middle", grid=(S // tm,),
        out_shape=(jax.ShapeDtypeStruct((S, D), F32),
                   jax.ShapeDtypeStruct((S, D_ATT), BF16),
                   jax.ShapeDtypeStruct((S, D), BF16),
                   jax.ShapeDtypeStruct((S, D_POOL), F32),
                   jax.ShapeDtypeStruct((D, D), F32),
                   jax.ShapeDtypeStruct((4, POOL_GROUP, POOL_GROUP), F32),
                   jax.ShapeDtypeStruct((8, D), F32),
                   jax.ShapeDtypeStruct((1, 1), F32)),
        in_specs=[row(D), row(D), row(D_ATT), row(D), row(D_POOL),
                  pl.BlockSpec((POOL_HALO, D_POOL), lambda i: (jnp.maximum(i * halo_blocks - 1, 0), 0)),
                  full2(gate), full3(w_mix), full2(b_mix), full2(pool_scale), full2(w_out), full2(b_out),
                  full2(ln_g), full2(ln_b)],
        out_specs=(row(D), row(D_ATT), row(D), row(D_POOL),
                   pl.BlockSpec(memory_space=pl.ANY),
                   pl.BlockSpec((4, POOL_GROUP, POOL_GROUP), lambda i: (0, 0, 0)),
                   pl.BlockSpec((8, D), lambda i: (0, 0)),
                   pl.BlockSpec((1, 1), lambda i: (0, 0))),
        scratch_shapes=[pltpu.VMEM((D, D), F32), pltpu.SemaphoreType.DMA],
        compiler_params=_params(dimension_semantics=("arbitrary",)),
    )(x, tgt, att, g, p, p, gate, w_mix, b_mix, pool_scale, w_out, b_out, ln_g, ln_b)


def _tail(dpl, dfk, dfq, f):
    S = dpl.shape[0]
    tm = min(T_ATT, S)
    n_t = S // tm
    halo_blocks = tm // POOL_HALO
    last_halo = S // POOL_HALO - 1

    def body(d_ref, dn_ref, dfk_ref, dfq_ref, f_ref, dp_ref, df_ref, cs_ref, carry):
        s = pl.program_id(0)
        i = n_t - 1 - s

        @pl.when(s == 0)
        def _():
            carry[...] = jnp.zeros_like(carry)
            cs_ref[...] = jnp.zeros_like(cs_ref)

        dc = d_ref[...]
        nxt = jnp.where(s > 0, dn_ref[...], 0.0)
        de = jnp.concatenate([dc, nxt], axis=0)
        n_e = tm + POOL_HALO
        parts = []
        for gi, w in enumerate(POOL_WINDOWS):
            cur = de[:, gi * POOL_GROUP:(gi + 1) * POOL_GROUP] / _window_counts(i * tm, n_e, w)
            span = 1
            while span < w:
                cur = cur + pltpu.roll(cur, n_e - span, 0)
                span *= 2
            parts.append(cur[:tm, :] - dc[:, gi * POOL_GROUP:(gi + 1) * POOL_GROUP])
        dp = jnp.concatenate(parts, axis=1)
        dp_ref[...] = dp.astype(BF16)
        cs_ref[0:1, :] += _colsum(dp)

        r = lax.broadcasted_iota(jnp.int32, (tm, tm), 0)
        c = lax.broadcasted_iota(jnp.int32, (tm, tm), 1)
        tri = (r >= c).astype(F32)
        k_cols = dfk_ref[0]
        rows8 = dfq_ref[0]
        for hp in range(1, N_PAIR):
            k_cols = k_cols + pltpu.roll(dfk_ref[hp], 2 * hp, 1)
            rows8 = rows8 + pltpu.roll(dfq_ref[hp], 2 * hp, 0)
        rows8 = rows8 + k_cols.T[0:8, :]
        dlogf8 = jnp.dot(rows8, tri, preferred_element_type=F32, precision=lax.Precision.HIGHEST) + carry[...]
        first = lax.broadcasted_iota(jnp.int32, (1, tm), 1) == 0
        carry[...] = jnp.sum(jnp.where(first, dlogf8, 0.0), axis=1, keepdims=True)
        dlogf = jnp.concatenate([dlogf8, jnp.zeros((128 - 8, tm), F32)], axis=0).T
        df = dlogf * _sigmoid(-f_ref[...])
        df_ref[...] = df.astype(BF16)
        cs_ref[1:2, 0:128] += _colsum(df)

    rev = lambda w: pl.BlockSpec((tm, w), lambda s: (n_t - 1 - s, 0))
    return pl.pallas_call(
        body, name="tail", grid=(n_t,),
        out_shape=(jax.ShapeDtypeStruct((S, D_POOL), BF16), jax.ShapeDtypeStruct((S, 128), BF16),
                   jax.ShapeDtypeStruct((8, D_POOL), F32)),
        in_specs=[rev(D_POOL),
                  pl.BlockSpec((POOL_HALO, D_POOL),
                               lambda s: (jnp.minimum((n_t - s) * halo_blocks, last_halo), 0)),
                  pl.BlockSpec((N_PAIR, tm, 128), lambda s: (0, n_t - 1 - s, 0)),
                  pl.BlockSpec((N_PAIR, None, 8, tm), lambda s: (0, n_t - 1 - s, 0, 0)),
                  rev(128)],
        out_specs=(rev(D_POOL), rev(128), pl.BlockSpec((8, D_POOL), lambda s: (0, 0))),
        scratch_shapes=[pltpu.VMEM((8, 1), F32)],
        compiler_params=_params(dimension_semantics=("arbitrary",)),
    )(dpl, dpl, dfk, dfq, f)


PIECES = ((O_QKV, D_ATT), (O_QKV + D_ATT, D_ATT), (O_QKV + 2 * D_ATT, D_ATT), (O_F, 128), (O_P, D_POOL), (O_G, D))


def _grad_w_in(u, pieces):
    S = u.shape[0]
    tm = min(TM_GW, S)
    n_t = S // tm

    def body(u_ref, *rest):
        piece_refs, out_ref, acc, sem = rest[:6], rest[6], rest[7], rest[8]
        i = pl.program_id(0)

        @pl.when(i == 0)
        def _():
            acc[...] = jnp.zeros_like(acc)

        last = i == n_t - 1
        copies = [pltpu.make_async_copy(acc.at[:, off:off + w], out_ref.at[:, off:off + w], sem.at[k])
                  for k, (off, w) in enumerate(PIECES)]
        u_t = u_ref[...]
        for k, ((off, w), ref) in enumerate(zip(PIECES, piece_refs)):
            acc[:, off:off + w] += _dot_tn(u_t, ref[...])
            pl.when(last)(copies[k].start)

        @pl.when(last)
        def _():
            for cp in copies:
                cp.wait()

    return pl.pallas_call(
        body, name="grad_w_in", grid=(n_t,),
        out_shape=jax.ShapeDtypeStruct((D, D_PAD), F32),
        in_specs=[pl.BlockSpec((tm, D), lambda i: (i, 0))]
        + [pl.BlockSpec((tm, w), lambda i: (i, 0)) for _, w in PIECES],
        out_specs=pl.BlockSpec(memory_space=pl.ANY),
        scratch_shapes=[pltpu.VMEM((D, D_PAD), F32), pltpu.SemaphoreType.DMA((len(PIECES),))],
        compiler_params=_params(dimension_semantics=("arbitrary",)),
    )(u, *pieces)


def _grad_x(pieces, wt_pad, dh, x, scale):
    S = x.shape[0]
    tm = min(TM_DU, S)

    def body(*refs):
        piece_refs = refs[:6]
        w_ref, dh_ref, x_ref, sc_ref, gx_ref, vec_ref = refs[6:]

        @pl.when(pl.program_id(0) == 0)
        def _():
            vec_ref[...] = jnp.zeros_like(vec_ref)

        du = jnp.zeros((tm, D), F32)
        for (off, w), ref in zip(PIECES, piece_refs):
            du = du + _dot(ref[...], w_ref[off:off + w, :])
        xv = x_ref[...]
        gx_ref[...] = ALPHA * dh_ref[...] + du * (1.0 + sc_ref[...])
        vec_ref[0:1, :] += _colsum(du)
        vec_ref[1:2, :] += _colsum(du * xv)

    row = lambda w: pl.BlockSpec((tm, w), lambda i: (i, 0))
    return pl.pallas_call(
        body, name="grad_x", grid=(S // tm,),
        out_shape=(jax.ShapeDtypeStruct((S, D), F32), jax.ShapeDtypeStruct((8, D), F32)),
        in_specs=[row(w) for _, w in PIECES]
        + [pl.BlockSpec(wt_pad.shape, lambda i: (0, 0)), row(D), row(D), pl.BlockSpec((1, D), lambda i: (0, 0))],
        out_specs=(row(D), pl.BlockSpec((8, D), lambda i: (0, 0))),
        compiler_params=_params(dimension_semantics=("arbitrary",)),
    )(*pieces, wt_pad, dh, x, scale)


def _adamw_math(w, g, m, v):
    m = ADAM_B1 * m + (1.0 - ADAM_B1) * g
    v = ADAM_B2 * v + (1.0 - ADAM_B2) * (g * g)
    m_hat = m / (1.0 - ADAM_B1 ** ADAM_STEP)
    v_hat = v / (1.0 - ADAM_B2 ** ADAM_STEP)
    delta = -ADAM_LR * (m_hat / (jnp.sqrt(v_hat) + ADAM_EPS) + ADAM_WD * w)
    return delta, m, v


def _adamw(groups, n_steps):
    n = len(groups)

    def body(*refs):
        ins, outs = refs[:4 * n], refs[4 * n:]
        for t in range(n):
            w, g, m, v = (r[...] for r in ins[4 * t:4 * t + 4])
            d, m2, v2 = _adamw_math(w, g, m, v)
            outs[4 * t][...] = d
            outs[4 * t + 1][...] = m2
            outs[4 * t + 2][...] = v2
            outs[4 * t + 3][...] = g

    in_specs, out_specs, out_shape, args = [], [], [], []
    for (w, g, m, v) in groups:
        rest = w.shape[1:]
        spec = pl.BlockSpec((w.shape[0] // n_steps,) + rest, lambda i, nd=len(rest): (i,) + (0,) * nd)
        in_specs += [spec] * 4
        out_specs += [spec] * 4
        out_shape += [jax.ShapeDtypeStruct(w.shape, F32)] * 4
        args += [w, g, m, v]
    return pl.pallas_call(
        body, name="adamw_%d_%d" % (n, n_steps), grid=(n_steps,),
        out_shape=tuple(out_shape), in_specs=in_specs, out_specs=tuple(out_specs),
        compiler_params=_params(dimension_semantics=("arbitrary",)),
    )(*args)


def _adamw_small(small_sum, g_b_ada, params):
    n = len(params)

    def body(gs_ref, gba_ref, *refs):
        ins, outs = refs[:3 * n], refs[3 * n:]
        for t, (name, w0, _, _) in enumerate(params):
            w_ref, m_ref, v_ref = ins[3 * t:3 * t + 3]
            first = SMALL_SEGS[name][0] if name in SMALL_SEGS else None
            if w0.shape[0] > 1:
                pieces = [((slice(None), slice(None)), gs_ref[first:first + w0.shape[0], :])]
            else:
                pieces = []
                for r in range(-(-w0.shape[1] // 128)):
                    lanes = slice(128 * r, min(128 * r + 128, w0.shape[1]))
                    g = gba_ref[0:1, lanes] if first is None else gs_ref[first + r:first + r + 1, 0:lanes.stop - lanes.start]
                    pieces.append(((slice(0, 1), lanes), g))
            for where, g in pieces:
                d, m2, v2 = _adamw_math(w_ref[where], g, m_ref[where], v_ref[where])
                for ref, val in zip(outs[4 * t:4 * t + 4], (g, d, m2, v2)):
                    ref[where] = val

    vm = pl.BlockSpec(memory_space=pltpu.VMEM)
    args = [small_sum, g_b_ada]
    out_shape = []
    for _, w, m, v in params:
        args += [w, m, v]
        out_shape += [jax.ShapeDtypeStruct(w.shape, F32)] * 4
    return pl.pallas_call(
        body, name="adamw_small",
        out_shape=tuple(out_shape), in_specs=[vm] * len(args), out_specs=(vm,) * len(out_shape),
        compiler_params=_params(),
    )(*args)


def _pack_small(parts):
    rows = []
    used = 0
    for name, (first, n_rows) in SMALL_SEGS.items():
        if first > used:
            rows.append(jnp.zeros((first - used, 128), F32))
        flat = parts[name].reshape(-1)
        flat = jnp.pad(flat, (0, n_rows * 128 - flat.shape[0]))
        rows.append(flat.reshape(n_rows, 128))
        used = first + n_rows
    rows.append(jnp.zeros((SMALL_ROWS - used, 128), F32))
    return jnp.concatenate(rows, axis=0)


def _pad_in(v):
    r = v.shape[0]
    z = jnp.zeros((r, O_P - O_F - N_HEADS), v.dtype)
    return jnp.concatenate([v[:, :3 * D_ATT + N_HEADS], z, v[:, 3 * D_ATT + N_HEADS:]], axis=1)


def _unpad_in(v):
    return jnp.concatenate([v[:, :O_F + N_HEADS], v[:, O_P:]], axis=1)


def _shards_in(v):
    gap = O_P - (O_F + N_HEADS)
    parts = []
    for a in range(N_CHIPS):
        lo, hi = a * SHARD_IN, (a + 1) * SHARD_IN
        cut = O_F + N_HEADS
        if hi <= cut:
            parts.append(v[:, lo:hi])
        elif lo >= cut:
            parts.append(v[:, lo + gap:hi + gap])
        else:
            parts.append(jnp.concatenate([v[:, lo:cut], v[:, cut + gap:hi + gap]], axis=1))
    return jnp.stack(parts, axis=0)


def kernel(x, c, w_ada, b_ada, w_in, b_in, w_pool_mix, b_pool_mix, pool_scale, w_out, b_out, ln_g, ln_b, loss_target, m_w_ada, m_b_ada, m_w_in, m_b_in, m_w_pool_mix, m_b_pool_mix, m_pool_scale, m_w_out, m_b_out, m_ln_g, m_ln_b, v_w_ada, v_b_ada, v_w_in, v_b_in, v_w_pool_mix, v_b_pool_mix, v_pool_scale, v_w_out, v_b_out, v_ln_g, v_ln_b):
    S = x.shape[1]
    T = min(T_ATT, S)
    n_t = S // T
    x2 = x[0]
    tgt = loss_target[0]
    q_scale = jnp.concatenate([jnp.full((1, D_ATT), Q_SCALE, F32), jnp.ones((1, D_PAD - D_ATT), F32)], axis=1)

    to_cols = lambda a: jnp.transpose(a, (2, 0, 1))
    from_cols = lambda a: jnp.transpose(a, (1, 2, 0))
    c_all, shift, scale, gate, wt_pad = _gather_and_ada(
        c, w_ada[0], b_ada.reshape(4, 1, SHARD_ADA), to_cols(w_in))
    b_pad = _pad_in(b_in) * q_scale

    u, qkv, f, p, g, w_out_all = _in_proj(x2, shift, scale, wt_pad, b_pad, w_out[0])
    w_out_full = w_out_all.reshape(D, D)
    big_f = _forget_cumsum(f)
    att, lse = _attention_fwd(qkv, big_f)

    dh, datt, dg, dpl, gw_out, gw_mix, vec, loss_part = _middle(
        x2, tgt, att, g, p, gate, w_pool_mix[0], b_pool_mix.reshape(1, D_POOL), pool_scale, w_out_full, b_out, ln_g, ln_b)
    dq, dk, dv, cs_att, dfk, dfq, g_w_out = _attention_bwd(
        qkv, datt, att, lse, big_f, gw_out.reshape(N_CHIPS, SHARD_OUT, D))
    dp, df, cs_tail = _tail(dpl, dfk, dfq, f)
    pieces = (dq, dk, dv, df, dp, dg)
    gw_pad = _grad_w_in(u, pieces)
    grad_x, vec_x = _grad_x(pieces, wt_pad, dh, x2, scale)

    cs_qkv = jnp.transpose(cs_att.reshape(N_PAIR, 3, 128), (1, 0, 2)).reshape(1, 3 * D_ATT)
    gb_pad = jnp.concatenate([cs_qkv, cs_tail[1:2, 0:128], cs_tail[0:1, :], vec[4:5, :]], axis=1) * q_scale
    dada = jnp.concatenate([vec_x[0:1, :], vec_x[1:2, :], vec[2:3, :]], axis=1)
    small = _pack_small({
        "b_in": _unpad_in(gb_pad), "w_pool_mix": gw_mix, "b_pool_mix": vec[6:7, :D_POOL],
        "pool_scale": vec[5:6, :D_POOL], "b_out": vec[3:4, :], "ln_g": vec[0:1, :], "ln_b": vec[1:2, :],
        "loss": loss_part})

    g_w_in, small_sum, g_w_ada, g_b_ada, loss = _reduce_all(
        gw_pad, _shards_in(q_scale), small, dada, c_all)

    big = _adamw([(w_ada[0], g_w_ada, m_w_ada[0], v_w_ada[0]),
                  (w_out[0], g_w_out, m_w_out[0], v_w_out[0])], 2)
    big_in = _adamw([(to_cols(w_in), g_w_in, to_cols(m_w_in), to_cols(v_w_in))], 2)
    tiles = lambda a: a.reshape(4 * POOL_GROUP, POOL_GROUP)
    flat = lambda a: a.reshape(1, D_POOL)
    small_params = [("b_ada", b_ada, m_b_ada, v_b_ada), ("b_in", b_in, m_b_in, v_b_in),
                    ("w_pool_mix", tiles(w_pool_mix), tiles(m_w_pool_mix), tiles(v_w_pool_mix)),
                    ("b_pool_mix", flat(b_pool_mix), flat(m_b_pool_mix), flat(v_b_pool_mix)),
                    ("pool_scale", pool_scale, m_pool_scale, v_pool_scale), ("b_out", b_out, m_b_out, v_b_out),
                    ("ln_g", ln_g, m_ln_g, v_ln_g), ("ln_b", ln_b, m_ln_b, v_ln_b)]
    sm = _adamw_small(small_sum, g_b_ada, small_params)
    sm_idx = {p[0]: n for n, p in enumerate(small_params)}
    shapes = {"w_pool_mix": (1, 4, POOL_GROUP, POOL_GROUP), "b_pool_mix": (1, 4, POOL_GROUP)}

    names = ["w_ada", "b_ada", "w_in", "b_in", "w_pool_mix", "b_pool_mix", "pool_scale", "w_out", "b_out",
             "ln_g", "ln_b"]
    big_idx = {"w_ada": 0, "w_out": 1}

    def leaf(kind, name):
        if name == "w_in":
            return from_cols(big_in[(kind - 1) % 4])
        if name in big_idx:
            return big[4 * big_idx[name] + (kind - 1) % 4][None]
        val = sm[4 * sm_idx[name] + kind]
        return val.reshape(shapes[name]) if name in shapes else val

    outs = [loss.reshape(()), grad_x[None]]
    for kind in range(4):
        outs += [leaf(kind, n) for n in names]
    return tuple(outs)
```

```python
import functools

import numpy as np
import jax
import jax.numpy as jnp
from jax import lax
from jax.experimental import pallas as pl
from jax.experimental.pallas import tpu as pltpu

F32 = jnp.float32
BF16 = jnp.bfloat16
MESH = pl.DeviceIdType.MESH

D = 1024
D_ATT = 512
D_POOL = 512
N_HEADS = 8
HEAD_DIM = 64
N_PAIR = N_HEADS // 2
POOL_WINDOWS = (2, 4, 8, 16)
POOL_GROUP = 128
POOL_HALO = 16
LN_EPS = 1e-5
ALPHA = 2.0 ** 0.25
D_IN = 3 * D_ATT + N_HEADS + D_POOL + D_ATT + D_POOL
N_CHIPS = 4
SHARD_IN = D_IN // N_CHIPS
SHARD_ADA = 3 * D // N_CHIPS
SHARD_OUT = D // N_CHIPS

O_QKV, O_F, O_P, O_G, D_PAD = 0, 1536, 1664, 2176, 3200
Q_SCALE = HEAD_DIM ** -0.5

ADAM_LR, ADAM_B1, ADAM_B2, ADAM_EPS, ADAM_WD, ADAM_STEP = 0.001, 0.9, 0.999, 1e-08, 0.01, 10

NEG = -1e30

VMEM_LIMIT = 56 * 1024 * 1024

TM_PROJ = 512
T_ATT = 512
TM_MID = 512
TM_GW = 1024
TM_DU = 512

REL7 = [(0, 0, 1), (0, 1, 0), (0, 1, 1), (1, 0, 0), (1, 0, 1), (1, 1, 0), (1, 1, 1)]
REL3 = [(0, 1), (1, 0), (1, 1)]

SMALL_SEGS = {}
_row = 0
for _name, _n in (("b_in", D_IN), ("w_pool_mix", 65536), ("b_pool_mix", 512), ("pool_scale", 512),
                  ("b_out", 1024), ("ln_g", 1024), ("ln_b", 1024), ("loss", 1)):
    _rows = -(-_n // 1024) * 8
    SMALL_SEGS[_name] = (_row, _rows)
    _row += _rows
SMALL_ROWS = -(-_row // 16) * 16


def _params(**kw):
    return pltpu.CompilerParams(vmem_limit_bytes=VMEM_LIMIT, **kw)


def _flip(v, d):
    return v if d == 0 else 1 - v


def _dot(a, b):
    return jnp.dot(a, b, preferred_element_type=F32)


def _dot_nt(a, b):
    return lax.dot_general(a, b, (((1,), (1,)), ((), ())), preferred_element_type=F32)


def _dot_tn(a, b):
    return lax.dot_general(a, b, (((0,), (0,)), ((), ())), preferred_element_type=F32)


def _sigmoid(v):
    return 1.0 / (1.0 + jnp.exp(-v))


def _colsum(v):
    return jnp.sum(v, axis=0, keepdims=True)


def _gather_stages(pos, src_ref, dst_ref, half, own_sem, s_sem, r_sem, fs_sem, fr_sem):
    x, y, cc, chip, sib = pos
    own = pltpu.make_async_copy(src_ref, dst_ref.at[chip], own_sem)
    first, landed, others = [], [], []
    for k, (dx, dy) in enumerate(REL3):
        px, py = _flip(x, dx), _flip(y, dy)
        first.append(pltpu.make_async_remote_copy(
            src_ref=src_ref.at[half(cc)], dst_ref=dst_ref.at[(chip,) + half(cc)],
            send_sem=s_sem.at[k], recv_sem=r_sem.at[k], device_id=(px, py, cc), device_id_type=MESH))
        landed.append(dst_ref.at[(2 * px + py,) + half(cc)])
        others.append(dst_ref.at[(2 * px + py,) + half(1 - cc)])
    passed = [pltpu.make_async_remote_copy(src_ref=landed[k], dst_ref=landed[k], send_sem=fs_sem.at[k],
                                           recv_sem=fr_sem.at[k], device_id=sib, device_id_type=MESH)
              for k in range(3)]

    def start(finish_src=None):
        for cp in first:
            cp.start()
        if finish_src is not None:
            finish_src()
        own.start()

    def forward():
        for k in range(3):
            pltpu.make_async_remote_copy(src_ref=landed[k], dst_ref=landed[k], send_sem=s_sem.at[k],
                                         recv_sem=r_sem.at[k], device_id=sib, device_id_type=MESH).wait_recv()
            passed[k].start()

    def finish():
        for k in range(3):
            pltpu.make_async_remote_copy(src_ref=others[k], dst_ref=others[k], send_sem=fs_sem.at[k],
                                         recv_sem=fr_sem.at[k], device_id=sib, device_id_type=MESH).wait_recv()
        for cp in first + passed:
            cp.wait_send()
        own.wait()

    return start, forward, finish


def _gather_scratch():
    return [pltpu.SemaphoreType.DMA, pltpu.SemaphoreType.DMA((3,)), pltpu.SemaphoreType.DMA((3,)),
            pltpu.SemaphoreType.DMA((3,)), pltpu.SemaphoreType.DMA((3,))]


def _gather_and_ada(c, w_ada, b_ada4, w_in_sh):
    def body(c_ref, w_ref, b_ref, win_ref, call_ref, shift_ref, scale_ref, gate_ref, wt_pad_ref,
             win_all, win_bf, ada_ref, cslab, sbuf, rbuf, cs_sem, cr_sem, as_sem, ar_sem, *gather_sems):
        x, y, cc = lax.axis_index("x"), lax.axis_index("y"), lax.axis_index("c")
        me = 4 * x + 2 * y + cc
        chip = 2 * x + y
        lane_half = lambda which: (slice(None), pl.ds(pl.multiple_of(which * (D // 2), D // 2), D // 2))
        def round_half(which):
            for h in range(2):
                @pl.when(which == h)
                def _():
                    lanes = slice(h * (D // 2), (h + 1) * (D // 2))
                    win_bf[:, lanes] = win_ref[:, 0, lanes].astype(BF16)

        start, forward, finish = _gather_stages((x, y, cc, chip, (x, y, 1 - cc)), win_bf, win_all, lane_half,
                                                *gather_sems)
        round_half(cc)
        start(lambda: round_half(1 - cc))

        cslab[...] = jnp.broadcast_to(c_ref[...], (8, D))
        call_ref[me] = cslab[...]
        gathers = []
        for k, (dx, dy, dc) in enumerate(REL7):
            cp = pltpu.make_async_remote_copy(
                src_ref=cslab, dst_ref=call_ref.at[me], send_sem=cs_sem.at[k], recv_sem=cr_sem.at[k],
                device_id=(_flip(x, dx), _flip(y, dy), _flip(cc, dc)), device_id_type=MESH)
            cp.start()
            gathers.append(cp)
        for cp in gathers:
            cp.wait()
        slab_row = lax.broadcasted_iota(jnp.int32, (8, 1), 0)
        mat = jnp.zeros((8, D), F32)
        for r in range(8):
            mat = jnp.where(slab_row == r, call_ref[r], mat)
        act = (mat * _sigmoid(mat)).astype(BF16)
        part = _dot(act, w_ref[...].astype(BF16))
        sends = []
        for k, (dx, dy) in enumerate(REL3):
            px, py = _flip(x, dx), _flip(y, dy)
            r = 4 * px + 2 * py + cc
            piece = _colsum(jnp.where(slab_row == r, part, 0.0))
            sbuf[k] = jnp.broadcast_to(piece, (8, SHARD_ADA))
            cp = pltpu.make_async_remote_copy(
                src_ref=sbuf.at[k], dst_ref=rbuf.at[k], send_sem=as_sem.at[k], recv_sem=ar_sem.at[k],
                device_id=(px, py, cc), device_id_type=MESH)
            cp.start()
            sends.append(cp)
        own_piece = _colsum(jnp.where(slab_row == me, part, 0.0))
        ada_ref[chip] = jnp.broadcast_to(own_piece, (8, SHARD_ADA)) + b_ref[chip]
        for k, (dx, dy) in enumerate(REL3):
            sends[k].wait()
            a = 2 * _flip(x, dx) + _flip(y, dy)
            ada_ref[a] = rbuf[k] + b_ref[a]
        ada = jnp.concatenate([ada_ref[a][0:1, :] for a in range(N_CHIPS)], axis=1)
        shift_ref[...] = ada[:, 0:D]
        scale_ref[...] = ada[:, D:2 * D]
        gate_ref[...] = ada[:, 2 * D:3 * D]

        forward()
        finish()
        n_real = 3 * D_ATT + N_HEADS
        for a in range(N_CHIPS):
            lo, hi = a * SHARD_IN, (a + 1) * SHARD_IN
            for s0, s1 in ((lo, min(hi, D_ATT)), (max(lo, D_ATT), min(hi, n_real)), (max(lo, n_real), hi)):
                if s0 < s1:
                    rows = win_all[a, s0 - lo:s1 - lo, :]
                    if s1 <= D_ATT:
                        rows = rows * jnp.asarray(Q_SCALE, BF16)
                    shift = O_P - n_real if s0 >= n_real else 0
                    wt_pad_ref[s0 + shift:s1 + shift, :] = rows
        wt_pad_ref[n_real:O_P, :] = jnp.zeros((O_P - n_real, D), BF16)

    vm = pl.BlockSpec(memory_space=pltpu.VMEM)
    return pl.pallas_call(
        body, name="gather_and_ada",
        out_shape=(jax.ShapeDtypeStruct((8, 8, D), F32),) + (jax.ShapeDtypeStruct((1, D), F32),) * 3
        + (jax.ShapeDtypeStruct((D_PAD, D), BF16),),
        in_specs=[vm] * 4, out_specs=(vm,) * 5,
        scratch_shapes=[pltpu.VMEM((N_CHIPS, SHARD_IN, D), BF16), pltpu.VMEM((SHARD_IN, D), BF16),
                        pltpu.VMEM((N_CHIPS, 8, SHARD_ADA), F32), pltpu.VMEM((8, D), F32), pltpu.VMEM((3, 8, SHARD_ADA), F32),
                        pltpu.VMEM((3, 8, SHARD_ADA), F32),
                        pltpu.SemaphoreType.DMA((7,)), pltpu.SemaphoreType.DMA((7,)),
                        pltpu.SemaphoreType.DMA((3,)), pltpu.SemaphoreType.DMA((3,))] + _gather_scratch(),
        compiler_params=_params(),
    )(c, w_ada, b_ada4, w_in_sh)


def _shard_cols():
    cut, gap = O_F + N_HEADS, O_P - (O_F + N_HEADS)
    out = []
    for a in range(N_CHIPS):
        lo, hi = a * SHARD_IN, (a + 1) * SHARD_IN
        out.append(([(lo, min(hi, cut))] if lo < cut else []) + ([(max(lo, cut) + gap, hi + gap)] if hi > cut else []))
    return out


def _scatter_stages(pos, g_ref, sc_ref, out_ref, sib_buf, send_buf, ici_buf, sem1, sem2s, sem2r, sem3, part=(0, 1),
                    cols=None, own_buf=None, staged=None):
    x, y, cc, chip, sib = pos
    q, n_parts = part
    RH = (g_ref.shape[1] if cols is None else g_ref.shape[0]) // 2 // n_parts
    mine = pl.ds(pl.multiple_of((cc * n_parts + q) * RH, RH), RH)
    theirs = pl.ds(pl.multiple_of(((1 - cc) * n_parts + q) * RH, RH), RH)
    cp1 = pltpu.make_async_remote_copy(
        src_ref=g_ref.at[:, theirs, :] if cols is None else g_ref.at[theirs, :], dst_ref=sib_buf,
        send_sem=sem1.at[0], recv_sem=sem1.at[1], device_id=sib, device_id_type=MESH)
    sends = []
    for k, (dx, dy) in enumerate(REL3):
        px, py = _flip(x, dx), _flip(y, dy)
        sends.append(pltpu.make_async_remote_copy(
            src_ref=send_buf.at[2 * px + py], dst_ref=ici_buf.at[chip],
            send_sem=sem2s.at[k], recv_sem=sem2r.at[k], device_id=(px, py, cc), device_id_type=MESH))
    cp3 = pltpu.make_async_remote_copy(
        src_ref=out_ref.at[mine, :], dst_ref=out_ref.at[mine, :], send_sem=sem3.at[0], recv_sem=sem3.at[1],
        device_id=sib, device_id_type=MESH)

    if staged is not None:
        stage = pltpu.make_async_copy(g_ref.at[mine, :], staged[0], staged[1])

    def start1():
        cp1.start()
        if staged is not None:
            stage.start()

    def finish1():
        cp1.wait()
        if cols is None:
            for a in range(N_CHIPS):
                both = g_ref[a, mine, :] + sib_buf[a]
                sib_buf[a] = both
                send_buf[a] = both.astype(BF16)
        else:
            if staged is not None:
                stage.wait()
            both = (g_ref[mine, :] if staged is None else staged[0][...]) + sib_buf[...]
            for a, pieces in enumerate(cols):
                at = 0
                for lo, hi in pieces:
                    own_buf[a, :, at:at + hi - lo] = both[:, lo:hi]
                    send_buf[a, :, at:at + hi - lo] = both[:, lo:hi].astype(BF16)
                    at += hi - lo

    def start2():
        for cp in sends:
            cp.start()
        ici_buf[chip] = send_buf[chip]

    def finish2():
        for cp in sends:
            cp.wait()
        own = (sib_buf if cols is None else own_buf)[chip]
        parts = [jnp.where(chip == a, own, ici_buf[a].astype(F32)) for a in range(N_CHIPS)]
        total = (parts[0] + parts[1]) + (parts[2] + parts[3])
        out_ref[mine, 0:own.shape[1]] = total if sc_ref is None else total * sc_ref[chip]

    return [(start1, finish1), (start2, finish2), (cp3.start, cp3.wait)]


def _all_reduce_stages(pos, g_ref, out_ref, sib_buf, ici_buf, sem1, sem2s, sem2r, sem3):
    x, y, cc, chip, sib = pos
    RH = g_ref.shape[0] // 2
    mine = pl.ds(pl.multiple_of(cc * RH, 8), RH)
    theirs = pl.ds(pl.multiple_of((1 - cc) * RH, 8), RH)
    cp1 = pltpu.make_async_remote_copy(
        src_ref=g_ref.at[theirs, :], dst_ref=sib_buf, send_sem=sem1.at[0], recv_sem=sem1.at[1],
        device_id=sib, device_id_type=MESH)
    sends = []
    for k, (dx, dy) in enumerate(REL3):
        px, py = _flip(x, dx), _flip(y, dy)
        sends.append(pltpu.make_async_remote_copy(
            src_ref=sib_buf, dst_ref=ici_buf.at[chip],
            send_sem=sem2s.at[k], recv_sem=sem2r.at[k], device_id=(px, py, cc), device_id_type=MESH))
    cp3 = pltpu.make_async_remote_copy(
        src_ref=out_ref.at[mine, :], dst_ref=out_ref.at[mine, :], send_sem=sem3.at[0], recv_sem=sem3.at[1],
        device_id=sib, device_id_type=MESH)

    def finish1():
        cp1.wait()
        sib_buf[...] = g_ref[mine, :] + sib_buf[...]

    def start2():
        for cp in sends:
            cp.start()
        ici_buf[chip] = sib_buf[...]

    def finish2():
        for cp in sends:
            cp.wait()
        out_ref[mine, :] = (ici_buf[0] + ici_buf[1]) + (ici_buf[2] + ici_buf[3])

    return [(cp1.start, finish1), (start2, finish2), (cp3.start, cp3.wait)]


def _stage_sems():
    return [pltpu.SemaphoreType.DMA((2,)), pltpu.SemaphoreType.DMA((3,)),
            pltpu.SemaphoreType.DMA((3,)), pltpu.SemaphoreType.DMA((2,))]


def _scatter_scratch(r, c):
    return [pltpu.VMEM((N_CHIPS, r // 2, c), F32), pltpu.VMEM((N_CHIPS, r // 2, c), BF16),
            pltpu.VMEM((N_CHIPS, r // 2, c), BF16)] + _stage_sems()


def _reduce_all(gw_pad, sc_in, small, dada, c_all):
    R = small.shape[0]
    W = dada.shape[1]
    r_in, p_in = gw_pad.shape
    c_in = SHARD_IN
    chunk = r_in // 4

    def chunk_scratch():
        return ([pltpu.VMEM((chunk, p_in), F32), pltpu.VMEM((N_CHIPS, chunk, c_in), BF16),
                 pltpu.VMEM((N_CHIPS, chunk, c_in), BF16)] + _stage_sems()
                + [pltpu.VMEM((N_CHIPS, chunk, c_in), F32), pltpu.VMEM((chunk, p_in), F32), pltpu.SemaphoreType.DMA])

    n_in = len(chunk_scratch())

    c_wide = -(-c_in // 128) * 128

    def body(gin_ref, scin_ref, sm_ref, d_ref, c_ref, ocols_ref, osm_ref, gwa_ref, gba_ref, loss_ref, oin_ref,
             dall_ref, *scratch):
        x, y, cc = lax.axis_index("x"), lax.axis_index("y"), lax.axis_index("c")
        me = 4 * x + 2 * y + cc
        chip = 2 * x + y
        pos = (x, y, cc, chip, (x, y, 1 - cc))
        oin_ref[:, c_in:c_wide] = jnp.zeros((r_in, c_wide - c_in), F32)
        dslab, ds_sem, dr_sem = scratch[0:3]
        a_bufs, b_bufs, sm_bufs = scratch[3:3 + n_in], scratch[3 + n_in:3 + 2 * n_in], scratch[3 + 2 * n_in:]
        dslab[...] = jnp.broadcast_to(d_ref[...], (8, W))
        dall_ref[me] = dslab[...]
        gathers = []
        for k, (dx, dy, dc) in enumerate(REL7):
            cp = pltpu.make_async_remote_copy(
                src_ref=dslab, dst_ref=dall_ref.at[me], send_sem=ds_sem.at[k], recv_sem=dr_sem.at[k],
                device_id=(_flip(x, dx), _flip(y, dy), _flip(cc, dc)), device_id_type=MESH)
            cp.start()
            gathers.append(cp)
        cols = _shard_cols()
        first = _scatter_stages(pos, gin_ref, scin_ref, oin_ref, *a_bufs[:-3], part=(0, 2), cols=cols,
                                own_buf=a_bufs[-3], staged=a_bufs[-2:])
        second = _scatter_stages(pos, gin_ref, scin_ref, oin_ref, *b_bufs[:-3], part=(1, 2), cols=cols,
                                 own_buf=b_bufs[-3], staged=b_bufs[-2:])
        little = _all_reduce_stages(pos, sm_ref, osm_ref, *sm_bufs)
        for plan in (first, second, little):
            plan[0][0]()
        first[0][1]()
        first[1][0]()
        little[0][1]()
        little[1][0]()
        second[0][1]()
        second[1][0]()
        for cp in gathers:
            cp.wait()
        slab_row = lax.broadcasted_iota(jnp.int32, (8, 1), 0)
        cm = jnp.zeros((8, D), F32)
        dm = jnp.zeros((8, W), F32)
        for r in range(8):
            cm = jnp.where(slab_row == r, c_ref[r], cm)
            dm = jnp.where(slab_row == r, dall_ref[r], dm)
        act = cm * _sigmoid(cm)
        dcol = dm[:, 0:SHARD_ADA]
        for a in range(1, N_CHIPS):
            dcol = jnp.where(chip == a, dm[:, a * SHARD_ADA:(a + 1) * SHARD_ADA], dcol)
        lhs = jnp.concatenate([act, jnp.zeros((8, D), F32)], axis=0).astype(BF16)
        rhs = jnp.concatenate([dcol, jnp.zeros((8, SHARD_ADA), F32)], axis=0).astype(BF16)
        gwa_ref[...] = _dot_tn(lhs, rhs)
        gba_ref[...] = _colsum(dm)
        first[1][1]()
        first[2][0]()
        second[1][1]()
        second[2][0]()
        little[1][1]()
        little[2][0]()
        for plan in (first, second, little):
            plan[2][1]()
        loss_row = SMALL_SEGS["loss"][0]
        loss_ref[...] = osm_ref[loss_row:loss_row + 1, 0:1]
        ocols_ref[...] = oin_ref[...].T[0:c_in, :][:, None, :]

    scratch = [pltpu.VMEM((r_in, c_wide), F32), pltpu.VMEM((8, 8, W), F32),
               pltpu.VMEM((8, W), F32), pltpu.SemaphoreType.DMA((7,)), pltpu.SemaphoreType.DMA((7,))]
    scratch += chunk_scratch() + chunk_scratch()
    scratch += [pltpu.VMEM((R // 2, 128), F32), pltpu.VMEM((N_CHIPS, R // 2, 128), F32)] + _stage_sems()
    vm = pl.BlockSpec(memory_space=pltpu.VMEM)
    return pl.pallas_call(
        body, name="reduce_all",
        out_shape=(jax.ShapeDtypeStruct((c_in, 1, r_in), F32), jax.ShapeDtypeStruct((R, 128), F32),
                   jax.ShapeDtypeStruct((D, SHARD_ADA), F32), jax.ShapeDtypeStruct((1, W), F32),
                   jax.ShapeDtypeStruct((1, 1), F32)),
        in_specs=[pl.BlockSpec(memory_space=pl.ANY)] + [vm] * 4, out_specs=(vm,) * 5,
        scratch_shapes=scratch,
        compiler_params=_params(),
    )(gw_pad, sc_in, small, dada, c_all)


def _in_proj(x, shift, scale, wt_pad, b_pad, w_out_sh):
    S = x.shape[0]
    tm = min(TM_PROJ, S)
    n_steps = S // tm
    assert n_steps >= 3

    def body(x_ref, sh_ref, sc_ref, w_ref, b_ref, wo_ref, u_ref, qkv_ref, f_ref, p_ref, g_ref, wo_all,
             wo_buf, wo_bf, *gather_sems):
        i = pl.program_id(0)
        xx, yy, cc = lax.axis_index("x"), lax.axis_index("y"), lax.axis_index("c")
        row_half = lambda which: (pl.ds(pl.multiple_of(which * (SHARD_OUT // 2), SHARD_OUT // 2), SHARD_OUT // 2),
                                  slice(None))
        start, forward, finish = _gather_stages((xx, yy, cc, 2 * xx + yy, (xx, yy, 1 - cc)), wo_bf, wo_buf,
                                                row_half, *gather_sems)

        @pl.when(i == 0)
        def _():
            wo_bf[...] = wo_ref[...].astype(BF16)
            start()

        pl.when(i == n_steps // 2)(forward)

        @pl.when(i == n_steps - 1)
        def _():
            finish()
            wo_all[...] = wo_buf[...]

        u = (x_ref[...] * (1.0 + sc_ref[...]) + sh_ref[...]).astype(BF16)
        u_ref[...] = u
        qkv_ref[...] = (_dot_nt(u, w_ref[O_QKV:O_F, :]) + b_ref[:, O_QKV:O_F]).astype(BF16)
        f_ref[...] = _dot_nt(u, w_ref[O_F:O_P, :]) + b_ref[:, O_F:O_P]
        p_ref[...] = _dot_nt(u, w_ref[O_P:O_G, :]) + b_ref[:, O_P:O_G]
        g_ref[...] = _dot_nt(u, w_ref[O_G:D_PAD, :]) + b_ref[:, O_G:D_PAD]

    row = lambda w: pl.BlockSpec((tm, w), lambda i: (i, 0))
    full = lambda a: pl.BlockSpec(a.shape, lambda i: (0, 0))
    vm = pl.BlockSpec(memory_space=pltpu.VMEM)
    return pl.pallas_call(
        body, name="in_proj", grid=(n_steps,),
        out_shape=(jax.ShapeDtypeStruct((S, D), BF16), jax.ShapeDtypeStruct((S, 3 * D_ATT), BF16),
                   jax.ShapeDtypeStruct((S, 128), F32), jax.ShapeDtypeStruct((S, D_POOL), F32),
                   jax.ShapeDtypeStruct((S, D), F32), jax.ShapeDtypeStruct((N_CHIPS,) + w_out_sh.shape, BF16)),
        in_specs=[row(D), full(shift), full(scale), full(wt_pad), full(b_pad), vm],
        out_specs=(row(D), row(3 * D_ATT), row(128), row(D_POOL), row(D), vm),
        scratch_shapes=[pltpu.VMEM((N_CHIPS,) + w_out_sh.shape, BF16), pltpu.VMEM(w_out_sh.shape, BF16)]
        + _gather_scratch(),
        compiler_params=_params(dimension_semantics=("arbitrary",)),
    )(x, shift, scale, wt_pad, b_pad, w_out_sh)


def _forget_cumsum(f):
    S = f.shape[0]
    tm = min(T_ATT, S)

    def body(f_ref, out_ref, carry):
        @pl.when(pl.program_id(0) == 0)
        def _():
            carry[...] = jnp.zeros_like(carry)
        v = f_ref[...]
        logf = jnp.minimum(v, 0.0) - jnp.log(1.0 + jnp.exp(-jnp.abs(v)))
        r = lax.broadcasted_iota(jnp.int32, (tm, tm), 0)
        c = lax.broadcasted_iota(jnp.int32, (tm, tm), 1)
        tri = (r <= c).astype(F32)
        rows8 = logf.T[0:8, :]
        cum8 = jnp.dot(rows8, tri, preferred_element_type=F32, precision=lax.Precision.HIGHEST) + carry[...]
        out_ref[...] = jnp.concatenate([cum8, jnp.zeros((128 - 8, tm), F32)], axis=0).T
        last = lax.broadcasted_iota(jnp.int32, (1, tm), 1) == tm - 1
        carry[...] = jnp.sum(jnp.where(last, cum8, 0.0), axis=1, keepdims=True)

    return pl.pallas_call(
        body, name="forget_cumsum", grid=(S // tm,),
        out_shape=jax.ShapeDtypeStruct((S, 128), F32),
        in_specs=[pl.BlockSpec((tm, 128), lambda i: (i, 0))],
        out_specs=pl.BlockSpec((tm, 128), lambda i: (i, 0)),
        scratch_shapes=[pltpu.VMEM((8, 1), F32)],
        compiler_params=_params(dimension_semantics=("arbitrary",)),
    )(f)


def _split3(v):
    hi = v.astype(BF16)
    rest = v - hi.astype(F32)
    mid = rest.astype(BF16)
    lo = (rest - mid.astype(F32)).astype(BF16)
    return hi, mid, lo


def _attention_fwd(qkv, big_f):
    S = qkv.shape[0]
    T = min(T_ATT, S)
    n_t = S // T

    def body(q_ref, k_ref, v_ref, f_ref, o_ref, lse_ref, kaug_sc, vt_sc, m_sc, l_sc, acc_sc):
        hp = pl.program_id(0)
        i = pl.program_id(1)
        lane = lax.broadcasted_iota(jnp.int32, (1, 128), 1)
        sub = lax.broadcasted_iota(jnp.int32, (128, 1), 0)
        head_sel = (lane < HEAD_DIM, lane >= HEAD_DIM)
        head_sel_t = (sub < HEAD_DIM, sub >= HEAD_DIM)
        spare = (HEAD_DIM, 0)
        zero = jnp.zeros((), BF16)

        @pl.when(i == 0)
        def _():
            def prep(jt, carry):
                rows = pl.ds(pl.multiple_of(jt * T, T), T)
                k = k_ref[rows, :]
                ft = f_ref[rows, :]
                vt = v_ref[rows, :].astype(F32).T
                for h in range(2):
                    fh = jnp.sum(jnp.where(lane == 2 * hp + h, ft, 0.0), axis=1, keepdims=True)
                    hi, mid, lo = _split3(-fh)
                    b = spare[h]
                    bias = jnp.where(lane == b, hi, jnp.where(lane == b + 1, mid, jnp.where(lane == b + 2, lo, zero)))
                    kaug_sc[h, rows, :] = jnp.where(head_sel[h], k, bias)
                    vt_sc[h, jt] = jnp.where(head_sel_t[h], vt, 0.0).astype(BF16)
                return carry

            lax.fori_loop(0, n_t, prep, 0)

        q = q_ref[...]
        q_heads = []
        for h in range(2):
            ones = jnp.where((lane >= spare[h]) & (lane < spare[h] + 3), jnp.ones((), BF16), zero)
            q_heads.append(jnp.where(head_sel[h], q, ones))
        m_sc[...] = jnp.full((8, T), NEG, F32)
        l_sc[...] = jnp.zeros((8, T), F32)
        acc_sc[...] = jnp.zeros((128, T), F32)

        def update(j, k_lo, n_k, q_lo, masked):
            rows = pl.ds(pl.multiple_of(j * T + k_lo, n_k), n_k)
            n_q = T - q_lo
            alphas, pvs = [], []
            for h in range(2):
                s_t = _dot_nt(kaug_sc[h, rows, :], q_heads[h][q_lo:, :])
                if masked:
                    rr = lax.broadcasted_iota(jnp.int32, (n_k, n_q), 0) + k_lo
                    cc = lax.broadcasted_iota(jnp.int32, (n_k, n_q), 1) + q_lo
                    s_t = jnp.where(rr <= cc, s_t, NEG)
                m_prev = m_sc[h:h + 1, q_lo:]
                m_new = jnp.maximum(m_prev, jnp.max(s_t, axis=0, keepdims=True))
                alpha = jnp.exp(m_prev - m_new)
                p_t = jnp.exp(s_t - m_new)
                l_sc[h:h + 1, q_lo:] = alpha * l_sc[h:h + 1, q_lo:] + jnp.sum(p_t, axis=0, keepdims=True)
                m_sc[h:h + 1, q_lo:] = m_new
                alphas.append(alpha)
                pvs.append(_dot(vt_sc[h, j, :, k_lo:k_lo + n_k], p_t.astype(BF16)))
            acc_sc[:, q_lo:] = (acc_sc[:, q_lo:] * jnp.where(head_sel_t[0], alphas[0], alphas[1])
                                + (pvs[0] + pvs[1]))

        def two_off_diagonal(jj, carry):
            update(2 * jj, 0, T, 0, False)
            update(2 * jj + 1, 0, T, 0, False)
            return carry

        lax.fori_loop(0, i // 2, two_off_diagonal, 0)

        @pl.when(i % 2 == 1)
        def _():
            update(i - 1, 0, T, 0, False)
            update(i, 0, T, 0, True)

        @pl.when(i % 2 == 0)
        def _():
            update(i, 0, T, 0, True)

        l = l_sc[...]
        o_ref[...] = (acc_sc[...] / jnp.where(head_sel_t[0], l[0:1, :], l[1:2, :])).T
        is_head = lax.broadcasted_iota(jnp.int32, (8, 1), 0) < 2
        lse_ref[...] = jnp.where(is_head, m_sc[...] + jnp.log(jnp.where(is_head, l, 1.0)), 0.0)

    return pl.pallas_call(
        body, name="attention_fwd", grid=(N_PAIR, n_t),
        out_shape=(jax.ShapeDtypeStruct((S, D_ATT), F32), jax.ShapeDtypeStruct((N_PAIR, n_t, 8, T), F32)),
        in_specs=[pl.BlockSpec((T, 128), lambda hp, i: (i, hp)),
                  pl.BlockSpec((S, 128), lambda hp, i: (0, N_PAIR + hp)),
                  pl.BlockSpec((S, 128), lambda hp, i: (0, 2 * N_PAIR + hp)),
                  pl.BlockSpec((S, 128), lambda hp, i: (0, 0))],
        out_specs=(pl.BlockSpec((T, 128), lambda hp, i: (i, hp)),
                   pl.BlockSpec((None, None, 8, T), lambda hp, i: (hp, i, 0, 0))),
        scratch_shapes=[pltpu.VMEM((2, S, 128), BF16), pltpu.VMEM((2, n_t, 128, T), BF16),
                        pltpu.VMEM((8, T), F32), pltpu.VMEM((8, T), F32), pltpu.VMEM((128, T), F32)],
        compiler_params=_params(dimension_semantics=("arbitrary", "arbitrary")),
    )(qkv, qkv, qkv, big_f)


def _attention_bwd(qkv, datt, att, lse, big_f, gw_out4):
    S = qkv.shape[0]
    T = min(T_ATT, S)
    n_t = S // T
    n_steps = N_PAIR * n_t
    marks = (0, n_steps // 8, n_steps // 2, n_steps // 2 + n_steps // 8)

    def body(q_ref, do_ref, o_ref, lse_ref, k_ref, v_ref, fk_ref, gout_ref,
             dq_ref, dk_ref, dv_ref, cs_ref, dfk_ref, dfq_ref, oout_ref, stat_sc, dqt_sc, qaug_sc,
             out_buf, *red_bufs):
        hp = pl.program_id(0)
        j = pl.program_id(1)
        x, y, cc = lax.axis_index("x"), lax.axis_index("y"), lax.axis_index("c")
        plan = _scatter_stages((x, y, cc, 2 * x + y, (x, y, 1 - cc)), gout_ref, None, out_buf, *red_bufs)
        step = hp * n_t + j
        for n, mark in enumerate(marks):
            @pl.when(step == mark)
            def _(n=n):
                if n > 0:
                    plan[n - 1][1]()
                if n < 3:
                    plan[n][0]()
                else:
                    oout_ref[...] = out_buf[...]

        lane = lax.broadcasted_iota(jnp.int32, (1, 128), 1)
        sub = lax.broadcasted_iota(jnp.int32, (128, 1), 0)
        head_sel = (lane < HEAD_DIM, lane >= HEAD_DIM)
        head_sel_t = (sub < HEAD_DIM, sub >= HEAD_DIM)
        spare = (HEAD_DIM, 0)
        zero = jnp.zeros((), BF16)
        one = jnp.ones((), BF16)

        def bias_lanes(first, pieces):
            hi, mid, lo = pieces
            return lambda rest: jnp.where(lane == first, hi, jnp.where(lane == first + 1, mid,
                                                                        jnp.where(lane == first + 2, lo, rest)))

        @pl.when(j == 0)
        def _():
            dqt_sc[...] = jnp.zeros_like(dqt_sc)
            cs_ref[...] = jnp.zeros_like(cs_ref)
            dfq_ref[...] = jnp.zeros_like(dfq_ref)

            def prep(i, carry):
                rows = pl.ds(pl.multiple_of(i * T, T), T)
                q = q_ref[rows, :]
                do = do_ref[rows, :]
                prod = o_ref[rows, :] * do.astype(F32)
                d_a = jnp.sum(jnp.where(head_sel[0], prod, 0.0), axis=1, keepdims=True)
                d_b = jnp.sum(jnp.where(head_sel[0], 0.0, prod), axis=1, keepdims=True)
                delta_t = jnp.where(head_sel[0], d_a, d_b).T
                stat_sc[i, 0:1, :] = delta_t[0:1, :]
                stat_sc[i, 1:2, :] = delta_t[HEAD_DIM:HEAD_DIM + 1, :]
                lse = lse_ref[i]
                lse_cols = jnp.where(head_sel_t[0], lse[0:1, :], lse[1:2, :]).T
                for h in range(2):
                    neg_lse = -lse_cols[:, h * HEAD_DIM:h * HEAD_DIM + 1]
                    ones = jnp.where((lane >= spare[h]) & (lane < spare[h] + 3), one, zero)
                    qaug_sc[h, rows, :] = jnp.where(head_sel[h], q, bias_lanes(spare[h] + 3, _split3(neg_lse))(ones))
                return carry

            lax.fori_loop(0, n_t, prep, 0)

        k = k_ref[...]
        v = v_ref[...]
        fk = fk_ref[...]
        kt = k.astype(F32).T
        heads = []
        for h in range(2):
            fkh = jnp.sum(jnp.where(lane == 2 * hp + h, fk, 0.0), axis=1, keepdims=True)
            ones = jnp.where((lane >= spare[h] + 3) & (lane < spare[h] + 6), one, zero)
            kaug = jnp.where(head_sel[h], k, bias_lanes(spare[h], _split3(-fkh))(ones))
            heads.append((kaug, jnp.where(head_sel[h], v, zero), jnp.where(head_sel_t[h], kt, 0.0).astype(BF16)))

        def block(i, k_lo, n_k, q_lo, masked):
            n_q = T - q_lo
            rows = pl.ds(pl.multiple_of(i * T + q_lo, n_q), n_q)
            q = q_ref[rows, :]
            do = do_ref[rows, :]
            stat = stat_sc[i]
            dk = jnp.zeros((n_k, 128), F32)
            dv = jnp.zeros((n_k, 128), F32)
            dqt = jnp.zeros((128, n_q), F32)
            dfs = []
            for h in range(2):
                kaug, vh, kth = heads[h]
                arg = _dot_nt(kaug[k_lo:k_lo + n_k, :], qaug_sc[h, rows, :])
                if masked:
                    rr = lax.broadcasted_iota(jnp.int32, (n_k, n_q), 0) + k_lo
                    cc = lax.broadcasted_iota(jnp.int32, (n_k, n_q), 1) + q_lo
                    arg = jnp.where(rr <= cc, arg, NEG)
                p_t = jnp.exp(arg)
                ds_t = p_t * (_dot_nt(vh[k_lo:k_lo + n_k, :], do) - stat[h:h + 1, q_lo:])
                ds_bf = ds_t.astype(BF16)
                dv = dv + _dot(p_t.astype(BF16), jnp.where(head_sel[h], do, zero))
                dk = dk + _dot(ds_bf, jnp.where(head_sel[h], q, zero))
                dqt = dqt + _dot(kth[:, k_lo:k_lo + n_k], ds_bf)
                dfs.append(jnp.sum(ds_t, axis=1, keepdims=True))
                dfq_ref[i, h:h + 1, q_lo:] += _colsum(ds_t)
            dqt_sc[i, :, q_lo:] += dqt
            return dk, dv, dfs[0], dfs[1]

        def off_diagonal(i, acc):
            return tuple(a + b for a, b in zip(acc, block(i, 0, T, 0, False)))

        half = T // 2
        early = block(j, 0, half, 0, True)
        late = block(j, half, half, half, True)
        acc1 = tuple(jnp.concatenate([a, b], axis=0) for a, b in zip(early, late))
        n_off = n_t - 1 - j
        acc2 = lax.fori_loop(0, n_off // 2,
                             lambda ii, a: off_diagonal(j + 2 + 2 * ii, off_diagonal(j + 1 + 2 * ii, a)), acc1)
        dk_acc, dv_acc, dfa, dfb = lax.fori_loop(0, n_off % 2, lambda _, a: off_diagonal(n_t - 1, a), acc2)
        dk_ref[...] = dk_acc.astype(BF16)
        dv_ref[...] = dv_acc.astype(BF16)
        dfk_ref[...] = -jnp.where(lane == 0, dfa, jnp.where(lane == 1, dfb, 0.0))
        cs_ref[:, 128:256] = cs_ref[:, 128:256] + _colsum(dk_acc)
        cs_ref[:, 256:384] = cs_ref[:, 256:384] + _colsum(dv_acc)

        @pl.when(j == n_t - 1)
        def _():
            def finish(i, tot):
                dq = dqt_sc[i].T
                dq_ref[pl.ds(pl.multiple_of(i * T, T), T), :] = dq.astype(BF16)
                return tot + _colsum(dq)

            cs_ref[:, 0:128] = lax.fori_loop(0, n_t, finish, jnp.zeros((1, 128), F32))

    pair_rows = lambda hp, j: (hp, 0, 0)
    vm = pl.BlockSpec(memory_space=pltpu.VMEM)
    _, r_out, c_out = gw_out4.shape
    return pl.pallas_call(
        body, name="attention_bwd", grid=(N_PAIR, n_t),
        out_shape=(jax.ShapeDtypeStruct((S, D_ATT), BF16), jax.ShapeDtypeStruct((S, D_ATT), BF16),
                   jax.ShapeDtypeStruct((S, D_ATT), BF16), jax.ShapeDtypeStruct((N_PAIR, 1, 384), F32),
                   jax.ShapeDtypeStruct((N_PAIR, S, 128), F32),
                   jax.ShapeDtypeStruct((N_PAIR, n_t, 8, T), F32),
                   jax.ShapeDtypeStruct((r_out, c_out), F32)),
        in_specs=[pl.BlockSpec((S, 128), lambda hp, j: (0, hp)),
                  pl.BlockSpec((S, 128), lambda hp, j: (0, hp)),
                  pl.BlockSpec((S, 128), lambda hp, j: (0, hp)),
                  pl.BlockSpec((None, n_t, 8, T), lambda hp, j: (hp, 0, 0, 0)),
                  pl.BlockSpec((T, 128), lambda hp, j: (j, N_PAIR + hp)),
                  pl.BlockSpec((T, 128), lambda hp, j: (j, 2 * N_PAIR + hp)),
                  pl.BlockSpec((T, 128), lambda hp, j: (j, 0)),
                  vm],
        out_specs=(pl.BlockSpec((S, 128), lambda hp, j: (0, hp)),
                   pl.BlockSpec((T, 128), lambda hp, j: (j, hp)),
                   pl.BlockSpec((T, 128), lambda hp, j: (j, hp)),
                   pl.BlockSpec((None, 1, 384), pair_rows),
                   pl.BlockSpec((None, T, 128), lambda hp, j: (hp, j, 0)),
                   pl.BlockSpec((None, n_t, 8, T), lambda hp, j: (hp, 0, 0, 0)),
                   vm),
        scratch_shapes=[pltpu.VMEM((n_t, 8, T), F32), pltpu.VMEM((n_t, 128, T), F32),
                        pltpu.VMEM((2, S, 128), BF16), pltpu.VMEM((r_out, c_out), F32)]
        + _scatter_scratch(r_out, c_out),
        compiler_params=_params(dimension_semantics=("arbitrary", "arbitrary")),
    )(qkv, datt, att, lse, qkv, qkv, big_f, gw_out4)


def _window_counts(first_row, n_rows, window):
    t = lax.broadcasted_iota(jnp.int32, (n_rows, 1), 0) + first_row
    return jnp.minimum((t + 1).astype(F32), float(window))


def _middle(x, tgt, att, g, p, gate, w_mix, b_mix, pool_scale, w_out, b_out, ln_g, ln_b):
    S = x.shape[0]
    tm = min(TM_MID, S)
    halo_blocks = tm // POOL_HALO

    def body(x_ref, t_ref, att_ref, g_ref, p_ref, ph_ref, gate_ref, wm_ref, bm_ref, ps_ref, wo_ref, bo_ref,
             lg_ref, lb_ref,
             dh_ref, datt_ref, dg_ref, dpl_ref, gwo_ref, gwm_ref, vec_ref, loss_ref):
        i = pl.program_id(0)

        @pl.when(i == 0)
        def _():
            gwo_ref[...] = jnp.zeros_like(gwo_ref)
            gwm_ref[...] = jnp.zeros_like(gwm_ref)
            vec_ref[...] = jnp.zeros_like(vec_ref)
            loss_ref[...] = jnp.zeros_like(loss_ref)

        pc = p_ref[...]
        halo = jnp.where(i > 0, ph_ref[...], 0.0)
        pe = jnp.concatenate([halo, pc], axis=0)
        pooled_parts = []
        for gi, w in enumerate(POOL_WINDOWS):
            cur = pe[:, gi * POOL_GROUP:(gi + 1) * POOL_GROUP]
            span = 1
            while span < w:
                cur = cur + pltpu.roll(cur, span, 0)
                span *= 2
            wsum = cur[POOL_HALO:, :]
            mean = wsum / _window_counts(i * tm, tm, w)
            pooled_parts.append(mean - pc[:, gi * POOL_GROUP:(gi + 1) * POOL_GROUP])
        pooled_bf =[v.astype(BF16) for v in pooled_parts]
        wm = [wm_ref[gi].astype(BF16) for gi in range(4)]
        mixed = jnp.concatenate([_dot(pooled_bf[gi], wm[gi]) for gi in range(4)], axis=1) + bm_ref[...]
        ps = ps_ref[...]
        pool_out = mixed * ps
        gv = g_ref[...]
        sig = _sigmoid(gv)
        silu = gv * sig
        att = att_ref[...]
        y = jnp.concatenate([att * silu[:, :D_ATT], pool_out * silu[:, D_ATT:]], axis=1)
        y_bf = y.astype(BF16)
        wo = wo_ref[...]
        yo = _dot(y_bf, wo) + bo_ref[...]
        gate = gate_ref[...]
        h = ALPHA * x_ref[...] + gate * yo
        mu = jnp.mean(h, axis=1, keepdims=True)
        hc = h - mu
        var = jnp.mean(hc * hc, axis=1, keepdims=True)
        rstd = lax.rsqrt(var + LN_EPS)
        yhat = hc * rstd
        lg = lg_ref[...]
        out = yhat * lg + lb_ref[...]
        err = out - t_ref[...]
        loss_ref[...] += 0.5 * jnp.sum(jnp.mean(err * err, axis=1, keepdims=True), axis=0, keepdims=True)

        dout = err * (1.0 / D)
        g_ln_b = _colsum(dout)
        g_ln_g = _colsum(dout * yhat)
        dyh = dout * lg
        dh = rstd * (dyh - jnp.mean(dyh, axis=1, keepdims=True)
                     - yhat * jnp.mean(dyh * yhat, axis=1, keepdims=True))
        dh_ref[...] = dh
        d_gate = _colsum(dh * yo)
        dyo = gate * dh
        g_b_out = _colsum(dyo)
        dyo_bf = dyo.astype(BF16)
        gwo_ref[...] += _dot_tn(y_bf, dyo_bf)
        dy = _dot_nt(dyo_bf, wo)
        dsilu = sig * (1.0 + gv * (1.0 - sig))
        dy_a = dy[:, :D_ATT]
        dy_p = dy[:, D_ATT:]
        datt_ref[...] = (dy_a * silu[:, :D_ATT]).astype(BF16)
        dpo = dy_p * silu[:, D_ATT:]
        dg = jnp.concatenate([dy_a * att * dsilu[:, :D_ATT], dy_p * pool_out * dsilu[:, D_ATT:]], axis=1)
        dg_ref[...] = dg.astype(BF16)
        g_dg = _colsum(dg)
        g_ps = _colsum(dpo * mixed)
        dmixed = dpo * ps
        g_bm = _colsum(dmixed)
        dmixed_bf = dmixed.astype(BF16)
        dpl = []
        for gi in range(4):
            dm = dmixed_bf[:, gi * POOL_GROUP:(gi + 1) * POOL_GROUP]
            gwm_ref[gi] += _dot_tn(pooled_bf[gi], dm)
            dpl.append(_dot_nt(dm, wm[gi]))
        dpl_ref[...] = jnp.concatenate(dpl, axis=1)
        vec_ref[0:1, :] += g_ln_g
        vec_ref[1:2, :] += g_ln_b
        vec_ref[2:3, :] += d_gate
        vec_ref[3:4, :] += g_b_out
        vec_ref[4:5, :] += g_dg
        vec_ref[5:6, 0:D_POOL] += g_ps
        vec_ref[6:7, 0:D_POOL] += g_bm

    row = lambda w: pl.BlockSpec((tm, w), lambda i: (i, 0))
    full2 = lambda a: pl.BlockSpec(a.shape, lambda i: (0, 0))
    full3 = lambda a: pl.BlockSpec(a.shape, lambda i: (0, 0, 0))
    return pl.pallas_call(
        body, name="middle", grid=(S // tm,),
        out_shape=(jax.ShapeDtypeStruct((S, D), F32),
                   jax.ShapeDtypeStruct((S, D_ATT), BF16),
                   jax.ShapeDtypeStruct((S, D), BF16),
                   jax.ShapeDtypeStruct((S, D_POOL), F32),
                   jax.ShapeDtypeStruct((D, D), F32),
                   jax.ShapeDtypeStruct((4, POOL_GROUP, POOL_GROUP), F32),
                   jax.ShapeDtypeStruct((8, D), F32),
                   jax.ShapeDtypeStruct((1, 1), F32)),
        in_specs=[row(D), row(D), row(D_ATT), row(D), row(D_POOL),
                  pl.BlockSpec((POOL_HALO, D_POOL), lambda i: (jnp.maximum(i * halo_blocks - 1, 0), 0)),
                  full2(gate), full3(w_mix), full2(b_mix), full2(pool_scale), full2(w_out), full2(b_out),
                  full2(ln_g), full2(ln_b)],
        out_specs=(row(D), row(D_ATT), row(D), row(D_POOL),
                   pl.BlockSpec((D, D), lambda i: (0, 0)),
                   pl.BlockSpec((4, POOL_GROUP, POOL_GROUP), lambda i: (0, 0, 0)),
                   pl.BlockSpec((8, D), lambda i: (0, 0)),
                   pl.BlockSpec((1, 1), lambda i: (0, 0))),
        compiler_params=_params(dimension_semantics=("arbitrary",)),
    )(x, tgt, att, g, p, p, gate, w_mix, b_mix, pool_scale, w_out, b_out, ln_g, ln_b)


def _tail(dpl, dfk, dfq, f):
    S = dpl.shape[0]
    tm = min(T_ATT, S)
    n_t = S // tm
    halo_blocks = tm // POOL_HALO
    last_halo = S // POOL_HALO - 1

    def body(d_ref, dn_ref, dfk_ref, dfq_ref, f_ref, dp_ref, df_ref, cs_ref, carry):
        s = pl.program_id(0)
        i = n_t - 1 - s

        @pl.when(s == 0)
        def _():
            carry[...] = jnp.zeros_like(carry)
            cs_ref[...] = jnp.zeros_like(cs_ref)

        dc = d_ref[...]
        nxt = jnp.where(s > 0, dn_ref[...], 0.0)
        de = jnp.concatenate([dc, nxt], axis=0)
        n_e = tm + POOL_HALO
        parts = []
        for gi, w in enumerate(POOL_WINDOWS):
            cur = de[:, gi * POOL_GROUP:(gi + 1) * POOL_GROUP] / _window_counts(i * tm, n_e, w)
            span = 1
            while span < w:
                cur = cur + pltpu.roll(cur, n_e - span, 0)
                span *= 2
            parts.append(cur[:tm, :] - dc[:, gi * POOL_GROUP:(gi + 1) * POOL_GROUP])
        dp = jnp.concatenate(parts, axis=1)
        dp_ref[...] = dp.astype(BF16)
        cs_ref[0:1, :] += _colsum(dp)

        r = lax.broadcasted_iota(jnp.int32, (tm, tm), 0)
        c = lax.broadcasted_iota(jnp.int32, (tm, tm), 1)
        tri = (r >= c).astype(F32)
        k_cols = dfk_ref[0]
        rows8 = dfq_ref[0]
        for hp in range(1, N_PAIR):
            k_cols = k_cols + pltpu.roll(dfk_ref[hp], 2 * hp, 1)
            rows8 = rows8 + pltpu.roll(dfq_ref[hp], 2 * hp, 0)
        rows8 = rows8 + k_cols.T[0:8, :]
        dlogf8 = jnp.dot(rows8, tri, preferred_element_type=F32, precision=lax.Precision.HIGHEST) + carry[...]
        first = lax.broadcasted_iota(jnp.int32, (1, tm), 1) == 0
        carry[...] = jnp.sum(jnp.where(first, dlogf8, 0.0), axis=1, keepdims=True)
        dlogf = jnp.concatenate([dlogf8, jnp.zeros((128 - 8, tm), F32)], axis=0).T
        df = dlogf * _sigmoid(-f_ref[...])
        df_ref[...] = df.astype(BF16)
        cs_ref[1:2, 0:128] += _colsum(df)

    rev = lambda w: pl.BlockSpec((tm, w), lambda s: (n_t - 1 - s, 0))
    return pl.pallas_call(
        body, name="tail", grid=(n_t,),
        out_shape=(jax.ShapeDtypeStruct((S, D_POOL), BF16), jax.ShapeDtypeStruct((S, 128), BF16),
                   jax.ShapeDtypeStruct((8, D_POOL), F32)),
        in_specs=[rev(D_POOL),
                  pl.BlockSpec((POOL_HALO, D_POOL),
                               lambda s: (jnp.minimum((n_t - s) * halo_blocks, last_halo), 0)),
                  pl.BlockSpec((N_PAIR, tm, 128), lambda s: (0, n_t - 1 - s, 0)),
                  pl.BlockSpec((N_PAIR, None, 8, tm), lambda s: (0, n_t - 1 - s, 0, 0)),
                  rev(128)],
        out_specs=(rev(D_POOL), rev(128), pl.BlockSpec((8, D_POOL), lambda s: (0, 0))),
        scratch_shapes=[pltpu.VMEM((8, 1), F32)],
        compiler_params=_params(dimension_semantics=("arbitrary",)),
    )(dpl, dpl, dfk, dfq, f)


PIECES = ((O_QKV, D_ATT), (O_QKV + D_ATT, D_ATT), (O_QKV + 2 * D_ATT, D_ATT), (O_F, 128), (O_P, D_POOL), (O_G, D))


def _grad_w_in(u, pieces):
    S = u.shape[0]
    tm = min(TM_GW, S)
    n_t = S // tm

    def body(u_ref, *rest):
        piece_refs, out_ref, acc, sem = rest[:6], rest[6], rest[7], rest[8]
        i = pl.program_id(0)

        @pl.when(i == 0)
        def _():
            acc[...] = jnp.zeros_like(acc)

        last = i == n_t - 1
        copies = [pltpu.make_async_copy(acc.at[:, off:off + w], out_ref.at[:, off:off + w], sem.at[k])
                  for k, (off, w) in enumerate(PIECES)]
        u_t = u_ref[...]
        for k, ((off, w), ref) in enumerate(zip(PIECES, piece_refs)):
            acc[:, off:off + w] += _dot_tn(u_t, ref[...])
            pl.when(last)(copies[k].start)

        @pl.when(last)
        def _():
            for cp in copies:
                cp.wait()

    return pl.pallas_call(
        body, name="grad_w_in", grid=(n_t,),
        out_shape=jax.ShapeDtypeStruct((D, D_PAD), F32),
        in_specs=[pl.BlockSpec((tm, D), lambda i: (i, 0))]
        + [pl.BlockSpec((tm, w), lambda i: (i, 0)) for _, w in PIECES],
        out_specs=pl.BlockSpec(memory_space=pl.ANY),
        scratch_shapes=[pltpu.VMEM((D, D_PAD), F32), pltpu.SemaphoreType.DMA((len(PIECES),))],
        compiler_params=_params(dimension_semantics=("arbitrary",)),
    )(u, *pieces)


def _grad_x(pieces, wt_pad, dh, x, scale):
    S = x.shape[0]
    tm = min(TM_DU, S)

    def body(*refs):
        piece_refs = refs[:6]
        w_ref, dh_ref, x_ref, sc_ref, gx_ref, vec_ref = refs[6:]

        @pl.when(pl.program_id(0) == 0)
        def _():
            vec_ref[...] = jnp.zeros_like(vec_ref)

        du = jnp.zeros((tm, D), F32)
        for (off, w), ref in zip(PIECES, piece_refs):
            du = du + _dot(ref[...], w_ref[off:off + w, :])
        xv = x_ref[...]
        gx_ref[...] = ALPHA * dh_ref[...] + du * (1.0 + sc_ref[...])
        vec_ref[0:1, :] += _colsum(du)
        vec_ref[1:2, :] += _colsum(du * xv)

    row = lambda w: pl.BlockSpec((tm, w), lambda i: (i, 0))
    return pl.pallas_call(
        body, name="grad_x", grid=(S // tm,),
        out_shape=(jax.ShapeDtypeStruct((S, D), F32), jax.ShapeDtypeStruct((8, D), F32)),
        in_specs=[row(w) for _, w in PIECES]
        + [pl.BlockSpec(wt_pad.shape, lambda i: (0, 0)), row(D), row(D), pl.BlockSpec((1, D), lambda i: (0, 0))],
        out_specs=(row(D), pl.BlockSpec((8, D), lambda i: (0, 0))),
        compiler_params=_params(dimension_semantics=("arbitrary",)),
    )(*pieces, wt_pad, dh, x, scale)


def _grad_in(u, pieces, wt_pad, dh, x, scale):
    S = x.shape[0]
    tm = min(TM_DU, S)
    n_t = S // tm

    def body(u_ref, *refs):
        piece_refs = refs[:6]
        w_ref, dh_ref, x_ref, sc_ref, gx_ref, vec_ref, gw_ref, acc, sem = refs[6:]
        i = pl.program_id(0)
        last = i == n_t - 1

        @pl.when(i == 0)
        def _():
            vec_ref[...] = jnp.zeros_like(vec_ref)
            acc[...] = jnp.zeros_like(acc)

        copies = [pltpu.make_async_copy(acc.at[:, off:off + w], gw_ref.at[:, off:off + w], sem.at[k])
                  for k, (off, w) in enumerate(PIECES)]
        u_t = u_ref[...]
        du = jnp.zeros((tm, D), F32)
        for k, ((off, w), ref) in enumerate(zip(PIECES, piece_refs)):
            piece = ref[...]
            du = du + _dot(piece, w_ref[off:off + w, :])
            acc[:, off:off + w] += _dot_tn(u_t, piece)
            pl.when(last)(copies[k].start)
        gx_ref[...] = ALPHA * dh_ref[...] + du * (1.0 + sc_ref[...])
        vec_ref[0:1, :] += _colsum(du)
        vec_ref[1:2, :] += _colsum(du * x_ref[...])

        @pl.when(last)
        def _():
            for cp in copies:
                cp.wait()

    row = lambda w: pl.BlockSpec((tm, w), lambda i: (i, 0))
    return pl.pallas_call(
        body, name="grad_in", grid=(n_t,),
        out_shape=(jax.ShapeDtypeStruct((S, D), F32), jax.ShapeDtypeStruct((8, D), F32),
                   jax.ShapeDtypeStruct((D, D_PAD), F32)),
        in_specs=[row(D)] + [row(w) for _, w in PIECES]
        + [pl.BlockSpec(wt_pad.shape, lambda i: (0, 0)), row(D), row(D), pl.BlockSpec((1, D), lambda i: (0, 0))],
        out_specs=(row(D), pl.BlockSpec((8, D), lambda i: (0, 0)), pl.BlockSpec(memory_space=pl.ANY)),
        scratch_shapes=[pltpu.VMEM((D, D_PAD), F32), pltpu.SemaphoreType.DMA((len(PIECES),))],
        compiler_params=_params(dimension_semantics=("arbitrary",)),
    )(u, *pieces, wt_pad, dh, x, scale)


def _adamw_math(w, g, m, v):
    m = ADAM_B1 * m + (1.0 - ADAM_B1) * g
    v = ADAM_B2 * v + (1.0 - ADAM_B2) * (g * g)
    m_hat = m / (1.0 - ADAM_B1 ** ADAM_STEP)
    v_hat = v / (1.0 - ADAM_B2 ** ADAM_STEP)
    delta = -ADAM_LR * (m_hat / (jnp.sqrt(v_hat) + ADAM_EPS) + ADAM_WD * w)
    return delta, m, v


def _adamw(groups, n_steps):
    n = len(groups)

    def body(*refs):
        ins, outs = refs[:4 * n], refs[4 * n:]
        for t in range(n):
            w, g, m, v = (r[...] for r in ins[4 * t:4 * t + 4])
            d, m2, v2 = _adamw_math(w, g, m, v)
            outs[4 * t][...] = d
            outs[4 * t + 1][...] = m2
            outs[4 * t + 2][...] = v2
            outs[4 * t + 3][...] = g

    in_specs, out_specs, out_shape, args = [], [], [], []
    for (w, g, m, v) in groups:
        rest = w.shape[1:]
        spec = pl.BlockSpec((w.shape[0] // n_steps,) + rest, lambda i, nd=len(rest): (i,) + (0,) * nd)
        in_specs += [spec] * 4
        out_specs += [spec] * 4
        out_shape += [jax.ShapeDtypeStruct(w.shape, F32)] * 4
        args += [w, g, m, v]
    return pl.pallas_call(
        body, name="adamw_%d_%d" % (n, n_steps), grid=(n_steps,),
        out_shape=tuple(out_shape), in_specs=in_specs, out_specs=tuple(out_specs),
        compiler_params=_params(dimension_semantics=("arbitrary",)),
    )(*args)


def _adamw_small(small_sum, g_b_ada, params):
    n = len(params)

    def body(gs_ref, gba_ref, *refs):
        ins, outs = refs[:3 * n], refs[3 * n:]
        for t, (name, w0, _, _) in enumerate(params):
            w_ref, m_ref, v_ref = ins[3 * t:3 * t + 3]
            first = SMALL_SEGS[name][0] if name in SMALL_SEGS else None
            if w0.shape[0] > 1:
                pieces = [((slice(None), slice(None)), gs_ref[first:first + w0.shape[0], :])]
            else:
                pieces = []
                for r in range(-(-w0.shape[1] // 128)):
                    lanes = slice(128 * r, min(128 * r + 128, w0.shape[1]))
                    g = gba_ref[0:1, lanes] if first is None else gs_ref[first + r:first + r + 1, 0:lanes.stop - lanes.start]
                    pieces.append(((slice(0, 1), lanes), g))
            for where, g in pieces:
                d, m2, v2 = _adamw_math(w_ref[where], g, m_ref[where], v_ref[where])
                for ref, val in zip(outs[4 * t:4 * t + 4], (g, d, m2, v2)):
                    ref[where] = val

    vm = pl.BlockSpec(memory_space=pltpu.VMEM)
    args = [small_sum, g_b_ada]
    out_shape = []
    for _, w, m, v in params:
        args += [w, m, v]
        out_shape += [jax.ShapeDtypeStruct(w.shape, F32)] * 4
    return pl.pallas_call(
        body, name="adamw_small",
        out_shape=tuple(out_shape), in_specs=[vm] * len(args), out_specs=(vm,) * len(out_shape),
        compiler_params=_params(),
    )(*args)


def _pack_small(parts):
    rows = []
    used = 0
    for name, (first, n_rows) in SMALL_SEGS.items():
        if first > used:
            rows.append(jnp.zeros((first - used, 128), F32))
        flat = parts[name].reshape(-1)
        flat = jnp.pad(flat, (0, n_rows * 128 - flat.shape[0]))
        rows.append(flat.reshape(n_rows, 128))
        used = first + n_rows
    rows.append(jnp.zeros((SMALL_ROWS - used, 128), F32))
    return jnp.concatenate(rows, axis=0)


def _pad_in(v):
    r = v.shape[0]
    z = jnp.zeros((r, O_P - O_F - N_HEADS), v.dtype)
    return jnp.concatenate([v[:, :3 * D_ATT + N_HEADS], z, v[:, 3 * D_ATT + N_HEADS:]], axis=1)


def _unpad_in(v):
    return jnp.concatenate([v[:, :O_F + N_HEADS], v[:, O_P:]], axis=1)


def _shards_in(v):
    gap = O_P - (O_F + N_HEADS)
    parts = []
    for a in range(N_CHIPS):
        lo, hi = a * SHARD_IN, (a + 1) * SHARD_IN
        cut = O_F + N_HEADS
        if hi <= cut:
            parts.append(v[:, lo:hi])
        elif lo >= cut:
            parts.append(v[:, lo + gap:hi + gap])
        else:
            parts.append(jnp.concatenate([v[:, lo:cut], v[:, cut + gap:hi + gap]], axis=1))
    return jnp.stack(parts, axis=0)


def kernel(x, c, w_ada, b_ada, w_in, b_in, w_pool_mix, b_pool_mix, pool_scale, w_out, b_out, ln_g, ln_b, loss_target, m_w_ada, m_b_ada, m_w_in, m_b_in, m_w_pool_mix, m_b_pool_mix, m_pool_scale, m_w_out, m_b_out, m_ln_g, m_ln_b, v_w_ada, v_b_ada, v_w_in, v_b_in, v_w_pool_mix, v_b_pool_mix, v_pool_scale, v_w_out, v_b_out, v_ln_g, v_ln_b):
    S = x.shape[1]
    T = min(T_ATT, S)
    n_t = S // T
    x2 = x[0]
    tgt = loss_target[0]
    q_scale = jnp.concatenate([jnp.full((1, D_ATT), Q_SCALE, F32), jnp.ones((1, D_PAD - D_ATT), F32)], axis=1)

    to_cols = lambda a: jnp.transpose(a, (2, 0, 1))
    from_cols = lambda a: jnp.transpose(a, (1, 2, 0))
    c_all, shift, scale, gate, wt_pad = _gather_and_ada(
        c, w_ada[0], b_ada.reshape(4, 1, SHARD_ADA), to_cols(w_in))
    b_pad = _pad_in(b_in) * q_scale

    u, qkv, f, p, g, w_out_all = _in_proj(x2, shift, scale, wt_pad, b_pad, w_out[0])
    w_out_full = w_out_all.reshape(D, D)
    big_f = _forget_cumsum(f)
    att, lse = _attention_fwd(qkv, big_f)

    dh, datt, dg, dpl, gw_out, gw_mix, vec, loss_part = _middle(
        x2, tgt, att, g, p, gate, w_pool_mix[0], b_pool_mix.reshape(1, D_POOL), pool_scale, w_out_full, b_out, ln_g, ln_b)
    dq, dk, dv, cs_att, dfk, dfq, g_w_out = _attention_bwd(
        qkv, datt, att, lse, big_f, gw_out.reshape(N_CHIPS, SHARD_OUT, D))
    dp, df, cs_tail = _tail(dpl, dfk, dfq, f)
    pieces = (dq, dk, dv, df, dp, dg)
    grad_x, vec_x, gw_pad = _grad_in(u, pieces, wt_pad, dh, x2, scale)

    cs_qkv = jnp.transpose(cs_att.reshape(N_PAIR, 3, 128), (1, 0, 2)).reshape(1, 3 * D_ATT)
    gb_pad = jnp.concatenate([cs_qkv, cs_tail[1:2, 0:128], cs_tail[0:1, :], vec[4:5, :]], axis=1) * q_scale
    dada = jnp.concatenate([vec_x[0:1, :], vec_x[1:2, :], vec[2:3, :]], axis=1)
    small = _pack_small({
        "b_in": _unpad_in(gb_pad), "w_pool_mix": gw_mix, "b_pool_mix": vec[6:7, :D_POOL],
        "pool_scale": vec[5:6, :D_POOL], "b_out": vec[3:4, :], "ln_g": vec[0:1, :], "ln_b": vec[1:2, :],
        "loss": loss_part})

    g_w_in, small_sum, g_w_ada, g_b_ada, loss = _reduce_all(
        gw_pad, _shards_in(q_scale), small, dada, c_all)

    big = _adamw([(w_ada[0], g_w_ada, m_w_ada[0], v_w_ada[0]),
                  (w_out[0], g_w_out, m_w_out[0], v_w_out[0])], 2)
    big_in = _adamw([(to_cols(w_in), g_w_in, to_cols(m_w_in), to_cols(v_w_in))], 2)
    tiles = lambda a: a.reshape(4 * POOL_GROUP, POOL_GROUP)
    flat = lambda a: a.reshape(1, D_POOL)
    small_params = [("b_ada", b_ada, m_b_ada, v_b_ada), ("b_in", b_in, m_b_in, v_b_in),
                    ("w_pool_mix", tiles(w_pool_mix), tiles(m_w_pool_mix), tiles(v_w_pool_mix)),
                    ("b_pool_mix", flat(b_pool_mix), flat(m_b_pool_mix), flat(v_b_pool_mix)),
                    ("pool_scale", pool_scale, m_pool_scale, v_pool_scale), ("b_out", b_out, m_b_out, v_b_out),
                    ("ln_g", ln_g, m_ln_g, v_ln_g), ("ln_b", ln_b, m_ln_b, v_ln_b)]
    sm = _adamw_small(small_sum, g_b_ada, small_params)
    sm_idx = {p[0]: n for n, p in enumerate(small_params)}
    shapes = {"w_pool_mix": (1, 4, POOL_GROUP, POOL_GROUP), "b_pool_mix": (1, 4, POOL_GROUP)}

    names = ["w_ada", "b_ada", "w_in", "b_in", "w_pool_mix", "b_pool_mix", "pool_scale", "w_out", "b_out",
             "ln_g", "ln_b"]
    big_idx = {"w_ada": 0, "w_out": 1}

    def leaf(kind, name):
        if name == "w_in":
            return from_cols(big_in[(kind - 1) % 4])
        if name in big_idx:
            return big[4 * big_idx[name] + (kind - 1) % 4][None]
        val = sm[4 * sm_idx[name] + kind]
        return val.reshape(shapes[name]) if name in shapes else val

    outs = [loss.reshape(()), grad_x[None]]
    for kind in range(4):
        outs += [leaf(kind, n) for n in names]
    return tuple(outs)
```
